```python
import math
import jax
import jax.numpy as jnp
from jax import lax
import numpy as np

D_MODEL = 1024
BATCH = 16
SEQ = 4096
DEPTH = 2

GRID_W = 64
CTX_LEN = 256
EPS = 1e-6

SSD_HEADS = 6
SSD_HEAD_DIM = 64
SSD_INNER = SSD_HEADS * SSD_HEAD_DIM
SSD_GROUPS = 2
SSD_STATE = 128
SSD_CONV = 4
SSD_CHUNK = 128
XBC_DIM = SSD_INNER + 2 * SSD_GROUPS * SSD_STATE

MLA_HEADS = 6
Q_LORA = 256
KV_LORA = 256
QK_NOPE = 64
QK_ROPE = 32
V_HEAD = 64
QK_DIM = QK_NOPE + QK_ROPE
MLA_OUT = MLA_HEADS * V_HEAD
ROPE_THETA = 10000.0
ROPE_PAIRS = QK_ROPE // 4
Q_BLOCK = 128

POOL_WINDOWS = (2, 4, 8, 16)
POOL_GROUPS = len(POOL_WINDOWS)
POOL_GROUP_DIM = 64
POOL_DIM = POOL_GROUPS * POOL_GROUP_DIM

MIX_DIM = SSD_INNER + MLA_OUT + POOL_DIM
D_FF = 4 * D_MODEL

OFF_Z = 0
OFF_XBC = OFF_Z + SSD_INNER
OFF_DT = OFF_XBC + XBC_DIM
OFF_QA = OFF_DT + 2 * SSD_HEADS
OFF_KVA = OFF_QA + Q_LORA
OFF_KROPE = OFF_KVA + KV_LORA
OFF_POOL = OFF_KROPE + QK_ROPE
IN_COLS = OFF_POOL + POOL_DIM

kernel_name = "hybrid_ssd_mla_pool_diffusion_block"


def rmsnorm(u, w):
    uf = u.astype(jnp.float32)
    y = uf * lax.rsqrt(jnp.mean(uf * uf, axis=-1, keepdims=True) + EPS)
    return (y * w.astype(jnp.float32)).astype(u.dtype)


def adaln(cond, mod_w, mod_b):
    m = jax.nn.silu(cond) @ mod_w + mod_b
    return jnp.split(m[..., None, :], 6, axis=-1)


def modulate(h, shift, scale):
    return h * (1.0 + scale) + shift


def squared_relu_mlp(h, w1, w2):
    return jnp.square(jax.nn.relu(h @ w1)) @ w2


def axial_rope_tables(n):
    rows = n // GRID_W
    row = jnp.repeat(jnp.arange(rows, dtype=jnp.float32), GRID_W)
    col = jnp.tile(jnp.arange(GRID_W, dtype=jnp.float32), rows)
    inv_freq = ROPE_THETA ** (-jnp.arange(ROPE_PAIRS, dtype=jnp.float32) / ROPE_PAIRS)
    ang = jnp.stack([row[:, None] * inv_freq, col[:, None] * inv_freq], axis=1)
    return jnp.cos(ang), jnp.sin(ang)


def apply_axial_rope(u, cos, sin):
    shp = u.shape
    ur = u.astype(jnp.float32).reshape(shp[:-1] + (2, 2, ROPE_PAIRS))
    u1, u2 = ur[..., 0, :], ur[..., 1, :]
    bshape = (shp[1],) + (1,) * (u.ndim - 3) + (2, ROPE_PAIRS)
    cos = cos.reshape(bshape)
    sin = sin.reshape(bshape)
    out = jnp.stack([u1 * cos - u2 * sin, u2 * cos + u1 * sin], axis=-2)
    return out.reshape(shp).astype(u.dtype)


def depthwise_conv_centred(u, w, b):
    k = w.shape[0]
    out = lax.conv_general_dilated(u, w[:, None, :].astype(u.dtype), (1,), [((k - 1) // 2, k // 2)],
                                   dimension_numbers=('NWC', 'WIO', 'NWC'),
                                   feature_group_count=u.shape[-1])
    return out + b


def ssd_inputs(proj, conv_w, conv_b, dt_bias):
    bsz, n = proj.shape[:2]
    z = proj[..., OFF_Z:OFF_XBC]
    xbc = jax.nn.silu(depthwise_conv_centred(proj[..., OFF_XBC:OFF_DT], conv_w, conv_b))
    xs = xbc[..., :SSD_INNER].reshape(bsz, n, SSD_HEADS, SSD_HEAD_DIM)
    bm = xbc[..., SSD_INNER:SSD_INNER + SSD_GROUPS * SSD_STATE].reshape(bsz, n, SSD_GROUPS, SSD_STATE)
    cm = xbc[..., SSD_INNER + SSD_GROUPS * SSD_STATE:].reshape(bsz, n, SSD_GROUPS, SSD_STATE)
    dt_raw = proj[..., OFF_DT:OFF_QA].astype(jnp.float32).reshape(bsz, n, 2, SSD_HEADS)
    dt = jax.nn.softplus(dt_raw + dt_bias.astype(jnp.float32))
    return z, xs, bm, cm, dt


def segsum_exp(a):
    l = a.shape[-1]
    cs = jnp.cumsum(a, axis=-1)
    diff = cs[..., :, None] - cs[..., None, :]
    mask = jnp.tril(jnp.ones((l, l), dtype=bool))
    return jnp.exp(jnp.where(mask, diff, -jnp.inf))


def ssd_chunked_scan(xs, dt, a, b_mat, c_mat, h0):
    f32 = jnp.float32
    bsz, n = xs.shape[:2]
    nc = n // SSD_CHUNK
    r = SSD_HEADS // SSD_GROUPS
    dt = dt.astype(f32)
    x = (xs.astype(f32) * dt[..., None]).reshape(bsz, nc, SSD_CHUNK, SSD_GROUPS, r, SSD_HEAD_DIM)
    adt = jnp.moveaxis((dt * a.astype(f32)).reshape(bsz, nc, SSD_CHUNK, SSD_GROUPS, r), 2, -1)
    bm = b_mat.astype(f32).reshape(bsz, nc, SSD_CHUNK, SSD_GROUPS, SSD_STATE)
    cm = c_mat.astype(f32).reshape(bsz, nc, SSD_CHUNK, SSD_GROUPS, SSD_STATE)
    a_cs = jnp.cumsum(adt, axis=-1)
    l_mat = segsum_exp(adt)
    cb = jnp.einsum('bclgn,bcsgn->bcgls', cm, bm)
    y_diag = jnp.einsum('bcgrls,bcsgrp->bclgrp', cb[:, :, :, None] * l_mat, x)
    decay_to_end = jnp.exp(a_cs[..., -1:] - a_cs)
    chunk_states = jnp.einsum('bclgn,bcgrl,bclgrp->bcgrpn', bm, decay_to_end, x)
    chunk_decay = jnp.exp(a_cs[..., -1])

    def carry_step(h, inp):
        s_c, d_c = inp
        return h * d_c[..., None, None] + s_c, h

    h_init = h0.astype(f32).reshape(bsz, SSD_GROUPS, r, SSD_HEAD_DIM, SSD_STATE)
    h_last, h_in = lax.scan(carry_step, h_init,
                            (jnp.moveaxis(chunk_states, 1, 0), jnp.moveaxis(chunk_decay, 1, 0)))
    h_in = jnp.moveaxis(h_in, 0, 1)
    y_off = jnp.einsum('bclgn,bcgrpn,bcgrl->bclgrp', cm, h_in, jnp.exp(a_cs))
    y = (y_diag + y_off).reshape(bsz, n, SSD_HEADS, SSD_HEAD_DIM)
    return y, h_last.reshape(bsz, SSD_HEADS, SSD_HEAD_DIM, SSD_STATE)


def ssd_bidirectional(xs, bm, cm, dt, a, h0_fwd, h0_bwd):
    flip = lambda t: jnp.flip(t, axis=1)
    y_f, h_f = ssd_chunked_scan(xs, dt[:, :, 0], a[0], bm, cm, h0_fwd)
    y_b, h_b = ssd_chunked_scan(flip(xs), flip(dt[:, :, 1]), a[1], flip(bm), flip(cm), h0_bwd)
    return y_f + flip(y_b), h_f, h_b


def ssd_output(y, xs, z, d_skip, norm_w):
    bsz, n = y.shape[:2]
    y = y + xs.astype(jnp.float32) * d_skip.astype(jnp.float32)[:, None]
    g = (y.reshape(bsz, n, SSD_INNER) * jax.nn.silu(z.astype(jnp.float32)))
    g = g.reshape(bsz, n, SSD_GROUPS, SSD_INNER // SSD_GROUPS)
    g = g * lax.rsqrt(jnp.mean(g * g, axis=-1, keepdims=True) + EPS)
    return (g.reshape(bsz, n, SSD_INNER) * norm_w.astype(jnp.float32)).astype(z.dtype)


def mla_qkv(proj, q_a_norm_w, w_q_b, kv_a_norm_w, w_kv_b, rope):
    bsz, n = proj.shape[:2]
    cq = rmsnorm(proj[..., OFF_QA:OFF_KVA], q_a_norm_w)
    q = (cq @ w_q_b).reshape(bsz, n, MLA_HEADS, QK_DIM)
    ckv = rmsnorm(proj[..., OFF_KVA:OFF_KROPE], kv_a_norm_w)
    k_rope = proj[..., OFF_KROPE:OFF_POOL]
    kv = (ckv @ w_kv_b).reshape(bsz, n, MLA_HEADS, QK_NOPE + V_HEAD)
    k_nope, v = kv[..., :QK_NOPE], kv[..., QK_NOPE:]
    q_nope, q_rope = q[..., :QK_NOPE], q[..., QK_NOPE:]
    if rope is not None:
        cos, sin = rope
        q_rope = apply_axial_rope(q_rope, cos, sin)
        k_rope = apply_axial_rope(k_rope, cos, sin)
    q = jnp.concatenate([q_nope, q_rope], axis=-1)
    k = jnp.concatenate([k_nope, jnp.broadcast_to(k_rope[:, :, None, :], (bsz, n, MLA_HEADS, QK_ROPE))], axis=-1)
    return q, k, v


def attend(q, k, v):
    s = jnp.einsum('bqhd,bkhd->bhqk', q, k, preferred_element_type=jnp.float32) * (QK_DIM ** -0.5)
    p = jax.nn.softmax(s, axis=-1).astype(v.dtype)
    return jnp.einsum('bhqk,bkhd->bqhd', p, v)


def attend_blocked(q, k, v):
    bsz, n, h, dq = q.shape
    qb = jnp.moveaxis(q.reshape(bsz, n // Q_BLOCK, Q_BLOCK, h, dq), 1, 0)
    out = lax.map(lambda qi: attend(qi, k, v), qb)
    return jnp.moveaxis(out, 0, 1).reshape(bsz, n, h, v.shape[-1])


def multiscale_pool(u, pool_w, pool_scale):
    bsz, n, _ = u.shape
    uf = u.astype(jnp.float32)
    cs = jnp.pad(jnp.cumsum(uf, axis=1), ((0, 0), (1, 0), (0, 0)))
    t = jnp.arange(n)
    outs = []
    for gi, w in enumerate(POOL_WINDOWS):
        sl = slice(gi * POOL_GROUP_DIM, (gi + 1) * POOL_GROUP_DIM)
        lo = jnp.clip(t - w // 2, 0, n)
        hi = jnp.clip(t + w - w // 2, 0, n)
        csg = cs[..., sl]
        mean = (jnp.take(csg, hi, axis=1) - jnp.take(csg, lo, axis=1)) / (hi - lo).astype(jnp.float32)[:, None]
        outs.append(mean - uf[..., sl])
    d = jnp.stack(outs, axis=2)
    y = jnp.einsum('blgc,gcd->blgd', d, pool_w.astype(jnp.float32)).reshape(bsz, n, POOL_DIM)
    return (y * pool_scale.astype(jnp.float32)).astype(u.dtype)


def hybrid_layer(x, ctx, c, c_ctx, rope, mod_w, mod_b, norm1_w, norm2_w, w_in, conv_w, conv_b,
                 dt_bias, a_log, ssd_d, ssd_norm_w, q_a_norm_w, w_q_b, kv_a_norm_w, w_kv_b,
                 pool_w, pool_scale, w_out, w_mlp1, w_mlp2, update_ctx):
    bsz, n, _ = x.shape
    m = ctx.shape[1]
    sh1, sc1, g1, sh2, sc2, g2 = adaln(c, mod_w, mod_b)
    csh1, csc1, cg1, csh2, csc2, cg2 = adaln(c_ctx, mod_w, mod_b)
    px = modulate(rmsnorm(x, norm1_w), sh1, sc1) @ w_in
    pc = modulate(rmsnorm(ctx, norm1_w), csh1, csc1) @ w_in

    a = -jnp.exp(a_log.astype(jnp.float32))
    zc, xsc, bmc, cmc, dtc = ssd_inputs(pc, conv_w, conv_b, dt_bias)
    zx, xsx, bmx, cmx, dtx = ssd_inputs(px, conv_w, conv_b, dt_bias)
    h_zero = jnp.zeros((bsz, SSD_HEADS, SSD_HEAD_DIM, SSD_STATE), jnp.float32)
    yc, hf_ctx, hb_ctx = ssd_bidirectional(xsc, bmc, cmc, dtc, a, h_zero, h_zero)
    yx, _, _ = ssd_bidirectional(xsx, bmx, cmx, dtx, a, hf_ctx, hb_ctx)
    ssd_x = ssd_output(yx, xsx, zx, ssd_d, ssd_norm_w)

    qc, kc, vc = mla_qkv(pc, q_a_norm_w, w_q_b, kv_a_norm_w, w_kv_b, None)
    qx, kx, vx = mla_qkv(px, q_a_norm_w, w_q_b, kv_a_norm_w, w_kv_b, rope)
    attn_x = attend_blocked(qx, jnp.concatenate([kx, kc], axis=1), jnp.concatenate([vx, vc], axis=1))

    pool_x = multiscale_pool(px[..., OFF_POOL:], pool_w, pool_scale)

    mix_x = jnp.concatenate([ssd_x, attn_x.reshape(bsz, n, MLA_OUT), pool_x], axis=-1) @ w_out
    x = x + g1 * mix_x
    x = x + g2 * squared_relu_mlp(modulate(rmsnorm(x, norm2_w), sh2, sc2), w_mlp1, w_mlp2)

    if update_ctx:
        ssd_c = ssd_output(yc, xsc, zc, ssd_d, ssd_norm_w)
        attn_c = attend(qc, kc, vc).reshape(bsz, m, MLA_OUT)
        pool_c = multiscale_pool(pc[..., OFF_POOL:], pool_w, pool_scale)
        mix_c = jnp.concatenate([ssd_c, attn_c, pool_c], axis=-1) @ w_out
        ctx = ctx + cg1 * mix_c
        ctx = ctx + cg2 * squared_relu_mlp(modulate(rmsnorm(ctx, norm2_w), csh2, csc2), w_mlp1, w_mlp2)
    return x, ctx


def _fwd_setup_inputs(seed: int = 0) -> dict:
    key = jax.random.key(seed)
    ks = jax.random.split(key, 32)
    f32 = jnp.float32

    def nrm(k, shape, scale):
        return jax.random.normal(k, shape, f32) * scale

    def gain(k, shape):
        return 1.0 + 0.02 * jax.random.normal(k, shape, f32)

    dt0 = jnp.exp(jax.random.uniform(ks[11], (DEPTH, 2, SSD_HEADS), f32, math.log(1e-3), math.log(1e-1)))
    return {
        "x": nrm(ks[0], (BATCH, SEQ, D_MODEL), 1.0),
        "c": nrm(ks[1], (BATCH, D_MODEL), 1.0),
        "ctx": nrm(ks[2], (BATCH, CTX_LEN, D_MODEL), 1.0),
        "c_ctx": nrm(ks[3], (D_MODEL,), 1.0),
        "mod_w": nrm(ks[4], (DEPTH, D_MODEL, 6 * D_MODEL), 0.5 * D_MODEL ** -0.5),
        "mod_b": nrm(ks[5], (DEPTH, 6 * D_MODEL), 0.01),
        "norm1_w": gain(ks[6], (DEPTH, D_MODEL)),
        "norm2_w": gain(ks[7], (DEPTH, D_MODEL)),
        "w_in": nrm(ks[8], (DEPTH, D_MODEL, IN_COLS), D_MODEL ** -0.5),
        "conv_w": nrm(ks[9], (DEPTH, SSD_CONV, XBC_DIM), SSD_CONV ** -0.5),
        "conv_b": nrm(ks[10], (DEPTH, XBC_DIM), 0.01),
        "dt_bias": dt0 + jnp.log(-jnp.expm1(-dt0)),
        "a_log": jnp.log(jax.random.uniform(ks[12], (DEPTH, 2, SSD_HEADS), f32, 1.0, 16.0)),
        "ssd_d": 1.0 + 0.1 * jax.random.normal(ks[13], (DEPTH, SSD_HEADS), f32),
        "ssd_norm_w": gain(ks[14], (DEPTH, SSD_INNER)),
        "q_a_norm_w": gain(ks[15], (DEPTH, Q_LORA)),
        "w_q_b": nrm(ks[16], (DEPTH, Q_LORA, MLA_HEADS * QK_DIM), Q_LORA ** -0.5),
        "kv_a_norm_w": gain(ks[17], (DEPTH, KV_LORA)),
        "w_kv_b": nrm(ks[18], (DEPTH, KV_LORA, MLA_HEADS * (QK_NOPE + V_HEAD)), KV_LORA ** -0.5),
        "pool_w": nrm(ks[19], (DEPTH, POOL_GROUPS, POOL_GROUP_DIM, POOL_GROUP_DIM), POOL_GROUP_DIM ** -0.5),
        "pool_scale": gain(ks[20], (DEPTH, POOL_DIM)),
        "w_out": nrm(ks[21], (DEPTH, MIX_DIM, D_MODEL), MIX_DIM ** -0.5),
        "w_mlp1": nrm(ks[22], (DEPTH, D_MODEL, D_FF), D_MODEL ** -0.5),
        "w_mlp2": nrm(ks[23], (DEPTH, D_FF, D_MODEL), D_FF ** -0.5),
        "final_norm_w": gain(ks[24], (D_MODEL,)),
    }


def _fwd_reference(x, c, ctx, c_ctx, mod_w, mod_b, norm1_w, norm2_w, w_in, conv_w, conv_b, dt_bias, a_log,
              ssd_d, ssd_norm_w, q_a_norm_w, w_q_b, kv_a_norm_w, w_kv_b, pool_w, pool_scale, w_out,
              w_mlp1, w_mlp2, final_norm_w):
    rope = axial_rope_tables(x.shape[1])
    for i in range(DEPTH):
        x, ctx = hybrid_layer(x, ctx, c, c_ctx, rope, mod_w[i], mod_b[i], norm1_w[i], norm2_w[i], w_in[i],
                              conv_w[i], conv_b[i], dt_bias[i], a_log[i], ssd_d[i], ssd_norm_w[i],
                              q_a_norm_w[i], w_q_b[i], kv_a_norm_w[i], w_kv_b[i], pool_w[i], pool_scale[i],
                              w_out[i], w_mlp1[i], w_mlp2[i], update_ctx=(i < DEPTH - 1))
    return rmsnorm(x, final_norm_w)


import jax as _jax
import jax.numpy as _jnp

TWIN_FORMAT = 'train_step'
FWD_PARAMS = ['x', 'c', 'ctx', 'c_ctx', 'mod_w', 'mod_b', 'norm1_w', 'norm2_w', 'w_in', 'conv_w', 'conv_b', 'dt_bias', 'a_log', 'ssd_d', 'ssd_norm_w', 'q_a_norm_w', 'w_q_b', 'kv_a_norm_w', 'w_kv_b', 'pool_w', 'pool_scale', 'w_out', 'w_mlp1', 'w_mlp2', 'final_norm_w']
TWIN_WEIGHTS = ['c_ctx', 'mod_w', 'mod_b', 'norm1_w', 'norm2_w', 'w_in', 'conv_w', 'conv_b', 'dt_bias', 'a_log', 'ssd_d', 'ssd_norm_w', 'q_a_norm_w', 'w_q_b', 'kv_a_norm_w', 'w_kv_b', 'pool_w', 'pool_scale', 'w_out', 'w_mlp1', 'w_mlp2', 'final_norm_w']
TWIN_DIFF_INPUT = 'x'
TWIN_INPUTS = ['x', 'c', 'ctx', 'c_ctx', 'mod_w', 'mod_b', 'norm1_w', 'norm2_w', 'w_in', 'conv_w', 'conv_b', 'dt_bias', 'a_log', 'ssd_d', 'ssd_norm_w', 'q_a_norm_w', 'w_q_b', 'kv_a_norm_w', 'w_kv_b', 'pool_w', 'pool_scale', 'w_out', 'w_mlp1', 'w_mlp2', 'final_norm_w', 'loss_target', 'm_c_ctx', 'm_mod_w', 'm_mod_b', 'm_norm1_w', 'm_norm2_w', 'm_w_in', 'm_conv_w', 'm_conv_b', 'm_dt_bias', 'm_a_log', 'm_ssd_d', 'm_ssd_norm_w', 'm_q_a_norm_w', 'm_w_q_b', 'm_kv_a_norm_w', 'm_w_kv_b', 'm_pool_w', 'm_pool_scale', 'm_w_out', 'm_w_mlp1', 'm_w_mlp2', 'm_final_norm_w', 'v_c_ctx', 'v_mod_w', 'v_mod_b', 'v_norm1_w', 'v_norm2_w', 'v_w_in', 'v_conv_w', 'v_conv_b', 'v_dt_bias', 'v_a_log', 'v_ssd_d', 'v_ssd_norm_w', 'v_q_a_norm_w', 'v_w_q_b', 'v_kv_a_norm_w', 'v_w_kv_b', 'v_pool_w', 'v_pool_scale', 'v_w_out', 'v_w_mlp1', 'v_w_mlp2', 'v_final_norm_w']
TWIN_OUTPUTS = ['loss', 'grad_x', 'grad_c_ctx', 'grad_mod_w', 'grad_mod_b', 'grad_norm1_w', 'grad_norm2_w', 'grad_w_in', 'grad_conv_w', 'grad_conv_b', 'grad_dt_bias', 'grad_a_log', 'grad_ssd_d', 'grad_ssd_norm_w', 'grad_q_a_norm_w', 'grad_w_q_b', 'grad_kv_a_norm_w', 'grad_w_kv_b', 'grad_pool_w', 'grad_pool_scale', 'grad_w_out', 'grad_w_mlp1', 'grad_w_mlp2', 'grad_final_norm_w', 'delta_c_ctx', 'delta_mod_w', 'delta_mod_b', 'delta_norm1_w', 'delta_norm2_w', 'delta_w_in', 'delta_conv_w', 'delta_conv_b', 'delta_dt_bias', 'delta_a_log', 'delta_ssd_d', 'delta_ssd_norm_w', 'delta_q_a_norm_w', 'delta_w_q_b', 'delta_kv_a_norm_w', 'delta_w_kv_b', 'delta_pool_w', 'delta_pool_scale', 'delta_w_out', 'delta_w_mlp1', 'delta_w_mlp2', 'delta_final_norm_w', 'new_m_c_ctx', 'new_m_mod_w', 'new_m_mod_b', 'new_m_norm1_w', 'new_m_norm2_w', 'new_m_w_in', 'new_m_conv_w', 'new_m_conv_b', 'new_m_dt_bias', 'new_m_a_log', 'new_m_ssd_d', 'new_m_ssd_norm_w', 'new_m_q_a_norm_w', 'new_m_w_q_b', 'new_m_kv_a_norm_w', 'new_m_w_kv_b', 'new_m_pool_w', 'new_m_pool_scale', 'new_m_w_out', 'new_m_w_mlp1', 'new_m_w_mlp2', 'new_m_final_norm_w', 'new_v_c_ctx', 'new_v_mod_w', 'new_v_mod_b', 'new_v_norm1_w', 'new_v_norm2_w', 'new_v_w_in', 'new_v_conv_w', 'new_v_conv_b', 'new_v_dt_bias', 'new_v_a_log', 'new_v_ssd_d', 'new_v_ssd_norm_w', 'new_v_q_a_norm_w', 'new_v_w_q_b', 'new_v_kv_a_norm_w', 'new_v_w_kv_b', 'new_v_pool_w', 'new_v_pool_scale', 'new_v_w_out', 'new_v_w_mlp1', 'new_v_w_mlp2', 'new_v_final_norm_w']
TWIN_LEAF_KINDS = {'loss': 'loss', 'grad_x': 'grad_x', 'grad_c_ctx': 'grad_w', 'grad_mod_w': 'grad_w', 'grad_mod_b': 'grad_w', 'grad_norm1_w': 'grad_w', 'grad_norm2_w': 'grad_w', 'grad_w_in': 'grad_w', 'grad_conv_w': 'grad_w', 'grad_conv_b': 'grad_w', 'grad_dt_bias': 'grad_w', 'grad_a_log': 'grad_w', 'grad_ssd_d': 'grad_w', 'grad_ssd_norm_w': 'grad_w', 'grad_q_a_norm_w': 'grad_w', 'grad_w_q_b': 'grad_w', 'grad_kv_a_norm_w': 'grad_w', 'grad_w_kv_b': 'grad_w', 'grad_pool_w': 'grad_w', 'grad_pool_scale': 'grad_w', 'grad_w_out': 'grad_w', 'grad_w_mlp1': 'grad_w', 'grad_w_mlp2': 'grad_w', 'grad_final_norm_w': 'grad_w', 'delta_c_ctx': 'delta_w', 'delta_mod_w': 'delta_w', 'delta_mod_b': 'delta_w', 'delta_norm1_w': 'delta_w', 'delta_norm2_w': 'delta_w', 'delta_w_in': 'delta_w', 'delta_conv_w': 'delta_w', 'delta_conv_b': 'delta_w', 'delta_dt_bias': 'delta_w', 'delta_a_log': 'delta_w', 'delta_ssd_d': 'delta_w', 'delta_ssd_norm_w': 'delta_w', 'delta_q_a_norm_w': 'delta_w', 'delta_w_q_b': 'delta_w', 'delta_kv_a_norm_w': 'delta_w', 'delta_w_kv_b': 'delta_w', 'delta_pool_w': 'delta_w', 'delta_pool_scale': 'delta_w', 'delta_w_out': 'delta_w', 'delta_w_mlp1': 'delta_w', 'delta_w_mlp2': 'delta_w', 'delta_final_norm_w': 'delta_w', 'new_m_c_ctx': 'new_m', 'new_m_mod_w': 'new_m', 'new_m_mod_b': 'new_m', 'new_m_norm1_w': 'new_m', 'new_m_norm2_w': 'new_m', 'new_m_w_in': 'new_m', 'new_m_conv_w': 'new_m', 'new_m_conv_b': 'new_m', 'new_m_dt_bias': 'new_m', 'new_m_a_log': 'new_m', 'new_m_ssd_d': 'new_m', 'new_m_ssd_norm_w': 'new_m', 'new_m_q_a_norm_w': 'new_m', 'new_m_w_q_b': 'new_m', 'new_m_kv_a_norm_w': 'new_m', 'new_m_w_kv_b': 'new_m', 'new_m_pool_w': 'new_m', 'new_m_pool_scale': 'new_m', 'new_m_w_out': 'new_m', 'new_m_w_mlp1': 'new_m', 'new_m_w_mlp2': 'new_m', 'new_m_final_norm_w': 'new_m', 'new_v_c_ctx': 'new_v', 'new_v_mod_w': 'new_v', 'new_v_mod_b': 'new_v', 'new_v_norm1_w': 'new_v', 'new_v_norm2_w': 'new_v', 'new_v_w_in': 'new_v', 'new_v_conv_w': 'new_v', 'new_v_conv_b': 'new_v', 'new_v_dt_bias': 'new_v', 'new_v_a_log': 'new_v', 'new_v_ssd_d': 'new_v', 'new_v_ssd_norm_w': 'new_v', 'new_v_q_a_norm_w': 'new_v', 'new_v_w_q_b': 'new_v', 'new_v_kv_a_norm_w': 'new_v', 'new_v_w_kv_b': 'new_v', 'new_v_pool_w': 'new_v', 'new_v_pool_scale': 'new_v', 'new_v_w_out': 'new_v', 'new_v_w_mlp1': 'new_v', 'new_v_w_mlp2': 'new_v', 'new_v_final_norm_w': 'new_v'}


def _forward(args):
    return _fwd_reference(*[args[k] for k in FWD_PARAMS])


def _output_shape():
    out = _jax.eval_shape(lambda: _forward(_fwd_setup_inputs(0)))
    return out.shape, out.dtype

N_MICROBATCH = 1
ADAM_LR = 0.001
ADAM_B1 = 0.9
ADAM_B2 = 0.999
ADAM_EPS = 1e-08
ADAM_WD = 0.01
ADAM_STEP = 10
PER_EXAMPLE_BATCH_AXIS = {'x': 0, 'c': 0, 'ctx': 0, 'loss_target': 0}
SHARED_INPUTS = []
_WEIGHT_DTYPES = {'c_ctx': _jnp.float32, 'mod_w': _jnp.float32, 'mod_b': _jnp.float32, 'norm1_w': _jnp.float32, 'norm2_w': _jnp.float32, 'w_in': _jnp.float32, 'conv_w': _jnp.float32, 'conv_b': _jnp.float32, 'dt_bias': _jnp.float32, 'a_log': _jnp.float32, 'ssd_d': _jnp.float32, 'ssd_norm_w': _jnp.float32, 'q_a_norm_w': _jnp.float32, 'w_q_b': _jnp.float32, 'kv_a_norm_w': _jnp.float32, 'w_kv_b': _jnp.float32, 'pool_w': _jnp.float32, 'pool_scale': _jnp.float32, 'w_out': _jnp.float32, 'w_mlp1': _jnp.float32, 'w_mlp2': _jnp.float32, 'final_norm_w': _jnp.float32}
MOMENT_SCALE = {'c_ctx': 1.139293e-02, 'mod_w': 1.217795e-01, 'mod_b': 1.979327e-01, 'norm1_w': 7.284417e-02, 'norm2_w': 1.116555e-01, 'w_in': 5.391271e-02, 'conv_w': 5.556804e-02, 'conv_b': 7.565505e-02, 'dt_bias': 2.262561e-01, 'a_log': 2.201985e-01, 'ssd_d': 3.151190e-01, 'ssd_norm_w': 1.019612e-01, 'q_a_norm_w': 8.327058e-03, 'w_q_b': 5.933333e-03, 'kv_a_norm_w': 3.863055e-02, 'w_kv_b': 2.000567e-02, 'pool_w': 6.706835e-02, 'pool_scale': 6.810406e-02, 'w_out': 5.910850e-02, 'w_mlp1': 5.744932e-02, 'w_mlp2': 1.036890e-01, 'final_norm_w': 6.468598e+01}


def _to_microbatches(a, axis):
    t = _jnp.moveaxis(a, axis, 0)
    t = t.reshape((N_MICROBATCH, t.shape[0] // N_MICROBATCH) + t.shape[1:])
    return _jnp.moveaxis(t, 1, axis + 1)


def setup_inputs(seed: int = 0) -> dict:
    inp = _fwd_setup_inputs(seed)
    key = _jax.random.fold_in(_jax.random.key(seed), 7919)
    shape, _ = _output_shape()
    out = dict(inp)
    out["loss_target"] = _jax.random.normal(_jax.random.fold_in(key, 0), shape, _jnp.float32)
    for i, name in enumerate(TWIN_WEIGHTS):
        w = inp[name].astype(_jnp.float32)
        if MOMENT_SCALE is None:
            s = _jnp.sqrt(_jnp.mean(_jnp.square(w)) + 1e-30)
        else:
            s = MOMENT_SCALE[name]
        km, kv = _jax.random.split(_jax.random.fold_in(key, i + 1))
        out[name] = w
        out["m_" + name] = s * _jax.random.normal(km, w.shape, _jnp.float32)
        out["v_" + name] = (s * s) * _jax.random.uniform(kv, w.shape, _jnp.float32, 0.5, 1.5)
    if N_MICROBATCH > 1:
        for name, axis in PER_EXAMPLE_BATCH_AXIS.items():
            out[name] = _to_microbatches(out[name], axis)
    return {'x': out['x'], 'c': out['c'], 'ctx': out['ctx'], 'c_ctx': out['c_ctx'], 'mod_w': out['mod_w'], 'mod_b': out['mod_b'], 'norm1_w': out['norm1_w'], 'norm2_w': out['norm2_w'], 'w_in': out['w_in'], 'conv_w': out['conv_w'], 'conv_b': out['conv_b'], 'dt_bias': out['dt_bias'], 'a_log': out['a_log'], 'ssd_d': out['ssd_d'], 'ssd_norm_w': out['ssd_norm_w'], 'q_a_norm_w': out['q_a_norm_w'], 'w_q_b': out['w_q_b'], 'kv_a_norm_w': out['kv_a_norm_w'], 'w_kv_b': out['w_kv_b'], 'pool_w': out['pool_w'], 'pool_scale': out['pool_scale'], 'w_out': out['w_out'], 'w_mlp1': out['w_mlp1'], 'w_mlp2': out['w_mlp2'], 'final_norm_w': out['final_norm_w'], 'loss_target': out['loss_target'], 'm_c_ctx': out['m_c_ctx'], 'm_mod_w': out['m_mod_w'], 'm_mod_b': out['m_mod_b'], 'm_norm1_w': out['m_norm1_w'], 'm_norm2_w': out['m_norm2_w'], 'm_w_in': out['m_w_in'], 'm_conv_w': out['m_conv_w'], 'm_conv_b': out['m_conv_b'], 'm_dt_bias': out['m_dt_bias'], 'm_a_log': out['m_a_log'], 'm_ssd_d': out['m_ssd_d'], 'm_ssd_norm_w': out['m_ssd_norm_w'], 'm_q_a_norm_w': out['m_q_a_norm_w'], 'm_w_q_b': out['m_w_q_b'], 'm_kv_a_norm_w': out['m_kv_a_norm_w'], 'm_w_kv_b': out['m_w_kv_b'], 'm_pool_w': out['m_pool_w'], 'm_pool_scale': out['m_pool_scale'], 'm_w_out': out['m_w_out'], 'm_w_mlp1': out['m_w_mlp1'], 'm_w_mlp2': out['m_w_mlp2'], 'm_final_norm_w': out['m_final_norm_w'], 'v_c_ctx': out['v_c_ctx'], 'v_mod_w': out['v_mod_w'], 'v_mod_b': out['v_mod_b'], 'v_norm1_w': out['v_norm1_w'], 'v_norm2_w': out['v_norm2_w'], 'v_w_in': out['v_w_in'], 'v_conv_w': out['v_conv_w'], 'v_conv_b': out['v_conv_b'], 'v_dt_bias': out['v_dt_bias'], 'v_a_log': out['v_a_log'], 'v_ssd_d': out['v_ssd_d'], 'v_ssd_norm_w': out['v_ssd_norm_w'], 'v_q_a_norm_w': out['v_q_a_norm_w'], 'v_w_q_b': out['v_w_q_b'], 'v_kv_a_norm_w': out['v_kv_a_norm_w'], 'v_w_kv_b': out['v_w_kv_b'], 'v_pool_w': out['v_pool_w'], 'v_pool_scale': out['v_pool_scale'], 'v_w_out': out['v_w_out'], 'v_w_mlp1': out['v_w_mlp1'], 'v_w_mlp2': out['v_w_mlp2'], 'v_final_norm_w': out['v_final_norm_w']}


def _loss(weights, diff, rest, loss_target):
    with _jax.named_scope("forward"):
        args = {**rest, TWIN_DIFF_INPUT: diff, **{k: w.astype(_WEIGHT_DTYPES[k]) for k, w in weights.items()}}
        y = _forward(args)
    with _jax.named_scope("loss_head"):
        err = _jnp.square(y.astype(_jnp.float32) - loss_target)
        return 0.5 * _jnp.sum(_jnp.mean(err, axis=-1)) if err.ndim else 0.5 * err


def _adamw(w, g, m, v):
    m = ADAM_B1 * m + (1.0 - ADAM_B1) * g
    v = ADAM_B2 * v + (1.0 - ADAM_B2) * _jnp.square(g)
    m_hat = m / (1.0 - ADAM_B1 ** ADAM_STEP)
    v_hat = v / (1.0 - ADAM_B2 ** ADAM_STEP)
    delta = -ADAM_LR * (m_hat / (_jnp.sqrt(v_hat) + ADAM_EPS) + ADAM_WD * w)
    return delta, m, v


def reference(x, c, ctx, c_ctx, mod_w, mod_b, norm1_w, norm2_w, w_in, conv_w, conv_b, dt_bias, a_log, ssd_d, ssd_norm_w, q_a_norm_w, w_q_b, kv_a_norm_w, w_kv_b, pool_w, pool_scale, w_out, w_mlp1, w_mlp2, final_norm_w, loss_target, m_c_ctx, m_mod_w, m_mod_b, m_norm1_w, m_norm2_w, m_w_in, m_conv_w, m_conv_b, m_dt_bias, m_a_log, m_ssd_d, m_ssd_norm_w, m_q_a_norm_w, m_w_q_b, m_kv_a_norm_w, m_w_kv_b, m_pool_w, m_pool_scale, m_w_out, m_w_mlp1, m_w_mlp2, m_final_norm_w, v_c_ctx, v_mod_w, v_mod_b, v_norm1_w, v_norm2_w, v_w_in, v_conv_w, v_conv_b, v_dt_bias, v_a_log, v_ssd_d, v_ssd_norm_w, v_q_a_norm_w, v_w_q_b, v_kv_a_norm_w, v_w_kv_b, v_pool_w, v_pool_scale, v_w_out, v_w_mlp1, v_w_mlp2, v_final_norm_w):
    given = dict(x=x, c=c, ctx=ctx, c_ctx=c_ctx, mod_w=mod_w, mod_b=mod_b, norm1_w=norm1_w, norm2_w=norm2_w, w_in=w_in, conv_w=conv_w, conv_b=conv_b, dt_bias=dt_bias, a_log=a_log, ssd_d=ssd_d, ssd_norm_w=ssd_norm_w, q_a_norm_w=q_a_norm_w, w_q_b=w_q_b, kv_a_norm_w=kv_a_norm_w, w_kv_b=w_kv_b, pool_w=pool_w, pool_scale=pool_scale, w_out=w_out, w_mlp1=w_mlp1, w_mlp2=w_mlp2, final_norm_w=final_norm_w, loss_target=loss_target, m_c_ctx=m_c_ctx, m_mod_w=m_mod_w, m_mod_b=m_mod_b, m_norm1_w=m_norm1_w, m_norm2_w=m_norm2_w, m_w_in=m_w_in, m_conv_w=m_conv_w, m_conv_b=m_conv_b, m_dt_bias=m_dt_bias, m_a_log=m_a_log, m_ssd_d=m_ssd_d, m_ssd_norm_w=m_ssd_norm_w, m_q_a_norm_w=m_q_a_norm_w, m_w_q_b=m_w_q_b, m_kv_a_norm_w=m_kv_a_norm_w, m_w_kv_b=m_w_kv_b, m_pool_w=m_pool_w, m_pool_scale=m_pool_scale, m_w_out=m_w_out, m_w_mlp1=m_w_mlp1, m_w_mlp2=m_w_mlp2, m_final_norm_w=m_final_norm_w, v_c_ctx=v_c_ctx, v_mod_w=v_mod_w, v_mod_b=v_mod_b, v_norm1_w=v_norm1_w, v_norm2_w=v_norm2_w, v_w_in=v_w_in, v_conv_w=v_conv_w, v_conv_b=v_conv_b, v_dt_bias=v_dt_bias, v_a_log=v_a_log, v_ssd_d=v_ssd_d, v_ssd_norm_w=v_ssd_norm_w, v_q_a_norm_w=v_q_a_norm_w, v_w_q_b=v_w_q_b, v_kv_a_norm_w=v_kv_a_norm_w, v_w_kv_b=v_w_kv_b, v_pool_w=v_pool_w, v_pool_scale=v_pool_scale, v_w_out=v_w_out, v_w_mlp1=v_w_mlp1, v_w_mlp2=v_w_mlp2, v_final_norm_w=v_final_norm_w)
    weights = {n: given[n] for n in TWIN_WEIGHTS}
    shared = {n: given[n] for n in SHARED_INPUTS}
    per_example = {n: given[n] for n in ['x', 'c', 'ctx']}
    grad_fn = _jax.value_and_grad(_loss, argnums=(0, 1))

    def one_microbatch(ex, loss_target):
        ex = dict(ex)
        diff = ex.pop(TWIN_DIFF_INPUT)
        return grad_fn(weights, diff, {**shared, **ex}, loss_target)

    if N_MICROBATCH == 1:
        loss, (grad_w, grad_x) = one_microbatch(per_example, given["loss_target"])
    else:
        def body(carry, xs):
            loss_sum, grad_sum = carry
            l_k, (gw_k, gx_k) = one_microbatch(xs[0], xs[1])
            with _jax.named_scope("update"):
                return (loss_sum + l_k, _jax.tree.map(_jnp.add, grad_sum, gw_k)), gx_k

        init = (_jnp.zeros((), _jnp.float32), _jax.tree.map(_jnp.zeros_like, weights))
        (loss, grad_w), grad_x = _jax.lax.scan(body, init, (per_example, given["loss_target"]))
    with _jax.named_scope("update"):
        delta_w, new_m, new_v = {}, {}, {}
        for n in TWIN_WEIGHTS:
            delta_w[n], new_m[n], new_v[n] = _adamw(weights[n], grad_w[n], given["m_" + n], given["v_" + n])
    return (loss, grad_x, *[grad_w[n] for n in TWIN_WEIGHTS], *[delta_w[n] for n in TWIN_WEIGHTS],
            *[new_m[n] for n in TWIN_WEIGHTS], *[new_v[n] for n in TWIN_WEIGHTS])
```

```python
import functools
import math

import numpy as np
import jax
import jax.numpy as jnp
from jax import lax
from jax.experimental import pallas as pl
from jax.experimental.pallas import tpu as pltpu

F32 = jnp.float32
BF16 = jnp.bfloat16
MXU = jnp.bfloat16
HI = lax.Precision.HIGHEST

D = 1024
DEPTH = 2
GRID_W = 64
CTX = 256
EPS = 1e-6
SSD_HEADS = 6
SSD_P = 64
SSD_INNER = 384
SSD_N = 128
CHUNK = 128
XBC = 896
MLA_HEADS = 6
QK_NOPE = 64
QK_ROPE = 32
QK_DIM = 96
HP = 128
QW = MLA_HEADS * HP
POOL_DIM = 256
D_FF = 4096
FF_BLK = 1024
IN_COLS = 2092
NP = 2176
P_SPLITS = (384, 896, 256, 256, 256, 128)
DT0 = 32
CAT = QW + SSD_INNER + POOL_DIM

SB = 256
TM = 512
HALO = 8

ADAM_LR = 0.001
ADAM_B1 = 0.9
ADAM_B2 = 0.999
ADAM_EPS = 1e-08
ADAM_WD = 0.01
ADAM_STEP = 10

NT = (((1,), (1,)), ((), ()))
TN = (((0,), (0,)), ((), ()))


def _cp(vmem_mb=None):
    if vmem_mb is None:
        return pltpu.CompilerParams()
    return pltpu.CompilerParams(vmem_limit_bytes=vmem_mb << 20)


def _dot(a, b):
    return jnp.dot(a, b, preferred_element_type=F32)


def _dotg(a, b, dims):
    return lax.dot_general(a, b, dims, preferred_element_type=F32)


def _dot_hi(a, b, dims=None):
    if dims is None:
        return jnp.dot(a, b, precision=HI, preferred_element_type=F32)
    return lax.dot_general(a, b, dims, precision=HI, preferred_element_type=F32)


def _rms_hat(x):
    rstd = lax.rsqrt(jnp.mean(x * x, axis=-1, keepdims=True) + EPS)
    return x * rstd, rstd


def _rms_bwd(dn, xhat, rstd, w):
    dxhat = dn * w
    dx = rstd * (dxhat - xhat * jnp.mean(dxhat * xhat, axis=-1, keepdims=True))
    return dx, jnp.sum(dn * xhat, axis=0, keepdims=True)


def _sigmoid(z):
    return 1.0 / (1.0 + jnp.exp(-z))


def _colsum(a):
    return jnp.sum(a, axis=0, keepdims=True)


def _rowspec(cols, tm=TM):
    return pl.BlockSpec((tm, cols), lambda i: (i, 0))


def _fullspec(shape):
    n = len(shape)
    return pl.BlockSpec(shape, lambda *_: (0,) * n)


def _halo_specs(cols, nrows):
    per = SB // HALO
    last = nrows // HALO - 1
    prev = pl.BlockSpec((HALO, cols), lambda i: (jnp.maximum(i * per - 1, 0), 0))
    nxt = pl.BlockSpec((HALO, cols), lambda i: (jnp.minimum((i + 1) * per, last), 0))
    return prev, nxt


def _ext_rows(cur, prev, nxt, i, blocks_per_sample):
    j = i % blocks_per_sample
    first = jnp.logical_or(j == 0, j == 1)
    last = jnp.logical_or(j == 0, j == blocks_per_sample - 1)
    p = jnp.where(first, 0.0, prev)
    n = jnp.where(last, 0.0, nxt)
    return jnp.concatenate([p, cur, n], axis=0)


def _shift(ext, s):
    n = ext.shape[0]
    return pltpu.roll(ext, (-s) % n, axis=0)[HALO:HALO + SB, :]


def in_proj(x, bm, nw, w):
    R = x.shape[0]

    def body(x_ref, bm_ref, nw_ref, w_ref, h_ref, *outs):
        for s in range(TM // SB):
            rows = slice(s * SB, (s + 1) * SB)
            xhat, _ = _rms_hat(x_ref[rows, :])
            h = xhat * nw_ref[...] * (1.0 + bm_ref[s, 1:2, :]) + bm_ref[s, 0:1, :]
            h_ref[rows, :] = h.astype(h_ref.dtype)
        p = _dot(h_ref[...], w_ref[...])
        off = 0
        for o, n in zip(outs, P_SPLITS):
            o[...] = p[:, off:off + n]
            off += n

    return pl.pallas_call(
        body, name="in_proj", grid=(R // TM,),
        in_specs=[_rowspec(D), pl.BlockSpec((TM // SB, 8, D), lambda i: (i, 0, 0)), _fullspec((1, D)),
                  _fullspec((D, NP))],
        out_specs=[_rowspec(D)] + [_rowspec(n) for n in P_SPLITS],
        out_shape=[jax.ShapeDtypeStruct((R, D), MXU)] + [jax.ShapeDtypeStruct((R, n), F32) for n in P_SPLITS],
        compiler_params=_cp(56),
    )(x, bm, nw, w)


def in_proj_bwd(dx1, x, dz, dxbc, dqa, dkva, dpool, dkr, ddt, bm, nw, wt):
    R = x.shape[0]

    def body(dx1_ref, x_ref, dz_ref, dxbc_ref, dqa_ref, dkva_ref, dpool_ref, dkr_ref, ddt_ref, bm_ref, nw_ref,
             wt_ref, dx_ref, dp_ref, part_ref):
        dp_ref[:, 0:384] = dz_ref[...].astype(dp_ref.dtype)
        dp_ref[:, 384:1280] = dxbc_ref[...].astype(dp_ref.dtype)
        dp_ref[:, 1280:1536] = dqa_ref[...].astype(dp_ref.dtype)
        dp_ref[:, 1536:1792] = dkva_ref[...].astype(dp_ref.dtype)
        dp_ref[:, 1792:2048] = dpool_ref[...].astype(dp_ref.dtype)
        dp_ref[:, 2048:2176] = (dkr_ref[...] + ddt_ref[...]).astype(dp_ref.dtype)
        dh = _dot(dp_ref[...], wt_ref[...])
        w = nw_ref[...]
        for s in range(TM // SB):
            rows = slice(s * SB, (s + 1) * SB)
            xhat, rstd = _rms_hat(x_ref[rows, :])
            dhs = dh[rows, :]
            sc1 = 1.0 + bm_ref[s, 1:2, :]
            dx, dnw = _rms_bwd(dhs * sc1, xhat, rstd, w)
            dx_ref[rows, :] = dx1_ref[rows, :] + dx
            part_ref[s] = jnp.concatenate(
                [_colsum(dhs), _colsum(dhs * xhat * w), dnw, jnp.zeros((5, D), F32)], axis=0)

    return pl.pallas_call(
        body, name="in_proj_bwd", grid=(R // TM,),
        in_specs=[_rowspec(D), _rowspec(D), _rowspec(384), _rowspec(896), _rowspec(256), _rowspec(256),
                  _rowspec(256), _rowspec(128), _rowspec(128),
                  pl.BlockSpec((TM // SB, 8, D), lambda i: (i, 0, 0)), _fullspec((1, D)), _fullspec((NP, D))],
        out_specs=[_rowspec(D), _rowspec(NP), pl.BlockSpec((TM // SB, 8, D), lambda i: (i, 0, 0))],
        out_shape=[jax.ShapeDtypeStruct((R, D), F32), jax.ShapeDtypeStruct((R, NP), MXU),
                   jax.ShapeDtypeStruct((R // SB, 8, D), F32)],
        compiler_params=_cp(56),
    )(dx1, x, dz, dxbc, dqa, dkva, dpool, dkr, ddt, bm, nw, wt)


def mix_fwd(x, attn, ssd, pool, bm, wo):
    R = x.shape[0]

    def body(x_ref, a_ref, s_ref, p_ref, bm_ref, wo_ref, x1_ref, mix_ref, cat_ref):
        cat_ref[:, 0:QW] = a_ref[...].astype(cat_ref.dtype)
        cat_ref[:, QW:QW + SSD_INNER] = s_ref[...].astype(cat_ref.dtype)
        cat_ref[:, QW + SSD_INNER:CAT] = p_ref[...].astype(cat_ref.dtype)
        mix = _dot(cat_ref[...], wo_ref[...])
        mix_ref[...] = mix
        for s in range(TM // SB):
            rows = slice(s * SB, (s + 1) * SB)
            x1_ref[rows, :] = x_ref[rows, :] + bm_ref[s, 2:3, :] * mix[rows, :]

    return pl.pallas_call(
        body, name="mix_fwd", grid=(R // TM,),
        in_specs=[_rowspec(D), _rowspec(QW), _rowspec(SSD_INNER), _rowspec(POOL_DIM),
                  pl.BlockSpec((TM // SB, 8, D), lambda i: (i, 0, 0)), _fullspec((CAT, D))],
        out_specs=[_rowspec(D), _rowspec(D), _rowspec(CAT)],
        out_shape=[jax.ShapeDtypeStruct((R, D), F32), jax.ShapeDtypeStruct((R, D), F32),
                   jax.ShapeDtypeStruct((R, CAT), MXU)],
        compiler_params=_cp(48),
    )(x, attn, ssd, pool, bm, wo)


def mix_bwd(dx1, mix, bm, wot):
    R = dx1.shape[0]

    def body(dx1_ref, mix_ref, bm_ref, wot_ref, da_ref, ds_ref, dpl_ref, dmb_ref, part_ref):
        for s in range(TM // SB):
            rows = slice(s * SB, (s + 1) * SB)
            d = dx1_ref[rows, :]
            dmb_ref[rows, :] = (d * bm_ref[s, 2:3, :]).astype(dmb_ref.dtype)
            part_ref[s] = jnp.concatenate([_colsum(d * mix_ref[rows, :]), jnp.zeros((7, D), F32)], axis=0)
        dcat = _dot(dmb_ref[...], wot_ref[...])
        da_ref[...] = dcat[:, 0:QW]
        ds_ref[...] = dcat[:, QW:QW + SSD_INNER]
        dpl_ref[...] = dcat[:, QW + SSD_INNER:CAT]

    return pl.pallas_call(
        body, name="mix_bwd", grid=(R // TM,),
        in_specs=[_rowspec(D), _rowspec(D), pl.BlockSpec((TM // SB, 8, D), lambda i: (i, 0, 0)),
                  _fullspec((D, CAT))],
        out_specs=[_rowspec(QW), _rowspec(SSD_INNER), _rowspec(POOL_DIM), _rowspec(D),
                   pl.BlockSpec((TM // SB, 8, D), lambda i: (i, 0, 0))],
        out_shape=[jax.ShapeDtypeStruct((R, QW), F32), jax.ShapeDtypeStruct((R, SSD_INNER), F32),
                   jax.ShapeDtypeStruct((R, POOL_DIM), F32), jax.ShapeDtypeStruct((R, D), MXU),
                   jax.ShapeDtypeStruct((R // SB, 8, D), F32)],
        compiler_params=_cp(48),
    )(dx1, mix, bm, wot)


def mlp_fwd(x1, bm, nw, w1, w2):
    R = x1.shape[0]
    nj = D_FF // FF_BLK

    def body(x1_ref, bm_ref, nw_ref, w1_ref, w2_ref, x2_ref, mo_ref, r_ref, h2_ref, acc_ref):
        j = pl.program_id(1)

        @pl.when(j == 0)
        def _():
            for s in range(TM // SB):
                rows = slice(s * SB, (s + 1) * SB)
                xhat, _ = _rms_hat(x1_ref[rows, :])
                h = xhat * nw_ref[...] * (1.0 + bm_ref[s, 4:5, :]) + bm_ref[s, 3:4, :]
                h2_ref[rows, :] = h.astype(h2_ref.dtype)
            acc_ref[...] = jnp.zeros_like(acc_ref)

        r = jnp.maximum(_dot(h2_ref[...], w1_ref[...]), 0.0)
        r_ref[...] = r.astype(r_ref.dtype)
        acc_ref[...] += _dot((r * r).astype(MXU), w2_ref[...])

        @pl.when(j == nj - 1)
        def _():
            mo_ref[...] = acc_ref[...]
            for s in range(TM // SB):
                rows = slice(s * SB, (s + 1) * SB)
                x2_ref[rows, :] = x1_ref[rows, :] + bm_ref[s, 5:6, :] * acc_ref[rows, :]

    return pl.pallas_call(
        body, name="mlp_fwd", grid=(R // TM, nj),
        in_specs=[pl.BlockSpec((TM, D), lambda i, j: (i, 0)),
                  pl.BlockSpec((TM // SB, 8, D), lambda i, j: (i, 0, 0)),
                  pl.BlockSpec((1, D), lambda i, j: (0, 0)),
                  pl.BlockSpec((D, FF_BLK), lambda i, j: (0, j)),
                  pl.BlockSpec((FF_BLK, D), lambda i, j: (j, 0))],
        out_specs=[pl.BlockSpec((TM, D), lambda i, j: (i, 0)), pl.BlockSpec((TM, D), lambda i, j: (i, 0)),
                   pl.BlockSpec((TM, FF_BLK), lambda i, j: (i, j)), pl.BlockSpec((TM, D), lambda i, j: (i, 0))],
        out_shape=[jax.ShapeDtypeStruct((R, D), F32), jax.ShapeDtypeStruct((R, D), F32),
                   jax.ShapeDtypeStruct((R, D_FF), BF16), jax.ShapeDtypeStruct((R, D), MXU)],
        scratch_shapes=[pltpu.VMEM((TM, D), F32)],
        compiler_params=_cp(48),
    )(x1, bm, nw, w1, w2)


def mlp_bwd(dx2, x1, mo, r, bm, nw, w2t, w1t):
    R = x1.shape[0]
    nj = D_FF // FF_BLK

    def body(dx2_ref, x1_ref, mo_ref, r_ref, bm_ref, nw_ref, w2t_ref, w1t_ref, dx1_ref, du_ref, dob_ref, part_ref,
             acc_ref):
        j = pl.program_id(1)

        @pl.when(j == 0)
        def _():
            for s in range(TM // SB):
                rows = slice(s * SB, (s + 1) * SB)
                dob_ref[rows, :] = (dx2_ref[rows, :] * bm_ref[s, 5:6, :]).astype(dob_ref.dtype)
            acc_ref[...] = jnp.zeros_like(acc_ref)

        du = _dot(dob_ref[...], w2t_ref[...]) * (2.0 * r_ref[...].astype(F32))
        du_ref[...] = du.astype(du_ref.dtype)
        acc_ref[...] += _dot(du_ref[...], w1t_ref[...])

        @pl.when(j == nj - 1)
        def _():
            w = nw_ref[...]
            for s in range(TM // SB):
                rows = slice(s * SB, (s + 1) * SB)
                xhat, rstd = _rms_hat(x1_ref[rows, :])
                dh = acc_ref[rows, :]
                dx, dnw = _rms_bwd(dh * (1.0 + bm_ref[s, 4:5, :]), xhat, rstd, w)
                d2 = dx2_ref[rows, :]
                dx1_ref[rows, :] = d2 + dx
                part_ref[s] = jnp.concatenate(
                    [_colsum(dh), _colsum(dh * xhat * w), _colsum(d2 * mo_ref[rows, :]), dnw,
                     jnp.zeros((4, D), F32)], axis=0)

    return pl.pallas_call(
        body, name="mlp_bwd", grid=(R // TM, nj),
        in_specs=[pl.BlockSpec((TM, D), lambda i, j: (i, 0)), pl.BlockSpec((TM, D), lambda i, j: (i, 0)),
                  pl.BlockSpec((TM, D), lambda i, j: (i, 0)), pl.BlockSpec((TM, FF_BLK), lambda i, j: (i, j)),
                  pl.BlockSpec((TM // SB, 8, D), lambda i, j: (i, 0, 0)),
                  pl.BlockSpec((1, D), lambda i, j: (0, 0)),
                  pl.BlockSpec((D, FF_BLK), lambda i, j: (0, j)),
                  pl.BlockSpec((FF_BLK, D), lambda i, j: (j, 0))],
        out_specs=[pl.BlockSpec((TM, D), lambda i, j: (i, 0)), pl.BlockSpec((TM, FF_BLK), lambda i, j: (i, j)),
                   pl.BlockSpec((TM, D), lambda i, j: (i, 0)),
                   pl.BlockSpec((TM // SB, 8, D), lambda i, j: (i, 0, 0))],
        out_shape=[jax.ShapeDtypeStruct((R, D), F32), jax.ShapeDtypeStruct((R, D_FF), MXU),
                   jax.ShapeDtypeStruct((R, D), MXU), jax.ShapeDtypeStruct((R // SB, 8, D), F32)],
        scratch_shapes=[pltpu.VMEM((TM, D), F32)],
        compiler_params=_cp(48),
    )(dx2, x1, mo, r, bm, nw, w2t, w1t)


def mm_tn(a, b, square_a=False, name="mm_tn"):
    R, M = a.shape
    N = b.shape[1]
    tm = M if M <= 1408 else 1024
    tn = N if N <= 2176 else 1024
    tk = 512 if R % 512 == 0 else R

    def body(a_ref, b_ref, o_ref):
        @pl.when(pl.program_id(2) == 0)
        def _():
            o_ref[...] = jnp.zeros_like(o_ref)

        av = a_ref[...]
        if square_a:
            av = av.astype(F32)
            av = (av * av).astype(MXU)
        o_ref[...] += _dotg(av.astype(MXU), b_ref[...].astype(MXU), TN)

    return pl.pallas_call(
        body, name=name, grid=(M // tm, N // tn, R // tk),
        in_specs=[pl.BlockSpec((tk, tm), lambda i, j, k: (k, i)), pl.BlockSpec((tk, tn), lambda i, j, k: (k, j))],
        out_specs=pl.BlockSpec((tm, tn), lambda i, j, k: (i, j)),
        out_shape=jax.ShapeDtypeStruct((M, N), F32),
        compiler_params=_cp(48),
    )(a, b)


def final_loss(x, tgt, fw, blocks_per_sample):
    R = x.shape[0]
    nxb = blocks_per_sample - 1

    def body(x_ref, t_ref, fw_ref, dx_ref, part_ref):
        i = pl.program_id(0)
        is_ctx = (i % blocks_per_sample) == 0
        xhat, rstd = _rms_hat(x_ref[...])
        w = fw_ref[...]
        err = xhat * w - t_ref[...]
        dx, dfw = _rms_bwd(err * (1.0 / D), xhat, rstd, w)
        keep = jnp.where(is_ctx, 0.0, 1.0)
        dx_ref[...] = dx * keep
        part_ref[0] = jnp.concatenate([dfw * keep, _colsum(err * err) * keep, jnp.zeros((6, D), F32)], axis=0)

    def tmap(i):
        return ((i // blocks_per_sample) * nxb + jnp.maximum(i % blocks_per_sample - 1, 0), 0)

    return pl.pallas_call(
        body, name="final_loss", grid=(R // SB,),
        in_specs=[_rowspec(D, SB), pl.BlockSpec((SB, D), tmap), _fullspec((1, D))],
        out_specs=[_rowspec(D, SB), pl.BlockSpec((1, 8, D), lambda i: (i, 0, 0))],
        out_shape=[jax.ShapeDtypeStruct((R, D), F32), jax.ShapeDtypeStruct((R // SB, 8, D), F32)],
    )(x, tgt, fw)


def _softplus(v):
    return jnp.maximum(v, 0.0) + jnp.log(1.0 + jnp.exp(-jnp.abs(v)))


def _conv_out(ext, cw_ref, cb_ref):
    return (cb_ref[...] + cw_ref[0:1, :] * _shift(ext, -1) + cw_ref[1:2, :] * _shift(ext, 0)
            + cw_ref[2:3, :] * _shift(ext, 1) + cw_ref[3:4, :] * _shift(ext, 2))


def _dt_dir(v, d):
    lane = lax.broadcasted_iota(jnp.int32, v.shape, 1)
    return jnp.where(lane < SSD_HEADS, pltpu.roll(v, (128 - DT0 - SSD_HEADS * d) % 128, axis=1), 0.0)


def ssd_prep(pxbc, plast, cw, cb, dtb, blocks_per_sample):
    R = pxbc.shape[0]
    prev, nxt = _halo_specs(XBC, R)

    def body(cur_ref, prev_ref, nxt_ref, pl_ref, cw_ref, cb_ref, dtb_ref, xs_ref, bm_ref, cm_ref, dt_ref):
        i = pl.program_id(0)
        ext = _ext_rows(cur_ref[...], prev_ref[...], nxt_ref[...], i, blocks_per_sample)
        co = _conv_out(ext, cw_ref, cb_ref)
        a = co * _sigmoid(co)
        xs_ref[...] = a[:, 0:384]
        bm_ref[...] = a[:, 384:640]
        cm_ref[...] = a[:, 640:896]
        sp = _softplus(pl_ref[...] + dtb_ref[...])
        dt_ref[0] = _dt_dir(sp, 0)
        dt_ref[1] = _dt_dir(sp, 1)

    return pl.pallas_call(
        body, name="ssd_prep", grid=(R // SB,),
        in_specs=[_rowspec(XBC, SB), prev, nxt, _rowspec(128, SB), _fullspec((8, XBC)), _fullspec((1, XBC)),
                  _fullspec((1, 128))],
        out_specs=[_rowspec(384, SB), _rowspec(256, SB), _rowspec(256, SB),
                   pl.BlockSpec((2, SB, 128), lambda i: (0, i, 0))],
        out_shape=[jax.ShapeDtypeStruct((R, 384), F32), jax.ShapeDtypeStruct((R, 256), F32),
                   jax.ShapeDtypeStruct((R, 256), F32), jax.ShapeDtypeStruct((2, R, 128), F32)],
    )(pxbc, pxbc, pxbc, plast, cw, cb, dtb)


def _chunk_index(d, s, nc):
    nctx = CTX // CHUNK
    back = jnp.where(s < nctx, nctx - 1 - s, nc + nctx - 1 - s)
    return jnp.where(d == 0, s, back)


def _scan_common(d, dt, arow, eexp, xs):
    ii = lax.broadcasted_iota(jnp.int32, (CHUNK, CHUNK), 0)
    jj = lax.broadcasted_iota(jnp.int32, (CHUNK, CHUNK), 1)
    mask = ((ii - jj) * (1 - 2 * d)) >= 0
    adt = dt * arow
    tmat = jnp.where(mask, 1.0, 0.0)
    cs = _dot_hi(tmat, adt)
    tot = _colsum(adt)
    dtx = _dot_hi(dt, eexp)
    xt = xs * dtx
    ecs = jnp.exp(cs)
    ecx = _dot_hi(ecs, eexp)
    dte = jnp.exp(tot - cs)
    dtex = _dot_hi(dte, eexp)
    etot = jnp.exp(tot)
    etx = _dot_hi(jnp.broadcast_to(etot, (8, 128)), eexp)[0:1, :]
    return mask, tmat, adt, cs, tot, dtx, xt, ecs, ecx, dte, dtex, etot, etx


def _decay_matrix(mask, cs, cst, h):
    return jnp.exp(jnp.where(mask, cs[:, h:h + 1] - cst[h:h + 1, :], -1e30))


def ssd_scan_fwd(xs, bm, cm, dtv, arow, eexp, nb, T):
    R = xs.shape[0]
    nc = T // CHUNK

    def rowmap(b, d, s):
        return (b * nc + _chunk_index(d, s, nc), 0)

    def body(xs_ref, bm_ref, cm_ref, dt_ref, a_ref, e_ref, y_ref, hin_ref, st_ref):
        d = pl.program_id(1)
        s = pl.program_id(2)

        @pl.when(s == 0)
        def _():
            st_ref[...] = jnp.zeros_like(st_ref)

        eexp = e_ref[...]
        mask, _, _, cs, _, _, xt, _, ecx, _, dtex, _, etx = _scan_common(
            d, dt_ref[0], a_ref[0, 0:1, :], eexp, xs_ref[...])
        cst = cs.T
        sin = st_ref[...]
        hin_ref[0] = sin
        sb = sin.astype(MXU)
        xtb = xt.astype(MXU)
        xw = (xt * dtex).astype(MXU)
        g0 = lax.broadcasted_iota(jnp.int32, (CHUNK, SSD_INNER), 1) < 192
        lane = lax.broadcasted_iota(jnp.int32, (CHUNK, 128), 1)
        c = [cm_ref[:, 0:128].astype(MXU), cm_ref[:, 128:256].astype(MXU)]
        b = [bm_ref[:, 0:128].astype(MXU), bm_ref[:, 128:256].astype(MXU)]
        y = jnp.where(g0, _dot(c[0], sb), _dot(c[1], sb)) * ecx
        cb = [_dotg(c[0], b[0], NT), _dotg(c[1], b[1], NT)]
        blocks = []
        for blk in range(3):
            acc = None
            for hh in range(2):
                h = blk * 2 + hh
                m = (cb[h // 3] * _decay_matrix(mask, cs, cst, h)).astype(MXU)
                res = _dot(m, xtb[:, blk * 128:(blk + 1) * 128])
                acc = res if hh == 0 else jnp.where(lane < 64, acc, res)
            blocks.append(acc)
        y_ref[0] = y + jnp.concatenate(blocks, axis=1)
        st_ref[...] = sin * etx + jnp.where(g0, _dotg(b[0], xw, TN), _dotg(b[1], xw, TN))

    return pl.pallas_call(
        body, name="ssd_scan_fwd", grid=(nb, 2, nc),
        in_specs=[pl.BlockSpec((CHUNK, 384), rowmap), pl.BlockSpec((CHUNK, 256), rowmap),
                  pl.BlockSpec((CHUNK, 256), rowmap),
                  pl.BlockSpec((1, CHUNK, 128), lambda b, d, s: (d, b * nc + _chunk_index(d, s, nc), 0)),
                  pl.BlockSpec((1, 8, 128), lambda b, d, s: (d, 0, 0)),
                  pl.BlockSpec((128, 384), lambda b, d, s: (0, 0))],
        out_specs=[pl.BlockSpec((1, CHUNK, 384), lambda b, d, s: (d, b * nc + _chunk_index(d, s, nc), 0)),
                   pl.BlockSpec((1, CHUNK, 384), lambda b, d, s: ((b * 2 + d) * nc + _chunk_index(d, s, nc), 0, 0))],
        out_shape=[jax.ShapeDtypeStruct((2, R, 384), F32), jax.ShapeDtypeStruct((nb * 2 * nc, CHUNK, 384), F32)],
        scratch_shapes=[pltpu.VMEM((CHUNK, 384), F32)],
    )(xs, bm, cm, dtv, arow, eexp)


def ssd_scan_bwd(xs, bm, cm, dtv, arow, eexp, hin, dy, nb, T):
    R = xs.shape[0]
    nc = T // CHUNK

    def chunk(d, s):
        return _chunk_index(d, nc - 1 - s, nc)

    def rowmap(b, d, s):
        return (b * nc + chunk(d, s), 0)

    def dirmap(b, d, s):
        return (d, b * nc + chunk(d, s), 0)

    def body(xs_ref, bm_ref, cm_ref, dt_ref, a_ref, e_ref, hin_ref, dy_ref,
             dxs_ref, dbm_ref, dcm_ref, ddt_ref, da_ref, ds_ref):
        d = pl.program_id(1)
        s = pl.program_id(2)

        @pl.when(s == 0)
        def _():
            ds_ref[...] = jnp.zeros_like(ds_ref)
            da_ref[...] = jnp.zeros_like(da_ref)

        eexp = e_ref[...]
        dt = dt_ref[0]
        arow = a_ref[0, 0:1, :]
        xs_v = xs_ref[...]
        mask, tmat, adt, cs, tot, dtx, xt, ecs, ecx, dte, dtex, etot, etx = _scan_common(d, dt, arow, eexp, xs_v)
        cst = cs.T
        sin = hin_ref[0]
        sb = sin.astype(MXU)
        dsp = ds_ref[...]
        dyv = dy_ref[...]
        xtb = xt.astype(MXU)
        xw = (xt * dtex).astype(MXU)
        g0 = lax.broadcasted_iota(jnp.int32, (CHUNK, SSD_INNER), 1) < 192
        lane = lax.broadcasted_iota(jnp.int32, (CHUNK, 128), 1)
        sub = lax.broadcasted_iota(jnp.int32, (CHUNK, 128), 0)
        c = [cm_ref[:, 0:128].astype(MXU), cm_ref[:, 128:256].astype(MXU)]
        b = [bm_ref[:, 0:128].astype(MXU), bm_ref[:, 128:256].astype(MXU)]

        cs_prod = jnp.where(g0, _dot(c[0], sb), _dot(c[1], sb))
        dcsp = dyv * ecx
        dcsp_g = [jnp.where(g0, dcsp, 0.0).astype(MXU), jnp.where(g0, 0.0, dcsp).astype(MXU)]
        dcs = _dot_hi(dyv * cs_prod, eexp, NT) * ecs
        dc = [_dotg(dcsp_g[0], sb, NT), _dotg(dcsp_g[1], sb, NT)]
        dsin = _dotg(c[0], dcsp_g[0], TN) + _dotg(c[1], dcsp_g[1], TN) + dsp * etx

        detx = _colsum(dsp * sin)
        dtot = _dot_hi(jnp.broadcast_to(detx, (8, SSD_INNER)), eexp, NT)[0:1, :] * etot
        dsp_g = [jnp.where(g0, dsp, 0.0).astype(MXU), jnp.where(g0, 0.0, dsp).astype(MXU)]
        dxw = _dot(b[0], dsp_g[0]) + _dot(b[1], dsp_g[1])
        db = [_dotg(xw, dsp_g[0], NT), _dotg(xw, dsp_g[1], NT)]
        dxt = dxw * dtex
        ddte = _dot_hi(dxw * xt, eexp, NT) * dte
        dtot = dtot + _colsum(ddte)
        dcs = dcs - ddte

        cb = [_dotg(c[0], b[0], NT), _dotg(c[1], b[1], NT)]
        dg = [jnp.zeros((CHUNK, CHUNK), F32), jnp.zeros((CHUNK, CHUNK), F32)]
        dcs_rows = jnp.zeros((CHUNK, 128), F32)
        dxt_blocks = []
        for blk in range(3):
            acc = jnp.zeros((CHUNK, 128), F32)
            for hh in range(2):
                h = blk * 2 + hh
                g = h // 3
                mine = (lane < 64) if hh == 0 else (lane >= 64)
                dyh = jnp.where(mine, dyv[:, blk * 128:(blk + 1) * 128], 0.0).astype(MXU)
                lh = _decay_matrix(mask, cs, cst, h)
                m = cb[g] * lh
                dm = _dotg(dyh, xtb[:, blk * 128:(blk + 1) * 128], NT)
                acc = acc + _dotg(m.astype(MXU), dyh, TN)
                dg[g] = dg[g] + dm * lh
                q = dm * m
                dcs = dcs + jnp.where(lane == h, jnp.sum(q, axis=1, keepdims=True), 0.0)
                dcs_rows = dcs_rows - jnp.where(sub == h, jnp.sum(q, axis=0, keepdims=True), 0.0)
            dxt_blocks.append(acc)
        dxt = dxt + jnp.concatenate(dxt_blocks, axis=1)
        for g in range(2):
            dgb = dg[g].astype(MXU)
            dc[g] = dc[g] + _dot(dgb, b[g])
            db[g] = db[g] + _dotg(dgb, c[g], TN)
        dcs = dcs + dcs_rows.T

        dadt = _dot_hi(tmat, dcs, TN) + dtot
        ddt = dadt * arow + _dot_hi(dxt * xs_v, eexp, NT)
        da_ref[0, 0:1, :] += _colsum(dadt * dt)
        dxs_ref[0] = dxt * dtx
        dbm_ref[0] = jnp.concatenate(db, axis=1)
        dcm_ref[0] = jnp.concatenate(dc, axis=1)
        ddt_ref[0] = ddt
        ds_ref[...] = dsin

    return pl.pallas_call(
        body, name="ssd_scan_bwd", grid=(nb, 2, nc),
        in_specs=[pl.BlockSpec((CHUNK, 384), rowmap), pl.BlockSpec((CHUNK, 256), rowmap),
                  pl.BlockSpec((CHUNK, 256), rowmap), pl.BlockSpec((1, CHUNK, 128), dirmap),
                  pl.BlockSpec((1, 8, 128), lambda b, d, s: (d, 0, 0)),
                  pl.BlockSpec((128, 384), lambda b, d, s: (0, 0)),
                  pl.BlockSpec((1, CHUNK, 384), lambda b, d, s: ((b * 2 + d) * nc + chunk(d, s), 0, 0)),
                  pl.BlockSpec((CHUNK, 384), rowmap)],
        out_specs=[pl.BlockSpec((1, CHUNK, 384), dirmap), pl.BlockSpec((1, CHUNK, 256), dirmap),
                   pl.BlockSpec((1, CHUNK, 256), dirmap), pl.BlockSpec((1, CHUNK, 128), dirmap),
                   pl.BlockSpec((1, 8, 128), lambda b, d, s: (b * 2 + d, 0, 0))],
        out_shape=[jax.ShapeDtypeStruct((2, R, 384), F32), jax.ShapeDtypeStruct((2, R, 256), F32),
                   jax.ShapeDtypeStruct((2, R, 256), F32), jax.ShapeDtypeStruct((2, R, 128), F32),
                   jax.ShapeDtypeStruct((nb * 2, 8, 128), F32)],
        scratch_shapes=[pltpu.VMEM((CHUNK, 384), F32)],
    )(xs, bm, cm, dtv, arow, eexp, hin, dy)


def _group_rms(g):
    lane = lax.broadcasted_iota(jnp.int32, g.shape, 1)
    g0 = lane < 192
    gg = g * g
    s0 = jnp.sum(jnp.where(g0, gg, 0.0), axis=-1, keepdims=True)
    s1 = jnp.sum(gg, axis=-1, keepdims=True) - s0
    rstd = jnp.where(g0, lax.rsqrt(s0 * (1.0 / 192) + EPS), lax.rsqrt(s1 * (1.0 / 192) + EPS))
    return rstd, g0


def ssd_out_fwd(y2, xs, pz, dexp, nw):
    R = xs.shape[0]

    def body(y_ref, xs_ref, z_ref, d_ref, nw_ref, o_ref):
        z = z_ref[...]
        yy = y_ref[0] + y_ref[1] + xs_ref[...] * d_ref[...]
        g = yy * (z * _sigmoid(z))
        rstd, _ = _group_rms(g)
        o_ref[...] = g * rstd * nw_ref[...]

    return pl.pallas_call(
        body, name="ssd_out_fwd", grid=(R // TM,),
        in_specs=[pl.BlockSpec((2, TM, 384), lambda i: (0, i, 0)), _rowspec(384), _rowspec(384),
                  _fullspec((1, 384)), _fullspec((1, 384))],
        out_specs=_rowspec(384),
        out_shape=jax.ShapeDtypeStruct((R, 384), F32),
    )(y2, xs, pz, dexp, nw)


def ssd_out_bwd(dout, y2, xs, pz, dexp, nw):
    R = xs.shape[0]

    def body(do_ref, y_ref, xs_ref, z_ref, d_ref, nw_ref, dy_ref, dz_ref, dxs_ref, part_ref):
        z = z_ref[...]
        xs_v = xs_ref[...]
        yy = y_ref[0] + y_ref[1] + xs_v * d_ref[...]
        sig = _sigmoid(z)
        sz = z * sig
        g = yy * sz
        rstd, g0 = _group_rms(g)
        ghat = g * rstd
        do = do_ref[...]
        dgn = do * nw_ref[...]
        t = dgn * ghat
        t0 = jnp.sum(jnp.where(g0, t, 0.0), axis=-1, keepdims=True)
        t1 = jnp.sum(t, axis=-1, keepdims=True) - t0
        dg = rstd * (dgn - ghat * jnp.where(g0, t0, t1) * (1.0 / 192))
        dyy = dg * sz
        dy_ref[...] = dyy
        dz_ref[...] = dg * yy * (sig * (1.0 + z * (1.0 - sig)))
        dxs_ref[...] = dyy * d_ref[...]
        part_ref[0] = jnp.concatenate([_colsum(do * ghat), _colsum(dyy * xs_v), jnp.zeros((6, 384), F32)], axis=0)

    return pl.pallas_call(
        body, name="ssd_out_bwd", grid=(R // TM,),
        in_specs=[_rowspec(384), pl.BlockSpec((2, TM, 384), lambda i: (0, i, 0)), _rowspec(384), _rowspec(384),
                  _fullspec((1, 384)), _fullspec((1, 384))],
        out_specs=[_rowspec(384), _rowspec(384), _rowspec(384), pl.BlockSpec((1, 8, 384), lambda i: (i, 0, 0))],
        out_shape=[jax.ShapeDtypeStruct((R, 384), F32), jax.ShapeDtypeStruct((R, 384), F32),
                   jax.ShapeDtypeStruct((R, 384), F32), jax.ShapeDtypeStruct((R // TM, 8, 384), F32)],
    )(dout, y2, xs, pz, dexp, nw)


def ssd_prep_bwd_a(pxbc, plast, cw, cb, dtb, dxs_skip, dxs2, dbm2, dcm2, ddt2, blocks_per_sample):
    R = pxbc.shape[0]
    prev, nxt = _halo_specs(XBC, R)

    def body(cur_ref, prev_ref, nxt_ref, pl_ref, cw_ref, cb_ref, dtb_ref, dsk_ref, dxs_ref, dbm_ref, dcm_ref, ddt_ref,
             dpre_ref, dlast_ref, part_ref):
        i = pl.program_id(0)
        ext = _ext_rows(cur_ref[...], prev_ref[...], nxt_ref[...], i, blocks_per_sample)
        co = _conv_out(ext, cw_ref, cb_ref)
        sig = _sigmoid(co)
        up = jnp.concatenate([dsk_ref[...] + dxs_ref[0] + dxs_ref[1], dbm_ref[0] + dbm_ref[1],
                              dcm_ref[0] + dcm_ref[1]], axis=1)
        dpre = up * (sig * (1.0 + co * (1.0 - sig)))
        dpre_ref[...] = dpre
        raw = pl_ref[...] + dtb_ref[...]
        lane = lax.broadcasted_iota(jnp.int32, raw.shape, 1)
        ddt = (pltpu.roll(ddt_ref[0], DT0, axis=1) + pltpu.roll(ddt_ref[1], DT0 + SSD_HEADS, axis=1))
        ddt = jnp.where(jnp.logical_and(lane >= DT0, lane < DT0 + 2 * SSD_HEADS), ddt * _sigmoid(raw), 0.0)
        dlast_ref[...] = ddt
        rows = [_colsum(dpre * _shift(ext, k - 1)) for k in range(4)]
        rows.append(_colsum(dpre))
        rows.append(jnp.concatenate([_colsum(ddt), jnp.zeros((1, XBC - 128), F32)], axis=1))
        rows.append(jnp.zeros((2, XBC), F32))
        part_ref[0] = jnp.concatenate(rows, axis=0)

    dirspec = lambda n: pl.BlockSpec((2, SB, n), lambda i: (0, i, 0))
    return pl.pallas_call(
        body, name="ssd_prep_bwd_a", grid=(R // SB,),
        in_specs=[_rowspec(XBC, SB), prev, nxt, _rowspec(128, SB), _fullspec((8, XBC)), _fullspec((1, XBC)),
                  _fullspec((1, 128)), _rowspec(384, SB), dirspec(384), dirspec(256), dirspec(256), dirspec(128)],
        out_specs=[_rowspec(XBC, SB), _rowspec(128, SB), pl.BlockSpec((1, 8, XBC), lambda i: (i, 0, 0))],
        out_shape=[jax.ShapeDtypeStruct((R, XBC), F32), jax.ShapeDtypeStruct((R, 128), F32),
                   jax.ShapeDtypeStruct((R // SB, 8, XBC), F32)],
    )(pxbc, pxbc, pxbc, plast, cw, cb, dtb, dxs_skip, dxs2, dbm2, dcm2, ddt2)


def ssd_prep_bwd_b(dpre, cw, blocks_per_sample):
    R = dpre.shape[0]
    prev, nxt = _halo_specs(XBC, R)

    def body(cur_ref, prev_ref, nxt_ref, cw_ref, o_ref):
        i = pl.program_id(0)
        ext = _ext_rows(cur_ref[...], prev_ref[...], nxt_ref[...], i, blocks_per_sample)
        o_ref[...] = (cw_ref[0:1, :] * _shift(ext, 1) + cw_ref[1:2, :] * _shift(ext, 0)
                      + cw_ref[2:3, :] * _shift(ext, -1) + cw_ref[3:4, :] * _shift(ext, -2))

    return pl.pallas_call(
        body, name="ssd_prep_bwd_b", grid=(R // SB,),
        in_specs=[_rowspec(XBC, SB), prev, nxt, _fullspec((8, XBC))],
        out_specs=_rowspec(XBC, SB),
        out_shape=jax.ShapeDtypeStruct((R, XBC), F32),
    )(dpre, dpre, dpre, cw)


def _rope(u, cos, sa, sb):
    return u * cos + pltpu.roll(u, 120, axis=1) * sa + pltpu.roll(u, 8, axis=1) * sb


def _rope_t(du, cos, sa, sb):
    return du * cos + pltpu.roll(du * sa, 8, axis=1) + pltpu.roll(du * sb, 120, axis=1)


def mla_prep(pqa, pkva, plast, qnw, kvnw, wq, wk, wv, cos, sa, sb):
    R = pqa.shape[0]

    def body(qa_ref, kva_ref, pl_ref, qnw_ref, kvnw_ref, wq_ref, wk_ref, wv_ref, cos_ref, sa_ref, sb_ref,
             q_ref, k_ref, v_ref, cq_ref, ckv_ref):
        cos_v, sa_v, sb_v = cos_ref[...], sa_ref[...], sb_ref[...]
        xq, _ = _rms_hat(qa_ref[...])
        cq_ref[...] = (xq * qnw_ref[...]).astype(cq_ref.dtype)
        xkv, _ = _rms_hat(kva_ref[...])
        ckv_ref[...] = (xkv * kvnw_ref[...]).astype(ckv_ref.dtype)
        q = _dot(cq_ref[...], wq_ref[...])
        kn = _dot(ckv_ref[...], wk_ref[...])
        v_ref[...] = _dot(ckv_ref[...], wv_ref[...]).astype(v_ref.dtype)
        lane = lax.broadcasted_iota(jnp.int32, (TM, HP), 1)
        rope_lanes = jnp.logical_and(lane >= QK_NOPE, lane < QK_DIM)
        kr = _rope(jnp.where(rope_lanes, pltpu.roll(pl_ref[...], QK_NOPE, axis=1), 0.0), cos_v, sa_v, sb_v)
        for h in range(MLA_HEADS):
            cols = slice(h * HP, (h + 1) * HP)
            q_ref[:, cols] = _rope(q[:, cols], cos_v, sa_v, sb_v).astype(q_ref.dtype)
            k_ref[:, cols] = (kn[:, cols] + kr).astype(k_ref.dtype)

    return pl.pallas_call(
        body, name="mla_prep", grid=(R // TM,),
        in_specs=[_rowspec(256), _rowspec(256), _rowspec(128), _fullspec((1, 256)), _fullspec((1, 256)),
                  _fullspec((256, QW)), _fullspec((256, QW)), _fullspec((256, QW)),
                  _rowspec(HP), _rowspec(HP), _rowspec(HP)],
        out_specs=[_rowspec(QW), _rowspec(QW), _rowspec(QW), _rowspec(256), _rowspec(256)],
        out_shape=[jax.ShapeDtypeStruct((R, QW), MXU)] * 3 + [jax.ShapeDtypeStruct((R, 256), MXU)] * 2,
    )(pqa, pkva, plast, qnw, kvnw, wq, wk, wv, cos, sa, sb)


def mla_prep_bwd(dq, dk, dv, pqa, pkva, qnw, kvnw, wqt, wkt, wvt, cos, sa, sb):
    R = pqa.shape[0]

    def body(dq_ref, dk_ref, dv_ref, qa_ref, kva_ref, qnw_ref, kvnw_ref, wqt_ref, wkt_ref, wvt_ref,
             cos_ref, sa_ref, sb_ref, dqa_ref, dkva_ref, dkr_ref, dql_ref, dkm_ref, dvb_ref, part_ref):
        cos_v, sa_v, sb_v = cos_ref[...], sa_ref[...], sb_ref[...]
        lane = lax.broadcasted_iota(jnp.int32, (TM, HP), 1)
        rope_lanes = jnp.logical_and(lane >= QK_NOPE, lane < QK_DIM)
        dkr = jnp.zeros((TM, HP), F32)
        for h in range(MLA_HEADS):
            cols = slice(h * HP, (h + 1) * HP)
            dql_ref[:, cols] = _rope_t(dq_ref[:, cols], cos_v, sa_v, sb_v).astype(dql_ref.dtype)
            dkh = dk_ref[:, cols]
            dkm_ref[:, cols] = jnp.where(lane < QK_NOPE, dkh, 0.0).astype(dkm_ref.dtype)
            dkr = dkr + jnp.where(rope_lanes, dkh, 0.0)
        dvb_ref[...] = dv_ref[...].astype(dvb_ref.dtype)
        dkr = jnp.where(rope_lanes, _rope_t(dkr, cos_v, sa_v, sb_v), 0.0)
        dkr_ref[...] = pltpu.roll(dkr, HP - QK_NOPE, axis=1)
        xq, rq = _rms_hat(qa_ref[...])
        dqa, dqnw = _rms_bwd(_dot(dql_ref[...], wqt_ref[...]), xq, rq, qnw_ref[...])
        dqa_ref[...] = dqa
        xkv, rkv = _rms_hat(kva_ref[...])
        dckv = _dot(dkm_ref[...], wkt_ref[...]) + _dot(dvb_ref[...], wvt_ref[...])
        dkva, dkvnw = _rms_bwd(dckv, xkv, rkv, kvnw_ref[...])
        dkva_ref[...] = dkva
        part_ref[0] = jnp.concatenate([dqnw, dkvnw, jnp.zeros((6, 256), F32)], axis=0)

    return pl.pallas_call(
        body, name="mla_prep_bwd", grid=(R // TM,),
        in_specs=[_rowspec(QW), _rowspec(QW), _rowspec(QW), _rowspec(256), _rowspec(256), _fullspec((1, 256)),
                  _fullspec((1, 256)), _fullspec((QW, 256)), _fullspec((QW, 256)), _fullspec((QW, 256)),
                  _rowspec(HP), _rowspec(HP), _rowspec(HP)],
        out_specs=[_rowspec(256), _rowspec(256), _rowspec(128), _rowspec(QW), _rowspec(QW), _rowspec(QW),
                   pl.BlockSpec((1, 8, 256), lambda i: (i, 0, 0))],
        out_shape=[jax.ShapeDtypeStruct((R, 256), F32), jax.ShapeDtypeStruct((R, 256), F32),
                   jax.ShapeDtypeStruct((R, 128), F32)] + [jax.ShapeDtypeStruct((R, QW), MXU)] * 3
                  + [jax.ShapeDtypeStruct((R // TM, 8, 256), F32)],
    )(dq, dk, dv, pqa, pkva, qnw, kvnw, wqt, wkt, wvt, cos, sa, sb)


ATT_SCALE = QK_DIM ** -0.5
TQ = 256


def _softmax_rows(q, k):
    s = _dotg(q, k, NT) * ATT_SCALE
    p = jnp.exp(s - jnp.max(s, axis=-1, keepdims=True))
    return p, jnp.sum(p, axis=-1, keepdims=True)


def attn_fwd(q, k, v, nb, T):
    R = q.shape[0]
    nq = T // TQ

    def body(q_ref, k_ref, v_ref, o_ref):
        def run(nk):
            p, l = _softmax_rows(q_ref[...], k_ref[0:nk, :])
            o_ref[...] = _dot(p.astype(MXU), v_ref[0:nk, :]) / l

        i = pl.program_id(2)
        pl.when(i == 0)(lambda: run(CTX))
        pl.when(i > 0)(lambda: run(T))

    return pl.pallas_call(
        body, name="attn_fwd", grid=(nb, MLA_HEADS, nq),
        in_specs=[pl.BlockSpec((TQ, HP), lambda b, h, i: (b * nq + i, h)),
                  pl.BlockSpec((T, HP), lambda b, h, i: (b, h)), pl.BlockSpec((T, HP), lambda b, h, i: (b, h))],
        out_specs=pl.BlockSpec((TQ, HP), lambda b, h, i: (b * nq + i, h)),
        out_shape=jax.ShapeDtypeStruct((R, QW), F32),
        compiler_params=_cp(48),
    )(q, k, v)


def attn_bwd(q, k, v, o, do, nb, T):
    R = q.shape[0]
    nq = T // TQ

    def body(q_ref, k_ref, v_ref, o_ref, do_ref, dq_ref, dk_ref, dv_ref):
        i = pl.program_id(2)

        @pl.when(i == 0)
        def _():
            dk_ref[...] = jnp.zeros_like(dk_ref)
            dv_ref[...] = jnp.zeros_like(dv_ref)

        def run(nk):
            qv = q_ref[...]
            kv = k_ref[0:nk, :]
            p, l = _softmax_rows(qv, kv)
            p = p / l
            dov = do_ref[...]
            dob = dov.astype(MXU)
            dp = _dotg(dob, v_ref[0:nk, :], NT)
            delta = jnp.sum(dov * o_ref[...], axis=-1, keepdims=True)
            dsb = (p * (dp - delta) * ATT_SCALE).astype(MXU)
            dq_ref[...] = _dot(dsb, kv)
            dk_ref[0:nk, :] += _dotg(dsb, qv, TN)
            dv_ref[0:nk, :] += _dotg(p.astype(MXU), dob, TN)

        pl.when(i == 0)(lambda: run(CTX))
        pl.when(i > 0)(lambda: run(T))

    qspec = pl.BlockSpec((TQ, HP), lambda b, h, i: (b * nq + i, h))
    kspec = pl.BlockSpec((T, HP), lambda b, h, i: (b, h))
    return pl.pallas_call(
        body, name="attn_bwd", grid=(nb, MLA_HEADS, nq),
        in_specs=[qspec, kspec, kspec, qspec, qspec],
        out_specs=[qspec, kspec, kspec],
        out_shape=[jax.ShapeDtypeStruct((R, QW), F32)] * 3,
        compiler_params=_cp(56),
    )(q, k, v, o, do)


def _pool_geometry(i, blocks_per_sample, seq):
    j = i % blocks_per_sample
    n = jnp.where(j == 0, CTX, seq)
    t0 = jnp.where(j == 0, 0, (j - 1) * SB) - HALO
    lane = lax.broadcasted_iota(jnp.int32, (SB + 2 * HALO, POOL_DIM), 1)
    t = lax.broadcasted_iota(jnp.int32, (SB + 2 * HALO, POOL_DIM), 0) + t0
    wh = jnp.where(lane < 64, 1, jnp.where(lane < 128, 2, jnp.where(lane < 192, 4, 8)))
    cnt = jnp.minimum(t + wh, n) - jnp.maximum(t - wh, 0)
    return lane, 1.0 / jnp.maximum(cnt, 1).astype(F32)


def _by_window(lane, c2, c4, c8, c16):
    return jnp.where(lane < 64, c2, jnp.where(lane < 128, c4, jnp.where(lane < 192, c8, c16)))


def _window_sums(ext, lane, first):
    n = ext.shape[0]
    r = lambda a, s: pltpu.roll(a, s % n, axis=0)
    c2 = ext + r(ext, first)
    c4 = r(c2, 1) + r(c2, -1)
    c8 = r(c4, 2) + r(c4, -2)
    c16 = r(c8, 4) + r(c8, -4)
    return _by_window(lane, c2, c4, c8, c16)


def _pool_delta(ext, lane, inv):
    return (_window_sums(ext, lane, 1) * inv - ext)[HALO:HALO + SB, :]


def pool_fwd(ppool, wbd, scale, blocks_per_sample, seq):
    R = ppool.shape[0]
    prev, nxt = _halo_specs(POOL_DIM, R)

    def body(cur_ref, prev_ref, nxt_ref, w_ref, s_ref, o_ref):
        i = pl.program_id(0)
        ext = _ext_rows(cur_ref[...], prev_ref[...], nxt_ref[...], i, blocks_per_sample)
        lane, inv = _pool_geometry(i, blocks_per_sample, seq)
        dlt = _pool_delta(ext, lane, inv)
        o_ref[...] = _dot(dlt.astype(MXU), w_ref[...]) * s_ref[...]

    return pl.pallas_call(
        body, name="pool_fwd", grid=(R // SB,),
        in_specs=[_rowspec(POOL_DIM, SB), prev, nxt, _fullspec((POOL_DIM, POOL_DIM)), _fullspec((1, POOL_DIM))],
        out_specs=_rowspec(POOL_DIM, SB),
        out_shape=jax.ShapeDtypeStruct((R, POOL_DIM), F32),
    )(ppool, ppool, ppool, wbd, scale)


def pool_bwd(ppool, dpool, wbd, wbdt, scale, blocks_per_sample, seq):
    R = ppool.shape[0]
    prev, nxt = _halo_specs(POOL_DIM, R)

    def body(cur_ref, prev_ref, nxt_ref, dcur_ref, dprev_ref, dnxt_ref, w_ref, wt_ref, s_ref, du_ref, dw_ref, part_ref):
        i = pl.program_id(0)

        @pl.when(i == 0)
        def _():
            dw_ref[...] = jnp.zeros_like(dw_ref)

        ext = _ext_rows(cur_ref[...], prev_ref[...], nxt_ref[...], i, blocks_per_sample)
        lane, inv = _pool_geometry(i, blocks_per_sample, seq)
        dlt = _pool_delta(ext, lane, inv).astype(MXU)
        dy = dcur_ref[...]
        part_ref[0] = jnp.concatenate([_colsum(dy * _dot(dlt, w_ref[...])), jnp.zeros((7, POOL_DIM), F32)], axis=0)
        dyp = (dy * s_ref[...]).astype(MXU)
        dw_ref[...] += _dotg(dlt, dyp, TN)
        dext = _ext_rows(dy, dprev_ref[...], dnxt_ref[...], i, blocks_per_sample)
        dd = _dot((dext * s_ref[...]).astype(MXU), wt_ref[...])
        du_ref[...] = (_window_sums(dd * inv, lane, -1) - dd)[HALO:HALO + SB, :]

    return pl.pallas_call(
        body, name="pool_bwd", grid=(R // SB,),
        in_specs=[_rowspec(POOL_DIM, SB), prev, nxt, _rowspec(POOL_DIM, SB), prev, nxt,
                  _fullspec((POOL_DIM, POOL_DIM)), _fullspec((POOL_DIM, POOL_DIM)), _fullspec((1, POOL_DIM))],
        out_specs=[_rowspec(POOL_DIM, SB), _fullspec((POOL_DIM, POOL_DIM)),
                   pl.BlockSpec((1, 8, POOL_DIM), lambda i: (i, 0, 0))],
        out_shape=[jax.ShapeDtypeStruct((R, POOL_DIM), F32), jax.ShapeDtypeStruct((POOL_DIM, POOL_DIM), F32),
                   jax.ShapeDtypeStruct((R // SB, 8, POOL_DIM), F32)],
    )(ppool, ppool, ppool, dpool, dpool, dpool, wbd, wbdt, scale)


def adamw(w, g, m, v, name="adamw"):
    rows, cols = w.shape
    tr = rows
    for cand in (512, 256, 128, 64, 32, 16, 8):
        if rows % cand == 0:
            tr = cand
            break
    bc1 = 1.0 - ADAM_B1 ** ADAM_STEP
    bc2 = 1.0 - ADAM_B2 ** ADAM_STEP

    def body(w_ref, g_ref, m_ref, v_ref, d_ref, nm_ref, nv_ref):
        g_v = g_ref[...]
        nm = ADAM_B1 * m_ref[...] + (1.0 - ADAM_B1) * g_v
        nv = ADAM_B2 * v_ref[...] + (1.0 - ADAM_B2) * (g_v * g_v)
        nm_ref[...] = nm
        nv_ref[...] = nv
        d_ref[...] = -ADAM_LR * ((nm / bc1) / (jnp.sqrt(nv / bc2) + ADAM_EPS) + ADAM_WD * w_ref[...])

    spec = pl.BlockSpec((tr, cols), lambda i: (i, 0))
    return pl.pallas_call(
        body, name=name, grid=(rows // tr,),
        in_specs=[spec] * 4, out_specs=[spec] * 3,
        out_shape=[jax.ShapeDtypeStruct((rows, cols), F32)] * 3,
    )(w, g, m, v)


MODR = 32


def _silu(v):
    return v * _sigmoid(v)


def mod_fwd(cond, w, b):
    n = w.shape[1]

    def body(c_ref, w_ref, b_ref, o_ref):
        o_ref[...] = _dot(_silu(c_ref[...]).astype(MXU), w_ref[...].astype(MXU)) + b_ref[...]

    return pl.pallas_call(
        body, name="mod_fwd", out_shape=jax.ShapeDtypeStruct((MODR, n), F32),
        in_specs=[_fullspec((MODR, D)), _fullspec((D, n)), _fullspec((1, n))], out_specs=_fullspec((MODR, n)),
        grid=(1,), compiler_params=_cp(40),
    )(cond, w, b)


def mod_wgrad(cond, dm):
    n = dm.shape[1]

    def body(c_ref, d_ref, o_ref):
        o_ref[...] = _dotg(_silu(c_ref[...]).astype(MXU), d_ref[...].astype(MXU), TN)

    return pl.pallas_call(
        body, name="mod_wgrad", out_shape=jax.ShapeDtypeStruct((D, n), F32),
        in_specs=[_fullspec((MODR, D)), _fullspec((MODR, n))], out_specs=_fullspec((D, n)),
        grid=(1,), compiler_params=_cp(40),
    )(cond, dm)


def mod_dgrad(dm, w):
    n = w.shape[1]

    def body(d_ref, w_ref, o_ref):
        o_ref[...] = _dotg(d_ref[...].astype(MXU), w_ref[...].astype(MXU), NT)

    return pl.pallas_call(
        body, name="mod_dgrad", out_shape=jax.ShapeDtypeStruct((8, D), F32),
        in_specs=[_fullspec((8, n)), _fullspec((D, n))], out_specs=_fullspec((8, D)),
        grid=(1,), compiler_params=_cp(40),
    )(dm, w)


def sum_leading(a, name="sum_leading"):
    n, r, c = a.shape

    def body(a_ref, o_ref):
        acc = a_ref[0]
        for k in range(1, n):
            acc = acc + a_ref[k]
        o_ref[...] = acc

    return pl.pallas_call(
        body, name=name, out_shape=jax.ShapeDtypeStruct((r, c), F32),
        in_specs=[_fullspec((n, r, c))], out_specs=_fullspec((r, c)), grid=(1,),
    )(a)


MESH = pl.DeviceIdType.MESH
NDEV = 8
ANY = pl.BlockSpec(memory_space=pl.ANY)
PACK_LAYER = (523, 36, 48, 256, 1024, 1024)
PACK_ROWS = 5824
HALF = PACK_ROWS // 2


def _place():
    return lax.axis_index("x"), lax.axis_index("y"), lax.axis_index("c")


def _other_chips(x, y):
    return [(1 - x, y), (x, 1 - y), (1 - x, 1 - y)]


def allgather_small(v, name):
    r, cols = v.shape

    def body(v_ref, o_ref, send_sems, recv_sems):
        x, y, c = _place()
        me = 4 * x + 2 * y + c
        o_ref[me] = v_ref[...]
        copies = []
        for rel in range(1, NDEV):
            peer = (1 - x if rel & 4 else x, 1 - y if rel & 2 else y, 1 - c if rel & 1 else c)
            cp = pltpu.make_async_remote_copy(src_ref=v_ref, dst_ref=o_ref.at[me], send_sem=send_sems.at[rel - 1],
                                              recv_sem=recv_sems.at[rel - 1], device_id=peer, device_id_type=MESH)
            cp.start()
            copies.append(cp)
        for cp in copies:
            cp.wait_recv()
        for cp in copies:
            cp.wait_send()

    return pl.pallas_call(
        body, name=name, out_shape=jax.ShapeDtypeStruct((NDEV, r, cols), F32),
        in_specs=[pl.BlockSpec(memory_space=pltpu.VMEM)], out_specs=pl.BlockSpec(memory_space=pltpu.VMEM),
        scratch_shapes=[pltpu.SemaphoreType.DMA((NDEV - 1,)), pltpu.SemaphoreType.DMA((NDEV - 1,))],
        compiler_params=_cp(40),
    )(v)


def gather_weights(packed):
    dt = packed.dtype

    def body(src_ref, o_ref, send_sems, recv_sems, local_sem):
        x, y, c = _place()
        k = 2 * x + y
        sib = (x, y, 1 - c)
        chips = _other_chips(x, y)
        mine = pltpu.make_async_copy(src_ref, o_ref.at[k], local_sem)
        mine.start()

        def half(kk, cc):
            return o_ref.at[kk, pl.ds(cc * HALF, HALF), :]

        first = []
        for j, (px, py) in enumerate(chips):
            cp = pltpu.make_async_remote_copy(src_ref=src_ref.at[pl.ds(c * HALF, HALF), :], dst_ref=half(k, c),
                                              send_sem=send_sems.at[j], recv_sem=recv_sems.at[j],
                                              device_id=(px, py, c), device_id_type=MESH)
            cp.start()
            first.append(cp)
        passed = []
        for j, (px, py) in enumerate(chips):
            kj = 2 * px + py
            pltpu.make_async_remote_copy(src_ref=half(kj, c), dst_ref=half(kj, c), send_sem=send_sems.at[j],
                                         recv_sem=recv_sems.at[j], device_id=(px, py, c),
                                         device_id_type=MESH).wait_recv()
            cp = pltpu.make_async_remote_copy(src_ref=half(kj, c), dst_ref=half(kj, c), send_sem=send_sems.at[3 + j],
                                              recv_sem=recv_sems.at[3 + j], device_id=sib, device_id_type=MESH)
            cp.start()
            passed.append(cp)
        for j, (px, py) in enumerate(chips):
            kj = 2 * px + py
            pltpu.make_async_remote_copy(src_ref=half(kj, 1 - c), dst_ref=half(kj, 1 - c),
                                         send_sem=send_sems.at[3 + j], recv_sem=recv_sems.at[3 + j], device_id=sib,
                                         device_id_type=MESH).wait_recv()
        for cp in first + passed:
            cp.wait_send()
        mine.wait()

    return pl.pallas_call(
        body, name="gather_weights", out_shape=jax.ShapeDtypeStruct((4, PACK_ROWS, D), dt),
        in_specs=[ANY], out_specs=ANY,
        scratch_shapes=[pltpu.SemaphoreType.DMA((6,)), pltpu.SemaphoreType.DMA((6,)), pltpu.SemaphoreType.DMA],
    )(packed)


def exchange_core_halves(g):
    def body(g_ref, o_ref, send_sem, recv_sem):
        x, y, c = _place()
        cp = pltpu.make_async_remote_copy(src_ref=g_ref.at[:, pl.ds((1 - c) * HALF, HALF), :], dst_ref=o_ref,
                                          send_sem=send_sem, recv_sem=recv_sem, device_id=(x, y, 1 - c),
                                          device_id_type=MESH)
        cp.start()
        cp.wait()

    return pl.pallas_call(
        body, name="exchange_core_halves", out_shape=jax.ShapeDtypeStruct((4, HALF, D), F32),
        in_specs=[ANY], out_specs=ANY,
        scratch_shapes=[pltpu.SemaphoreType.DMA, pltpu.SemaphoreType.DMA],
    )(g)


def add_own_half(g, r1, cidx):
    tr = 728
    nblk = HALF // tr

    def body(c_ref, g_ref, r_ref, o_ref):
        o_ref[...] = g_ref[...] + r_ref[...]

    return pl.pallas_call(
        body, name="add_own_half", out_shape=jax.ShapeDtypeStruct((4, HALF, D), F32),
        grid_spec=pltpu.PrefetchScalarGridSpec(
            num_scalar_prefetch=1, grid=(4, nblk),
            in_specs=[pl.BlockSpec((1, tr, D), lambda k, i, c_ref: (k, c_ref[0] * nblk + i, 0)),
                      pl.BlockSpec((1, tr, D), lambda k, i, c_ref: (k, i, 0))],
            out_specs=pl.BlockSpec((1, tr, D), lambda k, i, c_ref: (k, i, 0))),
    )(cidx, g, r1)


def exchange_chip_parts(s1):
    def body(s_ref, o_ref, send_sems, recv_sems):
        x, y, c = _place()
        copies = []
        for j, (px, py) in enumerate(_other_chips(x, y)):
            cp = pltpu.make_async_remote_copy(src_ref=s_ref.at[2 * px + py], dst_ref=o_ref.at[j],
                                              send_sem=send_sems.at[j], recv_sem=recv_sems.at[j],
                                              device_id=(px, py, c), device_id_type=MESH)
            cp.start()
            copies.append(cp)
        for cp in copies:
            cp.wait()

    return pl.pallas_call(
        body, name="exchange_chip_parts", out_shape=jax.ShapeDtypeStruct((3, HALF, D), F32),
        in_specs=[ANY], out_specs=ANY,
        scratch_shapes=[pltpu.SemaphoreType.DMA((3,)), pltpu.SemaphoreType.DMA((3,))],
    )(s1)


def sum_chip_parts(s1, r2, kidx):
    tr = 728
    nblk = HALF // tr

    def body(k_ref, s_ref, r_ref, o_ref):
        o_ref[...] = ((s_ref[0] + r_ref[0]) + r_ref[1]) + r_ref[2]

    return pl.pallas_call(
        body, name="sum_chip_parts", out_shape=jax.ShapeDtypeStruct((HALF, D), F32),
        grid_spec=pltpu.PrefetchScalarGridSpec(
            num_scalar_prefetch=1, grid=(nblk,),
            in_specs=[pl.BlockSpec((1, tr, D), lambda i, k_ref: (k_ref[0], i, 0)),
                      pl.BlockSpec((3, tr, D), lambda i, k_ref: (0, i, 0))],
            out_specs=pl.BlockSpec((tr, D), lambda i, k_ref: (i, 0))),
    )(kidx, s1, r2)


def join_core_halves(h):
    def body(h_ref, o_ref, send_sem, recv_sem, local_sem):
        x, y, c = _place()
        mine = pltpu.make_async_copy(h_ref, o_ref.at[pl.ds(c * HALF, HALF), :], local_sem)
        mine.start()
        cp = pltpu.make_async_remote_copy(src_ref=h_ref, dst_ref=o_ref.at[pl.ds(c * HALF, HALF), :],
                                          send_sem=send_sem, recv_sem=recv_sem, device_id=(x, y, 1 - c),
                                          device_id_type=MESH)
        cp.start()
        pltpu.make_async_remote_copy(src_ref=h_ref, dst_ref=o_ref.at[pl.ds((1 - c) * HALF, HALF), :],
                                     send_sem=send_sem, recv_sem=recv_sem, device_id=(x, y, 1 - c),
                                     device_id_type=MESH).wait_recv()
        cp.wait_send()
        mine.wait()

    return pl.pallas_call(
        body, name="join_core_halves", out_shape=jax.ShapeDtypeStruct((PACK_ROWS, D), F32),
        in_specs=[ANY], out_specs=ANY,
        scratch_shapes=[pltpu.SemaphoreType.DMA, pltpu.SemaphoreType.DMA, pltpu.SemaphoreType.DMA],
    )(h)


class _NS:
    def __init__(self, **kw):
        self.__dict__.update(kw)


def _pack(arrs):
    rows = [a[l].reshape(-1, D) for l in range(DEPTH) for a in arrs]
    used = sum(r.shape[0] for r in rows)
    rows.append(jnp.zeros((PACK_ROWS - used, D), rows[0].dtype))
    return jnp.concatenate(rows, axis=0)


PACK_SHAPES = ((D, 523), (256, 144), (256, 192), (256, D), (D, D), (D, D))


def _unpack(p):
    out = [[] for _ in PACK_SHAPES]
    off = 0
    for _ in range(DEPTH):
        for i, (n, shp) in enumerate(zip(PACK_LAYER, PACK_SHAPES)):
            out[i].append(p[off:off + n].reshape(shp))
            off += n
    return [jnp.stack(o) for o in out]


def _full_weights(wg):
    parts = [_unpack(wg[k]) for k in range(4)]
    axes = (2, 2, 2, 1, 2, 1)
    return [jnp.concatenate([parts[k][i] for k in range(4)], axis=axes[i]) for i in range(6)]


def _grad_packs(full):
    axes = (2, 2, 2, 1, 2, 1)
    packs = []
    for k in range(4):
        shards = []
        for g, ax in zip(full, axes):
            n = g.shape[ax] // 4
            shards.append(lax.slice_in_dim(g, k * n, (k + 1) * n, axis=ax))
        packs.append(_pack(shards))
    return jnp.stack(packs)


def _prep_layer(win, wqb, wkvb, wout, w1, w2, conv_w, conv_b, dt_bias, a_log, ssd_d, ssd_nw, qnw, kvnw, pool_w,
                pool_scale, n1, n2):
    winp = jnp.concatenate([win[:, 0:384], win[:, 384:1280], win[:, 1292:1548], win[:, 1548:1804], win[:, 1836:2092],
                            win[:, 1804:1836], win[:, 1280:1292], jnp.zeros((D, NP - IN_COLS), win.dtype)], axis=1)
    wq = jnp.pad(wqb.reshape(256, MLA_HEADS, QK_DIM), ((0, 0), (0, 0), (0, HP - QK_DIM))).reshape(256, QW)
    kv3 = wkvb.reshape(256, MLA_HEADS, 128)
    wk = jnp.pad(kv3[:, :, :64], ((0, 0), (0, 0), (0, 64))).reshape(256, QW)
    wv = jnp.pad(kv3[:, :, 64:], ((0, 0), (0, 0), (0, 64))).reshape(256, QW)
    wo = jnp.concatenate([jnp.pad(wout[384:768].reshape(MLA_HEADS, 64, D), ((0, 0), (0, 64), (0, 0))).reshape(QW, D),
                          wout[0:384], wout[768:1024]], axis=0)
    wbd = (jnp.eye(4, dtype=F32)[:, None, :, None] * pool_w[:, :, None, :]).reshape(POOL_DIM, POOL_DIM).astype(MXU)
    a = -jnp.exp(a_log)
    return _NS(
        winp=winp, wint=winp.T, wq=wq, wqt=wq.T, wk=wk, wkt=wk.T, wv=wv, wvt=wv.T, wo=wo, wot=wo.T,
        w1=w1, w1t=w1.T, w2=w2, w2t=w2.T, wbd=wbd, wbdt=wbd.T,
        cw8=jnp.pad(conv_w, ((0, 4), (0, 0))), cb=conv_b[None],
        dtb=jnp.pad(dt_bias.reshape(1, 12), ((0, 0), (DT0, 128 - DT0 - 12))),
        arow=jnp.pad(a[:, None, :], ((0, 0), (0, 7), (0, 128 - SSD_HEADS))), a=a,
        dexp=jnp.repeat(ssd_d, SSD_P)[None], ssd_nw=ssd_nw[None], qnw=qnw[None], kvnw=kvnw[None],
        pscale=pool_scale[None], n1=n1[None], n2=n2[None])


def _unprep_grads(dwinp, dwq, dwk, dwv, dwo):
    dwin = jnp.concatenate([dwinp[:, 0:384], dwinp[:, 384:1280], dwinp[:, 2080:2092], dwinp[:, 1280:1536],
                            dwinp[:, 1536:1792], dwinp[:, 2048:2080], dwinp[:, 1792:2048]], axis=1)
    dwqb = dwq.reshape(256, MLA_HEADS, HP)[:, :, :QK_DIM].reshape(256, MLA_HEADS * QK_DIM)
    dwkvb = jnp.concatenate([dwk.reshape(256, MLA_HEADS, HP)[:, :, :64], dwv.reshape(256, MLA_HEADS, HP)[:, :, :64]],
                            axis=2).reshape(256, MLA_HEADS * 128)
    dwout = jnp.concatenate([dwo[QW:QW + 384], dwo[0:QW].reshape(MLA_HEADS, HP, D)[:, :64].reshape(384, D),
                             dwo[QW + 384:CAT]], axis=0)
    return dwin, dwqb, dwkvb, dwout


def _rope_tables(nb, N):
    t = jnp.arange(N, dtype=F32)
    row = jnp.floor(t / GRID_W)
    col = t - row * GRID_W
    inv = jnp.asarray(10000.0 ** (-np.arange(8, dtype=np.float32) / 8), F32)
    ang = jnp.stack([row[:, None] * inv, col[:, None] * inv], axis=1)
    cs, sn = jnp.cos(ang), jnp.sin(ang)
    zero = jnp.zeros_like(sn)
    lanes = lambda first, second: jnp.stack([first, second], axis=2).reshape(N, 32)
    pad = lambda a, fill: jnp.concatenate([jnp.full((N, 64), fill, F32), a, jnp.full((N, 32), fill, F32)], axis=1)
    tabs = []
    for tab, fill in ((pad(lanes(cs, cs), 1.0), 1.0), (pad(lanes(-sn, zero), 0.0), 0.0), (pad(lanes(zero, sn), 0.0), 0.0)):
        one = jnp.concatenate([jnp.full((CTX, 128), fill, F32), tab], axis=0)
        tabs.append(jnp.tile(one, (nb, 1)))
    return tabs


def _eexp():
    e = np.zeros((128, SSD_INNER), np.float32)
    for h in range(SSD_HEADS):
        e[h, h * SSD_P:(h + 1) * SSD_P] = 1.0
    return jnp.asarray(e)


def _layer_fwd(X, bm, lw, cst):
    nb, T, bps, N = cst.nb, cst.T, cst.bps, cst.N
    h1, pz, pxbc, pqa, pkva, ppool, plast = in_proj(X, bm, lw.n1, lw.winp)
    xs, bmat, cmat, dtv = ssd_prep(pxbc, plast, lw.cw8, lw.cb, lw.dtb, bps)
    y2, hin = ssd_scan_fwd(xs, bmat, cmat, dtv, lw.arow, cst.eexp, nb, T)
    ssd = ssd_out_fwd(y2, xs, pz, lw.dexp, lw.ssd_nw)
    q, k, v, cq, ckv = mla_prep(pqa, pkva, plast, lw.qnw, lw.kvnw, lw.wq, lw.wk, lw.wv, *cst.rope)
    attn = attn_fwd(q, k, v, nb, T)
    pool = pool_fwd(ppool, lw.wbd, lw.pscale, bps, N)
    x1, mix, cat = mix_fwd(X, attn, ssd, pool, bm, lw.wo)
    x2, mo, r, h2 = mlp_fwd(x1, bm, lw.n2, lw.w1, lw.w2)
    sv = _NS(X=X, h1=h1, pz=pz, pxbc=pxbc, pqa=pqa, pkva=pkva, ppool=ppool, plast=plast, xs=xs, bmat=bmat, cmat=cmat,
             dtv=dtv, y2=y2, hin=hin, q=q, k=k, v=v, cq=cq, ckv=ckv, attn=attn, x1=x1, mix=mix, cat=cat, mo=mo, r=r,
             h2=h2)
    return x2, sv


def _layer_bwd(dx2, bm, lw, sv, cst):
    nb, T, bps, N = cst.nb, cst.T, cst.bps, cst.N
    dx1, du, dob, part_mlp = mlp_bwd(dx2, sv.x1, sv.mo, sv.r, bm, lw.n2, lw.w2t, lw.w1t)
    dw1 = mm_tn(sv.h2, du, name="wgrad_mlp1")
    dw2 = mm_tn(sv.r, dob, square_a=True, name="wgrad_mlp2")
    dattn, dssd, dpool, dmb, part_mix = mix_bwd(dx1, sv.mix, bm, lw.wot)
    dwo = mm_tn(sv.cat, dmb, name="wgrad_out")
    dppool, dwbd, part_pool = pool_bwd(sv.ppool, dpool, lw.wbd, lw.wbdt, lw.pscale, bps, N)
    dq, dk, dv = attn_bwd(sv.q, sv.k, sv.v, sv.attn, dattn, nb, T)
    dpqa, dpkva, dkr, dql, dkm, dvb, part_mla = mla_prep_bwd(dq, dk, dv, sv.pqa, sv.pkva, lw.qnw, lw.kvnw, lw.wqt,
                                                             lw.wkt, lw.wvt, *cst.rope)
    dwq = mm_tn(sv.cq, dql, name="wgrad_q")
    dwk = mm_tn(sv.ckv, dkm, name="wgrad_k")
    dwv = mm_tn(sv.ckv, dvb, name="wgrad_v")
    dyy, dz, dxs_skip, part_so = ssd_out_bwd(dssd, sv.y2, sv.xs, sv.pz, lw.dexp, lw.ssd_nw)
    dxs2, dbm2, dcm2, ddt2, da = ssd_scan_bwd(sv.xs, sv.bmat, sv.cmat, sv.dtv, lw.arow, cst.eexp, sv.hin, dyy, nb, T)
    dpre, dlast_dt, part_conv = ssd_prep_bwd_a(sv.pxbc, sv.plast, lw.cw8, lw.cb, lw.dtb, dxs_skip, dxs2, dbm2, dcm2,
                                               ddt2, bps)
    dpxbc = ssd_prep_bwd_b(dpre, lw.cw8, bps)
    dx, dpb, part_in = in_proj_bwd(dx1, sv.X, dz, dpxbc, dpqa, dpkva, dppool, dkr, dlast_dt, bm, lw.n1, lw.wint)
    dwinp = mm_tn(sv.h1, dpb, name="wgrad_in")

    dwin, dwqb, dwkvb, dwout = _unprep_grads(dwinp, dwq, dwk, dwv, dwo)
    dmod = jnp.stack([part_in[:, 0], part_in[:, 1], part_mix[:, 0], part_mlp[:, 0], part_mlp[:, 1], part_mlp[:, 2]],
                     axis=1)
    dm_rows = jnp.stack([jnp.sum(dmod[np.asarray(idx)], axis=0) for idx in cst.mod_blocks])
    da_dh = jnp.sum(da.reshape(nb, 2, 8, 128)[:, :, 0, :SSD_HEADS], axis=0)
    conv_parts = jnp.sum(part_conv, axis=0)
    g = _NS(
        w_in=dwin, w_q_b=dwqb, w_kv_b=dwkvb, w_out=dwout, w_mlp1=dw1, w_mlp2=dw2,
        dm_rows=dm_rows.reshape(3, 6 * D),
        norm1_w=jnp.sum(part_in[:, 2], axis=0), norm2_w=jnp.sum(part_mlp[:, 3], axis=0),
        conv_w=conv_parts[0:4], conv_b=conv_parts[4],
        dt_bias=conv_parts[5, DT0:DT0 + 12].reshape(2, SSD_HEADS), a_log=da_dh * lw.a,
        ssd_d=jnp.sum(jnp.sum(part_so[:, 1], axis=0).reshape(SSD_HEADS, SSD_P), axis=1),
        ssd_norm_w=jnp.sum(part_so[:, 0], axis=0),
        q_a_norm_w=jnp.sum(part_mla[:, 0], axis=0), kv_a_norm_w=jnp.sum(part_mla[:, 1], axis=0),
        pool_w=jnp.stack([dwbd[i * 64:(i + 1) * 64, i * 64:(i + 1) * 64] for i in range(4)]),
        pool_scale=jnp.sum(part_pool[:, 0], axis=0))
    return dx, g


def _local_step(x, ctx, tgt, bms, lws, fw, cst):
    nb, N = x.shape[0], x.shape[1]
    R = nb * cst.T
    X = jnp.concatenate([ctx, x], axis=1).reshape(R, D)
    saved = []
    for l in range(DEPTH):
        X, sv = _layer_fwd(X, bms[l], lws[l], cst)
        saved.append(sv)
    dX, part_fin = final_loss(X, tgt.reshape(nb * N, D), fw[None], cst.bps)
    loss = (0.5 / D) * jnp.sum(part_fin[:, 1])
    dfw = jnp.sum(part_fin[:, 0], axis=0)
    grads = [None] * DEPTH
    for l in reversed(range(DEPTH)):
        dX, grads[l] = _layer_bwd(dX, bms[l], lws[l], saved[l], cst)
    grad_x = dX.reshape(nb, cst.T, D)[:, CTX:, :]
    return loss, grad_x, grads, dfw


def _consts(nb, N):
    T = CTX + N
    bps = T // SB
    mod_blocks = [[b * bps + j for j in range(1, bps)] for b in range(nb)] + [[b * bps for b in range(nb)]]
    rowidx = np.asarray([nb if kb % bps == 0 else kb // bps for kb in range(nb * bps)])
    return _NS(nb=nb, N=N, T=T, bps=bps, eexp=_eexp(), rope=_rope_tables(nb, N), mod_blocks=mod_blocks, rowidx=rowidx)


def _block_mod(modrows, cst):
    return jnp.pad(modrows[cst.rowidx], ((0, 0), (0, 2), (0, 0)))


SMALL = (("norm1_w", (2, D)), ("norm2_w", (2, D)), ("conv_w", (2, 4, XBC)), ("conv_b", (2, XBC)),
         ("dt_bias", (2, 2, 6)), ("a_log", (2, 2, 6)), ("ssd_d", (2, 6)), ("ssd_norm_w", (2, 384)),
         ("q_a_norm_w", (2, 256)), ("kv_a_norm_w", (2, 256)), ("pool_w", (2, 4, 64, 64)), ("pool_scale", (2, 256)),
         ("final_norm_w", (D,)), ("mod_b", (2, 6 * D)))
SMALL_ROWS = 64
DM_ROWS = 48


def _pack_small(vals):
    flat = jnp.concatenate([vals[n].reshape(-1) for n, _ in SMALL])
    return jnp.pad(flat, (0, SMALL_ROWS * D - flat.shape[0])).reshape(SMALL_ROWS, D)


def _unpack_small(p):
    flat = p.reshape(-1)
    out, off = {}, 0
    for n, shp in SMALL:
        size = int(np.prod(shp))
        out[n] = flat[off:off + size].reshape(shp)
        off += size
    return out


def cctx_grad(parts, c_ctx):
    def body(p_ref, c_ref, o_ref):
        acc = ((p_ref[0] + p_ref[1]) + p_ref[2]) + p_ref[3]
        v = c_ref[...]
        sig = _sigmoid(v)
        o_ref[...] = acc * (sig * (1.0 + v * (1.0 - sig)))

    return pl.pallas_call(
        body, name="cctx_grad", out_shape=jax.ShapeDtypeStruct((8, D), F32),
        in_specs=[_fullspec((4, 8, D)), _fullspec((1, D))], out_specs=_fullspec((8, D)), grid=(1,),
    )(parts, c_ctx)


def kernel(x, c, ctx, c_ctx, mod_w, mod_b, norm1_w, norm2_w, w_in, conv_w, conv_b, dt_bias, a_log, ssd_d, ssd_norm_w, q_a_norm_w, w_q_b, kv_a_norm_w, w_kv_b, pool_w, pool_scale, w_out, w_mlp1, w_mlp2, final_norm_w, loss_target, m_c_ctx, m_mod_w, m_mod_b, m_norm1_w, m_norm2_w, m_w_in, m_conv_w, m_conv_b, m_dt_bias, m_a_log, m_ssd_d, m_ssd_norm_w, m_q_a_norm_w, m_w_q_b, m_kv_a_norm_w, m_w_kv_b, m_pool_w, m_pool_scale, m_w_out, m_w_mlp1, m_w_mlp2, m_final_norm_w, v_c_ctx, v_mod_w, v_mod_b, v_norm1_w, v_norm2_w, v_w_in, v_conv_w, v_conv_b, v_dt_bias, v_a_log, v_ssd_d, v_ssd_norm_w, v_q_a_norm_w, v_w_q_b, v_kv_a_norm_w, v_w_kv_b, v_pool_w, v_pool_scale, v_w_out, v_w_mlp1, v_w_mlp2, v_final_norm_w):
    nb, N = x.shape[0], x.shape[1]
    cst = _consts(nb, N)
    xi, yi, ci = _place()
    me = 4 * xi + 2 * yi + ci
    kchip = 2 * xi + yi
    mcols = mod_w.shape[2]
    cshard = conv_w.shape[2]

    blk = jnp.zeros((16, D), F32).at[0:nb].set(c).at[8:16, 0:cshard].set(conv_w.reshape(8, cshard))
    g1 = allgather_small(blk, "gather_cond")
    cond = jnp.concatenate([g1[:, 0:nb].reshape(NDEV * nb, D), c_ctx[None],
                            jnp.zeros((MODR - NDEV * nb - 1, D), F32)], axis=0)
    conv_full = [jnp.concatenate([g1[2 * k, 8 + 4 * l:12 + 4 * l, 0:cshard] for k in range(4)], axis=1)
                 for l in range(DEPTH)]

    mb = [lax.dynamic_slice_in_dim(mod_b[l], kchip * mcols, mcols)[None] for l in range(DEPTH)]
    ms = jnp.concatenate([mod_fwd(cond, mod_w[l], mb[l]) for l in range(DEPTH)], axis=0)
    g2 = allgather_small(ms, "gather_mod")
    bms = []
    for l in range(DEPTH):
        m_all = jnp.concatenate([g2[2 * k, MODR * l:MODR * (l + 1)] for k in range(4)], axis=1)
        mine = jnp.concatenate([lax.dynamic_slice_in_dim(m_all, nb * me, nb), m_all[NDEV * nb:NDEV * nb + 1]], axis=0)
        bms.append(_block_mod(mine.reshape(nb + 1, 6, D), cst))

    wg = gather_weights(_pack([a.astype(MXU) for a in (w_in, w_q_b, w_kv_b, w_out, w_mlp1, w_mlp2)]))
    fw_in, fw_qb, fw_kvb, fw_out, fw1, fw2 = _full_weights(wg)
    lws = [_prep_layer(fw_in[l], fw_qb[l], fw_kvb[l], fw_out[l], fw1[l], fw2[l], conv_full[l], conv_b[l], dt_bias[l],
                       a_log[l], ssd_d[l], ssd_norm_w[l], q_a_norm_w[l], kv_a_norm_w[l], pool_w[l], pool_scale[l],
                       norm1_w[l], norm2_w[l]) for l in range(DEPTH)]

    loss_part, grad_x, grads, dfw = _local_step(x, ctx, loss_target, bms, lws, final_norm_w, cst)
    loss = lax.psum(loss_part, ("x", "y", "c"))

    G = _grad_packs([jnp.stack([getattr(grads[l], n) for l in range(DEPTH)])
                     for n in ("w_in", "w_q_b", "w_kv_b", "w_out", "w_mlp1", "w_mlp2")])
    cidx = jnp.reshape(ci, (1,)).astype(jnp.int32)
    kidx = jnp.reshape(kchip, (1,)).astype(jnp.int32)
    s1 = add_own_half(G, exchange_core_halves(G), cidx)
    gpack = join_core_halves(sum_chip_parts(s1, exchange_chip_parts(s1), kidx))

    small = {n: jnp.stack([getattr(grads[l], n) for l in range(DEPTH)]) for n, _ in SMALL if n not in ("final_norm_w", "mod_b")}
    small["final_norm_w"] = dfw
    small["mod_b"] = jnp.stack([jnp.sum(grads[l].dm_rows, axis=0) for l in range(DEPTH)])
    dm = jnp.pad(jnp.concatenate([grads[l].dm_rows for l in range(DEPTH)], axis=0), ((0, 8 - 3 * DEPTH), (0, 0)))
    g3 = allgather_small(jnp.concatenate([_pack_small(small), dm.reshape(DM_ROWS, D)], axis=0), "gather_small")
    tot = sum_leading(g3, "sum_small")
    gsmall = _unpack_small(tot[0:SMALL_ROWS])
    ctx_sum = tot[SMALL_ROWS:].reshape(8, 6 * D)
    dm_dev = g3[:, SMALL_ROWS:].reshape(NDEV, 8, 6 * D)
    g_mod_w, dpart = [], jnp.zeros((8, D), F32)
    for l in range(DEPTH):
        dm_all = jnp.concatenate([dm_dev[:, 3 * l:3 * l + nb].reshape(NDEV * nb, 6 * D), ctx_sum[3 * l + nb:3 * l + nb + 1],
                                  jnp.zeros((MODR - NDEV * nb - 1, 6 * D), F32)], axis=0)
        g_mod_w.append(mod_wgrad(cond, lax.dynamic_slice_in_dim(dm_all, kchip * mcols, mcols, axis=1)))
        dctx = jnp.pad(lax.dynamic_slice_in_dim(ctx_sum[3 * l + nb:3 * l + nb + 1], kchip * mcols, mcols, axis=1), ((0, 7), (0, 0)))
        dpart = dpart + mod_dgrad(dctx, mod_w[l])
    g4 = allgather_small(dpart, "gather_cctx")
    g_c_ctx = cctx_grad(g4[0::2], c_ctx[None])[0]

    names = ("w_in", "w_q_b", "w_kv_b", "w_out", "w_mlp1", "w_mlp2")
    big = adamw(_pack([w_in, w_q_b, w_kv_b, w_out, w_mlp1, w_mlp2]), gpack,
                _pack([m_w_in, m_w_q_b, m_w_kv_b, m_w_out, m_w_mlp1, m_w_mlp2]),
                _pack([v_w_in, v_w_q_b, v_w_kv_b, v_w_out, v_w_mlp1, v_w_mlp2]), name="adamw_big")
    res = {}
    for n, parts in zip(names, zip(*[_unpack(p) for p in (gpack,) + tuple(big)])):
        res[n] = parts
    g_mw = jnp.stack(g_mod_w)
    r_mw = adamw(mod_w.reshape(-1, mcols), g_mw.reshape(-1, mcols), m_mod_w.reshape(-1, mcols),
                 v_mod_w.reshape(-1, mcols), name="adamw_mod_w")
    res["mod_w"] = (g_mw,) + tuple(a.reshape(mod_w.shape) for a in r_mw)

    given = dict(norm1_w=(norm1_w, m_norm1_w, v_norm1_w), norm2_w=(norm2_w, m_norm2_w, v_norm2_w),
                 conv_b=(conv_b, m_conv_b, v_conv_b), dt_bias=(dt_bias, m_dt_bias, v_dt_bias),
                 a_log=(a_log, m_a_log, v_a_log), ssd_d=(ssd_d, m_ssd_d, v_ssd_d),
                 ssd_norm_w=(ssd_norm_w, m_ssd_norm_w, v_ssd_norm_w), q_a_norm_w=(q_a_norm_w, m_q_a_norm_w, v_q_a_norm_w),
                 kv_a_norm_w=(kv_a_norm_w, m_kv_a_norm_w, v_kv_a_norm_w), pool_w=(pool_w, m_pool_w, v_pool_w),
                 pool_scale=(pool_scale, m_pool_scale, v_pool_scale),
                 final_norm_w=(final_norm_w, m_final_norm_w, v_final_norm_w), mod_b=(mod_b, m_mod_b, v_mod_b))
    zero_cw = jnp.zeros((2, 4, XBC), F32)
    packs = [_pack_small({n: (given[n][i] if n in given else zero_cw) for n, _ in SMALL}) for i in range(3)]
    r_small = [_unpack_small(a) for a in adamw(packs[0], tot[0:SMALL_ROWS], packs[1], packs[2], name="adamw_small")]
    for n in given:
        res[n] = (gsmall[n], r_small[0][n], r_small[1][n], r_small[2][n])

    g_cw = lax.dynamic_slice_in_dim(gsmall["conv_w"], kchip * cshard, cshard, axis=2)
    padcw = lambda a: jnp.pad(a.reshape(8, cshard), ((0, 0), (0, 256 - cshard)))
    r_cw = adamw(padcw(conv_w), padcw(g_cw), padcw(m_conv_w), padcw(v_conv_w), name="adamw_conv_w")
    res["conv_w"] = (g_cw,) + tuple(a[:, 0:cshard].reshape(conv_w.shape) for a in r_cw)
    r_cc = adamw(c_ctx.reshape(8, 128), g_c_ctx.reshape(8, 128), m_c_ctx.reshape(8, 128), v_c_ctx.reshape(8, 128),
                 name="adamw_c_ctx")
    res["c_ctx"] = (g_c_ctx,) + tuple(a.reshape(D) for a in r_cc)

    order = ("c_ctx", "mod_w", "mod_b", "norm1_w", "norm2_w", "w_in", "conv_w", "conv_b", "dt_bias", "a_log", "ssd_d",
             "ssd_norm_w", "q_a_norm_w", "w_q_b", "kv_a_norm_w", "w_kv_b", "pool_w", "pool_scale", "w_out", "w_mlp1",
             "w_mlp2", "final_norm_w")
    return (loss, grad_x) + tuple(res[n][i] for i in range(4) for n in order)
```

```python
import functools
import math

import numpy as np
import jax
import jax.numpy as jnp
from jax import lax
from jax.experimental import pallas as pl
from jax.experimental.pallas import tpu as pltpu

F32 = jnp.float32
BF16 = jnp.bfloat16
MXU = jnp.bfloat16
HI = lax.Precision.HIGHEST

D = 1024
DEPTH = 2
GRID_W = 64
CTX = 256
EPS = 1e-6
SSD_HEADS = 6
SSD_P = 64
SSD_INNER = 384
SSD_N = 128
CHUNK = 128
XBC = 896
MLA_HEADS = 6
QK_NOPE = 64
QK_ROPE = 32
QK_DIM = 96
HP = 128
QW = MLA_HEADS * HP
POOL_DIM = 256
D_FF = 4096
FF_BLK = 1024
IN_COLS = 2092
NP = 2176
P_SPLITS = (384, 896, 256, 256, 256, 128)
DT0 = 32
CAT = QW + SSD_INNER + POOL_DIM

SB = 256
TM = 512
HALO = 8

ADAM_LR = 0.001
ADAM_B1 = 0.9
ADAM_B2 = 0.999
ADAM_EPS = 1e-08
ADAM_WD = 0.01
ADAM_STEP = 10

NT = (((1,), (1,)), ((), ()))
TN = (((0,), (0,)), ((), ()))


def _cp(vmem_mb=None):
    if vmem_mb is None:
        return pltpu.CompilerParams()
    return pltpu.CompilerParams(vmem_limit_bytes=vmem_mb << 20)


def _dot(a, b):
    return jnp.dot(a, b, preferred_element_type=F32)


def _dotg(a, b, dims):
    return lax.dot_general(a, b, dims, preferred_element_type=F32)


def _dot_hi(a, b, dims=None):
    if dims is None:
        return jnp.dot(a, b, precision=HI, preferred_element_type=F32)
    return lax.dot_general(a, b, dims, precision=HI, preferred_element_type=F32)


def _rms_hat(x):
    rstd = lax.rsqrt(jnp.mean(x * x, axis=-1, keepdims=True) + EPS)
    return x * rstd, rstd


def _rms_bwd(dn, xhat, rstd, w):
    dxhat = dn * w
    dx = rstd * (dxhat - xhat * jnp.mean(dxhat * xhat, axis=-1, keepdims=True))
    return dx, jnp.sum(dn * xhat, axis=0, keepdims=True)


def _sigmoid(z):
    return 1.0 / (1.0 + jnp.exp(-z))


def _colsum(a):
    return jnp.sum(a, axis=0, keepdims=True)


def _rowspec(cols, tm=TM):
    return pl.BlockSpec((tm, cols), lambda i: (i, 0))


def _fullspec(shape):
    n = len(shape)
    return pl.BlockSpec(shape, lambda *_: (0,) * n)


def _halo_specs(cols, nrows):
    per = SB // HALO
    last = nrows // HALO - 1
    prev = pl.BlockSpec((HALO, cols), lambda i: (jnp.maximum(i * per - 1, 0), 0))
    nxt = pl.BlockSpec((HALO, cols), lambda i: (jnp.minimum((i + 1) * per, last), 0))
    return prev, nxt


def _ext_rows(cur, prev, nxt, i, blocks_per_sample):
    j = i % blocks_per_sample
    first = jnp.logical_or(j == 0, j == 1)
    last = jnp.logical_or(j == 0, j == blocks_per_sample - 1)
    p = jnp.where(first, 0.0, prev)
    n = jnp.where(last, 0.0, nxt)
    return jnp.concatenate([p, cur, n], axis=0)


def _shift(ext, s):
    n = ext.shape[0]
    return pltpu.roll(ext, (-s) % n, axis=0)[HALO:HALO + SB, :]


def in_proj(x, bm, nw, w):
    R = x.shape[0]

    def body(x_ref, bm_ref, nw_ref, w_ref, h_ref, *outs):
        for s in range(TM // SB):
            rows = slice(s * SB, (s + 1) * SB)
            xhat, _ = _rms_hat(x_ref[rows, :])
            h = xhat * nw_ref[...] * (1.0 + bm_ref[s, 1:2, :]) + bm_ref[s, 0:1, :]
            h_ref[rows, :] = h.astype(h_ref.dtype)
        p = _dot(h_ref[...], w_ref[...])
        off = 0
        for o, n in zip(outs, P_SPLITS):
            o[...] = p[:, off:off + n]
            off += n

    return pl.pallas_call(
        body, name="in_proj", grid=(R // TM,),
        in_specs=[_rowspec(D), pl.BlockSpec((TM // SB, 8, D), lambda i: (i, 0, 0)), _fullspec((1, D)),
                  _fullspec((D, NP))],
        out_specs=[_rowspec(D)] + [_rowspec(n) for n in P_SPLITS],
        out_shape=[jax.ShapeDtypeStruct((R, D), MXU)] + [jax.ShapeDtypeStruct((R, n), F32) for n in P_SPLITS],
        compiler_params=_cp(56),
    )(x, bm, nw, w)


def in_proj_bwd(dx1, x, dz, dxbc, dqa, dkva, dpool, dkr, ddt, bm, nw, wt):
    R = x.shape[0]

    def body(dx1_ref, x_ref, dz_ref, dxbc_ref, dqa_ref, dkva_ref, dpool_ref, dkr_ref, ddt_ref, bm_ref, nw_ref,
             wt_ref, dx_ref, dp_ref, part_ref):
        dp_ref[:, 0:384] = dz_ref[...].astype(dp_ref.dtype)
        dp_ref[:, 384:1280] = dxbc_ref[...].astype(dp_ref.dtype)
        dp_ref[:, 1280:1536] = dqa_ref[...].astype(dp_ref.dtype)
        dp_ref[:, 1536:1792] = dkva_ref[...].astype(dp_ref.dtype)
        dp_ref[:, 1792:2048] = dpool_ref[...].astype(dp_ref.dtype)
        dp_ref[:, 2048:2176] = (dkr_ref[...] + ddt_ref[...]).astype(dp_ref.dtype)
        dh = _dot(dp_ref[...], wt_ref[...])
        w = nw_ref[...]
        for s in range(TM // SB):
            rows = slice(s * SB, (s + 1) * SB)
            xhat, rstd = _rms_hat(x_ref[rows, :])
            dhs = dh[rows, :]
            sc1 = 1.0 + bm_ref[s, 1:2, :]
            dx, dnw = _rms_bwd(dhs * sc1, xhat, rstd, w)
            dx_ref[rows, :] = dx1_ref[rows, :] + dx
            part_ref[s] = jnp.concatenate(
                [_colsum(dhs), _colsum(dhs * xhat * w), dnw, jnp.zeros((5, D), F32)], axis=0)

    return pl.pallas_call(
        body, name="in_proj_bwd", grid=(R // TM,),
        in_specs=[_rowspec(D), _rowspec(D), _rowspec(384), _rowspec(896), _rowspec(256), _rowspec(256),
                  _rowspec(256), _rowspec(128), _rowspec(128),
                  pl.BlockSpec((TM // SB, 8, D), lambda i: (i, 0, 0)), _fullspec((1, D)), _fullspec((NP, D))],
        out_specs=[_rowspec(D), _rowspec(NP), pl.BlockSpec((TM // SB, 8, D), lambda i: (i, 0, 0))],
        out_shape=[jax.ShapeDtypeStruct((R, D), F32), jax.ShapeDtypeStruct((R, NP), MXU),
                   jax.ShapeDtypeStruct((R // SB, 8, D), F32)],
        compiler_params=_cp(56),
    )(dx1, x, dz, dxbc, dqa, dkva, dpool, dkr, ddt, bm, nw, wt)


def mix_fwd(x, attn, ssd, pool, bm, wo):
    R = x.shape[0]

    def body(x_ref, a_ref, s_ref, p_ref, bm_ref, wo_ref, x1_ref, mix_ref, cat_ref):
        cat_ref[:, 0:QW] = a_ref[...].astype(cat_ref.dtype)
        cat_ref[:, QW:QW + SSD_INNER] = s_ref[...].astype(cat_ref.dtype)
        cat_ref[:, QW + SSD_INNER:CAT] = p_ref[...].astype(cat_ref.dtype)
        mix = _dot(cat_ref[...], wo_ref[...])
        mix_ref[...] = mix
        for s in range(TM // SB):
            rows = slice(s * SB, (s + 1) * SB)
            x1_ref[rows, :] = x_ref[rows, :] + bm_ref[s, 2:3, :] * mix[rows, :]

    return pl.pallas_call(
        body, name="mix_fwd", grid=(R // TM,),
        in_specs=[_rowspec(D), _rowspec(QW), _rowspec(SSD_INNER), _rowspec(POOL_DIM),
                  pl.BlockSpec((TM // SB, 8, D), lambda i: (i, 0, 0)), _fullspec((CAT, D))],
        out_specs=[_rowspec(D), _rowspec(D), _rowspec(CAT)],
        out_shape=[jax.ShapeDtypeStruct((R, D), F32), jax.ShapeDtypeStruct((R, D), F32),
                   jax.ShapeDtypeStruct((R, CAT), MXU)],
        compiler_params=_cp(48),
    )(x, attn, ssd, pool, bm, wo)


def mix_bwd(dx1, mix, bm, wot):
    R = dx1.shape[0]

    def body(dx1_ref, mix_ref, bm_ref, wot_ref, da_ref, ds_ref, dpl_ref, dmb_ref, part_ref):
        for s in range(TM // SB):
            rows = slice(s * SB, (s + 1) * SB)
            d = dx1_ref[rows, :]
            dmb_ref[rows, :] = (d * bm_ref[s, 2:3, :]).astype(dmb_ref.dtype)
            part_ref[s] = jnp.concatenate([_colsum(d * mix_ref[rows, :]), jnp.zeros((7, D), F32)], axis=0)
        dcat = _dot(dmb_ref[...], wot_ref[...])
        da_ref[...] = dcat[:, 0:QW]
        ds_ref[...] = dcat[:, QW:QW + SSD_INNER]
        dpl_ref[...] = dcat[:, QW + SSD_INNER:CAT]

    return pl.pallas_call(
        body, name="mix_bwd", grid=(R // TM,),
        in_specs=[_rowspec(D), _rowspec(D), pl.BlockSpec((TM // SB, 8, D), lambda i: (i, 0, 0)),
                  _fullspec((D, CAT))],
        out_specs=[_rowspec(QW), _rowspec(SSD_INNER), _rowspec(POOL_DIM), _rowspec(D),
                   pl.BlockSpec((TM // SB, 8, D), lambda i: (i, 0, 0))],
        out_shape=[jax.ShapeDtypeStruct((R, QW), F32), jax.ShapeDtypeStruct((R, SSD_INNER), F32),
                   jax.ShapeDtypeStruct((R, POOL_DIM), F32), jax.ShapeDtypeStruct((R, D), MXU),
                   jax.ShapeDtypeStruct((R // SB, 8, D), F32)],
        compiler_params=_cp(48),
    )(dx1, mix, bm, wot)


def mlp_fwd(x1, bm, nw, w1, w2):
    R = x1.shape[0]
    nj = D_FF // FF_BLK

    def body(x1_ref, bm_ref, nw_ref, w1_ref, w2_ref, x2_ref, mo_ref, r_ref, h2_ref, acc_ref):
        j = pl.program_id(1)

        @pl.when(j == 0)
        def _():
            for s in range(TM // SB):
                rows = slice(s * SB, (s + 1) * SB)
                xhat, _ = _rms_hat(x1_ref[rows, :])
                h = xhat * nw_ref[...] * (1.0 + bm_ref[s, 4:5, :]) + bm_ref[s, 3:4, :]
                h2_ref[rows, :] = h.astype(h2_ref.dtype)
            acc_ref[...] = jnp.zeros_like(acc_ref)

        r = jnp.maximum(_dot(h2_ref[...], w1_ref[...]), 0.0)
        r_ref[...] = r.astype(r_ref.dtype)
        acc_ref[...] += _dot((r * r).astype(MXU), w2_ref[...])

        @pl.when(j == nj - 1)
        def _():
            mo_ref[...] = acc_ref[...]
            for s in range(TM // SB):
                rows = slice(s * SB, (s + 1) * SB)
                x2_ref[rows, :] = x1_ref[rows, :] + bm_ref[s, 5:6, :] * acc_ref[rows, :]

    return pl.pallas_call(
        body, name="mlp_fwd", grid=(R // TM, nj),
        in_specs=[pl.BlockSpec((TM, D), lambda i, j: (i, 0)),
                  pl.BlockSpec((TM // SB, 8, D), lambda i, j: (i, 0, 0)),
                  pl.BlockSpec((1, D), lambda i, j: (0, 0)),
                  pl.BlockSpec((D, FF_BLK), lambda i, j: (0, j)),
                  pl.BlockSpec((FF_BLK, D), lambda i, j: (j, 0))],
        out_specs=[pl.BlockSpec((TM, D), lambda i, j: (i, 0)), pl.BlockSpec((TM, D), lambda i, j: (i, 0)),
                   pl.BlockSpec((TM, FF_BLK), lambda i, j: (i, j)), pl.BlockSpec((TM, D), lambda i, j: (i, 0))],
        out_shape=[jax.ShapeDtypeStruct((R, D), F32), jax.ShapeDtypeStruct((R, D), F32),
                   jax.ShapeDtypeStruct((R, D_FF), BF16), jax.ShapeDtypeStruct((R, D), MXU)],
        scratch_shapes=[pltpu.VMEM((TM, D), F32)],
        compiler_params=_cp(48),
    )(x1, bm, nw, w1, w2)


def mlp_bwd(dx2, x1, mo, r, bm, nw, w2t, w1t):
    R = x1.shape[0]
    nj = D_FF // FF_BLK

    def body(dx2_ref, x1_ref, mo_ref, r_ref, bm_ref, nw_ref, w2t_ref, w1t_ref, dx1_ref, du_ref, dob_ref, part_ref,
             acc_ref):
        j = pl.program_id(1)

        @pl.when(j == 0)
        def _():
            for s in range(TM // SB):
                rows = slice(s * SB, (s + 1) * SB)
                dob_ref[rows, :] = (dx2_ref[rows, :] * bm_ref[s, 5:6, :]).astype(dob_ref.dtype)
            acc_ref[...] = jnp.zeros_like(acc_ref)

        du = _dot(dob_ref[...], w2t_ref[...]) * (2.0 * r_ref[...].astype(F32))
        du_ref[...] = du.astype(du_ref.dtype)
        acc_ref[...] += _dot(du_ref[...], w1t_ref[...])

        @pl.when(j == nj - 1)
        def _():
            w = nw_ref[...]
            for s in range(TM // SB):
                rows = slice(s * SB, (s + 1) * SB)
                xhat, rstd = _rms_hat(x1_ref[rows, :])
                dh = acc_ref[rows, :]
                dx, dnw = _rms_bwd(dh * (1.0 + bm_ref[s, 4:5, :]), xhat, rstd, w)
                d2 = dx2_ref[rows, :]
                dx1_ref[rows, :] = d2 + dx
                part_ref[s] = jnp.concatenate(
                    [_colsum(dh), _colsum(dh * xhat * w), _colsum(d2 * mo_ref[rows, :]), dnw,
                     jnp.zeros((4, D), F32)], axis=0)

    return pl.pallas_call(
        body, name="mlp_bwd", grid=(R // TM, nj),
        in_specs=[pl.BlockSpec((TM, D), lambda i, j: (i, 0)), pl.BlockSpec((TM, D), lambda i, j: (i, 0)),
                  pl.BlockSpec((TM, D), lambda i, j: (i, 0)), pl.BlockSpec((TM, FF_BLK), lambda i, j: (i, j)),
                  pl.BlockSpec((TM // SB, 8, D), lambda i, j: (i, 0, 0)),
                  pl.BlockSpec((1, D), lambda i, j: (0, 0)),
                  pl.BlockSpec((D, FF_BLK), lambda i, j: (0, j)),
                  pl.BlockSpec((FF_BLK, D), lambda i, j: (j, 0))],
        out_specs=[pl.BlockSpec((TM, D), lambda i, j: (i, 0)), pl.BlockSpec((TM, FF_BLK), lambda i, j: (i, j)),
                   pl.BlockSpec((TM, D), lambda i, j: (i, 0)),
                   pl.BlockSpec((TM // SB, 8, D), lambda i, j: (i, 0, 0))],
        out_shape=[jax.ShapeDtypeStruct((R, D), F32), jax.ShapeDtypeStruct((R, D_FF), MXU),
                   jax.ShapeDtypeStruct((R, D), MXU), jax.ShapeDtypeStruct((R // SB, 8, D), F32)],
        scratch_shapes=[pltpu.VMEM((TM, D), F32)],
        compiler_params=_cp(48),
    )(dx2, x1, mo, r, bm, nw, w2t, w1t)


def mm_tn(a, b, square_a=False, name="mm_tn", col_blocks=False):
    R, M = a.shape
    N = b.shape[1]
    tm = M if M <= 1408 else 1024
    tn = N if N <= 2176 else 1024
    tk = 512 if R % 512 == 0 else R
    assert not col_blocks or tm == M

    def body(a_ref, b_ref, o_ref):
        @pl.when(pl.program_id(2) == 0)
        def _():
            o_ref[...] = jnp.zeros_like(o_ref)

        av = a_ref[...]
        if square_a:
            av = av.astype(F32)
            av = (av * av).astype(MXU)
        prod = _dotg(av.astype(MXU), b_ref[...].astype(MXU), TN)
        if col_blocks:
            o_ref[0] += prod
        else:
            o_ref[...] += prod

    if col_blocks:
        out_spec = pl.BlockSpec((1, tm, tn), lambda i, j, k: (j, 0, 0))
        out_shape = jax.ShapeDtypeStruct((N // tn, M, tn), F32)
    else:
        out_spec = pl.BlockSpec((tm, tn), lambda i, j, k: (i, j))
        out_shape = jax.ShapeDtypeStruct((M, N), F32)
    return pl.pallas_call(
        body, name=name, grid=(M // tm, N // tn, R // tk),
        in_specs=[pl.BlockSpec((tk, tm), lambda i, j, k: (k, i)), pl.BlockSpec((tk, tn), lambda i, j, k: (k, j))],
        out_specs=out_spec, out_shape=out_shape,
        compiler_params=_cp(48),
    )(a, b)


def final_loss(x, tgt, fw, blocks_per_sample):
    R = x.shape[0]
    nxb = blocks_per_sample - 1

    def body(x_ref, t_ref, fw_ref, dx_ref, part_ref):
        i = pl.program_id(0)
        is_ctx = (i % blocks_per_sample) == 0
        xhat, rstd = _rms_hat(x_ref[...])
        w = fw_ref[...]
        err = xhat * w - t_ref[...]
        dx, dfw = _rms_bwd(err * (1.0 / D), xhat, rstd, w)
        keep = jnp.where(is_ctx, 0.0, 1.0)
        dx_ref[...] = dx * keep
        part_ref[0] = jnp.concatenate([dfw * keep, _colsum(err * err) * keep, jnp.zeros((6, D), F32)], axis=0)

    def tmap(i):
        return ((i // blocks_per_sample) * nxb + jnp.maximum(i % blocks_per_sample - 1, 0), 0)

    return pl.pallas_call(
        body, name="final_loss", grid=(R // SB,),
        in_specs=[_rowspec(D, SB), pl.BlockSpec((SB, D), tmap), _fullspec((1, D))],
        out_specs=[_rowspec(D, SB), pl.BlockSpec((1, 8, D), lambda i: (i, 0, 0))],
        out_shape=[jax.ShapeDtypeStruct((R, D), F32), jax.ShapeDtypeStruct((R // SB, 8, D), F32)],
    )(x, tgt, fw)


def _softplus(v):
    return jnp.maximum(v, 0.0) + jnp.log(1.0 + jnp.exp(-jnp.abs(v)))


def _conv_out(ext, cw_ref, cb_ref):
    return (cb_ref[...] + cw_ref[0:1, :] * _shift(ext, -1) + cw_ref[1:2, :] * _shift(ext, 0)
            + cw_ref[2:3, :] * _shift(ext, 1) + cw_ref[3:4, :] * _shift(ext, 2))


def _dt_dir(v, d):
    lane = lax.broadcasted_iota(jnp.int32, v.shape, 1)
    return jnp.where(lane < SSD_HEADS, pltpu.roll(v, (128 - DT0 - SSD_HEADS * d) % 128, axis=1), 0.0)


def ssd_prep(pxbc, plast, cw, cb, dtb, blocks_per_sample):
    R = pxbc.shape[0]
    prev, nxt = _halo_specs(XBC, R)

    def body(cur_ref, prev_ref, nxt_ref, pl_ref, cw_ref, cb_ref, dtb_ref, xs_ref, bm_ref, cm_ref, dt_ref):
        i = pl.program_id(0)
        ext = _ext_rows(cur_ref[...], prev_ref[...], nxt_ref[...], i, blocks_per_sample)
        co = _conv_out(ext, cw_ref, cb_ref)
        a = co * _sigmoid(co)
        xs_ref[...] = a[:, 0:384]
        bm_ref[...] = a[:, 384:640]
        cm_ref[...] = a[:, 640:896]
        sp = _softplus(pl_ref[...] + dtb_ref[...])
        dt_ref[0] = _dt_dir(sp, 0)
        dt_ref[1] = _dt_dir(sp, 1)

    return pl.pallas_call(
        body, name="ssd_prep", grid=(R // SB,),
        in_specs=[_rowspec(XBC, SB), prev, nxt, _rowspec(128, SB), _fullspec((8, XBC)), _fullspec((1, XBC)),
                  _fullspec((1, 128))],
        out_specs=[_rowspec(384, SB), _rowspec(256, SB), _rowspec(256, SB),
                   pl.BlockSpec((2, SB, 128), lambda i: (0, i, 0))],
        out_shape=[jax.ShapeDtypeStruct((R, 384), F32), jax.ShapeDtypeStruct((R, 256), F32),
                   jax.ShapeDtypeStruct((R, 256), F32), jax.ShapeDtypeStruct((2, R, 128), F32)],
    )(pxbc, pxbc, pxbc, plast, cw, cb, dtb)


def _chunk_index(d, s, nc):
    nctx = CTX // CHUNK
    back = jnp.where(s < nctx, nctx - 1 - s, nc + nctx - 1 - s)
    return jnp.where(d == 0, s, back)


def _scan_common(d, dt, arow, eexp, xs):
    ii = lax.broadcasted_iota(jnp.int32, (CHUNK, CHUNK), 0)
    jj = lax.broadcasted_iota(jnp.int32, (CHUNK, CHUNK), 1)
    mask = ((ii - jj) * (1 - 2 * d)) >= 0
    adt = dt * arow
    tmat = jnp.where(mask, 1.0, 0.0)
    cs = _dot_hi(tmat, adt)
    tot = _colsum(adt)
    dtx = _dot_hi(dt, eexp)
    xt = xs * dtx
    ecs = jnp.exp(cs)
    ecx = _dot_hi(ecs, eexp)
    dte = jnp.exp(tot - cs)
    dtex = _dot_hi(dte, eexp)
    etot = jnp.exp(tot)
    etx = _dot_hi(jnp.broadcast_to(etot, (8, 128)), eexp)[0:1, :]
    return mask, tmat, adt, cs, tot, dtx, xt, ecs, ecx, dte, dtex, etot, etx


def _decay_matrix(mask, cs, cst, h):
    return jnp.exp(jnp.where(mask, cs[:, h:h + 1] - cst[h:h + 1, :], -1e30))


def ssd_scan_fwd(xs, bm, cm, dtv, arow, eexp, nb, T):
    R = xs.shape[0]
    nc = T // CHUNK

    def rowmap(b, d, s):
        return (b * nc + _chunk_index(d, s, nc), 0)

    def body(xs_ref, bm_ref, cm_ref, dt_ref, a_ref, e_ref, y_ref, hin_ref, st_ref):
        d = pl.program_id(1)
        s = pl.program_id(2)

        @pl.when(s == 0)
        def _():
            st_ref[...] = jnp.zeros_like(st_ref)

        eexp = e_ref[...]
        mask, _, _, cs, _, _, xt, _, ecx, _, dtex, _, etx = _scan_common(
            d, dt_ref[0], a_ref[0, 0:1, :], eexp, xs_ref[...])
        cst = cs.T
        sin = st_ref[...]
        hin_ref[0] = sin
        sb = sin.astype(MXU)
        xtb = xt.astype(MXU)
        xw = (xt * dtex).astype(MXU)
        g0 = lax.broadcasted_iota(jnp.int32, (CHUNK, SSD_INNER), 1) < 192
        lane = lax.broadcasted_iota(jnp.int32, (CHUNK, 128), 1)
        c = [cm_ref[:, 0:128].astype(MXU), cm_ref[:, 128:256].astype(MXU)]
        b = [bm_ref[:, 0:128].astype(MXU), bm_ref[:, 128:256].astype(MXU)]
        y = jnp.where(g0, _dot(c[0], sb), _dot(c[1], sb)) * ecx
        cb = [_dotg(c[0], b[0], NT), _dotg(c[1], b[1], NT)]
        blocks = []
        for blk in range(3):
            acc = None
            for hh in range(2):
                h = blk * 2 + hh
                m = (cb[h // 3] * _decay_matrix(mask, cs, cst, h)).astype(MXU)
                res = _dot(m, xtb[:, blk * 128:(blk + 1) * 128])
                acc = res if hh == 0 else jnp.where(lane < 64, acc, res)
            blocks.append(acc)
        y_ref[0] = y + jnp.concatenate(blocks, axis=1)
        st_ref[...] = sin * etx + jnp.where(g0, _dotg(b[0], xw, TN), _dotg(b[1], xw, TN))

    return pl.pallas_call(
        body, name="ssd_scan_fwd", grid=(nb, 2, nc),
        in_specs=[pl.BlockSpec((CHUNK, 384), rowmap), pl.BlockSpec((CHUNK, 256), rowmap),
                  pl.BlockSpec((CHUNK, 256), rowmap),
                  pl.BlockSpec((1, CHUNK, 128), lambda b, d, s: (d, b * nc + _chunk_index(d, s, nc), 0)),
                  pl.BlockSpec((1, 8, 128), lambda b, d, s: (d, 0, 0)),
                  pl.BlockSpec((128, 384), lambda b, d, s: (0, 0))],
        out_specs=[pl.BlockSpec((1, CHUNK, 384), lambda b, d, s: (d, b * nc + _chunk_index(d, s, nc), 0)),
                   pl.BlockSpec((1, CHUNK, 384), lambda b, d, s: ((b * 2 + d) * nc + _chunk_index(d, s, nc), 0, 0))],
        out_shape=[jax.ShapeDtypeStruct((2, R, 384), F32), jax.ShapeDtypeStruct((nb * 2 * nc, CHUNK, 384), F32)],
        scratch_shapes=[pltpu.VMEM((CHUNK, 384), F32)],
    )(xs, bm, cm, dtv, arow, eexp)


def ssd_scan_bwd(xs, bm, cm, dtv, arow, eexp, hin, dy, nb, T):
    R = xs.shape[0]
    nc = T // CHUNK

    def chunk(d, s):
        return _chunk_index(d, nc - 1 - s, nc)

    def rowmap(b, d, s):
        return (b * nc + chunk(d, s), 0)

    def dirmap(b, d, s):
        return (d, b * nc + chunk(d, s), 0)

    def body(xs_ref, bm_ref, cm_ref, dt_ref, a_ref, e_ref, hin_ref, dy_ref,
             dxs_ref, dbm_ref, dcm_ref, ddt_ref, da_ref, ds_ref):
        d = pl.program_id(1)
        s = pl.program_id(2)

        @pl.when(s == 0)
        def _():
            ds_ref[...] = jnp.zeros_like(ds_ref)
            da_ref[...] = jnp.zeros_like(da_ref)

        eexp = e_ref[...]
        dt = dt_ref[0]
        arow = a_ref[0, 0:1, :]
        xs_v = xs_ref[...]
        mask, tmat, adt, cs, tot, dtx, xt, ecs, ecx, dte, dtex, etot, etx = _scan_common(d, dt, arow, eexp, xs_v)
        cst = cs.T
        sin = hin_ref[0]
        sb = sin.astype(MXU)
        dsp = ds_ref[...]
        dyv = dy_ref[...]
        xtb = xt.astype(MXU)
        xw = (xt * dtex).astype(MXU)
        g0 = lax.broadcasted_iota(jnp.int32, (CHUNK, SSD_INNER), 1) < 192
        lane = lax.broadcasted_iota(jnp.int32, (CHUNK, 128), 1)
        sub = lax.broadcasted_iota(jnp.int32, (CHUNK, 128), 0)
        c = [cm_ref[:, 0:128].astype(MXU), cm_ref[:, 128:256].astype(MXU)]
        b = [bm_ref[:, 0:128].astype(MXU), bm_ref[:, 128:256].astype(MXU)]

        cs_prod = jnp.where(g0, _dot(c[0], sb), _dot(c[1], sb))
        dcsp = dyv * ecx
        dcsp_g = [jnp.where(g0, dcsp, 0.0).astype(MXU), jnp.where(g0, 0.0, dcsp).astype(MXU)]
        dcs = _dot_hi(dyv * cs_prod, eexp, NT) * ecs
        dc = [_dotg(dcsp_g[0], sb, NT), _dotg(dcsp_g[1], sb, NT)]
        dsin = _dotg(c[0], dcsp_g[0], TN) + _dotg(c[1], dcsp_g[1], TN) + dsp * etx

        detx = _colsum(dsp * sin)
        dtot = _dot_hi(jnp.broadcast_to(detx, (8, SSD_INNER)), eexp, NT)[0:1, :] * etot
        dsp_g = [jnp.where(g0, dsp, 0.0).astype(MXU), jnp.where(g0, 0.0, dsp).astype(MXU)]
        dxw = _dot(b[0], dsp_g[0]) + _dot(b[1], dsp_g[1])
        db = [_dotg(xw, dsp_g[0], NT), _dotg(xw, dsp_g[1], NT)]
        dxt = dxw * dtex
        ddte = _dot_hi(dxw * xt, eexp, NT) * dte
        dtot = dtot + _colsum(ddte)
        dcs = dcs - ddte

        cb = [_dotg(c[0], b[0], NT), _dotg(c[1], b[1], NT)]
        dg = [jnp.zeros((CHUNK, CHUNK), F32), jnp.zeros((CHUNK, CHUNK), F32)]
        dcs_rows = jnp.zeros((CHUNK, 128), F32)
        dxt_blocks = []
        for blk in range(3):
            acc = jnp.zeros((CHUNK, 128), F32)
            for hh in range(2):
                h = blk * 2 + hh
                g = h // 3
                mine = (lane < 64) if hh == 0 else (lane >= 64)
                dyh = jnp.where(mine, dyv[:, blk * 128:(blk + 1) * 128], 0.0).astype(MXU)
                lh = _decay_matrix(mask, cs, cst, h)
                m = cb[g] * lh
                dm = _dotg(dyh, xtb[:, blk * 128:(blk + 1) * 128], NT)
                acc = acc + _dotg(m.astype(MXU), dyh, TN)
                dg[g] = dg[g] + dm * lh
                q = dm * m
                dcs = dcs + jnp.where(lane == h, jnp.sum(q, axis=1, keepdims=True), 0.0)
                dcs_rows = dcs_rows - jnp.where(sub == h, jnp.sum(q, axis=0, keepdims=True), 0.0)
            dxt_blocks.append(acc)
        dxt = dxt + jnp.concatenate(dxt_blocks, axis=1)
        for g in range(2):
            dgb = dg[g].astype(MXU)
            dc[g] = dc[g] + _dot(dgb, b[g])
            db[g] = db[g] + _dotg(dgb, c[g], TN)
        dcs = dcs + dcs_rows.T

        dadt = _dot_hi(tmat, dcs, TN) + dtot
        ddt = dadt * arow + _dot_hi(dxt * xs_v, eexp, NT)
        da_ref[0, 0:1, :] += _colsum(dadt * dt)
        dxs_ref[0] = dxt * dtx
        dbm_ref[0] = jnp.concatenate(db, axis=1)
        dcm_ref[0] = jnp.concatenate(dc, axis=1)
        ddt_ref[0] = ddt
        ds_ref[...] = dsin

    return pl.pallas_call(
        body, name="ssd_scan_bwd", grid=(nb, 2, nc),
        in_specs=[pl.BlockSpec((CHUNK, 384), rowmap), pl.BlockSpec((CHUNK, 256), rowmap),
                  pl.BlockSpec((CHUNK, 256), rowmap), pl.BlockSpec((1, CHUNK, 128), dirmap),
                  pl.BlockSpec((1, 8, 128), lambda b, d, s: (d, 0, 0)),
                  pl.BlockSpec((128, 384), lambda b, d, s: (0, 0)),
                  pl.BlockSpec((1, CHUNK, 384), lambda b, d, s: ((b * 2 + d) * nc + chunk(d, s), 0, 0)),
                  pl.BlockSpec((CHUNK, 384), rowmap)],
        out_specs=[pl.BlockSpec((1, CHUNK, 384), dirmap), pl.BlockSpec((1, CHUNK, 256), dirmap),
                   pl.BlockSpec((1, CHUNK, 256), dirmap), pl.BlockSpec((1, CHUNK, 128), dirmap),
                   pl.BlockSpec((1, 8, 128), lambda b, d, s: (b * 2 + d, 0, 0))],
        out_shape=[jax.ShapeDtypeStruct((2, R, 384), F32), jax.ShapeDtypeStruct((2, R, 256), F32),
                   jax.ShapeDtypeStruct((2, R, 256), F32), jax.ShapeDtypeStruct((2, R, 128), F32),
                   jax.ShapeDtypeStruct((nb * 2, 8, 128), F32)],
        scratch_shapes=[pltpu.VMEM((CHUNK, 384), F32)],
    )(xs, bm, cm, dtv, arow, eexp, hin, dy)


def _group_rms(g):
    lane = lax.broadcasted_iota(jnp.int32, g.shape, 1)
    g0 = lane < 192
    gg = g * g
    s0 = jnp.sum(jnp.where(g0, gg, 0.0), axis=-1, keepdims=True)
    s1 = jnp.sum(gg, axis=-1, keepdims=True) - s0
    rstd = jnp.where(g0, lax.rsqrt(s0 * (1.0 / 192) + EPS), lax.rsqrt(s1 * (1.0 / 192) + EPS))
    return rstd, g0


def ssd_out_fwd(y2, xs, pz, dexp, nw):
    R = xs.shape[0]

    def body(y_ref, xs_ref, z_ref, d_ref, nw_ref, o_ref):
        z = z_ref[...]
        yy = y_ref[0] + y_ref[1] + xs_ref[...] * d_ref[...]
        g = yy * (z * _sigmoid(z))
        rstd, _ = _group_rms(g)
        o_ref[...] = g * rstd * nw_ref[...]

    return pl.pallas_call(
        body, name="ssd_out_fwd", grid=(R // TM,),
        in_specs=[pl.BlockSpec((2, TM, 384), lambda i: (0, i, 0)), _rowspec(384), _rowspec(384),
                  _fullspec((1, 384)), _fullspec((1, 384))],
        out_specs=_rowspec(384),
        out_shape=jax.ShapeDtypeStruct((R, 384), F32),
    )(y2, xs, pz, dexp, nw)


def ssd_out_bwd(dout, y2, xs, pz, dexp, nw):
    R = xs.shape[0]

    def body(do_ref, y_ref, xs_ref, z_ref, d_ref, nw_ref, dy_ref, dz_ref, dxs_ref, part_ref):
        z = z_ref[...]
        xs_v = xs_ref[...]
        yy = y_ref[0] + y_ref[1] + xs_v * d_ref[...]
        sig = _sigmoid(z)
        sz = z * sig
        g = yy * sz
        rstd, g0 = _group_rms(g)
        ghat = g * rstd
        do = do_ref[...]
        dgn = do * nw_ref[...]
        t = dgn * ghat
        t0 = jnp.sum(jnp.where(g0, t, 0.0), axis=-1, keepdims=True)
        t1 = jnp.sum(t, axis=-1, keepdims=True) - t0
        dg = rstd * (dgn - ghat * jnp.where(g0, t0, t1) * (1.0 / 192))
        dyy = dg * sz
        dy_ref[...] = dyy
        dz_ref[...] = dg * yy * (sig * (1.0 + z * (1.0 - sig)))
        dxs_ref[...] = dyy * d_ref[...]
        part_ref[0] = jnp.concatenate([_colsum(do * ghat), _colsum(dyy * xs_v), jnp.zeros((6, 384), F32)], axis=0)

    return pl.pallas_call(
        body, name="ssd_out_bwd", grid=(R // TM,),
        in_specs=[_rowspec(384), pl.BlockSpec((2, TM, 384), lambda i: (0, i, 0)), _rowspec(384), _rowspec(384),
                  _fullspec((1, 384)), _fullspec((1, 384))],
        out_specs=[_rowspec(384), _rowspec(384), _rowspec(384), pl.BlockSpec((1, 8, 384), lambda i: (i, 0, 0))],
        out_shape=[jax.ShapeDtypeStruct((R, 384), F32), jax.ShapeDtypeStruct((R, 384), F32),
                   jax.ShapeDtypeStruct((R, 384), F32), jax.ShapeDtypeStruct((R // TM, 8, 384), F32)],
    )(dout, y2, xs, pz, dexp, nw)


def ssd_prep_bwd_a(pxbc, plast, cw, cb, dtb, dxs_skip, dxs2, dbm2, dcm2, ddt2, blocks_per_sample):
    R = pxbc.shape[0]
    prev, nxt = _halo_specs(XBC, R)

    def body(cur_ref, prev_ref, nxt_ref, pl_ref, cw_ref, cb_ref, dtb_ref, dsk_ref, dxs_ref, dbm_ref, dcm_ref, ddt_ref,
             dpre_ref, dlast_ref, part_ref):
        i = pl.program_id(0)
        ext = _ext_rows(cur_ref[...], prev_ref[...], nxt_ref[...], i, blocks_per_sample)
        co = _conv_out(ext, cw_ref, cb_ref)
        sig = _sigmoid(co)
        up = jnp.concatenate([dsk_ref[...] + dxs_ref[0] + dxs_ref[1], dbm_ref[0] + dbm_ref[1],
                              dcm_ref[0] + dcm_ref[1]], axis=1)
        dpre = up * (sig * (1.0 + co * (1.0 - sig)))
        dpre_ref[...] = dpre
        raw = pl_ref[...] + dtb_ref[...]
        lane = lax.broadcasted_iota(jnp.int32, raw.shape, 1)
        ddt = (pltpu.roll(ddt_ref[0], DT0, axis=1) + pltpu.roll(ddt_ref[1], DT0 + SSD_HEADS, axis=1))
        ddt = jnp.where(jnp.logical_and(lane >= DT0, lane < DT0 + 2 * SSD_HEADS), ddt * _sigmoid(raw), 0.0)
        dlast_ref[...] = ddt
        rows = [_colsum(dpre * _shift(ext, k - 1)) for k in range(4)]
        rows.append(_colsum(dpre))
        rows.append(jnp.concatenate([_colsum(ddt), jnp.zeros((1, XBC - 128), F32)], axis=1))
        rows.append(jnp.zeros((2, XBC), F32))
        part_ref[0] = jnp.concatenate(rows, axis=0)

    dirspec = lambda n: pl.BlockSpec((2, SB, n), lambda i: (0, i, 0))
    return pl.pallas_call(
        body, name="ssd_prep_bwd_a", grid=(R // SB,),
        in_specs=[_rowspec(XBC, SB), prev, nxt, _rowspec(128, SB), _fullspec((8, XBC)), _fullspec((1, XBC)),
                  _fullspec((1, 128)), _rowspec(384, SB), dirspec(384), dirspec(256), dirspec(256), dirspec(128)],
        out_specs=[_rowspec(XBC, SB), _rowspec(128, SB), pl.BlockSpec((1, 8, XBC), lambda i: (i, 0, 0))],
        out_shape=[jax.ShapeDtypeStruct((R, XBC), F32), jax.ShapeDtypeStruct((R, 128), F32),
                   jax.ShapeDtypeStruct((R // SB, 8, XBC), F32)],
    )(pxbc, pxbc, pxbc, plast, cw, cb, dtb, dxs_skip, dxs2, dbm2, dcm2, ddt2)


def ssd_prep_bwd_b(dpre, cw, blocks_per_sample):
    R = dpre.shape[0]
    prev, nxt = _halo_specs(XBC, R)

    def body(cur_ref, prev_ref, nxt_ref, cw_ref, o_ref):
        i = pl.program_id(0)
        ext = _ext_rows(cur_ref[...], prev_ref[...], nxt_ref[...], i, blocks_per_sample)
        o_ref[...] = (cw_ref[0:1, :] * _shift(ext, 1) + cw_ref[1:2, :] * _shift(ext, 0)
                      + cw_ref[2:3, :] * _shift(ext, -1) + cw_ref[3:4, :] * _shift(ext, -2))

    return pl.pallas_call(
        body, name="ssd_prep_bwd_b", grid=(R // SB,),
        in_specs=[_rowspec(XBC, SB), prev, nxt, _fullspec((8, XBC))],
        out_specs=_rowspec(XBC, SB),
        out_shape=jax.ShapeDtypeStruct((R, XBC), F32),
    )(dpre, dpre, dpre, cw)


def _rope(u, cos, sa, sb):
    return u * cos + pltpu.roll(u, 120, axis=1) * sa + pltpu.roll(u, 8, axis=1) * sb


def _rope_t(du, cos, sa, sb):
    return du * cos + pltpu.roll(du * sa, 8, axis=1) + pltpu.roll(du * sb, 120, axis=1)


def mla_prep(pqa, pkva, plast, qnw, kvnw, wq, wk, wv, cos, sa, sb):
    R = pqa.shape[0]

    def body(qa_ref, kva_ref, pl_ref, qnw_ref, kvnw_ref, wq_ref, wk_ref, wv_ref, cos_ref, sa_ref, sb_ref,
             q_ref, k_ref, v_ref, cq_ref, ckv_ref):
        cos_v, sa_v, sb_v = cos_ref[...], sa_ref[...], sb_ref[...]
        xq, _ = _rms_hat(qa_ref[...])
        cq_ref[...] = (xq * qnw_ref[...]).astype(cq_ref.dtype)
        xkv, _ = _rms_hat(kva_ref[...])
        ckv_ref[...] = (xkv * kvnw_ref[...]).astype(ckv_ref.dtype)
        q = _dot(cq_ref[...], wq_ref[...])
        kn = _dot(ckv_ref[...], wk_ref[...])
        v_ref[...] = _dot(ckv_ref[...], wv_ref[...]).astype(v_ref.dtype)
        lane = lax.broadcasted_iota(jnp.int32, (TM, HP), 1)
        rope_lanes = jnp.logical_and(lane >= QK_NOPE, lane < QK_DIM)
        kr = _rope(jnp.where(rope_lanes, pltpu.roll(pl_ref[...], QK_NOPE, axis=1), 0.0), cos_v, sa_v, sb_v)
        for h in range(MLA_HEADS):
            cols = slice(h * HP, (h + 1) * HP)
            q_ref[:, cols] = (_rope(q[:, cols], cos_v, sa_v, sb_v) * Q_SCALE).astype(q_ref.dtype)
            k_ref[:, cols] = (kn[:, cols] + kr).astype(k_ref.dtype)

    return pl.pallas_call(
        body, name="mla_prep", grid=(R // TM,),
        in_specs=[_rowspec(256), _rowspec(256), _rowspec(128), _fullspec((1, 256)), _fullspec((1, 256)),
                  _fullspec((256, QW)), _fullspec((256, QW)), _fullspec((256, QW)),
                  _rowspec(HP), _rowspec(HP), _rowspec(HP)],
        out_specs=[_rowspec(QW), _rowspec(QW), _rowspec(QW), _rowspec(256), _rowspec(256)],
        out_shape=[jax.ShapeDtypeStruct((R, QW), MXU)] * 3 + [jax.ShapeDtypeStruct((R, 256), MXU)] * 2,
    )(pqa, pkva, plast, qnw, kvnw, wq, wk, wv, cos, sa, sb)


def mla_prep_bwd(dq, dk, dv, pqa, pkva, qnw, kvnw, wqt, wkt, wvt, cos, sa, sb):
    R = pqa.shape[0]

    def body(dq_ref, dk_ref, dv_ref, qa_ref, kva_ref, qnw_ref, kvnw_ref, wqt_ref, wkt_ref, wvt_ref,
             cos_ref, sa_ref, sb_ref, dqa_ref, dkva_ref, dkr_ref, dql_ref, dkm_ref, dvb_ref, part_ref):
        cos_v, sa_v, sb_v = cos_ref[...], sa_ref[...], sb_ref[...]
        lane = lax.broadcasted_iota(jnp.int32, (TM, HP), 1)
        rope_lanes = jnp.logical_and(lane >= QK_NOPE, lane < QK_DIM)
        dkr = jnp.zeros((TM, HP), F32)
        for h in range(MLA_HEADS):
            cols = slice(h * HP, (h + 1) * HP)
            dql_ref[:, cols] = (_rope_t(dq_ref[:, cols], cos_v, sa_v, sb_v) * ATT_SCALE).astype(dql_ref.dtype)
            dkh = dk_ref[:, cols] * LN2
            dkm_ref[:, cols] = jnp.where(lane < QK_NOPE, dkh, 0.0).astype(dkm_ref.dtype)
            dkr = dkr + jnp.where(rope_lanes, dkh, 0.0)
        dvb_ref[...] = dv_ref[...].astype(dvb_ref.dtype)
        dkr = jnp.where(rope_lanes, _rope_t(dkr, cos_v, sa_v, sb_v), 0.0)
        dkr_ref[...] = pltpu.roll(dkr, HP - QK_NOPE, axis=1)
        xq, rq = _rms_hat(qa_ref[...])
        dqa, dqnw = _rms_bwd(_dot(dql_ref[...], wqt_ref[...]), xq, rq, qnw_ref[...])
        dqa_ref[...] = dqa
        xkv, rkv = _rms_hat(kva_ref[...])
        dckv = _dot(dkm_ref[...], wkt_ref[...]) + _dot(dvb_ref[...], wvt_ref[...])
        dkva, dkvnw = _rms_bwd(dckv, xkv, rkv, kvnw_ref[...])
        dkva_ref[...] = dkva
        part_ref[0] = jnp.concatenate([dqnw, dkvnw, jnp.zeros((6, 256), F32)], axis=0)

    return pl.pallas_call(
        body, name="mla_prep_bwd", grid=(R // TM,),
        in_specs=[_rowspec(QW), _rowspec(QW), _rowspec(QW), _rowspec(256), _rowspec(256), _fullspec((1, 256)),
                  _fullspec((1, 256)), _fullspec((QW, 256)), _fullspec((QW, 256)), _fullspec((QW, 256)),
                  _rowspec(HP), _rowspec(HP), _rowspec(HP)],
        out_specs=[_rowspec(256), _rowspec(256), _rowspec(128), _rowspec(QW), _rowspec(QW), _rowspec(QW),
                   pl.BlockSpec((1, 8, 256), lambda i: (i, 0, 0))],
        out_shape=[jax.ShapeDtypeStruct((R, 256), F32), jax.ShapeDtypeStruct((R, 256), F32),
                   jax.ShapeDtypeStruct((R, 128), F32)] + [jax.ShapeDtypeStruct((R, QW), MXU)] * 3
                  + [jax.ShapeDtypeStruct((R // TM, 8, 256), F32)],
    )(dq, dk, dv, pqa, pkva, qnw, kvnw, wqt, wkt, wvt, cos, sa, sb)


ATT_SCALE = QK_DIM ** -0.5
TQ = 256


LOG2E = 1.4426950408889634
LN2 = 0.6931471805599453
Q_SCALE = ATT_SCALE * LOG2E


def attn_fwd(q, k, v, nb, T):
    R = q.shape[0]
    nq = T // TQ
    kc = T // 2

    def body(q_ref, k_ref, v_ref, o_ref, lse_ref):
        def part(lo, n):
            s = _dotg(q_ref[...], k_ref[lo:lo + n, :], NT)
            m = jnp.max(s, axis=-1, keepdims=True)
            p = jnp.exp2(s - m)
            return m, jnp.sum(p, axis=-1, keepdims=True), _dot(p.astype(MXU), v_ref[lo:lo + n, :])

        def finish(parts):
            m = parts[0][0]
            for pm, _, _ in parts[1:]:
                m = jnp.maximum(m, pm)
            l, o = 0.0, 0.0
            for pm, pl_, po in parts:
                a = jnp.exp2(pm - m)
                l = l + a * pl_
                o = o + a * po
            o_ref[...] = o / l
            lse_ref[...] = jnp.broadcast_to(m + jnp.log(l) * LOG2E, (TQ, HP))

        i = pl.program_id(2)
        pl.when(i == 0)(lambda: finish([part(0, CTX)]))
        pl.when(i > 0)(lambda: finish([part(0, kc), part(kc, kc)]))

    qspec = pl.BlockSpec((TQ, HP), lambda b, h, i: (b * nq + i, h))
    kspec = pl.BlockSpec((T, HP), lambda b, h, i: (b, h))
    return pl.pallas_call(
        body, name="attn_fwd", grid=(nb, MLA_HEADS, nq),
        in_specs=[qspec, kspec, kspec], out_specs=[qspec, qspec],
        out_shape=[jax.ShapeDtypeStruct((R, QW), F32)] * 2,
        compiler_params=_cp(48),
    )(q, k, v)


def attn_bwd(q, k, v, o, lse, do, nb, T):
    R = q.shape[0]
    nq = T // TQ
    kc = T // 2

    def body(q_ref, k_ref, v_ref, o_ref, lse_ref, do_ref, dq_ref, dk_ref, dv_ref):
        i = pl.program_id(2)

        @pl.when(i == 0)
        def _():
            dk_ref[...] = jnp.zeros_like(dk_ref)
            dv_ref[...] = jnp.zeros_like(dv_ref)

        def run(chunks):
            qv = q_ref[...]
            dov = do_ref[...]
            dob = dov.astype(MXU)
            delta = jnp.sum(dov * o_ref[...], axis=-1, keepdims=True)
            lse_v = lse_ref[:, 0:1]
            dq = 0.0
            for lo, n in chunks:
                kv = k_ref[lo:lo + n, :]
                p = jnp.exp2(_dotg(qv, kv, NT) - lse_v)
                dp = _dotg(dob, v_ref[lo:lo + n, :], NT)
                dsb = (p * (dp - delta)).astype(MXU)
                dq = dq + _dot(dsb, kv)
                dk_ref[lo:lo + n, :] += _dotg(dsb, qv, TN)
                dv_ref[lo:lo + n, :] += _dotg(p.astype(MXU), dob, TN)
            dq_ref[...] = dq

        pl.when(i == 0)(lambda: run([(0, CTX)]))
        pl.when(i > 0)(lambda: run([(0, kc), (kc, kc)]))

    qspec = pl.BlockSpec((TQ, HP), lambda b, h, i: (b * nq + i, h))
    kspec = pl.BlockSpec((T, HP), lambda b, h, i: (b, h))
    return pl.pallas_call(
        body, name="attn_bwd", grid=(nb, MLA_HEADS, nq),
        in_specs=[qspec, kspec, kspec, qspec, qspec, qspec],
        out_specs=[qspec, kspec, kspec],
        out_shape=[jax.ShapeDtypeStruct((R, QW), F32)] * 3,
        compiler_params=_cp(56),
    )(q, k, v, o, lse, do)


def _pool_geometry(i, blocks_per_sample, seq):
    j = i % blocks_per_sample
    n = jnp.where(j == 0, CTX, seq)
    t0 = jnp.where(j == 0, 0, (j - 1) * SB) - HALO
    lane = lax.broadcasted_iota(jnp.int32, (SB + 2 * HALO, POOL_DIM), 1)
    t = lax.broadcasted_iota(jnp.int32, (SB + 2 * HALO, POOL_DIM), 0) + t0
    wh = jnp.where(lane < 64, 1, jnp.where(lane < 128, 2, jnp.where(lane < 192, 4, 8)))
    cnt = jnp.minimum(t + wh, n) - jnp.maximum(t - wh, 0)
    return lane, 1.0 / jnp.maximum(cnt, 1).astype(F32)


def _by_window(lane, c2, c4, c8, c16):
    return jnp.where(lane < 64, c2, jnp.where(lane < 128, c4, jnp.where(lane < 192, c8, c16)))


def _window_sums(ext, lane, first):
    n = ext.shape[0]
    r = lambda a, s: pltpu.roll(a, s % n, axis=0)
    c2 = ext + r(ext, first)
    c4 = r(c2, 1) + r(c2, -1)
    c8 = r(c4, 2) + r(c4, -2)
    c16 = r(c8, 4) + r(c8, -4)
    return _by_window(lane, c2, c4, c8, c16)


def _pool_delta(ext, lane, inv):
    return (_window_sums(ext, lane, 1) * inv - ext)[HALO:HALO + SB, :]


def pool_fwd(ppool, wbd, scale, blocks_per_sample, seq):
    R = ppool.shape[0]
    prev, nxt = _halo_specs(POOL_DIM, R)

    def body(cur_ref, prev_ref, nxt_ref, w_ref, s_ref, o_ref):
        i = pl.program_id(0)
        ext = _ext_rows(cur_ref[...], prev_ref[...], nxt_ref[...], i, blocks_per_sample)
        lane, inv = _pool_geometry(i, blocks_per_sample, seq)
        dlt = _pool_delta(ext, lane, inv)
        o_ref[...] = _dot(dlt.astype(MXU), w_ref[...]) * s_ref[...]

    return pl.pallas_call(
        body, name="pool_fwd", grid=(R // SB,),
        in_specs=[_rowspec(POOL_DIM, SB), prev, nxt, _fullspec((POOL_DIM, POOL_DIM)), _fullspec((1, POOL_DIM))],
        out_specs=_rowspec(POOL_DIM, SB),
        out_shape=jax.ShapeDtypeStruct((R, POOL_DIM), F32),
    )(ppool, ppool, ppool, wbd, scale)


def pool_bwd(ppool, dpool, wbd, wbdt, scale, blocks_per_sample, seq):
    R = ppool.shape[0]
    prev, nxt = _halo_specs(POOL_DIM, R)

    def body(cur_ref, prev_ref, nxt_ref, dcur_ref, dprev_ref, dnxt_ref, w_ref, wt_ref, s_ref, du_ref, dw_ref, part_ref):
        i = pl.program_id(0)

        @pl.when(i == 0)
        def _():
            dw_ref[...] = jnp.zeros_like(dw_ref)

        ext = _ext_rows(cur_ref[...], prev_ref[...], nxt_ref[...], i, blocks_per_sample)
        lane, inv = _pool_geometry(i, blocks_per_sample, seq)
        dlt = _pool_delta(ext, lane, inv).astype(MXU)
        dy = dcur_ref[...]
        part_ref[0] = jnp.concatenate([_colsum(dy * _dot(dlt, w_ref[...])), jnp.zeros((7, POOL_DIM), F32)], axis=0)
        dyp = (dy * s_ref[...]).astype(MXU)
        dw_ref[...] += _dotg(dlt, dyp, TN)
        dext = _ext_rows(dy, dprev_ref[...], dnxt_ref[...], i, blocks_per_sample)
        dd = _dot((dext * s_ref[...]).astype(MXU), wt_ref[...])
        du_ref[...] = (_window_sums(dd * inv, lane, -1) - dd)[HALO:HALO + SB, :]

    return pl.pallas_call(
        body, name="pool_bwd", grid=(R // SB,),
        in_specs=[_rowspec(POOL_DIM, SB), prev, nxt, _rowspec(POOL_DIM, SB), prev, nxt,
                  _fullspec((POOL_DIM, POOL_DIM)), _fullspec((POOL_DIM, POOL_DIM)), _fullspec((1, POOL_DIM))],
        out_specs=[_rowspec(POOL_DIM, SB), _fullspec((POOL_DIM, POOL_DIM)),
                   pl.BlockSpec((1, 8, POOL_DIM), lambda i: (i, 0, 0))],
        out_shape=[jax.ShapeDtypeStruct((R, POOL_DIM), F32), jax.ShapeDtypeStruct((POOL_DIM, POOL_DIM), F32),
                   jax.ShapeDtypeStruct((R // SB, 8, POOL_DIM), F32)],
    )(ppool, ppool, ppool, dpool, dpool, dpool, wbd, wbdt, scale)


def adamw(w, g, m, v, name="adamw"):
    rows, cols = w.shape
    tr = rows
    for cand in (512, 256, 128, 64, 32, 16, 8):
        if rows % cand == 0:
            tr = cand
            break
    bc1 = 1.0 - ADAM_B1 ** ADAM_STEP
    bc2 = 1.0 - ADAM_B2 ** ADAM_STEP

    def body(w_ref, g_ref, m_ref, v_ref, d_ref, nm_ref, nv_ref):
        g_v = g_ref[...]
        nm = ADAM_B1 * m_ref[...] + (1.0 - ADAM_B1) * g_v
        nv = ADAM_B2 * v_ref[...] + (1.0 - ADAM_B2) * (g_v * g_v)
        nm_ref[...] = nm
        nv_ref[...] = nv
        d_ref[...] = -ADAM_LR * ((nm / bc1) / (jnp.sqrt(nv / bc2) + ADAM_EPS) + ADAM_WD * w_ref[...])

    spec = pl.BlockSpec((tr, cols), lambda i: (i, 0))
    return pl.pallas_call(
        body, name=name, grid=(rows // tr,),
        in_specs=[spec] * 4, out_specs=[spec] * 3,
        out_shape=[jax.ShapeDtypeStruct((rows, cols), F32)] * 3,
    )(w, g, m, v)


MODR = 32


def _silu(v):
    return v * _sigmoid(v)


def mod_fwd(cond, w, b):
    n = w.shape[1]

    def body(c_ref, w_ref, b_ref, o_ref):
        o_ref[...] = _dot(_silu(c_ref[...]).astype(MXU), w_ref[...].astype(MXU)) + b_ref[...]

    return pl.pallas_call(
        body, name="mod_fwd", out_shape=jax.ShapeDtypeStruct((MODR, n), F32),
        in_specs=[_fullspec((MODR, D)), _fullspec((D, n)), _fullspec((1, n))], out_specs=_fullspec((MODR, n)),
        grid=(1,), compiler_params=_cp(40),
    )(cond, w, b)


def mod_wgrad(cond, dm):
    n = dm.shape[1]

    def body(c_ref, d_ref, o_ref):
        o_ref[...] = _dotg(_silu(c_ref[...]).astype(MXU), d_ref[...].astype(MXU), TN)

    return pl.pallas_call(
        body, name="mod_wgrad", out_shape=jax.ShapeDtypeStruct((D, n), F32),
        in_specs=[_fullspec((MODR, D)), _fullspec((MODR, n))], out_specs=_fullspec((D, n)),
        grid=(1,), compiler_params=_cp(40),
    )(cond, dm)


def mod_dgrad(dm, w):
    n = w.shape[1]

    def body(d_ref, w_ref, o_ref):
        o_ref[...] = _dotg(d_ref[...].astype(MXU), w_ref[...].astype(MXU), NT)

    return pl.pallas_call(
        body, name="mod_dgrad", out_shape=jax.ShapeDtypeStruct((8, D), F32),
        in_specs=[_fullspec((8, n)), _fullspec((D, n))], out_specs=_fullspec((8, D)),
        grid=(1,), compiler_params=_cp(40),
    )(dm, w)


def sum_leading(a, name="sum_leading"):
    n, r, c = a.shape

    def body(a_ref, o_ref):
        acc = a_ref[0]
        for k in range(1, n):
            acc = acc + a_ref[k]
        o_ref[...] = acc

    return pl.pallas_call(
        body, name=name, out_shape=jax.ShapeDtypeStruct((r, c), F32),
        in_specs=[_fullspec((n, r, c))], out_specs=_fullspec((r, c)), grid=(1,),
    )(a)


MESH = pl.DeviceIdType.MESH
NDEV = 8
ANY = pl.BlockSpec(memory_space=pl.ANY)


def _place():
    return lax.axis_index("x"), lax.axis_index("y"), lax.axis_index("c")


def _other_chips(x, y):
    return [(1 - x, y), (x, 1 - y), (1 - x, 1 - y)]


def allgather_small(v, name):
    r, cols = v.shape

    def body(v_ref, o_ref, send_sems, recv_sems):
        x, y, c = _place()
        me = 4 * x + 2 * y + c
        o_ref[me] = v_ref[...]
        copies = []
        for rel in range(1, NDEV):
            peer = (1 - x if rel & 4 else x, 1 - y if rel & 2 else y, 1 - c if rel & 1 else c)
            cp = pltpu.make_async_remote_copy(src_ref=v_ref, dst_ref=o_ref.at[me], send_sem=send_sems.at[rel - 1],
                                              recv_sem=recv_sems.at[rel - 1], device_id=peer, device_id_type=MESH)
            cp.start()
            copies.append(cp)
        for cp in copies:
            cp.wait_recv()
        for cp in copies:
            cp.wait_send()

    return pl.pallas_call(
        body, name=name, out_shape=jax.ShapeDtypeStruct((NDEV, r, cols), F32),
        in_specs=[pl.BlockSpec(memory_space=pltpu.VMEM)], out_specs=pl.BlockSpec(memory_space=pltpu.VMEM),
        scratch_shapes=[pltpu.SemaphoreType.DMA((NDEV - 1,)), pltpu.SemaphoreType.DMA((NDEV - 1,))],
        compiler_params=_cp(40),
    )(v)


def _sems(n):
    return [pltpu.SemaphoreType.DMA((n,)), pltpu.SemaphoreType.DMA((n,))]


def gather_shards(arrs):
    n = len(arrs)

    def body(*refs):
        srcs, outs = refs[:n], refs[n:2 * n]
        send_sems, recv_sems, local_sems = refs[2 * n:]
        x, y, c = _place()
        k = 2 * x + y
        sib = (x, y, 1 - c)
        chips = _other_chips(x, y)
        local = [pltpu.make_async_copy(srcs[i], outs[i].at[k], local_sems.at[i]) for i in range(n)]
        for cp in local:
            cp.start()

        def half(i, kk, cc):
            hr = arrs[i].shape[1] // 2
            return outs[i].at[kk, :, pl.ds(cc * hr, hr), :]

        def copy(i, slot, kk, cc, to, src=None):
            return pltpu.make_async_remote_copy(src_ref=half(i, kk, cc) if src is None else src, dst_ref=half(i, kk, cc),
                                                send_sem=send_sems.at[slot * n + i], recv_sem=recv_sems.at[slot * n + i],
                                                device_id=to, device_id_type=MESH)

        started = []
        for j, (px, py) in enumerate(chips):
            for i in range(n):
                hr = arrs[i].shape[1] // 2
                cp = copy(i, j, k, c, (px, py, c), src=srcs[i].at[:, pl.ds(c * hr, hr), :])
                cp.start()
                started.append(cp)
        for j, (px, py) in enumerate(chips):
            for i in range(n):
                copy(i, j, 2 * px + py, c, (px, py, c)).wait_recv()
                cp = copy(i, 3 + j, 2 * px + py, c, sib)
                cp.start()
                started.append(cp)
        for j, (px, py) in enumerate(chips):
            for i in range(n):
                copy(i, 3 + j, 2 * px + py, 1 - c, sib).wait_recv()
        for cp in started:
            cp.wait_send()
        for cp in local:
            cp.wait()

    return pl.pallas_call(
        body, name="gather_shards", out_shape=[jax.ShapeDtypeStruct((4,) + a.shape, a.dtype) for a in arrs],
        in_specs=[ANY] * n, out_specs=[ANY] * n,
        scratch_shapes=_sems(6 * n) + [pltpu.SemaphoreType.DMA((n,))],
    )(*arrs)


def swap_core_halves(gs):
    n = len(gs)

    def body(*refs):
        srcs, outs = refs[:n], refs[n:2 * n]
        send_sems, recv_sems = refs[2 * n:]
        x, y, c = _place()
        copies = []
        for i in range(n):
            hr = gs[i].shape[1] // 2
            cp = pltpu.make_async_remote_copy(src_ref=srcs[i].at[:, pl.ds((1 - c) * hr, hr), :], dst_ref=outs[i],
                                              send_sem=send_sems.at[i], recv_sem=recv_sems.at[i],
                                              device_id=(x, y, 1 - c), device_id_type=MESH)
            cp.start()
            copies.append(cp)
        for cp in copies:
            cp.wait()

    return pl.pallas_call(
        body, name="swap_core_halves",
        out_shape=[jax.ShapeDtypeStruct((4, g.shape[1] // 2, g.shape[2]), g.dtype) for g in gs],
        in_specs=[ANY] * n, out_specs=[ANY] * n, scratch_shapes=_sems(n),
    )(*gs)


def add_half(g, r1, cidx, name):
    _, rows, cols = g.shape
    hr = rows // 2

    def body(c_ref, g_ref, r_ref, o_ref, ob_ref):
        s = g_ref[...] + r_ref[...]
        o_ref[...] = s
        ob_ref[...] = s.astype(BF16)

    blk = lambda f: pl.BlockSpec((1, hr, cols), f)
    return pl.pallas_call(
        body, name=name,
        out_shape=[jax.ShapeDtypeStruct((4, hr, cols), F32), jax.ShapeDtypeStruct((4, hr, cols), BF16)],
        grid_spec=pltpu.PrefetchScalarGridSpec(
            num_scalar_prefetch=1, grid=(4,),
            in_specs=[blk(lambda k, c_ref: (k, c_ref[0], 0)), blk(lambda k, c_ref: (k, 0, 0))],
            out_specs=[blk(lambda k, c_ref: (k, 0, 0)), blk(lambda k, c_ref: (k, 0, 0))]),
    )(cidx, g, r1)


def swap_chip_parts(ss):
    n = len(ss)

    def body(*refs):
        srcs, outs = refs[:n], refs[n:2 * n]
        send_sems, recv_sems = refs[2 * n:]
        x, y, c = _place()
        copies = []
        for j, (px, py) in enumerate(_other_chips(x, y)):
            for i in range(n):
                cp = pltpu.make_async_remote_copy(src_ref=srcs[i].at[2 * px + py], dst_ref=outs[i].at[j],
                                                  send_sem=send_sems.at[j * n + i], recv_sem=recv_sems.at[j * n + i],
                                                  device_id=(px, py, c), device_id_type=MESH)
                cp.start()
                copies.append(cp)
        for cp in copies:
            cp.wait()

    return pl.pallas_call(
        body, name="swap_chip_parts", out_shape=[jax.ShapeDtypeStruct((3,) + s.shape[1:], s.dtype) for s in ss],
        in_specs=[ANY] * n, out_specs=[ANY] * n, scratch_shapes=_sems(3 * n),
    )(*ss)


def sum_parts(s1, r2, kidx, name):
    _, hr, cols = s1.shape

    def body(k_ref, s_ref, r_ref, o_ref):
        o_ref[...] = ((s_ref[0] + r_ref[0].astype(F32)) + r_ref[1].astype(F32)) + r_ref[2].astype(F32)

    return pl.pallas_call(
        body, name=name, out_shape=jax.ShapeDtypeStruct((hr, cols), F32),
        grid_spec=pltpu.PrefetchScalarGridSpec(
            num_scalar_prefetch=1, grid=(1,),
            in_specs=[pl.BlockSpec((1, hr, cols), lambda i, k_ref: (k_ref[0], 0, 0)),
                      pl.BlockSpec((3, hr, cols), lambda i, k_ref: (0, 0, 0))],
            out_specs=pl.BlockSpec((hr, cols), lambda i, k_ref: (0, 0))),
    )(kidx, s1, r2)


def join_halves(hs):
    n = len(hs)
    nw = n // DEPTH

    def body(*refs):
        srcs, outs = refs[:n], refs[n:n + nw]
        send_sems, recv_sems, local_sems = refs[n + nw:]
        x, y, c = _place()
        started = []
        for i in range(n):
            w, l = divmod(i, DEPTH)
            hr = hs[i].shape[0]
            dst = outs[w].at[l, pl.ds(c * hr, hr), :]
            mine = pltpu.make_async_copy(srcs[i], dst, local_sems.at[i])
            mine.start()
            cp = pltpu.make_async_remote_copy(src_ref=srcs[i], dst_ref=dst, send_sem=send_sems.at[i],
                                              recv_sem=recv_sems.at[i], device_id=(x, y, 1 - c), device_id_type=MESH)
            cp.start()
            started.append((mine, cp))
        for i in range(n):
            w, l = divmod(i, DEPTH)
            hr = hs[i].shape[0]
            other = outs[w].at[l, pl.ds((1 - c) * hr, hr), :]
            pltpu.make_async_remote_copy(src_ref=srcs[i], dst_ref=other, send_sem=send_sems.at[i],
                                         recv_sem=recv_sems.at[i], device_id=(x, y, 1 - c),
                                         device_id_type=MESH).wait_recv()
        for mine, cp in started:
            cp.wait_send()
            mine.wait()

    return pl.pallas_call(
        body, name="join_halves",
        out_shape=[jax.ShapeDtypeStruct((DEPTH, 2 * hs[w * DEPTH].shape[0], hs[w * DEPTH].shape[1]), F32)
                   for w in range(nw)],
        in_specs=[ANY] * n, out_specs=[ANY] * nw, scratch_shapes=_sems(n) + [pltpu.SemaphoreType.DMA((n,))],
    )(*hs)


class _NS:
    def __init__(self, **kw):
        self.__dict__.update(kw)


def _prep_layer(win, wqb, wkvb, wout, w1, w2, conv_w, conv_b, dt_bias, a_log, ssd_d, ssd_nw, qnw, kvnw, pool_w,
                pool_scale, n1, n2):
    winp = jnp.concatenate([win[:, 0:384], win[:, 384:1280], win[:, 1292:1548], win[:, 1548:1804], win[:, 1836:2092],
                            win[:, 1804:1836], win[:, 1280:1292], jnp.zeros((D, NP - IN_COLS), win.dtype)], axis=1)
    wq = jnp.pad(wqb.reshape(256, MLA_HEADS, QK_DIM), ((0, 0), (0, 0), (0, HP - QK_DIM))).reshape(256, QW)
    kv3 = wkvb.reshape(256, MLA_HEADS, 128)
    wk = jnp.pad(kv3[:, :, :64], ((0, 0), (0, 0), (0, 64))).reshape(256, QW)
    wv = jnp.pad(kv3[:, :, 64:], ((0, 0), (0, 0), (0, 64))).reshape(256, QW)
    wo = jnp.concatenate([jnp.pad(wout[384:768].reshape(MLA_HEADS, 64, D), ((0, 0), (0, 64), (0, 0))).reshape(QW, D),
                          wout[0:384], wout[768:1024]], axis=0)
    wbd = (jnp.eye(4, dtype=F32)[:, None, :, None] * pool_w[:, :, None, :]).reshape(POOL_DIM, POOL_DIM).astype(MXU)
    a = -jnp.exp(a_log)
    return _NS(
        winp=winp, wint=winp.T, wq=wq, wqt=wq.T, wk=wk, wkt=wk.T, wv=wv, wvt=wv.T, wo=wo, wot=wo.T,
        w1=w1, w1t=w1.T, w2=w2, w2t=w2.T, wbd=wbd, wbdt=wbd.T,
        cw8=jnp.pad(conv_w, ((0, 4), (0, 0))), cb=conv_b[None],
        dtb=jnp.pad(dt_bias.reshape(1, 12), ((0, 0), (DT0, 128 - DT0 - 12))),
        arow=jnp.pad(a[:, None, :], ((0, 0), (0, 7), (0, 128 - SSD_HEADS))), a=a,
        dexp=jnp.repeat(ssd_d, SSD_P)[None], ssd_nw=ssd_nw[None], qnw=qnw[None], kvnw=kvnw[None],
        pscale=pool_scale[None], n1=n1[None], n2=n2[None])


def _unprep_grads(dwinp, dwq, dwk, dwv, dwo):
    dwin = jnp.concatenate([dwinp[:, 0:384], dwinp[:, 384:1280], dwinp[:, 2080:2092], dwinp[:, 1280:1536],
                            dwinp[:, 1536:1792], dwinp[:, 2048:2080], dwinp[:, 1792:2048]], axis=1)
    dwqb = dwq.reshape(256, MLA_HEADS, HP)[:, :, :QK_DIM].reshape(256, MLA_HEADS * QK_DIM)
    dwkvb = jnp.concatenate([dwk.reshape(256, MLA_HEADS, HP)[:, :, :64], dwv.reshape(256, MLA_HEADS, HP)[:, :, :64]],
                            axis=2).reshape(256, MLA_HEADS * 128)
    dwout = jnp.concatenate([dwo[QW:QW + 384], dwo[0:QW].reshape(MLA_HEADS, HP, D)[:, :64].reshape(384, D),
                             dwo[QW + 384:CAT]], axis=0)
    return dwin, dwqb, dwkvb, dwout


def _rope_tables(nb, N):
    t = jnp.arange(N, dtype=F32)
    row = jnp.floor(t / GRID_W)
    col = t - row * GRID_W
    inv = jnp.asarray(10000.0 ** (-np.arange(8, dtype=np.float32) / 8), F32)
    ang = jnp.stack([row[:, None] * inv, col[:, None] * inv], axis=1)
    cs, sn = jnp.cos(ang), jnp.sin(ang)
    zero = jnp.zeros_like(sn)
    lanes = lambda first, second: jnp.stack([first, second], axis=2).reshape(N, 32)
    pad = lambda a, fill: jnp.concatenate([jnp.full((N, 64), fill, F32), a, jnp.full((N, 32), fill, F32)], axis=1)
    tabs = []
    for tab, fill in ((pad(lanes(cs, cs), 1.0), 1.0), (pad(lanes(-sn, zero), 0.0), 0.0), (pad(lanes(zero, sn), 0.0), 0.0)):
        one = jnp.concatenate([jnp.full((CTX, 128), fill, F32), tab], axis=0)
        tabs.append(jnp.tile(one, (nb, 1)))
    return tabs


def _eexp():
    e = np.zeros((128, SSD_INNER), np.float32)
    for h in range(SSD_HEADS):
        e[h, h * SSD_P:(h + 1) * SSD_P] = 1.0
    return jnp.asarray(e)


def _layer_fwd(X, bm, lw, cst):
    nb, T, bps, N = cst.nb, cst.T, cst.bps, cst.N
    h1, pz, pxbc, pqa, pkva, ppool, plast = in_proj(X, bm, lw.n1, lw.winp)
    xs, bmat, cmat, dtv = ssd_prep(pxbc, plast, lw.cw8, lw.cb, lw.dtb, bps)
    y2, hin = ssd_scan_fwd(xs, bmat, cmat, dtv, lw.arow, cst.eexp, nb, T)
    ssd = ssd_out_fwd(y2, xs, pz, lw.dexp, lw.ssd_nw)
    q, k, v, cq, ckv = mla_prep(pqa, pkva, plast, lw.qnw, lw.kvnw, lw.wq, lw.wk, lw.wv, *cst.rope)
    attn, lse = attn_fwd(q, k, v, nb, T)
    pool = pool_fwd(ppool, lw.wbd, lw.pscale, bps, N)
    x1, mix, cat = mix_fwd(X, attn, ssd, pool, bm, lw.wo)
    x2, mo, r, h2 = mlp_fwd(x1, bm, lw.n2, lw.w1, lw.w2)
    sv = _NS(X=X, h1=h1, pz=pz, pxbc=pxbc, pqa=pqa, pkva=pkva, ppool=ppool, plast=plast, xs=xs, bmat=bmat, cmat=cmat,
             dtv=dtv, y2=y2, hin=hin, q=q, k=k, v=v, cq=cq, ckv=ckv, attn=attn, lse=lse, x1=x1, mix=mix, cat=cat, mo=mo, r=r,
             h2=h2)
    return x2, sv


def _layer_bwd(dx2, bm, lw, sv, cst):
    nb, T, bps, N = cst.nb, cst.T, cst.bps, cst.N
    dx1, du, dob, part_mlp = mlp_bwd(dx2, sv.x1, sv.mo, sv.r, bm, lw.n2, lw.w2t, lw.w1t)
    dw1 = mm_tn(sv.h2, du, name="wgrad_mlp1", col_blocks=True)
    dw2 = mm_tn(sv.r, dob, square_a=True, name="wgrad_mlp2")
    dattn, dssd, dpool, dmb, part_mix = mix_bwd(dx1, sv.mix, bm, lw.wot)
    dwo = mm_tn(sv.cat, dmb, name="wgrad_out")
    dppool, dwbd, part_pool = pool_bwd(sv.ppool, dpool, lw.wbd, lw.wbdt, lw.pscale, bps, N)
    dq, dk, dv = attn_bwd(sv.q, sv.k, sv.v, sv.attn, sv.lse, dattn, nb, T)
    dpqa, dpkva, dkr, dql, dkm, dvb, part_mla = mla_prep_bwd(dq, dk, dv, sv.pqa, sv.pkva, lw.qnw, lw.kvnw, lw.wqt,
                                                             lw.wkt, lw.wvt, *cst.rope)
    dwq = mm_tn(sv.cq, dql, name="wgrad_q")
    dwk = mm_tn(sv.ckv, dkm, name="wgrad_k")
    dwv = mm_tn(sv.ckv, dvb, name="wgrad_v")
    dyy, dz, dxs_skip, part_so = ssd_out_bwd(dssd, sv.y2, sv.xs, sv.pz, lw.dexp, lw.ssd_nw)
    dxs2, dbm2, dcm2, ddt2, da = ssd_scan_bwd(sv.xs, sv.bmat, sv.cmat, sv.dtv, lw.arow, cst.eexp, sv.hin, dyy, nb, T)
    dpre, dlast_dt, part_conv = ssd_prep_bwd_a(sv.pxbc, sv.plast, lw.cw8, lw.cb, lw.dtb, dxs_skip, dxs2, dbm2, dcm2,
                                               ddt2, bps)
    dpxbc = ssd_prep_bwd_b(dpre, lw.cw8, bps)
    dx, dpb, part_in = in_proj_bwd(dx1, sv.X, dz, dpxbc, dpqa, dpkva, dppool, dkr, dlast_dt, bm, lw.n1, lw.wint)
    dwinp = mm_tn(sv.h1, dpb, name="wgrad_in")

    dwin, dwqb, dwkvb, dwout = _unprep_grads(dwinp, dwq, dwk, dwv, dwo)
    dmod = jnp.stack([part_in[:, 0], part_in[:, 1], part_mix[:, 0], part_mlp[:, 0], part_mlp[:, 1], part_mlp[:, 2]],
                     axis=1)
    dmod = dmod.reshape(nb, bps, 6, D)
    dm_rows = jnp.concatenate([jnp.sum(dmod[:, 1:], axis=1), jnp.sum(dmod[:, 0], axis=0)[None]], axis=0)
    da_dh = jnp.sum(da.reshape(nb, 2, 8, 128)[:, :, 0, :SSD_HEADS], axis=0)
    conv_parts = jnp.sum(part_conv, axis=0)
    by_chip_cols = lambda a: a.reshape(a.shape[0], 4, a.shape[1] // 4).transpose(1, 0, 2)
    by_chip_rows = lambda a: a.reshape(4, a.shape[0] // 4, a.shape[1])
    g = _NS(
        w_in=by_chip_cols(dwin), w_q_b=by_chip_cols(dwqb), w_kv_b=by_chip_cols(dwkvb), w_out=by_chip_rows(dwout),
        w_mlp1=dw1, w_mlp2=by_chip_rows(dw2),
        dm_rows=dm_rows.reshape(3, 6 * D),
        norm1_w=jnp.sum(part_in[:, 2], axis=0), norm2_w=jnp.sum(part_mlp[:, 3], axis=0),
        conv_w=conv_parts[0:4], conv_b=conv_parts[4],
        dt_bias=conv_parts[5, DT0:DT0 + 12].reshape(2, SSD_HEADS), a_log=da_dh * lw.a,
        ssd_d=jnp.sum(jnp.sum(part_so[:, 1], axis=0).reshape(SSD_HEADS, SSD_P), axis=1),
        ssd_norm_w=jnp.sum(part_so[:, 0], axis=0),
        q_a_norm_w=jnp.sum(part_mla[:, 0], axis=0), kv_a_norm_w=jnp.sum(part_mla[:, 1], axis=0),
        pool_w=jnp.stack([dwbd[i * 64:(i + 1) * 64, i * 64:(i + 1) * 64] for i in range(4)]),
        pool_scale=jnp.sum(part_pool[:, 0], axis=0))
    return dx, g


def _local_step(x, ctx, tgt, bms, lws, fw, cst):
    nb, N = x.shape[0], x.shape[1]
    R = nb * cst.T
    X = jnp.concatenate([ctx, x], axis=1).reshape(R, D)
    saved = []
    for l in range(DEPTH):
        X, sv = _layer_fwd(X, bms[l], lws[l], cst)
        saved.append(sv)
    dX, part_fin = final_loss(X, tgt.reshape(nb * N, D), fw[None], cst.bps)
    loss = (0.5 / D) * jnp.sum(part_fin[:, 1])
    dfw = jnp.sum(part_fin[:, 0], axis=0)
    grads = [None] * DEPTH
    for l in reversed(range(DEPTH)):
        dX, grads[l] = _layer_bwd(dX, bms[l], lws[l], saved[l], cst)
    grad_x = dX.reshape(nb, cst.T, D)[:, CTX:, :]
    return loss, grad_x, grads, dfw


def _consts(nb, N):
    T = CTX + N
    bps = T // SB
    return _NS(nb=nb, N=N, T=T, bps=bps, eexp=_eexp(), rope=_rope_tables(nb, N))


def _block_mod(modrows, cst):
    rows = []
    for b in range(cst.nb):
        rows.append(modrows[cst.nb:cst.nb + 1])
        rows.append(jnp.broadcast_to(modrows[b:b + 1], (cst.bps - 1, 6, D)))
    return jnp.pad(jnp.concatenate(rows, axis=0), ((0, 0), (0, 2), (0, 0)))


SMALL = (("norm1_w", (2, D)), ("norm2_w", (2, D)), ("conv_w", (2, 4, XBC)), ("conv_b", (2, XBC)),
         ("dt_bias", (2, 2, 6)), ("a_log", (2, 2, 6)), ("ssd_d", (2, 6)), ("ssd_norm_w", (2, 384)),
         ("q_a_norm_w", (2, 256)), ("kv_a_norm_w", (2, 256)), ("pool_w", (2, 4, 64, 64)), ("pool_scale", (2, 256)),
         ("final_norm_w", (D,)), ("mod_b", (2, 6 * D)))
SMALL_ROWS = 64
DM_ROWS = 48


def _pack_small(vals):
    flat = jnp.concatenate([vals[n].reshape(-1) for n, _ in SMALL])
    return jnp.pad(flat, (0, SMALL_ROWS * D - flat.shape[0])).reshape(SMALL_ROWS, D)


def _unpack_small(p):
    flat = p.reshape(-1)
    out, off = {}, 0
    for n, shp in SMALL:
        size = int(np.prod(shp))
        out[n] = flat[off:off + size].reshape(shp)
        off += size
    return out


def cctx_grad(parts, c_ctx):
    def body(p_ref, c_ref, o_ref):
        acc = ((p_ref[0] + p_ref[1]) + p_ref[2]) + p_ref[3]
        v = c_ref[...]
        sig = _sigmoid(v)
        o_ref[...] = acc * (sig * (1.0 + v * (1.0 - sig)))

    return pl.pallas_call(
        body, name="cctx_grad", out_shape=jax.ShapeDtypeStruct((8, D), F32),
        in_specs=[_fullspec((4, 8, D)), _fullspec((1, D))], out_specs=_fullspec((8, D)), grid=(1,),
    )(parts, c_ctx)


def kernel(x, c, ctx, c_ctx, mod_w, mod_b, norm1_w, norm2_w, w_in, conv_w, conv_b, dt_bias, a_log, ssd_d, ssd_norm_w, q_a_norm_w, w_q_b, kv_a_norm_w, w_kv_b, pool_w, pool_scale, w_out, w_mlp1, w_mlp2, final_norm_w, loss_target, m_c_ctx, m_mod_w, m_mod_b, m_norm1_w, m_norm2_w, m_w_in, m_conv_w, m_conv_b, m_dt_bias, m_a_log, m_ssd_d, m_ssd_norm_w, m_q_a_norm_w, m_w_q_b, m_kv_a_norm_w, m_w_kv_b, m_pool_w, m_pool_scale, m_w_out, m_w_mlp1, m_w_mlp2, m_final_norm_w, v_c_ctx, v_mod_w, v_mod_b, v_norm1_w, v_norm2_w, v_w_in, v_conv_w, v_conv_b, v_dt_bias, v_a_log, v_ssd_d, v_ssd_norm_w, v_q_a_norm_w, v_w_q_b, v_kv_a_norm_w, v_w_kv_b, v_pool_w, v_pool_scale, v_w_out, v_w_mlp1, v_w_mlp2, v_final_norm_w):
    nb, N = x.shape[0], x.shape[1]
    cst = _consts(nb, N)
    xi, yi, ci = _place()
    me = 4 * xi + 2 * yi + ci
    kchip = 2 * xi + yi
    mcols = mod_w.shape[2]
    cshard = conv_w.shape[2]

    blk = jnp.zeros((16, D), F32).at[0:nb].set(c).at[8:16, 0:cshard].set(conv_w.reshape(8, cshard))
    g1 = allgather_small(blk, "gather_cond")
    cond = jnp.concatenate([g1[:, 0:nb].reshape(NDEV * nb, D), c_ctx[None],
                            jnp.zeros((MODR - NDEV * nb - 1, D), F32)], axis=0)
    conv_full = [jnp.concatenate([g1[2 * k, 8 + 4 * l:12 + 4 * l, 0:cshard] for k in range(4)], axis=1)
                 for l in range(DEPTH)]

    mb = [lax.dynamic_slice_in_dim(mod_b[l], kchip * mcols, mcols)[None] for l in range(DEPTH)]
    ms = jnp.concatenate([mod_fwd(cond, mod_w[l], mb[l]) for l in range(DEPTH)], axis=0)
    g2 = allgather_small(ms, "gather_mod")
    bms = []
    for l in range(DEPTH):
        m_all = jnp.concatenate([g2[2 * k, MODR * l:MODR * (l + 1)] for k in range(4)], axis=1)
        mine = jnp.concatenate([lax.dynamic_slice_in_dim(m_all, nb * me, nb), m_all[NDEV * nb:NDEV * nb + 1]], axis=0)
        bms.append(_block_mod(mine.reshape(nb + 1, 6, D), cst))

    big = (w_in, w_q_b, w_kv_b, w_out, w_mlp1, w_mlp2)
    wg = gather_shards([a.astype(MXU) for a in big])
    fw_in, fw_qb, fw_kvb, fw_out, fw1, fw2 = [
        jnp.concatenate([g[k] for k in range(4)], axis=ax) for g, ax in zip(wg, (2, 2, 2, 1, 2, 1))]
    lws = [_prep_layer(fw_in[l], fw_qb[l], fw_kvb[l], fw_out[l], fw1[l], fw2[l], conv_full[l], conv_b[l], dt_bias[l],
                       a_log[l], ssd_d[l], ssd_norm_w[l], q_a_norm_w[l], kv_a_norm_w[l], pool_w[l], pool_scale[l],
                       norm1_w[l], norm2_w[l]) for l in range(DEPTH)]

    loss_part, grad_x, grads, dfw = _local_step(x, ctx, loss_target, bms, lws, final_norm_w, cst)
    loss = lax.psum(loss_part, ("x", "y", "c"))

    names = ("w_in", "w_q_b", "w_kv_b", "w_out", "w_mlp1", "w_mlp2")
    gs = [getattr(grads[l], n) for n in names for l in range(DEPTH)]
    cidx = jnp.reshape(ci, (1,)).astype(jnp.int32)
    kidx = jnp.reshape(kchip, (1,)).astype(jnp.int32)
    s1 = [add_half(g, r, cidx, "add_half_" + names[i // DEPTH]) for i, (g, r) in enumerate(zip(gs, swap_core_halves(gs)))]
    r2 = swap_chip_parts([s[1] for s in s1])
    g_big = join_halves([sum_parts(s[0], r, kidx, "sum_parts_" + names[i // DEPTH]) for i, (s, r) in enumerate(zip(s1, r2))])

    small = {n: jnp.stack([getattr(grads[l], n) for l in range(DEPTH)]) for n, _ in SMALL if n not in ("final_norm_w", "mod_b")}
    small["final_norm_w"] = dfw
    small["mod_b"] = jnp.stack([jnp.sum(grads[l].dm_rows, axis=0) for l in range(DEPTH)])
    dm = jnp.pad(jnp.concatenate([grads[l].dm_rows for l in range(DEPTH)], axis=0), ((0, 8 - 3 * DEPTH), (0, 0)))
    g3 = allgather_small(jnp.concatenate([_pack_small(small), dm.reshape(DM_ROWS, D)], axis=0), "gather_small")
    tot = sum_leading(g3, "sum_small")
    gsmall = _unpack_small(tot[0:SMALL_ROWS])
    ctx_sum = tot[SMALL_ROWS:].reshape(8, 6 * D)
    dm_dev = g3[:, SMALL_ROWS:].reshape(NDEV, 8, 6 * D)
    g_mod_w, dpart = [], jnp.zeros((8, D), F32)
    for l in range(DEPTH):
        dm_all = jnp.concatenate([dm_dev[:, 3 * l:3 * l + nb].reshape(NDEV * nb, 6 * D), ctx_sum[3 * l + nb:3 * l + nb + 1],
                                  jnp.zeros((MODR - NDEV * nb - 1, 6 * D), F32)], axis=0)
        g_mod_w.append(mod_wgrad(cond, lax.dynamic_slice_in_dim(dm_all, kchip * mcols, mcols, axis=1)))
        dctx = jnp.pad(lax.dynamic_slice_in_dim(ctx_sum[3 * l + nb:3 * l + nb + 1], kchip * mcols, mcols, axis=1), ((0, 7), (0, 0)))
        dpart = dpart + mod_dgrad(dctx, mod_w[l])
    g4 = allgather_small(dpart, "gather_cctx")
    g_c_ctx = cctx_grad(g4[0::2], c_ctx[None])[0]

    res = {}
    moments = ((m_w_in, v_w_in), (m_w_q_b, v_w_q_b), (m_w_kv_b, v_w_kv_b), (m_w_out, v_w_out), (m_w_mlp1, v_w_mlp1),
               (m_w_mlp2, v_w_mlp2))
    for n, w, g, (m, v) in zip(names, big, g_big, moments):
        flat = lambda a: a.reshape(-1, a.shape[-1])
        r = adamw(flat(w), flat(g), flat(m), flat(v), name="adamw_" + n)
        res[n] = (g,) + tuple(a.reshape(w.shape) for a in r)
    g_mw = jnp.stack(g_mod_w)
    r_mw = adamw(mod_w.reshape(-1, mcols), g_mw.reshape(-1, mcols), m_mod_w.reshape(-1, mcols),
                 v_mod_w.reshape(-1, mcols), name="adamw_mod_w")
    res["mod_w"] = (g_mw,) + tuple(a.reshape(mod_w.shape) for a in r_mw)

    given = dict(norm1_w=(norm1_w, m_norm1_w, v_norm1_w), norm2_w=(norm2_w, m_norm2_w, v_norm2_w),
                 conv_b=(conv_b, m_conv_b, v_conv_b), dt_bias=(dt_bias, m_dt_bias, v_dt_bias),
                 a_log=(a_log, m_a_log, v_a_log), ssd_d=(ssd_d, m_ssd_d, v_ssd_d),
                 ssd_norm_w=(ssd_norm_w, m_ssd_norm_w, v_ssd_norm_w), q_a_norm_w=(q_a_norm_w, m_q_a_norm_w, v_q_a_norm_w),
                 kv_a_norm_w=(kv_a_norm_w, m_kv_a_norm_w, v_kv_a_norm_w), pool_w=(pool_w, m_pool_w, v_pool_w),
                 pool_scale=(pool_scale, m_pool_scale, v_pool_scale),
                 final_norm_w=(final_norm_w, m_final_norm_w, v_final_norm_w), mod_b=(mod_b, m_mod_b, v_mod_b))
    zero_cw = jnp.zeros((2, 4, XBC), F32)
    packs = [_pack_small({n: (given[n][i] if n in given else zero_cw) for n, _ in SMALL}) for i in range(3)]
    r_small = [_unpack_small(a) for a in adamw(packs[0], tot[0:SMALL_ROWS], packs[1], packs[2], name="adamw_small")]
    for n in given:
        res[n] = (gsmall[n], r_small[0][n], r_small[1][n], r_small[2][n])

    g_cw = lax.dynamic_slice_in_dim(gsmall["conv_w"], kchip * cshard, cshard, axis=2)
    padcw = lambda a: jnp.pad(a.reshape(8, cshard), ((0, 0), (0, 256 - cshard)))
    r_cw = adamw(padcw(conv_w), padcw(g_cw), padcw(m_conv_w), padcw(v_conv_w), name="adamw_conv_w")
    res["conv_w"] = (g_cw,) + tuple(a[:, 0:cshard].reshape(conv_w.shape) for a in r_cw)
    r_cc = adamw(c_ctx.reshape(8, 128), g_c_ctx.reshape(8, 128), m_c_ctx.reshape(8, 128), v_c_ctx.reshape(8, 128),
                 name="adamw_c_ctx")
    res["c_ctx"] = (g_c_ctx,) + tuple(a.reshape(D) for a in r_cc)

    order = ("c_ctx", "mod_w", "mod_b", "norm1_w", "norm2_w", "w_in", "conv_w", "conv_b", "dt_bias", "a_log", "ssd_d",
             "ssd_norm_w", "q_a_norm_w", "w_q_b", "kv_a_norm_w", "w_kv_b", "pool_w", "pool_scale", "w_out", "w_mlp1",
             "w_mlp2", "final_norm_w")
    return (loss, grad_x) + tuple(res[n][i] for i in range(4) for n in order)
```

```python
import functools
import math

import numpy as np
import jax
import jax.numpy as jnp
from jax import lax
from jax.experimental import pallas as pl
from jax.experimental.pallas import tpu as pltpu

F32 = jnp.float32
BF16 = jnp.bfloat16
MXU = jnp.bfloat16
HI = lax.Precision.HIGHEST

D = 1024
DEPTH = 2
GRID_W = 64
CTX = 256
EPS = 1e-6
SSD_HEADS = 6
SSD_P = 64
SSD_INNER = 384
SSD_N = 128
CHUNK = 128
XBC = 896
MLA_HEADS = 6
QK_NOPE = 64
QK_ROPE = 32
QK_DIM = 96
HP = 128
QW = MLA_HEADS * HP
POOL_DIM = 256
D_FF = 4096
FF_BLK = 1024
IN_COLS = 2092
NP = 2176
P_SPLITS = (384, 896, 256, 256, 256, 128)
DT0 = 32
CAT = QW + SSD_INNER + POOL_DIM

SB = 256
TM = 512
HALO = 8

ADAM_LR = 0.001
ADAM_B1 = 0.9
ADAM_B2 = 0.999
ADAM_EPS = 1e-08
ADAM_WD = 0.01
ADAM_STEP = 10

NT = (((1,), (1,)), ((), ()))
TN = (((0,), (0,)), ((), ()))


def _cp(vmem_mb=None):
    if vmem_mb is None:
        return pltpu.CompilerParams()
    return pltpu.CompilerParams(vmem_limit_bytes=vmem_mb << 20)


def _dot(a, b):
    return jnp.dot(a, b, preferred_element_type=F32)


def _dotg(a, b, dims):
    return lax.dot_general(a, b, dims, preferred_element_type=F32)


def _dot_hi(a, b, dims=None, sel_first=False):
    dims = (((1,), (0,)), ((), ())) if dims is None else dims
    v, s = (b, a) if sel_first else (a, b)
    hi = v.astype(BF16)
    lo = (v - hi.astype(F32)).astype(BF16)
    s = s.astype(BF16)
    if sel_first:
        return _dotg(s, hi, dims) + _dotg(s, lo, dims)
    return _dotg(hi, s, dims) + _dotg(lo, s, dims)


def _rms_hat(x):
    rstd = lax.rsqrt(jnp.mean(x * x, axis=-1, keepdims=True) + EPS)
    return x * rstd, rstd


def _rms_bwd(dn, xhat, rstd, w):
    dxhat = dn * w
    dx = rstd * (dxhat - xhat * jnp.mean(dxhat * xhat, axis=-1, keepdims=True))
    return dx, jnp.sum(dn * xhat, axis=0, keepdims=True)


def _sigmoid(z):
    return 1.0 / (1.0 + jnp.exp(-z))


def _colsum(a):
    return jnp.sum(a, axis=0, keepdims=True)


def _rowspec(cols, tm=TM):
    return pl.BlockSpec((tm, cols), lambda i: (i, 0))


def _fullspec(shape):
    n = len(shape)
    return pl.BlockSpec(shape, lambda *_: (0,) * n)


def _halo_specs(cols, nrows):
    per = SB // HALO
    last = nrows // HALO - 1
    prev = pl.BlockSpec((HALO, cols), lambda i: (jnp.maximum(i * per - 1, 0), 0))
    nxt = pl.BlockSpec((HALO, cols), lambda i: (jnp.minimum((i + 1) * per, last), 0))
    return prev, nxt


def _ext_rows(cur, prev, nxt, i, blocks_per_sample):
    j = i % blocks_per_sample
    first = jnp.logical_or(j == 0, j == 1)
    last = jnp.logical_or(j == 0, j == blocks_per_sample - 1)
    p = jnp.where(first, 0.0, prev)
    n = jnp.where(last, 0.0, nxt)
    return jnp.concatenate([p, cur, n], axis=0)


def _shift(ext, s):
    n = ext.shape[0]
    return pltpu.roll(ext, (-s) % n, axis=0)[HALO:HALO + SB, :]


def in_proj(x, bm, nw, w):
    R = x.shape[0]

    def body(x_ref, bm_ref, nw_ref, w_ref, h_ref, *outs):
        for s in range(TM // SB):
            rows = slice(s * SB, (s + 1) * SB)
            xhat, _ = _rms_hat(x_ref[rows, :])
            h = xhat * nw_ref[...] * (1.0 + bm_ref[s, 1:2, :]) + bm_ref[s, 0:1, :]
            h_ref[rows, :] = h.astype(h_ref.dtype)
        p = _dot(h_ref[...], w_ref[...])
        off = 0
        for o, n in zip(outs, P_SPLITS):
            o[...] = p[:, off:off + n]
            off += n

    return pl.pallas_call(
        body, name="in_proj", grid=(R // TM,),
        in_specs=[_rowspec(D), pl.BlockSpec((TM // SB, 8, D), lambda i: (i, 0, 0)), _fullspec((1, D)),
                  _fullspec((D, NP))],
        out_specs=[_rowspec(D)] + [_rowspec(n) for n in P_SPLITS],
        out_shape=[jax.ShapeDtypeStruct((R, D), MXU)] + [jax.ShapeDtypeStruct((R, n), F32) for n in P_SPLITS],
        compiler_params=_cp(56),
    )(x, bm, nw, w)


def in_proj_bwd(dx1, x, dz, dxbc, dqa, dkva, dpool, dkr, ddt, bm, nw, wt):
    R = x.shape[0]

    def body(dx1_ref, x_ref, dz_ref, dxbc_ref, dqa_ref, dkva_ref, dpool_ref, dkr_ref, ddt_ref, bm_ref, nw_ref,
             wt_ref, dx_ref, dp_ref, part_ref):
        dp_ref[:, 0:384] = dz_ref[...].astype(dp_ref.dtype)
        dp_ref[:, 384:1280] = dxbc_ref[...].astype(dp_ref.dtype)
        dp_ref[:, 1280:1536] = dqa_ref[...].astype(dp_ref.dtype)
        dp_ref[:, 1536:1792] = dkva_ref[...].astype(dp_ref.dtype)
        dp_ref[:, 1792:2048] = dpool_ref[...].astype(dp_ref.dtype)
        dp_ref[:, 2048:2176] = (dkr_ref[...] + ddt_ref[...]).astype(dp_ref.dtype)
        dh = _dot(dp_ref[...], wt_ref[...])
        w = nw_ref[...]
        for s in range(TM // SB):
            rows = slice(s * SB, (s + 1) * SB)
            xhat, rstd = _rms_hat(x_ref[rows, :])
            dhs = dh[rows, :]
            sc1 = 1.0 + bm_ref[s, 1:2, :]
            dx, dnw = _rms_bwd(dhs * sc1, xhat, rstd, w)
            dx_ref[rows, :] = dx1_ref[rows, :] + dx
            part_ref[s] = jnp.concatenate(
                [_colsum(dhs), _colsum(dhs * xhat * w), dnw, jnp.zeros((5, D), F32)], axis=0)

    return pl.pallas_call(
        body, name="in_proj_bwd", grid=(R // TM,),
        in_specs=[_rowspec(D), _rowspec(D), _rowspec(384), _rowspec(896), _rowspec(256), _rowspec(256),
                  _rowspec(256), _rowspec(128), _rowspec(128),
                  pl.BlockSpec((TM // SB, 8, D), lambda i: (i, 0, 0)), _fullspec((1, D)), _fullspec((NP, D))],
        out_specs=[_rowspec(D), _rowspec(NP), pl.BlockSpec((TM // SB, 8, D), lambda i: (i, 0, 0))],
        out_shape=[jax.ShapeDtypeStruct((R, D), F32), jax.ShapeDtypeStruct((R, NP), MXU),
                   jax.ShapeDtypeStruct((R // SB, 8, D), F32)],
        compiler_params=_cp(56),
    )(dx1, x, dz, dxbc, dqa, dkva, dpool, dkr, ddt, bm, nw, wt)


def mix_fwd(x, attn, ssd, pool, bm, wo):
    R = x.shape[0]

    def body(x_ref, a_ref, s_ref, p_ref, bm_ref, wo_ref, x1_ref, mix_ref, cat_ref):
        cat_ref[:, 0:QW] = a_ref[...].astype(cat_ref.dtype)
        cat_ref[:, QW:QW + SSD_INNER] = s_ref[...].astype(cat_ref.dtype)
        cat_ref[:, QW + SSD_INNER:CAT] = p_ref[...].astype(cat_ref.dtype)
        mix = _dot(cat_ref[...], wo_ref[...])
        mix_ref[...] = mix
        for s in range(TM // SB):
            rows = slice(s * SB, (s + 1) * SB)
            x1_ref[rows, :] = x_ref[rows, :] + bm_ref[s, 2:3, :] * mix[rows, :]

    return pl.pallas_call(
        body, name="mix_fwd", grid=(R // TM,),
        in_specs=[_rowspec(D), _rowspec(QW), _rowspec(SSD_INNER), _rowspec(POOL_DIM),
                  pl.BlockSpec((TM // SB, 8, D), lambda i: (i, 0, 0)), _fullspec((CAT, D))],
        out_specs=[_rowspec(D), _rowspec(D), _rowspec(CAT)],
        out_shape=[jax.ShapeDtypeStruct((R, D), F32), jax.ShapeDtypeStruct((R, D), F32),
                   jax.ShapeDtypeStruct((R, CAT), MXU)],
        compiler_params=_cp(48),
    )(x, attn, ssd, pool, bm, wo)


def mix_bwd(dx1, mix, bm, wot):
    R = dx1.shape[0]

    def body(dx1_ref, mix_ref, bm_ref, wot_ref, da_ref, ds_ref, dpl_ref, dmb_ref, part_ref):
        for s in range(TM // SB):
            rows = slice(s * SB, (s + 1) * SB)
            d = dx1_ref[rows, :]
            dmb_ref[rows, :] = (d * bm_ref[s, 2:3, :]).astype(dmb_ref.dtype)
            part_ref[s] = jnp.concatenate([_colsum(d * mix_ref[rows, :]), jnp.zeros((7, D), F32)], axis=0)
        dcat = _dot(dmb_ref[...], wot_ref[...])
        da_ref[...] = dcat[:, 0:QW]
        ds_ref[...] = dcat[:, QW:QW + SSD_INNER]
        dpl_ref[...] = dcat[:, QW + SSD_INNER:CAT]

    return pl.pallas_call(
        body, name="mix_bwd", grid=(R // TM,),
        in_specs=[_rowspec(D), _rowspec(D), pl.BlockSpec((TM // SB, 8, D), lambda i: (i, 0, 0)),
                  _fullspec((D, CAT))],
        out_specs=[_rowspec(QW), _rowspec(SSD_INNER), _rowspec(POOL_DIM), _rowspec(D),
                   pl.BlockSpec((TM // SB, 8, D), lambda i: (i, 0, 0))],
        out_shape=[jax.ShapeDtypeStruct((R, QW), F32), jax.ShapeDtypeStruct((R, SSD_INNER), F32),
                   jax.ShapeDtypeStruct((R, POOL_DIM), F32), jax.ShapeDtypeStruct((R, D), MXU),
                   jax.ShapeDtypeStruct((R // SB, 8, D), F32)],
        compiler_params=_cp(48),
    )(dx1, mix, bm, wot)


def mlp_fwd(x1, bm, nw, w1, w2):
    R = x1.shape[0]
    nj = D_FF // FF_BLK

    def body(x1_ref, bm_ref, nw_ref, w1_ref, w2_ref, x2_ref, mo_ref, r_ref, h2_ref, acc_ref):
        j = pl.program_id(1)

        @pl.when(j == 0)
        def _():
            for s in range(TM // SB):
                rows = slice(s * SB, (s + 1) * SB)
                xhat, _ = _rms_hat(x1_ref[rows, :])
                h = xhat * nw_ref[...] * (1.0 + bm_ref[s, 4:5, :]) + bm_ref[s, 3:4, :]
                h2_ref[rows, :] = h.astype(h2_ref.dtype)
            acc_ref[...] = jnp.zeros_like(acc_ref)

        r = jnp.maximum(_dot(h2_ref[...], w1_ref[...]), 0.0)
        r_ref[...] = r.astype(r_ref.dtype)
        acc_ref[...] += _dot((r * r).astype(MXU), w2_ref[...])

        @pl.when(j == nj - 1)
        def _():
            mo_ref[...] = acc_ref[...]
            for s in range(TM // SB):
                rows = slice(s * SB, (s + 1) * SB)
                x2_ref[rows, :] = x1_ref[rows, :] + bm_ref[s, 5:6, :] * acc_ref[rows, :]

    return pl.pallas_call(
        body, name="mlp_fwd", grid=(R // TM, nj),
        in_specs=[pl.BlockSpec((TM, D), lambda i, j: (i, 0)),
                  pl.BlockSpec((TM // SB, 8, D), lambda i, j: (i, 0, 0)),
                  pl.BlockSpec((1, D), lambda i, j: (0, 0)),
                  pl.BlockSpec((D, FF_BLK), lambda i, j: (0, j)),
                  pl.BlockSpec((FF_BLK, D), lambda i, j: (j, 0))],
        out_specs=[pl.BlockSpec((TM, D), lambda i, j: (i, 0)), pl.BlockSpec((TM, D), lambda i, j: (i, 0)),
                   pl.BlockSpec((TM, FF_BLK), lambda i, j: (i, j)), pl.BlockSpec((TM, D), lambda i, j: (i, 0))],
        out_shape=[jax.ShapeDtypeStruct((R, D), F32), jax.ShapeDtypeStruct((R, D), F32),
                   jax.ShapeDtypeStruct((R, D_FF), BF16), jax.ShapeDtypeStruct((R, D), MXU)],
        scratch_shapes=[pltpu.VMEM((TM, D), F32)],
        compiler_params=_cp(48),
    )(x1, bm, nw, w1, w2)


def mlp_bwd(dx2, x1, mo, r, bm, nw, w2t, w1t):
    R = x1.shape[0]
    nj = D_FF // FF_BLK

    def body(dx2_ref, x1_ref, mo_ref, r_ref, bm_ref, nw_ref, w2t_ref, w1t_ref, dx1_ref, du_ref, dob_ref, part_ref,
             acc_ref):
        j = pl.program_id(1)

        @pl.when(j == 0)
        def _():
            for s in range(TM // SB):
                rows = slice(s * SB, (s + 1) * SB)
                dob_ref[rows, :] = (dx2_ref[rows, :] * bm_ref[s, 5:6, :]).astype(dob_ref.dtype)
            acc_ref[...] = jnp.zeros_like(acc_ref)

        du = _dot(dob_ref[...], w2t_ref[...]) * (2.0 * r_ref[...].astype(F32))
        du_ref[...] = du.astype(du_ref.dtype)
        acc_ref[...] += _dot(du_ref[...], w1t_ref[...])

        @pl.when(j == nj - 1)
        def _():
            w = nw_ref[...]
            for s in range(TM // SB):
                rows = slice(s * SB, (s + 1) * SB)
                xhat, rstd = _rms_hat(x1_ref[rows, :])
                dh = acc_ref[rows, :]
                dx, dnw = _rms_bwd(dh * (1.0 + bm_ref[s, 4:5, :]), xhat, rstd, w)
                d2 = dx2_ref[rows, :]
                dx1_ref[rows, :] = d2 + dx
                part_ref[s] = jnp.concatenate(
                    [_colsum(dh), _colsum(dh * xhat * w), _colsum(d2 * mo_ref[rows, :]), dnw,
                     jnp.zeros((4, D), F32)], axis=0)

    return pl.pallas_call(
        body, name="mlp_bwd", grid=(R // TM, nj),
        in_specs=[pl.BlockSpec((TM, D), lambda i, j: (i, 0)), pl.BlockSpec((TM, D), lambda i, j: (i, 0)),
                  pl.BlockSpec((TM, D), lambda i, j: (i, 0)), pl.BlockSpec((TM, FF_BLK), lambda i, j: (i, j)),
                  pl.BlockSpec((TM // SB, 8, D), lambda i, j: (i, 0, 0)),
                  pl.BlockSpec((1, D), lambda i, j: (0, 0)),
                  pl.BlockSpec((D, FF_BLK), lambda i, j: (0, j)),
                  pl.BlockSpec((FF_BLK, D), lambda i, j: (j, 0))],
        out_specs=[pl.BlockSpec((TM, D), lambda i, j: (i, 0)), pl.BlockSpec((TM, FF_BLK), lambda i, j: (i, j)),
                   pl.BlockSpec((TM, D), lambda i, j: (i, 0)),
                   pl.BlockSpec((TM // SB, 8, D), lambda i, j: (i, 0, 0))],
        out_shape=[jax.ShapeDtypeStruct((R, D), F32), jax.ShapeDtypeStruct((R, D_FF), MXU),
                   jax.ShapeDtypeStruct((R, D), MXU), jax.ShapeDtypeStruct((R // SB, 8, D), F32)],
        scratch_shapes=[pltpu.VMEM((TM, D), F32)],
        compiler_params=_cp(48),
    )(dx2, x1, mo, r, bm, nw, w2t, w1t)


def mm_tn(a, b, square_a=False, name="mm_tn", col_blocks=False):
    R, M = a.shape
    N = b.shape[1]
    tm = M if M <= 1408 else 1024
    tn = N if N <= 2176 else 1024
    tk = 512 if R % 512 == 0 else R
    assert not col_blocks or tm == M

    def body(a_ref, b_ref, o_ref):
        @pl.when(pl.program_id(2) == 0)
        def _():
            o_ref[...] = jnp.zeros_like(o_ref)

        av = a_ref[...]
        if square_a:
            av = av.astype(F32)
            av = (av * av).astype(MXU)
        prod = _dotg(av.astype(MXU), b_ref[...].astype(MXU), TN)
        if col_blocks:
            o_ref[0] += prod
        else:
            o_ref[...] += prod

    if col_blocks:
        out_spec = pl.BlockSpec((1, tm, tn), lambda i, j, k: (j, 0, 0))
        out_shape = jax.ShapeDtypeStruct((N // tn, M, tn), F32)
    else:
        out_spec = pl.BlockSpec((tm, tn), lambda i, j, k: (i, j))
        out_shape = jax.ShapeDtypeStruct((M, N), F32)
    return pl.pallas_call(
        body, name=name, grid=(M // tm, N // tn, R // tk),
        in_specs=[pl.BlockSpec((tk, tm), lambda i, j, k: (k, i)), pl.BlockSpec((tk, tn), lambda i, j, k: (k, j))],
        out_specs=out_spec, out_shape=out_shape,
        compiler_params=_cp(48),
    )(a, b)


def final_loss(x, tgt, fw, blocks_per_sample):
    R = x.shape[0]
    nxb = blocks_per_sample - 1

    def body(x_ref, t_ref, fw_ref, dx_ref, part_ref):
        i = pl.program_id(0)
        is_ctx = (i % blocks_per_sample) == 0
        xhat, rstd = _rms_hat(x_ref[...])
        w = fw_ref[...]
        err = xhat * w - t_ref[...]
        dx, dfw = _rms_bwd(err * (1.0 / D), xhat, rstd, w)
        keep = jnp.where(is_ctx, 0.0, 1.0)
        dx_ref[...] = dx * keep
        part_ref[0] = jnp.concatenate([dfw * keep, _colsum(err * err) * keep, jnp.zeros((6, D), F32)], axis=0)

    def tmap(i):
        return ((i // blocks_per_sample) * nxb + jnp.maximum(i % blocks_per_sample - 1, 0), 0)

    return pl.pallas_call(
        body, name="final_loss", grid=(R // SB,),
        in_specs=[_rowspec(D, SB), pl.BlockSpec((SB, D), tmap), _fullspec((1, D))],
        out_specs=[_rowspec(D, SB), pl.BlockSpec((1, 8, D), lambda i: (i, 0, 0))],
        out_shape=[jax.ShapeDtypeStruct((R, D), F32), jax.ShapeDtypeStruct((R // SB, 8, D), F32)],
    )(x, tgt, fw)


def _softplus(v):
    return jnp.maximum(v, 0.0) + jnp.log(1.0 + jnp.exp(-jnp.abs(v)))


def _conv_out(ext, cw_ref, cb_ref):
    return (cb_ref[...] + cw_ref[0:1, :] * _shift(ext, -1) + cw_ref[1:2, :] * _shift(ext, 0)
            + cw_ref[2:3, :] * _shift(ext, 1) + cw_ref[3:4, :] * _shift(ext, 2))


def _dt_dir(v, d):
    lane = lax.broadcasted_iota(jnp.int32, v.shape, 1)
    return jnp.where(lane < SSD_HEADS, pltpu.roll(v, (128 - DT0 - SSD_HEADS * d) % 128, axis=1), 0.0)


def ssd_prep(pxbc, plast, cw, cb, dtb, blocks_per_sample):
    R = pxbc.shape[0]
    prev, nxt = _halo_specs(XBC, R)

    def body(cur_ref, prev_ref, nxt_ref, pl_ref, cw_ref, cb_ref, dtb_ref, xs_ref, bm_ref, cm_ref, dt_ref):
        i = pl.program_id(0)
        ext = _ext_rows(cur_ref[...], prev_ref[...], nxt_ref[...], i, blocks_per_sample)
        co = _conv_out(ext, cw_ref, cb_ref)
        a = co * _sigmoid(co)
        xs_ref[...] = a[:, 0:384]
        bm_ref[...] = a[:, 384:640]
        cm_ref[...] = a[:, 640:896]
        sp = _softplus(pl_ref[...] + dtb_ref[...])
        dt_ref[0] = _dt_dir(sp, 0)
        dt_ref[1] = _dt_dir(sp, 1)

    return pl.pallas_call(
        body, name="ssd_prep", grid=(R // SB,),
        in_specs=[_rowspec(XBC, SB), prev, nxt, _rowspec(128, SB), _fullspec((8, XBC)), _fullspec((1, XBC)),
                  _fullspec((1, 128))],
        out_specs=[_rowspec(384, SB), _rowspec(256, SB), _rowspec(256, SB),
                   pl.BlockSpec((2, SB, 128), lambda i: (0, i, 0))],
        out_shape=[jax.ShapeDtypeStruct((R, 384), F32), jax.ShapeDtypeStruct((R, 256), F32),
                   jax.ShapeDtypeStruct((R, 256), F32), jax.ShapeDtypeStruct((2, R, 128), F32)],
    )(pxbc, pxbc, pxbc, plast, cw, cb, dtb)


def _chunk_index(d, s, nc):
    nctx = CTX // CHUNK
    back = jnp.where(s < nctx, nctx - 1 - s, nc + nctx - 1 - s)
    return jnp.where(d == 0, s, back)


def _scan_common(d, dt, arow, eexp, xs):
    ii = lax.broadcasted_iota(jnp.int32, (CHUNK, CHUNK), 0)
    jj = lax.broadcasted_iota(jnp.int32, (CHUNK, CHUNK), 1)
    mask = ((ii - jj) * (1 - 2 * d)) >= 0
    adt = dt * arow
    tmat = jnp.where(mask, 1.0, 0.0)
    cs = _dot_hi(tmat, adt, sel_first=True)
    tot = _colsum(adt)
    dtx = _dot_hi(dt, eexp)
    xt = xs * dtx
    ecs = jnp.exp(cs)
    ecx = _dot_hi(ecs, eexp)
    dte = jnp.exp(tot - cs)
    dtex = _dot_hi(dte, eexp)
    etot = jnp.exp(tot)
    etx = _dot_hi(jnp.broadcast_to(etot, (8, 128)), eexp)[0:1, :]
    return mask, tmat, adt, cs, tot, dtx, xt, ecs, ecx, dte, dtex, etot, etx


def _decay_matrix(mask, cs, cst, h):
    return jnp.exp(jnp.where(mask, cs[:, h:h + 1] - cst[h:h + 1, :], -1e30))


def ssd_scan_fwd(xs, bm, cm, dtv, arow, eexp, nb, T):
    R = xs.shape[0]
    nc = T // CHUNK

    def rowmap(b, d, s):
        return (b * nc + _chunk_index(d, s, nc), 0)

    def body(xs_ref, bm_ref, cm_ref, dt_ref, a_ref, e_ref, y_ref, hin_ref, st_ref):
        d = pl.program_id(1)
        s = pl.program_id(2)

        @pl.when(s == 0)
        def _():
            st_ref[...] = jnp.zeros_like(st_ref)

        eexp = e_ref[...]
        mask, _, _, cs, _, _, xt, _, ecx, _, dtex, _, etx = _scan_common(
            d, dt_ref[0], a_ref[0, 0:1, :], eexp, xs_ref[...])
        cst = cs.T
        sin = st_ref[...]
        hin_ref[0] = sin
        sb = sin.astype(MXU)
        xtb = xt.astype(MXU)
        xw = (xt * dtex).astype(MXU)
        g0 = lax.broadcasted_iota(jnp.int32, (CHUNK, SSD_INNER), 1) < 192
        lane = lax.broadcasted_iota(jnp.int32, (CHUNK, 128), 1)
        c = [cm_ref[:, 0:128].astype(MXU), cm_ref[:, 128:256].astype(MXU)]
        b = [bm_ref[:, 0:128].astype(MXU), bm_ref[:, 128:256].astype(MXU)]
        y = jnp.where(g0, _dot(c[0], sb), _dot(c[1], sb)) * ecx
        cb = [_dotg(c[0], b[0], NT), _dotg(c[1], b[1], NT)]
        blocks = []
        for blk in range(3):
            acc = None
            for hh in range(2):
                h = blk * 2 + hh
                m = (cb[h // 3] * _decay_matrix(mask, cs, cst, h)).astype(MXU)
                res = _dot(m, xtb[:, blk * 128:(blk + 1) * 128])
                acc = res if hh == 0 else jnp.where(lane < 64, acc, res)
            blocks.append(acc)
        y_ref[0] = y + jnp.concatenate(blocks, axis=1)
        st_ref[...] = sin * etx + jnp.where(g0, _dotg(b[0], xw, TN), _dotg(b[1], xw, TN))

    return pl.pallas_call(
        body, name="ssd_scan_fwd", grid=(nb, 2, nc),
        in_specs=[pl.BlockSpec((CHUNK, 384), rowmap), pl.BlockSpec((CHUNK, 256), rowmap),
                  pl.BlockSpec((CHUNK, 256), rowmap),
                  pl.BlockSpec((1, CHUNK, 128), lambda b, d, s: (d, b * nc + _chunk_index(d, s, nc), 0)),
                  pl.BlockSpec((1, 8, 128), lambda b, d, s: (d, 0, 0)),
                  pl.BlockSpec((128, 384), lambda b, d, s: (0, 0))],
        out_specs=[pl.BlockSpec((1, CHUNK, 384), lambda b, d, s: (d, b * nc + _chunk_index(d, s, nc), 0)),
                   pl.BlockSpec((1, CHUNK, 384), lambda b, d, s: ((b * 2 + d) * nc + _chunk_index(d, s, nc), 0, 0))],
        out_shape=[jax.ShapeDtypeStruct((2, R, 384), F32), jax.ShapeDtypeStruct((nb * 2 * nc, CHUNK, 384), F32)],
        scratch_shapes=[pltpu.VMEM((CHUNK, 384), F32)],
    )(xs, bm, cm, dtv, arow, eexp)


def ssd_scan_bwd(xs, bm, cm, dtv, arow, eexp, hin, dy, nb, T):
    R = xs.shape[0]
    nc = T // CHUNK

    def chunk(d, s):
        return _chunk_index(d, nc - 1 - s, nc)

    def rowmap(b, d, s):
        return (b * nc + chunk(d, s), 0)

    def dirmap(b, d, s):
        return (d, b * nc + chunk(d, s), 0)

    def body(xs_ref, bm_ref, cm_ref, dt_ref, a_ref, e_ref, hin_ref, dy_ref,
             dxs_ref, dbm_ref, dcm_ref, ddt_ref, da_ref, ds_ref):
        d = pl.program_id(1)
        s = pl.program_id(2)

        @pl.when(s == 0)
        def _():
            ds_ref[...] = jnp.zeros_like(ds_ref)
            da_ref[...] = jnp.zeros_like(da_ref)

        eexp = e_ref[...]
        dt = dt_ref[0]
        arow = a_ref[0, 0:1, :]
        xs_v = xs_ref[...]
        mask, tmat, adt, cs, tot, dtx, xt, ecs, ecx, dte, dtex, etot, etx = _scan_common(d, dt, arow, eexp, xs_v)
        cst = cs.T
        sin = hin_ref[0]
        sb = sin.astype(MXU)
        dsp = ds_ref[...]
        dyv = dy_ref[...]
        xtb = xt.astype(MXU)
        xw = (xt * dtex).astype(MXU)
        g0 = lax.broadcasted_iota(jnp.int32, (CHUNK, SSD_INNER), 1) < 192
        lane = lax.broadcasted_iota(jnp.int32, (CHUNK, 128), 1)
        sub = lax.broadcasted_iota(jnp.int32, (CHUNK, 128), 0)
        c = [cm_ref[:, 0:128].astype(MXU), cm_ref[:, 128:256].astype(MXU)]
        b = [bm_ref[:, 0:128].astype(MXU), bm_ref[:, 128:256].astype(MXU)]

        cs_prod = jnp.where(g0, _dot(c[0], sb), _dot(c[1], sb))
        dcsp = dyv * ecx
        dcsp_g = [jnp.where(g0, dcsp, 0.0).astype(MXU), jnp.where(g0, 0.0, dcsp).astype(MXU)]
        dcs = _dot_hi(dyv * cs_prod, eexp, NT) * ecs
        dc = [_dotg(dcsp_g[0], sb, NT), _dotg(dcsp_g[1], sb, NT)]
        dsin = _dotg(c[0], dcsp_g[0], TN) + _dotg(c[1], dcsp_g[1], TN) + dsp * etx

        detx = _colsum(dsp * sin)
        dtot = _dot_hi(jnp.broadcast_to(detx, (8, SSD_INNER)), eexp, NT)[0:1, :] * etot
        dsp_g = [jnp.where(g0, dsp, 0.0).astype(MXU), jnp.where(g0, 0.0, dsp).astype(MXU)]
        dxw = _dot(b[0], dsp_g[0]) + _dot(b[1], dsp_g[1])
        db = [_dotg(xw, dsp_g[0], NT), _dotg(xw, dsp_g[1], NT)]
        dxt = dxw * dtex
        ddte = _dot_hi(dxw * xt, eexp, NT) * dte
        dtot = dtot + _colsum(ddte)
        dcs = dcs - ddte

        cb = [_dotg(c[0], b[0], NT), _dotg(c[1], b[1], NT)]
        dg = [jnp.zeros((CHUNK, CHUNK), F32), jnp.zeros((CHUNK, CHUNK), F32)]
        dcs_rows = jnp.zeros((CHUNK, 128), F32)
        dxt_blocks = []
        for blk in range(3):
            acc = jnp.zeros((CHUNK, 128), F32)
            for hh in range(2):
                h = blk * 2 + hh
                g = h // 3
                mine = (lane < 64) if hh == 0 else (lane >= 64)
                dyh = jnp.where(mine, dyv[:, blk * 128:(blk + 1) * 128], 0.0).astype(MXU)
                lh = _decay_matrix(mask, cs, cst, h)
                m = cb[g] * lh
                dm = _dotg(dyh, xtb[:, blk * 128:(blk + 1) * 128], NT)
                acc = acc + _dotg(m.astype(MXU), dyh, TN)
                dg[g] = dg[g] + dm * lh
                q = dm * m
                dcs = dcs + jnp.where(lane == h, jnp.sum(q, axis=1, keepdims=True), 0.0)
                dcs_rows = dcs_rows - jnp.where(sub == h, jnp.sum(q, axis=0, keepdims=True), 0.0)
            dxt_blocks.append(acc)
        dxt = dxt + jnp.concatenate(dxt_blocks, axis=1)
        for g in range(2):
            dgb = dg[g].astype(MXU)
            dc[g] = dc[g] + _dot(dgb, b[g])
            db[g] = db[g] + _dotg(dgb, c[g], TN)
        dcs = dcs + dcs_rows.T

        dadt = _dot_hi(tmat, dcs, TN, sel_first=True) + dtot
        ddt = dadt * arow + _dot_hi(dxt * xs_v, eexp, NT)
        da_ref[0, 0:1, :] += _colsum(dadt * dt)
        dxs_ref[0] = dxt * dtx
        dbm_ref[0] = jnp.concatenate(db, axis=1)
        dcm_ref[0] = jnp.concatenate(dc, axis=1)
        ddt_ref[0] = ddt
        ds_ref[...] = dsin

    return pl.pallas_call(
        body, name="ssd_scan_bwd", grid=(nb, 2, nc),
        in_specs=[pl.BlockSpec((CHUNK, 384), rowmap), pl.BlockSpec((CHUNK, 256), rowmap),
                  pl.BlockSpec((CHUNK, 256), rowmap), pl.BlockSpec((1, CHUNK, 128), dirmap),
                  pl.BlockSpec((1, 8, 128), lambda b, d, s: (d, 0, 0)),
                  pl.BlockSpec((128, 384), lambda b, d, s: (0, 0)),
                  pl.BlockSpec((1, CHUNK, 384), lambda b, d, s: ((b * 2 + d) * nc + chunk(d, s), 0, 0)),
                  pl.BlockSpec((CHUNK, 384), rowmap)],
        out_specs=[pl.BlockSpec((1, CHUNK, 384), dirmap), pl.BlockSpec((1, CHUNK, 256), dirmap),
                   pl.BlockSpec((1, CHUNK, 256), dirmap), pl.BlockSpec((1, CHUNK, 128), dirmap),
                   pl.BlockSpec((1, 8, 128), lambda b, d, s: (b * 2 + d, 0, 0))],
        out_shape=[jax.ShapeDtypeStruct((2, R, 384), F32), jax.ShapeDtypeStruct((2, R, 256), F32),
                   jax.ShapeDtypeStruct((2, R, 256), F32), jax.ShapeDtypeStruct((2, R, 128), F32),
                   jax.ShapeDtypeStruct((nb * 2, 8, 128), F32)],
        scratch_shapes=[pltpu.VMEM((CHUNK, 384), F32)],
    )(xs, bm, cm, dtv, arow, eexp, hin, dy)


def _group_rms(g):
    lane = lax.broadcasted_iota(jnp.int32, g.shape, 1)
    g0 = lane < 192
    gg = g * g
    s0 = jnp.sum(jnp.where(g0, gg, 0.0), axis=-1, keepdims=True)
    s1 = jnp.sum(gg, axis=-1, keepdims=True) - s0
    rstd = jnp.where(g0, lax.rsqrt(s0 * (1.0 / 192) + EPS), lax.rsqrt(s1 * (1.0 / 192) + EPS))
    return rstd, g0


def ssd_out_fwd(y2, xs, pz, dexp, nw):
    R = xs.shape[0]

    def body(y_ref, xs_ref, z_ref, d_ref, nw_ref, o_ref):
        z = z_ref[...]
        yy = y_ref[0] + y_ref[1] + xs_ref[...] * d_ref[...]
        g = yy * (z * _sigmoid(z))
        rstd, _ = _group_rms(g)
        o_ref[...] = g * rstd * nw_ref[...]

    return pl.pallas_call(
        body, name="ssd_out_fwd", grid=(R // TM,),
        in_specs=[pl.BlockSpec((2, TM, 384), lambda i: (0, i, 0)), _rowspec(384), _rowspec(384),
                  _fullspec((1, 384)), _fullspec((1, 384))],
        out_specs=_rowspec(384),
        out_shape=jax.ShapeDtypeStruct((R, 384), F32),
    )(y2, xs, pz, dexp, nw)


def ssd_out_bwd(dout, y2, xs, pz, dexp, nw):
    R = xs.shape[0]

    def body(do_ref, y_ref, xs_ref, z_ref, d_ref, nw_ref, dy_ref, dz_ref, dxs_ref, part_ref):
        z = z_ref[...]
        xs_v = xs_ref[...]
        yy = y_ref[0] + y_ref[1] + xs_v * d_ref[...]
        sig = _sigmoid(z)
        sz = z * sig
        g = yy * sz
        rstd, g0 = _group_rms(g)
        ghat = g * rstd
        do = do_ref[...]
        dgn = do * nw_ref[...]
        t = dgn * ghat
        t0 = jnp.sum(jnp.where(g0, t, 0.0), axis=-1, keepdims=True)
        t1 = jnp.sum(t, axis=-1, keepdims=True) - t0
        dg = rstd * (dgn - ghat * jnp.where(g0, t0, t1) * (1.0 / 192))
        dyy = dg * sz
        dy_ref[...] = dyy
        dz_ref[...] = dg * yy * (sig * (1.0 + z * (1.0 - sig)))
        dxs_ref[...] = dyy * d_ref[...]
        part_ref[0] = jnp.concatenate([_colsum(do * ghat), _colsum(dyy * xs_v), jnp.zeros((6, 384), F32)], axis=0)

    return pl.pallas_call(
        body, name="ssd_out_bwd", grid=(R // TM,),
        in_specs=[_rowspec(384), pl.BlockSpec((2, TM, 384), lambda i: (0, i, 0)), _rowspec(384), _rowspec(384),
                  _fullspec((1, 384)), _fullspec((1, 384))],
        out_specs=[_rowspec(384), _rowspec(384), _rowspec(384), pl.BlockSpec((1, 8, 384), lambda i: (i, 0, 0))],
        out_shape=[jax.ShapeDtypeStruct((R, 384), F32), jax.ShapeDtypeStruct((R, 384), F32),
                   jax.ShapeDtypeStruct((R, 384), F32), jax.ShapeDtypeStruct((R // TM, 8, 384), F32)],
    )(dout, y2, xs, pz, dexp, nw)


def ssd_prep_bwd_a(pxbc, plast, cw, cb, dtb, dxs_skip, dxs2, dbm2, dcm2, ddt2, blocks_per_sample):
    R = pxbc.shape[0]
    prev, nxt = _halo_specs(XBC, R)

    def body(cur_ref, prev_ref, nxt_ref, pl_ref, cw_ref, cb_ref, dtb_ref, dsk_ref, dxs_ref, dbm_ref, dcm_ref, ddt_ref,
             dpre_ref, dlast_ref, part_ref):
        i = pl.program_id(0)
        ext = _ext_rows(cur_ref[...], prev_ref[...], nxt_ref[...], i, blocks_per_sample)
        co = _conv_out(ext, cw_ref, cb_ref)
        sig = _sigmoid(co)
        up = jnp.concatenate([dsk_ref[...] + dxs_ref[0] + dxs_ref[1], dbm_ref[0] + dbm_ref[1],
                              dcm_ref[0] + dcm_ref[1]], axis=1)
        dpre = up * (sig * (1.0 + co * (1.0 - sig)))
        dpre_ref[...] = dpre
        raw = pl_ref[...] + dtb_ref[...]
        lane = lax.broadcasted_iota(jnp.int32, raw.shape, 1)
        ddt = (pltpu.roll(ddt_ref[0], DT0, axis=1) + pltpu.roll(ddt_ref[1], DT0 + SSD_HEADS, axis=1))
        ddt = jnp.where(jnp.logical_and(lane >= DT0, lane < DT0 + 2 * SSD_HEADS), ddt * _sigmoid(raw), 0.0)
        dlast_ref[...] = ddt
        rows = [_colsum(dpre * _shift(ext, k - 1)) for k in range(4)]
        rows.append(_colsum(dpre))
        rows.append(jnp.concatenate([_colsum(ddt), jnp.zeros((1, XBC - 128), F32)], axis=1))
        rows.append(jnp.zeros((2, XBC), F32))
        part_ref[0] = jnp.concatenate(rows, axis=0)

    dirspec = lambda n: pl.BlockSpec((2, SB, n), lambda i: (0, i, 0))
    return pl.pallas_call(
        body, name="ssd_prep_bwd_a", grid=(R // SB,),
        in_specs=[_rowspec(XBC, SB), prev, nxt, _rowspec(128, SB), _fullspec((8, XBC)), _fullspec((1, XBC)),
                  _fullspec((1, 128)), _rowspec(384, SB), dirspec(384), dirspec(256), dirspec(256), dirspec(128)],
        out_specs=[_rowspec(XBC, SB), _rowspec(128, SB), pl.BlockSpec((1, 8, XBC), lambda i: (i, 0, 0))],
        out_shape=[jax.ShapeDtypeStruct((R, XBC), F32), jax.ShapeDtypeStruct((R, 128), F32),
                   jax.ShapeDtypeStruct((R // SB, 8, XBC), F32)],
    )(pxbc, pxbc, pxbc, plast, cw, cb, dtb, dxs_skip, dxs2, dbm2, dcm2, ddt2)


def ssd_prep_bwd_b(dpre, cw, blocks_per_sample):
    R = dpre.shape[0]
    prev, nxt = _halo_specs(XBC, R)

    def body(cur_ref, prev_ref, nxt_ref, cw_ref, o_ref):
        i = pl.program_id(0)
        ext = _ext_rows(cur_ref[...], prev_ref[...], nxt_ref[...], i, blocks_per_sample)
        o_ref[...] = (cw_ref[0:1, :] * _shift(ext, 1) + cw_ref[1:2, :] * _shift(ext, 0)
                      + cw_ref[2:3, :] * _shift(ext, -1) + cw_ref[3:4, :] * _shift(ext, -2))

    return pl.pallas_call(
        body, name="ssd_prep_bwd_b", grid=(R // SB,),
        in_specs=[_rowspec(XBC, SB), prev, nxt, _fullspec((8, XBC))],
        out_specs=_rowspec(XBC, SB),
        out_shape=jax.ShapeDtypeStruct((R, XBC), F32),
    )(dpre, dpre, dpre, cw)


def _rope(u, cos, sa, sb):
    return u * cos + pltpu.roll(u, 120, axis=1) * sa + pltpu.roll(u, 8, axis=1) * sb


def _rope_t(du, cos, sa, sb):
    return du * cos + pltpu.roll(du * sa, 8, axis=1) + pltpu.roll(du * sb, 120, axis=1)


def mla_prep(pqa, pkva, plast, qnw, kvnw, wq, wk, wv, cos, sa, sb):
    R = pqa.shape[0]

    def body(qa_ref, kva_ref, pl_ref, qnw_ref, kvnw_ref, wq_ref, wk_ref, wv_ref, cos_ref, sa_ref, sb_ref,
             q_ref, k_ref, v_ref, cq_ref, ckv_ref):
        cos_v, sa_v, sb_v = cos_ref[...], sa_ref[...], sb_ref[...]
        xq, _ = _rms_hat(qa_ref[...])
        cq_ref[...] = (xq * qnw_ref[...]).astype(cq_ref.dtype)
        xkv, _ = _rms_hat(kva_ref[...])
        ckv_ref[...] = (xkv * kvnw_ref[...]).astype(ckv_ref.dtype)
        q = _dot(cq_ref[...], wq_ref[...])
        kn = _dot(ckv_ref[...], wk_ref[...])
        v_ref[...] = _dot(ckv_ref[...], wv_ref[...]).astype(v_ref.dtype)
        lane = lax.broadcasted_iota(jnp.int32, (TM, HP), 1)
        rope_lanes = jnp.logical_and(lane >= QK_NOPE, lane < QK_DIM)
        kr = _rope(jnp.where(rope_lanes, pltpu.roll(pl_ref[...], QK_NOPE, axis=1), 0.0), cos_v, sa_v, sb_v)
        for h in range(MLA_HEADS):
            cols = slice(h * HP, (h + 1) * HP)
            q_ref[:, cols] = (_rope(q[:, cols], cos_v, sa_v, sb_v) * Q_SCALE).astype(q_ref.dtype)
            k_ref[:, cols] = (kn[:, cols] + kr).astype(k_ref.dtype)

    return pl.pallas_call(
        body, name="mla_prep", grid=(R // TM,),
        in_specs=[_rowspec(256), _rowspec(256), _rowspec(128), _fullspec((1, 256)), _fullspec((1, 256)),
                  _fullspec((256, QW)), _fullspec((256, QW)), _fullspec((256, QW)),
                  _rowspec(HP), _rowspec(HP), _rowspec(HP)],
        out_specs=[_rowspec(QW), _rowspec(QW), _rowspec(QW), _rowspec(256), _rowspec(256)],
        out_shape=[jax.ShapeDtypeStruct((R, QW), MXU)] * 3 + [jax.ShapeDtypeStruct((R, 256), MXU)] * 2,
    )(pqa, pkva, plast, qnw, kvnw, wq, wk, wv, cos, sa, sb)


def mla_prep_bwd(dq, dk, dv, pqa, pkva, qnw, kvnw, wqt, wkt, wvt, cos, sa, sb):
    R = pqa.shape[0]

    def body(dq_ref, dk_ref, dv_ref, qa_ref, kva_ref, qnw_ref, kvnw_ref, wqt_ref, wkt_ref, wvt_ref,
             cos_ref, sa_ref, sb_ref, dqa_ref, dkva_ref, dkr_ref, dql_ref, dkm_ref, dvb_ref, part_ref):
        cos_v, sa_v, sb_v = cos_ref[...], sa_ref[...], sb_ref[...]
        lane = lax.broadcasted_iota(jnp.int32, (TM, HP), 1)
        rope_lanes = jnp.logical_and(lane >= QK_NOPE, lane < QK_DIM)
        dkr = jnp.zeros((TM, HP), F32)
        for h in range(MLA_HEADS):
            cols = slice(h * HP, (h + 1) * HP)
            dql_ref[:, cols] = (_rope_t(dq_ref[:, cols], cos_v, sa_v, sb_v) * ATT_SCALE).astype(dql_ref.dtype)
            dkh = dk_ref[:, cols] * LN2
            dkm_ref[:, cols] = jnp.where(lane < QK_NOPE, dkh, 0.0).astype(dkm_ref.dtype)
            dkr = dkr + jnp.where(rope_lanes, dkh, 0.0)
        dvb_ref[...] = dv_ref[...].astype(dvb_ref.dtype)
        dkr = jnp.where(rope_lanes, _rope_t(dkr, cos_v, sa_v, sb_v), 0.0)
        dkr_ref[...] = pltpu.roll(dkr, HP - QK_NOPE, axis=1)
        xq, rq = _rms_hat(qa_ref[...])
        dqa, dqnw = _rms_bwd(_dot(dql_ref[...], wqt_ref[...]), xq, rq, qnw_ref[...])
        dqa_ref[...] = dqa
        xkv, rkv = _rms_hat(kva_ref[...])
        dckv = _dot(dkm_ref[...], wkt_ref[...]) + _dot(dvb_ref[...], wvt_ref[...])
        dkva, dkvnw = _rms_bwd(dckv, xkv, rkv, kvnw_ref[...])
        dkva_ref[...] = dkva
        part_ref[0] = jnp.concatenate([dqnw, dkvnw, jnp.zeros((6, 256), F32)], axis=0)

    return pl.pallas_call(
        body, name="mla_prep_bwd", grid=(R // TM,),
        in_specs=[_rowspec(QW), _rowspec(QW), _rowspec(QW), _rowspec(256), _rowspec(256), _fullspec((1, 256)),
                  _fullspec((1, 256)), _fullspec((QW, 256)), _fullspec((QW, 256)), _fullspec((QW, 256)),
                  _rowspec(HP), _rowspec(HP), _rowspec(HP)],
        out_specs=[_rowspec(256), _rowspec(256), _rowspec(128), _rowspec(QW), _rowspec(QW), _rowspec(QW),
                   pl.BlockSpec((1, 8, 256), lambda i: (i, 0, 0))],
        out_shape=[jax.ShapeDtypeStruct((R, 256), F32), jax.ShapeDtypeStruct((R, 256), F32),
                   jax.ShapeDtypeStruct((R, 128), F32)] + [jax.ShapeDtypeStruct((R, QW), MXU)] * 3
                  + [jax.ShapeDtypeStruct((R // TM, 8, 256), F32)],
    )(dq, dk, dv, pqa, pkva, qnw, kvnw, wqt, wkt, wvt, cos, sa, sb)


ATT_SCALE = QK_DIM ** -0.5
TQ = 256


LOG2E = 1.4426950408889634
LN2 = 0.6931471805599453
Q_SCALE = ATT_SCALE * LOG2E


def attn_fwd(q, k, v, nb, T):
    R = q.shape[0]
    nq = T // TQ
    kc = T // 2

    def body(q_ref, k_ref, v_ref, o_ref, lse_ref):
        def part(lo, n):
            s = _dotg(q_ref[...], k_ref[lo:lo + n, :], NT)
            m = jnp.max(s, axis=-1, keepdims=True)
            p = jnp.exp2(s - m)
            return m, jnp.sum(p, axis=-1, keepdims=True), _dot(p.astype(MXU), v_ref[lo:lo + n, :])

        def finish(parts):
            m = parts[0][0]
            for pm, _, _ in parts[1:]:
                m = jnp.maximum(m, pm)
            l, o = 0.0, 0.0
            for pm, pl_, po in parts:
                a = jnp.exp2(pm - m)
                l = l + a * pl_
                o = o + a * po
            o_ref[...] = o / l
            lse_ref[...] = jnp.broadcast_to(m + jnp.log(l) * LOG2E, (TQ, HP))

        i = pl.program_id(2)
        pl.when(i == 0)(lambda: finish([part(0, CTX)]))
        pl.when(i > 0)(lambda: finish([part(0, kc), part(kc, kc)]))

    qspec = pl.BlockSpec((TQ, HP), lambda b, h, i: (b * nq + i, h))
    kspec = pl.BlockSpec((T, HP), lambda b, h, i: (b, h))
    return pl.pallas_call(
        body, name="attn_fwd", grid=(nb, MLA_HEADS, nq),
        in_specs=[qspec, kspec, kspec], out_specs=[qspec, qspec],
        out_shape=[jax.ShapeDtypeStruct((R, QW), F32)] * 2,
        compiler_params=_cp(48),
    )(q, k, v)


def attn_bwd(q, k, v, o, lse, do, nb, T):
    R = q.shape[0]
    nq = T // TQ
    kc = T // 2

    def body(q_ref, k_ref, v_ref, o_ref, lse_ref, do_ref, dq_ref, dk_ref, dv_ref):
        i = pl.program_id(2)

        @pl.when(i == 0)
        def _():
            dk_ref[...] = jnp.zeros_like(dk_ref)
            dv_ref[...] = jnp.zeros_like(dv_ref)

        def run(chunks):
            qv = q_ref[...]
            dov = do_ref[...]
            dob = dov.astype(MXU)
            delta = jnp.sum(dov * o_ref[...], axis=-1, keepdims=True)
            lse_v = lse_ref[:, 0:1]
            dq = 0.0
            for lo, n in chunks:
                kv = k_ref[lo:lo + n, :]
                p = jnp.exp2(_dotg(qv, kv, NT) - lse_v)
                dp = _dotg(dob, v_ref[lo:lo + n, :], NT)
                dsb = (p * (dp - delta)).astype(MXU)
                dq = dq + _dot(dsb, kv)
                dk_ref[lo:lo + n, :] += _dotg(dsb, qv, TN)
                dv_ref[lo:lo + n, :] += _dotg(p.astype(MXU), dob, TN)
            dq_ref[...] = dq

        pl.when(i == 0)(lambda: run([(0, CTX)]))
        pl.when(i > 0)(lambda: run([(0, kc), (kc, kc)]))

    qspec = pl.BlockSpec((TQ, HP), lambda b, h, i: (b * nq + i, h))
    kspec = pl.BlockSpec((T, HP), lambda b, h, i: (b, h))
    return pl.pallas_call(
        body, name="attn_bwd", grid=(nb, MLA_HEADS, nq),
        in_specs=[qspec, kspec, kspec, qspec, qspec, qspec],
        out_specs=[qspec, kspec, kspec],
        out_shape=[jax.ShapeDtypeStruct((R, QW), F32)] * 3,
        compiler_params=_cp(56),
    )(q, k, v, o, lse, do)


def _pool_geometry(i, blocks_per_sample, seq):
    j = i % blocks_per_sample
    n = jnp.where(j == 0, CTX, seq)
    t0 = jnp.where(j == 0, 0, (j - 1) * SB) - HALO
    lane = lax.broadcasted_iota(jnp.int32, (SB + 2 * HALO, POOL_DIM), 1)
    t = lax.broadcasted_iota(jnp.int32, (SB + 2 * HALO, POOL_DIM), 0) + t0
    wh = jnp.where(lane < 64, 1, jnp.where(lane < 128, 2, jnp.where(lane < 192, 4, 8)))
    cnt = jnp.minimum(t + wh, n) - jnp.maximum(t - wh, 0)
    return lane, 1.0 / jnp.maximum(cnt, 1).astype(F32)


def _by_window(lane, c2, c4, c8, c16):
    return jnp.where(lane < 64, c2, jnp.where(lane < 128, c4, jnp.where(lane < 192, c8, c16)))


def _window_sums(ext, lane, first):
    n = ext.shape[0]
    r = lambda a, s: pltpu.roll(a, s % n, axis=0)
    c2 = ext + r(ext, first)
    c4 = r(c2, 1) + r(c2, -1)
    c8 = r(c4, 2) + r(c4, -2)
    c16 = r(c8, 4) + r(c8, -4)
    return _by_window(lane, c2, c4, c8, c16)


def _pool_delta(ext, lane, inv):
    return (_window_sums(ext, lane, 1) * inv - ext)[HALO:HALO + SB, :]


def pool_fwd(ppool, wbd, scale, blocks_per_sample, seq):
    R = ppool.shape[0]
    prev, nxt = _halo_specs(POOL_DIM, R)

    def body(cur_ref, prev_ref, nxt_ref, w_ref, s_ref, o_ref):
        i = pl.program_id(0)
        ext = _ext_rows(cur_ref[...], prev_ref[...], nxt_ref[...], i, blocks_per_sample)
        lane, inv = _pool_geometry(i, blocks_per_sample, seq)
        dlt = _pool_delta(ext, lane, inv)
        o_ref[...] = _dot(dlt.astype(MXU), w_ref[...]) * s_ref[...]

    return pl.pallas_call(
        body, name="pool_fwd", grid=(R // SB,),
        in_specs=[_rowspec(POOL_DIM, SB), prev, nxt, _fullspec((POOL_DIM, POOL_DIM)), _fullspec((1, POOL_DIM))],
        out_specs=_rowspec(POOL_DIM, SB),
        out_shape=jax.ShapeDtypeStruct((R, POOL_DIM), F32),
    )(ppool, ppool, ppool, wbd, scale)


def pool_bwd(ppool, dpool, wbd, wbdt, scale, blocks_per_sample, seq):
    R = ppool.shape[0]
    prev, nxt = _halo_specs(POOL_DIM, R)

    def body(cur_ref, prev_ref, nxt_ref, dcur_ref, dprev_ref, dnxt_ref, w_ref, wt_ref, s_ref, du_ref, dw_ref, part_ref):
        i = pl.program_id(0)

        @pl.when(i == 0)
        def _():
            dw_ref[...] = jnp.zeros_like(dw_ref)

        ext = _ext_rows(cur_ref[...], prev_ref[...], nxt_ref[...], i, blocks_per_sample)
        lane, inv = _pool_geometry(i, blocks_per_sample, seq)
        dlt = _pool_delta(ext, lane, inv).astype(MXU)
        dy = dcur_ref[...]
        part_ref[0] = jnp.concatenate([_colsum(dy * _dot(dlt, w_ref[...])), jnp.zeros((7, POOL_DIM), F32)], axis=0)
        dyp = (dy * s_ref[...]).astype(MXU)
        dw_ref[...] += _dotg(dlt, dyp, TN)
        dext = _ext_rows(dy, dprev_ref[...], dnxt_ref[...], i, blocks_per_sample)
        dd = _dot((dext * s_ref[...]).astype(MXU), wt_ref[...])
        du_ref[...] = (_window_sums(dd * inv, lane, -1) - dd)[HALO:HALO + SB, :]

    return pl.pallas_call(
        body, name="pool_bwd", grid=(R // SB,),
        in_specs=[_rowspec(POOL_DIM, SB), prev, nxt, _rowspec(POOL_DIM, SB), prev, nxt,
                  _fullspec((POOL_DIM, POOL_DIM)), _fullspec((POOL_DIM, POOL_DIM)), _fullspec((1, POOL_DIM))],
        out_specs=[_rowspec(POOL_DIM, SB), _fullspec((POOL_DIM, POOL_DIM)),
                   pl.BlockSpec((1, 8, POOL_DIM), lambda i: (i, 0, 0))],
        out_shape=[jax.ShapeDtypeStruct((R, POOL_DIM), F32), jax.ShapeDtypeStruct((POOL_DIM, POOL_DIM), F32),
                   jax.ShapeDtypeStruct((R // SB, 8, POOL_DIM), F32)],
    )(ppool, ppool, ppool, dpool, dpool, dpool, wbd, wbdt, scale)


def adamw(w, g, m, v, name="adamw"):
    rows, cols = w.shape
    tr = rows
    for cand in (512, 256, 128, 64, 32, 16, 8):
        if rows % cand == 0:
            tr = cand
            break
    bc1 = 1.0 - ADAM_B1 ** ADAM_STEP
    bc2 = 1.0 - ADAM_B2 ** ADAM_STEP

    def body(w_ref, g_ref, m_ref, v_ref, d_ref, nm_ref, nv_ref):
        g_v = g_ref[...]
        nm = ADAM_B1 * m_ref[...] + (1.0 - ADAM_B1) * g_v
        nv = ADAM_B2 * v_ref[...] + (1.0 - ADAM_B2) * (g_v * g_v)
        nm_ref[...] = nm
        nv_ref[...] = nv
        d_ref[...] = -ADAM_LR * ((nm / bc1) / (jnp.sqrt(nv / bc2) + ADAM_EPS) + ADAM_WD * w_ref[...])

    spec = pl.BlockSpec((tr, cols), lambda i: (i, 0))
    return pl.pallas_call(
        body, name=name, grid=(rows // tr,),
        in_specs=[spec] * 4, out_specs=[spec] * 3,
        out_shape=[jax.ShapeDtypeStruct((rows, cols), F32)] * 3,
    )(w, g, m, v)


MODR = 32


def _silu(v):
    return v * _sigmoid(v)


def mod_fwd(cond, w, b):
    n = w.shape[1]

    def body(c_ref, w_ref, b_ref, o_ref):
        o_ref[...] = _dot(_silu(c_ref[...]).astype(MXU), w_ref[...].astype(MXU)) + b_ref[...]

    return pl.pallas_call(
        body, name="mod_fwd", out_shape=jax.ShapeDtypeStruct((MODR, n), F32),
        in_specs=[_fullspec((MODR, D)), _fullspec((D, n)), _fullspec((1, n))], out_specs=_fullspec((MODR, n)),
        grid=(1,), compiler_params=_cp(40),
    )(cond, w, b)


def mod_wgrad(cond, dm):
    n = dm.shape[1]

    def body(c_ref, d_ref, o_ref):
        o_ref[...] = _dotg(_silu(c_ref[...]).astype(MXU), d_ref[...].astype(MXU), TN)

    return pl.pallas_call(
        body, name="mod_wgrad", out_shape=jax.ShapeDtypeStruct((D, n), F32),
        in_specs=[_fullspec((MODR, D)), _fullspec((MODR, n))], out_specs=_fullspec((D, n)),
        grid=(1,), compiler_params=_cp(40),
    )(cond, dm)


def mod_dgrad(dm, w):
    n = w.shape[1]

    def body(d_ref, w_ref, o_ref):
        o_ref[...] = _dotg(d_ref[...].astype(MXU), w_ref[...].astype(MXU), NT)

    return pl.pallas_call(
        body, name="mod_dgrad", out_shape=jax.ShapeDtypeStruct((8, D), F32),
        in_specs=[_fullspec((8, n)), _fullspec((D, n))], out_specs=_fullspec((8, D)),
        grid=(1,), compiler_params=_cp(40),
    )(dm, w)


def sum_leading(a, name="sum_leading"):
    n, r, c = a.shape

    def body(a_ref, o_ref):
        acc = a_ref[0]
        for k in range(1, n):
            acc = acc + a_ref[k]
        o_ref[...] = acc

    return pl.pallas_call(
        body, name=name, out_shape=jax.ShapeDtypeStruct((r, c), F32),
        in_specs=[_fullspec((n, r, c))], out_specs=_fullspec((r, c)), grid=(1,),
    )(a)


MESH = pl.DeviceIdType.MESH
NDEV = 8
ANY = pl.BlockSpec(memory_space=pl.ANY)


def _place():
    return lax.axis_index("x"), lax.axis_index("y"), lax.axis_index("c")


def _other_chips(x, y):
    return [(1 - x, y), (x, 1 - y), (1 - x, 1 - y)]


def allgather_small(v, name):
    r, cols = v.shape

    def body(v_ref, o_ref, send_sems, recv_sems):
        x, y, c = _place()
        me = 4 * x + 2 * y + c
        o_ref[me] = v_ref[...]
        copies = []
        for rel in range(1, NDEV):
            peer = (1 - x if rel & 4 else x, 1 - y if rel & 2 else y, 1 - c if rel & 1 else c)
            cp = pltpu.make_async_remote_copy(src_ref=v_ref, dst_ref=o_ref.at[me], send_sem=send_sems.at[rel - 1],
                                              recv_sem=recv_sems.at[rel - 1], device_id=peer, device_id_type=MESH)
            cp.start()
            copies.append(cp)
        for cp in copies:
            cp.wait_recv()
        for cp in copies:
            cp.wait_send()

    return pl.pallas_call(
        body, name=name, out_shape=jax.ShapeDtypeStruct((NDEV, r, cols), F32),
        in_specs=[pl.BlockSpec(memory_space=pltpu.VMEM)], out_specs=pl.BlockSpec(memory_space=pltpu.VMEM),
        scratch_shapes=[pltpu.SemaphoreType.DMA((NDEV - 1,)), pltpu.SemaphoreType.DMA((NDEV - 1,))],
        compiler_params=_cp(40),
    )(v)


def _sems(n):
    return [pltpu.SemaphoreType.DMA((n,)), pltpu.SemaphoreType.DMA((n,))]


def gather_shards(arrs):
    n = len(arrs)

    def body(*refs):
        srcs, outs = refs[:n], refs[n:2 * n]
        send_sems, recv_sems = refs[2 * n:]
        x, y, c = _place()
        k = 2 * x + y
        sib = (x, y, 1 - c)
        chips = _other_chips(x, y)

        def half(i, kk, cc):
            hr = arrs[i].shape[1] // 2
            return outs[i].at[kk, :, pl.ds(cc * hr, hr), :]

        def copy(i, slot, kk, cc, to, src=None):
            return pltpu.make_async_remote_copy(src_ref=half(i, kk, cc) if src is None else src, dst_ref=half(i, kk, cc),
                                                send_sem=send_sems.at[slot * n + i], recv_sem=recv_sems.at[slot * n + i],
                                                device_id=to, device_id_type=MESH)

        started = []
        for j, (px, py) in enumerate(chips):
            for i in range(n):
                hr = arrs[i].shape[1] // 2
                cp = copy(i, j, k, c, (px, py, c), src=srcs[i].at[:, pl.ds(c * hr, hr), :])
                cp.start()
                started.append(cp)
        for j, (px, py) in enumerate(chips):
            for i in range(n):
                copy(i, j, 2 * px + py, c, (px, py, c)).wait_recv()
                cp = copy(i, 3 + j, 2 * px + py, c, sib)
                cp.start()
                started.append(cp)
        for j, (px, py) in enumerate(chips):
            for i in range(n):
                copy(i, 3 + j, 2 * px + py, 1 - c, sib).wait_recv()
        for cp in started:
            cp.wait_send()

    return pl.pallas_call(
        body, name="gather_shards", out_shape=[jax.ShapeDtypeStruct((4,) + a.shape, a.dtype) for a in arrs],
        in_specs=[ANY] * n, out_specs=[ANY] * n, scratch_shapes=_sems(6 * n),
    )(*arrs)


def swap_core_halves(gs):
    n = len(gs)

    def body(*refs):
        srcs, outs = refs[:n], refs[n:2 * n]
        send_sems, recv_sems = refs[2 * n:]
        x, y, c = _place()
        copies = []
        for i in range(n):
            hr = gs[i].shape[1] // 2
            cp = pltpu.make_async_remote_copy(src_ref=srcs[i].at[:, pl.ds((1 - c) * hr, hr), :], dst_ref=outs[i],
                                              send_sem=send_sems.at[i], recv_sem=recv_sems.at[i],
                                              device_id=(x, y, 1 - c), device_id_type=MESH)
            cp.start()
            copies.append(cp)
        for cp in copies:
            cp.wait()

    return pl.pallas_call(
        body, name="swap_core_halves",
        out_shape=[jax.ShapeDtypeStruct((4, g.shape[1] // 2, g.shape[2]), g.dtype) for g in gs],
        in_specs=[ANY] * n, out_specs=[ANY] * n, scratch_shapes=_sems(n),
    )(*gs)


def add_half(g, r1, cidx, name):
    _, rows, cols = g.shape
    hr = rows // 2

    def body(c_ref, g_ref, r_ref, o_ref, ob_ref):
        s = g_ref[...] + r_ref[...]
        o_ref[...] = s
        ob_ref[...] = s.astype(BF16)

    blk = lambda f: pl.BlockSpec((1, hr, cols), f)
    return pl.pallas_call(
        body, name=name,
        out_shape=[jax.ShapeDtypeStruct((4, hr, cols), F32), jax.ShapeDtypeStruct((4, hr, cols), BF16)],
        grid_spec=pltpu.PrefetchScalarGridSpec(
            num_scalar_prefetch=1, grid=(4,),
            in_specs=[blk(lambda k, c_ref: (k, c_ref[0], 0)), blk(lambda k, c_ref: (k, 0, 0))],
            out_specs=[blk(lambda k, c_ref: (k, 0, 0)), blk(lambda k, c_ref: (k, 0, 0))]),
    )(cidx, g, r1)


def swap_chip_parts(ss):
    n = len(ss)

    def body(*refs):
        srcs, outs = refs[:n], refs[n:2 * n]
        send_sems, recv_sems = refs[2 * n:]
        x, y, c = _place()
        copies = []
        for j, (px, py) in enumerate(_other_chips(x, y)):
            for i in range(n):
                cp = pltpu.make_async_remote_copy(src_ref=srcs[i].at[2 * px + py], dst_ref=outs[i].at[j],
                                                  send_sem=send_sems.at[j * n + i], recv_sem=recv_sems.at[j * n + i],
                                                  device_id=(px, py, c), device_id_type=MESH)
                cp.start()
                copies.append(cp)
        for cp in copies:
            cp.wait()

    return pl.pallas_call(
        body, name="swap_chip_parts", out_shape=[jax.ShapeDtypeStruct((3,) + s.shape[1:], s.dtype) for s in ss],
        in_specs=[ANY] * n, out_specs=[ANY] * n, scratch_shapes=_sems(3 * n),
    )(*ss)


def sum_parts(s1, r2, kidx, name):
    _, hr, cols = s1.shape

    def body(k_ref, s_ref, r_ref, o_ref):
        o_ref[...] = ((s_ref[0] + r_ref[0].astype(F32)) + r_ref[1].astype(F32)) + r_ref[2].astype(F32)

    return pl.pallas_call(
        body, name=name, out_shape=jax.ShapeDtypeStruct((hr, cols), F32),
        grid_spec=pltpu.PrefetchScalarGridSpec(
            num_scalar_prefetch=1, grid=(1,),
            in_specs=[pl.BlockSpec((1, hr, cols), lambda i, k_ref: (k_ref[0], 0, 0)),
                      pl.BlockSpec((3, hr, cols), lambda i, k_ref: (0, 0, 0))],
            out_specs=pl.BlockSpec((hr, cols), lambda i, k_ref: (0, 0))),
    )(kidx, s1, r2)


def swap_reduced_halves(hs):
    n = len(hs)

    def body(*refs):
        srcs, outs = refs[:n], refs[n:2 * n]
        send_sems, recv_sems = refs[2 * n:]
        x, y, c = _place()
        copies = []
        for i in range(n):
            cp = pltpu.make_async_remote_copy(src_ref=srcs[i], dst_ref=outs[i], send_sem=send_sems.at[i],
                                              recv_sem=recv_sems.at[i], device_id=(x, y, 1 - c), device_id_type=MESH)
            cp.start()
            copies.append(cp)
        for cp in copies:
            cp.wait()

    return pl.pallas_call(
        body, name="swap_reduced_halves", out_shape=[jax.ShapeDtypeStruct(h.shape, h.dtype) for h in hs],
        in_specs=[ANY] * n, out_specs=[ANY] * n, scratch_shapes=_sems(n),
    )(*hs)


def adamw_halves(w, m, v, own, oth, cidx, name):
    depth, rows, cols = w.shape
    hr = rows // 2
    tr = min(hr, 256)
    nblk = hr // tr
    bc1 = 1.0 - ADAM_B1 ** ADAM_STEP
    bc2 = 1.0 - ADAM_B2 ** ADAM_STEP

    def body(c_ref, w_ref, m_ref, v_ref, own0, own1, oth0, oth1, g_ref, d_ref, nm_ref, nv_ref):
        l = pl.program_id(0)
        hi = pl.program_id(1)
        mine = jnp.where(l == 0, own0[...], own1[...])
        other = jnp.where(l == 0, oth0[...], oth1[...])
        g_v = jnp.where(hi == c_ref[0], mine, other)
        nm = ADAM_B1 * m_ref[0] + (1.0 - ADAM_B1) * g_v
        nv = ADAM_B2 * v_ref[0] + (1.0 - ADAM_B2) * (g_v * g_v)
        g_ref[0] = g_v
        nm_ref[0] = nm
        nv_ref[0] = nv
        d_ref[0] = -ADAM_LR * ((nm / bc1) / (jnp.sqrt(nv / bc2) + ADAM_EPS) + ADAM_WD * w_ref[0])

    wspec = pl.BlockSpec((1, tr, cols), lambda l, hi, b, c_ref: (l, hi * nblk + b, 0))
    gspec = pl.BlockSpec((tr, cols), lambda l, hi, b, c_ref: (b, 0))
    assert depth == 2
    return pl.pallas_call(
        body, name=name, out_shape=[jax.ShapeDtypeStruct(w.shape, F32)] * 4,
        grid_spec=pltpu.PrefetchScalarGridSpec(
            num_scalar_prefetch=1, grid=(depth, 2, nblk),
            in_specs=[wspec] * 3 + [gspec] * 4, out_specs=[wspec] * 4),
    )(cidx, w, m, v, own[0], own[1], oth[0], oth[1])


class _NS:
    def __init__(self, **kw):
        self.__dict__.update(kw)


def _prep_layer(win, wqb, wkvb, wout, w1, w2, conv_w, conv_b, dt_bias, a_log, ssd_d, ssd_nw, qnw, kvnw, pool_w,
                pool_scale, n1, n2):
    winp = jnp.concatenate([win[:, 0:384], win[:, 384:1280], win[:, 1292:1548], win[:, 1548:1804], win[:, 1836:2092],
                            win[:, 1804:1836], win[:, 1280:1292], jnp.zeros((D, NP - IN_COLS), win.dtype)], axis=1)
    wq = jnp.pad(wqb.reshape(256, MLA_HEADS, QK_DIM), ((0, 0), (0, 0), (0, HP - QK_DIM))).reshape(256, QW)
    kv3 = wkvb.reshape(256, MLA_HEADS, 128)
    wk = jnp.pad(kv3[:, :, :64], ((0, 0), (0, 0), (0, 64))).reshape(256, QW)
    wv = jnp.pad(kv3[:, :, 64:], ((0, 0), (0, 0), (0, 64))).reshape(256, QW)
    wo = jnp.concatenate([jnp.pad(wout[384:768].reshape(MLA_HEADS, 64, D), ((0, 0), (0, 64), (0, 0))).reshape(QW, D),
                          wout[0:384], wout[768:1024]], axis=0)
    wbd = (jnp.eye(4, dtype=F32)[:, None, :, None] * pool_w[:, :, None, :]).reshape(POOL_DIM, POOL_DIM).astype(MXU)
    a = -jnp.exp(a_log)
    return _NS(
        winp=winp, wint=winp.T, wq=wq, wqt=wq.T, wk=wk, wkt=wk.T, wv=wv, wvt=wv.T, wo=wo, wot=wo.T,
        w1=w1, w1t=w1.T, w2=w2, w2t=w2.T, wbd=wbd, wbdt=wbd.T,
        cw8=jnp.pad(conv_w, ((0, 4), (0, 0))), cb=conv_b[None],
        dtb=jnp.pad(dt_bias.reshape(1, 12), ((0, 0), (DT0, 128 - DT0 - 12))),
        arow=jnp.pad(a[:, None, :], ((0, 0), (0, 7), (0, 128 - SSD_HEADS))), a=a,
        dexp=jnp.repeat(ssd_d, SSD_P)[None], ssd_nw=ssd_nw[None], qnw=qnw[None], kvnw=kvnw[None],
        pscale=pool_scale[None], n1=n1[None], n2=n2[None])


def _unprep_grads(dwinp, dwq, dwk, dwv, dwo):
    dwin = jnp.concatenate([dwinp[:, 0:384], dwinp[:, 384:1280], dwinp[:, 2080:2092], dwinp[:, 1280:1536],
                            dwinp[:, 1536:1792], dwinp[:, 2048:2080], dwinp[:, 1792:2048]], axis=1)
    dwqb = dwq.reshape(256, MLA_HEADS, HP)[:, :, :QK_DIM].reshape(256, MLA_HEADS * QK_DIM)
    dwkvb = jnp.concatenate([dwk.reshape(256, MLA_HEADS, HP)[:, :, :64], dwv.reshape(256, MLA_HEADS, HP)[:, :, :64]],
                            axis=2).reshape(256, MLA_HEADS * 128)
    dwout = jnp.concatenate([dwo[QW:QW + 384], dwo[0:QW].reshape(MLA_HEADS, HP, D)[:, :64].reshape(384, D),
                             dwo[QW + 384:CAT]], axis=0)
    return dwin, dwqb, dwkvb, dwout


def _rope_tables(nb, N):
    t = jnp.arange(N, dtype=F32)
    row = jnp.floor(t / GRID_W)
    col = t - row * GRID_W
    inv = jnp.asarray(10000.0 ** (-np.arange(8, dtype=np.float32) / 8), F32)
    ang = jnp.stack([row[:, None] * inv, col[:, None] * inv], axis=1)
    cs, sn = jnp.cos(ang), jnp.sin(ang)
    zero = jnp.zeros_like(sn)
    lanes = lambda first, second: jnp.stack([first, second], axis=2).reshape(N, 32)
    pad = lambda a, fill: jnp.concatenate([jnp.full((N, 64), fill, F32), a, jnp.full((N, 32), fill, F32)], axis=1)
    tabs = []
    for tab, fill in ((pad(lanes(cs, cs), 1.0), 1.0), (pad(lanes(-sn, zero), 0.0), 0.0), (pad(lanes(zero, sn), 0.0), 0.0)):
        one = jnp.concatenate([jnp.full((CTX, 128), fill, F32), tab], axis=0)
        tabs.append(jnp.tile(one, (nb, 1)))
    return tabs


def _eexp():
    e = np.zeros((128, SSD_INNER), np.float32)
    for h in range(SSD_HEADS):
        e[h, h * SSD_P:(h + 1) * SSD_P] = 1.0
    return jnp.asarray(e)


def _layer_fwd(X, bm, lw, cst):
    nb, T, bps, N = cst.nb, cst.T, cst.bps, cst.N
    h1, pz, pxbc, pqa, pkva, ppool, plast = in_proj(X, bm, lw.n1, lw.winp)
    xs, bmat, cmat, dtv = ssd_prep(pxbc, plast, lw.cw8, lw.cb, lw.dtb, bps)
    y2, hin = ssd_scan_fwd(xs, bmat, cmat, dtv, lw.arow, cst.eexp, nb, T)
    ssd = ssd_out_fwd(y2, xs, pz, lw.dexp, lw.ssd_nw)
    q, k, v, cq, ckv = mla_prep(pqa, pkva, plast, lw.qnw, lw.kvnw, lw.wq, lw.wk, lw.wv, *cst.rope)
    attn, lse = attn_fwd(q, k, v, nb, T)
    pool = pool_fwd(ppool, lw.wbd, lw.pscale, bps, N)
    x1, mix, cat = mix_fwd(X, attn, ssd, pool, bm, lw.wo)
    x2, mo, r, h2 = mlp_fwd(x1, bm, lw.n2, lw.w1, lw.w2)
    sv = _NS(X=X, h1=h1, pz=pz, pxbc=pxbc, pqa=pqa, pkva=pkva, ppool=ppool, plast=plast, xs=xs, bmat=bmat, cmat=cmat,
             dtv=dtv, y2=y2, hin=hin, q=q, k=k, v=v, cq=cq, ckv=ckv, attn=attn, lse=lse, x1=x1, mix=mix, cat=cat, mo=mo, r=r,
             h2=h2)
    return x2, sv


def _layer_bwd(dx2, bm, lw, sv, cst):
    nb, T, bps, N = cst.nb, cst.T, cst.bps, cst.N
    dx1, du, dob, part_mlp = mlp_bwd(dx2, sv.x1, sv.mo, sv.r, bm, lw.n2, lw.w2t, lw.w1t)
    dw1 = mm_tn(sv.h2, du, name="wgrad_mlp1", col_blocks=True)
    dw2 = mm_tn(sv.r, dob, square_a=True, name="wgrad_mlp2")
    dattn, dssd, dpool, dmb, part_mix = mix_bwd(dx1, sv.mix, bm, lw.wot)
    dwo = mm_tn(sv.cat, dmb, name="wgrad_out")
    dppool, dwbd, part_pool = pool_bwd(sv.ppool, dpool, lw.wbd, lw.wbdt, lw.pscale, bps, N)
    dq, dk, dv = attn_bwd(sv.q, sv.k, sv.v, sv.attn, sv.lse, dattn, nb, T)
    dpqa, dpkva, dkr, dql, dkm, dvb, part_mla = mla_prep_bwd(dq, dk, dv, sv.pqa, sv.pkva, lw.qnw, lw.kvnw, lw.wqt,
                                                             lw.wkt, lw.wvt, *cst.rope)
    dwq = mm_tn(sv.cq, dql, name="wgrad_q")
    dwk = mm_tn(sv.ckv, dkm, name="wgrad_k")
    dwv = mm_tn(sv.ckv, dvb, name="wgrad_v")
    dyy, dz, dxs_skip, part_so = ssd_out_bwd(dssd, sv.y2, sv.xs, sv.pz, lw.dexp, lw.ssd_nw)
    dxs2, dbm2, dcm2, ddt2, da = ssd_scan_bwd(sv.xs, sv.bmat, sv.cmat, sv.dtv, lw.arow, cst.eexp, sv.hin, dyy, nb, T)
    dpre, dlast_dt, part_conv = ssd_prep_bwd_a(sv.pxbc, sv.plast, lw.cw8, lw.cb, lw.dtb, dxs_skip, dxs2, dbm2, dcm2,
                                               ddt2, bps)
    dpxbc = ssd_prep_bwd_b(dpre, lw.cw8, bps)
    dx, dpb, part_in = in_proj_bwd(dx1, sv.X, dz, dpxbc, dpqa, dpkva, dppool, dkr, dlast_dt, bm, lw.n1, lw.wint)
    dwinp = mm_tn(sv.h1, dpb, name="wgrad_in")

    dwin, dwqb, dwkvb, dwout = _unprep_grads(dwinp, dwq, dwk, dwv, dwo)
    dmod = jnp.stack([part_in[:, 0], part_in[:, 1], part_mix[:, 0], part_mlp[:, 0], part_mlp[:, 1], part_mlp[:, 2]],
                     axis=1)
    dmod = dmod.reshape(nb, bps, 6, D)
    dm_rows = jnp.concatenate([jnp.sum(dmod[:, 1:], axis=1), jnp.sum(dmod[:, 0], axis=0)[None]], axis=0)
    da_dh = jnp.sum(da.reshape(nb, 2, 8, 128)[:, :, 0, :SSD_HEADS], axis=0)
    conv_parts = jnp.sum(part_conv, axis=0)
    by_chip_cols = lambda a: jnp.stack([a[:, k * (a.shape[1] // 4):(k + 1) * (a.shape[1] // 4)] for k in range(4)])
    by_chip_rows = lambda a: a.reshape(4, a.shape[0] // 4, a.shape[1])
    g = _NS(
        w_in=by_chip_cols(dwin), w_q_b=by_chip_cols(dwqb), w_kv_b=by_chip_cols(dwkvb), w_out=by_chip_rows(dwout),
        w_mlp1=dw1, w_mlp2=by_chip_rows(dw2),
        dm_rows=dm_rows.reshape(3, 6 * D),
        norm1_w=jnp.sum(part_in[:, 2], axis=0), norm2_w=jnp.sum(part_mlp[:, 3], axis=0),
        conv_w=conv_parts[0:4], conv_b=conv_parts[4],
        dt_bias=conv_parts[5, DT0:DT0 + 12].reshape(2, SSD_HEADS), a_log=da_dh * lw.a,
        ssd_d=jnp.sum(jnp.sum(part_so[:, 1], axis=0).reshape(SSD_HEADS, SSD_P), axis=1),
        ssd_norm_w=jnp.sum(part_so[:, 0], axis=0),
        q_a_norm_w=jnp.sum(part_mla[:, 0], axis=0), kv_a_norm_w=jnp.sum(part_mla[:, 1], axis=0),
        pool_w=jnp.stack([dwbd[i * 64:(i + 1) * 64, i * 64:(i + 1) * 64] for i in range(4)]),
        pool_scale=jnp.sum(part_pool[:, 0], axis=0))
    return dx, g


def _local_step(x, ctx, tgt, bms, lws, fw, cst):
    nb, N = x.shape[0], x.shape[1]
    R = nb * cst.T
    X = jnp.concatenate([ctx, x], axis=1).reshape(R, D)
    saved = []
    for l in range(DEPTH):
        X, sv = _layer_fwd(X, bms[l], lws[l], cst)
        saved.append(sv)
    dX, part_fin = final_loss(X, tgt.reshape(nb * N, D), fw[None], cst.bps)
    loss = (0.5 / D) * jnp.sum(part_fin[:, 1])
    dfw = jnp.sum(part_fin[:, 0], axis=0)
    grads = [None] * DEPTH
    for l in reversed(range(DEPTH)):
        dX, grads[l] = _layer_bwd(dX, bms[l], lws[l], saved[l], cst)
    grad_x = dX.reshape(nb, cst.T, D)[:, CTX:, :]
    return loss, grad_x, grads, dfw


def _consts(nb, N):
    T = CTX + N
    bps = T // SB
    return _NS(nb=nb, N=N, T=T, bps=bps, eexp=_eexp(), rope=_rope_tables(nb, N))


def _block_mod(modrows, cst):
    rows = []
    for b in range(cst.nb):
        rows.append(modrows[cst.nb:cst.nb + 1])
        rows.append(jnp.broadcast_to(modrows[b:b + 1], (cst.bps - 1, 6, D)))
    return jnp.pad(jnp.concatenate(rows, axis=0), ((0, 0), (0, 2), (0, 0)))


SMALL = (("norm1_w", (2, D)), ("norm2_w", (2, D)), ("conv_w", (2, 4, XBC)), ("conv_b", (2, XBC)),
         ("dt_bias", (2, 2, 6)), ("a_log", (2, 2, 6)), ("ssd_d", (2, 6)), ("ssd_norm_w", (2, 384)),
         ("q_a_norm_w", (2, 256)), ("kv_a_norm_w", (2, 256)), ("pool_w", (2, 4, 64, 64)), ("pool_scale", (2, 256)),
         ("final_norm_w", (D,)), ("mod_b", (2, 6 * D)))
SMALL_ROWS = 64
DM_ROWS = 48


def _pack_small(vals):
    flat = jnp.concatenate([vals[n].reshape(-1) for n, _ in SMALL])
    return jnp.pad(flat, (0, SMALL_ROWS * D - flat.shape[0])).reshape(SMALL_ROWS, D)


def _unpack_small(p):
    flat = p.reshape(-1)
    out, off = {}, 0
    for n, shp in SMALL:
        size = int(np.prod(shp))
        out[n] = flat[off:off + size].reshape(shp)
        off += size
    return out


def cctx_grad(parts, c_ctx):
    def body(p_ref, c_ref, o_ref):
        acc = ((p_ref[0] + p_ref[1]) + p_ref[2]) + p_ref[3]
        v = c_ref[...]
        sig = _sigmoid(v)
        o_ref[...] = acc * (sig * (1.0 + v * (1.0 - sig)))

    return pl.pallas_call(
        body, name="cctx_grad", out_shape=jax.ShapeDtypeStruct((8, D), F32),
        in_specs=[_fullspec((4, 8, D)), _fullspec((1, D))], out_specs=_fullspec((8, D)), grid=(1,),
    )(parts, c_ctx)


def kernel(x, c, ctx, c_ctx, mod_w, mod_b, norm1_w, norm2_w, w_in, conv_w, conv_b, dt_bias, a_log, ssd_d, ssd_norm_w, q_a_norm_w, w_q_b, kv_a_norm_w, w_kv_b, pool_w, pool_scale, w_out, w_mlp1, w_mlp2, final_norm_w, loss_target, m_c_ctx, m_mod_w, m_mod_b, m_norm1_w, m_norm2_w, m_w_in, m_conv_w, m_conv_b, m_dt_bias, m_a_log, m_ssd_d, m_ssd_norm_w, m_q_a_norm_w, m_w_q_b, m_kv_a_norm_w, m_w_kv_b, m_pool_w, m_pool_scale, m_w_out, m_w_mlp1, m_w_mlp2, m_final_norm_w, v_c_ctx, v_mod_w, v_mod_b, v_norm1_w, v_norm2_w, v_w_in, v_conv_w, v_conv_b, v_dt_bias, v_a_log, v_ssd_d, v_ssd_norm_w, v_q_a_norm_w, v_w_q_b, v_kv_a_norm_w, v_w_kv_b, v_pool_w, v_pool_scale, v_w_out, v_w_mlp1, v_w_mlp2, v_final_norm_w):
    nb, N = x.shape[0], x.shape[1]
    cst = _consts(nb, N)
    xi, yi, ci = _place()
    me = 4 * xi + 2 * yi + ci
    kchip = 2 * xi + yi
    mcols = mod_w.shape[2]
    cshard = conv_w.shape[2]

    blk = jnp.zeros((16, D), F32).at[0:nb].set(c).at[8:16, 0:cshard].set(conv_w.reshape(8, cshard))
    g1 = allgather_small(blk, "gather_cond")
    cond = jnp.concatenate([g1[:, 0:nb].reshape(NDEV * nb, D), c_ctx[None],
                            jnp.zeros((MODR - NDEV * nb - 1, D), F32)], axis=0)
    conv_full = [jnp.concatenate([g1[2 * k, 8 + 4 * l:12 + 4 * l, 0:cshard] for k in range(4)], axis=1)
                 for l in range(DEPTH)]

    mb = [lax.dynamic_slice_in_dim(mod_b[l], kchip * mcols, mcols)[None] for l in range(DEPTH)]
    ms = jnp.concatenate([mod_fwd(cond, mod_w[l], mb[l]) for l in range(DEPTH)], axis=0)
    g2 = allgather_small(ms, "gather_mod")
    bms = []
    for l in range(DEPTH):
        m_all = jnp.concatenate([g2[2 * k, MODR * l:MODR * (l + 1)] for k in range(4)], axis=1)
        mine = jnp.concatenate([lax.dynamic_slice_in_dim(m_all, nb * me, nb), m_all[NDEV * nb:NDEV * nb + 1]], axis=0)
        bms.append(_block_mod(mine.reshape(nb + 1, 6, D), cst))

    big = (w_in, w_q_b, w_kv_b, w_out, w_mlp1, w_mlp2)
    wg = gather_shards([a.astype(MXU) for a in big])
    fw_in, fw_qb, fw_kvb, fw_out, fw1, fw2 = [
        jnp.concatenate([jnp.where(kchip == k, a.astype(MXU), g[k]) for k in range(4)], axis=ax)
        for a, g, ax in zip(big, wg, (2, 2, 2, 1, 2, 1))]
    lws = [_prep_layer(fw_in[l], fw_qb[l], fw_kvb[l], fw_out[l], fw1[l], fw2[l], conv_full[l], conv_b[l], dt_bias[l],
                       a_log[l], ssd_d[l], ssd_norm_w[l], q_a_norm_w[l], kv_a_norm_w[l], pool_w[l], pool_scale[l],
                       norm1_w[l], norm2_w[l]) for l in range(DEPTH)]

    loss_part, grad_x, grads, dfw = _local_step(x, ctx, loss_target, bms, lws, final_norm_w, cst)
    loss = lax.psum(loss_part, ("x", "y", "c"))

    names = ("w_in", "w_q_b", "w_kv_b", "w_out", "w_mlp1", "w_mlp2")
    gs = [getattr(grads[l], n) for n in names for l in range(DEPTH)]
    cidx = jnp.reshape(ci, (1,)).astype(jnp.int32)
    kidx = jnp.reshape(kchip, (1,)).astype(jnp.int32)
    s1 = [add_half(g, r, cidx, "add_half_" + names[i // DEPTH]) for i, (g, r) in enumerate(zip(gs, swap_core_halves(gs)))]
    r2 = swap_chip_parts([s[1] for s in s1])
    g_own = [sum_parts(s[0], r, kidx, "sum_parts_" + names[i // DEPTH]) for i, (s, r) in enumerate(zip(s1, r2))]
    g_oth = swap_reduced_halves(g_own)

    small = {n: jnp.stack([getattr(grads[l], n) for l in range(DEPTH)]) for n, _ in SMALL if n not in ("final_norm_w", "mod_b")}
    small["final_norm_w"] = dfw
    small["mod_b"] = jnp.stack([jnp.sum(grads[l].dm_rows, axis=0) for l in range(DEPTH)])
    dm = jnp.pad(jnp.concatenate([grads[l].dm_rows for l in range(DEPTH)], axis=0), ((0, 8 - 3 * DEPTH), (0, 0)))
    g3 = allgather_small(jnp.concatenate([_pack_small(small), dm.reshape(DM_ROWS, D)], axis=0), "gather_small")
    tot = sum_leading(g3, "sum_small")
    gsmall = _unpack_small(tot[0:SMALL_ROWS])
    ctx_sum = tot[SMALL_ROWS:].reshape(8, 6 * D)
    dm_dev = g3[:, SMALL_ROWS:].reshape(NDEV, 8, 6 * D)
    g_mod_w, dpart = [], jnp.zeros((8, D), F32)
    for l in range(DEPTH):
        dm_all = jnp.concatenate([dm_dev[:, 3 * l:3 * l + nb].reshape(NDEV * nb, 6 * D), ctx_sum[3 * l + nb:3 * l + nb + 1],
                                  jnp.zeros((MODR - NDEV * nb - 1, 6 * D), F32)], axis=0)
        g_mod_w.append(mod_wgrad(cond, lax.dynamic_slice_in_dim(dm_all, kchip * mcols, mcols, axis=1)))
        dctx = jnp.pad(lax.dynamic_slice_in_dim(ctx_sum[3 * l + nb:3 * l + nb + 1], kchip * mcols, mcols, axis=1), ((0, 7), (0, 0)))
        dpart = dpart + mod_dgrad(dctx, mod_w[l])
    g4 = allgather_small(dpart, "gather_cctx")
    g_c_ctx = cctx_grad(g4[0::2], c_ctx[None])[0]

    res = {}
    moments = ((m_w_in, v_w_in), (m_w_q_b, v_w_q_b), (m_w_kv_b, v_w_kv_b), (m_w_out, v_w_out), (m_w_mlp1, v_w_mlp1),
               (m_w_mlp2, v_w_mlp2))
    for i, (n, w, (m, v)) in enumerate(zip(names, big, moments)):
        res[n] = tuple(adamw_halves(w, m, v, g_own[DEPTH * i:DEPTH * (i + 1)], g_oth[DEPTH * i:DEPTH * (i + 1)], cidx,
                                    "adamw_" + n))
    g_mw = jnp.stack(g_mod_w)
    r_mw = adamw(mod_w.reshape(-1, mcols), g_mw.reshape(-1, mcols), m_mod_w.reshape(-1, mcols),
                 v_mod_w.reshape(-1, mcols), name="adamw_mod_w")
    res["mod_w"] = (g_mw,) + tuple(a.reshape(mod_w.shape) for a in r_mw)

    given = dict(norm1_w=(norm1_w, m_norm1_w, v_norm1_w), norm2_w=(norm2_w, m_norm2_w, v_norm2_w),
                 conv_b=(conv_b, m_conv_b, v_conv_b), dt_bias=(dt_bias, m_dt_bias, v_dt_bias),
                 a_log=(a_log, m_a_log, v_a_log), ssd_d=(ssd_d, m_ssd_d, v_ssd_d),
                 ssd_norm_w=(ssd_norm_w, m_ssd_norm_w, v_ssd_norm_w), q_a_norm_w=(q_a_norm_w, m_q_a_norm_w, v_q_a_norm_w),
                 kv_a_norm_w=(kv_a_norm_w, m_kv_a_norm_w, v_kv_a_norm_w), pool_w=(pool_w, m_pool_w, v_pool_w),
                 pool_scale=(pool_scale, m_pool_scale, v_pool_scale),
                 final_norm_w=(final_norm_w, m_final_norm_w, v_final_norm_w), mod_b=(mod_b, m_mod_b, v_mod_b))
    zero_cw = jnp.zeros((2, 4, XBC), F32)
    packs = [_pack_small({n: (given[n][i] if n in given else zero_cw) for n, _ in SMALL}) for i in range(3)]
    r_small = [_unpack_small(a) for a in adamw(packs[0], tot[0:SMALL_ROWS], packs[1], packs[2], name="adamw_small")]
    for n in given:
        res[n] = (gsmall[n], r_small[0][n], r_small[1][n], r_small[2][n])

    g_cw = lax.dynamic_slice_in_dim(gsmall["conv_w"], kchip * cshard, cshard, axis=2)
    padcw = lambda a: jnp.pad(a.reshape(8, cshard), ((0, 0), (0, 256 - cshard)))
    r_cw = adamw(padcw(conv_w), padcw(g_cw), padcw(m_conv_w), padcw(v_conv_w), name="adamw_conv_w")
    res["conv_w"] = (g_cw,) + tuple(a[:, 0:cshard].reshape(conv_w.shape) for a in r_cw)
    r_cc = adamw(c_ctx.reshape(8, 128), g_c_ctx.reshape(8, 128), m_c_ctx.reshape(8, 128), v_c_ctx.reshape(8, 128),
                 name="adamw_c_ctx")
    res["c_ctx"] = (g_c_ctx,) + tuple(a.reshape(D) for a in r_cc)

    order = ("c_ctx", "mod_w", "mod_b", "norm1_w", "norm2_w", "w_in", "conv_w", "conv_b", "dt_bias", "a_log", "ssd_d",
             "ssd_norm_w", "q_a_norm_w", "w_q_b", "kv_a_norm_w", "w_kv_b", "pool_w", "pool_scale", "w_out", "w_mlp1",
             "w_mlp2", "final_norm_w")
    return (loss, grad_x) + tuple(res[n][i] for i in range(4) for n in order)
```

```python
import functools
import math

import numpy as np
import jax
import jax.numpy as jnp
from jax import lax
from jax.experimental import pallas as pl
from jax.experimental.pallas import tpu as pltpu

F32 = jnp.float32
BF16 = jnp.bfloat16
MXU = jnp.bfloat16
HI = lax.Precision.HIGHEST

D = 1024
DEPTH = 2
GRID_W = 64
CTX = 256
EPS = 1e-6
SSD_HEADS = 6
SSD_P = 64
SSD_INNER = 384
SSD_N = 128
CHUNK = 128
XBC = 896
MLA_HEADS = 6
QK_NOPE = 64
QK_ROPE = 32
QK_DIM = 96
HP = 128
QW = MLA_HEADS * HP
POOL_DIM = 256
D_FF = 4096
FF_BLK = 1024
IN_COLS = 2092
NP = 2176
P_SPLITS = (384, 896, 256, 256, 256, 128)
DT0 = 32
CAT = QW + SSD_INNER + POOL_DIM

SB = 256
TM = 512
HALO = 8

ADAM_LR = 0.001
ADAM_B1 = 0.9
ADAM_B2 = 0.999
ADAM_EPS = 1e-08
ADAM_WD = 0.01
ADAM_STEP = 10

NT = (((1,), (1,)), ((), ()))
TN = (((0,), (0,)), ((), ()))


def _cp(vmem_mb=None):
    if vmem_mb is None:
        return pltpu.CompilerParams()
    return pltpu.CompilerParams(vmem_limit_bytes=vmem_mb << 20)


def _dot(a, b):
    return jnp.dot(a, b, preferred_element_type=F32)


def _dotg(a, b, dims):
    return lax.dot_general(a, b, dims, preferred_element_type=F32)


def _dot_hi(a, b, dims=None, sel_first=False):
    dims = (((1,), (0,)), ((), ())) if dims is None else dims
    v, s = (b, a) if sel_first else (a, b)
    hi = v.astype(BF16)
    lo = (v - hi.astype(F32)).astype(BF16)
    s = s.astype(BF16)
    if sel_first:
        return _dotg(s, hi, dims) + _dotg(s, lo, dims)
    return _dotg(hi, s, dims) + _dotg(lo, s, dims)


def _rms_hat(x):
    rstd = lax.rsqrt(jnp.mean(x * x, axis=-1, keepdims=True) + EPS)
    return x * rstd, rstd


def _rms_bwd(dn, xhat, rstd, w):
    dxhat = dn * w
    dx = rstd * (dxhat - xhat * jnp.mean(dxhat * xhat, axis=-1, keepdims=True))
    return dx, jnp.sum(dn * xhat, axis=0, keepdims=True)


def _sigmoid(z):
    return 1.0 / (1.0 + jnp.exp(-z))


def _colsum(a):
    return jnp.sum(a, axis=0, keepdims=True)


def _rowspec(cols, tm=TM):
    return pl.BlockSpec((tm, cols), lambda i: (i, 0))


def _fullspec(shape):
    n = len(shape)
    return pl.BlockSpec(shape, lambda *_: (0,) * n)


def _resident(shape):
    n = len(shape)
    return pl.BlockSpec(shape, lambda *_: (0,) * n, pipeline_mode=pl.Buffered(1))


def _halo_specs(cols, nrows):
    per = SB // HALO
    last = nrows // HALO - 1
    prev = pl.BlockSpec((HALO, cols), lambda i: (jnp.maximum(i * per - 1, 0), 0))
    nxt = pl.BlockSpec((HALO, cols), lambda i: (jnp.minimum((i + 1) * per, last), 0))
    return prev, nxt


def _ext_rows(cur, prev, nxt, i, blocks_per_sample):
    j = i % blocks_per_sample
    first = jnp.logical_or(j == 0, j == 1)
    last = jnp.logical_or(j == 0, j == blocks_per_sample - 1)
    p = jnp.where(first, 0.0, prev)
    n = jnp.where(last, 0.0, nxt)
    return jnp.concatenate([p, cur, n], axis=0)


def _shift(ext, s):
    n = ext.shape[0]
    return pltpu.roll(ext, (-s) % n, axis=0)[HALO:HALO + SB, :]


def in_proj(x, bm, nw, w):
    R = x.shape[0]

    def body(x_ref, bm_ref, nw_ref, w_ref, h_ref, *outs):
        for s in range(TM // SB):
            rows = slice(s * SB, (s + 1) * SB)
            xhat, _ = _rms_hat(x_ref[rows, :])
            h = xhat * nw_ref[...] * (1.0 + bm_ref[s, 1:2, :]) + bm_ref[s, 0:1, :]
            h_ref[rows, :] = h.astype(h_ref.dtype)
        p = _dot(h_ref[...], w_ref[...])
        off = 0
        for o, n in zip(outs, P_SPLITS):
            o[...] = p[:, off:off + n]
            off += n

    return pl.pallas_call(
        body, name="in_proj", grid=(R // TM,),
        in_specs=[_rowspec(D), pl.BlockSpec((TM // SB, 8, D), lambda i: (i, 0, 0)), _fullspec((1, D)),
                  _fullspec((D, NP))],
        out_specs=[_rowspec(D)] + [_rowspec(n) for n in P_SPLITS],
        out_shape=[jax.ShapeDtypeStruct((R, D), MXU)] + [jax.ShapeDtypeStruct((R, n), F32) for n in P_SPLITS],
        compiler_params=_cp(56),
    )(x, bm, nw, w)


def in_proj_bwd(dx1, x, dz, dxbc, dqa, dkva, dpool, dkr, ddt, bm, nw, wt):
    R = x.shape[0]

    def body(dx1_ref, x_ref, dz_ref, dxbc_ref, dqa_ref, dkva_ref, dpool_ref, dkr_ref, ddt_ref, bm_ref, nw_ref,
             wt_ref, dx_ref, dp_ref, part_ref):
        dp_ref[:, 0:384] = dz_ref[...].astype(dp_ref.dtype)
        dp_ref[:, 384:1280] = dxbc_ref[...].astype(dp_ref.dtype)
        dp_ref[:, 1280:1536] = dqa_ref[...].astype(dp_ref.dtype)
        dp_ref[:, 1536:1792] = dkva_ref[...].astype(dp_ref.dtype)
        dp_ref[:, 1792:2048] = dpool_ref[...].astype(dp_ref.dtype)
        dp_ref[:, 2048:2176] = (dkr_ref[...] + ddt_ref[...]).astype(dp_ref.dtype)
        dh = _dot(dp_ref[...], wt_ref[...])
        w = nw_ref[...]
        for s in range(TM // SB):
            rows = slice(s * SB, (s + 1) * SB)
            xhat, rstd = _rms_hat(x_ref[rows, :])
            dhs = dh[rows, :]
            sc1 = 1.0 + bm_ref[s, 1:2, :]
            dx, dnw = _rms_bwd(dhs * sc1, xhat, rstd, w)
            dx_ref[rows, :] = dx1_ref[rows, :] + dx
            part_ref[s] = jnp.concatenate(
                [_colsum(dhs), _colsum(dhs * xhat * w), dnw, jnp.zeros((5, D), F32)], axis=0)

    return pl.pallas_call(
        body, name="in_proj_bwd", grid=(R // TM,),
        in_specs=[_rowspec(D), _rowspec(D), _rowspec(384), _rowspec(896), _rowspec(256), _rowspec(256),
                  _rowspec(256), _rowspec(128), _rowspec(128),
                  pl.BlockSpec((TM // SB, 8, D), lambda i: (i, 0, 0)), _fullspec((1, D)), _fullspec((NP, D))],
        out_specs=[_rowspec(D), _rowspec(NP), pl.BlockSpec((TM // SB, 8, D), lambda i: (i, 0, 0))],
        out_shape=[jax.ShapeDtypeStruct((R, D), F32), jax.ShapeDtypeStruct((R, NP), MXU),
                   jax.ShapeDtypeStruct((R // SB, 8, D), F32)],
        compiler_params=_cp(56),
    )(dx1, x, dz, dxbc, dqa, dkva, dpool, dkr, ddt, bm, nw, wt)


def mix_fwd(x, attn, ssd, pool, bm, wo):
    R = x.shape[0]

    def body(x_ref, a_ref, s_ref, p_ref, bm_ref, wo_ref, x1_ref, mix_ref, cat_ref):
        cat_ref[:, 0:QW] = a_ref[...].astype(cat_ref.dtype)
        cat_ref[:, QW:QW + SSD_INNER] = s_ref[...].astype(cat_ref.dtype)
        cat_ref[:, QW + SSD_INNER:CAT] = p_ref[...].astype(cat_ref.dtype)
        mix = _dot(cat_ref[...], wo_ref[...])
        mix_ref[...] = mix
        for s in range(TM // SB):
            rows = slice(s * SB, (s + 1) * SB)
            x1_ref[rows, :] = x_ref[rows, :] + bm_ref[s, 2:3, :] * mix[rows, :]

    return pl.pallas_call(
        body, name="mix_fwd", grid=(R // TM,),
        in_specs=[_rowspec(D), _rowspec(QW), _rowspec(SSD_INNER), _rowspec(POOL_DIM),
                  pl.BlockSpec((TM // SB, 8, D), lambda i: (i, 0, 0)), _fullspec((CAT, D))],
        out_specs=[_rowspec(D), _rowspec(D), _rowspec(CAT)],
        out_shape=[jax.ShapeDtypeStruct((R, D), F32), jax.ShapeDtypeStruct((R, D), F32),
                   jax.ShapeDtypeStruct((R, CAT), MXU)],
        compiler_params=_cp(48),
    )(x, attn, ssd, pool, bm, wo)


def mix_bwd(dx1, mix, bm, wot):
    R = dx1.shape[0]

    def body(dx1_ref, mix_ref, bm_ref, wot_ref, da_ref, ds_ref, dpl_ref, dmb_ref, part_ref):
        for s in range(TM // SB):
            rows = slice(s * SB, (s + 1) * SB)
            d = dx1_ref[rows, :]
            dmb_ref[rows, :] = (d * bm_ref[s, 2:3, :]).astype(dmb_ref.dtype)
            part_ref[s] = jnp.concatenate([_colsum(d * mix_ref[rows, :]), jnp.zeros((7, D), F32)], axis=0)
        dcat = _dot(dmb_ref[...], wot_ref[...])
        da_ref[...] = dcat[:, 0:QW]
        ds_ref[...] = dcat[:, QW:QW + SSD_INNER]
        dpl_ref[...] = dcat[:, QW + SSD_INNER:CAT]

    return pl.pallas_call(
        body, name="mix_bwd", grid=(R // TM,),
        in_specs=[_rowspec(D), _rowspec(D), pl.BlockSpec((TM // SB, 8, D), lambda i: (i, 0, 0)),
                  _fullspec((D, CAT))],
        out_specs=[_rowspec(QW), _rowspec(SSD_INNER), _rowspec(POOL_DIM), _rowspec(D),
                   pl.BlockSpec((TM // SB, 8, D), lambda i: (i, 0, 0))],
        out_shape=[jax.ShapeDtypeStruct((R, QW), F32), jax.ShapeDtypeStruct((R, SSD_INNER), F32),
                   jax.ShapeDtypeStruct((R, POOL_DIM), F32), jax.ShapeDtypeStruct((R, D), MXU),
                   jax.ShapeDtypeStruct((R // SB, 8, D), F32)],
        compiler_params=_cp(48),
    )(dx1, mix, bm, wot)


def mlp_fwd(x1, bm, nw, w1, w2):
    R = x1.shape[0]

    def body(x1_ref, bm_ref, nw_ref, w1_ref, w2_ref, x2_ref, mo_ref, r_ref, h2_ref):
        for s in range(TM // SB):
            rows = slice(s * SB, (s + 1) * SB)
            xhat, _ = _rms_hat(x1_ref[rows, :])
            h = xhat * nw_ref[...] * (1.0 + bm_ref[s, 4:5, :]) + bm_ref[s, 3:4, :]
            h2_ref[rows, :] = h.astype(h2_ref.dtype)
        for j in range(D_FF // FF_BLK):
            cols = slice(j * FF_BLK, (j + 1) * FF_BLK)
            r = jnp.maximum(_dot(h2_ref[...], w1_ref[:, cols]), 0.0)
            r_ref[:, cols] = r.astype(r_ref.dtype)
            d = _dot((r * r).astype(MXU), w2_ref[cols, :])
            if j == 0:
                mo_ref[...] = d
            else:
                mo_ref[...] += d
        for s in range(TM // SB):
            rows = slice(s * SB, (s + 1) * SB)
            x2_ref[rows, :] = x1_ref[rows, :] + bm_ref[s, 5:6, :] * mo_ref[rows, :]

    return pl.pallas_call(
        body, name="mlp_fwd", grid=(R // TM,),
        in_specs=[_rowspec(D), pl.BlockSpec((TM // SB, 8, D), lambda i: (i, 0, 0)), _fullspec((1, D)),
                  _resident((D, D_FF)), _resident((D_FF, D))],
        out_specs=[_rowspec(D), _rowspec(D), _rowspec(D_FF), _rowspec(D)],
        out_shape=[jax.ShapeDtypeStruct((R, D), F32), jax.ShapeDtypeStruct((R, D), F32),
                   jax.ShapeDtypeStruct((R, D_FF), BF16), jax.ShapeDtypeStruct((R, D), MXU)],
        compiler_params=_cp(56),
    )(x1, bm, nw, w1, w2)


def mlp_bwd(dx2, x1, mo, r, bm, nw, w2t, w1t):
    R = x1.shape[0]

    def body(dx2_ref, x1_ref, mo_ref, r_ref, bm_ref, nw_ref, w2t_ref, w1t_ref, dx1_ref, du_ref, dob_ref, part_ref,
             acc_ref):
        for s in range(TM // SB):
            rows = slice(s * SB, (s + 1) * SB)
            dob_ref[rows, :] = (dx2_ref[rows, :] * bm_ref[s, 5:6, :]).astype(dob_ref.dtype)
        for j in range(D_FF // FF_BLK):
            cols = slice(j * FF_BLK, (j + 1) * FF_BLK)
            du = _dot(dob_ref[...], w2t_ref[:, cols]) * (2.0 * r_ref[:, cols].astype(F32))
            du_ref[:, cols] = du.astype(du_ref.dtype)
            d = _dot(du_ref[:, cols], w1t_ref[cols, :])
            if j == 0:
                acc_ref[...] = d
            else:
                acc_ref[...] += d
        w = nw_ref[...]
        for s in range(TM // SB):
            rows = slice(s * SB, (s + 1) * SB)
            xhat, rstd = _rms_hat(x1_ref[rows, :])
            dh = acc_ref[rows, :]
            dx, dnw = _rms_bwd(dh * (1.0 + bm_ref[s, 4:5, :]), xhat, rstd, w)
            d2 = dx2_ref[rows, :]
            dx1_ref[rows, :] = d2 + dx
            part_ref[s] = jnp.concatenate(
                [_colsum(dh), _colsum(dh * xhat * w), _colsum(d2 * mo_ref[rows, :]), dnw,
                 jnp.zeros((4, D), F32)], axis=0)

    return pl.pallas_call(
        body, name="mlp_bwd", grid=(R // TM,),
        in_specs=[_rowspec(D), _rowspec(D), _rowspec(D), _rowspec(D_FF),
                  pl.BlockSpec((TM // SB, 8, D), lambda i: (i, 0, 0)), _fullspec((1, D)),
                  _resident((D, D_FF)), _resident((D_FF, D))],
        out_specs=[_rowspec(D), _rowspec(D_FF), _rowspec(D), pl.BlockSpec((TM // SB, 8, D), lambda i: (i, 0, 0))],
        out_shape=[jax.ShapeDtypeStruct((R, D), F32), jax.ShapeDtypeStruct((R, D_FF), MXU),
                   jax.ShapeDtypeStruct((R, D), MXU), jax.ShapeDtypeStruct((R // SB, 8, D), F32)],
        scratch_shapes=[pltpu.VMEM((TM, D), F32)],
        compiler_params=_cp(56),
    )(dx2, x1, mo, r, bm, nw, w2t, w1t)


def mm_tn(a, b, square_a=False, name="mm_tn", col_blocks=False):
    R, M = a.shape
    N = b.shape[1]
    tm = M if M <= 1408 else 1024
    tn = N if N <= 2176 else 1024
    tk = next((c for c in ((2176, 1088, 512) if tm + tn <= 2048 else (1088, 512)) if R % c == 0), R)
    assert not col_blocks or tm == M

    def body(a_ref, b_ref, o_ref):
        @pl.when(pl.program_id(2) == 0)
        def _():
            o_ref[...] = jnp.zeros_like(o_ref)

        av = a_ref[...]
        if square_a:
            av = av.astype(F32)
            av = (av * av).astype(MXU)
        prod = _dotg(av.astype(MXU), b_ref[...].astype(MXU), TN)
        if col_blocks:
            o_ref[0] += prod
        else:
            o_ref[...] += prod

    if col_blocks:
        out_spec = pl.BlockSpec((1, tm, tn), lambda i, j, k: (j, 0, 0))
        out_shape = jax.ShapeDtypeStruct((N // tn, M, tn), F32)
    else:
        out_spec = pl.BlockSpec((tm, tn), lambda i, j, k: (i, j))
        out_shape = jax.ShapeDtypeStruct((M, N), F32)
    return pl.pallas_call(
        body, name=name, grid=(M // tm, N // tn, R // tk),
        in_specs=[pl.BlockSpec((tk, tm), lambda i, j, k: (k, i)), pl.BlockSpec((tk, tn), lambda i, j, k: (k, j))],
        out_specs=out_spec, out_shape=out_shape,
        compiler_params=_cp(48),
    )(a, b)


def final_loss(x, tgt, fw, blocks_per_sample):
    R = x.shape[0]
    nxb = blocks_per_sample - 1

    def body(x_ref, t_ref, fw_ref, dx_ref, part_ref):
        i = pl.program_id(0)
        is_ctx = (i % blocks_per_sample) == 0
        xhat, rstd = _rms_hat(x_ref[...])
        w = fw_ref[...]
        err = xhat * w - t_ref[...]
        dx, dfw = _rms_bwd(err * (1.0 / D), xhat, rstd, w)
        keep = jnp.where(is_ctx, 0.0, 1.0)
        dx_ref[...] = dx * keep
        part_ref[0] = jnp.concatenate([dfw * keep, _colsum(err * err) * keep, jnp.zeros((6, D), F32)], axis=0)

    def tmap(i):
        return ((i // blocks_per_sample) * nxb + jnp.maximum(i % blocks_per_sample - 1, 0), 0)

    return pl.pallas_call(
        body, name="final_loss", grid=(R // SB,),
        in_specs=[_rowspec(D, SB), pl.BlockSpec((SB, D), tmap), _fullspec((1, D))],
        out_specs=[_rowspec(D, SB), pl.BlockSpec((1, 8, D), lambda i: (i, 0, 0))],
        out_shape=[jax.ShapeDtypeStruct((R, D), F32), jax.ShapeDtypeStruct((R // SB, 8, D), F32)],
    )(x, tgt, fw)


def _softplus(v):
    return jnp.maximum(v, 0.0) + jnp.log(1.0 + jnp.exp(-jnp.abs(v)))


def _conv_out(ext, cw_ref, cb_ref):
    return (cb_ref[...] + cw_ref[0:1, :] * _shift(ext, -1) + cw_ref[1:2, :] * _shift(ext, 0)
            + cw_ref[2:3, :] * _shift(ext, 1) + cw_ref[3:4, :] * _shift(ext, 2))


def _dt_dir(v, d):
    lane = lax.broadcasted_iota(jnp.int32, v.shape, 1)
    return jnp.where(lane < SSD_HEADS, pltpu.roll(v, (128 - DT0 - SSD_HEADS * d) % 128, axis=1), 0.0)


def ssd_prep(pxbc, plast, cw, cb, dtb, blocks_per_sample):
    R = pxbc.shape[0]
    prev, nxt = _halo_specs(XBC, R)

    def body(cur_ref, prev_ref, nxt_ref, pl_ref, cw_ref, cb_ref, dtb_ref, xs_ref, bm_ref, cm_ref, dt_ref):
        i = pl.program_id(0)
        ext = _ext_rows(cur_ref[...], prev_ref[...], nxt_ref[...], i, blocks_per_sample)
        co = _conv_out(ext, cw_ref, cb_ref)
        a = co * _sigmoid(co)
        xs_ref[...] = a[:, 0:384]
        bm_ref[...] = a[:, 384:640]
        cm_ref[...] = a[:, 640:896]
        sp = _softplus(pl_ref[...] + dtb_ref[...])
        dt_ref[0] = _dt_dir(sp, 0)
        dt_ref[1] = _dt_dir(sp, 1)

    return pl.pallas_call(
        body, name="ssd_prep", grid=(R // SB,),
        in_specs=[_rowspec(XBC, SB), prev, nxt, _rowspec(128, SB), _fullspec((8, XBC)), _fullspec((1, XBC)),
                  _fullspec((1, 128))],
        out_specs=[_rowspec(384, SB), _rowspec(256, SB), _rowspec(256, SB),
                   pl.BlockSpec((2, SB, 128), lambda i: (0, i, 0))],
        out_shape=[jax.ShapeDtypeStruct((R, 384), F32), jax.ShapeDtypeStruct((R, 256), F32),
                   jax.ShapeDtypeStruct((R, 256), F32), jax.ShapeDtypeStruct((2, R, 128), F32)],
    )(pxbc, pxbc, pxbc, plast, cw, cb, dtb)


def _chunk_index(d, s, nc):
    nctx = CTX // CHUNK
    back = jnp.where(s < nctx, nctx - 1 - s, nc + nctx - 1 - s)
    return jnp.where(d == 0, s, back)


def _scan_common(d, dt, arow, eexp, xs):
    ii = lax.broadcasted_iota(jnp.int32, (CHUNK, CHUNK), 0)
    jj = lax.broadcasted_iota(jnp.int32, (CHUNK, CHUNK), 1)
    mask = ((ii - jj) * (1 - 2 * d)) >= 0
    adt = dt * arow
    tmat = jnp.where(mask, 1.0, 0.0)
    cs = _dot_hi(tmat, adt, sel_first=True)
    tot = _colsum(adt)
    dtx = _dot_hi(dt, eexp)
    xt = xs * dtx
    ecs = jnp.exp(cs)
    ecx = _dot_hi(ecs, eexp)
    dte = jnp.exp(tot - cs)
    dtex = _dot_hi(dte, eexp)
    etot = jnp.exp(tot)
    etx = _dot_hi(jnp.broadcast_to(etot, (8, 128)), eexp)[0:1, :]
    return mask, tmat, adt, cs, tot, dtx, xt, ecs, ecx, dte, dtex, etot, etx


def _decay_matrix(mask, cs, cst, h):
    return jnp.exp(jnp.where(mask, cs[:, h:h + 1] - cst[h:h + 1, :], -1e30))


def ssd_scan_fwd(xs, bm, cm, dtv, arow, eexp, nb, T):
    R = xs.shape[0]
    nc = T // CHUNK

    def rowmap(b, d, s):
        return (b * nc + _chunk_index(d, s, nc), 0)

    def body(xs_ref, bm_ref, cm_ref, dt_ref, a_ref, e_ref, y_ref, hin_ref, st_ref):
        d = pl.program_id(1)
        s = pl.program_id(2)

        @pl.when(s == 0)
        def _():
            st_ref[...] = jnp.zeros_like(st_ref)

        eexp = e_ref[...]
        mask, _, _, cs, _, _, xt, _, ecx, _, dtex, _, etx = _scan_common(
            d, dt_ref[0], a_ref[0, 0:1, :], eexp, xs_ref[...])
        cst = cs.T
        sin = st_ref[...]
        hin_ref[0] = sin
        sb = sin.astype(MXU)
        xtb = xt.astype(MXU)
        xw = (xt * dtex).astype(MXU)
        g0 = lax.broadcasted_iota(jnp.int32, (CHUNK, SSD_INNER), 1) < 192
        lane = lax.broadcasted_iota(jnp.int32, (CHUNK, 128), 1)
        c = [cm_ref[:, 0:128].astype(MXU), cm_ref[:, 128:256].astype(MXU)]
        b = [bm_ref[:, 0:128].astype(MXU), bm_ref[:, 128:256].astype(MXU)]
        y = jnp.where(g0, _dot(c[0], sb), _dot(c[1], sb)) * ecx
        cb = [_dotg(c[0], b[0], NT), _dotg(c[1], b[1], NT)]
        blocks = []
        for blk in range(3):
            acc = None
            for hh in range(2):
                h = blk * 2 + hh
                m = (cb[h // 3] * _decay_matrix(mask, cs, cst, h)).astype(MXU)
                res = _dot(m, xtb[:, blk * 128:(blk + 1) * 128])
                acc = res if hh == 0 else jnp.where(lane < 64, acc, res)
            blocks.append(acc)
        y_ref[0] = y + jnp.concatenate(blocks, axis=1)
        st_ref[...] = sin * etx + jnp.where(g0, _dotg(b[0], xw, TN), _dotg(b[1], xw, TN))

    return pl.pallas_call(
        body, name="ssd_scan_fwd", grid=(nb, 2, nc),
        in_specs=[pl.BlockSpec((CHUNK, 384), rowmap), pl.BlockSpec((CHUNK, 256), rowmap),
                  pl.BlockSpec((CHUNK, 256), rowmap),
                  pl.BlockSpec((1, CHUNK, 128), lambda b, d, s: (d, b * nc + _chunk_index(d, s, nc), 0)),
                  pl.BlockSpec((1, 8, 128), lambda b, d, s: (d, 0, 0)),
                  pl.BlockSpec((128, 384), lambda b, d, s: (0, 0))],
        out_specs=[pl.BlockSpec((1, CHUNK, 384), lambda b, d, s: (d, b * nc + _chunk_index(d, s, nc), 0)),
                   pl.BlockSpec((1, CHUNK, 384), lambda b, d, s: ((b * 2 + d) * nc + _chunk_index(d, s, nc), 0, 0))],
        out_shape=[jax.ShapeDtypeStruct((2, R, 384), F32), jax.ShapeDtypeStruct((nb * 2 * nc, CHUNK, 384), F32)],
        scratch_shapes=[pltpu.VMEM((CHUNK, 384), F32)],
    )(xs, bm, cm, dtv, arow, eexp)


def ssd_scan_bwd(xs, bm, cm, dtv, arow, eexp, hin, dy, nb, T):
    R = xs.shape[0]
    nc = T // CHUNK

    def chunk(d, s):
        return _chunk_index(d, nc - 1 - s, nc)

    def rowmap(b, d, s):
        return (b * nc + chunk(d, s), 0)

    def dirmap(b, d, s):
        return (d, b * nc + chunk(d, s), 0)

    def body(xs_ref, bm_ref, cm_ref, dt_ref, a_ref, e_ref, hin_ref, dy_ref,
             dxs_ref, dbm_ref, dcm_ref, ddt_ref, da_ref, ds_ref):
        d = pl.program_id(1)
        s = pl.program_id(2)

        @pl.when(s == 0)
        def _():
            ds_ref[...] = jnp.zeros_like(ds_ref)
            da_ref[...] = jnp.zeros_like(da_ref)

        eexp = e_ref[...]
        dt = dt_ref[0]
        arow = a_ref[0, 0:1, :]
        xs_v = xs_ref[...]
        mask, tmat, adt, cs, tot, dtx, xt, ecs, ecx, dte, dtex, etot, etx = _scan_common(d, dt, arow, eexp, xs_v)
        cst = cs.T
        sin = hin_ref[0]
        sb = sin.astype(MXU)
        dsp = ds_ref[...]
        dyv = dy_ref[...]
        xtb = xt.astype(MXU)
        xw = (xt * dtex).astype(MXU)
        g0 = lax.broadcasted_iota(jnp.int32, (CHUNK, SSD_INNER), 1) < 192
        lane = lax.broadcasted_iota(jnp.int32, (CHUNK, 128), 1)
        sub = lax.broadcasted_iota(jnp.int32, (CHUNK, 128), 0)
        c = [cm_ref[:, 0:128].astype(MXU), cm_ref[:, 128:256].astype(MXU)]
        b = [bm_ref[:, 0:128].astype(MXU), bm_ref[:, 128:256].astype(MXU)]

        cs_prod = jnp.where(g0, _dot(c[0], sb), _dot(c[1], sb))
        dcsp = dyv * ecx
        dcsp_g = [jnp.where(g0, dcsp, 0.0).astype(MXU), jnp.where(g0, 0.0, dcsp).astype(MXU)]
        dcs = _dot_hi(dyv * cs_prod, eexp, NT) * ecs
        dc = [_dotg(dcsp_g[0], sb, NT), _dotg(dcsp_g[1], sb, NT)]
        dsin = _dotg(c[0], dcsp_g[0], TN) + _dotg(c[1], dcsp_g[1], TN) + dsp * etx

        detx = _colsum(dsp * sin)
        dtot = _dot_hi(jnp.broadcast_to(detx, (8, SSD_INNER)), eexp, NT)[0:1, :] * etot
        dsp_g = [jnp.where(g0, dsp, 0.0).astype(MXU), jnp.where(g0, 0.0, dsp).astype(MXU)]
        dxw = _dot(b[0], dsp_g[0]) + _dot(b[1], dsp_g[1])
        db = [_dotg(xw, dsp_g[0], NT), _dotg(xw, dsp_g[1], NT)]
        dxt = dxw * dtex
        ddte = _dot_hi(dxw * xt, eexp, NT) * dte
        dtot = dtot + _colsum(ddte)
        dcs = dcs - ddte

        cb = [_dotg(c[0], b[0], NT), _dotg(c[1], b[1], NT)]
        dg = [jnp.zeros((CHUNK, CHUNK), F32), jnp.zeros((CHUNK, CHUNK), F32)]
        dcs_rows = jnp.zeros((CHUNK, 128), F32)
        dxt_blocks = []
        for blk in range(3):
            acc = jnp.zeros((CHUNK, 128), F32)
            for hh in range(2):
                h = blk * 2 + hh
                g = h // 3
                mine = (lane < 64) if hh == 0 else (lane >= 64)
                dyh = jnp.where(mine, dyv[:, blk * 128:(blk + 1) * 128], 0.0).astype(MXU)
                lh = _decay_matrix(mask, cs, cst, h)
                m = cb[g] * lh
                dm = _dotg(dyh, xtb[:, blk * 128:(blk + 1) * 128], NT)
                acc = acc + _dotg(m.astype(MXU), dyh, TN)
                dg[g] = dg[g] + dm * lh
                q = dm * m
                dcs = dcs + jnp.where(lane == h, jnp.sum(q, axis=1, keepdims=True), 0.0)
                dcs_rows = dcs_rows - jnp.where(sub == h, jnp.sum(q, axis=0, keepdims=True), 0.0)
            dxt_blocks.append(acc)
        dxt = dxt + jnp.concatenate(dxt_blocks, axis=1)
        for g in range(2):
            dgb = dg[g].astype(MXU)
            dc[g] = dc[g] + _dot(dgb, b[g])
            db[g] = db[g] + _dotg(dgb, c[g], TN)
        dcs = dcs + dcs_rows.T

        dadt = _dot_hi(tmat, dcs, TN, sel_first=True) + dtot
        ddt = dadt * arow + _dot_hi(dxt * xs_v, eexp, NT)
        da_ref[0, 0:1, :] += _colsum(dadt * dt)
        dxs_ref[0] = dxt * dtx
        dbm_ref[0] = jnp.concatenate(db, axis=1)
        dcm_ref[0] = jnp.concatenate(dc, axis=1)
        ddt_ref[0] = ddt
        ds_ref[...] = dsin

    return pl.pallas_call(
        body, name="ssd_scan_bwd", grid=(nb, 2, nc),
        in_specs=[pl.BlockSpec((CHUNK, 384), rowmap), pl.BlockSpec((CHUNK, 256), rowmap),
                  pl.BlockSpec((CHUNK, 256), rowmap), pl.BlockSpec((1, CHUNK, 128), dirmap),
                  pl.BlockSpec((1, 8, 128), lambda b, d, s: (d, 0, 0)),
                  pl.BlockSpec((128, 384), lambda b, d, s: (0, 0)),
                  pl.BlockSpec((1, CHUNK, 384), lambda b, d, s: ((b * 2 + d) * nc + chunk(d, s), 0, 0)),
                  pl.BlockSpec((CHUNK, 384), rowmap)],
        out_specs=[pl.BlockSpec((1, CHUNK, 384), dirmap), pl.BlockSpec((1, CHUNK, 256), dirmap),
                   pl.BlockSpec((1, CHUNK, 256), dirmap), pl.BlockSpec((1, CHUNK, 128), dirmap),
                   pl.BlockSpec((1, 8, 128), lambda b, d, s: (b * 2 + d, 0, 0))],
        out_shape=[jax.ShapeDtypeStruct((2, R, 384), F32), jax.ShapeDtypeStruct((2, R, 256), F32),
                   jax.ShapeDtypeStruct((2, R, 256), F32), jax.ShapeDtypeStruct((2, R, 128), F32),
                   jax.ShapeDtypeStruct((nb * 2, 8, 128), F32)],
        scratch_shapes=[pltpu.VMEM((CHUNK, 384), F32)],
    )(xs, bm, cm, dtv, arow, eexp, hin, dy)


def _group_rms(g):
    lane = lax.broadcasted_iota(jnp.int32, g.shape, 1)
    g0 = lane < 192
    gg = g * g
    s0 = jnp.sum(jnp.where(g0, gg, 0.0), axis=-1, keepdims=True)
    s1 = jnp.sum(gg, axis=-1, keepdims=True) - s0
    rstd = jnp.where(g0, lax.rsqrt(s0 * (1.0 / 192) + EPS), lax.rsqrt(s1 * (1.0 / 192) + EPS))
    return rstd, g0


def ssd_out_fwd(y2, xs, pz, dexp, nw):
    R = xs.shape[0]

    def body(y_ref, xs_ref, z_ref, d_ref, nw_ref, o_ref):
        z = z_ref[...]
        yy = y_ref[0] + y_ref[1] + xs_ref[...] * d_ref[...]
        g = yy * (z * _sigmoid(z))
        rstd, _ = _group_rms(g)
        o_ref[...] = g * rstd * nw_ref[...]

    return pl.pallas_call(
        body, name="ssd_out_fwd", grid=(R // TM,),
        in_specs=[pl.BlockSpec((2, TM, 384), lambda i: (0, i, 0)), _rowspec(384), _rowspec(384),
                  _fullspec((1, 384)), _fullspec((1, 384))],
        out_specs=_rowspec(384),
        out_shape=jax.ShapeDtypeStruct((R, 384), F32),
    )(y2, xs, pz, dexp, nw)


def ssd_out_bwd(dout, y2, xs, pz, dexp, nw):
    R = xs.shape[0]

    def body(do_ref, y_ref, xs_ref, z_ref, d_ref, nw_ref, dy_ref, dz_ref, dxs_ref, part_ref):
        z = z_ref[...]
        xs_v = xs_ref[...]
        yy = y_ref[0] + y_ref[1] + xs_v * d_ref[...]
        sig = _sigmoid(z)
        sz = z * sig
        g = yy * sz
        rstd, g0 = _group_rms(g)
        ghat = g * rstd
        do = do_ref[...]
        dgn = do * nw_ref[...]
        t = dgn * ghat
        t0 = jnp.sum(jnp.where(g0, t, 0.0), axis=-1, keepdims=True)
        t1 = jnp.sum(t, axis=-1, keepdims=True) - t0
        dg = rstd * (dgn - ghat * jnp.where(g0, t0, t1) * (1.0 / 192))
        dyy = dg * sz
        dy_ref[...] = dyy
        dz_ref[...] = dg * yy * (sig * (1.0 + z * (1.0 - sig)))
        dxs_ref[...] = dyy * d_ref[...]
        part_ref[0] = jnp.concatenate([_colsum(do * ghat), _colsum(dyy * xs_v), jnp.zeros((6, 384), F32)], axis=0)

    return pl.pallas_call(
        body, name="ssd_out_bwd", grid=(R // TM,),
        in_specs=[_rowspec(384), pl.BlockSpec((2, TM, 384), lambda i: (0, i, 0)), _rowspec(384), _rowspec(384),
                  _fullspec((1, 384)), _fullspec((1, 384))],
        out_specs=[_rowspec(384), _rowspec(384), _rowspec(384), pl.BlockSpec((1, 8, 384), lambda i: (i, 0, 0))],
        out_shape=[jax.ShapeDtypeStruct((R, 384), F32), jax.ShapeDtypeStruct((R, 384), F32),
                   jax.ShapeDtypeStruct((R, 384), F32), jax.ShapeDtypeStruct((R // TM, 8, 384), F32)],
    )(dout, y2, xs, pz, dexp, nw)


def ssd_prep_bwd_a(pxbc, plast, cw, cb, dtb, dxs_skip, dxs2, dbm2, dcm2, ddt2, blocks_per_sample):
    R = pxbc.shape[0]
    prev, nxt = _halo_specs(XBC, R)

    def body(cur_ref, prev_ref, nxt_ref, pl_ref, cw_ref, cb_ref, dtb_ref, dsk_ref, dxs_ref, dbm_ref, dcm_ref, ddt_ref,
             dpre_ref, dlast_ref, part_ref):
        i = pl.program_id(0)
        ext = _ext_rows(cur_ref[...], prev_ref[...], nxt_ref[...], i, blocks_per_sample)
        co = _conv_out(ext, cw_ref, cb_ref)
        sig = _sigmoid(co)
        up = jnp.concatenate([dsk_ref[...] + dxs_ref[0] + dxs_ref[1], dbm_ref[0] + dbm_ref[1],
                              dcm_ref[0] + dcm_ref[1]], axis=1)
        dpre = up * (sig * (1.0 + co * (1.0 - sig)))
        dpre_ref[...] = dpre
        raw = pl_ref[...] + dtb_ref[...]
        lane = lax.broadcasted_iota(jnp.int32, raw.shape, 1)
        ddt = (pltpu.roll(ddt_ref[0], DT0, axis=1) + pltpu.roll(ddt_ref[1], DT0 + SSD_HEADS, axis=1))
        ddt = jnp.where(jnp.logical_and(lane >= DT0, lane < DT0 + 2 * SSD_HEADS), ddt * _sigmoid(raw), 0.0)
        dlast_ref[...] = ddt
        rows = [_colsum(dpre * _shift(ext, k - 1)) for k in range(4)]
        rows.append(_colsum(dpre))
        rows.append(jnp.concatenate([_colsum(ddt), jnp.zeros((1, XBC - 128), F32)], axis=1))
        rows.append(jnp.zeros((2, XBC), F32))
        part_ref[0] = jnp.concatenate(rows, axis=0)

    dirspec = lambda n: pl.BlockSpec((2, SB, n), lambda i: (0, i, 0))
    return pl.pallas_call(
        body, name="ssd_prep_bwd_a", grid=(R // SB,),
        in_specs=[_rowspec(XBC, SB), prev, nxt, _rowspec(128, SB), _fullspec((8, XBC)), _fullspec((1, XBC)),
                  _fullspec((1, 128)), _rowspec(384, SB), dirspec(384), dirspec(256), dirspec(256), dirspec(128)],
        out_specs=[_rowspec(XBC, SB), _rowspec(128, SB), pl.BlockSpec((1, 8, XBC), lambda i: (i, 0, 0))],
        out_shape=[jax.ShapeDtypeStruct((R, XBC), F32), jax.ShapeDtypeStruct((R, 128), F32),
                   jax.ShapeDtypeStruct((R // SB, 8, XBC), F32)],
    )(pxbc, pxbc, pxbc, plast, cw, cb, dtb, dxs_skip, dxs2, dbm2, dcm2, ddt2)


def ssd_prep_bwd_b(dpre, cw, blocks_per_sample):
    R = dpre.shape[0]
    prev, nxt = _halo_specs(XBC, R)

    def body(cur_ref, prev_ref, nxt_ref, cw_ref, o_ref):
        i = pl.program_id(0)
        ext = _ext_rows(cur_ref[...], prev_ref[...], nxt_ref[...], i, blocks_per_sample)
        o_ref[...] = (cw_ref[0:1, :] * _shift(ext, 1) + cw_ref[1:2, :] * _shift(ext, 0)
                      + cw_ref[2:3, :] * _shift(ext, -1) + cw_ref[3:4, :] * _shift(ext, -2))

    return pl.pallas_call(
        body, name="ssd_prep_bwd_b", grid=(R // SB,),
        in_specs=[_rowspec(XBC, SB), prev, nxt, _fullspec((8, XBC))],
        out_specs=_rowspec(XBC, SB),
        out_shape=jax.ShapeDtypeStruct((R, XBC), F32),
    )(dpre, dpre, dpre, cw)


def _rope(u, cos, sa, sb):
    return u * cos + pltpu.roll(u, 120, axis=1) * sa + pltpu.roll(u, 8, axis=1) * sb


def _rope_t(du, cos, sa, sb):
    return du * cos + pltpu.roll(du * sa, 8, axis=1) + pltpu.roll(du * sb, 120, axis=1)


def mla_prep(pqa, pkva, plast, qnw, kvnw, wq, wk, wv, cos, sa, sb):
    R = pqa.shape[0]

    def body(qa_ref, kva_ref, pl_ref, qnw_ref, kvnw_ref, wq_ref, wk_ref, wv_ref, cos_ref, sa_ref, sb_ref,
             q_ref, k_ref, v_ref, cq_ref, ckv_ref):
        cos_v, sa_v, sb_v = cos_ref[...], sa_ref[...], sb_ref[...]
        xq, _ = _rms_hat(qa_ref[...])
        cq_ref[...] = (xq * qnw_ref[...]).astype(cq_ref.dtype)
        xkv, _ = _rms_hat(kva_ref[...])
        ckv_ref[...] = (xkv * kvnw_ref[...]).astype(ckv_ref.dtype)
        q = _dot(cq_ref[...], wq_ref[...])
        kn = _dot(ckv_ref[...], wk_ref[...])
        v_ref[...] = _dot(ckv_ref[...], wv_ref[...]).astype(v_ref.dtype)
        lane = lax.broadcasted_iota(jnp.int32, (TM, HP), 1)
        rope_lanes = jnp.logical_and(lane >= QK_NOPE, lane < QK_DIM)
        kr = _rope(jnp.where(rope_lanes, pltpu.roll(pl_ref[...], QK_NOPE, axis=1), 0.0), cos_v, sa_v, sb_v)
        for h in range(MLA_HEADS):
            cols = slice(h * HP, (h + 1) * HP)
            q_ref[:, cols] = (_rope(q[:, cols], cos_v, sa_v, sb_v) * Q_SCALE).astype(q_ref.dtype)
            k_ref[:, cols] = (kn[:, cols] + kr).astype(k_ref.dtype)

    return pl.pallas_call(
        body, name="mla_prep", grid=(R // TM,),
        in_specs=[_rowspec(256), _rowspec(256), _rowspec(128), _fullspec((1, 256)), _fullspec((1, 256)),
                  _fullspec((256, QW)), _fullspec((256, QW)), _fullspec((256, QW)),
                  _rowspec(HP), _rowspec(HP), _rowspec(HP)],
        out_specs=[_rowspec(QW), _rowspec(QW), _rowspec(QW), _rowspec(256), _rowspec(256)],
        out_shape=[jax.ShapeDtypeStruct((R, QW), MXU)] * 3 + [jax.ShapeDtypeStruct((R, 256), MXU)] * 2,
    )(pqa, pkva, plast, qnw, kvnw, wq, wk, wv, cos, sa, sb)


def mla_prep_bwd(dq, dk, dv, pqa, pkva, qnw, kvnw, wqt, wkt, wvt, cos, sa, sb):
    R = pqa.shape[0]

    def body(dq_ref, dk_ref, dv_ref, qa_ref, kva_ref, qnw_ref, kvnw_ref, wqt_ref, wkt_ref, wvt_ref,
             cos_ref, sa_ref, sb_ref, dqa_ref, dkva_ref, dkr_ref, dql_ref, dkm_ref, dvb_ref, part_ref):
        cos_v, sa_v, sb_v = cos_ref[...], sa_ref[...], sb_ref[...]
        lane = lax.broadcasted_iota(jnp.int32, (TM, HP), 1)
        rope_lanes = jnp.logical_and(lane >= QK_NOPE, lane < QK_DIM)
        dkr = jnp.zeros((TM, HP), F32)
        for h in range(MLA_HEADS):
            cols = slice(h * HP, (h + 1) * HP)
            dql_ref[:, cols] = (_rope_t(dq_ref[:, cols], cos_v, sa_v, sb_v) * ATT_SCALE).astype(dql_ref.dtype)
            dkh = dk_ref[:, cols] * LN2
            dkm_ref[:, cols] = jnp.where(lane < QK_NOPE, dkh, 0.0).astype(dkm_ref.dtype)
            dkr = dkr + jnp.where(rope_lanes, dkh, 0.0)
        dvb_ref[...] = dv_ref[...].astype(dvb_ref.dtype)
        dkr = jnp.where(rope_lanes, _rope_t(dkr, cos_v, sa_v, sb_v), 0.0)
        dkr_ref[...] = pltpu.roll(dkr, HP - QK_NOPE, axis=1)
        xq, rq = _rms_hat(qa_ref[...])
        dqa, dqnw = _rms_bwd(_dot(dql_ref[...], wqt_ref[...]), xq, rq, qnw_ref[...])
        dqa_ref[...] = dqa
        xkv, rkv = _rms_hat(kva_ref[...])
        dckv = _dot(dkm_ref[...], wkt_ref[...]) + _dot(dvb_ref[...], wvt_ref[...])
        dkva, dkvnw = _rms_bwd(dckv, xkv, rkv, kvnw_ref[...])
        dkva_ref[...] = dkva
        part_ref[0] = jnp.concatenate([dqnw, dkvnw, jnp.zeros((6, 256), F32)], axis=0)

    return pl.pallas_call(
        body, name="mla_prep_bwd", grid=(R // TM,),
        in_specs=[_rowspec(QW), _rowspec(QW), _rowspec(QW), _rowspec(256), _rowspec(256), _fullspec((1, 256)),
                  _fullspec((1, 256)), _fullspec((QW, 256)), _fullspec((QW, 256)), _fullspec((QW, 256)),
                  _rowspec(HP), _rowspec(HP), _rowspec(HP)],
        out_specs=[_rowspec(256), _rowspec(256), _rowspec(128), _rowspec(QW), _rowspec(QW), _rowspec(QW),
                   pl.BlockSpec((1, 8, 256), lambda i: (i, 0, 0))],
        out_shape=[jax.ShapeDtypeStruct((R, 256), F32), jax.ShapeDtypeStruct((R, 256), F32),
                   jax.ShapeDtypeStruct((R, 128), F32)] + [jax.ShapeDtypeStruct((R, QW), MXU)] * 3
                  + [jax.ShapeDtypeStruct((R // TM, 8, 256), F32)],
    )(dq, dk, dv, pqa, pkva, qnw, kvnw, wqt, wkt, wvt, cos, sa, sb)


ATT_SCALE = QK_DIM ** -0.5
TQ = 256


LOG2E = 1.4426950408889634
LN2 = 0.6931471805599453
Q_SCALE = ATT_SCALE * LOG2E


def _key_chunks(T, n=2):
    units = T // 128
    sizes = [(units // n + (1 if i < units % n else 0)) * 128 for i in range(n)]
    return [(sum(sizes[:i]), sz) for i, sz in enumerate(sizes) if sz]


def attn_fwd(q, k, v, nb, T):
    R = q.shape[0]
    nq = T // TQ
    chunks = _key_chunks(T)

    def body(q_ref, k_ref, v_ref, o_ref, lse_ref):
        def part(lo, n):
            s = _dotg(q_ref[...], k_ref[lo:lo + n, :], NT)
            m = jnp.max(s, axis=-1, keepdims=True)
            p = jnp.exp2(s - m)
            return m, jnp.sum(p, axis=-1, keepdims=True), _dot(p.astype(MXU), v_ref[lo:lo + n, :])

        def finish(parts):
            m = parts[0][0]
            for pm, _, _ in parts[1:]:
                m = jnp.maximum(m, pm)
            l, o = 0.0, 0.0
            for pm, pl_, po in parts:
                a = jnp.exp2(pm - m)
                l = l + a * pl_
                o = o + a * po
            o_ref[...] = o / l
            lse_ref[...] = jnp.broadcast_to(m + jnp.log(l) * LOG2E, (TQ, HP))

        i = pl.program_id(2)
        pl.when(i == 0)(lambda: finish([part(0, CTX)]))
        pl.when(i > 0)(lambda: finish([part(lo, n) for lo, n in chunks]))

    qspec = pl.BlockSpec((TQ, HP), lambda b, h, i: (b * nq + i, h))
    kspec = pl.BlockSpec((T, HP), lambda b, h, i: (b, h))
    return pl.pallas_call(
        body, name="attn_fwd", grid=(nb, MLA_HEADS, nq),
        in_specs=[qspec, kspec, kspec], out_specs=[qspec, qspec],
        out_shape=[jax.ShapeDtypeStruct((R, QW), F32)] * 2,
        compiler_params=_cp(48),
    )(q, k, v)


def attn_bwd(q, k, v, o, lse, do, nb, T):
    R = q.shape[0]
    nq = T // TQ
    chunks = _key_chunks(T)

    def body(q_ref, k_ref, v_ref, o_ref, lse_ref, do_ref, dq_ref, dk_ref, dv_ref):
        i = pl.program_id(2)

        @pl.when(i == 0)
        def _():
            dk_ref[...] = jnp.zeros_like(dk_ref)
            dv_ref[...] = jnp.zeros_like(dv_ref)

        def run(chunks):
            qv = q_ref[...]
            dov = do_ref[...]
            dob = dov.astype(MXU)
            delta = jnp.sum(dov * o_ref[...], axis=-1, keepdims=True)
            lse_v = lse_ref[:, 0:1]
            dq = 0.0
            for lo, n in chunks:
                kv = k_ref[lo:lo + n, :]
                p = jnp.exp2(_dotg(qv, kv, NT) - lse_v)
                dp = _dotg(dob, v_ref[lo:lo + n, :], NT)
                dsb = (p * (dp - delta)).astype(MXU)
                dq = dq + _dot(dsb, kv)
                dk_ref[lo:lo + n, :] += _dotg(dsb, qv, TN)
                dv_ref[lo:lo + n, :] += _dotg(p.astype(MXU), dob, TN)
            dq_ref[...] = dq

        pl.when(i == 0)(lambda: run([(0, CTX)]))
        pl.when(i > 0)(lambda: run(chunks))

    qspec = pl.BlockSpec((TQ, HP), lambda b, h, i: (b * nq + i, h))
    kspec = pl.BlockSpec((T, HP), lambda b, h, i: (b, h))
    return pl.pallas_call(
        body, name="attn_bwd", grid=(nb, MLA_HEADS, nq),
        in_specs=[qspec, kspec, kspec, qspec, qspec, qspec],
        out_specs=[qspec, kspec, kspec],
        out_shape=[jax.ShapeDtypeStruct((R, QW), F32)] * 3,
        compiler_params=_cp(56),
    )(q, k, v, o, lse, do)


def _pool_geometry(i, blocks_per_sample, seq):
    j = i % blocks_per_sample
    n = jnp.where(j == 0, CTX, seq)
    t0 = jnp.where(j == 0, 0, (j - 1) * SB) - HALO
    lane = lax.broadcasted_iota(jnp.int32, (SB + 2 * HALO, POOL_DIM), 1)
    t = lax.broadcasted_iota(jnp.int32, (SB + 2 * HALO, POOL_DIM), 0) + t0
    wh = jnp.where(lane < 64, 1, jnp.where(lane < 128, 2, jnp.where(lane < 192, 4, 8)))
    cnt = jnp.minimum(t + wh, n) - jnp.maximum(t - wh, 0)
    return lane, 1.0 / jnp.maximum(cnt, 1).astype(F32)


def _by_window(lane, c2, c4, c8, c16):
    return jnp.where(lane < 64, c2, jnp.where(lane < 128, c4, jnp.where(lane < 192, c8, c16)))


def _window_sums(ext, lane, first):
    n = ext.shape[0]
    r = lambda a, s: pltpu.roll(a, s % n, axis=0)
    c2 = ext + r(ext, first)
    c4 = r(c2, 1) + r(c2, -1)
    c8 = r(c4, 2) + r(c4, -2)
    c16 = r(c8, 4) + r(c8, -4)
    return _by_window(lane, c2, c4, c8, c16)


def _pool_delta(ext, lane, inv):
    return (_window_sums(ext, lane, 1) * inv - ext)[HALO:HALO + SB, :]


def pool_fwd(ppool, wbd, scale, blocks_per_sample, seq):
    R = ppool.shape[0]
    prev, nxt = _halo_specs(POOL_DIM, R)

    def body(cur_ref, prev_ref, nxt_ref, w_ref, s_ref, o_ref):
        i = pl.program_id(0)
        ext = _ext_rows(cur_ref[...], prev_ref[...], nxt_ref[...], i, blocks_per_sample)
        lane, inv = _pool_geometry(i, blocks_per_sample, seq)
        dlt = _pool_delta(ext, lane, inv)
        o_ref[...] = _dot(dlt.astype(MXU), w_ref[...]) * s_ref[...]

    return pl.pallas_call(
        body, name="pool_fwd", grid=(R // SB,),
        in_specs=[_rowspec(POOL_DIM, SB), prev, nxt, _fullspec((POOL_DIM, POOL_DIM)), _fullspec((1, POOL_DIM))],
        out_specs=_rowspec(POOL_DIM, SB),
        out_shape=jax.ShapeDtypeStruct((R, POOL_DIM), F32),
    )(ppool, ppool, ppool, wbd, scale)


def pool_bwd(ppool, dpool, wbd, wbdt, scale, blocks_per_sample, seq):
    R = ppool.shape[0]
    prev, nxt = _halo_specs(POOL_DIM, R)

    def body(cur_ref, prev_ref, nxt_ref, dcur_ref, dprev_ref, dnxt_ref, w_ref, wt_ref, s_ref, du_ref, dw_ref, part_ref):
        i = pl.program_id(0)

        @pl.when(i == 0)
        def _():
            dw_ref[...] = jnp.zeros_like(dw_ref)

        ext = _ext_rows(cur_ref[...], prev_ref[...], nxt_ref[...], i, blocks_per_sample)
        lane, inv = _pool_geometry(i, blocks_per_sample, seq)
        dlt = _pool_delta(ext, lane, inv).astype(MXU)
        dy = dcur_ref[...]
        part_ref[0] = jnp.concatenate([_colsum(dy * _dot(dlt, w_ref[...])), jnp.zeros((7, POOL_DIM), F32)], axis=0)
        dyp = (dy * s_ref[...]).astype(MXU)
        dw_ref[...] += _dotg(dlt, dyp, TN)
        dext = _ext_rows(dy, dprev_ref[...], dnxt_ref[...], i, blocks_per_sample)
        dd = _dot((dext * s_ref[...]).astype(MXU), wt_ref[...])
        du_ref[...] = (_window_sums(dd * inv, lane, -1) - dd)[HALO:HALO + SB, :]

    return pl.pallas_call(
        body, name="pool_bwd", grid=(R // SB,),
        in_specs=[_rowspec(POOL_DIM, SB), prev, nxt, _rowspec(POOL_DIM, SB), prev, nxt,
                  _fullspec((POOL_DIM, POOL_DIM)), _fullspec((POOL_DIM, POOL_DIM)), _fullspec((1, POOL_DIM))],
        out_specs=[_rowspec(POOL_DIM, SB), _fullspec((POOL_DIM, POOL_DIM)),
                   pl.BlockSpec((1, 8, POOL_DIM), lambda i: (i, 0, 0))],
        out_shape=[jax.ShapeDtypeStruct((R, POOL_DIM), F32), jax.ShapeDtypeStruct((POOL_DIM, POOL_DIM), F32),
                   jax.ShapeDtypeStruct((R // SB, 8, POOL_DIM), F32)],
    )(ppool, ppool, ppool, dpool, dpool, dpool, wbd, wbdt, scale)


def adamw(w, g, m, v, name="adamw"):
    rows, cols = w.shape
    tr = rows
    for cand in (512, 256, 128, 64, 32, 16, 8):
        if rows % cand == 0:
            tr = cand
            break
    bc1 = 1.0 - ADAM_B1 ** ADAM_STEP
    bc2 = 1.0 - ADAM_B2 ** ADAM_STEP

    def body(w_ref, g_ref, m_ref, v_ref, d_ref, nm_ref, nv_ref):
        g_v = g_ref[...]
        nm = ADAM_B1 * m_ref[...] + (1.0 - ADAM_B1) * g_v
        nv = ADAM_B2 * v_ref[...] + (1.0 - ADAM_B2) * (g_v * g_v)
        nm_ref[...] = nm
        nv_ref[...] = nv
        d_ref[...] = -ADAM_LR * ((nm / bc1) / (jnp.sqrt(nv / bc2) + ADAM_EPS) + ADAM_WD * w_ref[...])

    spec = pl.BlockSpec((tr, cols), lambda i: (i, 0))
    return pl.pallas_call(
        body, name=name, grid=(rows // tr,),
        in_specs=[spec] * 4, out_specs=[spec] * 3,
        out_shape=[jax.ShapeDtypeStruct((rows, cols), F32)] * 3,
    )(w, g, m, v)


MODR = 32


def _silu(v):
    return v * _sigmoid(v)


def mod_fwd(cond, w, b):
    n = w.shape[1]

    def body(c_ref, w_ref, b_ref, o_ref):
        o_ref[...] = _dot(_silu(c_ref[...]).astype(MXU), w_ref[...].astype(MXU)) + b_ref[...]

    return pl.pallas_call(
        body, name="mod_fwd", out_shape=jax.ShapeDtypeStruct((MODR, n), F32),
        in_specs=[_fullspec((MODR, D)), _fullspec((D, n)), _fullspec((1, n))], out_specs=_fullspec((MODR, n)),
        grid=(1,), compiler_params=_cp(40),
    )(cond, w, b)


def mod_wgrad(cond, dm):
    n = dm.shape[1]

    def body(c_ref, d_ref, o_ref):
        o_ref[...] = _dotg(_silu(c_ref[...]).astype(MXU), d_ref[...].astype(MXU), TN)

    return pl.pallas_call(
        body, name="mod_wgrad", out_shape=jax.ShapeDtypeStruct((D, n), F32),
        in_specs=[_fullspec((MODR, D)), _fullspec((MODR, n))], out_specs=_fullspec((D, n)),
        grid=(1,), compiler_params=_cp(40),
    )(cond, dm)


def mod_dgrad(dm, w):
    n = w.shape[1]

    def body(d_ref, w_ref, o_ref):
        o_ref[...] = _dotg(d_ref[...].astype(MXU), w_ref[...].astype(MXU), NT)

    return pl.pallas_call(
        body, name="mod_dgrad", out_shape=jax.ShapeDtypeStruct((8, D), F32),
        in_specs=[_fullspec((8, n)), _fullspec((D, n))], out_specs=_fullspec((8, D)),
        grid=(1,), compiler_params=_cp(40),
    )(dm, w)


def sum_leading(a, name="sum_leading"):
    n, r, c = a.shape

    def body(a_ref, o_ref):
        acc = a_ref[0]
        for k in range(1, n):
            acc = acc + a_ref[k]
        o_ref[...] = acc

    return pl.pallas_call(
        body, name=name, out_shape=jax.ShapeDtypeStruct((r, c), F32),
        in_specs=[_fullspec((n, r, c))], out_specs=_fullspec((r, c)), grid=(1,),
    )(a)


MESH = pl.DeviceIdType.MESH
NDEV = 8
ANY = pl.BlockSpec(memory_space=pl.ANY)


def _place():
    return lax.axis_index("x"), lax.axis_index("y"), lax.axis_index("c")


def _other_chips(x, y):
    return [(1 - x, y), (x, 1 - y), (1 - x, 1 - y)]


def allgather_small(v, name):
    r, cols = v.shape

    def body(v_ref, o_ref, send_sems, recv_sems):
        x, y, c = _place()
        me = 4 * x + 2 * y + c
        o_ref[me] = v_ref[...]
        copies = []
        for rel in range(1, NDEV):
            peer = (1 - x if rel & 4 else x, 1 - y if rel & 2 else y, 1 - c if rel & 1 else c)
            cp = pltpu.make_async_remote_copy(src_ref=v_ref, dst_ref=o_ref.at[me], send_sem=send_sems.at[rel - 1],
                                              recv_sem=recv_sems.at[rel - 1], device_id=peer, device_id_type=MESH)
            cp.start()
            copies.append(cp)
        for cp in copies:
            cp.wait_recv()
        for cp in copies:
            cp.wait_send()

    return pl.pallas_call(
        body, name=name, out_shape=jax.ShapeDtypeStruct((NDEV, r, cols), F32),
        in_specs=[pl.BlockSpec(memory_space=pltpu.VMEM)], out_specs=pl.BlockSpec(memory_space=pltpu.VMEM),
        scratch_shapes=[pltpu.SemaphoreType.DMA((NDEV - 1,)), pltpu.SemaphoreType.DMA((NDEV - 1,))],
        compiler_params=_cp(40),
    )(v)


def _sems(n):
    return [pltpu.SemaphoreType.DMA((n,)), pltpu.SemaphoreType.DMA((n,))]


def gather_shards(arrs):
    n = len(arrs)

    def body(*refs):
        srcs, outs = refs[:n], refs[n:2 * n]
        send_sems, recv_sems = refs[2 * n:]
        x, y, c = _place()
        k = 2 * x + y
        sib = (x, y, 1 - c)
        chips = _other_chips(x, y)

        def half(i, kk, cc):
            hr = arrs[i].shape[1] // 2
            return outs[i].at[kk, :, pl.ds(cc * hr, hr), :]

        def copy(i, slot, kk, cc, to, src=None):
            return pltpu.make_async_remote_copy(src_ref=half(i, kk, cc) if src is None else src, dst_ref=half(i, kk, cc),
                                                send_sem=send_sems.at[slot * n + i], recv_sem=recv_sems.at[slot * n + i],
                                                device_id=to, device_id_type=MESH)

        started = []
        for j, (px, py) in enumerate(chips):
            for i in range(n):
                hr = arrs[i].shape[1] // 2
                cp = copy(i, j, k, c, (px, py, c), src=srcs[i].at[:, pl.ds(c * hr, hr), :])
                cp.start()
                started.append(cp)
        for j, (px, py) in enumerate(chips):
            for i in range(n):
                copy(i, j, 2 * px + py, c, (px, py, c)).wait_recv()
                cp = copy(i, 3 + j, 2 * px + py, c, sib)
                cp.start()
                started.append(cp)
        for j, (px, py) in enumerate(chips):
            for i in range(n):
                copy(i, 3 + j, 2 * px + py, 1 - c, sib).wait_recv()
        for cp in started:
            cp.wait_send()

    return pl.pallas_call(
        body, name="gather_shards", out_shape=[jax.ShapeDtypeStruct((4,) + a.shape, a.dtype) for a in arrs],
        in_specs=[ANY] * n, out_specs=[ANY] * n, scratch_shapes=_sems(6 * n),
    )(*arrs)


def swap_core_halves(gs):
    n = len(gs)

    def body(*refs):
        srcs, outs = refs[:n], refs[n:2 * n]
        send_sems, recv_sems = refs[2 * n:]
        x, y, c = _place()
        copies = []
        for i in range(n):
            hr = gs[i].shape[1] // 2
            cp = pltpu.make_async_remote_copy(src_ref=srcs[i].at[:, pl.ds((1 - c) * hr, hr), :], dst_ref=outs[i],
                                              send_sem=send_sems.at[i], recv_sem=recv_sems.at[i],
                                              device_id=(x, y, 1 - c), device_id_type=MESH)
            cp.start()
            copies.append(cp)
        for cp in copies:
            cp.wait()

    return pl.pallas_call(
        body, name="swap_core_halves",
        out_shape=[jax.ShapeDtypeStruct((4, g.shape[1] // 2, g.shape[2]), g.dtype) for g in gs],
        in_specs=[ANY] * n, out_specs=[ANY] * n, scratch_shapes=_sems(n),
    )(*gs)


def add_half(g, r1, cidx, name):
    _, rows, cols = g.shape
    hr = rows // 2

    def body(c_ref, g_ref, r_ref, o_ref, ob_ref):
        s = g_ref[...] + r_ref[...]
        o_ref[...] = s
        ob_ref[...] = s.astype(BF16)

    blk = lambda f: pl.BlockSpec((1, hr, cols), f)
    return pl.pallas_call(
        body, name=name,
        out_shape=[jax.ShapeDtypeStruct((4, hr, cols), F32), jax.ShapeDtypeStruct((4, hr, cols), BF16)],
        grid_spec=pltpu.PrefetchScalarGridSpec(
            num_scalar_prefetch=1, grid=(4,),
            in_specs=[blk(lambda k, c_ref: (k, c_ref[0], 0)), blk(lambda k, c_ref: (k, 0, 0))],
            out_specs=[blk(lambda k, c_ref: (k, 0, 0)), blk(lambda k, c_ref: (k, 0, 0))]),
    )(cidx, g, r1)


def swap_chip_parts(ss):
    n = len(ss)

    def body(*refs):
        srcs, outs = refs[:n], refs[n:2 * n]
        send_sems, recv_sems = refs[2 * n:]
        x, y, c = _place()
        copies = []
        for j, (px, py) in enumerate(_other_chips(x, y)):
            for i in range(n):
                cp = pltpu.make_async_remote_copy(src_ref=srcs[i].at[2 * px + py], dst_ref=outs[i].at[j],
                                                  send_sem=send_sems.at[j * n + i], recv_sem=recv_sems.at[j * n + i],
                                                  device_id=(px, py, c), device_id_type=MESH)
                cp.start()
                copies.append(cp)
        for cp in copies:
            cp.wait()

    return pl.pallas_call(
        body, name="swap_chip_parts", out_shape=[jax.ShapeDtypeStruct((3,) + s.shape[1:], s.dtype) for s in ss],
        in_specs=[ANY] * n, out_specs=[ANY] * n, scratch_shapes=_sems(3 * n),
    )(*ss)


def sum_parts(s1, r2, kidx, name):
    _, hr, cols = s1.shape

    def body(k_ref, s_ref, r_ref, o_ref):
        o_ref[...] = ((s_ref[0] + r_ref[0].astype(F32)) + r_ref[1].astype(F32)) + r_ref[2].astype(F32)

    return pl.pallas_call(
        body, name=name, out_shape=jax.ShapeDtypeStruct((hr, cols), F32),
        grid_spec=pltpu.PrefetchScalarGridSpec(
            num_scalar_prefetch=1, grid=(1,),
            in_specs=[pl.BlockSpec((1, hr, cols), lambda i, k_ref: (k_ref[0], 0, 0)),
                      pl.BlockSpec((3, hr, cols), lambda i, k_ref: (0, 0, 0))],
            out_specs=pl.BlockSpec((hr, cols), lambda i, k_ref: (0, 0))),
    )(kidx, s1, r2)


def swap_reduced_halves(hs):
    n = len(hs)

    def body(*refs):
        srcs, outs = refs[:n], refs[n:2 * n]
        send_sems, recv_sems = refs[2 * n:]
        x, y, c = _place()
        copies = []
        for i in range(n):
            cp = pltpu.make_async_remote_copy(src_ref=srcs[i], dst_ref=outs[i], send_sem=send_sems.at[i],
                                              recv_sem=recv_sems.at[i], device_id=(x, y, 1 - c), device_id_type=MESH)
            cp.start()
            copies.append(cp)
        for cp in copies:
            cp.wait()

    return pl.pallas_call(
        body, name="swap_reduced_halves", out_shape=[jax.ShapeDtypeStruct(h.shape, h.dtype) for h in hs],
        in_specs=[ANY] * n, out_specs=[ANY] * n, scratch_shapes=_sems(n),
    )(*hs)


def adamw_halves(w, m, v, own, oth, cidx, name):
    depth, rows, cols = w.shape
    hr = rows // 2
    tr = min(hr, 256)
    nblk = hr // tr
    bc1 = 1.0 - ADAM_B1 ** ADAM_STEP
    bc2 = 1.0 - ADAM_B2 ** ADAM_STEP

    def body(c_ref, w_ref, m_ref, v_ref, own0, own1, oth0, oth1, g_ref, d_ref, nm_ref, nv_ref):
        l = pl.program_id(0)
        hi = pl.program_id(1)
        mine = jnp.where(l == 0, own0[...], own1[...])
        other = jnp.where(l == 0, oth0[...], oth1[...])
        g_v = jnp.where(hi == c_ref[0], mine, other)
        nm = ADAM_B1 * m_ref[0] + (1.0 - ADAM_B1) * g_v
        nv = ADAM_B2 * v_ref[0] + (1.0 - ADAM_B2) * (g_v * g_v)
        g_ref[0] = g_v
        nm_ref[0] = nm
        nv_ref[0] = nv
        d_ref[0] = -ADAM_LR * ((nm / bc1) / (jnp.sqrt(nv / bc2) + ADAM_EPS) + ADAM_WD * w_ref[0])

    wspec = pl.BlockSpec((1, tr, cols), lambda l, hi, b, c_ref: (l, hi * nblk + b, 0))
    gspec = pl.BlockSpec((tr, cols), lambda l, hi, b, c_ref: (b, 0))
    assert depth == 2
    return pl.pallas_call(
        body, name=name, out_shape=[jax.ShapeDtypeStruct(w.shape, F32)] * 4,
        grid_spec=pltpu.PrefetchScalarGridSpec(
            num_scalar_prefetch=1, grid=(depth, 2, nblk),
            in_specs=[wspec] * 3 + [gspec] * 4, out_specs=[wspec] * 4),
    )(cidx, w, m, v, own[0], own[1], oth[0], oth[1])


class _NS:
    def __init__(self, **kw):
        self.__dict__.update(kw)


def _prep_layer(win, wqb, wkvb, wout, w1, w2, conv_w, conv_b, dt_bias, a_log, ssd_d, ssd_nw, qnw, kvnw, pool_w,
                pool_scale, n1, n2):
    winp = jnp.concatenate([win[:, 0:384], win[:, 384:1280], win[:, 1292:1548], win[:, 1548:1804], win[:, 1836:2092],
                            win[:, 1804:1836], win[:, 1280:1292], jnp.zeros((D, NP - IN_COLS), win.dtype)], axis=1)
    wq = jnp.pad(wqb.reshape(256, MLA_HEADS, QK_DIM), ((0, 0), (0, 0), (0, HP - QK_DIM))).reshape(256, QW)
    kv3 = wkvb.reshape(256, MLA_HEADS, 128)
    wk = jnp.pad(kv3[:, :, :64], ((0, 0), (0, 0), (0, 64))).reshape(256, QW)
    wv = jnp.pad(kv3[:, :, 64:], ((0, 0), (0, 0), (0, 64))).reshape(256, QW)
    wo = jnp.concatenate([jnp.pad(wout[384:768].reshape(MLA_HEADS, 64, D), ((0, 0), (0, 64), (0, 0))).reshape(QW, D),
                          wout[0:384], wout[768:1024]], axis=0)
    wbd = (jnp.eye(4, dtype=F32)[:, None, :, None] * pool_w[:, :, None, :]).reshape(POOL_DIM, POOL_DIM).astype(MXU)
    a = -jnp.exp(a_log)
    return _NS(
        winp=winp, wint=winp.T, wq=wq, wqt=wq.T, wk=wk, wkt=wk.T, wv=wv, wvt=wv.T, wo=wo, wot=wo.T,
        w1=w1, w1t=w1.T, w2=w2, w2t=w2.T, wbd=wbd, wbdt=wbd.T,
        cw8=jnp.pad(conv_w, ((0, 4), (0, 0))), cb=conv_b[None],
        dtb=jnp.pad(dt_bias.reshape(1, 12), ((0, 0), (DT0, 128 - DT0 - 12))),
        arow=jnp.pad(a[:, None, :], ((0, 0), (0, 7), (0, 128 - SSD_HEADS))), a=a,
        dexp=jnp.repeat(ssd_d, SSD_P)[None], ssd_nw=ssd_nw[None], qnw=qnw[None], kvnw=kvnw[None],
        pscale=pool_scale[None], n1=n1[None], n2=n2[None])


def _unprep_grads(dwinp, dwq, dwk, dwv, dwo):
    dwin = jnp.concatenate([dwinp[:, 0:384], dwinp[:, 384:1280], dwinp[:, 2080:2092], dwinp[:, 1280:1536],
                            dwinp[:, 1536:1792], dwinp[:, 2048:2080], dwinp[:, 1792:2048]], axis=1)
    dwqb = dwq.reshape(256, MLA_HEADS, HP)[:, :, :QK_DIM].reshape(256, MLA_HEADS * QK_DIM)
    dwkvb = jnp.concatenate([dwk.reshape(256, MLA_HEADS, HP)[:, :, :64], dwv.reshape(256, MLA_HEADS, HP)[:, :, :64]],
                            axis=2).reshape(256, MLA_HEADS * 128)
    dwout = jnp.concatenate([dwo[QW:QW + 384], dwo[0:QW].reshape(MLA_HEADS, HP, D)[:, :64].reshape(384, D),
                             dwo[QW + 384:CAT]], axis=0)
    return dwin, dwqb, dwkvb, dwout


def _rope_tables(nb, N):
    t = jnp.arange(N, dtype=F32)
    row = jnp.floor(t / GRID_W)
    col = t - row * GRID_W
    inv = jnp.asarray(10000.0 ** (-np.arange(8, dtype=np.float32) / 8), F32)
    ang = jnp.stack([row[:, None] * inv, col[:, None] * inv], axis=1)
    cs, sn = jnp.cos(ang), jnp.sin(ang)
    zero = jnp.zeros_like(sn)
    lanes = lambda first, second: jnp.stack([first, second], axis=2).reshape(N, 32)
    pad = lambda a, fill: jnp.concatenate([jnp.full((N, 64), fill, F32), a, jnp.full((N, 32), fill, F32)], axis=1)
    tabs = []
    for tab, fill in ((pad(lanes(cs, cs), 1.0), 1.0), (pad(lanes(-sn, zero), 0.0), 0.0), (pad(lanes(zero, sn), 0.0), 0.0)):
        one = jnp.concatenate([jnp.full((CTX, 128), fill, F32), tab], axis=0)
        tabs.append(jnp.tile(one, (nb, 1)))
    return tabs


def _eexp():
    e = np.zeros((128, SSD_INNER), np.float32)
    for h in range(SSD_HEADS):
        e[h, h * SSD_P:(h + 1) * SSD_P] = 1.0
    return jnp.asarray(e)


def _layer_fwd(X, bm, lw, cst):
    nb, T, bps, N = cst.nb, cst.T, cst.bps, cst.N
    h1, pz, pxbc, pqa, pkva, ppool, plast = in_proj(X, bm, lw.n1, lw.winp)
    xs, bmat, cmat, dtv = ssd_prep(pxbc, plast, lw.cw8, lw.cb, lw.dtb, bps)
    y2, hin = ssd_scan_fwd(xs, bmat, cmat, dtv, lw.arow, cst.eexp, nb, T)
    ssd = ssd_out_fwd(y2, xs, pz, lw.dexp, lw.ssd_nw)
    q, k, v, cq, ckv = mla_prep(pqa, pkva, plast, lw.qnw, lw.kvnw, lw.wq, lw.wk, lw.wv, *cst.rope)
    attn, lse = attn_fwd(q, k, v, nb, T)
    pool = pool_fwd(ppool, lw.wbd, lw.pscale, bps, N)
    x1, mix, cat = mix_fwd(X, attn, ssd, pool, bm, lw.wo)
    x2, mo, r, h2 = mlp_fwd(x1, bm, lw.n2, lw.w1, lw.w2)
    sv = _NS(X=X, h1=h1, pz=pz, pxbc=pxbc, pqa=pqa, pkva=pkva, ppool=ppool, plast=plast, xs=xs, bmat=bmat, cmat=cmat,
             dtv=dtv, y2=y2, hin=hin, q=q, k=k, v=v, cq=cq, ckv=ckv, attn=attn, lse=lse, x1=x1, mix=mix, cat=cat, mo=mo, r=r,
             h2=h2)
    return x2, sv


def _layer_bwd(dx2, bm, lw, sv, cst):
    nb, T, bps, N = cst.nb, cst.T, cst.bps, cst.N
    dx1, du, dob, part_mlp = mlp_bwd(dx2, sv.x1, sv.mo, sv.r, bm, lw.n2, lw.w2t, lw.w1t)
    dw1 = mm_tn(sv.h2, du, name="wgrad_mlp1", col_blocks=True)
    dw2 = mm_tn(sv.r, dob, square_a=True, name="wgrad_mlp2")
    dattn, dssd, dpool, dmb, part_mix = mix_bwd(dx1, sv.mix, bm, lw.wot)
    dwo = mm_tn(sv.cat, dmb, name="wgrad_out")
    dppool, dwbd, part_pool = pool_bwd(sv.ppool, dpool, lw.wbd, lw.wbdt, lw.pscale, bps, N)
    dq, dk, dv = attn_bwd(sv.q, sv.k, sv.v, sv.attn, sv.lse, dattn, nb, T)
    dpqa, dpkva, dkr, dql, dkm, dvb, part_mla = mla_prep_bwd(dq, dk, dv, sv.pqa, sv.pkva, lw.qnw, lw.kvnw, lw.wqt,
                                                             lw.wkt, lw.wvt, *cst.rope)
    dwq = mm_tn(sv.cq, dql, name="wgrad_q")
    dwk = mm_tn(sv.ckv, dkm, name="wgrad_k")
    dwv = mm_tn(sv.ckv, dvb, name="wgrad_v")
    dyy, dz, dxs_skip, part_so = ssd_out_bwd(dssd, sv.y2, sv.xs, sv.pz, lw.dexp, lw.ssd_nw)
    dxs2, dbm2, dcm2, ddt2, da = ssd_scan_bwd(sv.xs, sv.bmat, sv.cmat, sv.dtv, lw.arow, cst.eexp, sv.hin, dyy, nb, T)
    dpre, dlast_dt, part_conv = ssd_prep_bwd_a(sv.pxbc, sv.plast, lw.cw8, lw.cb, lw.dtb, dxs_skip, dxs2, dbm2, dcm2,
                                               ddt2, bps)
    dpxbc = ssd_prep_bwd_b(dpre, lw.cw8, bps)
    dx, dpb, part_in = in_proj_bwd(dx1, sv.X, dz, dpxbc, dpqa, dpkva, dppool, dkr, dlast_dt, bm, lw.n1, lw.wint)
    dwinp = mm_tn(sv.h1, dpb, name="wgrad_in")

    dwin, dwqb, dwkvb, dwout = _unprep_grads(dwinp, dwq, dwk, dwv, dwo)
    dmod = jnp.stack([part_in[:, 0], part_in[:, 1], part_mix[:, 0], part_mlp[:, 0], part_mlp[:, 1], part_mlp[:, 2]],
                     axis=1)
    dmod = dmod.reshape(nb, bps, 6, D)
    dm_rows = jnp.concatenate([jnp.sum(dmod[:, 1:], axis=1), jnp.sum(dmod[:, 0], axis=0)[None]], axis=0)
    da_dh = jnp.sum(da.reshape(nb, 2, 8, 128)[:, :, 0, :SSD_HEADS], axis=0)
    conv_parts = jnp.sum(part_conv, axis=0)
    by_chip_cols = lambda a: jnp.stack([a[:, k * (a.shape[1] // 4):(k + 1) * (a.shape[1] // 4)] for k in range(4)])
    by_chip_rows = lambda a: a.reshape(4, a.shape[0] // 4, a.shape[1])
    g = _NS(
        w_in=by_chip_cols(dwin), w_q_b=by_chip_cols(dwqb), w_kv_b=by_chip_cols(dwkvb), w_out=by_chip_rows(dwout),
        w_mlp1=dw1, w_mlp2=by_chip_rows(dw2),
        dm_rows=dm_rows.reshape(3, 6 * D),
        norm1_w=jnp.sum(part_in[:, 2], axis=0), norm2_w=jnp.sum(part_mlp[:, 3], axis=0),
        conv_w=conv_parts[0:4], conv_b=conv_parts[4],
        dt_bias=conv_parts[5, DT0:DT0 + 12].reshape(2, SSD_HEADS), a_log=da_dh * lw.a,
        ssd_d=jnp.sum(jnp.sum(part_so[:, 1], axis=0).reshape(SSD_HEADS, SSD_P), axis=1),
        ssd_norm_w=jnp.sum(part_so[:, 0], axis=0),
        q_a_norm_w=jnp.sum(part_mla[:, 0], axis=0), kv_a_norm_w=jnp.sum(part_mla[:, 1], axis=0),
        pool_w=jnp.stack([dwbd[i * 64:(i + 1) * 64, i * 64:(i + 1) * 64] for i in range(4)]),
        pool_scale=jnp.sum(part_pool[:, 0], axis=0))
    return dx, g


def _local_step(x, ctx, tgt, bms, lws, fw, cst):
    nb, N = x.shape[0], x.shape[1]
    R = nb * cst.T
    X = jnp.concatenate([ctx, x], axis=1).reshape(R, D)
    saved = []
    for l in range(DEPTH):
        X, sv = _layer_fwd(X, bms[l], lws[l], cst)
        saved.append(sv)
    dX, part_fin = final_loss(X, tgt.reshape(nb * N, D), fw[None], cst.bps)
    loss = (0.5 / D) * jnp.sum(part_fin[:, 1])
    dfw = jnp.sum(part_fin[:, 0], axis=0)
    grads = [None] * DEPTH
    for l in reversed(range(DEPTH)):
        dX, grads[l] = _layer_bwd(dX, bms[l], lws[l], saved[l], cst)
    grad_x = dX.reshape(nb, cst.T, D)[:, CTX:, :]
    return loss, grad_x, grads, dfw


def _consts(nb, N):
    T = CTX + N
    bps = T // SB
    return _NS(nb=nb, N=N, T=T, bps=bps, eexp=_eexp(), rope=_rope_tables(nb, N))


def _block_mod(modrows, cst):
    rows = []
    for b in range(cst.nb):
        rows.append(modrows[cst.nb:cst.nb + 1])
        rows.append(jnp.broadcast_to(modrows[b:b + 1], (cst.bps - 1, 6, D)))
    return jnp.pad(jnp.concatenate(rows, axis=0), ((0, 0), (0, 2), (0, 0)))


SMALL = (("norm1_w", (2, D)), ("norm2_w", (2, D)), ("conv_w", (2, 4, XBC)), ("conv_b", (2, XBC)),
         ("dt_bias", (2, 2, 6)), ("a_log", (2, 2, 6)), ("ssd_d", (2, 6)), ("ssd_norm_w", (2, 384)),
         ("q_a_norm_w", (2, 256)), ("kv_a_norm_w", (2, 256)), ("pool_w", (2, 4, 64, 64)), ("pool_scale", (2, 256)),
         ("final_norm_w", (D,)), ("mod_b", (2, 6 * D)))
SMALL_ROWS = 64
DM_ROWS = 48


def _pack_small(vals):
    flat = jnp.concatenate([vals[n].reshape(-1) for n, _ in SMALL])
    return jnp.pad(flat, (0, SMALL_ROWS * D - flat.shape[0])).reshape(SMALL_ROWS, D)


def _unpack_small(p):
    flat = p.reshape(-1)
    out, off = {}, 0
    for n, shp in SMALL:
        size = int(np.prod(shp))
        out[n] = flat[off:off + size].reshape(shp)
        off += size
    return out


def cctx_grad(parts, c_ctx):
    def body(p_ref, c_ref, o_ref):
        acc = ((p_ref[0] + p_ref[1]) + p_ref[2]) + p_ref[3]
        v = c_ref[...]
        sig = _sigmoid(v)
        o_ref[...] = acc * (sig * (1.0 + v * (1.0 - sig)))

    return pl.pallas_call(
        body, name="cctx_grad", out_shape=jax.ShapeDtypeStruct((8, D), F32),
        in_specs=[_fullspec((4, 8, D)), _fullspec((1, D))], out_specs=_fullspec((8, D)), grid=(1,),
    )(parts, c_ctx)


def kernel(x, c, ctx, c_ctx, mod_w, mod_b, norm1_w, norm2_w, w_in, conv_w, conv_b, dt_bias, a_log, ssd_d, ssd_norm_w, q_a_norm_w, w_q_b, kv_a_norm_w, w_kv_b, pool_w, pool_scale, w_out, w_mlp1, w_mlp2, final_norm_w, loss_target, m_c_ctx, m_mod_w, m_mod_b, m_norm1_w, m_norm2_w, m_w_in, m_conv_w, m_conv_b, m_dt_bias, m_a_log, m_ssd_d, m_ssd_norm_w, m_q_a_norm_w, m_w_q_b, m_kv_a_norm_w, m_w_kv_b, m_pool_w, m_pool_scale, m_w_out, m_w_mlp1, m_w_mlp2, m_final_norm_w, v_c_ctx, v_mod_w, v_mod_b, v_norm1_w, v_norm2_w, v_w_in, v_conv_w, v_conv_b, v_dt_bias, v_a_log, v_ssd_d, v_ssd_norm_w, v_q_a_norm_w, v_w_q_b, v_kv_a_norm_w, v_w_kv_b, v_pool_w, v_pool_scale, v_w_out, v_w_mlp1, v_w_mlp2, v_final_norm_w):
    nb, N = x.shape[0], x.shape[1]
    cst = _consts(nb, N)
    xi, yi, ci = _place()
    me = 4 * xi + 2 * yi + ci
    kchip = 2 * xi + yi
    mcols = mod_w.shape[2]
    cshard = conv_w.shape[2]

    blk = jnp.zeros((16, D), F32).at[0:nb].set(c).at[8:16, 0:cshard].set(conv_w.reshape(8, cshard))
    g1 = allgather_small(blk, "gather_cond")
    cond = jnp.concatenate([g1[:, 0:nb].reshape(NDEV * nb, D), c_ctx[None],
                            jnp.zeros((MODR - NDEV * nb - 1, D), F32)], axis=0)
    conv_full = [jnp.concatenate([g1[2 * k, 8 + 4 * l:12 + 4 * l, 0:cshard] for k in range(4)], axis=1)
                 for l in range(DEPTH)]

    mb = [lax.dynamic_slice_in_dim(mod_b[l], kchip * mcols, mcols)[None] for l in range(DEPTH)]
    ms = jnp.concatenate([mod_fwd(cond, mod_w[l], mb[l]) for l in range(DEPTH)], axis=0)
    g2 = allgather_small(ms, "gather_mod")
    bms = []
    for l in range(DEPTH):
        m_all = jnp.concatenate([g2[2 * k, MODR * l:MODR * (l + 1)] for k in range(4)], axis=1)
        mine = jnp.concatenate([lax.dynamic_slice_in_dim(m_all, nb * me, nb), m_all[NDEV * nb:NDEV * nb + 1]], axis=0)
        bms.append(_block_mod(mine.reshape(nb + 1, 6, D), cst))

    big = (w_in, w_q_b, w_kv_b, w_out, w_mlp1, w_mlp2)
    wg = gather_shards([a.astype(MXU) for a in big])
    fw_in, fw_qb, fw_kvb, fw_out, fw1, fw2 = [
        jnp.concatenate([jnp.where(kchip == k, a.astype(MXU), g[k]) for k in range(4)], axis=ax)
        for a, g, ax in zip(big, wg, (2, 2, 2, 1, 2, 1))]
    lws = [_prep_layer(fw_in[l], fw_qb[l], fw_kvb[l], fw_out[l], fw1[l], fw2[l], conv_full[l], conv_b[l], dt_bias[l],
                       a_log[l], ssd_d[l], ssd_norm_w[l], q_a_norm_w[l], kv_a_norm_w[l], pool_w[l], pool_scale[l],
                       norm1_w[l], norm2_w[l]) for l in range(DEPTH)]

    loss_part, grad_x, grads, dfw = _local_step(x, ctx, loss_target, bms, lws, final_norm_w, cst)
    loss = lax.psum(loss_part, ("x", "y", "c"))

    names = ("w_in", "w_q_b", "w_kv_b", "w_out", "w_mlp1", "w_mlp2")
    gs = [getattr(grads[l], n) for n in names for l in range(DEPTH)]
    cidx = jnp.reshape(ci, (1,)).astype(jnp.int32)
    kidx = jnp.reshape(kchip, (1,)).astype(jnp.int32)
    s1 = [add_half(g, r, cidx, "add_half_" + names[i // DEPTH]) for i, (g, r) in enumerate(zip(gs, swap_core_halves(gs)))]
    r2 = swap_chip_parts([s[1] for s in s1])
    g_own = [sum_parts(s[0], r, kidx, "sum_parts_" + names[i // DEPTH]) for i, (s, r) in enumerate(zip(s1, r2))]
    g_oth = swap_reduced_halves(g_own)

    small = {n: jnp.stack([getattr(grads[l], n) for l in range(DEPTH)]) for n, _ in SMALL if n not in ("final_norm_w", "mod_b")}
    small["final_norm_w"] = dfw
    small["mod_b"] = jnp.stack([jnp.sum(grads[l].dm_rows, axis=0) for l in range(DEPTH)])
    dm = jnp.pad(jnp.concatenate([grads[l].dm_rows for l in range(DEPTH)], axis=0), ((0, 8 - 3 * DEPTH), (0, 0)))
    g3 = allgather_small(jnp.concatenate([_pack_small(small), dm.reshape(DM_ROWS, D)], axis=0), "gather_small")
    tot = sum_leading(g3, "sum_small")
    gsmall = _unpack_small(tot[0:SMALL_ROWS])
    ctx_sum = tot[SMALL_ROWS:].reshape(8, 6 * D)
    dm_dev = g3[:, SMALL_ROWS:].reshape(NDEV, 8, 6 * D)
    g_mod_w, dpart = [], jnp.zeros((8, D), F32)
    for l in range(DEPTH):
        dm_all = jnp.concatenate([dm_dev[:, 3 * l:3 * l + nb].reshape(NDEV * nb, 6 * D), ctx_sum[3 * l + nb:3 * l + nb + 1],
                                  jnp.zeros((MODR - NDEV * nb - 1, 6 * D), F32)], axis=0)
        g_mod_w.append(mod_wgrad(cond, lax.dynamic_slice_in_dim(dm_all, kchip * mcols, mcols, axis=1)))
        dctx = jnp.pad(lax.dynamic_slice_in_dim(ctx_sum[3 * l + nb:3 * l + nb + 1], kchip * mcols, mcols, axis=1), ((0, 7), (0, 0)))
        dpart = dpart + mod_dgrad(dctx, mod_w[l])
    g4 = allgather_small(dpart, "gather_cctx")
    g_c_ctx = cctx_grad(g4[0::2], c_ctx[None])[0]

    res = {}
    moments = ((m_w_in, v_w_in), (m_w_q_b, v_w_q_b), (m_w_kv_b, v_w_kv_b), (m_w_out, v_w_out), (m_w_mlp1, v_w_mlp1),
               (m_w_mlp2, v_w_mlp2))
    for i, (n, w, (m, v)) in enumerate(zip(names, big, moments)):
        res[n] = tuple(adamw_halves(w, m, v, g_own[DEPTH * i:DEPTH * (i + 1)], g_oth[DEPTH * i:DEPTH * (i + 1)], cidx,
                                    "adamw_" + n))
    g_mw = jnp.stack(g_mod_w)
    r_mw = adamw(mod_w.reshape(-1, mcols), g_mw.reshape(-1, mcols), m_mod_w.reshape(-1, mcols),
                 v_mod_w.reshape(-1, mcols), name="adamw_mod_w")
    res["mod_w"] = (g_mw,) + tuple(a.reshape(mod_w.shape) for a in r_mw)

    given = dict(norm1_w=(norm1_w, m_norm1_w, v_norm1_w), norm2_w=(norm2_w, m_norm2_w, v_norm2_w),
                 conv_b=(conv_b, m_conv_b, v_conv_b), dt_bias=(dt_bias, m_dt_bias, v_dt_bias),
                 a_log=(a_log, m_a_log, v_a_log), ssd_d=(ssd_d, m_ssd_d, v_ssd_d),
                 ssd_norm_w=(ssd_norm_w, m_ssd_norm_w, v_ssd_norm_w), q_a_norm_w=(q_a_norm_w, m_q_a_norm_w, v_q_a_norm_w),
                 kv_a_norm_w=(kv_a_norm_w, m_kv_a_norm_w, v_kv_a_norm_w), pool_w=(pool_w, m_pool_w, v_pool_w),
                 pool_scale=(pool_scale, m_pool_scale, v_pool_scale),
                 final_norm_w=(final_norm_w, m_final_norm_w, v_final_norm_w), mod_b=(mod_b, m_mod_b, v_mod_b))
    zero_cw = jnp.zeros((2, 4, XBC), F32)
    packs = [_pack_small({n: (given[n][i] if n in given else zero_cw) for n, _ in SMALL}) for i in range(3)]
    r_small = [_unpack_small(a) for a in adamw(packs[0], tot[0:SMALL_ROWS], packs[1], packs[2], name="adamw_small")]
    for n in given:
        res[n] = (gsmall[n], r_small[0][n], r_small[1][n], r_small[2][n])

    g_cw = lax.dynamic_slice_in_dim(gsmall["conv_w"], kchip * cshard, cshard, axis=2)
    padcw = lambda a: jnp.pad(a.reshape(8, cshard), ((0, 0), (0, 256 - cshard)))
    r_cw = adamw(padcw(conv_w), padcw(g_cw), padcw(m_conv_w), padcw(v_conv_w), name="adamw_conv_w")
    res["conv_w"] = (g_cw,) + tuple(a[:, 0:cshard].reshape(conv_w.shape) for a in r_cw)
    r_cc = adamw(c_ctx.reshape(8, 128), g_c_ctx.reshape(8, 128), m_c_ctx.reshape(8, 128), v_c_ctx.reshape(8, 128),
                 name="adamw_c_ctx")
    res["c_ctx"] = (g_c_ctx,) + tuple(a.reshape(D) for a in r_cc)

    order = ("c_ctx", "mod_w", "mod_b", "norm1_w", "norm2_w", "w_in", "conv_w", "conv_b", "dt_bias", "a_log", "ssd_d",
             "ssd_norm_w", "q_a_norm_w", "w_q_b", "kv_a_norm_w", "w_kv_b", "pool_w", "pool_scale", "w_out", "w_mlp1",
             "w_mlp2", "final_norm_w")
    return (loss, grad_x) + tuple(res[n][i] for i in range(4) for n in order)
```

```python
import functools
import math

import numpy as np
import jax
import jax.numpy as jnp
from jax import lax
from jax.experimental import pallas as pl
from jax.experimental.pallas import tpu as pltpu

F32 = jnp.float32
BF16 = jnp.bfloat16
MXU = jnp.bfloat16

D = 1024
DEPTH = 2
GRID_W = 64
CTX = 256
EPS = 1e-6
SSD_HEADS = 6
SSD_P = 64
SSD_INNER = 384
SSD_N = 128
CHUNK = 128
XBC = 896
MLA_HEADS = 6
QK_NOPE = 64
QK_ROPE = 32
QK_DIM = 96
HP = 128
QW = MLA_HEADS * HP
POOL_DIM = 256
D_FF = 4096
FF_BLK = 1024
IN_COLS = 2092
NP = 2176
P_SPLITS = (384, 896, 256, 256, 256, 128)
DT0 = 32
CAT = QW + SSD_INNER + POOL_DIM

SB = 256
TM = 512
HALO = 8

ADAM_LR = 0.001
ADAM_B1 = 0.9
ADAM_B2 = 0.999
ADAM_EPS = 1e-08
ADAM_WD = 0.01
ADAM_STEP = 10

NT = (((1,), (1,)), ((), ()))
TN = (((0,), (0,)), ((), ()))


def _cp(vmem_mb=None):
    if vmem_mb is None:
        return pltpu.CompilerParams()
    return pltpu.CompilerParams(vmem_limit_bytes=vmem_mb << 20)


def _dot(a, b):
    return jnp.dot(a, b, preferred_element_type=F32)


def _dotg(a, b, dims):
    return lax.dot_general(a, b, dims, preferred_element_type=F32)


def _dot_hi(a, b, dims=None, sel_first=False):
    dims = (((1,), (0,)), ((), ())) if dims is None else dims
    v, s = (b, a) if sel_first else (a, b)
    hi = v.astype(BF16)
    lo = (v - hi.astype(F32)).astype(BF16)
    s = s.astype(BF16)
    if sel_first:
        return _dotg(s, hi, dims) + _dotg(s, lo, dims)
    return _dotg(hi, s, dims) + _dotg(lo, s, dims)


def _rms_hat(x):
    rstd = lax.rsqrt(jnp.mean(x * x, axis=-1, keepdims=True) + EPS)
    return x * rstd, rstd


def _rms_bwd(dn, xhat, rstd, w):
    dxhat = dn * w
    dx = rstd * (dxhat - xhat * jnp.mean(dxhat * xhat, axis=-1, keepdims=True))
    return dx, jnp.sum(dn * xhat, axis=0, keepdims=True)


def _sigmoid(z):
    return 1.0 / (1.0 + jnp.exp(-z))


def _colsum(a):
    return jnp.sum(a, axis=0, keepdims=True)


def _rowspec(cols, tm=TM):
    return pl.BlockSpec((tm, cols), lambda i: (i, 0))


def _fullspec(shape):
    n = len(shape)
    return pl.BlockSpec(shape, lambda *_: (0,) * n)


def _resident(shape):
    n = len(shape)
    return pl.BlockSpec(shape, lambda *_: (0,) * n, pipeline_mode=pl.Buffered(1))


def _halo_specs(cols, nrows):
    per = SB // HALO
    last = nrows // HALO - 1
    prev = pl.BlockSpec((HALO, cols), lambda i: (jnp.maximum(i * per - 1, 0), 0))
    nxt = pl.BlockSpec((HALO, cols), lambda i: (jnp.minimum((i + 1) * per, last), 0))
    return prev, nxt


def _ext_rows(cur, prev, nxt, i, blocks_per_sample):
    j = i % blocks_per_sample
    first = jnp.logical_or(j == 0, j == 1)
    last = jnp.logical_or(j == 0, j == blocks_per_sample - 1)
    p = jnp.where(first, 0.0, prev)
    n = jnp.where(last, 0.0, nxt)
    return jnp.concatenate([p, cur, n], axis=0)


def _shift(ext, s):
    n = ext.shape[0]
    return pltpu.roll(ext, (-s) % n, axis=0)[HALO:HALO + SB, :]


def in_proj(x, bm, nw, w):
    R = x.shape[0]

    def body(x_ref, bm_ref, nw_ref, w_ref, h_ref, *outs):
        for s in range(TM // SB):
            rows = slice(s * SB, (s + 1) * SB)
            xhat, _ = _rms_hat(x_ref[rows, :])
            h = xhat * nw_ref[...] * (1.0 + bm_ref[s, 1:2, :]) + bm_ref[s, 0:1, :]
            h_ref[rows, :] = h.astype(h_ref.dtype)
        p = _dot(h_ref[...], w_ref[...])
        off = 0
        for o, n in zip(outs, P_SPLITS):
            o[...] = p[:, off:off + n]
            off += n

    return pl.pallas_call(
        body, name="in_proj", grid=(R // TM,),
        in_specs=[_rowspec(D), pl.BlockSpec((TM // SB, 8, D), lambda i: (i, 0, 0)), _fullspec((1, D)),
                  _fullspec((D, NP))],
        out_specs=[_rowspec(D)] + [_rowspec(n) for n in P_SPLITS],
        out_shape=[jax.ShapeDtypeStruct((R, D), MXU)] + [jax.ShapeDtypeStruct((R, n), F32) for n in P_SPLITS],
        compiler_params=_cp(56),
    )(x, bm, nw, w)


def in_proj_bwd(dx1, x, dz, dxbc, dqa, dkva, dpool, dkr, ddt, bm, nw, wt):
    R = x.shape[0]

    def body(dx1_ref, x_ref, dz_ref, dxbc_ref, dqa_ref, dkva_ref, dpool_ref, dkr_ref, ddt_ref, bm_ref, nw_ref,
             wt_ref, dx_ref, dp_ref, part_ref):
        dp_ref[:, 0:384] = dz_ref[...].astype(dp_ref.dtype)
        dp_ref[:, 384:1280] = dxbc_ref[...].astype(dp_ref.dtype)
        dp_ref[:, 1280:1536] = dqa_ref[...].astype(dp_ref.dtype)
        dp_ref[:, 1536:1792] = dkva_ref[...].astype(dp_ref.dtype)
        dp_ref[:, 1792:2048] = dpool_ref[...].astype(dp_ref.dtype)
        dp_ref[:, 2048:2176] = (dkr_ref[...] + ddt_ref[...]).astype(dp_ref.dtype)
        dh = _dot(dp_ref[...], wt_ref[...])
        w = nw_ref[...]
        for s in range(TM // SB):
            rows = slice(s * SB, (s + 1) * SB)
            xhat, rstd = _rms_hat(x_ref[rows, :])
            dhs = dh[rows, :]
            sc1 = 1.0 + bm_ref[s, 1:2, :]
            dx, dnw = _rms_bwd(dhs * sc1, xhat, rstd, w)
            dx_ref[rows, :] = dx1_ref[rows, :] + dx
            part_ref[s] = jnp.concatenate(
                [_colsum(dhs), _colsum(dhs * xhat * w), dnw, jnp.zeros((5, D), F32)], axis=0)

    return pl.pallas_call(
        body, name="in_proj_bwd", grid=(R // TM,),
        in_specs=[_rowspec(D), _rowspec(D), _rowspec(384), _rowspec(896), _rowspec(256), _rowspec(256),
                  _rowspec(256), _rowspec(128), _rowspec(128),
                  pl.BlockSpec((TM // SB, 8, D), lambda i: (i, 0, 0)), _fullspec((1, D)), _fullspec((NP, D))],
        out_specs=[_rowspec(D), _rowspec(NP), pl.BlockSpec((TM // SB, 8, D), lambda i: (i, 0, 0))],
        out_shape=[jax.ShapeDtypeStruct((R, D), F32), jax.ShapeDtypeStruct((R, NP), MXU),
                   jax.ShapeDtypeStruct((R // SB, 8, D), F32)],
        compiler_params=_cp(56),
    )(dx1, x, dz, dxbc, dqa, dkva, dpool, dkr, ddt, bm, nw, wt)


def mix_fwd(x, attn, ssd, pool, bm, wo):
    R = x.shape[0]

    def body(x_ref, a_ref, s_ref, p_ref, bm_ref, wo_ref, x1_ref, mix_ref, cat_ref):
        cat_ref[:, 0:QW] = a_ref[...].astype(cat_ref.dtype)
        cat_ref[:, QW:QW + SSD_INNER] = s_ref[...].astype(cat_ref.dtype)
        cat_ref[:, QW + SSD_INNER:CAT] = p_ref[...].astype(cat_ref.dtype)
        mix = _dot(cat_ref[...], wo_ref[...])
        mix_ref[...] = mix
        for s in range(TM // SB):
            rows = slice(s * SB, (s + 1) * SB)
            x1_ref[rows, :] = x_ref[rows, :] + bm_ref[s, 2:3, :] * mix[rows, :]

    return pl.pallas_call(
        body, name="mix_fwd", grid=(R // TM,),
        in_specs=[_rowspec(D), _rowspec(QW), _rowspec(SSD_INNER), _rowspec(POOL_DIM),
                  pl.BlockSpec((TM // SB, 8, D), lambda i: (i, 0, 0)), _fullspec((CAT, D))],
        out_specs=[_rowspec(D), _rowspec(D), _rowspec(CAT)],
        out_shape=[jax.ShapeDtypeStruct((R, D), F32), jax.ShapeDtypeStruct((R, D), F32),
                   jax.ShapeDtypeStruct((R, CAT), MXU)],
        compiler_params=_cp(48),
    )(x, attn, ssd, pool, bm, wo)


def mix_bwd(dx1, mix, bm, wot):
    R = dx1.shape[0]

    def body(dx1_ref, mix_ref, bm_ref, wot_ref, da_ref, ds_ref, dpl_ref, dmb_ref, part_ref):
        for s in range(TM // SB):
            rows = slice(s * SB, (s + 1) * SB)
            d = dx1_ref[rows, :]
            dmb_ref[rows, :] = (d * bm_ref[s, 2:3, :]).astype(dmb_ref.dtype)
            part_ref[s] = jnp.concatenate([_colsum(d * mix_ref[rows, :]), jnp.zeros((7, D), F32)], axis=0)
        dcat = _dot(dmb_ref[...], wot_ref[...])
        da_ref[...] = dcat[:, 0:QW]
        ds_ref[...] = dcat[:, QW:QW + SSD_INNER]
        dpl_ref[...] = dcat[:, QW + SSD_INNER:CAT]

    return pl.pallas_call(
        body, name="mix_bwd", grid=(R // TM,),
        in_specs=[_rowspec(D), _rowspec(D), pl.BlockSpec((TM // SB, 8, D), lambda i: (i, 0, 0)),
                  _fullspec((D, CAT))],
        out_specs=[_rowspec(QW), _rowspec(SSD_INNER), _rowspec(POOL_DIM), _rowspec(D),
                   pl.BlockSpec((TM // SB, 8, D), lambda i: (i, 0, 0))],
        out_shape=[jax.ShapeDtypeStruct((R, QW), F32), jax.ShapeDtypeStruct((R, SSD_INNER), F32),
                   jax.ShapeDtypeStruct((R, POOL_DIM), F32), jax.ShapeDtypeStruct((R, D), MXU),
                   jax.ShapeDtypeStruct((R // SB, 8, D), F32)],
        compiler_params=_cp(48),
    )(dx1, mix, bm, wot)


def mlp_fwd(x1, bm, nw, w1, w2):
    R = x1.shape[0]

    def body(x1_ref, bm_ref, nw_ref, w1_ref, w2_ref, x2_ref, mo_ref, r_ref, h2_ref):
        for s in range(TM // SB):
            rows = slice(s * SB, (s + 1) * SB)
            xhat, _ = _rms_hat(x1_ref[rows, :])
            h = xhat * nw_ref[...] * (1.0 + bm_ref[s, 4:5, :]) + bm_ref[s, 3:4, :]
            h2_ref[rows, :] = h.astype(h2_ref.dtype)
        for j in range(D_FF // FF_BLK):
            cols = slice(j * FF_BLK, (j + 1) * FF_BLK)
            r = jnp.maximum(_dot(h2_ref[...], w1_ref[:, cols]), 0.0)
            r_ref[:, cols] = r.astype(r_ref.dtype)
            d = _dot((r * r).astype(MXU), w2_ref[cols, :])
            if j == 0:
                mo_ref[...] = d
            else:
                mo_ref[...] += d
        for s in range(TM // SB):
            rows = slice(s * SB, (s + 1) * SB)
            x2_ref[rows, :] = x1_ref[rows, :] + bm_ref[s, 5:6, :] * mo_ref[rows, :]

    return pl.pallas_call(
        body, name="mlp_fwd", grid=(R // TM,),
        in_specs=[_rowspec(D), pl.BlockSpec((TM // SB, 8, D), lambda i: (i, 0, 0)), _fullspec((1, D)),
                  _resident((D, D_FF)), _resident((D_FF, D))],
        out_specs=[_rowspec(D), _rowspec(D), _rowspec(D_FF), _rowspec(D)],
        out_shape=[jax.ShapeDtypeStruct((R, D), F32), jax.ShapeDtypeStruct((R, D), F32),
                   jax.ShapeDtypeStruct((R, D_FF), BF16), jax.ShapeDtypeStruct((R, D), MXU)],
        compiler_params=_cp(56),
    )(x1, bm, nw, w1, w2)


def mlp_bwd(dx2, x1, mo, r, bm, nw, w2t, w1t):
    R = x1.shape[0]

    def body(dx2_ref, x1_ref, mo_ref, r_ref, bm_ref, nw_ref, w2t_ref, w1t_ref, dx1_ref, du_ref, dob_ref, part_ref,
             acc_ref):
        for s in range(TM // SB):
            rows = slice(s * SB, (s + 1) * SB)
            dob_ref[rows, :] = (dx2_ref[rows, :] * bm_ref[s, 5:6, :]).astype(dob_ref.dtype)
        for j in range(D_FF // FF_BLK):
            cols = slice(j * FF_BLK, (j + 1) * FF_BLK)
            du = _dot(dob_ref[...], w2t_ref[:, cols]) * (2.0 * r_ref[:, cols].astype(F32))
            du_ref[:, cols] = du.astype(du_ref.dtype)
            d = _dot(du_ref[:, cols], w1t_ref[cols, :])
            if j == 0:
                acc_ref[...] = d
            else:
                acc_ref[...] += d
        w = nw_ref[...]
        for s in range(TM // SB):
            rows = slice(s * SB, (s + 1) * SB)
            xhat, rstd = _rms_hat(x1_ref[rows, :])
            dh = acc_ref[rows, :]
            dx, dnw = _rms_bwd(dh * (1.0 + bm_ref[s, 4:5, :]), xhat, rstd, w)
            d2 = dx2_ref[rows, :]
            dx1_ref[rows, :] = d2 + dx
            part_ref[s] = jnp.concatenate(
                [_colsum(dh), _colsum(dh * xhat * w), _colsum(d2 * mo_ref[rows, :]), dnw,
                 jnp.zeros((4, D), F32)], axis=0)

    return pl.pallas_call(
        body, name="mlp_bwd", grid=(R // TM,),
        in_specs=[_rowspec(D), _rowspec(D), _rowspec(D), _rowspec(D_FF),
                  pl.BlockSpec((TM // SB, 8, D), lambda i: (i, 0, 0)), _fullspec((1, D)),
                  _resident((D, D_FF)), _resident((D_FF, D))],
        out_specs=[_rowspec(D), _rowspec(D_FF), _rowspec(D), pl.BlockSpec((TM // SB, 8, D), lambda i: (i, 0, 0))],
        out_shape=[jax.ShapeDtypeStruct((R, D), F32), jax.ShapeDtypeStruct((R, D_FF), MXU),
                   jax.ShapeDtypeStruct((R, D), MXU), jax.ShapeDtypeStruct((R // SB, 8, D), F32)],
        scratch_shapes=[pltpu.VMEM((TM, D), F32)],
        compiler_params=_cp(56),
    )(dx2, x1, mo, r, bm, nw, w2t, w1t)


def mm_tn(a, b, square_a=False, name="mm_tn", col_blocks=False):
    R, M = a.shape
    N = b.shape[1]
    tm = M if M <= 1408 else 1024
    tn = N if N <= 2176 else 1024
    tk = next((c for c in ((2176, 1088, 512) if tm + tn <= 2048 else (1088, 512)) if R % c == 0), R)
    assert not col_blocks or tm == M

    def body(a_ref, b_ref, o_ref):
        @pl.when(pl.program_id(2) == 0)
        def _():
            o_ref[...] = jnp.zeros_like(o_ref)

        av = a_ref[...]
        if square_a:
            av = av.astype(F32)
            av = (av * av).astype(MXU)
        prod = _dotg(av.astype(MXU), b_ref[...].astype(MXU), TN)
        if col_blocks:
            o_ref[0] += prod
        else:
            o_ref[...] += prod

    if col_blocks:
        out_spec = pl.BlockSpec((1, tm, tn), lambda i, j, k: (j, 0, 0))
        out_shape = jax.ShapeDtypeStruct((N // tn, M, tn), F32)
    else:
        out_spec = pl.BlockSpec((tm, tn), lambda i, j, k: (i, j))
        out_shape = jax.ShapeDtypeStruct((M, N), F32)
    return pl.pallas_call(
        body, name=name, grid=(M // tm, N // tn, R // tk),
        in_specs=[pl.BlockSpec((tk, tm), lambda i, j, k: (k, i)), pl.BlockSpec((tk, tn), lambda i, j, k: (k, j))],
        out_specs=out_spec, out_shape=out_shape,
        compiler_params=_cp(48),
    )(a, b)


def final_loss(x, tgt, fw, blocks_per_sample):
    R = x.shape[0]
    nxb = blocks_per_sample - 1

    def body(x_ref, t_ref, fw_ref, dx_ref, part_ref):
        i = pl.program_id(0)
        is_ctx = (i % blocks_per_sample) == 0
        xhat, rstd = _rms_hat(x_ref[...])
        w = fw_ref[...]
        err = xhat * w - t_ref[...]
        dx, dfw = _rms_bwd(err * (1.0 / D), xhat, rstd, w)
        keep = jnp.where(is_ctx, 0.0, 1.0)
        dx_ref[...] = dx * keep
        part_ref[0] = jnp.concatenate([dfw * keep, _colsum(err * err) * keep, jnp.zeros((6, D), F32)], axis=0)

    def tmap(i):
        return ((i // blocks_per_sample) * nxb + jnp.maximum(i % blocks_per_sample - 1, 0), 0)

    return pl.pallas_call(
        body, name="final_loss", grid=(R // SB,),
        in_specs=[_rowspec(D, SB), pl.BlockSpec((SB, D), tmap), _fullspec((1, D))],
        out_specs=[_rowspec(D, SB), pl.BlockSpec((1, 8, D), lambda i: (i, 0, 0))],
        out_shape=[jax.ShapeDtypeStruct((R, D), F32), jax.ShapeDtypeStruct((R // SB, 8, D), F32)],
    )(x, tgt, fw)


def _softplus(v):
    return jnp.maximum(v, 0.0) + jnp.log(1.0 + jnp.exp(-jnp.abs(v)))


def _conv_out(ext, cw_ref, cb_ref):
    return (cb_ref[...] + cw_ref[0:1, :] * _shift(ext, -1) + cw_ref[1:2, :] * _shift(ext, 0)
            + cw_ref[2:3, :] * _shift(ext, 1) + cw_ref[3:4, :] * _shift(ext, 2))


def _dt_dir(v, d):
    lane = lax.broadcasted_iota(jnp.int32, v.shape, 1)
    return jnp.where(lane < SSD_HEADS, pltpu.roll(v, (128 - DT0 - SSD_HEADS * d) % 128, axis=1), 0.0)


def ssd_prep(pxbc, plast, cw, cb, dtb, blocks_per_sample):
    R = pxbc.shape[0]
    prev, nxt = _halo_specs(XBC, R)

    def body(cur_ref, prev_ref, nxt_ref, pl_ref, cw_ref, cb_ref, dtb_ref, xs_ref, bm_ref, cm_ref, dt_ref):
        i = pl.program_id(0)
        ext = _ext_rows(cur_ref[...], prev_ref[...], nxt_ref[...], i, blocks_per_sample)
        co = _conv_out(ext, cw_ref, cb_ref)
        a = co * _sigmoid(co)
        xs_ref[...] = a[:, 0:384]
        bm_ref[...] = a[:, 384:640]
        cm_ref[...] = a[:, 640:896]
        sp = _softplus(pl_ref[...] + dtb_ref[...])
        dt_ref[0] = _dt_dir(sp, 0)
        dt_ref[1] = _dt_dir(sp, 1)

    return pl.pallas_call(
        body, name="ssd_prep", grid=(R // SB,),
        in_specs=[_rowspec(XBC, SB), prev, nxt, _rowspec(128, SB), _fullspec((8, XBC)), _fullspec((1, XBC)),
                  _fullspec((1, 128))],
        out_specs=[_rowspec(384, SB), _rowspec(256, SB), _rowspec(256, SB),
                   pl.BlockSpec((2, SB, 128), lambda i: (0, i, 0))],
        out_shape=[jax.ShapeDtypeStruct((R, 384), F32), jax.ShapeDtypeStruct((R, 256), F32),
                   jax.ShapeDtypeStruct((R, 256), F32), jax.ShapeDtypeStruct((2, R, 128), F32)],
    )(pxbc, pxbc, pxbc, plast, cw, cb, dtb)


def _chunk_index(d, s, nc):
    nctx = CTX // CHUNK
    back = jnp.where(s < nctx, nctx - 1 - s, nc + nctx - 1 - s)
    return jnp.where(d == 0, s, back)


def _scan_common(d, dt, arow, eexp, xs):
    ii = lax.broadcasted_iota(jnp.int32, (CHUNK, CHUNK), 0)
    jj = lax.broadcasted_iota(jnp.int32, (CHUNK, CHUNK), 1)
    mask = ((ii - jj) * (1 - 2 * d)) >= 0
    adt = dt * arow
    tmat = jnp.where(mask, 1.0, 0.0)
    cs = _dot_hi(tmat, adt, sel_first=True)
    tot = _colsum(adt)
    dtx = _dot_hi(dt, eexp)
    xt = xs * dtx
    ecs = jnp.exp(cs)
    ecx = _dot_hi(ecs, eexp)
    dte = jnp.exp(tot - cs)
    dtex = _dot_hi(dte, eexp)
    etot = jnp.exp(tot)
    etx = _dot_hi(jnp.broadcast_to(etot, (8, 128)), eexp)[0:1, :]
    return mask, tmat, adt, cs, tot, dtx, xt, ecs, ecx, dte, dtex, etot, etx


def _decay_matrix(mask, cs, cst, h):
    return jnp.exp(jnp.where(mask, cs[:, h:h + 1] - cst[h:h + 1, :], -1e30))


def _side_wrap(body, n_in, n_out, n_scratch, side, grid):
    if side is None:
        return body, [], [], [], [], []
    ni, no = len(side.ins), len(side.out_shapes)

    def wrapped(*refs):
        ins, refs = refs[:n_in], refs[n_in:]
        side_ins, refs = refs[:ni], refs[ni:]
        outs, refs = refs[:n_out], refs[n_out:]
        side_outs, refs = refs[:no], refs[no:]
        scratch, sems = refs[:n_scratch], refs[n_scratch:]
        ids = [pl.program_id(a) for a in range(len(grid))]
        first = functools.reduce(jnp.logical_and, [i == 0 for i in ids])
        last = functools.reduce(jnp.logical_and, [i == g - 1 for i, g in zip(ids, grid)])
        pl.when(first)(lambda: side.start(side_ins, side_outs, sems))
        body(*ins, *outs, *scratch)
        pl.when(last)(lambda: side.finish(side_ins, side_outs, sems))

    return wrapped, [ANY] * ni, [ANY] * no, list(side.out_shapes), _sems(side.nsem), list(side.ins)


def ssd_scan_fwd(xs, bm, cm, dtv, arow, eexp, nb, T, side=None):
    R = xs.shape[0]
    nc = T // CHUNK

    def rowmap(b, d, s):
        return (b * nc + _chunk_index(d, s, nc), 0)

    def body(xs_ref, bm_ref, cm_ref, dt_ref, a_ref, e_ref, y_ref, hin_ref, st_ref):
        d = pl.program_id(1)
        s = pl.program_id(2)

        @pl.when(s == 0)
        def _():
            st_ref[...] = jnp.zeros_like(st_ref)

        eexp = e_ref[...]
        mask, _, _, cs, _, _, xt, _, ecx, _, dtex, _, etx = _scan_common(
            d, dt_ref[0], a_ref[0, 0:1, :], eexp, xs_ref[...])
        cst = cs.T
        sin = st_ref[...]
        hin_ref[0] = sin
        sb = sin.astype(MXU)
        xtb = xt.astype(MXU)
        xw = (xt * dtex).astype(MXU)
        g0 = lax.broadcasted_iota(jnp.int32, (CHUNK, SSD_INNER), 1) < 192
        lane = lax.broadcasted_iota(jnp.int32, (CHUNK, 128), 1)
        c = [cm_ref[:, 0:128].astype(MXU), cm_ref[:, 128:256].astype(MXU)]
        b = [bm_ref[:, 0:128].astype(MXU), bm_ref[:, 128:256].astype(MXU)]
        y = jnp.where(g0, _dot(c[0], sb), _dot(c[1], sb)) * ecx
        cb = [_dotg(c[0], b[0], NT), _dotg(c[1], b[1], NT)]
        blocks = []
        for blk in range(3):
            acc = None
            for hh in range(2):
                h = blk * 2 + hh
                m = (cb[h // 3] * _decay_matrix(mask, cs, cst, h)).astype(MXU)
                res = _dot(m, xtb[:, blk * 128:(blk + 1) * 128])
                acc = res if hh == 0 else jnp.where(lane < 64, acc, res)
            blocks.append(acc)
        y_ref[0] = y + jnp.concatenate(blocks, axis=1)
        st_ref[...] = sin * etx + jnp.where(g0, _dotg(b[0], xw, TN), _dotg(b[1], xw, TN))

    grid = (nb, 2, nc)
    body, side_in, side_out, side_shapes, side_scratch, side_args = _side_wrap(body, 6, 2, 1, side, grid)
    outs = pl.pallas_call(
        body, name="ssd_scan_fwd" if side is None else "ssd_scan_fwd_comm", grid=grid,
        in_specs=[pl.BlockSpec((CHUNK, 384), rowmap), pl.BlockSpec((CHUNK, 256), rowmap),
                  pl.BlockSpec((CHUNK, 256), rowmap),
                  pl.BlockSpec((1, CHUNK, 128), lambda b, d, s: (d, b * nc + _chunk_index(d, s, nc), 0)),
                  pl.BlockSpec((1, 8, 128), lambda b, d, s: (d, 0, 0)),
                  pl.BlockSpec((128, 384), lambda b, d, s: (0, 0))] + side_in,
        out_specs=[pl.BlockSpec((1, CHUNK, 384), lambda b, d, s: (d, b * nc + _chunk_index(d, s, nc), 0)),
                   pl.BlockSpec((1, CHUNK, 384), lambda b, d, s: ((b * 2 + d) * nc + _chunk_index(d, s, nc), 0, 0))]
                  + side_out,
        out_shape=[jax.ShapeDtypeStruct((2, R, 384), F32), jax.ShapeDtypeStruct((nb * 2 * nc, CHUNK, 384), F32)]
                  + side_shapes,
        scratch_shapes=[pltpu.VMEM((CHUNK, 384), F32)] + side_scratch,
    )(xs, bm, cm, dtv, arow, eexp, *side_args)
    return outs[0], outs[1], list(outs[2:])


def ssd_scan_bwd(xs, bm, cm, dtv, arow, eexp, hin, dy, nb, T, side=None):
    R = xs.shape[0]
    nc = T // CHUNK

    def chunk(d, s):
        return _chunk_index(d, nc - 1 - s, nc)

    def rowmap(b, d, s):
        return (b * nc + chunk(d, s), 0)

    def dirmap(b, d, s):
        return (d, b * nc + chunk(d, s), 0)

    def body(xs_ref, bm_ref, cm_ref, dt_ref, a_ref, e_ref, hin_ref, dy_ref,
             dxs_ref, dbm_ref, dcm_ref, ddt_ref, da_ref, ds_ref):
        d = pl.program_id(1)
        s = pl.program_id(2)

        @pl.when(s == 0)
        def _():
            ds_ref[...] = jnp.zeros_like(ds_ref)
            da_ref[...] = jnp.zeros_like(da_ref)

        eexp = e_ref[...]
        dt = dt_ref[0]
        arow = a_ref[0, 0:1, :]
        xs_v = xs_ref[...]
        mask, tmat, adt, cs, tot, dtx, xt, ecs, ecx, dte, dtex, etot, etx = _scan_common(d, dt, arow, eexp, xs_v)
        cst = cs.T
        sin = hin_ref[0]
        sb = sin.astype(MXU)
        dsp = ds_ref[...]
        dyv = dy_ref[...]
        xtb = xt.astype(MXU)
        xw = (xt * dtex).astype(MXU)
        g0 = lax.broadcasted_iota(jnp.int32, (CHUNK, SSD_INNER), 1) < 192
        lane = lax.broadcasted_iota(jnp.int32, (CHUNK, 128), 1)
        sub = lax.broadcasted_iota(jnp.int32, (CHUNK, 128), 0)
        c = [cm_ref[:, 0:128].astype(MXU), cm_ref[:, 128:256].astype(MXU)]
        b = [bm_ref[:, 0:128].astype(MXU), bm_ref[:, 128:256].astype(MXU)]

        cs_prod = jnp.where(g0, _dot(c[0], sb), _dot(c[1], sb))
        dcsp = dyv * ecx
        dcsp_g = [jnp.where(g0, dcsp, 0.0).astype(MXU), jnp.where(g0, 0.0, dcsp).astype(MXU)]
        dcs = _dot_hi(dyv * cs_prod, eexp, NT) * ecs
        dc = [_dotg(dcsp_g[0], sb, NT), _dotg(dcsp_g[1], sb, NT)]
        dsin = _dotg(c[0], dcsp_g[0], TN) + _dotg(c[1], dcsp_g[1], TN) + dsp * etx

        detx = _colsum(dsp * sin)
        dtot = _dot_hi(jnp.broadcast_to(detx, (8, SSD_INNER)), eexp, NT)[0:1, :] * etot
        dsp_g = [jnp.where(g0, dsp, 0.0).astype(MXU), jnp.where(g0, 0.0, dsp).astype(MXU)]
        dxw = _dot(b[0], dsp_g[0]) + _dot(b[1], dsp_g[1])
        db = [_dotg(xw, dsp_g[0], NT), _dotg(xw, dsp_g[1], NT)]
        dxt = dxw * dtex
        ddte = _dot_hi(dxw * xt, eexp, NT) * dte
        dtot = dtot + _colsum(ddte)
        dcs = dcs - ddte

        cb = [_dotg(c[0], b[0], NT), _dotg(c[1], b[1], NT)]
        dg = [jnp.zeros((CHUNK, CHUNK), F32), jnp.zeros((CHUNK, CHUNK), F32)]
        dcs_rows = jnp.zeros((CHUNK, 128), F32)
        dxt_blocks = []
        for blk in range(3):
            acc = jnp.zeros((CHUNK, 128), F32)
            for hh in range(2):
                h = blk * 2 + hh
                g = h // 3
                mine = (lane < 64) if hh == 0 else (lane >= 64)
                dyh = jnp.where(mine, dyv[:, blk * 128:(blk + 1) * 128], 0.0).astype(MXU)
                lh = _decay_matrix(mask, cs, cst, h)
                m = cb[g] * lh
                dm = _dotg(dyh, xtb[:, blk * 128:(blk + 1) * 128], NT)
                acc = acc + _dotg(m.astype(MXU), dyh, TN)
                dg[g] = dg[g] + dm * lh
                q = dm * m
                dcs = dcs + jnp.where(lane == h, jnp.sum(q, axis=1, keepdims=True), 0.0)
                dcs_rows = dcs_rows - jnp.where(sub == h, jnp.sum(q, axis=0, keepdims=True), 0.0)
            dxt_blocks.append(acc)
        dxt = dxt + jnp.concatenate(dxt_blocks, axis=1)
        for g in range(2):
            dgb = dg[g].astype(MXU)
            dc[g] = dc[g] + _dot(dgb, b[g])
            db[g] = db[g] + _dotg(dgb, c[g], TN)
        dcs = dcs + dcs_rows.T

        dadt = _dot_hi(tmat, dcs, TN, sel_first=True) + dtot
        ddt = dadt * arow + _dot_hi(dxt * xs_v, eexp, NT)
        da_ref[0, 0:1, :] += _colsum(dadt * dt)
        dxs_ref[0] = dxt * dtx
        dbm_ref[0] = jnp.concatenate(db, axis=1)
        dcm_ref[0] = jnp.concatenate(dc, axis=1)
        ddt_ref[0] = ddt
        ds_ref[...] = dsin

    grid = (nb, 2, nc)
    body, side_in, side_out, side_shapes, side_scratch, side_args = _side_wrap(body, 8, 5, 1, side, grid)
    outs = pl.pallas_call(
        body, name="ssd_scan_bwd" if side is None else "ssd_scan_bwd_comm", grid=grid,
        in_specs=[pl.BlockSpec((CHUNK, 384), rowmap), pl.BlockSpec((CHUNK, 256), rowmap),
                  pl.BlockSpec((CHUNK, 256), rowmap), pl.BlockSpec((1, CHUNK, 128), dirmap),
                  pl.BlockSpec((1, 8, 128), lambda b, d, s: (d, 0, 0)),
                  pl.BlockSpec((128, 384), lambda b, d, s: (0, 0)),
                  pl.BlockSpec((1, CHUNK, 384), lambda b, d, s: ((b * 2 + d) * nc + chunk(d, s), 0, 0)),
                  pl.BlockSpec((CHUNK, 384), rowmap)] + side_in,
        out_specs=[pl.BlockSpec((1, CHUNK, 384), dirmap), pl.BlockSpec((1, CHUNK, 256), dirmap),
                   pl.BlockSpec((1, CHUNK, 256), dirmap), pl.BlockSpec((1, CHUNK, 128), dirmap),
                   pl.BlockSpec((1, 8, 128), lambda b, d, s: (b * 2 + d, 0, 0))] + side_out,
        out_shape=[jax.ShapeDtypeStruct((2, R, 384), F32), jax.ShapeDtypeStruct((2, R, 256), F32),
                   jax.ShapeDtypeStruct((2, R, 256), F32), jax.ShapeDtypeStruct((2, R, 128), F32),
                   jax.ShapeDtypeStruct((nb * 2, 8, 128), F32)] + side_shapes,
        scratch_shapes=[pltpu.VMEM((CHUNK, 384), F32)] + side_scratch,
    )(xs, bm, cm, dtv, arow, eexp, hin, dy, *side_args)
    return tuple(outs[:5]) + (list(outs[5:]),)


def _group_rms(g):
    lane = lax.broadcasted_iota(jnp.int32, g.shape, 1)
    g0 = lane < 192
    gg = g * g
    s0 = jnp.sum(jnp.where(g0, gg, 0.0), axis=-1, keepdims=True)
    s1 = jnp.sum(gg, axis=-1, keepdims=True) - s0
    rstd = jnp.where(g0, lax.rsqrt(s0 * (1.0 / 192) + EPS), lax.rsqrt(s1 * (1.0 / 192) + EPS))
    return rstd, g0


def ssd_out_fwd(y2, xs, pz, dexp, nw):
    R = xs.shape[0]

    def body(y_ref, xs_ref, z_ref, d_ref, nw_ref, o_ref):
        z = z_ref[...]
        yy = y_ref[0] + y_ref[1] + xs_ref[...] * d_ref[...]
        g = yy * (z * _sigmoid(z))
        rstd, _ = _group_rms(g)
        o_ref[...] = g * rstd * nw_ref[...]

    return pl.pallas_call(
        body, name="ssd_out_fwd", grid=(R // TM,),
        in_specs=[pl.BlockSpec((2, TM, 384), lambda i: (0, i, 0)), _rowspec(384), _rowspec(384),
                  _fullspec((1, 384)), _fullspec((1, 384))],
        out_specs=_rowspec(384),
        out_shape=jax.ShapeDtypeStruct((R, 384), F32),
    )(y2, xs, pz, dexp, nw)


def ssd_out_bwd(dout, y2, xs, pz, dexp, nw):
    R = xs.shape[0]

    def body(do_ref, y_ref, xs_ref, z_ref, d_ref, nw_ref, dy_ref, dz_ref, dxs_ref, part_ref):
        z = z_ref[...]
        xs_v = xs_ref[...]
        yy = y_ref[0] + y_ref[1] + xs_v * d_ref[...]
        sig = _sigmoid(z)
        sz = z * sig
        g = yy * sz
        rstd, g0 = _group_rms(g)
        ghat = g * rstd
        do = do_ref[...]
        dgn = do * nw_ref[...]
        t = dgn * ghat
        t0 = jnp.sum(jnp.where(g0, t, 0.0), axis=-1, keepdims=True)
        t1 = jnp.sum(t, axis=-1, keepdims=True) - t0
        dg = rstd * (dgn - ghat * jnp.where(g0, t0, t1) * (1.0 / 192))
        dyy = dg * sz
        dy_ref[...] = dyy
        dz_ref[...] = (dg * yy * (sig * (1.0 + z * (1.0 - sig)))).astype(dz_ref.dtype)
        dxs_ref[...] = dyy * d_ref[...]
        part_ref[0] = jnp.concatenate([_colsum(do * ghat), _colsum(dyy * xs_v), jnp.zeros((6, 384), F32)], axis=0)

    return pl.pallas_call(
        body, name="ssd_out_bwd", grid=(R // TM,),
        in_specs=[_rowspec(384), pl.BlockSpec((2, TM, 384), lambda i: (0, i, 0)), _rowspec(384), _rowspec(384),
                  _fullspec((1, 384)), _fullspec((1, 384))],
        out_specs=[_rowspec(384), _rowspec(384), _rowspec(384), pl.BlockSpec((1, 8, 384), lambda i: (i, 0, 0))],
        out_shape=[jax.ShapeDtypeStruct((R, 384), F32), jax.ShapeDtypeStruct((R, 384), MXU),
                   jax.ShapeDtypeStruct((R, 384), F32), jax.ShapeDtypeStruct((R // TM, 8, 384), F32)],
    )(dout, y2, xs, pz, dexp, nw)


def ssd_prep_bwd_a(pxbc, plast, cw, cb, dtb, dxs_skip, dxs2, dbm2, dcm2, ddt2, blocks_per_sample):
    R = pxbc.shape[0]
    prev, nxt = _halo_specs(XBC, R)

    def body(cur_ref, prev_ref, nxt_ref, pl_ref, cw_ref, cb_ref, dtb_ref, dsk_ref, dxs_ref, dbm_ref, dcm_ref, ddt_ref,
             dpre_ref, dlast_ref, part_ref):
        i = pl.program_id(0)
        ext = _ext_rows(cur_ref[...], prev_ref[...], nxt_ref[...], i, blocks_per_sample)
        co = _conv_out(ext, cw_ref, cb_ref)
        sig = _sigmoid(co)
        up = jnp.concatenate([dsk_ref[...] + dxs_ref[0] + dxs_ref[1], dbm_ref[0] + dbm_ref[1],
                              dcm_ref[0] + dcm_ref[1]], axis=1)
        dpre = up * (sig * (1.0 + co * (1.0 - sig)))
        dpre_ref[...] = dpre
        raw = pl_ref[...] + dtb_ref[...]
        lane = lax.broadcasted_iota(jnp.int32, raw.shape, 1)
        ddt = (pltpu.roll(ddt_ref[0], DT0, axis=1) + pltpu.roll(ddt_ref[1], DT0 + SSD_HEADS, axis=1))
        ddt = jnp.where(jnp.logical_and(lane >= DT0, lane < DT0 + 2 * SSD_HEADS), ddt * _sigmoid(raw), 0.0)
        dlast_ref[...] = ddt.astype(dlast_ref.dtype)
        rows = [_colsum(dpre * _shift(ext, k - 1)) for k in range(4)]
        rows.append(_colsum(dpre))
        rows.append(jnp.concatenate([_colsum(ddt), jnp.zeros((1, XBC - 128), F32)], axis=1))
        rows.append(jnp.zeros((2, XBC), F32))
        part_ref[0] = jnp.concatenate(rows, axis=0)

    dirspec = lambda n: pl.BlockSpec((2, SB, n), lambda i: (0, i, 0))
    return pl.pallas_call(
        body, name="ssd_prep_bwd_a", grid=(R // SB,),
        in_specs=[_rowspec(XBC, SB), prev, nxt, _rowspec(128, SB), _fullspec((8, XBC)), _fullspec((1, XBC)),
                  _fullspec((1, 128)), _rowspec(384, SB), dirspec(384), dirspec(256), dirspec(256), dirspec(128)],
        out_specs=[_rowspec(XBC, SB), _rowspec(128, SB), pl.BlockSpec((1, 8, XBC), lambda i: (i, 0, 0))],
        out_shape=[jax.ShapeDtypeStruct((R, XBC), F32), jax.ShapeDtypeStruct((R, 128), MXU),
                   jax.ShapeDtypeStruct((R // SB, 8, XBC), F32)],
    )(pxbc, pxbc, pxbc, plast, cw, cb, dtb, dxs_skip, dxs2, dbm2, dcm2, ddt2)


def ssd_prep_bwd_b(dpre, cw, blocks_per_sample):
    R = dpre.shape[0]
    prev, nxt = _halo_specs(XBC, R)

    def body(cur_ref, prev_ref, nxt_ref, cw_ref, o_ref):
        i = pl.program_id(0)
        ext = _ext_rows(cur_ref[...], prev_ref[...], nxt_ref[...], i, blocks_per_sample)
        o_ref[...] = (cw_ref[0:1, :] * _shift(ext, 1) + cw_ref[1:2, :] * _shift(ext, 0)
                      + cw_ref[2:3, :] * _shift(ext, -1) + cw_ref[3:4, :] * _shift(ext, -2)).astype(o_ref.dtype)

    return pl.pallas_call(
        body, name="ssd_prep_bwd_b", grid=(R // SB,),
        in_specs=[_rowspec(XBC, SB), prev, nxt, _fullspec((8, XBC))],
        out_specs=_rowspec(XBC, SB),
        out_shape=jax.ShapeDtypeStruct((R, XBC), MXU),
    )(dpre, dpre, dpre, cw)


def _rope(u, cos, sa, sb):
    return u * cos + pltpu.roll(u, 120, axis=1) * sa + pltpu.roll(u, 8, axis=1) * sb


def _rope_t(du, cos, sa, sb):
    return du * cos + pltpu.roll(du * sa, 8, axis=1) + pltpu.roll(du * sb, 120, axis=1)


def mla_prep(pqa, pkva, plast, qnw, kvnw, wq, wk, wv, cos, sa, sb):
    R = pqa.shape[0]

    def body(qa_ref, kva_ref, pl_ref, qnw_ref, kvnw_ref, wq_ref, wk_ref, wv_ref, cos_ref, sa_ref, sb_ref,
             q_ref, k_ref, v_ref, cq_ref, ckv_ref):
        cos_v, sa_v, sb_v = cos_ref[...], sa_ref[...], sb_ref[...]
        xq, _ = _rms_hat(qa_ref[...])
        cq_ref[...] = (xq * qnw_ref[...]).astype(cq_ref.dtype)
        xkv, _ = _rms_hat(kva_ref[...])
        ckv_ref[...] = (xkv * kvnw_ref[...]).astype(ckv_ref.dtype)
        q = _dot(cq_ref[...], wq_ref[...])
        kn = _dot(ckv_ref[...], wk_ref[...])
        v_ref[...] = _dot(ckv_ref[...], wv_ref[...]).astype(v_ref.dtype)
        lane = lax.broadcasted_iota(jnp.int32, (TM, HP), 1)
        rope_lanes = jnp.logical_and(lane >= QK_NOPE, lane < QK_DIM)
        kr = _rope(jnp.where(rope_lanes, pltpu.roll(pl_ref[...], QK_NOPE, axis=1), 0.0), cos_v, sa_v, sb_v)
        for h in range(MLA_HEADS):
            cols = slice(h * HP, (h + 1) * HP)
            q_ref[:, cols] = (_rope(q[:, cols], cos_v, sa_v, sb_v) * Q_SCALE).astype(q_ref.dtype)
            k_ref[:, cols] = (kn[:, cols] + kr).astype(k_ref.dtype)

    return pl.pallas_call(
        body, name="mla_prep", grid=(R // TM,),
        in_specs=[_rowspec(256), _rowspec(256), _rowspec(128), _fullspec((1, 256)), _fullspec((1, 256)),
                  _fullspec((256, QW)), _fullspec((256, QW)), _fullspec((256, QW)),
                  _rowspec(HP), _rowspec(HP), _rowspec(HP)],
        out_specs=[_rowspec(QW), _rowspec(QW), _rowspec(QW), _rowspec(256), _rowspec(256)],
        out_shape=[jax.ShapeDtypeStruct((R, QW), MXU)] * 3 + [jax.ShapeDtypeStruct((R, 256), MXU)] * 2,
    )(pqa, pkva, plast, qnw, kvnw, wq, wk, wv, cos, sa, sb)


def mla_prep_bwd(dq, dk, dv, pqa, pkva, qnw, kvnw, wqt, wkt, wvt, cos, sa, sb):
    R = pqa.shape[0]

    def body(dq_ref, dk_ref, dv_ref, qa_ref, kva_ref, qnw_ref, kvnw_ref, wqt_ref, wkt_ref, wvt_ref,
             cos_ref, sa_ref, sb_ref, dqa_ref, dkva_ref, dkr_ref, dql_ref, dkm_ref, dvb_ref, part_ref):
        cos_v, sa_v, sb_v = cos_ref[...], sa_ref[...], sb_ref[...]
        lane = lax.broadcasted_iota(jnp.int32, (TM, HP), 1)
        rope_lanes = jnp.logical_and(lane >= QK_NOPE, lane < QK_DIM)
        dkr = jnp.zeros((TM, HP), F32)
        for h in range(MLA_HEADS):
            cols = slice(h * HP, (h + 1) * HP)
            dql_ref[:, cols] = (_rope_t(dq_ref[:, cols], cos_v, sa_v, sb_v) * ATT_SCALE).astype(dql_ref.dtype)
            dkh = dk_ref[:, cols] * LN2
            dkm_ref[:, cols] = jnp.where(lane < QK_NOPE, dkh, 0.0).astype(dkm_ref.dtype)
            dkr = dkr + jnp.where(rope_lanes, dkh, 0.0)
        dvb_ref[...] = dv_ref[...].astype(dvb_ref.dtype)
        dkr = jnp.where(rope_lanes, _rope_t(dkr, cos_v, sa_v, sb_v), 0.0)
        dkr_ref[...] = pltpu.roll(dkr, HP - QK_NOPE, axis=1).astype(dkr_ref.dtype)
        xq, rq = _rms_hat(qa_ref[...])
        dqa, dqnw = _rms_bwd(_dot(dql_ref[...], wqt_ref[...]), xq, rq, qnw_ref[...])
        dqa_ref[...] = dqa.astype(dqa_ref.dtype)
        xkv, rkv = _rms_hat(kva_ref[...])
        dckv = _dot(dkm_ref[...], wkt_ref[...]) + _dot(dvb_ref[...], wvt_ref[...])
        dkva, dkvnw = _rms_bwd(dckv, xkv, rkv, kvnw_ref[...])
        dkva_ref[...] = dkva.astype(dkva_ref.dtype)
        part_ref[0] = jnp.concatenate([dqnw, dkvnw, jnp.zeros((6, 256), F32)], axis=0)

    return pl.pallas_call(
        body, name="mla_prep_bwd", grid=(R // TM,),
        in_specs=[_rowspec(QW), _rowspec(QW), _rowspec(QW), _rowspec(256), _rowspec(256), _fullspec((1, 256)),
                  _fullspec((1, 256)), _fullspec((QW, 256)), _fullspec((QW, 256)), _fullspec((QW, 256)),
                  _rowspec(HP), _rowspec(HP), _rowspec(HP)],
        out_specs=[_rowspec(256), _rowspec(256), _rowspec(128), _rowspec(QW), _rowspec(QW), _rowspec(QW),
                   pl.BlockSpec((1, 8, 256), lambda i: (i, 0, 0))],
        out_shape=[jax.ShapeDtypeStruct((R, 256), MXU), jax.ShapeDtypeStruct((R, 256), MXU),
                   jax.ShapeDtypeStruct((R, 128), MXU)] + [jax.ShapeDtypeStruct((R, QW), MXU)] * 3
                  + [jax.ShapeDtypeStruct((R // TM, 8, 256), F32)],
    )(dq, dk, dv, pqa, pkva, qnw, kvnw, wqt, wkt, wvt, cos, sa, sb)


ATT_SCALE = QK_DIM ** -0.5
TQ = 256


LOG2E = 1.4426950408889634
LN2 = 0.6931471805599453
Q_SCALE = ATT_SCALE * LOG2E


def _key_chunks(T, n=2):
    units = T // 128
    sizes = [(units // n + (1 if i < units % n else 0)) * 128 for i in range(n)]
    return [(sum(sizes[:i]), sz) for i, sz in enumerate(sizes) if sz]


def attn_fwd(q, k, v, nb, T):
    R = q.shape[0]
    nq = T // TQ
    chunks = _key_chunks(T)

    def body(q_ref, k_ref, v_ref, o_ref, lse_ref):
        def part(lo, n):
            s = _dotg(q_ref[...], k_ref[lo:lo + n, :], NT)
            m = jnp.max(s, axis=-1, keepdims=True)
            p = jnp.exp2(s - m)
            return m, jnp.sum(p, axis=-1, keepdims=True), _dot(p.astype(MXU), v_ref[lo:lo + n, :])

        def finish(parts):
            m = parts[0][0]
            for pm, _, _ in parts[1:]:
                m = jnp.maximum(m, pm)
            l, o = 0.0, 0.0
            for pm, pl_, po in parts:
                a = jnp.exp2(pm - m)
                l = l + a * pl_
                o = o + a * po
            o_ref[...] = o / l
            lse_ref[...] = jnp.broadcast_to(m + jnp.log(l) * LOG2E, (TQ, HP))

        i = pl.program_id(2)
        pl.when(i == 0)(lambda: finish([part(0, CTX)]))
        pl.when(i > 0)(lambda: finish([part(lo, n) for lo, n in chunks]))

    qspec = pl.BlockSpec((TQ, HP), lambda b, h, i: (b * nq + i, h))
    kspec = pl.BlockSpec((T, HP), lambda b, h, i: (b, h))
    return pl.pallas_call(
        body, name="attn_fwd", grid=(nb, MLA_HEADS, nq),
        in_specs=[qspec, kspec, kspec], out_specs=[qspec, qspec],
        out_shape=[jax.ShapeDtypeStruct((R, QW), F32)] * 2,
        compiler_params=_cp(48),
    )(q, k, v)


def attn_bwd(q, k, v, o, lse, do, nb, T):
    R = q.shape[0]
    nq = T // TQ
    chunks = _key_chunks(T)

    def body(q_ref, k_ref, v_ref, o_ref, lse_ref, do_ref, dq_ref, dk_ref, dv_ref):
        i = pl.program_id(2)

        @pl.when(i == 0)
        def _():
            dk_ref[...] = jnp.zeros_like(dk_ref)
            dv_ref[...] = jnp.zeros_like(dv_ref)

        def run(chunks):
            qv = q_ref[...]
            dov = do_ref[...]
            dob = dov.astype(MXU)
            delta = jnp.sum(dov * o_ref[...], axis=-1, keepdims=True)
            lse_v = lse_ref[:, 0:1]
            dq = 0.0
            for lo, n in chunks:
                kv = k_ref[lo:lo + n, :]
                p = jnp.exp2(_dotg(qv, kv, NT) - lse_v)
                dp = _dotg(dob, v_ref[lo:lo + n, :], NT)
                dsb = (p * (dp - delta)).astype(MXU)
                dq = dq + _dot(dsb, kv)
                dk_ref[lo:lo + n, :] += _dotg(dsb, qv, TN)
                dv_ref[lo:lo + n, :] += _dotg(p.astype(MXU), dob, TN)
            dq_ref[...] = dq

        pl.when(i == 0)(lambda: run([(0, CTX)]))
        pl.when(i > 0)(lambda: run(chunks))

    qspec = pl.BlockSpec((TQ, HP), lambda b, h, i: (b * nq + i, h))
    kspec = pl.BlockSpec((T, HP), lambda b, h, i: (b, h))
    return pl.pallas_call(
        body, name="attn_bwd", grid=(nb, MLA_HEADS, nq),
        in_specs=[qspec, kspec, kspec, qspec, qspec, qspec],
        out_specs=[qspec, kspec, kspec],
        out_shape=[jax.ShapeDtypeStruct((R, QW), F32)] * 3,
        compiler_params=_cp(56),
    )(q, k, v, o, lse, do)


def _pool_geometry(i, blocks_per_sample, seq):
    j = i % blocks_per_sample
    n = jnp.where(j == 0, CTX, seq)
    t0 = jnp.where(j == 0, 0, (j - 1) * SB) - HALO
    lane = lax.broadcasted_iota(jnp.int32, (SB + 2 * HALO, POOL_DIM), 1)
    t = lax.broadcasted_iota(jnp.int32, (SB + 2 * HALO, POOL_DIM), 0) + t0
    wh = jnp.where(lane < 64, 1, jnp.where(lane < 128, 2, jnp.where(lane < 192, 4, 8)))
    cnt = jnp.minimum(t + wh, n) - jnp.maximum(t - wh, 0)
    return lane, 1.0 / jnp.maximum(cnt, 1).astype(F32)


def _by_window(lane, c2, c4, c8, c16):
    return jnp.where(lane < 64, c2, jnp.where(lane < 128, c4, jnp.where(lane < 192, c8, c16)))


def _window_sums(ext, lane, first):
    n = ext.shape[0]
    r = lambda a, s: pltpu.roll(a, s % n, axis=0)
    c2 = ext + r(ext, first)
    c4 = r(c2, 1) + r(c2, -1)
    c8 = r(c4, 2) + r(c4, -2)
    c16 = r(c8, 4) + r(c8, -4)
    return _by_window(lane, c2, c4, c8, c16)


def _pool_delta(ext, lane, inv):
    return (_window_sums(ext, lane, 1) * inv - ext)[HALO:HALO + SB, :]


def pool_fwd(ppool, wbd, scale, blocks_per_sample, seq):
    R = ppool.shape[0]
    prev, nxt = _halo_specs(POOL_DIM, R)

    def body(cur_ref, prev_ref, nxt_ref, w_ref, s_ref, o_ref):
        i = pl.program_id(0)
        ext = _ext_rows(cur_ref[...], prev_ref[...], nxt_ref[...], i, blocks_per_sample)
        lane, inv = _pool_geometry(i, blocks_per_sample, seq)
        dlt = _pool_delta(ext, lane, inv)
        o_ref[...] = _dot(dlt.astype(MXU), w_ref[...]) * s_ref[...]

    return pl.pallas_call(
        body, name="pool_fwd", grid=(R // SB,),
        in_specs=[_rowspec(POOL_DIM, SB), prev, nxt, _fullspec((POOL_DIM, POOL_DIM)), _fullspec((1, POOL_DIM))],
        out_specs=_rowspec(POOL_DIM, SB),
        out_shape=jax.ShapeDtypeStruct((R, POOL_DIM), F32),
    )(ppool, ppool, ppool, wbd, scale)


def pool_bwd(ppool, dpool, wbd, wbdt, scale, blocks_per_sample, seq):
    R = ppool.shape[0]
    prev, nxt = _halo_specs(POOL_DIM, R)

    def body(cur_ref, prev_ref, nxt_ref, dcur_ref, dprev_ref, dnxt_ref, w_ref, wt_ref, s_ref, du_ref, dw_ref, part_ref):
        i = pl.program_id(0)

        @pl.when(i == 0)
        def _():
            dw_ref[...] = jnp.zeros_like(dw_ref)

        ext = _ext_rows(cur_ref[...], prev_ref[...], nxt_ref[...], i, blocks_per_sample)
        lane, inv = _pool_geometry(i, blocks_per_sample, seq)
        dlt = _pool_delta(ext, lane, inv).astype(MXU)
        dy = dcur_ref[...]
        part_ref[0] = jnp.concatenate([_colsum(dy * _dot(dlt, w_ref[...])), jnp.zeros((7, POOL_DIM), F32)], axis=0)
        dyp = (dy * s_ref[...]).astype(MXU)
        dw_ref[...] += _dotg(dlt, dyp, TN)
        dext = _ext_rows(dy, dprev_ref[...], dnxt_ref[...], i, blocks_per_sample)
        dd = _dot((dext * s_ref[...]).astype(MXU), wt_ref[...])
        du_ref[...] = (_window_sums(dd * inv, lane, -1) - dd)[HALO:HALO + SB, :].astype(du_ref.dtype)

    return pl.pallas_call(
        body, name="pool_bwd", grid=(R // SB,),
        in_specs=[_rowspec(POOL_DIM, SB), prev, nxt, _rowspec(POOL_DIM, SB), prev, nxt,
                  _fullspec((POOL_DIM, POOL_DIM)), _fullspec((POOL_DIM, POOL_DIM)), _fullspec((1, POOL_DIM))],
        out_specs=[_rowspec(POOL_DIM, SB), _fullspec((POOL_DIM, POOL_DIM)),
                   pl.BlockSpec((1, 8, POOL_DIM), lambda i: (i, 0, 0))],
        out_shape=[jax.ShapeDtypeStruct((R, POOL_DIM), MXU), jax.ShapeDtypeStruct((POOL_DIM, POOL_DIM), F32),
                   jax.ShapeDtypeStruct((R // SB, 8, POOL_DIM), F32)],
    )(ppool, ppool, ppool, dpool, dpool, dpool, wbd, wbdt, scale)


def adamw(w, g, m, v, name="adamw"):
    rows, cols = w.shape
    tr = rows
    for cand in (512, 256, 128, 64, 32, 16, 8):
        if rows % cand == 0:
            tr = cand
            break
    bc1 = 1.0 - ADAM_B1 ** ADAM_STEP
    bc2 = 1.0 - ADAM_B2 ** ADAM_STEP

    def body(w_ref, g_ref, m_ref, v_ref, d_ref, nm_ref, nv_ref):
        g_v = g_ref[...]
        nm = ADAM_B1 * m_ref[...] + (1.0 - ADAM_B1) * g_v
        nv = ADAM_B2 * v_ref[...] + (1.0 - ADAM_B2) * (g_v * g_v)
        nm_ref[...] = nm
        nv_ref[...] = nv
        d_ref[...] = -ADAM_LR * ((nm / bc1) / (jnp.sqrt(nv / bc2) + ADAM_EPS) + ADAM_WD * w_ref[...])

    spec = pl.BlockSpec((tr, cols), lambda i: (i, 0))
    return pl.pallas_call(
        body, name=name, grid=(rows // tr,),
        in_specs=[spec] * 4, out_specs=[spec] * 3,
        out_shape=[jax.ShapeDtypeStruct((rows, cols), F32)] * 3,
    )(w, g, m, v)


MODR = 32


def _silu(v):
    return v * _sigmoid(v)


def mod_fwd(cond, w, b):
    n = w.shape[1]

    def body(c_ref, w_ref, b_ref, o_ref):
        o_ref[...] = _dot(_silu(c_ref[...]).astype(MXU), w_ref[...].astype(MXU)) + b_ref[...]

    return pl.pallas_call(
        body, name="mod_fwd", out_shape=jax.ShapeDtypeStruct((MODR, n), F32),
        in_specs=[_fullspec((MODR, D)), _fullspec((D, n)), _fullspec((1, n))], out_specs=_fullspec((MODR, n)),
        grid=(1,), compiler_params=_cp(40),
    )(cond, w, b)


def mod_wgrad(cond, dm):
    n = dm.shape[1]

    def body(c_ref, d_ref, o_ref):
        o_ref[...] = _dotg(_silu(c_ref[...]).astype(MXU), d_ref[...].astype(MXU), TN)

    return pl.pallas_call(
        body, name="mod_wgrad", out_shape=jax.ShapeDtypeStruct((D, n), F32),
        in_specs=[_fullspec((MODR, D)), _fullspec((MODR, n))], out_specs=_fullspec((D, n)),
        grid=(1,), compiler_params=_cp(40),
    )(cond, dm)


def mod_dgrad(dm, w):
    n = w.shape[1]

    def body(d_ref, w_ref, o_ref):
        o_ref[...] = _dotg(d_ref[...].astype(MXU), w_ref[...].astype(MXU), NT)

    return pl.pallas_call(
        body, name="mod_dgrad", out_shape=jax.ShapeDtypeStruct((8, D), F32),
        in_specs=[_fullspec((8, n)), _fullspec((D, n))], out_specs=_fullspec((8, D)),
        grid=(1,), compiler_params=_cp(40),
    )(dm, w)


def sum_leading(a, name="sum_leading"):
    n, r, c = a.shape

    def body(a_ref, o_ref):
        acc = a_ref[0]
        for k in range(1, n):
            acc = acc + a_ref[k]
        o_ref[...] = acc

    return pl.pallas_call(
        body, name=name, out_shape=jax.ShapeDtypeStruct((r, c), F32),
        in_specs=[_fullspec((n, r, c))], out_specs=_fullspec((r, c)), grid=(1,),
    )(a)


MESH = pl.DeviceIdType.MESH
NDEV = 8
ANY = pl.BlockSpec(memory_space=pl.ANY)


def _place():
    return lax.axis_index("x"), lax.axis_index("y"), lax.axis_index("c")


def _other_chips(x, y):
    return [(1 - x, y), (x, 1 - y), (1 - x, 1 - y)]


def allgather_small(v, name):
    r, cols = v.shape

    def body(v_ref, o_ref, send_sems, recv_sems):
        x, y, c = _place()
        me = 4 * x + 2 * y + c
        o_ref[me] = v_ref[...]
        copies = []
        for rel in range(1, NDEV):
            peer = (1 - x if rel & 4 else x, 1 - y if rel & 2 else y, 1 - c if rel & 1 else c)
            cp = pltpu.make_async_remote_copy(src_ref=v_ref, dst_ref=o_ref.at[me], send_sem=send_sems.at[rel - 1],
                                              recv_sem=recv_sems.at[rel - 1], device_id=peer, device_id_type=MESH)
            cp.start()
            copies.append(cp)
        for cp in copies:
            cp.wait_recv()
        for cp in copies:
            cp.wait_send()

    return pl.pallas_call(
        body, name=name, out_shape=jax.ShapeDtypeStruct((NDEV, r, cols), F32),
        in_specs=[pl.BlockSpec(memory_space=pltpu.VMEM)], out_specs=pl.BlockSpec(memory_space=pltpu.VMEM),
        scratch_shapes=[pltpu.SemaphoreType.DMA((NDEV - 1,)), pltpu.SemaphoreType.DMA((NDEV - 1,))],
        compiler_params=_cp(40),
    )(v)


def _sems(n):
    return [pltpu.SemaphoreType.DMA((n,)), pltpu.SemaphoreType.DMA((n,))]


def gather_job(arrs):
    n = len(arrs)

    def copy(srcs, outs, sems, i, slot, kk, cc, to, from_src=False):
        hr = arrs[i].shape[0] // 2
        dst = outs[i].at[kk, pl.ds(cc * hr, hr), :]
        return pltpu.make_async_remote_copy(src_ref=srcs[i].at[pl.ds(cc * hr, hr), :] if from_src else dst, dst_ref=dst,
                                            send_sem=sems[0].at[slot * n + i], recv_sem=sems[1].at[slot * n + i],
                                            device_id=to, device_id_type=MESH)

    def start(srcs, outs, sems):
        x, y, c = _place()
        for j, (px, py) in enumerate(_other_chips(x, y)):
            for i in range(n):
                copy(srcs, outs, sems, i, j, 2 * x + y, c, (px, py, c), True).start()

    def finish(srcs, outs, sems):
        x, y, c = _place()
        sib = (x, y, 1 - c)
        chips = _other_chips(x, y)
        passed = []
        for j, (px, py) in enumerate(chips):
            for i in range(n):
                copy(srcs, outs, sems, i, j, 2 * px + py, c, (px, py, c)).wait_recv()
                cp = copy(srcs, outs, sems, i, 3 + j, 2 * px + py, c, sib)
                cp.start()
                passed.append(cp)
        for j, (px, py) in enumerate(chips):
            for i in range(n):
                copy(srcs, outs, sems, i, 3 + j, 2 * px + py, 1 - c, sib).wait_recv()
        for j, (px, py) in enumerate(chips):
            for i in range(n):
                copy(srcs, outs, sems, i, j, 2 * x + y, c, (px, py, c), True).wait_send()
        for cp in passed:
            cp.wait_send()

    return _NS(ins=list(arrs), out_shapes=[jax.ShapeDtypeStruct((4,) + a.shape, a.dtype) for a in arrs], nsem=6 * n,
               start=start, finish=finish)


def chip_swap_job(ss):
    n = len(ss)

    def copies(srcs, outs, sems):
        x, y, c = _place()
        return [pltpu.make_async_remote_copy(src_ref=srcs[i].at[2 * px + py], dst_ref=outs[i].at[j],
                                             send_sem=sems[0].at[j * n + i], recv_sem=sems[1].at[j * n + i],
                                             device_id=(px, py, c), device_id_type=MESH)
                for j, (px, py) in enumerate(_other_chips(x, y)) for i in range(n)]

    def start(srcs, outs, sems):
        for cp in copies(srcs, outs, sems):
            cp.start()

    def finish(srcs, outs, sems):
        for cp in copies(srcs, outs, sems):
            cp.wait()

    return _NS(ins=list(ss), out_shapes=[jax.ShapeDtypeStruct((3,) + s.shape[1:], s.dtype) for s in ss], nsem=3 * n,
               start=start, finish=finish)


def run_job(job, name):
    n, m = len(job.ins), len(job.out_shapes)

    def body(*refs):
        srcs, outs, sems = refs[:n], refs[n:n + m], refs[n + m:]
        job.start(srcs, outs, sems)
        job.finish(srcs, outs, sems)

    return pl.pallas_call(body, name=name, out_shape=job.out_shapes, in_specs=[ANY] * n, out_specs=[ANY] * m,
                          scratch_shapes=_sems(job.nsem))(*job.ins)


def swap_core_halves(gs):
    n = len(gs)

    def body(*refs):
        srcs, outs = refs[:n], refs[n:2 * n]
        send_sems, recv_sems = refs[2 * n:]
        x, y, c = _place()
        copies = []
        for i in range(n):
            hr = gs[i].shape[1] // 2
            cp = pltpu.make_async_remote_copy(src_ref=srcs[i].at[:, pl.ds((1 - c) * hr, hr), :], dst_ref=outs[i],
                                              send_sem=send_sems.at[i], recv_sem=recv_sems.at[i],
                                              device_id=(x, y, 1 - c), device_id_type=MESH)
            cp.start()
            copies.append(cp)
        for cp in copies:
            cp.wait()

    return pl.pallas_call(
        body, name="swap_core_halves",
        out_shape=[jax.ShapeDtypeStruct((4, g.shape[1] // 2, g.shape[2]), g.dtype) for g in gs],
        in_specs=[ANY] * n, out_specs=[ANY] * n, scratch_shapes=_sems(n),
    )(*gs)


def add_half(g, r1, cidx, name):
    _, rows, cols = g.shape
    hr = rows // 2

    def body(c_ref, g_ref, r_ref, o_ref, ob_ref):
        s = g_ref[...] + r_ref[...]
        o_ref[...] = s
        ob_ref[...] = s.astype(BF16)

    blk = lambda f: pl.BlockSpec((1, hr, cols), f)
    return pl.pallas_call(
        body, name=name,
        out_shape=[jax.ShapeDtypeStruct((4, hr, cols), F32), jax.ShapeDtypeStruct((4, hr, cols), BF16)],
        grid_spec=pltpu.PrefetchScalarGridSpec(
            num_scalar_prefetch=1, grid=(4,),
            in_specs=[blk(lambda k, c_ref: (k, c_ref[0], 0)), blk(lambda k, c_ref: (k, 0, 0))],
            out_specs=[blk(lambda k, c_ref: (k, 0, 0)), blk(lambda k, c_ref: (k, 0, 0))]),
    )(cidx, g, r1)


def sum_parts(s1, r2, kidx, name):
    _, hr, cols = s1.shape

    def body(k_ref, s_ref, r_ref, o_ref):
        o_ref[...] = ((s_ref[0] + r_ref[0].astype(F32)) + r_ref[1].astype(F32)) + r_ref[2].astype(F32)

    return pl.pallas_call(
        body, name=name, out_shape=jax.ShapeDtypeStruct((hr, cols), F32),
        grid_spec=pltpu.PrefetchScalarGridSpec(
            num_scalar_prefetch=1, grid=(1,),
            in_specs=[pl.BlockSpec((1, hr, cols), lambda i, k_ref: (k_ref[0], 0, 0)),
                      pl.BlockSpec((3, hr, cols), lambda i, k_ref: (0, 0, 0))],
            out_specs=pl.BlockSpec((hr, cols), lambda i, k_ref: (0, 0))),
    )(kidx, s1, r2)


def swap_reduced_halves(hs):
    n = len(hs)

    def body(*refs):
        srcs, outs = refs[:n], refs[n:2 * n]
        send_sems, recv_sems = refs[2 * n:]
        x, y, c = _place()
        copies = []
        for i in range(n):
            cp = pltpu.make_async_remote_copy(src_ref=srcs[i], dst_ref=outs[i], send_sem=send_sems.at[i],
                                              recv_sem=recv_sems.at[i], device_id=(x, y, 1 - c), device_id_type=MESH)
            cp.start()
            copies.append(cp)
        for cp in copies:
            cp.wait()

    return pl.pallas_call(
        body, name="swap_reduced_halves", out_shape=[jax.ShapeDtypeStruct(h.shape, h.dtype) for h in hs],
        in_specs=[ANY] * n, out_specs=[ANY] * n, scratch_shapes=_sems(n),
    )(*hs)


def adamw_halves(w, m, v, own, oth, cidx, name):
    depth, rows, cols = w.shape
    hr = rows // 2
    tr = min(hr, 256)
    nblk = hr // tr
    bc1 = 1.0 - ADAM_B1 ** ADAM_STEP
    bc2 = 1.0 - ADAM_B2 ** ADAM_STEP

    def body(c_ref, w_ref, m_ref, v_ref, own0, own1, oth0, oth1, g_ref, d_ref, nm_ref, nv_ref):
        l = pl.program_id(0)
        hi = pl.program_id(1)
        mine = jnp.where(l == 0, own0[...], own1[...])
        other = jnp.where(l == 0, oth0[...], oth1[...])
        g_v = jnp.where(hi == c_ref[0], mine, other)
        nm = ADAM_B1 * m_ref[0] + (1.0 - ADAM_B1) * g_v
        nv = ADAM_B2 * v_ref[0] + (1.0 - ADAM_B2) * (g_v * g_v)
        g_ref[0] = g_v
        nm_ref[0] = nm
        nv_ref[0] = nv
        d_ref[0] = -ADAM_LR * ((nm / bc1) / (jnp.sqrt(nv / bc2) + ADAM_EPS) + ADAM_WD * w_ref[0])

    wspec = pl.BlockSpec((1, tr, cols), lambda l, hi, b, c_ref: (l, hi * nblk + b, 0))
    gspec = pl.BlockSpec((tr, cols), lambda l, hi, b, c_ref: (b, 0))
    assert depth == 2
    return pl.pallas_call(
        body, name=name, out_shape=[jax.ShapeDtypeStruct(w.shape, F32)] * 4,
        grid_spec=pltpu.PrefetchScalarGridSpec(
            num_scalar_prefetch=1, grid=(depth, 2, nblk),
            in_specs=[wspec] * 3 + [gspec] * 4, out_specs=[wspec] * 4),
    )(cidx, w, m, v, own[0], own[1], oth[0], oth[1])


class _NS:
    def __init__(self, **kw):
        self.__dict__.update(kw)


def _prep_layer(win, wqb, wkvb, wout, w1, w2, conv_w, conv_b, dt_bias, a_log, ssd_d, ssd_nw, qnw, kvnw, pool_w,
                pool_scale, n1, n2):
    winp = jnp.concatenate([win[:, 0:384], win[:, 384:1280], win[:, 1292:1548], win[:, 1548:1804], win[:, 1836:2092],
                            win[:, 1804:1836], win[:, 1280:1292], jnp.zeros((D, NP - IN_COLS), win.dtype)], axis=1)
    wq = jnp.pad(wqb.reshape(256, MLA_HEADS, QK_DIM), ((0, 0), (0, 0), (0, HP - QK_DIM))).reshape(256, QW)
    kv3 = wkvb.reshape(256, MLA_HEADS, 128)
    wk = jnp.pad(kv3[:, :, :64], ((0, 0), (0, 0), (0, 64))).reshape(256, QW)
    wv = jnp.pad(kv3[:, :, 64:], ((0, 0), (0, 0), (0, 64))).reshape(256, QW)
    wo = jnp.concatenate([jnp.pad(wout[384:768].reshape(MLA_HEADS, 64, D), ((0, 0), (0, 64), (0, 0))).reshape(QW, D),
                          wout[0:384], wout[768:1024]], axis=0)
    wbd = (jnp.eye(4, dtype=F32)[:, None, :, None] * pool_w[:, :, None, :]).reshape(POOL_DIM, POOL_DIM).astype(MXU)
    a = -jnp.exp(a_log)
    return _NS(
        winp=winp, wint=winp.T, wq=wq, wqt=wq.T, wk=wk, wkt=wk.T, wv=wv, wvt=wv.T, wo=wo, wot=wo.T,
        w1=w1, w1t=w1.T, w2=w2, w2t=w2.T, wbd=wbd, wbdt=wbd.T,
        cw8=jnp.pad(conv_w, ((0, 4), (0, 0))), cb=conv_b[None],
        dtb=jnp.pad(dt_bias.reshape(1, 12), ((0, 0), (DT0, 128 - DT0 - 12))),
        arow=jnp.pad(a[:, None, :], ((0, 0), (0, 7), (0, 128 - SSD_HEADS))), a=a,
        dexp=jnp.repeat(ssd_d, SSD_P)[None], ssd_nw=ssd_nw[None], qnw=qnw[None], kvnw=kvnw[None],
        pscale=pool_scale[None], n1=n1[None], n2=n2[None])


def _unprep_grads(dwinp, dwq, dwk, dwv, dwo):
    dwin = jnp.concatenate([dwinp[:, 0:384], dwinp[:, 384:1280], dwinp[:, 2080:2092], dwinp[:, 1280:1536],
                            dwinp[:, 1536:1792], dwinp[:, 2048:2080], dwinp[:, 1792:2048]], axis=1)
    dwqb = dwq.reshape(256, MLA_HEADS, HP)[:, :, :QK_DIM].reshape(256, MLA_HEADS * QK_DIM)
    dwkvb = jnp.concatenate([dwk.reshape(256, MLA_HEADS, HP)[:, :, :64], dwv.reshape(256, MLA_HEADS, HP)[:, :, :64]],
                            axis=2).reshape(256, MLA_HEADS * 128)
    dwout = jnp.concatenate([dwo[QW:QW + 384], dwo[0:QW].reshape(MLA_HEADS, HP, D)[:, :64].reshape(384, D),
                             dwo[QW + 384:CAT]], axis=0)
    return dwin, dwqb, dwkvb, dwout


def _rope_tables(nb, N):
    t = jnp.arange(N, dtype=F32)
    row = jnp.floor(t / GRID_W)
    col = t - row * GRID_W
    inv = jnp.asarray(10000.0 ** (-np.arange(8, dtype=np.float32) / 8), F32)
    ang = jnp.stack([row[:, None] * inv, col[:, None] * inv], axis=1)
    cs, sn = jnp.cos(ang), jnp.sin(ang)
    zero = jnp.zeros_like(sn)
    lanes = lambda first, second: jnp.stack([first, second], axis=2).reshape(N, 32)
    pad = lambda a, fill: jnp.concatenate([jnp.full((N, 64), fill, F32), a, jnp.full((N, 32), fill, F32)], axis=1)
    tabs = []
    for tab, fill in ((pad(lanes(cs, cs), 1.0), 1.0), (pad(lanes(-sn, zero), 0.0), 0.0), (pad(lanes(zero, sn), 0.0), 0.0)):
        one = jnp.concatenate([jnp.full((CTX, 128), fill, F32), tab], axis=0)
        tabs.append(jnp.tile(one, (nb, 1)))
    return tabs


def _eexp():
    e = np.zeros((128, SSD_INNER), np.float32)
    for h in range(SSD_HEADS):
        e[h, h * SSD_P:(h + 1) * SSD_P] = 1.0
    return jnp.asarray(e)


def _layer_fwd(X, bm, lw, cst, side=None):
    nb, T, bps, N = cst.nb, cst.T, cst.bps, cst.N
    h1, pz, pxbc, pqa, pkva, ppool, plast = in_proj(X, bm, lw.n1, lw.winp)
    xs, bmat, cmat, dtv = ssd_prep(pxbc, plast, lw.cw8, lw.cb, lw.dtb, bps)
    y2, hin, side_out = ssd_scan_fwd(xs, bmat, cmat, dtv, lw.arow, cst.eexp, nb, T, side)
    ssd = ssd_out_fwd(y2, xs, pz, lw.dexp, lw.ssd_nw)
    q, k, v, cq, ckv = mla_prep(pqa, pkva, plast, lw.qnw, lw.kvnw, lw.wq, lw.wk, lw.wv, *cst.rope)
    attn, lse = attn_fwd(q, k, v, nb, T)
    pool = pool_fwd(ppool, lw.wbd, lw.pscale, bps, N)
    x1, mix, cat = mix_fwd(X, attn, ssd, pool, bm, lw.wo)
    x2, mo, r, h2 = mlp_fwd(x1, bm, lw.n2, lw.w1, lw.w2)
    sv = _NS(X=X, h1=h1, pz=pz, pxbc=pxbc, pqa=pqa, pkva=pkva, ppool=ppool, plast=plast, xs=xs, bmat=bmat, cmat=cmat,
             dtv=dtv, y2=y2, hin=hin, q=q, k=k, v=v, cq=cq, ckv=ckv, attn=attn, lse=lse, x1=x1, mix=mix, cat=cat, mo=mo, r=r,
             h2=h2)
    return x2, sv, side_out


def _layer_bwd(dx2, bm, lw, sv, cst, side=None):
    nb, T, bps, N = cst.nb, cst.T, cst.bps, cst.N
    dx1, du, dob, part_mlp = mlp_bwd(dx2, sv.x1, sv.mo, sv.r, bm, lw.n2, lw.w2t, lw.w1t)
    dw1 = mm_tn(sv.h2, du, name="wgrad_mlp1", col_blocks=True)
    dw2 = mm_tn(sv.r, dob, square_a=True, name="wgrad_mlp2")
    dattn, dssd, dpool, dmb, part_mix = mix_bwd(dx1, sv.mix, bm, lw.wot)
    dwo = mm_tn(sv.cat, dmb, name="wgrad_out")
    dppool, dwbd, part_pool = pool_bwd(sv.ppool, dpool, lw.wbd, lw.wbdt, lw.pscale, bps, N)
    dq, dk, dv = attn_bwd(sv.q, sv.k, sv.v, sv.attn, sv.lse, dattn, nb, T)
    dpqa, dpkva, dkr, dql, dkm, dvb, part_mla = mla_prep_bwd(dq, dk, dv, sv.pqa, sv.pkva, lw.qnw, lw.kvnw, lw.wqt,
                                                             lw.wkt, lw.wvt, *cst.rope)
    dwq = mm_tn(sv.cq, dql, name="wgrad_q")
    dwk = mm_tn(sv.ckv, dkm, name="wgrad_k")
    dwv = mm_tn(sv.ckv, dvb, name="wgrad_v")
    dyy, dz, dxs_skip, part_so = ssd_out_bwd(dssd, sv.y2, sv.xs, sv.pz, lw.dexp, lw.ssd_nw)
    dxs2, dbm2, dcm2, ddt2, da, side_out = ssd_scan_bwd(sv.xs, sv.bmat, sv.cmat, sv.dtv, lw.arow, cst.eexp, sv.hin, dyy,
                                                        nb, T, side)
    dpre, dlast_dt, part_conv = ssd_prep_bwd_a(sv.pxbc, sv.plast, lw.cw8, lw.cb, lw.dtb, dxs_skip, dxs2, dbm2, dcm2,
                                               ddt2, bps)
    dpxbc = ssd_prep_bwd_b(dpre, lw.cw8, bps)
    dx, dpb, part_in = in_proj_bwd(dx1, sv.X, dz, dpxbc, dpqa, dpkva, dppool, dkr, dlast_dt, bm, lw.n1, lw.wint)
    dwinp = mm_tn(sv.h1, dpb, name="wgrad_in")

    dwin, dwqb, dwkvb, dwout = _unprep_grads(dwinp, dwq, dwk, dwv, dwo)
    dmod = jnp.stack([part_in[:, 0], part_in[:, 1], part_mix[:, 0], part_mlp[:, 0], part_mlp[:, 1], part_mlp[:, 2]],
                     axis=1)
    dmod = dmod.reshape(nb, bps, 6, D)
    dm_rows = jnp.concatenate([jnp.sum(dmod[:, 1:], axis=1), jnp.sum(dmod[:, 0], axis=0)[None]], axis=0)
    da_dh = jnp.sum(da.reshape(nb, 2, 8, 128)[:, :, 0, :SSD_HEADS], axis=0)
    conv_parts = jnp.sum(part_conv, axis=0)
    by_chip_cols = lambda a: jnp.stack([a[:, k * (a.shape[1] // 4):(k + 1) * (a.shape[1] // 4)] for k in range(4)])
    by_chip_rows = lambda a: a.reshape(4, a.shape[0] // 4, a.shape[1])
    g = _NS(
        w_in=by_chip_cols(dwin), w_q_b=by_chip_cols(dwqb), w_kv_b=by_chip_cols(dwkvb), w_out=by_chip_rows(dwout),
        w_mlp1=dw1, w_mlp2=by_chip_rows(dw2),
        dm_rows=dm_rows.reshape(3, 6 * D),
        norm1_w=jnp.sum(part_in[:, 2], axis=0), norm2_w=jnp.sum(part_mlp[:, 3], axis=0),
        conv_w=conv_parts[0:4], conv_b=conv_parts[4],
        dt_bias=conv_parts[5, DT0:DT0 + 12].reshape(2, SSD_HEADS), a_log=da_dh * lw.a,
        ssd_d=jnp.sum(jnp.sum(part_so[:, 1], axis=0).reshape(SSD_HEADS, SSD_P), axis=1),
        ssd_norm_w=jnp.sum(part_so[:, 0], axis=0),
        q_a_norm_w=jnp.sum(part_mla[:, 0], axis=0), kv_a_norm_w=jnp.sum(part_mla[:, 1], axis=0),
        pool_w=jnp.stack([dwbd[i * 64:(i + 1) * 64, i * 64:(i + 1) * 64] for i in range(4)]),
        pool_scale=jnp.sum(part_pool[:, 0], axis=0))
    return dx, g, side_out


def _local_step(x, ctx, tgt, bms, lws, fw, cst, hooks=None):
    nb, N = x.shape[0], x.shape[1]
    R = nb * cst.T
    X = jnp.concatenate([ctx, x], axis=1).reshape(R, D)
    saved, used = [], []
    for l in range(DEPTH):
        lw = lws[l] if hooks is None else hooks.weights(l)
        X, sv, out = _layer_fwd(X, bms[l], lw, cst, None if hooks is None else hooks.fwd_side(l))
        if hooks is not None:
            hooks.fwd_done(l, out)
        saved.append(sv)
        used.append(lw)
    dX, part_fin = final_loss(X, tgt.reshape(nb * N, D), fw[None], cst.bps)
    loss = (0.5 / D) * jnp.sum(part_fin[:, 1])
    dfw = jnp.sum(part_fin[:, 0], axis=0)
    grads = [None] * DEPTH
    for l in reversed(range(DEPTH)):
        dX, grads[l], out = _layer_bwd(dX, bms[l], used[l], saved[l], cst, None if hooks is None else hooks.bwd_side(l))
        if hooks is not None:
            hooks.bwd_done(l, grads[l], out)
    grad_x = dX.reshape(nb, cst.T, D)[:, CTX:, :]
    return loss, grad_x, grads, dfw


def _consts(nb, N):
    T = CTX + N
    bps = T // SB
    return _NS(nb=nb, N=N, T=T, bps=bps, eexp=_eexp(), rope=_rope_tables(nb, N))


def _block_mod(modrows, cst):
    rows = []
    for b in range(cst.nb):
        rows.append(modrows[cst.nb:cst.nb + 1])
        rows.append(jnp.broadcast_to(modrows[b:b + 1], (cst.bps - 1, 6, D)))
    return jnp.pad(jnp.concatenate(rows, axis=0), ((0, 0), (0, 2), (0, 0)))


SMALL = (("norm1_w", (2, D)), ("norm2_w", (2, D)), ("conv_w", (2, 4, XBC)), ("conv_b", (2, XBC)),
         ("dt_bias", (2, 2, 6)), ("a_log", (2, 2, 6)), ("ssd_d", (2, 6)), ("ssd_norm_w", (2, 384)),
         ("q_a_norm_w", (2, 256)), ("kv_a_norm_w", (2, 256)), ("pool_w", (2, 4, 64, 64)), ("pool_scale", (2, 256)),
         ("final_norm_w", (D,)), ("mod_b", (2, 6 * D)))
SMALL_ROWS = 64
DM_ROWS = 48


def _pack_small(vals):
    flat = jnp.concatenate([vals[n].reshape(-1) for n, _ in SMALL])
    return jnp.pad(flat, (0, SMALL_ROWS * D - flat.shape[0])).reshape(SMALL_ROWS, D)


def _unpack_small(p):
    flat = p.reshape(-1)
    out, off = {}, 0
    for n, shp in SMALL:
        size = int(np.prod(shp))
        out[n] = flat[off:off + size].reshape(shp)
        off += size
    return out


def cctx_grad(parts, c_ctx):
    def body(p_ref, c_ref, o_ref):
        acc = ((p_ref[0] + p_ref[1]) + p_ref[2]) + p_ref[3]
        v = c_ref[...]
        sig = _sigmoid(v)
        o_ref[...] = acc * (sig * (1.0 + v * (1.0 - sig)))

    return pl.pallas_call(
        body, name="cctx_grad", out_shape=jax.ShapeDtypeStruct((8, D), F32),
        in_specs=[_fullspec((4, 8, D)), _fullspec((1, D))], out_specs=_fullspec((8, D)), grid=(1,),
    )(parts, c_ctx)


def kernel(x, c, ctx, c_ctx, mod_w, mod_b, norm1_w, norm2_w, w_in, conv_w, conv_b, dt_bias, a_log, ssd_d, ssd_norm_w, q_a_norm_w, w_q_b, kv_a_norm_w, w_kv_b, pool_w, pool_scale, w_out, w_mlp1, w_mlp2, final_norm_w, loss_target, m_c_ctx, m_mod_w, m_mod_b, m_norm1_w, m_norm2_w, m_w_in, m_conv_w, m_conv_b, m_dt_bias, m_a_log, m_ssd_d, m_ssd_norm_w, m_q_a_norm_w, m_w_q_b, m_kv_a_norm_w, m_w_kv_b, m_pool_w, m_pool_scale, m_w_out, m_w_mlp1, m_w_mlp2, m_final_norm_w, v_c_ctx, v_mod_w, v_mod_b, v_norm1_w, v_norm2_w, v_w_in, v_conv_w, v_conv_b, v_dt_bias, v_a_log, v_ssd_d, v_ssd_norm_w, v_q_a_norm_w, v_w_q_b, v_kv_a_norm_w, v_w_kv_b, v_pool_w, v_pool_scale, v_w_out, v_w_mlp1, v_w_mlp2, v_final_norm_w):
    nb, N = x.shape[0], x.shape[1]
    cst = _consts(nb, N)
    xi, yi, ci = _place()
    me = 4 * xi + 2 * yi + ci
    kchip = 2 * xi + yi
    mcols = mod_w.shape[2]
    cshard = conv_w.shape[2]

    blk = jnp.zeros((16, D), F32).at[0:nb].set(c).at[8:16, 0:cshard].set(conv_w.reshape(8, cshard))
    g1 = allgather_small(blk, "gather_cond")
    cond = jnp.concatenate([g1[:, 0:nb].reshape(NDEV * nb, D), c_ctx[None],
                            jnp.zeros((MODR - NDEV * nb - 1, D), F32)], axis=0)
    conv_full = [jnp.concatenate([g1[2 * k, 8 + 4 * l:12 + 4 * l, 0:cshard] for k in range(4)], axis=1)
                 for l in range(DEPTH)]

    mb = [lax.dynamic_slice_in_dim(mod_b[l], kchip * mcols, mcols)[None] for l in range(DEPTH)]
    ms = jnp.concatenate([mod_fwd(cond, mod_w[l], mb[l]) for l in range(DEPTH)], axis=0)
    g2 = allgather_small(ms, "gather_mod")
    bms = []
    for l in range(DEPTH):
        m_all = jnp.concatenate([g2[2 * k, MODR * l:MODR * (l + 1)] for k in range(4)], axis=1)
        mine = jnp.concatenate([lax.dynamic_slice_in_dim(m_all, nb * me, nb), m_all[NDEV * nb:NDEV * nb + 1]], axis=0)
        bms.append(_block_mod(mine.reshape(nb + 1, 6, D), cst))

    assert DEPTH == 2
    big = (w_in, w_q_b, w_kv_b, w_out, w_mlp1, w_mlp2)
    names = ("w_in", "w_q_b", "w_kv_b", "w_out", "w_mlp1", "w_mlp2")
    cidx = jnp.reshape(ci, (1,)).astype(jnp.int32)
    kidx = jnp.reshape(kchip, (1,)).astype(jnp.int32)
    shards = [[a[l].astype(MXU) for a in big] for l in range(DEPTH)]

    def core_sums(g):
        gs = [getattr(g, n) for n in names]
        return [add_half(a, r, cidx, "add_half_" + n) for n, a, r in zip(names, gs, swap_core_halves(gs))]

    class Hooks:
        gathered = [run_job(gather_job(shards[0]), "gather_shards"), None]
        core_sum = [None, None]
        received = [None, None]

        def weights(self, l):
            full = [jnp.concatenate([jnp.where(kchip == k, a, g[k]) for k in range(4)], axis=ax)
                    for a, g, ax in zip(shards[l], self.gathered[l], (1, 1, 1, 0, 1, 0))]
            return _prep_layer(*full, conv_full[l], conv_b[l], dt_bias[l], a_log[l], ssd_d[l], ssd_norm_w[l],
                               q_a_norm_w[l], kv_a_norm_w[l], pool_w[l], pool_scale[l], norm1_w[l], norm2_w[l])

        def fwd_side(self, l):
            return gather_job(shards[1]) if l == 0 else None

        def fwd_done(self, l, out):
            if l == 0:
                self.gathered[1] = out

        def bwd_side(self, l):
            return chip_swap_job([s[1] for s in self.core_sum[1]]) if l == 0 else None

        def bwd_done(self, l, g, out):
            self.core_sum[l] = core_sums(g)
            if l == 0:
                self.received[1] = out
                self.received[0] = run_job(chip_swap_job([s[1] for s in self.core_sum[0]]), "swap_chip_parts")

    hooks = Hooks()
    loss_part, grad_x, grads, dfw = _local_step(x, ctx, loss_target, bms, None, final_norm_w, cst, hooks)
    loss = lax.psum(loss_part, ("x", "y", "c"))
    g_own = [sum_parts(hooks.core_sum[l][i][0], hooks.received[l][i], kidx, "sum_parts_" + n)
             for i, n in enumerate(names) for l in range(DEPTH)]
    g_oth = swap_reduced_halves(g_own)

    small = {n: jnp.stack([getattr(grads[l], n) for l in range(DEPTH)]) for n, _ in SMALL if n not in ("final_norm_w", "mod_b")}
    small["final_norm_w"] = dfw
    small["mod_b"] = jnp.stack([jnp.sum(grads[l].dm_rows, axis=0) for l in range(DEPTH)])
    dm = jnp.pad(jnp.concatenate([grads[l].dm_rows for l in range(DEPTH)], axis=0), ((0, 8 - 3 * DEPTH), (0, 0)))
    g3 = allgather_small(jnp.concatenate([_pack_small(small), dm.reshape(DM_ROWS, D)], axis=0), "gather_small")
    tot = sum_leading(g3, "sum_small")
    gsmall = _unpack_small(tot[0:SMALL_ROWS])
    ctx_sum = tot[SMALL_ROWS:].reshape(8, 6 * D)
    dm_dev = g3[:, SMALL_ROWS:].reshape(NDEV, 8, 6 * D)
    g_mod_w, dpart = [], jnp.zeros((8, D), F32)
    for l in range(DEPTH):
        dm_all = jnp.concatenate([dm_dev[:, 3 * l:3 * l + nb].reshape(NDEV * nb, 6 * D), ctx_sum[3 * l + nb:3 * l + nb + 1],
                                  jnp.zeros((MODR - NDEV * nb - 1, 6 * D), F32)], axis=0)
        g_mod_w.append(mod_wgrad(cond, lax.dynamic_slice_in_dim(dm_all, kchip * mcols, mcols, axis=1)))
        dctx = jnp.pad(lax.dynamic_slice_in_dim(ctx_sum[3 * l + nb:3 * l + nb + 1], kchip * mcols, mcols, axis=1), ((0, 7), (0, 0)))
        dpart = dpart + mod_dgrad(dctx, mod_w[l])
    g4 = allgather_small(dpart, "gather_cctx")
    g_c_ctx = cctx_grad(g4[0::2], c_ctx[None])[0]

    res = {}
    moments = ((m_w_in, v_w_in), (m_w_q_b, v_w_q_b), (m_w_kv_b, v_w_kv_b), (m_w_out, v_w_out), (m_w_mlp1, v_w_mlp1),
               (m_w_mlp2, v_w_mlp2))
    for i, (n, w, (m, v)) in enumerate(zip(names, big, moments)):
        res[n] = tuple(adamw_halves(w, m, v, g_own[DEPTH * i:DEPTH * (i + 1)], g_oth[DEPTH * i:DEPTH * (i + 1)], cidx,
                                    "adamw_" + n))
    g_mw = jnp.stack(g_mod_w)
    r_mw = adamw(mod_w.reshape(-1, mcols), g_mw.reshape(-1, mcols), m_mod_w.reshape(-1, mcols),
                 v_mod_w.reshape(-1, mcols), name="adamw_mod_w")
    res["mod_w"] = (g_mw,) + tuple(a.reshape(mod_w.shape) for a in r_mw)

    given = dict(norm1_w=(norm1_w, m_norm1_w, v_norm1_w), norm2_w=(norm2_w, m_norm2_w, v_norm2_w),
                 conv_b=(conv_b, m_conv_b, v_conv_b), dt_bias=(dt_bias, m_dt_bias, v_dt_bias),
                 a_log=(a_log, m_a_log, v_a_log), ssd_d=(ssd_d, m_ssd_d, v_ssd_d),
                 ssd_norm_w=(ssd_norm_w, m_ssd_norm_w, v_ssd_norm_w), q_a_norm_w=(q_a_norm_w, m_q_a_norm_w, v_q_a_norm_w),
                 kv_a_norm_w=(kv_a_norm_w, m_kv_a_norm_w, v_kv_a_norm_w), pool_w=(pool_w, m_pool_w, v_pool_w),
                 pool_scale=(pool_scale, m_pool_scale, v_pool_scale),
                 final_norm_w=(final_norm_w, m_final_norm_w, v_final_norm_w), mod_b=(mod_b, m_mod_b, v_mod_b))
    zero_cw = jnp.zeros((2, 4, XBC), F32)
    packs = [_pack_small({n: (given[n][i] if n in given else zero_cw) for n, _ in SMALL}) for i in range(3)]
    r_small = [_unpack_small(a) for a in adamw(packs[0], tot[0:SMALL_ROWS], packs[1], packs[2], name="adamw_small")]
    for n in given:
        res[n] = (gsmall[n], r_small[0][n], r_small[1][n], r_small[2][n])

    g_cw = lax.dynamic_slice_in_dim(gsmall["conv_w"], kchip * cshard, cshard, axis=2)
    padcw = lambda a: jnp.pad(a.reshape(8, cshard), ((0, 0), (0, 256 - cshard)))
    r_cw = adamw(padcw(conv_w), padcw(g_cw), padcw(m_conv_w), padcw(v_conv_w), name="adamw_conv_w")
    res["conv_w"] = (g_cw,) + tuple(a[:, 0:cshard].reshape(conv_w.shape) for a in r_cw)
    r_cc = adamw(c_ctx.reshape(8, 128), g_c_ctx.reshape(8, 128), m_c_ctx.reshape(8, 128), v_c_ctx.reshape(8, 128),
                 name="adamw_c_ctx")
    res["c_ctx"] = (g_c_ctx,) + tuple(a.reshape(D) for a in r_cc)

    order = ("c_ctx", "mod_w", "mod_b", "norm1_w", "norm2_w", "w_in", "conv_w", "conv_b", "dt_bias", "a_log", "ssd_d",
             "ssd_norm_w", "q_a_norm_w", "w_q_b", "kv_a_norm_w", "w_kv_b", "pool_w", "pool_scale", "w_out", "w_mlp1",
             "w_mlp2", "final_norm_w")
    return (loss, grad_x) + tuple(res[n][i] for i in range(4) for n in order)
```

```python
import functools
import math

import numpy as np
import jax
import jax.numpy as jnp
from jax import lax
from jax.experimental import pallas as pl
from jax.experimental.pallas import tpu as pltpu

F32 = jnp.float32
BF16 = jnp.bfloat16
MXU = jnp.bfloat16

D = 1024
DEPTH = 2
GRID_W = 64
CTX = 256
EPS = 1e-6
SSD_HEADS = 6
SSD_P = 64
SSD_INNER = 384
SSD_N = 128
CHUNK = 128
XBC = 896
MLA_HEADS = 6
QK_NOPE = 64
QK_ROPE = 32
QK_DIM = 96
HP = 128
QW = MLA_HEADS * HP
POOL_DIM = 256
D_FF = 4096
FF_BLK = 1024
IN_COLS = 2092
NP = 2176
P_SPLITS = (384, 896, 256, 256, 256, 128)
DT0 = 32
CAT = QW + SSD_INNER + POOL_DIM

SB = 256
TM = 512
HALO = 8

ADAM_LR = 0.001
ADAM_B1 = 0.9
ADAM_B2 = 0.999
ADAM_EPS = 1e-08
ADAM_WD = 0.01
ADAM_STEP = 10

NT = (((1,), (1,)), ((), ()))
TN = (((0,), (0,)), ((), ()))


def _cp(vmem_mb=None):
    if vmem_mb is None:
        return pltpu.CompilerParams()
    return pltpu.CompilerParams(vmem_limit_bytes=vmem_mb << 20)


def _dot(a, b):
    return jnp.dot(a, b, preferred_element_type=F32)


def _dotg(a, b, dims):
    return lax.dot_general(a, b, dims, preferred_element_type=F32)


def _dot_hi(a, b, dims=None, sel_first=False):
    dims = (((1,), (0,)), ((), ())) if dims is None else dims
    v, s = (b, a) if sel_first else (a, b)
    hi = v.astype(BF16)
    lo = (v - hi.astype(F32)).astype(BF16)
    s = s.astype(BF16)
    if sel_first:
        return _dotg(s, hi, dims) + _dotg(s, lo, dims)
    return _dotg(hi, s, dims) + _dotg(lo, s, dims)


def _rms_hat(x):
    rstd = lax.rsqrt(jnp.mean(x * x, axis=-1, keepdims=True) + EPS)
    return x * rstd, rstd


def _rms_bwd(dn, xhat, rstd, w):
    dxhat = dn * w
    dx = rstd * (dxhat - xhat * jnp.mean(dxhat * xhat, axis=-1, keepdims=True))
    return dx, jnp.sum(dn * xhat, axis=0, keepdims=True)


def _sigmoid(z):
    return 1.0 / (1.0 + jnp.exp(-z))


def _colsum(a):
    return jnp.sum(a, axis=0, keepdims=True)


def _rowspec(cols, tm=TM):
    return pl.BlockSpec((tm, cols), lambda i: (i, 0))


def _fullspec(shape):
    n = len(shape)
    return pl.BlockSpec(shape, lambda *_: (0,) * n)


def _resident(shape):
    n = len(shape)
    return pl.BlockSpec(shape, lambda *_: (0,) * n, pipeline_mode=pl.Buffered(1))


def _halo_specs(cols, nrows):
    per = SB // HALO
    last = nrows // HALO - 1
    prev = pl.BlockSpec((HALO, cols), lambda i: (jnp.maximum(i * per - 1, 0), 0))
    nxt = pl.BlockSpec((HALO, cols), lambda i: (jnp.minimum((i + 1) * per, last), 0))
    return prev, nxt


def _ext_rows(cur, prev, nxt, i, blocks_per_sample):
    j = i % blocks_per_sample
    first = jnp.logical_or(j == 0, j == 1)
    last = jnp.logical_or(j == 0, j == blocks_per_sample - 1)
    p = jnp.where(first, 0.0, prev)
    n = jnp.where(last, 0.0, nxt)
    return jnp.concatenate([p, cur, n], axis=0)


def _shift(ext, s):
    n = ext.shape[0]
    return pltpu.roll(ext, (-s) % n, axis=0)[HALO:HALO + SB, :]


def in_proj(x, bm, nw, w):
    R = x.shape[0]

    def body(x_ref, bm_ref, nw_ref, w_ref, h_ref, *outs):
        for s in range(TM // SB):
            rows = slice(s * SB, (s + 1) * SB)
            xhat, _ = _rms_hat(x_ref[rows, :])
            h = xhat * nw_ref[...] * (1.0 + bm_ref[s, 1:2, :]) + bm_ref[s, 0:1, :]
            h_ref[rows, :] = h.astype(h_ref.dtype)
        p = _dot(h_ref[...], w_ref[...])
        off = 0
        for o, n in zip(outs, P_SPLITS):
            o[...] = p[:, off:off + n]
            off += n

    return pl.pallas_call(
        body, name="in_proj", grid=(R // TM,),
        in_specs=[_rowspec(D), pl.BlockSpec((TM // SB, 8, D), lambda i: (i, 0, 0)), _fullspec((1, D)),
                  _fullspec((D, NP))],
        out_specs=[_rowspec(D)] + [_rowspec(n) for n in P_SPLITS],
        out_shape=[jax.ShapeDtypeStruct((R, D), MXU)] + [jax.ShapeDtypeStruct((R, n), F32) for n in P_SPLITS],
        compiler_params=_cp(56),
    )(x, bm, nw, w)


def in_proj_bwd(dx1, x, dz, dxbc, dqa, dkva, dpool, dkr, ddt, bm, nw, wt):
    R = x.shape[0]

    def body(dx1_ref, x_ref, dz_ref, dxbc_ref, dqa_ref, dkva_ref, dpool_ref, dkr_ref, ddt_ref, bm_ref, nw_ref,
             wt_ref, dx_ref, dp_ref, part_ref):
        dp_ref[:, 0:384] = dz_ref[...].astype(dp_ref.dtype)
        dp_ref[:, 384:1280] = dxbc_ref[...].astype(dp_ref.dtype)
        dp_ref[:, 1280:1536] = dqa_ref[...].astype(dp_ref.dtype)
        dp_ref[:, 1536:1792] = dkva_ref[...].astype(dp_ref.dtype)
        dp_ref[:, 1792:2048] = dpool_ref[...].astype(dp_ref.dtype)
        dp_ref[:, 2048:2176] = (dkr_ref[...] + ddt_ref[...]).astype(dp_ref.dtype)
        dh = _dot(dp_ref[...], wt_ref[...])
        w = nw_ref[...]
        for s in range(TM // SB):
            rows = slice(s * SB, (s + 1) * SB)
            xhat, rstd = _rms_hat(x_ref[rows, :])
            dhs = dh[rows, :]
            sc1 = 1.0 + bm_ref[s, 1:2, :]
            dx, dnw = _rms_bwd(dhs * sc1, xhat, rstd, w)
            dx_ref[rows, :] = dx1_ref[rows, :] + dx
            part_ref[s] = jnp.concatenate(
                [_colsum(dhs), _colsum(dhs * xhat * w), dnw, jnp.zeros((5, D), F32)], axis=0)

    return pl.pallas_call(
        body, name="in_proj_bwd", grid=(R // TM,),
        in_specs=[_rowspec(D), _rowspec(D), _rowspec(384), _rowspec(896), _rowspec(256), _rowspec(256),
                  _rowspec(256), _rowspec(128), _rowspec(128),
                  pl.BlockSpec((TM // SB, 8, D), lambda i: (i, 0, 0)), _fullspec((1, D)), _fullspec((NP, D))],
        out_specs=[_rowspec(D), _rowspec(NP), pl.BlockSpec((TM // SB, 8, D), lambda i: (i, 0, 0))],
        out_shape=[jax.ShapeDtypeStruct((R, D), F32), jax.ShapeDtypeStruct((R, NP), MXU),
                   jax.ShapeDtypeStruct((R // SB, 8, D), F32)],
        compiler_params=_cp(56),
    )(dx1, x, dz, dxbc, dqa, dkva, dpool, dkr, ddt, bm, nw, wt)


def mix_fwd(x, attn, ssd, pool, bm, wo):
    R = x.shape[0]

    def body(x_ref, a_ref, s_ref, p_ref, bm_ref, wo_ref, x1_ref, mix_ref, cat_ref):
        cat_ref[:, 0:QW] = a_ref[...].astype(cat_ref.dtype)
        cat_ref[:, QW:QW + SSD_INNER] = s_ref[...].astype(cat_ref.dtype)
        cat_ref[:, QW + SSD_INNER:CAT] = p_ref[...].astype(cat_ref.dtype)
        mix = _dot(cat_ref[...], wo_ref[...])
        mix_ref[...] = mix
        for s in range(TM // SB):
            rows = slice(s * SB, (s + 1) * SB)
            x1_ref[rows, :] = x_ref[rows, :] + bm_ref[s, 2:3, :] * mix[rows, :]

    return pl.pallas_call(
        body, name="mix_fwd", grid=(R // TM,),
        in_specs=[_rowspec(D), _rowspec(QW), _rowspec(SSD_INNER), _rowspec(POOL_DIM),
                  pl.BlockSpec((TM // SB, 8, D), lambda i: (i, 0, 0)), _fullspec((CAT, D))],
        out_specs=[_rowspec(D), _rowspec(D), _rowspec(CAT)],
        out_shape=[jax.ShapeDtypeStruct((R, D), F32), jax.ShapeDtypeStruct((R, D), F32),
                   jax.ShapeDtypeStruct((R, CAT), MXU)],
        compiler_params=_cp(48),
    )(x, attn, ssd, pool, bm, wo)


def mix_bwd(dx1, mix, bm, wot):
    R = dx1.shape[0]

    def body(dx1_ref, mix_ref, bm_ref, wot_ref, da_ref, ds_ref, dpl_ref, dmb_ref, part_ref):
        for s in range(TM // SB):
            rows = slice(s * SB, (s + 1) * SB)
            d = dx1_ref[rows, :]
            dmb_ref[rows, :] = (d * bm_ref[s, 2:3, :]).astype(dmb_ref.dtype)
            part_ref[s] = jnp.concatenate([_colsum(d * mix_ref[rows, :]), jnp.zeros((7, D), F32)], axis=0)
        dcat = _dot(dmb_ref[...], wot_ref[...])
        da_ref[...] = dcat[:, 0:QW]
        ds_ref[...] = dcat[:, QW:QW + SSD_INNER]
        dpl_ref[...] = dcat[:, QW + SSD_INNER:CAT]

    return pl.pallas_call(
        body, name="mix_bwd", grid=(R // TM,),
        in_specs=[_rowspec(D), _rowspec(D), pl.BlockSpec((TM // SB, 8, D), lambda i: (i, 0, 0)),
                  _fullspec((D, CAT))],
        out_specs=[_rowspec(QW), _rowspec(SSD_INNER), _rowspec(POOL_DIM), _rowspec(D),
                   pl.BlockSpec((TM // SB, 8, D), lambda i: (i, 0, 0))],
        out_shape=[jax.ShapeDtypeStruct((R, QW), F32), jax.ShapeDtypeStruct((R, SSD_INNER), F32),
                   jax.ShapeDtypeStruct((R, POOL_DIM), F32), jax.ShapeDtypeStruct((R, D), MXU),
                   jax.ShapeDtypeStruct((R // SB, 8, D), F32)],
        compiler_params=_cp(48),
    )(dx1, mix, bm, wot)


def mlp_fwd(x1, bm, nw, w1, w2, side=None):
    R = x1.shape[0]

    def body(x1_ref, bm_ref, nw_ref, w1_ref, w2_ref, x2_ref, mo_ref, r_ref, h2_ref):
        for s in range(TM // SB):
            rows = slice(s * SB, (s + 1) * SB)
            xhat, _ = _rms_hat(x1_ref[rows, :])
            h = xhat * nw_ref[...] * (1.0 + bm_ref[s, 4:5, :]) + bm_ref[s, 3:4, :]
            h2_ref[rows, :] = h.astype(h2_ref.dtype)
        for j in range(D_FF // FF_BLK):
            cols = slice(j * FF_BLK, (j + 1) * FF_BLK)
            r = jnp.maximum(_dot(h2_ref[...], w1_ref[:, cols]), 0.0)
            r_ref[:, cols] = r.astype(r_ref.dtype)
            d = _dot((r * r).astype(MXU), w2_ref[cols, :])
            if j == 0:
                mo_ref[...] = d
            else:
                mo_ref[...] += d
        for s in range(TM // SB):
            rows = slice(s * SB, (s + 1) * SB)
            x2_ref[rows, :] = x1_ref[rows, :] + bm_ref[s, 5:6, :] * mo_ref[rows, :]

    grid = (R // TM,)
    body, side_in, side_out, side_shapes, side_scratch, side_args = _side_wrap(body, 5, 4, 0, side, grid)
    outs = pl.pallas_call(
        body, name="mlp_fwd" if side is None else "mlp_fwd_comm", grid=grid,
        in_specs=[_rowspec(D), pl.BlockSpec((TM // SB, 8, D), lambda i: (i, 0, 0)), _fullspec((1, D)),
                  _resident((D, D_FF)), _resident((D_FF, D))] + side_in,
        out_specs=[_rowspec(D), _rowspec(D), _rowspec(D_FF), _rowspec(D)] + side_out,
        out_shape=[jax.ShapeDtypeStruct((R, D), F32), jax.ShapeDtypeStruct((R, D), F32),
                   jax.ShapeDtypeStruct((R, D_FF), BF16), jax.ShapeDtypeStruct((R, D), MXU)] + side_shapes,
        scratch_shapes=side_scratch,
        compiler_params=_cp(56),
    )(x1, bm, nw, w1, w2, *side_args)
    return tuple(outs[:4]) + (list(outs[4:]),)


def mlp_bwd(dx2, x1, mo, r, bm, nw, w2t, w1t, side=None):
    R = x1.shape[0]

    def body(dx2_ref, x1_ref, mo_ref, r_ref, bm_ref, nw_ref, w2t_ref, w1t_ref, dx1_ref, du_ref, dob_ref, part_ref,
             acc_ref):
        for s in range(TM // SB):
            rows = slice(s * SB, (s + 1) * SB)
            dob_ref[rows, :] = (dx2_ref[rows, :] * bm_ref[s, 5:6, :]).astype(dob_ref.dtype)
        for j in range(D_FF // FF_BLK):
            cols = slice(j * FF_BLK, (j + 1) * FF_BLK)
            du = _dot(dob_ref[...], w2t_ref[:, cols]) * (2.0 * r_ref[:, cols].astype(F32))
            du_ref[:, cols] = du.astype(du_ref.dtype)
            d = _dot(du_ref[:, cols], w1t_ref[cols, :])
            if j == 0:
                acc_ref[...] = d
            else:
                acc_ref[...] += d
        w = nw_ref[...]
        for s in range(TM // SB):
            rows = slice(s * SB, (s + 1) * SB)
            xhat, rstd = _rms_hat(x1_ref[rows, :])
            dh = acc_ref[rows, :]
            dx, dnw = _rms_bwd(dh * (1.0 + bm_ref[s, 4:5, :]), xhat, rstd, w)
            d2 = dx2_ref[rows, :]
            dx1_ref[rows, :] = d2 + dx
            part_ref[s] = jnp.concatenate(
                [_colsum(dh), _colsum(dh * xhat * w), _colsum(d2 * mo_ref[rows, :]), dnw,
                 jnp.zeros((4, D), F32)], axis=0)

    grid = (R // TM,)
    body, side_in, side_out, side_shapes, side_scratch, side_args = _side_wrap(body, 8, 4, 1, side, grid)
    outs = pl.pallas_call(
        body, name="mlp_bwd" if side is None else "mlp_bwd_comm", grid=grid,
        in_specs=[_rowspec(D), _rowspec(D), _rowspec(D), _rowspec(D_FF),
                  pl.BlockSpec((TM // SB, 8, D), lambda i: (i, 0, 0)), _fullspec((1, D)),
                  _resident((D, D_FF)), _resident((D_FF, D))] + side_in,
        out_specs=[_rowspec(D), _rowspec(D_FF), _rowspec(D), pl.BlockSpec((TM // SB, 8, D), lambda i: (i, 0, 0))]
                  + side_out,
        out_shape=[jax.ShapeDtypeStruct((R, D), F32), jax.ShapeDtypeStruct((R, D_FF), MXU),
                   jax.ShapeDtypeStruct((R, D), MXU), jax.ShapeDtypeStruct((R // SB, 8, D), F32)] + side_shapes,
        scratch_shapes=[pltpu.VMEM((TM, D), F32)] + side_scratch,
        compiler_params=_cp(56),
    )(dx2, x1, mo, r, bm, nw, w2t, w1t, *side_args)
    return tuple(outs[:4]) + (list(outs[4:]),)


def mm_tn(a, b, square_a=False, name="mm_tn", col_blocks=False):
    R, M = a.shape
    N = b.shape[1]
    tm = M if M <= 1408 else 1024
    tn = N if N <= 2176 else 1024
    tk = next((c for c in ((2176, 1088, 512) if tm + tn <= 2048 else (1088, 512)) if R % c == 0), R)
    assert not col_blocks or tm == M

    def body(a_ref, b_ref, o_ref):
        @pl.when(pl.program_id(2) == 0)
        def _():
            o_ref[...] = jnp.zeros_like(o_ref)

        av = a_ref[...]
        if square_a:
            av = av.astype(F32)
            av = (av * av).astype(MXU)
        prod = _dotg(av.astype(MXU), b_ref[...].astype(MXU), TN)
        if col_blocks:
            o_ref[0] += prod
        else:
            o_ref[...] += prod

    if col_blocks:
        out_spec = pl.BlockSpec((1, tm, tn), lambda i, j, k: (j, 0, 0))
        out_shape = jax.ShapeDtypeStruct((N // tn, M, tn), F32)
    else:
        out_spec = pl.BlockSpec((tm, tn), lambda i, j, k: (i, j))
        out_shape = jax.ShapeDtypeStruct((M, N), F32)
    return pl.pallas_call(
        body, name=name, grid=(M // tm, N // tn, R // tk),
        in_specs=[pl.BlockSpec((tk, tm), lambda i, j, k: (k, i)), pl.BlockSpec((tk, tn), lambda i, j, k: (k, j))],
        out_specs=out_spec, out_shape=out_shape,
        compiler_params=_cp(48),
    )(a, b)


def final_loss(x, tgt, fw, blocks_per_sample):
    R = x.shape[0]
    nxb = blocks_per_sample - 1

    def body(x_ref, t_ref, fw_ref, dx_ref, part_ref):
        i = pl.program_id(0)
        is_ctx = (i % blocks_per_sample) == 0
        xhat, rstd = _rms_hat(x_ref[...])
        w = fw_ref[...]
        err = xhat * w - t_ref[...]
        dx, dfw = _rms_bwd(err * (1.0 / D), xhat, rstd, w)
        keep = jnp.where(is_ctx, 0.0, 1.0)
        dx_ref[...] = dx * keep
        part_ref[0] = jnp.concatenate([dfw * keep, _colsum(err * err) * keep, jnp.zeros((6, D), F32)], axis=0)

    def tmap(i):
        return ((i // blocks_per_sample) * nxb + jnp.maximum(i % blocks_per_sample - 1, 0), 0)

    return pl.pallas_call(
        body, name="final_loss", grid=(R // SB,),
        in_specs=[_rowspec(D, SB), pl.BlockSpec((SB, D), tmap), _fullspec((1, D))],
        out_specs=[_rowspec(D, SB), pl.BlockSpec((1, 8, D), lambda i: (i, 0, 0))],
        out_shape=[jax.ShapeDtypeStruct((R, D), F32), jax.ShapeDtypeStruct((R // SB, 8, D), F32)],
    )(x, tgt, fw)


def _softplus(v):
    return jnp.maximum(v, 0.0) + jnp.log(1.0 + jnp.exp(-jnp.abs(v)))


def _conv_out(ext, cw_ref, cb_ref):
    return (cb_ref[...] + cw_ref[0:1, :] * _shift(ext, -1) + cw_ref[1:2, :] * _shift(ext, 0)
            + cw_ref[2:3, :] * _shift(ext, 1) + cw_ref[3:4, :] * _shift(ext, 2))


def _dt_dir(v, d):
    lane = lax.broadcasted_iota(jnp.int32, v.shape, 1)
    return jnp.where(lane < SSD_HEADS, pltpu.roll(v, (128 - DT0 - SSD_HEADS * d) % 128, axis=1), 0.0)


def ssd_prep(pxbc, plast, cw, cb, dtb, blocks_per_sample):
    R = pxbc.shape[0]
    prev, nxt = _halo_specs(XBC, R)

    def body(cur_ref, prev_ref, nxt_ref, pl_ref, cw_ref, cb_ref, dtb_ref, xs_ref, bm_ref, cm_ref, dt_ref):
        i = pl.program_id(0)
        ext = _ext_rows(cur_ref[...], prev_ref[...], nxt_ref[...], i, blocks_per_sample)
        co = _conv_out(ext, cw_ref, cb_ref)
        a = co * _sigmoid(co)
        xs_ref[...] = a[:, 0:384]
        bm_ref[...] = a[:, 384:640]
        cm_ref[...] = a[:, 640:896]
        sp = _softplus(pl_ref[...] + dtb_ref[...])
        dt_ref[0] = _dt_dir(sp, 0)
        dt_ref[1] = _dt_dir(sp, 1)

    return pl.pallas_call(
        body, name="ssd_prep", grid=(R // SB,),
        in_specs=[_rowspec(XBC, SB), prev, nxt, _rowspec(128, SB), _fullspec((8, XBC)), _fullspec((1, XBC)),
                  _fullspec((1, 128))],
        out_specs=[_rowspec(384, SB), _rowspec(256, SB), _rowspec(256, SB),
                   pl.BlockSpec((2, SB, 128), lambda i: (0, i, 0))],
        out_shape=[jax.ShapeDtypeStruct((R, 384), F32), jax.ShapeDtypeStruct((R, 256), F32),
                   jax.ShapeDtypeStruct((R, 256), F32), jax.ShapeDtypeStruct((2, R, 128), F32)],
    )(pxbc, pxbc, pxbc, plast, cw, cb, dtb)


def _chunk_index(d, s, nc):
    nctx = CTX // CHUNK
    back = jnp.where(s < nctx, nctx - 1 - s, nc + nctx - 1 - s)
    return jnp.where(d == 0, s, back)


def _scan_common(d, dt, arow, eexp, xs):
    ii = lax.broadcasted_iota(jnp.int32, (CHUNK, CHUNK), 0)
    jj = lax.broadcasted_iota(jnp.int32, (CHUNK, CHUNK), 1)
    mask = ((ii - jj) * (1 - 2 * d)) >= 0
    adt = dt * arow
    tmat = jnp.where(mask, 1.0, 0.0)
    cs = _dot_hi(tmat, adt, sel_first=True)
    tot = _colsum(adt)
    dtx = _dot_hi(dt, eexp)
    xt = xs * dtx
    ecs = jnp.exp(cs)
    ecx = _dot_hi(ecs, eexp)
    dte = jnp.exp(tot - cs)
    dtex = _dot_hi(dte, eexp)
    etot = jnp.exp(tot)
    etx = _dot_hi(jnp.broadcast_to(etot, (8, 128)), eexp)[0:1, :]
    return mask, tmat, adt, cs, tot, dtx, xt, ecs, ecx, dte, dtex, etot, etx


def _decay_matrix(mask, cs, cst, h):
    return jnp.exp(jnp.where(mask, cs[:, h:h + 1] - cst[h:h + 1, :], -1e30))


def _side_wrap(body, n_in, n_out, n_scratch, side, grid):
    if side is None:
        return body, [], [], [], [], []
    ni, no = len(side.ins), len(side.out_shapes)

    def wrapped(*refs):
        ins, refs = refs[:n_in], refs[n_in:]
        side_ins, refs = refs[:ni], refs[ni:]
        outs, refs = refs[:n_out], refs[n_out:]
        side_outs, refs = refs[:no], refs[no:]
        scratch, sems = refs[:n_scratch], refs[n_scratch:]
        ids = [pl.program_id(a) for a in range(len(grid))]
        first = functools.reduce(jnp.logical_and, [i == 0 for i in ids])
        last = functools.reduce(jnp.logical_and, [i == g - 1 for i, g in zip(ids, grid)])
        pl.when(first)(lambda: side.start(side_ins, side_outs, sems))
        body(*ins, *outs, *scratch)
        pl.when(last)(lambda: side.finish(side_ins, side_outs, sems))

    return wrapped, [ANY] * ni, [ANY] * no, list(side.out_shapes), _sems(side.nsem), list(side.ins)


def ssd_scan_fwd(xs, bm, cm, dtv, arow, eexp, nb, T, side=None):
    R = xs.shape[0]
    nc = T // CHUNK

    def rowmap(b, d, s):
        return (b * nc + _chunk_index(d, s, nc), 0)

    def body(xs_ref, bm_ref, cm_ref, dt_ref, a_ref, e_ref, y_ref, hin_ref, st_ref):
        d = pl.program_id(1)
        s = pl.program_id(2)

        @pl.when(s == 0)
        def _():
            st_ref[...] = jnp.zeros_like(st_ref)

        eexp = e_ref[...]
        mask, _, _, cs, _, _, xt, _, ecx, _, dtex, _, etx = _scan_common(
            d, dt_ref[0], a_ref[0, 0:1, :], eexp, xs_ref[...])
        cst = cs.T
        sin = st_ref[...]
        hin_ref[0] = sin
        sb = sin.astype(MXU)
        xtb = xt.astype(MXU)
        xw = (xt * dtex).astype(MXU)
        g0 = lax.broadcasted_iota(jnp.int32, (CHUNK, SSD_INNER), 1) < 192
        lane = lax.broadcasted_iota(jnp.int32, (CHUNK, 128), 1)
        c = [cm_ref[:, 0:128].astype(MXU), cm_ref[:, 128:256].astype(MXU)]
        b = [bm_ref[:, 0:128].astype(MXU), bm_ref[:, 128:256].astype(MXU)]
        y = jnp.where(g0, _dot(c[0], sb), _dot(c[1], sb)) * ecx
        cb = [_dotg(c[0], b[0], NT), _dotg(c[1], b[1], NT)]
        blocks = []
        for blk in range(3):
            acc = None
            for hh in range(2):
                h = blk * 2 + hh
                m = (cb[h // 3] * _decay_matrix(mask, cs, cst, h)).astype(MXU)
                res = _dot(m, xtb[:, blk * 128:(blk + 1) * 128])
                acc = res if hh == 0 else jnp.where(lane < 64, acc, res)
            blocks.append(acc)
        y_ref[0] = y + jnp.concatenate(blocks, axis=1)
        st_ref[...] = sin * etx + jnp.where(g0, _dotg(b[0], xw, TN), _dotg(b[1], xw, TN))

    grid = (nb, 2, nc)
    body, side_in, side_out, side_shapes, side_scratch, side_args = _side_wrap(body, 6, 2, 1, side, grid)
    outs = pl.pallas_call(
        body, name="ssd_scan_fwd" if side is None else "ssd_scan_fwd_comm", grid=grid,
        in_specs=[pl.BlockSpec((CHUNK, 384), rowmap), pl.BlockSpec((CHUNK, 256), rowmap),
                  pl.BlockSpec((CHUNK, 256), rowmap),
                  pl.BlockSpec((1, CHUNK, 128), lambda b, d, s: (d, b * nc + _chunk_index(d, s, nc), 0)),
                  pl.BlockSpec((1, 8, 128), lambda b, d, s: (d, 0, 0)),
                  pl.BlockSpec((128, 384), lambda b, d, s: (0, 0))] + side_in,
        out_specs=[pl.BlockSpec((1, CHUNK, 384), lambda b, d, s: (d, b * nc + _chunk_index(d, s, nc), 0)),
                   pl.BlockSpec((1, CHUNK, 384), lambda b, d, s: ((b * 2 + d) * nc + _chunk_index(d, s, nc), 0, 0))]
                  + side_out,
        out_shape=[jax.ShapeDtypeStruct((2, R, 384), F32), jax.ShapeDtypeStruct((nb * 2 * nc, CHUNK, 384), F32)]
                  + side_shapes,
        scratch_shapes=[pltpu.VMEM((CHUNK, 384), F32)] + side_scratch,
    )(xs, bm, cm, dtv, arow, eexp, *side_args)
    return outs[0], outs[1], list(outs[2:])


def ssd_scan_bwd(xs, bm, cm, dtv, arow, eexp, hin, dy, nb, T, side=None):
    R = xs.shape[0]
    nc = T // CHUNK

    def chunk(d, s):
        return _chunk_index(d, nc - 1 - s, nc)

    def rowmap(b, d, s):
        return (b * nc + chunk(d, s), 0)

    def dirmap(b, d, s):
        return (d, b * nc + chunk(d, s), 0)

    def body(xs_ref, bm_ref, cm_ref, dt_ref, a_ref, e_ref, hin_ref, dy_ref,
             dxs_ref, dbm_ref, dcm_ref, ddt_ref, da_ref, ds_ref):
        d = pl.program_id(1)
        s = pl.program_id(2)

        @pl.when(s == 0)
        def _():
            ds_ref[...] = jnp.zeros_like(ds_ref)
            da_ref[...] = jnp.zeros_like(da_ref)

        eexp = e_ref[...]
        dt = dt_ref[0]
        arow = a_ref[0, 0:1, :]
        xs_v = xs_ref[...]
        mask, tmat, adt, cs, tot, dtx, xt, ecs, ecx, dte, dtex, etot, etx = _scan_common(d, dt, arow, eexp, xs_v)
        cst = cs.T
        sin = hin_ref[0]
        sb = sin.astype(MXU)
        dsp = ds_ref[...]
        dyv = dy_ref[...]
        xtb = xt.astype(MXU)
        xw = (xt * dtex).astype(MXU)
        g0 = lax.broadcasted_iota(jnp.int32, (CHUNK, SSD_INNER), 1) < 192
        lane = lax.broadcasted_iota(jnp.int32, (CHUNK, 128), 1)
        sub = lax.broadcasted_iota(jnp.int32, (CHUNK, 128), 0)
        c = [cm_ref[:, 0:128].astype(MXU), cm_ref[:, 128:256].astype(MXU)]
        b = [bm_ref[:, 0:128].astype(MXU), bm_ref[:, 128:256].astype(MXU)]

        cs_prod = jnp.where(g0, _dot(c[0], sb), _dot(c[1], sb))
        dcsp = dyv * ecx
        dcsp_g = [jnp.where(g0, dcsp, 0.0).astype(MXU), jnp.where(g0, 0.0, dcsp).astype(MXU)]
        dcs = _dot_hi(dyv * cs_prod, eexp, NT) * ecs
        dc = [_dotg(dcsp_g[0], sb, NT), _dotg(dcsp_g[1], sb, NT)]
        dsin = _dotg(c[0], dcsp_g[0], TN) + _dotg(c[1], dcsp_g[1], TN) + dsp * etx

        detx = _colsum(dsp * sin)
        dtot = _dot_hi(jnp.broadcast_to(detx, (8, SSD_INNER)), eexp, NT)[0:1, :] * etot
        dsp_g = [jnp.where(g0, dsp, 0.0).astype(MXU), jnp.where(g0, 0.0, dsp).astype(MXU)]
        dxw = _dot(b[0], dsp_g[0]) + _dot(b[1], dsp_g[1])
        db = [_dotg(xw, dsp_g[0], NT), _dotg(xw, dsp_g[1], NT)]
        dxt = dxw * dtex
        ddte = _dot_hi(dxw * xt, eexp, NT) * dte
        dtot = dtot + _colsum(ddte)
        dcs = dcs - ddte

        cb = [_dotg(c[0], b[0], NT), _dotg(c[1], b[1], NT)]
        dg = [jnp.zeros((CHUNK, CHUNK), F32), jnp.zeros((CHUNK, CHUNK), F32)]
        dcs_rows = jnp.zeros((CHUNK, 128), F32)
        dxt_blocks = []
        for blk in range(3):
            acc = jnp.zeros((CHUNK, 128), F32)
            for hh in range(2):
                h = blk * 2 + hh
                g = h // 3
                mine = (lane < 64) if hh == 0 else (lane >= 64)
                dyh = jnp.where(mine, dyv[:, blk * 128:(blk + 1) * 128], 0.0).astype(MXU)
                lh = _decay_matrix(mask, cs, cst, h)
                m = cb[g] * lh
                dm = _dotg(dyh, xtb[:, blk * 128:(blk + 1) * 128], NT)
                acc = acc + _dotg(m.astype(MXU), dyh, TN)
                dg[g] = dg[g] + dm * lh
                q = dm * m
                dcs = dcs + jnp.where(lane == h, jnp.sum(q, axis=1, keepdims=True), 0.0)
                dcs_rows = dcs_rows - jnp.where(sub == h, jnp.sum(q, axis=0, keepdims=True), 0.0)
            dxt_blocks.append(acc)
        dxt = dxt + jnp.concatenate(dxt_blocks, axis=1)
        for g in range(2):
            dgb = dg[g].astype(MXU)
            dc[g] = dc[g] + _dot(dgb, b[g])
            db[g] = db[g] + _dotg(dgb, c[g], TN)
        dcs = dcs + dcs_rows.T

        dadt = _dot_hi(tmat, dcs, TN, sel_first=True) + dtot
        ddt = dadt * arow + _dot_hi(dxt * xs_v, eexp, NT)
        da_ref[0, 0:1, :] += _colsum(dadt * dt)
        dxs_ref[0] = dxt * dtx
        dbm_ref[0] = jnp.concatenate(db, axis=1)
        dcm_ref[0] = jnp.concatenate(dc, axis=1)
        ddt_ref[0] = ddt
        ds_ref[...] = dsin

    grid = (nb, 2, nc)
    body, side_in, side_out, side_shapes, side_scratch, side_args = _side_wrap(body, 8, 5, 1, side, grid)
    outs = pl.pallas_call(
        body, name="ssd_scan_bwd" if side is None else "ssd_scan_bwd_comm", grid=grid,
        in_specs=[pl.BlockSpec((CHUNK, 384), rowmap), pl.BlockSpec((CHUNK, 256), rowmap),
                  pl.BlockSpec((CHUNK, 256), rowmap), pl.BlockSpec((1, CHUNK, 128), dirmap),
                  pl.BlockSpec((1, 8, 128), lambda b, d, s: (d, 0, 0)),
                  pl.BlockSpec((128, 384), lambda b, d, s: (0, 0)),
                  pl.BlockSpec((1, CHUNK, 384), lambda b, d, s: ((b * 2 + d) * nc + chunk(d, s), 0, 0)),
                  pl.BlockSpec((CHUNK, 384), rowmap)] + side_in,
        out_specs=[pl.BlockSpec((1, CHUNK, 384), dirmap), pl.BlockSpec((1, CHUNK, 256), dirmap),
                   pl.BlockSpec((1, CHUNK, 256), dirmap), pl.BlockSpec((1, CHUNK, 128), dirmap),
                   pl.BlockSpec((1, 8, 128), lambda b, d, s: (b * 2 + d, 0, 0))] + side_out,
        out_shape=[jax.ShapeDtypeStruct((2, R, 384), F32), jax.ShapeDtypeStruct((2, R, 256), F32),
                   jax.ShapeDtypeStruct((2, R, 256), F32), jax.ShapeDtypeStruct((2, R, 128), F32),
                   jax.ShapeDtypeStruct((nb * 2, 8, 128), F32)] + side_shapes,
        scratch_shapes=[pltpu.VMEM((CHUNK, 384), F32)] + side_scratch,
    )(xs, bm, cm, dtv, arow, eexp, hin, dy, *side_args)
    return tuple(outs[:5]) + (list(outs[5:]),)


def _group_rms(g):
    lane = lax.broadcasted_iota(jnp.int32, g.shape, 1)
    g0 = lane < 192
    gg = g * g
    s0 = jnp.sum(jnp.where(g0, gg, 0.0), axis=-1, keepdims=True)
    s1 = jnp.sum(gg, axis=-1, keepdims=True) - s0
    rstd = jnp.where(g0, lax.rsqrt(s0 * (1.0 / 192) + EPS), lax.rsqrt(s1 * (1.0 / 192) + EPS))
    return rstd, g0


def ssd_out_fwd(y2, xs, pz, dexp, nw):
    R = xs.shape[0]

    def body(y_ref, xs_ref, z_ref, d_ref, nw_ref, o_ref):
        z = z_ref[...]
        yy = y_ref[0] + y_ref[1] + xs_ref[...] * d_ref[...]
        g = yy * (z * _sigmoid(z))
        rstd, _ = _group_rms(g)
        o_ref[...] = g * rstd * nw_ref[...]

    return pl.pallas_call(
        body, name="ssd_out_fwd", grid=(R // TM,),
        in_specs=[pl.BlockSpec((2, TM, 384), lambda i: (0, i, 0)), _rowspec(384), _rowspec(384),
                  _fullspec((1, 384)), _fullspec((1, 384))],
        out_specs=_rowspec(384),
        out_shape=jax.ShapeDtypeStruct((R, 384), F32),
    )(y2, xs, pz, dexp, nw)


def ssd_out_bwd(dout, y2, xs, pz, dexp, nw):
    R = xs.shape[0]

    def body(do_ref, y_ref, xs_ref, z_ref, d_ref, nw_ref, dy_ref, dz_ref, dxs_ref, part_ref):
        z = z_ref[...]
        xs_v = xs_ref[...]
        yy = y_ref[0] + y_ref[1] + xs_v * d_ref[...]
        sig = _sigmoid(z)
        sz = z * sig
        g = yy * sz
        rstd, g0 = _group_rms(g)
        ghat = g * rstd
        do = do_ref[...]
        dgn = do * nw_ref[...]
        t = dgn * ghat
        t0 = jnp.sum(jnp.where(g0, t, 0.0), axis=-1, keepdims=True)
        t1 = jnp.sum(t, axis=-1, keepdims=True) - t0
        dg = rstd * (dgn - ghat * jnp.where(g0, t0, t1) * (1.0 / 192))
        dyy = dg * sz
        dy_ref[...] = dyy
        dz_ref[...] = (dg * yy * (sig * (1.0 + z * (1.0 - sig)))).astype(dz_ref.dtype)
        dxs_ref[...] = dyy * d_ref[...]
        part_ref[0] = jnp.concatenate([_colsum(do * ghat), _colsum(dyy * xs_v), jnp.zeros((6, 384), F32)], axis=0)

    return pl.pallas_call(
        body, name="ssd_out_bwd", grid=(R // TM,),
        in_specs=[_rowspec(384), pl.BlockSpec((2, TM, 384), lambda i: (0, i, 0)), _rowspec(384), _rowspec(384),
                  _fullspec((1, 384)), _fullspec((1, 384))],
        out_specs=[_rowspec(384), _rowspec(384), _rowspec(384), pl.BlockSpec((1, 8, 384), lambda i: (i, 0, 0))],
        out_shape=[jax.ShapeDtypeStruct((R, 384), F32), jax.ShapeDtypeStruct((R, 384), MXU),
                   jax.ShapeDtypeStruct((R, 384), F32), jax.ShapeDtypeStruct((R // TM, 8, 384), F32)],
    )(dout, y2, xs, pz, dexp, nw)


def ssd_prep_bwd_a(pxbc, plast, cw, cb, dtb, dxs_skip, dxs2, dbm2, dcm2, ddt2, blocks_per_sample):
    R = pxbc.shape[0]
    prev, nxt = _halo_specs(XBC, R)

    def body(cur_ref, prev_ref, nxt_ref, pl_ref, cw_ref, cb_ref, dtb_ref, dsk_ref, dxs_ref, dbm_ref, dcm_ref, ddt_ref,
             dpre_ref, dlast_ref, part_ref):
        i = pl.program_id(0)
        ext = _ext_rows(cur_ref[...], prev_ref[...], nxt_ref[...], i, blocks_per_sample)
        co = _conv_out(ext, cw_ref, cb_ref)
        sig = _sigmoid(co)
        up = jnp.concatenate([dsk_ref[...] + dxs_ref[0] + dxs_ref[1], dbm_ref[0] + dbm_ref[1],
                              dcm_ref[0] + dcm_ref[1]], axis=1)
        dpre = up * (sig * (1.0 + co * (1.0 - sig)))
        dpre_ref[...] = dpre
        raw = pl_ref[...] + dtb_ref[...]
        lane = lax.broadcasted_iota(jnp.int32, raw.shape, 1)
        ddt = (pltpu.roll(ddt_ref[0], DT0, axis=1) + pltpu.roll(ddt_ref[1], DT0 + SSD_HEADS, axis=1))
        ddt = jnp.where(jnp.logical_and(lane >= DT0, lane < DT0 + 2 * SSD_HEADS), ddt * _sigmoid(raw), 0.0)
        dlast_ref[...] = ddt.astype(dlast_ref.dtype)
        rows = [_colsum(dpre * _shift(ext, k - 1)) for k in range(4)]
        rows.append(_colsum(dpre))
        rows.append(jnp.concatenate([_colsum(ddt), jnp.zeros((1, XBC - 128), F32)], axis=1))
        rows.append(jnp.zeros((2, XBC), F32))
        part_ref[0] = jnp.concatenate(rows, axis=0)

    dirspec = lambda n: pl.BlockSpec((2, SB, n), lambda i: (0, i, 0))
    return pl.pallas_call(
        body, name="ssd_prep_bwd_a", grid=(R // SB,),
        in_specs=[_rowspec(XBC, SB), prev, nxt, _rowspec(128, SB), _fullspec((8, XBC)), _fullspec((1, XBC)),
                  _fullspec((1, 128)), _rowspec(384, SB), dirspec(384), dirspec(256), dirspec(256), dirspec(128)],
        out_specs=[_rowspec(XBC, SB), _rowspec(128, SB), pl.BlockSpec((1, 8, XBC), lambda i: (i, 0, 0))],
        out_shape=[jax.ShapeDtypeStruct((R, XBC), F32), jax.ShapeDtypeStruct((R, 128), MXU),
                   jax.ShapeDtypeStruct((R // SB, 8, XBC), F32)],
    )(pxbc, pxbc, pxbc, plast, cw, cb, dtb, dxs_skip, dxs2, dbm2, dcm2, ddt2)


def ssd_prep_bwd_b(dpre, cw, blocks_per_sample):
    R = dpre.shape[0]
    prev, nxt = _halo_specs(XBC, R)

    def body(cur_ref, prev_ref, nxt_ref, cw_ref, o_ref):
        i = pl.program_id(0)
        ext = _ext_rows(cur_ref[...], prev_ref[...], nxt_ref[...], i, blocks_per_sample)
        o_ref[...] = (cw_ref[0:1, :] * _shift(ext, 1) + cw_ref[1:2, :] * _shift(ext, 0)
                      + cw_ref[2:3, :] * _shift(ext, -1) + cw_ref[3:4, :] * _shift(ext, -2)).astype(o_ref.dtype)

    return pl.pallas_call(
        body, name="ssd_prep_bwd_b", grid=(R // SB,),
        in_specs=[_rowspec(XBC, SB), prev, nxt, _fullspec((8, XBC))],
        out_specs=_rowspec(XBC, SB),
        out_shape=jax.ShapeDtypeStruct((R, XBC), MXU),
    )(dpre, dpre, dpre, cw)


def _rope(u, cos, sa, sb):
    return u * cos + pltpu.roll(u, 120, axis=1) * sa + pltpu.roll(u, 8, axis=1) * sb


def _rope_t(du, cos, sa, sb):
    return du * cos + pltpu.roll(du * sa, 8, axis=1) + pltpu.roll(du * sb, 120, axis=1)


def mla_prep(pqa, pkva, plast, qnw, kvnw, wq, wk, wv, cos, sa, sb):
    R = pqa.shape[0]

    def body(qa_ref, kva_ref, pl_ref, qnw_ref, kvnw_ref, wq_ref, wk_ref, wv_ref, cos_ref, sa_ref, sb_ref,
             q_ref, k_ref, v_ref, cq_ref, ckv_ref):
        cos_v, sa_v, sb_v = cos_ref[...], sa_ref[...], sb_ref[...]
        xq, _ = _rms_hat(qa_ref[...])
        cq_ref[...] = (xq * qnw_ref[...]).astype(cq_ref.dtype)
        xkv, _ = _rms_hat(kva_ref[...])
        ckv_ref[...] = (xkv * kvnw_ref[...]).astype(ckv_ref.dtype)
        q = _dot(cq_ref[...], wq_ref[...])
        kn = _dot(ckv_ref[...], wk_ref[...])
        v_ref[...] = _dot(ckv_ref[...], wv_ref[...]).astype(v_ref.dtype)
        lane = lax.broadcasted_iota(jnp.int32, (TM, HP), 1)
        rope_lanes = jnp.logical_and(lane >= QK_NOPE, lane < QK_DIM)
        kr = _rope(jnp.where(rope_lanes, pltpu.roll(pl_ref[...], QK_NOPE, axis=1), 0.0), cos_v, sa_v, sb_v)
        for h in range(MLA_HEADS):
            cols = slice(h * HP, (h + 1) * HP)
            q_ref[:, cols] = (_rope(q[:, cols], cos_v, sa_v, sb_v) * Q_SCALE).astype(q_ref.dtype)
            k_ref[:, cols] = (kn[:, cols] + kr).astype(k_ref.dtype)

    return pl.pallas_call(
        body, name="mla_prep", grid=(R // TM,),
        in_specs=[_rowspec(256), _rowspec(256), _rowspec(128), _fullspec((1, 256)), _fullspec((1, 256)),
                  _fullspec((256, QW)), _fullspec((256, QW)), _fullspec((256, QW)),
                  _rowspec(HP), _rowspec(HP), _rowspec(HP)],
        out_specs=[_rowspec(QW), _rowspec(QW), _rowspec(QW), _rowspec(256), _rowspec(256)],
        out_shape=[jax.ShapeDtypeStruct((R, QW), MXU)] * 3 + [jax.ShapeDtypeStruct((R, 256), MXU)] * 2,
    )(pqa, pkva, plast, qnw, kvnw, wq, wk, wv, cos, sa, sb)


def mla_prep_bwd(dq, dk, dv, pqa, pkva, qnw, kvnw, wqt, wkt, wvt, cos, sa, sb):
    R = pqa.shape[0]

    def body(dq_ref, dk_ref, dv_ref, qa_ref, kva_ref, qnw_ref, kvnw_ref, wqt_ref, wkt_ref, wvt_ref,
             cos_ref, sa_ref, sb_ref, dqa_ref, dkva_ref, dkr_ref, dql_ref, dkm_ref, dvb_ref, part_ref):
        cos_v, sa_v, sb_v = cos_ref[...], sa_ref[...], sb_ref[...]
        lane = lax.broadcasted_iota(jnp.int32, (TM, HP), 1)
        rope_lanes = jnp.logical_and(lane >= QK_NOPE, lane < QK_DIM)
        dkr = jnp.zeros((TM, HP), F32)
        for h in range(MLA_HEADS):
            cols = slice(h * HP, (h + 1) * HP)
            dql_ref[:, cols] = (_rope_t(dq_ref[:, cols], cos_v, sa_v, sb_v) * ATT_SCALE).astype(dql_ref.dtype)
            dkh = dk_ref[:, cols] * LN2
            dkm_ref[:, cols] = jnp.where(lane < QK_NOPE, dkh, 0.0).astype(dkm_ref.dtype)
            dkr = dkr + jnp.where(rope_lanes, dkh, 0.0)
        dvb_ref[...] = dv_ref[...].astype(dvb_ref.dtype)
        dkr = jnp.where(rope_lanes, _rope_t(dkr, cos_v, sa_v, sb_v), 0.0)
        dkr_ref[...] = pltpu.roll(dkr, HP - QK_NOPE, axis=1).astype(dkr_ref.dtype)
        xq, rq = _rms_hat(qa_ref[...])
        dqa, dqnw = _rms_bwd(_dot(dql_ref[...], wqt_ref[...]), xq, rq, qnw_ref[...])
        dqa_ref[...] = dqa.astype(dqa_ref.dtype)
        xkv, rkv = _rms_hat(kva_ref[...])
        dckv = _dot(dkm_ref[...], wkt_ref[...]) + _dot(dvb_ref[...], wvt_ref[...])
        dkva, dkvnw = _rms_bwd(dckv, xkv, rkv, kvnw_ref[...])
        dkva_ref[...] = dkva.astype(dkva_ref.dtype)
        part_ref[0] = jnp.concatenate([dqnw, dkvnw, jnp.zeros((6, 256), F32)], axis=0)

    return pl.pallas_call(
        body, name="mla_prep_bwd", grid=(R // TM,),
        in_specs=[_rowspec(QW), _rowspec(QW), _rowspec(QW), _rowspec(256), _rowspec(256), _fullspec((1, 256)),
                  _fullspec((1, 256)), _fullspec((QW, 256)), _fullspec((QW, 256)), _fullspec((QW, 256)),
                  _rowspec(HP), _rowspec(HP), _rowspec(HP)],
        out_specs=[_rowspec(256), _rowspec(256), _rowspec(128), _rowspec(QW), _rowspec(QW), _rowspec(QW),
                   pl.BlockSpec((1, 8, 256), lambda i: (i, 0, 0))],
        out_shape=[jax.ShapeDtypeStruct((R, 256), MXU), jax.ShapeDtypeStruct((R, 256), MXU),
                   jax.ShapeDtypeStruct((R, 128), MXU)] + [jax.ShapeDtypeStruct((R, QW), MXU)] * 3
                  + [jax.ShapeDtypeStruct((R // TM, 8, 256), F32)],
    )(dq, dk, dv, pqa, pkva, qnw, kvnw, wqt, wkt, wvt, cos, sa, sb)


ATT_SCALE = QK_DIM ** -0.5
TQ = 256


LOG2E = 1.4426950408889634
LN2 = 0.6931471805599453
Q_SCALE = ATT_SCALE * LOG2E


def _key_chunks(T, n=2):
    unit = 256 if T % 256 == 0 else 128
    units = T // unit
    sizes = [(units // n + (1 if i < units % n else 0)) * unit for i in range(n)]
    return [(sum(sizes[:i]), sz) for i, sz in enumerate(sizes) if sz]


def attn_fwd(q, k, v, nb, T):
    R = q.shape[0]
    nq = T // TQ
    chunks = _key_chunks(T, 4)

    def body(q_ref, k_ref, v_ref, o_ref, lse_ref):
        def logits(lo, n):
            return _dotg(q_ref[...], k_ref[lo:lo + n, :], NT)

        def weigh(s, lo, n):
            m = jnp.max(s, axis=-1, keepdims=True)
            p = jnp.exp2(s - m)
            return m, jnp.sum(p, axis=-1, keepdims=True), _dot(p.astype(MXU), v_ref[lo:lo + n, :])

        def parts_of(ranges):
            out, s = [], logits(*ranges[0])
            for j, (lo, n) in enumerate(ranges):
                nxt = logits(*ranges[j + 1]) if j + 1 < len(ranges) else None
                out.append(weigh(s, lo, n))
                s = nxt
            return out

        def finish(parts):
            m = parts[0][0]
            for pm, _, _ in parts[1:]:
                m = jnp.maximum(m, pm)
            l, o = 0.0, 0.0
            for pm, pl_, po in parts:
                a = jnp.exp2(pm - m)
                l = l + a * pl_
                o = o + a * po
            o_ref[...] = o / l
            lse_ref[...] = jnp.broadcast_to(m + jnp.log(l) * LOG2E, (TQ, HP))

        i = pl.program_id(2)
        pl.when(i == 0)(lambda: finish(parts_of([(0, CTX)])))
        pl.when(i > 0)(lambda: finish(parts_of(chunks)))

    qspec = pl.BlockSpec((TQ, HP), lambda b, h, i: (b * nq + i, h))
    kspec = pl.BlockSpec((T, HP), lambda b, h, i: (b, h))
    return pl.pallas_call(
        body, name="attn_fwd", grid=(nb, MLA_HEADS, nq),
        in_specs=[qspec, kspec, kspec], out_specs=[qspec, qspec],
        out_shape=[jax.ShapeDtypeStruct((R, QW), F32)] * 2,
        compiler_params=_cp(48),
    )(q, k, v)


def attn_bwd(q, k, v, o, lse, do, nb, T):
    R = q.shape[0]
    nq = T // TQ
    chunks = _key_chunks(T)

    def body(q_ref, k_ref, v_ref, o_ref, lse_ref, do_ref, dq_ref, dk_ref, dv_ref):
        i = pl.program_id(2)

        @pl.when(i == 0)
        def _():
            dk_ref[...] = jnp.zeros_like(dk_ref)
            dv_ref[...] = jnp.zeros_like(dv_ref)

        def run(chunks):
            qv = q_ref[...]
            dov = do_ref[...]
            dob = dov.astype(MXU)
            delta = jnp.sum(dov * o_ref[...], axis=-1, keepdims=True)
            lse_v = lse_ref[:, 0:1]
            dq = 0.0
            for lo, n in chunks:
                kv = k_ref[lo:lo + n, :]
                p = jnp.exp2(_dotg(qv, kv, NT) - lse_v)
                dp = _dotg(dob, v_ref[lo:lo + n, :], NT)
                dsb = (p * (dp - delta)).astype(MXU)
                dq = dq + _dot(dsb, kv)
                dk_ref[lo:lo + n, :] += _dotg(dsb, qv, TN)
                dv_ref[lo:lo + n, :] += _dotg(p.astype(MXU), dob, TN)
            dq_ref[...] = dq

        pl.when(i == 0)(lambda: run([(0, CTX)]))
        pl.when(i > 0)(lambda: run(chunks))

    qspec = pl.BlockSpec((TQ, HP), lambda b, h, i: (b * nq + i, h))
    kspec = pl.BlockSpec((T, HP), lambda b, h, i: (b, h))
    return pl.pallas_call(
        body, name="attn_bwd", grid=(nb, MLA_HEADS, nq),
        in_specs=[qspec, kspec, kspec, qspec, qspec, qspec],
        out_specs=[qspec, kspec, kspec],
        out_shape=[jax.ShapeDtypeStruct((R, QW), F32)] * 3,
        compiler_params=_cp(56),
    )(q, k, v, o, lse, do)


def _pool_geometry(i, blocks_per_sample, seq):
    j = i % blocks_per_sample
    n = jnp.where(j == 0, CTX, seq)
    t0 = jnp.where(j == 0, 0, (j - 1) * SB) - HALO
    lane = lax.broadcasted_iota(jnp.int32, (SB + 2 * HALO, POOL_DIM), 1)
    t = lax.broadcasted_iota(jnp.int32, (SB + 2 * HALO, POOL_DIM), 0) + t0
    wh = jnp.where(lane < 64, 1, jnp.where(lane < 128, 2, jnp.where(lane < 192, 4, 8)))
    cnt = jnp.minimum(t + wh, n) - jnp.maximum(t - wh, 0)
    return lane, 1.0 / jnp.maximum(cnt, 1).astype(F32)


def _by_window(lane, c2, c4, c8, c16):
    return jnp.where(lane < 64, c2, jnp.where(lane < 128, c4, jnp.where(lane < 192, c8, c16)))


def _window_sums(ext, lane, first):
    n = ext.shape[0]
    r = lambda a, s: pltpu.roll(a, s % n, axis=0)
    c2 = ext + r(ext, first)
    c4 = r(c2, 1) + r(c2, -1)
    c8 = r(c4, 2) + r(c4, -2)
    c16 = r(c8, 4) + r(c8, -4)
    return _by_window(lane, c2, c4, c8, c16)


def _pool_delta(ext, lane, inv):
    return (_window_sums(ext, lane, 1) * inv - ext)[HALO:HALO + SB, :]


def pool_fwd(ppool, wbd, scale, blocks_per_sample, seq):
    R = ppool.shape[0]
    prev, nxt = _halo_specs(POOL_DIM, R)

    def body(cur_ref, prev_ref, nxt_ref, w_ref, s_ref, o_ref):
        i = pl.program_id(0)
        ext = _ext_rows(cur_ref[...], prev_ref[...], nxt_ref[...], i, blocks_per_sample)
        lane, inv = _pool_geometry(i, blocks_per_sample, seq)
        dlt = _pool_delta(ext, lane, inv)
        o_ref[...] = _dot(dlt.astype(MXU), w_ref[...]) * s_ref[...]

    return pl.pallas_call(
        body, name="pool_fwd", grid=(R // SB,),
        in_specs=[_rowspec(POOL_DIM, SB), prev, nxt, _fullspec((POOL_DIM, POOL_DIM)), _fullspec((1, POOL_DIM))],
        out_specs=_rowspec(POOL_DIM, SB),
        out_shape=jax.ShapeDtypeStruct((R, POOL_DIM), F32),
    )(ppool, ppool, ppool, wbd, scale)


def pool_bwd(ppool, dpool, wbd, wbdt, scale, blocks_per_sample, seq):
    R = ppool.shape[0]
    prev, nxt = _halo_specs(POOL_DIM, R)

    def body(cur_ref, prev_ref, nxt_ref, dcur_ref, dprev_ref, dnxt_ref, w_ref, wt_ref, s_ref, du_ref, dw_ref, part_ref):
        i = pl.program_id(0)

        @pl.when(i == 0)
        def _():
            dw_ref[...] = jnp.zeros_like(dw_ref)

        ext = _ext_rows(cur_ref[...], prev_ref[...], nxt_ref[...], i, blocks_per_sample)
        lane, inv = _pool_geometry(i, blocks_per_sample, seq)
        dlt = _pool_delta(ext, lane, inv).astype(MXU)
        dy = dcur_ref[...]
        part_ref[0] = jnp.concatenate([_colsum(dy * _dot(dlt, w_ref[...])), jnp.zeros((7, POOL_DIM), F32)], axis=0)
        dyp = (dy * s_ref[...]).astype(MXU)
        dw_ref[...] += _dotg(dlt, dyp, TN)
        dext = _ext_rows(dy, dprev_ref[...], dnxt_ref[...], i, blocks_per_sample)
        dd = _dot((dext * s_ref[...]).astype(MXU), wt_ref[...])
        du_ref[...] = (_window_sums(dd * inv, lane, -1) - dd)[HALO:HALO + SB, :].astype(du_ref.dtype)

    return pl.pallas_call(
        body, name="pool_bwd", grid=(R // SB,),
        in_specs=[_rowspec(POOL_DIM, SB), prev, nxt, _rowspec(POOL_DIM, SB), prev, nxt,
                  _fullspec((POOL_DIM, POOL_DIM)), _fullspec((POOL_DIM, POOL_DIM)), _fullspec((1, POOL_DIM))],
        out_specs=[_rowspec(POOL_DIM, SB), _fullspec((POOL_DIM, POOL_DIM)),
                   pl.BlockSpec((1, 8, POOL_DIM), lambda i: (i, 0, 0))],
        out_shape=[jax.ShapeDtypeStruct((R, POOL_DIM), MXU), jax.ShapeDtypeStruct((POOL_DIM, POOL_DIM), F32),
                   jax.ShapeDtypeStruct((R // SB, 8, POOL_DIM), F32)],
    )(ppool, ppool, ppool, dpool, dpool, dpool, wbd, wbdt, scale)


def adamw(w, g, m, v, name="adamw"):
    rows, cols = w.shape
    tr = rows
    for cand in (512, 256, 128, 64, 32, 16, 8):
        if rows % cand == 0:
            tr = cand
            break
    bc1 = 1.0 - ADAM_B1 ** ADAM_STEP
    bc2 = 1.0 - ADAM_B2 ** ADAM_STEP

    def body(w_ref, g_ref, m_ref, v_ref, d_ref, nm_ref, nv_ref):
        g_v = g_ref[...]
        nm = ADAM_B1 * m_ref[...] + (1.0 - ADAM_B1) * g_v
        nv = ADAM_B2 * v_ref[...] + (1.0 - ADAM_B2) * (g_v * g_v)
        nm_ref[...] = nm
        nv_ref[...] = nv
        d_ref[...] = -ADAM_LR * ((nm / bc1) / (jnp.sqrt(nv / bc2) + ADAM_EPS) + ADAM_WD * w_ref[...])

    spec = pl.BlockSpec((tr, cols), lambda i: (i, 0))
    return pl.pallas_call(
        body, name=name, grid=(rows // tr,),
        in_specs=[spec] * 4, out_specs=[spec] * 3,
        out_shape=[jax.ShapeDtypeStruct((rows, cols), F32)] * 3,
    )(w, g, m, v)


MODR = 32


def _silu(v):
    return v * _sigmoid(v)


def mod_fwd(cond, w, b):
    n = w.shape[1]

    def body(c_ref, w_ref, b_ref, o_ref):
        o_ref[...] = _dot(_silu(c_ref[...]).astype(MXU), w_ref[...].astype(MXU)) + b_ref[...]

    return pl.pallas_call(
        body, name="mod_fwd", out_shape=jax.ShapeDtypeStruct((MODR, n), F32),
        in_specs=[_fullspec((MODR, D)), _fullspec((D, n)), _fullspec((1, n))], out_specs=_fullspec((MODR, n)),
        grid=(1,), compiler_params=_cp(40),
    )(cond, w, b)


def mod_wgrad(cond, dm):
    n = dm.shape[1]

    def body(c_ref, d_ref, o_ref):
        o_ref[...] = _dotg(_silu(c_ref[...]).astype(MXU), d_ref[...].astype(MXU), TN)

    return pl.pallas_call(
        body, name="mod_wgrad", out_shape=jax.ShapeDtypeStruct((D, n), F32),
        in_specs=[_fullspec((MODR, D)), _fullspec((MODR, n))], out_specs=_fullspec((D, n)),
        grid=(1,), compiler_params=_cp(40),
    )(cond, dm)


def mod_dgrad(dm, w):
    n = w.shape[1]

    def body(d_ref, w_ref, o_ref):
        o_ref[...] = _dotg(d_ref[...].astype(MXU), w_ref[...].astype(MXU), NT)

    return pl.pallas_call(
        body, name="mod_dgrad", out_shape=jax.ShapeDtypeStruct((8, D), F32),
        in_specs=[_fullspec((8, n)), _fullspec((D, n))], out_specs=_fullspec((8, D)),
        grid=(1,), compiler_params=_cp(40),
    )(dm, w)


def sum_leading(a, name="sum_leading"):
    n, r, c = a.shape

    def body(a_ref, o_ref):
        acc = a_ref[0]
        for k in range(1, n):
            acc = acc + a_ref[k]
        o_ref[...] = acc

    return pl.pallas_call(
        body, name=name, out_shape=jax.ShapeDtypeStruct((r, c), F32),
        in_specs=[_fullspec((n, r, c))], out_specs=_fullspec((r, c)), grid=(1,),
    )(a)


MESH = pl.DeviceIdType.MESH
NDEV = 8
ANY = pl.BlockSpec(memory_space=pl.ANY)


def _place():
    return lax.axis_index("x"), lax.axis_index("y"), lax.axis_index("c")


def _other_chips(x, y):
    return [(1 - x, y), (x, 1 - y), (1 - x, 1 - y)]


def allgather_small(v, name):
    r, cols = v.shape

    def body(v_ref, o_ref, send_sems, recv_sems):
        x, y, c = _place()
        me = 4 * x + 2 * y + c
        o_ref[me] = v_ref[...]
        copies = []
        for rel in range(1, NDEV):
            peer = (1 - x if rel & 4 else x, 1 - y if rel & 2 else y, 1 - c if rel & 1 else c)
            cp = pltpu.make_async_remote_copy(src_ref=v_ref, dst_ref=o_ref.at[me], send_sem=send_sems.at[rel - 1],
                                              recv_sem=recv_sems.at[rel - 1], device_id=peer, device_id_type=MESH)
            cp.start()
            copies.append(cp)
        for cp in copies:
            cp.wait_recv()
        for cp in copies:
            cp.wait_send()

    return pl.pallas_call(
        body, name=name, out_shape=jax.ShapeDtypeStruct((NDEV, r, cols), F32),
        in_specs=[pl.BlockSpec(memory_space=pltpu.VMEM)], out_specs=pl.BlockSpec(memory_space=pltpu.VMEM),
        scratch_shapes=[pltpu.SemaphoreType.DMA((NDEV - 1,)), pltpu.SemaphoreType.DMA((NDEV - 1,))],
        compiler_params=_cp(40),
    )(v)


def _sems(n):
    return [pltpu.SemaphoreType.DMA((n,)), pltpu.SemaphoreType.DMA((n,))]


def gather_job(arrs):
    n = len(arrs)

    def copy(srcs, outs, sems, i, slot, kk, cc, to, from_src=False):
        hr = arrs[i].shape[0] // 2
        dst = outs[i].at[kk, pl.ds(cc * hr, hr), :]
        return pltpu.make_async_remote_copy(src_ref=srcs[i].at[pl.ds(cc * hr, hr), :] if from_src else dst, dst_ref=dst,
                                            send_sem=sems[0].at[slot * n + i], recv_sem=sems[1].at[slot * n + i],
                                            device_id=to, device_id_type=MESH)

    def start(srcs, outs, sems):
        x, y, c = _place()
        for j, (px, py) in enumerate(_other_chips(x, y)):
            for i in range(n):
                copy(srcs, outs, sems, i, j, 2 * x + y, c, (px, py, c), True).start()

    def finish(srcs, outs, sems):
        x, y, c = _place()
        sib = (x, y, 1 - c)
        chips = _other_chips(x, y)
        passed = []
        for j, (px, py) in enumerate(chips):
            for i in range(n):
                copy(srcs, outs, sems, i, j, 2 * px + py, c, (px, py, c)).wait_recv()
                cp = copy(srcs, outs, sems, i, 3 + j, 2 * px + py, c, sib)
                cp.start()
                passed.append(cp)
        for j, (px, py) in enumerate(chips):
            for i in range(n):
                copy(srcs, outs, sems, i, 3 + j, 2 * px + py, 1 - c, sib).wait_recv()
        for j, (px, py) in enumerate(chips):
            for i in range(n):
                copy(srcs, outs, sems, i, j, 2 * x + y, c, (px, py, c), True).wait_send()
        for cp in passed:
            cp.wait_send()

    return _NS(ins=list(arrs), out_shapes=[jax.ShapeDtypeStruct((4,) + a.shape, a.dtype) for a in arrs], nsem=6 * n,
               start=start, finish=finish)


def chip_swap_job(ss):
    n = len(ss)

    def copies(srcs, outs, sems):
        x, y, c = _place()
        return [pltpu.make_async_remote_copy(src_ref=srcs[i].at[2 * px + py], dst_ref=outs[i].at[j],
                                             send_sem=sems[0].at[j * n + i], recv_sem=sems[1].at[j * n + i],
                                             device_id=(px, py, c), device_id_type=MESH)
                for j, (px, py) in enumerate(_other_chips(x, y)) for i in range(n)]

    def start(srcs, outs, sems):
        for cp in copies(srcs, outs, sems):
            cp.start()

    def finish(srcs, outs, sems):
        for cp in copies(srcs, outs, sems):
            cp.wait()

    return _NS(ins=list(ss), out_shapes=[jax.ShapeDtypeStruct((3,) + s.shape[1:], s.dtype) for s in ss], nsem=3 * n,
               start=start, finish=finish)


def run_job(job, name):
    n, m = len(job.ins), len(job.out_shapes)

    def body(*refs):
        srcs, outs, sems = refs[:n], refs[n:n + m], refs[n + m:]
        job.start(srcs, outs, sems)
        job.finish(srcs, outs, sems)

    return pl.pallas_call(body, name=name, out_shape=job.out_shapes, in_specs=[ANY] * n, out_specs=[ANY] * m,
                          scratch_shapes=_sems(job.nsem))(*job.ins)


def swap_core_halves(gs):
    n = len(gs)

    def body(*refs):
        srcs, outs = refs[:n], refs[n:2 * n]
        send_sems, recv_sems = refs[2 * n:]
        x, y, c = _place()
        copies = []
        for i in range(n):
            hr = gs[i].shape[1] // 2
            cp = pltpu.make_async_remote_copy(src_ref=srcs[i].at[:, pl.ds((1 - c) * hr, hr), :], dst_ref=outs[i],
                                              send_sem=send_sems.at[i], recv_sem=recv_sems.at[i],
                                              device_id=(x, y, 1 - c), device_id_type=MESH)
            cp.start()
            copies.append(cp)
        for cp in copies:
            cp.wait()

    return pl.pallas_call(
        body, name="swap_core_halves",
        out_shape=[jax.ShapeDtypeStruct((4, g.shape[1] // 2, g.shape[2]), g.dtype) for g in gs],
        in_specs=[ANY] * n, out_specs=[ANY] * n, scratch_shapes=_sems(n),
    )(*gs)


def add_half(g, r1, cidx, name):
    _, rows, cols = g.shape
    hr = rows // 2

    def body(c_ref, g_ref, r_ref, o_ref, ob_ref):
        s = g_ref[...] + r_ref[...]
        o_ref[...] = s
        ob_ref[...] = s.astype(BF16)

    blk = lambda f: pl.BlockSpec((1, hr, cols), f)
    return pl.pallas_call(
        body, name=name,
        out_shape=[jax.ShapeDtypeStruct((4, hr, cols), F32), jax.ShapeDtypeStruct((4, hr, cols), BF16)],
        grid_spec=pltpu.PrefetchScalarGridSpec(
            num_scalar_prefetch=1, grid=(4,),
            in_specs=[blk(lambda k, c_ref: (k, c_ref[0], 0)), blk(lambda k, c_ref: (k, 0, 0))],
            out_specs=[blk(lambda k, c_ref: (k, 0, 0)), blk(lambda k, c_ref: (k, 0, 0))]),
    )(cidx, g, r1)


def sum_parts(s1, r2, kidx, name):
    _, hr, cols = s1.shape

    def body(k_ref, s_ref, r_ref, o_ref):
        o_ref[...] = ((s_ref[0] + r_ref[0].astype(F32)) + r_ref[1].astype(F32)) + r_ref[2].astype(F32)

    return pl.pallas_call(
        body, name=name, out_shape=jax.ShapeDtypeStruct((hr, cols), F32),
        grid_spec=pltpu.PrefetchScalarGridSpec(
            num_scalar_prefetch=1, grid=(1,),
            in_specs=[pl.BlockSpec((1, hr, cols), lambda i, k_ref: (k_ref[0], 0, 0)),
                      pl.BlockSpec((3, hr, cols), lambda i, k_ref: (0, 0, 0))],
            out_specs=pl.BlockSpec((hr, cols), lambda i, k_ref: (0, 0))),
    )(kidx, s1, r2)


def swap_reduced_halves(hs):
    n = len(hs)

    def body(*refs):
        srcs, outs = refs[:n], refs[n:2 * n]
        send_sems, recv_sems = refs[2 * n:]
        x, y, c = _place()
        copies = []
        for i in range(n):
            cp = pltpu.make_async_remote_copy(src_ref=srcs[i], dst_ref=outs[i], send_sem=send_sems.at[i],
                                              recv_sem=recv_sems.at[i], device_id=(x, y, 1 - c), device_id_type=MESH)
            cp.start()
            copies.append(cp)
        for cp in copies:
            cp.wait()

    return pl.pallas_call(
        body, name="swap_reduced_halves", out_shape=[jax.ShapeDtypeStruct(h.shape, h.dtype) for h in hs],
        in_specs=[ANY] * n, out_specs=[ANY] * n, scratch_shapes=_sems(n),
    )(*hs)


def adamw_halves(w, m, v, own, oth, cidx, name):
    depth, rows, cols = w.shape
    hr = rows // 2
    tr = min(hr, 256)
    nblk = hr // tr
    bc1 = 1.0 - ADAM_B1 ** ADAM_STEP
    bc2 = 1.0 - ADAM_B2 ** ADAM_STEP

    def body(c_ref, w_ref, m_ref, v_ref, own0, own1, oth0, oth1, g_ref, d_ref, nm_ref, nv_ref):
        l = pl.program_id(0)
        hi = pl.program_id(1)
        mine = jnp.where(l == 0, own0[...], own1[...])
        other = jnp.where(l == 0, oth0[...], oth1[...])
        g_v = jnp.where(hi == c_ref[0], mine, other)
        nm = ADAM_B1 * m_ref[0] + (1.0 - ADAM_B1) * g_v
        nv = ADAM_B2 * v_ref[0] + (1.0 - ADAM_B2) * (g_v * g_v)
        g_ref[0] = g_v
        nm_ref[0] = nm
        nv_ref[0] = nv
        d_ref[0] = -ADAM_LR * ((nm / bc1) / (jnp.sqrt(nv / bc2) + ADAM_EPS) + ADAM_WD * w_ref[0])

    wspec = pl.BlockSpec((1, tr, cols), lambda l, hi, b, c_ref: (l, hi * nblk + b, 0))
    gspec = pl.BlockSpec((tr, cols), lambda l, hi, b, c_ref: (b, 0))
    assert depth == 2
    return pl.pallas_call(
        body, name=name, out_shape=[jax.ShapeDtypeStruct(w.shape, F32)] * 4,
        grid_spec=pltpu.PrefetchScalarGridSpec(
            num_scalar_prefetch=1, grid=(depth, 2, nblk),
            in_specs=[wspec] * 3 + [gspec] * 4, out_specs=[wspec] * 4),
    )(cidx, w, m, v, own[0], own[1], oth[0], oth[1])


class _NS:
    def __init__(self, **kw):
        self.__dict__.update(kw)


def _prep_in(win, conv_w, conv_b, dt_bias, a_log, ssd_d, ssd_nw, qnw, kvnw, pool_w, pool_scale, n1, n2):
    winp = jnp.concatenate([win[:, 0:384], win[:, 384:1280], win[:, 1292:1548], win[:, 1548:1804], win[:, 1836:2092],
                            win[:, 1804:1836], win[:, 1280:1292], jnp.zeros((D, NP - IN_COLS), win.dtype)], axis=1)
    wbd = (jnp.eye(4, dtype=F32)[:, None, :, None] * pool_w[:, :, None, :]).reshape(POOL_DIM, POOL_DIM).astype(MXU)
    a = -jnp.exp(a_log)
    return _NS(
        winp=winp, wint=winp.T, wbd=wbd, wbdt=wbd.T,
        cw8=jnp.pad(conv_w, ((0, 4), (0, 0))), cb=conv_b[None],
        dtb=jnp.pad(dt_bias.reshape(1, 12), ((0, 0), (DT0, 128 - DT0 - 12))),
        arow=jnp.pad(a[:, None, :], ((0, 0), (0, 7), (0, 128 - SSD_HEADS))), a=a,
        dexp=jnp.repeat(ssd_d, SSD_P)[None], ssd_nw=ssd_nw[None], qnw=qnw[None], kvnw=kvnw[None],
        pscale=pool_scale[None], n1=n1[None], n2=n2[None])


def _prep_rest(wqb, wkvb, wout, w1, w2):
    wq = jnp.pad(wqb.reshape(256, MLA_HEADS, QK_DIM), ((0, 0), (0, 0), (0, HP - QK_DIM))).reshape(256, QW)
    kv3 = wkvb.reshape(256, MLA_HEADS, 128)
    wk = jnp.pad(kv3[:, :, :64], ((0, 0), (0, 0), (0, 64))).reshape(256, QW)
    wv = jnp.pad(kv3[:, :, 64:], ((0, 0), (0, 0), (0, 64))).reshape(256, QW)
    wo = jnp.concatenate([jnp.pad(wout[384:768].reshape(MLA_HEADS, 64, D), ((0, 0), (0, 64), (0, 0))).reshape(QW, D),
                          wout[0:384], wout[768:1024]], axis=0)
    return _NS(wq=wq, wqt=wq.T, wk=wk, wkt=wk.T, wv=wv, wvt=wv.T, wo=wo, wot=wo.T, w1=w1, w1t=w1.T, w2=w2, w2t=w2.T)


def _prep_layer(win, wqb, wkvb, wout, w1, w2, *small):
    lw = _prep_in(win, *small)
    lw.__dict__.update(_prep_rest(wqb, wkvb, wout, w1, w2).__dict__)
    return lw


def _by_chip_cols(a):
    return jnp.stack([a[:, k * (a.shape[1] // 4):(k + 1) * (a.shape[1] // 4)] for k in range(4)])


def _by_chip_rows(a):
    return a.reshape(4, a.shape[0] // 4, a.shape[1])


def _unprep_in(dwinp):
    return jnp.concatenate([dwinp[:, 0:384], dwinp[:, 384:1280], dwinp[:, 2080:2092], dwinp[:, 1280:1536],
                            dwinp[:, 1536:1792], dwinp[:, 2048:2080], dwinp[:, 1792:2048]], axis=1)


def _unprep_rest(dwq, dwk, dwv, dwo):
    dwqb = dwq.reshape(256, MLA_HEADS, HP)[:, :, :QK_DIM].reshape(256, MLA_HEADS * QK_DIM)
    dwkvb = jnp.concatenate([dwk.reshape(256, MLA_HEADS, HP)[:, :, :64], dwv.reshape(256, MLA_HEADS, HP)[:, :, :64]],
                            axis=2).reshape(256, MLA_HEADS * 128)
    dwout = jnp.concatenate([dwo[QW:QW + 384], dwo[0:QW].reshape(MLA_HEADS, HP, D)[:, :64].reshape(384, D),
                             dwo[QW + 384:CAT]], axis=0)
    return dwqb, dwkvb, dwout


def _rope_tables(nb, N):
    t = jnp.arange(N, dtype=F32)
    row = jnp.floor(t / GRID_W)
    col = t - row * GRID_W
    inv = jnp.asarray(10000.0 ** (-np.arange(8, dtype=np.float32) / 8), F32)
    ang = jnp.stack([row[:, None] * inv, col[:, None] * inv], axis=1)
    cs, sn = jnp.cos(ang), jnp.sin(ang)
    zero = jnp.zeros_like(sn)
    lanes = lambda first, second: jnp.stack([first, second], axis=2).reshape(N, 32)
    pad = lambda a, fill: jnp.concatenate([jnp.full((N, 64), fill, F32), a, jnp.full((N, 32), fill, F32)], axis=1)
    tabs = []
    for tab, fill in ((pad(lanes(cs, cs), 1.0), 1.0), (pad(lanes(-sn, zero), 0.0), 0.0), (pad(lanes(zero, sn), 0.0), 0.0)):
        one = jnp.concatenate([jnp.full((CTX, 128), fill, F32), tab], axis=0)
        tabs.append(jnp.tile(one, (nb, 1)))
    return tabs


def _eexp():
    e = np.zeros((128, SSD_INNER), np.float32)
    for h in range(SSD_HEADS):
        e[h, h * SSD_P:(h + 1) * SSD_P] = 1.0
    return jnp.asarray(e)


class _NoHooks:
    def __init__(self, lws):
        self.lws = lws

    def weights_in(self, l):
        return _NS(**self.lws[l].__dict__)

    def weights_rest(self, l, scan_out):
        return self.lws[l]

    def job(self, where, l, early=None):
        return None

    def done(self, where, l, out):
        pass

    def layer_grads(self, l, g):
        pass


def _layer_fwd(X, bm, l, cst, hooks):
    nb, T, bps, N = cst.nb, cst.T, cst.bps, cst.N
    lw = hooks.weights_in(l)
    h1, pz, pxbc, pqa, pkva, ppool, plast = in_proj(X, bm, lw.n1, lw.winp)
    xs, bmat, cmat, dtv = ssd_prep(pxbc, plast, lw.cw8, lw.cb, lw.dtb, bps)
    y2, hin, out = ssd_scan_fwd(xs, bmat, cmat, dtv, lw.arow, cst.eexp, nb, T, hooks.job("fwd_scan", l))
    lw.__dict__.update(hooks.weights_rest(l, out).__dict__)
    ssd = ssd_out_fwd(y2, xs, pz, lw.dexp, lw.ssd_nw)
    q, k, v, cq, ckv = mla_prep(pqa, pkva, plast, lw.qnw, lw.kvnw, lw.wq, lw.wk, lw.wv, *cst.rope)
    attn, lse = attn_fwd(q, k, v, nb, T)
    pool = pool_fwd(ppool, lw.wbd, lw.pscale, bps, N)
    x1, mix, cat = mix_fwd(X, attn, ssd, pool, bm, lw.wo)
    x2, mo, r, h2, out = mlp_fwd(x1, bm, lw.n2, lw.w1, lw.w2, hooks.job("fwd_mlp", l))
    hooks.done("fwd_mlp", l, out)
    sv = _NS(X=X, h1=h1, pz=pz, pxbc=pxbc, pqa=pqa, pkva=pkva, ppool=ppool, plast=plast, xs=xs, bmat=bmat, cmat=cmat,
             dtv=dtv, y2=y2, hin=hin, q=q, k=k, v=v, cq=cq, ckv=ckv, attn=attn, lse=lse, x1=x1, mix=mix, cat=cat, mo=mo, r=r,
             h2=h2, lw=lw)
    return x2, sv


def _layer_bwd(dx2, bm, l, sv, cst, hooks):
    nb, T, bps, N = cst.nb, cst.T, cst.bps, cst.N
    lw = sv.lw
    dx1, du, dob, part_mlp, out = mlp_bwd(dx2, sv.x1, sv.mo, sv.r, bm, lw.n2, lw.w2t, lw.w1t, hooks.job("bwd_mlp", l))
    hooks.done("bwd_mlp", l, out)
    dw1 = mm_tn(sv.h2, du, name="wgrad_mlp1", col_blocks=True)
    dw2 = mm_tn(sv.r, dob, square_a=True, name="wgrad_mlp2")
    dattn, dssd, dpool, dmb, part_mix = mix_bwd(dx1, sv.mix, bm, lw.wot)
    dwo = mm_tn(sv.cat, dmb, name="wgrad_out")
    dppool, dwbd, part_pool = pool_bwd(sv.ppool, dpool, lw.wbd, lw.wbdt, lw.pscale, bps, N)
    dq, dk, dv = attn_bwd(sv.q, sv.k, sv.v, sv.attn, sv.lse, dattn, nb, T)
    dpqa, dpkva, dkr, dql, dkm, dvb, part_mla = mla_prep_bwd(dq, dk, dv, sv.pqa, sv.pkva, lw.qnw, lw.kvnw, lw.wqt,
                                                             lw.wkt, lw.wvt, *cst.rope)
    dwq = mm_tn(sv.cq, dql, name="wgrad_q")
    dwk = mm_tn(sv.ckv, dkm, name="wgrad_k")
    dwv = mm_tn(sv.ckv, dvb, name="wgrad_v")
    dwqb, dwkvb, dwout = _unprep_rest(dwq, dwk, dwv, dwo)
    early = dict(w_q_b=_by_chip_cols(dwqb), w_kv_b=_by_chip_cols(dwkvb), w_out=_by_chip_rows(dwout), w_mlp1=dw1,
                 w_mlp2=_by_chip_rows(dw2))
    dyy, dz, dxs_skip, part_so = ssd_out_bwd(dssd, sv.y2, sv.xs, sv.pz, lw.dexp, lw.ssd_nw)
    dxs2, dbm2, dcm2, ddt2, da, out = ssd_scan_bwd(sv.xs, sv.bmat, sv.cmat, sv.dtv, lw.arow, cst.eexp, sv.hin, dyy,
                                                   nb, T, hooks.job("bwd_scan", l, early))
    hooks.done("bwd_scan", l, out)
    dpre, dlast_dt, part_conv = ssd_prep_bwd_a(sv.pxbc, sv.plast, lw.cw8, lw.cb, lw.dtb, dxs_skip, dxs2, dbm2, dcm2,
                                               ddt2, bps)
    dpxbc = ssd_prep_bwd_b(dpre, lw.cw8, bps)
    dx, dpb, part_in = in_proj_bwd(dx1, sv.X, dz, dpxbc, dpqa, dpkva, dppool, dkr, dlast_dt, bm, lw.n1, lw.wint)
    dwinp = mm_tn(sv.h1, dpb, name="wgrad_in")

    dmod = jnp.stack([part_in[:, 0], part_in[:, 1], part_mix[:, 0], part_mlp[:, 0], part_mlp[:, 1], part_mlp[:, 2]],
                     axis=1)
    dmod = dmod.reshape(nb, bps, 6, D)
    dm_rows = jnp.concatenate([jnp.sum(dmod[:, 1:], axis=1), jnp.sum(dmod[:, 0], axis=0)[None]], axis=0)
    da_dh = jnp.sum(da.reshape(nb, 2, 8, 128)[:, :, 0, :SSD_HEADS], axis=0)
    conv_parts = jnp.sum(part_conv, axis=0)
    g = _NS(
        w_in=_by_chip_cols(_unprep_in(dwinp)), dm_rows=dm_rows.reshape(3, 6 * D), **early,
        norm1_w=jnp.sum(part_in[:, 2], axis=0), norm2_w=jnp.sum(part_mlp[:, 3], axis=0),
        conv_w=conv_parts[0:4], conv_b=conv_parts[4],
        dt_bias=conv_parts[5, DT0:DT0 + 12].reshape(2, SSD_HEADS), a_log=da_dh * lw.a,
        ssd_d=jnp.sum(jnp.sum(part_so[:, 1], axis=0).reshape(SSD_HEADS, SSD_P), axis=1),
        ssd_norm_w=jnp.sum(part_so[:, 0], axis=0),
        q_a_norm_w=jnp.sum(part_mla[:, 0], axis=0), kv_a_norm_w=jnp.sum(part_mla[:, 1], axis=0),
        pool_w=jnp.stack([dwbd[i * 64:(i + 1) * 64, i * 64:(i + 1) * 64] for i in range(4)]),
        pool_scale=jnp.sum(part_pool[:, 0], axis=0))
    hooks.layer_grads(l, g)
    return dx, g


def _local_step(x, ctx, tgt, bms, lws, fw, cst, hooks=None):
    nb, N = x.shape[0], x.shape[1]
    R = nb * cst.T
    hooks = _NoHooks(lws) if hooks is None else hooks
    X = jnp.concatenate([ctx, x], axis=1).reshape(R, D)
    saved = []
    for l in range(DEPTH):
        X, sv = _layer_fwd(X, bms[l], l, cst, hooks)
        saved.append(sv)
    dX, part_fin = final_loss(X, tgt.reshape(nb * N, D), fw[None], cst.bps)
    loss = (0.5 / D) * jnp.sum(part_fin[:, 1])
    dfw = jnp.sum(part_fin[:, 0], axis=0)
    grads = [None] * DEPTH
    for l in reversed(range(DEPTH)):
        dX, grads[l] = _layer_bwd(dX, bms[l], l, saved[l], cst, hooks)
    grad_x = dX.reshape(nb, cst.T, D)[:, CTX:, :]
    return loss, grad_x, grads, dfw


def _consts(nb, N):
    T = CTX + N
    bps = T // SB
    return _NS(nb=nb, N=N, T=T, bps=bps, eexp=_eexp(), rope=_rope_tables(nb, N))


def _block_mod(modrows, cst):
    rows = []
    for b in range(cst.nb):
        rows.append(modrows[cst.nb:cst.nb + 1])
        rows.append(jnp.broadcast_to(modrows[b:b + 1], (cst.bps - 1, 6, D)))
    return jnp.pad(jnp.concatenate(rows, axis=0), ((0, 0), (0, 2), (0, 0)))


SMALL = (("norm1_w", (2, D)), ("norm2_w", (2, D)), ("conv_w", (2, 4, XBC)), ("conv_b", (2, XBC)),
         ("dt_bias", (2, 2, 6)), ("a_log", (2, 2, 6)), ("ssd_d", (2, 6)), ("ssd_norm_w", (2, 384)),
         ("q_a_norm_w", (2, 256)), ("kv_a_norm_w", (2, 256)), ("pool_w", (2, 4, 64, 64)), ("pool_scale", (2, 256)),
         ("final_norm_w", (D,)), ("mod_b", (2, 6 * D)))
SMALL_ROWS = 64
DM_ROWS = 48


def _pack_small(vals):
    flat = jnp.concatenate([vals[n].reshape(-1) for n, _ in SMALL])
    return jnp.pad(flat, (0, SMALL_ROWS * D - flat.shape[0])).reshape(SMALL_ROWS, D)


def _unpack_small(p):
    flat = p.reshape(-1)
    out, off = {}, 0
    for n, shp in SMALL:
        size = int(np.prod(shp))
        out[n] = flat[off:off + size].reshape(shp)
        off += size
    return out


def cctx_grad(parts, c_ctx):
    def body(p_ref, c_ref, o_ref):
        acc = ((p_ref[0] + p_ref[1]) + p_ref[2]) + p_ref[3]
        v = c_ref[...]
        sig = _sigmoid(v)
        o_ref[...] = acc * (sig * (1.0 + v * (1.0 - sig)))

    return pl.pallas_call(
        body, name="cctx_grad", out_shape=jax.ShapeDtypeStruct((8, D), F32),
        in_specs=[_fullspec((4, 8, D)), _fullspec((1, D))], out_specs=_fullspec((8, D)), grid=(1,),
    )(parts, c_ctx)


def kernel(x, c, ctx, c_ctx, mod_w, mod_b, norm1_w, norm2_w, w_in, conv_w, conv_b, dt_bias, a_log, ssd_d, ssd_norm_w, q_a_norm_w, w_q_b, kv_a_norm_w, w_kv_b, pool_w, pool_scale, w_out, w_mlp1, w_mlp2, final_norm_w, loss_target, m_c_ctx, m_mod_w, m_mod_b, m_norm1_w, m_norm2_w, m_w_in, m_conv_w, m_conv_b, m_dt_bias, m_a_log, m_ssd_d, m_ssd_norm_w, m_q_a_norm_w, m_w_q_b, m_kv_a_norm_w, m_w_kv_b, m_pool_w, m_pool_scale, m_w_out, m_w_mlp1, m_w_mlp2, m_final_norm_w, v_c_ctx, v_mod_w, v_mod_b, v_norm1_w, v_norm2_w, v_w_in, v_conv_w, v_conv_b, v_dt_bias, v_a_log, v_ssd_d, v_ssd_norm_w, v_q_a_norm_w, v_w_q_b, v_kv_a_norm_w, v_w_kv_b, v_pool_w, v_pool_scale, v_w_out, v_w_mlp1, v_w_mlp2, v_final_norm_w):
    nb, N = x.shape[0], x.shape[1]
    cst = _consts(nb, N)
    xi, yi, ci = _place()
    me = 4 * xi + 2 * yi + ci
    kchip = 2 * xi + yi
    mcols = mod_w.shape[2]
    cshard = conv_w.shape[2]

    blk = jnp.zeros((16, D), F32).at[0:nb].set(c).at[8:16, 0:cshard].set(conv_w.reshape(8, cshard))
    g1 = allgather_small(blk, "gather_cond")
    cond = jnp.concatenate([g1[:, 0:nb].reshape(NDEV * nb, D), c_ctx[None],
                            jnp.zeros((MODR - NDEV * nb - 1, D), F32)], axis=0)
    conv_full = [jnp.concatenate([g1[2 * k, 8 + 4 * l:12 + 4 * l, 0:cshard] for k in range(4)], axis=1)
                 for l in range(DEPTH)]

    mb = [lax.dynamic_slice_in_dim(mod_b[l], kchip * mcols, mcols)[None] for l in range(DEPTH)]
    ms = jnp.concatenate([mod_fwd(cond, mod_w[l], mb[l]) for l in range(DEPTH)], axis=0)
    g2 = allgather_small(ms, "gather_mod")
    bms = []
    for l in range(DEPTH):
        m_all = jnp.concatenate([g2[2 * k, MODR * l:MODR * (l + 1)] for k in range(4)], axis=1)
        mine = jnp.concatenate([lax.dynamic_slice_in_dim(m_all, nb * me, nb), m_all[NDEV * nb:NDEV * nb + 1]], axis=0)
        bms.append(_block_mod(mine.reshape(nb + 1, 6, D), cst))

    assert DEPTH == 2
    big = (w_in, w_q_b, w_kv_b, w_out, w_mlp1, w_mlp2)
    names = ("w_in", "w_q_b", "w_kv_b", "w_out", "w_mlp1", "w_mlp2")
    concat_axis = dict(w_in=1, w_q_b=1, w_kv_b=1, w_out=0, w_mlp1=1, w_mlp2=0)
    cidx = jnp.reshape(ci, (1,)).astype(jnp.int32)
    kidx = jnp.reshape(kchip, (1,)).astype(jnp.int32)
    shards = [{n: a[l].astype(MXU) for n, a in zip(names, big)} for l in range(DEPTH)]

    def core_sums(gs):
        ns = list(gs)
        got = swap_core_halves([gs[n] for n in ns])
        return {n: add_half(gs[n], r, cidx, "add_half_" + n) for n, r in zip(ns, got)}

    class Hooks:
        gathered = [dict(w_in=run_job(gather_job([shards[0]["w_in"]]), "gather_w_in")[0]), {}]
        core_sum = [{}, {}]
        received = [{}, {}]

        def whole(self, l, n):
            return jnp.concatenate([jnp.where(kchip == k, shards[l][n], self.gathered[l][n][k]) for k in range(4)],
                                   axis=concat_axis[n])

        def weights_in(self, l):
            return _prep_in(self.whole(l, "w_in"), conv_full[l], conv_b[l], dt_bias[l], a_log[l], ssd_d[l], ssd_norm_w[l],
                            q_a_norm_w[l], kv_a_norm_w[l], pool_w[l], pool_scale[l], norm1_w[l], norm2_w[l])

        def weights_rest(self, l, scan_out):
            if l == 0:
                self.gathered[0].update(zip(names[1:], scan_out))
            return _prep_rest(*[self.whole(l, n) for n in names[1:]])

        def job(self, where, l, early=None):
            if l != 0:
                return None
            if where == "fwd_scan":
                return gather_job([shards[0][n] for n in names[1:]])
            if where == "fwd_mlp":
                return gather_job([shards[1][n] for n in names])
            if where == "bwd_mlp":
                return chip_swap_job([self.core_sum[1][n][1] for n in names])
            self.core_sum[0].update(core_sums(early))
            return chip_swap_job([self.core_sum[0][n][1] for n in names[1:]])

        def done(self, where, l, out):
            if l != 0:
                return
            if where == "fwd_mlp":
                self.gathered[1].update(zip(names, out))
            elif where == "bwd_mlp":
                self.received[1].update(zip(names, out))
            elif where == "bwd_scan":
                self.received[0].update(zip(names[1:], out))

        def layer_grads(self, l, g):
            if l == 1:
                self.core_sum[1] = core_sums({n: getattr(g, n) for n in names})
            else:
                self.core_sum[0].update(core_sums(dict(w_in=g.w_in)))
                self.received[0]["w_in"] = run_job(chip_swap_job([self.core_sum[0]["w_in"][1]]), "swap_w_in")[0]

    hooks = Hooks()
    loss_part, grad_x, grads, dfw = _local_step(x, ctx, loss_target, bms, None, final_norm_w, cst, hooks)
    loss = lax.psum(loss_part, ("x", "y", "c"))
    g_own = [sum_parts(hooks.core_sum[l][n][0], hooks.received[l][n], kidx, "sum_parts_" + n)
             for n in names for l in range(DEPTH)]
    g_oth = swap_reduced_halves(g_own)

    small = {n: jnp.stack([getattr(grads[l], n) for l in range(DEPTH)]) for n, _ in SMALL if n not in ("final_norm_w", "mod_b")}
    small["final_norm_w"] = dfw
    small["mod_b"] = jnp.stack([jnp.sum(grads[l].dm_rows, axis=0) for l in range(DEPTH)])
    dm = jnp.pad(jnp.concatenate([grads[l].dm_rows for l in range(DEPTH)], axis=0), ((0, 8 - 3 * DEPTH), (0, 0)))
    g3 = allgather_small(jnp.concatenate([_pack_small(small), dm.reshape(DM_ROWS, D)], axis=0), "gather_small")
    tot = sum_leading(g3, "sum_small")
    gsmall = _unpack_small(tot[0:SMALL_ROWS])
    ctx_sum = tot[SMALL_ROWS:].reshape(8, 6 * D)
    dm_dev = g3[:, SMALL_ROWS:].reshape(NDEV, 8, 6 * D)
    g_mod_w, dpart = [], jnp.zeros((8, D), F32)
    for l in range(DEPTH):
        dm_all = jnp.concatenate([dm_dev[:, 3 * l:3 * l + nb].reshape(NDEV * nb, 6 * D), ctx_sum[3 * l + nb:3 * l + nb + 1],
                                  jnp.zeros((MODR - NDEV * nb - 1, 6 * D), F32)], axis=0)
        g_mod_w.append(mod_wgrad(cond, lax.dynamic_slice_in_dim(dm_all, kchip * mcols, mcols, axis=1)))
        dctx = jnp.pad(lax.dynamic_slice_in_dim(ctx_sum[3 * l + nb:3 * l + nb + 1], kchip * mcols, mcols, axis=1), ((0, 7), (0, 0)))
        dpart = dpart + mod_dgrad(dctx, mod_w[l])
    g4 = allgather_small(dpart, "gather_cctx")
    g_c_ctx = cctx_grad(g4[0::2], c_ctx[None])[0]

    res = {}
    moments = ((m_w_in, v_w_in), (m_w_q_b, v_w_q_b), (m_w_kv_b, v_w_kv_b), (m_w_out, v_w_out), (m_w_mlp1, v_w_mlp1),
               (m_w_mlp2, v_w_mlp2))
    for i, (n, w, (m, v)) in enumerate(zip(names, big, moments)):
        res[n] = tuple(adamw_halves(w, m, v, g_own[DEPTH * i:DEPTH * (i + 1)], g_oth[DEPTH * i:DEPTH * (i + 1)], cidx,
                                    "adamw_" + n))
    g_mw = jnp.stack(g_mod_w)
    r_mw = adamw(mod_w.reshape(-1, mcols), g_mw.reshape(-1, mcols), m_mod_w.reshape(-1, mcols),
                 v_mod_w.reshape(-1, mcols), name="adamw_mod_w")
    res["mod_w"] = (g_mw,) + tuple(a.reshape(mod_w.shape) for a in r_mw)

    given = dict(norm1_w=(norm1_w, m_norm1_w, v_norm1_w), norm2_w=(norm2_w, m_norm2_w, v_norm2_w),
                 conv_b=(conv_b, m_conv_b, v_conv_b), dt_bias=(dt_bias, m_dt_bias, v_dt_bias),
                 a_log=(a_log, m_a_log, v_a_log), ssd_d=(ssd_d, m_ssd_d, v_ssd_d),
                 ssd_norm_w=(ssd_norm_w, m_ssd_norm_w, v_ssd_norm_w), q_a_norm_w=(q_a_norm_w, m_q_a_norm_w, v_q_a_norm_w),
                 kv_a_norm_w=(kv_a_norm_w, m_kv_a_norm_w, v_kv_a_norm_w), pool_w=(pool_w, m_pool_w, v_pool_w),
                 pool_scale=(pool_scale, m_pool_scale, v_pool_scale),
                 final_norm_w=(final_norm_w, m_final_norm_w, v_final_norm_w), mod_b=(mod_b, m_mod_b, v_mod_b))
    zero_cw = jnp.zeros((2, 4, XBC), F32)
    packs = [_pack_small({n: (given[n][i] if n in given else zero_cw) for n, _ in SMALL}) for i in range(3)]
    r_small = [_unpack_small(a) for a in adamw(packs[0], tot[0:SMALL_ROWS], packs[1], packs[2], name="adamw_small")]
    for n in given:
        res[n] = (gsmall[n], r_small[0][n], r_small[1][n], r_small[2][n])

    g_cw = lax.dynamic_slice_in_dim(gsmall["conv_w"], kchip * cshard, cshard, axis=2)
    padcw = lambda a: jnp.pad(a.reshape(8, cshard), ((0, 0), (0, 256 - cshard)))
    r_cw = adamw(padcw(conv_w), padcw(g_cw), padcw(m_conv_w), padcw(v_conv_w), name="adamw_conv_w")
    res["conv_w"] = (g_cw,) + tuple(a[:, 0:cshard].reshape(conv_w.shape) for a in r_cw)
    r_cc = adamw(c_ctx.reshape(8, 128), g_c_ctx.reshape(8, 128), m_c_ctx.reshape(8, 128), v_c_ctx.reshape(8, 128),
                 name="adamw_c_ctx")
    res["c_ctx"] = (g_c_ctx,) + tuple(a.reshape(D) for a in r_cc)

    order = ("c_ctx", "mod_w", "mod_b", "norm1_w", "norm2_w", "w_in", "conv_w", "conv_b", "dt_bias", "a_log", "ssd_d",
             "ssd_norm_w", "q_a_norm_w", "w_q_b", "kv_a_norm_w", "w_kv_b", "pool_w", "pool_scale", "w_out", "w_mlp1",
             "w_mlp2", "final_norm_w")
    return (loss, grad_x) + tuple(res[n][i] for i in range(4) for n in order)
```

```python
import functools
import math

import numpy as np
import jax
import jax.numpy as jnp
from jax import lax
from jax.experimental import pallas as pl
from jax.experimental.pallas import tpu as pltpu

F32 = jnp.float32
BF16 = jnp.bfloat16
MXU = jnp.bfloat16

D = 1024
DEPTH = 2
GRID_W = 64
CTX = 256
EPS = 1e-6
SSD_HEADS = 6
SSD_P = 64
SSD_INNER = 384
SSD_N = 128
CHUNK = 128
XBC = 896
MLA_HEADS = 6
QK_NOPE = 64
QK_ROPE = 32
QK_DIM = 96
HP = 128
QW = MLA_HEADS * HP
POOL_DIM = 256
D_FF = 4096
FF_BLK = 1024
IN_COLS = 2092
NP = 2176
P_SPLITS = (384, 896, 256, 256, 256, 128)
DT0 = 32
CAT = QW + SSD_INNER + POOL_DIM

SB = 256
TM = 512
HALO = 8

ADAM_LR = 0.001
ADAM_B1 = 0.9
ADAM_B2 = 0.999
ADAM_EPS = 1e-08
ADAM_WD = 0.01
ADAM_STEP = 10

NT = (((1,), (1,)), ((), ()))
TN = (((0,), (0,)), ((), ()))


def _cp(vmem_mb=None):
    if vmem_mb is None:
        return pltpu.CompilerParams()
    return pltpu.CompilerParams(vmem_limit_bytes=vmem_mb << 20)


def _dot(a, b):
    return jnp.dot(a, b, preferred_element_type=F32)


def _dotg(a, b, dims):
    return lax.dot_general(a, b, dims, preferred_element_type=F32)


def _dot_hi(a, b, dims=None, sel_first=False):
    dims = (((1,), (0,)), ((), ())) if dims is None else dims
    v, s = (b, a) if sel_first else (a, b)
    hi = v.astype(BF16)
    lo = (v - hi.astype(F32)).astype(BF16)
    s = s.astype(BF16)
    if sel_first:
        return _dotg(s, hi, dims) + _dotg(s, lo, dims)
    return _dotg(hi, s, dims) + _dotg(lo, s, dims)


def _rms_hat(x):
    rstd = lax.rsqrt(jnp.mean(x * x, axis=-1, keepdims=True) + EPS)
    return x * rstd, rstd


def _rms_bwd(dn, xhat, rstd, w):
    dxhat = dn * w
    dx = rstd * (dxhat - xhat * jnp.mean(dxhat * xhat, axis=-1, keepdims=True))
    return dx, jnp.sum(dn * xhat, axis=0, keepdims=True)


def _sigmoid(z):
    return 1.0 / (1.0 + jnp.exp(-z))


def _colsum(a):
    return jnp.sum(a, axis=0, keepdims=True)


def _rowspec(cols, tm=TM):
    return pl.BlockSpec((tm, cols), lambda i: (i, 0))


def _fullspec(shape):
    n = len(shape)
    return pl.BlockSpec(shape, lambda *_: (0,) * n)


def _resident(shape):
    n = len(shape)
    return pl.BlockSpec(shape, lambda *_: (0,) * n, pipeline_mode=pl.Buffered(1))


def _halo_specs(cols, nrows):
    per = SB // HALO
    last = nrows // HALO - 1
    prev = pl.BlockSpec((HALO, cols), lambda i: (jnp.maximum(i * per - 1, 0), 0))
    nxt = pl.BlockSpec((HALO, cols), lambda i: (jnp.minimum((i + 1) * per, last), 0))
    return prev, nxt


def _ext_rows(cur, prev, nxt, i, blocks_per_sample):
    j = i % blocks_per_sample
    first = jnp.logical_or(j == 0, j == 1)
    last = jnp.logical_or(j == 0, j == blocks_per_sample - 1)
    p = jnp.where(first, 0.0, prev)
    n = jnp.where(last, 0.0, nxt)
    return jnp.concatenate([p, cur, n], axis=0)


def _shift(ext, s):
    n = ext.shape[0]
    return pltpu.roll(ext, (-s) % n, axis=0)[HALO:HALO + SB, :]


def in_proj(x, bm, nw, w):
    R = x.shape[0]

    def body(x_ref, bm_ref, nw_ref, w_ref, h_ref, *outs):
        for s in range(TM // SB):
            rows = slice(s * SB, (s + 1) * SB)
            xhat, _ = _rms_hat(x_ref[rows, :])
            h = xhat * nw_ref[...] * (1.0 + bm_ref[s, 1:2, :]) + bm_ref[s, 0:1, :]
            h_ref[rows, :] = h.astype(h_ref.dtype)
        p = _dot(h_ref[...], w_ref[...])
        off = 0
        for o, n in zip(outs, P_SPLITS):
            o[...] = p[:, off:off + n]
            off += n

    return pl.pallas_call(
        body, name="in_proj", grid=(R // TM,),
        in_specs=[_rowspec(D), pl.BlockSpec((TM // SB, 8, D), lambda i: (i, 0, 0)), _fullspec((1, D)),
                  _fullspec((D, NP))],
        out_specs=[_rowspec(D)] + [_rowspec(n) for n in P_SPLITS],
        out_shape=[jax.ShapeDtypeStruct((R, D), MXU)] + [jax.ShapeDtypeStruct((R, n), F32) for n in P_SPLITS],
        compiler_params=_cp(56),
    )(x, bm, nw, w)


def in_proj_bwd(dx1, x, dz, dxbc, dqa, dkva, dpool, dkr, ddt, bm, nw, wt):
    R = x.shape[0]

    def body(dx1_ref, x_ref, dz_ref, dxbc_ref, dqa_ref, dkva_ref, dpool_ref, dkr_ref, ddt_ref, bm_ref, nw_ref,
             wt_ref, dx_ref, dp_ref, part_ref):
        dp_ref[:, 0:384] = dz_ref[...].astype(dp_ref.dtype)
        dp_ref[:, 384:1280] = dxbc_ref[...].astype(dp_ref.dtype)
        dp_ref[:, 1280:1536] = dqa_ref[...].astype(dp_ref.dtype)
        dp_ref[:, 1536:1792] = dkva_ref[...].astype(dp_ref.dtype)
        dp_ref[:, 1792:2048] = dpool_ref[...].astype(dp_ref.dtype)
        dp_ref[:, 2048:2176] = (dkr_ref[...] + ddt_ref[...]).astype(dp_ref.dtype)
        dh = _dotg(dp_ref[...], wt_ref[...], NT)
        w = nw_ref[...]
        for s in range(TM // SB):
            rows = slice(s * SB, (s + 1) * SB)
            xhat, rstd = _rms_hat(x_ref[rows, :])
            dhs = dh[rows, :]
            sc1 = 1.0 + bm_ref[s, 1:2, :]
            dx, dnw = _rms_bwd(dhs * sc1, xhat, rstd, w)
            dx_ref[rows, :] = dx1_ref[rows, :] + dx
            part_ref[s] = jnp.concatenate(
                [_colsum(dhs), _colsum(dhs * xhat * w), dnw, jnp.zeros((5, D), F32)], axis=0)

    return pl.pallas_call(
        body, name="in_proj_bwd", grid=(R // TM,),
        in_specs=[_rowspec(D), _rowspec(D), _rowspec(384), _rowspec(896), _rowspec(256), _rowspec(256),
                  _rowspec(256), _rowspec(128), _rowspec(128),
                  pl.BlockSpec((TM // SB, 8, D), lambda i: (i, 0, 0)), _fullspec((1, D)), _fullspec((D, NP))],
        out_specs=[_rowspec(D), _rowspec(NP), pl.BlockSpec((TM // SB, 8, D), lambda i: (i, 0, 0))],
        out_shape=[jax.ShapeDtypeStruct((R, D), F32), jax.ShapeDtypeStruct((R, NP), MXU),
                   jax.ShapeDtypeStruct((R // SB, 8, D), F32)],
        compiler_params=_cp(56),
    )(dx1, x, dz, dxbc, dqa, dkva, dpool, dkr, ddt, bm, nw, wt)


def mix_fwd(x, attn, ssd, pool, bm, wo):
    R = x.shape[0]

    def body(x_ref, a_ref, s_ref, p_ref, bm_ref, wo_ref, x1_ref, mix_ref, cat_ref):
        cat_ref[:, 0:QW] = a_ref[...].astype(cat_ref.dtype)
        cat_ref[:, QW:QW + SSD_INNER] = s_ref[...].astype(cat_ref.dtype)
        cat_ref[:, QW + SSD_INNER:CAT] = p_ref[...].astype(cat_ref.dtype)
        mix = _dot(cat_ref[...], wo_ref[...])
        mix_ref[...] = mix
        for s in range(TM // SB):
            rows = slice(s * SB, (s + 1) * SB)
            x1_ref[rows, :] = x_ref[rows, :] + bm_ref[s, 2:3, :] * mix[rows, :]

    return pl.pallas_call(
        body, name="mix_fwd", grid=(R // TM,),
        in_specs=[_rowspec(D), _rowspec(QW), _rowspec(SSD_INNER), _rowspec(POOL_DIM),
                  pl.BlockSpec((TM // SB, 8, D), lambda i: (i, 0, 0)), _fullspec((CAT, D))],
        out_specs=[_rowspec(D), _rowspec(D), _rowspec(CAT)],
        out_shape=[jax.ShapeDtypeStruct((R, D), F32), jax.ShapeDtypeStruct((R, D), F32),
                   jax.ShapeDtypeStruct((R, CAT), MXU)],
        compiler_params=_cp(48),
    )(x, attn, ssd, pool, bm, wo)


def mix_bwd(dx1, mix, bm, wot):
    R = dx1.shape[0]

    def body(dx1_ref, mix_ref, bm_ref, wot_ref, da_ref, ds_ref, dpl_ref, dmb_ref, part_ref):
        for s in range(TM // SB):
            rows = slice(s * SB, (s + 1) * SB)
            d = dx1_ref[rows, :]
            dmb_ref[rows, :] = (d * bm_ref[s, 2:3, :]).astype(dmb_ref.dtype)
            part_ref[s] = jnp.concatenate([_colsum(d * mix_ref[rows, :]), jnp.zeros((7, D), F32)], axis=0)
        dcat = _dotg(dmb_ref[...], wot_ref[...], NT)
        da_ref[...] = dcat[:, 0:QW]
        ds_ref[...] = dcat[:, QW:QW + SSD_INNER]
        dpl_ref[...] = dcat[:, QW + SSD_INNER:CAT]

    return pl.pallas_call(
        body, name="mix_bwd", grid=(R // TM,),
        in_specs=[_rowspec(D), _rowspec(D), pl.BlockSpec((TM // SB, 8, D), lambda i: (i, 0, 0)),
                  _fullspec((CAT, D))],
        out_specs=[_rowspec(QW), _rowspec(SSD_INNER), _rowspec(POOL_DIM), _rowspec(D),
                   pl.BlockSpec((TM // SB, 8, D), lambda i: (i, 0, 0))],
        out_shape=[jax.ShapeDtypeStruct((R, QW), F32), jax.ShapeDtypeStruct((R, SSD_INNER), F32),
                   jax.ShapeDtypeStruct((R, POOL_DIM), F32), jax.ShapeDtypeStruct((R, D), MXU),
                   jax.ShapeDtypeStruct((R // SB, 8, D), F32)],
        compiler_params=_cp(48),
    )(dx1, mix, bm, wot)


def mlp_fwd(x1, bm, nw, w1, w2, side=None):
    R = x1.shape[0]

    def body(x1_ref, bm_ref, nw_ref, w1_ref, w2_ref, x2_ref, mo_ref, r_ref, h2_ref):
        for s in range(TM // SB):
            rows = slice(s * SB, (s + 1) * SB)
            xhat, _ = _rms_hat(x1_ref[rows, :])
            h = xhat * nw_ref[...] * (1.0 + bm_ref[s, 4:5, :]) + bm_ref[s, 3:4, :]
            h2_ref[rows, :] = h.astype(h2_ref.dtype)
        for j in range(D_FF // FF_BLK):
            cols = slice(j * FF_BLK, (j + 1) * FF_BLK)
            r = jnp.maximum(_dot(h2_ref[...], w1_ref[:, cols]), 0.0)
            r_ref[:, cols] = r.astype(r_ref.dtype)
            d = _dot((r * r).astype(MXU), w2_ref[cols, :])
            if j == 0:
                mo_ref[...] = d
            else:
                mo_ref[...] += d
        for s in range(TM // SB):
            rows = slice(s * SB, (s + 1) * SB)
            x2_ref[rows, :] = x1_ref[rows, :] + bm_ref[s, 5:6, :] * mo_ref[rows, :]

    grid = (R // TM,)
    body, side_in, side_out, side_shapes, side_scratch, side_args = _side_wrap(body, 5, 4, 0, side, grid)
    outs = pl.pallas_call(
        body, name="mlp_fwd" if side is None else "mlp_fwd_comm", grid=grid,
        in_specs=[_rowspec(D), pl.BlockSpec((TM // SB, 8, D), lambda i: (i, 0, 0)), _fullspec((1, D)),
                  _resident((D, D_FF)), _resident((D_FF, D))] + side_in,
        out_specs=[_rowspec(D), _rowspec(D), _rowspec(D_FF), _rowspec(D)] + side_out,
        out_shape=[jax.ShapeDtypeStruct((R, D), F32), jax.ShapeDtypeStruct((R, D), F32),
                   jax.ShapeDtypeStruct((R, D_FF), BF16), jax.ShapeDtypeStruct((R, D), MXU)] + side_shapes,
        scratch_shapes=side_scratch,
        compiler_params=_cp(56),
    )(x1, bm, nw, w1, w2, *side_args)
    return tuple(outs[:4]) + (list(outs[4:]),)


def mlp_bwd(dx2, x1, mo, r, bm, nw, w2t, w1t, side=None):
    R = x1.shape[0]

    def body(dx2_ref, x1_ref, mo_ref, r_ref, bm_ref, nw_ref, w2t_ref, w1t_ref, dx1_ref, du_ref, dob_ref, part_ref,
             acc_ref):
        for s in range(TM // SB):
            rows = slice(s * SB, (s + 1) * SB)
            dob_ref[rows, :] = (dx2_ref[rows, :] * bm_ref[s, 5:6, :]).astype(dob_ref.dtype)
        for j in range(D_FF // FF_BLK):
            cols = slice(j * FF_BLK, (j + 1) * FF_BLK)
            du = _dotg(dob_ref[...], w2t_ref[cols, :], NT) * (2.0 * r_ref[:, cols].astype(F32))
            du_ref[:, cols] = du.astype(du_ref.dtype)
            d = _dotg(du_ref[:, cols], w1t_ref[:, cols], NT)
            if j == 0:
                acc_ref[...] = d
            else:
                acc_ref[...] += d
        w = nw_ref[...]
        for s in range(TM // SB):
            rows = slice(s * SB, (s + 1) * SB)
            xhat, rstd = _rms_hat(x1_ref[rows, :])
            dh = acc_ref[rows, :]
            dx, dnw = _rms_bwd(dh * (1.0 + bm_ref[s, 4:5, :]), xhat, rstd, w)
            d2 = dx2_ref[rows, :]
            dx1_ref[rows, :] = d2 + dx
            part_ref[s] = jnp.concatenate(
                [_colsum(dh), _colsum(dh * xhat * w), _colsum(d2 * mo_ref[rows, :]), dnw,
                 jnp.zeros((4, D), F32)], axis=0)

    grid = (R // TM,)
    body, side_in, side_out, side_shapes, side_scratch, side_args = _side_wrap(body, 8, 4, 1, side, grid)
    outs = pl.pallas_call(
        body, name="mlp_bwd" if side is None else "mlp_bwd_comm", grid=grid,
        in_specs=[_rowspec(D), _rowspec(D), _rowspec(D), _rowspec(D_FF),
                  pl.BlockSpec((TM // SB, 8, D), lambda i: (i, 0, 0)), _fullspec((1, D)),
                  _resident((D_FF, D)), _resident((D, D_FF))] + side_in,
        out_specs=[_rowspec(D), _rowspec(D_FF), _rowspec(D), pl.BlockSpec((TM // SB, 8, D), lambda i: (i, 0, 0))]
                  + side_out,
        out_shape=[jax.ShapeDtypeStruct((R, D), F32), jax.ShapeDtypeStruct((R, D_FF), MXU),
                   jax.ShapeDtypeStruct((R, D), MXU), jax.ShapeDtypeStruct((R // SB, 8, D), F32)] + side_shapes,
        scratch_shapes=[pltpu.VMEM((TM, D), F32)] + side_scratch,
        compiler_params=_cp(56),
    )(dx2, x1, mo, r, bm, nw, w2t, w1t, *side_args)
    return tuple(outs[:4]) + (list(outs[4:]),)


def mm_tn(a, b, square_a=False, name="mm_tn", col_blocks=False):
    R, M = a.shape
    N = b.shape[1]
    tm = M if M <= 1408 else 1024
    tn = N if N <= 2176 else 1024
    tk = next((c for c in ((2176, 1088, 512) if tm + tn <= 2048 else (1088, 512)) if R % c == 0), R)
    assert not col_blocks or tm == M

    def body(a_ref, b_ref, o_ref):
        @pl.when(pl.program_id(2) == 0)
        def _():
            o_ref[...] = jnp.zeros_like(o_ref)

        av = a_ref[...]
        if square_a:
            av = av.astype(F32)
            av = (av * av).astype(MXU)
        prod = _dotg(av.astype(MXU), b_ref[...].astype(MXU), TN)
        if col_blocks:
            o_ref[0] += prod
        else:
            o_ref[...] += prod

    if col_blocks:
        out_spec = pl.BlockSpec((1, tm, tn), lambda i, j, k: (j, 0, 0))
        out_shape = jax.ShapeDtypeStruct((N // tn, M, tn), F32)
    else:
        out_spec = pl.BlockSpec((tm, tn), lambda i, j, k: (i, j))
        out_shape = jax.ShapeDtypeStruct((M, N), F32)
    return pl.pallas_call(
        body, name=name, grid=(M // tm, N // tn, R // tk),
        in_specs=[pl.BlockSpec((tk, tm), lambda i, j, k: (k, i)), pl.BlockSpec((tk, tn), lambda i, j, k: (k, j))],
        out_specs=out_spec, out_shape=out_shape,
        compiler_params=_cp(48),
    )(a, b)


def final_loss(x, tgt, fw, blocks_per_sample):
    R = x.shape[0]
    nxb = blocks_per_sample - 1

    def body(x_ref, t_ref, fw_ref, dx_ref, part_ref):
        i = pl.program_id(0)
        is_ctx = (i % blocks_per_sample) == 0
        xhat, rstd = _rms_hat(x_ref[...])
        w = fw_ref[...]
        err = xhat * w - t_ref[...]
        dx, dfw = _rms_bwd(err * (1.0 / D), xhat, rstd, w)
        keep = jnp.where(is_ctx, 0.0, 1.0)
        dx_ref[...] = dx * keep
        part_ref[0] = jnp.concatenate([dfw * keep, _colsum(err * err) * keep, jnp.zeros((6, D), F32)], axis=0)

    def tmap(i):
        return ((i // blocks_per_sample) * nxb + jnp.maximum(i % blocks_per_sample - 1, 0), 0)

    return pl.pallas_call(
        body, name="final_loss", grid=(R // SB,),
        in_specs=[_rowspec(D, SB), pl.BlockSpec((SB, D), tmap), _fullspec((1, D))],
        out_specs=[_rowspec(D, SB), pl.BlockSpec((1, 8, D), lambda i: (i, 0, 0))],
        out_shape=[jax.ShapeDtypeStruct((R, D), F32), jax.ShapeDtypeStruct((R // SB, 8, D), F32)],
    )(x, tgt, fw)


def _softplus(v):
    return jnp.maximum(v, 0.0) + jnp.log(1.0 + jnp.exp(-jnp.abs(v)))


def _conv_out(ext, cw_ref, cb_ref):
    return (cb_ref[...] + cw_ref[0:1, :] * _shift(ext, -1) + cw_ref[1:2, :] * _shift(ext, 0)
            + cw_ref[2:3, :] * _shift(ext, 1) + cw_ref[3:4, :] * _shift(ext, 2))


def _dt_dir(v, d):
    lane = lax.broadcasted_iota(jnp.int32, v.shape, 1)
    return jnp.where(lane < SSD_HEADS, pltpu.roll(v, (128 - DT0 - SSD_HEADS * d) % 128, axis=1), 0.0)


def ssd_prep(pxbc, plast, cw, cb, dtb, blocks_per_sample):
    R = pxbc.shape[0]
    prev, nxt = _halo_specs(XBC, R)

    def body(cur_ref, prev_ref, nxt_ref, pl_ref, cw_ref, cb_ref, dtb_ref, xs_ref, bm_ref, cm_ref, dt_ref):
        i = pl.program_id(0)
        ext = _ext_rows(cur_ref[...], prev_ref[...], nxt_ref[...], i, blocks_per_sample)
        co = _conv_out(ext, cw_ref, cb_ref)
        a = co * _sigmoid(co)
        xs_ref[...] = a[:, 0:384]
        bm_ref[...] = a[:, 384:640]
        cm_ref[...] = a[:, 640:896]
        sp = _softplus(pl_ref[...] + dtb_ref[...])
        dt_ref[0] = _dt_dir(sp, 0)
        dt_ref[1] = _dt_dir(sp, 1)

    return pl.pallas_call(
        body, name="ssd_prep", grid=(R // SB,),
        in_specs=[_rowspec(XBC, SB), prev, nxt, _rowspec(128, SB), _fullspec((8, XBC)), _fullspec((1, XBC)),
                  _fullspec((1, 128))],
        out_specs=[_rowspec(384, SB), _rowspec(256, SB), _rowspec(256, SB),
                   pl.BlockSpec((2, SB, 128), lambda i: (0, i, 0))],
        out_shape=[jax.ShapeDtypeStruct((R, 384), F32), jax.ShapeDtypeStruct((R, 256), F32),
                   jax.ShapeDtypeStruct((R, 256), F32), jax.ShapeDtypeStruct((2, R, 128), F32)],
    )(pxbc, pxbc, pxbc, plast, cw, cb, dtb)


def _chunk_index(d, s, nc):
    nctx = CTX // CHUNK
    back = jnp.where(s < nctx, nctx - 1 - s, nc + nctx - 1 - s)
    return jnp.where(d == 0, s, back)


def _scan_common(d, dt, arow, eexp, xs):
    ii = lax.broadcasted_iota(jnp.int32, (CHUNK, CHUNK), 0)
    jj = lax.broadcasted_iota(jnp.int32, (CHUNK, CHUNK), 1)
    mask = ((ii - jj) * (1 - 2 * d)) >= 0
    adt = dt * arow
    tmat = jnp.where(mask, 1.0, 0.0)
    cs = _dot_hi(tmat, adt, sel_first=True)
    tot = _colsum(adt)
    dtx = _dot_hi(dt, eexp)
    xt = xs * dtx
    ecs = jnp.exp(cs)
    ecx = _dot_hi(ecs, eexp)
    dte = jnp.exp(tot - cs)
    dtex = _dot_hi(dte, eexp)
    etot = jnp.exp(tot)
    etx = _dot_hi(jnp.broadcast_to(etot, (8, 128)), eexp)[0:1, :]
    return mask, tmat, adt, cs, tot, dtx, xt, ecs, ecx, dte, dtex, etot, etx


def _decay_matrix(mask, cs, cst, h):
    return jnp.exp(jnp.where(mask, cs[:, h:h + 1] - cst[h:h + 1, :], -1e30))


def _side_wrap(body, n_in, n_out, n_scratch, side, grid):
    if side is None:
        return body, [], [], [], [], []
    ni, no = len(side.ins), len(side.out_shapes)

    def wrapped(*refs):
        ins, refs = refs[:n_in], refs[n_in:]
        side_ins, refs = refs[:ni], refs[ni:]
        outs, refs = refs[:n_out], refs[n_out:]
        side_outs, refs = refs[:no], refs[no:]
        scratch, sems = refs[:n_scratch], refs[n_scratch:]
        ids = [pl.program_id(a) for a in range(len(grid))]
        first = functools.reduce(jnp.logical_and, [i == 0 for i in ids])
        last = functools.reduce(jnp.logical_and, [i == g - 1 for i, g in zip(ids, grid)])
        pl.when(first)(lambda: side.start(side_ins, side_outs, sems))
        body(*ins, *outs, *scratch)
        pl.when(last)(lambda: side.finish(side_ins, side_outs, sems))

    return wrapped, [ANY] * ni, [ANY] * no, list(side.out_shapes), _sems(side.nsem), list(side.ins)


def ssd_scan_fwd(xs, bm, cm, dtv, arow, eexp, nb, T, side=None):
    R = xs.shape[0]
    nc = T // CHUNK
    B = range(nb)

    def body(xs_ref, bm_ref, cm_ref, dt_ref, a_ref, e_ref, y_ref, hin_ref, st_ref):
        d = pl.program_id(0)
        s = pl.program_id(1)

        @pl.when(s == 0)
        def _():
            st_ref[...] = jnp.zeros_like(st_ref)

        eexp = e_ref[...]
        com = [_scan_common(d, dt_ref[0, b], a_ref[0, 0:1, :], eexp, xs_ref[b]) for b in B]
        mask = com[0][0]
        cs = [com[b][3] for b in B]
        cst = [cs[b].T for b in B]
        sin = [st_ref[b] for b in B]
        for b in B:
            hin_ref[0, b] = sin[b]
        sb = [sin[b].astype(MXU) for b in B]
        xtb = [com[b][6].astype(MXU) for b in B]
        xw = [(com[b][6] * com[b][10]).astype(MXU) for b in B]
        g0 = lax.broadcasted_iota(jnp.int32, (CHUNK, SSD_INNER), 1) < 192
        lane = lax.broadcasted_iota(jnp.int32, (CHUNK, 128), 1)
        c = [[cm_ref[b, :, 0:128].astype(MXU), cm_ref[b, :, 128:256].astype(MXU)] for b in B]
        bq = [[bm_ref[b, :, 0:128].astype(MXU), bm_ref[b, :, 128:256].astype(MXU)] for b in B]
        y = [jnp.where(g0, _dot(c[b][0], sb[b]), _dot(c[b][1], sb[b])) * com[b][8] for b in B]
        cb = [[_dotg(c[b][g], bq[b][g], NT) for g in range(2)] for b in B]
        blocks = [[] for _ in B]
        for blk in range(3):
            acc = [None for _ in B]
            for hh in range(2):
                h = blk * 2 + hh
                for b in B:
                    m = (cb[b][h // 3] * _decay_matrix(mask, cs[b], cst[b], h)).astype(MXU)
                    res = _dot(m, xtb[b][:, blk * 128:(blk + 1) * 128])
                    acc[b] = res if hh == 0 else jnp.where(lane < 64, acc[b], res)
            for b in B:
                blocks[b].append(acc[b])
        for b in B:
            y_ref[0, b] = y[b] + jnp.concatenate(blocks[b], axis=1)
            st_ref[b] = sin[b] * com[b][12] + jnp.where(g0, _dotg(bq[b][0], xw[b], TN), _dotg(bq[b][1], xw[b], TN))

    def rows(cols):
        return pl.BlockSpec((nb, CHUNK, cols), lambda d, s: (0, _chunk_index(d, s, nc), 0))

    def by_dir(cols):
        return pl.BlockSpec((1, nb, CHUNK, cols), lambda d, s: (d, 0, _chunk_index(d, s, nc), 0))

    grid = (2, nc)
    body, side_in, side_out, side_shapes, side_scratch, side_args = _side_wrap(body, 6, 2, 1, side, grid)
    outs = pl.pallas_call(
        body, name="ssd_scan_fwd" if side is None else "ssd_scan_fwd_comm", grid=grid,
        in_specs=[rows(384), rows(256), rows(256), by_dir(128), pl.BlockSpec((1, 8, 128), lambda d, s: (d, 0, 0)),
                  pl.BlockSpec((128, 384), lambda d, s: (0, 0))] + side_in,
        out_specs=[by_dir(384),
                   pl.BlockSpec((1, nb, CHUNK, 384), lambda d, s: (d * nc + _chunk_index(d, s, nc), 0, 0, 0))] + side_out,
        out_shape=[jax.ShapeDtypeStruct((2, nb, T, 384), F32), jax.ShapeDtypeStruct((2 * nc, nb, CHUNK, 384), F32)]
                  + side_shapes,
        scratch_shapes=[pltpu.VMEM((nb, CHUNK, 384), F32)] + side_scratch,
    )(xs.reshape(nb, T, 384), bm.reshape(nb, T, 256), cm.reshape(nb, T, 256), dtv.reshape(2, nb, T, 128), arow, eexp,
      *side_args)
    return outs[0].reshape(2, R, 384), outs[1], list(outs[2:])


def ssd_scan_bwd(xs, bm, cm, dtv, arow, eexp, hin, dy, nb, T, side=None):
    R = xs.shape[0]
    nc = T // CHUNK
    B = range(nb)

    def chunk(d, s):
        return _chunk_index(d, nc - 1 - s, nc)

    def body(xs_ref, bm_ref, cm_ref, dt_ref, a_ref, e_ref, hin_ref, dy_ref,
             dxs_ref, dbm_ref, dcm_ref, ddt_ref, da_ref, ds_ref):
        d = pl.program_id(0)
        s = pl.program_id(1)

        @pl.when(s == 0)
        def _():
            ds_ref[...] = jnp.zeros_like(ds_ref)
            da_ref[...] = jnp.zeros_like(da_ref)

        eexp = e_ref[...]
        arow = a_ref[0, 0:1, :]
        dt = [dt_ref[0, b] for b in B]
        xs_v = [xs_ref[b] for b in B]
        com = [_scan_common(d, dt[b], arow, eexp, xs_v[b]) for b in B]
        mask, tmat = com[0][0], com[0][1]
        cs, dtx, xt, ecs, ecx, dte, dtex, etot, etx = [[com[b][i] for b in B] for i in (3, 5, 6, 7, 8, 9, 10, 11, 12)]
        cst = [cs[b].T for b in B]
        sin = [hin_ref[0, b] for b in B]
        sb = [sin[b].astype(MXU) for b in B]
        dsp = [ds_ref[b] for b in B]
        dyv = [dy_ref[b] for b in B]
        xtb = [xt[b].astype(MXU) for b in B]
        xw = [(xt[b] * dtex[b]).astype(MXU) for b in B]
        g0 = lax.broadcasted_iota(jnp.int32, (CHUNK, SSD_INNER), 1) < 192
        lane = lax.broadcasted_iota(jnp.int32, (CHUNK, 128), 1)
        sub = lax.broadcasted_iota(jnp.int32, (CHUNK, 128), 0)
        c = [[cm_ref[b, :, 0:128].astype(MXU), cm_ref[b, :, 128:256].astype(MXU)] for b in B]
        bq = [[bm_ref[b, :, 0:128].astype(MXU), bm_ref[b, :, 128:256].astype(MXU)] for b in B]

        cs_prod = [jnp.where(g0, _dot(c[b][0], sb[b]), _dot(c[b][1], sb[b])) for b in B]
        dcsp = [dyv[b] * ecx[b] for b in B]
        dcsp_g = [[jnp.where(g0, dcsp[b], 0.0).astype(MXU), jnp.where(g0, 0.0, dcsp[b]).astype(MXU)] for b in B]
        dcs = [_dot_hi(dyv[b] * cs_prod[b], eexp, NT) * ecs[b] for b in B]
        dc = [[_dotg(dcsp_g[b][g], sb[b], NT) for g in range(2)] for b in B]
        dsin = [_dotg(c[b][0], dcsp_g[b][0], TN) + _dotg(c[b][1], dcsp_g[b][1], TN) + dsp[b] * etx[b] for b in B]

        dtot = [_dot_hi(jnp.broadcast_to(_colsum(dsp[b] * sin[b]), (8, SSD_INNER)), eexp, NT)[0:1, :] * etot[b] for b in B]
        dsp_g = [[jnp.where(g0, dsp[b], 0.0).astype(MXU), jnp.where(g0, 0.0, dsp[b]).astype(MXU)] for b in B]
        dxw = [_dot(bq[b][0], dsp_g[b][0]) + _dot(bq[b][1], dsp_g[b][1]) for b in B]
        db = [[_dotg(xw[b], dsp_g[b][g], NT) for g in range(2)] for b in B]
        dxt = [dxw[b] * dtex[b] for b in B]
        ddte = [_dot_hi(dxw[b] * xt[b], eexp, NT) * dte[b] for b in B]
        dtot = [dtot[b] + _colsum(ddte[b]) for b in B]
        dcs = [dcs[b] - ddte[b] for b in B]

        cb = [[_dotg(c[b][g], bq[b][g], NT) for g in range(2)] for b in B]
        dg = [[jnp.zeros((CHUNK, CHUNK), F32), jnp.zeros((CHUNK, CHUNK), F32)] for _ in B]
        dcs_rows = [jnp.zeros((CHUNK, 128), F32) for _ in B]
        dxt_blocks = [[] for _ in B]
        for blk in range(3):
            acc = [jnp.zeros((CHUNK, 128), F32) for _ in B]
            for hh in range(2):
                h = blk * 2 + hh
                g = h // 3
                mine = (lane < 64) if hh == 0 else (lane >= 64)
                for b in B:
                    dyh = jnp.where(mine, dyv[b][:, blk * 128:(blk + 1) * 128], 0.0).astype(MXU)
                    lh = _decay_matrix(mask, cs[b], cst[b], h)
                    m = cb[b][g] * lh
                    dm = _dotg(dyh, xtb[b][:, blk * 128:(blk + 1) * 128], NT)
                    acc[b] = acc[b] + _dotg(m.astype(MXU), dyh, TN)
                    dg[b][g] = dg[b][g] + dm * lh
                    q = dm * m
                    dcs[b] = dcs[b] + jnp.where(lane == h, jnp.sum(q, axis=1, keepdims=True), 0.0)
                    dcs_rows[b] = dcs_rows[b] - jnp.where(sub == h, jnp.sum(q, axis=0, keepdims=True), 0.0)
            for b in B:
                dxt_blocks[b].append(acc[b])
        for b in B:
            dxt[b] = dxt[b] + jnp.concatenate(dxt_blocks[b], axis=1)
            for g in range(2):
                dgb = dg[b][g].astype(MXU)
                dc[b][g] = dc[b][g] + _dot(dgb, bq[b][g])
                db[b][g] = db[b][g] + _dotg(dgb, c[b][g], TN)
            dcs[b] = dcs[b] + dcs_rows[b].T

        for b in B:
            dadt = _dot_hi(tmat, dcs[b], TN, sel_first=True) + dtot[b]
            ddt_ref[0, b] = dadt * arow + _dot_hi(dxt[b] * xs_v[b], eexp, NT)
            da_ref[0, b, 0:1, :] += _colsum(dadt * dt[b])
            dxs_ref[0, b] = dxt[b] * dtx[b]
            dbm_ref[0, b] = jnp.concatenate(db[b], axis=1)
            dcm_ref[0, b] = jnp.concatenate(dc[b], axis=1)
            ds_ref[b] = dsin[b]

    def rows(cols):
        return pl.BlockSpec((nb, CHUNK, cols), lambda d, s: (0, chunk(d, s), 0))

    def by_dir(cols):
        return pl.BlockSpec((1, nb, CHUNK, cols), lambda d, s: (d, 0, chunk(d, s), 0))

    grid = (2, nc)
    body, side_in, side_out, side_shapes, side_scratch, side_args = _side_wrap(body, 8, 5, 1, side, grid)
    outs = pl.pallas_call(
        body, name="ssd_scan_bwd" if side is None else "ssd_scan_bwd_comm", grid=grid,
        in_specs=[rows(384), rows(256), rows(256), by_dir(128), pl.BlockSpec((1, 8, 128), lambda d, s: (d, 0, 0)),
                  pl.BlockSpec((128, 384), lambda d, s: (0, 0)),
                  pl.BlockSpec((1, nb, CHUNK, 384), lambda d, s: (d * nc + chunk(d, s), 0, 0, 0)), rows(384)] + side_in,
        out_specs=[by_dir(384), by_dir(256), by_dir(256), by_dir(128),
                   pl.BlockSpec((1, nb, 8, 128), lambda d, s: (d, 0, 0, 0))] + side_out,
        out_shape=[jax.ShapeDtypeStruct((2, nb, T, 384), F32), jax.ShapeDtypeStruct((2, nb, T, 256), F32),
                   jax.ShapeDtypeStruct((2, nb, T, 256), F32), jax.ShapeDtypeStruct((2, nb, T, 128), F32),
                   jax.ShapeDtypeStruct((2, nb, 8, 128), F32)] + side_shapes,
        scratch_shapes=[pltpu.VMEM((nb, CHUNK, 384), F32)] + side_scratch,
    )(xs.reshape(nb, T, 384), bm.reshape(nb, T, 256), cm.reshape(nb, T, 256), dtv.reshape(2, nb, T, 128), arow, eexp,
      hin, dy.reshape(nb, T, 384), *side_args)
    return (outs[0].reshape(2, R, 384), outs[1].reshape(2, R, 256), outs[2].reshape(2, R, 256),
            outs[3].reshape(2, R, 128), outs[4], list(outs[5:]))


def _group_rms(g):
    lane = lax.broadcasted_iota(jnp.int32, g.shape, 1)
    g0 = lane < 192
    gg = g * g
    s0 = jnp.sum(jnp.where(g0, gg, 0.0), axis=-1, keepdims=True)
    s1 = jnp.sum(gg, axis=-1, keepdims=True) - s0
    rstd = jnp.where(g0, lax.rsqrt(s0 * (1.0 / 192) + EPS), lax.rsqrt(s1 * (1.0 / 192) + EPS))
    return rstd, g0


def ssd_out_fwd(y2, xs, pz, dexp, nw):
    R = xs.shape[0]

    def body(y_ref, xs_ref, z_ref, d_ref, nw_ref, o_ref):
        z = z_ref[...]
        yy = y_ref[0] + y_ref[1] + xs_ref[...] * d_ref[...]
        g = yy * (z * _sigmoid(z))
        rstd, _ = _group_rms(g)
        o_ref[...] = g * rstd * nw_ref[...]

    return pl.pallas_call(
        body, name="ssd_out_fwd", grid=(R // TM,),
        in_specs=[pl.BlockSpec((2, TM, 384), lambda i: (0, i, 0)), _rowspec(384), _rowspec(384),
                  _fullspec((1, 384)), _fullspec((1, 384))],
        out_specs=_rowspec(384),
        out_shape=jax.ShapeDtypeStruct((R, 384), F32),
    )(y2, xs, pz, dexp, nw)


def ssd_out_bwd(dout, y2, xs, pz, dexp, nw):
    R = xs.shape[0]

    def body(do_ref, y_ref, xs_ref, z_ref, d_ref, nw_ref, dy_ref, dz_ref, dxs_ref, part_ref):
        z = z_ref[...]
        xs_v = xs_ref[...]
        yy = y_ref[0] + y_ref[1] + xs_v * d_ref[...]
        sig = _sigmoid(z)
        sz = z * sig
        g = yy * sz
        rstd, g0 = _group_rms(g)
        ghat = g * rstd
        do = do_ref[...]
        dgn = do * nw_ref[...]
        t = dgn * ghat
        t0 = jnp.sum(jnp.where(g0, t, 0.0), axis=-1, keepdims=True)
        t1 = jnp.sum(t, axis=-1, keepdims=True) - t0
        dg = rstd * (dgn - ghat * jnp.where(g0, t0, t1) * (1.0 / 192))
        dyy = dg * sz
        dy_ref[...] = dyy
        dz_ref[...] = (dg * yy * (sig * (1.0 + z * (1.0 - sig)))).astype(dz_ref.dtype)
        dxs_ref[...] = dyy * d_ref[...]
        part_ref[0] = jnp.concatenate([_colsum(do * ghat), _colsum(dyy * xs_v), jnp.zeros((6, 384), F32)], axis=0)

    return pl.pallas_call(
        body, name="ssd_out_bwd", grid=(R // TM,),
        in_specs=[_rowspec(384), pl.BlockSpec((2, TM, 384), lambda i: (0, i, 0)), _rowspec(384), _rowspec(384),
                  _fullspec((1, 384)), _fullspec((1, 384))],
        out_specs=[_rowspec(384), _rowspec(384), _rowspec(384), pl.BlockSpec((1, 8, 384), lambda i: (i, 0, 0))],
        out_shape=[jax.ShapeDtypeStruct((R, 384), F32), jax.ShapeDtypeStruct((R, 384), MXU),
                   jax.ShapeDtypeStruct((R, 384), F32), jax.ShapeDtypeStruct((R // TM, 8, 384), F32)],
    )(dout, y2, xs, pz, dexp, nw)


def ssd_prep_bwd_a(pxbc, plast, cw, cb, dtb, dxs_skip, dxs2, dbm2, dcm2, ddt2, blocks_per_sample):
    R = pxbc.shape[0]
    prev, nxt = _halo_specs(XBC, R)

    def body(cur_ref, prev_ref, nxt_ref, pl_ref, cw_ref, cb_ref, dtb_ref, dsk_ref, dxs_ref, dbm_ref, dcm_ref, ddt_ref,
             dpre_ref, dlast_ref, part_ref):
        i = pl.program_id(0)
        ext = _ext_rows(cur_ref[...], prev_ref[...], nxt_ref[...], i, blocks_per_sample)
        co = _conv_out(ext, cw_ref, cb_ref)
        sig = _sigmoid(co)
        up = jnp.concatenate([dsk_ref[...] + dxs_ref[0] + dxs_ref[1], dbm_ref[0] + dbm_ref[1],
                              dcm_ref[0] + dcm_ref[1]], axis=1)
        dpre = up * (sig * (1.0 + co * (1.0 - sig)))
        dpre_ref[...] = dpre
        raw = pl_ref[...] + dtb_ref[...]
        lane = lax.broadcasted_iota(jnp.int32, raw.shape, 1)
        ddt = (pltpu.roll(ddt_ref[0], DT0, axis=1) + pltpu.roll(ddt_ref[1], DT0 + SSD_HEADS, axis=1))
        ddt = jnp.where(jnp.logical_and(lane >= DT0, lane < DT0 + 2 * SSD_HEADS), ddt * _sigmoid(raw), 0.0)
        dlast_ref[...] = ddt.astype(dlast_ref.dtype)
        rows = [_colsum(dpre * _shift(ext, k - 1)) for k in range(4)]
        rows.append(_colsum(dpre))
        rows.append(jnp.concatenate([_colsum(ddt), jnp.zeros((1, XBC - 128), F32)], axis=1))
        rows.append(jnp.zeros((2, XBC), F32))
        part_ref[0] = jnp.concatenate(rows, axis=0)

    dirspec = lambda n: pl.BlockSpec((2, SB, n), lambda i: (0, i, 0))
    return pl.pallas_call(
        body, name="ssd_prep_bwd_a", grid=(R // SB,),
        in_specs=[_rowspec(XBC, SB), prev, nxt, _rowspec(128, SB), _fullspec((8, XBC)), _fullspec((1, XBC)),
                  _fullspec((1, 128)), _rowspec(384, SB), dirspec(384), dirspec(256), dirspec(256), dirspec(128)],
        out_specs=[_rowspec(XBC, SB), _rowspec(128, SB), pl.BlockSpec((1, 8, XBC), lambda i: (i, 0, 0))],
        out_shape=[jax.ShapeDtypeStruct((R, XBC), F32), jax.ShapeDtypeStruct((R, 128), MXU),
                   jax.ShapeDtypeStruct((R // SB, 8, XBC), F32)],
    )(pxbc, pxbc, pxbc, plast, cw, cb, dtb, dxs_skip, dxs2, dbm2, dcm2, ddt2)


def ssd_prep_bwd_b(dpre, cw, blocks_per_sample):
    R = dpre.shape[0]
    prev, nxt = _halo_specs(XBC, R)

    def body(cur_ref, prev_ref, nxt_ref, cw_ref, o_ref):
        i = pl.program_id(0)
        ext = _ext_rows(cur_ref[...], prev_ref[...], nxt_ref[...], i, blocks_per_sample)
        o_ref[...] = (cw_ref[0:1, :] * _shift(ext, 1) + cw_ref[1:2, :] * _shift(ext, 0)
                      + cw_ref[2:3, :] * _shift(ext, -1) + cw_ref[3:4, :] * _shift(ext, -2)).astype(o_ref.dtype)

    return pl.pallas_call(
        body, name="ssd_prep_bwd_b", grid=(R // SB,),
        in_specs=[_rowspec(XBC, SB), prev, nxt, _fullspec((8, XBC))],
        out_specs=_rowspec(XBC, SB),
        out_shape=jax.ShapeDtypeStruct((R, XBC), MXU),
    )(dpre, dpre, dpre, cw)


def _rope(u, cos, sa, sb):
    return u * cos + pltpu.roll(u, 120, axis=1) * sa + pltpu.roll(u, 8, axis=1) * sb


def _rope_t(du, cos, sa, sb):
    return du * cos + pltpu.roll(du * sa, 8, axis=1) + pltpu.roll(du * sb, 120, axis=1)


def mla_prep(pqa, pkva, plast, qnw, kvnw, wq, wk, wv, cos, sa, sb):
    R = pqa.shape[0]

    def body(qa_ref, kva_ref, pl_ref, qnw_ref, kvnw_ref, wq_ref, wk_ref, wv_ref, cos_ref, sa_ref, sb_ref,
             q_ref, k_ref, v_ref, cq_ref, ckv_ref):
        cos_v, sa_v, sb_v = cos_ref[...], sa_ref[...], sb_ref[...]
        xq, _ = _rms_hat(qa_ref[...])
        cq_ref[...] = (xq * qnw_ref[...]).astype(cq_ref.dtype)
        xkv, _ = _rms_hat(kva_ref[...])
        ckv_ref[...] = (xkv * kvnw_ref[...]).astype(ckv_ref.dtype)
        q = _dot(cq_ref[...], wq_ref[...])
        kn = _dot(ckv_ref[...], wk_ref[...])
        v_ref[...] = _dot(ckv_ref[...], wv_ref[...]).astype(v_ref.dtype)
        lane = lax.broadcasted_iota(jnp.int32, (TM, HP), 1)
        rope_lanes = jnp.logical_and(lane >= QK_NOPE, lane < QK_DIM)
        kr = _rope(jnp.where(rope_lanes, pltpu.roll(pl_ref[...], QK_NOPE, axis=1), 0.0), cos_v, sa_v, sb_v)
        for h in range(MLA_HEADS):
            cols = slice(h * HP, (h + 1) * HP)
            q_ref[:, cols] = (_rope(q[:, cols], cos_v, sa_v, sb_v) * Q_SCALE).astype(q_ref.dtype)
            k_ref[:, cols] = (kn[:, cols] + kr).astype(k_ref.dtype)

    return pl.pallas_call(
        body, name="mla_prep", grid=(R // TM,),
        in_specs=[_rowspec(256), _rowspec(256), _rowspec(128), _fullspec((1, 256)), _fullspec((1, 256)),
                  _fullspec((256, QW)), _fullspec((256, QW)), _fullspec((256, QW)),
                  _rowspec(HP), _rowspec(HP), _rowspec(HP)],
        out_specs=[_rowspec(QW), _rowspec(QW), _rowspec(QW), _rowspec(256), _rowspec(256)],
        out_shape=[jax.ShapeDtypeStruct((R, QW), MXU)] * 3 + [jax.ShapeDtypeStruct((R, 256), MXU)] * 2,
    )(pqa, pkva, plast, qnw, kvnw, wq, wk, wv, cos, sa, sb)


def mla_prep_bwd(dq, dk, dv, pqa, pkva, qnw, kvnw, wqt, wkt, wvt, cos, sa, sb):
    R = pqa.shape[0]

    def body(dq_ref, dk_ref, dv_ref, qa_ref, kva_ref, qnw_ref, kvnw_ref, wqt_ref, wkt_ref, wvt_ref,
             cos_ref, sa_ref, sb_ref, dqa_ref, dkva_ref, dkr_ref, dql_ref, dkm_ref, dvb_ref, part_ref):
        cos_v, sa_v, sb_v = cos_ref[...], sa_ref[...], sb_ref[...]
        lane = lax.broadcasted_iota(jnp.int32, (TM, HP), 1)
        rope_lanes = jnp.logical_and(lane >= QK_NOPE, lane < QK_DIM)
        dkr = jnp.zeros((TM, HP), F32)
        for h in range(MLA_HEADS):
            cols = slice(h * HP, (h + 1) * HP)
            dql_ref[:, cols] = (_rope_t(dq_ref[:, cols], cos_v, sa_v, sb_v) * ATT_SCALE).astype(dql_ref.dtype)
            dkh = dk_ref[:, cols] * LN2
            dkm_ref[:, cols] = jnp.where(lane < QK_NOPE, dkh, 0.0).astype(dkm_ref.dtype)
            dkr = dkr + jnp.where(rope_lanes, dkh, 0.0)
        dvb_ref[...] = dv_ref[...].astype(dvb_ref.dtype)
        dkr = jnp.where(rope_lanes, _rope_t(dkr, cos_v, sa_v, sb_v), 0.0)
        dkr_ref[...] = pltpu.roll(dkr, HP - QK_NOPE, axis=1).astype(dkr_ref.dtype)
        xq, rq = _rms_hat(qa_ref[...])
        dqa, dqnw = _rms_bwd(_dotg(dql_ref[...], wqt_ref[...], NT), xq, rq, qnw_ref[...])
        dqa_ref[...] = dqa.astype(dqa_ref.dtype)
        xkv, rkv = _rms_hat(kva_ref[...])
        dckv = _dotg(dkm_ref[...], wkt_ref[...], NT) + _dotg(dvb_ref[...], wvt_ref[...], NT)
        dkva, dkvnw = _rms_bwd(dckv, xkv, rkv, kvnw_ref[...])
        dkva_ref[...] = dkva.astype(dkva_ref.dtype)
        part_ref[0] = jnp.concatenate([dqnw, dkvnw, jnp.zeros((6, 256), F32)], axis=0)

    return pl.pallas_call(
        body, name="mla_prep_bwd", grid=(R // TM,),
        in_specs=[_rowspec(QW), _rowspec(QW), _rowspec(QW), _rowspec(256), _rowspec(256), _fullspec((1, 256)),
                  _fullspec((1, 256)), _fullspec((256, QW)), _fullspec((256, QW)), _fullspec((256, QW)),
                  _rowspec(HP), _rowspec(HP), _rowspec(HP)],
        out_specs=[_rowspec(256), _rowspec(256), _rowspec(128), _rowspec(QW), _rowspec(QW), _rowspec(QW),
                   pl.BlockSpec((1, 8, 256), lambda i: (i, 0, 0))],
        out_shape=[jax.ShapeDtypeStruct((R, 256), MXU), jax.ShapeDtypeStruct((R, 256), MXU),
                   jax.ShapeDtypeStruct((R, 128), MXU)] + [jax.ShapeDtypeStruct((R, QW), MXU)] * 3
                  + [jax.ShapeDtypeStruct((R // TM, 8, 256), F32)],
    )(dq, dk, dv, pqa, pkva, qnw, kvnw, wqt, wkt, wvt, cos, sa, sb)


ATT_SCALE = QK_DIM ** -0.5
TQ = 256


LOG2E = 1.4426950408889634
LN2 = 0.6931471805599453
Q_SCALE = ATT_SCALE * LOG2E


def _key_chunks(T, n=2):
    unit = 256 if T % 256 == 0 else 128
    units = T // unit
    sizes = [(units // n + (1 if i < units % n else 0)) * unit for i in range(n)]
    return [(sum(sizes[:i]), sz) for i, sz in enumerate(sizes) if sz]


def attn_fwd(q, k, v, nb, T):
    R = q.shape[0]
    nq = T // TQ
    chunks = _key_chunks(T, 4)

    def body(q_ref, k_ref, v_ref, o_ref, lse_ref):
        def logits(lo, n):
            return _dotg(q_ref[...], k_ref[lo:lo + n, :], NT)

        def weigh(s, lo, n):
            m = jnp.max(s, axis=-1, keepdims=True)
            p = jnp.exp2(s - m)
            return m, jnp.sum(p, axis=-1, keepdims=True), _dot(p.astype(MXU), v_ref[lo:lo + n, :])

        def parts_of(ranges):
            out, s = [], logits(*ranges[0])
            for j, (lo, n) in enumerate(ranges):
                nxt = logits(*ranges[j + 1]) if j + 1 < len(ranges) else None
                out.append(weigh(s, lo, n))
                s = nxt
            return out

        def finish(parts):
            m = parts[0][0]
            for pm, _, _ in parts[1:]:
                m = jnp.maximum(m, pm)
            l, o = 0.0, 0.0
            for pm, pl_, po in parts:
                a = jnp.exp2(pm - m)
                l = l + a * pl_
                o = o + a * po
            o_ref[...] = o / l
            lse_ref[...] = jnp.broadcast_to(m + jnp.log(l) * LOG2E, (TQ, HP))

        i = pl.program_id(2)
        pl.when(i == 0)(lambda: finish(parts_of([(0, CTX)])))
        pl.when(i > 0)(lambda: finish(parts_of(chunks)))

    qspec = pl.BlockSpec((TQ, HP), lambda b, h, i: (b * nq + i, h))
    kspec = pl.BlockSpec((T, HP), lambda b, h, i: (b, h))
    return pl.pallas_call(
        body, name="attn_fwd", grid=(nb, MLA_HEADS, nq),
        in_specs=[qspec, kspec, kspec], out_specs=[qspec, qspec],
        out_shape=[jax.ShapeDtypeStruct((R, QW), F32)] * 2,
        compiler_params=_cp(48),
    )(q, k, v)


def attn_bwd(q, k, v, o, lse, do, nb, T):
    R = q.shape[0]
    nq = T // TQ
    chunks = _key_chunks(T)

    def body(q_ref, k_ref, v_ref, o_ref, lse_ref, do_ref, dq_ref, dk_ref, dv_ref):
        i = pl.program_id(2)

        @pl.when(i == 0)
        def _():
            dk_ref[...] = jnp.zeros_like(dk_ref)
            dv_ref[...] = jnp.zeros_like(dv_ref)

        def run(chunks):
            qv = q_ref[...]
            dov = do_ref[...]
            dob = dov.astype(MXU)
            delta = jnp.sum(dov * o_ref[...], axis=-1, keepdims=True)
            lse_v = lse_ref[:, 0:1]
            dq = 0.0
            for lo, n in chunks:
                kv = k_ref[lo:lo + n, :]
                p = jnp.exp2(_dotg(qv, kv, NT) - lse_v)
                dp = _dotg(dob, v_ref[lo:lo + n, :], NT)
                dsb = (p * (dp - delta)).astype(MXU)
                dq = dq + _dot(dsb, kv)
                dk_ref[lo:lo + n, :] += _dotg(dsb, qv, TN)
                dv_ref[lo:lo + n, :] += _dotg(p.astype(MXU), dob, TN)
            dq_ref[...] = dq

        pl.when(i == 0)(lambda: run([(0, CTX)]))
        pl.when(i > 0)(lambda: run(chunks))

    qspec = pl.BlockSpec((TQ, HP), lambda b, h, i: (b * nq + i, h))
    kspec = pl.BlockSpec((T, HP), lambda b, h, i: (b, h))
    return pl.pallas_call(
        body, name="attn_bwd", grid=(nb, MLA_HEADS, nq),
        in_specs=[qspec, kspec, kspec, qspec, qspec, qspec],
        out_specs=[qspec, kspec, kspec],
        out_shape=[jax.ShapeDtypeStruct((R, QW), F32)] * 3,
        compiler_params=_cp(56),
    )(q, k, v, o, lse, do)


def _pool_geometry(i, blocks_per_sample, seq):
    j = i % blocks_per_sample
    n = jnp.where(j == 0, CTX, seq)
    t0 = jnp.where(j == 0, 0, (j - 1) * SB) - HALO
    lane = lax.broadcasted_iota(jnp.int32, (SB + 2 * HALO, POOL_DIM), 1)
    t = lax.broadcasted_iota(jnp.int32, (SB + 2 * HALO, POOL_DIM), 0) + t0
    wh = jnp.where(lane < 64, 1, jnp.where(lane < 128, 2, jnp.where(lane < 192, 4, 8)))
    cnt = jnp.minimum(t + wh, n) - jnp.maximum(t - wh, 0)
    return lane, 1.0 / jnp.maximum(cnt, 1).astype(F32)


def _by_window(lane, c2, c4, c8, c16):
    return jnp.where(lane < 64, c2, jnp.where(lane < 128, c4, jnp.where(lane < 192, c8, c16)))


def _window_sums(ext, lane, first):
    n = ext.shape[0]
    r = lambda a, s: pltpu.roll(a, s % n, axis=0)
    c2 = ext + r(ext, first)
    c4 = r(c2, 1) + r(c2, -1)
    c8 = r(c4, 2) + r(c4, -2)
    c16 = r(c8, 4) + r(c8, -4)
    return _by_window(lane, c2, c4, c8, c16)


def _pool_delta(ext, lane, inv):
    return (_window_sums(ext, lane, 1) * inv - ext)[HALO:HALO + SB, :]


def pool_fwd(ppool, wbd, scale, blocks_per_sample, seq):
    R = ppool.shape[0]
    prev, nxt = _halo_specs(POOL_DIM, R)

    def body(cur_ref, prev_ref, nxt_ref, w_ref, s_ref, o_ref):
        i = pl.program_id(0)
        ext = _ext_rows(cur_ref[...], prev_ref[...], nxt_ref[...], i, blocks_per_sample)
        lane, inv = _pool_geometry(i, blocks_per_sample, seq)
        dlt = _pool_delta(ext, lane, inv)
        o_ref[...] = _dot(dlt.astype(MXU), w_ref[...]) * s_ref[...]

    return pl.pallas_call(
        body, name="pool_fwd", grid=(R // SB,),
        in_specs=[_rowspec(POOL_DIM, SB), prev, nxt, _fullspec((POOL_DIM, POOL_DIM)), _fullspec((1, POOL_DIM))],
        out_specs=_rowspec(POOL_DIM, SB),
        out_shape=jax.ShapeDtypeStruct((R, POOL_DIM), F32),
    )(ppool, ppool, ppool, wbd, scale)


def pool_bwd(ppool, dpool, wbd, scale, blocks_per_sample, seq):
    R = ppool.shape[0]
    prev, nxt = _halo_specs(POOL_DIM, R)

    def body(cur_ref, prev_ref, nxt_ref, dcur_ref, dprev_ref, dnxt_ref, w_ref, s_ref, du_ref, dw_ref, part_ref):
        i = pl.program_id(0)

        @pl.when(i == 0)
        def _():
            dw_ref[...] = jnp.zeros_like(dw_ref)

        ext = _ext_rows(cur_ref[...], prev_ref[...], nxt_ref[...], i, blocks_per_sample)
        lane, inv = _pool_geometry(i, blocks_per_sample, seq)
        dlt = _pool_delta(ext, lane, inv).astype(MXU)
        dy = dcur_ref[...]
        part_ref[0] = jnp.concatenate([_colsum(dy * _dot(dlt, w_ref[...])), jnp.zeros((7, POOL_DIM), F32)], axis=0)
        dyp = (dy * s_ref[...]).astype(MXU)
        dw_ref[...] += _dotg(dlt, dyp, TN)
        dext = _ext_rows(dy, dprev_ref[...], dnxt_ref[...], i, blocks_per_sample)
        dd = _dotg((dext * s_ref[...]).astype(MXU), w_ref[...], NT)
        du_ref[...] = (_window_sums(dd * inv, lane, -1) - dd)[HALO:HALO + SB, :].astype(du_ref.dtype)

    return pl.pallas_call(
        body, name="pool_bwd", grid=(R // SB,),
        in_specs=[_rowspec(POOL_DIM, SB), prev, nxt, _rowspec(POOL_DIM, SB), prev, nxt,
                  _fullspec((POOL_DIM, POOL_DIM)), _fullspec((1, POOL_DIM))],
        out_specs=[_rowspec(POOL_DIM, SB), _fullspec((POOL_DIM, POOL_DIM)),
                   pl.BlockSpec((1, 8, POOL_DIM), lambda i: (i, 0, 0))],
        out_shape=[jax.ShapeDtypeStruct((R, POOL_DIM), MXU), jax.ShapeDtypeStruct((POOL_DIM, POOL_DIM), F32),
                   jax.ShapeDtypeStruct((R // SB, 8, POOL_DIM), F32)],
    )(ppool, ppool, ppool, dpool, dpool, dpool, wbd, scale)


def adamw(w, g, m, v, name="adamw"):
    rows, cols = w.shape
    tr = rows
    for cand in (512, 256, 128, 64, 32, 16, 8):
        if rows % cand == 0:
            tr = cand
            break
    bc1 = 1.0 - ADAM_B1 ** ADAM_STEP
    bc2 = 1.0 - ADAM_B2 ** ADAM_STEP

    def body(w_ref, g_ref, m_ref, v_ref, d_ref, nm_ref, nv_ref):
        g_v = g_ref[...]
        nm = ADAM_B1 * m_ref[...] + (1.0 - ADAM_B1) * g_v
        nv = ADAM_B2 * v_ref[...] + (1.0 - ADAM_B2) * (g_v * g_v)
        nm_ref[...] = nm
        nv_ref[...] = nv
        d_ref[...] = -ADAM_LR * ((nm / bc1) / (jnp.sqrt(nv / bc2) + ADAM_EPS) + ADAM_WD * w_ref[...])

    spec = pl.BlockSpec((tr, cols), lambda i: (i, 0))
    return pl.pallas_call(
        body, name=name, grid=(rows // tr,),
        in_specs=[spec] * 4, out_specs=[spec] * 3,
        out_shape=[jax.ShapeDtypeStruct((rows, cols), F32)] * 3,
    )(w, g, m, v)


MODR = 32


def _silu(v):
    return v * _sigmoid(v)


def mod_fwd(cond, w, b):
    n = w.shape[1]

    def body(c_ref, w_ref, b_ref, o_ref):
        o_ref[...] = _dot(_silu(c_ref[...]).astype(MXU), w_ref[...].astype(MXU)) + b_ref[...]

    return pl.pallas_call(
        body, name="mod_fwd", out_shape=jax.ShapeDtypeStruct((MODR, n), F32),
        in_specs=[_fullspec((MODR, D)), _fullspec((D, n)), _fullspec((1, n))], out_specs=_fullspec((MODR, n)),
        grid=(1,), compiler_params=_cp(40),
    )(cond, w, b)


def mod_wgrad(cond, dm):
    n = dm.shape[1]

    def body(c_ref, d_ref, o_ref):
        o_ref[...] = _dotg(_silu(c_ref[...]).astype(MXU), d_ref[...].astype(MXU), TN)

    return pl.pallas_call(
        body, name="mod_wgrad", out_shape=jax.ShapeDtypeStruct((D, n), F32),
        in_specs=[_fullspec((MODR, D)), _fullspec((MODR, n))], out_specs=_fullspec((D, n)),
        grid=(1,), compiler_params=_cp(40),
    )(cond, dm)


def mod_dgrad(dm, w):
    n = w.shape[1]

    def body(d_ref, w_ref, o_ref):
        o_ref[...] = _dotg(d_ref[...].astype(MXU), w_ref[...].astype(MXU), NT)

    return pl.pallas_call(
        body, name="mod_dgrad", out_shape=jax.ShapeDtypeStruct((8, D), F32),
        in_specs=[_fullspec((8, n)), _fullspec((D, n))], out_specs=_fullspec((8, D)),
        grid=(1,), compiler_params=_cp(40),
    )(dm, w)


def sum_leading(a, name="sum_leading"):
    n, r, c = a.shape

    def body(a_ref, o_ref):
        acc = a_ref[0]
        for k in range(1, n):
            acc = acc + a_ref[k]
        o_ref[...] = acc

    return pl.pallas_call(
        body, name=name, out_shape=jax.ShapeDtypeStruct((r, c), F32),
        in_specs=[_fullspec((n, r, c))], out_specs=_fullspec((r, c)), grid=(1,),
    )(a)


MESH = pl.DeviceIdType.MESH
NDEV = 8
ANY = pl.BlockSpec(memory_space=pl.ANY)


def _place():
    return lax.axis_index("x"), lax.axis_index("y"), lax.axis_index("c")


def _other_chips(x, y):
    return [(1 - x, y), (x, 1 - y), (1 - x, 1 - y)]


def allgather_small(v, name):
    r, cols = v.shape

    def body(v_ref, o_ref, send_sems, recv_sems):
        x, y, c = _place()
        me = 4 * x + 2 * y + c
        o_ref[me] = v_ref[...]
        copies = []
        for rel in range(1, NDEV):
            peer = (1 - x if rel & 4 else x, 1 - y if rel & 2 else y, 1 - c if rel & 1 else c)
            cp = pltpu.make_async_remote_copy(src_ref=v_ref, dst_ref=o_ref.at[me], send_sem=send_sems.at[rel - 1],
                                              recv_sem=recv_sems.at[rel - 1], device_id=peer, device_id_type=MESH)
            cp.start()
            copies.append(cp)
        for cp in copies:
            cp.wait_recv()
        for cp in copies:
            cp.wait_send()

    return pl.pallas_call(
        body, name=name, out_shape=jax.ShapeDtypeStruct((NDEV, r, cols), F32),
        in_specs=[pl.BlockSpec(memory_space=pltpu.VMEM)], out_specs=pl.BlockSpec(memory_space=pltpu.VMEM),
        scratch_shapes=[pltpu.SemaphoreType.DMA((NDEV - 1,)), pltpu.SemaphoreType.DMA((NDEV - 1,))],
        compiler_params=_cp(40),
    )(v)


def _sems(n):
    return [pltpu.SemaphoreType.DMA((n,)), pltpu.SemaphoreType.DMA((n,))]


def gather_job(arrs):
    n = len(arrs)

    def copy(srcs, outs, sems, i, slot, kk, cc, to, from_src=False):
        hr = arrs[i].shape[0] // 2
        dst = outs[i].at[kk, pl.ds(cc * hr, hr), :]
        return pltpu.make_async_remote_copy(src_ref=srcs[i].at[pl.ds(cc * hr, hr), :] if from_src else dst, dst_ref=dst,
                                            send_sem=sems[0].at[slot * n + i], recv_sem=sems[1].at[slot * n + i],
                                            device_id=to, device_id_type=MESH)

    def start(srcs, outs, sems):
        x, y, c = _place()
        for j, (px, py) in enumerate(_other_chips(x, y)):
            for i in range(n):
                copy(srcs, outs, sems, i, j, 2 * x + y, c, (px, py, c), True).start()

    def finish(srcs, outs, sems):
        x, y, c = _place()
        sib = (x, y, 1 - c)
        chips = _other_chips(x, y)
        passed = []
        for j, (px, py) in enumerate(chips):
            for i in range(n):
                copy(srcs, outs, sems, i, j, 2 * px + py, c, (px, py, c)).wait_recv()
                cp = copy(srcs, outs, sems, i, 3 + j, 2 * px + py, c, sib)
                cp.start()
                passed.append(cp)
        for j, (px, py) in enumerate(chips):
            for i in range(n):
                copy(srcs, outs, sems, i, 3 + j, 2 * px + py, 1 - c, sib).wait_recv()
        for j, (px, py) in enumerate(chips):
            for i in range(n):
                copy(srcs, outs, sems, i, j, 2 * x + y, c, (px, py, c), True).wait_send()
        for cp in passed:
            cp.wait_send()

    return _NS(ins=list(arrs), out_shapes=[jax.ShapeDtypeStruct((4,) + a.shape, a.dtype) for a in arrs], nsem=6 * n,
               start=start, finish=finish)


def chip_swap_job(ss):
    n = len(ss)

    def copies(srcs, outs, sems):
        x, y, c = _place()
        return [pltpu.make_async_remote_copy(src_ref=srcs[i].at[2 * px + py], dst_ref=outs[i].at[j],
                                             send_sem=sems[0].at[j * n + i], recv_sem=sems[1].at[j * n + i],
                                             device_id=(px, py, c), device_id_type=MESH)
                for j, (px, py) in enumerate(_other_chips(x, y)) for i in range(n)]

    def start(srcs, outs, sems):
        for cp in copies(srcs, outs, sems):
            cp.start()

    def finish(srcs, outs, sems):
        for cp in copies(srcs, outs, sems):
            cp.wait()

    return _NS(ins=list(ss), out_shapes=[jax.ShapeDtypeStruct((3,) + s.shape[1:], s.dtype) for s in ss], nsem=3 * n,
               start=start, finish=finish)


def run_job(job, name):
    n, m = len(job.ins), len(job.out_shapes)

    def body(*refs):
        srcs, outs, sems = refs[:n], refs[n:n + m], refs[n + m:]
        job.start(srcs, outs, sems)
        job.finish(srcs, outs, sems)

    return pl.pallas_call(body, name=name, out_shape=job.out_shapes, in_specs=[ANY] * n, out_specs=[ANY] * m,
                          scratch_shapes=_sems(job.nsem))(*job.ins)


def swap_core_halves(gs):
    n = len(gs)

    def body(*refs):
        srcs, outs = refs[:n], refs[n:2 * n]
        send_sems, recv_sems = refs[2 * n:]
        x, y, c = _place()
        copies = []
        for i in range(n):
            hr = gs[i].shape[1] // 2
            cp = pltpu.make_async_remote_copy(src_ref=srcs[i].at[:, pl.ds((1 - c) * hr, hr), :], dst_ref=outs[i],
                                              send_sem=send_sems.at[i], recv_sem=recv_sems.at[i],
                                              device_id=(x, y, 1 - c), device_id_type=MESH)
            cp.start()
            copies.append(cp)
        for cp in copies:
            cp.wait()

    return pl.pallas_call(
        body, name="swap_core_halves",
        out_shape=[jax.ShapeDtypeStruct((4, g.shape[1] // 2, g.shape[2]), g.dtype) for g in gs],
        in_specs=[ANY] * n, out_specs=[ANY] * n, scratch_shapes=_sems(n),
    )(*gs)


def add_half(g, r1, cidx, name):
    _, rows, cols = g.shape
    hr = rows // 2

    def body(c_ref, g_ref, r_ref, o_ref, ob_ref):
        s = g_ref[...] + r_ref[...]
        o_ref[...] = s
        ob_ref[...] = s.astype(BF16)

    blk = lambda f: pl.BlockSpec((1, hr, cols), f)
    return pl.pallas_call(
        body, name=name,
        out_shape=[jax.ShapeDtypeStruct((4, hr, cols), F32), jax.ShapeDtypeStruct((4, hr, cols), BF16)],
        grid_spec=pltpu.PrefetchScalarGridSpec(
            num_scalar_prefetch=1, grid=(4,),
            in_specs=[blk(lambda k, c_ref: (k, c_ref[0], 0)), blk(lambda k, c_ref: (k, 0, 0))],
            out_specs=[blk(lambda k, c_ref: (k, 0, 0)), blk(lambda k, c_ref: (k, 0, 0))]),
    )(cidx, g, r1)


def sum_parts(s1, r2, kidx, name):
    _, hr, cols = s1.shape

    def body(k_ref, s_ref, r_ref, o_ref):
        o_ref[...] = ((s_ref[0] + r_ref[0].astype(F32)) + r_ref[1].astype(F32)) + r_ref[2].astype(F32)

    return pl.pallas_call(
        body, name=name, out_shape=jax.ShapeDtypeStruct((hr, cols), F32),
        grid_spec=pltpu.PrefetchScalarGridSpec(
            num_scalar_prefetch=1, grid=(1,),
            in_specs=[pl.BlockSpec((1, hr, cols), lambda i, k_ref: (k_ref[0], 0, 0)),
                      pl.BlockSpec((3, hr, cols), lambda i, k_ref: (0, 0, 0))],
            out_specs=pl.BlockSpec((hr, cols), lambda i, k_ref: (0, 0))),
    )(kidx, s1, r2)


def swap_reduced_halves(hs):
    n = len(hs)

    def body(*refs):
        srcs, outs = refs[:n], refs[n:2 * n]
        send_sems, recv_sems = refs[2 * n:]
        x, y, c = _place()
        copies = []
        for i in range(n):
            cp = pltpu.make_async_remote_copy(src_ref=srcs[i], dst_ref=outs[i], send_sem=send_sems.at[i],
                                              recv_sem=recv_sems.at[i], device_id=(x, y, 1 - c), device_id_type=MESH)
            cp.start()
            copies.append(cp)
        for cp in copies:
            cp.wait()

    return pl.pallas_call(
        body, name="swap_reduced_halves", out_shape=[jax.ShapeDtypeStruct(h.shape, h.dtype) for h in hs],
        in_specs=[ANY] * n, out_specs=[ANY] * n, scratch_shapes=_sems(n),
    )(*hs)


def adamw_halves(w, m, v, own, oth, cidx, name):
    depth, rows, cols = w.shape
    hr = rows // 2
    tr = min(hr, 256)
    nblk = hr // tr
    bc1 = 1.0 - ADAM_B1 ** ADAM_STEP
    bc2 = 1.0 - ADAM_B2 ** ADAM_STEP

    def body(c_ref, w_ref, m_ref, v_ref, own0, own1, oth0, oth1, g_ref, d_ref, nm_ref, nv_ref):
        l = pl.program_id(0)
        hi = pl.program_id(1)
        mine = jnp.where(l == 0, own0[...], own1[...])
        other = jnp.where(l == 0, oth0[...], oth1[...])
        g_v = jnp.where(hi == c_ref[0], mine, other)
        nm = ADAM_B1 * m_ref[0] + (1.0 - ADAM_B1) * g_v
        nv = ADAM_B2 * v_ref[0] + (1.0 - ADAM_B2) * (g_v * g_v)
        g_ref[0] = g_v
        nm_ref[0] = nm
        nv_ref[0] = nv
        d_ref[0] = -ADAM_LR * ((nm / bc1) / (jnp.sqrt(nv / bc2) + ADAM_EPS) + ADAM_WD * w_ref[0])

    wspec = pl.BlockSpec((1, tr, cols), lambda l, hi, b, c_ref: (l, hi * nblk + b, 0))
    gspec = pl.BlockSpec((tr, cols), lambda l, hi, b, c_ref: (b, 0))
    assert depth == 2
    return pl.pallas_call(
        body, name=name, out_shape=[jax.ShapeDtypeStruct(w.shape, F32)] * 4,
        grid_spec=pltpu.PrefetchScalarGridSpec(
            num_scalar_prefetch=1, grid=(depth, 2, nblk),
            in_specs=[wspec] * 3 + [gspec] * 4, out_specs=[wspec] * 4),
    )(cidx, w, m, v, own[0], own[1], oth[0], oth[1])


class _NS:
    def __init__(self, **kw):
        self.__dict__.update(kw)


def _prep_in(win, conv_w, conv_b, dt_bias, a_log, ssd_d, ssd_nw, qnw, kvnw, pool_w, pool_scale, n1, n2):
    winp = jnp.concatenate([win[:, 0:384], win[:, 384:1280], win[:, 1292:1548], win[:, 1548:1804], win[:, 1836:2092],
                            win[:, 1804:1836], win[:, 1280:1292], jnp.zeros((D, NP - IN_COLS), win.dtype)], axis=1)
    wbd = (jnp.eye(4, dtype=F32)[:, None, :, None] * pool_w[:, :, None, :]).reshape(POOL_DIM, POOL_DIM).astype(MXU)
    a = -jnp.exp(a_log)
    return _NS(
        winp=winp, wbd=wbd,
        cw8=jnp.pad(conv_w, ((0, 4), (0, 0))), cb=conv_b[None],
        dtb=jnp.pad(dt_bias.reshape(1, 12), ((0, 0), (DT0, 128 - DT0 - 12))),
        arow=jnp.pad(a[:, None, :], ((0, 0), (0, 7), (0, 128 - SSD_HEADS))), a=a,
        dexp=jnp.repeat(ssd_d, SSD_P)[None], ssd_nw=ssd_nw[None], qnw=qnw[None], kvnw=kvnw[None],
        pscale=pool_scale[None], n1=n1[None], n2=n2[None])


def _prep_rest(wqb, wkvb, wout, w1, w2):
    wq = jnp.pad(wqb.reshape(256, MLA_HEADS, QK_DIM), ((0, 0), (0, 0), (0, HP - QK_DIM))).reshape(256, QW)
    kv3 = wkvb.reshape(256, MLA_HEADS, 128)
    wk = jnp.pad(kv3[:, :, :64], ((0, 0), (0, 0), (0, 64))).reshape(256, QW)
    wv = jnp.pad(kv3[:, :, 64:], ((0, 0), (0, 0), (0, 64))).reshape(256, QW)
    wo = jnp.concatenate([jnp.pad(wout[384:768].reshape(MLA_HEADS, 64, D), ((0, 0), (0, 64), (0, 0))).reshape(QW, D),
                          wout[0:384], wout[768:1024]], axis=0)
    return _NS(wq=wq, wk=wk, wv=wv, wo=wo, w1=w1, w2=w2)


def _prep_layer(win, wqb, wkvb, wout, w1, w2, *small):
    lw = _prep_in(win, *small)
    lw.__dict__.update(_prep_rest(wqb, wkvb, wout, w1, w2).__dict__)
    return lw


def _by_chip_cols(a):
    return jnp.stack([a[:, k * (a.shape[1] // 4):(k + 1) * (a.shape[1] // 4)] for k in range(4)])


def _by_chip_rows(a):
    return a.reshape(4, a.shape[0] // 4, a.shape[1])


def _unprep_in(dwinp):
    return jnp.concatenate([dwinp[:, 0:384], dwinp[:, 384:1280], dwinp[:, 2080:2092], dwinp[:, 1280:1536],
                            dwinp[:, 1536:1792], dwinp[:, 2048:2080], dwinp[:, 1792:2048]], axis=1)


def _unprep_rest(dwq, dwk, dwv, dwo):
    dwqb = dwq.reshape(256, MLA_HEADS, HP)[:, :, :QK_DIM].reshape(256, MLA_HEADS * QK_DIM)
    dwkvb = jnp.concatenate([dwk.reshape(256, MLA_HEADS, HP)[:, :, :64], dwv.reshape(256, MLA_HEADS, HP)[:, :, :64]],
                            axis=2).reshape(256, MLA_HEADS * 128)
    dwout = jnp.concatenate([dwo[QW:QW + 384], dwo[0:QW].reshape(MLA_HEADS, HP, D)[:, :64].reshape(384, D),
                             dwo[QW + 384:CAT]], axis=0)
    return dwqb, dwkvb, dwout


def _rope_tables(nb, N):
    t = jnp.arange(N, dtype=F32)
    row = jnp.floor(t / GRID_W)
    col = t - row * GRID_W
    inv = jnp.asarray(10000.0 ** (-np.arange(8, dtype=np.float32) / 8), F32)
    ang = jnp.stack([row[:, None] * inv, col[:, None] * inv], axis=1)
    cs, sn = jnp.cos(ang), jnp.sin(ang)
    zero = jnp.zeros_like(sn)
    lanes = lambda first, second: jnp.stack([first, second], axis=2).reshape(N, 32)
    pad = lambda a, fill: jnp.concatenate([jnp.full((N, 64), fill, F32), a, jnp.full((N, 32), fill, F32)], axis=1)
    tabs = []
    for tab, fill in ((pad(lanes(cs, cs), 1.0), 1.0), (pad(lanes(-sn, zero), 0.0), 0.0), (pad(lanes(zero, sn), 0.0), 0.0)):
        one = jnp.concatenate([jnp.full((CTX, 128), fill, F32), tab], axis=0)
        tabs.append(jnp.tile(one, (nb, 1)))
    return tabs


def _eexp():
    e = np.zeros((128, SSD_INNER), np.float32)
    for h in range(SSD_HEADS):
        e[h, h * SSD_P:(h + 1) * SSD_P] = 1.0
    return jnp.asarray(e)


class _NoHooks:
    def __init__(self, lws):
        self.lws = lws

    def weights_in(self, l):
        return _NS(**self.lws[l].__dict__)

    def weights_rest(self, l, scan_out):
        return self.lws[l]

    def job(self, where, l, early=None):
        return None

    def done(self, where, l, out):
        pass

    def layer_grads(self, l, g):
        pass


def _layer_fwd(X, bm, l, cst, hooks):
    nb, T, bps, N = cst.nb, cst.T, cst.bps, cst.N
    lw = hooks.weights_in(l)
    h1, pz, pxbc, pqa, pkva, ppool, plast = in_proj(X, bm, lw.n1, lw.winp)
    xs, bmat, cmat, dtv = ssd_prep(pxbc, plast, lw.cw8, lw.cb, lw.dtb, bps)
    y2, hin, out = ssd_scan_fwd(xs, bmat, cmat, dtv, lw.arow, cst.eexp, nb, T, hooks.job("fwd_scan", l))
    lw.__dict__.update(hooks.weights_rest(l, out).__dict__)
    ssd = ssd_out_fwd(y2, xs, pz, lw.dexp, lw.ssd_nw)
    q, k, v, cq, ckv = mla_prep(pqa, pkva, plast, lw.qnw, lw.kvnw, lw.wq, lw.wk, lw.wv, *cst.rope)
    attn, lse = attn_fwd(q, k, v, nb, T)
    pool = pool_fwd(ppool, lw.wbd, lw.pscale, bps, N)
    x1, mix, cat = mix_fwd(X, attn, ssd, pool, bm, lw.wo)
    x2, mo, r, h2, out = mlp_fwd(x1, bm, lw.n2, lw.w1, lw.w2, hooks.job("fwd_mlp", l))
    hooks.done("fwd_mlp", l, out)
    sv = _NS(X=X, h1=h1, pz=pz, pxbc=pxbc, pqa=pqa, pkva=pkva, ppool=ppool, plast=plast, xs=xs, bmat=bmat, cmat=cmat,
             dtv=dtv, y2=y2, hin=hin, q=q, k=k, v=v, cq=cq, ckv=ckv, attn=attn, lse=lse, x1=x1, mix=mix, cat=cat, mo=mo, r=r,
             h2=h2, lw=lw)
    return x2, sv


def _layer_bwd(dx2, bm, l, sv, cst, hooks):
    nb, T, bps, N = cst.nb, cst.T, cst.bps, cst.N
    lw = sv.lw
    dx1, du, dob, part_mlp, out = mlp_bwd(dx2, sv.x1, sv.mo, sv.r, bm, lw.n2, lw.w2, lw.w1, hooks.job("bwd_mlp", l))
    hooks.done("bwd_mlp", l, out)
    dw1 = mm_tn(sv.h2, du, name="wgrad_mlp1", col_blocks=True)
    dw2 = mm_tn(sv.r, dob, square_a=True, name="wgrad_mlp2")
    dattn, dssd, dpool, dmb, part_mix = mix_bwd(dx1, sv.mix, bm, lw.wo)
    dwo = mm_tn(sv.cat, dmb, name="wgrad_out")
    dppool, dwbd, part_pool = pool_bwd(sv.ppool, dpool, lw.wbd, lw.pscale, bps, N)
    dq, dk, dv = attn_bwd(sv.q, sv.k, sv.v, sv.attn, sv.lse, dattn, nb, T)
    dpqa, dpkva, dkr, dql, dkm, dvb, part_mla = mla_prep_bwd(dq, dk, dv, sv.pqa, sv.pkva, lw.qnw, lw.kvnw, lw.wq,
                                                             lw.wk, lw.wv, *cst.rope)
    dwq = mm_tn(sv.cq, dql, name="wgrad_q")
    dwk = mm_tn(sv.ckv, dkm, name="wgrad_k")
    dwv = mm_tn(sv.ckv, dvb, name="wgrad_v")
    dwqb, dwkvb, dwout = _unprep_rest(dwq, dwk, dwv, dwo)
    early = dict(w_q_b=_by_chip_cols(dwqb), w_kv_b=_by_chip_cols(dwkvb), w_out=_by_chip_rows(dwout), w_mlp1=dw1,
                 w_mlp2=_by_chip_rows(dw2))
    dyy, dz, dxs_skip, part_so = ssd_out_bwd(dssd, sv.y2, sv.xs, sv.pz, lw.dexp, lw.ssd_nw)
    dxs2, dbm2, dcm2, ddt2, da, out = ssd_scan_bwd(sv.xs, sv.bmat, sv.cmat, sv.dtv, lw.arow, cst.eexp, sv.hin, dyy,
                                                   nb, T, hooks.job("bwd_scan", l, early))
    hooks.done("bwd_scan", l, out)
    dpre, dlast_dt, part_conv = ssd_prep_bwd_a(sv.pxbc, sv.plast, lw.cw8, lw.cb, lw.dtb, dxs_skip, dxs2, dbm2, dcm2,
                                               ddt2, bps)
    dpxbc = ssd_prep_bwd_b(dpre, lw.cw8, bps)
    dx, dpb, part_in = in_proj_bwd(dx1, sv.X, dz, dpxbc, dpqa, dpkva, dppool, dkr, dlast_dt, bm, lw.n1, lw.winp)
    dwinp = mm_tn(sv.h1, dpb, name="wgrad_in")

    dmod = jnp.stack([part_in[:, 0], part_in[:, 1], part_mix[:, 0], part_mlp[:, 0], part_mlp[:, 1], part_mlp[:, 2]],
                     axis=1)
    dmod = dmod.reshape(nb, bps, 6, D)
    dm_rows = jnp.concatenate([jnp.sum(dmod[:, 1:], axis=1), jnp.sum(dmod[:, 0], axis=0)[None]], axis=0)
    da_dh = jnp.sum(da[:, :, 0, :SSD_HEADS], axis=1)
    conv_parts = jnp.sum(part_conv, axis=0)
    g = _NS(
        w_in=_by_chip_cols(_unprep_in(dwinp)), dm_rows=dm_rows.reshape(3, 6 * D), **early,
        norm1_w=jnp.sum(part_in[:, 2], axis=0), norm2_w=jnp.sum(part_mlp[:, 3], axis=0),
        conv_w=conv_parts[0:4], conv_b=conv_parts[4],
        dt_bias=conv_parts[5, DT0:DT0 + 12].reshape(2, SSD_HEADS), a_log=da_dh * lw.a,
        ssd_d=jnp.sum(jnp.sum(part_so[:, 1], axis=0).reshape(SSD_HEADS, SSD_P), axis=1),
        ssd_norm_w=jnp.sum(part_so[:, 0], axis=0),
        q_a_norm_w=jnp.sum(part_mla[:, 0], axis=0), kv_a_norm_w=jnp.sum(part_mla[:, 1], axis=0),
        pool_w=jnp.stack([dwbd[i * 64:(i + 1) * 64, i * 64:(i + 1) * 64] for i in range(4)]),
        pool_scale=jnp.sum(part_pool[:, 0], axis=0))
    hooks.layer_grads(l, g)
    return dx, g


def _local_step(x, ctx, tgt, bms, lws, fw, cst, hooks=None):
    nb, N = x.shape[0], x.shape[1]
    R = nb * cst.T
    hooks = _NoHooks(lws) if hooks is None else hooks
    X = jnp.concatenate([ctx, x], axis=1).reshape(R, D)
    saved = []
    for l in range(DEPTH):
        X, sv = _layer_fwd(X, bms[l], l, cst, hooks)
        saved.append(sv)
    dX, part_fin = final_loss(X, tgt.reshape(nb * N, D), fw[None], cst.bps)
    loss = (0.5 / D) * jnp.sum(part_fin[:, 1])
    dfw = jnp.sum(part_fin[:, 0], axis=0)
    grads = [None] * DEPTH
    for l in reversed(range(DEPTH)):
        dX, grads[l] = _layer_bwd(dX, bms[l], l, saved[l], cst, hooks)
    grad_x = dX.reshape(nb, cst.T, D)[:, CTX:, :]
    return loss, grad_x, grads, dfw


def _consts(nb, N):
    T = CTX + N
    bps = T // SB
    return _NS(nb=nb, N=N, T=T, bps=bps, eexp=_eexp(), rope=_rope_tables(nb, N))


def _block_mod(modrows, cst):
    rows = []
    for b in range(cst.nb):
        rows.append(modrows[cst.nb:cst.nb + 1])
        rows.append(jnp.broadcast_to(modrows[b:b + 1], (cst.bps - 1, 6, D)))
    return jnp.pad(jnp.concatenate(rows, axis=0), ((0, 0), (0, 2), (0, 0)))


SMALL = (("norm1_w", (2, D)), ("norm2_w", (2, D)), ("conv_w", (2, 4, XBC)), ("conv_b", (2, XBC)),
         ("dt_bias", (2, 2, 6)), ("a_log", (2, 2, 6)), ("ssd_d", (2, 6)), ("ssd_norm_w", (2, 384)),
         ("q_a_norm_w", (2, 256)), ("kv_a_norm_w", (2, 256)), ("pool_w", (2, 4, 64, 64)), ("pool_scale", (2, 256)),
         ("final_norm_w", (D,)), ("mod_b", (2, 6 * D)))
SMALL_ROWS = 64
DM_ROWS = 48


def _pack_small(vals):
    flat = jnp.concatenate([vals[n].reshape(-1) for n, _ in SMALL])
    return jnp.pad(flat, (0, SMALL_ROWS * D - flat.shape[0])).reshape(SMALL_ROWS, D)


def _unpack_small(p):
    flat = p.reshape(-1)
    out, off = {}, 0
    for n, shp in SMALL:
        size = int(np.prod(shp))
        out[n] = flat[off:off + size].reshape(shp)
        off += size
    return out


def cctx_grad(parts, c_ctx):
    def body(p_ref, c_ref, o_ref):
        acc = ((p_ref[0] + p_ref[1]) + p_ref[2]) + p_ref[3]
        v = c_ref[...]
        sig = _sigmoid(v)
        o_ref[...] = acc * (sig * (1.0 + v * (1.0 - sig)))

    return pl.pallas_call(
        body, name="cctx_grad", out_shape=jax.ShapeDtypeStruct((8, D), F32),
        in_specs=[_fullspec((4, 8, D)), _fullspec((1, D))], out_specs=_fullspec((8, D)), grid=(1,),
    )(parts, c_ctx)


def kernel(x, c, ctx, c_ctx, mod_w, mod_b, norm1_w, norm2_w, w_in, conv_w, conv_b, dt_bias, a_log, ssd_d, ssd_norm_w, q_a_norm_w, w_q_b, kv_a_norm_w, w_kv_b, pool_w, pool_scale, w_out, w_mlp1, w_mlp2, final_norm_w, loss_target, m_c_ctx, m_mod_w, m_mod_b, m_norm1_w, m_norm2_w, m_w_in, m_conv_w, m_conv_b, m_dt_bias, m_a_log, m_ssd_d, m_ssd_norm_w, m_q_a_norm_w, m_w_q_b, m_kv_a_norm_w, m_w_kv_b, m_pool_w, m_pool_scale, m_w_out, m_w_mlp1, m_w_mlp2, m_final_norm_w, v_c_ctx, v_mod_w, v_mod_b, v_norm1_w, v_norm2_w, v_w_in, v_conv_w, v_conv_b, v_dt_bias, v_a_log, v_ssd_d, v_ssd_norm_w, v_q_a_norm_w, v_w_q_b, v_kv_a_norm_w, v_w_kv_b, v_pool_w, v_pool_scale, v_w_out, v_w_mlp1, v_w_mlp2, v_final_norm_w):
    nb, N = x.shape[0], x.shape[1]
    cst = _consts(nb, N)
    xi, yi, ci = _place()
    me = 4 * xi + 2 * yi + ci
    kchip = 2 * xi + yi
    mcols = mod_w.shape[2]
    cshard = conv_w.shape[2]

    blk = jnp.zeros((16, D), F32).at[0:nb].set(c).at[8:16, 0:cshard].set(conv_w.reshape(8, cshard))
    g1 = allgather_small(blk, "gather_cond")
    cond = jnp.concatenate([g1[:, 0:nb].reshape(NDEV * nb, D), c_ctx[None],
                            jnp.zeros((MODR - NDEV * nb - 1, D), F32)], axis=0)
    conv_full = [jnp.concatenate([g1[2 * k, 8 + 4 * l:12 + 4 * l, 0:cshard] for k in range(4)], axis=1)
                 for l in range(DEPTH)]

    mb = [lax.dynamic_slice_in_dim(mod_b[l], kchip * mcols, mcols)[None] for l in range(DEPTH)]
    ms = jnp.concatenate([mod_fwd(cond, mod_w[l], mb[l]) for l in range(DEPTH)], axis=0)
    g2 = allgather_small(ms, "gather_mod")
    bms = []
    for l in range(DEPTH):
        m_all = jnp.concatenate([g2[2 * k, MODR * l:MODR * (l + 1)] for k in range(4)], axis=1)
        mine = jnp.concatenate([lax.dynamic_slice_in_dim(m_all, nb * me, nb), m_all[NDEV * nb:NDEV * nb + 1]], axis=0)
        bms.append(_block_mod(mine.reshape(nb + 1, 6, D), cst))

    assert DEPTH == 2
    big = (w_in, w_q_b, w_kv_b, w_out, w_mlp1, w_mlp2)
    names = ("w_in", "w_q_b", "w_kv_b", "w_out", "w_mlp1", "w_mlp2")
    concat_axis = dict(w_in=1, w_q_b=1, w_kv_b=1, w_out=0, w_mlp1=1, w_mlp2=0)
    cidx = jnp.reshape(ci, (1,)).astype(jnp.int32)
    kidx = jnp.reshape(kchip, (1,)).astype(jnp.int32)
    shards = [{n: a[l].astype(MXU) for n, a in zip(names, big)} for l in range(DEPTH)]

    def core_sums(gs):
        ns = list(gs)
        got = swap_core_halves([gs[n] for n in ns])
        return {n: add_half(gs[n], r, cidx, "add_half_" + n) for n, r in zip(ns, got)}

    class Hooks:
        gathered = [dict(w_in=run_job(gather_job([shards[0]["w_in"]]), "gather_w_in")[0]), {}]
        core_sum = [{}, {}]
        received = [{}, {}]

        def whole(self, l, n):
            return jnp.concatenate([jnp.where(kchip == k, shards[l][n], self.gathered[l][n][k]) for k in range(4)],
                                   axis=concat_axis[n])

        def weights_in(self, l):
            return _prep_in(self.whole(l, "w_in"), conv_full[l], conv_b[l], dt_bias[l], a_log[l], ssd_d[l], ssd_norm_w[l],
                            q_a_norm_w[l], kv_a_norm_w[l], pool_w[l], pool_scale[l], norm1_w[l], norm2_w[l])

        def weights_rest(self, l, scan_out):
            if l == 0:
                self.gathered[0].update(zip(names[1:], scan_out))
            return _prep_rest(*[self.whole(l, n) for n in names[1:]])

        def job(self, where, l, early=None):
            if l != 0:
                return None
            if where == "fwd_scan":
                return gather_job([shards[0][n] for n in names[1:]])
            if where == "fwd_mlp":
                return gather_job([shards[1][n] for n in names])
            if where == "bwd_mlp":
                return chip_swap_job([self.core_sum[1][n][1] for n in names])
            self.core_sum[0].update(core_sums(early))
            return chip_swap_job([self.core_sum[0][n][1] for n in names[1:]])

        def done(self, where, l, out):
            if l != 0:
                return
            if where == "fwd_mlp":
                self.gathered[1].update(zip(names, out))
            elif where == "bwd_mlp":
                self.received[1].update(zip(names, out))
            elif where == "bwd_scan":
                self.received[0].update(zip(names[1:], out))

        def layer_grads(self, l, g):
            if l == 1:
                self.core_sum[1] = core_sums({n: getattr(g, n) for n in names})
            else:
                self.core_sum[0].update(core_sums(dict(w_in=g.w_in)))
                self.received[0]["w_in"] = run_job(chip_swap_job([self.core_sum[0]["w_in"][1]]), "swap_w_in")[0]

    hooks = Hooks()
    loss_part, grad_x, grads, dfw = _local_step(x, ctx, loss_target, bms, None, final_norm_w, cst, hooks)
    loss = lax.psum(loss_part, ("x", "y", "c"))
    g_own = [sum_parts(hooks.core_sum[l][n][0], hooks.received[l][n], kidx, "sum_parts_" + n)
             for n in names for l in range(DEPTH)]
    g_oth = swap_reduced_halves(g_own)

    small = {n: jnp.stack([getattr(grads[l], n) for l in range(DEPTH)]) for n, _ in SMALL if n not in ("final_norm_w", "mod_b")}
    small["final_norm_w"] = dfw
    small["mod_b"] = jnp.stack([jnp.sum(grads[l].dm_rows, axis=0) for l in range(DEPTH)])
    dm = jnp.pad(jnp.concatenate([grads[l].dm_rows for l in range(DEPTH)], axis=0), ((0, 8 - 3 * DEPTH), (0, 0)))
    g3 = allgather_small(jnp.concatenate([_pack_small(small), dm.reshape(DM_ROWS, D)], axis=0), "gather_small")
    tot = sum_leading(g3, "sum_small")
    gsmall = _unpack_small(tot[0:SMALL_ROWS])
    ctx_sum = tot[SMALL_ROWS:].reshape(8, 6 * D)
    dm_dev = g3[:, SMALL_ROWS:].reshape(NDEV, 8, 6 * D)
    g_mod_w, dpart = [], jnp.zeros((8, D), F32)
    for l in range(DEPTH):
        dm_all = jnp.concatenate([dm_dev[:, 3 * l:3 * l + nb].reshape(NDEV * nb, 6 * D), ctx_sum[3 * l + nb:3 * l + nb + 1],
                                  jnp.zeros((MODR - NDEV * nb - 1, 6 * D), F32)], axis=0)
        g_mod_w.append(mod_wgrad(cond, lax.dynamic_slice_in_dim(dm_all, kchip * mcols, mcols, axis=1)))
        dctx = jnp.pad(lax.dynamic_slice_in_dim(ctx_sum[3 * l + nb:3 * l + nb + 1], kchip * mcols, mcols, axis=1), ((0, 7), (0, 0)))
        dpart = dpart + mod_dgrad(dctx, mod_w[l])
    g4 = allgather_small(dpart, "gather_cctx")
    g_c_ctx = cctx_grad(g4[0::2], c_ctx[None])[0]

    res = {}
    moments = ((m_w_in, v_w_in), (m_w_q_b, v_w_q_b), (m_w_kv_b, v_w_kv_b), (m_w_out, v_w_out), (m_w_mlp1, v_w_mlp1),
               (m_w_mlp2, v_w_mlp2))
    for i, (n, w, (m, v)) in enumerate(zip(names, big, moments)):
        res[n] = tuple(adamw_halves(w, m, v, g_own[DEPTH * i:DEPTH * (i + 1)], g_oth[DEPTH * i:DEPTH * (i + 1)], cidx,
                                    "adamw_" + n))
    g_mw = jnp.stack(g_mod_w)
    r_mw = adamw(mod_w.reshape(-1, mcols), g_mw.reshape(-1, mcols), m_mod_w.reshape(-1, mcols),
                 v_mod_w.reshape(-1, mcols), name="adamw_mod_w")
    res["mod_w"] = (g_mw,) + tuple(a.reshape(mod_w.shape) for a in r_mw)

    given = dict(norm1_w=(norm1_w, m_norm1_w, v_norm1_w), norm2_w=(norm2_w, m_norm2_w, v_norm2_w),
                 conv_b=(conv_b, m_conv_b, v_conv_b), dt_bias=(dt_bias, m_dt_bias, v_dt_bias),
                 a_log=(a_log, m_a_log, v_a_log), ssd_d=(ssd_d, m_ssd_d, v_ssd_d),
                 ssd_norm_w=(ssd_norm_w, m_ssd_norm_w, v_ssd_norm_w), q_a_norm_w=(q_a_norm_w, m_q_a_norm_w, v_q_a_norm_w),
                 kv_a_norm_w=(kv_a_norm_w, m_kv_a_norm_w, v_kv_a_norm_w), pool_w=(pool_w, m_pool_w, v_pool_w),
                 pool_scale=(pool_scale, m_pool_scale, v_pool_scale),
                 final_norm_w=(final_norm_w, m_final_norm_w, v_final_norm_w), mod_b=(mod_b, m_mod_b, v_mod_b))
    zero_cw = jnp.zeros((2, 4, XBC), F32)
    packs = [_pack_small({n: (given[n][i] if n in given else zero_cw) for n, _ in SMALL}) for i in range(3)]
    r_small = [_unpack_small(a) for a in adamw(packs[0], tot[0:SMALL_ROWS], packs[1], packs[2], name="adamw_small")]
    for n in given:
        res[n] = (gsmall[n], r_small[0][n], r_small[1][n], r_small[2][n])

    g_cw = lax.dynamic_slice_in_dim(gsmall["conv_w"], kchip * cshard, cshard, axis=2)
    padcw = lambda a: jnp.pad(a.reshape(8, cshard), ((0, 0), (0, 256 - cshard)))
    r_cw = adamw(padcw(conv_w), padcw(g_cw), padcw(m_conv_w), padcw(v_conv_w), name="adamw_conv_w")
    res["conv_w"] = (g_cw,) + tuple(a[:, 0:cshard].reshape(conv_w.shape) for a in r_cw)
    r_cc = adamw(c_ctx.reshape(8, 128), g_c_ctx.reshape(8, 128), m_c_ctx.reshape(8, 128), v_c_ctx.reshape(8, 128),
                 name="adamw_c_ctx")
    res["c_ctx"] = (g_c_ctx,) + tuple(a.reshape(D) for a in r_cc)

    order = ("c_ctx", "mod_w", "mod_b", "norm1_w", "norm2_w", "w_in", "conv_w", "conv_b", "dt_bias", "a_log", "ssd_d",
             "ssd_norm_w", "q_a_norm_w", "w_q_b", "kv_a_norm_w", "w_kv_b", "pool_w", "pool_scale", "w_out", "w_mlp1",
             "w_mlp2", "final_norm_w")
    return (loss, grad_x) + tuple(res[n][i] for i in range(4) for n in order)
```

```python
import functools
import math

import numpy as np
import jax
import jax.numpy as jnp
from jax import lax
from jax.experimental import pallas as pl
from jax.experimental.pallas import tpu as pltpu

F32 = jnp.float32
BF16 = jnp.bfloat16
MXU = jnp.bfloat16

D = 1024
DEPTH = 2
GRID_W = 64
CTX = 256
EPS = 1e-6
SSD_HEADS = 6
SSD_P = 64
SSD_INNER = 384
SSD_N = 128
CHUNK = 128
XBC = 896
MLA_HEADS = 6
QK_NOPE = 64
QK_ROPE = 32
QK_DIM = 96
HP = 128
QW = MLA_HEADS * HP
POOL_DIM = 256
D_FF = 4096
FF_BLK = 1024
IN_COLS = 2092
NP = 2176
P_SPLITS = (384, 896, 256, 256, 256, 128)
DT0 = 32
CAT = QW + SSD_INNER + POOL_DIM

SB = 256
TM = 512
HALO = 8

ADAM_LR = 0.001
ADAM_B1 = 0.9
ADAM_B2 = 0.999
ADAM_EPS = 1e-08
ADAM_WD = 0.01
ADAM_STEP = 10

NT = (((1,), (1,)), ((), ()))
TN = (((0,), (0,)), ((), ()))


def _cp(vmem_mb=None):
    if vmem_mb is None:
        return pltpu.CompilerParams()
    return pltpu.CompilerParams(vmem_limit_bytes=vmem_mb << 20)


def _dot(a, b):
    return jnp.dot(a, b, preferred_element_type=F32)


def _dotg(a, b, dims):
    return lax.dot_general(a, b, dims, preferred_element_type=F32)


def _dot_hi(a, b, dims=None, sel_first=False):
    dims = (((1,), (0,)), ((), ())) if dims is None else dims
    v, s = (b, a) if sel_first else (a, b)
    hi = v.astype(BF16)
    lo = (v - hi.astype(F32)).astype(BF16)
    s = s.astype(BF16)
    if sel_first:
        return _dotg(s, hi, dims) + _dotg(s, lo, dims)
    return _dotg(hi, s, dims) + _dotg(lo, s, dims)


def _rms_hat(x):
    rstd = lax.rsqrt(jnp.mean(x * x, axis=-1, keepdims=True) + EPS)
    return x * rstd, rstd


def _rms_bwd(dn, xhat, rstd, w):
    dxhat = dn * w
    dx = rstd * (dxhat - xhat * jnp.mean(dxhat * xhat, axis=-1, keepdims=True))
    return dx, jnp.sum(dn * xhat, axis=0, keepdims=True)


def _sigmoid(z):
    return 1.0 / (1.0 + jnp.exp(-z))


def _colsum(a):
    return jnp.sum(a, axis=0, keepdims=True)


def _rowspec(cols, tm=TM):
    return pl.BlockSpec((tm, cols), lambda i: (i, 0))


def _fullspec(shape):
    n = len(shape)
    return pl.BlockSpec(shape, lambda *_: (0,) * n)


def _resident(shape):
    n = len(shape)
    return pl.BlockSpec(shape, lambda *_: (0,) * n, pipeline_mode=pl.Buffered(1))


def _halo_specs(cols, nrows):
    per = SB // HALO
    last = nrows // HALO - 1
    prev = pl.BlockSpec((HALO, cols), lambda i: (jnp.maximum(i * per - 1, 0), 0))
    nxt = pl.BlockSpec((HALO, cols), lambda i: (jnp.minimum((i + 1) * per, last), 0))
    return prev, nxt


def _ext_rows(cur, prev, nxt, i, blocks_per_sample):
    j = i % blocks_per_sample
    first = jnp.logical_or(j == 0, j == 1)
    last = jnp.logical_or(j == 0, j == blocks_per_sample - 1)
    p = jnp.where(first, 0.0, prev)
    n = jnp.where(last, 0.0, nxt)
    return jnp.concatenate([p, cur, n], axis=0)


def _shift(ext, s):
    n = ext.shape[0]
    return pltpu.roll(ext, (-s) % n, axis=0)[HALO:HALO + SB, :]


def in_proj(x, bm, nw, w):
    R = x.shape[0]

    def body(x_ref, bm_ref, nw_ref, w_ref, h_ref, *outs):
        for s in range(TM // SB):
            rows = slice(s * SB, (s + 1) * SB)
            xhat, _ = _rms_hat(x_ref[rows, :])
            h = xhat * nw_ref[...] * (1.0 + bm_ref[s, 1:2, :]) + bm_ref[s, 0:1, :]
            h_ref[rows, :] = h.astype(h_ref.dtype)
        p = _dot(h_ref[...], w_ref[...])
        off = 0
        for o, n in zip(outs, P_SPLITS):
            o[...] = p[:, off:off + n]
            off += n

    return pl.pallas_call(
        body, name="in_proj", grid=(R // TM,),
        in_specs=[_rowspec(D), pl.BlockSpec((TM // SB, 8, D), lambda i: (i, 0, 0)), _fullspec((1, D)),
                  _fullspec((D, NP))],
        out_specs=[_rowspec(D)] + [_rowspec(n) for n in P_SPLITS],
        out_shape=[jax.ShapeDtypeStruct((R, D), MXU)] + [jax.ShapeDtypeStruct((R, n), F32) for n in P_SPLITS],
        compiler_params=_cp(56),
    )(x, bm, nw, w)


def in_proj_bwd(dx1, x, dz, dxbc, dqa, dkva, dpool, dkr, ddt, bm, nw, wt):
    R = x.shape[0]

    def body(dx1_ref, x_ref, dz_ref, dxbc_ref, dqa_ref, dkva_ref, dpool_ref, dkr_ref, ddt_ref, bm_ref, nw_ref,
             wt_ref, dx_ref, dp_ref, part_ref):
        dp_ref[:, 0:384] = dz_ref[...].astype(dp_ref.dtype)
        dp_ref[:, 384:1280] = dxbc_ref[...].astype(dp_ref.dtype)
        dp_ref[:, 1280:1536] = dqa_ref[...].astype(dp_ref.dtype)
        dp_ref[:, 1536:1792] = dkva_ref[...].astype(dp_ref.dtype)
        dp_ref[:, 1792:2048] = dpool_ref[...].astype(dp_ref.dtype)
        dp_ref[:, 2048:2176] = (dkr_ref[...] + ddt_ref[...]).astype(dp_ref.dtype)
        dh = _dotg(dp_ref[...], wt_ref[...], NT)
        w = nw_ref[...]
        for s in range(TM // SB):
            rows = slice(s * SB, (s + 1) * SB)
            xhat, rstd = _rms_hat(x_ref[rows, :])
            dhs = dh[rows, :]
            sc1 = 1.0 + bm_ref[s, 1:2, :]
            dx, dnw = _rms_bwd(dhs * sc1, xhat, rstd, w)
            dx_ref[rows, :] = dx1_ref[rows, :] + dx
            part_ref[s] = jnp.concatenate(
                [_colsum(dhs), _colsum(dhs * xhat * w), dnw, jnp.zeros((5, D), F32)], axis=0)

    return pl.pallas_call(
        body, name="in_proj_bwd", grid=(R // TM,),
        in_specs=[_rowspec(D), _rowspec(D), _rowspec(384), _rowspec(896), _rowspec(256), _rowspec(256),
                  _rowspec(256), _rowspec(128), _rowspec(128),
                  pl.BlockSpec((TM // SB, 8, D), lambda i: (i, 0, 0)), _fullspec((1, D)), _fullspec((D, NP))],
        out_specs=[_rowspec(D), _rowspec(NP), pl.BlockSpec((TM // SB, 8, D), lambda i: (i, 0, 0))],
        out_shape=[jax.ShapeDtypeStruct((R, D), F32), jax.ShapeDtypeStruct((R, NP), MXU),
                   jax.ShapeDtypeStruct((R // SB, 8, D), F32)],
        compiler_params=_cp(56),
    )(dx1, x, dz, dxbc, dqa, dkva, dpool, dkr, ddt, bm, nw, wt)


def mix_fwd(x, attn, ssd, pool, bm, wo):
    R = x.shape[0]

    def body(x_ref, a_ref, s_ref, p_ref, bm_ref, wo_ref, x1_ref, mix_ref, cat_ref):
        cat_ref[:, 0:QW] = a_ref[...].astype(cat_ref.dtype)
        cat_ref[:, QW:QW + SSD_INNER] = s_ref[...].astype(cat_ref.dtype)
        cat_ref[:, QW + SSD_INNER:CAT] = p_ref[...].astype(cat_ref.dtype)
        mix = _dot(cat_ref[...], wo_ref[...])
        mix_ref[...] = mix
        for s in range(TM // SB):
            rows = slice(s * SB, (s + 1) * SB)
            x1_ref[rows, :] = x_ref[rows, :] + bm_ref[s, 2:3, :] * mix[rows, :]

    return pl.pallas_call(
        body, name="mix_fwd", grid=(R // TM,),
        in_specs=[_rowspec(D), _rowspec(QW), _rowspec(SSD_INNER), _rowspec(POOL_DIM),
                  pl.BlockSpec((TM // SB, 8, D), lambda i: (i, 0, 0)), _fullspec((CAT, D))],
        out_specs=[_rowspec(D), _rowspec(D), _rowspec(CAT)],
        out_shape=[jax.ShapeDtypeStruct((R, D), F32), jax.ShapeDtypeStruct((R, D), F32),
                   jax.ShapeDtypeStruct((R, CAT), MXU)],
        compiler_params=_cp(48),
    )(x, attn, ssd, pool, bm, wo)


def mix_bwd(dx1, mix, bm, wot):
    R = dx1.shape[0]

    def body(dx1_ref, mix_ref, bm_ref, wot_ref, da_ref, ds_ref, dpl_ref, dmb_ref, part_ref):
        for s in range(TM // SB):
            rows = slice(s * SB, (s + 1) * SB)
            d = dx1_ref[rows, :]
            dmb_ref[rows, :] = (d * bm_ref[s, 2:3, :]).astype(dmb_ref.dtype)
            part_ref[s] = jnp.concatenate([_colsum(d * mix_ref[rows, :]), jnp.zeros((7, D), F32)], axis=0)
        dcat = _dotg(dmb_ref[...], wot_ref[...], NT)
        da_ref[...] = dcat[:, 0:QW]
        ds_ref[...] = dcat[:, QW:QW + SSD_INNER]
        dpl_ref[...] = dcat[:, QW + SSD_INNER:CAT]

    return pl.pallas_call(
        body, name="mix_bwd", grid=(R // TM,),
        in_specs=[_rowspec(D), _rowspec(D), pl.BlockSpec((TM // SB, 8, D), lambda i: (i, 0, 0)),
                  _fullspec((CAT, D))],
        out_specs=[_rowspec(QW), _rowspec(SSD_INNER), _rowspec(POOL_DIM), _rowspec(D),
                   pl.BlockSpec((TM // SB, 8, D), lambda i: (i, 0, 0))],
        out_shape=[jax.ShapeDtypeStruct((R, QW), F32), jax.ShapeDtypeStruct((R, SSD_INNER), F32),
                   jax.ShapeDtypeStruct((R, POOL_DIM), F32), jax.ShapeDtypeStruct((R, D), MXU),
                   jax.ShapeDtypeStruct((R // SB, 8, D), F32)],
        compiler_params=_cp(48),
    )(dx1, mix, bm, wot)


def mlp_fwd(x1, bm, nw, w1, w2, side=None):
    R = x1.shape[0]

    def body(x1_ref, bm_ref, nw_ref, w1_ref, w2_ref, x2_ref, mo_ref, r_ref, h2_ref):
        for s in range(TM // SB):
            rows = slice(s * SB, (s + 1) * SB)
            xhat, _ = _rms_hat(x1_ref[rows, :])
            h = xhat * nw_ref[...] * (1.0 + bm_ref[s, 4:5, :]) + bm_ref[s, 3:4, :]
            h2_ref[rows, :] = h.astype(h2_ref.dtype)
        for j in range(D_FF // FF_BLK):
            cols = slice(j * FF_BLK, (j + 1) * FF_BLK)
            r = jnp.maximum(_dot(h2_ref[...], w1_ref[:, cols]), 0.0)
            r_ref[:, cols] = r.astype(r_ref.dtype)
            d = _dot((r * r).astype(MXU), w2_ref[cols, :])
            if j == 0:
                mo_ref[...] = d
            else:
                mo_ref[...] += d
        for s in range(TM // SB):
            rows = slice(s * SB, (s + 1) * SB)
            x2_ref[rows, :] = x1_ref[rows, :] + bm_ref[s, 5:6, :] * mo_ref[rows, :]

    grid = (R // TM,)
    body, side_in, side_out, side_shapes, side_scratch, side_args = _side_wrap(body, 5, 4, 0, side, grid)
    outs = pl.pallas_call(
        body, name="mlp_fwd" if side is None else "mlp_fwd_comm", grid=grid,
        in_specs=[_rowspec(D), pl.BlockSpec((TM // SB, 8, D), lambda i: (i, 0, 0)), _fullspec((1, D)),
                  _resident((D, D_FF)), _resident((D_FF, D))] + side_in,
        out_specs=[_rowspec(D), _rowspec(D), _rowspec(D_FF), _rowspec(D)] + side_out,
        out_shape=[jax.ShapeDtypeStruct((R, D), F32), jax.ShapeDtypeStruct((R, D), F32),
                   jax.ShapeDtypeStruct((R, D_FF), BF16), jax.ShapeDtypeStruct((R, D), MXU)] + side_shapes,
        scratch_shapes=side_scratch,
        compiler_params=_cp(56),
    )(x1, bm, nw, w1, w2, *side_args)
    return tuple(outs[:4]) + (list(outs[4:]),)


def mlp_bwd(dx2, x1, mo, r, bm, nw, w2t, w1t, side=None):
    R = x1.shape[0]

    def body(dx2_ref, x1_ref, mo_ref, r_ref, bm_ref, nw_ref, w2t_ref, w1t_ref, dx1_ref, du_ref, dob_ref, part_ref,
             acc_ref):
        for s in range(TM // SB):
            rows = slice(s * SB, (s + 1) * SB)
            dob_ref[rows, :] = (dx2_ref[rows, :] * bm_ref[s, 5:6, :]).astype(dob_ref.dtype)
        for j in range(D_FF // FF_BLK):
            cols = slice(j * FF_BLK, (j + 1) * FF_BLK)
            du = _dotg(dob_ref[...], w2t_ref[cols, :], NT) * (2.0 * r_ref[:, cols].astype(F32))
            du_ref[:, cols] = du.astype(du_ref.dtype)
            d = _dotg(du_ref[:, cols], w1t_ref[:, cols], NT)
            if j == 0:
                acc_ref[...] = d
            else:
                acc_ref[...] += d
        w = nw_ref[...]
        for s in range(TM // SB):
            rows = slice(s * SB, (s + 1) * SB)
            xhat, rstd = _rms_hat(x1_ref[rows, :])
            dh = acc_ref[rows, :]
            dx, dnw = _rms_bwd(dh * (1.0 + bm_ref[s, 4:5, :]), xhat, rstd, w)
            d2 = dx2_ref[rows, :]
            dx1_ref[rows, :] = d2 + dx
            part_ref[s] = jnp.concatenate(
                [_colsum(dh), _colsum(dh * xhat * w), _colsum(d2 * mo_ref[rows, :]), dnw,
                 jnp.zeros((4, D), F32)], axis=0)

    grid = (R // TM,)
    body, side_in, side_out, side_shapes, side_scratch, side_args = _side_wrap(body, 8, 4, 1, side, grid)
    outs = pl.pallas_call(
        body, name="mlp_bwd" if side is None else "mlp_bwd_comm", grid=grid,
        in_specs=[_rowspec(D), _rowspec(D), _rowspec(D), _rowspec(D_FF),
                  pl.BlockSpec((TM // SB, 8, D), lambda i: (i, 0, 0)), _fullspec((1, D)),
                  _resident((D_FF, D)), _resident((D, D_FF))] + side_in,
        out_specs=[_rowspec(D), _rowspec(D_FF), _rowspec(D), pl.BlockSpec((TM // SB, 8, D), lambda i: (i, 0, 0))]
                  + side_out,
        out_shape=[jax.ShapeDtypeStruct((R, D), F32), jax.ShapeDtypeStruct((R, D_FF), MXU),
                   jax.ShapeDtypeStruct((R, D), MXU), jax.ShapeDtypeStruct((R // SB, 8, D), F32)] + side_shapes,
        scratch_shapes=[pltpu.VMEM((TM, D), F32)] + side_scratch,
        compiler_params=_cp(56),
    )(dx2, x1, mo, r, bm, nw, w2t, w1t, *side_args)
    return tuple(outs[:4]) + (list(outs[4:]),)


def mm_tn(a, b, square_a=False, name="mm_tn", col_blocks=False):
    R, M = a.shape
    N = b.shape[1]
    tm = M if M <= 1408 else 1024
    tn = N if N <= 2176 else 1024
    tk = next((c for c in ((2176, 1088, 512) if tm + tn <= 2048 else (1088, 512)) if R % c == 0), R)
    assert not col_blocks or tm == M

    def body(a_ref, b_ref, o_ref):
        @pl.when(pl.program_id(2) == 0)
        def _():
            o_ref[...] = jnp.zeros_like(o_ref)

        av = a_ref[...]
        if square_a:
            av = av.astype(F32)
            av = (av * av).astype(MXU)
        prod = _dotg(av.astype(MXU), b_ref[...].astype(MXU), TN)
        if col_blocks:
            o_ref[0] += prod
        else:
            o_ref[...] += prod

    if col_blocks:
        out_spec = pl.BlockSpec((1, tm, tn), lambda i, j, k: (j, 0, 0))
        out_shape = jax.ShapeDtypeStruct((N // tn, M, tn), F32)
    else:
        out_spec = pl.BlockSpec((tm, tn), lambda i, j, k: (i, j))
        out_shape = jax.ShapeDtypeStruct((M, N), F32)
    return pl.pallas_call(
        body, name=name, grid=(M // tm, N // tn, R // tk),
        in_specs=[pl.BlockSpec((tk, tm), lambda i, j, k: (k, i)), pl.BlockSpec((tk, tn), lambda i, j, k: (k, j))],
        out_specs=out_spec, out_shape=out_shape,
        compiler_params=_cp(48),
    )(a, b)


def final_loss(x, tgt, fw, blocks_per_sample):
    R = x.shape[0]
    nxb = blocks_per_sample - 1

    def body(x_ref, t_ref, fw_ref, dx_ref, part_ref):
        i = pl.program_id(0)
        is_ctx = (i % blocks_per_sample) == 0
        xhat, rstd = _rms_hat(x_ref[...])
        w = fw_ref[...]
        err = xhat * w - t_ref[...]
        dx, dfw = _rms_bwd(err * (1.0 / D), xhat, rstd, w)
        keep = jnp.where(is_ctx, 0.0, 1.0)
        dx_ref[...] = dx * keep
        part_ref[0] = jnp.concatenate([dfw * keep, _colsum(err * err) * keep, jnp.zeros((6, D), F32)], axis=0)

    def tmap(i):
        return ((i // blocks_per_sample) * nxb + jnp.maximum(i % blocks_per_sample - 1, 0), 0)

    return pl.pallas_call(
        body, name="final_loss", grid=(R // SB,),
        in_specs=[_rowspec(D, SB), pl.BlockSpec((SB, D), tmap), _fullspec((1, D))],
        out_specs=[_rowspec(D, SB), pl.BlockSpec((1, 8, D), lambda i: (i, 0, 0))],
        out_shape=[jax.ShapeDtypeStruct((R, D), F32), jax.ShapeDtypeStruct((R // SB, 8, D), F32)],
    )(x, tgt, fw)


def _softplus(v):
    return jnp.maximum(v, 0.0) + jnp.log(1.0 + jnp.exp(-jnp.abs(v)))


def _conv_out(ext, cw_ref, cb_ref):
    return (cb_ref[...] + cw_ref[0:1, :] * _shift(ext, -1) + cw_ref[1:2, :] * _shift(ext, 0)
            + cw_ref[2:3, :] * _shift(ext, 1) + cw_ref[3:4, :] * _shift(ext, 2))


def _dt_dir(v, d):
    lane = lax.broadcasted_iota(jnp.int32, v.shape, 1)
    return jnp.where(lane < SSD_HEADS, pltpu.roll(v, (128 - DT0 - SSD_HEADS * d) % 128, axis=1), 0.0)


def ssd_prep(pxbc, plast, cw, cb, dtb, blocks_per_sample):
    R = pxbc.shape[0]
    prev, nxt = _halo_specs(XBC, R)

    def body(cur_ref, prev_ref, nxt_ref, pl_ref, cw_ref, cb_ref, dtb_ref, xs_ref, bm_ref, cm_ref, dt_ref):
        i = pl.program_id(0)
        ext = _ext_rows(cur_ref[...], prev_ref[...], nxt_ref[...], i, blocks_per_sample)
        co = _conv_out(ext, cw_ref, cb_ref)
        a = co * _sigmoid(co)
        xs_ref[...] = a[:, 0:384]
        bm_ref[...] = a[:, 384:640]
        cm_ref[...] = a[:, 640:896]
        sp = _softplus(pl_ref[...] + dtb_ref[...])
        dt_ref[0] = _dt_dir(sp, 0)
        dt_ref[1] = _dt_dir(sp, 1)

    return pl.pallas_call(
        body, name="ssd_prep", grid=(R // SB,),
        in_specs=[_rowspec(XBC, SB), prev, nxt, _rowspec(128, SB), _fullspec((8, XBC)), _fullspec((1, XBC)),
                  _fullspec((1, 128))],
        out_specs=[_rowspec(384, SB), _rowspec(256, SB), _rowspec(256, SB),
                   pl.BlockSpec((2, SB, 128), lambda i: (0, i, 0))],
        out_shape=[jax.ShapeDtypeStruct((R, 384), F32), jax.ShapeDtypeStruct((R, 256), F32),
                   jax.ShapeDtypeStruct((R, 256), F32), jax.ShapeDtypeStruct((2, R, 128), F32)],
    )(pxbc, pxbc, pxbc, plast, cw, cb, dtb)


def _chunk_index(d, s, nc):
    nctx = CTX // CHUNK
    back = jnp.where(s < nctx, nctx - 1 - s, nc + nctx - 1 - s)
    return jnp.where(d == 0, s, back)


def _scan_common(d, dt, arow, eexp, xs):
    ii = lax.broadcasted_iota(jnp.int32, (CHUNK, CHUNK), 0)
    jj = lax.broadcasted_iota(jnp.int32, (CHUNK, CHUNK), 1)
    mask = ((ii - jj) * (1 - 2 * d)) >= 0
    adt = dt * arow
    tmat = jnp.where(mask, 1.0, 0.0)
    cs = _dot_hi(tmat, adt, sel_first=True)
    tot = _colsum(adt)
    dtx = _dot_hi(dt, eexp)
    xt = xs * dtx
    ecs = jnp.exp(cs)
    ecx = _dot_hi(ecs, eexp)
    dte = jnp.exp(tot - cs)
    dtex = _dot_hi(dte, eexp)
    etot = jnp.exp(tot)
    etx = _dot_hi(jnp.broadcast_to(etot, (8, 128)), eexp)[0:1, :]
    return mask, tmat, adt, cs, tot, dtx, xt, ecs, ecx, dte, dtex, etot, etx


def _decay_matrix(mask, cs, cst, h):
    return jnp.exp(jnp.where(mask, cs[:, h:h + 1] - cst[h:h + 1, :], -1e30))


def _side_wrap(body, n_in, n_out, n_scratch, side, grid):
    if side is None:
        return body, [], [], [], [], []
    ni, no = len(side.ins), len(side.out_shapes)

    def wrapped(*refs):
        ins, refs = refs[:n_in], refs[n_in:]
        side_ins, refs = refs[:ni], refs[ni:]
        outs, refs = refs[:n_out], refs[n_out:]
        side_outs, refs = refs[:no], refs[no:]
        scratch, sems = refs[:n_scratch], refs[n_scratch:]
        ids = [pl.program_id(a) for a in range(len(grid))]
        first = functools.reduce(jnp.logical_and, [i == 0 for i in ids])
        last = functools.reduce(jnp.logical_and, [i == g - 1 for i, g in zip(ids, grid)])
        pl.when(first)(lambda: side.start(side_ins, side_outs, sems))
        body(*ins, *outs, *scratch)
        pl.when(last)(lambda: side.finish(side_ins, side_outs, sems))

    return wrapped, [ANY] * ni, [ANY] * no, list(side.out_shapes), _sems(side.nsem), list(side.ins)


def ssd_scan_fwd(xs, bm, cm, dtv, arow, eexp, nb, T, side=None):
    R = xs.shape[0]
    nc = T // CHUNK
    B = range(nb)

    def body(xs_ref, bm_ref, cm_ref, dt_ref, a_ref, e_ref, y_ref, hin_ref, st_ref):
        d = pl.program_id(0)
        s = pl.program_id(1)

        @pl.when(s == 0)
        def _():
            st_ref[...] = jnp.zeros_like(st_ref)

        eexp = e_ref[...]
        com = [_scan_common(d, dt_ref[0, b], a_ref[0, 0:1, :], eexp, xs_ref[b]) for b in B]
        mask = com[0][0]
        cs = [com[b][3] for b in B]
        cst = [cs[b].T for b in B]
        sin = [st_ref[b] for b in B]
        for b in B:
            hin_ref[0, b] = sin[b]
        sb = [sin[b].astype(MXU) for b in B]
        xtb = [com[b][6].astype(MXU) for b in B]
        xw = [(com[b][6] * com[b][10]).astype(MXU) for b in B]
        g0 = lax.broadcasted_iota(jnp.int32, (CHUNK, SSD_INNER), 1) < 192
        lane = lax.broadcasted_iota(jnp.int32, (CHUNK, 128), 1)
        c = [[cm_ref[b, :, 0:128].astype(MXU), cm_ref[b, :, 128:256].astype(MXU)] for b in B]
        bq = [[bm_ref[b, :, 0:128].astype(MXU), bm_ref[b, :, 128:256].astype(MXU)] for b in B]
        y = [jnp.where(g0, _dot(c[b][0], sb[b]), _dot(c[b][1], sb[b])) * com[b][8] for b in B]
        cb = [[_dotg(c[b][g], bq[b][g], NT) for g in range(2)] for b in B]
        blocks = [[] for _ in B]
        for blk in range(3):
            acc = [None for _ in B]
            for hh in range(2):
                h = blk * 2 + hh
                for b in B:
                    m = (cb[b][h // 3] * _decay_matrix(mask, cs[b], cst[b], h)).astype(MXU)
                    res = _dot(m, xtb[b][:, blk * 128:(blk + 1) * 128])
                    acc[b] = res if hh == 0 else jnp.where(lane < 64, acc[b], res)
            for b in B:
                blocks[b].append(acc[b])
        for b in B:
            y_ref[0, b] = y[b] + jnp.concatenate(blocks[b], axis=1)
            st_ref[b] = sin[b] * com[b][12] + jnp.where(g0, _dotg(bq[b][0], xw[b], TN), _dotg(bq[b][1], xw[b], TN))

    def rows(cols):
        return pl.BlockSpec((nb, CHUNK, cols), lambda d, s: (0, _chunk_index(d, s, nc), 0))

    def by_dir(cols):
        return pl.BlockSpec((1, nb, CHUNK, cols), lambda d, s: (d, 0, _chunk_index(d, s, nc), 0))

    grid = (2, nc)
    body, side_in, side_out, side_shapes, side_scratch, side_args = _side_wrap(body, 6, 2, 1, side, grid)
    outs = pl.pallas_call(
        body, name="ssd_scan_fwd" if side is None else "ssd_scan_fwd_comm", grid=grid,
        in_specs=[rows(384), rows(256), rows(256), by_dir(128), pl.BlockSpec((1, 8, 128), lambda d, s: (d, 0, 0)),
                  pl.BlockSpec((128, 384), lambda d, s: (0, 0))] + side_in,
        out_specs=[by_dir(384),
                   pl.BlockSpec((1, nb, CHUNK, 384), lambda d, s: (d * nc + _chunk_index(d, s, nc), 0, 0, 0))] + side_out,
        out_shape=[jax.ShapeDtypeStruct((2, nb, T, 384), F32), jax.ShapeDtypeStruct((2 * nc, nb, CHUNK, 384), F32)]
                  + side_shapes,
        scratch_shapes=[pltpu.VMEM((nb, CHUNK, 384), F32)] + side_scratch,
    )(xs.reshape(nb, T, 384), bm.reshape(nb, T, 256), cm.reshape(nb, T, 256), dtv.reshape(2, nb, T, 128), arow, eexp,
      *side_args)
    return outs[0].reshape(2, R, 384), outs[1], list(outs[2:])


def ssd_scan_bwd(xs, bm, cm, dtv, arow, eexp, hin, dy, nb, T, side=None):
    R = xs.shape[0]
    nc = T // CHUNK
    B = range(nb)

    def chunk(d, s):
        return _chunk_index(d, nc - 1 - s, nc)

    def body(xs_ref, bm_ref, cm_ref, dt_ref, a_ref, e_ref, hin_ref, dy_ref,
             dxs_ref, dbm_ref, dcm_ref, ddt_ref, da_ref, ds_ref):
        d = pl.program_id(0)
        s = pl.program_id(1)

        @pl.when(s == 0)
        def _():
            ds_ref[...] = jnp.zeros_like(ds_ref)
            da_ref[...] = jnp.zeros_like(da_ref)

        eexp = e_ref[...]
        arow = a_ref[0, 0:1, :]
        dt = [dt_ref[0, b] for b in B]
        xs_v = [xs_ref[b] for b in B]
        com = [_scan_common(d, dt[b], arow, eexp, xs_v[b]) for b in B]
        mask, tmat = com[0][0], com[0][1]
        cs, dtx, xt, ecs, ecx, dte, dtex, etot, etx = [[com[b][i] for b in B] for i in (3, 5, 6, 7, 8, 9, 10, 11, 12)]
        cst = [cs[b].T for b in B]
        sin = [hin_ref[0, b] for b in B]
        sb = [sin[b].astype(MXU) for b in B]
        dsp = [ds_ref[b] for b in B]
        dyv = [dy_ref[b] for b in B]
        xtb = [xt[b].astype(MXU) for b in B]
        xw = [(xt[b] * dtex[b]).astype(MXU) for b in B]
        g0 = lax.broadcasted_iota(jnp.int32, (CHUNK, SSD_INNER), 1) < 192
        lane = lax.broadcasted_iota(jnp.int32, (CHUNK, 128), 1)
        sub = lax.broadcasted_iota(jnp.int32, (CHUNK, 128), 0)
        c = [[cm_ref[b, :, 0:128].astype(MXU), cm_ref[b, :, 128:256].astype(MXU)] for b in B]
        bq = [[bm_ref[b, :, 0:128].astype(MXU), bm_ref[b, :, 128:256].astype(MXU)] for b in B]

        cs_prod = [jnp.where(g0, _dot(c[b][0], sb[b]), _dot(c[b][1], sb[b])) for b in B]
        dcsp = [dyv[b] * ecx[b] for b in B]
        dcsp_g = [[jnp.where(g0, dcsp[b], 0.0).astype(MXU), jnp.where(g0, 0.0, dcsp[b]).astype(MXU)] for b in B]
        dcs = [_dot_hi(dyv[b] * cs_prod[b], eexp, NT) * ecs[b] for b in B]
        dc = [[_dotg(dcsp_g[b][g], sb[b], NT) for g in range(2)] for b in B]
        dsin = [_dotg(c[b][0], dcsp_g[b][0], TN) + _dotg(c[b][1], dcsp_g[b][1], TN) + dsp[b] * etx[b] for b in B]

        dtot = [_dot_hi(jnp.broadcast_to(_colsum(dsp[b] * sin[b]), (8, SSD_INNER)), eexp, NT)[0:1, :] * etot[b] for b in B]
        dsp_g = [[jnp.where(g0, dsp[b], 0.0).astype(MXU), jnp.where(g0, 0.0, dsp[b]).astype(MXU)] for b in B]
        dxw = [_dot(bq[b][0], dsp_g[b][0]) + _dot(bq[b][1], dsp_g[b][1]) for b in B]
        db = [[_dotg(xw[b], dsp_g[b][g], NT) for g in range(2)] for b in B]
        dxt = [dxw[b] * dtex[b] for b in B]
        ddte = [_dot_hi(dxw[b] * xt[b], eexp, NT) * dte[b] for b in B]
        dtot = [dtot[b] + _colsum(ddte[b]) for b in B]
        dcs = [dcs[b] - ddte[b] for b in B]

        cb = [[_dotg(c[b][g], bq[b][g], NT) for g in range(2)] for b in B]
        dg = [[jnp.zeros((CHUNK, CHUNK), F32), jnp.zeros((CHUNK, CHUNK), F32)] for _ in B]
        dcs_rows = [jnp.zeros((CHUNK, 128), F32) for _ in B]
        dxt_blocks = [[] for _ in B]
        for blk in range(3):
            acc = [jnp.zeros((CHUNK, 128), F32) for _ in B]
            for hh in range(2):
                h = blk * 2 + hh
                g = h // 3
                mine = (lane < 64) if hh == 0 else (lane >= 64)
                for b in B:
                    dyh = jnp.where(mine, dyv[b][:, blk * 128:(blk + 1) * 128], 0.0).astype(MXU)
                    lh = _decay_matrix(mask, cs[b], cst[b], h)
                    m = cb[b][g] * lh
                    dm = _dotg(dyh, xtb[b][:, blk * 128:(blk + 1) * 128], NT)
                    acc[b] = acc[b] + _dotg(m.astype(MXU), dyh, TN)
                    dg[b][g] = dg[b][g] + dm * lh
                    q = dm * m
                    dcs[b] = dcs[b] + jnp.where(lane == h, jnp.sum(q, axis=1, keepdims=True), 0.0)
                    dcs_rows[b] = dcs_rows[b] - jnp.where(sub == h, jnp.sum(q, axis=0, keepdims=True), 0.0)
            for b in B:
                dxt_blocks[b].append(acc[b])
        for b in B:
            dxt[b] = dxt[b] + jnp.concatenate(dxt_blocks[b], axis=1)
            for g in range(2):
                dgb = dg[b][g].astype(MXU)
                dc[b][g] = dc[b][g] + _dot(dgb, bq[b][g])
                db[b][g] = db[b][g] + _dotg(dgb, c[b][g], TN)
            dcs[b] = dcs[b] + dcs_rows[b].T

        for b in B:
            dadt = _dot_hi(tmat, dcs[b], TN, sel_first=True) + dtot[b]
            ddt_ref[0, b] = dadt * arow + _dot_hi(dxt[b] * xs_v[b], eexp, NT)
            da_ref[0, b, 0:1, :] += _colsum(dadt * dt[b])
            dxs_ref[0, b] = (dxt[b] * dtx[b]).astype(dxs_ref.dtype)
            dbm_ref[0, b] = jnp.concatenate(db[b], axis=1).astype(dbm_ref.dtype)
            dcm_ref[0, b] = jnp.concatenate(dc[b], axis=1).astype(dcm_ref.dtype)
            ds_ref[b] = dsin[b]

    def rows(cols):
        return pl.BlockSpec((nb, CHUNK, cols), lambda d, s: (0, chunk(d, s), 0))

    def by_dir(cols):
        return pl.BlockSpec((1, nb, CHUNK, cols), lambda d, s: (d, 0, chunk(d, s), 0))

    grid = (2, nc)
    body, side_in, side_out, side_shapes, side_scratch, side_args = _side_wrap(body, 8, 5, 1, side, grid)
    outs = pl.pallas_call(
        body, name="ssd_scan_bwd" if side is None else "ssd_scan_bwd_comm", grid=grid,
        in_specs=[rows(384), rows(256), rows(256), by_dir(128), pl.BlockSpec((1, 8, 128), lambda d, s: (d, 0, 0)),
                  pl.BlockSpec((128, 384), lambda d, s: (0, 0)),
                  pl.BlockSpec((1, nb, CHUNK, 384), lambda d, s: (d * nc + chunk(d, s), 0, 0, 0)), rows(384)] + side_in,
        out_specs=[by_dir(384), by_dir(256), by_dir(256), by_dir(128),
                   pl.BlockSpec((1, nb, 8, 128), lambda d, s: (d, 0, 0, 0))] + side_out,
        out_shape=[jax.ShapeDtypeStruct((2, nb, T, 384), MXU), jax.ShapeDtypeStruct((2, nb, T, 256), MXU),
                   jax.ShapeDtypeStruct((2, nb, T, 256), MXU), jax.ShapeDtypeStruct((2, nb, T, 128), F32),
                   jax.ShapeDtypeStruct((2, nb, 8, 128), F32)] + side_shapes,
        scratch_shapes=[pltpu.VMEM((nb, CHUNK, 384), F32)] + side_scratch,
    )(xs.reshape(nb, T, 384), bm.reshape(nb, T, 256), cm.reshape(nb, T, 256), dtv.reshape(2, nb, T, 128), arow, eexp,
      hin, dy.reshape(nb, T, 384), *side_args)
    return (outs[0].reshape(2, R, 384), outs[1].reshape(2, R, 256), outs[2].reshape(2, R, 256),
            outs[3].reshape(2, R, 128), outs[4], list(outs[5:]))


def _group_rms(g):
    lane = lax.broadcasted_iota(jnp.int32, g.shape, 1)
    g0 = lane < 192
    gg = g * g
    s0 = jnp.sum(jnp.where(g0, gg, 0.0), axis=-1, keepdims=True)
    s1 = jnp.sum(gg, axis=-1, keepdims=True) - s0
    rstd = jnp.where(g0, lax.rsqrt(s0 * (1.0 / 192) + EPS), lax.rsqrt(s1 * (1.0 / 192) + EPS))
    return rstd, g0


def ssd_out_fwd(y2, xs, pz, dexp, nw):
    R = xs.shape[0]

    def body(y_ref, xs_ref, z_ref, d_ref, nw_ref, o_ref):
        z = z_ref[...]
        yy = y_ref[0] + y_ref[1] + xs_ref[...] * d_ref[...]
        g = yy * (z * _sigmoid(z))
        rstd, _ = _group_rms(g)
        o_ref[...] = g * rstd * nw_ref[...]

    return pl.pallas_call(
        body, name="ssd_out_fwd", grid=(R // TM,),
        in_specs=[pl.BlockSpec((2, TM, 384), lambda i: (0, i, 0)), _rowspec(384), _rowspec(384),
                  _fullspec((1, 384)), _fullspec((1, 384))],
        out_specs=_rowspec(384),
        out_shape=jax.ShapeDtypeStruct((R, 384), F32),
    )(y2, xs, pz, dexp, nw)


def ssd_out_bwd(dout, y2, xs, pz, dexp, nw):
    R = xs.shape[0]

    def body(do_ref, y_ref, xs_ref, z_ref, d_ref, nw_ref, dy_ref, dz_ref, dxs_ref, part_ref):
        z = z_ref[...]
        xs_v = xs_ref[...]
        yy = y_ref[0] + y_ref[1] + xs_v * d_ref[...]
        sig = _sigmoid(z)
        sz = z * sig
        g = yy * sz
        rstd, g0 = _group_rms(g)
        ghat = g * rstd
        do = do_ref[...]
        dgn = do * nw_ref[...]
        t = dgn * ghat
        t0 = jnp.sum(jnp.where(g0, t, 0.0), axis=-1, keepdims=True)
        t1 = jnp.sum(t, axis=-1, keepdims=True) - t0
        dg = rstd * (dgn - ghat * jnp.where(g0, t0, t1) * (1.0 / 192))
        dyy = dg * sz
        dy_ref[...] = dyy
        dz_ref[...] = (dg * yy * (sig * (1.0 + z * (1.0 - sig)))).astype(dz_ref.dtype)
        dxs_ref[...] = dyy * d_ref[...]
        part_ref[0] = jnp.concatenate([_colsum(do * ghat), _colsum(dyy * xs_v), jnp.zeros((6, 384), F32)], axis=0)

    return pl.pallas_call(
        body, name="ssd_out_bwd", grid=(R // TM,),
        in_specs=[_rowspec(384), pl.BlockSpec((2, TM, 384), lambda i: (0, i, 0)), _rowspec(384), _rowspec(384),
                  _fullspec((1, 384)), _fullspec((1, 384))],
        out_specs=[_rowspec(384), _rowspec(384), _rowspec(384), pl.BlockSpec((1, 8, 384), lambda i: (i, 0, 0))],
        out_shape=[jax.ShapeDtypeStruct((R, 384), F32), jax.ShapeDtypeStruct((R, 384), MXU),
                   jax.ShapeDtypeStruct((R, 384), F32), jax.ShapeDtypeStruct((R // TM, 8, 384), F32)],
    )(dout, y2, xs, pz, dexp, nw)


def ssd_prep_bwd_a(pxbc, plast, cw, cb, dtb, dxs_skip, dxs2, dbm2, dcm2, ddt2, blocks_per_sample):
    R = pxbc.shape[0]
    prev, nxt = _halo_specs(XBC, R)

    def body(cur_ref, prev_ref, nxt_ref, pl_ref, cw_ref, cb_ref, dtb_ref, dsk_ref, dxs_ref, dbm_ref, dcm_ref, ddt_ref,
             dpre_ref, dlast_ref, part_ref):
        i = pl.program_id(0)
        ext = _ext_rows(cur_ref[...], prev_ref[...], nxt_ref[...], i, blocks_per_sample)
        co = _conv_out(ext, cw_ref, cb_ref)
        sig = _sigmoid(co)
        both = lambda ref: ref[0].astype(F32) + ref[1].astype(F32)
        up = jnp.concatenate([dsk_ref[...] + both(dxs_ref), both(dbm_ref), both(dcm_ref)], axis=1)
        dpre = up * (sig * (1.0 + co * (1.0 - sig)))
        dpre_ref[...] = dpre
        raw = pl_ref[...] + dtb_ref[...]
        lane = lax.broadcasted_iota(jnp.int32, raw.shape, 1)
        ddt = (pltpu.roll(ddt_ref[0], DT0, axis=1) + pltpu.roll(ddt_ref[1], DT0 + SSD_HEADS, axis=1))
        ddt = jnp.where(jnp.logical_and(lane >= DT0, lane < DT0 + 2 * SSD_HEADS), ddt * _sigmoid(raw), 0.0)
        dlast_ref[...] = ddt.astype(dlast_ref.dtype)
        rows = [_colsum(dpre * _shift(ext, k - 1)) for k in range(4)]
        rows.append(_colsum(dpre))
        rows.append(jnp.concatenate([_colsum(ddt), jnp.zeros((1, XBC - 128), F32)], axis=1))
        rows.append(jnp.zeros((2, XBC), F32))
        part_ref[0] = jnp.concatenate(rows, axis=0)

    dirspec = lambda n: pl.BlockSpec((2, SB, n), lambda i: (0, i, 0))
    return pl.pallas_call(
        body, name="ssd_prep_bwd_a", grid=(R // SB,),
        in_specs=[_rowspec(XBC, SB), prev, nxt, _rowspec(128, SB), _fullspec((8, XBC)), _fullspec((1, XBC)),
                  _fullspec((1, 128)), _rowspec(384, SB), dirspec(384), dirspec(256), dirspec(256), dirspec(128)],
        out_specs=[_rowspec(XBC, SB), _rowspec(128, SB), pl.BlockSpec((1, 8, XBC), lambda i: (i, 0, 0))],
        out_shape=[jax.ShapeDtypeStruct((R, XBC), F32), jax.ShapeDtypeStruct((R, 128), MXU),
                   jax.ShapeDtypeStruct((R // SB, 8, XBC), F32)],
    )(pxbc, pxbc, pxbc, plast, cw, cb, dtb, dxs_skip, dxs2, dbm2, dcm2, ddt2)


def ssd_prep_bwd_b(dpre, cw, blocks_per_sample):
    R = dpre.shape[0]
    prev, nxt = _halo_specs(XBC, R)

    def body(cur_ref, prev_ref, nxt_ref, cw_ref, o_ref):
        i = pl.program_id(0)
        ext = _ext_rows(cur_ref[...], prev_ref[...], nxt_ref[...], i, blocks_per_sample)
        o_ref[...] = (cw_ref[0:1, :] * _shift(ext, 1) + cw_ref[1:2, :] * _shift(ext, 0)
                      + cw_ref[2:3, :] * _shift(ext, -1) + cw_ref[3:4, :] * _shift(ext, -2)).astype(o_ref.dtype)

    return pl.pallas_call(
        body, name="ssd_prep_bwd_b", grid=(R // SB,),
        in_specs=[_rowspec(XBC, SB), prev, nxt, _fullspec((8, XBC))],
        out_specs=_rowspec(XBC, SB),
        out_shape=jax.ShapeDtypeStruct((R, XBC), MXU),
    )(dpre, dpre, dpre, cw)


def _rope(u, cos, sa, sb):
    return u * cos + pltpu.roll(u, 120, axis=1) * sa + pltpu.roll(u, 8, axis=1) * sb


def _rope_t(du, cos, sa, sb):
    return du * cos + pltpu.roll(du * sa, 8, axis=1) + pltpu.roll(du * sb, 120, axis=1)


def mla_prep(pqa, pkva, plast, qnw, kvnw, wq, wk, wv, cos, sa, sb):
    R = pqa.shape[0]

    def body(qa_ref, kva_ref, pl_ref, qnw_ref, kvnw_ref, wq_ref, wk_ref, wv_ref, cos_ref, sa_ref, sb_ref,
             q_ref, k_ref, v_ref, cq_ref, ckv_ref):
        cos_v, sa_v, sb_v = cos_ref[...], sa_ref[...], sb_ref[...]
        xq, _ = _rms_hat(qa_ref[...])
        cq_ref[...] = (xq * qnw_ref[...]).astype(cq_ref.dtype)
        xkv, _ = _rms_hat(kva_ref[...])
        ckv_ref[...] = (xkv * kvnw_ref[...]).astype(ckv_ref.dtype)
        q = _dot(cq_ref[...], wq_ref[...])
        kn = _dot(ckv_ref[...], wk_ref[...])
        v_ref[...] = _dot(ckv_ref[...], wv_ref[...]).astype(v_ref.dtype)
        lane = lax.broadcasted_iota(jnp.int32, (TM, HP), 1)
        rope_lanes = jnp.logical_and(lane >= QK_NOPE, lane < QK_DIM)
        kr = _rope(jnp.where(rope_lanes, pltpu.roll(pl_ref[...], QK_NOPE, axis=1), 0.0), cos_v, sa_v, sb_v)
        for h in range(MLA_HEADS):
            cols = slice(h * HP, (h + 1) * HP)
            q_ref[:, cols] = (_rope(q[:, cols], cos_v, sa_v, sb_v) * Q_SCALE).astype(q_ref.dtype)
            k_ref[:, cols] = (kn[:, cols] + kr).astype(k_ref.dtype)

    return pl.pallas_call(
        body, name="mla_prep", grid=(R // TM,),
        in_specs=[_rowspec(256), _rowspec(256), _rowspec(128), _fullspec((1, 256)), _fullspec((1, 256)),
                  _fullspec((256, QW)), _fullspec((256, QW)), _fullspec((256, QW)),
                  _rowspec(HP), _rowspec(HP), _rowspec(HP)],
        out_specs=[_rowspec(QW), _rowspec(QW), _rowspec(QW), _rowspec(256), _rowspec(256)],
        out_shape=[jax.ShapeDtypeStruct((R, QW), MXU)] * 3 + [jax.ShapeDtypeStruct((R, 256), MXU)] * 2,
    )(pqa, pkva, plast, qnw, kvnw, wq, wk, wv, cos, sa, sb)


def mla_prep_bwd(dq, dk, dv, pqa, pkva, qnw, kvnw, wqt, wkt, wvt, cos, sa, sb):
    R = pqa.shape[0]

    def body(dq_ref, dk_ref, dv_ref, qa_ref, kva_ref, qnw_ref, kvnw_ref, wqt_ref, wkt_ref, wvt_ref,
             cos_ref, sa_ref, sb_ref, dqa_ref, dkva_ref, dkr_ref, dql_ref, dkm_ref, dvb_ref, part_ref):
        cos_v, sa_v, sb_v = cos_ref[...], sa_ref[...], sb_ref[...]
        lane = lax.broadcasted_iota(jnp.int32, (TM, HP), 1)
        rope_lanes = jnp.logical_and(lane >= QK_NOPE, lane < QK_DIM)
        dkr = jnp.zeros((TM, HP), F32)
        for h in range(MLA_HEADS):
            cols = slice(h * HP, (h + 1) * HP)
            dql_ref[:, cols] = (_rope_t(dq_ref[:, cols], cos_v, sa_v, sb_v) * ATT_SCALE).astype(dql_ref.dtype)
            dkh = dk_ref[:, cols] * LN2
            dkm_ref[:, cols] = jnp.where(lane < QK_NOPE, dkh, 0.0).astype(dkm_ref.dtype)
            dkr = dkr + jnp.where(rope_lanes, dkh, 0.0)
        dvb_ref[...] = dv_ref[...].astype(dvb_ref.dtype)
        dkr = jnp.where(rope_lanes, _rope_t(dkr, cos_v, sa_v, sb_v), 0.0)
        dkr_ref[...] = pltpu.roll(dkr, HP - QK_NOPE, axis=1).astype(dkr_ref.dtype)
        xq, rq = _rms_hat(qa_ref[...])
        dqa, dqnw = _rms_bwd(_dotg(dql_ref[...], wqt_ref[...], NT), xq, rq, qnw_ref[...])
        dqa_ref[...] = dqa.astype(dqa_ref.dtype)
        xkv, rkv = _rms_hat(kva_ref[...])
        dckv = _dotg(dkm_ref[...], wkt_ref[...], NT) + _dotg(dvb_ref[...], wvt_ref[...], NT)
        dkva, dkvnw = _rms_bwd(dckv, xkv, rkv, kvnw_ref[...])
        dkva_ref[...] = dkva.astype(dkva_ref.dtype)
        part_ref[0] = jnp.concatenate([dqnw, dkvnw, jnp.zeros((6, 256), F32)], axis=0)

    return pl.pallas_call(
        body, name="mla_prep_bwd", grid=(R // TM,),
        in_specs=[_rowspec(QW), _rowspec(QW), _rowspec(QW), _rowspec(256), _rowspec(256), _fullspec((1, 256)),
                  _fullspec((1, 256)), _fullspec((256, QW)), _fullspec((256, QW)), _fullspec((256, QW)),
                  _rowspec(HP), _rowspec(HP), _rowspec(HP)],
        out_specs=[_rowspec(256), _rowspec(256), _rowspec(128), _rowspec(QW), _rowspec(QW), _rowspec(QW),
                   pl.BlockSpec((1, 8, 256), lambda i: (i, 0, 0))],
        out_shape=[jax.ShapeDtypeStruct((R, 256), MXU), jax.ShapeDtypeStruct((R, 256), MXU),
                   jax.ShapeDtypeStruct((R, 128), MXU)] + [jax.ShapeDtypeStruct((R, QW), MXU)] * 3
                  + [jax.ShapeDtypeStruct((R // TM, 8, 256), F32)],
    )(dq, dk, dv, pqa, pkva, qnw, kvnw, wqt, wkt, wvt, cos, sa, sb)


ATT_SCALE = QK_DIM ** -0.5
TQ = 256


LOG2E = 1.4426950408889634
LN2 = 0.6931471805599453
Q_SCALE = ATT_SCALE * LOG2E


def _key_chunks(T, n=2):
    unit = 256 if T % 256 == 0 else 128
    units = T // unit
    sizes = [(units // n + (1 if i < units % n else 0)) * unit for i in range(n)]
    return [(sum(sizes[:i]), sz) for i, sz in enumerate(sizes) if sz]


def attn_fwd(q, k, v, nb, T):
    R = q.shape[0]
    nq = T // TQ
    chunks = _key_chunks(T, 4)
    HEADS = range(2)

    def body(q_ref, k_ref, v_ref, o_ref, lse_ref):
        def lanes(h):
            return slice(h * HP, (h + 1) * HP)

        def logits(h, lo, n):
            return _dotg(q_ref[:, lanes(h)], k_ref[lo:lo + n, lanes(h)], NT)

        def weigh(h, s, lo, n):
            m = jnp.max(s, axis=-1, keepdims=True)
            p = jnp.exp2(s - m)
            return m, jnp.sum(p, axis=-1, keepdims=True), _dot(p.astype(MXU), v_ref[lo:lo + n, lanes(h)])

        def parts_of(ranges):
            out = [[] for _ in HEADS]
            s = [logits(h, *ranges[0]) for h in HEADS]
            for j, (lo, n) in enumerate(ranges):
                nxt = [logits(h, *ranges[j + 1]) for h in HEADS] if j + 1 < len(ranges) else None
                for h in HEADS:
                    out[h].append(weigh(h, s[h], lo, n))
                s = nxt
            return out

        def finish(all_parts):
            for h, parts in enumerate(all_parts):
                m = parts[0][0]
                for pm, _, _ in parts[1:]:
                    m = jnp.maximum(m, pm)
                l, o = 0.0, 0.0
                for pm, pl_, po in parts:
                    a = jnp.exp2(pm - m)
                    l = l + a * pl_
                    o = o + a * po
                o_ref[:, lanes(h)] = o / l
                lse_ref[:, lanes(h)] = jnp.broadcast_to(m + jnp.log(l) * LOG2E, (TQ, HP))

        i = pl.program_id(2)
        pl.when(i == 0)(lambda: finish(parts_of([(0, CTX)])))
        pl.when(i > 0)(lambda: finish(parts_of(chunks)))

    qspec = pl.BlockSpec((TQ, 2 * HP), lambda b, h, i: (b * nq + i, h))
    kspec = pl.BlockSpec((T, 2 * HP), lambda b, h, i: (b, h))
    return pl.pallas_call(
        body, name="attn_fwd", grid=(nb, MLA_HEADS // 2, nq),
        in_specs=[qspec, kspec, kspec], out_specs=[qspec, qspec],
        out_shape=[jax.ShapeDtypeStruct((R, QW), F32)] * 2,
        compiler_params=_cp(48),
    )(q, k, v)


def attn_bwd(q, k, v, o, lse, do, nb, T):
    R = q.shape[0]
    nq = T // TQ
    chunks = _key_chunks(T)

    def body(q_ref, k_ref, v_ref, o_ref, lse_ref, do_ref, dq_ref, dk_ref, dv_ref):
        i = pl.program_id(2)

        @pl.when(i == 0)
        def _():
            dk_ref[...] = jnp.zeros_like(dk_ref)
            dv_ref[...] = jnp.zeros_like(dv_ref)

        def run(chunks):
            qv = q_ref[...]
            dov = do_ref[...]
            dob = dov.astype(MXU)
            delta = jnp.sum(dov * o_ref[...], axis=-1, keepdims=True)
            lse_v = lse_ref[:, 0:1]
            dq = 0.0
            for lo, n in chunks:
                kv = k_ref[lo:lo + n, :]
                p = jnp.exp2(_dotg(qv, kv, NT) - lse_v)
                dp = _dotg(dob, v_ref[lo:lo + n, :], NT)
                dsb = (p * (dp - delta)).astype(MXU)
                dq = dq + _dot(dsb, kv)
                dk_ref[lo:lo + n, :] += _dotg(dsb, qv, TN)
                dv_ref[lo:lo + n, :] += _dotg(p.astype(MXU), dob, TN)
            dq_ref[...] = dq

        pl.when(i == 0)(lambda: run([(0, CTX)]))
        pl.when(i > 0)(lambda: run(chunks))

    qspec = pl.BlockSpec((TQ, HP), lambda b, h, i: (b * nq + i, h))
    kspec = pl.BlockSpec((T, HP), lambda b, h, i: (b, h))
    return pl.pallas_call(
        body, name="attn_bwd", grid=(nb, MLA_HEADS, nq),
        in_specs=[qspec, kspec, kspec, qspec, qspec, qspec],
        out_specs=[qspec, kspec, kspec],
        out_shape=[jax.ShapeDtypeStruct((R, QW), F32)] * 3,
        compiler_params=_cp(56),
    )(q, k, v, o, lse, do)


def _pool_geometry(i, blocks_per_sample, seq):
    j = i % blocks_per_sample
    n = jnp.where(j == 0, CTX, seq)
    t0 = jnp.where(j == 0, 0, (j - 1) * SB) - HALO
    lane = lax.broadcasted_iota(jnp.int32, (SB + 2 * HALO, POOL_DIM), 1)
    t = lax.broadcasted_iota(jnp.int32, (SB + 2 * HALO, POOL_DIM), 0) + t0
    wh = jnp.where(lane < 64, 1, jnp.where(lane < 128, 2, jnp.where(lane < 192, 4, 8)))
    cnt = jnp.minimum(t + wh, n) - jnp.maximum(t - wh, 0)
    return lane, 1.0 / jnp.maximum(cnt, 1).astype(F32)


def _by_window(lane, c2, c4, c8, c16):
    return jnp.where(lane < 64, c2, jnp.where(lane < 128, c4, jnp.where(lane < 192, c8, c16)))


def _window_sums(ext, lane, first):
    n = ext.shape[0]
    r = lambda a, s: pltpu.roll(a, s % n, axis=0)
    c2 = ext + r(ext, first)
    c4 = r(c2, 1) + r(c2, -1)
    c8 = r(c4, 2) + r(c4, -2)
    c16 = r(c8, 4) + r(c8, -4)
    return _by_window(lane, c2, c4, c8, c16)


def _pool_delta(ext, lane, inv):
    return (_window_sums(ext, lane, 1) * inv - ext)[HALO:HALO + SB, :]


def pool_fwd(ppool, wbd, scale, blocks_per_sample, seq):
    R = ppool.shape[0]
    prev, nxt = _halo_specs(POOL_DIM, R)

    def body(cur_ref, prev_ref, nxt_ref, w_ref, s_ref, o_ref):
        i = pl.program_id(0)
        ext = _ext_rows(cur_ref[...], prev_ref[...], nxt_ref[...], i, blocks_per_sample)
        lane, inv = _pool_geometry(i, blocks_per_sample, seq)
        dlt = _pool_delta(ext, lane, inv)
        o_ref[...] = _dot(dlt.astype(MXU), w_ref[...]) * s_ref[...]

    return pl.pallas_call(
        body, name="pool_fwd", grid=(R // SB,),
        in_specs=[_rowspec(POOL_DIM, SB), prev, nxt, _fullspec((POOL_DIM, POOL_DIM)), _fullspec((1, POOL_DIM))],
        out_specs=_rowspec(POOL_DIM, SB),
        out_shape=jax.ShapeDtypeStruct((R, POOL_DIM), F32),
    )(ppool, ppool, ppool, wbd, scale)


def pool_bwd(ppool, dpool, wbd, scale, blocks_per_sample, seq):
    R = ppool.shape[0]
    prev, nxt = _halo_specs(POOL_DIM, R)

    def body(cur_ref, prev_ref, nxt_ref, dcur_ref, dprev_ref, dnxt_ref, w_ref, s_ref, du_ref, dw_ref, part_ref):
        i = pl.program_id(0)

        @pl.when(i == 0)
        def _():
            dw_ref[...] = jnp.zeros_like(dw_ref)

        ext = _ext_rows(cur_ref[...], prev_ref[...], nxt_ref[...], i, blocks_per_sample)
        lane, inv = _pool_geometry(i, blocks_per_sample, seq)
        dlt = _pool_delta(ext, lane, inv).astype(MXU)
        dy = dcur_ref[...]
        part_ref[0] = jnp.concatenate([_colsum(dy * _dot(dlt, w_ref[...])), jnp.zeros((7, POOL_DIM), F32)], axis=0)
        dyp = (dy * s_ref[...]).astype(MXU)
        dw_ref[...] += _dotg(dlt, dyp, TN)
        dext = _ext_rows(dy, dprev_ref[...], dnxt_ref[...], i, blocks_per_sample)
        dd = _dotg((dext * s_ref[...]).astype(MXU), w_ref[...], NT)
        du_ref[...] = (_window_sums(dd * inv, lane, -1) - dd)[HALO:HALO + SB, :].astype(du_ref.dtype)

    return pl.pallas_call(
        body, name="pool_bwd", grid=(R // SB,),
        in_specs=[_rowspec(POOL_DIM, SB), prev, nxt, _rowspec(POOL_DIM, SB), prev, nxt,
                  _fullspec((POOL_DIM, POOL_DIM)), _fullspec((1, POOL_DIM))],
        out_specs=[_rowspec(POOL_DIM, SB), _fullspec((POOL_DIM, POOL_DIM)),
                   pl.BlockSpec((1, 8, POOL_DIM), lambda i: (i, 0, 0))],
        out_shape=[jax.ShapeDtypeStruct((R, POOL_DIM), MXU), jax.ShapeDtypeStruct((POOL_DIM, POOL_DIM), F32),
                   jax.ShapeDtypeStruct((R // SB, 8, POOL_DIM), F32)],
    )(ppool, ppool, ppool, dpool, dpool, dpool, wbd, scale)


def adamw(w, g, m, v, name="adamw"):
    rows, cols = w.shape
    tr = rows
    for cand in (512, 256, 128, 64, 32, 16, 8):
        if rows % cand == 0:
            tr = cand
            break
    bc1 = 1.0 - ADAM_B1 ** ADAM_STEP
    bc2 = 1.0 - ADAM_B2 ** ADAM_STEP

    def body(w_ref, g_ref, m_ref, v_ref, d_ref, nm_ref, nv_ref):
        g_v = g_ref[...]
        nm = ADAM_B1 * m_ref[...] + (1.0 - ADAM_B1) * g_v
        nv = ADAM_B2 * v_ref[...] + (1.0 - ADAM_B2) * (g_v * g_v)
        nm_ref[...] = nm
        nv_ref[...] = nv
        d_ref[...] = -ADAM_LR * ((nm / bc1) / (jnp.sqrt(nv / bc2) + ADAM_EPS) + ADAM_WD * w_ref[...])

    spec = pl.BlockSpec((tr, cols), lambda i: (i, 0))
    return pl.pallas_call(
        body, name=name, grid=(rows // tr,),
        in_specs=[spec] * 4, out_specs=[spec] * 3,
        out_shape=[jax.ShapeDtypeStruct((rows, cols), F32)] * 3,
    )(w, g, m, v)


MODR = 32


def _silu(v):
    return v * _sigmoid(v)


def mod_fwd(cond, w, b):
    n = w.shape[1]

    def body(c_ref, w_ref, b_ref, o_ref):
        o_ref[...] = _dot(_silu(c_ref[...]).astype(MXU), w_ref[...].astype(MXU)) + b_ref[...]

    return pl.pallas_call(
        body, name="mod_fwd", out_shape=jax.ShapeDtypeStruct((MODR, n), F32),
        in_specs=[_fullspec((MODR, D)), _fullspec((D, n)), _fullspec((1, n))], out_specs=_fullspec((MODR, n)),
        grid=(1,), compiler_params=_cp(40),
    )(cond, w, b)


def mod_wgrad(cond, dm):
    n = dm.shape[1]

    def body(c_ref, d_ref, o_ref):
        o_ref[...] = _dotg(_silu(c_ref[...]).astype(MXU), d_ref[...].astype(MXU), TN)

    return pl.pallas_call(
        body, name="mod_wgrad", out_shape=jax.ShapeDtypeStruct((D, n), F32),
        in_specs=[_fullspec((MODR, D)), _fullspec((MODR, n))], out_specs=_fullspec((D, n)),
        grid=(1,), compiler_params=_cp(40),
    )(cond, dm)


def mod_dgrad(dm, w):
    n = w.shape[1]

    def body(d_ref, w_ref, o_ref):
        o_ref[...] = _dotg(d_ref[...].astype(MXU), w_ref[...].astype(MXU), NT)

    return pl.pallas_call(
        body, name="mod_dgrad", out_shape=jax.ShapeDtypeStruct((8, D), F32),
        in_specs=[_fullspec((8, n)), _fullspec((D, n))], out_specs=_fullspec((8, D)),
        grid=(1,), compiler_params=_cp(40),
    )(dm, w)


def sum_leading(a, name="sum_leading"):
    n, r, c = a.shape

    def body(a_ref, o_ref):
        acc = a_ref[0]
        for k in range(1, n):
            acc = acc + a_ref[k]
        o_ref[...] = acc

    return pl.pallas_call(
        body, name=name, out_shape=jax.ShapeDtypeStruct((r, c), F32),
        in_specs=[_fullspec((n, r, c))], out_specs=_fullspec((r, c)), grid=(1,),
    )(a)


MESH = pl.DeviceIdType.MESH
NDEV = 8
ANY = pl.BlockSpec(memory_space=pl.ANY)


def _place():
    return lax.axis_index("x"), lax.axis_index("y"), lax.axis_index("c")


def _other_chips(x, y):
    return [(1 - x, y), (x, 1 - y), (1 - x, 1 - y)]


def allgather_small(v, name):
    r, cols = v.shape

    def body(v_ref, o_ref, send_sems, recv_sems):
        x, y, c = _place()
        me = 4 * x + 2 * y + c
        o_ref[me] = v_ref[...]
        copies = []
        for rel in range(1, NDEV):
            peer = (1 - x if rel & 4 else x, 1 - y if rel & 2 else y, 1 - c if rel & 1 else c)
            cp = pltpu.make_async_remote_copy(src_ref=v_ref, dst_ref=o_ref.at[me], send_sem=send_sems.at[rel - 1],
                                              recv_sem=recv_sems.at[rel - 1], device_id=peer, device_id_type=MESH)
            cp.start()
            copies.append(cp)
        for cp in copies:
            cp.wait_recv()
        for cp in copies:
            cp.wait_send()

    return pl.pallas_call(
        body, name=name, out_shape=jax.ShapeDtypeStruct((NDEV, r, cols), F32),
        in_specs=[pl.BlockSpec(memory_space=pltpu.VMEM)], out_specs=pl.BlockSpec(memory_space=pltpu.VMEM),
        scratch_shapes=[pltpu.SemaphoreType.DMA((NDEV - 1,)), pltpu.SemaphoreType.DMA((NDEV - 1,))],
        compiler_params=_cp(40),
    )(v)


def _sems(n):
    return [pltpu.SemaphoreType.DMA((n,)), pltpu.SemaphoreType.DMA((n,))]


def gather_job(arrs):
    n = len(arrs)

    def copy(srcs, outs, sems, i, slot, kk, cc, to, from_src=False):
        hr = arrs[i].shape[0] // 2
        dst = outs[i].at[kk, pl.ds(cc * hr, hr), :]
        return pltpu.make_async_remote_copy(src_ref=srcs[i].at[pl.ds(cc * hr, hr), :] if from_src else dst, dst_ref=dst,
                                            send_sem=sems[0].at[slot * n + i], recv_sem=sems[1].at[slot * n + i],
                                            device_id=to, device_id_type=MESH)

    def start(srcs, outs, sems):
        x, y, c = _place()
        for j, (px, py) in enumerate(_other_chips(x, y)):
            for i in range(n):
                copy(srcs, outs, sems, i, j, 2 * x + y, c, (px, py, c), True).start()

    def finish(srcs, outs, sems):
        x, y, c = _place()
        sib = (x, y, 1 - c)
        chips = _other_chips(x, y)
        passed = []
        for j, (px, py) in enumerate(chips):
            for i in range(n):
                copy(srcs, outs, sems, i, j, 2 * px + py, c, (px, py, c)).wait_recv()
                cp = copy(srcs, outs, sems, i, 3 + j, 2 * px + py, c, sib)
                cp.start()
                passed.append(cp)
        for j, (px, py) in enumerate(chips):
            for i in range(n):
                copy(srcs, outs, sems, i, 3 + j, 2 * px + py, 1 - c, sib).wait_recv()
        for j, (px, py) in enumerate(chips):
            for i in range(n):
                copy(srcs, outs, sems, i, j, 2 * x + y, c, (px, py, c), True).wait_send()
        for cp in passed:
            cp.wait_send()

    return _NS(ins=list(arrs), out_shapes=[jax.ShapeDtypeStruct((4,) + a.shape, a.dtype) for a in arrs], nsem=6 * n,
               start=start, finish=finish)


def chip_swap_job(ss):
    n = len(ss)

    def copies(srcs, outs, sems):
        x, y, c = _place()
        return [pltpu.make_async_remote_copy(src_ref=srcs[i].at[2 * px + py], dst_ref=outs[i].at[j],
                                             send_sem=sems[0].at[j * n + i], recv_sem=sems[1].at[j * n + i],
                                             device_id=(px, py, c), device_id_type=MESH)
                for j, (px, py) in enumerate(_other_chips(x, y)) for i in range(n)]

    def start(srcs, outs, sems):
        for cp in copies(srcs, outs, sems):
            cp.start()

    def finish(srcs, outs, sems):
        for cp in copies(srcs, outs, sems):
            cp.wait()

    return _NS(ins=list(ss), out_shapes=[jax.ShapeDtypeStruct((3,) + s.shape[1:], s.dtype) for s in ss], nsem=3 * n,
               start=start, finish=finish)


def run_job(job, name):
    n, m = len(job.ins), len(job.out_shapes)

    def body(*refs):
        srcs, outs, sems = refs[:n], refs[n:n + m], refs[n + m:]
        job.start(srcs, outs, sems)
        job.finish(srcs, outs, sems)

    return pl.pallas_call(body, name=name, out_shape=job.out_shapes, in_specs=[ANY] * n, out_specs=[ANY] * m,
                          scratch_shapes=_sems(job.nsem))(*job.ins)


def swap_core_halves(gs):
    n = len(gs)

    def body(*refs):
        srcs, outs = refs[:n], refs[n:2 * n]
        send_sems, recv_sems = refs[2 * n:]
        x, y, c = _place()
        copies = []
        for i in range(n):
            hr = gs[i].shape[1] // 2
            cp = pltpu.make_async_remote_copy(src_ref=srcs[i].at[:, pl.ds((1 - c) * hr, hr), :], dst_ref=outs[i],
                                              send_sem=send_sems.at[i], recv_sem=recv_sems.at[i],
                                              device_id=(x, y, 1 - c), device_id_type=MESH)
            cp.start()
            copies.append(cp)
        for cp in copies:
            cp.wait()

    return pl.pallas_call(
        body, name="swap_core_halves",
        out_shape=[jax.ShapeDtypeStruct((4, g.shape[1] // 2, g.shape[2]), g.dtype) for g in gs],
        in_specs=[ANY] * n, out_specs=[ANY] * n, scratch_shapes=_sems(n),
    )(*gs)


def add_half(g, r1, cidx, name):
    _, rows, cols = g.shape
    hr = rows // 2

    def body(c_ref, g_ref, r_ref, o_ref, ob_ref):
        s = g_ref[...] + r_ref[...]
        o_ref[...] = s
        ob_ref[...] = s.astype(BF16)

    blk = lambda f: pl.BlockSpec((1, hr, cols), f)
    return pl.pallas_call(
        body, name=name,
        out_shape=[jax.ShapeDtypeStruct((4, hr, cols), F32), jax.ShapeDtypeStruct((4, hr, cols), BF16)],
        grid_spec=pltpu.PrefetchScalarGridSpec(
            num_scalar_prefetch=1, grid=(4,),
            in_specs=[blk(lambda k, c_ref: (k, c_ref[0], 0)), blk(lambda k, c_ref: (k, 0, 0))],
            out_specs=[blk(lambda k, c_ref: (k, 0, 0)), blk(lambda k, c_ref: (k, 0, 0))]),
    )(cidx, g, r1)


def sum_parts(s1, r2, kidx, name):
    _, hr, cols = s1.shape

    def body(k_ref, s_ref, r_ref, o_ref):
        o_ref[...] = ((s_ref[0] + r_ref[0].astype(F32)) + r_ref[1].astype(F32)) + r_ref[2].astype(F32)

    return pl.pallas_call(
        body, name=name, out_shape=jax.ShapeDtypeStruct((hr, cols), F32),
        grid_spec=pltpu.PrefetchScalarGridSpec(
            num_scalar_prefetch=1, grid=(1,),
            in_specs=[pl.BlockSpec((1, hr, cols), lambda i, k_ref: (k_ref[0], 0, 0)),
                      pl.BlockSpec((3, hr, cols), lambda i, k_ref: (0, 0, 0))],
            out_specs=pl.BlockSpec((hr, cols), lambda i, k_ref: (0, 0))),
    )(kidx, s1, r2)


def swap_reduced_halves(hs):
    n = len(hs)

    def body(*refs):
        srcs, outs = refs[:n], refs[n:2 * n]
        send_sems, recv_sems = refs[2 * n:]
        x, y, c = _place()
        copies = []
        for i in range(n):
            cp = pltpu.make_async_remote_copy(src_ref=srcs[i], dst_ref=outs[i], send_sem=send_sems.at[i],
                                              recv_sem=recv_sems.at[i], device_id=(x, y, 1 - c), device_id_type=MESH)
            cp.start()
            copies.append(cp)
        for cp in copies:
            cp.wait()

    return pl.pallas_call(
        body, name="swap_reduced_halves", out_shape=[jax.ShapeDtypeStruct(h.shape, h.dtype) for h in hs],
        in_specs=[ANY] * n, out_specs=[ANY] * n, scratch_shapes=_sems(n),
    )(*hs)


def adamw_halves(w, m, v, own, oth, cidx, name):
    depth, rows, cols = w.shape
    hr = rows // 2
    tr = min(hr, 256)
    nblk = hr // tr
    bc1 = 1.0 - ADAM_B1 ** ADAM_STEP
    bc2 = 1.0 - ADAM_B2 ** ADAM_STEP

    def body(c_ref, w_ref, m_ref, v_ref, own0, own1, oth0, oth1, g_ref, d_ref, nm_ref, nv_ref):
        l = pl.program_id(0)
        hi = pl.program_id(1)
        mine = jnp.where(l == 0, own0[...], own1[...])
        other = jnp.where(l == 0, oth0[...], oth1[...])
        g_v = jnp.where(hi == c_ref[0], mine, other)
        nm = ADAM_B1 * m_ref[0] + (1.0 - ADAM_B1) * g_v
        nv = ADAM_B2 * v_ref[0] + (1.0 - ADAM_B2) * (g_v * g_v)
        g_ref[0] = g_v
        nm_ref[0] = nm
        nv_ref[0] = nv
        d_ref[0] = -ADAM_LR * ((nm / bc1) / (jnp.sqrt(nv / bc2) + ADAM_EPS) + ADAM_WD * w_ref[0])

    wspec = pl.BlockSpec((1, tr, cols), lambda l, hi, b, c_ref: (l, hi * nblk + b, 0))
    gspec = pl.BlockSpec((tr, cols), lambda l, hi, b, c_ref: (b, 0))
    assert depth == 2
    return pl.pallas_call(
        body, name=name, out_shape=[jax.ShapeDtypeStruct(w.shape, F32)] * 4,
        grid_spec=pltpu.PrefetchScalarGridSpec(
            num_scalar_prefetch=1, grid=(depth, 2, nblk),
            in_specs=[wspec] * 3 + [gspec] * 4, out_specs=[wspec] * 4),
    )(cidx, w, m, v, own[0], own[1], oth[0], oth[1])


class _NS:
    def __init__(self, **kw):
        self.__dict__.update(kw)


def _prep_in(win, conv_w, conv_b, dt_bias, a_log, ssd_d, ssd_nw, qnw, kvnw, pool_w, pool_scale, n1, n2):
    winp = jnp.concatenate([win[:, 0:384], win[:, 384:1280], win[:, 1292:1548], win[:, 1548:1804], win[:, 1836:2092],
                            win[:, 1804:1836], win[:, 1280:1292], jnp.zeros((D, NP - IN_COLS), win.dtype)], axis=1)
    wbd = (jnp.eye(4, dtype=F32)[:, None, :, None] * pool_w[:, :, None, :]).reshape(POOL_DIM, POOL_DIM).astype(MXU)
    a = -jnp.exp(a_log)
    return _NS(
        winp=winp, wbd=wbd,
        cw8=jnp.pad(conv_w, ((0, 4), (0, 0))), cb=conv_b[None],
        dtb=jnp.pad(dt_bias.reshape(1, 12), ((0, 0), (DT0, 128 - DT0 - 12))),
        arow=jnp.pad(a[:, None, :], ((0, 0), (0, 7), (0, 128 - SSD_HEADS))), a=a,
        dexp=jnp.repeat(ssd_d, SSD_P)[None], ssd_nw=ssd_nw[None], qnw=qnw[None], kvnw=kvnw[None],
        pscale=pool_scale[None], n1=n1[None], n2=n2[None])


def _prep_rest(wqb, wkvb, wout, w1, w2):
    wq = jnp.pad(wqb.reshape(256, MLA_HEADS, QK_DIM), ((0, 0), (0, 0), (0, HP - QK_DIM))).reshape(256, QW)
    kv3 = wkvb.reshape(256, MLA_HEADS, 128)
    wk = jnp.pad(kv3[:, :, :64], ((0, 0), (0, 0), (0, 64))).reshape(256, QW)
    wv = jnp.pad(kv3[:, :, 64:], ((0, 0), (0, 0), (0, 64))).reshape(256, QW)
    wo = jnp.concatenate([jnp.pad(wout[384:768].reshape(MLA_HEADS, 64, D), ((0, 0), (0, 64), (0, 0))).reshape(QW, D),
                          wout[0:384], wout[768:1024]], axis=0)
    return _NS(wq=wq, wk=wk, wv=wv, wo=wo, w1=w1, w2=w2)


def _prep_layer(win, wqb, wkvb, wout, w1, w2, *small):
    lw = _prep_in(win, *small)
    lw.__dict__.update(_prep_rest(wqb, wkvb, wout, w1, w2).__dict__)
    return lw


def _by_chip_cols(a):
    return jnp.stack([a[:, k * (a.shape[1] // 4):(k + 1) * (a.shape[1] // 4)] for k in range(4)])


def _by_chip_rows(a):
    return a.reshape(4, a.shape[0] // 4, a.shape[1])


def _unprep_in(dwinp):
    return jnp.concatenate([dwinp[:, 0:384], dwinp[:, 384:1280], dwinp[:, 2080:2092], dwinp[:, 1280:1536],
                            dwinp[:, 1536:1792], dwinp[:, 2048:2080], dwinp[:, 1792:2048]], axis=1)


def _unprep_rest(dwq, dwk, dwv, dwo):
    dwqb = dwq.reshape(256, MLA_HEADS, HP)[:, :, :QK_DIM].reshape(256, MLA_HEADS * QK_DIM)
    dwkvb = jnp.concatenate([dwk.reshape(256, MLA_HEADS, HP)[:, :, :64], dwv.reshape(256, MLA_HEADS, HP)[:, :, :64]],
                            axis=2).reshape(256, MLA_HEADS * 128)
    dwout = jnp.concatenate([dwo[QW:QW + 384], dwo[0:QW].reshape(MLA_HEADS, HP, D)[:, :64].reshape(384, D),
                             dwo[QW + 384:CAT]], axis=0)
    return dwqb, dwkvb, dwout


def _rope_tables(nb, N):
    t = jnp.arange(N, dtype=F32)
    row = jnp.floor(t / GRID_W)
    col = t - row * GRID_W
    inv = jnp.asarray(10000.0 ** (-np.arange(8, dtype=np.float32) / 8), F32)
    ang = jnp.stack([row[:, None] * inv, col[:, None] * inv], axis=1)
    cs, sn = jnp.cos(ang), jnp.sin(ang)
    zero = jnp.zeros_like(sn)
    lanes = lambda first, second: jnp.stack([first, second], axis=2).reshape(N, 32)
    pad = lambda a, fill: jnp.concatenate([jnp.full((N, 64), fill, F32), a, jnp.full((N, 32), fill, F32)], axis=1)
    tabs = []
    for tab, fill in ((pad(lanes(cs, cs), 1.0), 1.0), (pad(lanes(-sn, zero), 0.0), 0.0), (pad(lanes(zero, sn), 0.0), 0.0)):
        one = jnp.concatenate([jnp.full((CTX, 128), fill, F32), tab], axis=0)
        tabs.append(jnp.tile(one, (nb, 1)))
    return tabs


def _eexp():
    e = np.zeros((128, SSD_INNER), np.float32)
    for h in range(SSD_HEADS):
        e[h, h * SSD_P:(h + 1) * SSD_P] = 1.0
    return jnp.asarray(e)


class _NoHooks:
    def __init__(self, lws):
        self.lws = lws

    def weights_in(self, l):
        return _NS(**self.lws[l].__dict__)

    def weights_rest(self, l, scan_out):
        return self.lws[l]

    def job(self, where, l, early=None):
        return None

    def done(self, where, l, out):
        pass

    def layer_grads(self, l, g):
        pass


def _layer_fwd(X, bm, l, cst, hooks):
    nb, T, bps, N = cst.nb, cst.T, cst.bps, cst.N
    lw = hooks.weights_in(l)
    h1, pz, pxbc, pqa, pkva, ppool, plast = in_proj(X, bm, lw.n1, lw.winp)
    xs, bmat, cmat, dtv = ssd_prep(pxbc, plast, lw.cw8, lw.cb, lw.dtb, bps)
    y2, hin, out = ssd_scan_fwd(xs, bmat, cmat, dtv, lw.arow, cst.eexp, nb, T, hooks.job("fwd_scan", l))
    lw.__dict__.update(hooks.weights_rest(l, out).__dict__)
    ssd = ssd_out_fwd(y2, xs, pz, lw.dexp, lw.ssd_nw)
    q, k, v, cq, ckv = mla_prep(pqa, pkva, plast, lw.qnw, lw.kvnw, lw.wq, lw.wk, lw.wv, *cst.rope)
    attn, lse = attn_fwd(q, k, v, nb, T)
    pool = pool_fwd(ppool, lw.wbd, lw.pscale, bps, N)
    x1, mix, cat = mix_fwd(X, attn, ssd, pool, bm, lw.wo)
    x2, mo, r, h2, out = mlp_fwd(x1, bm, lw.n2, lw.w1, lw.w2, hooks.job("fwd_mlp", l))
    hooks.done("fwd_mlp", l, out)
    sv = _NS(X=X, h1=h1, pz=pz, pxbc=pxbc, pqa=pqa, pkva=pkva, ppool=ppool, plast=plast, xs=xs, bmat=bmat, cmat=cmat,
             dtv=dtv, y2=y2, hin=hin, q=q, k=k, v=v, cq=cq, ckv=ckv, attn=attn, lse=lse, x1=x1, mix=mix, cat=cat, mo=mo, r=r,
             h2=h2, lw=lw)
    return x2, sv


def _layer_bwd(dx2, bm, l, sv, cst, hooks):
    nb, T, bps, N = cst.nb, cst.T, cst.bps, cst.N
    lw = sv.lw
    dx1, du, dob, part_mlp, out = mlp_bwd(dx2, sv.x1, sv.mo, sv.r, bm, lw.n2, lw.w2, lw.w1, hooks.job("bwd_mlp", l))
    hooks.done("bwd_mlp", l, out)
    dw1 = mm_tn(sv.h2, du, name="wgrad_mlp1", col_blocks=True)
    dw2 = mm_tn(sv.r, dob, square_a=True, name="wgrad_mlp2")
    dattn, dssd, dpool, dmb, part_mix = mix_bwd(dx1, sv.mix, bm, lw.wo)
    dwo = mm_tn(sv.cat, dmb, name="wgrad_out")
    dppool, dwbd, part_pool = pool_bwd(sv.ppool, dpool, lw.wbd, lw.pscale, bps, N)
    dq, dk, dv = attn_bwd(sv.q, sv.k, sv.v, sv.attn, sv.lse, dattn, nb, T)
    dpqa, dpkva, dkr, dql, dkm, dvb, part_mla = mla_prep_bwd(dq, dk, dv, sv.pqa, sv.pkva, lw.qnw, lw.kvnw, lw.wq,
                                                             lw.wk, lw.wv, *cst.rope)
    dwq = mm_tn(sv.cq, dql, name="wgrad_q")
    dwk = mm_tn(sv.ckv, dkm, name="wgrad_k")
    dwv = mm_tn(sv.ckv, dvb, name="wgrad_v")
    dwqb, dwkvb, dwout = _unprep_rest(dwq, dwk, dwv, dwo)
    early = dict(w_q_b=_by_chip_cols(dwqb), w_kv_b=_by_chip_cols(dwkvb), w_out=_by_chip_rows(dwout), w_mlp1=dw1,
                 w_mlp2=_by_chip_rows(dw2))
    dyy, dz, dxs_skip, part_so = ssd_out_bwd(dssd, sv.y2, sv.xs, sv.pz, lw.dexp, lw.ssd_nw)
    dxs2, dbm2, dcm2, ddt2, da, out = ssd_scan_bwd(sv.xs, sv.bmat, sv.cmat, sv.dtv, lw.arow, cst.eexp, sv.hin, dyy,
                                                   nb, T, hooks.job("bwd_scan", l, early))
    hooks.done("bwd_scan", l, out)
    dpre, dlast_dt, part_conv = ssd_prep_bwd_a(sv.pxbc, sv.plast, lw.cw8, lw.cb, lw.dtb, dxs_skip, dxs2, dbm2, dcm2,
                                               ddt2, bps)
    dpxbc = ssd_prep_bwd_b(dpre, lw.cw8, bps)
    dx, dpb, part_in = in_proj_bwd(dx1, sv.X, dz, dpxbc, dpqa, dpkva, dppool, dkr, dlast_dt, bm, lw.n1, lw.winp)
    dwinp = mm_tn(sv.h1, dpb, name="wgrad_in")

    dmod = jnp.stack([part_in[:, 0], part_in[:, 1], part_mix[:, 0], part_mlp[:, 0], part_mlp[:, 1], part_mlp[:, 2]],
                     axis=1)
    dmod = dmod.reshape(nb, bps, 6, D)
    dm_rows = jnp.concatenate([jnp.sum(dmod[:, 1:], axis=1), jnp.sum(dmod[:, 0], axis=0)[None]], axis=0)
    da_dh = jnp.sum(da[:, :, 0, :SSD_HEADS], axis=1)
    conv_parts = jnp.sum(part_conv, axis=0)
    g = _NS(
        w_in=_by_chip_cols(_unprep_in(dwinp)), dm_rows=dm_rows.reshape(3, 6 * D), **early,
        norm1_w=jnp.sum(part_in[:, 2], axis=0), norm2_w=jnp.sum(part_mlp[:, 3], axis=0),
        conv_w=conv_parts[0:4], conv_b=conv_parts[4],
        dt_bias=conv_parts[5, DT0:DT0 + 12].reshape(2, SSD_HEADS), a_log=da_dh * lw.a,
        ssd_d=jnp.sum(jnp.sum(part_so[:, 1], axis=0).reshape(SSD_HEADS, SSD_P), axis=1),
        ssd_norm_w=jnp.sum(part_so[:, 0], axis=0),
        q_a_norm_w=jnp.sum(part_mla[:, 0], axis=0), kv_a_norm_w=jnp.sum(part_mla[:, 1], axis=0),
        pool_w=jnp.stack([dwbd[i * 64:(i + 1) * 64, i * 64:(i + 1) * 64] for i in range(4)]),
        pool_scale=jnp.sum(part_pool[:, 0], axis=0))
    hooks.layer_grads(l, g)
    return dx, g


def _local_step(x, ctx, tgt, bms, lws, fw, cst, hooks=None):
    nb, N = x.shape[0], x.shape[1]
    R = nb * cst.T
    hooks = _NoHooks(lws) if hooks is None else hooks
    X = jnp.concatenate([ctx, x], axis=1).reshape(R, D)
    saved = []
    for l in range(DEPTH):
        X, sv = _layer_fwd(X, bms[l], l, cst, hooks)
        saved.append(sv)
    dX, part_fin = final_loss(X, tgt.reshape(nb * N, D), fw[None], cst.bps)
    loss = (0.5 / D) * jnp.sum(part_fin[:, 1])
    dfw = jnp.sum(part_fin[:, 0], axis=0)
    grads = [None] * DEPTH
    for l in reversed(range(DEPTH)):
        dX, grads[l] = _layer_bwd(dX, bms[l], l, saved[l], cst, hooks)
    grad_x = dX.reshape(nb, cst.T, D)[:, CTX:, :]
    return loss, grad_x, grads, dfw


def _consts(nb, N):
    T = CTX + N
    bps = T // SB
    return _NS(nb=nb, N=N, T=T, bps=bps, eexp=_eexp(), rope=_rope_tables(nb, N))


def _block_mod(modrows, cst):
    rows = []
    for b in range(cst.nb):
        rows.append(modrows[cst.nb:cst.nb + 1])
        rows.append(jnp.broadcast_to(modrows[b:b + 1], (cst.bps - 1, 6, D)))
    return jnp.pad(jnp.concatenate(rows, axis=0), ((0, 0), (0, 2), (0, 0)))


SMALL = (("norm1_w", (2, D)), ("norm2_w", (2, D)), ("conv_w", (2, 4, XBC)), ("conv_b", (2, XBC)),
         ("dt_bias", (2, 2, 6)), ("a_log", (2, 2, 6)), ("ssd_d", (2, 6)), ("ssd_norm_w", (2, 384)),
         ("q_a_norm_w", (2, 256)), ("kv_a_norm_w", (2, 256)), ("pool_w", (2, 4, 64, 64)), ("pool_scale", (2, 256)),
         ("final_norm_w", (D,)), ("mod_b", (2, 6 * D)))
SMALL_ROWS = 64
DM_ROWS = 48


def _pack_small(vals):
    flat = jnp.concatenate([vals[n].reshape(-1) for n, _ in SMALL])
    return jnp.pad(flat, (0, SMALL_ROWS * D - flat.shape[0])).reshape(SMALL_ROWS, D)


def _unpack_small(p):
    flat = p.reshape(-1)
    out, off = {}, 0
    for n, shp in SMALL:
        size = int(np.prod(shp))
        out[n] = flat[off:off + size].reshape(shp)
        off += size
    return out


def cctx_grad(parts, c_ctx):
    def body(p_ref, c_ref, o_ref):
        acc = ((p_ref[0] + p_ref[1]) + p_ref[2]) + p_ref[3]
        v = c_ref[...]
        sig = _sigmoid(v)
        o_ref[...] = acc * (sig * (1.0 + v * (1.0 - sig)))

    return pl.pallas_call(
        body, name="cctx_grad", out_shape=jax.ShapeDtypeStruct((8, D), F32),
        in_specs=[_fullspec((4, 8, D)), _fullspec((1, D))], out_specs=_fullspec((8, D)), grid=(1,),
    )(parts, c_ctx)


def kernel(x, c, ctx, c_ctx, mod_w, mod_b, norm1_w, norm2_w, w_in, conv_w, conv_b, dt_bias, a_log, ssd_d, ssd_norm_w, q_a_norm_w, w_q_b, kv_a_norm_w, w_kv_b, pool_w, pool_scale, w_out, w_mlp1, w_mlp2, final_norm_w, loss_target, m_c_ctx, m_mod_w, m_mod_b, m_norm1_w, m_norm2_w, m_w_in, m_conv_w, m_conv_b, m_dt_bias, m_a_log, m_ssd_d, m_ssd_norm_w, m_q_a_norm_w, m_w_q_b, m_kv_a_norm_w, m_w_kv_b, m_pool_w, m_pool_scale, m_w_out, m_w_mlp1, m_w_mlp2, m_final_norm_w, v_c_ctx, v_mod_w, v_mod_b, v_norm1_w, v_norm2_w, v_w_in, v_conv_w, v_conv_b, v_dt_bias, v_a_log, v_ssd_d, v_ssd_norm_w, v_q_a_norm_w, v_w_q_b, v_kv_a_norm_w, v_w_kv_b, v_pool_w, v_pool_scale, v_w_out, v_w_mlp1, v_w_mlp2, v_final_norm_w):
    nb, N = x.shape[0], x.shape[1]
    cst = _consts(nb, N)
    xi, yi, ci = _place()
    me = 4 * xi + 2 * yi + ci
    kchip = 2 * xi + yi
    mcols = mod_w.shape[2]
    cshard = conv_w.shape[2]

    blk = jnp.zeros((16, D), F32).at[0:nb].set(c).at[8:16, 0:cshard].set(conv_w.reshape(8, cshard))
    g1 = allgather_small(blk, "gather_cond")
    cond = jnp.concatenate([g1[:, 0:nb].reshape(NDEV * nb, D), c_ctx[None],
                            jnp.zeros((MODR - NDEV * nb - 1, D), F32)], axis=0)
    conv_full = [jnp.concatenate([g1[2 * k, 8 + 4 * l:12 + 4 * l, 0:cshard] for k in range(4)], axis=1)
                 for l in range(DEPTH)]

    mb = [lax.dynamic_slice_in_dim(mod_b[l], kchip * mcols, mcols)[None] for l in range(DEPTH)]
    ms = jnp.concatenate([mod_fwd(cond, mod_w[l], mb[l]) for l in range(DEPTH)], axis=0)
    g2 = allgather_small(ms, "gather_mod")
    bms = []
    for l in range(DEPTH):
        m_all = jnp.concatenate([g2[2 * k, MODR * l:MODR * (l + 1)] for k in range(4)], axis=1)
        mine = jnp.concatenate([lax.dynamic_slice_in_dim(m_all, nb * me, nb), m_all[NDEV * nb:NDEV * nb + 1]], axis=0)
        bms.append(_block_mod(mine.reshape(nb + 1, 6, D), cst))

    assert DEPTH == 2
    big = (w_in, w_q_b, w_kv_b, w_out, w_mlp1, w_mlp2)
    names = ("w_in", "w_q_b", "w_kv_b", "w_out", "w_mlp1", "w_mlp2")
    concat_axis = dict(w_in=1, w_q_b=1, w_kv_b=1, w_out=0, w_mlp1=1, w_mlp2=0)
    cidx = jnp.reshape(ci, (1,)).astype(jnp.int32)
    kidx = jnp.reshape(kchip, (1,)).astype(jnp.int32)
    shards = [{n: a[l].astype(MXU) for n, a in zip(names, big)} for l in range(DEPTH)]

    def core_sums(gs):
        ns = list(gs)
        got = swap_core_halves([gs[n] for n in ns])
        return {n: add_half(gs[n], r, cidx, "add_half_" + n) for n, r in zip(ns, got)}

    class Hooks:
        gathered = [dict(w_in=run_job(gather_job([shards[0]["w_in"]]), "gather_w_in")[0]), {}]
        core_sum = [{}, {}]
        received = [{}, {}]

        def whole(self, l, n):
            return jnp.concatenate([jnp.where(kchip == k, shards[l][n], self.gathered[l][n][k]) for k in range(4)],
                                   axis=concat_axis[n])

        def weights_in(self, l):
            return _prep_in(self.whole(l, "w_in"), conv_full[l], conv_b[l], dt_bias[l], a_log[l], ssd_d[l], ssd_norm_w[l],
                            q_a_norm_w[l], kv_a_norm_w[l], pool_w[l], pool_scale[l], norm1_w[l], norm2_w[l])

        def weights_rest(self, l, scan_out):
            if l == 0:
                self.gathered[0].update(zip(names[1:], scan_out))
            return _prep_rest(*[self.whole(l, n) for n in names[1:]])

        def job(self, where, l, early=None):
            if l != 0:
                return None
            if where == "fwd_scan":
                return gather_job([shards[0][n] for n in names[1:]])
            if where == "fwd_mlp":
                return gather_job([shards[1][n] for n in names])
            if where == "bwd_mlp":
                return chip_swap_job([self.core_sum[1][n][1] for n in names])
            self.core_sum[0].update(core_sums(early))
            return chip_swap_job([self.core_sum[0][n][1] for n in names[1:]])

        def done(self, where, l, out):
            if l != 0:
                return
            if where == "fwd_mlp":
                self.gathered[1].update(zip(names, out))
            elif where == "bwd_mlp":
                self.received[1].update(zip(names, out))
            elif where == "bwd_scan":
                self.received[0].update(zip(names[1:], out))

        def layer_grads(self, l, g):
            if l == 1:
                self.core_sum[1] = core_sums({n: getattr(g, n) for n in names})
            else:
                self.core_sum[0].update(core_sums(dict(w_in=g.w_in)))
                self.received[0]["w_in"] = run_job(chip_swap_job([self.core_sum[0]["w_in"][1]]), "swap_w_in")[0]

    hooks = Hooks()
    loss_part, grad_x, grads, dfw = _local_step(x, ctx, loss_target, bms, None, final_norm_w, cst, hooks)
    loss = lax.psum(loss_part, ("x", "y", "c"))
    g_own = [sum_parts(hooks.core_sum[l][n][0], hooks.received[l][n], kidx, "sum_parts_" + n)
             for n in names for l in range(DEPTH)]
    g_oth = swap_reduced_halves(g_own)

    small = {n: jnp.stack([getattr(grads[l], n) for l in range(DEPTH)]) for n, _ in SMALL if n not in ("final_norm_w", "mod_b")}
    small["final_norm_w"] = dfw
    small["mod_b"] = jnp.stack([jnp.sum(grads[l].dm_rows, axis=0) for l in range(DEPTH)])
    dm = jnp.pad(jnp.concatenate([grads[l].dm_rows for l in range(DEPTH)], axis=0), ((0, 8 - 3 * DEPTH), (0, 0)))
    g3 = allgather_small(jnp.concatenate([_pack_small(small), dm.reshape(DM_ROWS, D)], axis=0), "gather_small")
    tot = sum_leading(g3, "sum_small")
    gsmall = _unpack_small(tot[0:SMALL_ROWS])
    ctx_sum = tot[SMALL_ROWS:].reshape(8, 6 * D)
    dm_dev = g3[:, SMALL_ROWS:].reshape(NDEV, 8, 6 * D)
    g_mod_w, dpart = [], jnp.zeros((8, D), F32)
    for l in range(DEPTH):
        dm_all = jnp.concatenate([dm_dev[:, 3 * l:3 * l + nb].reshape(NDEV * nb, 6 * D), ctx_sum[3 * l + nb:3 * l + nb + 1],
                                  jnp.zeros((MODR - NDEV * nb - 1, 6 * D), F32)], axis=0)
        g_mod_w.append(mod_wgrad(cond, lax.dynamic_slice_in_dim(dm_all, kchip * mcols, mcols, axis=1)))
        dctx = jnp.pad(lax.dynamic_slice_in_dim(ctx_sum[3 * l + nb:3 * l + nb + 1], kchip * mcols, mcols, axis=1), ((0, 7), (0, 0)))
        dpart = dpart + mod_dgrad(dctx, mod_w[l])
    g4 = allgather_small(dpart, "gather_cctx")
    g_c_ctx = cctx_grad(g4[0::2], c_ctx[None])[0]

    res = {}
    moments = ((m_w_in, v_w_in), (m_w_q_b, v_w_q_b), (m_w_kv_b, v_w_kv_b), (m_w_out, v_w_out), (m_w_mlp1, v_w_mlp1),
               (m_w_mlp2, v_w_mlp2))
    for i, (n, w, (m, v)) in enumerate(zip(names, big, moments)):
        res[n] = tuple(adamw_halves(w, m, v, g_own[DEPTH * i:DEPTH * (i + 1)], g_oth[DEPTH * i:DEPTH * (i + 1)], cidx,
                                    "adamw_" + n))
    g_mw = jnp.stack(g_mod_w)
    r_mw = adamw(mod_w.reshape(-1, mcols), g_mw.reshape(-1, mcols), m_mod_w.reshape(-1, mcols),
                 v_mod_w.reshape(-1, mcols), name="adamw_mod_w")
    res["mod_w"] = (g_mw,) + tuple(a.reshape(mod_w.shape) for a in r_mw)

    given = dict(norm1_w=(norm1_w, m_norm1_w, v_norm1_w), norm2_w=(norm2_w, m_norm2_w, v_norm2_w),
                 conv_b=(conv_b, m_conv_b, v_conv_b), dt_bias=(dt_bias, m_dt_bias, v_dt_bias),
                 a_log=(a_log, m_a_log, v_a_log), ssd_d=(ssd_d, m_ssd_d, v_ssd_d),
                 ssd_norm_w=(ssd_norm_w, m_ssd_norm_w, v_ssd_norm_w), q_a_norm_w=(q_a_norm_w, m_q_a_norm_w, v_q_a_norm_w),
                 kv_a_norm_w=(kv_a_norm_w, m_kv_a_norm_w, v_kv_a_norm_w), pool_w=(pool_w, m_pool_w, v_pool_w),
                 pool_scale=(pool_scale, m_pool_scale, v_pool_scale),
                 final_norm_w=(final_norm_w, m_final_norm_w, v_final_norm_w), mod_b=(mod_b, m_mod_b, v_mod_b))
    zero_cw = jnp.zeros((2, 4, XBC), F32)
    packs = [_pack_small({n: (given[n][i] if n in given else zero_cw) for n, _ in SMALL}) for i in range(3)]
    r_small = [_unpack_small(a) for a in adamw(packs[0], tot[0:SMALL_ROWS], packs[1], packs[2], name="adamw_small")]
    for n in given:
        res[n] = (gsmall[n], r_small[0][n], r_small[1][n], r_small[2][n])

    g_cw = lax.dynamic_slice_in_dim(gsmall["conv_w"], kchip * cshard, cshard, axis=2)
    padcw = lambda a: jnp.pad(a.reshape(8, cshard), ((0, 0), (0, 256 - cshard)))
    r_cw = adamw(padcw(conv_w), padcw(g_cw), padcw(m_conv_w), padcw(v_conv_w), name="adamw_conv_w")
    res["conv_w"] = (g_cw,) + tuple(a[:, 0:cshard].reshape(conv_w.shape) for a in r_cw)
    r_cc = adamw(c_ctx.reshape(8, 128), g_c_ctx.reshape(8, 128), m_c_ctx.reshape(8, 128), v_c_ctx.reshape(8, 128),
                 name="adamw_c_ctx")
    res["c_ctx"] = (g_c_ctx,) + tuple(a.reshape(D) for a in r_cc)

    order = ("c_ctx", "mod_w", "mod_b", "norm1_w", "norm2_w", "w_in", "conv_w", "conv_b", "dt_bias", "a_log", "ssd_d",
             "ssd_norm_w", "q_a_norm_w", "w_q_b", "kv_a_norm_w", "w_kv_b", "pool_w", "pool_scale", "w_out", "w_mlp1",
             "w_mlp2", "final_norm_w")
    return (loss, grad_x) + tuple(res[n][i] for i in range(4) for n in order)
```

```python
import functools
import math

import numpy as np
import jax
import jax.numpy as jnp
from jax import lax
from jax.experimental import pallas as pl
from jax.experimental.pallas import tpu as pltpu

F32 = jnp.float32
BF16 = jnp.bfloat16
MXU = jnp.bfloat16

D = 1024
DEPTH = 2
GRID_W = 64
CTX = 256
EPS = 1e-6
SSD_HEADS = 6
SSD_P = 64
SSD_INNER = 384
SSD_N = 128
CHUNK = 128
XBC = 896
MLA_HEADS = 6
QK_NOPE = 64
QK_ROPE = 32
QK_DIM = 96
HP = 128
QW = MLA_HEADS * HP
POOL_DIM = 256
D_FF = 4096
FF_BLK = 1024
IN_COLS = 2092
NP = 2176
P_SPLITS = (384, 896, 256, 256, 256, 128)
DT0 = 32
CAT = QW + SSD_INNER + POOL_DIM

SB = 256
TM = 512
HALO = 8

ADAM_LR = 0.001
ADAM_B1 = 0.9
ADAM_B2 = 0.999
ADAM_EPS = 1e-08
ADAM_WD = 0.01
ADAM_STEP = 10

NT = (((1,), (1,)), ((), ()))
TN = (((0,), (0,)), ((), ()))


def _cp(vmem_mb=None):
    if vmem_mb is None:
        return pltpu.CompilerParams()
    return pltpu.CompilerParams(vmem_limit_bytes=vmem_mb << 20)


def _dot(a, b):
    return jnp.dot(a, b, preferred_element_type=F32)


def _dotg(a, b, dims):
    return lax.dot_general(a, b, dims, preferred_element_type=F32)


def _dot_hi(a, b, dims=None, sel_first=False):
    dims = (((1,), (0,)), ((), ())) if dims is None else dims
    v, s = (b, a) if sel_first else (a, b)
    hi = v.astype(BF16)
    lo = (v - hi.astype(F32)).astype(BF16)
    s = s.astype(BF16)
    if sel_first:
        return _dotg(s, hi, dims) + _dotg(s, lo, dims)
    return _dotg(hi, s, dims) + _dotg(lo, s, dims)


def _rms_hat(x):
    rstd = lax.rsqrt(jnp.mean(x * x, axis=-1, keepdims=True) + EPS)
    return x * rstd, rstd


def _rms_bwd(dn, xhat, rstd, w):
    dxhat = dn * w
    dx = rstd * (dxhat - xhat * jnp.mean(dxhat * xhat, axis=-1, keepdims=True))
    return dx, jnp.sum(dn * xhat, axis=0, keepdims=True)


def _sigmoid(z):
    return 1.0 / (1.0 + jnp.exp(-z))


def _colsum(a):
    return jnp.sum(a, axis=0, keepdims=True)


def _rowspec(cols, tm=TM):
    return pl.BlockSpec((tm, cols), lambda i: (i, 0))


def _fullspec(shape):
    n = len(shape)
    return pl.BlockSpec(shape, lambda *_: (0,) * n)


def _resident(shape):
    n = len(shape)
    return pl.BlockSpec(shape, lambda *_: (0,) * n, pipeline_mode=pl.Buffered(1))


def _halo_specs(cols, nrows):
    per = SB // HALO
    last = nrows // HALO - 1
    prev = pl.BlockSpec((HALO, cols), lambda i: (jnp.maximum(i * per - 1, 0), 0))
    nxt = pl.BlockSpec((HALO, cols), lambda i: (jnp.minimum((i + 1) * per, last), 0))
    return prev, nxt


def _ext_rows(cur, prev, nxt, i, blocks_per_sample):
    j = i % blocks_per_sample
    first = jnp.logical_or(j == 0, j == 1)
    last = jnp.logical_or(j == 0, j == blocks_per_sample - 1)
    p = jnp.where(first, 0.0, prev)
    n = jnp.where(last, 0.0, nxt)
    return jnp.concatenate([p, cur, n], axis=0)


def _shift(ext, s):
    n = ext.shape[0]
    return pltpu.roll(ext, (-s) % n, axis=0)[HALO:HALO + SB, :]


def in_proj(x, bm, nw, w):
    R = x.shape[0]

    def body(x_ref, bm_ref, nw_ref, w_ref, h_ref, *outs):
        for s in range(TM // SB):
            rows = slice(s * SB, (s + 1) * SB)
            xhat, _ = _rms_hat(x_ref[rows, :])
            h = xhat * nw_ref[...] * (1.0 + bm_ref[s, 1:2, :]) + bm_ref[s, 0:1, :]
            h_ref[rows, :] = h.astype(h_ref.dtype)
        p = _dot(h_ref[...], w_ref[...])
        off = 0
        for o, n in zip(outs, P_SPLITS):
            o[...] = p[:, off:off + n]
            off += n

    return pl.pallas_call(
        body, name="in_proj", grid=(R // TM,),
        in_specs=[_rowspec(D), pl.BlockSpec((TM // SB, 8, D), lambda i: (i, 0, 0)), _fullspec((1, D)),
                  _fullspec((D, NP))],
        out_specs=[_rowspec(D)] + [_rowspec(n) for n in P_SPLITS],
        out_shape=[jax.ShapeDtypeStruct((R, D), MXU)] + [jax.ShapeDtypeStruct((R, n), F32) for n in P_SPLITS],
        compiler_params=_cp(56),
    )(x, bm, nw, w)


def in_proj_bwd(dx1, x, h, dz, dxbc, dqa, dkva, dpool, dkr, ddt, bm, nw, wt):
    R = x.shape[0]

    def body(dx1_ref, x_ref, h_ref, dz_ref, dxbc_ref, dqa_ref, dkva_ref, dpool_ref, dkr_ref, ddt_ref, bm_ref, nw_ref,
             wt_ref, dx_ref, dw_ref, part_ref, dp_ref):
        @pl.when(pl.program_id(0) == 0)
        def _():
            dw_ref[...] = jnp.zeros_like(dw_ref)

        dp_ref[:, 0:384] = dz_ref[...].astype(dp_ref.dtype)
        dp_ref[:, 384:1280] = dxbc_ref[...].astype(dp_ref.dtype)
        dp_ref[:, 1280:1536] = dqa_ref[...].astype(dp_ref.dtype)
        dp_ref[:, 1536:1792] = dkva_ref[...].astype(dp_ref.dtype)
        dp_ref[:, 1792:2048] = dpool_ref[...].astype(dp_ref.dtype)
        dp_ref[:, 2048:2176] = (dkr_ref[...] + ddt_ref[...]).astype(dp_ref.dtype)
        dw_ref[...] += _dotg(h_ref[...], dp_ref[...], TN)
        dh = _dotg(dp_ref[...], wt_ref[...], NT)
        w = nw_ref[...]
        for s in range(TM // SB):
            rows = slice(s * SB, (s + 1) * SB)
            xhat, rstd = _rms_hat(x_ref[rows, :])
            dhs = dh[rows, :]
            sc1 = 1.0 + bm_ref[s, 1:2, :]
            dx, dnw = _rms_bwd(dhs * sc1, xhat, rstd, w)
            dx_ref[rows, :] = dx1_ref[rows, :] + dx
            part_ref[s] = jnp.concatenate(
                [_colsum(dhs), _colsum(dhs * xhat * w), dnw, jnp.zeros((5, D), F32)], axis=0)

    return pl.pallas_call(
        body, name="in_proj_bwd", grid=(R // TM,),
        in_specs=[_rowspec(D), _rowspec(D), _rowspec(D), _rowspec(384), _rowspec(896), _rowspec(256), _rowspec(256),
                  _rowspec(256), _rowspec(128), _rowspec(128),
                  pl.BlockSpec((TM // SB, 8, D), lambda i: (i, 0, 0)), _fullspec((1, D)), _resident((D, NP))],
        out_specs=[_rowspec(D), _fullspec((D, NP)), pl.BlockSpec((TM // SB, 8, D), lambda i: (i, 0, 0))],
        out_shape=[jax.ShapeDtypeStruct((R, D), F32), jax.ShapeDtypeStruct((D, NP), F32),
                   jax.ShapeDtypeStruct((R // SB, 8, D), F32)],
        scratch_shapes=[pltpu.VMEM((TM, NP), MXU)],
        compiler_params=_cp(56),
    )(dx1, x, h, dz, dxbc, dqa, dkva, dpool, dkr, ddt, bm, nw, wt)


def mix_fwd(x, attn, ssd, pool, bm, wo):
    R = x.shape[0]

    def body(x_ref, a_ref, s_ref, p_ref, bm_ref, wo_ref, x1_ref, mix_ref, cat_ref):
        cat_ref[:, 0:QW] = a_ref[...].astype(cat_ref.dtype)
        cat_ref[:, QW:QW + SSD_INNER] = s_ref[...].astype(cat_ref.dtype)
        cat_ref[:, QW + SSD_INNER:CAT] = p_ref[...].astype(cat_ref.dtype)
        mix = _dot(cat_ref[...], wo_ref[...])
        mix_ref[...] = mix
        for s in range(TM // SB):
            rows = slice(s * SB, (s + 1) * SB)
            x1_ref[rows, :] = x_ref[rows, :] + bm_ref[s, 2:3, :] * mix[rows, :]

    return pl.pallas_call(
        body, name="mix_fwd", grid=(R // TM,),
        in_specs=[_rowspec(D), _rowspec(QW), _rowspec(SSD_INNER), _rowspec(POOL_DIM),
                  pl.BlockSpec((TM // SB, 8, D), lambda i: (i, 0, 0)), _fullspec((CAT, D))],
        out_specs=[_rowspec(D), _rowspec(D), _rowspec(CAT)],
        out_shape=[jax.ShapeDtypeStruct((R, D), F32), jax.ShapeDtypeStruct((R, D), F32),
                   jax.ShapeDtypeStruct((R, CAT), MXU)],
        compiler_params=_cp(48),
    )(x, attn, ssd, pool, bm, wo)


def mix_bwd(dx1, mix, cat, bm, wot):
    R = dx1.shape[0]

    def body(dx1_ref, mix_ref, cat_ref, bm_ref, wot_ref, da_ref, ds_ref, dpl_ref, dw_ref, part_ref, dmb_ref):
        @pl.when(pl.program_id(0) == 0)
        def _():
            dw_ref[...] = jnp.zeros_like(dw_ref)

        for s in range(TM // SB):
            rows = slice(s * SB, (s + 1) * SB)
            d = dx1_ref[rows, :]
            dmb_ref[rows, :] = (d * bm_ref[s, 2:3, :]).astype(dmb_ref.dtype)
            part_ref[s] = jnp.concatenate([_colsum(d * mix_ref[rows, :]), jnp.zeros((7, D), F32)], axis=0)
        dw_ref[...] += _dotg(cat_ref[...], dmb_ref[...], TN)
        dcat = _dotg(dmb_ref[...], wot_ref[...], NT)
        da_ref[...] = dcat[:, 0:QW]
        ds_ref[...] = dcat[:, QW:QW + SSD_INNER]
        dpl_ref[...] = dcat[:, QW + SSD_INNER:CAT]

    return pl.pallas_call(
        body, name="mix_bwd", grid=(R // TM,),
        in_specs=[_rowspec(D), _rowspec(D), _rowspec(CAT), pl.BlockSpec((TM // SB, 8, D), lambda i: (i, 0, 0)),
                  _resident((CAT, D))],
        out_specs=[_rowspec(QW), _rowspec(SSD_INNER), _rowspec(POOL_DIM), _fullspec((CAT, D)),
                   pl.BlockSpec((TM // SB, 8, D), lambda i: (i, 0, 0))],
        out_shape=[jax.ShapeDtypeStruct((R, QW), F32), jax.ShapeDtypeStruct((R, SSD_INNER), F32),
                   jax.ShapeDtypeStruct((R, POOL_DIM), F32), jax.ShapeDtypeStruct((CAT, D), F32),
                   jax.ShapeDtypeStruct((R // SB, 8, D), F32)],
        scratch_shapes=[pltpu.VMEM((TM, D), MXU)],
        compiler_params=_cp(48),
    )(dx1, mix, cat, bm, wot)


def mlp_fwd(x1, bm, nw, w1, w2, side=None):
    R = x1.shape[0]

    def body(x1_ref, bm_ref, nw_ref, w1_ref, w2_ref, x2_ref, mo_ref, r_ref, h2_ref):
        for s in range(TM // SB):
            rows = slice(s * SB, (s + 1) * SB)
            xhat, _ = _rms_hat(x1_ref[rows, :])
            h = xhat * nw_ref[...] * (1.0 + bm_ref[s, 4:5, :]) + bm_ref[s, 3:4, :]
            h2_ref[rows, :] = h.astype(h2_ref.dtype)
        for j in range(D_FF // FF_BLK):
            cols = slice(j * FF_BLK, (j + 1) * FF_BLK)
            r = jnp.maximum(_dot(h2_ref[...], w1_ref[:, cols]), 0.0)
            r_ref[:, cols] = r.astype(r_ref.dtype)
            d = _dot((r * r).astype(MXU), w2_ref[cols, :])
            if j == 0:
                mo_ref[...] = d
            else:
                mo_ref[...] += d
        for s in range(TM // SB):
            rows = slice(s * SB, (s + 1) * SB)
            x2_ref[rows, :] = x1_ref[rows, :] + bm_ref[s, 5:6, :] * mo_ref[rows, :]

    grid = (R // TM,)
    body, side_in, side_out, side_shapes, side_scratch, side_args = _side_wrap(body, 5, 4, 0, side, grid)
    outs = pl.pallas_call(
        body, name="mlp_fwd" if side is None else "mlp_fwd_comm", grid=grid,
        in_specs=[_rowspec(D), pl.BlockSpec((TM // SB, 8, D), lambda i: (i, 0, 0)), _fullspec((1, D)),
                  _resident((D, D_FF)), _resident((D_FF, D))] + side_in,
        out_specs=[_rowspec(D), _rowspec(D), _rowspec(D_FF), _rowspec(D)] + side_out,
        out_shape=[jax.ShapeDtypeStruct((R, D), F32), jax.ShapeDtypeStruct((R, D), F32),
                   jax.ShapeDtypeStruct((R, D_FF), BF16), jax.ShapeDtypeStruct((R, D), MXU)] + side_shapes,
        scratch_shapes=side_scratch,
        compiler_params=_cp(56),
    )(x1, bm, nw, w1, w2, *side_args)
    return tuple(outs[:4]) + (list(outs[4:]),)


def mlp_bwd(dx2, x1, mo, r, bm, nw, w2t, w1t, side=None):
    R = x1.shape[0]

    def body(dx2_ref, x1_ref, mo_ref, r_ref, bm_ref, nw_ref, w2t_ref, w1t_ref, dx1_ref, du_ref, dob_ref, part_ref,
             acc_ref):
        for s in range(TM // SB):
            rows = slice(s * SB, (s + 1) * SB)
            dob_ref[rows, :] = (dx2_ref[rows, :] * bm_ref[s, 5:6, :]).astype(dob_ref.dtype)
        for j in range(D_FF // FF_BLK):
            cols = slice(j * FF_BLK, (j + 1) * FF_BLK)
            du = _dotg(dob_ref[...], w2t_ref[cols, :], NT) * (2.0 * r_ref[:, cols].astype(F32))
            du_ref[:, cols] = du.astype(du_ref.dtype)
            d = _dotg(du_ref[:, cols], w1t_ref[:, cols], NT)
            if j == 0:
                acc_ref[...] = d
            else:
                acc_ref[...] += d
        w = nw_ref[...]
        for s in range(TM // SB):
            rows = slice(s * SB, (s + 1) * SB)
            xhat, rstd = _rms_hat(x1_ref[rows, :])
            dh = acc_ref[rows, :]
            dx, dnw = _rms_bwd(dh * (1.0 + bm_ref[s, 4:5, :]), xhat, rstd, w)
            d2 = dx2_ref[rows, :]
            dx1_ref[rows, :] = d2 + dx
            part_ref[s] = jnp.concatenate(
                [_colsum(dh), _colsum(dh * xhat * w), _colsum(d2 * mo_ref[rows, :]), dnw,
                 jnp.zeros((4, D), F32)], axis=0)

    grid = (R // TM,)
    body, side_in, side_out, side_shapes, side_scratch, side_args = _side_wrap(body, 8, 4, 1, side, grid)
    outs = pl.pallas_call(
        body, name="mlp_bwd" if side is None else "mlp_bwd_comm", grid=grid,
        in_specs=[_rowspec(D), _rowspec(D), _rowspec(D), _rowspec(D_FF),
                  pl.BlockSpec((TM // SB, 8, D), lambda i: (i, 0, 0)), _fullspec((1, D)),
                  _resident((D_FF, D)), _resident((D, D_FF))] + side_in,
        out_specs=[_rowspec(D), _rowspec(D_FF), _rowspec(D), pl.BlockSpec((TM // SB, 8, D), lambda i: (i, 0, 0))]
                  + side_out,
        out_shape=[jax.ShapeDtypeStruct((R, D), F32), jax.ShapeDtypeStruct((R, D_FF), MXU),
                   jax.ShapeDtypeStruct((R, D), MXU), jax.ShapeDtypeStruct((R // SB, 8, D), F32)] + side_shapes,
        scratch_shapes=[pltpu.VMEM((TM, D), F32)] + side_scratch,
        compiler_params=_cp(56),
    )(dx2, x1, mo, r, bm, nw, w2t, w1t, *side_args)
    return tuple(outs[:4]) + (list(outs[4:]),)


def mm_tn(a, b, square_a=False, name="mm_tn", col_blocks=False):
    R, M = a.shape
    N = b.shape[1]
    tm = M if M <= 1408 else 1024
    tn = N if N <= 2176 else 1024
    tk = next((c for c in ((2176, 1088, 512) if tm + tn <= 2048 else (1088, 512)) if R % c == 0), R)
    assert not col_blocks or tm == M

    def body(a_ref, b_ref, o_ref):
        @pl.when(pl.program_id(2) == 0)
        def _():
            o_ref[...] = jnp.zeros_like(o_ref)

        av = a_ref[...]
        if square_a:
            av = av.astype(F32)
            av = (av * av).astype(MXU)
        prod = _dotg(av.astype(MXU), b_ref[...].astype(MXU), TN)
        if col_blocks:
            o_ref[0] += prod
        else:
            o_ref[...] += prod

    if col_blocks:
        out_spec = pl.BlockSpec((1, tm, tn), lambda i, j, k: (j, 0, 0))
        out_shape = jax.ShapeDtypeStruct((N // tn, M, tn), F32)
    else:
        out_spec = pl.BlockSpec((tm, tn), lambda i, j, k: (i, j))
        out_shape = jax.ShapeDtypeStruct((M, N), F32)
    return pl.pallas_call(
        body, name=name, grid=(M // tm, N // tn, R // tk),
        in_specs=[pl.BlockSpec((tk, tm), lambda i, j, k: (k, i)), pl.BlockSpec((tk, tn), lambda i, j, k: (k, j))],
        out_specs=out_spec, out_shape=out_shape,
        compiler_params=_cp(48),
    )(a, b)


def final_loss(x, tgt, fw, blocks_per_sample):
    R = x.shape[0]
    nxb = blocks_per_sample - 1

    def body(x_ref, t_ref, fw_ref, dx_ref, part_ref):
        i = pl.program_id(0)
        is_ctx = (i % blocks_per_sample) == 0
        xhat, rstd = _rms_hat(x_ref[...])
        w = fw_ref[...]
        err = xhat * w - t_ref[...]
        dx, dfw = _rms_bwd(err * (1.0 / D), xhat, rstd, w)
        keep = jnp.where(is_ctx, 0.0, 1.0)
        dx_ref[...] = dx * keep
        part_ref[0] = jnp.concatenate([dfw * keep, _colsum(err * err) * keep, jnp.zeros((6, D), F32)], axis=0)

    def tmap(i):
        return ((i // blocks_per_sample) * nxb + jnp.maximum(i % blocks_per_sample - 1, 0), 0)

    return pl.pallas_call(
        body, name="final_loss", grid=(R // SB,),
        in_specs=[_rowspec(D, SB), pl.BlockSpec((SB, D), tmap), _fullspec((1, D))],
        out_specs=[_rowspec(D, SB), pl.BlockSpec((1, 8, D), lambda i: (i, 0, 0))],
        out_shape=[jax.ShapeDtypeStruct((R, D), F32), jax.ShapeDtypeStruct((R // SB, 8, D), F32)],
    )(x, tgt, fw)


def _softplus(v):
    return jnp.maximum(v, 0.0) + jnp.log(1.0 + jnp.exp(-jnp.abs(v)))


def _conv_taps(ext):
    return [_shift(ext, k - 1) for k in range(4)]


def _conv_out(taps, cw_ref, cb_ref):
    return (cb_ref[...] + cw_ref[0:1, :] * taps[0] + cw_ref[1:2, :] * taps[1] + cw_ref[2:3, :] * taps[2]
            + cw_ref[3:4, :] * taps[3])


def _dt_dir(v, d):
    lane = lax.broadcasted_iota(jnp.int32, v.shape, 1)
    return jnp.where(lane < SSD_HEADS, pltpu.roll(v, (128 - DT0 - SSD_HEADS * d) % 128, axis=1), 0.0)


def ssd_prep(pxbc, plast, cw, cb, dtb, blocks_per_sample):
    R = pxbc.shape[0]
    prev, nxt = _halo_specs(XBC, R)

    def body(cur_ref, prev_ref, nxt_ref, pl_ref, cw_ref, cb_ref, dtb_ref, xs_ref, bm_ref, cm_ref, dt_ref):
        i = pl.program_id(0)
        ext = _ext_rows(cur_ref[...], prev_ref[...], nxt_ref[...], i, blocks_per_sample)
        co = _conv_out(_conv_taps(ext), cw_ref, cb_ref)
        a = co * _sigmoid(co)
        xs_ref[...] = a[:, 0:384]
        bm_ref[...] = a[:, 384:640]
        cm_ref[...] = a[:, 640:896]
        sp = _softplus(pl_ref[...] + dtb_ref[...])
        dt_ref[0] = _dt_dir(sp, 0)
        dt_ref[1] = _dt_dir(sp, 1)

    return pl.pallas_call(
        body, name="ssd_prep", grid=(R // SB,),
        in_specs=[_rowspec(XBC, SB), prev, nxt, _rowspec(128, SB), _fullspec((8, XBC)), _fullspec((1, XBC)),
                  _fullspec((1, 128))],
        out_specs=[_rowspec(384, SB), _rowspec(256, SB), _rowspec(256, SB),
                   pl.BlockSpec((2, SB, 128), lambda i: (0, i, 0))],
        out_shape=[jax.ShapeDtypeStruct((R, 384), F32), jax.ShapeDtypeStruct((R, 256), F32),
                   jax.ShapeDtypeStruct((R, 256), F32), jax.ShapeDtypeStruct((2, R, 128), F32)],
    )(pxbc, pxbc, pxbc, plast, cw, cb, dtb)


def _chunk_index(d, s, nc):
    nctx = CTX // CHUNK
    back = jnp.where(s < nctx, nctx - 1 - s, nc + nctx - 1 - s)
    return jnp.where(d == 0, s, back)


def _scan_common(d, dt, arow, eexp, xs):
    ii = lax.broadcasted_iota(jnp.int32, (CHUNK, CHUNK), 0)
    jj = lax.broadcasted_iota(jnp.int32, (CHUNK, CHUNK), 1)
    mask = ((ii - jj) * (1 - 2 * d)) >= 0
    adt = dt * arow
    tmat = jnp.where(mask, 1.0, 0.0)
    cs = _dot_hi(tmat, adt, sel_first=True)
    tot = _colsum(adt)
    dtx = _dot_hi(dt, eexp)
    xt = xs * dtx
    ecs = jnp.exp(cs)
    ecx = _dot_hi(ecs, eexp)
    dte = jnp.exp(tot - cs)
    dtex = _dot_hi(dte, eexp)
    etot = jnp.exp(tot)
    etx = _dot_hi(jnp.broadcast_to(etot, (8, 128)), eexp)[0:1, :]
    return mask, tmat, adt, cs, tot, dtx, xt, ecs, ecx, dte, dtex, etot, etx


def _decay_matrix(mask, cs, cst, h):
    return jnp.exp(jnp.where(mask, cs[:, h:h + 1] - cst[h:h + 1, :], -1e30))


def _side_wrap(body, n_in, n_out, n_scratch, side, grid):
    if side is None:
        return body, [], [], [], [], []
    ni, no = len(side.ins), len(side.out_shapes)

    def wrapped(*refs):
        ins, refs = refs[:n_in], refs[n_in:]
        side_ins, refs = refs[:ni], refs[ni:]
        outs, refs = refs[:n_out], refs[n_out:]
        side_outs, refs = refs[:no], refs[no:]
        scratch, sems = refs[:n_scratch], refs[n_scratch:]
        ids = [pl.program_id(a) for a in range(len(grid))]
        first = functools.reduce(jnp.logical_and, [i == 0 for i in ids])
        last = functools.reduce(jnp.logical_and, [i == g - 1 for i, g in zip(ids, grid)])
        pl.when(first)(lambda: side.start(side_ins, side_outs, sems))
        body(*ins, *outs, *scratch)
        pl.when(last)(lambda: side.finish(side_ins, side_outs, sems))

    return wrapped, [ANY] * ni, [ANY] * no, list(side.out_shapes), _sems(side.nsem), list(side.ins)


def ssd_scan_fwd(xs, bm, cm, dtv, arow, eexp, nb, T, side=None):
    R = xs.shape[0]
    nc = T // CHUNK
    B = range(nb)

    def body(xs_ref, bm_ref, cm_ref, dt_ref, a_ref, e_ref, y_ref, hin_ref, st_ref):
        d = pl.program_id(0)
        s = pl.program_id(1)

        @pl.when(s == 0)
        def _():
            st_ref[...] = jnp.zeros_like(st_ref)

        eexp = e_ref[...]
        com = [_scan_common(d, dt_ref[0, b], a_ref[0, 0:1, :], eexp, xs_ref[b]) for b in B]
        mask = com[0][0]
        cs = [com[b][3] for b in B]
        cst = [cs[b].T for b in B]
        sin = [st_ref[b] for b in B]
        for b in B:
            hin_ref[0, b] = sin[b]
        sb = [sin[b].astype(MXU) for b in B]
        xtb = [com[b][6].astype(MXU) for b in B]
        xw = [(com[b][6] * com[b][10]).astype(MXU) for b in B]
        g0 = lax.broadcasted_iota(jnp.int32, (CHUNK, SSD_INNER), 1) < 192
        lane = lax.broadcasted_iota(jnp.int32, (CHUNK, 128), 1)
        c = [[cm_ref[b, :, 0:128].astype(MXU), cm_ref[b, :, 128:256].astype(MXU)] for b in B]
        bq = [[bm_ref[b, :, 0:128].astype(MXU), bm_ref[b, :, 128:256].astype(MXU)] for b in B]
        y = [jnp.where(g0, _dot(c[b][0], sb[b]), _dot(c[b][1], sb[b])) * com[b][8] for b in B]
        cb = [[_dotg(c[b][g], bq[b][g], NT) for g in range(2)] for b in B]
        blocks = [[] for _ in B]
        for blk in range(3):
            acc = [None for _ in B]
            for hh in range(2):
                h = blk * 2 + hh
                for b in B:
                    m = (cb[b][h // 3] * _decay_matrix(mask, cs[b], cst[b], h)).astype(MXU)
                    res = _dot(m, xtb[b][:, blk * 128:(blk + 1) * 128])
                    acc[b] = res if hh == 0 else jnp.where(lane < 64, acc[b], res)
            for b in B:
                blocks[b].append(acc[b])
        for b in B:
            y_ref[0, b] = y[b] + jnp.concatenate(blocks[b], axis=1)
            st_ref[b] = sin[b] * com[b][12] + jnp.where(g0, _dotg(bq[b][0], xw[b], TN), _dotg(bq[b][1], xw[b], TN))

    def rows(cols):
        return pl.BlockSpec((nb, CHUNK, cols), lambda d, s: (0, _chunk_index(d, s, nc), 0))

    def by_dir(cols):
        return pl.BlockSpec((1, nb, CHUNK, cols), lambda d, s: (d, 0, _chunk_index(d, s, nc), 0))

    grid = (2, nc)
    body, side_in, side_out, side_shapes, side_scratch, side_args = _side_wrap(body, 6, 2, 1, side, grid)
    outs = pl.pallas_call(
        body, name="ssd_scan_fwd" if side is None else "ssd_scan_fwd_comm", grid=grid,
        in_specs=[rows(384), rows(256), rows(256), by_dir(128), pl.BlockSpec((1, 8, 128), lambda d, s: (d, 0, 0)),
                  pl.BlockSpec((128, 384), lambda d, s: (0, 0))] + side_in,
        out_specs=[by_dir(384),
                   pl.BlockSpec((1, nb, CHUNK, 384), lambda d, s: (d * nc + _chunk_index(d, s, nc), 0, 0, 0))] + side_out,
        out_shape=[jax.ShapeDtypeStruct((2, nb, T, 384), F32), jax.ShapeDtypeStruct((2 * nc, nb, CHUNK, 384), F32)]
                  + side_shapes,
        scratch_shapes=[pltpu.VMEM((nb, CHUNK, 384), F32)] + side_scratch,
    )(xs.reshape(nb, T, 384), bm.reshape(nb, T, 256), cm.reshape(nb, T, 256), dtv.reshape(2, nb, T, 128), arow, eexp,
      *side_args)
    return outs[0].reshape(2, R, 384), outs[1], list(outs[2:])


def ssd_scan_bwd(xs, bm, cm, dtv, arow, eexp, hin, dy, nb, T, side=None):
    R = xs.shape[0]
    nc = T // CHUNK
    B = range(nb)

    def chunk(d, s):
        return _chunk_index(d, nc - 1 - s, nc)

    def body(xs_ref, bm_ref, cm_ref, dt_ref, a_ref, e_ref, hin_ref, dy_ref,
             dxs_ref, dbm_ref, dcm_ref, ddt_ref, da_ref, ds_ref):
        d = pl.program_id(0)
        s = pl.program_id(1)

        @pl.when(s == 0)
        def _():
            ds_ref[...] = jnp.zeros_like(ds_ref)
            da_ref[...] = jnp.zeros_like(da_ref)

        eexp = e_ref[...]
        arow = a_ref[0, 0:1, :]
        dt = [dt_ref[0, b] for b in B]
        xs_v = [xs_ref[b] for b in B]
        com = [_scan_common(d, dt[b], arow, eexp, xs_v[b]) for b in B]
        mask, tmat = com[0][0], com[0][1]
        cs, dtx, xt, ecs, ecx, dte, dtex, etot, etx = [[com[b][i] for b in B] for i in (3, 5, 6, 7, 8, 9, 10, 11, 12)]
        cst = [cs[b].T for b in B]
        sin = [hin_ref[0, b] for b in B]
        sb = [sin[b].astype(MXU) for b in B]
        dsp = [ds_ref[b] for b in B]
        dyv = [dy_ref[b] for b in B]
        xtb = [xt[b].astype(MXU) for b in B]
        xw = [(xt[b] * dtex[b]).astype(MXU) for b in B]
        g0 = lax.broadcasted_iota(jnp.int32, (CHUNK, SSD_INNER), 1) < 192
        lane = lax.broadcasted_iota(jnp.int32, (CHUNK, 128), 1)
        sub = lax.broadcasted_iota(jnp.int32, (CHUNK, 128), 0)
        c = [[cm_ref[b, :, 0:128].astype(MXU), cm_ref[b, :, 128:256].astype(MXU)] for b in B]
        bq = [[bm_ref[b, :, 0:128].astype(MXU), bm_ref[b, :, 128:256].astype(MXU)] for b in B]

        cs_prod = [jnp.where(g0, _dot(c[b][0], sb[b]), _dot(c[b][1], sb[b])) for b in B]
        dcsp = [dyv[b] * ecx[b] for b in B]
        dcsp_g = [[jnp.where(g0, dcsp[b], 0.0).astype(MXU), jnp.where(g0, 0.0, dcsp[b]).astype(MXU)] for b in B]
        dcs = [_dot_hi(dyv[b] * cs_prod[b], eexp, NT) * ecs[b] for b in B]
        dc = [[_dotg(dcsp_g[b][g], sb[b], NT) for g in range(2)] for b in B]
        dsin = [_dotg(c[b][0], dcsp_g[b][0], TN) + _dotg(c[b][1], dcsp_g[b][1], TN) + dsp[b] * etx[b] for b in B]

        dtot = [_dot_hi(jnp.broadcast_to(_colsum(dsp[b] * sin[b]), (8, SSD_INNER)), eexp, NT)[0:1, :] * etot[b] for b in B]
        dsp_g = [[jnp.where(g0, dsp[b], 0.0).astype(MXU), jnp.where(g0, 0.0, dsp[b]).astype(MXU)] for b in B]
        dxw = [_dot(bq[b][0], dsp_g[b][0]) + _dot(bq[b][1], dsp_g[b][1]) for b in B]
        db = [[_dotg(xw[b], dsp_g[b][g], NT) for g in range(2)] for b in B]
        dxt = [dxw[b] * dtex[b] for b in B]
        ddte = [_dot_hi(dxw[b] * xt[b], eexp, NT) * dte[b] for b in B]
        dtot = [dtot[b] + _colsum(ddte[b]) for b in B]
        dcs = [dcs[b] - ddte[b] for b in B]

        cb = [[_dotg(c[b][g], bq[b][g], NT) for g in range(2)] for b in B]
        dg = [[jnp.zeros((CHUNK, CHUNK), F32), jnp.zeros((CHUNK, CHUNK), F32)] for _ in B]
        dcs_rows = [jnp.zeros((CHUNK, 128), F32) for _ in B]
        dxt_blocks = [[] for _ in B]
        for blk in range(3):
            acc = [jnp.zeros((CHUNK, 128), F32) for _ in B]
            for hh in range(2):
                h = blk * 2 + hh
                g = h // 3
                mine = (lane < 64) if hh == 0 else (lane >= 64)
                for b in B:
                    dyh = jnp.where(mine, dyv[b][:, blk * 128:(blk + 1) * 128], 0.0).astype(MXU)
                    lh = _decay_matrix(mask, cs[b], cst[b], h)
                    m = cb[b][g] * lh
                    dm = _dotg(dyh, xtb[b][:, blk * 128:(blk + 1) * 128], NT)
                    acc[b] = acc[b] + _dotg(m.astype(MXU), dyh, TN)
                    dg[b][g] = dg[b][g] + dm * lh
                    q = dm * m
                    dcs[b] = dcs[b] + jnp.where(lane == h, jnp.sum(q, axis=1, keepdims=True), 0.0)
                    dcs_rows[b] = dcs_rows[b] - jnp.where(sub == h, jnp.sum(q, axis=0, keepdims=True), 0.0)
            for b in B:
                dxt_blocks[b].append(acc[b])
        for b in B:
            dxt[b] = dxt[b] + jnp.concatenate(dxt_blocks[b], axis=1)
            for g in range(2):
                dgb = dg[b][g].astype(MXU)
                dc[b][g] = dc[b][g] + _dot(dgb, bq[b][g])
                db[b][g] = db[b][g] + _dotg(dgb, c[b][g], TN)
            dcs[b] = dcs[b] + dcs_rows[b].T

        for b in B:
            dadt = _dot_hi(tmat, dcs[b], TN, sel_first=True) + dtot[b]
            ddt_ref[0, b] = dadt * arow + _dot_hi(dxt[b] * xs_v[b], eexp, NT)
            da_ref[0, b, 0:1, :] += _colsum(dadt * dt[b])
            dxs_ref[0, b] = (dxt[b] * dtx[b]).astype(dxs_ref.dtype)
            dbm_ref[0, b] = jnp.concatenate(db[b], axis=1).astype(dbm_ref.dtype)
            dcm_ref[0, b] = jnp.concatenate(dc[b], axis=1).astype(dcm_ref.dtype)
            ds_ref[b] = dsin[b]

    def rows(cols):
        return pl.BlockSpec((nb, CHUNK, cols), lambda d, s: (0, chunk(d, s), 0))

    def by_dir(cols):
        return pl.BlockSpec((1, nb, CHUNK, cols), lambda d, s: (d, 0, chunk(d, s), 0))

    grid = (2, nc)
    body, side_in, side_out, side_shapes, side_scratch, side_args = _side_wrap(body, 8, 5, 1, side, grid)
    outs = pl.pallas_call(
        body, name="ssd_scan_bwd" if side is None else "ssd_scan_bwd_comm", grid=grid,
        in_specs=[rows(384), rows(256), rows(256), by_dir(128), pl.BlockSpec((1, 8, 128), lambda d, s: (d, 0, 0)),
                  pl.BlockSpec((128, 384), lambda d, s: (0, 0)),
                  pl.BlockSpec((1, nb, CHUNK, 384), lambda d, s: (d * nc + chunk(d, s), 0, 0, 0)), rows(384)] + side_in,
        out_specs=[by_dir(384), by_dir(256), by_dir(256), by_dir(128),
                   pl.BlockSpec((1, nb, 8, 128), lambda d, s: (d, 0, 0, 0))] + side_out,
        out_shape=[jax.ShapeDtypeStruct((2, nb, T, 384), MXU), jax.ShapeDtypeStruct((2, nb, T, 256), MXU),
                   jax.ShapeDtypeStruct((2, nb, T, 256), MXU), jax.ShapeDtypeStruct((2, nb, T, 128), F32),
                   jax.ShapeDtypeStruct((2, nb, 8, 128), F32)] + side_shapes,
        scratch_shapes=[pltpu.VMEM((nb, CHUNK, 384), F32)] + side_scratch,
    )(xs.reshape(nb, T, 384), bm.reshape(nb, T, 256), cm.reshape(nb, T, 256), dtv.reshape(2, nb, T, 128), arow, eexp,
      hin, dy.reshape(nb, T, 384), *side_args)
    return (outs[0].reshape(2, R, 384), outs[1].reshape(2, R, 256), outs[2].reshape(2, R, 256),
            outs[3].reshape(2, R, 128), outs[4], list(outs[5:]))


def _group_rms(g):
    lane = lax.broadcasted_iota(jnp.int32, g.shape, 1)
    g0 = lane < 192
    gg = g * g
    s0 = jnp.sum(jnp.where(g0, gg, 0.0), axis=-1, keepdims=True)
    s1 = jnp.sum(gg, axis=-1, keepdims=True) - s0
    rstd = jnp.where(g0, lax.rsqrt(s0 * (1.0 / 192) + EPS), lax.rsqrt(s1 * (1.0 / 192) + EPS))
    return rstd, g0


def ssd_out_fwd(y2, xs, pz, dexp, nw):
    R = xs.shape[0]

    def body(y_ref, xs_ref, z_ref, d_ref, nw_ref, o_ref):
        z = z_ref[...]
        yy = y_ref[0] + y_ref[1] + xs_ref[...] * d_ref[...]
        g = yy * (z * _sigmoid(z))
        rstd, _ = _group_rms(g)
        o_ref[...] = g * rstd * nw_ref[...]

    return pl.pallas_call(
        body, name="ssd_out_fwd", grid=(R // TM,),
        in_specs=[pl.BlockSpec((2, TM, 384), lambda i: (0, i, 0)), _rowspec(384), _rowspec(384),
                  _fullspec((1, 384)), _fullspec((1, 384))],
        out_specs=_rowspec(384),
        out_shape=jax.ShapeDtypeStruct((R, 384), F32),
    )(y2, xs, pz, dexp, nw)


def ssd_out_bwd(dout, y2, xs, pz, dexp, nw):
    R = xs.shape[0]

    def body(do_ref, y_ref, xs_ref, z_ref, d_ref, nw_ref, dy_ref, dz_ref, dxs_ref, part_ref):
        z = z_ref[...]
        xs_v = xs_ref[...]
        yy = y_ref[0] + y_ref[1] + xs_v * d_ref[...]
        sig = _sigmoid(z)
        sz = z * sig
        g = yy * sz
        rstd, g0 = _group_rms(g)
        ghat = g * rstd
        do = do_ref[...]
        dgn = do * nw_ref[...]
        t = dgn * ghat
        t0 = jnp.sum(jnp.where(g0, t, 0.0), axis=-1, keepdims=True)
        t1 = jnp.sum(t, axis=-1, keepdims=True) - t0
        dg = rstd * (dgn - ghat * jnp.where(g0, t0, t1) * (1.0 / 192))
        dyy = dg * sz
        dy_ref[...] = dyy
        dz_ref[...] = (dg * yy * (sig * (1.0 + z * (1.0 - sig)))).astype(dz_ref.dtype)
        dxs_ref[...] = dyy * d_ref[...]
        part_ref[0] = jnp.concatenate([_colsum(do * ghat), _colsum(dyy * xs_v), jnp.zeros((6, 384), F32)], axis=0)

    return pl.pallas_call(
        body, name="ssd_out_bwd", grid=(R // TM,),
        in_specs=[_rowspec(384), pl.BlockSpec((2, TM, 384), lambda i: (0, i, 0)), _rowspec(384), _rowspec(384),
                  _fullspec((1, 384)), _fullspec((1, 384))],
        out_specs=[_rowspec(384), _rowspec(384), _rowspec(384), pl.BlockSpec((1, 8, 384), lambda i: (i, 0, 0))],
        out_shape=[jax.ShapeDtypeStruct((R, 384), F32), jax.ShapeDtypeStruct((R, 384), MXU),
                   jax.ShapeDtypeStruct((R, 384), F32), jax.ShapeDtypeStruct((R // TM, 8, 384), F32)],
    )(dout, y2, xs, pz, dexp, nw)


def ssd_prep_bwd_a(pxbc, plast, cw, cb, dtb, dxs_skip, dxs2, dbm2, dcm2, ddt2, blocks_per_sample):
    R = pxbc.shape[0]
    prev, nxt = _halo_specs(XBC, R)

    def body(cur_ref, prev_ref, nxt_ref, pl_ref, cw_ref, cb_ref, dtb_ref, dsk_ref, dxs_ref, dbm_ref, dcm_ref, ddt_ref,
             dpre_ref, dlast_ref, part_ref):
        i = pl.program_id(0)
        ext = _ext_rows(cur_ref[...], prev_ref[...], nxt_ref[...], i, blocks_per_sample)
        taps = _conv_taps(ext)
        co = _conv_out(taps, cw_ref, cb_ref)
        sig = _sigmoid(co)
        both = lambda ref: ref[0].astype(F32) + ref[1].astype(F32)
        up = jnp.concatenate([dsk_ref[...] + both(dxs_ref), both(dbm_ref), both(dcm_ref)], axis=1)
        dpre = up * (sig * (1.0 + co * (1.0 - sig)))
        dpre_ref[...] = dpre
        raw = pl_ref[...] + dtb_ref[...]
        lane = lax.broadcasted_iota(jnp.int32, raw.shape, 1)
        ddt = (pltpu.roll(ddt_ref[0], DT0, axis=1) + pltpu.roll(ddt_ref[1], DT0 + SSD_HEADS, axis=1))
        ddt = jnp.where(jnp.logical_and(lane >= DT0, lane < DT0 + 2 * SSD_HEADS), ddt * _sigmoid(raw), 0.0)
        dlast_ref[...] = ddt.astype(dlast_ref.dtype)
        rows = [_colsum(dpre * taps[k]) for k in range(4)]
        rows.append(_colsum(dpre))
        rows.append(jnp.concatenate([_colsum(ddt), jnp.zeros((1, XBC - 128), F32)], axis=1))
        rows.append(jnp.zeros((2, XBC), F32))
        part_ref[0] = jnp.concatenate(rows, axis=0)

    dirspec = lambda n: pl.BlockSpec((2, SB, n), lambda i: (0, i, 0))
    return pl.pallas_call(
        body, name="ssd_prep_bwd_a", grid=(R // SB,),
        in_specs=[_rowspec(XBC, SB), prev, nxt, _rowspec(128, SB), _fullspec((8, XBC)), _fullspec((1, XBC)),
                  _fullspec((1, 128)), _rowspec(384, SB), dirspec(384), dirspec(256), dirspec(256), dirspec(128)],
        out_specs=[_rowspec(XBC, SB), _rowspec(128, SB), pl.BlockSpec((1, 8, XBC), lambda i: (i, 0, 0))],
        out_shape=[jax.ShapeDtypeStruct((R, XBC), F32), jax.ShapeDtypeStruct((R, 128), MXU),
                   jax.ShapeDtypeStruct((R // SB, 8, XBC), F32)],
    )(pxbc, pxbc, pxbc, plast, cw, cb, dtb, dxs_skip, dxs2, dbm2, dcm2, ddt2)


def ssd_prep_bwd_b(dpre, cw, blocks_per_sample):
    R = dpre.shape[0]
    prev, nxt = _halo_specs(XBC, R)

    def body(cur_ref, prev_ref, nxt_ref, cw_ref, o_ref):
        i = pl.program_id(0)
        ext = _ext_rows(cur_ref[...], prev_ref[...], nxt_ref[...], i, blocks_per_sample)
        o_ref[...] = (cw_ref[0:1, :] * _shift(ext, 1) + cw_ref[1:2, :] * _shift(ext, 0)
                      + cw_ref[2:3, :] * _shift(ext, -1) + cw_ref[3:4, :] * _shift(ext, -2)).astype(o_ref.dtype)

    return pl.pallas_call(
        body, name="ssd_prep_bwd_b", grid=(R // SB,),
        in_specs=[_rowspec(XBC, SB), prev, nxt, _fullspec((8, XBC))],
        out_specs=_rowspec(XBC, SB),
        out_shape=jax.ShapeDtypeStruct((R, XBC), MXU),
    )(dpre, dpre, dpre, cw)


def _rope(u, cos, sa, sb):
    return u * cos + pltpu.roll(u, 120, axis=1) * sa + pltpu.roll(u, 8, axis=1) * sb


def _rope_t(du, cos, sa, sb):
    return du * cos + pltpu.roll(du * sa, 8, axis=1) + pltpu.roll(du * sb, 120, axis=1)


def mla_prep(pqa, pkva, plast, qnw, kvnw, wq, wk, wv, cos, sa, sb):
    R = pqa.shape[0]

    def body(qa_ref, kva_ref, pl_ref, qnw_ref, kvnw_ref, wq_ref, wk_ref, wv_ref, cos_ref, sa_ref, sb_ref,
             q_ref, k_ref, v_ref, cq_ref, ckv_ref):
        cos_v, sa_v, sb_v = cos_ref[...], sa_ref[...], sb_ref[...]
        xq, _ = _rms_hat(qa_ref[...])
        cq_ref[...] = (xq * qnw_ref[...]).astype(cq_ref.dtype)
        xkv, _ = _rms_hat(kva_ref[...])
        ckv_ref[...] = (xkv * kvnw_ref[...]).astype(ckv_ref.dtype)
        q = _dot(cq_ref[...], wq_ref[...])
        kn = _dot(ckv_ref[...], wk_ref[...])
        v_ref[...] = _dot(ckv_ref[...], wv_ref[...]).astype(v_ref.dtype)
        lane = lax.broadcasted_iota(jnp.int32, (TM, HP), 1)
        rope_lanes = jnp.logical_and(lane >= QK_NOPE, lane < QK_DIM)
        kr = _rope(jnp.where(rope_lanes, pltpu.roll(pl_ref[...], QK_NOPE, axis=1), 0.0), cos_v, sa_v, sb_v)
        for h in range(MLA_HEADS):
            cols = slice(h * HP, (h + 1) * HP)
            q_ref[:, cols] = (_rope(q[:, cols], cos_v, sa_v, sb_v) * Q_SCALE).astype(q_ref.dtype)
            k_ref[:, cols] = (kn[:, cols] + kr).astype(k_ref.dtype)

    return pl.pallas_call(
        body, name="mla_prep", grid=(R // TM,),
        in_specs=[_rowspec(256), _rowspec(256), _rowspec(128), _fullspec((1, 256)), _fullspec((1, 256)),
                  _fullspec((256, QW)), _fullspec((256, QW)), _fullspec((256, QW)),
                  _rowspec(HP), _rowspec(HP), _rowspec(HP)],
        out_specs=[_rowspec(QW), _rowspec(QW), _rowspec(QW), _rowspec(256), _rowspec(256)],
        out_shape=[jax.ShapeDtypeStruct((R, QW), MXU)] * 3 + [jax.ShapeDtypeStruct((R, 256), MXU)] * 2,
    )(pqa, pkva, plast, qnw, kvnw, wq, wk, wv, cos, sa, sb)


def mla_prep_bwd(dq, dk, dv, pqa, pkva, qnw, kvnw, wqt, wkt, wvt, cos, sa, sb):
    R = pqa.shape[0]

    def body(dq_ref, dk_ref, dv_ref, qa_ref, kva_ref, qnw_ref, kvnw_ref, wqt_ref, wkt_ref, wvt_ref,
             cos_ref, sa_ref, sb_ref, dqa_ref, dkva_ref, dkr_ref, dql_ref, dkm_ref, dvb_ref, part_ref):
        cos_v, sa_v, sb_v = cos_ref[...], sa_ref[...], sb_ref[...]
        lane = lax.broadcasted_iota(jnp.int32, (TM, HP), 1)
        rope_lanes = jnp.logical_and(lane >= QK_NOPE, lane < QK_DIM)
        dkr = jnp.zeros((TM, HP), F32)
        for h in range(MLA_HEADS):
            cols = slice(h * HP, (h + 1) * HP)
            dql_ref[:, cols] = (_rope_t(dq_ref[:, cols], cos_v, sa_v, sb_v) * ATT_SCALE).astype(dql_ref.dtype)
            dkh = dk_ref[:, cols] * LN2
            dkm_ref[:, cols] = jnp.where(lane < QK_NOPE, dkh, 0.0).astype(dkm_ref.dtype)
            dkr = dkr + jnp.where(rope_lanes, dkh, 0.0)
        dvb_ref[...] = dv_ref[...].astype(dvb_ref.dtype)
        dkr = jnp.where(rope_lanes, _rope_t(dkr, cos_v, sa_v, sb_v), 0.0)
        dkr_ref[...] = pltpu.roll(dkr, HP - QK_NOPE, axis=1).astype(dkr_ref.dtype)
        xq, rq = _rms_hat(qa_ref[...])
        dqa, dqnw = _rms_bwd(_dotg(dql_ref[...], wqt_ref[...], NT), xq, rq, qnw_ref[...])
        dqa_ref[...] = dqa.astype(dqa_ref.dtype)
        xkv, rkv = _rms_hat(kva_ref[...])
        dckv = _dotg(dkm_ref[...], wkt_ref[...], NT) + _dotg(dvb_ref[...], wvt_ref[...], NT)
        dkva, dkvnw = _rms_bwd(dckv, xkv, rkv, kvnw_ref[...])
        dkva_ref[...] = dkva.astype(dkva_ref.dtype)
        part_ref[0] = jnp.concatenate([dqnw, dkvnw, jnp.zeros((6, 256), F32)], axis=0)

    return pl.pallas_call(
        body, name="mla_prep_bwd", grid=(R // TM,),
        in_specs=[_rowspec(QW), _rowspec(QW), _rowspec(QW), _rowspec(256), _rowspec(256), _fullspec((1, 256)),
                  _fullspec((1, 256)), _fullspec((256, QW)), _fullspec((256, QW)), _fullspec((256, QW)),
                  _rowspec(HP), _rowspec(HP), _rowspec(HP)],
        out_specs=[_rowspec(256), _rowspec(256), _rowspec(128), _rowspec(QW), _rowspec(QW), _rowspec(QW),
                   pl.BlockSpec((1, 8, 256), lambda i: (i, 0, 0))],
        out_shape=[jax.ShapeDtypeStruct((R, 256), MXU), jax.ShapeDtypeStruct((R, 256), MXU),
                   jax.ShapeDtypeStruct((R, 128), MXU)] + [jax.ShapeDtypeStruct((R, QW), MXU)] * 3
                  + [jax.ShapeDtypeStruct((R // TM, 8, 256), F32)],
    )(dq, dk, dv, pqa, pkva, qnw, kvnw, wqt, wkt, wvt, cos, sa, sb)


ATT_SCALE = QK_DIM ** -0.5
TQ = 256


LOG2E = 1.4426950408889634
LN2 = 0.6931471805599453
Q_SCALE = ATT_SCALE * LOG2E


def _key_chunks(T, n=2):
    unit = 256 if T % 256 == 0 else 128
    units = T // unit
    sizes = [(units // n + (1 if i < units % n else 0)) * unit for i in range(n)]
    return [(sum(sizes[:i]), sz) for i, sz in enumerate(sizes) if sz]


def attn_fwd(q, k, v, nb, T):
    R = q.shape[0]
    nq = T // TQ
    chunks = _key_chunks(T, 4)
    HEADS = range(2)

    def body(q_ref, k_ref, v_ref, o_ref, lse_ref):
        def lanes(h):
            return slice(h * HP, (h + 1) * HP)

        def logits(h, lo, n):
            return _dotg(q_ref[:, lanes(h)], k_ref[lo:lo + n, lanes(h)], NT)

        def weigh(h, s, lo, n):
            m = jnp.max(s, axis=-1, keepdims=True)
            p = jnp.exp2(s - m)
            return m, jnp.sum(p, axis=-1, keepdims=True), _dot(p.astype(MXU), v_ref[lo:lo + n, lanes(h)])

        def parts_of(ranges):
            out = [[] for _ in HEADS]
            s = [logits(h, *ranges[0]) for h in HEADS]
            for j, (lo, n) in enumerate(ranges):
                nxt = [logits(h, *ranges[j + 1]) for h in HEADS] if j + 1 < len(ranges) else None
                for h in HEADS:
                    out[h].append(weigh(h, s[h], lo, n))
                s = nxt
            return out

        def finish(all_parts):
            for h, parts in enumerate(all_parts):
                m = parts[0][0]
                for pm, _, _ in parts[1:]:
                    m = jnp.maximum(m, pm)
                l, o = 0.0, 0.0
                for pm, pl_, po in parts:
                    a = jnp.exp2(pm - m)
                    l = l + a * pl_
                    o = o + a * po
                o_ref[:, lanes(h)] = o / l
                lse_ref[:, lanes(h)] = jnp.broadcast_to(m + jnp.log(l) * LOG2E, (TQ, HP))

        i = pl.program_id(2)
        pl.when(i == 0)(lambda: finish(parts_of([(0, CTX)])))
        pl.when(i > 0)(lambda: finish(parts_of(chunks)))

    qspec = pl.BlockSpec((TQ, 2 * HP), lambda b, h, i: (b * nq + i, h))
    kspec = pl.BlockSpec((T, 2 * HP), lambda b, h, i: (b, h))
    return pl.pallas_call(
        body, name="attn_fwd", grid=(nb, MLA_HEADS // 2, nq),
        in_specs=[qspec, kspec, kspec], out_specs=[qspec, qspec],
        out_shape=[jax.ShapeDtypeStruct((R, QW), F32)] * 2,
        compiler_params=_cp(48),
    )(q, k, v)


def attn_bwd(q, k, v, o, lse, do, nb, T):
    R = q.shape[0]
    nq = T // TQ
    chunks = _key_chunks(T)

    def body(q_ref, k_ref, v_ref, o_ref, lse_ref, do_ref, dq_ref, dk_ref, dv_ref):
        i = pl.program_id(2)

        @pl.when(i == 0)
        def _():
            dk_ref[...] = jnp.zeros_like(dk_ref)
            dv_ref[...] = jnp.zeros_like(dv_ref)

        def run(chunks):
            qv = q_ref[...]
            dov = do_ref[...]
            dob = dov.astype(MXU)
            delta = jnp.sum(dov * o_ref[...], axis=-1, keepdims=True)
            lse_v = lse_ref[:, 0:1]
            dq = 0.0
            for lo, n in chunks:
                kv = k_ref[lo:lo + n, :]
                p = jnp.exp2(_dotg(qv, kv, NT) - lse_v)
                dp = _dotg(dob, v_ref[lo:lo + n, :], NT)
                dsb = (p * (dp - delta)).astype(MXU)
                dq = dq + _dot(dsb, kv)
                dk_ref[lo:lo + n, :] += _dotg(dsb, qv, TN)
                dv_ref[lo:lo + n, :] += _dotg(p.astype(MXU), dob, TN)
            dq_ref[...] = dq

        pl.when(i == 0)(lambda: run([(0, CTX)]))
        pl.when(i > 0)(lambda: run(chunks))

    qspec = pl.BlockSpec((TQ, HP), lambda b, h, i: (b * nq + i, h))
    kspec = pl.BlockSpec((T, HP), lambda b, h, i: (b, h))
    return pl.pallas_call(
        body, name="attn_bwd", grid=(nb, MLA_HEADS, nq),
        in_specs=[qspec, kspec, kspec, qspec, qspec, qspec],
        out_specs=[qspec, kspec, kspec],
        out_shape=[jax.ShapeDtypeStruct((R, QW), F32)] * 3,
        compiler_params=_cp(56),
    )(q, k, v, o, lse, do)


def _pool_geometry(i, blocks_per_sample, seq):
    j = i % blocks_per_sample
    n = jnp.where(j == 0, CTX, seq)
    t0 = jnp.where(j == 0, 0, (j - 1) * SB) - HALO
    lane = lax.broadcasted_iota(jnp.int32, (SB + 2 * HALO, POOL_DIM), 1)
    t = lax.broadcasted_iota(jnp.int32, (SB + 2 * HALO, POOL_DIM), 0) + t0
    wh = jnp.where(lane < 64, 1, jnp.where(lane < 128, 2, jnp.where(lane < 192, 4, 8)))
    cnt = jnp.minimum(t + wh, n) - jnp.maximum(t - wh, 0)
    return lane, 1.0 / jnp.maximum(cnt, 1).astype(F32)


def _by_window(lane, c2, c4, c8, c16):
    return jnp.where(lane < 64, c2, jnp.where(lane < 128, c4, jnp.where(lane < 192, c8, c16)))


def _window_sums(ext, lane, first):
    n = ext.shape[0]
    r = lambda a, s: pltpu.roll(a, s % n, axis=0)
    c2 = ext + r(ext, first)
    c4 = r(c2, 1) + r(c2, -1)
    c8 = r(c4, 2) + r(c4, -2)
    c16 = r(c8, 4) + r(c8, -4)
    return _by_window(lane, c2, c4, c8, c16)


def _pool_delta(ext, lane, inv):
    return (_window_sums(ext, lane, 1) * inv - ext)[HALO:HALO + SB, :]


def pool_fwd(ppool, wbd, scale, blocks_per_sample, seq):
    R = ppool.shape[0]
    prev, nxt = _halo_specs(POOL_DIM, R)

    def body(cur_ref, prev_ref, nxt_ref, w_ref, s_ref, o_ref):
        i = pl.program_id(0)
        ext = _ext_rows(cur_ref[...], prev_ref[...], nxt_ref[...], i, blocks_per_sample)
        lane, inv = _pool_geometry(i, blocks_per_sample, seq)
        dlt = _pool_delta(ext, lane, inv)
        o_ref[...] = _dot(dlt.astype(MXU), w_ref[...]) * s_ref[...]

    return pl.pallas_call(
        body, name="pool_fwd", grid=(R // SB,),
        in_specs=[_rowspec(POOL_DIM, SB), prev, nxt, _fullspec((POOL_DIM, POOL_DIM)), _fullspec((1, POOL_DIM))],
        out_specs=_rowspec(POOL_DIM, SB),
        out_shape=jax.ShapeDtypeStruct((R, POOL_DIM), F32),
    )(ppool, ppool, ppool, wbd, scale)


def pool_bwd(ppool, dpool, wbd, scale, blocks_per_sample, seq):
    R = ppool.shape[0]
    prev, nxt = _halo_specs(POOL_DIM, R)

    def body(cur_ref, prev_ref, nxt_ref, dcur_ref, dprev_ref, dnxt_ref, w_ref, s_ref, du_ref, dw_ref, part_ref):
        i = pl.program_id(0)

        @pl.when(i == 0)
        def _():
            dw_ref[...] = jnp.zeros_like(dw_ref)

        ext = _ext_rows(cur_ref[...], prev_ref[...], nxt_ref[...], i, blocks_per_sample)
        lane, inv = _pool_geometry(i, blocks_per_sample, seq)
        dlt = _pool_delta(ext, lane, inv).astype(MXU)
        dy = dcur_ref[...]
        part_ref[0] = jnp.concatenate([_colsum(dy * _dot(dlt, w_ref[...])), jnp.zeros((7, POOL_DIM), F32)], axis=0)
        dyp = (dy * s_ref[...]).astype(MXU)
        dw_ref[...] += _dotg(dlt, dyp, TN)
        dext = _ext_rows(dy, dprev_ref[...], dnxt_ref[...], i, blocks_per_sample)
        dd = _dotg((dext * s_ref[...]).astype(MXU), w_ref[...], NT)
        du_ref[...] = (_window_sums(dd * inv, lane, -1) - dd)[HALO:HALO + SB, :].astype(du_ref.dtype)

    return pl.pallas_call(
        body, name="pool_bwd", grid=(R // SB,),
        in_specs=[_rowspec(POOL_DIM, SB), prev, nxt, _rowspec(POOL_DIM, SB), prev, nxt,
                  _fullspec((POOL_DIM, POOL_DIM)), _fullspec((1, POOL_DIM))],
        out_specs=[_rowspec(POOL_DIM, SB), _fullspec((POOL_DIM, POOL_DIM)),
                   pl.BlockSpec((1, 8, POOL_DIM), lambda i: (i, 0, 0))],
        out_shape=[jax.ShapeDtypeStruct((R, POOL_DIM), MXU), jax.ShapeDtypeStruct((POOL_DIM, POOL_DIM), F32),
                   jax.ShapeDtypeStruct((R // SB, 8, POOL_DIM), F32)],
    )(ppool, ppool, ppool, dpool, dpool, dpool, wbd, scale)


def adamw(w, g, m, v, name="adamw"):
    rows, cols = w.shape
    tr = rows
    for cand in (512, 256, 128, 64, 32, 16, 8):
        if rows % cand == 0:
            tr = cand
            break
    bc1 = 1.0 - ADAM_B1 ** ADAM_STEP
    bc2 = 1.0 - ADAM_B2 ** ADAM_STEP

    def body(w_ref, g_ref, m_ref, v_ref, d_ref, nm_ref, nv_ref):
        g_v = g_ref[...]
        nm = ADAM_B1 * m_ref[...] + (1.0 - ADAM_B1) * g_v
        nv = ADAM_B2 * v_ref[...] + (1.0 - ADAM_B2) * (g_v * g_v)
        nm_ref[...] = nm
        nv_ref[...] = nv
        d_ref[...] = -ADAM_LR * ((nm / bc1) / (jnp.sqrt(nv / bc2) + ADAM_EPS) + ADAM_WD * w_ref[...])

    spec = pl.BlockSpec((tr, cols), lambda i: (i, 0))
    return pl.pallas_call(
        body, name=name, grid=(rows // tr,),
        in_specs=[spec] * 4, out_specs=[spec] * 3,
        out_shape=[jax.ShapeDtypeStruct((rows, cols), F32)] * 3,
    )(w, g, m, v)


MODR = 32


def _silu(v):
    return v * _sigmoid(v)


def mod_fwd(cond, w, b):
    n = w.shape[1]

    def body(c_ref, w_ref, b_ref, o_ref):
        o_ref[...] = _dot(_silu(c_ref[...]).astype(MXU), w_ref[...].astype(MXU)) + b_ref[...]

    return pl.pallas_call(
        body, name="mod_fwd", out_shape=jax.ShapeDtypeStruct((MODR, n), F32),
        in_specs=[_fullspec((MODR, D)), _fullspec((D, n)), _fullspec((1, n))], out_specs=_fullspec((MODR, n)),
        grid=(1,), compiler_params=_cp(40),
    )(cond, w, b)


def mod_wgrad(cond, dm):
    n = dm.shape[1]

    def body(c_ref, d_ref, o_ref):
        o_ref[...] = _dotg(_silu(c_ref[...]).astype(MXU), d_ref[...].astype(MXU), TN)

    return pl.pallas_call(
        body, name="mod_wgrad", out_shape=jax.ShapeDtypeStruct((D, n), F32),
        in_specs=[_fullspec((MODR, D)), _fullspec((MODR, n))], out_specs=_fullspec((D, n)),
        grid=(1,), compiler_params=_cp(40),
    )(cond, dm)


def mod_dgrad(dm, w):
    n = w.shape[1]

    def body(d_ref, w_ref, o_ref):
        o_ref[...] = _dotg(d_ref[...].astype(MXU), w_ref[...].astype(MXU), NT)

    return pl.pallas_call(
        body, name="mod_dgrad", out_shape=jax.ShapeDtypeStruct((8, D), F32),
        in_specs=[_fullspec((8, n)), _fullspec((D, n))], out_specs=_fullspec((8, D)),
        grid=(1,), compiler_params=_cp(40),
    )(dm, w)


def sum_leading(a, name="sum_leading"):
    n, r, c = a.shape

    def body(a_ref, o_ref):
        acc = a_ref[0]
        for k in range(1, n):
            acc = acc + a_ref[k]
        o_ref[...] = acc

    return pl.pallas_call(
        body, name=name, out_shape=jax.ShapeDtypeStruct((r, c), F32),
        in_specs=[_fullspec((n, r, c))], out_specs=_fullspec((r, c)), grid=(1,),
    )(a)


MESH = pl.DeviceIdType.MESH
NDEV = 8
ANY = pl.BlockSpec(memory_space=pl.ANY)


def _place():
    return lax.axis_index("x"), lax.axis_index("y"), lax.axis_index("c")


def _other_chips(x, y):
    return [(1 - x, y), (x, 1 - y), (1 - x, 1 - y)]


def allgather_small(v, name):
    r, cols = v.shape

    def body(v_ref, o_ref, send_sems, recv_sems):
        x, y, c = _place()
        me = 4 * x + 2 * y + c
        o_ref[me] = v_ref[...]
        copies = []
        for rel in range(1, NDEV):
            peer = (1 - x if rel & 4 else x, 1 - y if rel & 2 else y, 1 - c if rel & 1 else c)
            cp = pltpu.make_async_remote_copy(src_ref=v_ref, dst_ref=o_ref.at[me], send_sem=send_sems.at[rel - 1],
                                              recv_sem=recv_sems.at[rel - 1], device_id=peer, device_id_type=MESH)
            cp.start()
            copies.append(cp)
        for cp in copies:
            cp.wait_recv()
        for cp in copies:
            cp.wait_send()

    return pl.pallas_call(
        body, name=name, out_shape=jax.ShapeDtypeStruct((NDEV, r, cols), F32),
        in_specs=[pl.BlockSpec(memory_space=pltpu.VMEM)], out_specs=pl.BlockSpec(memory_space=pltpu.VMEM),
        scratch_shapes=[pltpu.SemaphoreType.DMA((NDEV - 1,)), pltpu.SemaphoreType.DMA((NDEV - 1,))],
        compiler_params=_cp(40),
    )(v)


def _sems(n):
    return [pltpu.SemaphoreType.DMA((n,)), pltpu.SemaphoreType.DMA((n,))]


def gather_job(arrs):
    n = len(arrs)

    def copy(srcs, outs, sems, i, slot, kk, cc, to, from_src=False):
        hr = arrs[i].shape[0] // 2
        dst = outs[i].at[kk, pl.ds(cc * hr, hr), :]
        return pltpu.make_async_remote_copy(src_ref=srcs[i].at[pl.ds(cc * hr, hr), :] if from_src else dst, dst_ref=dst,
                                            send_sem=sems[0].at[slot * n + i], recv_sem=sems[1].at[slot * n + i],
                                            device_id=to, device_id_type=MESH)

    def start(srcs, outs, sems):
        x, y, c = _place()
        for j, (px, py) in enumerate(_other_chips(x, y)):
            for i in range(n):
                copy(srcs, outs, sems, i, j, 2 * x + y, c, (px, py, c), True).start()

    def finish(srcs, outs, sems):
        x, y, c = _place()
        sib = (x, y, 1 - c)
        chips = _other_chips(x, y)
        passed = []
        for j, (px, py) in enumerate(chips):
            for i in range(n):
                copy(srcs, outs, sems, i, j, 2 * px + py, c, (px, py, c)).wait_recv()
                cp = copy(srcs, outs, sems, i, 3 + j, 2 * px + py, c, sib)
                cp.start()
                passed.append(cp)
        for j, (px, py) in enumerate(chips):
            for i in range(n):
                copy(srcs, outs, sems, i, 3 + j, 2 * px + py, 1 - c, sib).wait_recv()
        for j, (px, py) in enumerate(chips):
            for i in range(n):
                copy(srcs, outs, sems, i, j, 2 * x + y, c, (px, py, c), True).wait_send()
        for cp in passed:
            cp.wait_send()

    return _NS(ins=list(arrs), out_shapes=[jax.ShapeDtypeStruct((4,) + a.shape, a.dtype) for a in arrs], nsem=6 * n,
               start=start, finish=finish)


def chip_swap_job(ss):
    n = len(ss)

    def copies(srcs, outs, sems):
        x, y, c = _place()
        return [pltpu.make_async_remote_copy(src_ref=srcs[i].at[2 * px + py], dst_ref=outs[i].at[j],
                                             send_sem=sems[0].at[j * n + i], recv_sem=sems[1].at[j * n + i],
                                             device_id=(px, py, c), device_id_type=MESH)
                for j, (px, py) in enumerate(_other_chips(x, y)) for i in range(n)]

    def start(srcs, outs, sems):
        for cp in copies(srcs, outs, sems):
            cp.start()

    def finish(srcs, outs, sems):
        for cp in copies(srcs, outs, sems):
            cp.wait()

    return _NS(ins=list(ss), out_shapes=[jax.ShapeDtypeStruct((3,) + s.shape[1:], s.dtype) for s in ss], nsem=3 * n,
               start=start, finish=finish)


def run_job(job, name):
    n, m = len(job.ins), len(job.out_shapes)

    def body(*refs):
        srcs, outs, sems = refs[:n], refs[n:n + m], refs[n + m:]
        job.start(srcs, outs, sems)
        job.finish(srcs, outs, sems)

    return pl.pallas_call(body, name=name, out_shape=job.out_shapes, in_specs=[ANY] * n, out_specs=[ANY] * m,
                          scratch_shapes=_sems(job.nsem))(*job.ins)


def swap_core_halves(gs):
    n = len(gs)

    def body(*refs):
        srcs, outs = refs[:n], refs[n:2 * n]
        send_sems, recv_sems = refs[2 * n:]
        x, y, c = _place()
        copies = []
        for i in range(n):
            hr = gs[i].shape[1] // 2
            cp = pltpu.make_async_remote_copy(src_ref=srcs[i].at[:, pl.ds((1 - c) * hr, hr), :], dst_ref=outs[i],
                                              send_sem=send_sems.at[i], recv_sem=recv_sems.at[i],
                                              device_id=(x, y, 1 - c), device_id_type=MESH)
            cp.start()
            copies.append(cp)
        for cp in copies:
            cp.wait()

    return pl.pallas_call(
        body, name="swap_core_halves",
        out_shape=[jax.ShapeDtypeStruct((4, g.shape[1] // 2, g.shape[2]), g.dtype) for g in gs],
        in_specs=[ANY] * n, out_specs=[ANY] * n, scratch_shapes=_sems(n),
    )(*gs)


def add_half(g, r1, cidx, name):
    _, rows, cols = g.shape
    hr = rows // 2

    def body(c_ref, g_ref, r_ref, o_ref, ob_ref):
        s = g_ref[...] + r_ref[...]
        o_ref[...] = s
        ob_ref[...] = s.astype(BF16)

    blk = lambda f: pl.BlockSpec((1, hr, cols), f)
    return pl.pallas_call(
        body, name=name,
        out_shape=[jax.ShapeDtypeStruct((4, hr, cols), F32), jax.ShapeDtypeStruct((4, hr, cols), BF16)],
        grid_spec=pltpu.PrefetchScalarGridSpec(
            num_scalar_prefetch=1, grid=(4,),
            in_specs=[blk(lambda k, c_ref: (k, c_ref[0], 0)), blk(lambda k, c_ref: (k, 0, 0))],
            out_specs=[blk(lambda k, c_ref: (k, 0, 0)), blk(lambda k, c_ref: (k, 0, 0))]),
    )(cidx, g, r1)


def sum_parts(s1, r2, kidx, name):
    _, hr, cols = s1.shape

    def body(k_ref, s_ref, r_ref, o_ref):
        o_ref[...] = ((s_ref[0] + r_ref[0].astype(F32)) + r_ref[1].astype(F32)) + r_ref[2].astype(F32)

    return pl.pallas_call(
        body, name=name, out_shape=jax.ShapeDtypeStruct((hr, cols), F32),
        grid_spec=pltpu.PrefetchScalarGridSpec(
            num_scalar_prefetch=1, grid=(1,),
            in_specs=[pl.BlockSpec((1, hr, cols), lambda i, k_ref: (k_ref[0], 0, 0)),
                      pl.BlockSpec((3, hr, cols), lambda i, k_ref: (0, 0, 0))],
            out_specs=pl.BlockSpec((hr, cols), lambda i, k_ref: (0, 0))),
    )(kidx, s1, r2)


def swap_reduced_halves(hs):
    n = len(hs)

    def body(*refs):
        srcs, outs = refs[:n], refs[n:2 * n]
        send_sems, recv_sems = refs[2 * n:]
        x, y, c = _place()
        copies = []
        for i in range(n):
            cp = pltpu.make_async_remote_copy(src_ref=srcs[i], dst_ref=outs[i], send_sem=send_sems.at[i],
                                              recv_sem=recv_sems.at[i], device_id=(x, y, 1 - c), device_id_type=MESH)
            cp.start()
            copies.append(cp)
        for cp in copies:
            cp.wait()

    return pl.pallas_call(
        body, name="swap_reduced_halves", out_shape=[jax.ShapeDtypeStruct(h.shape, h.dtype) for h in hs],
        in_specs=[ANY] * n, out_specs=[ANY] * n, scratch_shapes=_sems(n),
    )(*hs)


def adamw_halves(w, m, v, own, oth, cidx, name):
    depth, rows, cols = w.shape
    hr = rows // 2
    tr = min(hr, 256)
    nblk = hr // tr
    bc1 = 1.0 - ADAM_B1 ** ADAM_STEP
    bc2 = 1.0 - ADAM_B2 ** ADAM_STEP

    def body(c_ref, w_ref, m_ref, v_ref, own0, own1, oth0, oth1, g_ref, d_ref, nm_ref, nv_ref):
        l = pl.program_id(0)
        hi = pl.program_id(1)
        mine = jnp.where(l == 0, own0[...], own1[...])
        other = jnp.where(l == 0, oth0[...], oth1[...])
        g_v = jnp.where(hi == c_ref[0], mine, other)
        nm = ADAM_B1 * m_ref[0] + (1.0 - ADAM_B1) * g_v
        nv = ADAM_B2 * v_ref[0] + (1.0 - ADAM_B2) * (g_v * g_v)
        g_ref[0] = g_v
        nm_ref[0] = nm
        nv_ref[0] = nv
        d_ref[0] = -ADAM_LR * ((nm / bc1) / (jnp.sqrt(nv / bc2) + ADAM_EPS) + ADAM_WD * w_ref[0])

    wspec = pl.BlockSpec((1, tr, cols), lambda l, hi, b, c_ref: (l, hi * nblk + b, 0))
    gspec = pl.BlockSpec((tr, cols), lambda l, hi, b, c_ref: (b, 0))
    assert depth == 2
    return pl.pallas_call(
        body, name=name, out_shape=[jax.ShapeDtypeStruct(w.shape, F32)] * 4,
        grid_spec=pltpu.PrefetchScalarGridSpec(
            num_scalar_prefetch=1, grid=(depth, 2, nblk),
            in_specs=[wspec] * 3 + [gspec] * 4, out_specs=[wspec] * 4),
    )(cidx, w, m, v, own[0], own[1], oth[0], oth[1])


class _NS:
    def __init__(self, **kw):
        self.__dict__.update(kw)


def _prep_in(win, conv_w, conv_b, dt_bias, a_log, ssd_d, ssd_nw, qnw, kvnw, pool_w, pool_scale, n1, n2):
    winp = jnp.concatenate([win[:, 0:384], win[:, 384:1280], win[:, 1292:1548], win[:, 1548:1804], win[:, 1836:2092],
                            win[:, 1804:1836], win[:, 1280:1292], jnp.zeros((D, NP - IN_COLS), win.dtype)], axis=1)
    wbd = (jnp.eye(4, dtype=F32)[:, None, :, None] * pool_w[:, :, None, :]).reshape(POOL_DIM, POOL_DIM).astype(MXU)
    a = -jnp.exp(a_log)
    return _NS(
        winp=winp, wbd=wbd,
        cw8=jnp.pad(conv_w, ((0, 4), (0, 0))), cb=conv_b[None],
        dtb=jnp.pad(dt_bias.reshape(1, 12), ((0, 0), (DT0, 128 - DT0 - 12))),
        arow=jnp.pad(a[:, None, :], ((0, 0), (0, 7), (0, 128 - SSD_HEADS))), a=a,
        dexp=jnp.repeat(ssd_d, SSD_P)[None], ssd_nw=ssd_nw[None], qnw=qnw[None], kvnw=kvnw[None],
        pscale=pool_scale[None], n1=n1[None], n2=n2[None])


def _prep_rest(wqb, wkvb, wout, w1, w2):
    wq = jnp.pad(wqb.reshape(256, MLA_HEADS, QK_DIM), ((0, 0), (0, 0), (0, HP - QK_DIM))).reshape(256, QW)
    kv3 = wkvb.reshape(256, MLA_HEADS, 128)
    wk = jnp.pad(kv3[:, :, :64], ((0, 0), (0, 0), (0, 64))).reshape(256, QW)
    wv = jnp.pad(kv3[:, :, 64:], ((0, 0), (0, 0), (0, 64))).reshape(256, QW)
    wo = jnp.concatenate([jnp.pad(wout[384:768].reshape(MLA_HEADS, 64, D), ((0, 0), (0, 64), (0, 0))).reshape(QW, D),
                          wout[0:384], wout[768:1024]], axis=0)
    return _NS(wq=wq, wk=wk, wv=wv, wo=wo, w1=w1, w2=w2)


def _prep_layer(win, wqb, wkvb, wout, w1, w2, *small):
    lw = _prep_in(win, *small)
    lw.__dict__.update(_prep_rest(wqb, wkvb, wout, w1, w2).__dict__)
    return lw


def _by_chip_cols(a):
    return jnp.stack([a[:, k * (a.shape[1] // 4):(k + 1) * (a.shape[1] // 4)] for k in range(4)])


def _by_chip_rows(a):
    return a.reshape(4, a.shape[0] // 4, a.shape[1])


def _unprep_in(dwinp):
    return jnp.concatenate([dwinp[:, 0:384], dwinp[:, 384:1280], dwinp[:, 2080:2092], dwinp[:, 1280:1536],
                            dwinp[:, 1536:1792], dwinp[:, 2048:2080], dwinp[:, 1792:2048]], axis=1)


def _unprep_rest(dwq, dwk, dwv, dwo):
    dwqb = dwq.reshape(256, MLA_HEADS, HP)[:, :, :QK_DIM].reshape(256, MLA_HEADS * QK_DIM)
    dwkvb = jnp.concatenate([dwk.reshape(256, MLA_HEADS, HP)[:, :, :64], dwv.reshape(256, MLA_HEADS, HP)[:, :, :64]],
                            axis=2).reshape(256, MLA_HEADS * 128)
    dwout = jnp.concatenate([dwo[QW:QW + 384], dwo[0:QW].reshape(MLA_HEADS, HP, D)[:, :64].reshape(384, D),
                             dwo[QW + 384:CAT]], axis=0)
    return dwqb, dwkvb, dwout


def _rope_tables(nb, N):
    t = jnp.arange(N, dtype=F32)
    row = jnp.floor(t / GRID_W)
    col = t - row * GRID_W
    inv = jnp.asarray(10000.0 ** (-np.arange(8, dtype=np.float32) / 8), F32)
    ang = jnp.stack([row[:, None] * inv, col[:, None] * inv], axis=1)
    cs, sn = jnp.cos(ang), jnp.sin(ang)
    zero = jnp.zeros_like(sn)
    lanes = lambda first, second: jnp.stack([first, second], axis=2).reshape(N, 32)
    pad = lambda a, fill: jnp.concatenate([jnp.full((N, 64), fill, F32), a, jnp.full((N, 32), fill, F32)], axis=1)
    tabs = []
    for tab, fill in ((pad(lanes(cs, cs), 1.0), 1.0), (pad(lanes(-sn, zero), 0.0), 0.0), (pad(lanes(zero, sn), 0.0), 0.0)):
        one = jnp.concatenate([jnp.full((CTX, 128), fill, F32), tab], axis=0)
        tabs.append(jnp.tile(one, (nb, 1)))
    return tabs


def _eexp():
    e = np.zeros((128, SSD_INNER), np.float32)
    for h in range(SSD_HEADS):
        e[h, h * SSD_P:(h + 1) * SSD_P] = 1.0
    return jnp.asarray(e)


class _NoHooks:
    def __init__(self, lws):
        self.lws = lws

    def weights_in(self, l):
        return _NS(**self.lws[l].__dict__)

    def weights_rest(self, l, scan_out):
        return self.lws[l]

    def job(self, where, l, early=None):
        return None

    def done(self, where, l, out):
        pass

    def layer_grads(self, l, g):
        pass


def _layer_fwd(X, bm, l, cst, hooks):
    nb, T, bps, N = cst.nb, cst.T, cst.bps, cst.N
    lw = hooks.weights_in(l)
    h1, pz, pxbc, pqa, pkva, ppool, plast = in_proj(X, bm, lw.n1, lw.winp)
    xs, bmat, cmat, dtv = ssd_prep(pxbc, plast, lw.cw8, lw.cb, lw.dtb, bps)
    y2, hin, out = ssd_scan_fwd(xs, bmat, cmat, dtv, lw.arow, cst.eexp, nb, T, hooks.job("fwd_scan", l))
    lw.__dict__.update(hooks.weights_rest(l, out).__dict__)
    ssd = ssd_out_fwd(y2, xs, pz, lw.dexp, lw.ssd_nw)
    q, k, v, cq, ckv = mla_prep(pqa, pkva, plast, lw.qnw, lw.kvnw, lw.wq, lw.wk, lw.wv, *cst.rope)
    attn, lse = attn_fwd(q, k, v, nb, T)
    pool = pool_fwd(ppool, lw.wbd, lw.pscale, bps, N)
    x1, mix, cat = mix_fwd(X, attn, ssd, pool, bm, lw.wo)
    x2, mo, r, h2, out = mlp_fwd(x1, bm, lw.n2, lw.w1, lw.w2, hooks.job("fwd_mlp", l))
    hooks.done("fwd_mlp", l, out)
    sv = _NS(X=X, h1=h1, pz=pz, pxbc=pxbc, pqa=pqa, pkva=pkva, ppool=ppool, plast=plast, xs=xs, bmat=bmat, cmat=cmat,
             dtv=dtv, y2=y2, hin=hin, q=q, k=k, v=v, cq=cq, ckv=ckv, attn=attn, lse=lse, x1=x1, mix=mix, cat=cat, mo=mo, r=r,
             h2=h2, lw=lw)
    return x2, sv


def _layer_bwd(dx2, bm, l, sv, cst, hooks):
    nb, T, bps, N = cst.nb, cst.T, cst.bps, cst.N
    lw = sv.lw
    dx1, du, dob, part_mlp, out = mlp_bwd(dx2, sv.x1, sv.mo, sv.r, bm, lw.n2, lw.w2, lw.w1, hooks.job("bwd_mlp", l))
    hooks.done("bwd_mlp", l, out)
    dw1 = mm_tn(sv.h2, du, name="wgrad_mlp1", col_blocks=True)
    dw2 = mm_tn(sv.r, dob, square_a=True, name="wgrad_mlp2")
    dattn, dssd, dpool, dwo, part_mix = mix_bwd(dx1, sv.mix, sv.cat, bm, lw.wo)
    dppool, dwbd, part_pool = pool_bwd(sv.ppool, dpool, lw.wbd, lw.pscale, bps, N)
    dq, dk, dv = attn_bwd(sv.q, sv.k, sv.v, sv.attn, sv.lse, dattn, nb, T)
    dpqa, dpkva, dkr, dql, dkm, dvb, part_mla = mla_prep_bwd(dq, dk, dv, sv.pqa, sv.pkva, lw.qnw, lw.kvnw, lw.wq,
                                                             lw.wk, lw.wv, *cst.rope)
    dwq = mm_tn(sv.cq, dql, name="wgrad_q")
    dwk = mm_tn(sv.ckv, dkm, name="wgrad_k")
    dwv = mm_tn(sv.ckv, dvb, name="wgrad_v")
    dwqb, dwkvb, dwout = _unprep_rest(dwq, dwk, dwv, dwo)
    early = dict(w_q_b=_by_chip_cols(dwqb), w_kv_b=_by_chip_cols(dwkvb), w_out=_by_chip_rows(dwout), w_mlp1=dw1,
                 w_mlp2=_by_chip_rows(dw2))
    dyy, dz, dxs_skip, part_so = ssd_out_bwd(dssd, sv.y2, sv.xs, sv.pz, lw.dexp, lw.ssd_nw)
    dxs2, dbm2, dcm2, ddt2, da, out = ssd_scan_bwd(sv.xs, sv.bmat, sv.cmat, sv.dtv, lw.arow, cst.eexp, sv.hin, dyy,
                                                   nb, T, hooks.job("bwd_scan", l, early))
    hooks.done("bwd_scan", l, out)
    dpre, dlast_dt, part_conv = ssd_prep_bwd_a(sv.pxbc, sv.plast, lw.cw8, lw.cb, lw.dtb, dxs_skip, dxs2, dbm2, dcm2,
                                               ddt2, bps)
    dpxbc = ssd_prep_bwd_b(dpre, lw.cw8, bps)
    dx, dwinp, part_in = in_proj_bwd(dx1, sv.X, sv.h1, dz, dpxbc, dpqa, dpkva, dppool, dkr, dlast_dt, bm, lw.n1, lw.winp)

    dmod = jnp.stack([part_in[:, 0], part_in[:, 1], part_mix[:, 0], part_mlp[:, 0], part_mlp[:, 1], part_mlp[:, 2]],
                     axis=1)
    dmod = dmod.reshape(nb, bps, 6, D)
    dm_rows = jnp.concatenate([jnp.sum(dmod[:, 1:], axis=1), jnp.sum(dmod[:, 0], axis=0)[None]], axis=0)
    da_dh = jnp.sum(da[:, :, 0, :SSD_HEADS], axis=1)
    conv_parts = jnp.sum(part_conv, axis=0)
    g = _NS(
        w_in=_by_chip_cols(_unprep_in(dwinp)), dm_rows=dm_rows.reshape(3, 6 * D), **early,
        norm1_w=jnp.sum(part_in[:, 2], axis=0), norm2_w=jnp.sum(part_mlp[:, 3], axis=0),
        conv_w=conv_parts[0:4], conv_b=conv_parts[4],
        dt_bias=conv_parts[5, DT0:DT0 + 12].reshape(2, SSD_HEADS), a_log=da_dh * lw.a,
        ssd_d=jnp.sum(jnp.sum(part_so[:, 1], axis=0).reshape(SSD_HEADS, SSD_P), axis=1),
        ssd_norm_w=jnp.sum(part_so[:, 0], axis=0),
        q_a_norm_w=jnp.sum(part_mla[:, 0], axis=0), kv_a_norm_w=jnp.sum(part_mla[:, 1], axis=0),
        pool_w=jnp.stack([dwbd[i * 64:(i + 1) * 64, i * 64:(i + 1) * 64] for i in range(4)]),
        pool_scale=jnp.sum(part_pool[:, 0], axis=0))
    hooks.layer_grads(l, g)
    return dx, g


def _local_step(x, ctx, tgt, bms, lws, fw, cst, hooks=None):
    nb, N = x.shape[0], x.shape[1]
    R = nb * cst.T
    hooks = _NoHooks(lws) if hooks is None else hooks
    X = jnp.concatenate([ctx, x], axis=1).reshape(R, D)
    saved = []
    for l in range(DEPTH):
        X, sv = _layer_fwd(X, bms[l], l, cst, hooks)
        saved.append(sv)
    dX, part_fin = final_loss(X, tgt.reshape(nb * N, D), fw[None], cst.bps)
    loss = (0.5 / D) * jnp.sum(part_fin[:, 1])
    dfw = jnp.sum(part_fin[:, 0], axis=0)
    grads = [None] * DEPTH
    for l in reversed(range(DEPTH)):
        dX, grads[l] = _layer_bwd(dX, bms[l], l, saved[l], cst, hooks)
    grad_x = dX.reshape(nb, cst.T, D)[:, CTX:, :]
    return loss, grad_x, grads, dfw


def _consts(nb, N):
    T = CTX + N
    bps = T // SB
    return _NS(nb=nb, N=N, T=T, bps=bps, eexp=_eexp(), rope=_rope_tables(nb, N))


def _block_mod(modrows, cst):
    rows = []
    for b in range(cst.nb):
        rows.append(modrows[cst.nb:cst.nb + 1])
        rows.append(jnp.broadcast_to(modrows[b:b + 1], (cst.bps - 1, 6, D)))
    return jnp.pad(jnp.concatenate(rows, axis=0), ((0, 0), (0, 2), (0, 0)))


SMALL = (("norm1_w", (2, D)), ("norm2_w", (2, D)), ("conv_w", (2, 4, XBC)), ("conv_b", (2, XBC)),
         ("dt_bias", (2, 2, 6)), ("a_log", (2, 2, 6)), ("ssd_d", (2, 6)), ("ssd_norm_w", (2, 384)),
         ("q_a_norm_w", (2, 256)), ("kv_a_norm_w", (2, 256)), ("pool_w", (2, 4, 64, 64)), ("pool_scale", (2, 256)),
         ("final_norm_w", (D,)), ("mod_b", (2, 6 * D)))
SMALL_ROWS = 64
DM_ROWS = 48


def _pack_small(vals):
    flat = jnp.concatenate([vals[n].reshape(-1) for n, _ in SMALL])
    return jnp.pad(flat, (0, SMALL_ROWS * D - flat.shape[0])).reshape(SMALL_ROWS, D)


def _unpack_small(p):
    flat = p.reshape(-1)
    out, off = {}, 0
    for n, shp in SMALL:
        size = int(np.prod(shp))
        out[n] = flat[off:off + size].reshape(shp)
        off += size
    return out


def cctx_grad(parts, c_ctx):
    def body(p_ref, c_ref, o_ref):
        acc = ((p_ref[0] + p_ref[1]) + p_ref[2]) + p_ref[3]
        v = c_ref[...]
        sig = _sigmoid(v)
        o_ref[...] = acc * (sig * (1.0 + v * (1.0 - sig)))

    return pl.pallas_call(
        body, name="cctx_grad", out_shape=jax.ShapeDtypeStruct((8, D), F32),
        in_specs=[_fullspec((4, 8, D)), _fullspec((1, D))], out_specs=_fullspec((8, D)), grid=(1,),
    )(parts, c_ctx)


def kernel(x, c, ctx, c_ctx, mod_w, mod_b, norm1_w, norm2_w, w_in, conv_w, conv_b, dt_bias, a_log, ssd_d, ssd_norm_w, q_a_norm_w, w_q_b, kv_a_norm_w, w_kv_b, pool_w, pool_scale, w_out, w_mlp1, w_mlp2, final_norm_w, loss_target, m_c_ctx, m_mod_w, m_mod_b, m_norm1_w, m_norm2_w, m_w_in, m_conv_w, m_conv_b, m_dt_bias, m_a_log, m_ssd_d, m_ssd_norm_w, m_q_a_norm_w, m_w_q_b, m_kv_a_norm_w, m_w_kv_b, m_pool_w, m_pool_scale, m_w_out, m_w_mlp1, m_w_mlp2, m_final_norm_w, v_c_ctx, v_mod_w, v_mod_b, v_norm1_w, v_norm2_w, v_w_in, v_conv_w, v_conv_b, v_dt_bias, v_a_log, v_ssd_d, v_ssd_norm_w, v_q_a_norm_w, v_w_q_b, v_kv_a_norm_w, v_w_kv_b, v_pool_w, v_pool_scale, v_w_out, v_w_mlp1, v_w_mlp2, v_final_norm_w):
    nb, N = x.shape[0], x.shape[1]
    cst = _consts(nb, N)
    xi, yi, ci = _place()
    me = 4 * xi + 2 * yi + ci
    kchip = 2 * xi + yi
    mcols = mod_w.shape[2]
    cshard = conv_w.shape[2]

    blk = jnp.zeros((16, D), F32).at[0:nb].set(c).at[8:16, 0:cshard].set(conv_w.reshape(8, cshard))
    g1 = allgather_small(blk, "gather_cond")
    cond = jnp.concatenate([g1[:, 0:nb].reshape(NDEV * nb, D), c_ctx[None],
                            jnp.zeros((MODR - NDEV * nb - 1, D), F32)], axis=0)
    conv_full = [jnp.concatenate([g1[2 * k, 8 + 4 * l:12 + 4 * l, 0:cshard] for k in range(4)], axis=1)
                 for l in range(DEPTH)]

    mb = [lax.dynamic_slice_in_dim(mod_b[l], kchip * mcols, mcols)[None] for l in range(DEPTH)]
    ms = jnp.concatenate([mod_fwd(cond, mod_w[l], mb[l]) for l in range(DEPTH)], axis=0)
    g2 = allgather_small(ms, "gather_mod")
    bms = []
    for l in range(DEPTH):
        m_all = jnp.concatenate([g2[2 * k, MODR * l:MODR * (l + 1)] for k in range(4)], axis=1)
        mine = jnp.concatenate([lax.dynamic_slice_in_dim(m_all, nb * me, nb), m_all[NDEV * nb:NDEV * nb + 1]], axis=0)
        bms.append(_block_mod(mine.reshape(nb + 1, 6, D), cst))

    assert DEPTH == 2
    big = (w_in, w_q_b, w_kv_b, w_out, w_mlp1, w_mlp2)
    names = ("w_in", "w_q_b", "w_kv_b", "w_out", "w_mlp1", "w_mlp2")
    concat_axis = dict(w_in=1, w_q_b=1, w_kv_b=1, w_out=0, w_mlp1=1, w_mlp2=0)
    cidx = jnp.reshape(ci, (1,)).astype(jnp.int32)
    kidx = jnp.reshape(kchip, (1,)).astype(jnp.int32)
    shards = [{n: a[l].astype(MXU) for n, a in zip(names, big)} for l in range(DEPTH)]

    def core_sums(gs):
        ns = list(gs)
        got = swap_core_halves([gs[n] for n in ns])
        return {n: add_half(gs[n], r, cidx, "add_half_" + n) for n, r in zip(ns, got)}

    class Hooks:
        gathered = [dict(w_in=run_job(gather_job([shards[0]["w_in"]]), "gather_w_in")[0]), {}]
        core_sum = [{}, {}]
        received = [{}, {}]

        def whole(self, l, n):
            return jnp.concatenate([jnp.where(kchip == k, shards[l][n], self.gathered[l][n][k]) for k in range(4)],
                                   axis=concat_axis[n])

        def weights_in(self, l):
            return _prep_in(self.whole(l, "w_in"), conv_full[l], conv_b[l], dt_bias[l], a_log[l], ssd_d[l], ssd_norm_w[l],
                            q_a_norm_w[l], kv_a_norm_w[l], pool_w[l], pool_scale[l], norm1_w[l], norm2_w[l])

        def weights_rest(self, l, scan_out):
            if l == 0:
                self.gathered[0].update(zip(names[1:], scan_out))
            return _prep_rest(*[self.whole(l, n) for n in names[1:]])

        def job(self, where, l, early=None):
            if l != 0:
                return None
            if where == "fwd_scan":
                return gather_job([shards[0][n] for n in names[1:]])
            if where == "fwd_mlp":
                return gather_job([shards[1][n] for n in names])
            if where == "bwd_mlp":
                return chip_swap_job([self.core_sum[1][n][1] for n in names])
            self.core_sum[0].update(core_sums(early))
            return chip_swap_job([self.core_sum[0][n][1] for n in names[1:]])

        def done(self, where, l, out):
            if l != 0:
                return
            if where == "fwd_mlp":
                self.gathered[1].update(zip(names, out))
            elif where == "bwd_mlp":
                self.received[1].update(zip(names, out))
            elif where == "bwd_scan":
                self.received[0].update(zip(names[1:], out))

        def layer_grads(self, l, g):
            if l == 1:
                self.core_sum[1] = core_sums({n: getattr(g, n) for n in names})
            else:
                self.core_sum[0].update(core_sums(dict(w_in=g.w_in)))
                self.received[0]["w_in"] = run_job(chip_swap_job([self.core_sum[0]["w_in"][1]]), "swap_w_in")[0]

    hooks = Hooks()
    loss_part, grad_x, grads, dfw = _local_step(x, ctx, loss_target, bms, None, final_norm_w, cst, hooks)
    loss = lax.psum(loss_part, ("x", "y", "c"))
    g_own = [sum_parts(hooks.core_sum[l][n][0], hooks.received[l][n], kidx, "sum_parts_" + n)
             for n in names for l in range(DEPTH)]
    g_oth = swap_reduced_halves(g_own)

    small = {n: jnp.stack([getattr(grads[l], n) for l in range(DEPTH)]) for n, _ in SMALL if n not in ("final_norm_w", "mod_b")}
    small["final_norm_w"] = dfw
    small["mod_b"] = jnp.stack([jnp.sum(grads[l].dm_rows, axis=0) for l in range(DEPTH)])
    dm = jnp.pad(jnp.concatenate([grads[l].dm_rows for l in range(DEPTH)], axis=0), ((0, 8 - 3 * DEPTH), (0, 0)))
    g3 = allgather_small(jnp.concatenate([_pack_small(small), dm.reshape(DM_ROWS, D)], axis=0), "gather_small")
    tot = sum_leading(g3, "sum_small")
    gsmall = _unpack_small(tot[0:SMALL_ROWS])
    ctx_sum = tot[SMALL_ROWS:].reshape(8, 6 * D)
    dm_dev = g3[:, SMALL_ROWS:].reshape(NDEV, 8, 6 * D)
    g_mod_w, dpart = [], jnp.zeros((8, D), F32)
    for l in range(DEPTH):
        dm_all = jnp.concatenate([dm_dev[:, 3 * l:3 * l + nb].reshape(NDEV * nb, 6 * D), ctx_sum[3 * l + nb:3 * l + nb + 1],
                                  jnp.zeros((MODR - NDEV * nb - 1, 6 * D), F32)], axis=0)
        g_mod_w.append(mod_wgrad(cond, lax.dynamic_slice_in_dim(dm_all, kchip * mcols, mcols, axis=1)))
        dctx = jnp.pad(lax.dynamic_slice_in_dim(ctx_sum[3 * l + nb:3 * l + nb + 1], kchip * mcols, mcols, axis=1), ((0, 7), (0, 0)))
        dpart = dpart + mod_dgrad(dctx, mod_w[l])
    g4 = allgather_small(dpart, "gather_cctx")
    g_c_ctx = cctx_grad(g4[0::2], c_ctx[None])[0]

    res = {}
    moments = ((m_w_in, v_w_in), (m_w_q_b, v_w_q_b), (m_w_kv_b, v_w_kv_b), (m_w_out, v_w_out), (m_w_mlp1, v_w_mlp1),
               (m_w_mlp2, v_w_mlp2))
    for i, (n, w, (m, v)) in enumerate(zip(names, big, moments)):
        res[n] = tuple(adamw_halves(w, m, v, g_own[DEPTH * i:DEPTH * (i + 1)], g_oth[DEPTH * i:DEPTH * (i + 1)], cidx,
                                    "adamw_" + n))
    g_mw = jnp.stack(g_mod_w)
    r_mw = adamw(mod_w.reshape(-1, mcols), g_mw.reshape(-1, mcols), m_mod_w.reshape(-1, mcols),
                 v_mod_w.reshape(-1, mcols), name="adamw_mod_w")
    res["mod_w"] = (g_mw,) + tuple(a.reshape(mod_w.shape) for a in r_mw)

    given = dict(norm1_w=(norm1_w, m_norm1_w, v_norm1_w), norm2_w=(norm2_w, m_norm2_w, v_norm2_w),
                 conv_b=(conv_b, m_conv_b, v_conv_b), dt_bias=(dt_bias, m_dt_bias, v_dt_bias),
                 a_log=(a_log, m_a_log, v_a_log), ssd_d=(ssd_d, m_ssd_d, v_ssd_d),
                 ssd_norm_w=(ssd_norm_w, m_ssd_norm_w, v_ssd_norm_w), q_a_norm_w=(q_a_norm_w, m_q_a_norm_w, v_q_a_norm_w),
                 kv_a_norm_w=(kv_a_norm_w, m_kv_a_norm_w, v_kv_a_norm_w), pool_w=(pool_w, m_pool_w, v_pool_w),
                 pool_scale=(pool_scale, m_pool_scale, v_pool_scale),
                 final_norm_w=(final_norm_w, m_final_norm_w, v_final_norm_w), mod_b=(mod_b, m_mod_b, v_mod_b))
    zero_cw = jnp.zeros((2, 4, XBC), F32)
    packs = [_pack_small({n: (given[n][i] if n in given else zero_cw) for n, _ in SMALL}) for i in range(3)]
    r_small = [_unpack_small(a) for a in adamw(packs[0], tot[0:SMALL_ROWS], packs[1], packs[2], name="adamw_small")]
    for n in given:
        res[n] = (gsmall[n], r_small[0][n], r_small[1][n], r_small[2][n])

    g_cw = lax.dynamic_slice_in_dim(gsmall["conv_w"], kchip * cshard, cshard, axis=2)
    padcw = lambda a: jnp.pad(a.reshape(8, cshard), ((0, 0), (0, 256 - cshard)))
    r_cw = adamw(padcw(conv_w), padcw(g_cw), padcw(m_conv_w), padcw(v_conv_w), name="adamw_conv_w")
    res["conv_w"] = (g_cw,) + tuple(a[:, 0:cshard].reshape(conv_w.shape) for a in r_cw)
    r_cc = adamw(c_ctx.reshape(8, 128), g_c_ctx.reshape(8, 128), m_c_ctx.reshape(8, 128), v_c_ctx.reshape(8, 128),
                 name="adamw_c_ctx")
    res["c_ctx"] = (g_c_ctx,) + tuple(a.reshape(D) for a in r_cc)

    order = ("c_ctx", "mod_w", "mod_b", "norm1_w", "norm2_w", "w_in", "conv_w", "conv_b", "dt_bias", "a_log", "ssd_d",
             "ssd_norm_w", "q_a_norm_w", "w_q_b", "kv_a_norm_w", "w_kv_b", "pool_w", "pool_scale", "w_out", "w_mlp1",
             "w_mlp2", "final_norm_w")
    return (loss, grad_x) + tuple(res[n][i] for i in range(4) for n in order)
```

```python
import functools
import math

import numpy as np
import jax
import jax.numpy as jnp
from jax import lax
from jax.experimental import pallas as pl
from jax.experimental.pallas import tpu as pltpu

F32 = jnp.float32
BF16 = jnp.bfloat16
MXU = jnp.bfloat16

D = 1024
DEPTH = 2
GRID_W = 64
CTX = 256
EPS = 1e-6
SSD_HEADS = 6
SSD_P = 64
SSD_INNER = 384
SSD_N = 128
CHUNK = 128
XBC = 896
MLA_HEADS = 6
QK_NOPE = 64
QK_ROPE = 32
QK_DIM = 96
HP = 128
QW = MLA_HEADS * HP
POOL_DIM = 256
D_FF = 4096
FF_BLK = 1024
IN_COLS = 2092
NP = 2176
P_SPLITS = (384, 896, 256, 256, 256, 128)
DT0 = 32
CAT = QW + SSD_INNER + POOL_DIM

SB = 256
TM = 512
HALO = 8

ADAM_LR = 0.001
ADAM_B1 = 0.9
ADAM_B2 = 0.999
ADAM_EPS = 1e-08
ADAM_WD = 0.01
ADAM_STEP = 10

NT = (((1,), (1,)), ((), ()))
TN = (((0,), (0,)), ((), ()))


def _cp(vmem_mb=None):
    if vmem_mb is None:
        return pltpu.CompilerParams()
    return pltpu.CompilerParams(vmem_limit_bytes=vmem_mb << 20)


def _dot(a, b):
    return jnp.dot(a, b, preferred_element_type=F32)


def _dotg(a, b, dims):
    return lax.dot_general(a, b, dims, preferred_element_type=F32)


def _dot_hi(a, b, dims=None, sel_first=False):
    dims = (((1,), (0,)), ((), ())) if dims is None else dims
    v, s = (b, a) if sel_first else (a, b)
    hi = v.astype(BF16)
    lo = (v - hi.astype(F32)).astype(BF16)
    s = s.astype(BF16)
    if sel_first:
        return _dotg(s, hi, dims) + _dotg(s, lo, dims)
    return _dotg(hi, s, dims) + _dotg(lo, s, dims)


def _rms_hat(x):
    rstd = lax.rsqrt(jnp.mean(x * x, axis=-1, keepdims=True) + EPS)
    return x * rstd, rstd


def _rms_bwd(dn, xhat, rstd, w):
    dxhat = dn * w
    dx = rstd * (dxhat - xhat * jnp.mean(dxhat * xhat, axis=-1, keepdims=True))
    return dx, jnp.sum(dn * xhat, axis=0, keepdims=True)


def _sigmoid(z):
    return 1.0 / (1.0 + jnp.exp(-z))


def _colsum(a):
    return jnp.sum(a, axis=0, keepdims=True)


def _rowspec(cols, tm=TM):
    return pl.BlockSpec((tm, cols), lambda i: (i, 0))


def _fullspec(shape):
    n = len(shape)
    return pl.BlockSpec(shape, lambda *_: (0,) * n)


def _resident(shape):
    n = len(shape)
    return pl.BlockSpec(shape, lambda *_: (0,) * n, pipeline_mode=pl.Buffered(1))


def _halo_specs(cols, nrows):
    per = SB // HALO
    last = nrows // HALO - 1
    prev = pl.BlockSpec((HALO, cols), lambda i: (jnp.maximum(i * per - 1, 0), 0))
    nxt = pl.BlockSpec((HALO, cols), lambda i: (jnp.minimum((i + 1) * per, last), 0))
    return prev, nxt


def _ext_rows(cur, prev, nxt, i, blocks_per_sample):
    j = i % blocks_per_sample
    first = jnp.logical_or(j == 0, j == 1)
    last = jnp.logical_or(j == 0, j == blocks_per_sample - 1)
    p = jnp.where(first, 0.0, prev)
    n = jnp.where(last, 0.0, nxt)
    return jnp.concatenate([p, cur, n], axis=0)


def _shift(ext, s):
    n = ext.shape[0]
    return pltpu.roll(ext, (-s) % n, axis=0)[HALO:HALO + SB, :]


def in_proj(x, bm, nw, w):
    R = x.shape[0]

    def body(x_ref, bm_ref, nw_ref, w_ref, h_ref, *outs):
        for s in range(TM // SB):
            rows = slice(s * SB, (s + 1) * SB)
            xhat, _ = _rms_hat(x_ref[rows, :])
            h = xhat * nw_ref[...] * (1.0 + bm_ref[s, 1:2, :]) + bm_ref[s, 0:1, :]
            h_ref[rows, :] = h.astype(h_ref.dtype)
        p = _dot(h_ref[...], w_ref[...])
        off = 0
        for o, n in zip(outs, P_SPLITS):
            o[...] = p[:, off:off + n]
            off += n

    return pl.pallas_call(
        body, name="in_proj", grid=(R // TM,),
        in_specs=[_rowspec(D), pl.BlockSpec((TM // SB, 8, D), lambda i: (i, 0, 0)), _fullspec((1, D)),
                  _fullspec((D, NP))],
        out_specs=[_rowspec(D)] + [_rowspec(n) for n in P_SPLITS],
        out_shape=[jax.ShapeDtypeStruct((R, D), MXU)] + [jax.ShapeDtypeStruct((R, n), F32) for n in P_SPLITS],
        compiler_params=_cp(56),
    )(x, bm, nw, w)


def in_proj_bwd(dx1, x, h, dz, dxbc, dqa, dkva, dpool, dkr, ddt, bm, nw, w):
    R = x.shape[0]

    def body(dx1_ref, x_ref, h_ref, dz_ref, dxbc_ref, dqa_ref, dkva_ref, dpool_ref, dkr_ref, ddt_ref, bm_ref, nw_ref,
             w_ref, dx_ref, dw_ref, part_ref, dp_ref):
        @pl.when(pl.program_id(0) == 0)
        def _():
            dw_ref[...] = jnp.zeros_like(dw_ref)

        dp_ref[:, 0:384] = dz_ref[...].astype(dp_ref.dtype)
        dp_ref[:, 384:1280] = dxbc_ref[...].astype(dp_ref.dtype)
        dp_ref[:, 1280:1536] = dqa_ref[...].astype(dp_ref.dtype)
        dp_ref[:, 1536:1792] = dkva_ref[...].astype(dp_ref.dtype)
        dp_ref[:, 1792:2048] = dpool_ref[...].astype(dp_ref.dtype)
        dp_ref[:, 2048:2176] = (dkr_ref[...] + ddt_ref[...]).astype(dp_ref.dtype)
        dw_ref[...] += _dotg(h_ref[...], dp_ref[...], TN)
        dh = _dotg(dp_ref[...], w_ref[...], NT)
        w = nw_ref[...]
        for s in range(TM // SB):
            rows = slice(s * SB, (s + 1) * SB)
            xhat, rstd = _rms_hat(x_ref[rows, :])
            dhs = dh[rows, :]
            sc1 = 1.0 + bm_ref[s, 1:2, :]
            dx, dnw = _rms_bwd(dhs * sc1, xhat, rstd, w)
            dx_ref[rows, :] = dx1_ref[rows, :] + dx
            part_ref[s] = jnp.concatenate(
                [_colsum(dhs), _colsum(dhs * xhat * w), dnw, jnp.zeros((5, D), F32)], axis=0)

    return pl.pallas_call(
        body, name="in_proj_bwd", grid=(R // TM,),
        in_specs=[_rowspec(D), _rowspec(D), _rowspec(D), _rowspec(384), _rowspec(896), _rowspec(256), _rowspec(256),
                  _rowspec(256), _rowspec(128), _rowspec(128),
                  pl.BlockSpec((TM // SB, 8, D), lambda i: (i, 0, 0)), _fullspec((1, D)), _resident((D, NP))],
        out_specs=[_rowspec(D), _fullspec((D, NP)), pl.BlockSpec((TM // SB, 8, D), lambda i: (i, 0, 0))],
        out_shape=[jax.ShapeDtypeStruct((R, D), F32), jax.ShapeDtypeStruct((D, NP), F32),
                   jax.ShapeDtypeStruct((R // SB, 8, D), F32)],
        scratch_shapes=[pltpu.VMEM((TM, NP), MXU)],
        compiler_params=_cp(56),
    )(dx1, x, h, dz, dxbc, dqa, dkva, dpool, dkr, ddt, bm, nw, w)


def mix_fwd(x, attn, ssd, pool, bm, wo):
    R = x.shape[0]

    def body(x_ref, a_ref, s_ref, p_ref, bm_ref, wo_ref, x1_ref, mix_ref, cat_ref):
        cat_ref[:, 0:QW] = a_ref[...].astype(cat_ref.dtype)
        cat_ref[:, QW:QW + SSD_INNER] = s_ref[...].astype(cat_ref.dtype)
        cat_ref[:, QW + SSD_INNER:CAT] = p_ref[...].astype(cat_ref.dtype)
        mix = _dot(cat_ref[...], wo_ref[...])
        mix_ref[...] = mix
        for s in range(TM // SB):
            rows = slice(s * SB, (s + 1) * SB)
            x1_ref[rows, :] = x_ref[rows, :] + bm_ref[s, 2:3, :] * mix[rows, :]

    return pl.pallas_call(
        body, name="mix_fwd", grid=(R // TM,),
        in_specs=[_rowspec(D), _rowspec(QW), _rowspec(SSD_INNER), _rowspec(POOL_DIM),
                  pl.BlockSpec((TM // SB, 8, D), lambda i: (i, 0, 0)), _fullspec((CAT, D))],
        out_specs=[_rowspec(D), _rowspec(D), _rowspec(CAT)],
        out_shape=[jax.ShapeDtypeStruct((R, D), F32), jax.ShapeDtypeStruct((R, D), F32),
                   jax.ShapeDtypeStruct((R, CAT), MXU)],
        compiler_params=_cp(48),
    )(x, attn, ssd, pool, bm, wo)


def mix_bwd(dx1, mix, cat, bm, wo):
    R = dx1.shape[0]

    def body(dx1_ref, mix_ref, cat_ref, bm_ref, wo_ref, da_ref, ds_ref, dpl_ref, dw_ref, part_ref, dmb_ref):
        @pl.when(pl.program_id(0) == 0)
        def _():
            dw_ref[...] = jnp.zeros_like(dw_ref)

        for s in range(TM // SB):
            rows = slice(s * SB, (s + 1) * SB)
            d = dx1_ref[rows, :]
            dmb_ref[rows, :] = (d * bm_ref[s, 2:3, :]).astype(dmb_ref.dtype)
            part_ref[s] = jnp.concatenate([_colsum(d * mix_ref[rows, :]), jnp.zeros((7, D), F32)], axis=0)
        dw_ref[...] += _dotg(cat_ref[...], dmb_ref[...], TN)
        dcat = _dotg(dmb_ref[...], wo_ref[...], NT)
        da_ref[...] = dcat[:, 0:QW]
        ds_ref[...] = dcat[:, QW:QW + SSD_INNER]
        dpl_ref[...] = dcat[:, QW + SSD_INNER:CAT]

    return pl.pallas_call(
        body, name="mix_bwd", grid=(R // TM,),
        in_specs=[_rowspec(D), _rowspec(D), _rowspec(CAT), pl.BlockSpec((TM // SB, 8, D), lambda i: (i, 0, 0)),
                  _resident((CAT, D))],
        out_specs=[_rowspec(QW), _rowspec(SSD_INNER), _rowspec(POOL_DIM), _fullspec((CAT, D)),
                   pl.BlockSpec((TM // SB, 8, D), lambda i: (i, 0, 0))],
        out_shape=[jax.ShapeDtypeStruct((R, QW), F32), jax.ShapeDtypeStruct((R, SSD_INNER), F32),
                   jax.ShapeDtypeStruct((R, POOL_DIM), F32), jax.ShapeDtypeStruct((CAT, D), F32),
                   jax.ShapeDtypeStruct((R // SB, 8, D), F32)],
        scratch_shapes=[pltpu.VMEM((TM, D), MXU)],
        compiler_params=_cp(48),
    )(dx1, mix, cat, bm, wo)


def mlp_fwd(x1, bm, nw, w1, w2, side=None):
    R = x1.shape[0]

    def body(x1_ref, bm_ref, nw_ref, w1_ref, w2_ref, x2_ref, mo_ref, r_ref, h2_ref):
        for s in range(TM // SB):
            rows = slice(s * SB, (s + 1) * SB)
            xhat, _ = _rms_hat(x1_ref[rows, :])
            h = xhat * nw_ref[...] * (1.0 + bm_ref[s, 4:5, :]) + bm_ref[s, 3:4, :]
            h2_ref[rows, :] = h.astype(h2_ref.dtype)
        for j in range(D_FF // FF_BLK):
            cols = slice(j * FF_BLK, (j + 1) * FF_BLK)
            r = jnp.maximum(_dot(h2_ref[...], w1_ref[:, cols]), 0.0)
            r_ref[:, cols] = r.astype(r_ref.dtype)
            d = _dot((r * r).astype(MXU), w2_ref[cols, :])
            if j == 0:
                mo_ref[...] = d
            else:
                mo_ref[...] += d
        for s in range(TM // SB):
            rows = slice(s * SB, (s + 1) * SB)
            x2_ref[rows, :] = x1_ref[rows, :] + bm_ref[s, 5:6, :] * mo_ref[rows, :]

    grid = (R // TM,)
    body, side_in, side_out, side_shapes, side_scratch, side_args = _side_wrap(body, 5, 4, 0, side, grid)
    outs = pl.pallas_call(
        body, name="mlp_fwd" if side is None else "mlp_fwd_comm", grid=grid,
        in_specs=[_rowspec(D), pl.BlockSpec((TM // SB, 8, D), lambda i: (i, 0, 0)), _fullspec((1, D)),
                  _resident((D, D_FF)), _resident((D_FF, D))] + side_in,
        out_specs=[_rowspec(D), _rowspec(D), _rowspec(D_FF), _rowspec(D)] + side_out,
        out_shape=[jax.ShapeDtypeStruct((R, D), F32), jax.ShapeDtypeStruct((R, D), F32),
                   jax.ShapeDtypeStruct((R, D_FF), BF16), jax.ShapeDtypeStruct((R, D), MXU)] + side_shapes,
        scratch_shapes=side_scratch,
        compiler_params=_cp(56),
    )(x1, bm, nw, w1, w2, *side_args)
    return tuple(outs[:4]) + (list(outs[4:]),)


def mlp_bwd(dx2, x1, mo, r, bm, nw, w2, w1, side=None):
    R = x1.shape[0]

    def body(dx2_ref, x1_ref, mo_ref, r_ref, bm_ref, nw_ref, w2_ref, w1_ref, dx1_ref, du_ref, dob_ref, part_ref,
             acc_ref):
        for s in range(TM // SB):
            rows = slice(s * SB, (s + 1) * SB)
            dob_ref[rows, :] = (dx2_ref[rows, :] * bm_ref[s, 5:6, :]).astype(dob_ref.dtype)
        for j in range(D_FF // FF_BLK):
            cols = slice(j * FF_BLK, (j + 1) * FF_BLK)
            du = _dotg(dob_ref[...], w2_ref[cols, :], NT) * (2.0 * r_ref[:, cols].astype(F32))
            du_ref[:, cols] = du.astype(du_ref.dtype)
            d = _dotg(du_ref[:, cols], w1_ref[:, cols], NT)
            if j == 0:
                acc_ref[...] = d
            else:
                acc_ref[...] += d
        w = nw_ref[...]
        for s in range(TM // SB):
            rows = slice(s * SB, (s + 1) * SB)
            xhat, rstd = _rms_hat(x1_ref[rows, :])
            dh = acc_ref[rows, :]
            dx, dnw = _rms_bwd(dh * (1.0 + bm_ref[s, 4:5, :]), xhat, rstd, w)
            d2 = dx2_ref[rows, :]
            dx1_ref[rows, :] = d2 + dx
            part_ref[s] = jnp.concatenate(
                [_colsum(dh), _colsum(dh * xhat * w), _colsum(d2 * mo_ref[rows, :]), dnw,
                 jnp.zeros((4, D), F32)], axis=0)

    grid = (R // TM,)
    body, side_in, side_out, side_shapes, side_scratch, side_args = _side_wrap(body, 8, 4, 1, side, grid)
    outs = pl.pallas_call(
        body, name="mlp_bwd" if side is None else "mlp_bwd_comm", grid=grid,
        in_specs=[_rowspec(D), _rowspec(D), _rowspec(D), _rowspec(D_FF),
                  pl.BlockSpec((TM // SB, 8, D), lambda i: (i, 0, 0)), _fullspec((1, D)),
                  _resident((D_FF, D)), _resident((D, D_FF))] + side_in,
        out_specs=[_rowspec(D), _rowspec(D_FF), _rowspec(D), pl.BlockSpec((TM // SB, 8, D), lambda i: (i, 0, 0))]
                  + side_out,
        out_shape=[jax.ShapeDtypeStruct((R, D), F32), jax.ShapeDtypeStruct((R, D_FF), MXU),
                   jax.ShapeDtypeStruct((R, D), MXU), jax.ShapeDtypeStruct((R // SB, 8, D), F32)] + side_shapes,
        scratch_shapes=[pltpu.VMEM((TM, D), F32)] + side_scratch,
        compiler_params=_cp(56),
    )(dx2, x1, mo, r, bm, nw, w2, w1, *side_args)
    return tuple(outs[:4]) + (list(outs[4:]),)


def mm_tn(a, b, square_a=False, name="mm_tn", col_blocks=False):
    R, M = a.shape
    N = b.shape[1]
    tm = M if M <= 1408 else 1024
    tn = N if N <= 2176 else 1024
    tk = next((c for c in ((2176, 1088, 512) if tm + tn <= 2048 else (1088, 512)) if R % c == 0), R)
    assert not col_blocks or tm == M

    def body(a_ref, b_ref, o_ref):
        @pl.when(pl.program_id(2) == 0)
        def _():
            o_ref[...] = jnp.zeros_like(o_ref)

        av = a_ref[...]
        if square_a:
            av = av.astype(F32)
            av = (av * av).astype(MXU)
        prod = _dotg(av.astype(MXU), b_ref[...].astype(MXU), TN)
        if col_blocks:
            o_ref[0] += prod
        else:
            o_ref[...] += prod

    if col_blocks:
        out_spec = pl.BlockSpec((1, tm, tn), lambda i, j, k: (j, 0, 0))
        out_shape = jax.ShapeDtypeStruct((N // tn, M, tn), F32)
    else:
        out_spec = pl.BlockSpec((tm, tn), lambda i, j, k: (i, j))
        out_shape = jax.ShapeDtypeStruct((M, N), F32)
    return pl.pallas_call(
        body, name=name, grid=(M // tm, N // tn, R // tk),
        in_specs=[pl.BlockSpec((tk, tm), lambda i, j, k: (k, i)), pl.BlockSpec((tk, tn), lambda i, j, k: (k, j))],
        out_specs=out_spec, out_shape=out_shape,
        compiler_params=_cp(48),
    )(a, b)


def final_loss(x, tgt, fw, blocks_per_sample):
    R = x.shape[0]
    nxb = blocks_per_sample - 1

    def body(x_ref, t_ref, fw_ref, dx_ref, part_ref):
        i = pl.program_id(0)
        is_ctx = (i % blocks_per_sample) == 0
        xhat, rstd = _rms_hat(x_ref[...])
        w = fw_ref[...]
        err = xhat * w - t_ref[...]
        dx, dfw = _rms_bwd(err * (1.0 / D), xhat, rstd, w)
        keep = jnp.where(is_ctx, 0.0, 1.0)
        dx_ref[...] = dx * keep
        part_ref[0] = jnp.concatenate([dfw * keep, _colsum(err * err) * keep, jnp.zeros((6, D), F32)], axis=0)

    def tmap(i):
        return ((i // blocks_per_sample) * nxb + jnp.maximum(i % blocks_per_sample - 1, 0), 0)

    return pl.pallas_call(
        body, name="final_loss", grid=(R // SB,),
        in_specs=[_rowspec(D, SB), pl.BlockSpec((SB, D), tmap), _fullspec((1, D))],
        out_specs=[_rowspec(D, SB), pl.BlockSpec((1, 8, D), lambda i: (i, 0, 0))],
        out_shape=[jax.ShapeDtypeStruct((R, D), F32), jax.ShapeDtypeStruct((R // SB, 8, D), F32)],
    )(x, tgt, fw)


def _softplus(v):
    return jnp.maximum(v, 0.0) + jnp.log(1.0 + jnp.exp(-jnp.abs(v)))


def _conv_taps(ext):
    return [_shift(ext, k - 1) for k in range(4)]


def _conv_out(taps, cw_ref, cb_ref):
    return (cb_ref[...] + cw_ref[0:1, :] * taps[0] + cw_ref[1:2, :] * taps[1] + cw_ref[2:3, :] * taps[2]
            + cw_ref[3:4, :] * taps[3])


def _dt_dir(v, d):
    lane = lax.broadcasted_iota(jnp.int32, v.shape, 1)
    return jnp.where(lane < SSD_HEADS, pltpu.roll(v, (128 - DT0 - SSD_HEADS * d) % 128, axis=1), 0.0)


def ssd_prep(pxbc, plast, cw, cb, dtb, blocks_per_sample):
    R = pxbc.shape[0]
    prev, nxt = _halo_specs(XBC, R)

    def body(cur_ref, prev_ref, nxt_ref, pl_ref, cw_ref, cb_ref, dtb_ref, xs_ref, bm_ref, cm_ref, dt_ref):
        i = pl.program_id(0)
        ext = _ext_rows(cur_ref[...], prev_ref[...], nxt_ref[...], i, blocks_per_sample)
        co = _conv_out(_conv_taps(ext), cw_ref, cb_ref)
        a = co * _sigmoid(co)
        xs_ref[...] = a[:, 0:384]
        bm_ref[...] = a[:, 384:640]
        cm_ref[...] = a[:, 640:896]
        sp = _softplus(pl_ref[...] + dtb_ref[...])
        dt_ref[0] = _dt_dir(sp, 0)
        dt_ref[1] = _dt_dir(sp, 1)

    return pl.pallas_call(
        body, name="ssd_prep", grid=(R // SB,),
        in_specs=[_rowspec(XBC, SB), prev, nxt, _rowspec(128, SB), _fullspec((8, XBC)), _fullspec((1, XBC)),
                  _fullspec((1, 128))],
        out_specs=[_rowspec(384, SB), _rowspec(256, SB), _rowspec(256, SB),
                   pl.BlockSpec((2, SB, 128), lambda i: (0, i, 0))],
        out_shape=[jax.ShapeDtypeStruct((R, 384), F32), jax.ShapeDtypeStruct((R, 256), F32),
                   jax.ShapeDtypeStruct((R, 256), F32), jax.ShapeDtypeStruct((2, R, 128), F32)],
    )(pxbc, pxbc, pxbc, plast, cw, cb, dtb)


def _chunk_index(d, s, nc):
    nctx = CTX // CHUNK
    back = jnp.where(s < nctx, nctx - 1 - s, nc + nctx - 1 - s)
    return jnp.where(d == 0, s, back)


def _scan_common(d, dt, arow, eexp, xs):
    ii = lax.broadcasted_iota(jnp.int32, (CHUNK, CHUNK), 0)
    jj = lax.broadcasted_iota(jnp.int32, (CHUNK, CHUNK), 1)
    mask = ((ii - jj) * (1 - 2 * d)) >= 0
    adt = dt * arow
    tmat = jnp.where(mask, 1.0, 0.0)
    cs = _dot_hi(tmat, adt, sel_first=True)
    tot = _colsum(adt)
    dtx = _dot_hi(dt, eexp)
    xt = xs * dtx
    ecs = jnp.exp(cs)
    ecx = _dot_hi(ecs, eexp)
    dte = jnp.exp(tot - cs)
    dtex = _dot_hi(dte, eexp)
    etot = jnp.exp(tot)
    etx = _dot_hi(jnp.broadcast_to(etot, (8, 128)), eexp)[0:1, :]
    return mask, tmat, adt, cs, tot, dtx, xt, ecs, ecx, dte, dtex, etot, etx


def _decay_matrix(mask, cs, cst, h):
    return jnp.exp(jnp.where(mask, cs[:, h:h + 1] - cst[h:h + 1, :], -1e30))


def _side_wrap(body, n_in, n_out, n_scratch, side, grid):
    if side is None:
        return body, [], [], [], [], []
    ni, no = len(side.ins), len(side.out_shapes)

    def wrapped(*refs):
        ins, refs = refs[:n_in], refs[n_in:]
        side_ins, refs = refs[:ni], refs[ni:]
        outs, refs = refs[:n_out], refs[n_out:]
        side_outs, refs = refs[:no], refs[no:]
        scratch, sems = refs[:n_scratch], refs[n_scratch:]
        ids = [pl.program_id(a) for a in range(len(grid))]
        first = functools.reduce(jnp.logical_and, [i == 0 for i in ids])
        last = functools.reduce(jnp.logical_and, [i == g - 1 for i, g in zip(ids, grid)])
        pl.when(first)(lambda: side.start(side_ins, side_outs, sems))
        body(*ins, *outs, *scratch)
        pl.when(last)(lambda: side.finish(side_ins, side_outs, sems))

    return wrapped, [ANY] * ni, [ANY] * no, list(side.out_shapes), _sems(side.nsem), list(side.ins)


def ssd_scan_fwd(xs, bm, cm, dtv, arow, eexp, nb, T, side=None):
    R = xs.shape[0]
    nc = T // CHUNK
    B = range(nb)

    def body(xs_ref, bm_ref, cm_ref, dt_ref, a_ref, e_ref, y_ref, hin_ref, st_ref):
        d = pl.program_id(0)
        s = pl.program_id(1)

        @pl.when(s == 0)
        def _():
            st_ref[...] = jnp.zeros_like(st_ref)

        eexp = e_ref[...]
        com = [_scan_common(d, dt_ref[0, b], a_ref[0, 0:1, :], eexp, xs_ref[b]) for b in B]
        mask = com[0][0]
        cs = [com[b][3] for b in B]
        cst = [cs[b].T for b in B]
        sin = [st_ref[b] for b in B]
        for b in B:
            hin_ref[0, b] = sin[b]
        sb = [sin[b].astype(MXU) for b in B]
        xtb = [com[b][6].astype(MXU) for b in B]
        xw = [(com[b][6] * com[b][10]).astype(MXU) for b in B]
        g0 = lax.broadcasted_iota(jnp.int32, (CHUNK, SSD_INNER), 1) < 192
        lane = lax.broadcasted_iota(jnp.int32, (CHUNK, 128), 1)
        c = [[cm_ref[b, :, 0:128].astype(MXU), cm_ref[b, :, 128:256].astype(MXU)] for b in B]
        bq = [[bm_ref[b, :, 0:128].astype(MXU), bm_ref[b, :, 128:256].astype(MXU)] for b in B]
        y = [jnp.where(g0, _dot(c[b][0], sb[b]), _dot(c[b][1], sb[b])) * com[b][8] for b in B]
        cb = [[_dotg(c[b][g], bq[b][g], NT) for g in range(2)] for b in B]
        blocks = [[] for _ in B]
        for blk in range(3):
            acc = [None for _ in B]
            for hh in range(2):
                h = blk * 2 + hh
                for b in B:
                    m = (cb[b][h // 3] * _decay_matrix(mask, cs[b], cst[b], h)).astype(MXU)
                    res = _dot(m, xtb[b][:, blk * 128:(blk + 1) * 128])
                    acc[b] = res if hh == 0 else jnp.where(lane < 64, acc[b], res)
            for b in B:
                blocks[b].append(acc[b])
        for b in B:
            y_ref[0, b] = y[b] + jnp.concatenate(blocks[b], axis=1)
            st_ref[b] = sin[b] * com[b][12] + jnp.where(g0, _dotg(bq[b][0], xw[b], TN), _dotg(bq[b][1], xw[b], TN))

    def rows(cols):
        return pl.BlockSpec((nb, CHUNK, cols), lambda d, s: (0, _chunk_index(d, s, nc), 0))

    def by_dir(cols):
        return pl.BlockSpec((1, nb, CHUNK, cols), lambda d, s: (d, 0, _chunk_index(d, s, nc), 0))

    grid = (2, nc)
    body, side_in, side_out, side_shapes, side_scratch, side_args = _side_wrap(body, 6, 2, 1, side, grid)
    outs = pl.pallas_call(
        body, name="ssd_scan_fwd" if side is None else "ssd_scan_fwd_comm", grid=grid,
        in_specs=[rows(384), rows(256), rows(256), by_dir(128), pl.BlockSpec((1, 8, 128), lambda d, s: (d, 0, 0)),
                  pl.BlockSpec((128, 384), lambda d, s: (0, 0))] + side_in,
        out_specs=[by_dir(384),
                   pl.BlockSpec((1, nb, CHUNK, 384), lambda d, s: (d * nc + _chunk_index(d, s, nc), 0, 0, 0))] + side_out,
        out_shape=[jax.ShapeDtypeStruct((2, nb, T, 384), F32), jax.ShapeDtypeStruct((2 * nc, nb, CHUNK, 384), F32)]
                  + side_shapes,
        scratch_shapes=[pltpu.VMEM((nb, CHUNK, 384), F32)] + side_scratch,
    )(xs.reshape(nb, T, 384), bm.reshape(nb, T, 256), cm.reshape(nb, T, 256), dtv.reshape(2, nb, T, 128), arow, eexp,
      *side_args)
    return outs[0].reshape(2, R, 384), outs[1], list(outs[2:])


def ssd_scan_bwd(xs, bm, cm, dtv, arow, eexp, hin, dy, nb, T, side=None):
    R = xs.shape[0]
    nc = T // CHUNK
    B = range(nb)

    def chunk(d, s):
        return _chunk_index(d, nc - 1 - s, nc)

    def body(xs_ref, bm_ref, cm_ref, dt_ref, a_ref, e_ref, hin_ref, dy_ref,
             dxs_ref, dbm_ref, dcm_ref, ddt_ref, da_ref, ds_ref):
        d = pl.program_id(0)
        s = pl.program_id(1)

        @pl.when(s == 0)
        def _():
            ds_ref[...] = jnp.zeros_like(ds_ref)
            da_ref[...] = jnp.zeros_like(da_ref)

        eexp = e_ref[...]
        arow = a_ref[0, 0:1, :]
        dt = [dt_ref[0, b] for b in B]
        xs_v = [xs_ref[b] for b in B]
        com = [_scan_common(d, dt[b], arow, eexp, xs_v[b]) for b in B]
        mask, tmat = com[0][0], com[0][1]
        cs, dtx, xt, ecs, ecx, dte, dtex, etot, etx = [[com[b][i] for b in B] for i in (3, 5, 6, 7, 8, 9, 10, 11, 12)]
        cst = [cs[b].T for b in B]
        sin = [hin_ref[0, b] for b in B]
        sb = [sin[b].astype(MXU) for b in B]
        dsp = [ds_ref[b] for b in B]
        dyv = [dy_ref[b] for b in B]
        xtb = [xt[b].astype(MXU) for b in B]
        xw = [(xt[b] * dtex[b]).astype(MXU) for b in B]
        g0 = lax.broadcasted_iota(jnp.int32, (CHUNK, SSD_INNER), 1) < 192
        lane = lax.broadcasted_iota(jnp.int32, (CHUNK, 128), 1)
        sub = lax.broadcasted_iota(jnp.int32, (CHUNK, 128), 0)
        c = [[cm_ref[b, :, 0:128].astype(MXU), cm_ref[b, :, 128:256].astype(MXU)] for b in B]
        bq = [[bm_ref[b, :, 0:128].astype(MXU), bm_ref[b, :, 128:256].astype(MXU)] for b in B]

        cs_prod = [jnp.where(g0, _dot(c[b][0], sb[b]), _dot(c[b][1], sb[b])) for b in B]
        dcsp = [dyv[b] * ecx[b] for b in B]
        dcsp_g = [[jnp.where(g0, dcsp[b], 0.0).astype(MXU), jnp.where(g0, 0.0, dcsp[b]).astype(MXU)] for b in B]
        dcs = [_dot_hi(dyv[b] * cs_prod[b], eexp, NT) * ecs[b] for b in B]
        dc = [[_dotg(dcsp_g[b][g], sb[b], NT) for g in range(2)] for b in B]
        dsin = [_dotg(c[b][0], dcsp_g[b][0], TN) + _dotg(c[b][1], dcsp_g[b][1], TN) + dsp[b] * etx[b] for b in B]

        dtot = [_dot_hi(jnp.broadcast_to(_colsum(dsp[b] * sin[b]), (8, SSD_INNER)), eexp, NT)[0:1, :] * etot[b] for b in B]
        dsp_g = [[jnp.where(g0, dsp[b], 0.0).astype(MXU), jnp.where(g0, 0.0, dsp[b]).astype(MXU)] for b in B]
        dxw = [_dot(bq[b][0], dsp_g[b][0]) + _dot(bq[b][1], dsp_g[b][1]) for b in B]
        db = [[_dotg(xw[b], dsp_g[b][g], NT) for g in range(2)] for b in B]
        dxt = [dxw[b] * dtex[b] for b in B]
        ddte = [_dot_hi(dxw[b] * xt[b], eexp, NT) * dte[b] for b in B]
        dtot = [dtot[b] + _colsum(ddte[b]) for b in B]
        dcs = [dcs[b] - ddte[b] for b in B]

        cb = [[_dotg(c[b][g], bq[b][g], NT) for g in range(2)] for b in B]
        dg = [[jnp.zeros((CHUNK, CHUNK), F32), jnp.zeros((CHUNK, CHUNK), F32)] for _ in B]
        dcs_rows = [jnp.zeros((CHUNK, 128), F32) for _ in B]
        dxt_blocks = [[] for _ in B]
        for blk in range(3):
            acc = [jnp.zeros((CHUNK, 128), F32) for _ in B]
            for hh in range(2):
                h = blk * 2 + hh
                g = h // 3
                mine = (lane < 64) if hh == 0 else (lane >= 64)
                for b in B:
                    dyh = jnp.where(mine, dyv[b][:, blk * 128:(blk + 1) * 128], 0.0).astype(MXU)
                    lh = _decay_matrix(mask, cs[b], cst[b], h)
                    m = cb[b][g] * lh
                    dm = _dotg(dyh, xtb[b][:, blk * 128:(blk + 1) * 128], NT)
                    acc[b] = acc[b] + _dotg(m.astype(MXU), dyh, TN)
                    dg[b][g] = dg[b][g] + dm * lh
                    q = dm * m
                    dcs[b] = dcs[b] + jnp.where(lane == h, jnp.sum(q, axis=1, keepdims=True), 0.0)
                    dcs_rows[b] = dcs_rows[b] - jnp.where(sub == h, jnp.sum(q, axis=0, keepdims=True), 0.0)
            for b in B:
                dxt_blocks[b].append(acc[b])
        for b in B:
            dxt[b] = dxt[b] + jnp.concatenate(dxt_blocks[b], axis=1)
            for g in range(2):
                dgb = dg[b][g].astype(MXU)
                dc[b][g] = dc[b][g] + _dot(dgb, bq[b][g])
                db[b][g] = db[b][g] + _dotg(dgb, c[b][g], TN)
            dcs[b] = dcs[b] + dcs_rows[b].T

        for b in B:
            dadt = _dot_hi(tmat, dcs[b], TN, sel_first=True) + dtot[b]
            ddt_ref[0, b] = dadt * arow + _dot_hi(dxt[b] * xs_v[b], eexp, NT)
            da_ref[0, b, 0:1, :] += _colsum(dadt * dt[b])
            dxs_ref[0, b] = (dxt[b] * dtx[b]).astype(dxs_ref.dtype)
            dbm_ref[0, b] = jnp.concatenate(db[b], axis=1).astype(dbm_ref.dtype)
            dcm_ref[0, b] = jnp.concatenate(dc[b], axis=1).astype(dcm_ref.dtype)
            ds_ref[b] = dsin[b]

    def rows(cols):
        return pl.BlockSpec((nb, CHUNK, cols), lambda d, s: (0, chunk(d, s), 0))

    def by_dir(cols):
        return pl.BlockSpec((1, nb, CHUNK, cols), lambda d, s: (d, 0, chunk(d, s), 0))

    grid = (2, nc)
    body, side_in, side_out, side_shapes, side_scratch, side_args = _side_wrap(body, 8, 5, 1, side, grid)
    outs = pl.pallas_call(
        body, name="ssd_scan_bwd" if side is None else "ssd_scan_bwd_comm", grid=grid,
        in_specs=[rows(384), rows(256), rows(256), by_dir(128), pl.BlockSpec((1, 8, 128), lambda d, s: (d, 0, 0)),
                  pl.BlockSpec((128, 384), lambda d, s: (0, 0)),
                  pl.BlockSpec((1, nb, CHUNK, 384), lambda d, s: (d * nc + chunk(d, s), 0, 0, 0)), rows(384)] + side_in,
        out_specs=[by_dir(384), by_dir(256), by_dir(256), by_dir(128),
                   pl.BlockSpec((1, nb, 8, 128), lambda d, s: (d, 0, 0, 0))] + side_out,
        out_shape=[jax.ShapeDtypeStruct((2, nb, T, 384), MXU), jax.ShapeDtypeStruct((2, nb, T, 256), MXU),
                   jax.ShapeDtypeStruct((2, nb, T, 256), MXU), jax.ShapeDtypeStruct((2, nb, T, 128), F32),
                   jax.ShapeDtypeStruct((2, nb, 8, 128), F32)] + side_shapes,
        scratch_shapes=[pltpu.VMEM((nb, CHUNK, 384), F32)] + side_scratch,
    )(xs.reshape(nb, T, 384), bm.reshape(nb, T, 256), cm.reshape(nb, T, 256), dtv.reshape(2, nb, T, 128), arow, eexp,
      hin, dy.reshape(nb, T, 384), *side_args)
    return (outs[0].reshape(2, R, 384), outs[1].reshape(2, R, 256), outs[2].reshape(2, R, 256),
            outs[3].reshape(2, R, 128), outs[4], list(outs[5:]))


def _group_rms(g):
    lane = lax.broadcasted_iota(jnp.int32, g.shape, 1)
    g0 = lane < 192
    gg = g * g
    s0 = jnp.sum(jnp.where(g0, gg, 0.0), axis=-1, keepdims=True)
    s1 = jnp.sum(gg, axis=-1, keepdims=True) - s0
    rstd = jnp.where(g0, lax.rsqrt(s0 * (1.0 / 192) + EPS), lax.rsqrt(s1 * (1.0 / 192) + EPS))
    return rstd, g0


def ssd_out_fwd(y2, xs, pz, dexp, nw):
    R = xs.shape[0]

    def body(y_ref, xs_ref, z_ref, d_ref, nw_ref, o_ref):
        z = z_ref[...]
        yy = y_ref[0] + y_ref[1] + xs_ref[...] * d_ref[...]
        g = yy * (z * _sigmoid(z))
        rstd, _ = _group_rms(g)
        o_ref[...] = g * rstd * nw_ref[...]

    return pl.pallas_call(
        body, name="ssd_out_fwd", grid=(R // TM,),
        in_specs=[pl.BlockSpec((2, TM, 384), lambda i: (0, i, 0)), _rowspec(384), _rowspec(384),
                  _fullspec((1, 384)), _fullspec((1, 384))],
        out_specs=_rowspec(384),
        out_shape=jax.ShapeDtypeStruct((R, 384), F32),
    )(y2, xs, pz, dexp, nw)


def ssd_out_bwd(dout, y2, xs, pz, dexp, nw):
    R = xs.shape[0]

    def body(do_ref, y_ref, xs_ref, z_ref, d_ref, nw_ref, dy_ref, dz_ref, dxs_ref, part_ref):
        z = z_ref[...]
        xs_v = xs_ref[...]
        yy = y_ref[0] + y_ref[1] + xs_v * d_ref[...]
        sig = _sigmoid(z)
        sz = z * sig
        g = yy * sz
        rstd, g0 = _group_rms(g)
        ghat = g * rstd
        do = do_ref[...]
        dgn = do * nw_ref[...]
        t = dgn * ghat
        t0 = jnp.sum(jnp.where(g0, t, 0.0), axis=-1, keepdims=True)
        t1 = jnp.sum(t, axis=-1, keepdims=True) - t0
        dg = rstd * (dgn - ghat * jnp.where(g0, t0, t1) * (1.0 / 192))
        dyy = dg * sz
        dy_ref[...] = dyy
        dz_ref[...] = (dg * yy * (sig * (1.0 + z * (1.0 - sig)))).astype(dz_ref.dtype)
        dxs_ref[...] = dyy * d_ref[...]
        part_ref[0] = jnp.concatenate([_colsum(do * ghat), _colsum(dyy * xs_v), jnp.zeros((6, 384), F32)], axis=0)

    return pl.pallas_call(
        body, name="ssd_out_bwd", grid=(R // TM,),
        in_specs=[_rowspec(384), pl.BlockSpec((2, TM, 384), lambda i: (0, i, 0)), _rowspec(384), _rowspec(384),
                  _fullspec((1, 384)), _fullspec((1, 384))],
        out_specs=[_rowspec(384), _rowspec(384), _rowspec(384), pl.BlockSpec((1, 8, 384), lambda i: (i, 0, 0))],
        out_shape=[jax.ShapeDtypeStruct((R, 384), F32), jax.ShapeDtypeStruct((R, 384), MXU),
                   jax.ShapeDtypeStruct((R, 384), F32), jax.ShapeDtypeStruct((R // TM, 8, 384), F32)],
    )(dout, y2, xs, pz, dexp, nw)


def ssd_prep_bwd_a(pxbc, plast, cw, cb, dtb, dxs_skip, dxs2, dbm2, dcm2, ddt2, blocks_per_sample):
    R = pxbc.shape[0]
    prev, nxt = _halo_specs(XBC, R)

    def body(cur_ref, prev_ref, nxt_ref, pl_ref, cw_ref, cb_ref, dtb_ref, dsk_ref, dxs_ref, dbm_ref, dcm_ref, ddt_ref,
             dpre_ref, dlast_ref, part_ref):
        i = pl.program_id(0)
        ext = _ext_rows(cur_ref[...], prev_ref[...], nxt_ref[...], i, blocks_per_sample)
        taps = _conv_taps(ext)
        co = _conv_out(taps, cw_ref, cb_ref)
        sig = _sigmoid(co)
        both = lambda ref: ref[0].astype(F32) + ref[1].astype(F32)
        up = jnp.concatenate([dsk_ref[...] + both(dxs_ref), both(dbm_ref), both(dcm_ref)], axis=1)
        dpre = up * (sig * (1.0 + co * (1.0 - sig)))
        dpre_ref[...] = dpre
        raw = pl_ref[...] + dtb_ref[...]
        lane = lax.broadcasted_iota(jnp.int32, raw.shape, 1)
        ddt = (pltpu.roll(ddt_ref[0], DT0, axis=1) + pltpu.roll(ddt_ref[1], DT0 + SSD_HEADS, axis=1))
        ddt = jnp.where(jnp.logical_and(lane >= DT0, lane < DT0 + 2 * SSD_HEADS), ddt * _sigmoid(raw), 0.0)
        dlast_ref[...] = ddt.astype(dlast_ref.dtype)
        rows = [_colsum(dpre * taps[k]) for k in range(4)]
        rows.append(_colsum(dpre))
        rows.append(jnp.concatenate([_colsum(ddt), jnp.zeros((1, XBC - 128), F32)], axis=1))
        rows.append(jnp.zeros((2, XBC), F32))
        part_ref[0] = jnp.concatenate(rows, axis=0)

    dirspec = lambda n: pl.BlockSpec((2, SB, n), lambda i: (0, i, 0))
    return pl.pallas_call(
        body, name="ssd_prep_bwd_a", grid=(R // SB,),
        in_specs=[_rowspec(XBC, SB), prev, nxt, _rowspec(128, SB), _fullspec((8, XBC)), _fullspec((1, XBC)),
                  _fullspec((1, 128)), _rowspec(384, SB), dirspec(384), dirspec(256), dirspec(256), dirspec(128)],
        out_specs=[_rowspec(XBC, SB), _rowspec(128, SB), pl.BlockSpec((1, 8, XBC), lambda i: (i, 0, 0))],
        out_shape=[jax.ShapeDtypeStruct((R, XBC), F32), jax.ShapeDtypeStruct((R, 128), MXU),
                   jax.ShapeDtypeStruct((R // SB, 8, XBC), F32)],
    )(pxbc, pxbc, pxbc, plast, cw, cb, dtb, dxs_skip, dxs2, dbm2, dcm2, ddt2)


def ssd_prep_bwd_b(dpre, cw, blocks_per_sample):
    R = dpre.shape[0]
    prev, nxt = _halo_specs(XBC, R)

    def body(cur_ref, prev_ref, nxt_ref, cw_ref, o_ref):
        i = pl.program_id(0)
        ext = _ext_rows(cur_ref[...], prev_ref[...], nxt_ref[...], i, blocks_per_sample)
        o_ref[...] = (cw_ref[0:1, :] * _shift(ext, 1) + cw_ref[1:2, :] * _shift(ext, 0)
                      + cw_ref[2:3, :] * _shift(ext, -1) + cw_ref[3:4, :] * _shift(ext, -2)).astype(o_ref.dtype)

    return pl.pallas_call(
        body, name="ssd_prep_bwd_b", grid=(R // SB,),
        in_specs=[_rowspec(XBC, SB), prev, nxt, _fullspec((8, XBC))],
        out_specs=_rowspec(XBC, SB),
        out_shape=jax.ShapeDtypeStruct((R, XBC), MXU),
    )(dpre, dpre, dpre, cw)


def _rope(u, cos, sa, sb):
    return u * cos + pltpu.roll(u, 120, axis=1) * sa + pltpu.roll(u, 8, axis=1) * sb


def _rope_t(du, cos, sa, sb):
    return du * cos + pltpu.roll(du * sa, 8, axis=1) + pltpu.roll(du * sb, 120, axis=1)


def mla_prep(pqa, pkva, plast, qnw, kvnw, wq, wk, wv, cos, sa, sb):
    R = pqa.shape[0]

    def body(qa_ref, kva_ref, pl_ref, qnw_ref, kvnw_ref, wq_ref, wk_ref, wv_ref, cos_ref, sa_ref, sb_ref,
             q_ref, k_ref, v_ref, cq_ref, ckv_ref):
        cos_v, sa_v, sb_v = cos_ref[...], sa_ref[...], sb_ref[...]
        xq, _ = _rms_hat(qa_ref[...])
        cq_ref[...] = (xq * qnw_ref[...]).astype(cq_ref.dtype)
        xkv, _ = _rms_hat(kva_ref[...])
        ckv_ref[...] = (xkv * kvnw_ref[...]).astype(ckv_ref.dtype)
        q = _dot(cq_ref[...], wq_ref[...])
        kn = _dot(ckv_ref[...], wk_ref[...])
        v_ref[...] = _dot(ckv_ref[...], wv_ref[...]).astype(v_ref.dtype)
        lane = lax.broadcasted_iota(jnp.int32, (TM, HP), 1)
        rope_lanes = jnp.logical_and(lane >= QK_NOPE, lane < QK_DIM)
        kr = _rope(jnp.where(rope_lanes, pltpu.roll(pl_ref[...], QK_NOPE, axis=1), 0.0), cos_v, sa_v, sb_v)
        for h in range(MLA_HEADS):
            cols = slice(h * HP, (h + 1) * HP)
            q_ref[:, cols] = (_rope(q[:, cols], cos_v, sa_v, sb_v) * Q_SCALE).astype(q_ref.dtype)
            k_ref[:, cols] = (kn[:, cols] + kr).astype(k_ref.dtype)

    return pl.pallas_call(
        body, name="mla_prep", grid=(R // TM,),
        in_specs=[_rowspec(256), _rowspec(256), _rowspec(128), _fullspec((1, 256)), _fullspec((1, 256)),
                  _fullspec((256, QW)), _fullspec((256, QW)), _fullspec((256, QW)),
                  _rowspec(HP), _rowspec(HP), _rowspec(HP)],
        out_specs=[_rowspec(QW), _rowspec(QW), _rowspec(QW), _rowspec(256), _rowspec(256)],
        out_shape=[jax.ShapeDtypeStruct((R, QW), MXU)] * 3 + [jax.ShapeDtypeStruct((R, 256), MXU)] * 2,
    )(pqa, pkva, plast, qnw, kvnw, wq, wk, wv, cos, sa, sb)


def mla_prep_bwd(dq, dk, dv, pqa, pkva, cq, ckv, qnw, kvnw, wq, wk, wv, cos, sa, sb):
    R = pqa.shape[0]

    def body(dq_ref, dk_ref, dv_ref, qa_ref, kva_ref, cq_ref, ckv_ref, qnw_ref, kvnw_ref, wq_ref, wk_ref, wv_ref,
             cos_ref, sa_ref, sb_ref, dqa_ref, dkva_ref, dkr_ref, dwq_ref, dwk_ref, dwv_ref, part_ref,
             dql_ref, dkm_ref, dvb_ref):
        @pl.when(pl.program_id(0) == 0)
        def _():
            dwq_ref[...] = jnp.zeros_like(dwq_ref)
            dwk_ref[...] = jnp.zeros_like(dwk_ref)
            dwv_ref[...] = jnp.zeros_like(dwv_ref)

        cos_v, sa_v, sb_v = cos_ref[...], sa_ref[...], sb_ref[...]
        lane = lax.broadcasted_iota(jnp.int32, (TM, HP), 1)
        rope_lanes = jnp.logical_and(lane >= QK_NOPE, lane < QK_DIM)
        dkr = jnp.zeros((TM, HP), F32)
        for h in range(MLA_HEADS):
            cols = slice(h * HP, (h + 1) * HP)
            dql_ref[:, cols] = (_rope_t(dq_ref[:, cols], cos_v, sa_v, sb_v) * ATT_SCALE).astype(dql_ref.dtype)
            dkh = dk_ref[:, cols] * LN2
            dkm_ref[:, cols] = jnp.where(lane < QK_NOPE, dkh, 0.0).astype(dkm_ref.dtype)
            dkr = dkr + jnp.where(rope_lanes, dkh, 0.0)
        dvb_ref[...] = dv_ref[...].astype(dvb_ref.dtype)
        dkr = jnp.where(rope_lanes, _rope_t(dkr, cos_v, sa_v, sb_v), 0.0)
        dkr_ref[...] = pltpu.roll(dkr, HP - QK_NOPE, axis=1).astype(dkr_ref.dtype)
        dwq_ref[...] += _dotg(cq_ref[...], dql_ref[...], TN)
        dwk_ref[...] += _dotg(ckv_ref[...], dkm_ref[...], TN)
        dwv_ref[...] += _dotg(ckv_ref[...], dvb_ref[...], TN)
        xq, rq = _rms_hat(qa_ref[...])
        dqa, dqnw = _rms_bwd(_dotg(dql_ref[...], wq_ref[...], NT), xq, rq, qnw_ref[...])
        dqa_ref[...] = dqa.astype(dqa_ref.dtype)
        xkv, rkv = _rms_hat(kva_ref[...])
        dckv = _dotg(dkm_ref[...], wk_ref[...], NT) + _dotg(dvb_ref[...], wv_ref[...], NT)
        dkva, dkvnw = _rms_bwd(dckv, xkv, rkv, kvnw_ref[...])
        dkva_ref[...] = dkva.astype(dkva_ref.dtype)
        part_ref[0] = jnp.concatenate([dqnw, dkvnw, jnp.zeros((6, 256), F32)], axis=0)

    return pl.pallas_call(
        body, name="mla_prep_bwd", grid=(R // TM,),
        in_specs=[_rowspec(QW), _rowspec(QW), _rowspec(QW), _rowspec(256), _rowspec(256), _rowspec(256), _rowspec(256),
                  _fullspec((1, 256)), _fullspec((1, 256)), _fullspec((256, QW)), _fullspec((256, QW)),
                  _fullspec((256, QW)), _rowspec(HP), _rowspec(HP), _rowspec(HP)],
        out_specs=[_rowspec(256), _rowspec(256), _rowspec(128), _fullspec((256, QW)), _fullspec((256, QW)),
                   _fullspec((256, QW)), pl.BlockSpec((1, 8, 256), lambda i: (i, 0, 0))],
        out_shape=[jax.ShapeDtypeStruct((R, 256), MXU), jax.ShapeDtypeStruct((R, 256), MXU),
                   jax.ShapeDtypeStruct((R, 128), MXU)] + [jax.ShapeDtypeStruct((256, QW), F32)] * 3
                  + [jax.ShapeDtypeStruct((R // TM, 8, 256), F32)],
        scratch_shapes=[pltpu.VMEM((TM, QW), MXU)] * 3,
    )(dq, dk, dv, pqa, pkva, cq, ckv, qnw, kvnw, wq, wk, wv, cos, sa, sb)


ATT_SCALE = QK_DIM ** -0.5
TQ = 256


LOG2E = 1.4426950408889634
LN2 = 0.6931471805599453
Q_SCALE = ATT_SCALE * LOG2E


def _key_chunks(T, n=2):
    unit = 256 if T % 256 == 0 else 128
    units = T // unit
    sizes = [(units // n + (1 if i < units % n else 0)) * unit for i in range(n)]
    return [(sum(sizes[:i]), sz) for i, sz in enumerate(sizes) if sz]


def attn_fwd(q, k, v, nb, T):
    R = q.shape[0]
    nq = T // TQ
    chunks = _key_chunks(T, 4)
    HEADS = range(2)

    def body(q_ref, k_ref, v_ref, o_ref, lse_ref):
        def lanes(h):
            return slice(h * HP, (h + 1) * HP)

        def logits(h, lo, n):
            return _dotg(q_ref[:, lanes(h)], k_ref[lo:lo + n, lanes(h)], NT)

        def weigh(h, s, lo, n):
            m = jnp.max(s, axis=-1, keepdims=True)
            p = jnp.exp2(s - m)
            return m, jnp.sum(p, axis=-1, keepdims=True), _dot(p.astype(MXU), v_ref[lo:lo + n, lanes(h)])

        def parts_of(ranges):
            out = [[] for _ in HEADS]
            s = [logits(h, *ranges[0]) for h in HEADS]
            for j, (lo, n) in enumerate(ranges):
                nxt = [logits(h, *ranges[j + 1]) for h in HEADS] if j + 1 < len(ranges) else None
                for h in HEADS:
                    out[h].append(weigh(h, s[h], lo, n))
                s = nxt
            return out

        def finish(all_parts):
            for h, parts in enumerate(all_parts):
                m = parts[0][0]
                for pm, _, _ in parts[1:]:
                    m = jnp.maximum(m, pm)
                l, o = 0.0, 0.0
                for pm, pl_, po in parts:
                    a = jnp.exp2(pm - m)
                    l = l + a * pl_
                    o = o + a * po
                o_ref[:, lanes(h)] = o / l
                lse_ref[:, lanes(h)] = jnp.broadcast_to(m + jnp.log(l) * LOG2E, (TQ, HP))

        i = pl.program_id(2)
        pl.when(i == 0)(lambda: finish(parts_of([(0, CTX)])))
        pl.when(i > 0)(lambda: finish(parts_of(chunks)))

    qspec = pl.BlockSpec((TQ, 2 * HP), lambda b, h, i: (b * nq + i, h))
    kspec = pl.BlockSpec((T, 2 * HP), lambda b, h, i: (b, h))
    return pl.pallas_call(
        body, name="attn_fwd", grid=(nb, MLA_HEADS // 2, nq),
        in_specs=[qspec, kspec, kspec], out_specs=[qspec, qspec],
        out_shape=[jax.ShapeDtypeStruct((R, QW), F32)] * 2,
        compiler_params=_cp(48),
    )(q, k, v)


def attn_bwd(q, k, v, o, lse, do, nb, T):
    R = q.shape[0]
    nq = T // TQ
    chunks = _key_chunks(T)

    def body(q_ref, k_ref, v_ref, o_ref, lse_ref, do_ref, dq_ref, dk_ref, dv_ref):
        i = pl.program_id(2)

        @pl.when(i == 0)
        def _():
            dk_ref[...] = jnp.zeros_like(dk_ref)
            dv_ref[...] = jnp.zeros_like(dv_ref)

        def run(chunks):
            for h in range(2):
                lanes = slice(h * HP, (h + 1) * HP)
                qv = q_ref[:, lanes]
                dov = do_ref[:, lanes]
                dob = dov.astype(MXU)
                delta = jnp.sum(dov * o_ref[:, lanes], axis=-1, keepdims=True)
                lse_v = lse_ref[:, h * HP:h * HP + 1]
                dq = 0.0
                for lo, n in chunks:
                    kv = k_ref[lo:lo + n, lanes]
                    p = jnp.exp2(_dotg(qv, kv, NT) - lse_v)
                    dp = _dotg(dob, v_ref[lo:lo + n, lanes], NT)
                    dsb = (p * (dp - delta)).astype(MXU)
                    dq = dq + _dot(dsb, kv)
                    dk_ref[lo:lo + n, lanes] += _dotg(dsb, qv, TN)
                    dv_ref[lo:lo + n, lanes] += _dotg(p.astype(MXU), dob, TN)
                dq_ref[:, lanes] = dq

        pl.when(i == 0)(lambda: run([(0, CTX)]))
        pl.when(i > 0)(lambda: run(chunks))

    qspec = pl.BlockSpec((TQ, 2 * HP), lambda b, h, i: (b * nq + i, h))
    kspec = pl.BlockSpec((T, 2 * HP), lambda b, h, i: (b, h))
    return pl.pallas_call(
        body, name="attn_bwd", grid=(nb, MLA_HEADS // 2, nq),
        in_specs=[qspec, kspec, kspec, qspec, qspec, qspec],
        out_specs=[qspec, kspec, kspec],
        out_shape=[jax.ShapeDtypeStruct((R, QW), F32)] * 3,
        compiler_params=_cp(56),
    )(q, k, v, o, lse, do)


def _pool_geometry(i, blocks_per_sample, seq):
    j = i % blocks_per_sample
    n = jnp.where(j == 0, CTX, seq)
    t0 = jnp.where(j == 0, 0, (j - 1) * SB) - HALO
    lane = lax.broadcasted_iota(jnp.int32, (SB + 2 * HALO, POOL_DIM), 1)
    t = lax.broadcasted_iota(jnp.int32, (SB + 2 * HALO, POOL_DIM), 0) + t0
    wh = jnp.where(lane < 64, 1, jnp.where(lane < 128, 2, jnp.where(lane < 192, 4, 8)))
    cnt = jnp.minimum(t + wh, n) - jnp.maximum(t - wh, 0)
    return lane, 1.0 / jnp.maximum(cnt, 1).astype(F32)


def _by_window(lane, c2, c4, c8, c16):
    return jnp.where(lane < 64, c2, jnp.where(lane < 128, c4, jnp.where(lane < 192, c8, c16)))


def _window_sums(ext, lane, first):
    n = ext.shape[0]
    r = lambda a, s: pltpu.roll(a, s % n, axis=0)
    c2 = ext + r(ext, first)
    c4 = r(c2, 1) + r(c2, -1)
    c8 = r(c4, 2) + r(c4, -2)
    c16 = r(c8, 4) + r(c8, -4)
    return _by_window(lane, c2, c4, c8, c16)


def _pool_delta(ext, lane, inv):
    return (_window_sums(ext, lane, 1) * inv - ext)[HALO:HALO + SB, :]


def pool_fwd(ppool, wbd, scale, blocks_per_sample, seq):
    R = ppool.shape[0]
    prev, nxt = _halo_specs(POOL_DIM, R)

    def body(cur_ref, prev_ref, nxt_ref, w_ref, s_ref, o_ref):
        i = pl.program_id(0)
        ext = _ext_rows(cur_ref[...], prev_ref[...], nxt_ref[...], i, blocks_per_sample)
        lane, inv = _pool_geometry(i, blocks_per_sample, seq)
        dlt = _pool_delta(ext, lane, inv)
        o_ref[...] = _dot(dlt.astype(MXU), w_ref[...]) * s_ref[...]

    return pl.pallas_call(
        body, name="pool_fwd", grid=(R // SB,),
        in_specs=[_rowspec(POOL_DIM, SB), prev, nxt, _fullspec((POOL_DIM, POOL_DIM)), _fullspec((1, POOL_DIM))],
        out_specs=_rowspec(POOL_DIM, SB),
        out_shape=jax.ShapeDtypeStruct((R, POOL_DIM), F32),
    )(ppool, ppool, ppool, wbd, scale)


def pool_bwd(ppool, dpool, wbd, scale, blocks_per_sample, seq):
    R = ppool.shape[0]
    prev, nxt = _halo_specs(POOL_DIM, R)

    def body(cur_ref, prev_ref, nxt_ref, dcur_ref, dprev_ref, dnxt_ref, w_ref, s_ref, du_ref, dw_ref, part_ref):
        i = pl.program_id(0)

        @pl.when(i == 0)
        def _():
            dw_ref[...] = jnp.zeros_like(dw_ref)

        ext = _ext_rows(cur_ref[...], prev_ref[...], nxt_ref[...], i, blocks_per_sample)
        lane, inv = _pool_geometry(i, blocks_per_sample, seq)
        dlt = _pool_delta(ext, lane, inv).astype(MXU)
        dy = dcur_ref[...]
        part_ref[0] = jnp.concatenate([_colsum(dy * _dot(dlt, w_ref[...])), jnp.zeros((7, POOL_DIM), F32)], axis=0)
        dyp = (dy * s_ref[...]).astype(MXU)
        dw_ref[...] += _dotg(dlt, dyp, TN)
        dext = _ext_rows(dy, dprev_ref[...], dnxt_ref[...], i, blocks_per_sample)
        dd = _dotg((dext * s_ref[...]).astype(MXU), w_ref[...], NT)
        du_ref[...] = (_window_sums(dd * inv, lane, -1) - dd)[HALO:HALO + SB, :].astype(du_ref.dtype)

    return pl.pallas_call(
        body, name="pool_bwd", grid=(R // SB,),
        in_specs=[_rowspec(POOL_DIM, SB), prev, nxt, _rowspec(POOL_DIM, SB), prev, nxt,
                  _fullspec((POOL_DIM, POOL_DIM)), _fullspec((1, POOL_DIM))],
        out_specs=[_rowspec(POOL_DIM, SB), _fullspec((POOL_DIM, POOL_DIM)),
                   pl.BlockSpec((1, 8, POOL_DIM), lambda i: (i, 0, 0))],
        out_shape=[jax.ShapeDtypeStruct((R, POOL_DIM), MXU), jax.ShapeDtypeStruct((POOL_DIM, POOL_DIM), F32),
                   jax.ShapeDtypeStruct((R // SB, 8, POOL_DIM), F32)],
    )(ppool, ppool, ppool, dpool, dpool, dpool, wbd, scale)


def adamw(w, g, m, v, name="adamw"):
    rows, cols = w.shape
    tr = rows
    for cand in (512, 256, 128, 64, 32, 16, 8):
        if rows % cand == 0:
            tr = cand
            break
    bc1 = 1.0 - ADAM_B1 ** ADAM_STEP
    bc2 = 1.0 - ADAM_B2 ** ADAM_STEP

    def body(w_ref, g_ref, m_ref, v_ref, d_ref, nm_ref, nv_ref):
        g_v = g_ref[...]
        nm = ADAM_B1 * m_ref[...] + (1.0 - ADAM_B1) * g_v
        nv = ADAM_B2 * v_ref[...] + (1.0 - ADAM_B2) * (g_v * g_v)
        nm_ref[...] = nm
        nv_ref[...] = nv
        d_ref[...] = -ADAM_LR * ((nm / bc1) / (jnp.sqrt(nv / bc2) + ADAM_EPS) + ADAM_WD * w_ref[...])

    spec = pl.BlockSpec((tr, cols), lambda i: (i, 0))
    return pl.pallas_call(
        body, name=name, grid=(rows // tr,),
        in_specs=[spec] * 4, out_specs=[spec] * 3,
        out_shape=[jax.ShapeDtypeStruct((rows, cols), F32)] * 3,
    )(w, g, m, v)


MODR = 32


def _silu(v):
    return v * _sigmoid(v)


def mod_fwd(cond, w, b):
    n = w.shape[1]

    def body(c_ref, w_ref, b_ref, o_ref):
        o_ref[...] = _dot(_silu(c_ref[...]).astype(MXU), w_ref[...].astype(MXU)) + b_ref[...]

    return pl.pallas_call(
        body, name="mod_fwd", out_shape=jax.ShapeDtypeStruct((MODR, n), F32),
        in_specs=[_fullspec((MODR, D)), _fullspec((D, n)), _fullspec((1, n))], out_specs=_fullspec((MODR, n)),
        grid=(1,), compiler_params=_cp(40),
    )(cond, w, b)


def mod_wgrad(cond, dm):
    n = dm.shape[1]

    def body(c_ref, d_ref, o_ref):
        o_ref[...] = _dotg(_silu(c_ref[...]).astype(MXU), d_ref[...].astype(MXU), TN)

    return pl.pallas_call(
        body, name="mod_wgrad", out_shape=jax.ShapeDtypeStruct((D, n), F32),
        in_specs=[_fullspec((MODR, D)), _fullspec((MODR, n))], out_specs=_fullspec((D, n)),
        grid=(1,), compiler_params=_cp(40),
    )(cond, dm)


def mod_dgrad(dm, w):
    n = w.shape[1]

    def body(d_ref, w_ref, o_ref):
        o_ref[...] = _dotg(d_ref[...].astype(MXU), w_ref[...].astype(MXU), NT)

    return pl.pallas_call(
        body, name="mod_dgrad", out_shape=jax.ShapeDtypeStruct((8, D), F32),
        in_specs=[_fullspec((8, n)), _fullspec((D, n))], out_specs=_fullspec((8, D)),
        grid=(1,), compiler_params=_cp(40),
    )(dm, w)


def sum_leading(a, name="sum_leading"):
    n, r, c = a.shape

    def body(a_ref, o_ref):
        acc = a_ref[0]
        for k in range(1, n):
            acc = acc + a_ref[k]
        o_ref[...] = acc

    return pl.pallas_call(
        body, name=name, out_shape=jax.ShapeDtypeStruct((r, c), F32),
        in_specs=[_fullspec((n, r, c))], out_specs=_fullspec((r, c)), grid=(1,),
    )(a)


MESH = pl.DeviceIdType.MESH
NDEV = 8
ANY = pl.BlockSpec(memory_space=pl.ANY)


def _place():
    return lax.axis_index("x"), lax.axis_index("y"), lax.axis_index("c")


def _other_chips(x, y):
    return [(1 - x, y), (x, 1 - y), (1 - x, 1 - y)]


def allgather_small(v, name):
    r, cols = v.shape

    def body(v_ref, o_ref, send_sems, recv_sems):
        x, y, c = _place()
        me = 4 * x + 2 * y + c
        o_ref[me] = v_ref[...]
        copies = []
        for rel in range(1, NDEV):
            peer = (1 - x if rel & 4 else x, 1 - y if rel & 2 else y, 1 - c if rel & 1 else c)
            cp = pltpu.make_async_remote_copy(src_ref=v_ref, dst_ref=o_ref.at[me], send_sem=send_sems.at[rel - 1],
                                              recv_sem=recv_sems.at[rel - 1], device_id=peer, device_id_type=MESH)
            cp.start()
            copies.append(cp)
        for cp in copies:
            cp.wait_recv()
        for cp in copies:
            cp.wait_send()

    return pl.pallas_call(
        body, name=name, out_shape=jax.ShapeDtypeStruct((NDEV, r, cols), F32),
        in_specs=[pl.BlockSpec(memory_space=pltpu.VMEM)], out_specs=pl.BlockSpec(memory_space=pltpu.VMEM),
        scratch_shapes=[pltpu.SemaphoreType.DMA((NDEV - 1,)), pltpu.SemaphoreType.DMA((NDEV - 1,))],
        compiler_params=_cp(40),
    )(v)


def _sems(n):
    return [pltpu.SemaphoreType.DMA((n,)), pltpu.SemaphoreType.DMA((n,))]


def allgather_chips(v, name):
    r, cols = v.shape

    def body(v_ref, o_ref, send_sems, recv_sems):
        x, y, c = _place()
        k = 2 * x + y
        o_ref[k] = v_ref[...]
        copies = []
        for j, (px, py) in enumerate(_other_chips(x, y)):
            cp = pltpu.make_async_remote_copy(src_ref=v_ref, dst_ref=o_ref.at[k], send_sem=send_sems.at[j],
                                              recv_sem=recv_sems.at[j], device_id=(px, py, c), device_id_type=MESH)
            cp.start()
            copies.append(cp)
        for cp in copies:
            cp.wait_recv()
        for cp in copies:
            cp.wait_send()

    return pl.pallas_call(
        body, name=name, out_shape=jax.ShapeDtypeStruct((4, r, cols), F32),
        in_specs=[pl.BlockSpec(memory_space=pltpu.VMEM)], out_specs=pl.BlockSpec(memory_space=pltpu.VMEM),
        scratch_shapes=_sems(3), compiler_params=_cp(40),
    )(v)


def gather_job(arrs):
    n = len(arrs)

    def copy(srcs, outs, sems, i, slot, kk, cc, to, from_src=False):
        hr = arrs[i].shape[0] // 2
        dst = outs[i].at[kk, pl.ds(cc * hr, hr), :]
        return pltpu.make_async_remote_copy(src_ref=srcs[i].at[pl.ds(cc * hr, hr), :] if from_src else dst, dst_ref=dst,
                                            send_sem=sems[0].at[slot * n + i], recv_sem=sems[1].at[slot * n + i],
                                            device_id=to, device_id_type=MESH)

    def start(srcs, outs, sems):
        x, y, c = _place()
        for j, (px, py) in enumerate(_other_chips(x, y)):
            for i in range(n):
                copy(srcs, outs, sems, i, j, 2 * x + y, c, (px, py, c), True).start()

    def finish(srcs, outs, sems):
        x, y, c = _place()
        sib = (x, y, 1 - c)
        chips = _other_chips(x, y)
        passed = []
        for j, (px, py) in enumerate(chips):
            for i in range(n):
                copy(srcs, outs, sems, i, j, 2 * px + py, c, (px, py, c)).wait_recv()
                cp = copy(srcs, outs, sems, i, 3 + j, 2 * px + py, c, sib)
                cp.start()
                passed.append(cp)
        for j, (px, py) in enumerate(chips):
            for i in range(n):
                copy(srcs, outs, sems, i, 3 + j, 2 * px + py, 1 - c, sib).wait_recv()
        for j, (px, py) in enumerate(chips):
            for i in range(n):
                copy(srcs, outs, sems, i, j, 2 * x + y, c, (px, py, c), True).wait_send()
        for cp in passed:
            cp.wait_send()

    return _NS(ins=list(arrs), out_shapes=[jax.ShapeDtypeStruct((4,) + a.shape, a.dtype) for a in arrs], nsem=6 * n,
               start=start, finish=finish)


def chip_swap_job(ss):
    n = len(ss)

    def copies(srcs, outs, sems):
        x, y, c = _place()
        return [pltpu.make_async_remote_copy(src_ref=srcs[i].at[2 * px + py], dst_ref=outs[i].at[j],
                                             send_sem=sems[0].at[j * n + i], recv_sem=sems[1].at[j * n + i],
                                             device_id=(px, py, c), device_id_type=MESH)
                for j, (px, py) in enumerate(_other_chips(x, y)) for i in range(n)]

    def start(srcs, outs, sems):
        for cp in copies(srcs, outs, sems):
            cp.start()

    def finish(srcs, outs, sems):
        for cp in copies(srcs, outs, sems):
            cp.wait()

    return _NS(ins=list(ss), out_shapes=[jax.ShapeDtypeStruct((3,) + s.shape[1:], s.dtype) for s in ss], nsem=3 * n,
               start=start, finish=finish)


def run_job(job, name):
    n, m = len(job.ins), len(job.out_shapes)

    def body(*refs):
        srcs, outs, sems = refs[:n], refs[n:n + m], refs[n + m:]
        job.start(srcs, outs, sems)
        job.finish(srcs, outs, sems)

    return pl.pallas_call(body, name=name, out_shape=job.out_shapes, in_specs=[ANY] * n, out_specs=[ANY] * m,
                          scratch_shapes=_sems(job.nsem))(*job.ins)


def swap_core_halves(gs):
    n = len(gs)

    def body(*refs):
        srcs, outs = refs[:n], refs[n:2 * n]
        send_sems, recv_sems = refs[2 * n:]
        x, y, c = _place()
        copies = []
        for i in range(n):
            hr = gs[i].shape[1] // 2
            cp = pltpu.make_async_remote_copy(src_ref=srcs[i].at[:, pl.ds((1 - c) * hr, hr), :], dst_ref=outs[i],
                                              send_sem=send_sems.at[i], recv_sem=recv_sems.at[i],
                                              device_id=(x, y, 1 - c), device_id_type=MESH)
            cp.start()
            copies.append(cp)
        for cp in copies:
            cp.wait()

    return pl.pallas_call(
        body, name="swap_core_halves",
        out_shape=[jax.ShapeDtypeStruct((4, g.shape[1] // 2, g.shape[2]), g.dtype) for g in gs],
        in_specs=[ANY] * n, out_specs=[ANY] * n, scratch_shapes=_sems(n),
    )(*gs)


def add_half(g, r1, cidx, name):
    _, rows, cols = g.shape
    hr = rows // 2

    def body(c_ref, g_ref, r_ref, o_ref, ob_ref):
        s = g_ref[...] + r_ref[...]
        o_ref[...] = s
        ob_ref[...] = s.astype(BF16)

    blk = lambda f: pl.BlockSpec((1, hr, cols), f)
    return pl.pallas_call(
        body, name=name,
        out_shape=[jax.ShapeDtypeStruct((4, hr, cols), F32), jax.ShapeDtypeStruct((4, hr, cols), BF16)],
        grid_spec=pltpu.PrefetchScalarGridSpec(
            num_scalar_prefetch=1, grid=(4,),
            in_specs=[blk(lambda k, c_ref: (k, c_ref[0], 0)), blk(lambda k, c_ref: (k, 0, 0))],
            out_specs=[blk(lambda k, c_ref: (k, 0, 0)), blk(lambda k, c_ref: (k, 0, 0))]),
    )(cidx, g, r1)


def sum_parts(s1, r2, kidx, name):
    _, hr, cols = s1.shape

    def body(k_ref, s_ref, r_ref, o_ref):
        o_ref[...] = ((s_ref[0] + r_ref[0].astype(F32)) + r_ref[1].astype(F32)) + r_ref[2].astype(F32)

    return pl.pallas_call(
        body, name=name, out_shape=jax.ShapeDtypeStruct((hr, cols), F32),
        grid_spec=pltpu.PrefetchScalarGridSpec(
            num_scalar_prefetch=1, grid=(1,),
            in_specs=[pl.BlockSpec((1, hr, cols), lambda i, k_ref: (k_ref[0], 0, 0)),
                      pl.BlockSpec((3, hr, cols), lambda i, k_ref: (0, 0, 0))],
            out_specs=pl.BlockSpec((hr, cols), lambda i, k_ref: (0, 0))),
    )(kidx, s1, r2)


def swap_reduced_halves(hs):
    n = len(hs)

    def body(*refs):
        srcs, outs = refs[:n], refs[n:2 * n]
        send_sems, recv_sems = refs[2 * n:]
        x, y, c = _place()
        copies = []
        for i in range(n):
            cp = pltpu.make_async_remote_copy(src_ref=srcs[i], dst_ref=outs[i], send_sem=send_sems.at[i],
                                              recv_sem=recv_sems.at[i], device_id=(x, y, 1 - c), device_id_type=MESH)
            cp.start()
            copies.append(cp)
        for cp in copies:
            cp.wait()

    return pl.pallas_call(
        body, name="swap_reduced_halves", out_shape=[jax.ShapeDtypeStruct(h.shape, h.dtype) for h in hs],
        in_specs=[ANY] * n, out_specs=[ANY] * n, scratch_shapes=_sems(n),
    )(*hs)


def adamw_halves(w, m, v, own, oth, cidx, name):
    depth, rows, cols = w.shape
    hr = rows // 2
    tr = min(hr, 256)
    nblk = hr // tr
    bc1 = 1.0 - ADAM_B1 ** ADAM_STEP
    bc2 = 1.0 - ADAM_B2 ** ADAM_STEP

    def body(c_ref, w_ref, m_ref, v_ref, own0, own1, oth0, oth1, g_ref, d_ref, nm_ref, nv_ref):
        l = pl.program_id(0)
        hi = pl.program_id(1)
        mine = jnp.where(l == 0, own0[...], own1[...])
        other = jnp.where(l == 0, oth0[...], oth1[...])
        g_v = jnp.where(hi == c_ref[0], mine, other)
        nm = ADAM_B1 * m_ref[0] + (1.0 - ADAM_B1) * g_v
        nv = ADAM_B2 * v_ref[0] + (1.0 - ADAM_B2) * (g_v * g_v)
        g_ref[0] = g_v
        nm_ref[0] = nm
        nv_ref[0] = nv
        d_ref[0] = -ADAM_LR * ((nm / bc1) / (jnp.sqrt(nv / bc2) + ADAM_EPS) + ADAM_WD * w_ref[0])

    wspec = pl.BlockSpec((1, tr, cols), lambda l, hi, b, c_ref: (l, hi * nblk + b, 0))
    gspec = pl.BlockSpec((tr, cols), lambda l, hi, b, c_ref: (b, 0))
    assert depth == 2
    return pl.pallas_call(
        body, name=name, out_shape=[jax.ShapeDtypeStruct(w.shape, F32)] * 4,
        grid_spec=pltpu.PrefetchScalarGridSpec(
            num_scalar_prefetch=1, grid=(depth, 2, nblk),
            in_specs=[wspec] * 3 + [gspec] * 4, out_specs=[wspec] * 4),
    )(cidx, w, m, v, own[0], own[1], oth[0], oth[1])


class _NS:
    def __init__(self, **kw):
        self.__dict__.update(kw)


def _prep_in(win, conv_w, conv_b, dt_bias, a_log, ssd_d, ssd_nw, qnw, kvnw, pool_w, pool_scale, n1, n2):
    winp = jnp.concatenate([win[:, 0:384], win[:, 384:1280], win[:, 1292:1548], win[:, 1548:1804], win[:, 1836:2092],
                            win[:, 1804:1836], win[:, 1280:1292], jnp.zeros((D, NP - IN_COLS), win.dtype)], axis=1)
    wbd = (jnp.eye(4, dtype=F32)[:, None, :, None] * pool_w[:, :, None, :]).reshape(POOL_DIM, POOL_DIM).astype(MXU)
    a = -jnp.exp(a_log)
    return _NS(
        winp=winp, wbd=wbd,
        cw8=jnp.pad(conv_w, ((0, 4), (0, 0))), cb=conv_b[None],
        dtb=jnp.pad(dt_bias.reshape(1, 12), ((0, 0), (DT0, 128 - DT0 - 12))),
        arow=jnp.pad(a[:, None, :], ((0, 0), (0, 7), (0, 128 - SSD_HEADS))), a=a,
        dexp=jnp.repeat(ssd_d, SSD_P)[None], ssd_nw=ssd_nw[None], qnw=qnw[None], kvnw=kvnw[None],
        pscale=pool_scale[None], n1=n1[None], n2=n2[None])


def _prep_rest(wqb, wkvb, wout, w1, w2):
    wq = jnp.pad(wqb.reshape(256, MLA_HEADS, QK_DIM), ((0, 0), (0, 0), (0, HP - QK_DIM))).reshape(256, QW)
    kv3 = wkvb.reshape(256, MLA_HEADS, 128)
    wk = jnp.pad(kv3[:, :, :64], ((0, 0), (0, 0), (0, 64))).reshape(256, QW)
    wv = jnp.pad(kv3[:, :, 64:], ((0, 0), (0, 0), (0, 64))).reshape(256, QW)
    wo = jnp.concatenate([jnp.pad(wout[384:768].reshape(MLA_HEADS, 64, D), ((0, 0), (0, 64), (0, 0))).reshape(QW, D),
                          wout[0:384], wout[768:1024]], axis=0)
    return _NS(wq=wq, wk=wk, wv=wv, wo=wo, w1=w1, w2=w2)


def _prep_layer(win, wqb, wkvb, wout, w1, w2, *small):
    lw = _prep_in(win, *small)
    lw.__dict__.update(_prep_rest(wqb, wkvb, wout, w1, w2).__dict__)
    return lw


def _by_chip_cols(a):
    return jnp.stack([a[:, k * (a.shape[1] // 4):(k + 1) * (a.shape[1] // 4)] for k in range(4)])


def _by_chip_rows(a):
    return a.reshape(4, a.shape[0] // 4, a.shape[1])


def _unprep_in(dwinp):
    return jnp.concatenate([dwinp[:, 0:384], dwinp[:, 384:1280], dwinp[:, 2080:2092], dwinp[:, 1280:1536],
                            dwinp[:, 1536:1792], dwinp[:, 2048:2080], dwinp[:, 1792:2048]], axis=1)


def _unprep_rest(dwq, dwk, dwv, dwo):
    dwqb = dwq.reshape(256, MLA_HEADS, HP)[:, :, :QK_DIM].reshape(256, MLA_HEADS * QK_DIM)
    dwkvb = jnp.concatenate([dwk.reshape(256, MLA_HEADS, HP)[:, :, :64], dwv.reshape(256, MLA_HEADS, HP)[:, :, :64]],
                            axis=2).reshape(256, MLA_HEADS * 128)
    dwout = jnp.concatenate([dwo[QW:QW + 384], dwo[0:QW].reshape(MLA_HEADS, HP, D)[:, :64].reshape(384, D),
                             dwo[QW + 384:CAT]], axis=0)
    return dwqb, dwkvb, dwout


def _rope_tables(nb, N):
    t = jnp.arange(N, dtype=F32)
    row = jnp.floor(t / GRID_W)
    col = t - row * GRID_W
    inv = jnp.asarray(10000.0 ** (-np.arange(8, dtype=np.float32) / 8), F32)
    ang = jnp.stack([row[:, None] * inv, col[:, None] * inv], axis=1)
    cs, sn = jnp.cos(ang), jnp.sin(ang)
    zero = jnp.zeros_like(sn)
    lanes = lambda first, second: jnp.stack([first, second], axis=2).reshape(N, 32)
    pad = lambda a, fill: jnp.concatenate([jnp.full((N, 64), fill, F32), a, jnp.full((N, 32), fill, F32)], axis=1)
    tabs = []
    for tab, fill in ((pad(lanes(cs, cs), 1.0), 1.0), (pad(lanes(-sn, zero), 0.0), 0.0), (pad(lanes(zero, sn), 0.0), 0.0)):
        one = jnp.concatenate([jnp.full((CTX, 128), fill, F32), tab], axis=0)
        tabs.append(jnp.tile(one, (nb, 1)))
    return tabs


def _eexp():
    e = np.zeros((128, SSD_INNER), np.float32)
    for h in range(SSD_HEADS):
        e[h, h * SSD_P:(h + 1) * SSD_P] = 1.0
    return jnp.asarray(e)


class _NoHooks:
    def __init__(self, lws):
        self.lws = lws

    def weights_in(self, l):
        return _NS(**self.lws[l].__dict__)

    def weights_rest(self, l, scan_out):
        return self.lws[l]

    def job(self, where, l, early=None):
        return None

    def done(self, where, l, out):
        pass

    def layer_grads(self, l, g):
        pass


def _layer_fwd(X, bm, l, cst, hooks):
    nb, T, bps, N = cst.nb, cst.T, cst.bps, cst.N
    lw = hooks.weights_in(l)
    h1, pz, pxbc, pqa, pkva, ppool, plast = in_proj(X, bm, lw.n1, lw.winp)
    xs, bmat, cmat, dtv = ssd_prep(pxbc, plast, lw.cw8, lw.cb, lw.dtb, bps)
    y2, hin, out = ssd_scan_fwd(xs, bmat, cmat, dtv, lw.arow, cst.eexp, nb, T, hooks.job("fwd_scan", l))
    lw.__dict__.update(hooks.weights_rest(l, out).__dict__)
    ssd = ssd_out_fwd(y2, xs, pz, lw.dexp, lw.ssd_nw)
    q, k, v, cq, ckv = mla_prep(pqa, pkva, plast, lw.qnw, lw.kvnw, lw.wq, lw.wk, lw.wv, *cst.rope)
    attn, lse = attn_fwd(q, k, v, nb, T)
    pool = pool_fwd(ppool, lw.wbd, lw.pscale, bps, N)
    x1, mix, cat = mix_fwd(X, attn, ssd, pool, bm, lw.wo)
    x2, mo, r, h2, out = mlp_fwd(x1, bm, lw.n2, lw.w1, lw.w2, hooks.job("fwd_mlp", l))
    hooks.done("fwd_mlp", l, out)
    sv = _NS(X=X, h1=h1, pz=pz, pxbc=pxbc, pqa=pqa, pkva=pkva, ppool=ppool, plast=plast, xs=xs, bmat=bmat, cmat=cmat,
             dtv=dtv, y2=y2, hin=hin, q=q, k=k, v=v, cq=cq, ckv=ckv, attn=attn, lse=lse, x1=x1, mix=mix, cat=cat, mo=mo, r=r,
             h2=h2, lw=lw)
    return x2, sv


def _layer_bwd(dx2, bm, l, sv, cst, hooks):
    nb, T, bps, N = cst.nb, cst.T, cst.bps, cst.N
    lw = sv.lw
    dx1, du, dob, part_mlp, out = mlp_bwd(dx2, sv.x1, sv.mo, sv.r, bm, lw.n2, lw.w2, lw.w1, hooks.job("bwd_mlp", l))
    hooks.done("bwd_mlp", l, out)
    dw1 = mm_tn(sv.h2, du, name="wgrad_mlp1", col_blocks=True)
    dw2 = mm_tn(sv.r, dob, square_a=True, name="wgrad_mlp2")
    dattn, dssd, dpool, dwo, part_mix = mix_bwd(dx1, sv.mix, sv.cat, bm, lw.wo)
    dppool, dwbd, part_pool = pool_bwd(sv.ppool, dpool, lw.wbd, lw.pscale, bps, N)
    dq, dk, dv = attn_bwd(sv.q, sv.k, sv.v, sv.attn, sv.lse, dattn, nb, T)
    dpqa, dpkva, dkr, dwq, dwk, dwv, part_mla = mla_prep_bwd(dq, dk, dv, sv.pqa, sv.pkva, sv.cq, sv.ckv, lw.qnw, lw.kvnw,
                                                             lw.wq, lw.wk, lw.wv, *cst.rope)
    dwqb, dwkvb, dwout = _unprep_rest(dwq, dwk, dwv, dwo)
    early = dict(w_q_b=_by_chip_cols(dwqb), w_kv_b=_by_chip_cols(dwkvb), w_out=_by_chip_rows(dwout), w_mlp1=dw1,
                 w_mlp2=_by_chip_rows(dw2))
    dyy, dz, dxs_skip, part_so = ssd_out_bwd(dssd, sv.y2, sv.xs, sv.pz, lw.dexp, lw.ssd_nw)
    dxs2, dbm2, dcm2, ddt2, da, out = ssd_scan_bwd(sv.xs, sv.bmat, sv.cmat, sv.dtv, lw.arow, cst.eexp, sv.hin, dyy,
                                                   nb, T, hooks.job("bwd_scan", l, early))
    hooks.done("bwd_scan", l, out)
    dpre, dlast_dt, part_conv = ssd_prep_bwd_a(sv.pxbc, sv.plast, lw.cw8, lw.cb, lw.dtb, dxs_skip, dxs2, dbm2, dcm2,
                                               ddt2, bps)
    dpxbc = ssd_prep_bwd_b(dpre, lw.cw8, bps)
    dx, dwinp, part_in = in_proj_bwd(dx1, sv.X, sv.h1, dz, dpxbc, dpqa, dpkva, dppool, dkr, dlast_dt, bm, lw.n1, lw.winp)

    dmod = jnp.stack([part_in[:, 0], part_in[:, 1], part_mix[:, 0], part_mlp[:, 0], part_mlp[:, 1], part_mlp[:, 2]],
                     axis=1)
    dmod = dmod.reshape(nb, bps, 6, D)
    dm_rows = jnp.concatenate([jnp.sum(dmod[:, 1:], axis=1), jnp.sum(dmod[:, 0], axis=0)[None]], axis=0)
    da_dh = jnp.sum(da[:, :, 0, :SSD_HEADS], axis=1)
    conv_parts = jnp.sum(part_conv, axis=0)
    g = _NS(
        w_in=_by_chip_cols(_unprep_in(dwinp)), dm_rows=dm_rows.reshape(3, 6 * D), **early,
        norm1_w=jnp.sum(part_in[:, 2], axis=0), norm2_w=jnp.sum(part_mlp[:, 3], axis=0),
        conv_w=conv_parts[0:4], conv_b=conv_parts[4],
        dt_bias=conv_parts[5, DT0:DT0 + 12].reshape(2, SSD_HEADS), a_log=da_dh * lw.a,
        ssd_d=jnp.sum(jnp.sum(part_so[:, 1], axis=0).reshape(SSD_HEADS, SSD_P), axis=1),
        ssd_norm_w=jnp.sum(part_so[:, 0], axis=0),
        q_a_norm_w=jnp.sum(part_mla[:, 0], axis=0), kv_a_norm_w=jnp.sum(part_mla[:, 1], axis=0),
        pool_w=jnp.stack([dwbd[i * 64:(i + 1) * 64, i * 64:(i + 1) * 64] for i in range(4)]),
        pool_scale=jnp.sum(part_pool[:, 0], axis=0))
    hooks.layer_grads(l, g)
    return dx, g


def _local_step(x, ctx, tgt, bms, lws, fw, cst, hooks=None):
    nb, N = x.shape[0], x.shape[1]
    R = nb * cst.T
    hooks = _NoHooks(lws) if hooks is None else hooks
    X = jnp.concatenate([ctx, x], axis=1).reshape(R, D)
    saved = []
    for l in range(DEPTH):
        X, sv = _layer_fwd(X, bms[l], l, cst, hooks)
        saved.append(sv)
    dX, part_fin = final_loss(X, tgt.reshape(nb * N, D), fw[None], cst.bps)
    loss = (0.5 / D) * jnp.sum(part_fin[:, 1])
    dfw = jnp.sum(part_fin[:, 0], axis=0)
    grads = [None] * DEPTH
    for l in reversed(range(DEPTH)):
        dX, grads[l] = _layer_bwd(dX, bms[l], l, saved[l], cst, hooks)
    grad_x = dX.reshape(nb, cst.T, D)[:, CTX:, :]
    return loss, grad_x, grads, dfw


def _consts(nb, N):
    T = CTX + N
    bps = T // SB
    return _NS(nb=nb, N=N, T=T, bps=bps, eexp=_eexp(), rope=_rope_tables(nb, N))


def _block_mod(modrows, cst):
    rows = []
    for b in range(cst.nb):
        rows.append(modrows[cst.nb:cst.nb + 1])
        rows.append(jnp.broadcast_to(modrows[b:b + 1], (cst.bps - 1, 6, D)))
    return jnp.pad(jnp.concatenate(rows, axis=0), ((0, 0), (0, 2), (0, 0)))


SMALL = (("norm1_w", (2, D)), ("norm2_w", (2, D)), ("conv_w", (2, 4, XBC)), ("conv_b", (2, XBC)),
         ("dt_bias", (2, 2, 6)), ("a_log", (2, 2, 6)), ("ssd_d", (2, 6)), ("ssd_norm_w", (2, 384)),
         ("q_a_norm_w", (2, 256)), ("kv_a_norm_w", (2, 256)), ("pool_w", (2, 4, 64, 64)), ("pool_scale", (2, 256)),
         ("final_norm_w", (D,)), ("mod_b", (2, 6 * D)))
SMALL_ROWS = 64
DM_ROWS = 48


def _pack_small(vals):
    flat = jnp.concatenate([vals[n].reshape(-1) for n, _ in SMALL])
    return jnp.pad(flat, (0, SMALL_ROWS * D - flat.shape[0])).reshape(SMALL_ROWS, D)


def _unpack_small(p):
    flat = p.reshape(-1)
    out, off = {}, 0
    for n, shp in SMALL:
        size = int(np.prod(shp))
        out[n] = flat[off:off + size].reshape(shp)
        off += size
    return out


def cctx_grad(parts, c_ctx):
    def body(p_ref, c_ref, o_ref):
        acc = ((p_ref[0] + p_ref[1]) + p_ref[2]) + p_ref[3]
        v = c_ref[...]
        sig = _sigmoid(v)
        o_ref[...] = acc * (sig * (1.0 + v * (1.0 - sig)))

    return pl.pallas_call(
        body, name="cctx_grad", out_shape=jax.ShapeDtypeStruct((8, D), F32),
        in_specs=[_fullspec((4, 8, D)), _fullspec((1, D))], out_specs=_fullspec((8, D)), grid=(1,),
    )(parts, c_ctx)


def kernel(x, c, ctx, c_ctx, mod_w, mod_b, norm1_w, norm2_w, w_in, conv_w, conv_b, dt_bias, a_log, ssd_d, ssd_norm_w, q_a_norm_w, w_q_b, kv_a_norm_w, w_kv_b, pool_w, pool_scale, w_out, w_mlp1, w_mlp2, final_norm_w, loss_target, m_c_ctx, m_mod_w, m_mod_b, m_norm1_w, m_norm2_w, m_w_in, m_conv_w, m_conv_b, m_dt_bias, m_a_log, m_ssd_d, m_ssd_norm_w, m_q_a_norm_w, m_w_q_b, m_kv_a_norm_w, m_w_kv_b, m_pool_w, m_pool_scale, m_w_out, m_w_mlp1, m_w_mlp2, m_final_norm_w, v_c_ctx, v_mod_w, v_mod_b, v_norm1_w, v_norm2_w, v_w_in, v_conv_w, v_conv_b, v_dt_bias, v_a_log, v_ssd_d, v_ssd_norm_w, v_q_a_norm_w, v_w_q_b, v_kv_a_norm_w, v_w_kv_b, v_pool_w, v_pool_scale, v_w_out, v_w_mlp1, v_w_mlp2, v_final_norm_w):
    nb, N = x.shape[0], x.shape[1]
    cst = _consts(nb, N)
    xi, yi, ci = _place()
    me = 4 * xi + 2 * yi + ci
    kchip = 2 * xi + yi
    mcols = mod_w.shape[2]
    cshard = conv_w.shape[2]

    blk = jnp.zeros((16, D), F32).at[0:nb].set(c).at[8:16, 0:cshard].set(conv_w.reshape(8, cshard))
    g1 = allgather_small(blk, "gather_cond")
    cond = jnp.concatenate([g1[:, 0:nb].reshape(NDEV * nb, D), c_ctx[None],
                            jnp.zeros((MODR - NDEV * nb - 1, D), F32)], axis=0)
    conv_full = [jnp.concatenate([g1[2 * k, 8 + 4 * l:12 + 4 * l, 0:cshard] for k in range(4)], axis=1)
                 for l in range(DEPTH)]

    mb = [lax.dynamic_slice_in_dim(mod_b[l], kchip * mcols, mcols)[None] for l in range(DEPTH)]
    ms = jnp.concatenate([mod_fwd(cond, mod_w[l], mb[l]) for l in range(DEPTH)], axis=0)
    g2 = allgather_chips(ms, "gather_mod")
    bms = []
    for l in range(DEPTH):
        m_all = jnp.concatenate([g2[k, MODR * l:MODR * (l + 1)] for k in range(4)], axis=1)
        mine = jnp.concatenate([lax.dynamic_slice_in_dim(m_all, nb * me, nb), m_all[NDEV * nb:NDEV * nb + 1]], axis=0)
        bms.append(_block_mod(mine.reshape(nb + 1, 6, D), cst))

    assert DEPTH == 2
    big = (w_in, w_q_b, w_kv_b, w_out, w_mlp1, w_mlp2)
    names = ("w_in", "w_q_b", "w_kv_b", "w_out", "w_mlp1", "w_mlp2")
    concat_axis = dict(w_in=1, w_q_b=1, w_kv_b=1, w_out=0, w_mlp1=1, w_mlp2=0)
    cidx = jnp.reshape(ci, (1,)).astype(jnp.int32)
    kidx = jnp.reshape(kchip, (1,)).astype(jnp.int32)
    shards = [{n: a[l].astype(MXU) for n, a in zip(names, big)} for l in range(DEPTH)]

    def core_sums(gs):
        ns = list(gs)
        got = swap_core_halves([gs[n] for n in ns])
        return {n: add_half(gs[n], r, cidx, "add_half_" + n) for n, r in zip(ns, got)}

    class Hooks:
        gathered = [dict(w_in=run_job(gather_job([shards[0]["w_in"]]), "gather_w_in")[0]), {}]
        core_sum = [{}, {}]
        received = [{}, {}]

        def whole(self, l, n):
            return jnp.concatenate([jnp.where(kchip == k, shards[l][n], self.gathered[l][n][k]) for k in range(4)],
                                   axis=concat_axis[n])

        def weights_in(self, l):
            return _prep_in(self.whole(l, "w_in"), conv_full[l], conv_b[l], dt_bias[l], a_log[l], ssd_d[l], ssd_norm_w[l],
                            q_a_norm_w[l], kv_a_norm_w[l], pool_w[l], pool_scale[l], norm1_w[l], norm2_w[l])

        def weights_rest(self, l, scan_out):
            if l == 0:
                self.gathered[0].update(zip(names[1:], scan_out))
            return _prep_rest(*[self.whole(l, n) for n in names[1:]])

        def job(self, where, l, early=None):
            if l != 0:
                return None
            if where == "fwd_scan":
                return gather_job([shards[0][n] for n in names[1:]])
            if where == "fwd_mlp":
                return gather_job([shards[1][n] for n in names])
            if where == "bwd_mlp":
                return chip_swap_job([self.core_sum[1][n][1] for n in names])
            self.core_sum[0].update(core_sums(early))
            return chip_swap_job([self.core_sum[0][n][1] for n in names[1:]])

        def done(self, where, l, out):
            if l != 0:
                return
            if where == "fwd_mlp":
                self.gathered[1].update(zip(names, out))
            elif where == "bwd_mlp":
                self.received[1].update(zip(names, out))
            elif where == "bwd_scan":
                self.received[0].update(zip(names[1:], out))

        def layer_grads(self, l, g):
            if l == 1:
                self.core_sum[1] = core_sums({n: getattr(g, n) for n in names})
            else:
                self.core_sum[0].update(core_sums(dict(w_in=g.w_in)))
                self.received[0]["w_in"] = run_job(chip_swap_job([self.core_sum[0]["w_in"][1]]), "swap_w_in")[0]

    hooks = Hooks()
    loss_part, grad_x, grads, dfw = _local_step(x, ctx, loss_target, bms, None, final_norm_w, cst, hooks)
    loss = lax.psum(loss_part, ("x", "y", "c"))
    g_own = [sum_parts(hooks.core_sum[l][n][0], hooks.received[l][n], kidx, "sum_parts_" + n)
             for n in names for l in range(DEPTH)]
    g_oth = swap_reduced_halves(g_own)

    small = {n: jnp.stack([getattr(grads[l], n) for l in range(DEPTH)]) for n, _ in SMALL if n not in ("final_norm_w", "mod_b")}
    small["final_norm_w"] = dfw
    small["mod_b"] = jnp.stack([jnp.sum(grads[l].dm_rows, axis=0) for l in range(DEPTH)])
    dm = jnp.pad(jnp.concatenate([grads[l].dm_rows for l in range(DEPTH)], axis=0), ((0, 8 - 3 * DEPTH), (0, 0)))
    g3 = allgather_small(jnp.concatenate([_pack_small(small), dm.reshape(DM_ROWS, D)], axis=0), "gather_small")
    tot = sum_leading(g3, "sum_small")
    gsmall = _unpack_small(tot[0:SMALL_ROWS])
    ctx_sum = tot[SMALL_ROWS:].reshape(8, 6 * D)
    dm_dev = g3[:, SMALL_ROWS:].reshape(NDEV, 8, 6 * D)
    g_mod_w, dpart = [], jnp.zeros((8, D), F32)
    for l in range(DEPTH):
        dm_all = jnp.concatenate([dm_dev[:, 3 * l:3 * l + nb].reshape(NDEV * nb, 6 * D), ctx_sum[3 * l + nb:3 * l + nb + 1],
                                  jnp.zeros((MODR - NDEV * nb - 1, 6 * D), F32)], axis=0)
        g_mod_w.append(mod_wgrad(cond, lax.dynamic_slice_in_dim(dm_all, kchip * mcols, mcols, axis=1)))
        dctx = jnp.pad(lax.dynamic_slice_in_dim(ctx_sum[3 * l + nb:3 * l + nb + 1], kchip * mcols, mcols, axis=1), ((0, 7), (0, 0)))
        dpart = dpart + mod_dgrad(dctx, mod_w[l])
    g_c_ctx = cctx_grad(allgather_chips(dpart, "gather_cctx"), c_ctx[None])[0]

    res = {}
    moments = ((m_w_in, v_w_in), (m_w_q_b, v_w_q_b), (m_w_kv_b, v_w_kv_b), (m_w_out, v_w_out), (m_w_mlp1, v_w_mlp1),
               (m_w_mlp2, v_w_mlp2))
    for i, (n, w, (m, v)) in enumerate(zip(names, big, moments)):
        res[n] = tuple(adamw_halves(w, m, v, g_own[DEPTH * i:DEPTH * (i + 1)], g_oth[DEPTH * i:DEPTH * (i + 1)], cidx,
                                    "adamw_" + n))
    g_mw = jnp.stack(g_mod_w)
    r_mw = adamw(mod_w.reshape(-1, mcols), g_mw.reshape(-1, mcols), m_mod_w.reshape(-1, mcols),
                 v_mod_w.reshape(-1, mcols), name="adamw_mod_w")
    res["mod_w"] = (g_mw,) + tuple(a.reshape(mod_w.shape) for a in r_mw)

    given = dict(norm1_w=(norm1_w, m_norm1_w, v_norm1_w), norm2_w=(norm2_w, m_norm2_w, v_norm2_w),
                 conv_b=(conv_b, m_conv_b, v_conv_b), dt_bias=(dt_bias, m_dt_bias, v_dt_bias),
                 a_log=(a_log, m_a_log, v_a_log), ssd_d=(ssd_d, m_ssd_d, v_ssd_d),
                 ssd_norm_w=(ssd_norm_w, m_ssd_norm_w, v_ssd_norm_w), q_a_norm_w=(q_a_norm_w, m_q_a_norm_w, v_q_a_norm_w),
                 kv_a_norm_w=(kv_a_norm_w, m_kv_a_norm_w, v_kv_a_norm_w), pool_w=(pool_w, m_pool_w, v_pool_w),
                 pool_scale=(pool_scale, m_pool_scale, v_pool_scale),
                 final_norm_w=(final_norm_w, m_final_norm_w, v_final_norm_w), mod_b=(mod_b, m_mod_b, v_mod_b))
    zero_cw = jnp.zeros((2, 4, XBC), F32)
    packs = [_pack_small({n: (given[n][i] if n in given else zero_cw) for n, _ in SMALL}) for i in range(3)]
    r_small = [_unpack_small(a) for a in adamw(packs[0], tot[0:SMALL_ROWS], packs[1], packs[2], name="adamw_small")]
    for n in given:
        res[n] = (gsmall[n], r_small[0][n], r_small[1][n], r_small[2][n])

    g_cw = lax.dynamic_slice_in_dim(gsmall["conv_w"], kchip * cshard, cshard, axis=2)
    padcw = lambda a: jnp.pad(a.reshape(8, cshard), ((0, 0), (0, 256 - cshard)))
    r_cw = adamw(padcw(conv_w), padcw(g_cw), padcw(m_conv_w), padcw(v_conv_w), name="adamw_conv_w")
    res["conv_w"] = (g_cw,) + tuple(a[:, 0:cshard].reshape(conv_w.shape) for a in r_cw)
    r_cc = adamw(c_ctx.reshape(8, 128), g_c_ctx.reshape(8, 128), m_c_ctx.reshape(8, 128), v_c_ctx.reshape(8, 128),
                 name="adamw_c_ctx")
    res["c_ctx"] = (g_c_ctx,) + tuple(a.reshape(D) for a in r_cc)

    order = ("c_ctx", "mod_w", "mod_b", "norm1_w", "norm2_w", "w_in", "conv_w", "conv_b", "dt_bias", "a_log", "ssd_d",
             "ssd_norm_w", "q_a_norm_w", "w_q_b", "kv_a_norm_w", "w_kv_b", "pool_w", "pool_scale", "w_out", "w_mlp1",
             "w_mlp2", "final_norm_w")
    return (loss, grad_x) + tuple(res[n][i] for i in range(4) for n in order)
```

```python
import functools
import math

import numpy as np
import jax
import jax.numpy as jnp
from jax import lax
from jax.experimental import pallas as pl
from jax.experimental.pallas import tpu as pltpu

F32 = jnp.float32
BF16 = jnp.bfloat16
MXU = jnp.bfloat16

D = 1024
DEPTH = 2
GRID_W = 64
CTX = 256
EPS = 1e-6
SSD_HEADS = 6
SSD_P = 64
SSD_INNER = 384
SSD_N = 128
CHUNK = 128
XBC = 896
MLA_HEADS = 6
QK_NOPE = 64
QK_ROPE = 32
QK_DIM = 96
HP = 128
QW = MLA_HEADS * HP
POOL_DIM = 256
D_FF = 4096
FF_BLK = 1024
IN_COLS = 2092
NP = 2176
P_SPLITS = (384, 896, 256, 256, 256, 128)
DT0 = 32
CAT = QW + SSD_INNER + POOL_DIM

SB = 256
TM = 512
HALO = 8

ADAM_LR = 0.001
ADAM_B1 = 0.9
ADAM_B2 = 0.999
ADAM_EPS = 1e-08
ADAM_WD = 0.01
ADAM_STEP = 10

NT = (((1,), (1,)), ((), ()))
TN = (((0,), (0,)), ((), ()))


def _cp(vmem_mb=None):
    if vmem_mb is None:
        return pltpu.CompilerParams()
    return pltpu.CompilerParams(vmem_limit_bytes=vmem_mb << 20)


def _dot(a, b):
    return jnp.dot(a, b, preferred_element_type=F32)


def _dotg(a, b, dims):
    return lax.dot_general(a, b, dims, preferred_element_type=F32)


def _dot_hi(a, b, dims=None, sel_first=False):
    dims = (((1,), (0,)), ((), ())) if dims is None else dims
    v, s = (b, a) if sel_first else (a, b)
    hi = v.astype(BF16)
    lo = (v - hi.astype(F32)).astype(BF16)
    s = s.astype(BF16)
    if sel_first:
        return _dotg(s, hi, dims) + _dotg(s, lo, dims)
    return _dotg(hi, s, dims) + _dotg(lo, s, dims)


def _rms_hat(x):
    rstd = lax.rsqrt(jnp.mean(x * x, axis=-1, keepdims=True) + EPS)
    return x * rstd, rstd


def _rms_bwd(dn, xhat, rstd, w):
    dxhat = dn * w
    dx = rstd * (dxhat - xhat * jnp.mean(dxhat * xhat, axis=-1, keepdims=True))
    return dx, jnp.sum(dn * xhat, axis=0, keepdims=True)


def _sigmoid(z):
    return 1.0 / (1.0 + jnp.exp(-z))


def _colsum(a):
    return jnp.sum(a, axis=0, keepdims=True)


def _rowspec(cols, tm=TM):
    return pl.BlockSpec((tm, cols), lambda i: (i, 0))


def _fullspec(shape):
    n = len(shape)
    return pl.BlockSpec(shape, lambda *_: (0,) * n)


def _resident(shape):
    n = len(shape)
    return pl.BlockSpec(shape, lambda *_: (0,) * n, pipeline_mode=pl.Buffered(1))


def _halo_specs(cols, nrows, halo=HALO):
    per = SB // halo
    last = nrows // halo - 1
    prev = pl.BlockSpec((halo, cols), lambda i: (jnp.maximum(i * per - 1, 0), 0))
    nxt = pl.BlockSpec((halo, cols), lambda i: (jnp.minimum((i + 1) * per, last), 0))
    return prev, nxt


def _ext_rows(cur, prev, nxt, i, blocks_per_sample):
    j = i % blocks_per_sample
    first = jnp.logical_or(j == 0, j == 1)
    last = jnp.logical_or(j == 0, j == blocks_per_sample - 1)
    p = jnp.where(first, 0.0, prev.astype(F32))
    n = jnp.where(last, 0.0, nxt.astype(F32))
    return jnp.concatenate([p, cur.astype(F32), n], axis=0)


def _shift(ext, s):
    n = ext.shape[0]
    halo = (n - SB) // 2
    return pltpu.roll(ext, (-s) % n, axis=0)[halo:halo + SB, :]


def in_proj(x, bm, nw, w):
    R = x.shape[0]

    def body(x_ref, bm_ref, nw_ref, w_ref, h_ref, *outs):
        for s in range(TM // SB):
            rows = slice(s * SB, (s + 1) * SB)
            xhat, _ = _rms_hat(x_ref[rows, :])
            h = xhat * nw_ref[...] * (1.0 + bm_ref[s, 1:2, :]) + bm_ref[s, 0:1, :]
            h_ref[rows, :] = h.astype(h_ref.dtype)
        p = _dot(h_ref[...], w_ref[...])
        off = 0
        for o, n in zip(outs, P_SPLITS):
            o[...] = p[:, off:off + n].astype(o.dtype)
            off += n

    return pl.pallas_call(
        body, name="in_proj", grid=(R // TM,),
        in_specs=[_rowspec(D), pl.BlockSpec((TM // SB, 8, D), lambda i: (i, 0, 0)), _fullspec((1, D)),
                  _fullspec((D, NP))],
        out_specs=[_rowspec(D)] + [_rowspec(n) for n in P_SPLITS],
        out_shape=[jax.ShapeDtypeStruct((R, D), MXU)]
                  + [jax.ShapeDtypeStruct((R, n), dt) for n, dt in zip(P_SPLITS, (MXU, MXU, MXU, MXU, F32, F32))],
        compiler_params=_cp(56),
    )(x, bm, nw, w)


def in_proj_bwd(dx1, x, h, dz, dxbc, dqa, dkva, dpool, dkr, ddt, bm, nw, w):
    R = x.shape[0]

    def body(dx1_ref, x_ref, h_ref, dz_ref, dxbc_ref, dqa_ref, dkva_ref, dpool_ref, dkr_ref, ddt_ref, bm_ref, nw_ref,
             w_ref, dx_ref, dw_ref, part_ref, dp_ref):
        @pl.when(pl.program_id(0) == 0)
        def _():
            dw_ref[...] = jnp.zeros_like(dw_ref)

        dp_ref[:, 0:384] = dz_ref[...].astype(dp_ref.dtype)
        dp_ref[:, 384:1280] = dxbc_ref[...].astype(dp_ref.dtype)
        dp_ref[:, 1280:1536] = dqa_ref[...].astype(dp_ref.dtype)
        dp_ref[:, 1536:1792] = dkva_ref[...].astype(dp_ref.dtype)
        dp_ref[:, 1792:2048] = dpool_ref[...].astype(dp_ref.dtype)
        dp_ref[:, 2048:2176] = (dkr_ref[...] + ddt_ref[...]).astype(dp_ref.dtype)
        dw_ref[...] += _dotg(h_ref[...], dp_ref[...], TN)
        dh = _dotg(dp_ref[...], w_ref[...], NT)
        w = nw_ref[...]
        for s in range(TM // SB):
            rows = slice(s * SB, (s + 1) * SB)
            xhat, rstd = _rms_hat(x_ref[rows, :])
            dhs = dh[rows, :]
            sc1 = 1.0 + bm_ref[s, 1:2, :]
            dx, dnw = _rms_bwd(dhs * sc1, xhat, rstd, w)
            dx_ref[rows, :] = dx1_ref[rows, :] + dx
            part_ref[s] = jnp.concatenate(
                [_colsum(dhs), _colsum(dhs * xhat * w), dnw, jnp.zeros((5, D), F32)], axis=0)

    return pl.pallas_call(
        body, name="in_proj_bwd", grid=(R // TM,),
        in_specs=[_rowspec(D), _rowspec(D), _rowspec(D), _rowspec(384), _rowspec(896), _rowspec(256), _rowspec(256),
                  _rowspec(256), _rowspec(128), _rowspec(128),
                  pl.BlockSpec((TM // SB, 8, D), lambda i: (i, 0, 0)), _fullspec((1, D)), _resident((D, NP))],
        out_specs=[_rowspec(D), _fullspec((D, NP)), pl.BlockSpec((TM // SB, 8, D), lambda i: (i, 0, 0))],
        out_shape=[jax.ShapeDtypeStruct((R, D), F32), jax.ShapeDtypeStruct((D, NP), F32),
                   jax.ShapeDtypeStruct((R // SB, 8, D), F32)],
        scratch_shapes=[pltpu.VMEM((TM, NP), MXU)],
        compiler_params=_cp(56),
    )(dx1, x, h, dz, dxbc, dqa, dkva, dpool, dkr, ddt, bm, nw, w)


def mix_fwd(x, attn, ssd, pool, bm, wo):
    R = x.shape[0]

    def body(x_ref, a_ref, s_ref, p_ref, bm_ref, wo_ref, x1_ref, mix_ref, cat_ref):
        cat_ref[:, 0:QW] = a_ref[...].astype(cat_ref.dtype)
        cat_ref[:, QW:QW + SSD_INNER] = s_ref[...].astype(cat_ref.dtype)
        cat_ref[:, QW + SSD_INNER:CAT] = p_ref[...].astype(cat_ref.dtype)
        mix = _dot(cat_ref[...], wo_ref[...])
        mix_ref[...] = mix.astype(mix_ref.dtype)
        for s in range(TM // SB):
            rows = slice(s * SB, (s + 1) * SB)
            x1_ref[rows, :] = x_ref[rows, :] + bm_ref[s, 2:3, :] * mix[rows, :]

    return pl.pallas_call(
        body, name="mix_fwd", grid=(R // TM,),
        in_specs=[_rowspec(D), _rowspec(QW), _rowspec(SSD_INNER), _rowspec(POOL_DIM),
                  pl.BlockSpec((TM // SB, 8, D), lambda i: (i, 0, 0)), _fullspec((CAT, D))],
        out_specs=[_rowspec(D), _rowspec(D), _rowspec(CAT)],
        out_shape=[jax.ShapeDtypeStruct((R, D), F32), jax.ShapeDtypeStruct((R, D), MXU),
                   jax.ShapeDtypeStruct((R, CAT), MXU)],
        compiler_params=_cp(48),
    )(x, attn, ssd, pool, bm, wo)


def mix_bwd(dx1, mix, cat, bm, wo):
    R = dx1.shape[0]

    def body(dx1_ref, mix_ref, cat_ref, bm_ref, wo_ref, da_ref, ds_ref, dpl_ref, dw_ref, part_ref, dmb_ref):
        @pl.when(pl.program_id(0) == 0)
        def _():
            dw_ref[...] = jnp.zeros_like(dw_ref)

        for s in range(TM // SB):
            rows = slice(s * SB, (s + 1) * SB)
            d = dx1_ref[rows, :]
            dmb_ref[rows, :] = (d * bm_ref[s, 2:3, :]).astype(dmb_ref.dtype)
            part_ref[s] = jnp.concatenate([_colsum(d * mix_ref[rows, :].astype(F32)), jnp.zeros((7, D), F32)], axis=0)
        dw_ref[...] += _dotg(cat_ref[...], dmb_ref[...], TN)
        dcat = _dotg(dmb_ref[...], wo_ref[...], NT)
        da_ref[...] = dcat[:, 0:QW]
        ds_ref[...] = dcat[:, QW:QW + SSD_INNER]
        dpl_ref[...] = dcat[:, QW + SSD_INNER:CAT]

    return pl.pallas_call(
        body, name="mix_bwd", grid=(R // TM,),
        in_specs=[_rowspec(D), _rowspec(D), _rowspec(CAT), pl.BlockSpec((TM // SB, 8, D), lambda i: (i, 0, 0)),
                  _resident((CAT, D))],
        out_specs=[_rowspec(QW), _rowspec(SSD_INNER), _rowspec(POOL_DIM), _fullspec((CAT, D)),
                   pl.BlockSpec((TM // SB, 8, D), lambda i: (i, 0, 0))],
        out_shape=[jax.ShapeDtypeStruct((R, QW), F32), jax.ShapeDtypeStruct((R, SSD_INNER), F32),
                   jax.ShapeDtypeStruct((R, POOL_DIM), F32), jax.ShapeDtypeStruct((CAT, D), F32),
                   jax.ShapeDtypeStruct((R // SB, 8, D), F32)],
        scratch_shapes=[pltpu.VMEM((TM, D), MXU)],
        compiler_params=_cp(48),
    )(dx1, mix, cat, bm, wo)


def mlp_fwd(x1, bm, nw, w1, w2, side=None):
    R = x1.shape[0]

    def body(x1_ref, bm_ref, nw_ref, w1_ref, w2_ref, x2_ref, mo_ref, r_ref, h2_ref):
        for s in range(TM // SB):
            rows = slice(s * SB, (s + 1) * SB)
            xhat, _ = _rms_hat(x1_ref[rows, :])
            h = xhat * nw_ref[...] * (1.0 + bm_ref[s, 4:5, :]) + bm_ref[s, 3:4, :]
            h2_ref[rows, :] = h.astype(h2_ref.dtype)
        for j in range(D_FF // FF_BLK):
            cols = slice(j * FF_BLK, (j + 1) * FF_BLK)
            r = jnp.maximum(_dot(h2_ref[...], w1_ref[:, cols]), 0.0)
            r_ref[:, cols] = r.astype(r_ref.dtype)
            d = _dot((r * r).astype(MXU), w2_ref[cols, :])
            if j == 0:
                x2_ref[...] = d
            else:
                x2_ref[...] += d
        mo_ref[...] = x2_ref[...].astype(mo_ref.dtype)
        for s in range(TM // SB):
            rows = slice(s * SB, (s + 1) * SB)
            x2_ref[rows, :] = x1_ref[rows, :] + bm_ref[s, 5:6, :] * x2_ref[rows, :]

    grid = (R // TM,)
    body, side_in, side_out, side_shapes, side_scratch, side_args = _side_wrap(body, 5, 4, 0, side, grid)
    outs = pl.pallas_call(
        body, name="mlp_fwd" if side is None else "mlp_fwd_comm", grid=grid,
        in_specs=[_rowspec(D), pl.BlockSpec((TM // SB, 8, D), lambda i: (i, 0, 0)), _fullspec((1, D)),
                  _resident((D, D_FF)), _resident((D_FF, D))] + side_in,
        out_specs=[_rowspec(D), _rowspec(D), _rowspec(D_FF), _rowspec(D)] + side_out,
        out_shape=[jax.ShapeDtypeStruct((R, D), F32), jax.ShapeDtypeStruct((R, D), MXU),
                   jax.ShapeDtypeStruct((R, D_FF), BF16), jax.ShapeDtypeStruct((R, D), MXU)] + side_shapes,
        scratch_shapes=side_scratch,
        compiler_params=_cp(56),
    )(x1, bm, nw, w1, w2, *side_args)
    return tuple(outs[:4]) + (list(outs[4:]),)


def mlp_bwd(dx2, x1, mo, r, bm, nw, w2, w1, side=None):
    R = x1.shape[0]

    def body(dx2_ref, x1_ref, mo_ref, r_ref, bm_ref, nw_ref, w2_ref, w1_ref, dx1_ref, du_ref, dob_ref, part_ref,
             acc_ref):
        for s in range(TM // SB):
            rows = slice(s * SB, (s + 1) * SB)
            dob_ref[rows, :] = (dx2_ref[rows, :] * bm_ref[s, 5:6, :]).astype(dob_ref.dtype)
        for j in range(D_FF // FF_BLK):
            cols = slice(j * FF_BLK, (j + 1) * FF_BLK)
            du = _dotg(dob_ref[...], w2_ref[cols, :], NT) * (2.0 * r_ref[:, cols].astype(F32))
            du_ref[:, cols] = du.astype(du_ref.dtype)
            d = _dotg(du_ref[:, cols], w1_ref[:, cols], NT)
            if j == 0:
                acc_ref[...] = d
            else:
                acc_ref[...] += d
        w = nw_ref[...]
        for s in range(TM // SB):
            rows = slice(s * SB, (s + 1) * SB)
            xhat, rstd = _rms_hat(x1_ref[rows, :])
            dh = acc_ref[rows, :]
            dx, dnw = _rms_bwd(dh * (1.0 + bm_ref[s, 4:5, :]), xhat, rstd, w)
            d2 = dx2_ref[rows, :]
            dx1_ref[rows, :] = d2 + dx
            part_ref[s] = jnp.concatenate(
                [_colsum(dh), _colsum(dh * xhat * w), _colsum(d2 * mo_ref[rows, :].astype(F32)), dnw,
                 jnp.zeros((4, D), F32)], axis=0)

    grid = (R // TM,)
    body, side_in, side_out, side_shapes, side_scratch, side_args = _side_wrap(body, 8, 4, 1, side, grid)
    outs = pl.pallas_call(
        body, name="mlp_bwd" if side is None else "mlp_bwd_comm", grid=grid,
        in_specs=[_rowspec(D), _rowspec(D), _rowspec(D), _rowspec(D_FF),
                  pl.BlockSpec((TM // SB, 8, D), lambda i: (i, 0, 0)), _fullspec((1, D)),
                  _resident((D_FF, D)), _resident((D, D_FF))] + side_in,
        out_specs=[_rowspec(D), _rowspec(D_FF), _rowspec(D), pl.BlockSpec((TM // SB, 8, D), lambda i: (i, 0, 0))]
                  + side_out,
        out_shape=[jax.ShapeDtypeStruct((R, D), F32), jax.ShapeDtypeStruct((R, D_FF), MXU),
                   jax.ShapeDtypeStruct((R, D), MXU), jax.ShapeDtypeStruct((R // SB, 8, D), F32)] + side_shapes,
        scratch_shapes=[pltpu.VMEM((TM, D), F32)] + side_scratch,
        compiler_params=_cp(56),
    )(dx2, x1, mo, r, bm, nw, w2, w1, *side_args)
    return tuple(outs[:4]) + (list(outs[4:]),)


def mm_tn(a, b, square_a=False, name="mm_tn", col_blocks=False):
    R, M = a.shape
    N = b.shape[1]
    tm = M if M <= 1408 else 1024
    tn = N if N <= 2176 else 1024
    tk = next((c for c in ((2176, 1088, 512) if tm + tn <= 2048 else (1088, 512)) if R % c == 0), R)
    assert not col_blocks or tm == M

    def body(a_ref, b_ref, o_ref):
        @pl.when(pl.program_id(2) == 0)
        def _():
            o_ref[...] = jnp.zeros_like(o_ref)

        av = a_ref[...]
        if square_a:
            av = av.astype(F32)
            av = (av * av).astype(MXU)
        prod = _dotg(av.astype(MXU), b_ref[...].astype(MXU), TN)
        if col_blocks:
            o_ref[0] += prod
        else:
            o_ref[...] += prod

    if col_blocks:
        out_spec = pl.BlockSpec((1, tm, tn), lambda i, j, k: (j, 0, 0))
        out_shape = jax.ShapeDtypeStruct((N // tn, M, tn), F32)
    else:
        out_spec = pl.BlockSpec((tm, tn), lambda i, j, k: (i, j))
        out_shape = jax.ShapeDtypeStruct((M, N), F32)
    return pl.pallas_call(
        body, name=name, grid=(M // tm, N // tn, R // tk),
        in_specs=[pl.BlockSpec((tk, tm), lambda i, j, k: (k, i)), pl.BlockSpec((tk, tn), lambda i, j, k: (k, j))],
        out_specs=out_spec, out_shape=out_shape,
        compiler_params=_cp(48),
    )(a, b)


def final_loss(x, tgt, fw, blocks_per_sample):
    R = x.shape[0]
    nxb = blocks_per_sample - 1

    def body(x_ref, t_ref, fw_ref, dx_ref, part_ref):
        i = pl.program_id(0)
        is_ctx = (i % blocks_per_sample) == 0
        xhat, rstd = _rms_hat(x_ref[...])
        w = fw_ref[...]
        err = xhat * w - t_ref[...]
        dx, dfw = _rms_bwd(err * (1.0 / D), xhat, rstd, w)
        keep = jnp.where(is_ctx, 0.0, 1.0)
        dx_ref[...] = dx * keep
        part_ref[0] = jnp.concatenate([dfw * keep, _colsum(err * err) * keep, jnp.zeros((6, D), F32)], axis=0)

    def tmap(i):
        return ((i // blocks_per_sample) * nxb + jnp.maximum(i % blocks_per_sample - 1, 0), 0)

    return pl.pallas_call(
        body, name="final_loss", grid=(R // SB,),
        in_specs=[_rowspec(D, SB), pl.BlockSpec((SB, D), tmap), _fullspec((1, D))],
        out_specs=[_rowspec(D, SB), pl.BlockSpec((1, 8, D), lambda i: (i, 0, 0))],
        out_shape=[jax.ShapeDtypeStruct((R, D), F32), jax.ShapeDtypeStruct((R // SB, 8, D), F32)],
    )(x, tgt, fw)


def _softplus(v):
    return jnp.maximum(v, 0.0) + jnp.log(1.0 + jnp.exp(-jnp.abs(v)))


def _conv_taps(ext):
    return [_shift(ext, k - 1) for k in range(4)]


def _conv_out(taps, cw_ref, cb_ref):
    return (cb_ref[...] + cw_ref[0:1, :] * taps[0] + cw_ref[1:2, :] * taps[1] + cw_ref[2:3, :] * taps[2]
            + cw_ref[3:4, :] * taps[3])


def _dt_dir(v, d):
    lane = lax.broadcasted_iota(jnp.int32, v.shape, 1)
    return jnp.where(lane < SSD_HEADS, pltpu.roll(v, (128 - DT0 - SSD_HEADS * d) % 128, axis=1), 0.0)


def ssd_prep(pxbc, plast, cw, cb, dtb, blocks_per_sample):
    R = pxbc.shape[0]
    prev, nxt = _halo_specs(XBC, R, 8 * 4 // pxbc.dtype.itemsize)

    def body(cur_ref, prev_ref, nxt_ref, pl_ref, cw_ref, cb_ref, dtb_ref, xs_ref, bm_ref, cm_ref, dt_ref):
        i = pl.program_id(0)
        ext = _ext_rows(cur_ref[...], prev_ref[...], nxt_ref[...], i, blocks_per_sample)
        co = _conv_out(_conv_taps(ext), cw_ref, cb_ref)
        a = co * _sigmoid(co)
        xs_ref[...] = a[:, 0:384]
        bm_ref[...] = a[:, 384:640]
        cm_ref[...] = a[:, 640:896]
        sp = _softplus(pl_ref[...] + dtb_ref[...])
        dt_ref[0] = _dt_dir(sp, 0)
        dt_ref[1] = _dt_dir(sp, 1)

    return pl.pallas_call(
        body, name="ssd_prep", grid=(R // SB,),
        in_specs=[_rowspec(XBC, SB), prev, nxt, _rowspec(128, SB), _fullspec((8, XBC)), _fullspec((1, XBC)),
                  _fullspec((1, 128))],
        out_specs=[_rowspec(384, SB), _rowspec(256, SB), _rowspec(256, SB),
                   pl.BlockSpec((2, SB, 128), lambda i: (0, i, 0))],
        out_shape=[jax.ShapeDtypeStruct((R, 384), F32), jax.ShapeDtypeStruct((R, 256), F32),
                   jax.ShapeDtypeStruct((R, 256), F32), jax.ShapeDtypeStruct((2, R, 128), F32)],
    )(pxbc, pxbc, pxbc, plast, cw, cb, dtb)


def _chunk_index(d, s, nc):
    nctx = CTX // CHUNK
    back = jnp.where(s < nctx, nctx - 1 - s, nc + nctx - 1 - s)
    return jnp.where(d == 0, s, back)


def _scan_common(d, dt, arow, eexp, xs):
    ii = lax.broadcasted_iota(jnp.int32, (CHUNK, CHUNK), 0)
    jj = lax.broadcasted_iota(jnp.int32, (CHUNK, CHUNK), 1)
    mask = ((ii - jj) * (1 - 2 * d)) >= 0
    adt = dt * arow
    tmat = jnp.where(mask, 1.0, 0.0)
    cs = _dot_hi(tmat, adt, sel_first=True)
    tot = _colsum(adt)
    dtx = _dot_hi(dt, eexp)
    xt = xs * dtx
    ecs = jnp.exp(cs)
    ecx = _dot_hi(ecs, eexp)
    dte = jnp.exp(tot - cs)
    dtex = _dot_hi(dte, eexp)
    etot = jnp.exp(tot)
    etx = _dot_hi(jnp.broadcast_to(etot, (8, 128)), eexp)[0:1, :]
    return mask, tmat, adt, cs, tot, dtx, xt, ecs, ecx, dte, dtex, etot, etx


def _decay_matrix(mask, cs, cst, h):
    return jnp.exp(jnp.where(mask, cs[:, h:h + 1] - cst[h:h + 1, :], -1e30))


def _side_wrap(body, n_in, n_out, n_scratch, side, grid):
    if side is None:
        return body, [], [], [], [], []
    ni, no = len(side.ins), len(side.out_shapes)

    def wrapped(*refs):
        ins, refs = refs[:n_in], refs[n_in:]
        side_ins, refs = refs[:ni], refs[ni:]
        outs, refs = refs[:n_out], refs[n_out:]
        side_outs, refs = refs[:no], refs[no:]
        scratch, sems = refs[:n_scratch], refs[n_scratch:]
        ids = [pl.program_id(a) for a in range(len(grid))]
        first = functools.reduce(jnp.logical_and, [i == 0 for i in ids])
        last = functools.reduce(jnp.logical_and, [i == g - 1 for i, g in zip(ids, grid)])
        pl.when(first)(lambda: side.start(side_ins, side_outs, sems))
        body(*ins, *outs, *scratch)
        pl.when(last)(lambda: side.finish(side_ins, side_outs, sems))

    return wrapped, [ANY] * ni, [ANY] * no, list(side.out_shapes), _sems(side.nsem), list(side.ins)


def ssd_scan_fwd(xs, bm, cm, dtv, arow, eexp, nb, T, side=None):
    R = xs.shape[0]
    nc = T // CHUNK
    B = range(nb)

    def body(xs_ref, bm_ref, cm_ref, dt_ref, a_ref, e_ref, y_ref, hin_ref, st_ref):
        d = pl.program_id(0)
        s = pl.program_id(1)

        @pl.when(s == 0)
        def _():
            st_ref[...] = jnp.zeros_like(st_ref)

        eexp = e_ref[...]
        com = [_scan_common(d, dt_ref[0, b], a_ref[0, 0:1, :], eexp, xs_ref[b]) for b in B]
        mask = com[0][0]
        cs = [com[b][3] for b in B]
        cst = [cs[b].T for b in B]
        sin = [st_ref[b] for b in B]
        for b in B:
            hin_ref[0, b] = sin[b]
        sb = [sin[b].astype(MXU) for b in B]
        xtb = [com[b][6].astype(MXU) for b in B]
        xw = [(com[b][6] * com[b][10]).astype(MXU) for b in B]
        g0 = lax.broadcasted_iota(jnp.int32, (CHUNK, SSD_INNER), 1) < 192
        lane = lax.broadcasted_iota(jnp.int32, (CHUNK, 128), 1)
        c = [[cm_ref[b, :, 0:128].astype(MXU), cm_ref[b, :, 128:256].astype(MXU)] for b in B]
        bq = [[bm_ref[b, :, 0:128].astype(MXU), bm_ref[b, :, 128:256].astype(MXU)] for b in B]
        y = [jnp.where(g0, _dot(c[b][0], sb[b]), _dot(c[b][1], sb[b])) * com[b][8] for b in B]
        cb = [[_dotg(c[b][g], bq[b][g], NT) for g in range(2)] for b in B]
        blocks = [[] for _ in B]
        for blk in range(3):
            acc = [None for _ in B]
            for hh in range(2):
                h = blk * 2 + hh
                for b in B:
                    m = (cb[b][h // 3] * _decay_matrix(mask, cs[b], cst[b], h)).astype(MXU)
                    res = _dot(m, xtb[b][:, blk * 128:(blk + 1) * 128])
                    acc[b] = res if hh == 0 else jnp.where(lane < 64, acc[b], res)
            for b in B:
                blocks[b].append(acc[b])
        for b in B:
            y_ref[0, b] = y[b] + jnp.concatenate(blocks[b], axis=1)
            st_ref[b] = sin[b] * com[b][12] + jnp.where(g0, _dotg(bq[b][0], xw[b], TN), _dotg(bq[b][1], xw[b], TN))

    def rows(cols):
        return pl.BlockSpec((nb, CHUNK, cols), lambda d, s: (0, _chunk_index(d, s, nc), 0))

    def by_dir(cols):
        return pl.BlockSpec((1, nb, CHUNK, cols), lambda d, s: (d, 0, _chunk_index(d, s, nc), 0))

    grid = (2, nc)
    body, side_in, side_out, side_shapes, side_scratch, side_args = _side_wrap(body, 6, 2, 1, side, grid)
    outs = pl.pallas_call(
        body, name="ssd_scan_fwd" if side is None else "ssd_scan_fwd_comm", grid=grid,
        in_specs=[rows(384), rows(256), rows(256), by_dir(128), pl.BlockSpec((1, 8, 128), lambda d, s: (d, 0, 0)),
                  pl.BlockSpec((128, 384), lambda d, s: (0, 0))] + side_in,
        out_specs=[by_dir(384),
                   pl.BlockSpec((1, nb, CHUNK, 384), lambda d, s: (d * nc + _chunk_index(d, s, nc), 0, 0, 0))] + side_out,
        out_shape=[jax.ShapeDtypeStruct((2, nb, T, 384), F32), jax.ShapeDtypeStruct((2 * nc, nb, CHUNK, 384), F32)]
                  + side_shapes,
        scratch_shapes=[pltpu.VMEM((nb, CHUNK, 384), F32)] + side_scratch,
    )(xs.reshape(nb, T, 384), bm.reshape(nb, T, 256), cm.reshape(nb, T, 256), dtv.reshape(2, nb, T, 128), arow, eexp,
      *side_args)
    return outs[0].reshape(2, R, 384), outs[1], list(outs[2:])


def ssd_scan_bwd(xs, bm, cm, dtv, arow, eexp, hin, dy, nb, T, side=None):
    R = xs.shape[0]
    nc = T // CHUNK
    B = range(nb)

    def chunk(d, s):
        return _chunk_index(d, nc - 1 - s, nc)

    def body(xs_ref, bm_ref, cm_ref, dt_ref, a_ref, e_ref, hin_ref, dy_ref,
             dxs_ref, dbm_ref, dcm_ref, ddt_ref, da_ref, ds_ref):
        d = pl.program_id(0)
        s = pl.program_id(1)

        @pl.when(s == 0)
        def _():
            ds_ref[...] = jnp.zeros_like(ds_ref)
            da_ref[...] = jnp.zeros_like(da_ref)

        eexp = e_ref[...]
        arow = a_ref[0, 0:1, :]
        dt = [dt_ref[0, b] for b in B]
        xs_v = [xs_ref[b] for b in B]
        com = [_scan_common(d, dt[b], arow, eexp, xs_v[b]) for b in B]
        mask, tmat = com[0][0], com[0][1]
        cs, dtx, xt, ecs, ecx, dte, dtex, etot, etx = [[com[b][i] for b in B] for i in (3, 5, 6, 7, 8, 9, 10, 11, 12)]
        cst = [cs[b].T for b in B]
        sin = [hin_ref[0, b] for b in B]
        sb = [sin[b].astype(MXU) for b in B]
        dsp = [ds_ref[b] for b in B]
        dyv = [dy_ref[b] for b in B]
        xtb = [xt[b].astype(MXU) for b in B]
        xw = [(xt[b] * dtex[b]).astype(MXU) for b in B]
        g0 = lax.broadcasted_iota(jnp.int32, (CHUNK, SSD_INNER), 1) < 192
        lane = lax.broadcasted_iota(jnp.int32, (CHUNK, 128), 1)
        sub = lax.broadcasted_iota(jnp.int32, (CHUNK, 128), 0)
        c = [[cm_ref[b, :, 0:128].astype(MXU), cm_ref[b, :, 128:256].astype(MXU)] for b in B]
        bq = [[bm_ref[b, :, 0:128].astype(MXU), bm_ref[b, :, 128:256].astype(MXU)] for b in B]

        cs_prod = [jnp.where(g0, _dot(c[b][0], sb[b]), _dot(c[b][1], sb[b])) for b in B]
        dcsp = [dyv[b] * ecx[b] for b in B]
        dcsp_g = [[jnp.where(g0, dcsp[b], 0.0).astype(MXU), jnp.where(g0, 0.0, dcsp[b]).astype(MXU)] for b in B]
        dcs = [_dot_hi(dyv[b] * cs_prod[b], eexp, NT) * ecs[b] for b in B]
        dc = [[_dotg(dcsp_g[b][g], sb[b], NT) for g in range(2)] for b in B]
        dsin = [_dotg(c[b][0], dcsp_g[b][0], TN) + _dotg(c[b][1], dcsp_g[b][1], TN) + dsp[b] * etx[b] for b in B]

        dtot = [_dot_hi(jnp.broadcast_to(_colsum(dsp[b] * sin[b]), (8, SSD_INNER)), eexp, NT)[0:1, :] * etot[b] for b in B]
        dsp_g = [[jnp.where(g0, dsp[b], 0.0).astype(MXU), jnp.where(g0, 0.0, dsp[b]).astype(MXU)] for b in B]
        dxw = [_dot(bq[b][0], dsp_g[b][0]) + _dot(bq[b][1], dsp_g[b][1]) for b in B]
        db = [[_dotg(xw[b], dsp_g[b][g], NT) for g in range(2)] for b in B]
        dxt = [dxw[b] * dtex[b] for b in B]
        ddte = [_dot_hi(dxw[b] * xt[b], eexp, NT) * dte[b] for b in B]
        dtot = [dtot[b] + _colsum(ddte[b]) for b in B]
        dcs = [dcs[b] - ddte[b] for b in B]

        cb = [[_dotg(c[b][g], bq[b][g], NT) for g in range(2)] for b in B]
        dg = [[jnp.zeros((CHUNK, CHUNK), F32), jnp.zeros((CHUNK, CHUNK), F32)] for _ in B]
        dcs_rows = [jnp.zeros((CHUNK, 128), F32) for _ in B]
        dxt_blocks = [[] for _ in B]
        for blk in range(3):
            acc = [jnp.zeros((CHUNK, 128), F32) for _ in B]
            for hh in range(2):
                h = blk * 2 + hh
                g = h // 3
                mine = (lane < 64) if hh == 0 else (lane >= 64)
                for b in B:
                    dyh = jnp.where(mine, dyv[b][:, blk * 128:(blk + 1) * 128], 0.0).astype(MXU)
                    lh = _decay_matrix(mask, cs[b], cst[b], h)
                    m = cb[b][g] * lh
                    dm = _dotg(dyh, xtb[b][:, blk * 128:(blk + 1) * 128], NT)
                    acc[b] = acc[b] + _dotg(m.astype(MXU), dyh, TN)
                    dg[b][g] = dg[b][g] + dm * lh
                    q = dm * m
                    dcs[b] = dcs[b] + jnp.where(lane == h, jnp.sum(q, axis=1, keepdims=True), 0.0)
                    dcs_rows[b] = dcs_rows[b] - jnp.where(sub == h, jnp.sum(q, axis=0, keepdims=True), 0.0)
            for b in B:
                dxt_blocks[b].append(acc[b])
        for b in B:
            dxt[b] = dxt[b] + jnp.concatenate(dxt_blocks[b], axis=1)
            for g in range(2):
                dgb = dg[b][g].astype(MXU)
                dc[b][g] = dc[b][g] + _dot(dgb, bq[b][g])
                db[b][g] = db[b][g] + _dotg(dgb, c[b][g], TN)
            dcs[b] = dcs[b] + dcs_rows[b].T

        for b in B:
            dadt = _dot_hi(tmat, dcs[b], TN, sel_first=True) + dtot[b]
            ddt_ref[0, b] = dadt * arow + _dot_hi(dxt[b] * xs_v[b], eexp, NT)
            da_ref[0, b, 0:1, :] += _colsum(dadt * dt[b])
            dxs_ref[0, b] = (dxt[b] * dtx[b]).astype(dxs_ref.dtype)
            dbm_ref[0, b] = jnp.concatenate(db[b], axis=1).astype(dbm_ref.dtype)
            dcm_ref[0, b] = jnp.concatenate(dc[b], axis=1).astype(dcm_ref.dtype)
            ds_ref[b] = dsin[b]

    def rows(cols):
        return pl.BlockSpec((nb, CHUNK, cols), lambda d, s: (0, chunk(d, s), 0))

    def by_dir(cols):
        return pl.BlockSpec((1, nb, CHUNK, cols), lambda d, s: (d, 0, chunk(d, s), 0))

    grid = (2, nc)
    body, side_in, side_out, side_shapes, side_scratch, side_args = _side_wrap(body, 8, 5, 1, side, grid)
    outs = pl.pallas_call(
        body, name="ssd_scan_bwd" if side is None else "ssd_scan_bwd_comm", grid=grid,
        in_specs=[rows(384), rows(256), rows(256), by_dir(128), pl.BlockSpec((1, 8, 128), lambda d, s: (d, 0, 0)),
                  pl.BlockSpec((128, 384), lambda d, s: (0, 0)),
                  pl.BlockSpec((1, nb, CHUNK, 384), lambda d, s: (d * nc + chunk(d, s), 0, 0, 0)), rows(384)] + side_in,
        out_specs=[by_dir(384), by_dir(256), by_dir(256), by_dir(128),
                   pl.BlockSpec((1, nb, 8, 128), lambda d, s: (d, 0, 0, 0))] + side_out,
        out_shape=[jax.ShapeDtypeStruct((2, nb, T, 384), MXU), jax.ShapeDtypeStruct((2, nb, T, 256), MXU),
                   jax.ShapeDtypeStruct((2, nb, T, 256), MXU), jax.ShapeDtypeStruct((2, nb, T, 128), F32),
                   jax.ShapeDtypeStruct((2, nb, 8, 128), F32)] + side_shapes,
        scratch_shapes=[pltpu.VMEM((nb, CHUNK, 384), F32)] + side_scratch,
    )(xs.reshape(nb, T, 384), bm.reshape(nb, T, 256), cm.reshape(nb, T, 256), dtv.reshape(2, nb, T, 128), arow, eexp,
      hin, dy.reshape(nb, T, 384), *side_args)
    return (outs[0].reshape(2, R, 384), outs[1].reshape(2, R, 256), outs[2].reshape(2, R, 256),
            outs[3].reshape(2, R, 128), outs[4], list(outs[5:]))


def _group_rms(g):
    lane = lax.broadcasted_iota(jnp.int32, g.shape, 1)
    g0 = lane < 192
    gg = g * g
    s0 = jnp.sum(jnp.where(g0, gg, 0.0), axis=-1, keepdims=True)
    s1 = jnp.sum(gg, axis=-1, keepdims=True) - s0
    rstd = jnp.where(g0, lax.rsqrt(s0 * (1.0 / 192) + EPS), lax.rsqrt(s1 * (1.0 / 192) + EPS))
    return rstd, g0


def ssd_out_fwd(y2, xs, pz, dexp, nw):
    R = xs.shape[0]

    def body(y_ref, xs_ref, z_ref, d_ref, nw_ref, o_ref):
        z = z_ref[...].astype(F32)
        yy = y_ref[0] + y_ref[1] + xs_ref[...] * d_ref[...]
        g = yy * (z * _sigmoid(z))
        rstd, _ = _group_rms(g)
        o_ref[...] = g * rstd * nw_ref[...]

    return pl.pallas_call(
        body, name="ssd_out_fwd", grid=(R // TM,),
        in_specs=[pl.BlockSpec((2, TM, 384), lambda i: (0, i, 0)), _rowspec(384), _rowspec(384),
                  _fullspec((1, 384)), _fullspec((1, 384))],
        out_specs=_rowspec(384),
        out_shape=jax.ShapeDtypeStruct((R, 384), F32),
    )(y2, xs, pz, dexp, nw)


def ssd_out_bwd(dout, y2, xs, pz, dexp, nw):
    R = xs.shape[0]

    def body(do_ref, y_ref, xs_ref, z_ref, d_ref, nw_ref, dy_ref, dz_ref, dxs_ref, part_ref):
        z = z_ref[...].astype(F32)
        xs_v = xs_ref[...]
        yy = y_ref[0] + y_ref[1] + xs_v * d_ref[...]
        sig = _sigmoid(z)
        sz = z * sig
        g = yy * sz
        rstd, g0 = _group_rms(g)
        ghat = g * rstd
        do = do_ref[...]
        dgn = do * nw_ref[...]
        t = dgn * ghat
        t0 = jnp.sum(jnp.where(g0, t, 0.0), axis=-1, keepdims=True)
        t1 = jnp.sum(t, axis=-1, keepdims=True) - t0
        dg = rstd * (dgn - ghat * jnp.where(g0, t0, t1) * (1.0 / 192))
        dyy = dg * sz
        dy_ref[...] = dyy
        dz_ref[...] = (dg * yy * (sig * (1.0 + z * (1.0 - sig)))).astype(dz_ref.dtype)
        dxs_ref[...] = dyy * d_ref[...]
        part_ref[0] = jnp.concatenate([_colsum(do * ghat), _colsum(dyy * xs_v), jnp.zeros((6, 384), F32)], axis=0)

    return pl.pallas_call(
        body, name="ssd_out_bwd", grid=(R // TM,),
        in_specs=[_rowspec(384), pl.BlockSpec((2, TM, 384), lambda i: (0, i, 0)), _rowspec(384), _rowspec(384),
                  _fullspec((1, 384)), _fullspec((1, 384))],
        out_specs=[_rowspec(384), _rowspec(384), _rowspec(384), pl.BlockSpec((1, 8, 384), lambda i: (i, 0, 0))],
        out_shape=[jax.ShapeDtypeStruct((R, 384), F32), jax.ShapeDtypeStruct((R, 384), MXU),
                   jax.ShapeDtypeStruct((R, 384), F32), jax.ShapeDtypeStruct((R // TM, 8, 384), F32)],
    )(dout, y2, xs, pz, dexp, nw)


def ssd_prep_bwd_a(pxbc, plast, cw, cb, dtb, dxs_skip, dxs2, dbm2, dcm2, ddt2, blocks_per_sample):
    R = pxbc.shape[0]
    prev, nxt = _halo_specs(XBC, R, 8 * 4 // pxbc.dtype.itemsize)

    def body(cur_ref, prev_ref, nxt_ref, pl_ref, cw_ref, cb_ref, dtb_ref, dsk_ref, dxs_ref, dbm_ref, dcm_ref, ddt_ref,
             dpre_ref, dlast_ref, part_ref):
        i = pl.program_id(0)
        ext = _ext_rows(cur_ref[...], prev_ref[...], nxt_ref[...], i, blocks_per_sample)
        taps = _conv_taps(ext)
        co = _conv_out(taps, cw_ref, cb_ref)
        sig = _sigmoid(co)
        both = lambda ref: ref[0].astype(F32) + ref[1].astype(F32)
        up = jnp.concatenate([dsk_ref[...] + both(dxs_ref), both(dbm_ref), both(dcm_ref)], axis=1)
        dpre = up * (sig * (1.0 + co * (1.0 - sig)))
        dpre_ref[...] = dpre
        raw = pl_ref[...] + dtb_ref[...]
        lane = lax.broadcasted_iota(jnp.int32, raw.shape, 1)
        ddt = (pltpu.roll(ddt_ref[0], DT0, axis=1) + pltpu.roll(ddt_ref[1], DT0 + SSD_HEADS, axis=1))
        ddt = jnp.where(jnp.logical_and(lane >= DT0, lane < DT0 + 2 * SSD_HEADS), ddt * _sigmoid(raw), 0.0)
        dlast_ref[...] = ddt.astype(dlast_ref.dtype)
        rows = [_colsum(dpre * taps[k]) for k in range(4)]
        rows.append(_colsum(dpre))
        rows.append(jnp.concatenate([_colsum(ddt), jnp.zeros((1, XBC - 128), F32)], axis=1))
        rows.append(jnp.zeros((2, XBC), F32))
        part_ref[0] = jnp.concatenate(rows, axis=0)

    dirspec = lambda n: pl.BlockSpec((2, SB, n), lambda i: (0, i, 0))
    return pl.pallas_call(
        body, name="ssd_prep_bwd_a", grid=(R // SB,),
        in_specs=[_rowspec(XBC, SB), prev, nxt, _rowspec(128, SB), _fullspec((8, XBC)), _fullspec((1, XBC)),
                  _fullspec((1, 128)), _rowspec(384, SB), dirspec(384), dirspec(256), dirspec(256), dirspec(128)],
        out_specs=[_rowspec(XBC, SB), _rowspec(128, SB), pl.BlockSpec((1, 8, XBC), lambda i: (i, 0, 0))],
        out_shape=[jax.ShapeDtypeStruct((R, XBC), F32), jax.ShapeDtypeStruct((R, 128), MXU),
                   jax.ShapeDtypeStruct((R // SB, 8, XBC), F32)],
    )(pxbc, pxbc, pxbc, plast, cw, cb, dtb, dxs_skip, dxs2, dbm2, dcm2, ddt2)


def ssd_prep_bwd_b(dpre, cw, blocks_per_sample):
    R = dpre.shape[0]
    prev, nxt = _halo_specs(XBC, R)

    def body(cur_ref, prev_ref, nxt_ref, cw_ref, o_ref):
        i = pl.program_id(0)
        ext = _ext_rows(cur_ref[...], prev_ref[...], nxt_ref[...], i, blocks_per_sample)
        o_ref[...] = (cw_ref[0:1, :] * _shift(ext, 1) + cw_ref[1:2, :] * _shift(ext, 0)
                      + cw_ref[2:3, :] * _shift(ext, -1) + cw_ref[3:4, :] * _shift(ext, -2)).astype(o_ref.dtype)

    return pl.pallas_call(
        body, name="ssd_prep_bwd_b", grid=(R // SB,),
        in_specs=[_rowspec(XBC, SB), prev, nxt, _fullspec((8, XBC))],
        out_specs=_rowspec(XBC, SB),
        out_shape=jax.ShapeDtypeStruct((R, XBC), MXU),
    )(dpre, dpre, dpre, cw)


def _rope(u, cos, sa, sb):
    return u * cos + pltpu.roll(u, 120, axis=1) * sa + pltpu.roll(u, 8, axis=1) * sb


def _rope_t(du, cos, sa, sb):
    return du * cos + pltpu.roll(du * sa, 8, axis=1) + pltpu.roll(du * sb, 120, axis=1)


def mla_prep(pqa, pkva, plast, qnw, kvnw, wq, wk, wv, cos, sa, sb):
    R = pqa.shape[0]

    def body(qa_ref, kva_ref, pl_ref, qnw_ref, kvnw_ref, wq_ref, wk_ref, wv_ref, cos_ref, sa_ref, sb_ref,
             q_ref, k_ref, v_ref, cq_ref, ckv_ref):
        cos_v, sa_v, sb_v = cos_ref[...], sa_ref[...], sb_ref[...]
        xq, _ = _rms_hat(qa_ref[...].astype(F32))
        cq_ref[...] = (xq * qnw_ref[...]).astype(cq_ref.dtype)
        xkv, _ = _rms_hat(kva_ref[...].astype(F32))
        ckv_ref[...] = (xkv * kvnw_ref[...]).astype(ckv_ref.dtype)
        q = _dot(cq_ref[...], wq_ref[...])
        kn = _dot(ckv_ref[...], wk_ref[...])
        v_ref[...] = _dot(ckv_ref[...], wv_ref[...]).astype(v_ref.dtype)
        lane = lax.broadcasted_iota(jnp.int32, (TM, HP), 1)
        rope_lanes = jnp.logical_and(lane >= QK_NOPE, lane < QK_DIM)
        kr = _rope(jnp.where(rope_lanes, pltpu.roll(pl_ref[...], QK_NOPE, axis=1), 0.0), cos_v, sa_v, sb_v)
        for h in range(MLA_HEADS):
            cols = slice(h * HP, (h + 1) * HP)
            q_ref[:, cols] = (_rope(q[:, cols], cos_v, sa_v, sb_v) * Q_SCALE).astype(q_ref.dtype)
            k_ref[:, cols] = (kn[:, cols] + kr).astype(k_ref.dtype)

    return pl.pallas_call(
        body, name="mla_prep", grid=(R // TM,),
        in_specs=[_rowspec(256), _rowspec(256), _rowspec(128), _fullspec((1, 256)), _fullspec((1, 256)),
                  _fullspec((256, QW)), _fullspec((256, QW)), _fullspec((256, QW)),
                  _rowspec(HP), _rowspec(HP), _rowspec(HP)],
        out_specs=[_rowspec(QW), _rowspec(QW), _rowspec(QW), _rowspec(256), _rowspec(256)],
        out_shape=[jax.ShapeDtypeStruct((R, QW), MXU)] * 3 + [jax.ShapeDtypeStruct((R, 256), MXU)] * 2,
    )(pqa, pkva, plast, qnw, kvnw, wq, wk, wv, cos, sa, sb)


def mla_prep_bwd(dq, dk, dv, pqa, pkva, cq, ckv, qnw, kvnw, wq, wk, wv, cos, sa, sb):
    R = pqa.shape[0]

    def body(dq_ref, dk_ref, dv_ref, qa_ref, kva_ref, cq_ref, ckv_ref, qnw_ref, kvnw_ref, wq_ref, wk_ref, wv_ref,
             cos_ref, sa_ref, sb_ref, dqa_ref, dkva_ref, dkr_ref, dwq_ref, dwk_ref, dwv_ref, part_ref,
             dql_ref, dkm_ref, dvb_ref):
        @pl.when(pl.program_id(0) == 0)
        def _():
            dwq_ref[...] = jnp.zeros_like(dwq_ref)
            dwk_ref[...] = jnp.zeros_like(dwk_ref)
            dwv_ref[...] = jnp.zeros_like(dwv_ref)

        cos_v, sa_v, sb_v = cos_ref[...], sa_ref[...], sb_ref[...]
        lane = lax.broadcasted_iota(jnp.int32, (TM, HP), 1)
        rope_lanes = jnp.logical_and(lane >= QK_NOPE, lane < QK_DIM)
        dkr = jnp.zeros((TM, HP), F32)
        for h in range(MLA_HEADS):
            cols = slice(h * HP, (h + 1) * HP)
            dql_ref[:, cols] = (_rope_t(dq_ref[:, cols], cos_v, sa_v, sb_v) * ATT_SCALE).astype(dql_ref.dtype)
            dkh = dk_ref[:, cols] * LN2
            dkm_ref[:, cols] = jnp.where(lane < QK_NOPE, dkh, 0.0).astype(dkm_ref.dtype)
            dkr = dkr + jnp.where(rope_lanes, dkh, 0.0)
        dvb_ref[...] = dv_ref[...].astype(dvb_ref.dtype)
        dkr = jnp.where(rope_lanes, _rope_t(dkr, cos_v, sa_v, sb_v), 0.0)
        dkr_ref[...] = pltpu.roll(dkr, HP - QK_NOPE, axis=1).astype(dkr_ref.dtype)
        dwq_ref[...] += _dotg(cq_ref[...], dql_ref[...], TN)
        dwk_ref[...] += _dotg(ckv_ref[...], dkm_ref[...], TN)
        dwv_ref[...] += _dotg(ckv_ref[...], dvb_ref[...], TN)
        xq, rq = _rms_hat(qa_ref[...].astype(F32))
        dqa, dqnw = _rms_bwd(_dotg(dql_ref[...], wq_ref[...], NT), xq, rq, qnw_ref[...])
        dqa_ref[...] = dqa.astype(dqa_ref.dtype)
        xkv, rkv = _rms_hat(kva_ref[...].astype(F32))
        dckv = _dotg(dkm_ref[...], wk_ref[...], NT) + _dotg(dvb_ref[...], wv_ref[...], NT)
        dkva, dkvnw = _rms_bwd(dckv, xkv, rkv, kvnw_ref[...])
        dkva_ref[...] = dkva.astype(dkva_ref.dtype)
        part_ref[0] = jnp.concatenate([dqnw, dkvnw, jnp.zeros((6, 256), F32)], axis=0)

    return pl.pallas_call(
        body, name="mla_prep_bwd", grid=(R // TM,),
        in_specs=[_rowspec(QW), _rowspec(QW), _rowspec(QW), _rowspec(256), _rowspec(256), _rowspec(256), _rowspec(256),
                  _fullspec((1, 256)), _fullspec((1, 256)), _fullspec((256, QW)), _fullspec((256, QW)),
                  _fullspec((256, QW)), _rowspec(HP), _rowspec(HP), _rowspec(HP)],
        out_specs=[_rowspec(256), _rowspec(256), _rowspec(128), _fullspec((256, QW)), _fullspec((256, QW)),
                   _fullspec((256, QW)), pl.BlockSpec((1, 8, 256), lambda i: (i, 0, 0))],
        out_shape=[jax.ShapeDtypeStruct((R, 256), MXU), jax.ShapeDtypeStruct((R, 256), MXU),
                   jax.ShapeDtypeStruct((R, 128), MXU)] + [jax.ShapeDtypeStruct((256, QW), F32)] * 3
                  + [jax.ShapeDtypeStruct((R // TM, 8, 256), F32)],
        scratch_shapes=[pltpu.VMEM((TM, QW), MXU)] * 3,
    )(dq, dk, dv, pqa, pkva, cq, ckv, qnw, kvnw, wq, wk, wv, cos, sa, sb)


ATT_SCALE = QK_DIM ** -0.5
TQ = 256


LOG2E = 1.4426950408889634
LN2 = 0.6931471805599453
Q_SCALE = ATT_SCALE * LOG2E


def _key_chunks(T, n=2):
    unit = 256 if T % 256 == 0 else 128
    units = T // unit
    sizes = [(units // n + (1 if i < units % n else 0)) * unit for i in range(n)]
    return [(sum(sizes[:i]), sz) for i, sz in enumerate(sizes) if sz]


def attn_fwd(q, k, v, nb, T):
    R = q.shape[0]
    nq = T // TQ
    chunks = _key_chunks(T, 4)
    HEADS = range(2)

    def body(q_ref, k_ref, v_ref, o_ref, lse_ref):
        def lanes(h):
            return slice(h * HP, (h + 1) * HP)

        def logits(h, lo, n):
            return _dotg(q_ref[:, lanes(h)], k_ref[lo:lo + n, lanes(h)], NT)

        def weigh(h, s, lo, n):
            m = jnp.max(s, axis=-1, keepdims=True)
            p = jnp.exp2(s - m)
            return m, jnp.sum(p, axis=-1, keepdims=True), _dot(p.astype(MXU), v_ref[lo:lo + n, lanes(h)])

        def parts_of(ranges):
            out = [[] for _ in HEADS]
            s = [logits(h, *ranges[0]) for h in HEADS]
            for j, (lo, n) in enumerate(ranges):
                nxt = [logits(h, *ranges[j + 1]) for h in HEADS] if j + 1 < len(ranges) else None
                for h in HEADS:
                    out[h].append(weigh(h, s[h], lo, n))
                s = nxt
            return out

        def finish(all_parts):
            for h, parts in enumerate(all_parts):
                m = parts[0][0]
                for pm, _, _ in parts[1:]:
                    m = jnp.maximum(m, pm)
                l, o = 0.0, 0.0
                for pm, pl_, po in parts:
                    a = jnp.exp2(pm - m)
                    l = l + a * pl_
                    o = o + a * po
                o_ref[:, lanes(h)] = o / l
                lse_ref[:, lanes(h)] = jnp.broadcast_to(m + jnp.log(l) * LOG2E, (TQ, HP))

        i = pl.program_id(2)
        pl.when(i == 0)(lambda: finish(parts_of([(0, CTX)])))
        pl.when(i > 0)(lambda: finish(parts_of(chunks)))

    qspec = pl.BlockSpec((TQ, 2 * HP), lambda b, h, i: (b * nq + i, h))
    kspec = pl.BlockSpec((T, 2 * HP), lambda b, h, i: (b, h))
    return pl.pallas_call(
        body, name="attn_fwd", grid=(nb, MLA_HEADS // 2, nq),
        in_specs=[qspec, kspec, kspec], out_specs=[qspec, qspec],
        out_shape=[jax.ShapeDtypeStruct((R, QW), F32)] * 2,
        compiler_params=_cp(48),
    )(q, k, v)


def attn_bwd(q, k, v, o, lse, do, nb, T):
    R = q.shape[0]
    nq = T // TQ
    chunks = _key_chunks(T)

    def body(q_ref, k_ref, v_ref, o_ref, lse_ref, do_ref, dq_ref, dk_ref, dv_ref):
        i = pl.program_id(2)

        @pl.when(i == 0)
        def _():
            dk_ref[...] = jnp.zeros_like(dk_ref)
            dv_ref[...] = jnp.zeros_like(dv_ref)

        def run(chunks):
            for h in range(2):
                lanes = slice(h * HP, (h + 1) * HP)
                qv = q_ref[:, lanes]
                dov = do_ref[:, lanes]
                dob = dov.astype(MXU)
                delta = jnp.sum(dov * o_ref[:, lanes], axis=-1, keepdims=True)
                lse_v = lse_ref[:, h * HP:h * HP + 1]
                dq = 0.0
                for lo, n in chunks:
                    kv = k_ref[lo:lo + n, lanes]
                    p = jnp.exp2(_dotg(qv, kv, NT) - lse_v)
                    dp = _dotg(dob, v_ref[lo:lo + n, lanes], NT)
                    dsb = (p * (dp - delta)).astype(MXU)
                    dq = dq + _dot(dsb, kv)
                    dk_ref[lo:lo + n, lanes] += _dotg(dsb, qv, TN)
                    dv_ref[lo:lo + n, lanes] += _dotg(p.astype(MXU), dob, TN)
                dq_ref[:, lanes] = dq

        pl.when(i == 0)(lambda: run([(0, CTX)]))
        pl.when(i > 0)(lambda: run(chunks))

    qspec = pl.BlockSpec((TQ, 2 * HP), lambda b, h, i: (b * nq + i, h))
    kspec = pl.BlockSpec((T, 2 * HP), lambda b, h, i: (b, h))
    return pl.pallas_call(
        body, name="attn_bwd", grid=(nb, MLA_HEADS // 2, nq),
        in_specs=[qspec, kspec, kspec, qspec, qspec, qspec],
        out_specs=[qspec, kspec, kspec],
        out_shape=[jax.ShapeDtypeStruct((R, QW), F32)] * 3,
        compiler_params=_cp(56),
    )(q, k, v, o, lse, do)


def _pool_geometry(i, blocks_per_sample, seq):
    j = i % blocks_per_sample
    n = jnp.where(j == 0, CTX, seq)
    t0 = jnp.where(j == 0, 0, (j - 1) * SB) - HALO
    lane = lax.broadcasted_iota(jnp.int32, (SB + 2 * HALO, POOL_DIM), 1)
    t = lax.broadcasted_iota(jnp.int32, (SB + 2 * HALO, POOL_DIM), 0) + t0
    wh = jnp.where(lane < 64, 1, jnp.where(lane < 128, 2, jnp.where(lane < 192, 4, 8)))
    cnt = jnp.minimum(t + wh, n) - jnp.maximum(t - wh, 0)
    return lane, 1.0 / jnp.maximum(cnt, 1).astype(F32)


def _by_window(lane, c2, c4, c8, c16):
    return jnp.where(lane < 64, c2, jnp.where(lane < 128, c4, jnp.where(lane < 192, c8, c16)))


def _window_sums(ext, lane, first):
    n = ext.shape[0]
    r = lambda a, s: pltpu.roll(a, s % n, axis=0)
    c2 = ext + r(ext, first)
    c4 = r(c2, 1) + r(c2, -1)
    c8 = r(c4, 2) + r(c4, -2)
    c16 = r(c8, 4) + r(c8, -4)
    return _by_window(lane, c2, c4, c8, c16)


def _pool_delta(ext, lane, inv):
    return (_window_sums(ext, lane, 1) * inv - ext)[HALO:HALO + SB, :]


def pool_fwd(ppool, wbd, scale, blocks_per_sample, seq):
    R = ppool.shape[0]
    prev, nxt = _halo_specs(POOL_DIM, R)

    def body(cur_ref, prev_ref, nxt_ref, w_ref, s_ref, o_ref):
        i = pl.program_id(0)
        ext = _ext_rows(cur_ref[...], prev_ref[...], nxt_ref[...], i, blocks_per_sample)
        lane, inv = _pool_geometry(i, blocks_per_sample, seq)
        dlt = _pool_delta(ext, lane, inv)
        o_ref[...] = _dot(dlt.astype(MXU), w_ref[...]) * s_ref[...]

    return pl.pallas_call(
        body, name="pool_fwd", grid=(R // SB,),
        in_specs=[_rowspec(POOL_DIM, SB), prev, nxt, _fullspec((POOL_DIM, POOL_DIM)), _fullspec((1, POOL_DIM))],
        out_specs=_rowspec(POOL_DIM, SB),
        out_shape=jax.ShapeDtypeStruct((R, POOL_DIM), F32),
    )(ppool, ppool, ppool, wbd, scale)


def pool_bwd(ppool, dpool, wbd, scale, blocks_per_sample, seq):
    R = ppool.shape[0]
    prev, nxt = _halo_specs(POOL_DIM, R)

    def body(cur_ref, prev_ref, nxt_ref, dcur_ref, dprev_ref, dnxt_ref, w_ref, s_ref, du_ref, dw_ref, part_ref):
        i = pl.program_id(0)

        @pl.when(i == 0)
        def _():
            dw_ref[...] = jnp.zeros_like(dw_ref)

        ext = _ext_rows(cur_ref[...], prev_ref[...], nxt_ref[...], i, blocks_per_sample)
        lane, inv = _pool_geometry(i, blocks_per_sample, seq)
        dlt = _pool_delta(ext, lane, inv).astype(MXU)
        dy = dcur_ref[...]
        part_ref[0] = jnp.concatenate([_colsum(dy * _dot(dlt, w_ref[...])), jnp.zeros((7, POOL_DIM), F32)], axis=0)
        dyp = (dy * s_ref[...]).astype(MXU)
        dw_ref[...] += _dotg(dlt, dyp, TN)
        dext = _ext_rows(dy, dprev_ref[...], dnxt_ref[...], i, blocks_per_sample)
        dd = _dotg((dext * s_ref[...]).astype(MXU), w_ref[...], NT)
        du_ref[...] = (_window_sums(dd * inv, lane, -1) - dd)[HALO:HALO + SB, :].astype(du_ref.dtype)

    return pl.pallas_call(
        body, name="pool_bwd", grid=(R // SB,),
        in_specs=[_rowspec(POOL_DIM, SB), prev, nxt, _rowspec(POOL_DIM, SB), prev, nxt,
                  _fullspec((POOL_DIM, POOL_DIM)), _fullspec((1, POOL_DIM))],
        out_specs=[_rowspec(POOL_DIM, SB), _fullspec((POOL_DIM, POOL_DIM)),
                   pl.BlockSpec((1, 8, POOL_DIM), lambda i: (i, 0, 0))],
        out_shape=[jax.ShapeDtypeStruct((R, POOL_DIM), MXU), jax.ShapeDtypeStruct((POOL_DIM, POOL_DIM), F32),
                   jax.ShapeDtypeStruct((R // SB, 8, POOL_DIM), F32)],
    )(ppool, ppool, ppool, dpool, dpool, dpool, wbd, scale)


def adamw(w, g, m, v, name="adamw"):
    rows, cols = w.shape
    tr = rows
    for cand in (512, 256, 128, 64, 32, 16, 8):
        if rows % cand == 0:
            tr = cand
            break
    bc1 = 1.0 - ADAM_B1 ** ADAM_STEP
    bc2 = 1.0 - ADAM_B2 ** ADAM_STEP

    def body(w_ref, g_ref, m_ref, v_ref, d_ref, nm_ref, nv_ref):
        g_v = g_ref[...]
        nm = ADAM_B1 * m_ref[...] + (1.0 - ADAM_B1) * g_v
        nv = ADAM_B2 * v_ref[...] + (1.0 - ADAM_B2) * (g_v * g_v)
        nm_ref[...] = nm
        nv_ref[...] = nv
        d_ref[...] = -ADAM_LR * ((nm / bc1) / (jnp.sqrt(nv / bc2) + ADAM_EPS) + ADAM_WD * w_ref[...])

    spec = pl.BlockSpec((tr, cols), lambda i: (i, 0))
    return pl.pallas_call(
        body, name=name, grid=(rows // tr,),
        in_specs=[spec] * 4, out_specs=[spec] * 3,
        out_shape=[jax.ShapeDtypeStruct((rows, cols), F32)] * 3,
    )(w, g, m, v)


MODR = 32


def _silu(v):
    return v * _sigmoid(v)


def mod_fwd(cond, w, b):
    n = w.shape[1]

    def body(c_ref, w_ref, b_ref, o_ref):
        o_ref[...] = _dot(_silu(c_ref[...]).astype(MXU), w_ref[...].astype(MXU)) + b_ref[...]

    return pl.pallas_call(
        body, name="mod_fwd", out_shape=jax.ShapeDtypeStruct((MODR, n), F32),
        in_specs=[_fullspec((MODR, D)), _fullspec((D, n)), _fullspec((1, n))], out_specs=_fullspec((MODR, n)),
        grid=(1,), compiler_params=_cp(40),
    )(cond, w, b)


def mod_wgrad(cond, dm):
    n = dm.shape[1]

    def body(c_ref, d_ref, o_ref):
        o_ref[...] = _dotg(_silu(c_ref[...]).astype(MXU), d_ref[...].astype(MXU), TN)

    return pl.pallas_call(
        body, name="mod_wgrad", out_shape=jax.ShapeDtypeStruct((D, n), F32),
        in_specs=[_fullspec((MODR, D)), _fullspec((MODR, n))], out_specs=_fullspec((D, n)),
        grid=(1,), compiler_params=_cp(40),
    )(cond, dm)


def mod_dgrad(dm, w):
    n = w.shape[1]

    def body(d_ref, w_ref, o_ref):
        o_ref[...] = _dotg(d_ref[...].astype(MXU), w_ref[...].astype(MXU), NT)

    return pl.pallas_call(
        body, name="mod_dgrad", out_shape=jax.ShapeDtypeStruct((8, D), F32),
        in_specs=[_fullspec((8, n)), _fullspec((D, n))], out_specs=_fullspec((8, D)),
        grid=(1,), compiler_params=_cp(40),
    )(dm, w)


def sum_leading(a, name="sum_leading"):
    n, r, c = a.shape

    def body(a_ref, o_ref):
        acc = a_ref[0]
        for k in range(1, n):
            acc = acc + a_ref[k]
        o_ref[...] = acc

    return pl.pallas_call(
        body, name=name, out_shape=jax.ShapeDtypeStruct((r, c), F32),
        in_specs=[_fullspec((n, r, c))], out_specs=_fullspec((r, c)), grid=(1,),
    )(a)


MESH = pl.DeviceIdType.MESH
NDEV = 8
ANY = pl.BlockSpec(memory_space=pl.ANY)


def _place():
    return lax.axis_index("x"), lax.axis_index("y"), lax.axis_index("c")


def _other_chips(x, y):
    return [(1 - x, y), (x, 1 - y), (1 - x, 1 - y)]


def allgather_small(v, name):
    r, cols = v.shape

    def body(v_ref, o_ref, send_sems, recv_sems):
        x, y, c = _place()
        me = 4 * x + 2 * y + c
        o_ref[me] = v_ref[...]
        copies = []
        for rel in range(1, NDEV):
            peer = (1 - x if rel & 4 else x, 1 - y if rel & 2 else y, 1 - c if rel & 1 else c)
            cp = pltpu.make_async_remote_copy(src_ref=v_ref, dst_ref=o_ref.at[me], send_sem=send_sems.at[rel - 1],
                                              recv_sem=recv_sems.at[rel - 1], device_id=peer, device_id_type=MESH)
            cp.start()
            copies.append(cp)
        for cp in copies:
            cp.wait_recv()
        for cp in copies:
            cp.wait_send()

    return pl.pallas_call(
        body, name=name, out_shape=jax.ShapeDtypeStruct((NDEV, r, cols), F32),
        in_specs=[pl.BlockSpec(memory_space=pltpu.VMEM)], out_specs=pl.BlockSpec(memory_space=pltpu.VMEM),
        scratch_shapes=[pltpu.SemaphoreType.DMA((NDEV - 1,)), pltpu.SemaphoreType.DMA((NDEV - 1,))],
        compiler_params=_cp(40),
    )(v)


def _sems(n):
    return [pltpu.SemaphoreType.DMA((n,)), pltpu.SemaphoreType.DMA((n,))]


def allgather_chips(v, name):
    r, cols = v.shape

    def body(v_ref, o_ref, send_sems, recv_sems):
        x, y, c = _place()
        k = 2 * x + y
        o_ref[k] = v_ref[...]
        copies = []
        for j, (px, py) in enumerate(_other_chips(x, y)):
            cp = pltpu.make_async_remote_copy(src_ref=v_ref, dst_ref=o_ref.at[k], send_sem=send_sems.at[j],
                                              recv_sem=recv_sems.at[j], device_id=(px, py, c), device_id_type=MESH)
            cp.start()
            copies.append(cp)
        for cp in copies:
            cp.wait_recv()
        for cp in copies:
            cp.wait_send()

    return pl.pallas_call(
        body, name=name, out_shape=jax.ShapeDtypeStruct((4, r, cols), F32),
        in_specs=[pl.BlockSpec(memory_space=pltpu.VMEM)], out_specs=pl.BlockSpec(memory_space=pltpu.VMEM),
        scratch_shapes=_sems(3), compiler_params=_cp(40),
    )(v)


def gather_job(arrs):
    n = len(arrs)

    def copy(srcs, outs, sems, i, slot, kk, cc, to, from_src=False):
        hr = arrs[i].shape[0] // 2
        dst = outs[i].at[kk, pl.ds(cc * hr, hr), :]
        return pltpu.make_async_remote_copy(src_ref=srcs[i].at[pl.ds(cc * hr, hr), :] if from_src else dst, dst_ref=dst,
                                            send_sem=sems[0].at[slot * n + i], recv_sem=sems[1].at[slot * n + i],
                                            device_id=to, device_id_type=MESH)

    def start(srcs, outs, sems):
        x, y, c = _place()
        for j, (px, py) in enumerate(_other_chips(x, y)):
            for i in range(n):
                copy(srcs, outs, sems, i, j, 2 * x + y, c, (px, py, c), True).start()

    def finish(srcs, outs, sems):
        x, y, c = _place()
        sib = (x, y, 1 - c)
        chips = _other_chips(x, y)
        passed = []
        for j, (px, py) in enumerate(chips):
            for i in range(n):
                copy(srcs, outs, sems, i, j, 2 * px + py, c, (px, py, c)).wait_recv()
                cp = copy(srcs, outs, sems, i, 3 + j, 2 * px + py, c, sib)
                cp.start()
                passed.append(cp)
        for j, (px, py) in enumerate(chips):
            for i in range(n):
                copy(srcs, outs, sems, i, 3 + j, 2 * px + py, 1 - c, sib).wait_recv()
        for j, (px, py) in enumerate(chips):
            for i in range(n):
                copy(srcs, outs, sems, i, j, 2 * x + y, c, (px, py, c), True).wait_send()
        for cp in passed:
            cp.wait_send()

    return _NS(ins=list(arrs), out_shapes=[jax.ShapeDtypeStruct((4,) + a.shape, a.dtype) for a in arrs], nsem=6 * n,
               start=start, finish=finish)


def chip_swap_job(ss):
    n = len(ss)

    def copies(srcs, outs, sems):
        x, y, c = _place()
        return [pltpu.make_async_remote_copy(src_ref=srcs[i].at[2 * px + py], dst_ref=outs[i].at[j],
                                             send_sem=sems[0].at[j * n + i], recv_sem=sems[1].at[j * n + i],
                                             device_id=(px, py, c), device_id_type=MESH)
                for j, (px, py) in enumerate(_other_chips(x, y)) for i in range(n)]

    def start(srcs, outs, sems):
        for cp in copies(srcs, outs, sems):
            cp.start()

    def finish(srcs, outs, sems):
        for cp in copies(srcs, outs, sems):
            cp.wait()

    return _NS(ins=list(ss), out_shapes=[jax.ShapeDtypeStruct((3,) + s.shape[1:], s.dtype) for s in ss], nsem=3 * n,
               start=start, finish=finish)


def run_job(job, name):
    n, m = len(job.ins), len(job.out_shapes)

    def body(*refs):
        srcs, outs, sems = refs[:n], refs[n:n + m], refs[n + m:]
        job.start(srcs, outs, sems)
        job.finish(srcs, outs, sems)

    return pl.pallas_call(body, name=name, out_shape=job.out_shapes, in_specs=[ANY] * n, out_specs=[ANY] * m,
                          scratch_shapes=_sems(job.nsem))(*job.ins)


def swap_core_halves(gs):
    n = len(gs)

    def body(*refs):
        srcs, outs = refs[:n], refs[n:2 * n]
        send_sems, recv_sems = refs[2 * n:]
        x, y, c = _place()
        copies = []
        for i in range(n):
            hr = gs[i].shape[1] // 2
            cp = pltpu.make_async_remote_copy(src_ref=srcs[i].at[:, pl.ds((1 - c) * hr, hr), :], dst_ref=outs[i],
                                              send_sem=send_sems.at[i], recv_sem=recv_sems.at[i],
                                              device_id=(x, y, 1 - c), device_id_type=MESH)
            cp.start()
            copies.append(cp)
        for cp in copies:
            cp.wait()

    return pl.pallas_call(
        body, name="swap_core_halves",
        out_shape=[jax.ShapeDtypeStruct((4, g.shape[1] // 2, g.shape[2]), g.dtype) for g in gs],
        in_specs=[ANY] * n, out_specs=[ANY] * n, scratch_shapes=_sems(n),
    )(*gs)


def add_half(g, r1, cidx, name):
    _, rows, cols = g.shape
    hr = rows // 2

    def body(c_ref, g_ref, r_ref, o_ref, ob_ref):
        s = g_ref[...] + r_ref[...]
        o_ref[...] = s
        ob_ref[...] = s.astype(BF16)

    blk = lambda f: pl.BlockSpec((1, hr, cols), f)
    return pl.pallas_call(
        body, name=name,
        out_shape=[jax.ShapeDtypeStruct((4, hr, cols), F32), jax.ShapeDtypeStruct((4, hr, cols), BF16)],
        grid_spec=pltpu.PrefetchScalarGridSpec(
            num_scalar_prefetch=1, grid=(4,),
            in_specs=[blk(lambda k, c_ref: (k, c_ref[0], 0)), blk(lambda k, c_ref: (k, 0, 0))],
            out_specs=[blk(lambda k, c_ref: (k, 0, 0)), blk(lambda k, c_ref: (k, 0, 0))]),
    )(cidx, g, r1)


def sum_parts(s1, r2, kidx, name):
    _, hr, cols = s1.shape

    def body(k_ref, s_ref, r_ref, o_ref):
        o_ref[...] = ((s_ref[0] + r_ref[0].astype(F32)) + r_ref[1].astype(F32)) + r_ref[2].astype(F32)

    return pl.pallas_call(
        body, name=name, out_shape=jax.ShapeDtypeStruct((hr, cols), F32),
        grid_spec=pltpu.PrefetchScalarGridSpec(
            num_scalar_prefetch=1, grid=(1,),
            in_specs=[pl.BlockSpec((1, hr, cols), lambda i, k_ref: (k_ref[0], 0, 0)),
                      pl.BlockSpec((3, hr, cols), lambda i, k_ref: (0, 0, 0))],
            out_specs=pl.BlockSpec((hr, cols), lambda i, k_ref: (0, 0))),
    )(kidx, s1, r2)


def swap_reduced_halves(hs):
    n = len(hs)

    def body(*refs):
        srcs, outs = refs[:n], refs[n:2 * n]
        send_sems, recv_sems = refs[2 * n:]
        x, y, c = _place()
        copies = []
        for i in range(n):
            cp = pltpu.make_async_remote_copy(src_ref=srcs[i], dst_ref=outs[i], send_sem=send_sems.at[i],
                                              recv_sem=recv_sems.at[i], device_id=(x, y, 1 - c), device_id_type=MESH)
            cp.start()
            copies.append(cp)
        for cp in copies:
            cp.wait()

    return pl.pallas_call(
        body, name="swap_reduced_halves", out_shape=[jax.ShapeDtypeStruct(h.shape, h.dtype) for h in hs],
        in_specs=[ANY] * n, out_specs=[ANY] * n, scratch_shapes=_sems(n),
    )(*hs)


def adamw_halves(w, m, v, own, oth, cidx, name):
    depth, rows, cols = w.shape
    hr = rows // 2
    tr = min(hr, 256)
    nblk = hr // tr
    bc1 = 1.0 - ADAM_B1 ** ADAM_STEP
    bc2 = 1.0 - ADAM_B2 ** ADAM_STEP

    def body(c_ref, w_ref, m_ref, v_ref, own0, own1, oth0, oth1, g_ref, d_ref, nm_ref, nv_ref):
        l = pl.program_id(0)
        hi = pl.program_id(1)
        mine = jnp.where(l == 0, own0[...], own1[...])
        other = jnp.where(l == 0, oth0[...], oth1[...])
        g_v = jnp.where(hi == c_ref[0], mine, other)
        nm = ADAM_B1 * m_ref[0] + (1.0 - ADAM_B1) * g_v
        nv = ADAM_B2 * v_ref[0] + (1.0 - ADAM_B2) * (g_v * g_v)
        g_ref[0] = g_v
        nm_ref[0] = nm
        nv_ref[0] = nv
        d_ref[0] = -ADAM_LR * ((nm / bc1) / (jnp.sqrt(nv / bc2) + ADAM_EPS) + ADAM_WD * w_ref[0])

    wspec = pl.BlockSpec((1, tr, cols), lambda l, hi, b, c_ref: (l, hi * nblk + b, 0))
    gspec = pl.BlockSpec((tr, cols), lambda l, hi, b, c_ref: (b, 0))
    assert depth == 2
    return pl.pallas_call(
        body, name=name, out_shape=[jax.ShapeDtypeStruct(w.shape, F32)] * 4,
        grid_spec=pltpu.PrefetchScalarGridSpec(
            num_scalar_prefetch=1, grid=(depth, 2, nblk),
            in_specs=[wspec] * 3 + [gspec] * 4, out_specs=[wspec] * 4),
    )(cidx, w, m, v, own[0], own[1], oth[0], oth[1])


class _NS:
    def __init__(self, **kw):
        self.__dict__.update(kw)


def _prep_in(win, conv_w, conv_b, dt_bias, a_log, ssd_d, ssd_nw, qnw, kvnw, pool_w, pool_scale, n1, n2):
    winp = jnp.concatenate([win[:, 0:384], win[:, 384:1280], win[:, 1292:1548], win[:, 1548:1804], win[:, 1836:2092],
                            win[:, 1804:1836], win[:, 1280:1292], jnp.zeros((D, NP - IN_COLS), win.dtype)], axis=1)
    wbd = (jnp.eye(4, dtype=F32)[:, None, :, None] * pool_w[:, :, None, :]).reshape(POOL_DIM, POOL_DIM).astype(MXU)
    a = -jnp.exp(a_log)
    return _NS(
        winp=winp, wbd=wbd,
        cw8=jnp.pad(conv_w, ((0, 4), (0, 0))), cb=conv_b[None],
        dtb=jnp.pad(dt_bias.reshape(1, 12), ((0, 0), (DT0, 128 - DT0 - 12))),
        arow=jnp.pad(a[:, None, :], ((0, 0), (0, 7), (0, 128 - SSD_HEADS))), a=a,
        dexp=jnp.repeat(ssd_d, SSD_P)[None], ssd_nw=ssd_nw[None], qnw=qnw[None], kvnw=kvnw[None],
        pscale=pool_scale[None], n1=n1[None], n2=n2[None])


def _prep_rest(wqb, wkvb, wout, w1, w2):
    wq = jnp.pad(wqb.reshape(256, MLA_HEADS, QK_DIM), ((0, 0), (0, 0), (0, HP - QK_DIM))).reshape(256, QW)
    kv3 = wkvb.reshape(256, MLA_HEADS, 128)
    wk = jnp.pad(kv3[:, :, :64], ((0, 0), (0, 0), (0, 64))).reshape(256, QW)
    wv = jnp.pad(kv3[:, :, 64:], ((0, 0), (0, 0), (0, 64))).reshape(256, QW)
    wo = jnp.concatenate([jnp.pad(wout[384:768].reshape(MLA_HEADS, 64, D), ((0, 0), (0, 64), (0, 0))).reshape(QW, D),
                          wout[0:384], wout[768:1024]], axis=0)
    return _NS(wq=wq, wk=wk, wv=wv, wo=wo, w1=w1, w2=w2)


def _prep_layer(win, wqb, wkvb, wout, w1, w2, *small):
    lw = _prep_in(win, *small)
    lw.__dict__.update(_prep_rest(wqb, wkvb, wout, w1, w2).__dict__)
    return lw


def _by_chip_cols(a):
    return jnp.stack([a[:, k * (a.shape[1] // 4):(k + 1) * (a.shape[1] // 4)] for k in range(4)])


def _by_chip_rows(a):
    return a.reshape(4, a.shape[0] // 4, a.shape[1])


def _unprep_in(dwinp):
    return jnp.concatenate([dwinp[:, 0:384], dwinp[:, 384:1280], dwinp[:, 2080:2092], dwinp[:, 1280:1536],
                            dwinp[:, 1536:1792], dwinp[:, 2048:2080], dwinp[:, 1792:2048]], axis=1)


def _unprep_rest(dwq, dwk, dwv, dwo):
    dwqb = dwq.reshape(256, MLA_HEADS, HP)[:, :, :QK_DIM].reshape(256, MLA_HEADS * QK_DIM)
    dwkvb = jnp.concatenate([dwk.reshape(256, MLA_HEADS, HP)[:, :, :64], dwv.reshape(256, MLA_HEADS, HP)[:, :, :64]],
                            axis=2).reshape(256, MLA_HEADS * 128)
    dwout = jnp.concatenate([dwo[QW:QW + 384], dwo[0:QW].reshape(MLA_HEADS, HP, D)[:, :64].reshape(384, D),
                             dwo[QW + 384:CAT]], axis=0)
    return dwqb, dwkvb, dwout


def _rope_tables(nb, N):
    t = jnp.arange(N, dtype=F32)
    row = jnp.floor(t / GRID_W)
    col = t - row * GRID_W
    inv = jnp.asarray(10000.0 ** (-np.arange(8, dtype=np.float32) / 8), F32)
    ang = jnp.stack([row[:, None] * inv, col[:, None] * inv], axis=1)
    cs, sn = jnp.cos(ang), jnp.sin(ang)
    zero = jnp.zeros_like(sn)
    lanes = lambda first, second: jnp.stack([first, second], axis=2).reshape(N, 32)
    pad = lambda a, fill: jnp.concatenate([jnp.full((N, 64), fill, F32), a, jnp.full((N, 32), fill, F32)], axis=1)
    tabs = []
    for tab, fill in ((pad(lanes(cs, cs), 1.0), 1.0), (pad(lanes(-sn, zero), 0.0), 0.0), (pad(lanes(zero, sn), 0.0), 0.0)):
        one = jnp.concatenate([jnp.full((CTX, 128), fill, F32), tab], axis=0)
        tabs.append(jnp.tile(one, (nb, 1)))
    return tabs


def _eexp():
    e = np.zeros((128, SSD_INNER), np.float32)
    for h in range(SSD_HEADS):
        e[h, h * SSD_P:(h + 1) * SSD_P] = 1.0
    return jnp.asarray(e)


class _NoHooks:
    def __init__(self, lws):
        self.lws = lws

    def weights_in(self, l):
        return _NS(**self.lws[l].__dict__)

    def weights_rest(self, l, scan_out):
        return self.lws[l]

    def job(self, where, l, early=None):
        return None

    def done(self, where, l, out):
        pass

    def layer_grads(self, l, g):
        pass


def _layer_fwd(X, bm, l, cst, hooks):
    nb, T, bps, N = cst.nb, cst.T, cst.bps, cst.N
    lw = hooks.weights_in(l)
    h1, pz, pxbc, pqa, pkva, ppool, plast = in_proj(X, bm, lw.n1, lw.winp)
    xs, bmat, cmat, dtv = ssd_prep(pxbc, plast, lw.cw8, lw.cb, lw.dtb, bps)
    y2, hin, out = ssd_scan_fwd(xs, bmat, cmat, dtv, lw.arow, cst.eexp, nb, T, hooks.job("fwd_scan", l))
    lw.__dict__.update(hooks.weights_rest(l, out).__dict__)
    ssd = ssd_out_fwd(y2, xs, pz, lw.dexp, lw.ssd_nw)
    q, k, v, cq, ckv = mla_prep(pqa, pkva, plast, lw.qnw, lw.kvnw, lw.wq, lw.wk, lw.wv, *cst.rope)
    attn, lse = attn_fwd(q, k, v, nb, T)
    pool = pool_fwd(ppool, lw.wbd, lw.pscale, bps, N)
    x1, mix, cat = mix_fwd(X, attn, ssd, pool, bm, lw.wo)
    x2, mo, r, h2, out = mlp_fwd(x1, bm, lw.n2, lw.w1, lw.w2, hooks.job("fwd_mlp", l))
    hooks.done("fwd_mlp", l, out)
    sv = _NS(X=X, h1=h1, pz=pz, pxbc=pxbc, pqa=pqa, pkva=pkva, ppool=ppool, plast=plast, xs=xs, bmat=bmat, cmat=cmat,
             dtv=dtv, y2=y2, hin=hin, q=q, k=k, v=v, cq=cq, ckv=ckv, attn=attn, lse=lse, x1=x1, mix=mix, cat=cat, mo=mo, r=r,
             h2=h2, lw=lw)
    return x2, sv


def _layer_bwd(dx2, bm, l, sv, cst, hooks):
    nb, T, bps, N = cst.nb, cst.T, cst.bps, cst.N
    lw = sv.lw
    dx1, du, dob, part_mlp, out = mlp_bwd(dx2, sv.x1, sv.mo, sv.r, bm, lw.n2, lw.w2, lw.w1, hooks.job("bwd_mlp", l))
    hooks.done("bwd_mlp", l, out)
    dw1 = mm_tn(sv.h2, du, name="wgrad_mlp1", col_blocks=True)
    dw2 = mm_tn(sv.r, dob, square_a=True, name="wgrad_mlp2")
    dattn, dssd, dpool, dwo, part_mix = mix_bwd(dx1, sv.mix, sv.cat, bm, lw.wo)
    dppool, dwbd, part_pool = pool_bwd(sv.ppool, dpool, lw.wbd, lw.pscale, bps, N)
    dq, dk, dv = attn_bwd(sv.q, sv.k, sv.v, sv.attn, sv.lse, dattn, nb, T)
    dpqa, dpkva, dkr, dwq, dwk, dwv, part_mla = mla_prep_bwd(dq, dk, dv, sv.pqa, sv.pkva, sv.cq, sv.ckv, lw.qnw, lw.kvnw,
                                                             lw.wq, lw.wk, lw.wv, *cst.rope)
    dwqb, dwkvb, dwout = _unprep_rest(dwq, dwk, dwv, dwo)
    early = dict(w_q_b=_by_chip_cols(dwqb), w_kv_b=_by_chip_cols(dwkvb), w_out=_by_chip_rows(dwout), w_mlp1=dw1,
                 w_mlp2=_by_chip_rows(dw2))
    dyy, dz, dxs_skip, part_so = ssd_out_bwd(dssd, sv.y2, sv.xs, sv.pz, lw.dexp, lw.ssd_nw)
    dxs2, dbm2, dcm2, ddt2, da, out = ssd_scan_bwd(sv.xs, sv.bmat, sv.cmat, sv.dtv, lw.arow, cst.eexp, sv.hin, dyy,
                                                   nb, T, hooks.job("bwd_scan", l, early))
    hooks.done("bwd_scan", l, out)
    dpre, dlast_dt, part_conv = ssd_prep_bwd_a(sv.pxbc, sv.plast, lw.cw8, lw.cb, lw.dtb, dxs_skip, dxs2, dbm2, dcm2,
                                               ddt2, bps)
    dpxbc = ssd_prep_bwd_b(dpre, lw.cw8, bps)
    dx, dwinp, part_in = in_proj_bwd(dx1, sv.X, sv.h1, dz, dpxbc, dpqa, dpkva, dppool, dkr, dlast_dt, bm, lw.n1, lw.winp)

    dmod = jnp.stack([part_in[:, 0], part_in[:, 1], part_mix[:, 0], part_mlp[:, 0], part_mlp[:, 1], part_mlp[:, 2]],
                     axis=1)
    dmod = dmod.reshape(nb, bps, 6, D)
    dm_rows = jnp.concatenate([jnp.sum(dmod[:, 1:], axis=1), jnp.sum(dmod[:, 0], axis=0)[None]], axis=0)
    da_dh = jnp.sum(da[:, :, 0, :SSD_HEADS], axis=1)
    conv_parts = jnp.sum(part_conv, axis=0)
    g = _NS(
        w_in=_by_chip_cols(_unprep_in(dwinp)), dm_rows=dm_rows.reshape(3, 6 * D), **early,
        norm1_w=jnp.sum(part_in[:, 2], axis=0), norm2_w=jnp.sum(part_mlp[:, 3], axis=0),
        conv_w=conv_parts[0:4], conv_b=conv_parts[4],
        dt_bias=conv_parts[5, DT0:DT0 + 12].reshape(2, SSD_HEADS), a_log=da_dh * lw.a,
        ssd_d=jnp.sum(jnp.sum(part_so[:, 1], axis=0).reshape(SSD_HEADS, SSD_P), axis=1),
        ssd_norm_w=jnp.sum(part_so[:, 0], axis=0),
        q_a_norm_w=jnp.sum(part_mla[:, 0], axis=0), kv_a_norm_w=jnp.sum(part_mla[:, 1], axis=0),
        pool_w=jnp.stack([dwbd[i * 64:(i + 1) * 64, i * 64:(i + 1) * 64] for i in range(4)]),
        pool_scale=jnp.sum(part_pool[:, 0], axis=0))
    hooks.layer_grads(l, g)
    return dx, g


def _local_step(x, ctx, tgt, bms, lws, fw, cst, hooks=None):
    nb, N = x.shape[0], x.shape[1]
    R = nb * cst.T
    hooks = _NoHooks(lws) if hooks is None else hooks
    X = jnp.concatenate([ctx, x], axis=1).reshape(R, D)
    saved = []
    for l in range(DEPTH):
        X, sv = _layer_fwd(X, bms[l], l, cst, hooks)
        saved.append(sv)
    dX, part_fin = final_loss(X, tgt.reshape(nb * N, D), fw[None], cst.bps)
    loss = (0.5 / D) * jnp.sum(part_fin[:, 1])
    dfw = jnp.sum(part_fin[:, 0], axis=0)
    grads = [None] * DEPTH
    for l in reversed(range(DEPTH)):
        dX, grads[l] = _layer_bwd(dX, bms[l], l, saved[l], cst, hooks)
    grad_x = dX.reshape(nb, cst.T, D)[:, CTX:, :]
    return loss, grad_x, grads, dfw


def _consts(nb, N):
    T = CTX + N
    bps = T // SB
    return _NS(nb=nb, N=N, T=T, bps=bps, eexp=_eexp(), rope=_rope_tables(nb, N))


def _block_mod(modrows, cst):
    rows = []
    for b in range(cst.nb):
        rows.append(modrows[cst.nb:cst.nb + 1])
        rows.append(jnp.broadcast_to(modrows[b:b + 1], (cst.bps - 1, 6, D)))
    return jnp.pad(jnp.concatenate(rows, axis=0), ((0, 0), (0, 2), (0, 0)))


SMALL = (("norm1_w", (2, D)), ("norm2_w", (2, D)), ("conv_w", (2, 4, XBC)), ("conv_b", (2, XBC)),
         ("dt_bias", (2, 2, 6)), ("a_log", (2, 2, 6)), ("ssd_d", (2, 6)), ("ssd_norm_w", (2, 384)),
         ("q_a_norm_w", (2, 256)), ("kv_a_norm_w", (2, 256)), ("pool_w", (2, 4, 64, 64)), ("pool_scale", (2, 256)),
         ("final_norm_w", (D,)), ("mod_b", (2, 6 * D)))
SMALL_ROWS = 64
DM_ROWS = 48


def _pack_small(vals):
    flat = jnp.concatenate([vals[n].reshape(-1) for n, _ in SMALL])
    return jnp.pad(flat, (0, SMALL_ROWS * D - flat.shape[0])).reshape(SMALL_ROWS, D)


def _unpack_small(p):
    flat = p.reshape(-1)
    out, off = {}, 0
    for n, shp in SMALL:
        size = int(np.prod(shp))
        out[n] = flat[off:off + size].reshape(shp)
        off += size
    return out


def cctx_grad(parts, c_ctx):
    def body(p_ref, c_ref, o_ref):
        acc = ((p_ref[0] + p_ref[1]) + p_ref[2]) + p_ref[3]
        v = c_ref[...]
        sig = _sigmoid(v)
        o_ref[...] = acc * (sig * (1.0 + v * (1.0 - sig)))

    return pl.pallas_call(
        body, name="cctx_grad", out_shape=jax.ShapeDtypeStruct((8, D), F32),
        in_specs=[_fullspec((4, 8, D)), _fullspec((1, D))], out_specs=_fullspec((8, D)), grid=(1,),
    )(parts, c_ctx)


def kernel(x, c, ctx, c_ctx, mod_w, mod_b, norm1_w, norm2_w, w_in, conv_w, conv_b, dt_bias, a_log, ssd_d, ssd_norm_w, q_a_norm_w, w_q_b, kv_a_norm_w, w_kv_b, pool_w, pool_scale, w_out, w_mlp1, w_mlp2, final_norm_w, loss_target, m_c_ctx, m_mod_w, m_mod_b, m_norm1_w, m_norm2_w, m_w_in, m_conv_w, m_conv_b, m_dt_bias, m_a_log, m_ssd_d, m_ssd_norm_w, m_q_a_norm_w, m_w_q_b, m_kv_a_norm_w, m_w_kv_b, m_pool_w, m_pool_scale, m_w_out, m_w_mlp1, m_w_mlp2, m_final_norm_w, v_c_ctx, v_mod_w, v_mod_b, v_norm1_w, v_norm2_w, v_w_in, v_conv_w, v_conv_b, v_dt_bias, v_a_log, v_ssd_d, v_ssd_norm_w, v_q_a_norm_w, v_w_q_b, v_kv_a_norm_w, v_w_kv_b, v_pool_w, v_pool_scale, v_w_out, v_w_mlp1, v_w_mlp2, v_final_norm_w):
    nb, N = x.shape[0], x.shape[1]
    cst = _consts(nb, N)
    xi, yi, ci = _place()
    me = 4 * xi + 2 * yi + ci
    kchip = 2 * xi + yi
    mcols = mod_w.shape[2]
    cshard = conv_w.shape[2]

    blk = jnp.zeros((16, D), F32).at[0:nb].set(c).at[8:16, 0:cshard].set(conv_w.reshape(8, cshard))
    g1 = allgather_small(blk, "gather_cond")
    cond = jnp.concatenate([g1[:, 0:nb].reshape(NDEV * nb, D), c_ctx[None],
                            jnp.zeros((MODR - NDEV * nb - 1, D), F32)], axis=0)
    conv_full = [jnp.concatenate([g1[2 * k, 8 + 4 * l:12 + 4 * l, 0:cshard] for k in range(4)], axis=1)
                 for l in range(DEPTH)]

    mb = [lax.dynamic_slice_in_dim(mod_b[l], kchip * mcols, mcols)[None] for l in range(DEPTH)]
    ms = jnp.concatenate([mod_fwd(cond, mod_w[l], mb[l]) for l in range(DEPTH)], axis=0)
    g2 = allgather_chips(ms, "gather_mod")
    bms = []
    for l in range(DEPTH):
        m_all = jnp.concatenate([g2[k, MODR * l:MODR * (l + 1)] for k in range(4)], axis=1)
        mine = jnp.concatenate([lax.dynamic_slice_in_dim(m_all, nb * me, nb), m_all[NDEV * nb:NDEV * nb + 1]], axis=0)
        bms.append(_block_mod(mine.reshape(nb + 1, 6, D), cst))

    assert DEPTH == 2
    big = (w_in, w_q_b, w_kv_b, w_out, w_mlp1, w_mlp2)
    names = ("w_in", "w_q_b", "w_kv_b", "w_out", "w_mlp1", "w_mlp2")
    concat_axis = dict(w_in=1, w_q_b=1, w_kv_b=1, w_out=0, w_mlp1=1, w_mlp2=0)
    cidx = jnp.reshape(ci, (1,)).astype(jnp.int32)
    kidx = jnp.reshape(kchip, (1,)).astype(jnp.int32)
    shards = [{n: a[l].astype(MXU) for n, a in zip(names, big)} for l in range(DEPTH)]

    def core_sums(gs):
        ns = list(gs)
        got = swap_core_halves([gs[n] for n in ns])
        return {n: add_half(gs[n], r, cidx, "add_half_" + n) for n, r in zip(ns, got)}

    class Hooks:
        gathered = [dict(w_in=run_job(gather_job([shards[0]["w_in"]]), "gather_w_in")[0]), {}]
        core_sum = [{}, {}]
        received = [{}, {}]

        def whole(self, l, n):
            return jnp.concatenate([jnp.where(kchip == k, shards[l][n], self.gathered[l][n][k]) for k in range(4)],
                                   axis=concat_axis[n])

        def weights_in(self, l):
            return _prep_in(self.whole(l, "w_in"), conv_full[l], conv_b[l], dt_bias[l], a_log[l], ssd_d[l], ssd_norm_w[l],
                            q_a_norm_w[l], kv_a_norm_w[l], pool_w[l], pool_scale[l], norm1_w[l], norm2_w[l])

        def weights_rest(self, l, scan_out):
            if l == 0:
                self.gathered[0].update(zip(names[1:], scan_out))
            return _prep_rest(*[self.whole(l, n) for n in names[1:]])

        def job(self, where, l, early=None):
            if l != 0:
                return None
            if where == "fwd_scan":
                return gather_job([shards[0][n] for n in names[1:]])
            if where == "fwd_mlp":
                return gather_job([shards[1][n] for n in names])
            if where == "bwd_mlp":
                return chip_swap_job([self.core_sum[1][n][1] for n in names])
            self.core_sum[0].update(core_sums(early))
            return chip_swap_job([self.core_sum[0][n][1] for n in names[1:]])

        def done(self, where, l, out):
            if l != 0:
                return
            if where == "fwd_mlp":
                self.gathered[1].update(zip(names, out))
            elif where == "bwd_mlp":
                self.received[1].update(zip(names, out))
            elif where == "bwd_scan":
                self.received[0].update(zip(names[1:], out))

        def layer_grads(self, l, g):
            if l == 1:
                self.core_sum[1] = core_sums({n: getattr(g, n) for n in names})
            else:
                self.core_sum[0].update(core_sums(dict(w_in=g.w_in)))
                self.received[0]["w_in"] = run_job(chip_swap_job([self.core_sum[0]["w_in"][1]]), "swap_w_in")[0]

    hooks = Hooks()
    loss_part, grad_x, grads, dfw = _local_step(x, ctx, loss_target, bms, None, final_norm_w, cst, hooks)
    loss = lax.psum(loss_part, ("x", "y", "c"))
    g_own = [sum_parts(hooks.core_sum[l][n][0], hooks.received[l][n], kidx, "sum_parts_" + n)
             for n in names for l in range(DEPTH)]
    g_oth = swap_reduced_halves(g_own)

    small = {n: jnp.stack([getattr(grads[l], n) for l in range(DEPTH)]) for n, _ in SMALL if n not in ("final_norm_w", "mod_b")}
    small["final_norm_w"] = dfw
    small["mod_b"] = jnp.stack([jnp.sum(grads[l].dm_rows, axis=0) for l in range(DEPTH)])
    dm = jnp.pad(jnp.concatenate([grads[l].dm_rows for l in range(DEPTH)], axis=0), ((0, 8 - 3 * DEPTH), (0, 0)))
    g3 = allgather_small(jnp.concatenate([_pack_small(small), dm.reshape(DM_ROWS, D)], axis=0), "gather_small")
    tot = sum_leading(g3, "sum_small")
    gsmall = _unpack_small(tot[0:SMALL_ROWS])
    ctx_sum = tot[SMALL_ROWS:].reshape(8, 6 * D)
    dm_dev = g3[:, SMALL_ROWS:].reshape(NDEV, 8, 6 * D)
    g_mod_w, dpart = [], jnp.zeros((8, D), F32)
    for l in range(DEPTH):
        dm_all = jnp.concatenate([dm_dev[:, 3 * l:3 * l + nb].reshape(NDEV * nb, 6 * D), ctx_sum[3 * l + nb:3 * l + nb + 1],
                                  jnp.zeros((MODR - NDEV * nb - 1, 6 * D), F32)], axis=0)
        g_mod_w.append(mod_wgrad(cond, lax.dynamic_slice_in_dim(dm_all, kchip * mcols, mcols, axis=1)))
        dctx = jnp.pad(lax.dynamic_slice_in_dim(ctx_sum[3 * l + nb:3 * l + nb + 1], kchip * mcols, mcols, axis=1), ((0, 7), (0, 0)))
        dpart = dpart + mod_dgrad(dctx, mod_w[l])
    g_c_ctx = cctx_grad(allgather_chips(dpart, "gather_cctx"), c_ctx[None])[0]

    res = {}
    moments = ((m_w_in, v_w_in), (m_w_q_b, v_w_q_b), (m_w_kv_b, v_w_kv_b), (m_w_out, v_w_out), (m_w_mlp1, v_w_mlp1),
               (m_w_mlp2, v_w_mlp2))
    for i, (n, w, (m, v)) in enumerate(zip(names, big, moments)):
        res[n] = tuple(adamw_halves(w, m, v, g_own[DEPTH * i:DEPTH * (i + 1)], g_oth[DEPTH * i:DEPTH * (i + 1)], cidx,
                                    "adamw_" + n))
    g_mw = jnp.stack(g_mod_w)
    r_mw = adamw(mod_w.reshape(-1, mcols), g_mw.reshape(-1, mcols), m_mod_w.reshape(-1, mcols),
                 v_mod_w.reshape(-1, mcols), name="adamw_mod_w")
    res["mod_w"] = (g_mw,) + tuple(a.reshape(mod_w.shape) for a in r_mw)

    given = dict(norm1_w=(norm1_w, m_norm1_w, v_norm1_w), norm2_w=(norm2_w, m_norm2_w, v_norm2_w),
                 conv_b=(conv_b, m_conv_b, v_conv_b), dt_bias=(dt_bias, m_dt_bias, v_dt_bias),
                 a_log=(a_log, m_a_log, v_a_log), ssd_d=(ssd_d, m_ssd_d, v_ssd_d),
                 ssd_norm_w=(ssd_norm_w, m_ssd_norm_w, v_ssd_norm_w), q_a_norm_w=(q_a_norm_w, m_q_a_norm_w, v_q_a_norm_w),
                 kv_a_norm_w=(kv_a_norm_w, m_kv_a_norm_w, v_kv_a_norm_w), pool_w=(pool_w, m_pool_w, v_pool_w),
                 pool_scale=(pool_scale, m_pool_scale, v_pool_scale),
                 final_norm_w=(final_norm_w, m_final_norm_w, v_final_norm_w), mod_b=(mod_b, m_mod_b, v_mod_b))
    zero_cw = jnp.zeros((2, 4, XBC), F32)
    packs = [_pack_small({n: (given[n][i] if n in given else zero_cw) for n, _ in SMALL}) for i in range(3)]
    r_small = [_unpack_small(a) for a in adamw(packs[0], tot[0:SMALL_ROWS], packs[1], packs[2], name="adamw_small")]
    for n in given:
        res[n] = (gsmall[n], r_small[0][n], r_small[1][n], r_small[2][n])

    g_cw = lax.dynamic_slice_in_dim(gsmall["conv_w"], kchip * cshard, cshard, axis=2)
    padcw = lambda a: jnp.pad(a.reshape(8, cshard), ((0, 0), (0, 256 - cshard)))
    r_cw = adamw(padcw(conv_w), padcw(g_cw), padcw(m_conv_w), padcw(v_conv_w), name="adamw_conv_w")
    res["conv_w"] = (g_cw,) + tuple(a[:, 0:cshard].reshape(conv_w.shape) for a in r_cw)
    r_cc = adamw(c_ctx.reshape(8, 128), g_c_ctx.reshape(8, 128), m_c_ctx.reshape(8, 128), v_c_ctx.reshape(8, 128),
                 name="adamw_c_ctx")
    res["c_ctx"] = (g_c_ctx,) + tuple(a.reshape(D) for a in r_cc)

    order = ("c_ctx", "mod_w", "mod_b", "norm1_w", "norm2_w", "w_in", "conv_w", "conv_b", "dt_bias", "a_log", "ssd_d",
             "ssd_norm_w", "q_a_norm_w", "w_q_b", "kv_a_norm_w", "w_kv_b", "pool_w", "pool_scale", "w_out", "w_mlp1",
             "w_mlp2", "final_norm_w")
    return (loss, grad_x) + tuple(res[n][i] for i in range(4) for n in order)
```

```python
import functools
import math

import numpy as np
import jax
import jax.numpy as jnp
from jax import lax
from jax.experimental import pallas as pl
from jax.experimental.pallas import tpu as pltpu

F32 = jnp.float32
BF16 = jnp.bfloat16
MXU = jnp.bfloat16

D = 1024
DEPTH = 2
GRID_W = 64
CTX = 256
EPS = 1e-6
SSD_HEADS = 6
SSD_P = 64
SSD_INNER = 384
SSD_N = 128
CHUNK = 128
XBC = 896
MLA_HEADS = 6
QK_NOPE = 64
QK_ROPE = 32
QK_DIM = 96
HP = 128
QW = MLA_HEADS * HP
POOL_DIM = 256
D_FF = 4096
FF_BLK = 1024
IN_COLS = 2092
NP = 2176
P_SPLITS = (384, 896, 256, 256, 256, 128)
DT0 = 32
CAT = QW + SSD_INNER + POOL_DIM

SB = 256
TM = 512
HALO = 8

ADAM_LR = 0.001
ADAM_B1 = 0.9
ADAM_B2 = 0.999
ADAM_EPS = 1e-08
ADAM_WD = 0.01
ADAM_STEP = 10

NT = (((1,), (1,)), ((), ()))
TN = (((0,), (0,)), ((), ()))


def _cp(vmem_mb=None):
    if vmem_mb is None:
        return pltpu.CompilerParams()
    return pltpu.CompilerParams(vmem_limit_bytes=vmem_mb << 20)


def _dot(a, b):
    return jnp.dot(a, b, preferred_element_type=F32)


def _dotg(a, b, dims):
    return lax.dot_general(a, b, dims, preferred_element_type=F32)


def _dot_hi(a, b, dims=None, sel_first=False):
    dims = (((1,), (0,)), ((), ())) if dims is None else dims
    v, s = (b, a) if sel_first else (a, b)
    hi = v.astype(BF16)
    lo = (v - hi.astype(F32)).astype(BF16)
    s = s.astype(BF16)
    if sel_first:
        return _dotg(s, hi, dims) + _dotg(s, lo, dims)
    return _dotg(hi, s, dims) + _dotg(lo, s, dims)


def _rms_hat(x):
    rstd = lax.rsqrt(jnp.mean(x * x, axis=-1, keepdims=True) + EPS)
    return x * rstd, rstd


def _rms_bwd(dn, xhat, rstd, w):
    dxhat = dn * w
    dx = rstd * (dxhat - xhat * jnp.mean(dxhat * xhat, axis=-1, keepdims=True))
    return dx, jnp.sum(dn * xhat, axis=0, keepdims=True)


def _sigmoid(z):
    return 1.0 / (1.0 + jnp.exp(-z))


def _colsum(a):
    return jnp.sum(a, axis=0, keepdims=True)


def _rowspec(cols, tm=TM):
    return pl.BlockSpec((tm, cols), lambda i: (i, 0))


def _fullspec(shape):
    n = len(shape)
    return pl.BlockSpec(shape, lambda *_: (0,) * n)


def _resident(shape):
    n = len(shape)
    return pl.BlockSpec(shape, lambda *_: (0,) * n, pipeline_mode=pl.Buffered(1))


def _halo_specs(cols, nrows, halo=HALO):
    per = SB // halo
    last = nrows // halo - 1
    prev = pl.BlockSpec((halo, cols), lambda i: (jnp.maximum(i * per - 1, 0), 0))
    nxt = pl.BlockSpec((halo, cols), lambda i: (jnp.minimum((i + 1) * per, last), 0))
    return prev, nxt


def _ext_rows(cur, prev, nxt, i, blocks_per_sample):
    j = i % blocks_per_sample
    first = jnp.logical_or(j == 0, j == 1)
    last = jnp.logical_or(j == 0, j == blocks_per_sample - 1)
    p = jnp.where(first, 0.0, prev.astype(F32))
    n = jnp.where(last, 0.0, nxt.astype(F32))
    return jnp.concatenate([p, cur.astype(F32), n], axis=0)


def _shift(ext, s):
    n = ext.shape[0]
    halo = (n - SB) // 2
    return pltpu.roll(ext, (-s) % n, axis=0)[halo:halo + SB, :]


def in_proj(x, bm, nw, w):
    R = x.shape[0]

    def body(x_ref, bm_ref, nw_ref, w_ref, h_ref, *outs):
        for s in range(TM // SB):
            rows = slice(s * SB, (s + 1) * SB)
            xhat, _ = _rms_hat(x_ref[rows, :])
            h = xhat * nw_ref[...] * (1.0 + bm_ref[s, 1:2, :]) + bm_ref[s, 0:1, :]
            h_ref[rows, :] = h.astype(h_ref.dtype)
        p = _dot(h_ref[...], w_ref[...])
        off = 0
        for o, n in zip(outs, P_SPLITS):
            o[...] = p[:, off:off + n].astype(o.dtype)
            off += n

    return pl.pallas_call(
        body, name="in_proj", grid=(R // TM,),
        in_specs=[_rowspec(D), pl.BlockSpec((TM // SB, 8, D), lambda i: (i, 0, 0)), _fullspec((1, D)),
                  _fullspec((D, NP))],
        out_specs=[_rowspec(D)] + [_rowspec(n) for n in P_SPLITS],
        out_shape=[jax.ShapeDtypeStruct((R, D), MXU)]
                  + [jax.ShapeDtypeStruct((R, n), dt) for n, dt in zip(P_SPLITS, (MXU, MXU, MXU, MXU, F32, F32))],
        compiler_params=_cp(56),
    )(x, bm, nw, w)


def in_proj_bwd(dx1, x, h, dz, dxbc, dqa, dkva, dpool, dkr, ddt, bm, nw, w):
    R = x.shape[0]

    def body(dx1_ref, x_ref, h_ref, dz_ref, dxbc_ref, dqa_ref, dkva_ref, dpool_ref, dkr_ref, ddt_ref, bm_ref, nw_ref,
             w_ref, dx_ref, dw_ref, part_ref, dp_ref):
        @pl.when(pl.program_id(0) == 0)
        def _():
            dw_ref[...] = jnp.zeros_like(dw_ref)

        dp_ref[:, 0:384] = dz_ref[...].astype(dp_ref.dtype)
        dp_ref[:, 384:1280] = dxbc_ref[...].astype(dp_ref.dtype)
        dp_ref[:, 1280:1536] = dqa_ref[...].astype(dp_ref.dtype)
        dp_ref[:, 1536:1792] = dkva_ref[...].astype(dp_ref.dtype)
        dp_ref[:, 1792:2048] = dpool_ref[...].astype(dp_ref.dtype)
        dp_ref[:, 2048:2176] = (dkr_ref[...] + ddt_ref[...]).astype(dp_ref.dtype)
        dw_ref[...] += _dotg(h_ref[...], dp_ref[...], TN)
        dh = _dotg(dp_ref[...], w_ref[...], NT)
        w = nw_ref[...]
        for s in range(TM // SB):
            rows = slice(s * SB, (s + 1) * SB)
            xhat, rstd = _rms_hat(x_ref[rows, :])
            dhs = dh[rows, :]
            sc1 = 1.0 + bm_ref[s, 1:2, :]
            dx, dnw = _rms_bwd(dhs * sc1, xhat, rstd, w)
            dx_ref[rows, :] = dx1_ref[rows, :] + dx
            part_ref[s] = jnp.concatenate(
                [_colsum(dhs), _colsum(dhs * xhat * w), dnw, jnp.zeros((5, D), F32)], axis=0)

    return pl.pallas_call(
        body, name="in_proj_bwd", grid=(R // TM,),
        in_specs=[_rowspec(D), _rowspec(D), _rowspec(D), _rowspec(384), _rowspec(896), _rowspec(256), _rowspec(256),
                  _rowspec(256), _rowspec(128), _rowspec(128),
                  pl.BlockSpec((TM // SB, 8, D), lambda i: (i, 0, 0)), _fullspec((1, D)), _resident((D, NP))],
        out_specs=[_rowspec(D), _fullspec((D, NP)), pl.BlockSpec((TM // SB, 8, D), lambda i: (i, 0, 0))],
        out_shape=[jax.ShapeDtypeStruct((R, D), F32), jax.ShapeDtypeStruct((D, NP), F32),
                   jax.ShapeDtypeStruct((R // SB, 8, D), F32)],
        scratch_shapes=[pltpu.VMEM((TM, NP), MXU)],
        compiler_params=_cp(56),
    )(dx1, x, h, dz, dxbc, dqa, dkva, dpool, dkr, ddt, bm, nw, w)


def mix_fwd(x, attn, ssd, pool, bm, wo):
    R = x.shape[0]

    def body(x_ref, a_ref, s_ref, p_ref, bm_ref, wo_ref, x1_ref, mix_ref, cat_ref):
        cat_ref[:, 0:QW] = a_ref[...].astype(cat_ref.dtype)
        cat_ref[:, QW:QW + SSD_INNER] = s_ref[...].astype(cat_ref.dtype)
        cat_ref[:, QW + SSD_INNER:CAT] = p_ref[...].astype(cat_ref.dtype)
        mix = _dot(cat_ref[...], wo_ref[...])
        mix_ref[...] = mix.astype(mix_ref.dtype)
        for s in range(TM // SB):
            rows = slice(s * SB, (s + 1) * SB)
            x1_ref[rows, :] = x_ref[rows, :] + bm_ref[s, 2:3, :] * mix[rows, :]

    return pl.pallas_call(
        body, name="mix_fwd", grid=(R // TM,),
        in_specs=[_rowspec(D), _rowspec(QW), _rowspec(SSD_INNER), _rowspec(POOL_DIM),
                  pl.BlockSpec((TM // SB, 8, D), lambda i: (i, 0, 0)), _fullspec((CAT, D))],
        out_specs=[_rowspec(D), _rowspec(D), _rowspec(CAT)],
        out_shape=[jax.ShapeDtypeStruct((R, D), F32), jax.ShapeDtypeStruct((R, D), MXU),
                   jax.ShapeDtypeStruct((R, CAT), MXU)],
        compiler_params=_cp(48),
    )(x, attn, ssd, pool, bm, wo)


def mix_bwd(dx1, mix, cat, bm, wo):
    R = dx1.shape[0]

    def body(dx1_ref, mix_ref, cat_ref, bm_ref, wo_ref, da_ref, ds_ref, dpl_ref, dw_ref, part_ref, dmb_ref):
        @pl.when(pl.program_id(0) == 0)
        def _():
            dw_ref[...] = jnp.zeros_like(dw_ref)

        for s in range(TM // SB):
            rows = slice(s * SB, (s + 1) * SB)
            d = dx1_ref[rows, :]
            dmb_ref[rows, :] = (d * bm_ref[s, 2:3, :]).astype(dmb_ref.dtype)
            part_ref[s] = jnp.concatenate([_colsum(d * mix_ref[rows, :].astype(F32)), jnp.zeros((7, D), F32)], axis=0)
        dw_ref[...] += _dotg(cat_ref[...], dmb_ref[...], TN)
        dcat = _dotg(dmb_ref[...], wo_ref[...], NT)
        da_ref[...] = dcat[:, 0:QW]
        ds_ref[...] = dcat[:, QW:QW + SSD_INNER]
        dpl_ref[...] = dcat[:, QW + SSD_INNER:CAT]

    return pl.pallas_call(
        body, name="mix_bwd", grid=(R // TM,),
        in_specs=[_rowspec(D), _rowspec(D), _rowspec(CAT), pl.BlockSpec((TM // SB, 8, D), lambda i: (i, 0, 0)),
                  _resident((CAT, D))],
        out_specs=[_rowspec(QW), _rowspec(SSD_INNER), _rowspec(POOL_DIM), _fullspec((CAT, D)),
                   pl.BlockSpec((TM // SB, 8, D), lambda i: (i, 0, 0))],
        out_shape=[jax.ShapeDtypeStruct((R, QW), F32), jax.ShapeDtypeStruct((R, SSD_INNER), F32),
                   jax.ShapeDtypeStruct((R, POOL_DIM), F32), jax.ShapeDtypeStruct((CAT, D), F32),
                   jax.ShapeDtypeStruct((R // SB, 8, D), F32)],
        scratch_shapes=[pltpu.VMEM((TM, D), MXU)],
        compiler_params=_cp(48),
    )(dx1, mix, cat, bm, wo)


def mlp_fwd(x1, bm, nw, w1, w2, side=None):
    R = x1.shape[0]

    def body(x1_ref, bm_ref, nw_ref, w1_ref, w2_ref, x2_ref, mo_ref, r_ref, h2_ref):
        for s in range(TM // SB):
            rows = slice(s * SB, (s + 1) * SB)
            xhat, _ = _rms_hat(x1_ref[rows, :])
            h = xhat * nw_ref[...] * (1.0 + bm_ref[s, 4:5, :]) + bm_ref[s, 3:4, :]
            h2_ref[rows, :] = h.astype(h2_ref.dtype)
        for j in range(D_FF // FF_BLK):
            cols = slice(j * FF_BLK, (j + 1) * FF_BLK)
            r = jnp.maximum(_dot(h2_ref[...], w1_ref[:, cols]), 0.0)
            r_ref[:, cols] = r.astype(r_ref.dtype)
            d = _dot((r * r).astype(MXU), w2_ref[cols, :])
            if j == 0:
                x2_ref[...] = d
            else:
                x2_ref[...] += d
        mo_ref[...] = x2_ref[...].astype(mo_ref.dtype)
        for s in range(TM // SB):
            rows = slice(s * SB, (s + 1) * SB)
            x2_ref[rows, :] = x1_ref[rows, :] + bm_ref[s, 5:6, :] * x2_ref[rows, :]

    grid = (R // TM,)
    body, side_in, side_out, side_shapes, side_scratch, side_args = _side_wrap(body, 5, 4, 0, side, grid)
    outs = pl.pallas_call(
        body, name="mlp_fwd" if side is None else "mlp_fwd_comm", grid=grid,
        in_specs=[_rowspec(D), pl.BlockSpec((TM // SB, 8, D), lambda i: (i, 0, 0)), _fullspec((1, D)),
                  _resident((D, D_FF)), _resident((D_FF, D))] + side_in,
        out_specs=[_rowspec(D), _rowspec(D), _rowspec(D_FF), _rowspec(D)] + side_out,
        out_shape=[jax.ShapeDtypeStruct((R, D), F32), jax.ShapeDtypeStruct((R, D), MXU),
                   jax.ShapeDtypeStruct((R, D_FF), BF16), jax.ShapeDtypeStruct((R, D), MXU)] + side_shapes,
        scratch_shapes=side_scratch,
        compiler_params=_cp(56),
    )(x1, bm, nw, w1, w2, *side_args)
    return tuple(outs[:4]) + (list(outs[4:]),)


def mlp_bwd(dx2, x1, mo, r, bm, nw, w2, w1, side=None):
    R = x1.shape[0]

    def body(dx2_ref, x1_ref, mo_ref, r_ref, bm_ref, nw_ref, w2_ref, w1_ref, dx1_ref, du_ref, dob_ref, part_ref,
             acc_ref):
        for s in range(TM // SB):
            rows = slice(s * SB, (s + 1) * SB)
            dob_ref[rows, :] = (dx2_ref[rows, :] * bm_ref[s, 5:6, :]).astype(dob_ref.dtype)
        for j in range(D_FF // FF_BLK):
            cols = slice(j * FF_BLK, (j + 1) * FF_BLK)
            du = _dotg(dob_ref[...], w2_ref[cols, :], NT) * (2.0 * r_ref[:, cols].astype(F32))
            du_ref[:, cols] = du.astype(du_ref.dtype)
            d = _dotg(du_ref[:, cols], w1_ref[:, cols], NT)
            if j == 0:
                acc_ref[...] = d
            else:
                acc_ref[...] += d
        w = nw_ref[...]
        for s in range(TM // SB):
            rows = slice(s * SB, (s + 1) * SB)
            xhat, rstd = _rms_hat(x1_ref[rows, :])
            dh = acc_ref[rows, :]
            dx, dnw = _rms_bwd(dh * (1.0 + bm_ref[s, 4:5, :]), xhat, rstd, w)
            d2 = dx2_ref[rows, :]
            dx1_ref[rows, :] = d2 + dx
            part_ref[s] = jnp.concatenate(
                [_colsum(dh), _colsum(dh * xhat * w), _colsum(d2 * mo_ref[rows, :].astype(F32)), dnw,
                 jnp.zeros((4, D), F32)], axis=0)

    grid = (R // TM,)
    body, side_in, side_out, side_shapes, side_scratch, side_args = _side_wrap(body, 8, 4, 1, side, grid)
    outs = pl.pallas_call(
        body, name="mlp_bwd" if side is None else "mlp_bwd_comm", grid=grid,
        in_specs=[_rowspec(D), _rowspec(D), _rowspec(D), _rowspec(D_FF),
                  pl.BlockSpec((TM // SB, 8, D), lambda i: (i, 0, 0)), _fullspec((1, D)),
                  _resident((D_FF, D)), _resident((D, D_FF))] + side_in,
        out_specs=[_rowspec(D), _rowspec(D_FF), _rowspec(D), pl.BlockSpec((TM // SB, 8, D), lambda i: (i, 0, 0))]
                  + side_out,
        out_shape=[jax.ShapeDtypeStruct((R, D), F32), jax.ShapeDtypeStruct((R, D_FF), MXU),
                   jax.ShapeDtypeStruct((R, D), MXU), jax.ShapeDtypeStruct((R // SB, 8, D), F32)] + side_shapes,
        scratch_shapes=[pltpu.VMEM((TM, D), F32)] + side_scratch,
        compiler_params=_cp(56),
    )(dx2, x1, mo, r, bm, nw, w2, w1, *side_args)
    return tuple(outs[:4]) + (list(outs[4:]),)


def mm_tn(a, b, square_a=False, name="mm_tn", col_blocks=False):
    R, M = a.shape
    N = b.shape[1]
    tm = M if M <= 1408 else 1024
    tn = N if N <= 2176 else 1024
    tk = next((c for c in ((2176, 1088, 512) if tm + tn <= 2048 else (1088, 512)) if R % c == 0), R)
    assert not col_blocks or tm == M

    def body(a_ref, b_ref, o_ref):
        @pl.when(pl.program_id(2) == 0)
        def _():
            o_ref[...] = jnp.zeros_like(o_ref)

        av = a_ref[...]
        if square_a:
            av = av.astype(F32)
            av = (av * av).astype(MXU)
        prod = _dotg(av.astype(MXU), b_ref[...].astype(MXU), TN)
        if col_blocks:
            o_ref[0] += prod
        else:
            o_ref[...] += prod

    if col_blocks:
        out_spec = pl.BlockSpec((1, tm, tn), lambda i, j, k: (j, 0, 0))
        out_shape = jax.ShapeDtypeStruct((N // tn, M, tn), F32)
    else:
        out_spec = pl.BlockSpec((tm, tn), lambda i, j, k: (i, j))
        out_shape = jax.ShapeDtypeStruct((M, N), F32)
    return pl.pallas_call(
        body, name=name, grid=(M // tm, N // tn, R // tk),
        in_specs=[pl.BlockSpec((tk, tm), lambda i, j, k: (k, i)), pl.BlockSpec((tk, tn), lambda i, j, k: (k, j))],
        out_specs=out_spec, out_shape=out_shape,
        compiler_params=_cp(48),
    )(a, b)


def final_loss(x, tgt, fw, blocks_per_sample):
    R = x.shape[0]
    nxb = blocks_per_sample - 1

    def body(x_ref, t_ref, fw_ref, dx_ref, part_ref):
        i = pl.program_id(0)
        is_ctx = (i % blocks_per_sample) == 0
        xhat, rstd = _rms_hat(x_ref[...])
        w = fw_ref[...]
        err = xhat * w - t_ref[...]
        dx, dfw = _rms_bwd(err * (1.0 / D), xhat, rstd, w)
        keep = jnp.where(is_ctx, 0.0, 1.0)
        dx_ref[...] = dx * keep
        part_ref[0] = jnp.concatenate([dfw * keep, _colsum(err * err) * keep, jnp.zeros((6, D), F32)], axis=0)

    def tmap(i):
        return ((i // blocks_per_sample) * nxb + jnp.maximum(i % blocks_per_sample - 1, 0), 0)

    return pl.pallas_call(
        body, name="final_loss", grid=(R // SB,),
        in_specs=[_rowspec(D, SB), pl.BlockSpec((SB, D), tmap), _fullspec((1, D))],
        out_specs=[_rowspec(D, SB), pl.BlockSpec((1, 8, D), lambda i: (i, 0, 0))],
        out_shape=[jax.ShapeDtypeStruct((R, D), F32), jax.ShapeDtypeStruct((R // SB, 8, D), F32)],
    )(x, tgt, fw)


def _softplus(v):
    return jnp.maximum(v, 0.0) + jnp.log(1.0 + jnp.exp(-jnp.abs(v)))


def _conv_taps(ext):
    return [_shift(ext, k - 1) for k in range(4)]


def _conv_out(taps, cw_ref, cb_ref):
    return (cb_ref[...] + cw_ref[0:1, :] * taps[0] + cw_ref[1:2, :] * taps[1] + cw_ref[2:3, :] * taps[2]
            + cw_ref[3:4, :] * taps[3])


def _dt_dir(v, d):
    lane = lax.broadcasted_iota(jnp.int32, v.shape, 1)
    return jnp.where(lane < SSD_HEADS, pltpu.roll(v, (128 - DT0 - SSD_HEADS * d) % 128, axis=1), 0.0)


def ssd_prep(pxbc, plast, cw, cb, dtb, blocks_per_sample):
    R = pxbc.shape[0]
    prev, nxt = _halo_specs(XBC, R, 8 * 4 // pxbc.dtype.itemsize)

    def body(cur_ref, prev_ref, nxt_ref, pl_ref, cw_ref, cb_ref, dtb_ref, xs_ref, bm_ref, cm_ref, dt_ref):
        i = pl.program_id(0)
        ext = _ext_rows(cur_ref[...], prev_ref[...], nxt_ref[...], i, blocks_per_sample)
        co = _conv_out(_conv_taps(ext), cw_ref, cb_ref)
        a = co * _sigmoid(co)
        xs_ref[...] = a[:, 0:384]
        bm_ref[...] = a[:, 384:640]
        cm_ref[...] = a[:, 640:896]
        sp = _softplus(pl_ref[...] + dtb_ref[...])
        dt_ref[0] = _dt_dir(sp, 0)
        dt_ref[1] = _dt_dir(sp, 1)

    return pl.pallas_call(
        body, name="ssd_prep", grid=(R // SB,),
        in_specs=[_rowspec(XBC, SB), prev, nxt, _rowspec(128, SB), _fullspec((8, XBC)), _fullspec((1, XBC)),
                  _fullspec((1, 128))],
        out_specs=[_rowspec(384, SB), _rowspec(256, SB), _rowspec(256, SB),
                   pl.BlockSpec((2, SB, 128), lambda i: (0, i, 0))],
        out_shape=[jax.ShapeDtypeStruct((R, 384), F32), jax.ShapeDtypeStruct((R, 256), F32),
                   jax.ShapeDtypeStruct((R, 256), F32), jax.ShapeDtypeStruct((2, R, 128), F32)],
    )(pxbc, pxbc, pxbc, plast, cw, cb, dtb)


def _chunk_index(d, s, nc):
    nctx = CTX // CHUNK
    back = jnp.where(s < nctx, nctx - 1 - s, nc + nctx - 1 - s)
    return jnp.where(d == 0, s, back)


def _scan_common(d, dt, arow, eexp, xs):
    ii = lax.broadcasted_iota(jnp.int32, (CHUNK, CHUNK), 0)
    jj = lax.broadcasted_iota(jnp.int32, (CHUNK, CHUNK), 1)
    mask = ((ii - jj) * (1 - 2 * d)) >= 0
    adt = dt * arow
    tmat = jnp.where(mask, 1.0, 0.0)
    cs = _dot_hi(tmat, adt, sel_first=True)
    tot = _colsum(adt)
    dtx = _dot_hi(dt, eexp)
    xt = xs * dtx
    ecs = jnp.exp(cs)
    ecx = _dot_hi(ecs, eexp)
    dte = jnp.exp(tot - cs)
    dtex = _dot_hi(dte, eexp)
    etot = jnp.exp(tot)
    etx = _dot_hi(jnp.broadcast_to(etot, (8, 128)), eexp)[0:1, :]
    return mask, tmat, adt, cs, tot, dtx, xt, ecs, ecx, dte, dtex, etot, etx


def _decay_matrix(mask, cs, cst, h):
    return jnp.exp(jnp.where(mask, cs[:, h:h + 1] - cst[h:h + 1, :], -1e30))


def _side_wrap(body, n_in, n_out, n_scratch, side, grid):
    if side is None:
        return body, [], [], [], [], []
    ni, no = len(side.ins), len(side.out_shapes)

    def wrapped(*refs):
        ins, refs = refs[:n_in], refs[n_in:]
        side_ins, refs = refs[:ni], refs[ni:]
        outs, refs = refs[:n_out], refs[n_out:]
        side_outs, refs = refs[:no], refs[no:]
        scratch, sems = refs[:n_scratch], refs[n_scratch:]
        ids = [pl.program_id(a) for a in range(len(grid))]
        first = functools.reduce(jnp.logical_and, [i == 0 for i in ids])
        last = functools.reduce(jnp.logical_and, [i == g - 1 for i, g in zip(ids, grid)])
        pl.when(first)(lambda: side.start(side_ins, side_outs, sems))
        body(*ins, *outs, *scratch)
        pl.when(last)(lambda: side.finish(side_ins, side_outs, sems))

    return wrapped, [ANY] * ni, [ANY] * no, list(side.out_shapes), _sems(side.nsem), list(side.ins)


def ssd_scan_fwd(xs, bm, cm, dtv, arow, eexp, nb, T, side=None):
    R = xs.shape[0]
    nc = T // CHUNK
    B = range(nb)

    def body(xs_ref, bm_ref, cm_ref, dt_ref, a_ref, e_ref, y_ref, hin_ref, st_ref):
        d = pl.program_id(0)
        s = pl.program_id(1)

        @pl.when(s == 0)
        def _():
            st_ref[...] = jnp.zeros_like(st_ref)

        eexp = e_ref[...]
        com = [_scan_common(d, dt_ref[0, b], a_ref[0, 0:1, :], eexp, xs_ref[b]) for b in B]
        mask = com[0][0]
        cs = [com[b][3] for b in B]
        cst = [cs[b].T for b in B]
        sin = [st_ref[b] for b in B]
        for b in B:
            hin_ref[0, b] = sin[b]
        sb = [sin[b].astype(MXU) for b in B]
        xtb = [com[b][6].astype(MXU) for b in B]
        xw = [(com[b][6] * com[b][10]).astype(MXU) for b in B]
        g0 = lax.broadcasted_iota(jnp.int32, (CHUNK, SSD_INNER), 1) < 192
        lane = lax.broadcasted_iota(jnp.int32, (CHUNK, 128), 1)
        c = [[cm_ref[b, :, 0:128].astype(MXU), cm_ref[b, :, 128:256].astype(MXU)] for b in B]
        bq = [[bm_ref[b, :, 0:128].astype(MXU), bm_ref[b, :, 128:256].astype(MXU)] for b in B]
        y = [jnp.where(g0, _dot(c[b][0], sb[b]), _dot(c[b][1], sb[b])) * com[b][8] for b in B]
        cb = [[_dotg(c[b][g], bq[b][g], NT) for g in range(2)] for b in B]
        blocks = [[] for _ in B]
        for blk in range(3):
            acc = [None for _ in B]
            for hh in range(2):
                h = blk * 2 + hh
                for b in B:
                    m = (cb[b][h // 3] * _decay_matrix(mask, cs[b], cst[b], h)).astype(MXU)
                    res = _dot(m, xtb[b][:, blk * 128:(blk + 1) * 128])
                    acc[b] = res if hh == 0 else jnp.where(lane < 64, acc[b], res)
            for b in B:
                blocks[b].append(acc[b])
        for b in B:
            y_ref[0, b] = y[b] + jnp.concatenate(blocks[b], axis=1)
            st_ref[b] = sin[b] * com[b][12] + jnp.where(g0, _dotg(bq[b][0], xw[b], TN), _dotg(bq[b][1], xw[b], TN))

    def rows(cols):
        return pl.BlockSpec((nb, CHUNK, cols), lambda d, s: (0, _chunk_index(d, s, nc), 0))

    def by_dir(cols):
        return pl.BlockSpec((1, nb, CHUNK, cols), lambda d, s: (d, 0, _chunk_index(d, s, nc), 0))

    grid = (2, nc)
    body, side_in, side_out, side_shapes, side_scratch, side_args = _side_wrap(body, 6, 2, 1, side, grid)
    outs = pl.pallas_call(
        body, name="ssd_scan_fwd" if side is None else "ssd_scan_fwd_comm", grid=grid,
        in_specs=[rows(384), rows(256), rows(256), by_dir(128), pl.BlockSpec((1, 8, 128), lambda d, s: (d, 0, 0)),
                  pl.BlockSpec((128, 384), lambda d, s: (0, 0))] + side_in,
        out_specs=[by_dir(384),
                   pl.BlockSpec((1, nb, CHUNK, 384), lambda d, s: (d * nc + _chunk_index(d, s, nc), 0, 0, 0))] + side_out,
        out_shape=[jax.ShapeDtypeStruct((2, nb, T, 384), F32), jax.ShapeDtypeStruct((2 * nc, nb, CHUNK, 384), F32)]
                  + side_shapes,
        scratch_shapes=[pltpu.VMEM((nb, CHUNK, 384), F32)] + side_scratch,
    )(xs.reshape(nb, T, 384), bm.reshape(nb, T, 256), cm.reshape(nb, T, 256), dtv.reshape(2, nb, T, 128), arow, eexp,
      *side_args)
    return outs[0].reshape(2, R, 384), outs[1], list(outs[2:])


def ssd_scan_bwd(xs, bm, cm, dtv, arow, eexp, hin, dy, nb, T, side=None):
    R = xs.shape[0]
    nc = T // CHUNK
    B = range(nb)

    def chunk(d, s):
        return _chunk_index(d, nc - 1 - s, nc)

    def body(xs_ref, bm_ref, cm_ref, dt_ref, a_ref, e_ref, hin_ref, dy_ref,
             dxs_ref, dbm_ref, dcm_ref, ddt_ref, da_ref, ds_ref):
        d = pl.program_id(0)
        s = pl.program_id(1)

        @pl.when(s == 0)
        def _():
            ds_ref[...] = jnp.zeros_like(ds_ref)
            da_ref[...] = jnp.zeros_like(da_ref)

        eexp = e_ref[...]
        arow = a_ref[0, 0:1, :]
        dt = [dt_ref[0, b] for b in B]
        xs_v = [xs_ref[b] for b in B]
        com = [_scan_common(d, dt[b], arow, eexp, xs_v[b]) for b in B]
        mask, tmat = com[0][0], com[0][1]
        cs, dtx, xt, ecs, ecx, dte, dtex, etot, etx = [[com[b][i] for b in B] for i in (3, 5, 6, 7, 8, 9, 10, 11, 12)]
        cst = [cs[b].T for b in B]
        sin = [hin_ref[0, b] for b in B]
        sb = [sin[b].astype(MXU) for b in B]
        dsp = [ds_ref[b] for b in B]
        dyv = [dy_ref[b] for b in B]
        xtb = [xt[b].astype(MXU) for b in B]
        xw = [(xt[b] * dtex[b]).astype(MXU) for b in B]
        g0 = lax.broadcasted_iota(jnp.int32, (CHUNK, SSD_INNER), 1) < 192
        lane = lax.broadcasted_iota(jnp.int32, (CHUNK, 128), 1)
        sub = lax.broadcasted_iota(jnp.int32, (CHUNK, 128), 0)
        c = [[cm_ref[b, :, 0:128].astype(MXU), cm_ref[b, :, 128:256].astype(MXU)] for b in B]
        bq = [[bm_ref[b, :, 0:128].astype(MXU), bm_ref[b, :, 128:256].astype(MXU)] for b in B]

        cs_prod = [jnp.where(g0, _dot(c[b][0], sb[b]), _dot(c[b][1], sb[b])) for b in B]
        dcsp = [dyv[b] * ecx[b] for b in B]
        dcsp_g = [[jnp.where(g0, dcsp[b], 0.0).astype(MXU), jnp.where(g0, 0.0, dcsp[b]).astype(MXU)] for b in B]
        dcs = [_dot_hi(dyv[b] * cs_prod[b], eexp, NT) * ecs[b] for b in B]
        dc = [[_dotg(dcsp_g[b][g], sb[b], NT) for g in range(2)] for b in B]
        dsin = [_dotg(c[b][0], dcsp_g[b][0], TN) + _dotg(c[b][1], dcsp_g[b][1], TN) + dsp[b] * etx[b] for b in B]

        dtot = [_dot_hi(jnp.broadcast_to(_colsum(dsp[b] * sin[b]), (8, SSD_INNER)), eexp, NT)[0:1, :] * etot[b] for b in B]
        dsp_g = [[jnp.where(g0, dsp[b], 0.0).astype(MXU), jnp.where(g0, 0.0, dsp[b]).astype(MXU)] for b in B]
        dxw = [_dot(bq[b][0], dsp_g[b][0]) + _dot(bq[b][1], dsp_g[b][1]) for b in B]
        db = [[_dotg(xw[b], dsp_g[b][g], NT) for g in range(2)] for b in B]
        dxt = [dxw[b] * dtex[b] for b in B]
        ddte = [_dot_hi(dxw[b] * xt[b], eexp, NT) * dte[b] for b in B]
        dtot = [dtot[b] + _colsum(ddte[b]) for b in B]
        dcs = [dcs[b] - ddte[b] for b in B]

        cb = [[_dotg(c[b][g], bq[b][g], NT) for g in range(2)] for b in B]
        dg = [[jnp.zeros((CHUNK, CHUNK), F32), jnp.zeros((CHUNK, CHUNK), F32)] for _ in B]
        dcs_rows = [jnp.zeros((CHUNK, 128), F32) for _ in B]
        dxt_blocks = [[] for _ in B]
        for blk in range(3):
            acc = [jnp.zeros((CHUNK, 128), F32) for _ in B]
            for hh in range(2):
                h = blk * 2 + hh
                g = h // 3
                mine = (lane < 64) if hh == 0 else (lane >= 64)
                for b in B:
                    dyh = jnp.where(mine, dyv[b][:, blk * 128:(blk + 1) * 128], 0.0).astype(MXU)
                    lh = _decay_matrix(mask, cs[b], cst[b], h)
                    m = cb[b][g] * lh
                    dm = _dotg(dyh, xtb[b][:, blk * 128:(blk + 1) * 128], NT)
                    acc[b] = acc[b] + _dotg(m.astype(MXU), dyh, TN)
                    dg[b][g] = dg[b][g] + dm * lh
                    q = dm * m
                    dcs[b] = dcs[b] + jnp.where(lane == h, jnp.sum(q, axis=1, keepdims=True), 0.0)
                    dcs_rows[b] = dcs_rows[b] - jnp.where(sub == h, jnp.sum(q, axis=0, keepdims=True), 0.0)
            for b in B:
                dxt_blocks[b].append(acc[b])
        for b in B:
            dxt[b] = dxt[b] + jnp.concatenate(dxt_blocks[b], axis=1)
            for g in range(2):
                dgb = dg[b][g].astype(MXU)
                dc[b][g] = dc[b][g] + _dot(dgb, bq[b][g])
                db[b][g] = db[b][g] + _dotg(dgb, c[b][g], TN)
            dcs[b] = dcs[b] + dcs_rows[b].T

        for b in B:
            dadt = _dot_hi(tmat, dcs[b], TN, sel_first=True) + dtot[b]
            ddt_ref[0, b] = dadt * arow + _dot_hi(dxt[b] * xs_v[b], eexp, NT)
            da_ref[0, b, 0:1, :] += _colsum(dadt * dt[b])
            dxs_ref[0, b] = (dxt[b] * dtx[b]).astype(dxs_ref.dtype)
            dbm_ref[0, b] = jnp.concatenate(db[b], axis=1).astype(dbm_ref.dtype)
            dcm_ref[0, b] = jnp.concatenate(dc[b], axis=1).astype(dcm_ref.dtype)
            ds_ref[b] = dsin[b]

    def rows(cols):
        return pl.BlockSpec((nb, CHUNK, cols), lambda d, s: (0, chunk(d, s), 0))

    def by_dir(cols):
        return pl.BlockSpec((1, nb, CHUNK, cols), lambda d, s: (d, 0, chunk(d, s), 0))

    grid = (2, nc)
    body, side_in, side_out, side_shapes, side_scratch, side_args = _side_wrap(body, 8, 5, 1, side, grid)
    outs = pl.pallas_call(
        body, name="ssd_scan_bwd" if side is None else "ssd_scan_bwd_comm", grid=grid,
        in_specs=[rows(384), rows(256), rows(256), by_dir(128), pl.BlockSpec((1, 8, 128), lambda d, s: (d, 0, 0)),
                  pl.BlockSpec((128, 384), lambda d, s: (0, 0)),
                  pl.BlockSpec((1, nb, CHUNK, 384), lambda d, s: (d * nc + chunk(d, s), 0, 0, 0)), rows(384)] + side_in,
        out_specs=[by_dir(384), by_dir(256), by_dir(256), by_dir(128),
                   pl.BlockSpec((1, nb, 8, 128), lambda d, s: (d, 0, 0, 0))] + side_out,
        out_shape=[jax.ShapeDtypeStruct((2, nb, T, 384), MXU), jax.ShapeDtypeStruct((2, nb, T, 256), MXU),
                   jax.ShapeDtypeStruct((2, nb, T, 256), MXU), jax.ShapeDtypeStruct((2, nb, T, 128), F32),
                   jax.ShapeDtypeStruct((2, nb, 8, 128), F32)] + side_shapes,
        scratch_shapes=[pltpu.VMEM((nb, CHUNK, 384), F32)] + side_scratch,
    )(xs.reshape(nb, T, 384), bm.reshape(nb, T, 256), cm.reshape(nb, T, 256), dtv.reshape(2, nb, T, 128), arow, eexp,
      hin, dy.reshape(nb, T, 384), *side_args)
    return (outs[0].reshape(2, R, 384), outs[1].reshape(2, R, 256), outs[2].reshape(2, R, 256),
            outs[3].reshape(2, R, 128), outs[4], list(outs[5:]))


def _group_rms(g):
    lane = lax.broadcasted_iota(jnp.int32, g.shape, 1)
    g0 = lane < 192
    gg = g * g
    s0 = jnp.sum(jnp.where(g0, gg, 0.0), axis=-1, keepdims=True)
    s1 = jnp.sum(gg, axis=-1, keepdims=True) - s0
    rstd = jnp.where(g0, lax.rsqrt(s0 * (1.0 / 192) + EPS), lax.rsqrt(s1 * (1.0 / 192) + EPS))
    return rstd, g0


def ssd_out_fwd(y2, xs, pz, dexp, nw):
    R = xs.shape[0]

    def body(y_ref, xs_ref, z_ref, d_ref, nw_ref, o_ref):
        z = z_ref[...].astype(F32)
        yy = y_ref[0] + y_ref[1] + xs_ref[...] * d_ref[...]
        g = yy * (z * _sigmoid(z))
        rstd, _ = _group_rms(g)
        o_ref[...] = g * rstd * nw_ref[...]

    return pl.pallas_call(
        body, name="ssd_out_fwd", grid=(R // TM,),
        in_specs=[pl.BlockSpec((2, TM, 384), lambda i: (0, i, 0)), _rowspec(384), _rowspec(384),
                  _fullspec((1, 384)), _fullspec((1, 384))],
        out_specs=_rowspec(384),
        out_shape=jax.ShapeDtypeStruct((R, 384), F32),
    )(y2, xs, pz, dexp, nw)


def ssd_out_bwd(dout, y2, xs, pz, dexp, nw):
    R = xs.shape[0]

    def body(do_ref, y_ref, xs_ref, z_ref, d_ref, nw_ref, dy_ref, dz_ref, dxs_ref, part_ref):
        z = z_ref[...].astype(F32)
        xs_v = xs_ref[...]
        yy = y_ref[0] + y_ref[1] + xs_v * d_ref[...]
        sig = _sigmoid(z)
        sz = z * sig
        g = yy * sz
        rstd, g0 = _group_rms(g)
        ghat = g * rstd
        do = do_ref[...]
        dgn = do * nw_ref[...]
        t = dgn * ghat
        t0 = jnp.sum(jnp.where(g0, t, 0.0), axis=-1, keepdims=True)
        t1 = jnp.sum(t, axis=-1, keepdims=True) - t0
        dg = rstd * (dgn - ghat * jnp.where(g0, t0, t1) * (1.0 / 192))
        dyy = dg * sz
        dy_ref[...] = dyy
        dz_ref[...] = (dg * yy * (sig * (1.0 + z * (1.0 - sig)))).astype(dz_ref.dtype)
        dxs_ref[...] = dyy * d_ref[...]
        part_ref[0] = jnp.concatenate([_colsum(do * ghat), _colsum(dyy * xs_v), jnp.zeros((6, 384), F32)], axis=0)

    return pl.pallas_call(
        body, name="ssd_out_bwd", grid=(R // TM,),
        in_specs=[_rowspec(384), pl.BlockSpec((2, TM, 384), lambda i: (0, i, 0)), _rowspec(384), _rowspec(384),
                  _fullspec((1, 384)), _fullspec((1, 384))],
        out_specs=[_rowspec(384), _rowspec(384), _rowspec(384), pl.BlockSpec((1, 8, 384), lambda i: (i, 0, 0))],
        out_shape=[jax.ShapeDtypeStruct((R, 384), F32), jax.ShapeDtypeStruct((R, 384), MXU),
                   jax.ShapeDtypeStruct((R, 384), F32), jax.ShapeDtypeStruct((R // TM, 8, 384), F32)],
    )(dout, y2, xs, pz, dexp, nw)


def ssd_prep_bwd_a(pxbc, plast, cw, cb, dtb, dxs_skip, dxs2, dbm2, dcm2, ddt2, blocks_per_sample):
    R = pxbc.shape[0]
    prev, nxt = _halo_specs(XBC, R, 8 * 4 // pxbc.dtype.itemsize)

    def body(cur_ref, prev_ref, nxt_ref, pl_ref, cw_ref, cb_ref, dtb_ref, dsk_ref, dxs_ref, dbm_ref, dcm_ref, ddt_ref,
             dpre_ref, dlast_ref, part_ref):
        i = pl.program_id(0)
        ext = _ext_rows(cur_ref[...], prev_ref[...], nxt_ref[...], i, blocks_per_sample)
        taps = _conv_taps(ext)
        co = _conv_out(taps, cw_ref, cb_ref)
        sig = _sigmoid(co)
        both = lambda ref: ref[0].astype(F32) + ref[1].astype(F32)
        up = jnp.concatenate([dsk_ref[...] + both(dxs_ref), both(dbm_ref), both(dcm_ref)], axis=1)
        dpre = up * (sig * (1.0 + co * (1.0 - sig)))
        dpre_ref[...] = dpre
        raw = pl_ref[...] + dtb_ref[...]
        lane = lax.broadcasted_iota(jnp.int32, raw.shape, 1)
        ddt = (pltpu.roll(ddt_ref[0], DT0, axis=1) + pltpu.roll(ddt_ref[1], DT0 + SSD_HEADS, axis=1))
        ddt = jnp.where(jnp.logical_and(lane >= DT0, lane < DT0 + 2 * SSD_HEADS), ddt * _sigmoid(raw), 0.0)
        dlast_ref[...] = ddt.astype(dlast_ref.dtype)
        rows = [_colsum(dpre * taps[k]) for k in range(4)]
        rows.append(_colsum(dpre))
        rows.append(jnp.concatenate([_colsum(ddt), jnp.zeros((1, XBC - 128), F32)], axis=1))
        rows.append(jnp.zeros((2, XBC), F32))
        part_ref[0] = jnp.concatenate(rows, axis=0)

    dirspec = lambda n: pl.BlockSpec((2, SB, n), lambda i: (0, i, 0))
    return pl.pallas_call(
        body, name="ssd_prep_bwd_a", grid=(R // SB,),
        in_specs=[_rowspec(XBC, SB), prev, nxt, _rowspec(128, SB), _fullspec((8, XBC)), _fullspec((1, XBC)),
                  _fullspec((1, 128)), _rowspec(384, SB), dirspec(384), dirspec(256), dirspec(256), dirspec(128)],
        out_specs=[_rowspec(XBC, SB), _rowspec(128, SB), pl.BlockSpec((1, 8, XBC), lambda i: (i, 0, 0))],
        out_shape=[jax.ShapeDtypeStruct((R, XBC), F32), jax.ShapeDtypeStruct((R, 128), MXU),
                   jax.ShapeDtypeStruct((R // SB, 8, XBC), F32)],
    )(pxbc, pxbc, pxbc, plast, cw, cb, dtb, dxs_skip, dxs2, dbm2, dcm2, ddt2)


def ssd_prep_bwd_b(dpre, cw, blocks_per_sample):
    R = dpre.shape[0]
    prev, nxt = _halo_specs(XBC, R)

    def body(cur_ref, prev_ref, nxt_ref, cw_ref, o_ref):
        i = pl.program_id(0)
        ext = _ext_rows(cur_ref[...], prev_ref[...], nxt_ref[...], i, blocks_per_sample)
        o_ref[...] = (cw_ref[0:1, :] * _shift(ext, 1) + cw_ref[1:2, :] * _shift(ext, 0)
                      + cw_ref[2:3, :] * _shift(ext, -1) + cw_ref[3:4, :] * _shift(ext, -2)).astype(o_ref.dtype)

    return pl.pallas_call(
        body, name="ssd_prep_bwd_b", grid=(R // SB,),
        in_specs=[_rowspec(XBC, SB), prev, nxt, _fullspec((8, XBC))],
        out_specs=_rowspec(XBC, SB),
        out_shape=jax.ShapeDtypeStruct((R, XBC), MXU),
    )(dpre, dpre, dpre, cw)


def _rope(u, cos, sa, sb):
    return u * cos + pltpu.roll(u, 120, axis=1) * sa + pltpu.roll(u, 8, axis=1) * sb


def _rope_t(du, cos, sa, sb):
    return du * cos + pltpu.roll(du * sa, 8, axis=1) + pltpu.roll(du * sb, 120, axis=1)


def mla_prep(pqa, pkva, plast, qnw, kvnw, wq, wk, wv, cos, sa, sb):
    R = pqa.shape[0]

    def body(qa_ref, kva_ref, pl_ref, qnw_ref, kvnw_ref, wq_ref, wk_ref, wv_ref, cos_ref, sa_ref, sb_ref,
             q_ref, k_ref, v_ref, cq_ref, ckv_ref):
        cos_v, sa_v, sb_v = cos_ref[...], sa_ref[...], sb_ref[...]
        xq, _ = _rms_hat(qa_ref[...].astype(F32))
        cq_ref[...] = (xq * qnw_ref[...]).astype(cq_ref.dtype)
        xkv, _ = _rms_hat(kva_ref[...].astype(F32))
        ckv_ref[...] = (xkv * kvnw_ref[...]).astype(ckv_ref.dtype)
        q = _dot(cq_ref[...], wq_ref[...])
        kn = _dot(ckv_ref[...], wk_ref[...])
        v_ref[...] = _dot(ckv_ref[...], wv_ref[...]).astype(v_ref.dtype)
        lane = lax.broadcasted_iota(jnp.int32, (TM, HP), 1)
        rope_lanes = jnp.logical_and(lane >= QK_NOPE, lane < QK_DIM)
        kr = _rope(jnp.where(rope_lanes, pltpu.roll(pl_ref[...], QK_NOPE, axis=1), 0.0), cos_v, sa_v, sb_v)
        for h in range(MLA_HEADS):
            cols = slice(h * HP, (h + 1) * HP)
            q_ref[:, cols] = (_rope(q[:, cols], cos_v, sa_v, sb_v) * Q_SCALE).astype(q_ref.dtype)
            k_ref[:, cols] = (kn[:, cols] + kr).astype(k_ref.dtype)

    return pl.pallas_call(
        body, name="mla_prep", grid=(R // TM,),
        in_specs=[_rowspec(256), _rowspec(256), _rowspec(128), _fullspec((1, 256)), _fullspec((1, 256)),
                  _fullspec((256, QW)), _fullspec((256, QW)), _fullspec((256, QW)),
                  _rowspec(HP), _rowspec(HP), _rowspec(HP)],
        out_specs=[_rowspec(QW), _rowspec(QW), _rowspec(QW), _rowspec(256), _rowspec(256)],
        out_shape=[jax.ShapeDtypeStruct((R, QW), MXU)] * 3 + [jax.ShapeDtypeStruct((R, 256), MXU)] * 2,
    )(pqa, pkva, plast, qnw, kvnw, wq, wk, wv, cos, sa, sb)


def mla_prep_bwd(dq, dk, dv, pqa, pkva, cq, ckv, qnw, kvnw, wq, wk, wv, cos, sa, sb):
    R = pqa.shape[0]

    def body(dq_ref, dk_ref, dv_ref, qa_ref, kva_ref, cq_ref, ckv_ref, qnw_ref, kvnw_ref, wq_ref, wk_ref, wv_ref,
             cos_ref, sa_ref, sb_ref, dqa_ref, dkva_ref, dkr_ref, dwq_ref, dwk_ref, dwv_ref, part_ref,
             dql_ref, dkm_ref, dvb_ref):
        @pl.when(pl.program_id(0) == 0)
        def _():
            dwq_ref[...] = jnp.zeros_like(dwq_ref)
            dwk_ref[...] = jnp.zeros_like(dwk_ref)
            dwv_ref[...] = jnp.zeros_like(dwv_ref)

        cos_v, sa_v, sb_v = cos_ref[...], sa_ref[...], sb_ref[...]
        lane = lax.broadcasted_iota(jnp.int32, (TM, HP), 1)
        rope_lanes = jnp.logical_and(lane >= QK_NOPE, lane < QK_DIM)
        dkr = jnp.zeros((TM, HP), F32)
        for h in range(MLA_HEADS):
            cols = slice(h * HP, (h + 1) * HP)
            dql_ref[:, cols] = (_rope_t(dq_ref[:, cols], cos_v, sa_v, sb_v) * ATT_SCALE).astype(dql_ref.dtype)
            dkh = dk_ref[:, cols] * LN2
            dkm_ref[:, cols] = jnp.where(lane < QK_NOPE, dkh, 0.0).astype(dkm_ref.dtype)
            dkr = dkr + jnp.where(rope_lanes, dkh, 0.0)
        dvb_ref[...] = dv_ref[...].astype(dvb_ref.dtype)
        dkr = jnp.where(rope_lanes, _rope_t(dkr, cos_v, sa_v, sb_v), 0.0)
        dkr_ref[...] = pltpu.roll(dkr, HP - QK_NOPE, axis=1).astype(dkr_ref.dtype)
        dwq_ref[...] += _dotg(cq_ref[...], dql_ref[...], TN)
        dwk_ref[...] += _dotg(ckv_ref[...], dkm_ref[...], TN)
        dwv_ref[...] += _dotg(ckv_ref[...], dvb_ref[...], TN)
        xq, rq = _rms_hat(qa_ref[...].astype(F32))
        dqa, dqnw = _rms_bwd(_dotg(dql_ref[...], wq_ref[...], NT), xq, rq, qnw_ref[...])
        dqa_ref[...] = dqa.astype(dqa_ref.dtype)
        xkv, rkv = _rms_hat(kva_ref[...].astype(F32))
        dckv = _dotg(dkm_ref[...], wk_ref[...], NT) + _dotg(dvb_ref[...], wv_ref[...], NT)
        dkva, dkvnw = _rms_bwd(dckv, xkv, rkv, kvnw_ref[...])
        dkva_ref[...] = dkva.astype(dkva_ref.dtype)
        part_ref[0] = jnp.concatenate([dqnw, dkvnw, jnp.zeros((6, 256), F32)], axis=0)

    return pl.pallas_call(
        body, name="mla_prep_bwd", grid=(R // TM,),
        in_specs=[_rowspec(QW), _rowspec(QW), _rowspec(QW), _rowspec(256), _rowspec(256), _rowspec(256), _rowspec(256),
                  _fullspec((1, 256)), _fullspec((1, 256)), _fullspec((256, QW)), _fullspec((256, QW)),
                  _fullspec((256, QW)), _rowspec(HP), _rowspec(HP), _rowspec(HP)],
        out_specs=[_rowspec(256), _rowspec(256), _rowspec(128), _fullspec((256, QW)), _fullspec((256, QW)),
                   _fullspec((256, QW)), pl.BlockSpec((1, 8, 256), lambda i: (i, 0, 0))],
        out_shape=[jax.ShapeDtypeStruct((R, 256), MXU), jax.ShapeDtypeStruct((R, 256), MXU),
                   jax.ShapeDtypeStruct((R, 128), MXU)] + [jax.ShapeDtypeStruct((256, QW), F32)] * 3
                  + [jax.ShapeDtypeStruct((R // TM, 8, 256), F32)],
        scratch_shapes=[pltpu.VMEM((TM, QW), MXU)] * 3,
    )(dq, dk, dv, pqa, pkva, cq, ckv, qnw, kvnw, wq, wk, wv, cos, sa, sb)


ATT_SCALE = QK_DIM ** -0.5
TQ = 256


LOG2E = 1.4426950408889634
LN2 = 0.6931471805599453
Q_SCALE = ATT_SCALE * LOG2E


def _key_chunks(T, n=2):
    unit = 256 if T % 256 == 0 else 128
    units = T // unit
    sizes = [(units // n + (1 if i < units % n else 0)) * unit for i in range(n)]
    return [(sum(sizes[:i]), sz) for i, sz in enumerate(sizes) if sz]


def attn_fwd(q, k, v, nb, T):
    R = q.shape[0]
    nq = T // TQ
    chunks = _key_chunks(T, 4)
    HEADS = range(2)

    def body(q_ref, k_ref, v_ref, o_ref, lse_ref):
        def lanes(h):
            return slice(h * HP, (h + 1) * HP)

        def logits(h, lo, n):
            return _dotg(q_ref[:, lanes(h)], k_ref[lo:lo + n, lanes(h)], NT)

        def weigh(h, s, lo, n):
            m = jnp.max(s, axis=-1, keepdims=True)
            p = jnp.exp2(s - m)
            return m, jnp.sum(p, axis=-1, keepdims=True), _dot(p.astype(MXU), v_ref[lo:lo + n, lanes(h)])

        def parts_of(ranges):
            out = [[] for _ in HEADS]
            s = [logits(h, *ranges[0]) for h in HEADS]
            for j, (lo, n) in enumerate(ranges):
                nxt = [logits(h, *ranges[j + 1]) for h in HEADS] if j + 1 < len(ranges) else None
                for h in HEADS:
                    out[h].append(weigh(h, s[h], lo, n))
                s = nxt
            return out

        def finish(all_parts):
            for h, parts in enumerate(all_parts):
                m = parts[0][0]
                for pm, _, _ in parts[1:]:
                    m = jnp.maximum(m, pm)
                l, o = 0.0, 0.0
                for pm, pl_, po in parts:
                    a = jnp.exp2(pm - m)
                    l = l + a * pl_
                    o = o + a * po
                o_ref[:, lanes(h)] = o / l
                lse_ref[:, lanes(h)] = jnp.broadcast_to(m + jnp.log(l) * LOG2E, (TQ, HP))

        i = pl.program_id(2)
        pl.when(i == 0)(lambda: finish(parts_of([(0, CTX)])))
        pl.when(i > 0)(lambda: finish(parts_of(chunks)))

    qspec = pl.BlockSpec((TQ, 2 * HP), lambda b, h, i: (b * nq + i, h))
    kspec = pl.BlockSpec((T, 2 * HP), lambda b, h, i: (b, h))
    return pl.pallas_call(
        body, name="attn_fwd", grid=(nb, MLA_HEADS // 2, nq),
        in_specs=[qspec, kspec, kspec], out_specs=[qspec, qspec],
        out_shape=[jax.ShapeDtypeStruct((R, QW), F32)] * 2,
        compiler_params=_cp(48),
    )(q, k, v)


def attn_bwd(q, k, v, o, lse, do, nb, T):
    R = q.shape[0]
    nq = T // TQ
    chunks = _key_chunks(T)

    def body(q_ref, k_ref, v_ref, o_ref, lse_ref, do_ref, dq_ref, dk_ref, dv_ref):
        i = pl.program_id(2)

        @pl.when(i == 0)
        def _():
            dk_ref[...] = jnp.zeros_like(dk_ref)
            dv_ref[...] = jnp.zeros_like(dv_ref)

        def run(chunks):
            for h in range(2):
                lanes = slice(h * HP, (h + 1) * HP)
                qv = q_ref[:, lanes]
                dov = do_ref[:, lanes]
                dob = dov.astype(MXU)
                delta = jnp.sum(dov * o_ref[:, lanes], axis=-1, keepdims=True)
                lse_v = lse_ref[:, h * HP:h * HP + 1]
                dq = 0.0
                for lo, n in chunks:
                    kv = k_ref[lo:lo + n, lanes]
                    p = jnp.exp2(_dotg(qv, kv, NT) - lse_v)
                    dp = _dotg(dob, v_ref[lo:lo + n, lanes], NT)
                    dsb = (p * (dp - delta)).astype(MXU)
                    dq = dq + _dot(dsb, kv)
                    dk_ref[lo:lo + n, lanes] += _dotg(dsb, qv, TN)
                    dv_ref[lo:lo + n, lanes] += _dotg(p.astype(MXU), dob, TN)
                dq_ref[:, lanes] = dq

        pl.when(i == 0)(lambda: run([(0, CTX)]))
        pl.when(i > 0)(lambda: run(chunks))

    qspec = pl.BlockSpec((TQ, 2 * HP), lambda b, h, i: (b * nq + i, h))
    kspec = pl.BlockSpec((T, 2 * HP), lambda b, h, i: (b, h))
    return pl.pallas_call(
        body, name="attn_bwd", grid=(nb, MLA_HEADS // 2, nq),
        in_specs=[qspec, kspec, kspec, qspec, qspec, qspec],
        out_specs=[qspec, kspec, kspec],
        out_shape=[jax.ShapeDtypeStruct((R, QW), F32)] * 3,
        compiler_params=_cp(56),
    )(q, k, v, o, lse, do)


def _pool_geometry(i, blocks_per_sample, seq):
    j = i % blocks_per_sample
    n = jnp.where(j == 0, CTX, seq)
    t0 = jnp.where(j == 0, 0, (j - 1) * SB) - HALO
    lane = lax.broadcasted_iota(jnp.int32, (SB + 2 * HALO, POOL_DIM), 1)
    t = lax.broadcasted_iota(jnp.int32, (SB + 2 * HALO, POOL_DIM), 0) + t0
    wh = jnp.where(lane < 64, 1, jnp.where(lane < 128, 2, jnp.where(lane < 192, 4, 8)))
    cnt = jnp.minimum(t + wh, n) - jnp.maximum(t - wh, 0)
    return lane, 1.0 / jnp.maximum(cnt, 1).astype(F32)


def _by_window(lane, c2, c4, c8, c16):
    return jnp.where(lane < 64, c2, jnp.where(lane < 128, c4, jnp.where(lane < 192, c8, c16)))


def _window_sums(ext, lane, first):
    n = ext.shape[0]
    r = lambda a, s: pltpu.roll(a, s % n, axis=0)
    c2 = ext + r(ext, first)
    c4 = r(c2, 1) + r(c2, -1)
    c8 = r(c4, 2) + r(c4, -2)
    c16 = r(c8, 4) + r(c8, -4)
    return _by_window(lane, c2, c4, c8, c16)


def _pool_delta(ext, lane, inv):
    return (_window_sums(ext, lane, 1) * inv - ext)[HALO:HALO + SB, :]


def pool_fwd(ppool, wbd, scale, blocks_per_sample, seq):
    R = ppool.shape[0]
    prev, nxt = _halo_specs(POOL_DIM, R)

    def body(cur_ref, prev_ref, nxt_ref, w_ref, s_ref, o_ref):
        i = pl.program_id(0)
        ext = _ext_rows(cur_ref[...], prev_ref[...], nxt_ref[...], i, blocks_per_sample)
        lane, inv = _pool_geometry(i, blocks_per_sample, seq)
        dlt = _pool_delta(ext, lane, inv)
        o_ref[...] = _dot(dlt.astype(MXU), w_ref[...]) * s_ref[...]

    return pl.pallas_call(
        body, name="pool_fwd", grid=(R // SB,),
        in_specs=[_rowspec(POOL_DIM, SB), prev, nxt, _fullspec((POOL_DIM, POOL_DIM)), _fullspec((1, POOL_DIM))],
        out_specs=_rowspec(POOL_DIM, SB),
        out_shape=jax.ShapeDtypeStruct((R, POOL_DIM), F32),
    )(ppool, ppool, ppool, wbd, scale)


def pool_bwd(ppool, dpool, wbd, scale, blocks_per_sample, seq):
    R = ppool.shape[0]
    prev, nxt = _halo_specs(POOL_DIM, R)

    def body(cur_ref, prev_ref, nxt_ref, dcur_ref, dprev_ref, dnxt_ref, w_ref, s_ref, du_ref, dw_ref, part_ref):
        i = pl.program_id(0)

        @pl.when(i == 0)
        def _():
            dw_ref[...] = jnp.zeros_like(dw_ref)

        ext = _ext_rows(cur_ref[...], prev_ref[...], nxt_ref[...], i, blocks_per_sample)
        lane, inv = _pool_geometry(i, blocks_per_sample, seq)
        dlt = _pool_delta(ext, lane, inv).astype(MXU)
        dy = dcur_ref[...]
        part_ref[0] = jnp.concatenate([_colsum(dy * _dot(dlt, w_ref[...])), jnp.zeros((7, POOL_DIM), F32)], axis=0)
        dyp = (dy * s_ref[...]).astype(MXU)
        dw_ref[...] += _dotg(dlt, dyp, TN)
        dext = _ext_rows(dy, dprev_ref[...], dnxt_ref[...], i, blocks_per_sample)
        dd = _dotg((dext * s_ref[...]).astype(MXU), w_ref[...], NT)
        du_ref[...] = (_window_sums(dd * inv, lane, -1) - dd)[HALO:HALO + SB, :].astype(du_ref.dtype)

    return pl.pallas_call(
        body, name="pool_bwd", grid=(R // SB,),
        in_specs=[_rowspec(POOL_DIM, SB), prev, nxt, _rowspec(POOL_DIM, SB), prev, nxt,
                  _fullspec((POOL_DIM, POOL_DIM)), _fullspec((1, POOL_DIM))],
        out_specs=[_rowspec(POOL_DIM, SB), _fullspec((POOL_DIM, POOL_DIM)),
                   pl.BlockSpec((1, 8, POOL_DIM), lambda i: (i, 0, 0))],
        out_shape=[jax.ShapeDtypeStruct((R, POOL_DIM), MXU), jax.ShapeDtypeStruct((POOL_DIM, POOL_DIM), F32),
                   jax.ShapeDtypeStruct((R // SB, 8, POOL_DIM), F32)],
    )(ppool, ppool, ppool, dpool, dpool, dpool, wbd, scale)


def adamw(w, g, m, v, name="adamw"):
    rows, cols = w.shape
    tr = rows
    for cand in (512, 256, 128, 64, 32, 16, 8):
        if rows % cand == 0:
            tr = cand
            break
    bc1 = 1.0 - ADAM_B1 ** ADAM_STEP
    bc2 = 1.0 - ADAM_B2 ** ADAM_STEP

    def body(w_ref, g_ref, m_ref, v_ref, d_ref, nm_ref, nv_ref):
        g_v = g_ref[...]
        nm = ADAM_B1 * m_ref[...] + (1.0 - ADAM_B1) * g_v
        nv = ADAM_B2 * v_ref[...] + (1.0 - ADAM_B2) * (g_v * g_v)
        nm_ref[...] = nm
        nv_ref[...] = nv
        d_ref[...] = -ADAM_LR * ((nm / bc1) / (jnp.sqrt(nv / bc2) + ADAM_EPS) + ADAM_WD * w_ref[...])

    spec = pl.BlockSpec((tr, cols), lambda i: (i, 0))
    return pl.pallas_call(
        body, name=name, grid=(rows // tr,),
        in_specs=[spec] * 4, out_specs=[spec] * 3,
        out_shape=[jax.ShapeDtypeStruct((rows, cols), F32)] * 3,
    )(w, g, m, v)


MODR = 32


def _silu(v):
    return v * _sigmoid(v)


def mod_fwd(cond, w, b):
    n = w.shape[1]

    def body(c_ref, w_ref, b_ref, o_ref):
        o_ref[...] = _dot(_silu(c_ref[...]).astype(MXU), w_ref[...].astype(MXU)) + b_ref[...]

    return pl.pallas_call(
        body, name="mod_fwd", out_shape=jax.ShapeDtypeStruct((MODR, n), F32),
        in_specs=[_fullspec((MODR, D)), _fullspec((D, n)), _fullspec((1, n))], out_specs=_fullspec((MODR, n)),
        grid=(1,), compiler_params=_cp(40),
    )(cond, w, b)


def mod_wgrad(cond, dm):
    n = dm.shape[1]

    def body(c_ref, d_ref, o_ref):
        o_ref[...] = _dotg(_silu(c_ref[...]).astype(MXU), d_ref[...].astype(MXU), TN)

    return pl.pallas_call(
        body, name="mod_wgrad", out_shape=jax.ShapeDtypeStruct((D, n), F32),
        in_specs=[_fullspec((MODR, D)), _fullspec((MODR, n))], out_specs=_fullspec((D, n)),
        grid=(1,), compiler_params=_cp(40),
    )(cond, dm)


def mod_dgrad(dm, w):
    n = w.shape[1]

    def body(d_ref, w_ref, o_ref):
        o_ref[...] = _dotg(d_ref[...].astype(MXU), w_ref[...].astype(MXU), NT)

    return pl.pallas_call(
        body, name="mod_dgrad", out_shape=jax.ShapeDtypeStruct((8, D), F32),
        in_specs=[_fullspec((8, n)), _fullspec((D, n))], out_specs=_fullspec((8, D)),
        grid=(1,), compiler_params=_cp(40),
    )(dm, w)


def sum_leading(a, name="sum_leading"):
    n, r, c = a.shape

    def body(a_ref, o_ref):
        acc = a_ref[0]
        for k in range(1, n):
            acc = acc + a_ref[k]
        o_ref[...] = acc

    return pl.pallas_call(
        body, name=name, out_shape=jax.ShapeDtypeStruct((r, c), F32),
        in_specs=[_fullspec((n, r, c))], out_specs=_fullspec((r, c)), grid=(1,),
    )(a)


MESH = pl.DeviceIdType.MESH
NDEV = 8
ANY = pl.BlockSpec(memory_space=pl.ANY)


def _place():
    return lax.axis_index("x"), lax.axis_index("y"), lax.axis_index("c")


def _other_chips(x, y):
    return [(1 - x, y), (x, 1 - y), (1 - x, 1 - y)]


def allgather_small(v, name):
    r, cols = v.shape

    def body(v_ref, o_ref, send_sems, recv_sems):
        x, y, c = _place()
        me = 4 * x + 2 * y + c
        o_ref[me] = v_ref[...]
        copies = []
        for rel in range(1, NDEV):
            peer = (1 - x if rel & 4 else x, 1 - y if rel & 2 else y, 1 - c if rel & 1 else c)
            cp = pltpu.make_async_remote_copy(src_ref=v_ref, dst_ref=o_ref.at[me], send_sem=send_sems.at[rel - 1],
                                              recv_sem=recv_sems.at[rel - 1], device_id=peer, device_id_type=MESH)
            cp.start()
            copies.append(cp)
        for cp in copies:
            cp.wait_recv()
        for cp in copies:
            cp.wait_send()

    return pl.pallas_call(
        body, name=name, out_shape=jax.ShapeDtypeStruct((NDEV, r, cols), F32),
        in_specs=[pl.BlockSpec(memory_space=pltpu.VMEM)], out_specs=pl.BlockSpec(memory_space=pltpu.VMEM),
        scratch_shapes=[pltpu.SemaphoreType.DMA((NDEV - 1,)), pltpu.SemaphoreType.DMA((NDEV - 1,))],
        compiler_params=_cp(40),
    )(v)


def _sems(n):
    return [pltpu.SemaphoreType.DMA((n,)), pltpu.SemaphoreType.DMA((n,))]


def allgather_chips(v, name):
    r, cols = v.shape

    def body(v_ref, o_ref, send_sems, recv_sems):
        x, y, c = _place()
        k = 2 * x + y
        o_ref[k] = v_ref[...]
        copies = []
        for j, (px, py) in enumerate(_other_chips(x, y)):
            cp = pltpu.make_async_remote_copy(src_ref=v_ref, dst_ref=o_ref.at[k], send_sem=send_sems.at[j],
                                              recv_sem=recv_sems.at[j], device_id=(px, py, c), device_id_type=MESH)
            cp.start()
            copies.append(cp)
        for cp in copies:
            cp.wait_recv()
        for cp in copies:
            cp.wait_send()

    return pl.pallas_call(
        body, name=name, out_shape=jax.ShapeDtypeStruct((4, r, cols), F32),
        in_specs=[pl.BlockSpec(memory_space=pltpu.VMEM)], out_specs=pl.BlockSpec(memory_space=pltpu.VMEM),
        scratch_shapes=_sems(3), compiler_params=_cp(40),
    )(v)


def gather_job(arrs):
    n = len(arrs)

    def copy(srcs, outs, sems, i, slot, kk, cc, to, from_src=False):
        hr = arrs[i].shape[0] // 2
        dst = outs[i].at[kk, pl.ds(cc * hr, hr), :]
        return pltpu.make_async_remote_copy(src_ref=srcs[i].at[pl.ds(cc * hr, hr), :] if from_src else dst, dst_ref=dst,
                                            send_sem=sems[0].at[slot * n + i], recv_sem=sems[1].at[slot * n + i],
                                            device_id=to, device_id_type=MESH)

    def start(srcs, outs, sems):
        x, y, c = _place()
        for j, (px, py) in enumerate(_other_chips(x, y)):
            for i in range(n):
                copy(srcs, outs, sems, i, j, 2 * x + y, c, (px, py, c), True).start()

    def finish(srcs, outs, sems):
        x, y, c = _place()
        sib = (x, y, 1 - c)
        chips = _other_chips(x, y)
        passed = []
        for j, (px, py) in enumerate(chips):
            for i in range(n):
                copy(srcs, outs, sems, i, j, 2 * px + py, c, (px, py, c)).wait_recv()
                cp = copy(srcs, outs, sems, i, 3 + j, 2 * px + py, c, sib)
                cp.start()
                passed.append(cp)
        for j, (px, py) in enumerate(chips):
            for i in range(n):
                copy(srcs, outs, sems, i, 3 + j, 2 * px + py, 1 - c, sib).wait_recv()
        for j, (px, py) in enumerate(chips):
            for i in range(n):
                copy(srcs, outs, sems, i, j, 2 * x + y, c, (px, py, c), True).wait_send()
        for cp in passed:
            cp.wait_send()

    return _NS(ins=list(arrs), out_shapes=[jax.ShapeDtypeStruct((4,) + a.shape, a.dtype) for a in arrs], nsem=6 * n,
               start=start, finish=finish)


def chip_swap_job(ss):
    n = len(ss)

    def copies(srcs, outs, sems):
        x, y, c = _place()
        return [pltpu.make_async_remote_copy(src_ref=srcs[i].at[2 * px + py], dst_ref=outs[i].at[j],
                                             send_sem=sems[0].at[j * n + i], recv_sem=sems[1].at[j * n + i],
                                             device_id=(px, py, c), device_id_type=MESH)
                for j, (px, py) in enumerate(_other_chips(x, y)) for i in range(n)]

    def start(srcs, outs, sems):
        for cp in copies(srcs, outs, sems):
            cp.start()

    def finish(srcs, outs, sems):
        for cp in copies(srcs, outs, sems):
            cp.wait()

    return _NS(ins=list(ss), out_shapes=[jax.ShapeDtypeStruct((3,) + s.shape[1:], s.dtype) for s in ss], nsem=3 * n,
               start=start, finish=finish)


def run_job(job, name):
    n, m = len(job.ins), len(job.out_shapes)

    def body(*refs):
        srcs, outs, sems = refs[:n], refs[n:n + m], refs[n + m:]
        job.start(srcs, outs, sems)
        job.finish(srcs, outs, sems)

    return pl.pallas_call(body, name=name, out_shape=job.out_shapes, in_specs=[ANY] * n, out_specs=[ANY] * m,
                          scratch_shapes=_sems(job.nsem))(*job.ins)


def core_swap_job(gs):
    n = len(gs)

    def copies(srcs, outs, sems):
        x, y, c = _place()
        return [pltpu.make_async_remote_copy(src_ref=srcs[i].at[:, pl.ds((1 - c) * (gs[i].shape[1] // 2), gs[i].shape[1] // 2), :],
                                             dst_ref=outs[i], send_sem=sems[0].at[i], recv_sem=sems[1].at[i],
                                             device_id=(x, y, 1 - c), device_id_type=MESH) for i in range(n)]

    def start(srcs, outs, sems):
        for cp in copies(srcs, outs, sems):
            cp.start()

    def finish(srcs, outs, sems):
        for cp in copies(srcs, outs, sems):
            cp.wait()

    return _NS(ins=list(gs), out_shapes=[jax.ShapeDtypeStruct((4, g.shape[1] // 2, g.shape[2]), g.dtype) for g in gs],
               nsem=n, start=start, finish=finish)


def add_half(g, r1, cidx, name):
    _, rows, cols = g.shape
    hr = rows // 2

    def body(c_ref, g_ref, r_ref, o_ref, ob_ref):
        s = g_ref[...] + r_ref[...]
        o_ref[...] = s
        ob_ref[...] = s.astype(BF16)

    blk = lambda f: pl.BlockSpec((1, hr, cols), f)
    return pl.pallas_call(
        body, name=name,
        out_shape=[jax.ShapeDtypeStruct((4, hr, cols), F32), jax.ShapeDtypeStruct((4, hr, cols), BF16)],
        grid_spec=pltpu.PrefetchScalarGridSpec(
            num_scalar_prefetch=1, grid=(4,),
            in_specs=[blk(lambda k, c_ref: (k, c_ref[0], 0)), blk(lambda k, c_ref: (k, 0, 0))],
            out_specs=[blk(lambda k, c_ref: (k, 0, 0)), blk(lambda k, c_ref: (k, 0, 0))]),
    )(cidx, g, r1)


def sum_parts(s1, r2, kidx, name):
    _, hr, cols = s1.shape

    def body(k_ref, s_ref, r_ref, o_ref):
        o_ref[...] = ((s_ref[0] + r_ref[0].astype(F32)) + r_ref[1].astype(F32)) + r_ref[2].astype(F32)

    return pl.pallas_call(
        body, name=name, out_shape=jax.ShapeDtypeStruct((hr, cols), F32),
        grid_spec=pltpu.PrefetchScalarGridSpec(
            num_scalar_prefetch=1, grid=(1,),
            in_specs=[pl.BlockSpec((1, hr, cols), lambda i, k_ref: (k_ref[0], 0, 0)),
                      pl.BlockSpec((3, hr, cols), lambda i, k_ref: (0, 0, 0))],
            out_specs=pl.BlockSpec((hr, cols), lambda i, k_ref: (0, 0))),
    )(kidx, s1, r2)


def swap_reduced_halves(hs):
    n = len(hs)

    def body(*refs):
        srcs, outs = refs[:n], refs[n:2 * n]
        send_sems, recv_sems = refs[2 * n:]
        x, y, c = _place()
        copies = []
        for i in range(n):
            cp = pltpu.make_async_remote_copy(src_ref=srcs[i], dst_ref=outs[i], send_sem=send_sems.at[i],
                                              recv_sem=recv_sems.at[i], device_id=(x, y, 1 - c), device_id_type=MESH)
            cp.start()
            copies.append(cp)
        for cp in copies:
            cp.wait()

    return pl.pallas_call(
        body, name="swap_reduced_halves", out_shape=[jax.ShapeDtypeStruct(h.shape, h.dtype) for h in hs],
        in_specs=[ANY] * n, out_specs=[ANY] * n, scratch_shapes=_sems(n),
    )(*hs)


def adamw_halves(w, m, v, own, oth, cidx, name):
    depth, rows, cols = w.shape
    hr = rows // 2
    tr = min(hr, 256)
    nblk = hr // tr
    bc1 = 1.0 - ADAM_B1 ** ADAM_STEP
    bc2 = 1.0 - ADAM_B2 ** ADAM_STEP

    def body(c_ref, w_ref, m_ref, v_ref, own0, own1, oth0, oth1, g_ref, d_ref, nm_ref, nv_ref):
        l = pl.program_id(0)
        hi = pl.program_id(1)
        mine = jnp.where(l == 0, own0[...], own1[...])
        other = jnp.where(l == 0, oth0[...], oth1[...])
        g_v = jnp.where(hi == c_ref[0], mine, other)
        nm = ADAM_B1 * m_ref[0] + (1.0 - ADAM_B1) * g_v
        nv = ADAM_B2 * v_ref[0] + (1.0 - ADAM_B2) * (g_v * g_v)
        g_ref[0] = g_v
        nm_ref[0] = nm
        nv_ref[0] = nv
        d_ref[0] = -ADAM_LR * ((nm / bc1) / (jnp.sqrt(nv / bc2) + ADAM_EPS) + ADAM_WD * w_ref[0])

    wspec = pl.BlockSpec((1, tr, cols), lambda l, hi, b, c_ref: (l, hi * nblk + b, 0))
    gspec = pl.BlockSpec((tr, cols), lambda l, hi, b, c_ref: (b, 0))
    assert depth == 2
    return pl.pallas_call(
        body, name=name, out_shape=[jax.ShapeDtypeStruct(w.shape, F32)] * 4,
        grid_spec=pltpu.PrefetchScalarGridSpec(
            num_scalar_prefetch=1, grid=(depth, 2, nblk),
            in_specs=[wspec] * 3 + [gspec] * 4, out_specs=[wspec] * 4),
    )(cidx, w, m, v, own[0], own[1], oth[0], oth[1])


class _NS:
    def __init__(self, **kw):
        self.__dict__.update(kw)


def _prep_in(win, conv_w, conv_b, dt_bias, a_log, ssd_d, ssd_nw, qnw, kvnw, pool_w, pool_scale, n1, n2):
    winp = jnp.concatenate([win[:, 0:384], win[:, 384:1280], win[:, 1292:1548], win[:, 1548:1804], win[:, 1836:2092],
                            win[:, 1804:1836], win[:, 1280:1292], jnp.zeros((D, NP - IN_COLS), win.dtype)], axis=1)
    wbd = (jnp.eye(4, dtype=F32)[:, None, :, None] * pool_w[:, :, None, :]).reshape(POOL_DIM, POOL_DIM).astype(MXU)
    a = -jnp.exp(a_log)
    return _NS(
        winp=winp, wbd=wbd,
        cw8=jnp.pad(conv_w, ((0, 4), (0, 0))), cb=conv_b[None],
        dtb=jnp.pad(dt_bias.reshape(1, 12), ((0, 0), (DT0, 128 - DT0 - 12))),
        arow=jnp.pad(a[:, None, :], ((0, 0), (0, 7), (0, 128 - SSD_HEADS))), a=a,
        dexp=jnp.repeat(ssd_d, SSD_P)[None], ssd_nw=ssd_nw[None], qnw=qnw[None], kvnw=kvnw[None],
        pscale=pool_scale[None], n1=n1[None], n2=n2[None])


def _prep_rest(wqb, wkvb, wout, w1, w2):
    wq = jnp.pad(wqb.reshape(256, MLA_HEADS, QK_DIM), ((0, 0), (0, 0), (0, HP - QK_DIM))).reshape(256, QW)
    kv3 = wkvb.reshape(256, MLA_HEADS, 128)
    wk = jnp.pad(kv3[:, :, :64], ((0, 0), (0, 0), (0, 64))).reshape(256, QW)
    wv = jnp.pad(kv3[:, :, 64:], ((0, 0), (0, 0), (0, 64))).reshape(256, QW)
    wo = jnp.concatenate([jnp.pad(wout[384:768].reshape(MLA_HEADS, 64, D), ((0, 0), (0, 64), (0, 0))).reshape(QW, D),
                          wout[0:384], wout[768:1024]], axis=0)
    return _NS(wq=wq, wk=wk, wv=wv, wo=wo, w1=w1, w2=w2)


def _prep_layer(win, wqb, wkvb, wout, w1, w2, *small):
    lw = _prep_in(win, *small)
    lw.__dict__.update(_prep_rest(wqb, wkvb, wout, w1, w2).__dict__)
    return lw


def _by_chip_cols(a):
    return jnp.stack([a[:, k * (a.shape[1] // 4):(k + 1) * (a.shape[1] // 4)] for k in range(4)])


def _by_chip_rows(a):
    return a.reshape(4, a.shape[0] // 4, a.shape[1])


def _unprep_in(dwinp):
    return jnp.concatenate([dwinp[:, 0:384], dwinp[:, 384:1280], dwinp[:, 2080:2092], dwinp[:, 1280:1536],
                            dwinp[:, 1536:1792], dwinp[:, 2048:2080], dwinp[:, 1792:2048]], axis=1)


def _unprep_rest(dwq, dwk, dwv, dwo):
    dwqb = dwq.reshape(256, MLA_HEADS, HP)[:, :, :QK_DIM].reshape(256, MLA_HEADS * QK_DIM)
    dwkvb = jnp.concatenate([dwk.reshape(256, MLA_HEADS, HP)[:, :, :64], dwv.reshape(256, MLA_HEADS, HP)[:, :, :64]],
                            axis=2).reshape(256, MLA_HEADS * 128)
    dwout = jnp.concatenate([dwo[QW:QW + 384], dwo[0:QW].reshape(MLA_HEADS, HP, D)[:, :64].reshape(384, D),
                             dwo[QW + 384:CAT]], axis=0)
    return dwqb, dwkvb, dwout


def _rope_tables(nb, N):
    t = jnp.arange(N, dtype=F32)
    row = jnp.floor(t / GRID_W)
    col = t - row * GRID_W
    inv = jnp.asarray(10000.0 ** (-np.arange(8, dtype=np.float32) / 8), F32)
    ang = jnp.stack([row[:, None] * inv, col[:, None] * inv], axis=1)
    cs, sn = jnp.cos(ang), jnp.sin(ang)
    zero = jnp.zeros_like(sn)
    lanes = lambda first, second: jnp.stack([first, second], axis=2).reshape(N, 32)
    pad = lambda a, fill: jnp.concatenate([jnp.full((N, 64), fill, F32), a, jnp.full((N, 32), fill, F32)], axis=1)
    tabs = []
    for tab, fill in ((pad(lanes(cs, cs), 1.0), 1.0), (pad(lanes(-sn, zero), 0.0), 0.0), (pad(lanes(zero, sn), 0.0), 0.0)):
        one = jnp.concatenate([jnp.full((CTX, 128), fill, F32), tab], axis=0)
        tabs.append(jnp.tile(one, (nb, 1)))
    return tabs


def _eexp():
    e = np.zeros((128, SSD_INNER), np.float32)
    for h in range(SSD_HEADS):
        e[h, h * SSD_P:(h + 1) * SSD_P] = 1.0
    return jnp.asarray(e)


class _NoHooks:
    def __init__(self, lws):
        self.lws = lws

    def weights_in(self, l):
        return _NS(**self.lws[l].__dict__)

    def weights_rest(self, l, scan_out):
        return self.lws[l]

    def job(self, where, l, early=None):
        return None

    def done(self, where, l, out):
        pass

    def layer_grads(self, l, g):
        pass


def _layer_fwd(X, bm, l, cst, hooks):
    nb, T, bps, N = cst.nb, cst.T, cst.bps, cst.N
    lw = hooks.weights_in(l)
    h1, pz, pxbc, pqa, pkva, ppool, plast = in_proj(X, bm, lw.n1, lw.winp)
    xs, bmat, cmat, dtv = ssd_prep(pxbc, plast, lw.cw8, lw.cb, lw.dtb, bps)
    y2, hin, out = ssd_scan_fwd(xs, bmat, cmat, dtv, lw.arow, cst.eexp, nb, T, hooks.job("fwd_scan", l))
    lw.__dict__.update(hooks.weights_rest(l, out).__dict__)
    ssd = ssd_out_fwd(y2, xs, pz, lw.dexp, lw.ssd_nw)
    q, k, v, cq, ckv = mla_prep(pqa, pkva, plast, lw.qnw, lw.kvnw, lw.wq, lw.wk, lw.wv, *cst.rope)
    attn, lse = attn_fwd(q, k, v, nb, T)
    pool = pool_fwd(ppool, lw.wbd, lw.pscale, bps, N)
    x1, mix, cat = mix_fwd(X, attn, ssd, pool, bm, lw.wo)
    x2, mo, r, h2, out = mlp_fwd(x1, bm, lw.n2, lw.w1, lw.w2, hooks.job("fwd_mlp", l))
    hooks.done("fwd_mlp", l, out)
    sv = _NS(X=X, h1=h1, pz=pz, pxbc=pxbc, pqa=pqa, pkva=pkva, ppool=ppool, plast=plast, xs=xs, bmat=bmat, cmat=cmat,
             dtv=dtv, y2=y2, hin=hin, q=q, k=k, v=v, cq=cq, ckv=ckv, attn=attn, lse=lse, x1=x1, mix=mix, cat=cat, mo=mo, r=r,
             h2=h2, lw=lw)
    return x2, sv


def _layer_bwd(dx2, bm, l, sv, cst, hooks):
    nb, T, bps, N = cst.nb, cst.T, cst.bps, cst.N
    lw = sv.lw
    dx1, du, dob, part_mlp, out = mlp_bwd(dx2, sv.x1, sv.mo, sv.r, bm, lw.n2, lw.w2, lw.w1, hooks.job("bwd_mlp", l))
    hooks.done("bwd_mlp", l, out)
    dw1 = mm_tn(sv.h2, du, name="wgrad_mlp1", col_blocks=True)
    dw2 = mm_tn(sv.r, dob, square_a=True, name="wgrad_mlp2")
    dattn, dssd, dpool, dwo, part_mix = mix_bwd(dx1, sv.mix, sv.cat, bm, lw.wo)
    dppool, dwbd, part_pool = pool_bwd(sv.ppool, dpool, lw.wbd, lw.pscale, bps, N)
    dq, dk, dv = attn_bwd(sv.q, sv.k, sv.v, sv.attn, sv.lse, dattn, nb, T)
    dpqa, dpkva, dkr, dwq, dwk, dwv, part_mla = mla_prep_bwd(dq, dk, dv, sv.pqa, sv.pkva, sv.cq, sv.ckv, lw.qnw, lw.kvnw,
                                                             lw.wq, lw.wk, lw.wv, *cst.rope)
    dwqb, dwkvb, dwout = _unprep_rest(dwq, dwk, dwv, dwo)
    early = dict(w_q_b=_by_chip_cols(dwqb), w_kv_b=_by_chip_cols(dwkvb), w_out=_by_chip_rows(dwout), w_mlp1=dw1,
                 w_mlp2=_by_chip_rows(dw2))
    dyy, dz, dxs_skip, part_so = ssd_out_bwd(dssd, sv.y2, sv.xs, sv.pz, lw.dexp, lw.ssd_nw)
    dxs2, dbm2, dcm2, ddt2, da, out = ssd_scan_bwd(sv.xs, sv.bmat, sv.cmat, sv.dtv, lw.arow, cst.eexp, sv.hin, dyy,
                                                   nb, T, hooks.job("bwd_scan", l, early))
    hooks.done("bwd_scan", l, out)
    dpre, dlast_dt, part_conv = ssd_prep_bwd_a(sv.pxbc, sv.plast, lw.cw8, lw.cb, lw.dtb, dxs_skip, dxs2, dbm2, dcm2,
                                               ddt2, bps)
    dpxbc = ssd_prep_bwd_b(dpre, lw.cw8, bps)
    dx, dwinp, part_in = in_proj_bwd(dx1, sv.X, sv.h1, dz, dpxbc, dpqa, dpkva, dppool, dkr, dlast_dt, bm, lw.n1, lw.winp)

    dmod = jnp.stack([part_in[:, 0], part_in[:, 1], part_mix[:, 0], part_mlp[:, 0], part_mlp[:, 1], part_mlp[:, 2]],
                     axis=1)
    dmod = dmod.reshape(nb, bps, 6, D)
    dm_rows = jnp.concatenate([jnp.sum(dmod[:, 1:], axis=1), jnp.sum(dmod[:, 0], axis=0)[None]], axis=0)
    da_dh = jnp.sum(da[:, :, 0, :SSD_HEADS], axis=1)
    conv_parts = jnp.sum(part_conv, axis=0)
    g = _NS(
        w_in=_by_chip_cols(_unprep_in(dwinp)), dm_rows=dm_rows.reshape(3, 6 * D), **early,
        norm1_w=jnp.sum(part_in[:, 2], axis=0), norm2_w=jnp.sum(part_mlp[:, 3], axis=0),
        conv_w=conv_parts[0:4], conv_b=conv_parts[4],
        dt_bias=conv_parts[5, DT0:DT0 + 12].reshape(2, SSD_HEADS), a_log=da_dh * lw.a,
        ssd_d=jnp.sum(jnp.sum(part_so[:, 1], axis=0).reshape(SSD_HEADS, SSD_P), axis=1),
        ssd_norm_w=jnp.sum(part_so[:, 0], axis=0),
        q_a_norm_w=jnp.sum(part_mla[:, 0], axis=0), kv_a_norm_w=jnp.sum(part_mla[:, 1], axis=0),
        pool_w=jnp.stack([dwbd[i * 64:(i + 1) * 64, i * 64:(i + 1) * 64] for i in range(4)]),
        pool_scale=jnp.sum(part_pool[:, 0], axis=0))
    hooks.layer_grads(l, g)
    return dx, g


def _local_step(x, ctx, tgt, bms, lws, fw, cst, hooks=None):
    nb, N = x.shape[0], x.shape[1]
    R = nb * cst.T
    hooks = _NoHooks(lws) if hooks is None else hooks
    X = jnp.concatenate([ctx, x], axis=1).reshape(R, D)
    saved = []
    for l in range(DEPTH):
        X, sv = _layer_fwd(X, bms[l], l, cst, hooks)
        saved.append(sv)
    dX, part_fin = final_loss(X, tgt.reshape(nb * N, D), fw[None], cst.bps)
    loss = (0.5 / D) * jnp.sum(part_fin[:, 1])
    dfw = jnp.sum(part_fin[:, 0], axis=0)
    grads = [None] * DEPTH
    for l in reversed(range(DEPTH)):
        dX, grads[l] = _layer_bwd(dX, bms[l], l, saved[l], cst, hooks)
    grad_x = dX.reshape(nb, cst.T, D)[:, CTX:, :]
    return loss, grad_x, grads, dfw


def _consts(nb, N):
    T = CTX + N
    bps = T // SB
    return _NS(nb=nb, N=N, T=T, bps=bps, eexp=_eexp(), rope=_rope_tables(nb, N))


def _block_mod(modrows, cst):
    rows = []
    for b in range(cst.nb):
        rows.append(modrows[cst.nb:cst.nb + 1])
        rows.append(jnp.broadcast_to(modrows[b:b + 1], (cst.bps - 1, 6, D)))
    return jnp.pad(jnp.concatenate(rows, axis=0), ((0, 0), (0, 2), (0, 0)))


SMALL = (("norm1_w", (2, D)), ("norm2_w", (2, D)), ("conv_w", (2, 4, XBC)), ("conv_b", (2, XBC)),
         ("dt_bias", (2, 2, 6)), ("a_log", (2, 2, 6)), ("ssd_d", (2, 6)), ("ssd_norm_w", (2, 384)),
         ("q_a_norm_w", (2, 256)), ("kv_a_norm_w", (2, 256)), ("pool_w", (2, 4, 64, 64)), ("pool_scale", (2, 256)),
         ("final_norm_w", (D,)), ("mod_b", (2, 6 * D)))
SMALL_ROWS = 64
DM_ROWS = 48


def _pack_small(vals):
    flat = jnp.concatenate([vals[n].reshape(-1) for n, _ in SMALL])
    return jnp.pad(flat, (0, SMALL_ROWS * D - flat.shape[0])).reshape(SMALL_ROWS, D)


def _unpack_small(p):
    flat = p.reshape(-1)
    out, off = {}, 0
    for n, shp in SMALL:
        size = int(np.prod(shp))
        out[n] = flat[off:off + size].reshape(shp)
        off += size
    return out


def cctx_grad(parts, c_ctx):
    def body(p_ref, c_ref, o_ref):
        acc = ((p_ref[0] + p_ref[1]) + p_ref[2]) + p_ref[3]
        v = c_ref[...]
        sig = _sigmoid(v)
        o_ref[...] = acc * (sig * (1.0 + v * (1.0 - sig)))

    return pl.pallas_call(
        body, name="cctx_grad", out_shape=jax.ShapeDtypeStruct((8, D), F32),
        in_specs=[_fullspec((4, 8, D)), _fullspec((1, D))], out_specs=_fullspec((8, D)), grid=(1,),
    )(parts, c_ctx)


def kernel(x, c, ctx, c_ctx, mod_w, mod_b, norm1_w, norm2_w, w_in, conv_w, conv_b, dt_bias, a_log, ssd_d, ssd_norm_w, q_a_norm_w, w_q_b, kv_a_norm_w, w_kv_b, pool_w, pool_scale, w_out, w_mlp1, w_mlp2, final_norm_w, loss_target, m_c_ctx, m_mod_w, m_mod_b, m_norm1_w, m_norm2_w, m_w_in, m_conv_w, m_conv_b, m_dt_bias, m_a_log, m_ssd_d, m_ssd_norm_w, m_q_a_norm_w, m_w_q_b, m_kv_a_norm_w, m_w_kv_b, m_pool_w, m_pool_scale, m_w_out, m_w_mlp1, m_w_mlp2, m_final_norm_w, v_c_ctx, v_mod_w, v_mod_b, v_norm1_w, v_norm2_w, v_w_in, v_conv_w, v_conv_b, v_dt_bias, v_a_log, v_ssd_d, v_ssd_norm_w, v_q_a_norm_w, v_w_q_b, v_kv_a_norm_w, v_w_kv_b, v_pool_w, v_pool_scale, v_w_out, v_w_mlp1, v_w_mlp2, v_final_norm_w):
    nb, N = x.shape[0], x.shape[1]
    cst = _consts(nb, N)
    xi, yi, ci = _place()
    me = 4 * xi + 2 * yi + ci
    kchip = 2 * xi + yi
    mcols = mod_w.shape[2]
    cshard = conv_w.shape[2]

    blk = jnp.zeros((16, D), F32).at[0:nb].set(c).at[8:16, 0:cshard].set(conv_w.reshape(8, cshard))
    g1 = allgather_small(blk, "gather_cond")
    cond = jnp.concatenate([g1[:, 0:nb].reshape(NDEV * nb, D), c_ctx[None],
                            jnp.zeros((MODR - NDEV * nb - 1, D), F32)], axis=0)
    conv_full = [jnp.concatenate([g1[2 * k, 8 + 4 * l:12 + 4 * l, 0:cshard] for k in range(4)], axis=1)
                 for l in range(DEPTH)]

    mb = [lax.dynamic_slice_in_dim(mod_b[l], kchip * mcols, mcols)[None] for l in range(DEPTH)]
    ms = jnp.concatenate([mod_fwd(cond, mod_w[l], mb[l]) for l in range(DEPTH)], axis=0)
    g2 = allgather_chips(ms, "gather_mod")
    bms = []
    for l in range(DEPTH):
        m_all = jnp.concatenate([g2[k, MODR * l:MODR * (l + 1)] for k in range(4)], axis=1)
        mine = jnp.concatenate([lax.dynamic_slice_in_dim(m_all, nb * me, nb), m_all[NDEV * nb:NDEV * nb + 1]], axis=0)
        bms.append(_block_mod(mine.reshape(nb + 1, 6, D), cst))

    assert DEPTH == 2
    big = (w_in, w_q_b, w_kv_b, w_out, w_mlp1, w_mlp2)
    names = ("w_in", "w_q_b", "w_kv_b", "w_out", "w_mlp1", "w_mlp2")
    concat_axis = dict(w_in=1, w_q_b=1, w_kv_b=1, w_out=0, w_mlp1=1, w_mlp2=0)
    cidx = jnp.reshape(ci, (1,)).astype(jnp.int32)
    kidx = jnp.reshape(kchip, (1,)).astype(jnp.int32)
    shards = [{n: a[l].astype(MXU) for n, a in zip(names, big)} for l in range(DEPTH)]

    def core_sums(gs, got=None):
        ns = list(gs)
        got = run_job(core_swap_job([gs[n] for n in ns]), "swap_core_halves") if got is None else got
        return {n: add_half(gs[n], r, cidx, "add_half_" + n) for n, r in zip(ns, got)}

    class Hooks:
        gathered = [dict(w_in=run_job(gather_job([shards[0]["w_in"]]), "gather_w_in")[0]), {}]
        core_sum = [{}, {}]
        received = [{}, {}]

        def whole(self, l, n):
            return jnp.concatenate([jnp.where(kchip == k, shards[l][n], self.gathered[l][n][k]) for k in range(4)],
                                   axis=concat_axis[n])

        def weights_in(self, l):
            return _prep_in(self.whole(l, "w_in"), conv_full[l], conv_b[l], dt_bias[l], a_log[l], ssd_d[l], ssd_norm_w[l],
                            q_a_norm_w[l], kv_a_norm_w[l], pool_w[l], pool_scale[l], norm1_w[l], norm2_w[l])

        def weights_rest(self, l, scan_out):
            if l == 0:
                self.gathered[0].update(zip(names[1:], scan_out))
            return _prep_rest(*[self.whole(l, n) for n in names[1:]])

        def job(self, where, l, early=None):
            if l == 1 and where == "bwd_scan":
                self.early1 = early
                return core_swap_job([early[n] for n in names[1:]])
            if l != 0:
                return None
            if where == "fwd_scan":
                return gather_job([shards[0][n] for n in names[1:]])
            if where == "fwd_mlp":
                return gather_job([shards[1][n] for n in names])
            if where == "bwd_mlp":
                return chip_swap_job([self.core_sum[1][n][1] for n in names])
            self.core_sum[0].update(core_sums(early))
            return chip_swap_job([self.core_sum[0][n][1] for n in names[1:]])

        def done(self, where, l, out):
            if l == 1 and where == "bwd_scan":
                self.core_sum[1].update(core_sums(self.early1, out))
            if l != 0:
                return
            if where == "fwd_mlp":
                self.gathered[1].update(zip(names, out))
            elif where == "bwd_mlp":
                self.received[1].update(zip(names, out))
            elif where == "bwd_scan":
                self.received[0].update(zip(names[1:], out))

        def layer_grads(self, l, g):
            if l == 1:
                self.core_sum[1].update(core_sums(dict(w_in=g.w_in)))
            else:
                self.core_sum[0].update(core_sums(dict(w_in=g.w_in)))
                self.received[0]["w_in"] = run_job(chip_swap_job([self.core_sum[0]["w_in"][1]]), "swap_w_in")[0]

    hooks = Hooks()
    loss_part, grad_x, grads, dfw = _local_step(x, ctx, loss_target, bms, None, final_norm_w, cst, hooks)
    loss = lax.psum(loss_part, ("x", "y", "c"))
    g_own = [sum_parts(hooks.core_sum[l][n][0], hooks.received[l][n], kidx, "sum_parts_" + n)
             for n in names for l in range(DEPTH)]
    g_oth = swap_reduced_halves(g_own)

    small = {n: jnp.stack([getattr(grads[l], n) for l in range(DEPTH)]) for n, _ in SMALL if n not in ("final_norm_w", "mod_b")}
    small["final_norm_w"] = dfw
    small["mod_b"] = jnp.stack([jnp.sum(grads[l].dm_rows, axis=0) for l in range(DEPTH)])
    dm = jnp.pad(jnp.concatenate([grads[l].dm_rows for l in range(DEPTH)], axis=0), ((0, 8 - 3 * DEPTH), (0, 0)))
    g3 = allgather_small(jnp.concatenate([_pack_small(small), dm.reshape(DM_ROWS, D)], axis=0), "gather_small")
    tot = sum_leading(g3, "sum_small")
    gsmall = _unpack_small(tot[0:SMALL_ROWS])
    ctx_sum = tot[SMALL_ROWS:].reshape(8, 6 * D)
    dm_dev = g3[:, SMALL_ROWS:].reshape(NDEV, 8, 6 * D)
    g_mod_w, dpart = [], jnp.zeros((8, D), F32)
    for l in range(DEPTH):
        dm_all = jnp.concatenate([dm_dev[:, 3 * l:3 * l + nb].reshape(NDEV * nb, 6 * D), ctx_sum[3 * l + nb:3 * l + nb + 1],
                                  jnp.zeros((MODR - NDEV * nb - 1, 6 * D), F32)], axis=0)
        g_mod_w.append(mod_wgrad(cond, lax.dynamic_slice_in_dim(dm_all, kchip * mcols, mcols, axis=1)))
        dctx = jnp.pad(lax.dynamic_slice_in_dim(ctx_sum[3 * l + nb:3 * l + nb + 1], kchip * mcols, mcols, axis=1), ((0, 7), (0, 0)))
        dpart = dpart + mod_dgrad(dctx, mod_w[l])
    g_c_ctx = cctx_grad(allgather_chips(dpart, "gather_cctx"), c_ctx[None])[0]

    res = {}
    moments = ((m_w_in, v_w_in), (m_w_q_b, v_w_q_b), (m_w_kv_b, v_w_kv_b), (m_w_out, v_w_out), (m_w_mlp1, v_w_mlp1),
               (m_w_mlp2, v_w_mlp2))
    for i, (n, w, (m, v)) in enumerate(zip(names, big, moments)):
        res[n] = tuple(adamw_halves(w, m, v, g_own[DEPTH * i:DEPTH * (i + 1)], g_oth[DEPTH * i:DEPTH * (i + 1)], cidx,
                                    "adamw_" + n))
    g_mw = jnp.stack(g_mod_w)
    r_mw = adamw(mod_w.reshape(-1, mcols), g_mw.reshape(-1, mcols), m_mod_w.reshape(-1, mcols),
                 v_mod_w.reshape(-1, mcols), name="adamw_mod_w")
    res["mod_w"] = (g_mw,) + tuple(a.reshape(mod_w.shape) for a in r_mw)

    given = dict(norm1_w=(norm1_w, m_norm1_w, v_norm1_w), norm2_w=(norm2_w, m_norm2_w, v_norm2_w),
                 conv_b=(conv_b, m_conv_b, v_conv_b), dt_bias=(dt_bias, m_dt_bias, v_dt_bias),
                 a_log=(a_log, m_a_log, v_a_log), ssd_d=(ssd_d, m_ssd_d, v_ssd_d),
                 ssd_norm_w=(ssd_norm_w, m_ssd_norm_w, v_ssd_norm_w), q_a_norm_w=(q_a_norm_w, m_q_a_norm_w, v_q_a_norm_w),
                 kv_a_norm_w=(kv_a_norm_w, m_kv_a_norm_w, v_kv_a_norm_w), pool_w=(pool_w, m_pool_w, v_pool_w),
                 pool_scale=(pool_scale, m_pool_scale, v_pool_scale),
                 final_norm_w=(final_norm_w, m_final_norm_w, v_final_norm_w), mod_b=(mod_b, m_mod_b, v_mod_b))
    zero_cw = jnp.zeros((2, 4, XBC), F32)
    packs = [_pack_small({n: (given[n][i] if n in given else zero_cw) for n, _ in SMALL}) for i in range(3)]
    r_small = [_unpack_small(a) for a in adamw(packs[0], tot[0:SMALL_ROWS], packs[1], packs[2], name="adamw_small")]
    for n in given:
        res[n] = (gsmall[n], r_small[0][n], r_small[1][n], r_small[2][n])

    g_cw = lax.dynamic_slice_in_dim(gsmall["conv_w"], kchip * cshard, cshard, axis=2)
    padcw = lambda a: jnp.pad(a.reshape(8, cshard), ((0, 0), (0, 256 - cshard)))
    r_cw = adamw(padcw(conv_w), padcw(g_cw), padcw(m_conv_w), padcw(v_conv_w), name="adamw_conv_w")
    res["conv_w"] = (g_cw,) + tuple(a[:, 0:cshard].reshape(conv_w.shape) for a in r_cw)
    r_cc = adamw(c_ctx.reshape(8, 128), g_c_ctx.reshape(8, 128), m_c_ctx.reshape(8, 128), v_c_ctx.reshape(8, 128),
                 name="adamw_c_ctx")
    res["c_ctx"] = (g_c_ctx,) + tuple(a.reshape(D) for a in r_cc)

    order = ("c_ctx", "mod_w", "mod_b", "norm1_w", "norm2_w", "w_in", "conv_w", "conv_b", "dt_bias", "a_log", "ssd_d",
             "ssd_norm_w", "q_a_norm_w", "w_q_b", "kv_a_norm_w", "w_kv_b", "pool_w", "pool_scale", "w_out", "w_mlp1",
             "w_mlp2", "final_norm_w")
    return (loss, grad_x) + tuple(res[n][i] for i in range(4) for n in order)
```

```python
import functools
import math

import numpy as np
import jax
import jax.numpy as jnp
from jax import lax
from jax.experimental import pallas as pl
from jax.experimental.pallas import tpu as pltpu

F32 = jnp.float32
BF16 = jnp.bfloat16
MXU = jnp.bfloat16

D = 1024
DEPTH = 2
GRID_W = 64
CTX = 256
EPS = 1e-6
SSD_HEADS = 6
SSD_P = 64
SSD_INNER = 384
SSD_N = 128
CHUNK = 128
XBC = 896
MLA_HEADS = 6
QK_NOPE = 64
QK_ROPE = 32
QK_DIM = 96
HP = 128
QW = MLA_HEADS * HP
POOL_DIM = 256
D_FF = 4096
FF_BLK = 1024
IN_COLS = 2092
NP = 2176
P_SPLITS = (384, 896, 256, 256, 256, 128)
DT0 = 32
CAT = QW + SSD_INNER + POOL_DIM

SB = 256
TM = 512
HALO = 8

ADAM_LR = 0.001
ADAM_B1 = 0.9
ADAM_B2 = 0.999
ADAM_EPS = 1e-08
ADAM_WD = 0.01
ADAM_STEP = 10

NT = (((1,), (1,)), ((), ()))
TN = (((0,), (0,)), ((), ()))


def _cp(vmem_mb=None):
    if vmem_mb is None:
        return pltpu.CompilerParams()
    return pltpu.CompilerParams(vmem_limit_bytes=vmem_mb << 20)


def _dot(a, b):
    return jnp.dot(a, b, preferred_element_type=F32)


def _dotg(a, b, dims):
    return lax.dot_general(a, b, dims, preferred_element_type=F32)


def _dot_hi(a, b, dims=None, sel_first=False):
    dims = (((1,), (0,)), ((), ())) if dims is None else dims
    v, s = (b, a) if sel_first else (a, b)
    hi = v.astype(BF16)
    lo = (v - hi.astype(F32)).astype(BF16)
    s = s.astype(BF16)
    if sel_first:
        return _dotg(s, hi, dims) + _dotg(s, lo, dims)
    return _dotg(hi, s, dims) + _dotg(lo, s, dims)


def _rms_hat(x):
    rstd = lax.rsqrt(jnp.mean(x * x, axis=-1, keepdims=True) + EPS)
    return x * rstd, rstd


def _rms_bwd(dn, xhat, rstd, w):
    dxhat = dn * w
    dx = rstd * (dxhat - xhat * jnp.mean(dxhat * xhat, axis=-1, keepdims=True))
    return dx, jnp.sum(dn * xhat, axis=0, keepdims=True)


def _sigmoid(z):
    return 1.0 / (1.0 + jnp.exp(-z))


def _colsum(a):
    return jnp.sum(a, axis=0, keepdims=True)


def _rowspec(cols, tm=TM):
    return pl.BlockSpec((tm, cols), lambda i: (i, 0))


def _fullspec(shape):
    n = len(shape)
    return pl.BlockSpec(shape, lambda *_: (0,) * n)


def _resident(shape):
    n = len(shape)
    return pl.BlockSpec(shape, lambda *_: (0,) * n, pipeline_mode=pl.Buffered(1))


def _halo_specs(cols, nrows, halo=HALO):
    per = SB // halo
    last = nrows // halo - 1
    prev = pl.BlockSpec((halo, cols), lambda i: (jnp.maximum(i * per - 1, 0), 0))
    nxt = pl.BlockSpec((halo, cols), lambda i: (jnp.minimum((i + 1) * per, last), 0))
    return prev, nxt


def _ext_rows(cur, prev, nxt, i, blocks_per_sample):
    j = i % blocks_per_sample
    first = jnp.logical_or(j == 0, j == 1)
    last = jnp.logical_or(j == 0, j == blocks_per_sample - 1)
    p = jnp.where(first, 0.0, prev.astype(F32))
    n = jnp.where(last, 0.0, nxt.astype(F32))
    return jnp.concatenate([p, cur.astype(F32), n], axis=0)


def _shift(ext, s):
    n = ext.shape[0]
    halo = (n - SB) // 2
    return pltpu.roll(ext, (-s) % n, axis=0)[halo:halo + SB, :]


def in_proj(x, bm, nw, w):
    R = x.shape[0]

    def body(x_ref, bm_ref, nw_ref, w_ref, h_ref, *outs):
        for s in range(TM // SB):
            rows = slice(s * SB, (s + 1) * SB)
            xhat, _ = _rms_hat(x_ref[rows, :])
            h = xhat * nw_ref[...] * (1.0 + bm_ref[s, 1:2, :]) + bm_ref[s, 0:1, :]
            h_ref[rows, :] = h.astype(h_ref.dtype)
        p = _dot(h_ref[...], w_ref[...])
        off = 0
        for o, n in zip(outs, P_SPLITS):
            o[...] = p[:, off:off + n].astype(o.dtype)
            off += n

    return pl.pallas_call(
        body, name="in_proj", grid=(R // TM,),
        in_specs=[_rowspec(D), pl.BlockSpec((TM // SB, 8, D), lambda i: (i, 0, 0)), _fullspec((1, D)),
                  _fullspec((D, NP))],
        out_specs=[_rowspec(D)] + [_rowspec(n) for n in P_SPLITS],
        out_shape=[jax.ShapeDtypeStruct((R, D), MXU)]
                  + [jax.ShapeDtypeStruct((R, n), dt) for n, dt in zip(P_SPLITS, (MXU, MXU, MXU, MXU, F32, F32))],
        compiler_params=_cp(56),
    )(x, bm, nw, w)


def in_proj_bwd(dx1, x, h, dz, dxbc, dqa, dkva, dpool, dkr, ddt, bm, nw, w):
    R = x.shape[0]

    def body(dx1_ref, x_ref, h_ref, dz_ref, dxbc_ref, dqa_ref, dkva_ref, dpool_ref, dkr_ref, ddt_ref, bm_ref, nw_ref,
             w_ref, dx_ref, dw_ref, part_ref, dp_ref):
        @pl.when(pl.program_id(0) == 0)
        def _():
            dw_ref[...] = jnp.zeros_like(dw_ref)

        dp_ref[:, 0:384] = dz_ref[...].astype(dp_ref.dtype)
        dp_ref[:, 384:1280] = dxbc_ref[...].astype(dp_ref.dtype)
        dp_ref[:, 1280:1536] = dqa_ref[...].astype(dp_ref.dtype)
        dp_ref[:, 1536:1792] = dkva_ref[...].astype(dp_ref.dtype)
        dp_ref[:, 1792:2048] = dpool_ref[...].astype(dp_ref.dtype)
        dp_ref[:, 2048:2176] = (dkr_ref[...] + ddt_ref[...]).astype(dp_ref.dtype)
        dw_ref[...] += _dotg(h_ref[...], dp_ref[...], TN)
        dh = _dotg(dp_ref[...], w_ref[...], NT)
        w = nw_ref[...]
        for s in range(TM // SB):
            rows = slice(s * SB, (s + 1) * SB)
            xhat, rstd = _rms_hat(x_ref[rows, :])
            dhs = dh[rows, :]
            sc1 = 1.0 + bm_ref[s, 1:2, :]
            dx, dnw = _rms_bwd(dhs * sc1, xhat, rstd, w)
            dx_ref[rows, :] = dx1_ref[rows, :] + dx
            part_ref[s] = jnp.concatenate(
                [_colsum(dhs), _colsum(dhs * xhat * w), dnw, jnp.zeros((5, D), F32)], axis=0)

    return pl.pallas_call(
        body, name="in_proj_bwd", grid=(R // TM,),
        in_specs=[_rowspec(D), _rowspec(D), _rowspec(D), _rowspec(384), _rowspec(896), _rowspec(256), _rowspec(256),
                  _rowspec(256), _rowspec(128), _rowspec(128),
                  pl.BlockSpec((TM // SB, 8, D), lambda i: (i, 0, 0)), _fullspec((1, D)), _resident((D, NP))],
        out_specs=[_rowspec(D), _fullspec((D, NP)), pl.BlockSpec((TM // SB, 8, D), lambda i: (i, 0, 0))],
        out_shape=[jax.ShapeDtypeStruct((R, D), F32), jax.ShapeDtypeStruct((D, NP), F32),
                   jax.ShapeDtypeStruct((R // SB, 8, D), F32)],
        scratch_shapes=[pltpu.VMEM((TM, NP), MXU)],
        compiler_params=_cp(56),
    )(dx1, x, h, dz, dxbc, dqa, dkva, dpool, dkr, ddt, bm, nw, w)


def mix_fwd(x, attn, ssd, pool, bm, wo):
    R = x.shape[0]

    def body(x_ref, a_ref, s_ref, p_ref, bm_ref, wo_ref, x1_ref, mix_ref, cat_ref):
        cat_ref[:, 0:QW] = a_ref[...].astype(cat_ref.dtype)
        cat_ref[:, QW:QW + SSD_INNER] = s_ref[...].astype(cat_ref.dtype)
        cat_ref[:, QW + SSD_INNER:CAT] = p_ref[...].astype(cat_ref.dtype)
        mix = _dot(cat_ref[...], wo_ref[...])
        mix_ref[...] = mix.astype(mix_ref.dtype)
        for s in range(TM // SB):
            rows = slice(s * SB, (s + 1) * SB)
            x1_ref[rows, :] = x_ref[rows, :] + bm_ref[s, 2:3, :] * mix[rows, :]

    return pl.pallas_call(
        body, name="mix_fwd", grid=(R // TM,),
        in_specs=[_rowspec(D), _rowspec(QW), _rowspec(SSD_INNER), _rowspec(POOL_DIM),
                  pl.BlockSpec((TM // SB, 8, D), lambda i: (i, 0, 0)), _fullspec((CAT, D))],
        out_specs=[_rowspec(D), _rowspec(D), _rowspec(CAT)],
        out_shape=[jax.ShapeDtypeStruct((R, D), F32), jax.ShapeDtypeStruct((R, D), MXU),
                   jax.ShapeDtypeStruct((R, CAT), MXU)],
        compiler_params=_cp(48),
    )(x, attn, ssd, pool, bm, wo)


def mix_bwd(dx1, mix, cat, bm, wo):
    R = dx1.shape[0]

    def body(dx1_ref, mix_ref, cat_ref, bm_ref, wo_ref, da_ref, ds_ref, dpl_ref, dw_ref, part_ref, dmb_ref):
        @pl.when(pl.program_id(0) == 0)
        def _():
            dw_ref[...] = jnp.zeros_like(dw_ref)

        for s in range(TM // SB):
            rows = slice(s * SB, (s + 1) * SB)
            d = dx1_ref[rows, :]
            dmb_ref[rows, :] = (d * bm_ref[s, 2:3, :]).astype(dmb_ref.dtype)
            part_ref[s] = jnp.concatenate([_colsum(d * mix_ref[rows, :].astype(F32)), jnp.zeros((7, D), F32)], axis=0)
        dw_ref[...] += _dotg(cat_ref[...], dmb_ref[...], TN)
        dcat = _dotg(dmb_ref[...], wo_ref[...], NT)
        da_ref[...] = dcat[:, 0:QW]
        ds_ref[...] = dcat[:, QW:QW + SSD_INNER]
        dpl_ref[...] = dcat[:, QW + SSD_INNER:CAT]

    return pl.pallas_call(
        body, name="mix_bwd", grid=(R // TM,),
        in_specs=[_rowspec(D), _rowspec(D), _rowspec(CAT), pl.BlockSpec((TM // SB, 8, D), lambda i: (i, 0, 0)),
                  _resident((CAT, D))],
        out_specs=[_rowspec(QW), _rowspec(SSD_INNER), _rowspec(POOL_DIM), _fullspec((CAT, D)),
                   pl.BlockSpec((TM // SB, 8, D), lambda i: (i, 0, 0))],
        out_shape=[jax.ShapeDtypeStruct((R, QW), F32), jax.ShapeDtypeStruct((R, SSD_INNER), F32),
                   jax.ShapeDtypeStruct((R, POOL_DIM), F32), jax.ShapeDtypeStruct((CAT, D), F32),
                   jax.ShapeDtypeStruct((R // SB, 8, D), F32)],
        scratch_shapes=[pltpu.VMEM((TM, D), MXU)],
        compiler_params=_cp(48),
    )(dx1, mix, cat, bm, wo)


def mlp_fwd(x1, bm, nw, w1, w2, side=None):
    R = x1.shape[0]

    def body(x1_ref, bm_ref, nw_ref, w1_ref, w2_ref, x2_ref, mo_ref, r_ref, h2_ref):
        for s in range(TM // SB):
            rows = slice(s * SB, (s + 1) * SB)
            xhat, _ = _rms_hat(x1_ref[rows, :])
            h = xhat * nw_ref[...] * (1.0 + bm_ref[s, 4:5, :]) + bm_ref[s, 3:4, :]
            h2_ref[rows, :] = h.astype(h2_ref.dtype)
        for j in range(D_FF // FF_BLK):
            cols = slice(j * FF_BLK, (j + 1) * FF_BLK)
            r = jnp.maximum(_dot(h2_ref[...], w1_ref[:, cols]), 0.0)
            r_ref[:, cols] = r.astype(r_ref.dtype)
            d = _dot((r * r).astype(MXU), w2_ref[cols, :])
            if j == 0:
                x2_ref[...] = d
            else:
                x2_ref[...] += d
        mo_ref[...] = x2_ref[...].astype(mo_ref.dtype)
        for s in range(TM // SB):
            rows = slice(s * SB, (s + 1) * SB)
            x2_ref[rows, :] = x1_ref[rows, :] + bm_ref[s, 5:6, :] * x2_ref[rows, :]

    grid = (R // TM,)
    body, side_in, side_out, side_shapes, side_scratch, side_args = _side_wrap(body, 5, 4, 0, side, grid)
    outs = pl.pallas_call(
        body, name="mlp_fwd" if side is None else "mlp_fwd_comm", grid=grid,
        in_specs=[_rowspec(D), pl.BlockSpec((TM // SB, 8, D), lambda i: (i, 0, 0)), _fullspec((1, D)),
                  _resident((D, D_FF)), _resident((D_FF, D))] + side_in,
        out_specs=[_rowspec(D), _rowspec(D), _rowspec(D_FF), _rowspec(D)] + side_out,
        out_shape=[jax.ShapeDtypeStruct((R, D), F32), jax.ShapeDtypeStruct((R, D), MXU),
                   jax.ShapeDtypeStruct((R, D_FF), BF16), jax.ShapeDtypeStruct((R, D), MXU)] + side_shapes,
        scratch_shapes=side_scratch,
        compiler_params=_cp(56),
    )(x1, bm, nw, w1, w2, *side_args)
    return tuple(outs[:4]) + (list(outs[4:]),)


def mlp_bwd(dx2, x1, mo, r, bm, nw, w2, w1, side=None):
    R = x1.shape[0]

    def body(dx2_ref, x1_ref, mo_ref, r_ref, bm_ref, nw_ref, w2_ref, w1_ref, dx1_ref, du_ref, dob_ref, part_ref,
             acc_ref):
        for s in range(TM // SB):
            rows = slice(s * SB, (s + 1) * SB)
            dob_ref[rows, :] = (dx2_ref[rows, :] * bm_ref[s, 5:6, :]).astype(dob_ref.dtype)
        for j in range(D_FF // FF_BLK):
            cols = slice(j * FF_BLK, (j + 1) * FF_BLK)
            du = _dotg(dob_ref[...], w2_ref[cols, :], NT) * (2.0 * r_ref[:, cols].astype(F32))
            du_ref[:, cols] = du.astype(du_ref.dtype)
            d = _dotg(du_ref[:, cols], w1_ref[:, cols], NT)
            if j == 0:
                acc_ref[...] = d
            else:
                acc_ref[...] += d
        w = nw_ref[...]
        for s in range(TM // SB):
            rows = slice(s * SB, (s + 1) * SB)
            xhat, rstd = _rms_hat(x1_ref[rows, :])
            dh = acc_ref[rows, :]
            dx, dnw = _rms_bwd(dh * (1.0 + bm_ref[s, 4:5, :]), xhat, rstd, w)
            d2 = dx2_ref[rows, :]
            dx1_ref[rows, :] = d2 + dx
            part_ref[s] = jnp.concatenate(
                [_colsum(dh), _colsum(dh * xhat * w), _colsum(d2 * mo_ref[rows, :].astype(F32)), dnw,
                 jnp.zeros((4, D), F32)], axis=0)

    grid = (R // TM,)
    body, side_in, side_out, side_shapes, side_scratch, side_args = _side_wrap(body, 8, 4, 1, side, grid)
    outs = pl.pallas_call(
        body, name="mlp_bwd" if side is None else "mlp_bwd_comm", grid=grid,
        in_specs=[_rowspec(D), _rowspec(D), _rowspec(D), _rowspec(D_FF),
                  pl.BlockSpec((TM // SB, 8, D), lambda i: (i, 0, 0)), _fullspec((1, D)),
                  _resident((D_FF, D)), _resident((D, D_FF))] + side_in,
        out_specs=[_rowspec(D), _rowspec(D_FF), _rowspec(D), pl.BlockSpec((TM // SB, 8, D), lambda i: (i, 0, 0))]
                  + side_out,
        out_shape=[jax.ShapeDtypeStruct((R, D), F32), jax.ShapeDtypeStruct((R, D_FF), MXU),
                   jax.ShapeDtypeStruct((R, D), MXU), jax.ShapeDtypeStruct((R // SB, 8, D), F32)] + side_shapes,
        scratch_shapes=[pltpu.VMEM((TM, D), F32)] + side_scratch,
        compiler_params=_cp(56),
    )(dx2, x1, mo, r, bm, nw, w2, w1, *side_args)
    return tuple(outs[:4]) + (list(outs[4:]),)


def mm_tn(a, b, square_a=False, name="mm_tn", col_blocks=False):
    R, M = a.shape
    N = b.shape[1]
    tm = M if M <= 1408 else 1024
    tn = N if N <= 2176 else 1024
    tk = next((c for c in ((2176, 1088, 512) if tm + tn <= 2048 else (1088, 512)) if R % c == 0), R)
    assert not col_blocks or tm == M

    def body(a_ref, b_ref, o_ref):
        @pl.when(pl.program_id(2) == 0)
        def _():
            o_ref[...] = jnp.zeros_like(o_ref)

        av = a_ref[...]
        if square_a:
            av = av.astype(F32)
            av = (av * av).astype(MXU)
        prod = _dotg(av.astype(MXU), b_ref[...].astype(MXU), TN)
        if col_blocks:
            o_ref[0] += prod
        else:
            o_ref[...] += prod

    if col_blocks:
        out_spec = pl.BlockSpec((1, tm, tn), lambda i, j, k: (j, 0, 0))
        out_shape = jax.ShapeDtypeStruct((N // tn, M, tn), F32)
    else:
        out_spec = pl.BlockSpec((tm, tn), lambda i, j, k: (i, j))
        out_shape = jax.ShapeDtypeStruct((M, N), F32)
    return pl.pallas_call(
        body, name=name, grid=(M // tm, N // tn, R // tk),
        in_specs=[pl.BlockSpec((tk, tm), lambda i, j, k: (k, i)), pl.BlockSpec((tk, tn), lambda i, j, k: (k, j))],
        out_specs=out_spec, out_shape=out_shape,
        compiler_params=_cp(48),
    )(a, b)


def final_loss(x, tgt, fw, blocks_per_sample):
    R = x.shape[0]
    nxb = blocks_per_sample - 1

    def body(x_ref, t_ref, fw_ref, dx_ref, part_ref):
        i = pl.program_id(0)
        is_ctx = (i % blocks_per_sample) == 0
        xhat, rstd = _rms_hat(x_ref[...])
        w = fw_ref[...]
        err = xhat * w - t_ref[...]
        dx, dfw = _rms_bwd(err * (1.0 / D), xhat, rstd, w)
        keep = jnp.where(is_ctx, 0.0, 1.0)
        dx_ref[...] = dx * keep
        part_ref[0] = jnp.concatenate([dfw * keep, _colsum(err * err) * keep, jnp.zeros((6, D), F32)], axis=0)

    def tmap(i):
        return ((i // blocks_per_sample) * nxb + jnp.maximum(i % blocks_per_sample - 1, 0), 0)

    return pl.pallas_call(
        body, name="final_loss", grid=(R // SB,),
        in_specs=[_rowspec(D, SB), pl.BlockSpec((SB, D), tmap), _fullspec((1, D))],
        out_specs=[_rowspec(D, SB), pl.BlockSpec((1, 8, D), lambda i: (i, 0, 0))],
        out_shape=[jax.ShapeDtypeStruct((R, D), F32), jax.ShapeDtypeStruct((R // SB, 8, D), F32)],
    )(x, tgt, fw)


def _softplus(v):
    return jnp.maximum(v, 0.0) + jnp.log(1.0 + jnp.exp(-jnp.abs(v)))


def _conv_taps(ext):
    return [_shift(ext, k - 1) for k in range(4)]


def _conv_out(taps, cw_ref, cb_ref):
    return (cb_ref[...] + cw_ref[0:1, :] * taps[0] + cw_ref[1:2, :] * taps[1] + cw_ref[2:3, :] * taps[2]
            + cw_ref[3:4, :] * taps[3])


def _dt_dir(v, d):
    lane = lax.broadcasted_iota(jnp.int32, v.shape, 1)
    return jnp.where(lane < SSD_HEADS, pltpu.roll(v, (128 - DT0 - SSD_HEADS * d) % 128, axis=1), 0.0)


def ssd_prep(pxbc, plast, cw, cb, dtb, blocks_per_sample):
    R = pxbc.shape[0]
    prev, nxt = _halo_specs(XBC, R, 8 * 4 // pxbc.dtype.itemsize)

    def body(cur_ref, prev_ref, nxt_ref, pl_ref, cw_ref, cb_ref, dtb_ref, xs_ref, bm_ref, cm_ref, dt_ref):
        i = pl.program_id(0)
        ext = _ext_rows(cur_ref[...], prev_ref[...], nxt_ref[...], i, blocks_per_sample)
        co = _conv_out(_conv_taps(ext), cw_ref, cb_ref)
        a = co * _sigmoid(co)
        xs_ref[...] = a[:, 0:384]
        bm_ref[...] = a[:, 384:640]
        cm_ref[...] = a[:, 640:896]
        sp = _softplus(pl_ref[...] + dtb_ref[...])
        dt_ref[0] = _dt_dir(sp, 0)
        dt_ref[1] = _dt_dir(sp, 1)

    return pl.pallas_call(
        body, name="ssd_prep", grid=(R // SB,),
        in_specs=[_rowspec(XBC, SB), prev, nxt, _rowspec(128, SB), _fullspec((8, XBC)), _fullspec((1, XBC)),
                  _fullspec((1, 128))],
        out_specs=[_rowspec(384, SB), _rowspec(256, SB), _rowspec(256, SB),
                   pl.BlockSpec((2, SB, 128), lambda i: (0, i, 0))],
        out_shape=[jax.ShapeDtypeStruct((R, 384), F32), jax.ShapeDtypeStruct((R, 256), F32),
                   jax.ShapeDtypeStruct((R, 256), F32), jax.ShapeDtypeStruct((2, R, 128), F32)],
    )(pxbc, pxbc, pxbc, plast, cw, cb, dtb)


def _chunk_index(d, s, nc):
    nctx = CTX // CHUNK
    back = jnp.where(s < nctx, nctx - 1 - s, nc + nctx - 1 - s)
    return jnp.where(d == 0, s, back)


def _scan_common(d, dt, arow, eexp, xs):
    ii = lax.broadcasted_iota(jnp.int32, (CHUNK, CHUNK), 0)
    jj = lax.broadcasted_iota(jnp.int32, (CHUNK, CHUNK), 1)
    mask = ((ii - jj) * (1 - 2 * d)) >= 0
    adt = dt * arow
    tmat = jnp.where(mask, 1.0, 0.0)
    cs = _dot_hi(tmat, adt, sel_first=True)
    tot = _colsum(adt)
    dtx = _dot_hi(dt, eexp)
    xt = xs * dtx
    ecs = jnp.exp(cs)
    ecx = _dot_hi(ecs, eexp)
    dte = jnp.exp(tot - cs)
    dtex = _dot_hi(dte, eexp)
    etot = jnp.exp(tot)
    etx = _dot_hi(jnp.broadcast_to(etot, (8, 128)), eexp)[0:1, :]
    return mask, tmat, adt, cs, tot, dtx, xt, ecs, ecx, dte, dtex, etot, etx


def _decay_matrix(mask, cs, cst, h):
    return jnp.exp(jnp.where(mask, cs[:, h:h + 1] - cst[h:h + 1, :], -1e30))


def _side_wrap(body, n_in, n_out, n_scratch, side, grid):
    if side is None:
        return body, [], [], [], [], []
    ni, no = len(side.ins), len(side.out_shapes)

    def wrapped(*refs):
        ins, refs = refs[:n_in], refs[n_in:]
        side_ins, refs = refs[:ni], refs[ni:]
        outs, refs = refs[:n_out], refs[n_out:]
        side_outs, refs = refs[:no], refs[no:]
        scratch, sems = refs[:n_scratch], refs[n_scratch:]
        ids = [pl.program_id(a) for a in range(len(grid))]
        first = functools.reduce(jnp.logical_and, [i == 0 for i in ids])
        last = functools.reduce(jnp.logical_and, [i == g - 1 for i, g in zip(ids, grid)])
        pl.when(first)(lambda: side.start(side_ins, side_outs, sems))
        body(*ins, *outs, *scratch)
        pl.when(last)(lambda: side.finish(side_ins, side_outs, sems))

    return wrapped, [ANY] * ni, [ANY] * no, list(side.out_shapes), _sems(side.nsem), list(side.ins)


def ssd_scan_fwd(xs, bm, cm, dtv, arow, eexp, nb, T, side=None):
    R = xs.shape[0]
    nc = T // CHUNK
    B = range(nb)

    def body(xs_ref, bm_ref, cm_ref, dt_ref, a_ref, e_ref, y_ref, hin_ref, st_ref):
        d = pl.program_id(0)
        s = pl.program_id(1)

        @pl.when(s == 0)
        def _():
            st_ref[...] = jnp.zeros_like(st_ref)

        eexp = e_ref[...]
        com = [_scan_common(d, dt_ref[0, b], a_ref[0, 0:1, :], eexp, xs_ref[b]) for b in B]
        mask = com[0][0]
        cs = [com[b][3] for b in B]
        cst = [cs[b].T for b in B]
        sin = [st_ref[b] for b in B]
        for b in B:
            hin_ref[0, b] = sin[b]
        sb = [sin[b].astype(MXU) for b in B]
        xtb = [com[b][6].astype(MXU) for b in B]
        xw = [(com[b][6] * com[b][10]).astype(MXU) for b in B]
        g0 = lax.broadcasted_iota(jnp.int32, (CHUNK, SSD_INNER), 1) < 192
        lane = lax.broadcasted_iota(jnp.int32, (CHUNK, 128), 1)
        c = [[cm_ref[b, :, 0:128].astype(MXU), cm_ref[b, :, 128:256].astype(MXU)] for b in B]
        bq = [[bm_ref[b, :, 0:128].astype(MXU), bm_ref[b, :, 128:256].astype(MXU)] for b in B]
        y = [jnp.where(g0, _dot(c[b][0], sb[b]), _dot(c[b][1], sb[b])) * com[b][8] for b in B]
        cb = [[_dotg(c[b][g], bq[b][g], NT) for g in range(2)] for b in B]
        blocks = [[] for _ in B]
        for blk in range(3):
            acc = [None for _ in B]
            for hh in range(2):
                h = blk * 2 + hh
                for b in B:
                    m = (cb[b][h // 3] * _decay_matrix(mask, cs[b], cst[b], h)).astype(MXU)
                    res = _dot(m, xtb[b][:, blk * 128:(blk + 1) * 128])
                    acc[b] = res if hh == 0 else jnp.where(lane < 64, acc[b], res)
            for b in B:
                blocks[b].append(acc[b])
        for b in B:
            y_ref[0, b] = y[b] + jnp.concatenate(blocks[b], axis=1)
            st_ref[b] = sin[b] * com[b][12] + jnp.where(g0, _dotg(bq[b][0], xw[b], TN), _dotg(bq[b][1], xw[b], TN))

    def rows(cols):
        return pl.BlockSpec((nb, CHUNK, cols), lambda d, s: (0, _chunk_index(d, s, nc), 0))

    def by_dir(cols):
        return pl.BlockSpec((1, nb, CHUNK, cols), lambda d, s: (d, 0, _chunk_index(d, s, nc), 0))

    grid = (2, nc)
    body, side_in, side_out, side_shapes, side_scratch, side_args = _side_wrap(body, 6, 2, 1, side, grid)
    outs = pl.pallas_call(
        body, name="ssd_scan_fwd" if side is None else "ssd_scan_fwd_comm", grid=grid,
        in_specs=[rows(384), rows(256), rows(256), by_dir(128), pl.BlockSpec((1, 8, 128), lambda d, s: (d, 0, 0)),
                  pl.BlockSpec((128, 384), lambda d, s: (0, 0))] + side_in,
        out_specs=[by_dir(384),
                   pl.BlockSpec((1, nb, CHUNK, 384), lambda d, s: (d * nc + _chunk_index(d, s, nc), 0, 0, 0))] + side_out,
        out_shape=[jax.ShapeDtypeStruct((2, nb, T, 384), F32), jax.ShapeDtypeStruct((2 * nc, nb, CHUNK, 384), F32)]
                  + side_shapes,
        scratch_shapes=[pltpu.VMEM((nb, CHUNK, 384), F32)] + side_scratch,
    )(xs.reshape(nb, T, 384), bm.reshape(nb, T, 256), cm.reshape(nb, T, 256), dtv.reshape(2, nb, T, 128), arow, eexp,
      *side_args)
    return outs[0].reshape(2, R, 384), outs[1], list(outs[2:])


def ssd_scan_bwd(xs, bm, cm, dtv, arow, eexp, hin, dy, nb, T, side=None):
    R = xs.shape[0]
    nc = T // CHUNK
    B = range(nb)

    def chunk(d, s):
        return _chunk_index(d, nc - 1 - s, nc)

    def body(xs_ref, bm_ref, cm_ref, dt_ref, a_ref, e_ref, hin_ref, dy_ref,
             dxs_ref, dbm_ref, dcm_ref, ddt_ref, da_ref, ds_ref):
        d = pl.program_id(0)
        s = pl.program_id(1)

        @pl.when(s == 0)
        def _():
            ds_ref[...] = jnp.zeros_like(ds_ref)
            da_ref[...] = jnp.zeros_like(da_ref)

        eexp = e_ref[...]
        arow = a_ref[0, 0:1, :]
        dt = [dt_ref[0, b] for b in B]
        xs_v = [xs_ref[b] for b in B]
        com = [_scan_common(d, dt[b], arow, eexp, xs_v[b]) for b in B]
        mask, tmat = com[0][0], com[0][1]
        cs, dtx, xt, ecs, ecx, dte, dtex, etot, etx = [[com[b][i] for b in B] for i in (3, 5, 6, 7, 8, 9, 10, 11, 12)]
        cst = [cs[b].T for b in B]
        sin = [hin_ref[0, b] for b in B]
        sb = [sin[b].astype(MXU) for b in B]
        dsp = [ds_ref[b] for b in B]
        dyv = [dy_ref[b] for b in B]
        xtb = [xt[b].astype(MXU) for b in B]
        xw = [(xt[b] * dtex[b]).astype(MXU) for b in B]
        g0 = lax.broadcasted_iota(jnp.int32, (CHUNK, SSD_INNER), 1) < 192
        lane = lax.broadcasted_iota(jnp.int32, (CHUNK, 128), 1)
        sub = lax.broadcasted_iota(jnp.int32, (CHUNK, 128), 0)
        c = [[cm_ref[b, :, 0:128].astype(MXU), cm_ref[b, :, 128:256].astype(MXU)] for b in B]
        bq = [[bm_ref[b, :, 0:128].astype(MXU), bm_ref[b, :, 128:256].astype(MXU)] for b in B]

        cs_prod = [jnp.where(g0, _dot(c[b][0], sb[b]), _dot(c[b][1], sb[b])) for b in B]
        dcsp = [dyv[b] * ecx[b] for b in B]
        dcsp_g = [[jnp.where(g0, dcsp[b], 0.0).astype(MXU), jnp.where(g0, 0.0, dcsp[b]).astype(MXU)] for b in B]
        dcs = [_dot_hi(dyv[b] * cs_prod[b], eexp, NT) * ecs[b] for b in B]
        dc = [[_dotg(dcsp_g[b][g], sb[b], NT) for g in range(2)] for b in B]
        dsin = [_dotg(c[b][0], dcsp_g[b][0], TN) + _dotg(c[b][1], dcsp_g[b][1], TN) + dsp[b] * etx[b] for b in B]

        dtot = [_dot_hi(jnp.broadcast_to(_colsum(dsp[b] * sin[b]), (8, SSD_INNER)), eexp, NT)[0:1, :] * etot[b] for b in B]
        dsp_g = [[jnp.where(g0, dsp[b], 0.0).astype(MXU), jnp.where(g0, 0.0, dsp[b]).astype(MXU)] for b in B]
        dxw = [_dot(bq[b][0], dsp_g[b][0]) + _dot(bq[b][1], dsp_g[b][1]) for b in B]
        db = [[_dotg(xw[b], dsp_g[b][g], NT) for g in range(2)] for b in B]
        dxt = [dxw[b] * dtex[b] for b in B]
        ddte = [_dot_hi(dxw[b] * xt[b], eexp, NT) * dte[b] for b in B]
        dtot = [dtot[b] + _colsum(ddte[b]) for b in B]
        dcs = [dcs[b] - ddte[b] for b in B]

        cb = [[_dotg(c[b][g], bq[b][g], NT) for g in range(2)] for b in B]
        dg = [[jnp.zeros((CHUNK, CHUNK), F32), jnp.zeros((CHUNK, CHUNK), F32)] for _ in B]
        dcs_rows = [jnp.zeros((CHUNK, 128), F32) for _ in B]
        dxt_blocks = [[] for _ in B]
        for blk in range(3):
            acc = [jnp.zeros((CHUNK, 128), F32) for _ in B]
            for hh in range(2):
                h = blk * 2 + hh
                g = h // 3
                mine = (lane < 64) if hh == 0 else (lane >= 64)
                for b in B:
                    dyh = jnp.where(mine, dyv[b][:, blk * 128:(blk + 1) * 128], 0.0).astype(MXU)
                    lh = _decay_matrix(mask, cs[b], cst[b], h)
                    m = cb[b][g] * lh
                    dm = _dotg(dyh, xtb[b][:, blk * 128:(blk + 1) * 128], NT)
                    acc[b] = acc[b] + _dotg(m.astype(MXU), dyh, TN)
                    dg[b][g] = dg[b][g] + dm * lh
                    q = dm * m
                    dcs[b] = dcs[b] + jnp.where(lane == h, jnp.sum(q, axis=1, keepdims=True), 0.0)
                    dcs_rows[b] = dcs_rows[b] - jnp.where(sub == h, jnp.sum(q, axis=0, keepdims=True), 0.0)
            for b in B:
                dxt_blocks[b].append(acc[b])
        for b in B:
            dxt[b] = dxt[b] + jnp.concatenate(dxt_blocks[b], axis=1)
            for g in range(2):
                dgb = dg[b][g].astype(MXU)
                dc[b][g] = dc[b][g] + _dot(dgb, bq[b][g])
                db[b][g] = db[b][g] + _dotg(dgb, c[b][g], TN)
            dcs[b] = dcs[b] + dcs_rows[b].T

        for b in B:
            dadt = _dot_hi(tmat, dcs[b], TN, sel_first=True) + dtot[b]
            ddt_ref[0, b] = dadt * arow + _dot_hi(dxt[b] * xs_v[b], eexp, NT)
            da_ref[0, b, 0:1, :] += _colsum(dadt * dt[b])
            dxs_ref[0, b] = (dxt[b] * dtx[b]).astype(dxs_ref.dtype)
            dbm_ref[0, b] = jnp.concatenate(db[b], axis=1).astype(dbm_ref.dtype)
            dcm_ref[0, b] = jnp.concatenate(dc[b], axis=1).astype(dcm_ref.dtype)
            ds_ref[b] = dsin[b]

    def rows(cols):
        return pl.BlockSpec((nb, CHUNK, cols), lambda d, s: (0, chunk(d, s), 0))

    def by_dir(cols):
        return pl.BlockSpec((1, nb, CHUNK, cols), lambda d, s: (d, 0, chunk(d, s), 0))

    grid = (2, nc)
    body, side_in, side_out, side_shapes, side_scratch, side_args = _side_wrap(body, 8, 5, 1, side, grid)
    outs = pl.pallas_call(
        body, name="ssd_scan_bwd" if side is None else "ssd_scan_bwd_comm", grid=grid,
        in_specs=[rows(384), rows(256), rows(256), by_dir(128), pl.BlockSpec((1, 8, 128), lambda d, s: (d, 0, 0)),
                  pl.BlockSpec((128, 384), lambda d, s: (0, 0)),
                  pl.BlockSpec((1, nb, CHUNK, 384), lambda d, s: (d * nc + chunk(d, s), 0, 0, 0)), rows(384)] + side_in,
        out_specs=[by_dir(384), by_dir(256), by_dir(256), by_dir(128),
                   pl.BlockSpec((1, nb, 8, 128), lambda d, s: (d, 0, 0, 0))] + side_out,
        out_shape=[jax.ShapeDtypeStruct((2, nb, T, 384), MXU), jax.ShapeDtypeStruct((2, nb, T, 256), MXU),
                   jax.ShapeDtypeStruct((2, nb, T, 256), MXU), jax.ShapeDtypeStruct((2, nb, T, 128), F32),
                   jax.ShapeDtypeStruct((2, nb, 8, 128), F32)] + side_shapes,
        scratch_shapes=[pltpu.VMEM((nb, CHUNK, 384), F32)] + side_scratch,
    )(xs.reshape(nb, T, 384), bm.reshape(nb, T, 256), cm.reshape(nb, T, 256), dtv.reshape(2, nb, T, 128), arow, eexp,
      hin, dy.reshape(nb, T, 384), *side_args)
    return (outs[0].reshape(2, R, 384), outs[1].reshape(2, R, 256), outs[2].reshape(2, R, 256),
            outs[3].reshape(2, R, 128), outs[4], list(outs[5:]))


def _group_rms(g):
    lane = lax.broadcasted_iota(jnp.int32, g.shape, 1)
    g0 = lane < 192
    gg = g * g
    s0 = jnp.sum(jnp.where(g0, gg, 0.0), axis=-1, keepdims=True)
    s1 = jnp.sum(gg, axis=-1, keepdims=True) - s0
    rstd = jnp.where(g0, lax.rsqrt(s0 * (1.0 / 192) + EPS), lax.rsqrt(s1 * (1.0 / 192) + EPS))
    return rstd, g0


def ssd_out_fwd(y2, xs, pz, dexp, nw):
    R = xs.shape[0]

    def body(y_ref, xs_ref, z_ref, d_ref, nw_ref, o_ref):
        z = z_ref[...].astype(F32)
        yy = y_ref[0] + y_ref[1] + xs_ref[...] * d_ref[...]
        g = yy * (z * _sigmoid(z))
        rstd, _ = _group_rms(g)
        o_ref[...] = g * rstd * nw_ref[...]

    return pl.pallas_call(
        body, name="ssd_out_fwd", grid=(R // TM,),
        in_specs=[pl.BlockSpec((2, TM, 384), lambda i: (0, i, 0)), _rowspec(384), _rowspec(384),
                  _fullspec((1, 384)), _fullspec((1, 384))],
        out_specs=_rowspec(384),
        out_shape=jax.ShapeDtypeStruct((R, 384), F32),
    )(y2, xs, pz, dexp, nw)


def ssd_out_bwd(dout, y2, xs, pz, dexp, nw):
    R = xs.shape[0]

    def body(do_ref, y_ref, xs_ref, z_ref, d_ref, nw_ref, dy_ref, dz_ref, dxs_ref, part_ref):
        z = z_ref[...].astype(F32)
        xs_v = xs_ref[...]
        yy = y_ref[0] + y_ref[1] + xs_v * d_ref[...]
        sig = _sigmoid(z)
        sz = z * sig
        g = yy * sz
        rstd, g0 = _group_rms(g)
        ghat = g * rstd
        do = do_ref[...]
        dgn = do * nw_ref[...]
        t = dgn * ghat
        t0 = jnp.sum(jnp.where(g0, t, 0.0), axis=-1, keepdims=True)
        t1 = jnp.sum(t, axis=-1, keepdims=True) - t0
        dg = rstd * (dgn - ghat * jnp.where(g0, t0, t1) * (1.0 / 192))
        dyy = dg * sz
        dy_ref[...] = dyy
        dz_ref[...] = (dg * yy * (sig * (1.0 + z * (1.0 - sig)))).astype(dz_ref.dtype)
        dxs_ref[...] = dyy * d_ref[...]
        part_ref[0] = jnp.concatenate([_colsum(do * ghat), _colsum(dyy * xs_v), jnp.zeros((6, 384), F32)], axis=0)

    return pl.pallas_call(
        body, name="ssd_out_bwd", grid=(R // TM,),
        in_specs=[_rowspec(384), pl.BlockSpec((2, TM, 384), lambda i: (0, i, 0)), _rowspec(384), _rowspec(384),
                  _fullspec((1, 384)), _fullspec((1, 384))],
        out_specs=[_rowspec(384), _rowspec(384), _rowspec(384), pl.BlockSpec((1, 8, 384), lambda i: (i, 0, 0))],
        out_shape=[jax.ShapeDtypeStruct((R, 384), F32), jax.ShapeDtypeStruct((R, 384), MXU),
                   jax.ShapeDtypeStruct((R, 384), F32), jax.ShapeDtypeStruct((R // TM, 8, 384), F32)],
    )(dout, y2, xs, pz, dexp, nw)


def ssd_prep_bwd_a(pxbc, plast, cw, cb, dtb, dxs_skip, dxs2, dbm2, dcm2, ddt2, blocks_per_sample):
    R = pxbc.shape[0]
    prev, nxt = _halo_specs(XBC, R, 8 * 4 // pxbc.dtype.itemsize)

    def body(cur_ref, prev_ref, nxt_ref, pl_ref, cw_ref, cb_ref, dtb_ref, dsk_ref, dxs_ref, dbm_ref, dcm_ref, ddt_ref,
             dpre_ref, dlast_ref, part_ref):
        i = pl.program_id(0)
        ext = _ext_rows(cur_ref[...], prev_ref[...], nxt_ref[...], i, blocks_per_sample)
        taps = _conv_taps(ext)
        co = _conv_out(taps, cw_ref, cb_ref)
        sig = _sigmoid(co)
        both = lambda ref: ref[0].astype(F32) + ref[1].astype(F32)
        up = jnp.concatenate([dsk_ref[...] + both(dxs_ref), both(dbm_ref), both(dcm_ref)], axis=1)
        dpre = up * (sig * (1.0 + co * (1.0 - sig)))
        dpre_ref[...] = dpre
        raw = pl_ref[...] + dtb_ref[...]
        lane = lax.broadcasted_iota(jnp.int32, raw.shape, 1)
        ddt = (pltpu.roll(ddt_ref[0], DT0, axis=1) + pltpu.roll(ddt_ref[1], DT0 + SSD_HEADS, axis=1))
        ddt = jnp.where(jnp.logical_and(lane >= DT0, lane < DT0 + 2 * SSD_HEADS), ddt * _sigmoid(raw), 0.0)
        dlast_ref[...] = ddt.astype(dlast_ref.dtype)
        rows = [_colsum(dpre * taps[k]) for k in range(4)]
        rows.append(_colsum(dpre))
        rows.append(jnp.concatenate([_colsum(ddt), jnp.zeros((1, XBC - 128), F32)], axis=1))
        rows.append(jnp.zeros((2, XBC), F32))
        part_ref[0] = jnp.concatenate(rows, axis=0)

    dirspec = lambda n: pl.BlockSpec((2, SB, n), lambda i: (0, i, 0))
    return pl.pallas_call(
        body, name="ssd_prep_bwd_a", grid=(R // SB,),
        in_specs=[_rowspec(XBC, SB), prev, nxt, _rowspec(128, SB), _fullspec((8, XBC)), _fullspec((1, XBC)),
                  _fullspec((1, 128)), _rowspec(384, SB), dirspec(384), dirspec(256), dirspec(256), dirspec(128)],
        out_specs=[_rowspec(XBC, SB), _rowspec(128, SB), pl.BlockSpec((1, 8, XBC), lambda i: (i, 0, 0))],
        out_shape=[jax.ShapeDtypeStruct((R, XBC), F32), jax.ShapeDtypeStruct((R, 128), MXU),
                   jax.ShapeDtypeStruct((R // SB, 8, XBC), F32)],
    )(pxbc, pxbc, pxbc, plast, cw, cb, dtb, dxs_skip, dxs2, dbm2, dcm2, ddt2)


def ssd_prep_bwd_b(dpre, cw, blocks_per_sample):
    R = dpre.shape[0]
    prev, nxt = _halo_specs(XBC, R)

    def body(cur_ref, prev_ref, nxt_ref, cw_ref, o_ref):
        i = pl.program_id(0)
        ext = _ext_rows(cur_ref[...], prev_ref[...], nxt_ref[...], i, blocks_per_sample)
        o_ref[...] = (cw_ref[0:1, :] * _shift(ext, 1) + cw_ref[1:2, :] * _shift(ext, 0)
                      + cw_ref[2:3, :] * _shift(ext, -1) + cw_ref[3:4, :] * _shift(ext, -2)).astype(o_ref.dtype)

    return pl.pallas_call(
        body, name="ssd_prep_bwd_b", grid=(R // SB,),
        in_specs=[_rowspec(XBC, SB), prev, nxt, _fullspec((8, XBC))],
        out_specs=_rowspec(XBC, SB),
        out_shape=jax.ShapeDtypeStruct((R, XBC), MXU),
    )(dpre, dpre, dpre, cw)


def _rope(u, cos, sa, sb):
    return u * cos + pltpu.roll(u, 120, axis=1) * sa + pltpu.roll(u, 8, axis=1) * sb


def _rope_t(du, cos, sa, sb):
    return du * cos + pltpu.roll(du * sa, 8, axis=1) + pltpu.roll(du * sb, 120, axis=1)


def mla_prep(pqa, pkva, plast, qnw, kvnw, wq, wk, wv, cos, sa, sb):
    R = pqa.shape[0]

    def body(qa_ref, kva_ref, pl_ref, qnw_ref, kvnw_ref, wq_ref, wk_ref, wv_ref, cos_ref, sa_ref, sb_ref,
             q_ref, k_ref, v_ref, cq_ref, ckv_ref):
        cos_v, sa_v, sb_v = cos_ref[...], sa_ref[...], sb_ref[...]
        xq, _ = _rms_hat(qa_ref[...].astype(F32))
        cq_ref[...] = (xq * qnw_ref[...]).astype(cq_ref.dtype)
        xkv, _ = _rms_hat(kva_ref[...].astype(F32))
        ckv_ref[...] = (xkv * kvnw_ref[...]).astype(ckv_ref.dtype)
        q = _dot(cq_ref[...], wq_ref[...])
        kn = _dot(ckv_ref[...], wk_ref[...])
        v_ref[...] = _dot(ckv_ref[...], wv_ref[...]).astype(v_ref.dtype)
        lane = lax.broadcasted_iota(jnp.int32, (TM, HP), 1)
        rope_lanes = jnp.logical_and(lane >= QK_NOPE, lane < QK_DIM)
        kr = _rope(jnp.where(rope_lanes, pltpu.roll(pl_ref[...], QK_NOPE, axis=1), 0.0), cos_v, sa_v, sb_v)
        for h in range(MLA_HEADS):
            cols = slice(h * HP, (h + 1) * HP)
            q_ref[:, cols] = (_rope(q[:, cols], cos_v, sa_v, sb_v) * Q_SCALE).astype(q_ref.dtype)
            k_ref[:, cols] = (kn[:, cols] + kr).astype(k_ref.dtype)

    return pl.pallas_call(
        body, name="mla_prep", grid=(R // TM,),
        in_specs=[_rowspec(256), _rowspec(256), _rowspec(128), _fullspec((1, 256)), _fullspec((1, 256)),
                  _fullspec((256, QW)), _fullspec((256, QW)), _fullspec((256, QW)),
                  _rowspec(HP), _rowspec(HP), _rowspec(HP)],
        out_specs=[_rowspec(QW), _rowspec(QW), _rowspec(QW), _rowspec(256), _rowspec(256)],
        out_shape=[jax.ShapeDtypeStruct((R, QW), MXU)] * 3 + [jax.ShapeDtypeStruct((R, 256), MXU)] * 2,
    )(pqa, pkva, plast, qnw, kvnw, wq, wk, wv, cos, sa, sb)


def mla_prep_bwd(dq, dk, dv, pqa, pkva, cq, ckv, qnw, kvnw, wq, wk, wv, cos, sa, sb):
    R = pqa.shape[0]

    def body(dq_ref, dk_ref, dv_ref, qa_ref, kva_ref, cq_ref, ckv_ref, qnw_ref, kvnw_ref, wq_ref, wk_ref, wv_ref,
             cos_ref, sa_ref, sb_ref, dqa_ref, dkva_ref, dkr_ref, dwq_ref, dwk_ref, dwv_ref, part_ref,
             dql_ref, dkm_ref, dvb_ref):
        @pl.when(pl.program_id(0) == 0)
        def _():
            dwq_ref[...] = jnp.zeros_like(dwq_ref)
            dwk_ref[...] = jnp.zeros_like(dwk_ref)
            dwv_ref[...] = jnp.zeros_like(dwv_ref)

        cos_v, sa_v, sb_v = cos_ref[...], sa_ref[...], sb_ref[...]
        lane = lax.broadcasted_iota(jnp.int32, (TM, HP), 1)
        rope_lanes = jnp.logical_and(lane >= QK_NOPE, lane < QK_DIM)
        dkr = jnp.zeros((TM, HP), F32)
        for h in range(MLA_HEADS):
            cols = slice(h * HP, (h + 1) * HP)
            dql_ref[:, cols] = (_rope_t(dq_ref[:, cols], cos_v, sa_v, sb_v) * ATT_SCALE).astype(dql_ref.dtype)
            dkh = dk_ref[:, cols] * LN2
            dkm_ref[:, cols] = jnp.where(lane < QK_NOPE, dkh, 0.0).astype(dkm_ref.dtype)
            dkr = dkr + jnp.where(rope_lanes, dkh, 0.0)
        dvb_ref[...] = dv_ref[...].astype(dvb_ref.dtype)
        dkr = jnp.where(rope_lanes, _rope_t(dkr, cos_v, sa_v, sb_v), 0.0)
        dkr_ref[...] = pltpu.roll(dkr, HP - QK_NOPE, axis=1).astype(dkr_ref.dtype)
        dwq_ref[...] += _dotg(cq_ref[...], dql_ref[...], TN)
        dwk_ref[...] += _dotg(ckv_ref[...], dkm_ref[...], TN)
        dwv_ref[...] += _dotg(ckv_ref[...], dvb_ref[...], TN)
        xq, rq = _rms_hat(qa_ref[...].astype(F32))
        dqa, dqnw = _rms_bwd(_dotg(dql_ref[...], wq_ref[...], NT), xq, rq, qnw_ref[...])
        dqa_ref[...] = dqa.astype(dqa_ref.dtype)
        xkv, rkv = _rms_hat(kva_ref[...].astype(F32))
        dckv = _dotg(dkm_ref[...], wk_ref[...], NT) + _dotg(dvb_ref[...], wv_ref[...], NT)
        dkva, dkvnw = _rms_bwd(dckv, xkv, rkv, kvnw_ref[...])
        dkva_ref[...] = dkva.astype(dkva_ref.dtype)
        part_ref[0] = jnp.concatenate([dqnw, dkvnw, jnp.zeros((6, 256), F32)], axis=0)

    return pl.pallas_call(
        body, name="mla_prep_bwd", grid=(R // TM,),
        in_specs=[_rowspec(QW), _rowspec(QW), _rowspec(QW), _rowspec(256), _rowspec(256), _rowspec(256), _rowspec(256),
                  _fullspec((1, 256)), _fullspec((1, 256)), _fullspec((256, QW)), _fullspec((256, QW)),
                  _fullspec((256, QW)), _rowspec(HP), _rowspec(HP), _rowspec(HP)],
        out_specs=[_rowspec(256), _rowspec(256), _rowspec(128), _fullspec((256, QW)), _fullspec((256, QW)),
                   _fullspec((256, QW)), pl.BlockSpec((1, 8, 256), lambda i: (i, 0, 0))],
        out_shape=[jax.ShapeDtypeStruct((R, 256), MXU), jax.ShapeDtypeStruct((R, 256), MXU),
                   jax.ShapeDtypeStruct((R, 128), MXU)] + [jax.ShapeDtypeStruct((256, QW), F32)] * 3
                  + [jax.ShapeDtypeStruct((R // TM, 8, 256), F32)],
        scratch_shapes=[pltpu.VMEM((TM, QW), MXU)] * 3,
    )(dq, dk, dv, pqa, pkva, cq, ckv, qnw, kvnw, wq, wk, wv, cos, sa, sb)


ATT_SCALE = QK_DIM ** -0.5
TQ = 256


LOG2E = 1.4426950408889634
LN2 = 0.6931471805599453
Q_SCALE = ATT_SCALE * LOG2E


def _key_chunks(T, n=2):
    unit = 256 if T % 256 == 0 else 128
    units = T // unit
    sizes = [(units // n + (1 if i < units % n else 0)) * unit for i in range(n)]
    return [(sum(sizes[:i]), sz) for i, sz in enumerate(sizes) if sz]


def attn_fwd(q, k, v, nb, T):
    R = q.shape[0]
    nq = T // TQ
    chunks = _key_chunks(T, 4)
    HEADS = range(3)

    def body(q_ref, k_ref, v_ref, o_ref, lse_ref):
        def lanes(h):
            return slice(h * HP, (h + 1) * HP)

        def logits(h, lo, n):
            return _dotg(q_ref[:, lanes(h)], k_ref[lo:lo + n, lanes(h)], NT)

        def weigh(h, s, lo, n):
            m = jnp.max(s, axis=-1, keepdims=True)
            p = jnp.exp2(s - m)
            return m, jnp.sum(p, axis=-1, keepdims=True), _dot(p.astype(MXU), v_ref[lo:lo + n, lanes(h)])

        def parts_of(ranges):
            out = [[] for _ in HEADS]
            s = [logits(h, *ranges[0]) for h in HEADS]
            for j, (lo, n) in enumerate(ranges):
                nxt = [logits(h, *ranges[j + 1]) for h in HEADS] if j + 1 < len(ranges) else None
                for h in HEADS:
                    out[h].append(weigh(h, s[h], lo, n))
                s = nxt
            return out

        def finish(all_parts):
            for h, parts in enumerate(all_parts):
                m = parts[0][0]
                for pm, _, _ in parts[1:]:
                    m = jnp.maximum(m, pm)
                l, o = 0.0, 0.0
                for pm, pl_, po in parts:
                    a = jnp.exp2(pm - m)
                    l = l + a * pl_
                    o = o + a * po
                o_ref[:, lanes(h)] = o / l
                lse_ref[:, lanes(h)] = jnp.broadcast_to(m + jnp.log(l) * LOG2E, (TQ, HP))

        i = pl.program_id(2)
        pl.when(i == 0)(lambda: finish(parts_of([(0, CTX)])))
        pl.when(i > 0)(lambda: finish(parts_of(chunks)))

    qspec = pl.BlockSpec((TQ, len(HEADS) * HP), lambda b, h, i: (b * nq + i, h))
    kspec = pl.BlockSpec((T, len(HEADS) * HP), lambda b, h, i: (b, h))
    return pl.pallas_call(
        body, name="attn_fwd", grid=(nb, MLA_HEADS // len(HEADS), nq),
        in_specs=[qspec, kspec, kspec], out_specs=[qspec, qspec],
        out_shape=[jax.ShapeDtypeStruct((R, QW), F32)] * 2,
        compiler_params=_cp(48),
    )(q, k, v)


def attn_bwd(q, k, v, o, lse, do, nb, T):
    R = q.shape[0]
    nq = T // TQ
    chunks = _key_chunks(T)

    def body(q_ref, k_ref, v_ref, o_ref, lse_ref, do_ref, dq_ref, dk_ref, dv_ref):
        i = pl.program_id(2)

        @pl.when(i == 0)
        def _():
            dk_ref[...] = jnp.zeros_like(dk_ref)
            dv_ref[...] = jnp.zeros_like(dv_ref)

        def run(chunks):
            for h in range(2):
                lanes = slice(h * HP, (h + 1) * HP)
                qv = q_ref[:, lanes]
                dov = do_ref[:, lanes]
                dob = dov.astype(MXU)
                delta = jnp.sum(dov * o_ref[:, lanes], axis=-1, keepdims=True)
                lse_v = lse_ref[:, h * HP:h * HP + 1]
                dq = 0.0
                for lo, n in chunks:
                    kv = k_ref[lo:lo + n, lanes]
                    p = jnp.exp2(_dotg(qv, kv, NT) - lse_v)
                    dp = _dotg(dob, v_ref[lo:lo + n, lanes], NT)
                    dsb = (p * (dp - delta)).astype(MXU)
                    dq = dq + _dot(dsb, kv)
                    dk_ref[lo:lo + n, lanes] += _dotg(dsb, qv, TN)
                    dv_ref[lo:lo + n, lanes] += _dotg(p.astype(MXU), dob, TN)
                dq_ref[:, lanes] = dq

        pl.when(i == 0)(lambda: run([(0, CTX)]))
        pl.when(i > 0)(lambda: run(chunks))

    qspec = pl.BlockSpec((TQ, 2 * HP), lambda b, h, i: (b * nq + i, h))
    kspec = pl.BlockSpec((T, 2 * HP), lambda b, h, i: (b, h))
    return pl.pallas_call(
        body, name="attn_bwd", grid=(nb, MLA_HEADS // 2, nq),
        in_specs=[qspec, kspec, kspec, qspec, qspec, qspec],
        out_specs=[qspec, kspec, kspec],
        out_shape=[jax.ShapeDtypeStruct((R, QW), F32)] * 3,
        compiler_params=_cp(56),
    )(q, k, v, o, lse, do)


def _pool_geometry(i, blocks_per_sample, seq):
    j = i % blocks_per_sample
    n = jnp.where(j == 0, CTX, seq)
    t0 = jnp.where(j == 0, 0, (j - 1) * SB) - HALO
    lane = lax.broadcasted_iota(jnp.int32, (SB + 2 * HALO, POOL_DIM), 1)
    t = lax.broadcasted_iota(jnp.int32, (SB + 2 * HALO, POOL_DIM), 0) + t0
    wh = jnp.where(lane < 64, 1, jnp.where(lane < 128, 2, jnp.where(lane < 192, 4, 8)))
    cnt = jnp.minimum(t + wh, n) - jnp.maximum(t - wh, 0)
    return lane, 1.0 / jnp.maximum(cnt, 1).astype(F32)


def _by_window(lane, c2, c4, c8, c16):
    return jnp.where(lane < 64, c2, jnp.where(lane < 128, c4, jnp.where(lane < 192, c8, c16)))


def _window_sums(ext, lane, first):
    n = ext.shape[0]
    r = lambda a, s: pltpu.roll(a, s % n, axis=0)
    c2 = ext + r(ext, first)
    c4 = r(c2, 1) + r(c2, -1)
    c8 = r(c4, 2) + r(c4, -2)
    c16 = r(c8, 4) + r(c8, -4)
    return _by_window(lane, c2, c4, c8, c16)


def _pool_delta(ext, lane, inv):
    return (_window_sums(ext, lane, 1) * inv - ext)[HALO:HALO + SB, :]


def pool_fwd(ppool, wbd, scale, blocks_per_sample, seq):
    R = ppool.shape[0]
    prev, nxt = _halo_specs(POOL_DIM, R)

    def body(cur_ref, prev_ref, nxt_ref, w_ref, s_ref, o_ref):
        i = pl.program_id(0)
        ext = _ext_rows(cur_ref[...], prev_ref[...], nxt_ref[...], i, blocks_per_sample)
        lane, inv = _pool_geometry(i, blocks_per_sample, seq)
        dlt = _pool_delta(ext, lane, inv)
        o_ref[...] = _dot(dlt.astype(MXU), w_ref[...]) * s_ref[...]

    return pl.pallas_call(
        body, name="pool_fwd", grid=(R // SB,),
        in_specs=[_rowspec(POOL_DIM, SB), prev, nxt, _fullspec((POOL_DIM, POOL_DIM)), _fullspec((1, POOL_DIM))],
        out_specs=_rowspec(POOL_DIM, SB),
        out_shape=jax.ShapeDtypeStruct((R, POOL_DIM), F32),
    )(ppool, ppool, ppool, wbd, scale)


def pool_bwd(ppool, dpool, wbd, scale, blocks_per_sample, seq):
    R = ppool.shape[0]
    prev, nxt = _halo_specs(POOL_DIM, R)

    def body(cur_ref, prev_ref, nxt_ref, dcur_ref, dprev_ref, dnxt_ref, w_ref, s_ref, du_ref, dw_ref, part_ref):
        i = pl.program_id(0)

        @pl.when(i == 0)
        def _():
            dw_ref[...] = jnp.zeros_like(dw_ref)

        ext = _ext_rows(cur_ref[...], prev_ref[...], nxt_ref[...], i, blocks_per_sample)
        lane, inv = _pool_geometry(i, blocks_per_sample, seq)
        dlt = _pool_delta(ext, lane, inv).astype(MXU)
        dy = dcur_ref[...]
        part_ref[0] = jnp.concatenate([_colsum(dy * _dot(dlt, w_ref[...])), jnp.zeros((7, POOL_DIM), F32)], axis=0)
        dyp = (dy * s_ref[...]).astype(MXU)
        dw_ref[...] += _dotg(dlt, dyp, TN)
        dext = _ext_rows(dy, dprev_ref[...], dnxt_ref[...], i, blocks_per_sample)
        dd = _dotg((dext * s_ref[...]).astype(MXU), w_ref[...], NT)
        du_ref[...] = (_window_sums(dd * inv, lane, -1) - dd)[HALO:HALO + SB, :].astype(du_ref.dtype)

    return pl.pallas_call(
        body, name="pool_bwd", grid=(R // SB,),
        in_specs=[_rowspec(POOL_DIM, SB), prev, nxt, _rowspec(POOL_DIM, SB), prev, nxt,
                  _fullspec((POOL_DIM, POOL_DIM)), _fullspec((1, POOL_DIM))],
        out_specs=[_rowspec(POOL_DIM, SB), _fullspec((POOL_DIM, POOL_DIM)),
                   pl.BlockSpec((1, 8, POOL_DIM), lambda i: (i, 0, 0))],
        out_shape=[jax.ShapeDtypeStruct((R, POOL_DIM), MXU), jax.ShapeDtypeStruct((POOL_DIM, POOL_DIM), F32),
                   jax.ShapeDtypeStruct((R // SB, 8, POOL_DIM), F32)],
    )(ppool, ppool, ppool, dpool, dpool, dpool, wbd, scale)


def adamw(w, g, m, v, name="adamw"):
    rows, cols = w.shape
    tr = rows
    for cand in (512, 256, 128, 64, 32, 16, 8):
        if rows % cand == 0:
            tr = cand
            break
    bc1 = 1.0 - ADAM_B1 ** ADAM_STEP
    bc2 = 1.0 - ADAM_B2 ** ADAM_STEP

    def body(w_ref, g_ref, m_ref, v_ref, d_ref, nm_ref, nv_ref):
        g_v = g_ref[...]
        nm = ADAM_B1 * m_ref[...] + (1.0 - ADAM_B1) * g_v
        nv = ADAM_B2 * v_ref[...] + (1.0 - ADAM_B2) * (g_v * g_v)
        nm_ref[...] = nm
        nv_ref[...] = nv
        d_ref[...] = -ADAM_LR * ((nm / bc1) / (jnp.sqrt(nv / bc2) + ADAM_EPS) + ADAM_WD * w_ref[...])

    spec = pl.BlockSpec((tr, cols), lambda i: (i, 0))
    return pl.pallas_call(
        body, name=name, grid=(rows // tr,),
        in_specs=[spec] * 4, out_specs=[spec] * 3,
        out_shape=[jax.ShapeDtypeStruct((rows, cols), F32)] * 3,
    )(w, g, m, v)


MODR = 32


def _silu(v):
    return v * _sigmoid(v)


def mod_fwd(cond, w, b):
    n = w.shape[1]

    def body(c_ref, w_ref, b_ref, o_ref):
        o_ref[...] = _dot(_silu(c_ref[...]).astype(MXU), w_ref[...].astype(MXU)) + b_ref[...]

    return pl.pallas_call(
        body, name="mod_fwd", out_shape=jax.ShapeDtypeStruct((MODR, n), F32),
        in_specs=[_fullspec((MODR, D)), _fullspec((D, n)), _fullspec((1, n))], out_specs=_fullspec((MODR, n)),
        grid=(1,), compiler_params=_cp(40),
    )(cond, w, b)


def mod_wgrad(cond, dm):
    n = dm.shape[1]

    def body(c_ref, d_ref, o_ref):
        o_ref[...] = _dotg(_silu(c_ref[...]).astype(MXU), d_ref[...].astype(MXU), TN)

    return pl.pallas_call(
        body, name="mod_wgrad", out_shape=jax.ShapeDtypeStruct((D, n), F32),
        in_specs=[_fullspec((MODR, D)), _fullspec((MODR, n))], out_specs=_fullspec((D, n)),
        grid=(1,), compiler_params=_cp(40),
    )(cond, dm)


def mod_dgrad(dm, w):
    n = w.shape[1]

    def body(d_ref, w_ref, o_ref):
        o_ref[...] = _dotg(d_ref[...].astype(MXU), w_ref[...].astype(MXU), NT)

    return pl.pallas_call(
        body, name="mod_dgrad", out_shape=jax.ShapeDtypeStruct((8, D), F32),
        in_specs=[_fullspec((8, n)), _fullspec((D, n))], out_specs=_fullspec((8, D)),
        grid=(1,), compiler_params=_cp(40),
    )(dm, w)


def sum_leading(a, name="sum_leading"):
    n, r, c = a.shape

    def body(a_ref, o_ref):
        acc = a_ref[0]
        for k in range(1, n):
            acc = acc + a_ref[k]
        o_ref[...] = acc

    return pl.pallas_call(
        body, name=name, out_shape=jax.ShapeDtypeStruct((r, c), F32),
        in_specs=[_fullspec((n, r, c))], out_specs=_fullspec((r, c)), grid=(1,),
    )(a)


MESH = pl.DeviceIdType.MESH
NDEV = 8
ANY = pl.BlockSpec(memory_space=pl.ANY)


def _place():
    return lax.axis_index("x"), lax.axis_index("y"), lax.axis_index("c")


def _other_chips(x, y):
    return [(1 - x, y), (x, 1 - y), (1 - x, 1 - y)]


def allgather_small(v, name):
    r, cols = v.shape

    def body(v_ref, o_ref, send_sems, recv_sems):
        x, y, c = _place()
        me = 4 * x + 2 * y + c
        o_ref[me] = v_ref[...]
        copies = []
        for rel in range(1, NDEV):
            peer = (1 - x if rel & 4 else x, 1 - y if rel & 2 else y, 1 - c if rel & 1 else c)
            cp = pltpu.make_async_remote_copy(src_ref=v_ref, dst_ref=o_ref.at[me], send_sem=send_sems.at[rel - 1],
                                              recv_sem=recv_sems.at[rel - 1], device_id=peer, device_id_type=MESH)
            cp.start()
            copies.append(cp)
        for cp in copies:
            cp.wait_recv()
        for cp in copies:
            cp.wait_send()

    return pl.pallas_call(
        body, name=name, out_shape=jax.ShapeDtypeStruct((NDEV, r, cols), F32),
        in_specs=[pl.BlockSpec(memory_space=pltpu.VMEM)], out_specs=pl.BlockSpec(memory_space=pltpu.VMEM),
        scratch_shapes=[pltpu.SemaphoreType.DMA((NDEV - 1,)), pltpu.SemaphoreType.DMA((NDEV - 1,))],
        compiler_params=_cp(40),
    )(v)


def _sems(n):
    return [pltpu.SemaphoreType.DMA((n,)), pltpu.SemaphoreType.DMA((n,))]


def allgather_chips(v, name):
    r, cols = v.shape

    def body(v_ref, o_ref, send_sems, recv_sems):
        x, y, c = _place()
        k = 2 * x + y
        o_ref[k] = v_ref[...]
        copies = []
        for j, (px, py) in enumerate(_other_chips(x, y)):
            cp = pltpu.make_async_remote_copy(src_ref=v_ref, dst_ref=o_ref.at[k], send_sem=send_sems.at[j],
                                              recv_sem=recv_sems.at[j], device_id=(px, py, c), device_id_type=MESH)
            cp.start()
            copies.append(cp)
        for cp in copies:
            cp.wait_recv()
        for cp in copies:
            cp.wait_send()

    return pl.pallas_call(
        body, name=name, out_shape=jax.ShapeDtypeStruct((4, r, cols), F32),
        in_specs=[pl.BlockSpec(memory_space=pltpu.VMEM)], out_specs=pl.BlockSpec(memory_space=pltpu.VMEM),
        scratch_shapes=_sems(3), compiler_params=_cp(40),
    )(v)


def gather_job(arrs):
    n = len(arrs)

    def copy(srcs, outs, sems, i, slot, kk, cc, to, from_src=False):
        hr = arrs[i].shape[0] // 2
        dst = outs[i].at[kk, pl.ds(cc * hr, hr), :]
        return pltpu.make_async_remote_copy(src_ref=srcs[i].at[pl.ds(cc * hr, hr), :] if from_src else dst, dst_ref=dst,
                                            send_sem=sems[0].at[slot * n + i], recv_sem=sems[1].at[slot * n + i],
                                            device_id=to, device_id_type=MESH)

    def start(srcs, outs, sems):
        x, y, c = _place()
        for j, (px, py) in enumerate(_other_chips(x, y)):
            for i in range(n):
                copy(srcs, outs, sems, i, j, 2 * x + y, c, (px, py, c), True).start()

    def finish(srcs, outs, sems):
        x, y, c = _place()
        sib = (x, y, 1 - c)
        chips = _other_chips(x, y)
        passed = []
        for j, (px, py) in enumerate(chips):
            for i in range(n):
                copy(srcs, outs, sems, i, j, 2 * px + py, c, (px, py, c)).wait_recv()
                cp = copy(srcs, outs, sems, i, 3 + j, 2 * px + py, c, sib)
                cp.start()
                passed.append(cp)
        for j, (px, py) in enumerate(chips):
            for i in range(n):
                copy(srcs, outs, sems, i, 3 + j, 2 * px + py, 1 - c, sib).wait_recv()
        for j, (px, py) in enumerate(chips):
            for i in range(n):
                copy(srcs, outs, sems, i, j, 2 * x + y, c, (px, py, c), True).wait_send()
        for cp in passed:
            cp.wait_send()

    return _NS(ins=list(arrs), out_shapes=[jax.ShapeDtypeStruct((4,) + a.shape, a.dtype) for a in arrs], nsem=6 * n,
               start=start, finish=finish)


def chip_swap_job(ss):
    n = len(ss)

    def copies(srcs, outs, sems):
        x, y, c = _place()
        return [pltpu.make_async_remote_copy(src_ref=srcs[i].at[2 * px + py], dst_ref=outs[i].at[j],
                                             send_sem=sems[0].at[j * n + i], recv_sem=sems[1].at[j * n + i],
                                             device_id=(px, py, c), device_id_type=MESH)
                for j, (px, py) in enumerate(_other_chips(x, y)) for i in range(n)]

    def start(srcs, outs, sems):
        for cp in copies(srcs, outs, sems):
            cp.start()

    def finish(srcs, outs, sems):
        for cp in copies(srcs, outs, sems):
            cp.wait()

    return _NS(ins=list(ss), out_shapes=[jax.ShapeDtypeStruct((3,) + s.shape[1:], s.dtype) for s in ss], nsem=3 * n,
               start=start, finish=finish)


def run_job(job, name):
    n, m = len(job.ins), len(job.out_shapes)

    def body(*refs):
        srcs, outs, sems = refs[:n], refs[n:n + m], refs[n + m:]
        job.start(srcs, outs, sems)
        job.finish(srcs, outs, sems)

    return pl.pallas_call(body, name=name, out_shape=job.out_shapes, in_specs=[ANY] * n, out_specs=[ANY] * m,
                          scratch_shapes=_sems(job.nsem))(*job.ins)


def core_swap_job(gs):
    n = len(gs)

    def copies(srcs, outs, sems):
        x, y, c = _place()
        return [pltpu.make_async_remote_copy(src_ref=srcs[i].at[:, pl.ds((1 - c) * (gs[i].shape[1] // 2), gs[i].shape[1] // 2), :],
                                             dst_ref=outs[i], send_sem=sems[0].at[i], recv_sem=sems[1].at[i],
                                             device_id=(x, y, 1 - c), device_id_type=MESH) for i in range(n)]

    def start(srcs, outs, sems):
        for cp in copies(srcs, outs, sems):
            cp.start()

    def finish(srcs, outs, sems):
        for cp in copies(srcs, outs, sems):
            cp.wait()

    return _NS(ins=list(gs), out_shapes=[jax.ShapeDtypeStruct((4, g.shape[1] // 2, g.shape[2]), g.dtype) for g in gs],
               nsem=n, start=start, finish=finish)


def add_half(g, r1, cidx, name):
    _, rows, cols = g.shape
    hr = rows // 2

    def body(c_ref, g_ref, r_ref, o_ref, ob_ref):
        s = g_ref[...] + r_ref[...]
        o_ref[...] = s
        ob_ref[...] = s.astype(BF16)

    blk = lambda f: pl.BlockSpec((1, hr, cols), f)
    return pl.pallas_call(
        body, name=name,
        out_shape=[jax.ShapeDtypeStruct((4, hr, cols), F32), jax.ShapeDtypeStruct((4, hr, cols), BF16)],
        grid_spec=pltpu.PrefetchScalarGridSpec(
            num_scalar_prefetch=1, grid=(4,),
            in_specs=[blk(lambda k, c_ref: (k, c_ref[0], 0)), blk(lambda k, c_ref: (k, 0, 0))],
            out_specs=[blk(lambda k, c_ref: (k, 0, 0)), blk(lambda k, c_ref: (k, 0, 0))]),
    )(cidx, g, r1)


def sum_parts(s1, r2, kidx, name):
    _, hr, cols = s1.shape

    def body(k_ref, s_ref, r_ref, o_ref):
        o_ref[...] = ((s_ref[0] + r_ref[0].astype(F32)) + r_ref[1].astype(F32)) + r_ref[2].astype(F32)

    return pl.pallas_call(
        body, name=name, out_shape=jax.ShapeDtypeStruct((hr, cols), F32),
        grid_spec=pltpu.PrefetchScalarGridSpec(
            num_scalar_prefetch=1, grid=(1,),
            in_specs=[pl.BlockSpec((1, hr, cols), lambda i, k_ref: (k_ref[0], 0, 0)),
                      pl.BlockSpec((3, hr, cols), lambda i, k_ref: (0, 0, 0))],
            out_specs=pl.BlockSpec((hr, cols), lambda i, k_ref: (0, 0))),
    )(kidx, s1, r2)


def swap_reduced_halves(hs):
    n = len(hs)

    def body(*refs):
        srcs, outs = refs[:n], refs[n:2 * n]
        send_sems, recv_sems = refs[2 * n:]
        x, y, c = _place()
        copies = []
        for i in range(n):
            cp = pltpu.make_async_remote_copy(src_ref=srcs[i], dst_ref=outs[i], send_sem=send_sems.at[i],
                                              recv_sem=recv_sems.at[i], device_id=(x, y, 1 - c), device_id_type=MESH)
            cp.start()
            copies.append(cp)
        for cp in copies:
            cp.wait()

    return pl.pallas_call(
        body, name="swap_reduced_halves", out_shape=[jax.ShapeDtypeStruct(h.shape, h.dtype) for h in hs],
        in_specs=[ANY] * n, out_specs=[ANY] * n, scratch_shapes=_sems(n),
    )(*hs)


def adamw_halves(w, m, v, own, oth, cidx, name):
    depth, rows, cols = w.shape
    hr = rows // 2
    tr = min(hr, 256)
    nblk = hr // tr
    bc1 = 1.0 - ADAM_B1 ** ADAM_STEP
    bc2 = 1.0 - ADAM_B2 ** ADAM_STEP

    def body(c_ref, w_ref, m_ref, v_ref, own0, own1, oth0, oth1, g_ref, d_ref, nm_ref, nv_ref):
        l = pl.program_id(0)
        hi = pl.program_id(1)
        mine = jnp.where(l == 0, own0[...], own1[...])
        other = jnp.where(l == 0, oth0[...], oth1[...])
        g_v = jnp.where(hi == c_ref[0], mine, other)
        nm = ADAM_B1 * m_ref[0] + (1.0 - ADAM_B1) * g_v
        nv = ADAM_B2 * v_ref[0] + (1.0 - ADAM_B2) * (g_v * g_v)
        g_ref[0] = g_v
        nm_ref[0] = nm
        nv_ref[0] = nv
        d_ref[0] = -ADAM_LR * ((nm / bc1) / (jnp.sqrt(nv / bc2) + ADAM_EPS) + ADAM_WD * w_ref[0])

    wspec = pl.BlockSpec((1, tr, cols), lambda l, hi, b, c_ref: (l, hi * nblk + b, 0))
    gspec = pl.BlockSpec((tr, cols), lambda l, hi, b, c_ref: (b, 0))
    assert depth == 2
    return pl.pallas_call(
        body, name=name, out_shape=[jax.ShapeDtypeStruct(w.shape, F32)] * 4,
        grid_spec=pltpu.PrefetchScalarGridSpec(
            num_scalar_prefetch=1, grid=(depth, 2, nblk),
            in_specs=[wspec] * 3 + [gspec] * 4, out_specs=[wspec] * 4),
    )(cidx, w, m, v, own[0], own[1], oth[0], oth[1])


class _NS:
    def __init__(self, **kw):
        self.__dict__.update(kw)


def _prep_in(win, conv_w, conv_b, dt_bias, a_log, ssd_d, ssd_nw, qnw, kvnw, pool_w, pool_scale, n1, n2):
    winp = jnp.concatenate([win[:, 0:384], win[:, 384:1280], win[:, 1292:1548], win[:, 1548:1804], win[:, 1836:2092],
                            win[:, 1804:1836], win[:, 1280:1292], jnp.zeros((D, NP - IN_COLS), win.dtype)], axis=1)
    wbd = (jnp.eye(4, dtype=F32)[:, None, :, None] * pool_w[:, :, None, :]).reshape(POOL_DIM, POOL_DIM).astype(MXU)
    a = -jnp.exp(a_log)
    return _NS(
        winp=winp, wbd=wbd,
        cw8=jnp.pad(conv_w, ((0, 4), (0, 0))), cb=conv_b[None],
        dtb=jnp.pad(dt_bias.reshape(1, 12), ((0, 0), (DT0, 128 - DT0 - 12))),
        arow=jnp.pad(a[:, None, :], ((0, 0), (0, 7), (0, 128 - SSD_HEADS))), a=a,
        dexp=jnp.repeat(ssd_d, SSD_P)[None], ssd_nw=ssd_nw[None], qnw=qnw[None], kvnw=kvnw[None],
        pscale=pool_scale[None], n1=n1[None], n2=n2[None])


def _prep_rest(wqb, wkvb, wout, w1, w2):
    wq = jnp.pad(wqb.reshape(256, MLA_HEADS, QK_DIM), ((0, 0), (0, 0), (0, HP - QK_DIM))).reshape(256, QW)
    kv3 = wkvb.reshape(256, MLA_HEADS, 128)
    wk = jnp.pad(kv3[:, :, :64], ((0, 0), (0, 0), (0, 64))).reshape(256, QW)
    wv = jnp.pad(kv3[:, :, 64:], ((0, 0), (0, 0), (0, 64))).reshape(256, QW)
    wo = jnp.concatenate([jnp.pad(wout[384:768].reshape(MLA_HEADS, 64, D), ((0, 0), (0, 64), (0, 0))).reshape(QW, D),
                          wout[0:384], wout[768:1024]], axis=0)
    return _NS(wq=wq, wk=wk, wv=wv, wo=wo, w1=w1, w2=w2)


def _prep_layer(win, wqb, wkvb, wout, w1, w2, *small):
    lw = _prep_in(win, *small)
    lw.__dict__.update(_prep_rest(wqb, wkvb, wout, w1, w2).__dict__)
    return lw


def _by_chip_cols(a):
    return jnp.stack([a[:, k * (a.shape[1] // 4):(k + 1) * (a.shape[1] // 4)] for k in range(4)])


def _by_chip_rows(a):
    return a.reshape(4, a.shape[0] // 4, a.shape[1])


def _unprep_in(dwinp):
    return jnp.concatenate([dwinp[:, 0:384], dwinp[:, 384:1280], dwinp[:, 2080:2092], dwinp[:, 1280:1536],
                            dwinp[:, 1536:1792], dwinp[:, 2048:2080], dwinp[:, 1792:2048]], axis=1)


def _unprep_rest(dwq, dwk, dwv, dwo):
    dwqb = dwq.reshape(256, MLA_HEADS, HP)[:, :, :QK_DIM].reshape(256, MLA_HEADS * QK_DIM)
    dwkvb = jnp.concatenate([dwk.reshape(256, MLA_HEADS, HP)[:, :, :64], dwv.reshape(256, MLA_HEADS, HP)[:, :, :64]],
                            axis=2).reshape(256, MLA_HEADS * 128)
    dwout = jnp.concatenate([dwo[QW:QW + 384], dwo[0:QW].reshape(MLA_HEADS, HP, D)[:, :64].reshape(384, D),
                             dwo[QW + 384:CAT]], axis=0)
    return dwqb, dwkvb, dwout


def _rope_tables(nb, N):
    t = jnp.arange(N, dtype=F32)
    row = jnp.floor(t / GRID_W)
    col = t - row * GRID_W
    inv = jnp.asarray(10000.0 ** (-np.arange(8, dtype=np.float32) / 8), F32)
    ang = jnp.stack([row[:, None] * inv, col[:, None] * inv], axis=1)
    cs, sn = jnp.cos(ang), jnp.sin(ang)
    zero = jnp.zeros_like(sn)
    lanes = lambda first, second: jnp.stack([first, second], axis=2).reshape(N, 32)
    pad = lambda a, fill: jnp.concatenate([jnp.full((N, 64), fill, F32), a, jnp.full((N, 32), fill, F32)], axis=1)
    tabs = []
    for tab, fill in ((pad(lanes(cs, cs), 1.0), 1.0), (pad(lanes(-sn, zero), 0.0), 0.0), (pad(lanes(zero, sn), 0.0), 0.0)):
        one = jnp.concatenate([jnp.full((CTX, 128), fill, F32), tab], axis=0)
        tabs.append(jnp.tile(one, (nb, 1)))
    return tabs


def _eexp():
    e = np.zeros((128, SSD_INNER), np.float32)
    for h in range(SSD_HEADS):
        e[h, h * SSD_P:(h + 1) * SSD_P] = 1.0
    return jnp.asarray(e)


class _NoHooks:
    def __init__(self, lws):
        self.lws = lws

    def weights_in(self, l):
        return _NS(**self.lws[l].__dict__)

    def weights_rest(self, l, scan_out):
        return self.lws[l]

    def job(self, where, l, early=None):
        return None

    def done(self, where, l, out):
        pass

    def layer_grads(self, l, g):
        pass


def _layer_fwd(X, bm, l, cst, hooks):
    nb, T, bps, N = cst.nb, cst.T, cst.bps, cst.N
    lw = hooks.weights_in(l)
    h1, pz, pxbc, pqa, pkva, ppool, plast = in_proj(X, bm, lw.n1, lw.winp)
    xs, bmat, cmat, dtv = ssd_prep(pxbc, plast, lw.cw8, lw.cb, lw.dtb, bps)
    y2, hin, out = ssd_scan_fwd(xs, bmat, cmat, dtv, lw.arow, cst.eexp, nb, T, hooks.job("fwd_scan", l))
    lw.__dict__.update(hooks.weights_rest(l, out).__dict__)
    ssd = ssd_out_fwd(y2, xs, pz, lw.dexp, lw.ssd_nw)
    q, k, v, cq, ckv = mla_prep(pqa, pkva, plast, lw.qnw, lw.kvnw, lw.wq, lw.wk, lw.wv, *cst.rope)
    attn, lse = attn_fwd(q, k, v, nb, T)
    pool = pool_fwd(ppool, lw.wbd, lw.pscale, bps, N)
    x1, mix, cat = mix_fwd(X, attn, ssd, pool, bm, lw.wo)
    x2, mo, r, h2, out = mlp_fwd(x1, bm, lw.n2, lw.w1, lw.w2, hooks.job("fwd_mlp", l))
    hooks.done("fwd_mlp", l, out)
    sv = _NS(X=X, h1=h1, pz=pz, pxbc=pxbc, pqa=pqa, pkva=pkva, ppool=ppool, plast=plast, xs=xs, bmat=bmat, cmat=cmat,
             dtv=dtv, y2=y2, hin=hin, q=q, k=k, v=v, cq=cq, ckv=ckv, attn=attn, lse=lse, x1=x1, mix=mix, cat=cat, mo=mo, r=r,
             h2=h2, lw=lw)
    return x2, sv


def _layer_bwd(dx2, bm, l, sv, cst, hooks):
    nb, T, bps, N = cst.nb, cst.T, cst.bps, cst.N
    lw = sv.lw
    dx1, du, dob, part_mlp, out = mlp_bwd(dx2, sv.x1, sv.mo, sv.r, bm, lw.n2, lw.w2, lw.w1, hooks.job("bwd_mlp", l))
    hooks.done("bwd_mlp", l, out)
    dw1 = mm_tn(sv.h2, du, name="wgrad_mlp1", col_blocks=True)
    dw2 = mm_tn(sv.r, dob, square_a=True, name="wgrad_mlp2")
    dattn, dssd, dpool, dwo, part_mix = mix_bwd(dx1, sv.mix, sv.cat, bm, lw.wo)
    dppool, dwbd, part_pool = pool_bwd(sv.ppool, dpool, lw.wbd, lw.pscale, bps, N)
    dq, dk, dv = attn_bwd(sv.q, sv.k, sv.v, sv.attn, sv.lse, dattn, nb, T)
    dpqa, dpkva, dkr, dwq, dwk, dwv, part_mla = mla_prep_bwd(dq, dk, dv, sv.pqa, sv.pkva, sv.cq, sv.ckv, lw.qnw, lw.kvnw,
                                                             lw.wq, lw.wk, lw.wv, *cst.rope)
    dwqb, dwkvb, dwout = _unprep_rest(dwq, dwk, dwv, dwo)
    early = dict(w_q_b=_by_chip_cols(dwqb), w_kv_b=_by_chip_cols(dwkvb), w_out=_by_chip_rows(dwout), w_mlp1=dw1,
                 w_mlp2=_by_chip_rows(dw2))
    dyy, dz, dxs_skip, part_so = ssd_out_bwd(dssd, sv.y2, sv.xs, sv.pz, lw.dexp, lw.ssd_nw)
    dxs2, dbm2, dcm2, ddt2, da, out = ssd_scan_bwd(sv.xs, sv.bmat, sv.cmat, sv.dtv, lw.arow, cst.eexp, sv.hin, dyy,
                                                   nb, T, hooks.job("bwd_scan", l, early))
    hooks.done("bwd_scan", l, out)
    dpre, dlast_dt, part_conv = ssd_prep_bwd_a(sv.pxbc, sv.plast, lw.cw8, lw.cb, lw.dtb, dxs_skip, dxs2, dbm2, dcm2,
                                               ddt2, bps)
    dpxbc = ssd_prep_bwd_b(dpre, lw.cw8, bps)
    dx, dwinp, part_in = in_proj_bwd(dx1, sv.X, sv.h1, dz, dpxbc, dpqa, dpkva, dppool, dkr, dlast_dt, bm, lw.n1, lw.winp)

    dmod = jnp.stack([part_in[:, 0], part_in[:, 1], part_mix[:, 0], part_mlp[:, 0], part_mlp[:, 1], part_mlp[:, 2]],
                     axis=1)
    dmod = dmod.reshape(nb, bps, 6, D)
    dm_rows = jnp.concatenate([jnp.sum(dmod[:, 1:], axis=1), jnp.sum(dmod[:, 0], axis=0)[None]], axis=0)
    da_dh = jnp.sum(da[:, :, 0, :SSD_HEADS], axis=1)
    conv_parts = jnp.sum(part_conv, axis=0)
    g = _NS(
        w_in=_by_chip_cols(_unprep_in(dwinp)), dm_rows=dm_rows.reshape(3, 6 * D), **early,
        norm1_w=jnp.sum(part_in[:, 2], axis=0), norm2_w=jnp.sum(part_mlp[:, 3], axis=0),
        conv_w=conv_parts[0:4], conv_b=conv_parts[4],
        dt_bias=conv_parts[5, DT0:DT0 + 12].reshape(2, SSD_HEADS), a_log=da_dh * lw.a,
        ssd_d=jnp.sum(jnp.sum(part_so[:, 1], axis=0).reshape(SSD_HEADS, SSD_P), axis=1),
        ssd_norm_w=jnp.sum(part_so[:, 0], axis=0),
        q_a_norm_w=jnp.sum(part_mla[:, 0], axis=0), kv_a_norm_w=jnp.sum(part_mla[:, 1], axis=0),
        pool_w=jnp.stack([dwbd[i * 64:(i + 1) * 64, i * 64:(i + 1) * 64] for i in range(4)]),
        pool_scale=jnp.sum(part_pool[:, 0], axis=0))
    hooks.layer_grads(l, g)
    return dx, g


def _local_step(x, ctx, tgt, bms, lws, fw, cst, hooks=None):
    nb, N = x.shape[0], x.shape[1]
    R = nb * cst.T
    hooks = _NoHooks(lws) if hooks is None else hooks
    X = jnp.concatenate([ctx, x], axis=1).reshape(R, D)
    saved = []
    for l in range(DEPTH):
        X, sv = _layer_fwd(X, bms[l], l, cst, hooks)
        saved.append(sv)
    dX, part_fin = final_loss(X, tgt.reshape(nb * N, D), fw[None], cst.bps)
    loss = (0.5 / D) * jnp.sum(part_fin[:, 1])
    dfw = jnp.sum(part_fin[:, 0], axis=0)
    grads = [None] * DEPTH
    for l in reversed(range(DEPTH)):
        dX, grads[l] = _layer_bwd(dX, bms[l], l, saved[l], cst, hooks)
    grad_x = dX.reshape(nb, cst.T, D)[:, CTX:, :]
    return loss, grad_x, grads, dfw


def _consts(nb, N):
    T = CTX + N
    bps = T // SB
    return _NS(nb=nb, N=N, T=T, bps=bps, eexp=_eexp(), rope=_rope_tables(nb, N))


def _block_mod(modrows, cst):
    rows = []
    for b in range(cst.nb):
        rows.append(modrows[cst.nb:cst.nb + 1])
        rows.append(jnp.broadcast_to(modrows[b:b + 1], (cst.bps - 1, 6, D)))
    return jnp.pad(jnp.concatenate(rows, axis=0), ((0, 0), (0, 2), (0, 0)))


SMALL = (("norm1_w", (2, D)), ("norm2_w", (2, D)), ("conv_w", (2, 4, XBC)), ("conv_b", (2, XBC)),
         ("dt_bias", (2, 2, 6)), ("a_log", (2, 2, 6)), ("ssd_d", (2, 6)), ("ssd_norm_w", (2, 384)),
         ("q_a_norm_w", (2, 256)), ("kv_a_norm_w", (2, 256)), ("pool_w", (2, 4, 64, 64)), ("pool_scale", (2, 256)),
         ("final_norm_w", (D,)), ("mod_b", (2, 6 * D)))
SMALL_ROWS = 64
DM_ROWS = 48


def _pack_small(vals):
    flat = jnp.concatenate([vals[n].reshape(-1) for n, _ in SMALL])
    return jnp.pad(flat, (0, SMALL_ROWS * D - flat.shape[0])).reshape(SMALL_ROWS, D)


def _unpack_small(p):
    flat = p.reshape(-1)
    out, off = {}, 0
    for n, shp in SMALL:
        size = int(np.prod(shp))
        out[n] = flat[off:off + size].reshape(shp)
        off += size
    return out


def cctx_grad(parts, c_ctx):
    def body(p_ref, c_ref, o_ref):
        acc = ((p_ref[0] + p_ref[1]) + p_ref[2]) + p_ref[3]
        v = c_ref[...]
        sig = _sigmoid(v)
        o_ref[...] = acc * (sig * (1.0 + v * (1.0 - sig)))

    return pl.pallas_call(
        body, name="cctx_grad", out_shape=jax.ShapeDtypeStruct((8, D), F32),
        in_specs=[_fullspec((4, 8, D)), _fullspec((1, D))], out_specs=_fullspec((8, D)), grid=(1,),
    )(parts, c_ctx)


def kernel(x, c, ctx, c_ctx, mod_w, mod_b, norm1_w, norm2_w, w_in, conv_w, conv_b, dt_bias, a_log, ssd_d, ssd_norm_w, q_a_norm_w, w_q_b, kv_a_norm_w, w_kv_b, pool_w, pool_scale, w_out, w_mlp1, w_mlp2, final_norm_w, loss_target, m_c_ctx, m_mod_w, m_mod_b, m_norm1_w, m_norm2_w, m_w_in, m_conv_w, m_conv_b, m_dt_bias, m_a_log, m_ssd_d, m_ssd_norm_w, m_q_a_norm_w, m_w_q_b, m_kv_a_norm_w, m_w_kv_b, m_pool_w, m_pool_scale, m_w_out, m_w_mlp1, m_w_mlp2, m_final_norm_w, v_c_ctx, v_mod_w, v_mod_b, v_norm1_w, v_norm2_w, v_w_in, v_conv_w, v_conv_b, v_dt_bias, v_a_log, v_ssd_d, v_ssd_norm_w, v_q_a_norm_w, v_w_q_b, v_kv_a_norm_w, v_w_kv_b, v_pool_w, v_pool_scale, v_w_out, v_w_mlp1, v_w_mlp2, v_final_norm_w):
    nb, N = x.shape[0], x.shape[1]
    cst = _consts(nb, N)
    xi, yi, ci = _place()
    me = 4 * xi + 2 * yi + ci
    kchip = 2 * xi + yi
    mcols = mod_w.shape[2]
    cshard = conv_w.shape[2]

    blk = jnp.zeros((16, D), F32).at[0:nb].set(c).at[8:16, 0:cshard].set(conv_w.reshape(8, cshard))
    g1 = allgather_small(blk, "gather_cond")
    cond = jnp.concatenate([g1[:, 0:nb].reshape(NDEV * nb, D), c_ctx[None],
                            jnp.zeros((MODR - NDEV * nb - 1, D), F32)], axis=0)
    conv_full = [jnp.concatenate([g1[2 * k, 8 + 4 * l:12 + 4 * l, 0:cshard] for k in range(4)], axis=1)
                 for l in range(DEPTH)]

    mb = [lax.dynamic_slice_in_dim(mod_b[l], kchip * mcols, mcols)[None] for l in range(DEPTH)]
    ms = jnp.concatenate([mod_fwd(cond, mod_w[l], mb[l]) for l in range(DEPTH)], axis=0)
    g2 = allgather_chips(ms, "gather_mod")
    bms = []
    for l in range(DEPTH):
        m_all = jnp.concatenate([g2[k, MODR * l:MODR * (l + 1)] for k in range(4)], axis=1)
        mine = jnp.concatenate([lax.dynamic_slice_in_dim(m_all, nb * me, nb), m_all[NDEV * nb:NDEV * nb + 1]], axis=0)
        bms.append(_block_mod(mine.reshape(nb + 1, 6, D), cst))

    assert DEPTH == 2
    big = (w_in, w_q_b, w_kv_b, w_out, w_mlp1, w_mlp2)
    names = ("w_in", "w_q_b", "w_kv_b", "w_out", "w_mlp1", "w_mlp2")
    concat_axis = dict(w_in=1, w_q_b=1, w_kv_b=1, w_out=0, w_mlp1=1, w_mlp2=0)
    cidx = jnp.reshape(ci, (1,)).astype(jnp.int32)
    kidx = jnp.reshape(kchip, (1,)).astype(jnp.int32)
    shards = [{n: a[l].astype(MXU) for n, a in zip(names, big)} for l in range(DEPTH)]

    def core_sums(gs, got=None):
        ns = list(gs)
        got = run_job(core_swap_job([gs[n] for n in ns]), "swap_core_halves") if got is None else got
        return {n: add_half(gs[n], r, cidx, "add_half_" + n) for n, r in zip(ns, got)}

    class Hooks:
        gathered = [dict(w_in=run_job(gather_job([shards[0]["w_in"]]), "gather_w_in")[0]), {}]
        core_sum = [{}, {}]
        received = [{}, {}]

        def whole(self, l, n):
            return jnp.concatenate([jnp.where(kchip == k, shards[l][n], self.gathered[l][n][k]) for k in range(4)],
                                   axis=concat_axis[n])

        def weights_in(self, l):
            return _prep_in(self.whole(l, "w_in"), conv_full[l], conv_b[l], dt_bias[l], a_log[l], ssd_d[l], ssd_norm_w[l],
                            q_a_norm_w[l], kv_a_norm_w[l], pool_w[l], pool_scale[l], norm1_w[l], norm2_w[l])

        def weights_rest(self, l, scan_out):
            if l == 0:
                self.gathered[0].update(zip(names[1:], scan_out))
            return _prep_rest(*[self.whole(l, n) for n in names[1:]])

        def job(self, where, l, early=None):
            if l == 1 and where == "bwd_scan":
                self.early1 = early
                return core_swap_job([early[n] for n in names[1:]])
            if l != 0:
                return None
            if where == "fwd_scan":
                return gather_job([shards[0][n] for n in names[1:]])
            if where == "fwd_mlp":
                return gather_job([shards[1][n] for n in names])
            if where == "bwd_mlp":
                return chip_swap_job([self.core_sum[1][n][1] for n in names])
            self.core_sum[0].update(core_sums(early))
            return chip_swap_job([self.core_sum[0][n][1] for n in names[1:]])

        def done(self, where, l, out):
            if l == 1 and where == "bwd_scan":
                self.core_sum[1].update(core_sums(self.early1, out))
            if l != 0:
                return
            if where == "fwd_mlp":
                self.gathered[1].update(zip(names, out))
            elif where == "bwd_mlp":
                self.received[1].update(zip(names, out))
            elif where == "bwd_scan":
                self.received[0].update(zip(names[1:], out))

        def layer_grads(self, l, g):
            if l == 1:
                self.core_sum[1].update(core_sums(dict(w_in=g.w_in)))
            else:
                self.core_sum[0].update(core_sums(dict(w_in=g.w_in)))
                self.received[0]["w_in"] = run_job(chip_swap_job([self.core_sum[0]["w_in"][1]]), "swap_w_in")[0]

    hooks = Hooks()
    loss_part, grad_x, grads, dfw = _local_step(x, ctx, loss_target, bms, None, final_norm_w, cst, hooks)
    loss = lax.psum(loss_part, ("x", "y", "c"))
    g_own = [sum_parts(hooks.core_sum[l][n][0], hooks.received[l][n], kidx, "sum_parts_" + n)
             for n in names for l in range(DEPTH)]
    g_oth = swap_reduced_halves(g_own)

    small = {n: jnp.stack([getattr(grads[l], n) for l in range(DEPTH)]) for n, _ in SMALL if n not in ("final_norm_w", "mod_b")}
    small["final_norm_w"] = dfw
    small["mod_b"] = jnp.stack([jnp.sum(grads[l].dm_rows, axis=0) for l in range(DEPTH)])
    dm = jnp.pad(jnp.concatenate([grads[l].dm_rows for l in range(DEPTH)], axis=0), ((0, 8 - 3 * DEPTH), (0, 0)))
    g3 = allgather_small(jnp.concatenate([_pack_small(small), dm.reshape(DM_ROWS, D)], axis=0), "gather_small")
    tot = sum_leading(g3, "sum_small")
    gsmall = _unpack_small(tot[0:SMALL_ROWS])
    ctx_sum = tot[SMALL_ROWS:].reshape(8, 6 * D)
    dm_dev = g3[:, SMALL_ROWS:].reshape(NDEV, 8, 6 * D)
    g_mod_w, dpart = [], jnp.zeros((8, D), F32)
    for l in range(DEPTH):
        dm_all = jnp.concatenate([dm_dev[:, 3 * l:3 * l + nb].reshape(NDEV * nb, 6 * D), ctx_sum[3 * l + nb:3 * l + nb + 1],
                                  jnp.zeros((MODR - NDEV * nb - 1, 6 * D), F32)], axis=0)
        g_mod_w.append(mod_wgrad(cond, lax.dynamic_slice_in_dim(dm_all, kchip * mcols, mcols, axis=1)))
        dctx = jnp.pad(lax.dynamic_slice_in_dim(ctx_sum[3 * l + nb:3 * l + nb + 1], kchip * mcols, mcols, axis=1), ((0, 7), (0, 0)))
        dpart = dpart + mod_dgrad(dctx, mod_w[l])
    g_c_ctx = cctx_grad(allgather_chips(dpart, "gather_cctx"), c_ctx[None])[0]

    res = {}
    moments = ((m_w_in, v_w_in), (m_w_q_b, v_w_q_b), (m_w_kv_b, v_w_kv_b), (m_w_out, v_w_out), (m_w_mlp1, v_w_mlp1),
               (m_w_mlp2, v_w_mlp2))
    for i, (n, w, (m, v)) in enumerate(zip(names, big, moments)):
        res[n] = tuple(adamw_halves(w, m, v, g_own[DEPTH * i:DEPTH * (i + 1)], g_oth[DEPTH * i:DEPTH * (i + 1)], cidx,
                                    "adamw_" + n))
    g_mw = jnp.stack(g_mod_w)
    r_mw = adamw(mod_w.reshape(-1, mcols), g_mw.reshape(-1, mcols), m_mod_w.reshape(-1, mcols),
                 v_mod_w.reshape(-1, mcols), name="adamw_mod_w")
    res["mod_w"] = (g_mw,) + tuple(a.reshape(mod_w.shape) for a in r_mw)

    given = dict(norm1_w=(norm1_w, m_norm1_w, v_norm1_w), norm2_w=(norm2_w, m_norm2_w, v_norm2_w),
                 conv_b=(conv_b, m_conv_b, v_conv_b), dt_bias=(dt_bias, m_dt_bias, v_dt_bias),
                 a_log=(a_log, m_a_log, v_a_log), ssd_d=(ssd_d, m_ssd_d, v_ssd_d),
                 ssd_norm_w=(ssd_norm_w, m_ssd_norm_w, v_ssd_norm_w), q_a_norm_w=(q_a_norm_w, m_q_a_norm_w, v_q_a_norm_w),
                 kv_a_norm_w=(kv_a_norm_w, m_kv_a_norm_w, v_kv_a_norm_w), pool_w=(pool_w, m_pool_w, v_pool_w),
                 pool_scale=(pool_scale, m_pool_scale, v_pool_scale),
                 final_norm_w=(final_norm_w, m_final_norm_w, v_final_norm_w), mod_b=(mod_b, m_mod_b, v_mod_b))
    zero_cw = jnp.zeros((2, 4, XBC), F32)
    packs = [_pack_small({n: (given[n][i] if n in given else zero_cw) for n, _ in SMALL}) for i in range(3)]
    r_small = [_unpack_small(a) for a in adamw(packs[0], tot[0:SMALL_ROWS], packs[1], packs[2], name="adamw_small")]
    for n in given:
        res[n] = (gsmall[n], r_small[0][n], r_small[1][n], r_small[2][n])

    g_cw = lax.dynamic_slice_in_dim(gsmall["conv_w"], kchip * cshard, cshard, axis=2)
    padcw = lambda a: jnp.pad(a.reshape(8, cshard), ((0, 0), (0, 256 - cshard)))
    r_cw = adamw(padcw(conv_w), padcw(g_cw), padcw(m_conv_w), padcw(v_conv_w), name="adamw_conv_w")
    res["conv_w"] = (g_cw,) + tuple(a[:, 0:cshard].reshape(conv_w.shape) for a in r_cw)
    r_cc = adamw(c_ctx.reshape(8, 128), g_c_ctx.reshape(8, 128), m_c_ctx.reshape(8, 128), v_c_ctx.reshape(8, 128),
                 name="adamw_c_ctx")
    res["c_ctx"] = (g_c_ctx,) + tuple(a.reshape(D) for a in r_cc)

    order = ("c_ctx", "mod_w", "mod_b", "norm1_w", "norm2_w", "w_in", "conv_w", "conv_b", "dt_bias", "a_log", "ssd_d",
             "ssd_norm_w", "q_a_norm_w", "w_q_b", "kv_a_norm_w", "w_kv_b", "pool_w", "pool_scale", "w_out", "w_mlp1",
             "w_mlp2", "final_norm_w")
    return (loss, grad_x) + tuple(res[n][i] for i in range(4) for n in order)
```

```python
import functools
import math

import numpy as np
import jax
import jax.numpy as jnp
from jax import lax
from jax.experimental import pallas as pl
from jax.experimental.pallas import tpu as pltpu

F32 = jnp.float32
BF16 = jnp.bfloat16
MXU = jnp.bfloat16

D = 1024
DEPTH = 2
GRID_W = 64
CTX = 256
EPS = 1e-6
SSD_HEADS = 6
SSD_P = 64
SSD_INNER = 384
SSD_N = 128
CHUNK = 128
XBC = 896
MLA_HEADS = 6
QK_NOPE = 64
QK_ROPE = 32
QK_DIM = 96
HP = 128
QW = MLA_HEADS * HP
POOL_DIM = 256
D_FF = 4096
FF_BLK = 1024
IN_COLS = 2092
NP = 2176
P_SPLITS = (384, 896, 256, 256, 256, 128)
DT0 = 32
CAT = QW + SSD_INNER + POOL_DIM

SB = 256
TM = 512
HALO = 8

ADAM_LR = 0.001
ADAM_B1 = 0.9
ADAM_B2 = 0.999
ADAM_EPS = 1e-08
ADAM_WD = 0.01
ADAM_STEP = 10

NT = (((1,), (1,)), ((), ()))
TN = (((0,), (0,)), ((), ()))


def _cp(vmem_mb=None):
    if vmem_mb is None:
        return pltpu.CompilerParams()
    return pltpu.CompilerParams(vmem_limit_bytes=vmem_mb << 20)


def _dot(a, b):
    return jnp.dot(a, b, preferred_element_type=F32)


def _dotg(a, b, dims):
    return lax.dot_general(a, b, dims, preferred_element_type=F32)


def _dot_hi(a, b, dims=None, sel_first=False):
    dims = (((1,), (0,)), ((), ())) if dims is None else dims
    v, s = (b, a) if sel_first else (a, b)
    hi = v.astype(BF16)
    lo = (v - hi.astype(F32)).astype(BF16)
    s = s.astype(BF16)
    if sel_first:
        return _dotg(s, hi, dims) + _dotg(s, lo, dims)
    return _dotg(hi, s, dims) + _dotg(lo, s, dims)


def _rms_hat(x):
    rstd = lax.rsqrt(jnp.mean(x * x, axis=-1, keepdims=True) + EPS)
    return x * rstd, rstd


def _rms_bwd(dn, xhat, rstd, w):
    dxhat = dn * w
    dx = rstd * (dxhat - xhat * jnp.mean(dxhat * xhat, axis=-1, keepdims=True))
    return dx, jnp.sum(dn * xhat, axis=0, keepdims=True)


def _sigmoid(z):
    return 1.0 / (1.0 + jnp.exp(-z))


def _colsum(a):
    return jnp.sum(a, axis=0, keepdims=True)


def _rowspec(cols, tm=TM):
    return pl.BlockSpec((tm, cols), lambda i: (i, 0))


def _fullspec(shape):
    n = len(shape)
    return pl.BlockSpec(shape, lambda *_: (0,) * n)


def _resident(shape):
    n = len(shape)
    return pl.BlockSpec(shape, lambda *_: (0,) * n, pipeline_mode=pl.Buffered(1))


def _halo_specs(cols, nrows, halo=HALO):
    per = SB // halo
    last = nrows // halo - 1
    prev = pl.BlockSpec((halo, cols), lambda i: (jnp.maximum(i * per - 1, 0), 0))
    nxt = pl.BlockSpec((halo, cols), lambda i: (jnp.minimum((i + 1) * per, last), 0))
    return prev, nxt


def _ext_rows(cur, prev, nxt, i, blocks_per_sample):
    j = i % blocks_per_sample
    first = jnp.logical_or(j == 0, j == 1)
    last = jnp.logical_or(j == 0, j == blocks_per_sample - 1)
    p = jnp.where(first, 0.0, prev.astype(F32))
    n = jnp.where(last, 0.0, nxt.astype(F32))
    return jnp.concatenate([p, cur.astype(F32), n], axis=0)


def _shift(ext, s):
    n = ext.shape[0]
    halo = (n - SB) // 2
    return pltpu.roll(ext, (-s) % n, axis=0)[halo:halo + SB, :]


def in_proj(x, bm, nw, w):
    R = x.shape[0]

    def body(x_ref, bm_ref, nw_ref, w_ref, h_ref, *outs):
        for s in range(TM // SB):
            rows = slice(s * SB, (s + 1) * SB)
            xhat, _ = _rms_hat(x_ref[rows, :])
            h = xhat * nw_ref[...] * (1.0 + bm_ref[s, 1:2, :]) + bm_ref[s, 0:1, :]
            h_ref[rows, :] = h.astype(h_ref.dtype)
        p = _dot(h_ref[...], w_ref[...])
        off = 0
        for o, n in zip(outs, P_SPLITS):
            o[...] = p[:, off:off + n].astype(o.dtype)
            off += n

    return pl.pallas_call(
        body, name="in_proj", grid=(R // TM,),
        in_specs=[_rowspec(D), pl.BlockSpec((TM // SB, 8, D), lambda i: (i, 0, 0)), _fullspec((1, D)),
                  _fullspec((D, NP))],
        out_specs=[_rowspec(D)] + [_rowspec(n) for n in P_SPLITS],
        out_shape=[jax.ShapeDtypeStruct((R, D), MXU)]
                  + [jax.ShapeDtypeStruct((R, n), dt) for n, dt in zip(P_SPLITS, (MXU, MXU, MXU, MXU, F32, F32))],
        compiler_params=_cp(56),
    )(x, bm, nw, w)


def in_proj_bwd(dx1, x, h, dz, dxbc, dqa, dkva, dpool, dkr, ddt, bm, nw, w):
    R = x.shape[0]

    def body(dx1_ref, x_ref, h_ref, dz_ref, dxbc_ref, dqa_ref, dkva_ref, dpool_ref, dkr_ref, ddt_ref, bm_ref, nw_ref,
             w_ref, dx_ref, dw_ref, part_ref, dp_ref):
        @pl.when(pl.program_id(0) == 0)
        def _():
            dw_ref[...] = jnp.zeros_like(dw_ref)

        dp_ref[:, 0:384] = dz_ref[...].astype(dp_ref.dtype)
        dp_ref[:, 384:1280] = dxbc_ref[...].astype(dp_ref.dtype)
        dp_ref[:, 1280:1536] = dqa_ref[...].astype(dp_ref.dtype)
        dp_ref[:, 1536:1792] = dkva_ref[...].astype(dp_ref.dtype)
        dp_ref[:, 1792:2048] = dpool_ref[...].astype(dp_ref.dtype)
        dp_ref[:, 2048:2176] = (dkr_ref[...] + ddt_ref[...]).astype(dp_ref.dtype)
        dw_ref[...] += _dotg(h_ref[...], dp_ref[...], TN)
        dh = _dotg(dp_ref[...], w_ref[...], NT)
        w = nw_ref[...]
        for s in range(TM // SB):
            rows = slice(s * SB, (s + 1) * SB)
            xhat, rstd = _rms_hat(x_ref[rows, :])
            dhs = dh[rows, :]
            sc1 = 1.0 + bm_ref[s, 1:2, :]
            dx, dnw = _rms_bwd(dhs * sc1, xhat, rstd, w)
            dx_ref[rows, :] = dx1_ref[rows, :] + dx
            part_ref[s] = jnp.concatenate(
                [_colsum(dhs), _colsum(dhs * xhat * w), dnw, jnp.zeros((5, D), F32)], axis=0)

    return pl.pallas_call(
        body, name="in_proj_bwd", grid=(R // TM,),
        in_specs=[_rowspec(D), _rowspec(D), _rowspec(D), _rowspec(384), _rowspec(896), _rowspec(256), _rowspec(256),
                  _rowspec(256), _rowspec(128), _rowspec(128),
                  pl.BlockSpec((TM // SB, 8, D), lambda i: (i, 0, 0)), _fullspec((1, D)), _resident((D, NP))],
        out_specs=[_rowspec(D), _fullspec((D, NP)), pl.BlockSpec((TM // SB, 8, D), lambda i: (i, 0, 0))],
        out_shape=[jax.ShapeDtypeStruct((R, D), F32), jax.ShapeDtypeStruct((D, NP), F32),
                   jax.ShapeDtypeStruct((R // SB, 8, D), F32)],
        scratch_shapes=[pltpu.VMEM((TM, NP), MXU)],
        compiler_params=_cp(56),
    )(dx1, x, h, dz, dxbc, dqa, dkva, dpool, dkr, ddt, bm, nw, w)


def mix_fwd(x, attn, ssd, pool, bm, wo):
    R = x.shape[0]

    def body(x_ref, a_ref, s_ref, p_ref, bm_ref, wo_ref, x1_ref, mix_ref, cat_ref):
        cat_ref[:, 0:QW] = a_ref[...].astype(cat_ref.dtype)
        cat_ref[:, QW:QW + SSD_INNER] = s_ref[...].astype(cat_ref.dtype)
        cat_ref[:, QW + SSD_INNER:CAT] = p_ref[...].astype(cat_ref.dtype)
        mix = _dot(cat_ref[...], wo_ref[...])
        mix_ref[...] = mix.astype(mix_ref.dtype)
        for s in range(TM // SB):
            rows = slice(s * SB, (s + 1) * SB)
            x1_ref[rows, :] = x_ref[rows, :] + bm_ref[s, 2:3, :] * mix[rows, :]

    return pl.pallas_call(
        body, name="mix_fwd", grid=(R // TM,),
        in_specs=[_rowspec(D), _rowspec(QW), _rowspec(SSD_INNER), _rowspec(POOL_DIM),
                  pl.BlockSpec((TM // SB, 8, D), lambda i: (i, 0, 0)), _fullspec((CAT, D))],
        out_specs=[_rowspec(D), _rowspec(D), _rowspec(CAT)],
        out_shape=[jax.ShapeDtypeStruct((R, D), F32), jax.ShapeDtypeStruct((R, D), MXU),
                   jax.ShapeDtypeStruct((R, CAT), MXU)],
        compiler_params=_cp(48),
    )(x, attn, ssd, pool, bm, wo)


def mix_bwd(dx1, mix, cat, bm, wo):
    R = dx1.shape[0]

    def body(dx1_ref, mix_ref, cat_ref, bm_ref, wo_ref, da_ref, ds_ref, dpl_ref, dw_ref, part_ref, dmb_ref):
        @pl.when(pl.program_id(0) == 0)
        def _():
            dw_ref[...] = jnp.zeros_like(dw_ref)

        for s in range(TM // SB):
            rows = slice(s * SB, (s + 1) * SB)
            d = dx1_ref[rows, :]
            dmb_ref[rows, :] = (d * bm_ref[s, 2:3, :]).astype(dmb_ref.dtype)
            part_ref[s] = jnp.concatenate([_colsum(d * mix_ref[rows, :].astype(F32)), jnp.zeros((7, D), F32)], axis=0)
        dw_ref[...] += _dotg(cat_ref[...], dmb_ref[...], TN)
        dcat = _dotg(dmb_ref[...], wo_ref[...], NT)
        da_ref[...] = dcat[:, 0:QW]
        ds_ref[...] = dcat[:, QW:QW + SSD_INNER]
        dpl_ref[...] = dcat[:, QW + SSD_INNER:CAT]

    return pl.pallas_call(
        body, name="mix_bwd", grid=(R // TM,),
        in_specs=[_rowspec(D), _rowspec(D), _rowspec(CAT), pl.BlockSpec((TM // SB, 8, D), lambda i: (i, 0, 0)),
                  _resident((CAT, D))],
        out_specs=[_rowspec(QW), _rowspec(SSD_INNER), _rowspec(POOL_DIM), _fullspec((CAT, D)),
                   pl.BlockSpec((TM // SB, 8, D), lambda i: (i, 0, 0))],
        out_shape=[jax.ShapeDtypeStruct((R, QW), F32), jax.ShapeDtypeStruct((R, SSD_INNER), F32),
                   jax.ShapeDtypeStruct((R, POOL_DIM), F32), jax.ShapeDtypeStruct((CAT, D), F32),
                   jax.ShapeDtypeStruct((R // SB, 8, D), F32)],
        scratch_shapes=[pltpu.VMEM((TM, D), MXU)],
        compiler_params=_cp(48),
    )(dx1, mix, cat, bm, wo)


def mlp_fwd(x1, bm, nw, w1, w2, side=None):
    R = x1.shape[0]

    def body(x1_ref, bm_ref, nw_ref, w1_ref, w2_ref, x2_ref, mo_ref, r_ref, h2_ref):
        for s in range(TM // SB):
            rows = slice(s * SB, (s + 1) * SB)
            xhat, _ = _rms_hat(x1_ref[rows, :])
            h = xhat * nw_ref[...] * (1.0 + bm_ref[s, 4:5, :]) + bm_ref[s, 3:4, :]
            h2_ref[rows, :] = h.astype(h2_ref.dtype)
        for j in range(D_FF // FF_BLK):
            cols = slice(j * FF_BLK, (j + 1) * FF_BLK)
            r = jnp.maximum(_dot(h2_ref[...], w1_ref[:, cols]), 0.0)
            r_ref[:, cols] = r.astype(r_ref.dtype)
            d = _dot((r * r).astype(MXU), w2_ref[cols, :])
            if j == 0:
                x2_ref[...] = d
            else:
                x2_ref[...] += d
        mo_ref[...] = x2_ref[...].astype(mo_ref.dtype)
        for s in range(TM // SB):
            rows = slice(s * SB, (s + 1) * SB)
            x2_ref[rows, :] = x1_ref[rows, :] + bm_ref[s, 5:6, :] * x2_ref[rows, :]

    grid = (R // TM,)
    body, side_in, side_out, side_shapes, side_scratch, side_args = _side_wrap(body, 5, 4, 0, side, grid)
    outs = pl.pallas_call(
        body, name="mlp_fwd" if side is None else "mlp_fwd_comm", grid=grid,
        in_specs=[_rowspec(D), pl.BlockSpec((TM // SB, 8, D), lambda i: (i, 0, 0)), _fullspec((1, D)),
                  _resident((D, D_FF)), _resident((D_FF, D))] + side_in,
        out_specs=[_rowspec(D), _rowspec(D), _rowspec(D_FF), _rowspec(D)] + side_out,
        out_shape=[jax.ShapeDtypeStruct((R, D), F32), jax.ShapeDtypeStruct((R, D), MXU),
                   jax.ShapeDtypeStruct((R, D_FF), BF16), jax.ShapeDtypeStruct((R, D), MXU)] + side_shapes,
        scratch_shapes=side_scratch,
        compiler_params=_cp(56),
    )(x1, bm, nw, w1, w2, *side_args)
    return tuple(outs[:4]) + (list(outs[4:]),)


def mlp_bwd(dx2, x1, mo, r, bm, nw, w2, w1, side=None):
    R = x1.shape[0]

    def body(dx2_ref, x1_ref, mo_ref, r_ref, bm_ref, nw_ref, w2_ref, w1_ref, dx1_ref, du_ref, dob_ref, part_ref,
             acc_ref):
        for s in range(TM // SB):
            rows = slice(s * SB, (s + 1) * SB)
            dob_ref[rows, :] = (dx2_ref[rows, :] * bm_ref[s, 5:6, :]).astype(dob_ref.dtype)
        for j in range(D_FF // FF_BLK):
            cols = slice(j * FF_BLK, (j + 1) * FF_BLK)
            du = _dotg(dob_ref[...], w2_ref[cols, :], NT) * (2.0 * r_ref[:, cols].astype(F32))
            du_ref[:, cols] = du.astype(du_ref.dtype)
            d = _dotg(du_ref[:, cols], w1_ref[:, cols], NT)
            if j == 0:
                acc_ref[...] = d
            else:
                acc_ref[...] += d
        w = nw_ref[...]
        for s in range(TM // SB):
            rows = slice(s * SB, (s + 1) * SB)
            xhat, rstd = _rms_hat(x1_ref[rows, :])
            dh = acc_ref[rows, :]
            dx, dnw = _rms_bwd(dh * (1.0 + bm_ref[s, 4:5, :]), xhat, rstd, w)
            d2 = dx2_ref[rows, :]
            dx1_ref[rows, :] = d2 + dx
            part_ref[s] = jnp.concatenate(
                [_colsum(dh), _colsum(dh * xhat * w), _colsum(d2 * mo_ref[rows, :].astype(F32)), dnw,
                 jnp.zeros((4, D), F32)], axis=0)

    grid = (R // TM,)
    body, side_in, side_out, side_shapes, side_scratch, side_args = _side_wrap(body, 8, 4, 1, side, grid)
    outs = pl.pallas_call(
        body, name="mlp_bwd" if side is None else "mlp_bwd_comm", grid=grid,
        in_specs=[_rowspec(D), _rowspec(D), _rowspec(D), _rowspec(D_FF),
                  pl.BlockSpec((TM // SB, 8, D), lambda i: (i, 0, 0)), _fullspec((1, D)),
                  _resident((D_FF, D)), _resident((D, D_FF))] + side_in,
        out_specs=[_rowspec(D), _rowspec(D_FF), _rowspec(D), pl.BlockSpec((TM // SB, 8, D), lambda i: (i, 0, 0))]
                  + side_out,
        out_shape=[jax.ShapeDtypeStruct((R, D), F32), jax.ShapeDtypeStruct((R, D_FF), MXU),
                   jax.ShapeDtypeStruct((R, D), MXU), jax.ShapeDtypeStruct((R // SB, 8, D), F32)] + side_shapes,
        scratch_shapes=[pltpu.VMEM((TM, D), F32)] + side_scratch,
        compiler_params=_cp(56),
    )(dx2, x1, mo, r, bm, nw, w2, w1, *side_args)
    return tuple(outs[:4]) + (list(outs[4:]),)


def mm_tn(a, b, square_a=False, name="mm_tn", col_blocks=False):
    R, M = a.shape
    N = b.shape[1]
    tm = M if M <= 1408 else 1024
    tn = N if N <= 2176 else 1024
    tk = next((c for c in ((2176, 1088, 512) if tm + tn <= 2048 else (1088, 512)) if R % c == 0), R)
    assert not col_blocks or tm == M

    def body(a_ref, b_ref, o_ref):
        @pl.when(pl.program_id(2) == 0)
        def _():
            o_ref[...] = jnp.zeros_like(o_ref)

        av = a_ref[...]
        if square_a:
            av = av.astype(F32)
            av = (av * av).astype(MXU)
        prod = _dotg(av.astype(MXU), b_ref[...].astype(MXU), TN)
        if col_blocks:
            o_ref[0] += prod
        else:
            o_ref[...] += prod

    if col_blocks:
        out_spec = pl.BlockSpec((1, tm, tn), lambda i, j, k: (j, 0, 0))
        out_shape = jax.ShapeDtypeStruct((N // tn, M, tn), F32)
    else:
        out_spec = pl.BlockSpec((tm, tn), lambda i, j, k: (i, j))
        out_shape = jax.ShapeDtypeStruct((M, N), F32)
    return pl.pallas_call(
        body, name=name, grid=(M // tm, N // tn, R // tk),
        in_specs=[pl.BlockSpec((tk, tm), lambda i, j, k: (k, i)), pl.BlockSpec((tk, tn), lambda i, j, k: (k, j))],
        out_specs=out_spec, out_shape=out_shape,
        compiler_params=_cp(48),
    )(a, b)


def final_loss(x, tgt, fw, blocks_per_sample):
    R = x.shape[0]
    nxb = blocks_per_sample - 1

    def body(x_ref, t_ref, fw_ref, dx_ref, part_ref):
        i = pl.program_id(0)
        is_ctx = (i % blocks_per_sample) == 0
        xhat, rstd = _rms_hat(x_ref[...])
        w = fw_ref[...]
        err = xhat * w - t_ref[...]
        dx, dfw = _rms_bwd(err * (1.0 / D), xhat, rstd, w)
        keep = jnp.where(is_ctx, 0.0, 1.0)
        dx_ref[...] = dx * keep
        part_ref[0] = jnp.concatenate([dfw * keep, _colsum(err * err) * keep, jnp.zeros((6, D), F32)], axis=0)

    def tmap(i):
        return ((i // blocks_per_sample) * nxb + jnp.maximum(i % blocks_per_sample - 1, 0), 0)

    return pl.pallas_call(
        body, name="final_loss", grid=(R // SB,),
        in_specs=[_rowspec(D, SB), pl.BlockSpec((SB, D), tmap), _fullspec((1, D))],
        out_specs=[_rowspec(D, SB), pl.BlockSpec((1, 8, D), lambda i: (i, 0, 0))],
        out_shape=[jax.ShapeDtypeStruct((R, D), F32), jax.ShapeDtypeStruct((R // SB, 8, D), F32)],
    )(x, tgt, fw)


def _softplus(v):
    return jnp.maximum(v, 0.0) + jnp.log(1.0 + jnp.exp(-jnp.abs(v)))


def _conv_taps(ext):
    return [_shift(ext, k - 1) for k in range(4)]


def _conv_out(taps, cw_ref, cb_ref):
    return (cb_ref[...] + cw_ref[0:1, :] * taps[0] + cw_ref[1:2, :] * taps[1] + cw_ref[2:3, :] * taps[2]
            + cw_ref[3:4, :] * taps[3])


def _dt_dir(v, d):
    lane = lax.broadcasted_iota(jnp.int32, v.shape, 1)
    return jnp.where(lane < SSD_HEADS, pltpu.roll(v, (128 - DT0 - SSD_HEADS * d) % 128, axis=1), 0.0)


def ssd_prep(pxbc, plast, cw, cb, dtb, blocks_per_sample):
    R = pxbc.shape[0]
    prev, nxt = _halo_specs(XBC, R, 8 * 4 // pxbc.dtype.itemsize)

    def body(cur_ref, prev_ref, nxt_ref, pl_ref, cw_ref, cb_ref, dtb_ref, xs_ref, bm_ref, cm_ref, dt_ref):
        i = pl.program_id(0)
        ext = _ext_rows(cur_ref[...], prev_ref[...], nxt_ref[...], i, blocks_per_sample)
        co = _conv_out(_conv_taps(ext), cw_ref, cb_ref)
        a = co * _sigmoid(co)
        xs_ref[...] = a[:, 0:384]
        bm_ref[...] = a[:, 384:640]
        cm_ref[...] = a[:, 640:896]
        sp = _softplus(pl_ref[...] + dtb_ref[...])
        dt_ref[0] = _dt_dir(sp, 0)
        dt_ref[1] = _dt_dir(sp, 1)

    return pl.pallas_call(
        body, name="ssd_prep", grid=(R // SB,),
        in_specs=[_rowspec(XBC, SB), prev, nxt, _rowspec(128, SB), _fullspec((8, XBC)), _fullspec((1, XBC)),
                  _fullspec((1, 128))],
        out_specs=[_rowspec(384, SB), _rowspec(256, SB), _rowspec(256, SB),
                   pl.BlockSpec((2, SB, 128), lambda i: (0, i, 0))],
        out_shape=[jax.ShapeDtypeStruct((R, 384), F32), jax.ShapeDtypeStruct((R, 256), F32),
                   jax.ShapeDtypeStruct((R, 256), F32), jax.ShapeDtypeStruct((2, R, 128), F32)],
    )(pxbc, pxbc, pxbc, plast, cw, cb, dtb)


def _chunk_index(d, s, nc):
    nctx = CTX // CHUNK
    back = jnp.where(s < nctx, nctx - 1 - s, nc + nctx - 1 - s)
    return jnp.where(d == 0, s, back)


def _scan_common(d, dt, arow, eexp, xs):
    ii = lax.broadcasted_iota(jnp.int32, (CHUNK, CHUNK), 0)
    jj = lax.broadcasted_iota(jnp.int32, (CHUNK, CHUNK), 1)
    mask = ((ii - jj) * (1 - 2 * d)) >= 0
    adt = dt * arow
    tmat = jnp.where(mask, 1.0, 0.0)
    cs = _dot_hi(tmat, adt, sel_first=True)
    tot = _colsum(adt)
    dtx = _dot_hi(dt, eexp)
    xt = xs * dtx
    ecs = jnp.exp(cs)
    ecx = _dot_hi(ecs, eexp)
    dte = jnp.exp(tot - cs)
    dtex = _dot_hi(dte, eexp)
    etot = jnp.exp(tot)
    etx = _dot_hi(jnp.broadcast_to(etot, (8, 128)), eexp)[0:1, :]
    return mask, tmat, adt, cs, tot, dtx, xt, ecs, ecx, dte, dtex, etot, etx


def _decay_matrix(mask, cs, cst, h):
    return jnp.exp(jnp.where(mask, cs[:, h:h + 1] - cst[h:h + 1, :], -1e30))


def _side_wrap(body, n_in, n_out, n_scratch, side, grid):
    if side is None:
        return body, [], [], [], [], []
    ni, no = len(side.ins), len(side.out_shapes)

    def wrapped(*refs):
        ins, refs = refs[:n_in], refs[n_in:]
        side_ins, refs = refs[:ni], refs[ni:]
        outs, refs = refs[:n_out], refs[n_out:]
        side_outs, refs = refs[:no], refs[no:]
        scratch, sems = refs[:n_scratch], refs[n_scratch:]
        ids = [pl.program_id(a) for a in range(len(grid))]
        first = functools.reduce(jnp.logical_and, [i == 0 for i in ids])
        last = functools.reduce(jnp.logical_and, [i == g - 1 for i, g in zip(ids, grid)])
        pl.when(first)(lambda: side.start(side_ins, side_outs, sems))
        if hasattr(side, "late"):
            at = [g - 1 for g in grid[:-1]] + [max(grid[-1] - 4, 0)]
            pl.when(functools.reduce(jnp.logical_and, [i == a for i, a in zip(ids, at)]))(
                lambda: side.late(side_ins, side_outs, sems))
        body(*ins, *outs, *scratch)
        pl.when(last)(lambda: side.finish(side_ins, side_outs, sems))

    return wrapped, [ANY] * ni, [ANY] * no, list(side.out_shapes), _sems(side.nsem), list(side.ins)


def ssd_scan_fwd(xs, bm, cm, dtv, arow, eexp, nb, T, side=None):
    R = xs.shape[0]
    nc = T // CHUNK
    B = range(nb)

    def body(xs_ref, bm_ref, cm_ref, dt_ref, a_ref, e_ref, y_ref, hin_ref, st_ref):
        d = pl.program_id(0)
        s = pl.program_id(1)

        @pl.when(s == 0)
        def _():
            st_ref[...] = jnp.zeros_like(st_ref)

        eexp = e_ref[...]
        com = [_scan_common(d, dt_ref[0, b], a_ref[0, 0:1, :], eexp, xs_ref[b]) for b in B]
        mask = com[0][0]
        cs = [com[b][3] for b in B]
        cst = [cs[b].T for b in B]
        sin = [st_ref[b] for b in B]
        for b in B:
            hin_ref[0, b] = sin[b]
        sb = [sin[b].astype(MXU) for b in B]
        xtb = [com[b][6].astype(MXU) for b in B]
        xw = [(com[b][6] * com[b][10]).astype(MXU) for b in B]
        g0 = lax.broadcasted_iota(jnp.int32, (CHUNK, SSD_INNER), 1) < 192
        lane = lax.broadcasted_iota(jnp.int32, (CHUNK, 128), 1)
        c = [[cm_ref[b, :, 0:128].astype(MXU), cm_ref[b, :, 128:256].astype(MXU)] for b in B]
        bq = [[bm_ref[b, :, 0:128].astype(MXU), bm_ref[b, :, 128:256].astype(MXU)] for b in B]
        y = [jnp.where(g0, _dot(c[b][0], sb[b]), _dot(c[b][1], sb[b])) * com[b][8] for b in B]
        cb = [[_dotg(c[b][g], bq[b][g], NT) for g in range(2)] for b in B]
        blocks = [[] for _ in B]
        for blk in range(3):
            acc = [None for _ in B]
            for hh in range(2):
                h = blk * 2 + hh
                for b in B:
                    m = (cb[b][h // 3] * _decay_matrix(mask, cs[b], cst[b], h)).astype(MXU)
                    res = _dot(m, xtb[b][:, blk * 128:(blk + 1) * 128])
                    acc[b] = res if hh == 0 else jnp.where(lane < 64, acc[b], res)
            for b in B:
                blocks[b].append(acc[b])
        for b in B:
            y_ref[0, b] = y[b] + jnp.concatenate(blocks[b], axis=1)
            st_ref[b] = sin[b] * com[b][12] + jnp.where(g0, _dotg(bq[b][0], xw[b], TN), _dotg(bq[b][1], xw[b], TN))

    def rows(cols):
        return pl.BlockSpec((nb, CHUNK, cols), lambda d, s: (0, _chunk_index(d, s, nc), 0))

    def by_dir(cols):
        return pl.BlockSpec((1, nb, CHUNK, cols), lambda d, s: (d, 0, _chunk_index(d, s, nc), 0))

    grid = (2, nc)
    body, side_in, side_out, side_shapes, side_scratch, side_args = _side_wrap(body, 6, 2, 1, side, grid)
    outs = pl.pallas_call(
        body, name="ssd_scan_fwd" if side is None else "ssd_scan_fwd_comm", grid=grid,
        in_specs=[rows(384), rows(256), rows(256), by_dir(128), pl.BlockSpec((1, 8, 128), lambda d, s: (d, 0, 0)),
                  pl.BlockSpec((128, 384), lambda d, s: (0, 0))] + side_in,
        out_specs=[by_dir(384),
                   pl.BlockSpec((1, nb, CHUNK, 384), lambda d, s: (d * nc + _chunk_index(d, s, nc), 0, 0, 0))] + side_out,
        out_shape=[jax.ShapeDtypeStruct((2, nb, T, 384), F32), jax.ShapeDtypeStruct((2 * nc, nb, CHUNK, 384), F32)]
                  + side_shapes,
        scratch_shapes=[pltpu.VMEM((nb, CHUNK, 384), F32)] + side_scratch,
    )(xs.reshape(nb, T, 384), bm.reshape(nb, T, 256), cm.reshape(nb, T, 256), dtv.reshape(2, nb, T, 128), arow, eexp,
      *side_args)
    return outs[0].reshape(2, R, 384), outs[1], list(outs[2:])


def ssd_scan_bwd(xs, bm, cm, dtv, arow, eexp, hin, dy, nb, T, side=None):
    R = xs.shape[0]
    nc = T // CHUNK
    B = range(nb)

    def chunk(d, s):
        return _chunk_index(d, nc - 1 - s, nc)

    def body(xs_ref, bm_ref, cm_ref, dt_ref, a_ref, e_ref, hin_ref, dy_ref,
             dxs_ref, dbm_ref, dcm_ref, ddt_ref, da_ref, ds_ref):
        d = pl.program_id(0)
        s = pl.program_id(1)

        @pl.when(s == 0)
        def _():
            ds_ref[...] = jnp.zeros_like(ds_ref)
            da_ref[...] = jnp.zeros_like(da_ref)

        eexp = e_ref[...]
        arow = a_ref[0, 0:1, :]
        dt = [dt_ref[0, b] for b in B]
        xs_v = [xs_ref[b] for b in B]
        com = [_scan_common(d, dt[b], arow, eexp, xs_v[b]) for b in B]
        mask, tmat = com[0][0], com[0][1]
        cs, dtx, xt, ecs, ecx, dte, dtex, etot, etx = [[com[b][i] for b in B] for i in (3, 5, 6, 7, 8, 9, 10, 11, 12)]
        cst = [cs[b].T for b in B]
        sin = [hin_ref[0, b] for b in B]
        sb = [sin[b].astype(MXU) for b in B]
        dsp = [ds_ref[b] for b in B]
        dyv = [dy_ref[b] for b in B]
        xtb = [xt[b].astype(MXU) for b in B]
        xw = [(xt[b] * dtex[b]).astype(MXU) for b in B]
        g0 = lax.broadcasted_iota(jnp.int32, (CHUNK, SSD_INNER), 1) < 192
        lane = lax.broadcasted_iota(jnp.int32, (CHUNK, 128), 1)
        sub = lax.broadcasted_iota(jnp.int32, (CHUNK, 128), 0)
        c = [[cm_ref[b, :, 0:128].astype(MXU), cm_ref[b, :, 128:256].astype(MXU)] for b in B]
        bq = [[bm_ref[b, :, 0:128].astype(MXU), bm_ref[b, :, 128:256].astype(MXU)] for b in B]

        cs_prod = [jnp.where(g0, _dot(c[b][0], sb[b]), _dot(c[b][1], sb[b])) for b in B]
        dcsp = [dyv[b] * ecx[b] for b in B]
        dcsp_g = [[jnp.where(g0, dcsp[b], 0.0).astype(MXU), jnp.where(g0, 0.0, dcsp[b]).astype(MXU)] for b in B]
        dcs = [_dot_hi(dyv[b] * cs_prod[b], eexp, NT) * ecs[b] for b in B]
        dc = [[_dotg(dcsp_g[b][g], sb[b], NT) for g in range(2)] for b in B]
        dsin = [_dotg(c[b][0], dcsp_g[b][0], TN) + _dotg(c[b][1], dcsp_g[b][1], TN) + dsp[b] * etx[b] for b in B]

        dtot = [_dot_hi(jnp.broadcast_to(_colsum(dsp[b] * sin[b]), (8, SSD_INNER)), eexp, NT)[0:1, :] * etot[b] for b in B]
        dsp_g = [[jnp.where(g0, dsp[b], 0.0).astype(MXU), jnp.where(g0, 0.0, dsp[b]).astype(MXU)] for b in B]
        dxw = [_dot(bq[b][0], dsp_g[b][0]) + _dot(bq[b][1], dsp_g[b][1]) for b in B]
        db = [[_dotg(xw[b], dsp_g[b][g], NT) for g in range(2)] for b in B]
        dxt = [dxw[b] * dtex[b] for b in B]
        ddte = [_dot_hi(dxw[b] * xt[b], eexp, NT) * dte[b] for b in B]
        dtot = [dtot[b] + _colsum(ddte[b]) for b in B]
        dcs = [dcs[b] - ddte[b] for b in B]

        cb = [[_dotg(c[b][g], bq[b][g], NT) for g in range(2)] for b in B]
        dg = [[jnp.zeros((CHUNK, CHUNK), F32), jnp.zeros((CHUNK, CHUNK), F32)] for _ in B]
        dcs_rows = [jnp.zeros((CHUNK, 128), F32) for _ in B]
        dxt_blocks = [[] for _ in B]
        for blk in range(3):
            acc = [jnp.zeros((CHUNK, 128), F32) for _ in B]
            for hh in range(2):
                h = blk * 2 + hh
                g = h // 3
                mine = (lane < 64) if hh == 0 else (lane >= 64)
                for b in B:
                    dyh = jnp.where(mine, dyv[b][:, blk * 128:(blk + 1) * 128], 0.0).astype(MXU)
                    lh = _decay_matrix(mask, cs[b], cst[b], h)
                    m = cb[b][g] * lh
                    dm = _dotg(dyh, xtb[b][:, blk * 128:(blk + 1) * 128], NT)
                    acc[b] = acc[b] + _dotg(m.astype(MXU), dyh, TN)
                    dg[b][g] = dg[b][g] + dm * lh
                    q = dm * m
                    dcs[b] = dcs[b] + jnp.where(lane == h, jnp.sum(q, axis=1, keepdims=True), 0.0)
                    dcs_rows[b] = dcs_rows[b] - jnp.where(sub == h, jnp.sum(q, axis=0, keepdims=True), 0.0)
            for b in B:
                dxt_blocks[b].append(acc[b])
        for b in B:
            dxt[b] = dxt[b] + jnp.concatenate(dxt_blocks[b], axis=1)
            for g in range(2):
                dgb = dg[b][g].astype(MXU)
                dc[b][g] = dc[b][g] + _dot(dgb, bq[b][g])
                db[b][g] = db[b][g] + _dotg(dgb, c[b][g], TN)
            dcs[b] = dcs[b] + dcs_rows[b].T

        for b in B:
            dadt = _dot_hi(tmat, dcs[b], TN, sel_first=True) + dtot[b]
            ddt_ref[0, b] = dadt * arow + _dot_hi(dxt[b] * xs_v[b], eexp, NT)
            da_ref[0, b, 0:1, :] += _colsum(dadt * dt[b])
            dxs_ref[0, b] = (dxt[b] * dtx[b]).astype(dxs_ref.dtype)
            dbm_ref[0, b] = jnp.concatenate(db[b], axis=1).astype(dbm_ref.dtype)
            dcm_ref[0, b] = jnp.concatenate(dc[b], axis=1).astype(dcm_ref.dtype)
            ds_ref[b] = dsin[b]

    def rows(cols):
        return pl.BlockSpec((nb, CHUNK, cols), lambda d, s: (0, chunk(d, s), 0))

    def by_dir(cols):
        return pl.BlockSpec((1, nb, CHUNK, cols), lambda d, s: (d, 0, chunk(d, s), 0))

    grid = (2, nc)
    body, side_in, side_out, side_shapes, side_scratch, side_args = _side_wrap(body, 8, 5, 1, side, grid)
    outs = pl.pallas_call(
        body, name="ssd_scan_bwd" if side is None else "ssd_scan_bwd_comm", grid=grid,
        in_specs=[rows(384), rows(256), rows(256), by_dir(128), pl.BlockSpec((1, 8, 128), lambda d, s: (d, 0, 0)),
                  pl.BlockSpec((128, 384), lambda d, s: (0, 0)),
                  pl.BlockSpec((1, nb, CHUNK, 384), lambda d, s: (d * nc + chunk(d, s), 0, 0, 0)), rows(384)] + side_in,
        out_specs=[by_dir(384), by_dir(256), by_dir(256), by_dir(128),
                   pl.BlockSpec((1, nb, 8, 128), lambda d, s: (d, 0, 0, 0))] + side_out,
        out_shape=[jax.ShapeDtypeStruct((2, nb, T, 384), MXU), jax.ShapeDtypeStruct((2, nb, T, 256), MXU),
                   jax.ShapeDtypeStruct((2, nb, T, 256), MXU), jax.ShapeDtypeStruct((2, nb, T, 128), F32),
                   jax.ShapeDtypeStruct((2, nb, 8, 128), F32)] + side_shapes,
        scratch_shapes=[pltpu.VMEM((nb, CHUNK, 384), F32)] + side_scratch,
    )(xs.reshape(nb, T, 384), bm.reshape(nb, T, 256), cm.reshape(nb, T, 256), dtv.reshape(2, nb, T, 128), arow, eexp,
      hin, dy.reshape(nb, T, 384), *side_args)
    return (outs[0].reshape(2, R, 384), outs[1].reshape(2, R, 256), outs[2].reshape(2, R, 256),
            outs[3].reshape(2, R, 128), outs[4], list(outs[5:]))


def _group_rms(g):
    lane = lax.broadcasted_iota(jnp.int32, g.shape, 1)
    g0 = lane < 192
    gg = g * g
    s0 = jnp.sum(jnp.where(g0, gg, 0.0), axis=-1, keepdims=True)
    s1 = jnp.sum(gg, axis=-1, keepdims=True) - s0
    rstd = jnp.where(g0, lax.rsqrt(s0 * (1.0 / 192) + EPS), lax.rsqrt(s1 * (1.0 / 192) + EPS))
    return rstd, g0


def ssd_out_fwd(y2, xs, pz, dexp, nw):
    R = xs.shape[0]

    def body(y_ref, xs_ref, z_ref, d_ref, nw_ref, o_ref):
        z = z_ref[...].astype(F32)
        yy = y_ref[0] + y_ref[1] + xs_ref[...] * d_ref[...]
        g = yy * (z * _sigmoid(z))
        rstd, _ = _group_rms(g)
        o_ref[...] = g * rstd * nw_ref[...]

    return pl.pallas_call(
        body, name="ssd_out_fwd", grid=(R // TM,),
        in_specs=[pl.BlockSpec((2, TM, 384), lambda i: (0, i, 0)), _rowspec(384), _rowspec(384),
                  _fullspec((1, 384)), _fullspec((1, 384))],
        out_specs=_rowspec(384),
        out_shape=jax.ShapeDtypeStruct((R, 384), F32),
    )(y2, xs, pz, dexp, nw)


def ssd_out_bwd(dout, y2, xs, pz, dexp, nw):
    R = xs.shape[0]

    def body(do_ref, y_ref, xs_ref, z_ref, d_ref, nw_ref, dy_ref, dz_ref, dxs_ref, part_ref):
        z = z_ref[...].astype(F32)
        xs_v = xs_ref[...]
        yy = y_ref[0] + y_ref[1] + xs_v * d_ref[...]
        sig = _sigmoid(z)
        sz = z * sig
        g = yy * sz
        rstd, g0 = _group_rms(g)
        ghat = g * rstd
        do = do_ref[...]
        dgn = do * nw_ref[...]
        t = dgn * ghat
        t0 = jnp.sum(jnp.where(g0, t, 0.0), axis=-1, keepdims=True)
        t1 = jnp.sum(t, axis=-1, keepdims=True) - t0
        dg = rstd * (dgn - ghat * jnp.where(g0, t0, t1) * (1.0 / 192))
        dyy = dg * sz
        dy_ref[...] = dyy
        dz_ref[...] = (dg * yy * (sig * (1.0 + z * (1.0 - sig)))).astype(dz_ref.dtype)
        dxs_ref[...] = dyy * d_ref[...]
        part_ref[0] = jnp.concatenate([_colsum(do * ghat), _colsum(dyy * xs_v), jnp.zeros((6, 384), F32)], axis=0)

    return pl.pallas_call(
        body, name="ssd_out_bwd", grid=(R // TM,),
        in_specs=[_rowspec(384), pl.BlockSpec((2, TM, 384), lambda i: (0, i, 0)), _rowspec(384), _rowspec(384),
                  _fullspec((1, 384)), _fullspec((1, 384))],
        out_specs=[_rowspec(384), _rowspec(384), _rowspec(384), pl.BlockSpec((1, 8, 384), lambda i: (i, 0, 0))],
        out_shape=[jax.ShapeDtypeStruct((R, 384), F32), jax.ShapeDtypeStruct((R, 384), MXU),
                   jax.ShapeDtypeStruct((R, 384), F32), jax.ShapeDtypeStruct((R // TM, 8, 384), F32)],
    )(dout, y2, xs, pz, dexp, nw)


def ssd_prep_bwd_a(pxbc, plast, cw, cb, dtb, dxs_skip, dxs2, dbm2, dcm2, ddt2, blocks_per_sample):
    R = pxbc.shape[0]
    prev, nxt = _halo_specs(XBC, R, 8 * 4 // pxbc.dtype.itemsize)

    def body(cur_ref, prev_ref, nxt_ref, pl_ref, cw_ref, cb_ref, dtb_ref, dsk_ref, dxs_ref, dbm_ref, dcm_ref, ddt_ref,
             dpre_ref, dlast_ref, part_ref):
        i = pl.program_id(0)
        ext = _ext_rows(cur_ref[...], prev_ref[...], nxt_ref[...], i, blocks_per_sample)
        taps = _conv_taps(ext)
        co = _conv_out(taps, cw_ref, cb_ref)
        sig = _sigmoid(co)
        both = lambda ref: ref[0].astype(F32) + ref[1].astype(F32)
        up = jnp.concatenate([dsk_ref[...] + both(dxs_ref), both(dbm_ref), both(dcm_ref)], axis=1)
        dpre = up * (sig * (1.0 + co * (1.0 - sig)))
        dpre_ref[...] = dpre
        raw = pl_ref[...] + dtb_ref[...]
        lane = lax.broadcasted_iota(jnp.int32, raw.shape, 1)
        ddt = (pltpu.roll(ddt_ref[0], DT0, axis=1) + pltpu.roll(ddt_ref[1], DT0 + SSD_HEADS, axis=1))
        ddt = jnp.where(jnp.logical_and(lane >= DT0, lane < DT0 + 2 * SSD_HEADS), ddt * _sigmoid(raw), 0.0)
        dlast_ref[...] = ddt.astype(dlast_ref.dtype)
        rows = [_colsum(dpre * taps[k]) for k in range(4)]
        rows.append(_colsum(dpre))
        rows.append(jnp.concatenate([_colsum(ddt), jnp.zeros((1, XBC - 128), F32)], axis=1))
        rows.append(jnp.zeros((2, XBC), F32))
        part_ref[0] = jnp.concatenate(rows, axis=0)

    dirspec = lambda n: pl.BlockSpec((2, SB, n), lambda i: (0, i, 0))
    return pl.pallas_call(
        body, name="ssd_prep_bwd_a", grid=(R // SB,),
        in_specs=[_rowspec(XBC, SB), prev, nxt, _rowspec(128, SB), _fullspec((8, XBC)), _fullspec((1, XBC)),
                  _fullspec((1, 128)), _rowspec(384, SB), dirspec(384), dirspec(256), dirspec(256), dirspec(128)],
        out_specs=[_rowspec(XBC, SB), _rowspec(128, SB), pl.BlockSpec((1, 8, XBC), lambda i: (i, 0, 0))],
        out_shape=[jax.ShapeDtypeStruct((R, XBC), F32), jax.ShapeDtypeStruct((R, 128), MXU),
                   jax.ShapeDtypeStruct((R // SB, 8, XBC), F32)],
    )(pxbc, pxbc, pxbc, plast, cw, cb, dtb, dxs_skip, dxs2, dbm2, dcm2, ddt2)


def ssd_prep_bwd_b(dpre, cw, blocks_per_sample):
    R = dpre.shape[0]
    prev, nxt = _halo_specs(XBC, R)

    def body(cur_ref, prev_ref, nxt_ref, cw_ref, o_ref):
        i = pl.program_id(0)
        ext = _ext_rows(cur_ref[...], prev_ref[...], nxt_ref[...], i, blocks_per_sample)
        o_ref[...] = (cw_ref[0:1, :] * _shift(ext, 1) + cw_ref[1:2, :] * _shift(ext, 0)
                      + cw_ref[2:3, :] * _shift(ext, -1) + cw_ref[3:4, :] * _shift(ext, -2)).astype(o_ref.dtype)

    return pl.pallas_call(
        body, name="ssd_prep_bwd_b", grid=(R // SB,),
        in_specs=[_rowspec(XBC, SB), prev, nxt, _fullspec((8, XBC))],
        out_specs=_rowspec(XBC, SB),
        out_shape=jax.ShapeDtypeStruct((R, XBC), MXU),
    )(dpre, dpre, dpre, cw)


def _rope(u, cos, sa, sb):
    return u * cos + pltpu.roll(u, 120, axis=1) * sa + pltpu.roll(u, 8, axis=1) * sb


def _rope_t(du, cos, sa, sb):
    return du * cos + pltpu.roll(du * sa, 8, axis=1) + pltpu.roll(du * sb, 120, axis=1)


def mla_prep(pqa, pkva, plast, qnw, kvnw, wq, wk, wv, cos, sa, sb):
    R = pqa.shape[0]

    def body(qa_ref, kva_ref, pl_ref, qnw_ref, kvnw_ref, wq_ref, wk_ref, wv_ref, cos_ref, sa_ref, sb_ref,
             q_ref, k_ref, v_ref, cq_ref, ckv_ref):
        cos_v, sa_v, sb_v = cos_ref[...], sa_ref[...], sb_ref[...]
        xq, _ = _rms_hat(qa_ref[...].astype(F32))
        cq_ref[...] = (xq * qnw_ref[...]).astype(cq_ref.dtype)
        xkv, _ = _rms_hat(kva_ref[...].astype(F32))
        ckv_ref[...] = (xkv * kvnw_ref[...]).astype(ckv_ref.dtype)
        q = _dot(cq_ref[...], wq_ref[...])
        kn = _dot(ckv_ref[...], wk_ref[...])
        v_ref[...] = _dot(ckv_ref[...], wv_ref[...]).astype(v_ref.dtype)
        lane = lax.broadcasted_iota(jnp.int32, (TM, HP), 1)
        rope_lanes = jnp.logical_and(lane >= QK_NOPE, lane < QK_DIM)
        kr = _rope(jnp.where(rope_lanes, pltpu.roll(pl_ref[...], QK_NOPE, axis=1), 0.0), cos_v, sa_v, sb_v)
        for h in range(MLA_HEADS):
            cols = slice(h * HP, (h + 1) * HP)
            q_ref[:, cols] = (_rope(q[:, cols], cos_v, sa_v, sb_v) * Q_SCALE).astype(q_ref.dtype)
            k_ref[:, cols] = (kn[:, cols] + kr).astype(k_ref.dtype)

    return pl.pallas_call(
        body, name="mla_prep", grid=(R // TM,),
        in_specs=[_rowspec(256), _rowspec(256), _rowspec(128), _fullspec((1, 256)), _fullspec((1, 256)),
                  _fullspec((256, QW)), _fullspec((256, QW)), _fullspec((256, QW)),
                  _rowspec(HP), _rowspec(HP), _rowspec(HP)],
        out_specs=[_rowspec(QW), _rowspec(QW), _rowspec(QW), _rowspec(256), _rowspec(256)],
        out_shape=[jax.ShapeDtypeStruct((R, QW), MXU)] * 3 + [jax.ShapeDtypeStruct((R, 256), MXU)] * 2,
    )(pqa, pkva, plast, qnw, kvnw, wq, wk, wv, cos, sa, sb)


def mla_prep_bwd(dq, dk, dv, pqa, pkva, cq, ckv, qnw, kvnw, wq, wk, wv, cos, sa, sb):
    R = pqa.shape[0]

    def body(dq_ref, dk_ref, dv_ref, qa_ref, kva_ref, cq_ref, ckv_ref, qnw_ref, kvnw_ref, wq_ref, wk_ref, wv_ref,
             cos_ref, sa_ref, sb_ref, dqa_ref, dkva_ref, dkr_ref, dwq_ref, dwk_ref, dwv_ref, part_ref,
             dql_ref, dkm_ref, dvb_ref):
        @pl.when(pl.program_id(0) == 0)
        def _():
            dwq_ref[...] = jnp.zeros_like(dwq_ref)
            dwk_ref[...] = jnp.zeros_like(dwk_ref)
            dwv_ref[...] = jnp.zeros_like(dwv_ref)

        cos_v, sa_v, sb_v = cos_ref[...], sa_ref[...], sb_ref[...]
        lane = lax.broadcasted_iota(jnp.int32, (TM, HP), 1)
        rope_lanes = jnp.logical_and(lane >= QK_NOPE, lane < QK_DIM)
        dkr = jnp.zeros((TM, HP), F32)
        for h in range(MLA_HEADS):
            cols = slice(h * HP, (h + 1) * HP)
            dql_ref[:, cols] = (_rope_t(dq_ref[:, cols], cos_v, sa_v, sb_v) * ATT_SCALE).astype(dql_ref.dtype)
            dkh = dk_ref[:, cols] * LN2
            dkm_ref[:, cols] = jnp.where(lane < QK_NOPE, dkh, 0.0).astype(dkm_ref.dtype)
            dkr = dkr + jnp.where(rope_lanes, dkh, 0.0)
        dvb_ref[...] = dv_ref[...].astype(dvb_ref.dtype)
        dkr = jnp.where(rope_lanes, _rope_t(dkr, cos_v, sa_v, sb_v), 0.0)
        dkr_ref[...] = pltpu.roll(dkr, HP - QK_NOPE, axis=1).astype(dkr_ref.dtype)
        dwq_ref[...] += _dotg(cq_ref[...], dql_ref[...], TN)
        dwk_ref[...] += _dotg(ckv_ref[...], dkm_ref[...], TN)
        dwv_ref[...] += _dotg(ckv_ref[...], dvb_ref[...], TN)
        xq, rq = _rms_hat(qa_ref[...].astype(F32))
        dqa, dqnw = _rms_bwd(_dotg(dql_ref[...], wq_ref[...], NT), xq, rq, qnw_ref[...])
        dqa_ref[...] = dqa.astype(dqa_ref.dtype)
        xkv, rkv = _rms_hat(kva_ref[...].astype(F32))
        dckv = _dotg(dkm_ref[...], wk_ref[...], NT) + _dotg(dvb_ref[...], wv_ref[...], NT)
        dkva, dkvnw = _rms_bwd(dckv, xkv, rkv, kvnw_ref[...])
        dkva_ref[...] = dkva.astype(dkva_ref.dtype)
        part_ref[0] = jnp.concatenate([dqnw, dkvnw, jnp.zeros((6, 256), F32)], axis=0)

    return pl.pallas_call(
        body, name="mla_prep_bwd", grid=(R // TM,),
        in_specs=[_rowspec(QW), _rowspec(QW), _rowspec(QW), _rowspec(256), _rowspec(256), _rowspec(256), _rowspec(256),
                  _fullspec((1, 256)), _fullspec((1, 256)), _fullspec((256, QW)), _fullspec((256, QW)),
                  _fullspec((256, QW)), _rowspec(HP), _rowspec(HP), _rowspec(HP)],
        out_specs=[_rowspec(256), _rowspec(256), _rowspec(128), _fullspec((256, QW)), _fullspec((256, QW)),
                   _fullspec((256, QW)), pl.BlockSpec((1, 8, 256), lambda i: (i, 0, 0))],
        out_shape=[jax.ShapeDtypeStruct((R, 256), MXU), jax.ShapeDtypeStruct((R, 256), MXU),
                   jax.ShapeDtypeStruct((R, 128), MXU)] + [jax.ShapeDtypeStruct((256, QW), F32)] * 3
                  + [jax.ShapeDtypeStruct((R // TM, 8, 256), F32)],
        scratch_shapes=[pltpu.VMEM((TM, QW), MXU)] * 3,
    )(dq, dk, dv, pqa, pkva, cq, ckv, qnw, kvnw, wq, wk, wv, cos, sa, sb)


ATT_SCALE = QK_DIM ** -0.5
TQ = 256


LOG2E = 1.4426950408889634
LN2 = 0.6931471805599453
Q_SCALE = ATT_SCALE * LOG2E


def _key_chunks(T, n=2):
    unit = 256 if T % 256 == 0 else 128
    units = T // unit
    sizes = [(units // n + (1 if i < units % n else 0)) * unit for i in range(n)]
    return [(sum(sizes[:i]), sz) for i, sz in enumerate(sizes) if sz]


def attn_fwd(q, k, v, nb, T):
    R = q.shape[0]
    nq = T // TQ
    chunks = _key_chunks(T, 4)
    HEADS = range(3)

    def body(q_ref, k_ref, v_ref, o_ref, lse_ref):
        def lanes(h):
            return slice(h * HP, (h + 1) * HP)

        def logits(h, lo, n):
            return _dotg(q_ref[:, lanes(h)], k_ref[lo:lo + n, lanes(h)], NT)

        def weigh(h, s, lo, n):
            m = jnp.max(s, axis=-1, keepdims=True)
            p = jnp.exp2(s - m)
            return m, jnp.sum(p, axis=-1, keepdims=True), _dot(p.astype(MXU), v_ref[lo:lo + n, lanes(h)])

        def parts_of(ranges):
            out = [[] for _ in HEADS]
            s = [logits(h, *ranges[0]) for h in HEADS]
            for j, (lo, n) in enumerate(ranges):
                nxt = [logits(h, *ranges[j + 1]) for h in HEADS] if j + 1 < len(ranges) else None
                for h in HEADS:
                    out[h].append(weigh(h, s[h], lo, n))
                s = nxt
            return out

        def finish(all_parts):
            for h, parts in enumerate(all_parts):
                m = parts[0][0]
                for pm, _, _ in parts[1:]:
                    m = jnp.maximum(m, pm)
                l, o = 0.0, 0.0
                for pm, pl_, po in parts:
                    a = jnp.exp2(pm - m)
                    l = l + a * pl_
                    o = o + a * po
                o_ref[:, lanes(h)] = o / l
                lse_ref[:, lanes(h)] = jnp.broadcast_to(m + jnp.log(l) * LOG2E, (TQ, HP))

        i = pl.program_id(2)
        pl.when(i == 0)(lambda: finish(parts_of([(0, CTX)])))
        pl.when(i > 0)(lambda: finish(parts_of(chunks)))

    qspec = pl.BlockSpec((TQ, len(HEADS) * HP), lambda b, h, i: (b * nq + i, h))
    kspec = pl.BlockSpec((T, len(HEADS) * HP), lambda b, h, i: (b, h))
    return pl.pallas_call(
        body, name="attn_fwd", grid=(nb, MLA_HEADS // len(HEADS), nq),
        in_specs=[qspec, kspec, kspec], out_specs=[qspec, qspec],
        out_shape=[jax.ShapeDtypeStruct((R, QW), F32)] * 2,
        compiler_params=_cp(48),
    )(q, k, v)


def attn_bwd(q, k, v, o, lse, do, nb, T):
    R = q.shape[0]
    nq = T // TQ
    chunks = _key_chunks(T)

    def body(q_ref, k_ref, v_ref, o_ref, lse_ref, do_ref, dq_ref, dk_ref, dv_ref):
        i = pl.program_id(2)

        @pl.when(i == 0)
        def _():
            dk_ref[...] = jnp.zeros_like(dk_ref)
            dv_ref[...] = jnp.zeros_like(dv_ref)

        def run(chunks):
            for h in range(2):
                lanes = slice(h * HP, (h + 1) * HP)
                qv = q_ref[:, lanes]
                dov = do_ref[:, lanes]
                dob = dov.astype(MXU)
                delta = jnp.sum(dov * o_ref[:, lanes], axis=-1, keepdims=True)
                lse_v = lse_ref[:, h * HP:h * HP + 1]
                dq = 0.0
                for lo, n in chunks:
                    kv = k_ref[lo:lo + n, lanes]
                    p = jnp.exp2(_dotg(qv, kv, NT) - lse_v)
                    dp = _dotg(dob, v_ref[lo:lo + n, lanes], NT)
                    dsb = (p * (dp - delta)).astype(MXU)
                    dq = dq + _dot(dsb, kv)
                    dk_ref[lo:lo + n, lanes] += _dotg(dsb, qv, TN)
                    dv_ref[lo:lo + n, lanes] += _dotg(p.astype(MXU), dob, TN)
                dq_ref[:, lanes] = dq

        pl.when(i == 0)(lambda: run([(0, CTX)]))
        pl.when(i > 0)(lambda: run(chunks))

    qspec = pl.BlockSpec((TQ, 2 * HP), lambda b, h, i: (b * nq + i, h))
    kspec = pl.BlockSpec((T, 2 * HP), lambda b, h, i: (b, h))
    return pl.pallas_call(
        body, name="attn_bwd", grid=(nb, MLA_HEADS // 2, nq),
        in_specs=[qspec, kspec, kspec, qspec, qspec, qspec],
        out_specs=[qspec, kspec, kspec],
        out_shape=[jax.ShapeDtypeStruct((R, QW), F32)] * 3,
        compiler_params=_cp(56),
    )(q, k, v, o, lse, do)


def _pool_geometry(i, blocks_per_sample, seq):
    j = i % blocks_per_sample
    n = jnp.where(j == 0, CTX, seq)
    t0 = jnp.where(j == 0, 0, (j - 1) * SB) - HALO
    lane = lax.broadcasted_iota(jnp.int32, (SB + 2 * HALO, POOL_DIM), 1)
    t = lax.broadcasted_iota(jnp.int32, (SB + 2 * HALO, POOL_DIM), 0) + t0
    wh = jnp.where(lane < 64, 1, jnp.where(lane < 128, 2, jnp.where(lane < 192, 4, 8)))
    cnt = jnp.minimum(t + wh, n) - jnp.maximum(t - wh, 0)
    return lane, 1.0 / jnp.maximum(cnt, 1).astype(F32)


def _by_window(lane, c2, c4, c8, c16):
    return jnp.where(lane < 64, c2, jnp.where(lane < 128, c4, jnp.where(lane < 192, c8, c16)))


def _window_sums(ext, lane, first):
    n = ext.shape[0]
    r = lambda a, s: pltpu.roll(a, s % n, axis=0)
    c2 = ext + r(ext, first)
    c4 = r(c2, 1) + r(c2, -1)
    c8 = r(c4, 2) + r(c4, -2)
    c16 = r(c8, 4) + r(c8, -4)
    return _by_window(lane, c2, c4, c8, c16)


def _pool_delta(ext, lane, inv):
    return (_window_sums(ext, lane, 1) * inv - ext)[HALO:HALO + SB, :]


def pool_fwd(ppool, wbd, scale, blocks_per_sample, seq):
    R = ppool.shape[0]
    prev, nxt = _halo_specs(POOL_DIM, R)

    def body(cur_ref, prev_ref, nxt_ref, w_ref, s_ref, o_ref):
        i = pl.program_id(0)
        ext = _ext_rows(cur_ref[...], prev_ref[...], nxt_ref[...], i, blocks_per_sample)
        lane, inv = _pool_geometry(i, blocks_per_sample, seq)
        dlt = _pool_delta(ext, lane, inv)
        o_ref[...] = _dot(dlt.astype(MXU), w_ref[...]) * s_ref[...]

    return pl.pallas_call(
        body, name="pool_fwd", grid=(R // SB,),
        in_specs=[_rowspec(POOL_DIM, SB), prev, nxt, _fullspec((POOL_DIM, POOL_DIM)), _fullspec((1, POOL_DIM))],
        out_specs=_rowspec(POOL_DIM, SB),
        out_shape=jax.ShapeDtypeStruct((R, POOL_DIM), F32),
    )(ppool, ppool, ppool, wbd, scale)


def pool_bwd(ppool, dpool, wbd, scale, blocks_per_sample, seq):
    R = ppool.shape[0]
    prev, nxt = _halo_specs(POOL_DIM, R)

    def body(cur_ref, prev_ref, nxt_ref, dcur_ref, dprev_ref, dnxt_ref, w_ref, s_ref, du_ref, dw_ref, part_ref):
        i = pl.program_id(0)

        @pl.when(i == 0)
        def _():
            dw_ref[...] = jnp.zeros_like(dw_ref)

        ext = _ext_rows(cur_ref[...], prev_ref[...], nxt_ref[...], i, blocks_per_sample)
        lane, inv = _pool_geometry(i, blocks_per_sample, seq)
        dlt = _pool_delta(ext, lane, inv).astype(MXU)
        dy = dcur_ref[...]
        part_ref[0] = jnp.concatenate([_colsum(dy * _dot(dlt, w_ref[...])), jnp.zeros((7, POOL_DIM), F32)], axis=0)
        dyp = (dy * s_ref[...]).astype(MXU)
        dw_ref[...] += _dotg(dlt, dyp, TN)
        dext = _ext_rows(dy, dprev_ref[...], dnxt_ref[...], i, blocks_per_sample)
        dd = _dotg((dext * s_ref[...]).astype(MXU), w_ref[...], NT)
        du_ref[...] = (_window_sums(dd * inv, lane, -1) - dd)[HALO:HALO + SB, :].astype(du_ref.dtype)

    return pl.pallas_call(
        body, name="pool_bwd", grid=(R // SB,),
        in_specs=[_rowspec(POOL_DIM, SB), prev, nxt, _rowspec(POOL_DIM, SB), prev, nxt,
                  _fullspec((POOL_DIM, POOL_DIM)), _fullspec((1, POOL_DIM))],
        out_specs=[_rowspec(POOL_DIM, SB), _fullspec((POOL_DIM, POOL_DIM)),
                   pl.BlockSpec((1, 8, POOL_DIM), lambda i: (i, 0, 0))],
        out_shape=[jax.ShapeDtypeStruct((R, POOL_DIM), MXU), jax.ShapeDtypeStruct((POOL_DIM, POOL_DIM), F32),
                   jax.ShapeDtypeStruct((R // SB, 8, POOL_DIM), F32)],
    )(ppool, ppool, ppool, dpool, dpool, dpool, wbd, scale)


def adamw(w, g, m, v, name="adamw"):
    rows, cols = w.shape
    tr = rows
    for cand in (512, 256, 128, 64, 32, 16, 8):
        if rows % cand == 0:
            tr = cand
            break
    bc1 = 1.0 - ADAM_B1 ** ADAM_STEP
    bc2 = 1.0 - ADAM_B2 ** ADAM_STEP

    def body(w_ref, g_ref, m_ref, v_ref, d_ref, nm_ref, nv_ref):
        g_v = g_ref[...]
        nm = ADAM_B1 * m_ref[...] + (1.0 - ADAM_B1) * g_v
        nv = ADAM_B2 * v_ref[...] + (1.0 - ADAM_B2) * (g_v * g_v)
        nm_ref[...] = nm
        nv_ref[...] = nv
        d_ref[...] = -ADAM_LR * ((nm / bc1) / (jnp.sqrt(nv / bc2) + ADAM_EPS) + ADAM_WD * w_ref[...])

    spec = pl.BlockSpec((tr, cols), lambda i: (i, 0))
    return pl.pallas_call(
        body, name=name, grid=(rows // tr,),
        in_specs=[spec] * 4, out_specs=[spec] * 3,
        out_shape=[jax.ShapeDtypeStruct((rows, cols), F32)] * 3,
    )(w, g, m, v)


MODR = 32


def _silu(v):
    return v * _sigmoid(v)


def mod_fwd(cond, w, b):
    n = w.shape[1]

    def body(c_ref, w_ref, b_ref, o_ref):
        o_ref[...] = _dot(_silu(c_ref[...]).astype(MXU), w_ref[...].astype(MXU)) + b_ref[...]

    return pl.pallas_call(
        body, name="mod_fwd", out_shape=jax.ShapeDtypeStruct((MODR, n), F32),
        in_specs=[_fullspec((MODR, D)), _fullspec((D, n)), _fullspec((1, n))], out_specs=_fullspec((MODR, n)),
        grid=(1,), compiler_params=_cp(40),
    )(cond, w, b)


def mod_wgrad(cond, dm):
    n = dm.shape[1]

    def body(c_ref, d_ref, o_ref):
        o_ref[...] = _dotg(_silu(c_ref[...]).astype(MXU), d_ref[...].astype(MXU), TN)

    return pl.pallas_call(
        body, name="mod_wgrad", out_shape=jax.ShapeDtypeStruct((D, n), F32),
        in_specs=[_fullspec((MODR, D)), _fullspec((MODR, n))], out_specs=_fullspec((D, n)),
        grid=(1,), compiler_params=_cp(40),
    )(cond, dm)


def mod_dgrad(dm, w):
    n = w.shape[1]

    def body(d_ref, w_ref, o_ref):
        o_ref[...] = _dotg(d_ref[...].astype(MXU), w_ref[...].astype(MXU), NT)

    return pl.pallas_call(
        body, name="mod_dgrad", out_shape=jax.ShapeDtypeStruct((8, D), F32),
        in_specs=[_fullspec((8, n)), _fullspec((D, n))], out_specs=_fullspec((8, D)),
        grid=(1,), compiler_params=_cp(40),
    )(dm, w)


def sum_leading(a, name="sum_leading"):
    n, r, c = a.shape

    def body(a_ref, o_ref):
        acc = a_ref[0]
        for k in range(1, n):
            acc = acc + a_ref[k]
        o_ref[...] = acc

    return pl.pallas_call(
        body, name=name, out_shape=jax.ShapeDtypeStruct((r, c), F32),
        in_specs=[_fullspec((n, r, c))], out_specs=_fullspec((r, c)), grid=(1,),
    )(a)


MESH = pl.DeviceIdType.MESH
NDEV = 8
ANY = pl.BlockSpec(memory_space=pl.ANY)


def _place():
    return lax.axis_index("x"), lax.axis_index("y"), lax.axis_index("c")


def _other_chips(x, y):
    return [(1 - x, y), (x, 1 - y), (1 - x, 1 - y)]


def allgather_small(v, name):
    r, cols = v.shape

    def body(v_ref, o_ref, send_sems, recv_sems):
        x, y, c = _place()
        me = 4 * x + 2 * y + c
        o_ref[me] = v_ref[...]
        copies = []
        for rel in range(1, NDEV):
            peer = (1 - x if rel & 4 else x, 1 - y if rel & 2 else y, 1 - c if rel & 1 else c)
            cp = pltpu.make_async_remote_copy(src_ref=v_ref, dst_ref=o_ref.at[me], send_sem=send_sems.at[rel - 1],
                                              recv_sem=recv_sems.at[rel - 1], device_id=peer, device_id_type=MESH)
            cp.start()
            copies.append(cp)
        for cp in copies:
            cp.wait_recv()
        for cp in copies:
            cp.wait_send()

    return pl.pallas_call(
        body, name=name, out_shape=jax.ShapeDtypeStruct((NDEV, r, cols), F32),
        in_specs=[pl.BlockSpec(memory_space=pltpu.VMEM)], out_specs=pl.BlockSpec(memory_space=pltpu.VMEM),
        scratch_shapes=[pltpu.SemaphoreType.DMA((NDEV - 1,)), pltpu.SemaphoreType.DMA((NDEV - 1,))],
        compiler_params=_cp(40),
    )(v)


def _sems(n):
    return [pltpu.SemaphoreType.DMA((n,)), pltpu.SemaphoreType.DMA((n,))]


def allgather_chips(v, name):
    r, cols = v.shape

    def body(v_ref, o_ref, send_sems, recv_sems):
        x, y, c = _place()
        k = 2 * x + y
        o_ref[k] = v_ref[...]
        copies = []
        for j, (px, py) in enumerate(_other_chips(x, y)):
            cp = pltpu.make_async_remote_copy(src_ref=v_ref, dst_ref=o_ref.at[k], send_sem=send_sems.at[j],
                                              recv_sem=recv_sems.at[j], device_id=(px, py, c), device_id_type=MESH)
            cp.start()
            copies.append(cp)
        for cp in copies:
            cp.wait_recv()
        for cp in copies:
            cp.wait_send()

    return pl.pallas_call(
        body, name=name, out_shape=jax.ShapeDtypeStruct((4, r, cols), F32),
        in_specs=[pl.BlockSpec(memory_space=pltpu.VMEM)], out_specs=pl.BlockSpec(memory_space=pltpu.VMEM),
        scratch_shapes=_sems(3), compiler_params=_cp(40),
    )(v)


def gather_job(arrs):
    n = len(arrs)

    def copy(srcs, outs, sems, i, slot, kk, cc, to, from_src=False):
        hr = arrs[i].shape[0] // 2
        dst = outs[i].at[kk, pl.ds(cc * hr, hr), :]
        return pltpu.make_async_remote_copy(src_ref=srcs[i].at[pl.ds(cc * hr, hr), :] if from_src else dst, dst_ref=dst,
                                            send_sem=sems[0].at[slot * n + i], recv_sem=sems[1].at[slot * n + i],
                                            device_id=to, device_id_type=MESH)

    def start(srcs, outs, sems):
        x, y, c = _place()
        for j, (px, py) in enumerate(_other_chips(x, y)):
            for i in range(n):
                copy(srcs, outs, sems, i, j, 2 * x + y, c, (px, py, c), True).start()

    def late(srcs, outs, sems):
        x, y, c = _place()
        for j, (px, py) in enumerate(_other_chips(x, y)):
            for i in range(n):
                copy(srcs, outs, sems, i, j, 2 * px + py, c, (px, py, c)).wait_recv()
                copy(srcs, outs, sems, i, 3 + j, 2 * px + py, c, (x, y, 1 - c)).start()

    def finish(srcs, outs, sems):
        x, y, c = _place()
        sib = (x, y, 1 - c)
        chips = _other_chips(x, y)
        for j, (px, py) in enumerate(chips):
            for i in range(n):
                copy(srcs, outs, sems, i, 3 + j, 2 * px + py, 1 - c, sib).wait_recv()
        for j, (px, py) in enumerate(chips):
            for i in range(n):
                copy(srcs, outs, sems, i, j, 2 * x + y, c, (px, py, c), True).wait_send()
                copy(srcs, outs, sems, i, 3 + j, 2 * px + py, c, sib).wait_send()

    return _NS(ins=list(arrs), out_shapes=[jax.ShapeDtypeStruct((4,) + a.shape, a.dtype) for a in arrs], nsem=6 * n,
               start=start, late=late, finish=finish)


def chip_swap_job(ss):
    n = len(ss)

    def copies(srcs, outs, sems):
        x, y, c = _place()
        return [pltpu.make_async_remote_copy(src_ref=srcs[i].at[2 * px + py], dst_ref=outs[i].at[j],
                                             send_sem=sems[0].at[j * n + i], recv_sem=sems[1].at[j * n + i],
                                             device_id=(px, py, c), device_id_type=MESH)
                for j, (px, py) in enumerate(_other_chips(x, y)) for i in range(n)]

    def start(srcs, outs, sems):
        for cp in copies(srcs, outs, sems):
            cp.start()

    def finish(srcs, outs, sems):
        for cp in copies(srcs, outs, sems):
            cp.wait()

    return _NS(ins=list(ss), out_shapes=[jax.ShapeDtypeStruct((3,) + s.shape[1:], s.dtype) for s in ss], nsem=3 * n,
               start=start, finish=finish)


def run_job(job, name):
    n, m = len(job.ins), len(job.out_shapes)

    def body(*refs):
        srcs, outs, sems = refs[:n], refs[n:n + m], refs[n + m:]
        job.start(srcs, outs, sems)
        if hasattr(job, "late"):
            job.late(srcs, outs, sems)
        job.finish(srcs, outs, sems)

    return pl.pallas_call(body, name=name, out_shape=job.out_shapes, in_specs=[ANY] * n, out_specs=[ANY] * m,
                          scratch_shapes=_sems(job.nsem))(*job.ins)


def core_swap_job(gs):
    n = len(gs)

    def copies(srcs, outs, sems):
        x, y, c = _place()
        return [pltpu.make_async_remote_copy(src_ref=srcs[i].at[:, pl.ds((1 - c) * (gs[i].shape[1] // 2), gs[i].shape[1] // 2), :],
                                             dst_ref=outs[i], send_sem=sems[0].at[i], recv_sem=sems[1].at[i],
                                             device_id=(x, y, 1 - c), device_id_type=MESH) for i in range(n)]

    def start(srcs, outs, sems):
        for cp in copies(srcs, outs, sems):
            cp.start()

    def finish(srcs, outs, sems):
        for cp in copies(srcs, outs, sems):
            cp.wait()

    return _NS(ins=list(gs), out_shapes=[jax.ShapeDtypeStruct((4, g.shape[1] // 2, g.shape[2]), g.dtype) for g in gs],
               nsem=n, start=start, finish=finish)


def add_half(g, r1, cidx, name):
    _, rows, cols = g.shape
    hr = rows // 2

    def body(c_ref, g_ref, r_ref, o_ref, ob_ref):
        s = g_ref[...] + r_ref[...]
        o_ref[...] = s
        ob_ref[...] = s.astype(BF16)

    blk = lambda f: pl.BlockSpec((1, hr, cols), f)
    return pl.pallas_call(
        body, name=name,
        out_shape=[jax.ShapeDtypeStruct((4, hr, cols), F32), jax.ShapeDtypeStruct((4, hr, cols), BF16)],
        grid_spec=pltpu.PrefetchScalarGridSpec(
            num_scalar_prefetch=1, grid=(4,),
            in_specs=[blk(lambda k, c_ref: (k, c_ref[0], 0)), blk(lambda k, c_ref: (k, 0, 0))],
            out_specs=[blk(lambda k, c_ref: (k, 0, 0)), blk(lambda k, c_ref: (k, 0, 0))]),
    )(cidx, g, r1)


def sum_parts(s1, r2, kidx, name):
    _, hr, cols = s1.shape

    def body(k_ref, s_ref, r_ref, o_ref):
        o_ref[...] = ((s_ref[0] + r_ref[0].astype(F32)) + r_ref[1].astype(F32)) + r_ref[2].astype(F32)

    return pl.pallas_call(
        body, name=name, out_shape=jax.ShapeDtypeStruct((hr, cols), F32),
        grid_spec=pltpu.PrefetchScalarGridSpec(
            num_scalar_prefetch=1, grid=(1,),
            in_specs=[pl.BlockSpec((1, hr, cols), lambda i, k_ref: (k_ref[0], 0, 0)),
                      pl.BlockSpec((3, hr, cols), lambda i, k_ref: (0, 0, 0))],
            out_specs=pl.BlockSpec((hr, cols), lambda i, k_ref: (0, 0))),
    )(kidx, s1, r2)


def swap_reduced_halves(hs):
    n = len(hs)

    def body(*refs):
        srcs, outs = refs[:n], refs[n:2 * n]
        send_sems, recv_sems = refs[2 * n:]
        x, y, c = _place()
        copies = []
        for i in range(n):
            cp = pltpu.make_async_remote_copy(src_ref=srcs[i], dst_ref=outs[i], send_sem=send_sems.at[i],
                                              recv_sem=recv_sems.at[i], device_id=(x, y, 1 - c), device_id_type=MESH)
            cp.start()
            copies.append(cp)
        for cp in copies:
            cp.wait()

    return pl.pallas_call(
        body, name="swap_reduced_halves", out_shape=[jax.ShapeDtypeStruct(h.shape, h.dtype) for h in hs],
        in_specs=[ANY] * n, out_specs=[ANY] * n, scratch_shapes=_sems(n),
    )(*hs)


def adamw_halves(w, m, v, own, oth, cidx, name):
    depth, rows, cols = w.shape
    hr = rows // 2
    tr = min(hr, 256)
    nblk = hr // tr
    bc1 = 1.0 - ADAM_B1 ** ADAM_STEP
    bc2 = 1.0 - ADAM_B2 ** ADAM_STEP

    def body(c_ref, w_ref, m_ref, v_ref, own0, own1, oth0, oth1, g_ref, d_ref, nm_ref, nv_ref):
        l = pl.program_id(0)
        hi = pl.program_id(1)
        mine = jnp.where(l == 0, own0[...], own1[...])
        other = jnp.where(l == 0, oth0[...], oth1[...])
        g_v = jnp.where(hi == c_ref[0], mine, other)
        nm = ADAM_B1 * m_ref[0] + (1.0 - ADAM_B1) * g_v
        nv = ADAM_B2 * v_ref[0] + (1.0 - ADAM_B2) * (g_v * g_v)
        g_ref[0] = g_v
        nm_ref[0] = nm
        nv_ref[0] = nv
        d_ref[0] = -ADAM_LR * ((nm / bc1) / (jnp.sqrt(nv / bc2) + ADAM_EPS) + ADAM_WD * w_ref[0])

    wspec = pl.BlockSpec((1, tr, cols), lambda l, hi, b, c_ref: (l, hi * nblk + b, 0))
    gspec = pl.BlockSpec((tr, cols), lambda l, hi, b, c_ref: (b, 0))
    assert depth == 2
    return pl.pallas_call(
        body, name=name, out_shape=[jax.ShapeDtypeStruct(w.shape, F32)] * 4,
        grid_spec=pltpu.PrefetchScalarGridSpec(
            num_scalar_prefetch=1, grid=(depth, 2, nblk),
            in_specs=[wspec] * 3 + [gspec] * 4, out_specs=[wspec] * 4),
    )(cidx, w, m, v, own[0], own[1], oth[0], oth[1])


class _NS:
    def __init__(self, **kw):
        self.__dict__.update(kw)


def _prep_in(win, conv_w, conv_b, dt_bias, a_log, ssd_d, ssd_nw, qnw, kvnw, pool_w, pool_scale, n1, n2):
    winp = jnp.concatenate([win[:, 0:384], win[:, 384:1280], win[:, 1292:1548], win[:, 1548:1804], win[:, 1836:2092],
                            win[:, 1804:1836], win[:, 1280:1292], jnp.zeros((D, NP - IN_COLS), win.dtype)], axis=1)
    wbd = (jnp.eye(4, dtype=F32)[:, None, :, None] * pool_w[:, :, None, :]).reshape(POOL_DIM, POOL_DIM).astype(MXU)
    a = -jnp.exp(a_log)
    return _NS(
        winp=winp, wbd=wbd,
        cw8=jnp.pad(conv_w, ((0, 4), (0, 0))), cb=conv_b[None],
        dtb=jnp.pad(dt_bias.reshape(1, 12), ((0, 0), (DT0, 128 - DT0 - 12))),
        arow=jnp.pad(a[:, None, :], ((0, 0), (0, 7), (0, 128 - SSD_HEADS))), a=a,
        dexp=jnp.repeat(ssd_d, SSD_P)[None], ssd_nw=ssd_nw[None], qnw=qnw[None], kvnw=kvnw[None],
        pscale=pool_scale[None], n1=n1[None], n2=n2[None])


def _prep_rest(wqb, wkvb, wout, w1, w2):
    wq = jnp.pad(wqb.reshape(256, MLA_HEADS, QK_DIM), ((0, 0), (0, 0), (0, HP - QK_DIM))).reshape(256, QW)
    kv3 = wkvb.reshape(256, MLA_HEADS, 128)
    wk = jnp.pad(kv3[:, :, :64], ((0, 0), (0, 0), (0, 64))).reshape(256, QW)
    wv = jnp.pad(kv3[:, :, 64:], ((0, 0), (0, 0), (0, 64))).reshape(256, QW)
    wo = jnp.concatenate([jnp.pad(wout[384:768].reshape(MLA_HEADS, 64, D), ((0, 0), (0, 64), (0, 0))).reshape(QW, D),
                          wout[0:384], wout[768:1024]], axis=0)
    return _NS(wq=wq, wk=wk, wv=wv, wo=wo, w1=w1, w2=w2)


def _prep_layer(win, wqb, wkvb, wout, w1, w2, *small):
    lw = _prep_in(win, *small)
    lw.__dict__.update(_prep_rest(wqb, wkvb, wout, w1, w2).__dict__)
    return lw


def _by_chip_cols(a):
    return jnp.stack([a[:, k * (a.shape[1] // 4):(k + 1) * (a.shape[1] // 4)] for k in range(4)])


def _by_chip_rows(a):
    return a.reshape(4, a.shape[0] // 4, a.shape[1])


def _unprep_in(dwinp):
    return jnp.concatenate([dwinp[:, 0:384], dwinp[:, 384:1280], dwinp[:, 2080:2092], dwinp[:, 1280:1536],
                            dwinp[:, 1536:1792], dwinp[:, 2048:2080], dwinp[:, 1792:2048]], axis=1)


def _unprep_rest(dwq, dwk, dwv, dwo):
    dwqb = dwq.reshape(256, MLA_HEADS, HP)[:, :, :QK_DIM].reshape(256, MLA_HEADS * QK_DIM)
    dwkvb = jnp.concatenate([dwk.reshape(256, MLA_HEADS, HP)[:, :, :64], dwv.reshape(256, MLA_HEADS, HP)[:, :, :64]],
                            axis=2).reshape(256, MLA_HEADS * 128)
    dwout = jnp.concatenate([dwo[QW:QW + 384], dwo[0:QW].reshape(MLA_HEADS, HP, D)[:, :64].reshape(384, D),
                             dwo[QW + 384:CAT]], axis=0)
    return dwqb, dwkvb, dwout


def _rope_tables(nb, N):
    t = jnp.arange(N, dtype=F32)
    row = jnp.floor(t / GRID_W)
    col = t - row * GRID_W
    inv = jnp.asarray(10000.0 ** (-np.arange(8, dtype=np.float32) / 8), F32)
    ang = jnp.stack([row[:, None] * inv, col[:, None] * inv], axis=1)
    cs, sn = jnp.cos(ang), jnp.sin(ang)
    zero = jnp.zeros_like(sn)
    lanes = lambda first, second: jnp.stack([first, second], axis=2).reshape(N, 32)
    pad = lambda a, fill: jnp.concatenate([jnp.full((N, 64), fill, F32), a, jnp.full((N, 32), fill, F32)], axis=1)
    tabs = []
    for tab, fill in ((pad(lanes(cs, cs), 1.0), 1.0), (pad(lanes(-sn, zero), 0.0), 0.0), (pad(lanes(zero, sn), 0.0), 0.0)):
        one = jnp.concatenate([jnp.full((CTX, 128), fill, F32), tab], axis=0)
        tabs.append(jnp.tile(one, (nb, 1)))
    return tabs


def _eexp():
    e = np.zeros((128, SSD_INNER), np.float32)
    for h in range(SSD_HEADS):
        e[h, h * SSD_P:(h + 1) * SSD_P] = 1.0
    return jnp.asarray(e)


class _NoHooks:
    def __init__(self, lws):
        self.lws = lws

    def weights_in(self, l):
        return _NS(**self.lws[l].__dict__)

    def weights_rest(self, l, scan_out):
        return self.lws[l]

    def job(self, where, l, early=None):
        return None

    def done(self, where, l, out):
        pass

    def layer_grads(self, l, g):
        pass


def _layer_fwd(X, bm, l, cst, hooks):
    nb, T, bps, N = cst.nb, cst.T, cst.bps, cst.N
    lw = hooks.weights_in(l)
    h1, pz, pxbc, pqa, pkva, ppool, plast = in_proj(X, bm, lw.n1, lw.winp)
    xs, bmat, cmat, dtv = ssd_prep(pxbc, plast, lw.cw8, lw.cb, lw.dtb, bps)
    y2, hin, out = ssd_scan_fwd(xs, bmat, cmat, dtv, lw.arow, cst.eexp, nb, T, hooks.job("fwd_scan", l))
    lw.__dict__.update(hooks.weights_rest(l, out).__dict__)
    ssd = ssd_out_fwd(y2, xs, pz, lw.dexp, lw.ssd_nw)
    q, k, v, cq, ckv = mla_prep(pqa, pkva, plast, lw.qnw, lw.kvnw, lw.wq, lw.wk, lw.wv, *cst.rope)
    attn, lse = attn_fwd(q, k, v, nb, T)
    pool = pool_fwd(ppool, lw.wbd, lw.pscale, bps, N)
    x1, mix, cat = mix_fwd(X, attn, ssd, pool, bm, lw.wo)
    x2, mo, r, h2, out = mlp_fwd(x1, bm, lw.n2, lw.w1, lw.w2, hooks.job("fwd_mlp", l))
    hooks.done("fwd_mlp", l, out)
    sv = _NS(X=X, h1=h1, pz=pz, pxbc=pxbc, pqa=pqa, pkva=pkva, ppool=ppool, plast=plast, xs=xs, bmat=bmat, cmat=cmat,
             dtv=dtv, y2=y2, hin=hin, q=q, k=k, v=v, cq=cq, ckv=ckv, attn=attn, lse=lse, x1=x1, mix=mix, cat=cat, mo=mo, r=r,
             h2=h2, lw=lw)
    return x2, sv


def _layer_bwd(dx2, bm, l, sv, cst, hooks):
    nb, T, bps, N = cst.nb, cst.T, cst.bps, cst.N
    lw = sv.lw
    dx1, du, dob, part_mlp, out = mlp_bwd(dx2, sv.x1, sv.mo, sv.r, bm, lw.n2, lw.w2, lw.w1, hooks.job("bwd_mlp", l))
    hooks.done("bwd_mlp", l, out)
    dw1 = mm_tn(sv.h2, du, name="wgrad_mlp1", col_blocks=True)
    dw2 = mm_tn(sv.r, dob, square_a=True, name="wgrad_mlp2")
    dattn, dssd, dpool, dwo, part_mix = mix_bwd(dx1, sv.mix, sv.cat, bm, lw.wo)
    dppool, dwbd, part_pool = pool_bwd(sv.ppool, dpool, lw.wbd, lw.pscale, bps, N)
    dq, dk, dv = attn_bwd(sv.q, sv.k, sv.v, sv.attn, sv.lse, dattn, nb, T)
    dpqa, dpkva, dkr, dwq, dwk, dwv, part_mla = mla_prep_bwd(dq, dk, dv, sv.pqa, sv.pkva, sv.cq, sv.ckv, lw.qnw, lw.kvnw,
                                                             lw.wq, lw.wk, lw.wv, *cst.rope)
    dwqb, dwkvb, dwout = _unprep_rest(dwq, dwk, dwv, dwo)
    early = dict(w_q_b=_by_chip_cols(dwqb), w_kv_b=_by_chip_cols(dwkvb), w_out=_by_chip_rows(dwout), w_mlp1=dw1,
                 w_mlp2=_by_chip_rows(dw2))
    dyy, dz, dxs_skip, part_so = ssd_out_bwd(dssd, sv.y2, sv.xs, sv.pz, lw.dexp, lw.ssd_nw)
    dxs2, dbm2, dcm2, ddt2, da, out = ssd_scan_bwd(sv.xs, sv.bmat, sv.cmat, sv.dtv, lw.arow, cst.eexp, sv.hin, dyy,
                                                   nb, T, hooks.job("bwd_scan", l, early))
    hooks.done("bwd_scan", l, out)
    dpre, dlast_dt, part_conv = ssd_prep_bwd_a(sv.pxbc, sv.plast, lw.cw8, lw.cb, lw.dtb, dxs_skip, dxs2, dbm2, dcm2,
                                               ddt2, bps)
    dpxbc = ssd_prep_bwd_b(dpre, lw.cw8, bps)
    dx, dwinp, part_in = in_proj_bwd(dx1, sv.X, sv.h1, dz, dpxbc, dpqa, dpkva, dppool, dkr, dlast_dt, bm, lw.n1, lw.winp)

    dmod = jnp.stack([part_in[:, 0], part_in[:, 1], part_mix[:, 0], part_mlp[:, 0], part_mlp[:, 1], part_mlp[:, 2]],
                     axis=1)
    dmod = dmod.reshape(nb, bps, 6, D)
    dm_rows = jnp.concatenate([jnp.sum(dmod[:, 1:], axis=1), jnp.sum(dmod[:, 0], axis=0)[None]], axis=0)
    da_dh = jnp.sum(da[:, :, 0, :SSD_HEADS], axis=1)
    conv_parts = jnp.sum(part_conv, axis=0)
    g = _NS(
        w_in=_by_chip_cols(_unprep_in(dwinp)), dm_rows=dm_rows.reshape(3, 6 * D), **early,
        norm1_w=jnp.sum(part_in[:, 2], axis=0), norm2_w=jnp.sum(part_mlp[:, 3], axis=0),
        conv_w=conv_parts[0:4], conv_b=conv_parts[4],
        dt_bias=conv_parts[5, DT0:DT0 + 12].reshape(2, SSD_HEADS), a_log=da_dh * lw.a,
        ssd_d=jnp.sum(jnp.sum(part_so[:, 1], axis=0).reshape(SSD_HEADS, SSD_P), axis=1),
        ssd_norm_w=jnp.sum(part_so[:, 0], axis=0),
        q_a_norm_w=jnp.sum(part_mla[:, 0], axis=0), kv_a_norm_w=jnp.sum(part_mla[:, 1], axis=0),
        pool_w=jnp.stack([dwbd[i * 64:(i + 1) * 64, i * 64:(i + 1) * 64] for i in range(4)]),
        pool_scale=jnp.sum(part_pool[:, 0], axis=0))
    hooks.layer_grads(l, g)
    return dx, g


def _local_step(x, ctx, tgt, bms, lws, fw, cst, hooks=None):
    nb, N = x.shape[0], x.shape[1]
    R = nb * cst.T
    hooks = _NoHooks(lws) if hooks is None else hooks
    X = jnp.concatenate([ctx, x], axis=1).reshape(R, D)
    saved = []
    for l in range(DEPTH):
        X, sv = _layer_fwd(X, bms[l], l, cst, hooks)
        saved.append(sv)
    dX, part_fin = final_loss(X, tgt.reshape(nb * N, D), fw[None], cst.bps)
    loss = (0.5 / D) * jnp.sum(part_fin[:, 1])
    dfw = jnp.sum(part_fin[:, 0], axis=0)
    grads = [None] * DEPTH
    for l in reversed(range(DEPTH)):
        dX, grads[l] = _layer_bwd(dX, bms[l], l, saved[l], cst, hooks)
    grad_x = dX.reshape(nb, cst.T, D)[:, CTX:, :]
    return loss, grad_x, grads, dfw


def _consts(nb, N):
    T = CTX + N
    bps = T // SB
    return _NS(nb=nb, N=N, T=T, bps=bps, eexp=_eexp(), rope=_rope_tables(nb, N))


def _block_mod(modrows, cst):
    rows = []
    for b in range(cst.nb):
        rows.append(modrows[cst.nb:cst.nb + 1])
        rows.append(jnp.broadcast_to(modrows[b:b + 1], (cst.bps - 1, 6, D)))
    return jnp.pad(jnp.concatenate(rows, axis=0), ((0, 0), (0, 2), (0, 0)))


SMALL = (("norm1_w", (2, D)), ("norm2_w", (2, D)), ("conv_w", (2, 4, XBC)), ("conv_b", (2, XBC)),
         ("dt_bias", (2, 2, 6)), ("a_log", (2, 2, 6)), ("ssd_d", (2, 6)), ("ssd_norm_w", (2, 384)),
         ("q_a_norm_w", (2, 256)), ("kv_a_norm_w", (2, 256)), ("pool_w", (2, 4, 64, 64)), ("pool_scale", (2, 256)),
         ("final_norm_w", (D,)), ("mod_b", (2, 6 * D)))
SMALL_ROWS = 64
DM_ROWS = 48


def _pack_small(vals):
    flat = jnp.concatenate([vals[n].reshape(-1) for n, _ in SMALL])
    return jnp.pad(flat, (0, SMALL_ROWS * D - flat.shape[0])).reshape(SMALL_ROWS, D)


def _unpack_small(p):
    flat = p.reshape(-1)
    out, off = {}, 0
    for n, shp in SMALL:
        size = int(np.prod(shp))
        out[n] = flat[off:off + size].reshape(shp)
        off += size
    return out


def cctx_grad(parts, c_ctx):
    def body(p_ref, c_ref, o_ref):
        acc = ((p_ref[0] + p_ref[1]) + p_ref[2]) + p_ref[3]
        v = c_ref[...]
        sig = _sigmoid(v)
        o_ref[...] = acc * (sig * (1.0 + v * (1.0 - sig)))

    return pl.pallas_call(
        body, name="cctx_grad", out_shape=jax.ShapeDtypeStruct((8, D), F32),
        in_specs=[_fullspec((4, 8, D)), _fullspec((1, D))], out_specs=_fullspec((8, D)), grid=(1,),
    )(parts, c_ctx)


def kernel(x, c, ctx, c_ctx, mod_w, mod_b, norm1_w, norm2_w, w_in, conv_w, conv_b, dt_bias, a_log, ssd_d, ssd_norm_w, q_a_norm_w, w_q_b, kv_a_norm_w, w_kv_b, pool_w, pool_scale, w_out, w_mlp1, w_mlp2, final_norm_w, loss_target, m_c_ctx, m_mod_w, m_mod_b, m_norm1_w, m_norm2_w, m_w_in, m_conv_w, m_conv_b, m_dt_bias, m_a_log, m_ssd_d, m_ssd_norm_w, m_q_a_norm_w, m_w_q_b, m_kv_a_norm_w, m_w_kv_b, m_pool_w, m_pool_scale, m_w_out, m_w_mlp1, m_w_mlp2, m_final_norm_w, v_c_ctx, v_mod_w, v_mod_b, v_norm1_w, v_norm2_w, v_w_in, v_conv_w, v_conv_b, v_dt_bias, v_a_log, v_ssd_d, v_ssd_norm_w, v_q_a_norm_w, v_w_q_b, v_kv_a_norm_w, v_w_kv_b, v_pool_w, v_pool_scale, v_w_out, v_w_mlp1, v_w_mlp2, v_final_norm_w):
    nb, N = x.shape[0], x.shape[1]
    cst = _consts(nb, N)
    xi, yi, ci = _place()
    me = 4 * xi + 2 * yi + ci
    kchip = 2 * xi + yi
    mcols = mod_w.shape[2]
    cshard = conv_w.shape[2]

    blk = jnp.zeros((16, D), F32).at[0:nb].set(c).at[8:16, 0:cshard].set(conv_w.reshape(8, cshard))
    g1 = allgather_small(blk, "gather_cond")
    cond = jnp.concatenate([g1[:, 0:nb].reshape(NDEV * nb, D), c_ctx[None],
                            jnp.zeros((MODR - NDEV * nb - 1, D), F32)], axis=0)
    conv_full = [jnp.concatenate([g1[2 * k, 8 + 4 * l:12 + 4 * l, 0:cshard] for k in range(4)], axis=1)
                 for l in range(DEPTH)]

    mb = [lax.dynamic_slice_in_dim(mod_b[l], kchip * mcols, mcols)[None] for l in range(DEPTH)]
    ms = jnp.concatenate([mod_fwd(cond, mod_w[l], mb[l]) for l in range(DEPTH)], axis=0)
    g2 = allgather_chips(ms, "gather_mod")
    bms = []
    for l in range(DEPTH):
        m_all = jnp.concatenate([g2[k, MODR * l:MODR * (l + 1)] for k in range(4)], axis=1)
        mine = jnp.concatenate([lax.dynamic_slice_in_dim(m_all, nb * me, nb), m_all[NDEV * nb:NDEV * nb + 1]], axis=0)
        bms.append(_block_mod(mine.reshape(nb + 1, 6, D), cst))

    assert DEPTH == 2
    big = (w_in, w_q_b, w_kv_b, w_out, w_mlp1, w_mlp2)
    names = ("w_in", "w_q_b", "w_kv_b", "w_out", "w_mlp1", "w_mlp2")
    concat_axis = dict(w_in=1, w_q_b=1, w_kv_b=1, w_out=0, w_mlp1=1, w_mlp2=0)
    cidx = jnp.reshape(ci, (1,)).astype(jnp.int32)
    kidx = jnp.reshape(kchip, (1,)).astype(jnp.int32)
    shards = [{n: a[l].astype(MXU) for n, a in zip(names, big)} for l in range(DEPTH)]

    def core_sums(gs, got=None):
        ns = list(gs)
        got = run_job(core_swap_job([gs[n] for n in ns]), "swap_core_halves") if got is None else got
        return {n: add_half(gs[n], r, cidx, "add_half_" + n) for n, r in zip(ns, got)}

    class Hooks:
        gathered = [dict(w_in=run_job(gather_job([shards[0]["w_in"]]), "gather_w_in")[0]), {}]
        core_sum = [{}, {}]
        received = [{}, {}]

        def whole(self, l, n):
            return jnp.concatenate([jnp.where(kchip == k, shards[l][n], self.gathered[l][n][k]) for k in range(4)],
                                   axis=concat_axis[n])

        def weights_in(self, l):
            return _prep_in(self.whole(l, "w_in"), conv_full[l], conv_b[l], dt_bias[l], a_log[l], ssd_d[l], ssd_norm_w[l],
                            q_a_norm_w[l], kv_a_norm_w[l], pool_w[l], pool_scale[l], norm1_w[l], norm2_w[l])

        def weights_rest(self, l, scan_out):
            if l == 0:
                self.gathered[0].update(zip(names[1:], scan_out))
            return _prep_rest(*[self.whole(l, n) for n in names[1:]])

        def job(self, where, l, early=None):
            if l == 1 and where == "bwd_scan":
                self.early1 = early
                return core_swap_job([early[n] for n in names[1:]])
            if l != 0:
                return None
            if where == "fwd_scan":
                return gather_job([shards[0][n] for n in names[1:]])
            if where == "fwd_mlp":
                return gather_job([shards[1][n] for n in names])
            if where == "bwd_mlp":
                return chip_swap_job([self.core_sum[1][n][1] for n in names])
            self.core_sum[0].update(core_sums(early))
            return chip_swap_job([self.core_sum[0][n][1] for n in names[1:]])

        def done(self, where, l, out):
            if l == 1 and where == "bwd_scan":
                self.core_sum[1].update(core_sums(self.early1, out))
            if l != 0:
                return
            if where == "fwd_mlp":
                self.gathered[1].update(zip(names, out))
            elif where == "bwd_mlp":
                self.received[1].update(zip(names, out))
            elif where == "bwd_scan":
                self.received[0].update(zip(names[1:], out))

        def layer_grads(self, l, g):
            if l == 1:
                self.core_sum[1].update(core_sums(dict(w_in=g.w_in)))
            else:
                self.core_sum[0].update(core_sums(dict(w_in=g.w_in)))
                self.received[0]["w_in"] = run_job(chip_swap_job([self.core_sum[0]["w_in"][1]]), "swap_w_in")[0]

    hooks = Hooks()
    loss_part, grad_x, grads, dfw = _local_step(x, ctx, loss_target, bms, None, final_norm_w, cst, hooks)
    loss = lax.psum(loss_part, ("x", "y", "c"))
    g_own = [sum_parts(hooks.core_sum[l][n][0], hooks.received[l][n], kidx, "sum_parts_" + n)
             for n in names for l in range(DEPTH)]
    g_oth = swap_reduced_halves(g_own)

    small = {n: jnp.stack([getattr(grads[l], n) for l in range(DEPTH)]) for n, _ in SMALL if n not in ("final_norm_w", "mod_b")}
    small["final_norm_w"] = dfw
    small["mod_b"] = jnp.stack([jnp.sum(grads[l].dm_rows, axis=0) for l in range(DEPTH)])
    dm = jnp.pad(jnp.concatenate([grads[l].dm_rows for l in range(DEPTH)], axis=0), ((0, 8 - 3 * DEPTH), (0, 0)))
    g3 = allgather_small(jnp.concatenate([_pack_small(small), dm.reshape(DM_ROWS, D)], axis=0), "gather_small")
    tot = sum_leading(g3, "sum_small")
    gsmall = _unpack_small(tot[0:SMALL_ROWS])
    ctx_sum = tot[SMALL_ROWS:].reshape(8, 6 * D)
    dm_dev = g3[:, SMALL_ROWS:].reshape(NDEV, 8, 6 * D)
    g_mod_w, dpart = [], jnp.zeros((8, D), F32)
    for l in range(DEPTH):
        dm_all = jnp.concatenate([dm_dev[:, 3 * l:3 * l + nb].reshape(NDEV * nb, 6 * D), ctx_sum[3 * l + nb:3 * l + nb + 1],
                                  jnp.zeros((MODR - NDEV * nb - 1, 6 * D), F32)], axis=0)
        g_mod_w.append(mod_wgrad(cond, lax.dynamic_slice_in_dim(dm_all, kchip * mcols, mcols, axis=1)))
        dctx = jnp.pad(lax.dynamic_slice_in_dim(ctx_sum[3 * l + nb:3 * l + nb + 1], kchip * mcols, mcols, axis=1), ((0, 7), (0, 0)))
        dpart = dpart + mod_dgrad(dctx, mod_w[l])
    g_c_ctx = cctx_grad(allgather_chips(dpart, "gather_cctx"), c_ctx[None])[0]

    res = {}
    moments = ((m_w_in, v_w_in), (m_w_q_b, v_w_q_b), (m_w_kv_b, v_w_kv_b), (m_w_out, v_w_out), (m_w_mlp1, v_w_mlp1),
               (m_w_mlp2, v_w_mlp2))
    for i, (n, w, (m, v)) in enumerate(zip(names, big, moments)):
        res[n] = tuple(adamw_halves(w, m, v, g_own[DEPTH * i:DEPTH * (i + 1)], g_oth[DEPTH * i:DEPTH * (i + 1)], cidx,
                                    "adamw_" + n))
    g_mw = jnp.stack(g_mod_w)
    r_mw = adamw(mod_w.reshape(-1, mcols), g_mw.reshape(-1, mcols), m_mod_w.reshape(-1, mcols),
                 v_mod_w.reshape(-1, mcols), name="adamw_mod_w")
    res["mod_w"] = (g_mw,) + tuple(a.reshape(mod_w.shape) for a in r_mw)

    given = dict(norm1_w=(norm1_w, m_norm1_w, v_norm1_w), norm2_w=(norm2_w, m_norm2_w, v_norm2_w),
                 conv_b=(conv_b, m_conv_b, v_conv_b), dt_bias=(dt_bias, m_dt_bias, v_dt_bias),
                 a_log=(a_log, m_a_log, v_a_log), ssd_d=(ssd_d, m_ssd_d, v_ssd_d),
                 ssd_norm_w=(ssd_norm_w, m_ssd_norm_w, v_ssd_norm_w), q_a_norm_w=(q_a_norm_w, m_q_a_norm_w, v_q_a_norm_w),
                 kv_a_norm_w=(kv_a_norm_w, m_kv_a_norm_w, v_kv_a_norm_w), pool_w=(pool_w, m_pool_w, v_pool_w),
                 pool_scale=(pool_scale, m_pool_scale, v_pool_scale),
                 final_norm_w=(final_norm_w, m_final_norm_w, v_final_norm_w), mod_b=(mod_b, m_mod_b, v_mod_b))
    zero_cw = jnp.zeros((2, 4, XBC), F32)
    packs = [_pack_small({n: (given[n][i] if n in given else zero_cw) for n, _ in SMALL}) for i in range(3)]
    r_small = [_unpack_small(a) for a in adamw(packs[0], tot[0:SMALL_ROWS], packs[1], packs[2], name="adamw_small")]
    for n in given:
        res[n] = (gsmall[n], r_small[0][n], r_small[1][n], r_small[2][n])

    g_cw = lax.dynamic_slice_in_dim(gsmall["conv_w"], kchip * cshard, cshard, axis=2)
    padcw = lambda a: jnp.pad(a.reshape(8, cshard), ((0, 0), (0, 256 - cshard)))
    r_cw = adamw(padcw(conv_w), padcw(g_cw), padcw(m_conv_w), padcw(v_conv_w), name="adamw_conv_w")
    res["conv_w"] = (g_cw,) + tuple(a[:, 0:cshard].reshape(conv_w.shape) for a in r_cw)
    r_cc = adamw(c_ctx.reshape(8, 128), g_c_ctx.reshape(8, 128), m_c_ctx.reshape(8, 128), v_c_ctx.reshape(8, 128),
                 name="adamw_c_ctx")
    res["c_ctx"] = (g_c_ctx,) + tuple(a.reshape(D) for a in r_cc)

    order = ("c_ctx", "mod_w", "mod_b", "norm1_w", "norm2_w", "w_in", "conv_w", "conv_b", "dt_bias", "a_log", "ssd_d",
             "ssd_norm_w", "q_a_norm_w", "w_q_b", "kv_a_norm_w", "w_kv_b", "pool_w", "pool_scale", "w_out", "w_mlp1",
             "w_mlp2", "final_norm_w")
    return (loss, grad_x) + tuple(res[n][i] for i in range(4) for n in order)
```

```python
import functools
import math

import numpy as np
import jax
import jax.numpy as jnp
from jax import lax
from jax.experimental import pallas as pl
from jax.experimental.pallas import tpu as pltpu

F32 = jnp.float32
BF16 = jnp.bfloat16
MXU = jnp.bfloat16

D = 1024
DEPTH = 2
GRID_W = 64
CTX = 256
EPS = 1e-6
SSD_HEADS = 6
SSD_P = 64
SSD_INNER = 384
SSD_N = 128
CHUNK = 128
XBC = 896
MLA_HEADS = 6
QK_NOPE = 64
QK_ROPE = 32
QK_DIM = 96
HP = 128
QW = MLA_HEADS * HP
POOL_DIM = 256
D_FF = 4096
FF_BLK = 1024
IN_COLS = 2092
NP = 2176
P_SPLITS = (384, 896, 256, 256, 256, 128)
DT0 = 32
CAT = QW + SSD_INNER + POOL_DIM

SB = 256
TM = 512
HALO = 8

ADAM_LR = 0.001
ADAM_B1 = 0.9
ADAM_B2 = 0.999
ADAM_EPS = 1e-08
ADAM_WD = 0.01
ADAM_STEP = 10

NT = (((1,), (1,)), ((), ()))
TN = (((0,), (0,)), ((), ()))


def _cp(vmem_mb=None):
    if vmem_mb is None:
        return pltpu.CompilerParams()
    return pltpu.CompilerParams(vmem_limit_bytes=vmem_mb << 20)


def _dot(a, b):
    return jnp.dot(a, b, preferred_element_type=F32)


def _dotg(a, b, dims):
    return lax.dot_general(a, b, dims, preferred_element_type=F32)


def _dot_hi(a, b, dims=None, sel_first=False):
    dims = (((1,), (0,)), ((), ())) if dims is None else dims
    v, s = (b, a) if sel_first else (a, b)
    hi = v.astype(BF16)
    lo = (v - hi.astype(F32)).astype(BF16)
    s = s.astype(BF16)
    if sel_first:
        return _dotg(s, hi, dims) + _dotg(s, lo, dims)
    return _dotg(hi, s, dims) + _dotg(lo, s, dims)


def _rms_hat(x):
    rstd = lax.rsqrt(jnp.mean(x * x, axis=-1, keepdims=True) + EPS)
    return x * rstd, rstd


def _rms_bwd(dn, xhat, rstd, w):
    dxhat = dn * w
    dx = rstd * (dxhat - xhat * jnp.mean(dxhat * xhat, axis=-1, keepdims=True))
    return dx, jnp.sum(dn * xhat, axis=0, keepdims=True)


def _sigmoid(z):
    return 1.0 / (1.0 + jnp.exp(-z))


def _colsum(a):
    return jnp.sum(a, axis=0, keepdims=True)


def _rowspec(cols, tm=TM):
    return pl.BlockSpec((tm, cols), lambda i: (i, 0))


def _fullspec(shape):
    n = len(shape)
    return pl.BlockSpec(shape, lambda *_: (0,) * n)


def _resident(shape):
    n = len(shape)
    return pl.BlockSpec(shape, lambda *_: (0,) * n, pipeline_mode=pl.Buffered(1))


def _halo_specs(cols, nrows, halo=HALO):
    per = SB // halo
    last = nrows // halo - 1
    prev = pl.BlockSpec((halo, cols), lambda i: (jnp.maximum(i * per - 1, 0), 0))
    nxt = pl.BlockSpec((halo, cols), lambda i: (jnp.minimum((i + 1) * per, last), 0))
    return prev, nxt


def _ext_rows(cur, prev, nxt, i, blocks_per_sample):
    j = i % blocks_per_sample
    first = jnp.logical_or(j == 0, j == 1)
    last = jnp.logical_or(j == 0, j == blocks_per_sample - 1)
    p = jnp.where(first, 0.0, prev.astype(F32))
    n = jnp.where(last, 0.0, nxt.astype(F32))
    return jnp.concatenate([p, cur.astype(F32), n], axis=0)


def _shift(ext, s):
    n = ext.shape[0]
    halo = (n - SB) // 2
    return pltpu.roll(ext, (-s) % n, axis=0)[halo:halo + SB, :]


def in_proj(x, bm, nw, w):
    R = x.shape[0]

    def body(x_ref, bm_ref, nw_ref, w_ref, h_ref, *outs):
        for s in range(TM // SB):
            rows = slice(s * SB, (s + 1) * SB)
            xhat, _ = _rms_hat(x_ref[rows, :])
            h = xhat * nw_ref[...] * (1.0 + bm_ref[s, 1:2, :]) + bm_ref[s, 0:1, :]
            h_ref[rows, :] = h.astype(h_ref.dtype)
        p = _dot(h_ref[...], w_ref[...])
        off = 0
        for o, n in zip(outs, P_SPLITS):
            o[...] = p[:, off:off + n].astype(o.dtype)
            off += n

    return pl.pallas_call(
        body, name="in_proj", grid=(R // TM,),
        in_specs=[_rowspec(D), pl.BlockSpec((TM // SB, 8, D), lambda i: (i, 0, 0)), _fullspec((1, D)),
                  _fullspec((D, NP))],
        out_specs=[_rowspec(D)] + [_rowspec(n) for n in P_SPLITS],
        out_shape=[jax.ShapeDtypeStruct((R, D), MXU)]
                  + [jax.ShapeDtypeStruct((R, n), dt) for n, dt in zip(P_SPLITS, (MXU, MXU, MXU, MXU, F32, F32))],
        compiler_params=_cp(56),
    )(x, bm, nw, w)


def in_proj_bwd(dx1, x, h, dz, dxbc, dqa, dkva, dpool, dkr, ddt, bm, nw, w):
    R = x.shape[0]

    def body(dx1_ref, x_ref, h_ref, dz_ref, dxbc_ref, dqa_ref, dkva_ref, dpool_ref, dkr_ref, ddt_ref, bm_ref, nw_ref,
             w_ref, dx_ref, dw_ref, part_ref, dp_ref):
        @pl.when(pl.program_id(0) == 0)
        def _():
            dw_ref[...] = jnp.zeros_like(dw_ref)

        dp_ref[:, 0:384] = dz_ref[...].astype(dp_ref.dtype)
        dp_ref[:, 384:1280] = dxbc_ref[...].astype(dp_ref.dtype)
        dp_ref[:, 1280:1536] = dqa_ref[...].astype(dp_ref.dtype)
        dp_ref[:, 1536:1792] = dkva_ref[...].astype(dp_ref.dtype)
        dp_ref[:, 1792:2048] = dpool_ref[...].astype(dp_ref.dtype)
        dp_ref[:, 2048:2176] = (dkr_ref[...] + ddt_ref[...]).astype(dp_ref.dtype)
        dw_ref[...] += _dotg(h_ref[...], dp_ref[...], TN)
        dh = _dotg(dp_ref[...], w_ref[...], NT)
        w = nw_ref[...]
        for s in range(TM // SB):
            rows = slice(s * SB, (s + 1) * SB)
            xhat, rstd = _rms_hat(x_ref[rows, :])
            dhs = dh[rows, :]
            sc1 = 1.0 + bm_ref[s, 1:2, :]
            dx, dnw = _rms_bwd(dhs * sc1, xhat, rstd, w)
            dx_ref[rows, :] = dx1_ref[rows, :] + dx
            part_ref[s] = jnp.concatenate(
                [_colsum(dhs), _colsum(dhs * xhat * w), dnw, jnp.zeros((5, D), F32)], axis=0)

    return pl.pallas_call(
        body, name="in_proj_bwd", grid=(R // TM,),
        in_specs=[_rowspec(D), _rowspec(D), _rowspec(D), _rowspec(384), _rowspec(896), _rowspec(256), _rowspec(256),
                  _rowspec(256), _rowspec(128), _rowspec(128),
                  pl.BlockSpec((TM // SB, 8, D), lambda i: (i, 0, 0)), _fullspec((1, D)), _resident((D, NP))],
        out_specs=[_rowspec(D), _fullspec((D, NP)), pl.BlockSpec((TM // SB, 8, D), lambda i: (i, 0, 0))],
        out_shape=[jax.ShapeDtypeStruct((R, D), F32), jax.ShapeDtypeStruct((D, NP), F32),
                   jax.ShapeDtypeStruct((R // SB, 8, D), F32)],
        scratch_shapes=[pltpu.VMEM((TM, NP), MXU)],
        compiler_params=_cp(56),
    )(dx1, x, h, dz, dxbc, dqa, dkva, dpool, dkr, ddt, bm, nw, w)


def mix_fwd(x, attn, ssd, pool, bm, wo):
    R = x.shape[0]

    def body(x_ref, a_ref, s_ref, p_ref, bm_ref, wo_ref, x1_ref, mix_ref, cat_ref):
        cat_ref[:, 0:QW] = a_ref[...].astype(cat_ref.dtype)
        cat_ref[:, QW:QW + SSD_INNER] = s_ref[...].astype(cat_ref.dtype)
        cat_ref[:, QW + SSD_INNER:CAT] = p_ref[...].astype(cat_ref.dtype)
        mix = _dot(cat_ref[...], wo_ref[...])
        mix_ref[...] = mix.astype(mix_ref.dtype)
        for s in range(TM // SB):
            rows = slice(s * SB, (s + 1) * SB)
            x1_ref[rows, :] = x_ref[rows, :] + bm_ref[s, 2:3, :] * mix[rows, :]

    return pl.pallas_call(
        body, name="mix_fwd", grid=(R // TM,),
        in_specs=[_rowspec(D), _rowspec(QW), _rowspec(SSD_INNER), _rowspec(POOL_DIM),
                  pl.BlockSpec((TM // SB, 8, D), lambda i: (i, 0, 0)), _fullspec((CAT, D))],
        out_specs=[_rowspec(D), _rowspec(D), _rowspec(CAT)],
        out_shape=[jax.ShapeDtypeStruct((R, D), F32), jax.ShapeDtypeStruct((R, D), MXU),
                   jax.ShapeDtypeStruct((R, CAT), MXU)],
        compiler_params=_cp(48),
    )(x, attn, ssd, pool, bm, wo)


def mix_bwd(dx1, mix, cat, bm, wo):
    R = dx1.shape[0]

    def body(dx1_ref, mix_ref, cat_ref, bm_ref, wo_ref, da_ref, ds_ref, dpl_ref, dw_ref, part_ref, dmb_ref):
        @pl.when(pl.program_id(0) == 0)
        def _():
            dw_ref[...] = jnp.zeros_like(dw_ref)

        for s in range(TM // SB):
            rows = slice(s * SB, (s + 1) * SB)
            d = dx1_ref[rows, :]
            dmb_ref[rows, :] = (d * bm_ref[s, 2:3, :]).astype(dmb_ref.dtype)
            part_ref[s] = jnp.concatenate([_colsum(d * mix_ref[rows, :].astype(F32)), jnp.zeros((7, D), F32)], axis=0)
        dw_ref[...] += _dotg(cat_ref[...], dmb_ref[...], TN)
        dcat = _dotg(dmb_ref[...], wo_ref[...], NT)
        da_ref[...] = dcat[:, 0:QW]
        ds_ref[...] = dcat[:, QW:QW + SSD_INNER]
        dpl_ref[...] = dcat[:, QW + SSD_INNER:CAT]

    return pl.pallas_call(
        body, name="mix_bwd", grid=(R // TM,),
        in_specs=[_rowspec(D), _rowspec(D), _rowspec(CAT), pl.BlockSpec((TM // SB, 8, D), lambda i: (i, 0, 0)),
                  _resident((CAT, D))],
        out_specs=[_rowspec(QW), _rowspec(SSD_INNER), _rowspec(POOL_DIM), _fullspec((CAT, D)),
                   pl.BlockSpec((TM // SB, 8, D), lambda i: (i, 0, 0))],
        out_shape=[jax.ShapeDtypeStruct((R, QW), F32), jax.ShapeDtypeStruct((R, SSD_INNER), F32),
                   jax.ShapeDtypeStruct((R, POOL_DIM), F32), jax.ShapeDtypeStruct((CAT, D), F32),
                   jax.ShapeDtypeStruct((R // SB, 8, D), F32)],
        scratch_shapes=[pltpu.VMEM((TM, D), MXU)],
        compiler_params=_cp(48),
    )(dx1, mix, cat, bm, wo)


def mlp_fwd(x1, bm, nw, w1, w2, side=None):
    R = x1.shape[0]

    def body(x1_ref, bm_ref, nw_ref, w1_ref, w2_ref, x2_ref, mo_ref, r_ref, h2_ref):
        for s in range(TM // SB):
            rows = slice(s * SB, (s + 1) * SB)
            xhat, _ = _rms_hat(x1_ref[rows, :])
            h = xhat * nw_ref[...] * (1.0 + bm_ref[s, 4:5, :]) + bm_ref[s, 3:4, :]
            h2_ref[rows, :] = h.astype(h2_ref.dtype)
        for j in range(D_FF // FF_BLK):
            cols = slice(j * FF_BLK, (j + 1) * FF_BLK)
            r = jnp.maximum(_dot(h2_ref[...], w1_ref[:, cols]), 0.0)
            r_ref[:, cols] = r.astype(r_ref.dtype)
            d = _dot((r * r).astype(MXU), w2_ref[cols, :])
            if j == 0:
                x2_ref[...] = d
            else:
                x2_ref[...] += d
        mo_ref[...] = x2_ref[...].astype(mo_ref.dtype)
        for s in range(TM // SB):
            rows = slice(s * SB, (s + 1) * SB)
            x2_ref[rows, :] = x1_ref[rows, :] + bm_ref[s, 5:6, :] * x2_ref[rows, :]

    grid = (R // TM,)
    body, side_in, side_out, side_shapes, side_scratch, side_args = _side_wrap(body, 5, 4, 0, side, grid)
    outs = pl.pallas_call(
        body, name="mlp_fwd" if side is None else "mlp_fwd_comm", grid=grid,
        in_specs=[_rowspec(D), pl.BlockSpec((TM // SB, 8, D), lambda i: (i, 0, 0)), _fullspec((1, D)),
                  _resident((D, D_FF)), _resident((D_FF, D))] + side_in,
        out_specs=[_rowspec(D), _rowspec(D), _rowspec(D_FF), _rowspec(D)] + side_out,
        out_shape=[jax.ShapeDtypeStruct((R, D), F32), jax.ShapeDtypeStruct((R, D), MXU),
                   jax.ShapeDtypeStruct((R, D_FF), BF16), jax.ShapeDtypeStruct((R, D), MXU)] + side_shapes,
        scratch_shapes=side_scratch,
        compiler_params=_cp(56),
    )(x1, bm, nw, w1, w2, *side_args)
    return tuple(outs[:4]) + (list(outs[4:]),)


def mlp_bwd(dx2, x1, mo, r, bm, nw, w2, w1, side=None):
    R = x1.shape[0]

    def body(dx2_ref, x1_ref, mo_ref, r_ref, bm_ref, nw_ref, w2_ref, w1_ref, dx1_ref, du_ref, dob_ref, part_ref,
             acc_ref):
        for s in range(TM // SB):
            rows = slice(s * SB, (s + 1) * SB)
            dob_ref[rows, :] = (dx2_ref[rows, :] * bm_ref[s, 5:6, :]).astype(dob_ref.dtype)
        for j in range(D_FF // FF_BLK):
            cols = slice(j * FF_BLK, (j + 1) * FF_BLK)
            du = _dotg(dob_ref[...], w2_ref[cols, :], NT) * (2.0 * r_ref[:, cols].astype(F32))
            du_ref[:, cols] = du.astype(du_ref.dtype)
            d = _dotg(du_ref[:, cols], w1_ref[:, cols], NT)
            if j == 0:
                acc_ref[...] = d
            else:
                acc_ref[...] += d
        w = nw_ref[...]
        for s in range(TM // SB):
            rows = slice(s * SB, (s + 1) * SB)
            xhat, rstd = _rms_hat(x1_ref[rows, :])
            dh = acc_ref[rows, :]
            dx, dnw = _rms_bwd(dh * (1.0 + bm_ref[s, 4:5, :]), xhat, rstd, w)
            d2 = dx2_ref[rows, :]
            dx1_ref[rows, :] = d2 + dx
            part_ref[s] = jnp.concatenate(
                [_colsum(dh), _colsum(dh * xhat * w), _colsum(d2 * mo_ref[rows, :].astype(F32)), dnw,
                 jnp.zeros((4, D), F32)], axis=0)

    grid = (R // TM,)
    body, side_in, side_out, side_shapes, side_scratch, side_args = _side_wrap(body, 8, 4, 1, side, grid)
    outs = pl.pallas_call(
        body, name="mlp_bwd" if side is None else "mlp_bwd_comm", grid=grid,
        in_specs=[_rowspec(D), _rowspec(D), _rowspec(D), _rowspec(D_FF),
                  pl.BlockSpec((TM // SB, 8, D), lambda i: (i, 0, 0)), _fullspec((1, D)),
                  _resident((D_FF, D)), _resident((D, D_FF))] + side_in,
        out_specs=[_rowspec(D), _rowspec(D_FF), _rowspec(D), pl.BlockSpec((TM // SB, 8, D), lambda i: (i, 0, 0))]
                  + side_out,
        out_shape=[jax.ShapeDtypeStruct((R, D), F32), jax.ShapeDtypeStruct((R, D_FF), MXU),
                   jax.ShapeDtypeStruct((R, D), MXU), jax.ShapeDtypeStruct((R // SB, 8, D), F32)] + side_shapes,
        scratch_shapes=[pltpu.VMEM((TM, D), F32)] + side_scratch,
        compiler_params=_cp(56),
    )(dx2, x1, mo, r, bm, nw, w2, w1, *side_args)
    return tuple(outs[:4]) + (list(outs[4:]),)


def mm_tn(a, b, square_a=False, name="mm_tn", col_blocks=False):
    R, M = a.shape
    N = b.shape[1]
    tm = M if M <= 1408 else 1024
    tn = N if N <= 2176 else 1024
    tk = next((c for c in ((2176, 1088, 512) if tm + tn <= 2048 else (1088, 512)) if R % c == 0), R)
    assert not col_blocks or tm == M

    def body(a_ref, b_ref, o_ref):
        @pl.when(pl.program_id(2) == 0)
        def _():
            o_ref[...] = jnp.zeros_like(o_ref)

        av = a_ref[...]
        if square_a:
            av = av.astype(F32)
            av = (av * av).astype(MXU)
        prod = _dotg(av.astype(MXU), b_ref[...].astype(MXU), TN)
        if col_blocks:
            o_ref[0] += prod
        else:
            o_ref[...] += prod

    if col_blocks:
        out_spec = pl.BlockSpec((1, tm, tn), lambda i, j, k: (j, 0, 0))
        out_shape = jax.ShapeDtypeStruct((N // tn, M, tn), F32)
    else:
        out_spec = pl.BlockSpec((tm, tn), lambda i, j, k: (i, j))
        out_shape = jax.ShapeDtypeStruct((M, N), F32)
    return pl.pallas_call(
        body, name=name, grid=(M // tm, N // tn, R // tk),
        in_specs=[pl.BlockSpec((tk, tm), lambda i, j, k: (k, i)), pl.BlockSpec((tk, tn), lambda i, j, k: (k, j))],
        out_specs=out_spec, out_shape=out_shape,
        compiler_params=_cp(48),
    )(a, b)


def final_loss(x, tgt, fw, blocks_per_sample):
    R = x.shape[0]
    nxb = blocks_per_sample - 1

    def body(x_ref, t_ref, fw_ref, dx_ref, part_ref):
        i = pl.program_id(0)
        is_ctx = (i % blocks_per_sample) == 0
        xhat, rstd = _rms_hat(x_ref[...])
        w = fw_ref[...]
        err = xhat * w - t_ref[...]
        dx, dfw = _rms_bwd(err * (1.0 / D), xhat, rstd, w)
        keep = jnp.where(is_ctx, 0.0, 1.0)
        dx_ref[...] = dx * keep
        part_ref[0] = jnp.concatenate([dfw * keep, _colsum(err * err) * keep, jnp.zeros((6, D), F32)], axis=0)

    def tmap(i):
        return ((i // blocks_per_sample) * nxb + jnp.maximum(i % blocks_per_sample - 1, 0), 0)

    return pl.pallas_call(
        body, name="final_loss", grid=(R // SB,),
        in_specs=[_rowspec(D, SB), pl.BlockSpec((SB, D), tmap), _fullspec((1, D))],
        out_specs=[_rowspec(D, SB), pl.BlockSpec((1, 8, D), lambda i: (i, 0, 0))],
        out_shape=[jax.ShapeDtypeStruct((R, D), F32), jax.ShapeDtypeStruct((R // SB, 8, D), F32)],
    )(x, tgt, fw)


def _softplus(v):
    return jnp.maximum(v, 0.0) + jnp.log(1.0 + jnp.exp(-jnp.abs(v)))


def _conv_taps(ext):
    return [_shift(ext, k - 1) for k in range(4)]


def _conv_out(taps, cw_ref, cb_ref):
    return (cb_ref[...] + cw_ref[0:1, :] * taps[0] + cw_ref[1:2, :] * taps[1] + cw_ref[2:3, :] * taps[2]
            + cw_ref[3:4, :] * taps[3])


def _dt_dir(v, d):
    lane = lax.broadcasted_iota(jnp.int32, v.shape, 1)
    return jnp.where(lane < SSD_HEADS, pltpu.roll(v, (128 - DT0 - SSD_HEADS * d) % 128, axis=1), 0.0)


def ssd_prep(pxbc, plast, cw, cb, dtb, blocks_per_sample):
    R = pxbc.shape[0]
    prev, nxt = _halo_specs(XBC, R, 8 * 4 // pxbc.dtype.itemsize)

    def body(cur_ref, prev_ref, nxt_ref, pl_ref, cw_ref, cb_ref, dtb_ref, xs_ref, bm_ref, cm_ref, dt_ref):
        i = pl.program_id(0)
        ext = _ext_rows(cur_ref[...], prev_ref[...], nxt_ref[...], i, blocks_per_sample)
        co = _conv_out(_conv_taps(ext), cw_ref, cb_ref)
        a = co * _sigmoid(co)
        xs_ref[...] = a[:, 0:384]
        bm_ref[...] = a[:, 384:640]
        cm_ref[...] = a[:, 640:896]
        sp = _softplus(pl_ref[...] + dtb_ref[...])
        dt_ref[0] = _dt_dir(sp, 0)
        dt_ref[1] = _dt_dir(sp, 1)

    return pl.pallas_call(
        body, name="ssd_prep", grid=(R // SB,),
        in_specs=[_rowspec(XBC, SB), prev, nxt, _rowspec(128, SB), _fullspec((8, XBC)), _fullspec((1, XBC)),
                  _fullspec((1, 128))],
        out_specs=[_rowspec(384, SB), _rowspec(256, SB), _rowspec(256, SB),
                   pl.BlockSpec((2, SB, 128), lambda i: (0, i, 0))],
        out_shape=[jax.ShapeDtypeStruct((R, 384), F32), jax.ShapeDtypeStruct((R, 256), F32),
                   jax.ShapeDtypeStruct((R, 256), F32), jax.ShapeDtypeStruct((2, R, 128), F32)],
    )(pxbc, pxbc, pxbc, plast, cw, cb, dtb)


def _chunk_index(d, s, nc):
    nctx = CTX // CHUNK
    back = jnp.where(s < nctx, nctx - 1 - s, nc + nctx - 1 - s)
    return jnp.where(d == 0, s, back)


def _scan_common(d, dt, arow, eexp, xs):
    ii = lax.broadcasted_iota(jnp.int32, (CHUNK, CHUNK), 0)
    jj = lax.broadcasted_iota(jnp.int32, (CHUNK, CHUNK), 1)
    mask = ((ii - jj) * (1 - 2 * d)) >= 0
    adt = dt * arow
    tmat = jnp.where(mask, 1.0, 0.0)
    cs = _dot_hi(tmat, adt, sel_first=True)
    tot = _colsum(adt)
    dtx = _dot_hi(dt, eexp)
    xt = xs * dtx
    ecs = jnp.exp(cs)
    ecx = _dot_hi(ecs, eexp)
    dte = jnp.exp(tot - cs)
    dtex = _dot_hi(dte, eexp)
    etot = jnp.exp(tot)
    etx = _dot_hi(jnp.broadcast_to(etot, (8, 128)), eexp)[0:1, :]
    return mask, tmat, adt, cs, tot, dtx, xt, ecs, ecx, dte, dtex, etot, etx


def _decay_matrix(mask, cs, cst, h):
    return jnp.exp(jnp.where(mask, cs[:, h:h + 1] - cst[h:h + 1, :], -1e30))


def _side_wrap(body, n_in, n_out, n_scratch, side, grid):
    if side is None:
        return body, [], [], [], [], []
    ni, no = len(side.ins), len(side.out_shapes)

    def wrapped(*refs):
        ins, refs = refs[:n_in], refs[n_in:]
        side_ins, refs = refs[:ni], refs[ni:]
        outs, refs = refs[:n_out], refs[n_out:]
        side_outs, refs = refs[:no], refs[no:]
        scratch, sems = refs[:n_scratch], refs[n_scratch:]
        ids = [pl.program_id(a) for a in range(len(grid))]
        first = functools.reduce(jnp.logical_and, [i == 0 for i in ids])
        last = functools.reduce(jnp.logical_and, [i == g - 1 for i, g in zip(ids, grid)])
        pl.when(first)(lambda: side.start(side_ins, side_outs, sems))
        body(*ins, *outs, *scratch)
        pl.when(last)(lambda: side.finish(side_ins, side_outs, sems))

    return wrapped, [ANY] * ni, [ANY] * no, list(side.out_shapes), _sems(side.nsem), list(side.ins)


def ssd_scan_fwd(xs, bm, cm, dtv, arow, eexp, nb, T, side=None):
    R = xs.shape[0]
    nc = T // CHUNK
    B = range(nb)

    def body(xs_ref, bm_ref, cm_ref, dt_ref, a_ref, e_ref, y_ref, hin_ref, st_ref):
        d = pl.program_id(0)
        s = pl.program_id(1)

        @pl.when(s == 0)
        def _():
            st_ref[...] = jnp.zeros_like(st_ref)

        eexp = e_ref[...]
        com = [_scan_common(d, dt_ref[0, b], a_ref[0, 0:1, :], eexp, xs_ref[b]) for b in B]
        mask = com[0][0]
        cs = [com[b][3] for b in B]
        cst = [cs[b].T for b in B]
        sin = [st_ref[b] for b in B]
        for b in B:
            hin_ref[0, b] = sin[b]
        sb = [sin[b].astype(MXU) for b in B]
        xtb = [com[b][6].astype(MXU) for b in B]
        xw = [(com[b][6] * com[b][10]).astype(MXU) for b in B]
        g0 = lax.broadcasted_iota(jnp.int32, (CHUNK, SSD_INNER), 1) < 192
        lane = lax.broadcasted_iota(jnp.int32, (CHUNK, 128), 1)
        c = [[cm_ref[b, :, 0:128].astype(MXU), cm_ref[b, :, 128:256].astype(MXU)] for b in B]
        bq = [[bm_ref[b, :, 0:128].astype(MXU), bm_ref[b, :, 128:256].astype(MXU)] for b in B]
        y = [jnp.where(g0, _dot(c[b][0], sb[b]), _dot(c[b][1], sb[b])) * com[b][8] for b in B]
        cb = [[_dotg(c[b][g], bq[b][g], NT) for g in range(2)] for b in B]
        blocks = [[] for _ in B]
        for blk in range(3):
            acc = [None for _ in B]
            for hh in range(2):
                h = blk * 2 + hh
                for b in B:
                    m = (cb[b][h // 3] * _decay_matrix(mask, cs[b], cst[b], h)).astype(MXU)
                    res = _dot(m, xtb[b][:, blk * 128:(blk + 1) * 128])
                    acc[b] = res if hh == 0 else jnp.where(lane < 64, acc[b], res)
            for b in B:
                blocks[b].append(acc[b])
        for b in B:
            y_ref[0, b] = y[b] + jnp.concatenate(blocks[b], axis=1)
            st_ref[b] = sin[b] * com[b][12] + jnp.where(g0, _dotg(bq[b][0], xw[b], TN), _dotg(bq[b][1], xw[b], TN))

    def rows(cols):
        return pl.BlockSpec((nb, CHUNK, cols), lambda d, s: (0, _chunk_index(d, s, nc), 0))

    def by_dir(cols):
        return pl.BlockSpec((1, nb, CHUNK, cols), lambda d, s: (d, 0, _chunk_index(d, s, nc), 0))

    grid = (2, nc)
    body, side_in, side_out, side_shapes, side_scratch, side_args = _side_wrap(body, 6, 2, 1, side, grid)
    outs = pl.pallas_call(
        body, name="ssd_scan_fwd" if side is None else "ssd_scan_fwd_comm", grid=grid,
        in_specs=[rows(384), rows(256), rows(256), by_dir(128), pl.BlockSpec((1, 8, 128), lambda d, s: (d, 0, 0)),
                  pl.BlockSpec((128, 384), lambda d, s: (0, 0))] + side_in,
        out_specs=[by_dir(384),
                   pl.BlockSpec((1, nb, CHUNK, 384), lambda d, s: (d * nc + _chunk_index(d, s, nc), 0, 0, 0))] + side_out,
        out_shape=[jax.ShapeDtypeStruct((2, nb, T, 384), F32), jax.ShapeDtypeStruct((2 * nc, nb, CHUNK, 384), F32)]
                  + side_shapes,
        scratch_shapes=[pltpu.VMEM((nb, CHUNK, 384), F32)] + side_scratch,
    )(xs.reshape(nb, T, 384), bm.reshape(nb, T, 256), cm.reshape(nb, T, 256), dtv.reshape(2, nb, T, 128), arow, eexp,
      *side_args)
    return outs[0].reshape(2, R, 384), outs[1], list(outs[2:])


def ssd_scan_bwd(xs, bm, cm, dtv, arow, eexp, hin, dy, nb, T, side=None):
    R = xs.shape[0]
    nc = T // CHUNK
    B = range(nb)

    def chunk(d, s):
        return _chunk_index(d, nc - 1 - s, nc)

    def body(xs_ref, bm_ref, cm_ref, dt_ref, a_ref, e_ref, hin_ref, dy_ref,
             dxs_ref, dbm_ref, dcm_ref, ddt_ref, da_ref, ds_ref):
        d = pl.program_id(0)
        s = pl.program_id(1)

        @pl.when(s == 0)
        def _():
            ds_ref[...] = jnp.zeros_like(ds_ref)
            da_ref[...] = jnp.zeros_like(da_ref)

        eexp = e_ref[...]
        arow = a_ref[0, 0:1, :]
        dt = [dt_ref[0, b] for b in B]
        xs_v = [xs_ref[b] for b in B]
        com = [_scan_common(d, dt[b], arow, eexp, xs_v[b]) for b in B]
        mask, tmat = com[0][0], com[0][1]
        cs, dtx, xt, ecs, ecx, dte, dtex, etot, etx = [[com[b][i] for b in B] for i in (3, 5, 6, 7, 8, 9, 10, 11, 12)]
        cst = [cs[b].T for b in B]
        sin = [hin_ref[0, b] for b in B]
        sb = [sin[b].astype(MXU) for b in B]
        dsp = [ds_ref[b] for b in B]
        dyv = [dy_ref[b] for b in B]
        xtb = [xt[b].astype(MXU) for b in B]
        xw = [(xt[b] * dtex[b]).astype(MXU) for b in B]
        g0 = lax.broadcasted_iota(jnp.int32, (CHUNK, SSD_INNER), 1) < 192
        lane = lax.broadcasted_iota(jnp.int32, (CHUNK, 128), 1)
        sub = lax.broadcasted_iota(jnp.int32, (CHUNK, 128), 0)
        c = [[cm_ref[b, :, 0:128].astype(MXU), cm_ref[b, :, 128:256].astype(MXU)] for b in B]
        bq = [[bm_ref[b, :, 0:128].astype(MXU), bm_ref[b, :, 128:256].astype(MXU)] for b in B]

        cs_prod = [jnp.where(g0, _dot(c[b][0], sb[b]), _dot(c[b][1], sb[b])) for b in B]
        dcsp = [dyv[b] * ecx[b] for b in B]
        dcsp_g = [[jnp.where(g0, dcsp[b], 0.0).astype(MXU), jnp.where(g0, 0.0, dcsp[b]).astype(MXU)] for b in B]
        dcs = [_dot_hi(dyv[b] * cs_prod[b], eexp, NT) * ecs[b] for b in B]
        dc = [[_dotg(dcsp_g[b][g], sb[b], NT) for g in range(2)] for b in B]
        dsin = [_dotg(c[b][0], dcsp_g[b][0], TN) + _dotg(c[b][1], dcsp_g[b][1], TN) + dsp[b] * etx[b] for b in B]

        dtot = [_dot_hi(jnp.broadcast_to(_colsum(dsp[b] * sin[b]), (8, SSD_INNER)), eexp, NT)[0:1, :] * etot[b] for b in B]
        dsp_g = [[jnp.where(g0, dsp[b], 0.0).astype(MXU), jnp.where(g0, 0.0, dsp[b]).astype(MXU)] for b in B]
        dxw = [_dot(bq[b][0], dsp_g[b][0]) + _dot(bq[b][1], dsp_g[b][1]) for b in B]
        db = [[_dotg(xw[b], dsp_g[b][g], NT) for g in range(2)] for b in B]
        dxt = [dxw[b] * dtex[b] for b in B]
        ddte = [_dot_hi(dxw[b] * xt[b], eexp, NT) * dte[b] for b in B]
        dtot = [dtot[b] + _colsum(ddte[b]) for b in B]
        dcs = [dcs[b] - ddte[b] for b in B]

        cb = [[_dotg(c[b][g], bq[b][g], NT) for g in range(2)] for b in B]
        dg = [[jnp.zeros((CHUNK, CHUNK), F32), jnp.zeros((CHUNK, CHUNK), F32)] for _ in B]
        dcs_rows = [jnp.zeros((CHUNK, 128), F32) for _ in B]
        dxt_blocks = [[] for _ in B]
        for blk in range(3):
            acc = [jnp.zeros((CHUNK, 128), F32) for _ in B]
            for hh in range(2):
                h = blk * 2 + hh
                g = h // 3
                mine = (lane < 64) if hh == 0 else (lane >= 64)
                for b in B:
                    dyh = jnp.where(mine, dyv[b][:, blk * 128:(blk + 1) * 128], 0.0).astype(MXU)
                    lh = _decay_matrix(mask, cs[b], cst[b], h)
                    m = cb[b][g] * lh
                    dm = _dotg(dyh, xtb[b][:, blk * 128:(blk + 1) * 128], NT)
                    acc[b] = acc[b] + _dotg(m.astype(MXU), dyh, TN)
                    dg[b][g] = dg[b][g] + dm * lh
                    q = dm * m
                    dcs[b] = dcs[b] + jnp.where(lane == h, jnp.sum(q, axis=1, keepdims=True), 0.0)
                    dcs_rows[b] = dcs_rows[b] - jnp.where(sub == h, jnp.sum(q, axis=0, keepdims=True), 0.0)
            for b in B:
                dxt_blocks[b].append(acc[b])
        for b in B:
            dxt[b] = dxt[b] + jnp.concatenate(dxt_blocks[b], axis=1)
            for g in range(2):
                dgb = dg[b][g].astype(MXU)
                dc[b][g] = dc[b][g] + _dot(dgb, bq[b][g])
                db[b][g] = db[b][g] + _dotg(dgb, c[b][g], TN)
            dcs[b] = dcs[b] + dcs_rows[b].T

        for b in B:
            dadt = _dot_hi(tmat, dcs[b], TN, sel_first=True) + dtot[b]
            ddt_ref[0, b] = dadt * arow + _dot_hi(dxt[b] * xs_v[b], eexp, NT)
            da_ref[0, b, 0:1, :] += _colsum(dadt * dt[b])
            dxs_ref[0, b] = (dxt[b] * dtx[b]).astype(dxs_ref.dtype)
            dbm_ref[0, b] = jnp.concatenate(db[b], axis=1).astype(dbm_ref.dtype)
            dcm_ref[0, b] = jnp.concatenate(dc[b], axis=1).astype(dcm_ref.dtype)
            ds_ref[b] = dsin[b]

    def rows(cols):
        return pl.BlockSpec((nb, CHUNK, cols), lambda d, s: (0, chunk(d, s), 0))

    def by_dir(cols):
        return pl.BlockSpec((1, nb, CHUNK, cols), lambda d, s: (d, 0, chunk(d, s), 0))

    grid = (2, nc)
    body, side_in, side_out, side_shapes, side_scratch, side_args = _side_wrap(body, 8, 5, 1, side, grid)
    outs = pl.pallas_call(
        body, name="ssd_scan_bwd" if side is None else "ssd_scan_bwd_comm", grid=grid,
        in_specs=[rows(384), rows(256), rows(256), by_dir(128), pl.BlockSpec((1, 8, 128), lambda d, s: (d, 0, 0)),
                  pl.BlockSpec((128, 384), lambda d, s: (0, 0)),
                  pl.BlockSpec((1, nb, CHUNK, 384), lambda d, s: (d * nc + chunk(d, s), 0, 0, 0)), rows(384)] + side_in,
        out_specs=[by_dir(384), by_dir(256), by_dir(256), by_dir(128),
                   pl.BlockSpec((1, nb, 8, 128), lambda d, s: (d, 0, 0, 0))] + side_out,
        out_shape=[jax.ShapeDtypeStruct((2, nb, T, 384), MXU), jax.ShapeDtypeStruct((2, nb, T, 256), MXU),
                   jax.ShapeDtypeStruct((2, nb, T, 256), MXU), jax.ShapeDtypeStruct((2, nb, T, 128), F32),
                   jax.ShapeDtypeStruct((2, nb, 8, 128), F32)] + side_shapes,
        scratch_shapes=[pltpu.VMEM((nb, CHUNK, 384), F32)] + side_scratch,
    )(xs.reshape(nb, T, 384), bm.reshape(nb, T, 256), cm.reshape(nb, T, 256), dtv.reshape(2, nb, T, 128), arow, eexp,
      hin, dy.reshape(nb, T, 384), *side_args)
    return (outs[0].reshape(2, R, 384), outs[1].reshape(2, R, 256), outs[2].reshape(2, R, 256),
            outs[3].reshape(2, R, 128), outs[4], list(outs[5:]))


def _group_rms(g):
    lane = lax.broadcasted_iota(jnp.int32, g.shape, 1)
    g0 = lane < 192
    gg = g * g
    s0 = jnp.sum(jnp.where(g0, gg, 0.0), axis=-1, keepdims=True)
    s1 = jnp.sum(gg, axis=-1, keepdims=True) - s0
    rstd = jnp.where(g0, lax.rsqrt(s0 * (1.0 / 192) + EPS), lax.rsqrt(s1 * (1.0 / 192) + EPS))
    return rstd, g0


def ssd_out_fwd(y2, xs, pz, dexp, nw):
    R = xs.shape[0]

    def body(y_ref, xs_ref, z_ref, d_ref, nw_ref, o_ref):
        z = z_ref[...].astype(F32)
        yy = y_ref[0] + y_ref[1] + xs_ref[...] * d_ref[...]
        g = yy * (z * _sigmoid(z))
        rstd, _ = _group_rms(g)
        o_ref[...] = g * rstd * nw_ref[...]

    return pl.pallas_call(
        body, name="ssd_out_fwd", grid=(R // TM,),
        in_specs=[pl.BlockSpec((2, TM, 384), lambda i: (0, i, 0)), _rowspec(384), _rowspec(384),
                  _fullspec((1, 384)), _fullspec((1, 384))],
        out_specs=_rowspec(384),
        out_shape=jax.ShapeDtypeStruct((R, 384), F32),
    )(y2, xs, pz, dexp, nw)


def ssd_out_bwd(dout, y2, xs, pz, dexp, nw):
    R = xs.shape[0]

    def body(do_ref, y_ref, xs_ref, z_ref, d_ref, nw_ref, dy_ref, dz_ref, dxs_ref, part_ref):
        z = z_ref[...].astype(F32)
        xs_v = xs_ref[...]
        yy = y_ref[0] + y_ref[1] + xs_v * d_ref[...]
        sig = _sigmoid(z)
        sz = z * sig
        g = yy * sz
        rstd, g0 = _group_rms(g)
        ghat = g * rstd
        do = do_ref[...]
        dgn = do * nw_ref[...]
        t = dgn * ghat
        t0 = jnp.sum(jnp.where(g0, t, 0.0), axis=-1, keepdims=True)
        t1 = jnp.sum(t, axis=-1, keepdims=True) - t0
        dg = rstd * (dgn - ghat * jnp.where(g0, t0, t1) * (1.0 / 192))
        dyy = dg * sz
        dy_ref[...] = dyy
        dz_ref[...] = (dg * yy * (sig * (1.0 + z * (1.0 - sig)))).astype(dz_ref.dtype)
        dxs_ref[...] = dyy * d_ref[...]
        part_ref[0] = jnp.concatenate([_colsum(do * ghat), _colsum(dyy * xs_v), jnp.zeros((6, 384), F32)], axis=0)

    return pl.pallas_call(
        body, name="ssd_out_bwd", grid=(R // TM,),
        in_specs=[_rowspec(384), pl.BlockSpec((2, TM, 384), lambda i: (0, i, 0)), _rowspec(384), _rowspec(384),
                  _fullspec((1, 384)), _fullspec((1, 384))],
        out_specs=[_rowspec(384), _rowspec(384), _rowspec(384), pl.BlockSpec((1, 8, 384), lambda i: (i, 0, 0))],
        out_shape=[jax.ShapeDtypeStruct((R, 384), F32), jax.ShapeDtypeStruct((R, 384), MXU),
                   jax.ShapeDtypeStruct((R, 384), F32), jax.ShapeDtypeStruct((R // TM, 8, 384), F32)],
    )(dout, y2, xs, pz, dexp, nw)


def ssd_prep_bwd_a(pxbc, plast, cw, cb, dtb, dxs_skip, dxs2, dbm2, dcm2, ddt2, blocks_per_sample):
    R = pxbc.shape[0]
    prev, nxt = _halo_specs(XBC, R, 8 * 4 // pxbc.dtype.itemsize)

    def body(cur_ref, prev_ref, nxt_ref, pl_ref, cw_ref, cb_ref, dtb_ref, dsk_ref, dxs_ref, dbm_ref, dcm_ref, ddt_ref,
             dpre_ref, dlast_ref, part_ref):
        i = pl.program_id(0)
        ext = _ext_rows(cur_ref[...], prev_ref[...], nxt_ref[...], i, blocks_per_sample)
        taps = _conv_taps(ext)
        co = _conv_out(taps, cw_ref, cb_ref)
        sig = _sigmoid(co)
        both = lambda ref: ref[0].astype(F32) + ref[1].astype(F32)
        up = jnp.concatenate([dsk_ref[...] + both(dxs_ref), both(dbm_ref), both(dcm_ref)], axis=1)
        dpre = up * (sig * (1.0 + co * (1.0 - sig)))
        dpre_ref[...] = dpre
        raw = pl_ref[...] + dtb_ref[...]
        lane = lax.broadcasted_iota(jnp.int32, raw.shape, 1)
        ddt = (pltpu.roll(ddt_ref[0], DT0, axis=1) + pltpu.roll(ddt_ref[1], DT0 + SSD_HEADS, axis=1))
        ddt = jnp.where(jnp.logical_and(lane >= DT0, lane < DT0 + 2 * SSD_HEADS), ddt * _sigmoid(raw), 0.0)
        dlast_ref[...] = ddt.astype(dlast_ref.dtype)
        rows = [_colsum(dpre * taps[k]) for k in range(4)]
        rows.append(_colsum(dpre))
        rows.append(jnp.concatenate([_colsum(ddt), jnp.zeros((1, XBC - 128), F32)], axis=1))
        rows.append(jnp.zeros((2, XBC), F32))
        part_ref[0] = jnp.concatenate(rows, axis=0)

    dirspec = lambda n: pl.BlockSpec((2, SB, n), lambda i: (0, i, 0))
    return pl.pallas_call(
        body, name="ssd_prep_bwd_a", grid=(R // SB,),
        in_specs=[_rowspec(XBC, SB), prev, nxt, _rowspec(128, SB), _fullspec((8, XBC)), _fullspec((1, XBC)),
                  _fullspec((1, 128)), _rowspec(384, SB), dirspec(384), dirspec(256), dirspec(256), dirspec(128)],
        out_specs=[_rowspec(XBC, SB), _rowspec(128, SB), pl.BlockSpec((1, 8, XBC), lambda i: (i, 0, 0))],
        out_shape=[jax.ShapeDtypeStruct((R, XBC), F32), jax.ShapeDtypeStruct((R, 128), MXU),
                   jax.ShapeDtypeStruct((R // SB, 8, XBC), F32)],
    )(pxbc, pxbc, pxbc, plast, cw, cb, dtb, dxs_skip, dxs2, dbm2, dcm2, ddt2)


def ssd_prep_bwd_b(dpre, cw, blocks_per_sample):
    R = dpre.shape[0]
    prev, nxt = _halo_specs(XBC, R)

    def body(cur_ref, prev_ref, nxt_ref, cw_ref, o_ref):
        i = pl.program_id(0)
        ext = _ext_rows(cur_ref[...], prev_ref[...], nxt_ref[...], i, blocks_per_sample)
        o_ref[...] = (cw_ref[0:1, :] * _shift(ext, 1) + cw_ref[1:2, :] * _shift(ext, 0)
                      + cw_ref[2:3, :] * _shift(ext, -1) + cw_ref[3:4, :] * _shift(ext, -2)).astype(o_ref.dtype)

    return pl.pallas_call(
        body, name="ssd_prep_bwd_b", grid=(R // SB,),
        in_specs=[_rowspec(XBC, SB), prev, nxt, _fullspec((8, XBC))],
        out_specs=_rowspec(XBC, SB),
        out_shape=jax.ShapeDtypeStruct((R, XBC), MXU),
    )(dpre, dpre, dpre, cw)


def _rope(u, cos, sa, sb):
    return u * cos + pltpu.roll(u, 120, axis=1) * sa + pltpu.roll(u, 8, axis=1) * sb


def _rope_t(du, cos, sa, sb):
    return du * cos + pltpu.roll(du * sa, 8, axis=1) + pltpu.roll(du * sb, 120, axis=1)


def mla_prep(pqa, pkva, plast, qnw, kvnw, wq, wk, wv, cos, sa, sb):
    R = pqa.shape[0]

    def body(qa_ref, kva_ref, pl_ref, qnw_ref, kvnw_ref, wq_ref, wk_ref, wv_ref, cos_ref, sa_ref, sb_ref,
             q_ref, k_ref, v_ref, cq_ref, ckv_ref):
        cos_v, sa_v, sb_v = cos_ref[...], sa_ref[...], sb_ref[...]
        xq, _ = _rms_hat(qa_ref[...].astype(F32))
        cq_ref[...] = (xq * qnw_ref[...]).astype(cq_ref.dtype)
        xkv, _ = _rms_hat(kva_ref[...].astype(F32))
        ckv_ref[...] = (xkv * kvnw_ref[...]).astype(ckv_ref.dtype)
        q = _dot(cq_ref[...], wq_ref[...])
        kn = _dot(ckv_ref[...], wk_ref[...])
        v_ref[...] = _dot(ckv_ref[...], wv_ref[...]).astype(v_ref.dtype)
        lane = lax.broadcasted_iota(jnp.int32, (TM, HP), 1)
        rope_lanes = jnp.logical_and(lane >= QK_NOPE, lane < QK_DIM)
        kr = _rope(jnp.where(rope_lanes, pltpu.roll(pl_ref[...], QK_NOPE, axis=1), 0.0), cos_v, sa_v, sb_v)
        for h in range(MLA_HEADS):
            cols = slice(h * HP, (h + 1) * HP)
            q_ref[:, cols] = (_rope(q[:, cols], cos_v, sa_v, sb_v) * Q_SCALE).astype(q_ref.dtype)
            k_ref[:, cols] = (kn[:, cols] + kr).astype(k_ref.dtype)

    return pl.pallas_call(
        body, name="mla_prep", grid=(R // TM,),
        in_specs=[_rowspec(256), _rowspec(256), _rowspec(128), _fullspec((1, 256)), _fullspec((1, 256)),
                  _fullspec((256, QW)), _fullspec((256, QW)), _fullspec((256, QW)),
                  _rowspec(HP), _rowspec(HP), _rowspec(HP)],
        out_specs=[_rowspec(QW), _rowspec(QW), _rowspec(QW), _rowspec(256), _rowspec(256)],
        out_shape=[jax.ShapeDtypeStruct((R, QW), MXU)] * 3 + [jax.ShapeDtypeStruct((R, 256), MXU)] * 2,
    )(pqa, pkva, plast, qnw, kvnw, wq, wk, wv, cos, sa, sb)


def mla_prep_bwd(dq, dk, dv, pqa, pkva, cq, ckv, qnw, kvnw, wq, wk, wv, cos, sa, sb):
    R = pqa.shape[0]

    def body(dq_ref, dk_ref, dv_ref, qa_ref, kva_ref, cq_ref, ckv_ref, qnw_ref, kvnw_ref, wq_ref, wk_ref, wv_ref,
             cos_ref, sa_ref, sb_ref, dqa_ref, dkva_ref, dkr_ref, dwq_ref, dwk_ref, dwv_ref, part_ref,
             dql_ref, dkm_ref, dvb_ref):
        @pl.when(pl.program_id(0) == 0)
        def _():
            dwq_ref[...] = jnp.zeros_like(dwq_ref)
            dwk_ref[...] = jnp.zeros_like(dwk_ref)
            dwv_ref[...] = jnp.zeros_like(dwv_ref)

        cos_v, sa_v, sb_v = cos_ref[...], sa_ref[...], sb_ref[...]
        lane = lax.broadcasted_iota(jnp.int32, (TM, HP), 1)
        rope_lanes = jnp.logical_and(lane >= QK_NOPE, lane < QK_DIM)
        dkr = jnp.zeros((TM, HP), F32)
        for h in range(MLA_HEADS):
            cols = slice(h * HP, (h + 1) * HP)
            dql_ref[:, cols] = (_rope_t(dq_ref[:, cols], cos_v, sa_v, sb_v) * ATT_SCALE).astype(dql_ref.dtype)
            dkh = dk_ref[:, cols] * LN2
            dkm_ref[:, cols] = jnp.where(lane < QK_NOPE, dkh, 0.0).astype(dkm_ref.dtype)
            dkr = dkr + jnp.where(rope_lanes, dkh, 0.0)
        dvb_ref[...] = dv_ref[...].astype(dvb_ref.dtype)
        dkr = jnp.where(rope_lanes, _rope_t(dkr, cos_v, sa_v, sb_v), 0.0)
        dkr_ref[...] = pltpu.roll(dkr, HP - QK_NOPE, axis=1).astype(dkr_ref.dtype)
        dwq_ref[...] += _dotg(cq_ref[...], dql_ref[...], TN)
        dwk_ref[...] += _dotg(ckv_ref[...], dkm_ref[...], TN)
        dwv_ref[...] += _dotg(ckv_ref[...], dvb_ref[...], TN)
        xq, rq = _rms_hat(qa_ref[...].astype(F32))
        dqa, dqnw = _rms_bwd(_dotg(dql_ref[...], wq_ref[...], NT), xq, rq, qnw_ref[...])
        dqa_ref[...] = dqa.astype(dqa_ref.dtype)
        xkv, rkv = _rms_hat(kva_ref[...].astype(F32))
        dckv = _dotg(dkm_ref[...], wk_ref[...], NT) + _dotg(dvb_ref[...], wv_ref[...], NT)
        dkva, dkvnw = _rms_bwd(dckv, xkv, rkv, kvnw_ref[...])
        dkva_ref[...] = dkva.astype(dkva_ref.dtype)
        part_ref[0] = jnp.concatenate([dqnw, dkvnw, jnp.zeros((6, 256), F32)], axis=0)

    return pl.pallas_call(
        body, name="mla_prep_bwd", grid=(R // TM,),
        in_specs=[_rowspec(QW), _rowspec(QW), _rowspec(QW), _rowspec(256), _rowspec(256), _rowspec(256), _rowspec(256),
                  _fullspec((1, 256)), _fullspec((1, 256)), _fullspec((256, QW)), _fullspec((256, QW)),
                  _fullspec((256, QW)), _rowspec(HP), _rowspec(HP), _rowspec(HP)],
        out_specs=[_rowspec(256), _rowspec(256), _rowspec(128), _fullspec((256, QW)), _fullspec((256, QW)),
                   _fullspec((256, QW)), pl.BlockSpec((1, 8, 256), lambda i: (i, 0, 0))],
        out_shape=[jax.ShapeDtypeStruct((R, 256), MXU), jax.ShapeDtypeStruct((R, 256), MXU),
                   jax.ShapeDtypeStruct((R, 128), MXU)] + [jax.ShapeDtypeStruct((256, QW), F32)] * 3
                  + [jax.ShapeDtypeStruct((R // TM, 8, 256), F32)],
        scratch_shapes=[pltpu.VMEM((TM, QW), MXU)] * 3,
    )(dq, dk, dv, pqa, pkva, cq, ckv, qnw, kvnw, wq, wk, wv, cos, sa, sb)


ATT_SCALE = QK_DIM ** -0.5
TQ = 256


LOG2E = 1.4426950408889634
LN2 = 0.6931471805599453
Q_SCALE = ATT_SCALE * LOG2E


def _key_chunks(T, n=2):
    unit = 256 if T % 256 == 0 else 128
    units = T // unit
    sizes = [(units // n + (1 if i < units % n else 0)) * unit for i in range(n)]
    return [(sum(sizes[:i]), sz) for i, sz in enumerate(sizes) if sz]


def attn_fwd(q, k, v, nb, T):
    R = q.shape[0]
    nq = T // TQ
    chunks = _key_chunks(T, 4)
    HEADS = range(3)

    def body(q_ref, k_ref, v_ref, o_ref, lse_ref):
        def lanes(h):
            return slice(h * HP, (h + 1) * HP)

        def logits(h, lo, n):
            return _dotg(q_ref[:, lanes(h)], k_ref[lo:lo + n, lanes(h)], NT)

        def weigh(h, s, lo, n):
            m = jnp.max(s, axis=-1, keepdims=True)
            p = jnp.exp2(s - m)
            return m, jnp.sum(p, axis=-1, keepdims=True), _dot(p.astype(MXU), v_ref[lo:lo + n, lanes(h)])

        def parts_of(ranges):
            out = [[] for _ in HEADS]
            s = [logits(h, *ranges[0]) for h in HEADS]
            for j, (lo, n) in enumerate(ranges):
                nxt = [logits(h, *ranges[j + 1]) for h in HEADS] if j + 1 < len(ranges) else None
                for h in HEADS:
                    out[h].append(weigh(h, s[h], lo, n))
                s = nxt
            return out

        def finish(all_parts):
            for h, parts in enumerate(all_parts):
                m = parts[0][0]
                for pm, _, _ in parts[1:]:
                    m = jnp.maximum(m, pm)
                l, o = 0.0, 0.0
                for pm, pl_, po in parts:
                    a = jnp.exp2(pm - m)
                    l = l + a * pl_
                    o = o + a * po
                o_ref[:, lanes(h)] = o / l
                lse_ref[:, lanes(h)] = jnp.broadcast_to(m + jnp.log(l) * LOG2E, (TQ, HP))

        i = pl.program_id(2)
        pl.when(i == 0)(lambda: finish(parts_of([(0, CTX)])))
        pl.when(i > 0)(lambda: finish(parts_of(chunks)))

    qspec = pl.BlockSpec((TQ, len(HEADS) * HP), lambda b, h, i: (b * nq + i, h))
    kspec = pl.BlockSpec((T, len(HEADS) * HP), lambda b, h, i: (b, h))
    return pl.pallas_call(
        body, name="attn_fwd", grid=(nb, MLA_HEADS // len(HEADS), nq),
        in_specs=[qspec, kspec, kspec], out_specs=[qspec, qspec],
        out_shape=[jax.ShapeDtypeStruct((R, QW), F32)] * 2,
        compiler_params=_cp(48),
    )(q, k, v)


def attn_bwd(q, k, v, o, lse, do, nb, T):
    R = q.shape[0]
    nq = T // TQ
    chunks = _key_chunks(T)

    def body(q_ref, k_ref, v_ref, o_ref, lse_ref, do_ref, dq_ref, dk_ref, dv_ref):
        i = pl.program_id(2)

        @pl.when(i == 0)
        def _():
            dk_ref[...] = jnp.zeros_like(dk_ref)
            dv_ref[...] = jnp.zeros_like(dv_ref)

        def run(chunks):
            for h in range(2):
                lanes = slice(h * HP, (h + 1) * HP)
                qv = q_ref[:, lanes]
                dov = do_ref[:, lanes]
                dob = dov.astype(MXU)
                delta = jnp.sum(dov * o_ref[:, lanes], axis=-1, keepdims=True)
                lse_v = lse_ref[:, h * HP:h * HP + 1]
                dq = 0.0
                for lo, n in chunks:
                    kv = k_ref[lo:lo + n, lanes]
                    p = jnp.exp2(_dotg(qv, kv, NT) - lse_v)
                    dp = _dotg(dob, v_ref[lo:lo + n, lanes], NT)
                    dsb = (p * (dp - delta)).astype(MXU)
                    dq = dq + _dot(dsb, kv)
                    dk_ref[lo:lo + n, lanes] += _dotg(dsb, qv, TN)
                    dv_ref[lo:lo + n, lanes] += _dotg(p.astype(MXU), dob, TN)
                dq_ref[:, lanes] = dq

        pl.when(i == 0)(lambda: run([(0, CTX)]))
        pl.when(i > 0)(lambda: run(chunks))

    qspec = pl.BlockSpec((TQ, 2 * HP), lambda b, h, i: (b * nq + i, h))
    kspec = pl.BlockSpec((T, 2 * HP), lambda b, h, i: (b, h))
    return pl.pallas_call(
        body, name="attn_bwd", grid=(nb, MLA_HEADS // 2, nq),
        in_specs=[qspec, kspec, kspec, qspec, qspec, qspec],
        out_specs=[qspec, kspec, kspec],
        out_shape=[jax.ShapeDtypeStruct((R, QW), F32)] * 3,
        compiler_params=_cp(56),
    )(q, k, v, o, lse, do)


def _pool_geometry(i, blocks_per_sample, seq):
    j = i % blocks_per_sample
    n = jnp.where(j == 0, CTX, seq)
    t0 = jnp.where(j == 0, 0, (j - 1) * SB) - HALO
    lane = lax.broadcasted_iota(jnp.int32, (SB + 2 * HALO, POOL_DIM), 1)
    t = lax.broadcasted_iota(jnp.int32, (SB + 2 * HALO, POOL_DIM), 0) + t0
    wh = jnp.where(lane < 64, 1, jnp.where(lane < 128, 2, jnp.where(lane < 192, 4, 8)))
    cnt = jnp.minimum(t + wh, n) - jnp.maximum(t - wh, 0)
    return lane, 1.0 / jnp.maximum(cnt, 1).astype(F32)


def _by_window(lane, c2, c4, c8, c16):
    return jnp.where(lane < 64, c2, jnp.where(lane < 128, c4, jnp.where(lane < 192, c8, c16)))


def _window_sums(ext, lane, first):
    n = ext.shape[0]
    r = lambda a, s: pltpu.roll(a, s % n, axis=0)
    c2 = ext + r(ext, first)
    c4 = r(c2, 1) + r(c2, -1)
    c8 = r(c4, 2) + r(c4, -2)
    c16 = r(c8, 4) + r(c8, -4)
    return _by_window(lane, c2, c4, c8, c16)


def _pool_delta(ext, lane, inv):
    return (_window_sums(ext, lane, 1) * inv - ext)[HALO:HALO + SB, :]


def pool_fwd(ppool, wbd, scale, blocks_per_sample, seq):
    R = ppool.shape[0]
    prev, nxt = _halo_specs(POOL_DIM, R)

    def body(cur_ref, prev_ref, nxt_ref, w_ref, s_ref, o_ref):
        i = pl.program_id(0)
        ext = _ext_rows(cur_ref[...], prev_ref[...], nxt_ref[...], i, blocks_per_sample)
        lane, inv = _pool_geometry(i, blocks_per_sample, seq)
        dlt = _pool_delta(ext, lane, inv)
        o_ref[...] = _dot(dlt.astype(MXU), w_ref[...]) * s_ref[...]

    return pl.pallas_call(
        body, name="pool_fwd", grid=(R // SB,),
        in_specs=[_rowspec(POOL_DIM, SB), prev, nxt, _fullspec((POOL_DIM, POOL_DIM)), _fullspec((1, POOL_DIM))],
        out_specs=_rowspec(POOL_DIM, SB),
        out_shape=jax.ShapeDtypeStruct((R, POOL_DIM), F32),
    )(ppool, ppool, ppool, wbd, scale)


def pool_bwd(ppool, dpool, wbd, scale, blocks_per_sample, seq):
    R = ppool.shape[0]
    prev, nxt = _halo_specs(POOL_DIM, R)

    def body(cur_ref, prev_ref, nxt_ref, dcur_ref, dprev_ref, dnxt_ref, w_ref, s_ref, du_ref, dw_ref, part_ref):
        i = pl.program_id(0)

        @pl.when(i == 0)
        def _():
            dw_ref[...] = jnp.zeros_like(dw_ref)

        ext = _ext_rows(cur_ref[...], prev_ref[...], nxt_ref[...], i, blocks_per_sample)
        lane, inv = _pool_geometry(i, blocks_per_sample, seq)
        dlt = _pool_delta(ext, lane, inv).astype(MXU)
        dy = dcur_ref[...]
        part_ref[0] = jnp.concatenate([_colsum(dy * _dot(dlt, w_ref[...])), jnp.zeros((7, POOL_DIM), F32)], axis=0)
        dyp = (dy * s_ref[...]).astype(MXU)
        dw_ref[...] += _dotg(dlt, dyp, TN)
        dext = _ext_rows(dy, dprev_ref[...], dnxt_ref[...], i, blocks_per_sample)
        dd = _dotg((dext * s_ref[...]).astype(MXU), w_ref[...], NT)
        du_ref[...] = (_window_sums(dd * inv, lane, -1) - dd)[HALO:HALO + SB, :].astype(du_ref.dtype)

    return pl.pallas_call(
        body, name="pool_bwd", grid=(R // SB,),
        in_specs=[_rowspec(POOL_DIM, SB), prev, nxt, _rowspec(POOL_DIM, SB), prev, nxt,
                  _fullspec((POOL_DIM, POOL_DIM)), _fullspec((1, POOL_DIM))],
        out_specs=[_rowspec(POOL_DIM, SB), _fullspec((POOL_DIM, POOL_DIM)),
                   pl.BlockSpec((1, 8, POOL_DIM), lambda i: (i, 0, 0))],
        out_shape=[jax.ShapeDtypeStruct((R, POOL_DIM), MXU), jax.ShapeDtypeStruct((POOL_DIM, POOL_DIM), F32),
                   jax.ShapeDtypeStruct((R // SB, 8, POOL_DIM), F32)],
    )(ppool, ppool, ppool, dpool, dpool, dpool, wbd, scale)


def adamw(w, g, m, v, name="adamw"):
    rows, cols = w.shape
    tr = rows
    for cand in (512, 256, 128, 64, 32, 16, 8):
        if rows % cand == 0:
            tr = cand
            break
    bc1 = 1.0 - ADAM_B1 ** ADAM_STEP
    bc2 = 1.0 - ADAM_B2 ** ADAM_STEP

    def body(w_ref, g_ref, m_ref, v_ref, d_ref, nm_ref, nv_ref):
        g_v = g_ref[...]
        nm = ADAM_B1 * m_ref[...] + (1.0 - ADAM_B1) * g_v
        nv = ADAM_B2 * v_ref[...] + (1.0 - ADAM_B2) * (g_v * g_v)
        nm_ref[...] = nm
        nv_ref[...] = nv
        d_ref[...] = -ADAM_LR * ((nm / bc1) / (jnp.sqrt(nv / bc2) + ADAM_EPS) + ADAM_WD * w_ref[...])

    spec = pl.BlockSpec((tr, cols), lambda i: (i, 0))
    return pl.pallas_call(
        body, name=name, grid=(rows // tr,),
        in_specs=[spec] * 4, out_specs=[spec] * 3,
        out_shape=[jax.ShapeDtypeStruct((rows, cols), F32)] * 3,
    )(w, g, m, v)


MODR = 32


def _silu(v):
    return v * _sigmoid(v)


def mod_fwd(cond, w, b):
    n = w.shape[1]

    def body(c_ref, w_ref, b_ref, o_ref):
        o_ref[...] = _dot(_silu(c_ref[...]).astype(MXU), w_ref[...].astype(MXU)) + b_ref[...]

    return pl.pallas_call(
        body, name="mod_fwd", out_shape=jax.ShapeDtypeStruct((MODR, n), F32),
        in_specs=[_fullspec((MODR, D)), _fullspec((D, n)), _fullspec((1, n))], out_specs=_fullspec((MODR, n)),
        grid=(1,), compiler_params=_cp(40),
    )(cond, w, b)


def mod_wgrad(cond, dm):
    n = dm.shape[1]

    def body(c_ref, d_ref, o_ref):
        o_ref[...] = _dotg(_silu(c_ref[...]).astype(MXU), d_ref[...].astype(MXU), TN)

    return pl.pallas_call(
        body, name="mod_wgrad", out_shape=jax.ShapeDtypeStruct((D, n), F32),
        in_specs=[_fullspec((MODR, D)), _fullspec((MODR, n))], out_specs=_fullspec((D, n)),
        grid=(1,), compiler_params=_cp(40),
    )(cond, dm)


def mod_dgrad(dm, w):
    n = w.shape[1]

    def body(d_ref, w_ref, o_ref):
        o_ref[...] = _dotg(d_ref[...].astype(MXU), w_ref[...].astype(MXU), NT)

    return pl.pallas_call(
        body, name="mod_dgrad", out_shape=jax.ShapeDtypeStruct((8, D), F32),
        in_specs=[_fullspec((8, n)), _fullspec((D, n))], out_specs=_fullspec((8, D)),
        grid=(1,), compiler_params=_cp(40),
    )(dm, w)


def sum_leading(a, name="sum_leading"):
    n, r, c = a.shape

    def body(a_ref, o_ref):
        acc = a_ref[0]
        for k in range(1, n):
            acc = acc + a_ref[k]
        o_ref[...] = acc

    return pl.pallas_call(
        body, name=name, out_shape=jax.ShapeDtypeStruct((r, c), F32),
        in_specs=[_fullspec((n, r, c))], out_specs=_fullspec((r, c)), grid=(1,),
    )(a)


MESH = pl.DeviceIdType.MESH
NDEV = 8
ANY = pl.BlockSpec(memory_space=pl.ANY)


def _place():
    return lax.axis_index("x"), lax.axis_index("y"), lax.axis_index("c")


def _other_chips(x, y):
    return [(1 - x, y), (x, 1 - y), (1 - x, 1 - y)]


def allgather_small(v, name):
    r, cols = v.shape

    def body(v_ref, o_ref, send_sems, recv_sems):
        x, y, c = _place()
        me = 4 * x + 2 * y + c
        o_ref[me] = v_ref[...]
        copies = []
        for rel in range(1, NDEV):
            peer = (1 - x if rel & 4 else x, 1 - y if rel & 2 else y, 1 - c if rel & 1 else c)
            cp = pltpu.make_async_remote_copy(src_ref=v_ref, dst_ref=o_ref.at[me], send_sem=send_sems.at[rel - 1],
                                              recv_sem=recv_sems.at[rel - 1], device_id=peer, device_id_type=MESH)
            cp.start()
            copies.append(cp)
        for cp in copies:
            cp.wait_recv()
        for cp in copies:
            cp.wait_send()

    return pl.pallas_call(
        body, name=name, out_shape=jax.ShapeDtypeStruct((NDEV, r, cols), F32),
        in_specs=[pl.BlockSpec(memory_space=pltpu.VMEM)], out_specs=pl.BlockSpec(memory_space=pltpu.VMEM),
        scratch_shapes=[pltpu.SemaphoreType.DMA((NDEV - 1,)), pltpu.SemaphoreType.DMA((NDEV - 1,))],
        compiler_params=_cp(40),
    )(v)


def _sems(n):
    return [pltpu.SemaphoreType.DMA((n,)), pltpu.SemaphoreType.DMA((n,))]


def allgather_chips(v, name):
    r, cols = v.shape

    def body(v_ref, o_ref, send_sems, recv_sems):
        x, y, c = _place()
        k = 2 * x + y
        o_ref[k] = v_ref[...]
        copies = []
        for j, (px, py) in enumerate(_other_chips(x, y)):
            cp = pltpu.make_async_remote_copy(src_ref=v_ref, dst_ref=o_ref.at[k], send_sem=send_sems.at[j],
                                              recv_sem=recv_sems.at[j], device_id=(px, py, c), device_id_type=MESH)
            cp.start()
            copies.append(cp)
        for cp in copies:
            cp.wait_recv()
        for cp in copies:
            cp.wait_send()

    return pl.pallas_call(
        body, name=name, out_shape=jax.ShapeDtypeStruct((4, r, cols), F32),
        in_specs=[pl.BlockSpec(memory_space=pltpu.VMEM)], out_specs=pl.BlockSpec(memory_space=pltpu.VMEM),
        scratch_shapes=_sems(3), compiler_params=_cp(40),
    )(v)


def gather_job(arrs):
    n = len(arrs)

    def copy(srcs, outs, sems, i, slot, kk, cc, to, from_src=False):
        hr = arrs[i].shape[0] // 2
        dst = outs[i].at[kk, pl.ds(cc * hr, hr), :]
        return pltpu.make_async_remote_copy(src_ref=srcs[i].at[pl.ds(cc * hr, hr), :] if from_src else dst, dst_ref=dst,
                                            send_sem=sems[0].at[slot * n + i], recv_sem=sems[1].at[slot * n + i],
                                            device_id=to, device_id_type=MESH)

    def start(srcs, outs, sems):
        x, y, c = _place()
        for j, (px, py) in enumerate(_other_chips(x, y)):
            for i in range(n):
                copy(srcs, outs, sems, i, j, 2 * x + y, c, (px, py, c), True).start()

    def finish(srcs, outs, sems):
        x, y, c = _place()
        sib = (x, y, 1 - c)
        chips = _other_chips(x, y)
        passed = []
        for j, (px, py) in enumerate(chips):
            for i in range(n):
                copy(srcs, outs, sems, i, j, 2 * px + py, c, (px, py, c)).wait_recv()
                cp = copy(srcs, outs, sems, i, 3 + j, 2 * px + py, c, sib)
                cp.start()
                passed.append(cp)
        for j, (px, py) in enumerate(chips):
            for i in range(n):
                copy(srcs, outs, sems, i, 3 + j, 2 * px + py, 1 - c, sib).wait_recv()
        for j, (px, py) in enumerate(chips):
            for i in range(n):
                copy(srcs, outs, sems, i, j, 2 * x + y, c, (px, py, c), True).wait_send()
        for cp in passed:
            cp.wait_send()

    return _NS(ins=list(arrs), out_shapes=[jax.ShapeDtypeStruct((4,) + a.shape, a.dtype) for a in arrs], nsem=6 * n,
               start=start, finish=finish)


def chip_swap_job(ss):
    n = len(ss)

    def copies(srcs, outs, sems):
        x, y, c = _place()
        return [pltpu.make_async_remote_copy(src_ref=srcs[i].at[2 * px + py], dst_ref=outs[i].at[j],
                                             send_sem=sems[0].at[j * n + i], recv_sem=sems[1].at[j * n + i],
                                             device_id=(px, py, c), device_id_type=MESH)
                for j, (px, py) in enumerate(_other_chips(x, y)) for i in range(n)]

    def start(srcs, outs, sems):
        for cp in copies(srcs, outs, sems):
            cp.start()

    def finish(srcs, outs, sems):
        for cp in copies(srcs, outs, sems):
            cp.wait()

    return _NS(ins=list(ss), out_shapes=[jax.ShapeDtypeStruct((3,) + s.shape[1:], s.dtype) for s in ss], nsem=3 * n,
               start=start, finish=finish)


def run_job(job, name):
    n, m = len(job.ins), len(job.out_shapes)

    def body(*refs):
        srcs, outs, sems = refs[:n], refs[n:n + m], refs[n + m:]
        job.start(srcs, outs, sems)
        job.finish(srcs, outs, sems)

    return pl.pallas_call(body, name=name, out_shape=job.out_shapes, in_specs=[ANY] * n, out_specs=[ANY] * m,
                          scratch_shapes=_sems(job.nsem))(*job.ins)


def core_swap_job(gs):
    n = len(gs)

    def copies(srcs, outs, sems):
        x, y, c = _place()
        return [pltpu.make_async_remote_copy(src_ref=srcs[i].at[:, pl.ds((1 - c) * (gs[i].shape[1] // 2), gs[i].shape[1] // 2), :],
                                             dst_ref=outs[i], send_sem=sems[0].at[i], recv_sem=sems[1].at[i],
                                             device_id=(x, y, 1 - c), device_id_type=MESH) for i in range(n)]

    def start(srcs, outs, sems):
        for cp in copies(srcs, outs, sems):
            cp.start()

    def finish(srcs, outs, sems):
        for cp in copies(srcs, outs, sems):
            cp.wait()

    return _NS(ins=list(gs), out_shapes=[jax.ShapeDtypeStruct((4, g.shape[1] // 2, g.shape[2]), g.dtype) for g in gs],
               nsem=n, start=start, finish=finish)


def add_half(g, r1, cidx, name):
    _, rows, cols = g.shape
    hr = rows // 2

    def body(c_ref, g_ref, r_ref, o_ref, ob_ref):
        s = g_ref[...] + r_ref[...]
        o_ref[...] = s
        ob_ref[...] = s.astype(BF16)

    blk = lambda f: pl.BlockSpec((1, hr, cols), f)
    return pl.pallas_call(
        body, name=name,
        out_shape=[jax.ShapeDtypeStruct((4, hr, cols), F32), jax.ShapeDtypeStruct((4, hr, cols), BF16)],
        grid_spec=pltpu.PrefetchScalarGridSpec(
            num_scalar_prefetch=1, grid=(4,),
            in_specs=[blk(lambda k, c_ref: (k, c_ref[0], 0)), blk(lambda k, c_ref: (k, 0, 0))],
            out_specs=[blk(lambda k, c_ref: (k, 0, 0)), blk(lambda k, c_ref: (k, 0, 0))]),
    )(cidx, g, r1)


def sum_parts(s1, r2, kidx, name):
    _, hr, cols = s1.shape

    def body(k_ref, s_ref, r_ref, o_ref):
        o_ref[...] = ((s_ref[0] + r_ref[0].astype(F32)) + r_ref[1].astype(F32)) + r_ref[2].astype(F32)

    return pl.pallas_call(
        body, name=name, out_shape=jax.ShapeDtypeStruct((hr, cols), F32),
        grid_spec=pltpu.PrefetchScalarGridSpec(
            num_scalar_prefetch=1, grid=(1,),
            in_specs=[pl.BlockSpec((1, hr, cols), lambda i, k_ref: (k_ref[0], 0, 0)),
                      pl.BlockSpec((3, hr, cols), lambda i, k_ref: (0, 0, 0))],
            out_specs=pl.BlockSpec((hr, cols), lambda i, k_ref: (0, 0))),
    )(kidx, s1, r2)


def swap_reduced_halves(hs):
    n = len(hs)

    def body(*refs):
        srcs, outs = refs[:n], refs[n:2 * n]
        send_sems, recv_sems = refs[2 * n:]
        x, y, c = _place()
        copies = []
        for i in range(n):
            cp = pltpu.make_async_remote_copy(src_ref=srcs[i], dst_ref=outs[i], send_sem=send_sems.at[i],
                                              recv_sem=recv_sems.at[i], device_id=(x, y, 1 - c), device_id_type=MESH)
            cp.start()
            copies.append(cp)
        for cp in copies:
            cp.wait()

    return pl.pallas_call(
        body, name="swap_reduced_halves", out_shape=[jax.ShapeDtypeStruct(h.shape, h.dtype) for h in hs],
        in_specs=[ANY] * n, out_specs=[ANY] * n, scratch_shapes=_sems(n),
    )(*hs)


def adamw_halves(w, m, v, own, oth, cidx, name):
    depth, rows, cols = w.shape
    hr = rows // 2
    tr = min(hr, 256)
    nblk = hr // tr
    bc1 = 1.0 - ADAM_B1 ** ADAM_STEP
    bc2 = 1.0 - ADAM_B2 ** ADAM_STEP

    def body(c_ref, w_ref, m_ref, v_ref, own0, own1, oth0, oth1, g_ref, d_ref, nm_ref, nv_ref):
        l = pl.program_id(0)
        hi = pl.program_id(1)
        mine = jnp.where(l == 0, own0[...], own1[...])
        other = jnp.where(l == 0, oth0[...], oth1[...])
        g_v = jnp.where(hi == c_ref[0], mine, other)
        nm = ADAM_B1 * m_ref[0] + (1.0 - ADAM_B1) * g_v
        nv = ADAM_B2 * v_ref[0] + (1.0 - ADAM_B2) * (g_v * g_v)
        g_ref[0] = g_v
        nm_ref[0] = nm
        nv_ref[0] = nv
        d_ref[0] = -ADAM_LR * ((nm / bc1) / (jnp.sqrt(nv / bc2) + ADAM_EPS) + ADAM_WD * w_ref[0])

    wspec = pl.BlockSpec((1, tr, cols), lambda l, hi, b, c_ref: (l, hi * nblk + b, 0))
    def gspec(layer, mine):
        def index(l, hi, b, c_ref):
            used = jnp.logical_and(l == layer, (hi == c_ref[0]) == mine)
            return (jnp.where(used, b, 0), 0)
        return pl.BlockSpec((tr, cols), index)

    assert depth == 2
    return pl.pallas_call(
        body, name=name, out_shape=[jax.ShapeDtypeStruct(w.shape, F32)] * 4,
        grid_spec=pltpu.PrefetchScalarGridSpec(
            num_scalar_prefetch=1, grid=(depth, 2, nblk),
            in_specs=[wspec] * 3 + [gspec(0, True), gspec(1, True), gspec(0, False), gspec(1, False)],
            out_specs=[wspec] * 4),
    )(cidx, w, m, v, own[0], own[1], oth[0], oth[1])


class _NS:
    def __init__(self, **kw):
        self.__dict__.update(kw)


def _prep_in(win, conv_w, conv_b, dt_bias, a_log, ssd_d, ssd_nw, qnw, kvnw, pool_w, pool_scale, n1, n2):
    winp = jnp.concatenate([win[:, 0:384], win[:, 384:1280], win[:, 1292:1548], win[:, 1548:1804], win[:, 1836:2092],
                            win[:, 1804:1836], win[:, 1280:1292], jnp.zeros((D, NP - IN_COLS), win.dtype)], axis=1)
    wbd = (jnp.eye(4, dtype=F32)[:, None, :, None] * pool_w[:, :, None, :]).reshape(POOL_DIM, POOL_DIM).astype(MXU)
    a = -jnp.exp(a_log)
    return _NS(
        winp=winp, wbd=wbd,
        cw8=jnp.pad(conv_w, ((0, 4), (0, 0))), cb=conv_b[None],
        dtb=jnp.pad(dt_bias.reshape(1, 12), ((0, 0), (DT0, 128 - DT0 - 12))),
        arow=jnp.pad(a[:, None, :], ((0, 0), (0, 7), (0, 128 - SSD_HEADS))), a=a,
        dexp=jnp.repeat(ssd_d, SSD_P)[None], ssd_nw=ssd_nw[None], qnw=qnw[None], kvnw=kvnw[None],
        pscale=pool_scale[None], n1=n1[None], n2=n2[None])


def _prep_rest(wqb, wkvb, wout, w1, w2):
    wq = jnp.pad(wqb.reshape(256, MLA_HEADS, QK_DIM), ((0, 0), (0, 0), (0, HP - QK_DIM))).reshape(256, QW)
    kv3 = wkvb.reshape(256, MLA_HEADS, 128)
    wk = jnp.pad(kv3[:, :, :64], ((0, 0), (0, 0), (0, 64))).reshape(256, QW)
    wv = jnp.pad(kv3[:, :, 64:], ((0, 0), (0, 0), (0, 64))).reshape(256, QW)
    wo = jnp.concatenate([jnp.pad(wout[384:768].reshape(MLA_HEADS, 64, D), ((0, 0), (0, 64), (0, 0))).reshape(QW, D),
                          wout[0:384], wout[768:1024]], axis=0)
    return _NS(wq=wq, wk=wk, wv=wv, wo=wo, w1=w1, w2=w2)


def _prep_layer(win, wqb, wkvb, wout, w1, w2, *small):
    lw = _prep_in(win, *small)
    lw.__dict__.update(_prep_rest(wqb, wkvb, wout, w1, w2).__dict__)
    return lw


def _by_chip_cols(a):
    return jnp.stack([a[:, k * (a.shape[1] // 4):(k + 1) * (a.shape[1] // 4)] for k in range(4)])


def _by_chip_rows(a):
    return a.reshape(4, a.shape[0] // 4, a.shape[1])


def _unprep_in(dwinp):
    return jnp.concatenate([dwinp[:, 0:384], dwinp[:, 384:1280], dwinp[:, 2080:2092], dwinp[:, 1280:1536],
                            dwinp[:, 1536:1792], dwinp[:, 2048:2080], dwinp[:, 1792:2048]], axis=1)


def _unprep_rest(dwq, dwk, dwv, dwo):
    dwqb = dwq.reshape(256, MLA_HEADS, HP)[:, :, :QK_DIM].reshape(256, MLA_HEADS * QK_DIM)
    dwkvb = jnp.concatenate([dwk.reshape(256, MLA_HEADS, HP)[:, :, :64], dwv.reshape(256, MLA_HEADS, HP)[:, :, :64]],
                            axis=2).reshape(256, MLA_HEADS * 128)
    dwout = jnp.concatenate([dwo[QW:QW + 384], dwo[0:QW].reshape(MLA_HEADS, HP, D)[:, :64].reshape(384, D),
                             dwo[QW + 384:CAT]], axis=0)
    return dwqb, dwkvb, dwout


def _rope_tables(nb, N):
    t = jnp.arange(N, dtype=F32)
    row = jnp.floor(t / GRID_W)
    col = t - row * GRID_W
    inv = jnp.asarray(10000.0 ** (-np.arange(8, dtype=np.float32) / 8), F32)
    ang = jnp.stack([row[:, None] * inv, col[:, None] * inv], axis=1)
    cs, sn = jnp.cos(ang), jnp.sin(ang)
    zero = jnp.zeros_like(sn)
    lanes = lambda first, second: jnp.stack([first, second], axis=2).reshape(N, 32)
    pad = lambda a, fill: jnp.concatenate([jnp.full((N, 64), fill, F32), a, jnp.full((N, 32), fill, F32)], axis=1)
    tabs = []
    for tab, fill in ((pad(lanes(cs, cs), 1.0), 1.0), (pad(lanes(-sn, zero), 0.0), 0.0), (pad(lanes(zero, sn), 0.0), 0.0)):
        one = jnp.concatenate([jnp.full((CTX, 128), fill, F32), tab], axis=0)
        tabs.append(jnp.tile(one, (nb, 1)))
    return tabs


def _eexp():
    e = np.zeros((128, SSD_INNER), np.float32)
    for h in range(SSD_HEADS):
        e[h, h * SSD_P:(h + 1) * SSD_P] = 1.0
    return jnp.asarray(e)


class _NoHooks:
    def __init__(self, lws):
        self.lws = lws

    def weights_in(self, l):
        return _NS(**self.lws[l].__dict__)

    def weights_rest(self, l, scan_out):
        return self.lws[l]

    def job(self, where, l, early=None):
        return None

    def done(self, where, l, out):
        pass

    def layer_grads(self, l, g):
        pass


def _layer_fwd(X, bm, l, cst, hooks):
    nb, T, bps, N = cst.nb, cst.T, cst.bps, cst.N
    lw = hooks.weights_in(l)
    h1, pz, pxbc, pqa, pkva, ppool, plast = in_proj(X, bm, lw.n1, lw.winp)
    xs, bmat, cmat, dtv = ssd_prep(pxbc, plast, lw.cw8, lw.cb, lw.dtb, bps)
    y2, hin, out = ssd_scan_fwd(xs, bmat, cmat, dtv, lw.arow, cst.eexp, nb, T, hooks.job("fwd_scan", l))
    lw.__dict__.update(hooks.weights_rest(l, out).__dict__)
    ssd = ssd_out_fwd(y2, xs, pz, lw.dexp, lw.ssd_nw)
    q, k, v, cq, ckv = mla_prep(pqa, pkva, plast, lw.qnw, lw.kvnw, lw.wq, lw.wk, lw.wv, *cst.rope)
    attn, lse = attn_fwd(q, k, v, nb, T)
    pool = pool_fwd(ppool, lw.wbd, lw.pscale, bps, N)
    x1, mix, cat = mix_fwd(X, attn, ssd, pool, bm, lw.wo)
    x2, mo, r, h2, out = mlp_fwd(x1, bm, lw.n2, lw.w1, lw.w2, hooks.job("fwd_mlp", l))
    hooks.done("fwd_mlp", l, out)
    sv = _NS(X=X, h1=h1, pz=pz, pxbc=pxbc, pqa=pqa, pkva=pkva, ppool=ppool, plast=plast, xs=xs, bmat=bmat, cmat=cmat,
             dtv=dtv, y2=y2, hin=hin, q=q, k=k, v=v, cq=cq, ckv=ckv, attn=attn, lse=lse, x1=x1, mix=mix, cat=cat, mo=mo, r=r,
             h2=h2, lw=lw)
    return x2, sv


def _layer_bwd(dx2, bm, l, sv, cst, hooks):
    nb, T, bps, N = cst.nb, cst.T, cst.bps, cst.N
    lw = sv.lw
    dx1, du, dob, part_mlp, out = mlp_bwd(dx2, sv.x1, sv.mo, sv.r, bm, lw.n2, lw.w2, lw.w1, hooks.job("bwd_mlp", l))
    hooks.done("bwd_mlp", l, out)
    dw1 = mm_tn(sv.h2, du, name="wgrad_mlp1", col_blocks=True)
    dw2 = mm_tn(sv.r, dob, square_a=True, name="wgrad_mlp2")
    dattn, dssd, dpool, dwo, part_mix = mix_bwd(dx1, sv.mix, sv.cat, bm, lw.wo)
    dppool, dwbd, part_pool = pool_bwd(sv.ppool, dpool, lw.wbd, lw.pscale, bps, N)
    dq, dk, dv = attn_bwd(sv.q, sv.k, sv.v, sv.attn, sv.lse, dattn, nb, T)
    dpqa, dpkva, dkr, dwq, dwk, dwv, part_mla = mla_prep_bwd(dq, dk, dv, sv.pqa, sv.pkva, sv.cq, sv.ckv, lw.qnw, lw.kvnw,
                                                             lw.wq, lw.wk, lw.wv, *cst.rope)
    dwqb, dwkvb, dwout = _unprep_rest(dwq, dwk, dwv, dwo)
    early = dict(w_q_b=_by_chip_cols(dwqb), w_kv_b=_by_chip_cols(dwkvb), w_out=_by_chip_rows(dwout), w_mlp1=dw1,
                 w_mlp2=_by_chip_rows(dw2))
    dyy, dz, dxs_skip, part_so = ssd_out_bwd(dssd, sv.y2, sv.xs, sv.pz, lw.dexp, lw.ssd_nw)
    dxs2, dbm2, dcm2, ddt2, da, out = ssd_scan_bwd(sv.xs, sv.bmat, sv.cmat, sv.dtv, lw.arow, cst.eexp, sv.hin, dyy,
                                                   nb, T, hooks.job("bwd_scan", l, early))
    hooks.done("bwd_scan", l, out)
    dpre, dlast_dt, part_conv = ssd_prep_bwd_a(sv.pxbc, sv.plast, lw.cw8, lw.cb, lw.dtb, dxs_skip, dxs2, dbm2, dcm2,
                                               ddt2, bps)
    dpxbc = ssd_prep_bwd_b(dpre, lw.cw8, bps)
    dx, dwinp, part_in = in_proj_bwd(dx1, sv.X, sv.h1, dz, dpxbc, dpqa, dpkva, dppool, dkr, dlast_dt, bm, lw.n1, lw.winp)

    dmod = jnp.stack([part_in[:, 0], part_in[:, 1], part_mix[:, 0], part_mlp[:, 0], part_mlp[:, 1], part_mlp[:, 2]],
                     axis=1)
    dmod = dmod.reshape(nb, bps, 6, D)
    dm_rows = jnp.concatenate([jnp.sum(dmod[:, 1:], axis=1), jnp.sum(dmod[:, 0], axis=0)[None]], axis=0)
    da_dh = jnp.sum(da[:, :, 0, :SSD_HEADS], axis=1)
    conv_parts = jnp.sum(part_conv, axis=0)
    g = _NS(
        w_in=_by_chip_cols(_unprep_in(dwinp)), dm_rows=dm_rows.reshape(3, 6 * D), **early,
        norm1_w=jnp.sum(part_in[:, 2], axis=0), norm2_w=jnp.sum(part_mlp[:, 3], axis=0),
        conv_w=conv_parts[0:4], conv_b=conv_parts[4],
        dt_bias=conv_parts[5, DT0:DT0 + 12].reshape(2, SSD_HEADS), a_log=da_dh * lw.a,
        ssd_d=jnp.sum(jnp.sum(part_so[:, 1], axis=0).reshape(SSD_HEADS, SSD_P), axis=1),
        ssd_norm_w=jnp.sum(part_so[:, 0], axis=0),
        q_a_norm_w=jnp.sum(part_mla[:, 0], axis=0), kv_a_norm_w=jnp.sum(part_mla[:, 1], axis=0),
        pool_w=jnp.stack([dwbd[i * 64:(i + 1) * 64, i * 64:(i + 1) * 64] for i in range(4)]),
        pool_scale=jnp.sum(part_pool[:, 0], axis=0))
    hooks.layer_grads(l, g)
    return dx, g


def _local_step(x, ctx, tgt, bms, lws, fw, cst, hooks=None):
    nb, N = x.shape[0], x.shape[1]
    R = nb * cst.T
    hooks = _NoHooks(lws) if hooks is None else hooks
    X = jnp.concatenate([ctx, x], axis=1).reshape(R, D)
    saved = []
    for l in range(DEPTH):
        X, sv = _layer_fwd(X, bms[l], l, cst, hooks)
        saved.append(sv)
    dX, part_fin = final_loss(X, tgt.reshape(nb * N, D), fw[None], cst.bps)
    loss = (0.5 / D) * jnp.sum(part_fin[:, 1])
    dfw = jnp.sum(part_fin[:, 0], axis=0)
    grads = [None] * DEPTH
    for l in reversed(range(DEPTH)):
        dX, grads[l] = _layer_bwd(dX, bms[l], l, saved[l], cst, hooks)
    grad_x = dX.reshape(nb, cst.T, D)[:, CTX:, :]
    return loss, grad_x, grads, dfw


def _consts(nb, N):
    T = CTX + N
    bps = T // SB
    return _NS(nb=nb, N=N, T=T, bps=bps, eexp=_eexp(), rope=_rope_tables(nb, N))


def _block_mod(modrows, cst):
    rows = []
    for b in range(cst.nb):
        rows.append(modrows[cst.nb:cst.nb + 1])
        rows.append(jnp.broadcast_to(modrows[b:b + 1], (cst.bps - 1, 6, D)))
    return jnp.pad(jnp.concatenate(rows, axis=0), ((0, 0), (0, 2), (0, 0)))


SMALL = (("norm1_w", (2, D)), ("norm2_w", (2, D)), ("conv_w", (2, 4, XBC)), ("conv_b", (2, XBC)),
         ("dt_bias", (2, 2, 6)), ("a_log", (2, 2, 6)), ("ssd_d", (2, 6)), ("ssd_norm_w", (2, 384)),
         ("q_a_norm_w", (2, 256)), ("kv_a_norm_w", (2, 256)), ("pool_w", (2, 4, 64, 64)), ("pool_scale", (2, 256)),
         ("final_norm_w", (D,)), ("mod_b", (2, 6 * D)))
SMALL_ROWS = 64
DM_ROWS = 48


def _pack_small(vals):
    flat = jnp.concatenate([vals[n].reshape(-1) for n, _ in SMALL])
    return jnp.pad(flat, (0, SMALL_ROWS * D - flat.shape[0])).reshape(SMALL_ROWS, D)


def _unpack_small(p):
    flat = p.reshape(-1)
    out, off = {}, 0
    for n, shp in SMALL:
        size = int(np.prod(shp))
        out[n] = flat[off:off + size].reshape(shp)
        off += size
    return out


def cctx_grad(parts, c_ctx):
    def body(p_ref, c_ref, o_ref):
        acc = ((p_ref[0] + p_ref[1]) + p_ref[2]) + p_ref[3]
        v = c_ref[...]
        sig = _sigmoid(v)
        o_ref[...] = acc * (sig * (1.0 + v * (1.0 - sig)))

    return pl.pallas_call(
        body, name="cctx_grad", out_shape=jax.ShapeDtypeStruct((8, D), F32),
        in_specs=[_fullspec((4, 8, D)), _fullspec((1, D))], out_specs=_fullspec((8, D)), grid=(1,),
    )(parts, c_ctx)


def kernel(x, c, ctx, c_ctx, mod_w, mod_b, norm1_w, norm2_w, w_in, conv_w, conv_b, dt_bias, a_log, ssd_d, ssd_norm_w, q_a_norm_w, w_q_b, kv_a_norm_w, w_kv_b, pool_w, pool_scale, w_out, w_mlp1, w_mlp2, final_norm_w, loss_target, m_c_ctx, m_mod_w, m_mod_b, m_norm1_w, m_norm2_w, m_w_in, m_conv_w, m_conv_b, m_dt_bias, m_a_log, m_ssd_d, m_ssd_norm_w, m_q_a_norm_w, m_w_q_b, m_kv_a_norm_w, m_w_kv_b, m_pool_w, m_pool_scale, m_w_out, m_w_mlp1, m_w_mlp2, m_final_norm_w, v_c_ctx, v_mod_w, v_mod_b, v_norm1_w, v_norm2_w, v_w_in, v_conv_w, v_conv_b, v_dt_bias, v_a_log, v_ssd_d, v_ssd_norm_w, v_q_a_norm_w, v_w_q_b, v_kv_a_norm_w, v_w_kv_b, v_pool_w, v_pool_scale, v_w_out, v_w_mlp1, v_w_mlp2, v_final_norm_w):
    nb, N = x.shape[0], x.shape[1]
    cst = _consts(nb, N)
    xi, yi, ci = _place()
    me = 4 * xi + 2 * yi + ci
    kchip = 2 * xi + yi
    mcols = mod_w.shape[2]
    cshard = conv_w.shape[2]

    blk = jnp.zeros((16, D), F32).at[0:nb].set(c).at[8:16, 0:cshard].set(conv_w.reshape(8, cshard))
    g1 = allgather_small(blk, "gather_cond")
    cond = jnp.concatenate([g1[:, 0:nb].reshape(NDEV * nb, D), c_ctx[None],
                            jnp.zeros((MODR - NDEV * nb - 1, D), F32)], axis=0)
    conv_full = [jnp.concatenate([g1[2 * k, 8 + 4 * l:12 + 4 * l, 0:cshard] for k in range(4)], axis=1)
                 for l in range(DEPTH)]

    mb = [lax.dynamic_slice_in_dim(mod_b[l], kchip * mcols, mcols)[None] for l in range(DEPTH)]
    ms = jnp.concatenate([mod_fwd(cond, mod_w[l], mb[l]) for l in range(DEPTH)], axis=0)
    g2 = allgather_chips(ms, "gather_mod")
    bms = []
    for l in range(DEPTH):
        m_all = jnp.concatenate([g2[k, MODR * l:MODR * (l + 1)] for k in range(4)], axis=1)
        mine = jnp.concatenate([lax.dynamic_slice_in_dim(m_all, nb * me, nb), m_all[NDEV * nb:NDEV * nb + 1]], axis=0)
        bms.append(_block_mod(mine.reshape(nb + 1, 6, D), cst))

    assert DEPTH == 2
    big = (w_in, w_q_b, w_kv_b, w_out, w_mlp1, w_mlp2)
    names = ("w_in", "w_q_b", "w_kv_b", "w_out", "w_mlp1", "w_mlp2")
    concat_axis = dict(w_in=1, w_q_b=1, w_kv_b=1, w_out=0, w_mlp1=1, w_mlp2=0)
    cidx = jnp.reshape(ci, (1,)).astype(jnp.int32)
    kidx = jnp.reshape(kchip, (1,)).astype(jnp.int32)
    shards = [{n: a[l].astype(MXU) for n, a in zip(names, big)} for l in range(DEPTH)]

    def core_sums(gs, got=None):
        ns = list(gs)
        got = run_job(core_swap_job([gs[n] for n in ns]), "swap_core_halves") if got is None else got
        return {n: add_half(gs[n], r, cidx, "add_half_" + n) for n, r in zip(ns, got)}

    class Hooks:
        gathered = [dict(w_in=run_job(gather_job([shards[0]["w_in"]]), "gather_w_in")[0]), {}]
        core_sum = [{}, {}]
        received = [{}, {}]

        def whole(self, l, n):
            return jnp.concatenate([jnp.where(kchip == k, shards[l][n], self.gathered[l][n][k]) for k in range(4)],
                                   axis=concat_axis[n])

        def weights_in(self, l):
            return _prep_in(self.whole(l, "w_in"), conv_full[l], conv_b[l], dt_bias[l], a_log[l], ssd_d[l], ssd_norm_w[l],
                            q_a_norm_w[l], kv_a_norm_w[l], pool_w[l], pool_scale[l], norm1_w[l], norm2_w[l])

        def weights_rest(self, l, scan_out):
            if l == 0:
                self.gathered[0].update(zip(names[1:], scan_out))
            return _prep_rest(*[self.whole(l, n) for n in names[1:]])

        def job(self, where, l, early=None):
            if l == 1 and where == "bwd_scan":
                self.early1 = early
                return core_swap_job([early[n] for n in names[1:]])
            if l != 0:
                return None
            if where == "fwd_scan":
                return gather_job([shards[0][n] for n in names[1:]])
            if where == "fwd_mlp":
                return gather_job([shards[1][n] for n in names])
            if where == "bwd_mlp":
                return chip_swap_job([self.core_sum[1][n][1] for n in names])
            self.core_sum[0].update(core_sums(early))
            return chip_swap_job([self.core_sum[0][n][1] for n in names[1:]])

        def done(self, where, l, out):
            if l == 1 and where == "bwd_scan":
                self.core_sum[1].update(core_sums(self.early1, out))
            if l != 0:
                return
            if where == "fwd_mlp":
                self.gathered[1].update(zip(names, out))
            elif where == "bwd_mlp":
                self.received[1].update(zip(names, out))
            elif where == "bwd_scan":
                self.received[0].update(zip(names[1:], out))

        def layer_grads(self, l, g):
            if l == 1:
                self.core_sum[1].update(core_sums(dict(w_in=g.w_in)))
            else:
                self.core_sum[0].update(core_sums(dict(w_in=g.w_in)))
                self.received[0]["w_in"] = run_job(chip_swap_job([self.core_sum[0]["w_in"][1]]), "swap_w_in")[0]

    hooks = Hooks()
    loss_part, grad_x, grads, dfw = _local_step(x, ctx, loss_target, bms, None, final_norm_w, cst, hooks)
    loss = lax.psum(loss_part, ("x", "y", "c"))
    g_own = [sum_parts(hooks.core_sum[l][n][0], hooks.received[l][n], kidx, "sum_parts_" + n)
             for n in names for l in range(DEPTH)]
    g_oth = swap_reduced_halves(g_own)

    small = {n: jnp.stack([getattr(grads[l], n) for l in range(DEPTH)]) for n, _ in SMALL if n not in ("final_norm_w", "mod_b")}
    small["final_norm_w"] = dfw
    small["mod_b"] = jnp.stack([jnp.sum(grads[l].dm_rows, axis=0) for l in range(DEPTH)])
    dm = jnp.pad(jnp.concatenate([grads[l].dm_rows for l in range(DEPTH)], axis=0), ((0, 8 - 3 * DEPTH), (0, 0)))
    g3 = allgather_small(jnp.concatenate([_pack_small(small), dm.reshape(DM_ROWS, D)], axis=0), "gather_small")
    tot = sum_leading(g3, "sum_small")
    gsmall = _unpack_small(tot[0:SMALL_ROWS])
    ctx_sum = tot[SMALL_ROWS:].reshape(8, 6 * D)
    dm_dev = g3[:, SMALL_ROWS:].reshape(NDEV, 8, 6 * D)
    g_mod_w, dpart = [], jnp.zeros((8, D), F32)
    for l in range(DEPTH):
        dm_all = jnp.concatenate([dm_dev[:, 3 * l:3 * l + nb].reshape(NDEV * nb, 6 * D), ctx_sum[3 * l + nb:3 * l + nb + 1],
                                  jnp.zeros((MODR - NDEV * nb - 1, 6 * D), F32)], axis=0)
        g_mod_w.append(mod_wgrad(cond, lax.dynamic_slice_in_dim(dm_all, kchip * mcols, mcols, axis=1)))
        dctx = jnp.pad(lax.dynamic_slice_in_dim(ctx_sum[3 * l + nb:3 * l + nb + 1], kchip * mcols, mcols, axis=1), ((0, 7), (0, 0)))
        dpart = dpart + mod_dgrad(dctx, mod_w[l])
    g_c_ctx = cctx_grad(allgather_chips(dpart, "gather_cctx"), c_ctx[None])[0]

    res = {}
    moments = ((m_w_in, v_w_in), (m_w_q_b, v_w_q_b), (m_w_kv_b, v_w_kv_b), (m_w_out, v_w_out), (m_w_mlp1, v_w_mlp1),
               (m_w_mlp2, v_w_mlp2))
    for i, (n, w, (m, v)) in enumerate(zip(names, big, moments)):
        res[n] = tuple(adamw_halves(w, m, v, g_own[DEPTH * i:DEPTH * (i + 1)], g_oth[DEPTH * i:DEPTH * (i + 1)], cidx,
                                    "adamw_" + n))
    g_mw = jnp.stack(g_mod_w)
    r_mw = adamw(mod_w.reshape(-1, mcols), g_mw.reshape(-1, mcols), m_mod_w.reshape(-1, mcols),
                 v_mod_w.reshape(-1, mcols), name="adamw_mod_w")
    res["mod_w"] = (g_mw,) + tuple(a.reshape(mod_w.shape) for a in r_mw)

    given = dict(norm1_w=(norm1_w, m_norm1_w, v_norm1_w), norm2_w=(norm2_w, m_norm2_w, v_norm2_w),
                 conv_b=(conv_b, m_conv_b, v_conv_b), dt_bias=(dt_bias, m_dt_bias, v_dt_bias),
                 a_log=(a_log, m_a_log, v_a_log), ssd_d=(ssd_d, m_ssd_d, v_ssd_d),
                 ssd_norm_w=(ssd_norm_w, m_ssd_norm_w, v_ssd_norm_w), q_a_norm_w=(q_a_norm_w, m_q_a_norm_w, v_q_a_norm_w),
                 kv_a_norm_w=(kv_a_norm_w, m_kv_a_norm_w, v_kv_a_norm_w), pool_w=(pool_w, m_pool_w, v_pool_w),
                 pool_scale=(pool_scale, m_pool_scale, v_pool_scale),
                 final_norm_w=(final_norm_w, m_final_norm_w, v_final_norm_w), mod_b=(mod_b, m_mod_b, v_mod_b))
    zero_cw = jnp.zeros((2, 4, XBC), F32)
    packs = [_pack_small({n: (given[n][i] if n in given else zero_cw) for n, _ in SMALL}) for i in range(3)]
    r_small = [_unpack_small(a) for a in adamw(packs[0], tot[0:SMALL_ROWS], packs[1], packs[2], name="adamw_small")]
    for n in given:
        res[n] = (gsmall[n], r_small[0][n], r_small[1][n], r_small[2][n])

    g_cw = lax.dynamic_slice_in_dim(gsmall["conv_w"], kchip * cshard, cshard, axis=2)
    padcw = lambda a: jnp.pad(a.reshape(8, cshard), ((0, 0), (0, 256 - cshard)))
    r_cw = adamw(padcw(conv_w), padcw(g_cw), padcw(m_conv_w), padcw(v_conv_w), name="adamw_conv_w")
    res["conv_w"] = (g_cw,) + tuple(a[:, 0:cshard].reshape(conv_w.shape) for a in r_cw)
    r_cc = adamw(c_ctx.reshape(8, 128), g_c_ctx.reshape(8, 128), m_c_ctx.reshape(8, 128), v_c_ctx.reshape(8, 128),
                 name="adamw_c_ctx")
    res["c_ctx"] = (g_c_ctx,) + tuple(a.reshape(D) for a in r_cc)

    order = ("c_ctx", "mod_w", "mod_b", "norm1_w", "norm2_w", "w_in", "conv_w", "conv_b", "dt_bias", "a_log", "ssd_d",
             "ssd_norm_w", "q_a_norm_w", "w_q_b", "kv_a_norm_w", "w_kv_b", "pool_w", "pool_scale", "w_out", "w_mlp1",
             "w_mlp2", "final_norm_w")
    return (loss, grad_x) + tuple(res[n][i] for i in range(4) for n in order)
```

```python
import functools
import math

import numpy as np
import jax
import jax.numpy as jnp
from jax import lax
from jax.experimental import pallas as pl
from jax.experimental.pallas import tpu as pltpu

F32 = jnp.float32
BF16 = jnp.bfloat16
MXU = jnp.bfloat16

D = 1024
DEPTH = 2
GRID_W = 64
CTX = 256
EPS = 1e-6
SSD_HEADS = 6
SSD_P = 64
SSD_INNER = 384
SSD_N = 128
CHUNK = 128
XBC = 896
MLA_HEADS = 6
QK_NOPE = 64
QK_ROPE = 32
QK_DIM = 96
HP = 128
QW = MLA_HEADS * HP
POOL_DIM = 256
D_FF = 4096
FF_BLK = 1024
IN_COLS = 2092
NP = 2176
P_SPLITS = (384, 896, 256, 256, 256, 128)
DT0 = 32
CAT = QW + SSD_INNER + POOL_DIM

SB = 256
TM = 512
HALO = 8

ADAM_LR = 0.001
ADAM_B1 = 0.9
ADAM_B2 = 0.999
ADAM_EPS = 1e-08
ADAM_WD = 0.01
ADAM_STEP = 10

NT = (((1,), (1,)), ((), ()))
TN = (((0,), (0,)), ((), ()))


def _cp(vmem_mb=None):
    if vmem_mb is None:
        return pltpu.CompilerParams()
    return pltpu.CompilerParams(vmem_limit_bytes=vmem_mb << 20)


def _dot(a, b):
    return jnp.dot(a, b, preferred_element_type=F32)


def _dotg(a, b, dims):
    return lax.dot_general(a, b, dims, preferred_element_type=F32)


def _dot_hi(a, b, dims=None, sel_first=False):
    dims = (((1,), (0,)), ((), ())) if dims is None else dims
    v, s = (b, a) if sel_first else (a, b)
    hi = v.astype(BF16)
    lo = (v - hi.astype(F32)).astype(BF16)
    s = s.astype(BF16)
    if sel_first:
        return _dotg(s, hi, dims) + _dotg(s, lo, dims)
    return _dotg(hi, s, dims) + _dotg(lo, s, dims)


def _rms_hat(x):
    rstd = lax.rsqrt(jnp.mean(x * x, axis=-1, keepdims=True) + EPS)
    return x * rstd, rstd


def _rms_bwd(dn, xhat, rstd, w):
    dxhat = dn * w
    dx = rstd * (dxhat - xhat * jnp.mean(dxhat * xhat, axis=-1, keepdims=True))
    return dx, jnp.sum(dn * xhat, axis=0, keepdims=True)


def _sigmoid(z):
    return 1.0 / (1.0 + jnp.exp(-z))


def _colsum(a):
    return jnp.sum(a, axis=0, keepdims=True)


def _rowspec(cols, tm=TM):
    return pl.BlockSpec((tm, cols), lambda i: (i, 0))


def _fullspec(shape):
    n = len(shape)
    return pl.BlockSpec(shape, lambda *_: (0,) * n)


def _resident(shape):
    n = len(shape)
    return pl.BlockSpec(shape, lambda *_: (0,) * n, pipeline_mode=pl.Buffered(1))


def _halo_specs(cols, nrows, halo=HALO):
    per = SB // halo
    last = nrows // halo - 1
    prev = pl.BlockSpec((halo, cols), lambda i: (jnp.maximum(i * per - 1, 0), 0))
    nxt = pl.BlockSpec((halo, cols), lambda i: (jnp.minimum((i + 1) * per, last), 0))
    return prev, nxt


def _ext_rows(cur, prev, nxt, i, blocks_per_sample):
    j = i % blocks_per_sample
    first = jnp.logical_or(j == 0, j == 1)
    last = jnp.logical_or(j == 0, j == blocks_per_sample - 1)
    p = jnp.where(first, 0.0, prev.astype(F32))
    n = jnp.where(last, 0.0, nxt.astype(F32))
    return jnp.concatenate([p, cur.astype(F32), n], axis=0)


def _shift(ext, s):
    n = ext.shape[0]
    halo = (n - SB) // 2
    return pltpu.roll(ext, (-s) % n, axis=0)[halo:halo + SB, :]


def in_proj(x, bm, nw, w):
    R = x.shape[0]

    def body(x_ref, bm_ref, nw_ref, w_ref, h_ref, *outs):
        for s in range(TM // SB):
            rows = slice(s * SB, (s + 1) * SB)
            xhat, _ = _rms_hat(x_ref[rows, :])
            h = xhat * nw_ref[...] * (1.0 + bm_ref[s, 1:2, :]) + bm_ref[s, 0:1, :]
            h_ref[rows, :] = h.astype(h_ref.dtype)
        p = _dot(h_ref[...], w_ref[...])
        off = 0
        for o, n in zip(outs, P_SPLITS):
            o[...] = p[:, off:off + n].astype(o.dtype)
            off += n

    return pl.pallas_call(
        body, name="in_proj", grid=(R // TM,),
        in_specs=[_rowspec(D), pl.BlockSpec((TM // SB, 8, D), lambda i: (i, 0, 0)), _fullspec((1, D)),
                  _fullspec((D, NP))],
        out_specs=[_rowspec(D)] + [_rowspec(n) for n in P_SPLITS],
        out_shape=[jax.ShapeDtypeStruct((R, D), MXU)]
                  + [jax.ShapeDtypeStruct((R, n), dt) for n, dt in zip(P_SPLITS, (MXU, MXU, MXU, MXU, F32, F32))],
        compiler_params=_cp(56),
    )(x, bm, nw, w)


def in_proj_bwd(dx1, x, h, dz, dxbc, dqa, dkva, dpool, dkr, ddt, bm, nw, w):
    R = x.shape[0]

    def body(dx1_ref, x_ref, h_ref, dz_ref, dxbc_ref, dqa_ref, dkva_ref, dpool_ref, dkr_ref, ddt_ref, bm_ref, nw_ref,
             w_ref, dx_ref, dw_ref, part_ref, dp_ref):
        @pl.when(pl.program_id(0) == 0)
        def _():
            dw_ref[...] = jnp.zeros_like(dw_ref)

        dp_ref[:, 0:384] = dz_ref[...].astype(dp_ref.dtype)
        dp_ref[:, 384:1280] = dxbc_ref[...].astype(dp_ref.dtype)
        dp_ref[:, 1280:1536] = dqa_ref[...].astype(dp_ref.dtype)
        dp_ref[:, 1536:1792] = dkva_ref[...].astype(dp_ref.dtype)
        dp_ref[:, 1792:2048] = dpool_ref[...].astype(dp_ref.dtype)
        dp_ref[:, 2048:2176] = (dkr_ref[...] + ddt_ref[...]).astype(dp_ref.dtype)
        dw_ref[...] += _dotg(h_ref[...], dp_ref[...], TN)
        dh = _dotg(dp_ref[...], w_ref[...], NT)
        w = nw_ref[...]
        for s in range(TM // SB):
            rows = slice(s * SB, (s + 1) * SB)
            xhat, rstd = _rms_hat(x_ref[rows, :])
            dhs = dh[rows, :]
            sc1 = 1.0 + bm_ref[s, 1:2, :]
            dx, dnw = _rms_bwd(dhs * sc1, xhat, rstd, w)
            dx_ref[rows, :] = dx1_ref[rows, :] + dx
            part_ref[s] = jnp.concatenate(
                [_colsum(dhs), _colsum(dhs * xhat * w), dnw, jnp.zeros((5, D), F32)], axis=0)

    return pl.pallas_call(
        body, name="in_proj_bwd", grid=(R // TM,),
        in_specs=[_rowspec(D), _rowspec(D), _rowspec(D), _rowspec(384), _rowspec(896), _rowspec(256), _rowspec(256),
                  _rowspec(256), _rowspec(128), _rowspec(128),
                  pl.BlockSpec((TM // SB, 8, D), lambda i: (i, 0, 0)), _fullspec((1, D)), _resident((D, NP))],
        out_specs=[_rowspec(D), _fullspec((D, NP)), pl.BlockSpec((TM // SB, 8, D), lambda i: (i, 0, 0))],
        out_shape=[jax.ShapeDtypeStruct((R, D), F32), jax.ShapeDtypeStruct((D, NP), F32),
                   jax.ShapeDtypeStruct((R // SB, 8, D), F32)],
        scratch_shapes=[pltpu.VMEM((TM, NP), MXU)],
        compiler_params=_cp(56),
    )(dx1, x, h, dz, dxbc, dqa, dkva, dpool, dkr, ddt, bm, nw, w)


def mix_fwd(x, attn, ssd, pool, bm, wo):
    R = x.shape[0]

    def body(x_ref, a_ref, s_ref, p_ref, bm_ref, wo_ref, x1_ref, mix_ref, cat_ref):
        cat_ref[:, 0:QW] = a_ref[...].astype(cat_ref.dtype)
        cat_ref[:, QW:QW + SSD_INNER] = s_ref[...].astype(cat_ref.dtype)
        cat_ref[:, QW + SSD_INNER:CAT] = p_ref[...].astype(cat_ref.dtype)
        mix = _dot(cat_ref[...], wo_ref[...])
        mix_ref[...] = mix.astype(mix_ref.dtype)
        for s in range(TM // SB):
            rows = slice(s * SB, (s + 1) * SB)
            x1_ref[rows, :] = x_ref[rows, :] + bm_ref[s, 2:3, :] * mix[rows, :]

    return pl.pallas_call(
        body, name="mix_fwd", grid=(R // TM,),
        in_specs=[_rowspec(D), _rowspec(QW), _rowspec(SSD_INNER), _rowspec(POOL_DIM),
                  pl.BlockSpec((TM // SB, 8, D), lambda i: (i, 0, 0)), _fullspec((CAT, D))],
        out_specs=[_rowspec(D), _rowspec(D), _rowspec(CAT)],
        out_shape=[jax.ShapeDtypeStruct((R, D), F32), jax.ShapeDtypeStruct((R, D), MXU),
                   jax.ShapeDtypeStruct((R, CAT), MXU)],
        compiler_params=_cp(48),
    )(x, attn, ssd, pool, bm, wo)


def mix_bwd(dx1, mix, cat, bm, wo):
    R = dx1.shape[0]

    def body(dx1_ref, mix_ref, cat_ref, bm_ref, wo_ref, da_ref, ds_ref, dpl_ref, dw_ref, part_ref, dmb_ref):
        @pl.when(pl.program_id(0) == 0)
        def _():
            dw_ref[...] = jnp.zeros_like(dw_ref)

        for s in range(TM // SB):
            rows = slice(s * SB, (s + 1) * SB)
            d = dx1_ref[rows, :]
            dmb_ref[rows, :] = (d * bm_ref[s, 2:3, :]).astype(dmb_ref.dtype)
            part_ref[s] = jnp.concatenate([_colsum(d * mix_ref[rows, :].astype(F32)), jnp.zeros((7, D), F32)], axis=0)
        dw_ref[...] += _dotg(cat_ref[...], dmb_ref[...], TN)
        dcat = _dotg(dmb_ref[...], wo_ref[...], NT)
        da_ref[...] = dcat[:, 0:QW]
        ds_ref[...] = dcat[:, QW:QW + SSD_INNER]
        dpl_ref[...] = dcat[:, QW + SSD_INNER:CAT]

    return pl.pallas_call(
        body, name="mix_bwd", grid=(R // TM,),
        in_specs=[_rowspec(D), _rowspec(D), _rowspec(CAT), pl.BlockSpec((TM // SB, 8, D), lambda i: (i, 0, 0)),
                  _resident((CAT, D))],
        out_specs=[_rowspec(QW), _rowspec(SSD_INNER), _rowspec(POOL_DIM), _fullspec((CAT, D)),
                   pl.BlockSpec((TM // SB, 8, D), lambda i: (i, 0, 0))],
        out_shape=[jax.ShapeDtypeStruct((R, QW), F32), jax.ShapeDtypeStruct((R, SSD_INNER), F32),
                   jax.ShapeDtypeStruct((R, POOL_DIM), F32), jax.ShapeDtypeStruct((CAT, D), F32),
                   jax.ShapeDtypeStruct((R // SB, 8, D), F32)],
        scratch_shapes=[pltpu.VMEM((TM, D), MXU)],
        compiler_params=_cp(48),
    )(dx1, mix, cat, bm, wo)


def mlp_fwd(x1, bm, nw, w1, w2, side=None):
    R = x1.shape[0]

    def body(x1_ref, bm_ref, nw_ref, w1_ref, w2_ref, x2_ref, mo_ref, r_ref, h2_ref):
        for s in range(TM // SB):
            rows = slice(s * SB, (s + 1) * SB)
            xhat, _ = _rms_hat(x1_ref[rows, :])
            h = xhat * nw_ref[...] * (1.0 + bm_ref[s, 4:5, :]) + bm_ref[s, 3:4, :]
            h2_ref[rows, :] = h.astype(h2_ref.dtype)
        for j in range(D_FF // FF_BLK):
            cols = slice(j * FF_BLK, (j + 1) * FF_BLK)
            r = jnp.maximum(_dot(h2_ref[...], w1_ref[:, cols]), 0.0)
            r_ref[:, cols] = r.astype(r_ref.dtype)
            d = _dot((r * r).astype(MXU), w2_ref[cols, :])
            if j == 0:
                x2_ref[...] = d
            else:
                x2_ref[...] += d
        mo_ref[...] = x2_ref[...].astype(mo_ref.dtype)
        for s in range(TM // SB):
            rows = slice(s * SB, (s + 1) * SB)
            x2_ref[rows, :] = x1_ref[rows, :] + bm_ref[s, 5:6, :] * x2_ref[rows, :]

    grid = (R // TM,)
    body, side_in, side_out, side_shapes, side_scratch, side_args = _side_wrap(body, 5, 4, 0, side, grid)
    outs = pl.pallas_call(
        body, name="mlp_fwd" if side is None else "mlp_fwd_comm", grid=grid,
        in_specs=[_rowspec(D), pl.BlockSpec((TM // SB, 8, D), lambda i: (i, 0, 0)), _fullspec((1, D)),
                  _resident((D, D_FF)), _resident((D_FF, D))] + side_in,
        out_specs=[_rowspec(D), _rowspec(D), _rowspec(D_FF), _rowspec(D)] + side_out,
        out_shape=[jax.ShapeDtypeStruct((R, D), F32), jax.ShapeDtypeStruct((R, D), MXU),
                   jax.ShapeDtypeStruct((R, D_FF), BF16), jax.ShapeDtypeStruct((R, D), MXU)] + side_shapes,
        scratch_shapes=side_scratch,
        compiler_params=_cp(56),
    )(x1, bm, nw, w1, w2, *side_args)
    return tuple(outs[:4]) + (list(outs[4:]),)


def mlp_bwd(dx2, x1, mo, r, bm, nw, w2, w1, side=None):
    R = x1.shape[0]

    def body(dx2_ref, x1_ref, mo_ref, r_ref, bm_ref, nw_ref, w2_ref, w1_ref, dx1_ref, du_ref, dob_ref, part_ref,
             acc_ref):
        for s in range(TM // SB):
            rows = slice(s * SB, (s + 1) * SB)
            dob_ref[rows, :] = (dx2_ref[rows, :] * bm_ref[s, 5:6, :]).astype(dob_ref.dtype)
        for j in range(D_FF // FF_BLK):
            cols = slice(j * FF_BLK, (j + 1) * FF_BLK)
            du = _dotg(dob_ref[...], w2_ref[cols, :], NT) * (2.0 * r_ref[:, cols].astype(F32))
            du_ref[:, cols] = du.astype(du_ref.dtype)
            d = _dotg(du_ref[:, cols], w1_ref[:, cols], NT)
            if j == 0:
                acc_ref[...] = d
            else:
                acc_ref[...] += d
        w = nw_ref[...]
        for s in range(TM // SB):
            rows = slice(s * SB, (s + 1) * SB)
            xhat, rstd = _rms_hat(x1_ref[rows, :])
            dh = acc_ref[rows, :]
            dx, dnw = _rms_bwd(dh * (1.0 + bm_ref[s, 4:5, :]), xhat, rstd, w)
            d2 = dx2_ref[rows, :]
            dx1_ref[rows, :] = d2 + dx
            part_ref[s] = jnp.concatenate(
                [_colsum(dh), _colsum(dh * xhat * w), _colsum(d2 * mo_ref[rows, :].astype(F32)), dnw,
                 jnp.zeros((4, D), F32)], axis=0)

    grid = (R // TM,)
    body, side_in, side_out, side_shapes, side_scratch, side_args = _side_wrap(body, 8, 4, 1, side, grid)
    outs = pl.pallas_call(
        body, name="mlp_bwd" if side is None else "mlp_bwd_comm", grid=grid,
        in_specs=[_rowspec(D), _rowspec(D), _rowspec(D), _rowspec(D_FF),
                  pl.BlockSpec((TM // SB, 8, D), lambda i: (i, 0, 0)), _fullspec((1, D)),
                  _resident((D_FF, D)), _resident((D, D_FF))] + side_in,
        out_specs=[_rowspec(D), _rowspec(D_FF), _rowspec(D), pl.BlockSpec((TM // SB, 8, D), lambda i: (i, 0, 0))]
                  + side_out,
        out_shape=[jax.ShapeDtypeStruct((R, D), F32), jax.ShapeDtypeStruct((R, D_FF), MXU),
                   jax.ShapeDtypeStruct((R, D), MXU), jax.ShapeDtypeStruct((R // SB, 8, D), F32)] + side_shapes,
        scratch_shapes=[pltpu.VMEM((TM, D), F32)] + side_scratch,
        compiler_params=_cp(56),
    )(dx2, x1, mo, r, bm, nw, w2, w1, *side_args)
    return tuple(outs[:4]) + (list(outs[4:]),)


def mm_tn(a, b, square_a=False, name="mm_tn", col_blocks=False):
    R, M = a.shape
    N = b.shape[1]
    tm = M if M <= 1408 else 1024
    tn = N if N <= 2176 else 1024
    tk = next((c for c in ((2176, 1088, 512) if tm + tn <= 2048 else (1088, 512)) if R % c == 0), R)
    assert not col_blocks or tm == M

    def body(a_ref, b_ref, o_ref):
        @pl.when(pl.program_id(2) == 0)
        def _():
            o_ref[...] = jnp.zeros_like(o_ref)

        av = a_ref[...]
        if square_a:
            av = av.astype(F32)
            av = (av * av).astype(MXU)
        prod = _dotg(av.astype(MXU), b_ref[...].astype(MXU), TN)
        if col_blocks:
            o_ref[0] += prod
        else:
            o_ref[...] += prod

    if col_blocks:
        out_spec = pl.BlockSpec((1, tm, tn), lambda i, j, k: (j, 0, 0))
        out_shape = jax.ShapeDtypeStruct((N // tn, M, tn), F32)
    else:
        out_spec = pl.BlockSpec((tm, tn), lambda i, j, k: (i, j))
        out_shape = jax.ShapeDtypeStruct((M, N), F32)
    return pl.pallas_call(
        body, name=name, grid=(M // tm, N // tn, R // tk),
        in_specs=[pl.BlockSpec((tk, tm), lambda i, j, k: (k, i)), pl.BlockSpec((tk, tn), lambda i, j, k: (k, j))],
        out_specs=out_spec, out_shape=out_shape,
        compiler_params=_cp(48),
    )(a, b)


def final_loss(x, tgt, fw, blocks_per_sample):
    R = x.shape[0]
    nxb = blocks_per_sample - 1

    def body(x_ref, t_ref, fw_ref, dx_ref, part_ref):
        i = pl.program_id(0)
        is_ctx = (i % blocks_per_sample) == 0
        xhat, rstd = _rms_hat(x_ref[...])
        w = fw_ref[...]
        err = xhat * w - t_ref[...]
        dx, dfw = _rms_bwd(err * (1.0 / D), xhat, rstd, w)
        keep = jnp.where(is_ctx, 0.0, 1.0)
        dx_ref[...] = dx * keep
        part_ref[0] = jnp.concatenate([dfw * keep, _colsum(err * err) * keep, jnp.zeros((6, D), F32)], axis=0)

    def tmap(i):
        return ((i // blocks_per_sample) * nxb + jnp.maximum(i % blocks_per_sample - 1, 0), 0)

    return pl.pallas_call(
        body, name="final_loss", grid=(R // SB,),
        in_specs=[_rowspec(D, SB), pl.BlockSpec((SB, D), tmap), _fullspec((1, D))],
        out_specs=[_rowspec(D, SB), pl.BlockSpec((1, 8, D), lambda i: (i, 0, 0))],
        out_shape=[jax.ShapeDtypeStruct((R, D), F32), jax.ShapeDtypeStruct((R // SB, 8, D), F32)],
    )(x, tgt, fw)


def _softplus(v):
    return jnp.maximum(v, 0.0) + jnp.log(1.0 + jnp.exp(-jnp.abs(v)))


def _conv_taps(ext):
    return [_shift(ext, k - 1) for k in range(4)]


def _conv_out(taps, cw_ref, cb_ref):
    return (cb_ref[...] + cw_ref[0:1, :] * taps[0] + cw_ref[1:2, :] * taps[1] + cw_ref[2:3, :] * taps[2]
            + cw_ref[3:4, :] * taps[3])


def _dt_dir(v, d):
    lane = lax.broadcasted_iota(jnp.int32, v.shape, 1)
    return jnp.where(lane < SSD_HEADS, pltpu.roll(v, (128 - DT0 - SSD_HEADS * d) % 128, axis=1), 0.0)


def ssd_prep(pxbc, plast, cw, cb, dtb, blocks_per_sample):
    R = pxbc.shape[0]
    prev, nxt = _halo_specs(XBC, R, 8 * 4 // pxbc.dtype.itemsize)

    def body(cur_ref, prev_ref, nxt_ref, pl_ref, cw_ref, cb_ref, dtb_ref, xs_ref, bm_ref, cm_ref, dt_ref):
        i = pl.program_id(0)
        ext = _ext_rows(cur_ref[...], prev_ref[...], nxt_ref[...], i, blocks_per_sample)
        co = _conv_out(_conv_taps(ext), cw_ref, cb_ref)
        a = co * _sigmoid(co)
        xs_ref[...] = a[:, 0:384]
        bm_ref[...] = a[:, 384:640]
        cm_ref[...] = a[:, 640:896]
        sp = _softplus(pl_ref[...] + dtb_ref[...])
        dt_ref[0] = _dt_dir(sp, 0)
        dt_ref[1] = _dt_dir(sp, 1)

    return pl.pallas_call(
        body, name="ssd_prep", grid=(R // SB,),
        in_specs=[_rowspec(XBC, SB), prev, nxt, _rowspec(128, SB), _fullspec((8, XBC)), _fullspec((1, XBC)),
                  _fullspec((1, 128))],
        out_specs=[_rowspec(384, SB), _rowspec(256, SB), _rowspec(256, SB),
                   pl.BlockSpec((2, SB, 128), lambda i: (0, i, 0))],
        out_shape=[jax.ShapeDtypeStruct((R, 384), F32), jax.ShapeDtypeStruct((R, 256), F32),
                   jax.ShapeDtypeStruct((R, 256), F32), jax.ShapeDtypeStruct((2, R, 128), F32)],
    )(pxbc, pxbc, pxbc, plast, cw, cb, dtb)


def _chunk_index(d, s, nc):
    nctx = CTX // CHUNK
    back = jnp.where(s < nctx, nctx - 1 - s, nc + nctx - 1 - s)
    return jnp.where(d == 0, s, back)


def _scan_common(d, dt, arow, eexp, xs):
    ii = lax.broadcasted_iota(jnp.int32, (CHUNK, CHUNK), 0)
    jj = lax.broadcasted_iota(jnp.int32, (CHUNK, CHUNK), 1)
    mask = ((ii - jj) * (1 - 2 * d)) >= 0
    adt = dt * arow
    tmat = jnp.where(mask, 1.0, 0.0)
    cs = _dot_hi(tmat, adt, sel_first=True)
    tot = _colsum(adt)
    dtx = _dot_hi(dt, eexp)
    xt = xs * dtx
    ecs = jnp.exp(cs)
    ecx = _dot_hi(ecs, eexp)
    dte = jnp.exp(tot - cs)
    dtex = _dot_hi(dte, eexp)
    etot = jnp.exp(tot)
    etx = _dot_hi(jnp.broadcast_to(etot, (8, 128)), eexp)[0:1, :]
    return mask, tmat, adt, cs, tot, dtx, xt, ecs, ecx, dte, dtex, etot, etx


def _decay_matrix(mask, cs, cst, h):
    return jnp.exp(jnp.where(mask, cs[:, h:h + 1] - cst[h:h + 1, :], -1e30))


def _side_wrap(body, n_in, n_out, n_scratch, side, grid):
    if side is None:
        return body, [], [], [], [], []
    ni, no = len(side.ins), len(side.out_shapes)

    def wrapped(*refs):
        ins, refs = refs[:n_in], refs[n_in:]
        side_ins, refs = refs[:ni], refs[ni:]
        outs, refs = refs[:n_out], refs[n_out:]
        side_outs, refs = refs[:no], refs[no:]
        scratch, sems = refs[:n_scratch], refs[n_scratch:]
        ids = [pl.program_id(a) for a in range(len(grid))]
        first = functools.reduce(jnp.logical_and, [i == 0 for i in ids])
        last = functools.reduce(jnp.logical_and, [i == g - 1 for i, g in zip(ids, grid)])
        pl.when(first)(lambda: side.start(side_ins, side_outs, sems))
        body(*ins, *outs, *scratch)
        pl.when(last)(lambda: side.finish(side_ins, side_outs, sems))

    return wrapped, [ANY] * ni, [ANY] * no, list(side.out_shapes), _sems(side.nsem), list(side.ins)


def ssd_scan_fwd(xs, bm, cm, dtv, arow, eexp, nb, T, side=None):
    R = xs.shape[0]
    nc = T // CHUNK
    B = range(nb)

    def body(xs_ref, bm_ref, cm_ref, dt_ref, a_ref, e_ref, y_ref, hin_ref, st_ref):
        d = pl.program_id(0)
        s = pl.program_id(1)

        @pl.when(s == 0)
        def _():
            st_ref[...] = jnp.zeros_like(st_ref)

        eexp = e_ref[...]
        com = [_scan_common(d, dt_ref[0, b], a_ref[0, 0:1, :], eexp, xs_ref[b]) for b in B]
        mask = com[0][0]
        cs = [com[b][3] for b in B]
        cst = [cs[b].T for b in B]
        sin = [st_ref[b] for b in B]
        for b in B:
            hin_ref[0, b] = sin[b]
        sb = [sin[b].astype(MXU) for b in B]
        xtb = [com[b][6].astype(MXU) for b in B]
        xw = [(com[b][6] * com[b][10]).astype(MXU) for b in B]
        g0 = lax.broadcasted_iota(jnp.int32, (CHUNK, SSD_INNER), 1) < 192
        lane = lax.broadcasted_iota(jnp.int32, (CHUNK, 128), 1)
        c = [[cm_ref[b, :, 0:128].astype(MXU), cm_ref[b, :, 128:256].astype(MXU)] for b in B]
        bq = [[bm_ref[b, :, 0:128].astype(MXU), bm_ref[b, :, 128:256].astype(MXU)] for b in B]
        y = [jnp.where(g0, _dot(c[b][0], sb[b]), _dot(c[b][1], sb[b])) * com[b][8] for b in B]
        cb = [[_dotg(c[b][g], bq[b][g], NT) for g in range(2)] for b in B]
        blocks = [[] for _ in B]
        for blk in range(3):
            acc = [None for _ in B]
            for hh in range(2):
                h = blk * 2 + hh
                for b in B:
                    m = (cb[b][h // 3] * _decay_matrix(mask, cs[b], cst[b], h)).astype(MXU)
                    res = _dot(m, xtb[b][:, blk * 128:(blk + 1) * 128])
                    acc[b] = res if hh == 0 else jnp.where(lane < 64, acc[b], res)
            for b in B:
                blocks[b].append(acc[b])
        for b in B:
            y_ref[0, b] = y[b] + jnp.concatenate(blocks[b], axis=1)
            st_ref[b] = sin[b] * com[b][12] + jnp.where(g0, _dotg(bq[b][0], xw[b], TN), _dotg(bq[b][1], xw[b], TN))

    def rows(cols):
        return pl.BlockSpec((nb, CHUNK, cols), lambda d, s: (0, _chunk_index(d, s, nc), 0))

    def by_dir(cols):
        return pl.BlockSpec((1, nb, CHUNK, cols), lambda d, s: (d, 0, _chunk_index(d, s, nc), 0))

    grid = (2, nc)
    body, side_in, side_out, side_shapes, side_scratch, side_args = _side_wrap(body, 6, 2, 1, side, grid)
    outs = pl.pallas_call(
        body, name="ssd_scan_fwd" if side is None else "ssd_scan_fwd_comm", grid=grid,
        in_specs=[rows(384), rows(256), rows(256), by_dir(128), pl.BlockSpec((1, 8, 128), lambda d, s: (d, 0, 0)),
                  pl.BlockSpec((128, 384), lambda d, s: (0, 0))] + side_in,
        out_specs=[by_dir(384),
                   pl.BlockSpec((1, nb, CHUNK, 384), lambda d, s: (d * nc + _chunk_index(d, s, nc), 0, 0, 0))] + side_out,
        out_shape=[jax.ShapeDtypeStruct((2, nb, T, 384), F32), jax.ShapeDtypeStruct((2 * nc, nb, CHUNK, 384), F32)]
                  + side_shapes,
        scratch_shapes=[pltpu.VMEM((nb, CHUNK, 384), F32)] + side_scratch,
    )(xs.reshape(nb, T, 384), bm.reshape(nb, T, 256), cm.reshape(nb, T, 256), dtv.reshape(2, nb, T, 128), arow, eexp,
      *side_args)
    return outs[0].reshape(2, R, 384), outs[1], list(outs[2:])


def ssd_scan_bwd(xs, bm, cm, dtv, arow, eexp, hin, dy, nb, T, side=None):
    R = xs.shape[0]
    nc = T // CHUNK
    B = range(nb)

    def chunk(d, s):
        return _chunk_index(d, nc - 1 - s, nc)

    def body(xs_ref, bm_ref, cm_ref, dt_ref, a_ref, e_ref, hin_ref, dy_ref,
             dxs_ref, dbm_ref, dcm_ref, ddt_ref, da_ref, ds_ref):
        d = pl.program_id(0)
        s = pl.program_id(1)

        @pl.when(s == 0)
        def _():
            ds_ref[...] = jnp.zeros_like(ds_ref)
            da_ref[...] = jnp.zeros_like(da_ref)

        eexp = e_ref[...]
        arow = a_ref[0, 0:1, :]
        dt = [dt_ref[0, b] for b in B]
        xs_v = [xs_ref[b] for b in B]
        com = [_scan_common(d, dt[b], arow, eexp, xs_v[b]) for b in B]
        mask, tmat = com[0][0], com[0][1]
        cs, dtx, xt, ecs, ecx, dte, dtex, etot, etx = [[com[b][i] for b in B] for i in (3, 5, 6, 7, 8, 9, 10, 11, 12)]
        cst = [cs[b].T for b in B]
        sin = [hin_ref[0, b] for b in B]
        sb = [sin[b].astype(MXU) for b in B]
        dsp = [ds_ref[b] for b in B]
        dyv = [dy_ref[b] for b in B]
        xtb = [xt[b].astype(MXU) for b in B]
        xw = [(xt[b] * dtex[b]).astype(MXU) for b in B]
        g0 = lax.broadcasted_iota(jnp.int32, (CHUNK, SSD_INNER), 1) < 192
        lane = lax.broadcasted_iota(jnp.int32, (CHUNK, 128), 1)
        sub = lax.broadcasted_iota(jnp.int32, (CHUNK, 128), 0)
        c = [[cm_ref[b, :, 0:128].astype(MXU), cm_ref[b, :, 128:256].astype(MXU)] for b in B]
        bq = [[bm_ref[b, :, 0:128].astype(MXU), bm_ref[b, :, 128:256].astype(MXU)] for b in B]

        cs_prod = [jnp.where(g0, _dot(c[b][0], sb[b]), _dot(c[b][1], sb[b])) for b in B]
        dcsp = [dyv[b] * ecx[b] for b in B]
        dcsp_g = [[jnp.where(g0, dcsp[b], 0.0).astype(MXU), jnp.where(g0, 0.0, dcsp[b]).astype(MXU)] for b in B]
        dcs = [_dot_hi(dyv[b] * cs_prod[b], eexp, NT) * ecs[b] for b in B]
        dc = [[_dotg(dcsp_g[b][g], sb[b], NT) for g in range(2)] for b in B]
        dsin = [_dotg(c[b][0], dcsp_g[b][0], TN) + _dotg(c[b][1], dcsp_g[b][1], TN) + dsp[b] * etx[b] for b in B]

        dtot = [_dot_hi(jnp.broadcast_to(_colsum(dsp[b] * sin[b]), (8, SSD_INNER)), eexp, NT)[0:1, :] * etot[b] for b in B]
        dsp_g = [[jnp.where(g0, dsp[b], 0.0).astype(MXU), jnp.where(g0, 0.0, dsp[b]).astype(MXU)] for b in B]
        dxw = [_dot(bq[b][0], dsp_g[b][0]) + _dot(bq[b][1], dsp_g[b][1]) for b in B]
        db = [[_dotg(xw[b], dsp_g[b][g], NT) for g in range(2)] for b in B]
        dxt = [dxw[b] * dtex[b] for b in B]
        ddte = [_dot_hi(dxw[b] * xt[b], eexp, NT) * dte[b] for b in B]
        dtot = [dtot[b] + _colsum(ddte[b]) for b in B]
        dcs = [dcs[b] - ddte[b] for b in B]

        cb = [[_dotg(c[b][g], bq[b][g], NT) for g in range(2)] for b in B]
        dg = [[jnp.zeros((CHUNK, CHUNK), F32), jnp.zeros((CHUNK, CHUNK), F32)] for _ in B]
        dcs_rows = [jnp.zeros((CHUNK, 128), F32) for _ in B]
        dxt_blocks = [[] for _ in B]
        for blk in range(3):
            acc = [jnp.zeros((CHUNK, 128), F32) for _ in B]
            for hh in range(2):
                h = blk * 2 + hh
                g = h // 3
                mine = (lane < 64) if hh == 0 else (lane >= 64)
                for b in B:
                    dyh = jnp.where(mine, dyv[b][:, blk * 128:(blk + 1) * 128], 0.0).astype(MXU)
                    lh = _decay_matrix(mask, cs[b], cst[b], h)
                    m = cb[b][g] * lh
                    dm = _dotg(dyh, xtb[b][:, blk * 128:(blk + 1) * 128], NT)
                    acc[b] = acc[b] + _dotg(m.astype(MXU), dyh, TN)
                    dg[b][g] = dg[b][g] + dm * lh
                    q = dm * m
                    dcs[b] = dcs[b] + jnp.where(lane == h, jnp.sum(q, axis=1, keepdims=True), 0.0)
                    dcs_rows[b] = dcs_rows[b] - jnp.where(sub == h, jnp.sum(q, axis=0, keepdims=True), 0.0)
            for b in B:
                dxt_blocks[b].append(acc[b])
        for b in B:
            dxt[b] = dxt[b] + jnp.concatenate(dxt_blocks[b], axis=1)
            for g in range(2):
                dgb = dg[b][g].astype(MXU)
                dc[b][g] = dc[b][g] + _dot(dgb, bq[b][g])
                db[b][g] = db[b][g] + _dotg(dgb, c[b][g], TN)
            dcs[b] = dcs[b] + dcs_rows[b].T

        for b in B:
            dadt = _dot_hi(tmat, dcs[b], TN, sel_first=True) + dtot[b]
            ddt_ref[0, b] = dadt * arow + _dot_hi(dxt[b] * xs_v[b], eexp, NT)
            da_ref[0, b, 0:1, :] += _colsum(dadt * dt[b])
            dxs_ref[0, b] = (dxt[b] * dtx[b]).astype(dxs_ref.dtype)
            dbm_ref[0, b] = jnp.concatenate(db[b], axis=1).astype(dbm_ref.dtype)
            dcm_ref[0, b] = jnp.concatenate(dc[b], axis=1).astype(dcm_ref.dtype)
            ds_ref[b] = dsin[b]

    def rows(cols):
        return pl.BlockSpec((nb, CHUNK, cols), lambda d, s: (0, chunk(d, s), 0))

    def by_dir(cols):
        return pl.BlockSpec((1, nb, CHUNK, cols), lambda d, s: (d, 0, chunk(d, s), 0))

    grid = (2, nc)
    body, side_in, side_out, side_shapes, side_scratch, side_args = _side_wrap(body, 8, 5, 1, side, grid)
    outs = pl.pallas_call(
        body, name="ssd_scan_bwd" if side is None else "ssd_scan_bwd_comm", grid=grid,
        in_specs=[rows(384), rows(256), rows(256), by_dir(128), pl.BlockSpec((1, 8, 128), lambda d, s: (d, 0, 0)),
                  pl.BlockSpec((128, 384), lambda d, s: (0, 0)),
                  pl.BlockSpec((1, nb, CHUNK, 384), lambda d, s: (d * nc + chunk(d, s), 0, 0, 0)), rows(384)] + side_in,
        out_specs=[by_dir(384), by_dir(256), by_dir(256), by_dir(128),
                   pl.BlockSpec((1, nb, 8, 128), lambda d, s: (d, 0, 0, 0))] + side_out,
        out_shape=[jax.ShapeDtypeStruct((2, nb, T, 384), MXU), jax.ShapeDtypeStruct((2, nb, T, 256), MXU),
                   jax.ShapeDtypeStruct((2, nb, T, 256), MXU), jax.ShapeDtypeStruct((2, nb, T, 128), F32),
                   jax.ShapeDtypeStruct((2, nb, 8, 128), F32)] + side_shapes,
        scratch_shapes=[pltpu.VMEM((nb, CHUNK, 384), F32)] + side_scratch,
    )(xs.reshape(nb, T, 384), bm.reshape(nb, T, 256), cm.reshape(nb, T, 256), dtv.reshape(2, nb, T, 128), arow, eexp,
      hin, dy.reshape(nb, T, 384), *side_args)
    return (outs[0].reshape(2, R, 384), outs[1].reshape(2, R, 256), outs[2].reshape(2, R, 256),
            outs[3].reshape(2, R, 128), outs[4], list(outs[5:]))


def _group_rms(g):
    lane = lax.broadcasted_iota(jnp.int32, g.shape, 1)
    g0 = lane < 192
    gg = g * g
    s0 = jnp.sum(jnp.where(g0, gg, 0.0), axis=-1, keepdims=True)
    s1 = jnp.sum(gg, axis=-1, keepdims=True) - s0
    rstd = jnp.where(g0, lax.rsqrt(s0 * (1.0 / 192) + EPS), lax.rsqrt(s1 * (1.0 / 192) + EPS))
    return rstd, g0


def ssd_out_fwd(y2, xs, pz, dexp, nw):
    R = xs.shape[0]

    def body(y_ref, xs_ref, z_ref, d_ref, nw_ref, o_ref):
        z = z_ref[...].astype(F32)
        yy = y_ref[0] + y_ref[1] + xs_ref[...] * d_ref[...]
        g = yy * (z * _sigmoid(z))
        rstd, _ = _group_rms(g)
        o_ref[...] = g * rstd * nw_ref[...]

    return pl.pallas_call(
        body, name="ssd_out_fwd", grid=(R // TM,),
        in_specs=[pl.BlockSpec((2, TM, 384), lambda i: (0, i, 0)), _rowspec(384), _rowspec(384),
                  _fullspec((1, 384)), _fullspec((1, 384))],
        out_specs=_rowspec(384),
        out_shape=jax.ShapeDtypeStruct((R, 384), F32),
    )(y2, xs, pz, dexp, nw)


def ssd_out_bwd(dout, y2, xs, pz, dexp, nw):
    R = xs.shape[0]

    def body(do_ref, y_ref, xs_ref, z_ref, d_ref, nw_ref, dy_ref, dz_ref, dxs_ref, part_ref):
        z = z_ref[...].astype(F32)
        xs_v = xs_ref[...]
        yy = y_ref[0] + y_ref[1] + xs_v * d_ref[...]
        sig = _sigmoid(z)
        sz = z * sig
        g = yy * sz
        rstd, g0 = _group_rms(g)
        ghat = g * rstd
        do = do_ref[...]
        dgn = do * nw_ref[...]
        t = dgn * ghat
        t0 = jnp.sum(jnp.where(g0, t, 0.0), axis=-1, keepdims=True)
        t1 = jnp.sum(t, axis=-1, keepdims=True) - t0
        dg = rstd * (dgn - ghat * jnp.where(g0, t0, t1) * (1.0 / 192))
        dyy = dg * sz
        dy_ref[...] = dyy
        dz_ref[...] = (dg * yy * (sig * (1.0 + z * (1.0 - sig)))).astype(dz_ref.dtype)
        dxs_ref[...] = dyy * d_ref[...]
        part_ref[0] = jnp.concatenate([_colsum(do * ghat), _colsum(dyy * xs_v), jnp.zeros((6, 384), F32)], axis=0)

    return pl.pallas_call(
        body, name="ssd_out_bwd", grid=(R // TM,),
        in_specs=[_rowspec(384), pl.BlockSpec((2, TM, 384), lambda i: (0, i, 0)), _rowspec(384), _rowspec(384),
                  _fullspec((1, 384)), _fullspec((1, 384))],
        out_specs=[_rowspec(384), _rowspec(384), _rowspec(384), pl.BlockSpec((1, 8, 384), lambda i: (i, 0, 0))],
        out_shape=[jax.ShapeDtypeStruct((R, 384), F32), jax.ShapeDtypeStruct((R, 384), MXU),
                   jax.ShapeDtypeStruct((R, 384), F32), jax.ShapeDtypeStruct((R // TM, 8, 384), F32)],
    )(dout, y2, xs, pz, dexp, nw)


def ssd_prep_bwd_a(pxbc, plast, cw, cb, dtb, dxs_skip, dxs2, dbm2, dcm2, ddt2, blocks_per_sample):
    R = pxbc.shape[0]
    prev, nxt = _halo_specs(XBC, R, 8 * 4 // pxbc.dtype.itemsize)

    def body(cur_ref, prev_ref, nxt_ref, pl_ref, cw_ref, cb_ref, dtb_ref, dsk_ref, dxs_ref, dbm_ref, dcm_ref, ddt_ref,
             dpre_ref, dlast_ref, part_ref):
        i = pl.program_id(0)
        ext = _ext_rows(cur_ref[...], prev_ref[...], nxt_ref[...], i, blocks_per_sample)
        taps = _conv_taps(ext)
        co = _conv_out(taps, cw_ref, cb_ref)
        sig = _sigmoid(co)
        both = lambda ref: ref[0].astype(F32) + ref[1].astype(F32)
        up = jnp.concatenate([dsk_ref[...] + both(dxs_ref), both(dbm_ref), both(dcm_ref)], axis=1)
        dpre = up * (sig * (1.0 + co * (1.0 - sig)))
        dpre_ref[...] = dpre
        raw = pl_ref[...] + dtb_ref[...]
        lane = lax.broadcasted_iota(jnp.int32, raw.shape, 1)
        ddt = (pltpu.roll(ddt_ref[0], DT0, axis=1) + pltpu.roll(ddt_ref[1], DT0 + SSD_HEADS, axis=1))
        ddt = jnp.where(jnp.logical_and(lane >= DT0, lane < DT0 + 2 * SSD_HEADS), ddt * _sigmoid(raw), 0.0)
        dlast_ref[...] = ddt.astype(dlast_ref.dtype)
        rows = [_colsum(dpre * taps[k]) for k in range(4)]
        rows.append(_colsum(dpre))
        rows.append(jnp.concatenate([_colsum(ddt), jnp.zeros((1, XBC - 128), F32)], axis=1))
        rows.append(jnp.zeros((2, XBC), F32))
        part_ref[0] = jnp.concatenate(rows, axis=0)

    dirspec = lambda n: pl.BlockSpec((2, SB, n), lambda i: (0, i, 0))
    return pl.pallas_call(
        body, name="ssd_prep_bwd_a", grid=(R // SB,),
        in_specs=[_rowspec(XBC, SB), prev, nxt, _rowspec(128, SB), _fullspec((8, XBC)), _fullspec((1, XBC)),
                  _fullspec((1, 128)), _rowspec(384, SB), dirspec(384), dirspec(256), dirspec(256), dirspec(128)],
        out_specs=[_rowspec(XBC, SB), _rowspec(128, SB), pl.BlockSpec((1, 8, XBC), lambda i: (i, 0, 0))],
        out_shape=[jax.ShapeDtypeStruct((R, XBC), F32), jax.ShapeDtypeStruct((R, 128), MXU),
                   jax.ShapeDtypeStruct((R // SB, 8, XBC), F32)],
    )(pxbc, pxbc, pxbc, plast, cw, cb, dtb, dxs_skip, dxs2, dbm2, dcm2, ddt2)


def ssd_prep_bwd_b(dpre, cw, blocks_per_sample):
    R = dpre.shape[0]
    prev, nxt = _halo_specs(XBC, R)

    def body(cur_ref, prev_ref, nxt_ref, cw_ref, o_ref):
        i = pl.program_id(0)
        ext = _ext_rows(cur_ref[...], prev_ref[...], nxt_ref[...], i, blocks_per_sample)
        o_ref[...] = (cw_ref[0:1, :] * _shift(ext, 1) + cw_ref[1:2, :] * _shift(ext, 0)
                      + cw_ref[2:3, :] * _shift(ext, -1) + cw_ref[3:4, :] * _shift(ext, -2)).astype(o_ref.dtype)

    return pl.pallas_call(
        body, name="ssd_prep_bwd_b", grid=(R // SB,),
        in_specs=[_rowspec(XBC, SB), prev, nxt, _fullspec((8, XBC))],
        out_specs=_rowspec(XBC, SB),
        out_shape=jax.ShapeDtypeStruct((R, XBC), MXU),
    )(dpre, dpre, dpre, cw)


def _rope(u, cos, sa, sb):
    return u * cos + pltpu.roll(u, 120, axis=1) * sa + pltpu.roll(u, 8, axis=1) * sb


def _rope_t(du, cos, sa, sb):
    return du * cos + pltpu.roll(du * sa, 8, axis=1) + pltpu.roll(du * sb, 120, axis=1)


def mla_prep(pqa, pkva, plast, qnw, kvnw, wq, wk, wv, cos, sa, sb):
    R = pqa.shape[0]

    def body(qa_ref, kva_ref, pl_ref, qnw_ref, kvnw_ref, wq_ref, wk_ref, wv_ref, cos_ref, sa_ref, sb_ref,
             q_ref, k_ref, v_ref, cq_ref, ckv_ref):
        cos_v, sa_v, sb_v = cos_ref[...], sa_ref[...], sb_ref[...]
        xq, _ = _rms_hat(qa_ref[...].astype(F32))
        cq_ref[...] = (xq * qnw_ref[...]).astype(cq_ref.dtype)
        xkv, _ = _rms_hat(kva_ref[...].astype(F32))
        ckv_ref[...] = (xkv * kvnw_ref[...]).astype(ckv_ref.dtype)
        q = _dot(cq_ref[...], wq_ref[...])
        kn = _dot(ckv_ref[...], wk_ref[...])
        v_ref[...] = _dot(ckv_ref[...], wv_ref[...]).astype(v_ref.dtype)
        lane = lax.broadcasted_iota(jnp.int32, (TM, HP), 1)
        rope_lanes = jnp.logical_and(lane >= QK_NOPE, lane < QK_DIM)
        kr = _rope(jnp.where(rope_lanes, pltpu.roll(pl_ref[...], QK_NOPE, axis=1), 0.0), cos_v, sa_v, sb_v)
        for h in range(MLA_HEADS):
            cols = slice(h * HP, (h + 1) * HP)
            q_ref[:, cols] = (_rope(q[:, cols], cos_v, sa_v, sb_v) * Q_SCALE).astype(q_ref.dtype)
            k_ref[:, cols] = (kn[:, cols] + kr).astype(k_ref.dtype)

    return pl.pallas_call(
        body, name="mla_prep", grid=(R // TM,),
        in_specs=[_rowspec(256), _rowspec(256), _rowspec(128), _fullspec((1, 256)), _fullspec((1, 256)),
                  _fullspec((256, QW)), _fullspec((256, QW)), _fullspec((256, QW)),
                  _rowspec(HP), _rowspec(HP), _rowspec(HP)],
        out_specs=[_rowspec(QW), _rowspec(QW), _rowspec(QW), _rowspec(256), _rowspec(256)],
        out_shape=[jax.ShapeDtypeStruct((R, QW), MXU)] * 3 + [jax.ShapeDtypeStruct((R, 256), MXU)] * 2,
    )(pqa, pkva, plast, qnw, kvnw, wq, wk, wv, cos, sa, sb)


def mla_prep_bwd(dq, dk, dv, pqa, pkva, cq, ckv, qnw, kvnw, wq, wk, wv, cos, sa, sb):
    R = pqa.shape[0]

    def body(dq_ref, dk_ref, dv_ref, qa_ref, kva_ref, cq_ref, ckv_ref, qnw_ref, kvnw_ref, wq_ref, wk_ref, wv_ref,
             cos_ref, sa_ref, sb_ref, dqa_ref, dkva_ref, dkr_ref, dwq_ref, dwk_ref, dwv_ref, part_ref,
             dql_ref, dkm_ref, dvb_ref):
        @pl.when(pl.program_id(0) == 0)
        def _():
            dwq_ref[...] = jnp.zeros_like(dwq_ref)
            dwk_ref[...] = jnp.zeros_like(dwk_ref)
            dwv_ref[...] = jnp.zeros_like(dwv_ref)

        cos_v, sa_v, sb_v = cos_ref[...], sa_ref[...], sb_ref[...]
        lane = lax.broadcasted_iota(jnp.int32, (TM, HP), 1)
        rope_lanes = jnp.logical_and(lane >= QK_NOPE, lane < QK_DIM)
        dkr = jnp.zeros((TM, HP), F32)
        for h in range(MLA_HEADS):
            cols = slice(h * HP, (h + 1) * HP)
            dql_ref[:, cols] = (_rope_t(dq_ref[:, cols], cos_v, sa_v, sb_v) * ATT_SCALE).astype(dql_ref.dtype)
            dkh = dk_ref[:, cols] * LN2
            dkm_ref[:, cols] = jnp.where(lane < QK_NOPE, dkh, 0.0).astype(dkm_ref.dtype)
            dkr = dkr + jnp.where(rope_lanes, dkh, 0.0)
        dvb_ref[...] = dv_ref[...].astype(dvb_ref.dtype)
        dkr = jnp.where(rope_lanes, _rope_t(dkr, cos_v, sa_v, sb_v), 0.0)
        dkr_ref[...] = pltpu.roll(dkr, HP - QK_NOPE, axis=1).astype(dkr_ref.dtype)
        dwq_ref[...] += _dotg(cq_ref[...], dql_ref[...], TN)
        dwk_ref[...] += _dotg(ckv_ref[...], dkm_ref[...], TN)
        dwv_ref[...] += _dotg(ckv_ref[...], dvb_ref[...], TN)
        xq, rq = _rms_hat(qa_ref[...].astype(F32))
        dqa, dqnw = _rms_bwd(_dotg(dql_ref[...], wq_ref[...], NT), xq, rq, qnw_ref[...])
        dqa_ref[...] = dqa.astype(dqa_ref.dtype)
        xkv, rkv = _rms_hat(kva_ref[...].astype(F32))
        dckv = _dotg(dkm_ref[...], wk_ref[...], NT) + _dotg(dvb_ref[...], wv_ref[...], NT)
        dkva, dkvnw = _rms_bwd(dckv, xkv, rkv, kvnw_ref[...])
        dkva_ref[...] = dkva.astype(dkva_ref.dtype)
        part_ref[0] = jnp.concatenate([dqnw, dkvnw, jnp.zeros((6, 256), F32)], axis=0)

    return pl.pallas_call(
        body, name="mla_prep_bwd", grid=(R // TM,),
        in_specs=[_rowspec(QW), _rowspec(QW), _rowspec(QW), _rowspec(256), _rowspec(256), _rowspec(256), _rowspec(256),
                  _fullspec((1, 256)), _fullspec((1, 256)), _fullspec((256, QW)), _fullspec((256, QW)),
                  _fullspec((256, QW)), _rowspec(HP), _rowspec(HP), _rowspec(HP)],
        out_specs=[_rowspec(256), _rowspec(256), _rowspec(128), _fullspec((256, QW)), _fullspec((256, QW)),
                   _fullspec((256, QW)), pl.BlockSpec((1, 8, 256), lambda i: (i, 0, 0))],
        out_shape=[jax.ShapeDtypeStruct((R, 256), MXU), jax.ShapeDtypeStruct((R, 256), MXU),
                   jax.ShapeDtypeStruct((R, 128), MXU)] + [jax.ShapeDtypeStruct((256, QW), F32)] * 3
                  + [jax.ShapeDtypeStruct((R // TM, 8, 256), F32)],
        scratch_shapes=[pltpu.VMEM((TM, QW), MXU)] * 3,
    )(dq, dk, dv, pqa, pkva, cq, ckv, qnw, kvnw, wq, wk, wv, cos, sa, sb)


ATT_SCALE = QK_DIM ** -0.5
TQ = 256


LOG2E = 1.4426950408889634
LN2 = 0.6931471805599453
Q_SCALE = ATT_SCALE * LOG2E


def _key_chunks(T, n=2):
    unit = 256 if T % 256 == 0 else 128
    units = T // unit
    sizes = [(units // n + (1 if i < units % n else 0)) * unit for i in range(n)]
    return [(sum(sizes[:i]), sz) for i, sz in enumerate(sizes) if sz]


def attn_fwd(q, k, v, nb, T):
    R = q.shape[0]
    nq = T // TQ
    chunks = _key_chunks(T, 4)
    HEADS = range(3)

    def body(q_ref, k_ref, v_ref, o_ref, lse_ref):
        def lanes(h):
            return slice(h * HP, (h + 1) * HP)

        def logits(h, lo, n):
            return _dotg(q_ref[:, lanes(h)], k_ref[lo:lo + n, lanes(h)], NT)

        def weigh(h, s, lo, n):
            m = jnp.max(s, axis=-1, keepdims=True)
            p = jnp.exp2(s - m)
            return m, jnp.sum(p, axis=-1, keepdims=True), _dot(p.astype(MXU), v_ref[lo:lo + n, lanes(h)])

        def parts_of(ranges):
            out = [[] for _ in HEADS]
            s = [logits(h, *ranges[0]) for h in HEADS]
            for j, (lo, n) in enumerate(ranges):
                nxt = [logits(h, *ranges[j + 1]) for h in HEADS] if j + 1 < len(ranges) else None
                for h in HEADS:
                    out[h].append(weigh(h, s[h], lo, n))
                s = nxt
            return out

        def finish(all_parts):
            for h, parts in enumerate(all_parts):
                m = parts[0][0]
                for pm, _, _ in parts[1:]:
                    m = jnp.maximum(m, pm)
                l, o = 0.0, 0.0
                for pm, pl_, po in parts:
                    a = jnp.exp2(pm - m)
                    l = l + a * pl_
                    o = o + a * po
                o_ref[:, lanes(h)] = o / l
                lse_ref[:, lanes(h)] = jnp.broadcast_to(m + jnp.log(l) * LOG2E, (TQ, HP))

        i = pl.program_id(2)
        pl.when(i == 0)(lambda: finish(parts_of([(0, CTX)])))
        pl.when(i > 0)(lambda: finish(parts_of(chunks)))

    qspec = pl.BlockSpec((TQ, len(HEADS) * HP), lambda b, h, i: (b * nq + i, h))
    kspec = pl.BlockSpec((T, len(HEADS) * HP), lambda b, h, i: (b, h))
    return pl.pallas_call(
        body, name="attn_fwd", grid=(nb, MLA_HEADS // len(HEADS), nq),
        in_specs=[qspec, kspec, kspec], out_specs=[qspec, qspec],
        out_shape=[jax.ShapeDtypeStruct((R, QW), F32)] * 2,
        compiler_params=_cp(48),
    )(q, k, v)


def attn_bwd(q, k, v, o, lse, do, nb, T):
    R = q.shape[0]
    nq = T // TQ
    chunks = _key_chunks(T, 1)

    def body(q_ref, k_ref, v_ref, o_ref, lse_ref, do_ref, dq_ref, dk_ref, dv_ref):
        i = pl.program_id(2)

        @pl.when(i == 0)
        def _():
            dk_ref[...] = jnp.zeros_like(dk_ref)
            dv_ref[...] = jnp.zeros_like(dv_ref)

        def run(chunks):
            for h in range(2):
                lanes = slice(h * HP, (h + 1) * HP)
                qv = q_ref[:, lanes]
                dov = do_ref[:, lanes]
                dob = dov.astype(MXU)
                delta = jnp.sum(dov * o_ref[:, lanes], axis=-1, keepdims=True)
                lse_v = lse_ref[:, h * HP:h * HP + 1]
                dq = 0.0
                for lo, n in chunks:
                    kv = k_ref[lo:lo + n, lanes]
                    p = jnp.exp2(_dotg(qv, kv, NT) - lse_v)
                    dp = _dotg(dob, v_ref[lo:lo + n, lanes], NT)
                    dsb = (p * (dp - delta)).astype(MXU)
                    dq = dq + _dot(dsb, kv)
                    dk_ref[lo:lo + n, lanes] += _dotg(dsb, qv, TN)
                    dv_ref[lo:lo + n, lanes] += _dotg(p.astype(MXU), dob, TN)
                dq_ref[:, lanes] = dq

        pl.when(i == 0)(lambda: run([(0, CTX)]))
        pl.when(i > 0)(lambda: run(chunks))

    qspec = pl.BlockSpec((TQ, 2 * HP), lambda b, h, i: (b * nq + i, h))
    kspec = pl.BlockSpec((T, 2 * HP), lambda b, h, i: (b, h))
    return pl.pallas_call(
        body, name="attn_bwd", grid=(nb, MLA_HEADS // 2, nq),
        in_specs=[qspec, kspec, kspec, qspec, qspec, qspec],
        out_specs=[qspec, kspec, kspec],
        out_shape=[jax.ShapeDtypeStruct((R, QW), F32)] * 3,
        compiler_params=_cp(56),
    )(q, k, v, o, lse, do)


def _pool_geometry(i, blocks_per_sample, seq):
    j = i % blocks_per_sample
    n = jnp.where(j == 0, CTX, seq)
    t0 = jnp.where(j == 0, 0, (j - 1) * SB) - HALO
    lane = lax.broadcasted_iota(jnp.int32, (SB + 2 * HALO, POOL_DIM), 1)
    t = lax.broadcasted_iota(jnp.int32, (SB + 2 * HALO, POOL_DIM), 0) + t0
    wh = jnp.where(lane < 64, 1, jnp.where(lane < 128, 2, jnp.where(lane < 192, 4, 8)))
    cnt = jnp.minimum(t + wh, n) - jnp.maximum(t - wh, 0)
    return lane, 1.0 / jnp.maximum(cnt, 1).astype(F32)


def _by_window(lane, c2, c4, c8, c16):
    return jnp.where(lane < 64, c2, jnp.where(lane < 128, c4, jnp.where(lane < 192, c8, c16)))


def _window_sums(ext, lane, first):
    n = ext.shape[0]
    r = lambda a, s: pltpu.roll(a, s % n, axis=0)
    c2 = ext + r(ext, first)
    c4 = r(c2, 1) + r(c2, -1)
    c8 = r(c4, 2) + r(c4, -2)
    c16 = r(c8, 4) + r(c8, -4)
    return _by_window(lane, c2, c4, c8, c16)


def _pool_delta(ext, lane, inv):
    return (_window_sums(ext, lane, 1) * inv - ext)[HALO:HALO + SB, :]


def pool_fwd(ppool, wbd, scale, blocks_per_sample, seq):
    R = ppool.shape[0]
    prev, nxt = _halo_specs(POOL_DIM, R)

    def body(cur_ref, prev_ref, nxt_ref, w_ref, s_ref, o_ref):
        i = pl.program_id(0)
        ext = _ext_rows(cur_ref[...], prev_ref[...], nxt_ref[...], i, blocks_per_sample)
        lane, inv = _pool_geometry(i, blocks_per_sample, seq)
        dlt = _pool_delta(ext, lane, inv)
        o_ref[...] = _dot(dlt.astype(MXU), w_ref[...]) * s_ref[...]

    return pl.pallas_call(
        body, name="pool_fwd", grid=(R // SB,),
        in_specs=[_rowspec(POOL_DIM, SB), prev, nxt, _fullspec((POOL_DIM, POOL_DIM)), _fullspec((1, POOL_DIM))],
        out_specs=_rowspec(POOL_DIM, SB),
        out_shape=jax.ShapeDtypeStruct((R, POOL_DIM), F32),
    )(ppool, ppool, ppool, wbd, scale)


def pool_bwd(ppool, dpool, wbd, scale, blocks_per_sample, seq):
    R = ppool.shape[0]
    prev, nxt = _halo_specs(POOL_DIM, R)

    def body(cur_ref, prev_ref, nxt_ref, dcur_ref, dprev_ref, dnxt_ref, w_ref, s_ref, du_ref, dw_ref, part_ref):
        i = pl.program_id(0)

        @pl.when(i == 0)
        def _():
            dw_ref[...] = jnp.zeros_like(dw_ref)

        ext = _ext_rows(cur_ref[...], prev_ref[...], nxt_ref[...], i, blocks_per_sample)
        lane, inv = _pool_geometry(i, blocks_per_sample, seq)
        dlt = _pool_delta(ext, lane, inv).astype(MXU)
        dy = dcur_ref[...]
        part_ref[0] = jnp.concatenate([_colsum(dy * _dot(dlt, w_ref[...])), jnp.zeros((7, POOL_DIM), F32)], axis=0)
        dyp = (dy * s_ref[...]).astype(MXU)
        dw_ref[...] += _dotg(dlt, dyp, TN)
        dext = _ext_rows(dy, dprev_ref[...], dnxt_ref[...], i, blocks_per_sample)
        dd = _dotg((dext * s_ref[...]).astype(MXU), w_ref[...], NT)
        du_ref[...] = (_window_sums(dd * inv, lane, -1) - dd)[HALO:HALO + SB, :].astype(du_ref.dtype)

    return pl.pallas_call(
        body, name="pool_bwd", grid=(R // SB,),
        in_specs=[_rowspec(POOL_DIM, SB), prev, nxt, _rowspec(POOL_DIM, SB), prev, nxt,
                  _fullspec((POOL_DIM, POOL_DIM)), _fullspec((1, POOL_DIM))],
        out_specs=[_rowspec(POOL_DIM, SB), _fullspec((POOL_DIM, POOL_DIM)),
                   pl.BlockSpec((1, 8, POOL_DIM), lambda i: (i, 0, 0))],
        out_shape=[jax.ShapeDtypeStruct((R, POOL_DIM), MXU), jax.ShapeDtypeStruct((POOL_DIM, POOL_DIM), F32),
                   jax.ShapeDtypeStruct((R // SB, 8, POOL_DIM), F32)],
    )(ppool, ppool, ppool, dpool, dpool, dpool, wbd, scale)


def adamw(w, g, m, v, name="adamw"):
    rows, cols = w.shape
    tr = rows
    for cand in (512, 256, 128, 64, 32, 16, 8):
        if rows % cand == 0:
            tr = cand
            break
    bc1 = 1.0 - ADAM_B1 ** ADAM_STEP
    bc2 = 1.0 - ADAM_B2 ** ADAM_STEP

    def body(w_ref, g_ref, m_ref, v_ref, d_ref, nm_ref, nv_ref):
        g_v = g_ref[...]
        nm = ADAM_B1 * m_ref[...] + (1.0 - ADAM_B1) * g_v
        nv = ADAM_B2 * v_ref[...] + (1.0 - ADAM_B2) * (g_v * g_v)
        nm_ref[...] = nm
        nv_ref[...] = nv
        d_ref[...] = -ADAM_LR * ((nm / bc1) / (jnp.sqrt(nv / bc2) + ADAM_EPS) + ADAM_WD * w_ref[...])

    spec = pl.BlockSpec((tr, cols), lambda i: (i, 0))
    return pl.pallas_call(
        body, name=name, grid=(rows // tr,),
        in_specs=[spec] * 4, out_specs=[spec] * 3,
        out_shape=[jax.ShapeDtypeStruct((rows, cols), F32)] * 3,
    )(w, g, m, v)


MODR = 32


def _silu(v):
    return v * _sigmoid(v)


def mod_fwd(cond, w, b):
    n = w.shape[1]

    def body(c_ref, w_ref, b_ref, o_ref):
        o_ref[...] = _dot(_silu(c_ref[...]).astype(MXU), w_ref[...].astype(MXU)) + b_ref[...]

    return pl.pallas_call(
        body, name="mod_fwd", out_shape=jax.ShapeDtypeStruct((MODR, n), F32),
        in_specs=[_fullspec((MODR, D)), _fullspec((D, n)), _fullspec((1, n))], out_specs=_fullspec((MODR, n)),
        grid=(1,), compiler_params=_cp(40),
    )(cond, w, b)


def mod_wgrad(cond, dm):
    n = dm.shape[1]

    def body(c_ref, d_ref, o_ref):
        o_ref[...] = _dotg(_silu(c_ref[...]).astype(MXU), d_ref[...].astype(MXU), TN)

    return pl.pallas_call(
        body, name="mod_wgrad", out_shape=jax.ShapeDtypeStruct((D, n), F32),
        in_specs=[_fullspec((MODR, D)), _fullspec((MODR, n))], out_specs=_fullspec((D, n)),
        grid=(1,), compiler_params=_cp(40),
    )(cond, dm)


def mod_dgrad(dm, w):
    n = w.shape[1]

    def body(d_ref, w_ref, o_ref):
        o_ref[...] = _dotg(d_ref[...].astype(MXU), w_ref[...].astype(MXU), NT)

    return pl.pallas_call(
        body, name="mod_dgrad", out_shape=jax.ShapeDtypeStruct((8, D), F32),
        in_specs=[_fullspec((8, n)), _fullspec((D, n))], out_specs=_fullspec((8, D)),
        grid=(1,), compiler_params=_cp(40),
    )(dm, w)


def sum_leading(a, name="sum_leading"):
    n, r, c = a.shape

    def body(a_ref, o_ref):
        acc = a_ref[0]
        for k in range(1, n):
            acc = acc + a_ref[k]
        o_ref[...] = acc

    return pl.pallas_call(
        body, name=name, out_shape=jax.ShapeDtypeStruct((r, c), F32),
        in_specs=[_fullspec((n, r, c))], out_specs=_fullspec((r, c)), grid=(1,),
    )(a)


MESH = pl.DeviceIdType.MESH
NDEV = 8
ANY = pl.BlockSpec(memory_space=pl.ANY)


def _place():
    return lax.axis_index("x"), lax.axis_index("y"), lax.axis_index("c")


def _other_chips(x, y):
    return [(1 - x, y), (x, 1 - y), (1 - x, 1 - y)]


def allgather_small(v, name):
    r, cols = v.shape

    def body(v_ref, o_ref, send_sems, recv_sems):
        x, y, c = _place()
        me = 4 * x + 2 * y + c
        o_ref[me] = v_ref[...]
        copies = []
        for rel in range(1, NDEV):
            peer = (1 - x if rel & 4 else x, 1 - y if rel & 2 else y, 1 - c if rel & 1 else c)
            cp = pltpu.make_async_remote_copy(src_ref=v_ref, dst_ref=o_ref.at[me], send_sem=send_sems.at[rel - 1],
                                              recv_sem=recv_sems.at[rel - 1], device_id=peer, device_id_type=MESH)
            cp.start()
            copies.append(cp)
        for cp in copies:
            cp.wait_recv()
        for cp in copies:
            cp.wait_send()

    return pl.pallas_call(
        body, name=name, out_shape=jax.ShapeDtypeStruct((NDEV, r, cols), F32),
        in_specs=[pl.BlockSpec(memory_space=pltpu.VMEM)], out_specs=pl.BlockSpec(memory_space=pltpu.VMEM),
        scratch_shapes=[pltpu.SemaphoreType.DMA((NDEV - 1,)), pltpu.SemaphoreType.DMA((NDEV - 1,))],
        compiler_params=_cp(40),
    )(v)


def _sems(n):
    return [pltpu.SemaphoreType.DMA((n,)), pltpu.SemaphoreType.DMA((n,))]


def allgather_chips(v, name):
    r, cols = v.shape

    def body(v_ref, o_ref, send_sems, recv_sems):
        x, y, c = _place()
        k = 2 * x + y
        o_ref[k] = v_ref[...]
        copies = []
        for j, (px, py) in enumerate(_other_chips(x, y)):
            cp = pltpu.make_async_remote_copy(src_ref=v_ref, dst_ref=o_ref.at[k], send_sem=send_sems.at[j],
                                              recv_sem=recv_sems.at[j], device_id=(px, py, c), device_id_type=MESH)
            cp.start()
            copies.append(cp)
        for cp in copies:
            cp.wait_recv()
        for cp in copies:
            cp.wait_send()

    return pl.pallas_call(
        body, name=name, out_shape=jax.ShapeDtypeStruct((4, r, cols), F32),
        in_specs=[pl.BlockSpec(memory_space=pltpu.VMEM)], out_specs=pl.BlockSpec(memory_space=pltpu.VMEM),
        scratch_shapes=_sems(3), compiler_params=_cp(40),
    )(v)


def gather_job(arrs):
    n = len(arrs)

    def copy(srcs, outs, sems, i, slot, kk, cc, to, from_src=False):
        hr = arrs[i].shape[0] // 2
        dst = outs[i].at[kk, pl.ds(cc * hr, hr), :]
        return pltpu.make_async_remote_copy(src_ref=srcs[i].at[pl.ds(cc * hr, hr), :] if from_src else dst, dst_ref=dst,
                                            send_sem=sems[0].at[slot * n + i], recv_sem=sems[1].at[slot * n + i],
                                            device_id=to, device_id_type=MESH)

    def start(srcs, outs, sems):
        x, y, c = _place()
        for j, (px, py) in enumerate(_other_chips(x, y)):
            for i in range(n):
                copy(srcs, outs, sems, i, j, 2 * x + y, c, (px, py, c), True).start()

    def finish(srcs, outs, sems):
        x, y, c = _place()
        sib = (x, y, 1 - c)
        chips = _other_chips(x, y)
        passed = []
        for j, (px, py) in enumerate(chips):
            for i in range(n):
                copy(srcs, outs, sems, i, j, 2 * px + py, c, (px, py, c)).wait_recv()
                cp = copy(srcs, outs, sems, i, 3 + j, 2 * px + py, c, sib)
                cp.start()
                passed.append(cp)
        for j, (px, py) in enumerate(chips):
            for i in range(n):
                copy(srcs, outs, sems, i, 3 + j, 2 * px + py, 1 - c, sib).wait_recv()
        for j, (px, py) in enumerate(chips):
            for i in range(n):
                copy(srcs, outs, sems, i, j, 2 * x + y, c, (px, py, c), True).wait_send()
        for cp in passed:
            cp.wait_send()

    return _NS(ins=list(arrs), out_shapes=[jax.ShapeDtypeStruct((4,) + a.shape, a.dtype) for a in arrs], nsem=6 * n,
               start=start, finish=finish)


def chip_swap_job(ss):
    n = len(ss)

    def copies(srcs, outs, sems):
        x, y, c = _place()
        return [pltpu.make_async_remote_copy(src_ref=srcs[i].at[2 * px + py], dst_ref=outs[i].at[j],
                                             send_sem=sems[0].at[j * n + i], recv_sem=sems[1].at[j * n + i],
                                             device_id=(px, py, c), device_id_type=MESH)
                for j, (px, py) in enumerate(_other_chips(x, y)) for i in range(n)]

    def start(srcs, outs, sems):
        for cp in copies(srcs, outs, sems):
            cp.start()

    def finish(srcs, outs, sems):
        for cp in copies(srcs, outs, sems):
            cp.wait()

    return _NS(ins=list(ss), out_shapes=[jax.ShapeDtypeStruct((3,) + s.shape[1:], s.dtype) for s in ss], nsem=3 * n,
               start=start, finish=finish)


def run_job(job, name):
    n, m = len(job.ins), len(job.out_shapes)

    def body(*refs):
        srcs, outs, sems = refs[:n], refs[n:n + m], refs[n + m:]
        job.start(srcs, outs, sems)
        job.finish(srcs, outs, sems)

    return pl.pallas_call(body, name=name, out_shape=job.out_shapes, in_specs=[ANY] * n, out_specs=[ANY] * m,
                          scratch_shapes=_sems(job.nsem))(*job.ins)


def core_swap_job(gs):
    n = len(gs)

    def copies(srcs, outs, sems):
        x, y, c = _place()
        return [pltpu.make_async_remote_copy(src_ref=srcs[i].at[:, pl.ds((1 - c) * (gs[i].shape[1] // 2), gs[i].shape[1] // 2), :],
                                             dst_ref=outs[i], send_sem=sems[0].at[i], recv_sem=sems[1].at[i],
                                             device_id=(x, y, 1 - c), device_id_type=MESH) for i in range(n)]

    def start(srcs, outs, sems):
        for cp in copies(srcs, outs, sems):
            cp.start()

    def finish(srcs, outs, sems):
        for cp in copies(srcs, outs, sems):
            cp.wait()

    return _NS(ins=list(gs), out_shapes=[jax.ShapeDtypeStruct((4, g.shape[1] // 2, g.shape[2]), g.dtype) for g in gs],
               nsem=n, start=start, finish=finish)


def add_half(g, r1, cidx, name):
    _, rows, cols = g.shape
    hr = rows // 2

    def body(c_ref, g_ref, r_ref, o_ref, ob_ref):
        s = g_ref[...] + r_ref[...]
        o_ref[...] = s
        ob_ref[...] = s.astype(BF16)

    blk = lambda f: pl.BlockSpec((1, hr, cols), f)
    return pl.pallas_call(
        body, name=name,
        out_shape=[jax.ShapeDtypeStruct((4, hr, cols), F32), jax.ShapeDtypeStruct((4, hr, cols), BF16)],
        grid_spec=pltpu.PrefetchScalarGridSpec(
            num_scalar_prefetch=1, grid=(4,),
            in_specs=[blk(lambda k, c_ref: (k, c_ref[0], 0)), blk(lambda k, c_ref: (k, 0, 0))],
            out_specs=[blk(lambda k, c_ref: (k, 0, 0)), blk(lambda k, c_ref: (k, 0, 0))]),
    )(cidx, g, r1)


def sum_parts(s1, r2, kidx, name):
    _, hr, cols = s1.shape

    def body(k_ref, s_ref, r_ref, o_ref):
        o_ref[...] = ((s_ref[0] + r_ref[0].astype(F32)) + r_ref[1].astype(F32)) + r_ref[2].astype(F32)

    return pl.pallas_call(
        body, name=name, out_shape=jax.ShapeDtypeStruct((hr, cols), F32),
        grid_spec=pltpu.PrefetchScalarGridSpec(
            num_scalar_prefetch=1, grid=(1,),
            in_specs=[pl.BlockSpec((1, hr, cols), lambda i, k_ref: (k_ref[0], 0, 0)),
                      pl.BlockSpec((3, hr, cols), lambda i, k_ref: (0, 0, 0))],
            out_specs=pl.BlockSpec((hr, cols), lambda i, k_ref: (0, 0))),
    )(kidx, s1, r2)


def swap_reduced_halves(hs):
    n = len(hs)

    def body(*refs):
        srcs, outs = refs[:n], refs[n:2 * n]
        send_sems, recv_sems = refs[2 * n:]
        x, y, c = _place()
        copies = []
        for i in range(n):
            cp = pltpu.make_async_remote_copy(src_ref=srcs[i], dst_ref=outs[i], send_sem=send_sems.at[i],
                                              recv_sem=recv_sems.at[i], device_id=(x, y, 1 - c), device_id_type=MESH)
            cp.start()
            copies.append(cp)
        for cp in copies:
            cp.wait()

    return pl.pallas_call(
        body, name="swap_reduced_halves", out_shape=[jax.ShapeDtypeStruct(h.shape, h.dtype) for h in hs],
        in_specs=[ANY] * n, out_specs=[ANY] * n, scratch_shapes=_sems(n),
    )(*hs)


def adamw_halves(w, m, v, own, oth, cidx, name):
    depth, rows, cols = w.shape
    hr = rows // 2
    tr = min(hr, 256)
    nblk = hr // tr
    bc1 = 1.0 - ADAM_B1 ** ADAM_STEP
    bc2 = 1.0 - ADAM_B2 ** ADAM_STEP

    def body(c_ref, w_ref, m_ref, v_ref, own0, own1, oth0, oth1, g_ref, d_ref, nm_ref, nv_ref):
        l = pl.program_id(0)
        hi = pl.program_id(1)
        mine = jnp.where(l == 0, own0[...], own1[...])
        other = jnp.where(l == 0, oth0[...], oth1[...])
        g_v = jnp.where(hi == c_ref[0], mine, other)
        nm = ADAM_B1 * m_ref[0] + (1.0 - ADAM_B1) * g_v
        nv = ADAM_B2 * v_ref[0] + (1.0 - ADAM_B2) * (g_v * g_v)
        g_ref[0] = g_v
        nm_ref[0] = nm
        nv_ref[0] = nv
        d_ref[0] = -ADAM_LR * ((nm / bc1) / (jnp.sqrt(nv / bc2) + ADAM_EPS) + ADAM_WD * w_ref[0])

    wspec = pl.BlockSpec((1, tr, cols), lambda l, hi, b, c_ref: (l, hi * nblk + b, 0))
    def gspec(layer, mine):
        def index(l, hi, b, c_ref):
            used = jnp.logical_and(l == layer, (hi == c_ref[0]) == mine)
            return (jnp.where(used, b, 0), 0)
        return pl.BlockSpec((tr, cols), index)

    assert depth == 2
    return pl.pallas_call(
        body, name=name, out_shape=[jax.ShapeDtypeStruct(w.shape, F32)] * 4,
        grid_spec=pltpu.PrefetchScalarGridSpec(
            num_scalar_prefetch=1, grid=(depth, 2, nblk),
            in_specs=[wspec] * 3 + [gspec(0, True), gspec(1, True), gspec(0, False), gspec(1, False)],
            out_specs=[wspec] * 4),
    )(cidx, w, m, v, own[0], own[1], oth[0], oth[1])


class _NS:
    def __init__(self, **kw):
        self.__dict__.update(kw)


def _prep_in(win, conv_w, conv_b, dt_bias, a_log, ssd_d, ssd_nw, qnw, kvnw, pool_w, pool_scale, n1, n2):
    winp = jnp.concatenate([win[:, 0:384], win[:, 384:1280], win[:, 1292:1548], win[:, 1548:1804], win[:, 1836:2092],
                            win[:, 1804:1836], win[:, 1280:1292], jnp.zeros((D, NP - IN_COLS), win.dtype)], axis=1)
    wbd = (jnp.eye(4, dtype=F32)[:, None, :, None] * pool_w[:, :, None, :]).reshape(POOL_DIM, POOL_DIM).astype(MXU)
    a = -jnp.exp(a_log)
    return _NS(
        winp=winp, wbd=wbd,
        cw8=jnp.pad(conv_w, ((0, 4), (0, 0))), cb=conv_b[None],
        dtb=jnp.pad(dt_bias.reshape(1, 12), ((0, 0), (DT0, 128 - DT0 - 12))),
        arow=jnp.pad(a[:, None, :], ((0, 0), (0, 7), (0, 128 - SSD_HEADS))), a=a,
        dexp=jnp.repeat(ssd_d, SSD_P)[None], ssd_nw=ssd_nw[None], qnw=qnw[None], kvnw=kvnw[None],
        pscale=pool_scale[None], n1=n1[None], n2=n2[None])


def _prep_rest(wqb, wkvb, wout, w1, w2):
    wq = jnp.pad(wqb.reshape(256, MLA_HEADS, QK_DIM), ((0, 0), (0, 0), (0, HP - QK_DIM))).reshape(256, QW)
    kv3 = wkvb.reshape(256, MLA_HEADS, 128)
    wk = jnp.pad(kv3[:, :, :64], ((0, 0), (0, 0), (0, 64))).reshape(256, QW)
    wv = jnp.pad(kv3[:, :, 64:], ((0, 0), (0, 0), (0, 64))).reshape(256, QW)
    wo = jnp.concatenate([jnp.pad(wout[384:768].reshape(MLA_HEADS, 64, D), ((0, 0), (0, 64), (0, 0))).reshape(QW, D),
                          wout[0:384], wout[768:1024]], axis=0)
    return _NS(wq=wq, wk=wk, wv=wv, wo=wo, w1=w1, w2=w2)


def _prep_layer(win, wqb, wkvb, wout, w1, w2, *small):
    lw = _prep_in(win, *small)
    lw.__dict__.update(_prep_rest(wqb, wkvb, wout, w1, w2).__dict__)
    return lw


def _by_chip_cols(a):
    return jnp.stack([a[:, k * (a.shape[1] // 4):(k + 1) * (a.shape[1] // 4)] for k in range(4)])


def _by_chip_rows(a):
    return a.reshape(4, a.shape[0] // 4, a.shape[1])


def _unprep_in(dwinp):
    return jnp.concatenate([dwinp[:, 0:384], dwinp[:, 384:1280], dwinp[:, 2080:2092], dwinp[:, 1280:1536],
                            dwinp[:, 1536:1792], dwinp[:, 2048:2080], dwinp[:, 1792:2048]], axis=1)


def _unprep_rest(dwq, dwk, dwv, dwo):
    dwqb = dwq.reshape(256, MLA_HEADS, HP)[:, :, :QK_DIM].reshape(256, MLA_HEADS * QK_DIM)
    dwkvb = jnp.concatenate([dwk.reshape(256, MLA_HEADS, HP)[:, :, :64], dwv.reshape(256, MLA_HEADS, HP)[:, :, :64]],
                            axis=2).reshape(256, MLA_HEADS * 128)
    dwout = jnp.concatenate([dwo[QW:QW + 384], dwo[0:QW].reshape(MLA_HEADS, HP, D)[:, :64].reshape(384, D),
                             dwo[QW + 384:CAT]], axis=0)
    return dwqb, dwkvb, dwout


def _rope_tables(nb, N):
    t = jnp.arange(N, dtype=F32)
    row = jnp.floor(t / GRID_W)
    col = t - row * GRID_W
    inv = jnp.asarray(10000.0 ** (-np.arange(8, dtype=np.float32) / 8), F32)
    ang = jnp.stack([row[:, None] * inv, col[:, None] * inv], axis=1)
    cs, sn = jnp.cos(ang), jnp.sin(ang)
    zero = jnp.zeros_like(sn)
    lanes = lambda first, second: jnp.stack([first, second], axis=2).reshape(N, 32)
    pad = lambda a, fill: jnp.concatenate([jnp.full((N, 64), fill, F32), a, jnp.full((N, 32), fill, F32)], axis=1)
    tabs = []
    for tab, fill in ((pad(lanes(cs, cs), 1.0), 1.0), (pad(lanes(-sn, zero), 0.0), 0.0), (pad(lanes(zero, sn), 0.0), 0.0)):
        one = jnp.concatenate([jnp.full((CTX, 128), fill, F32), tab], axis=0)
        tabs.append(jnp.tile(one, (nb, 1)))
    return tabs


def _eexp():
    e = np.zeros((128, SSD_INNER), np.float32)
    for h in range(SSD_HEADS):
        e[h, h * SSD_P:(h + 1) * SSD_P] = 1.0
    return jnp.asarray(e)


class _NoHooks:
    def __init__(self, lws):
        self.lws = lws

    def weights_in(self, l):
        return _NS(**self.lws[l].__dict__)

    def weights_rest(self, l, scan_out):
        return self.lws[l]

    def job(self, where, l, early=None):
        return None

    def done(self, where, l, out):
        pass

    def layer_grads(self, l, g):
        pass


def _layer_fwd(X, bm, l, cst, hooks):
    nb, T, bps, N = cst.nb, cst.T, cst.bps, cst.N
    lw = hooks.weights_in(l)
    h1, pz, pxbc, pqa, pkva, ppool, plast = in_proj(X, bm, lw.n1, lw.winp)
    xs, bmat, cmat, dtv = ssd_prep(pxbc, plast, lw.cw8, lw.cb, lw.dtb, bps)
    y2, hin, out = ssd_scan_fwd(xs, bmat, cmat, dtv, lw.arow, cst.eexp, nb, T, hooks.job("fwd_scan", l))
    lw.__dict__.update(hooks.weights_rest(l, out).__dict__)
    ssd = ssd_out_fwd(y2, xs, pz, lw.dexp, lw.ssd_nw)
    q, k, v, cq, ckv = mla_prep(pqa, pkva, plast, lw.qnw, lw.kvnw, lw.wq, lw.wk, lw.wv, *cst.rope)
    attn, lse = attn_fwd(q, k, v, nb, T)
    pool = pool_fwd(ppool, lw.wbd, lw.pscale, bps, N)
    x1, mix, cat = mix_fwd(X, attn, ssd, pool, bm, lw.wo)
    x2, mo, r, h2, out = mlp_fwd(x1, bm, lw.n2, lw.w1, lw.w2, hooks.job("fwd_mlp", l))
    hooks.done("fwd_mlp", l, out)
    sv = _NS(X=X, h1=h1, pz=pz, pxbc=pxbc, pqa=pqa, pkva=pkva, ppool=ppool, plast=plast, xs=xs, bmat=bmat, cmat=cmat,
             dtv=dtv, y2=y2, hin=hin, q=q, k=k, v=v, cq=cq, ckv=ckv, attn=attn, lse=lse, x1=x1, mix=mix, cat=cat, mo=mo, r=r,
             h2=h2, lw=lw)
    return x2, sv


def _layer_bwd(dx2, bm, l, sv, cst, hooks):
    nb, T, bps, N = cst.nb, cst.T, cst.bps, cst.N
    lw = sv.lw
    dx1, du, dob, part_mlp, out = mlp_bwd(dx2, sv.x1, sv.mo, sv.r, bm, lw.n2, lw.w2, lw.w1, hooks.job("bwd_mlp", l))
    hooks.done("bwd_mlp", l, out)
    dw1 = mm_tn(sv.h2, du, name="wgrad_mlp1", col_blocks=True)
    dw2 = mm_tn(sv.r, dob, square_a=True, name="wgrad_mlp2")
    dattn, dssd, dpool, dwo, part_mix = mix_bwd(dx1, sv.mix, sv.cat, bm, lw.wo)
    dppool, dwbd, part_pool = pool_bwd(sv.ppool, dpool, lw.wbd, lw.pscale, bps, N)
    dq, dk, dv = attn_bwd(sv.q, sv.k, sv.v, sv.attn, sv.lse, dattn, nb, T)
    dpqa, dpkva, dkr, dwq, dwk, dwv, part_mla = mla_prep_bwd(dq, dk, dv, sv.pqa, sv.pkva, sv.cq, sv.ckv, lw.qnw, lw.kvnw,
                                                             lw.wq, lw.wk, lw.wv, *cst.rope)
    dwqb, dwkvb, dwout = _unprep_rest(dwq, dwk, dwv, dwo)
    early = dict(w_q_b=_by_chip_cols(dwqb), w_kv_b=_by_chip_cols(dwkvb), w_out=_by_chip_rows(dwout), w_mlp1=dw1,
                 w_mlp2=_by_chip_rows(dw2))
    dyy, dz, dxs_skip, part_so = ssd_out_bwd(dssd, sv.y2, sv.xs, sv.pz, lw.dexp, lw.ssd_nw)
    dxs2, dbm2, dcm2, ddt2, da, out = ssd_scan_bwd(sv.xs, sv.bmat, sv.cmat, sv.dtv, lw.arow, cst.eexp, sv.hin, dyy,
                                                   nb, T, hooks.job("bwd_scan", l, early))
    hooks.done("bwd_scan", l, out)
    dpre, dlast_dt, part_conv = ssd_prep_bwd_a(sv.pxbc, sv.plast, lw.cw8, lw.cb, lw.dtb, dxs_skip, dxs2, dbm2, dcm2,
                                               ddt2, bps)
    dpxbc = ssd_prep_bwd_b(dpre, lw.cw8, bps)
    dx, dwinp, part_in = in_proj_bwd(dx1, sv.X, sv.h1, dz, dpxbc, dpqa, dpkva, dppool, dkr, dlast_dt, bm, lw.n1, lw.winp)

    dmod = jnp.stack([part_in[:, 0], part_in[:, 1], part_mix[:, 0], part_mlp[:, 0], part_mlp[:, 1], part_mlp[:, 2]],
                     axis=1)
    dmod = dmod.reshape(nb, bps, 6, D)
    dm_rows = jnp.concatenate([jnp.sum(dmod[:, 1:], axis=1), jnp.sum(dmod[:, 0], axis=0)[None]], axis=0)
    da_dh = jnp.sum(da[:, :, 0, :SSD_HEADS], axis=1)
    conv_parts = jnp.sum(part_conv, axis=0)
    g = _NS(
        w_in=_by_chip_cols(_unprep_in(dwinp)), dm_rows=dm_rows.reshape(3, 6 * D), **early,
        norm1_w=jnp.sum(part_in[:, 2], axis=0), norm2_w=jnp.sum(part_mlp[:, 3], axis=0),
        conv_w=conv_parts[0:4], conv_b=conv_parts[4],
        dt_bias=conv_parts[5, DT0:DT0 + 12].reshape(2, SSD_HEADS), a_log=da_dh * lw.a,
        ssd_d=jnp.sum(jnp.sum(part_so[:, 1], axis=0).reshape(SSD_HEADS, SSD_P), axis=1),
        ssd_norm_w=jnp.sum(part_so[:, 0], axis=0),
        q_a_norm_w=jnp.sum(part_mla[:, 0], axis=0), kv_a_norm_w=jnp.sum(part_mla[:, 1], axis=0),
        pool_w=jnp.stack([dwbd[i * 64:(i + 1) * 64, i * 64:(i + 1) * 64] for i in range(4)]),
        pool_scale=jnp.sum(part_pool[:, 0], axis=0))
    hooks.layer_grads(l, g)
    return dx, g


def _local_step(x, ctx, tgt, bms, lws, fw, cst, hooks=None):
    nb, N = x.shape[0], x.shape[1]
    R = nb * cst.T
    hooks = _NoHooks(lws) if hooks is None else hooks
    X = jnp.concatenate([ctx, x], axis=1).reshape(R, D)
    saved = []
    for l in range(DEPTH):
        X, sv = _layer_fwd(X, bms[l], l, cst, hooks)
        saved.append(sv)
    dX, part_fin = final_loss(X, tgt.reshape(nb * N, D), fw[None], cst.bps)
    loss = (0.5 / D) * jnp.sum(part_fin[:, 1])
    dfw = jnp.sum(part_fin[:, 0], axis=0)
    grads = [None] * DEPTH
    for l in reversed(range(DEPTH)):
        dX, grads[l] = _layer_bwd(dX, bms[l], l, saved[l], cst, hooks)
    grad_x = dX.reshape(nb, cst.T, D)[:, CTX:, :]
    return loss, grad_x, grads, dfw


def _consts(nb, N):
    T = CTX + N
    bps = T // SB
    return _NS(nb=nb, N=N, T=T, bps=bps, eexp=_eexp(), rope=_rope_tables(nb, N))


def _block_mod(modrows, cst):
    rows = []
    for b in range(cst.nb):
        rows.append(modrows[cst.nb:cst.nb + 1])
        rows.append(jnp.broadcast_to(modrows[b:b + 1], (cst.bps - 1, 6, D)))
    return jnp.pad(jnp.concatenate(rows, axis=0), ((0, 0), (0, 2), (0, 0)))


SMALL = (("norm1_w", (2, D)), ("norm2_w", (2, D)), ("conv_w", (2, 4, XBC)), ("conv_b", (2, XBC)),
         ("dt_bias", (2, 2, 6)), ("a_log", (2, 2, 6)), ("ssd_d", (2, 6)), ("ssd_norm_w", (2, 384)),
         ("q_a_norm_w", (2, 256)), ("kv_a_norm_w", (2, 256)), ("pool_w", (2, 4, 64, 64)), ("pool_scale", (2, 256)),
         ("final_norm_w", (D,)), ("mod_b", (2, 6 * D)))
SMALL_ROWS = 64
DM_ROWS = 48


def _pack_small(vals):
    flat = jnp.concatenate([vals[n].reshape(-1) for n, _ in SMALL])
    return jnp.pad(flat, (0, SMALL_ROWS * D - flat.shape[0])).reshape(SMALL_ROWS, D)


def _unpack_small(p):
    flat = p.reshape(-1)
    out, off = {}, 0
    for n, shp in SMALL:
        size = int(np.prod(shp))
        out[n] = flat[off:off + size].reshape(shp)
        off += size
    return out


def cctx_grad(parts, c_ctx):
    def body(p_ref, c_ref, o_ref):
        acc = ((p_ref[0] + p_ref[1]) + p_ref[2]) + p_ref[3]
        v = c_ref[...]
        sig = _sigmoid(v)
        o_ref[...] = acc * (sig * (1.0 + v * (1.0 - sig)))

    return pl.pallas_call(
        body, name="cctx_grad", out_shape=jax.ShapeDtypeStruct((8, D), F32),
        in_specs=[_fullspec((4, 8, D)), _fullspec((1, D))], out_specs=_fullspec((8, D)), grid=(1,),
    )(parts, c_ctx)


def kernel(x, c, ctx, c_ctx, mod_w, mod_b, norm1_w, norm2_w, w_in, conv_w, conv_b, dt_bias, a_log, ssd_d, ssd_norm_w, q_a_norm_w, w_q_b, kv_a_norm_w, w_kv_b, pool_w, pool_scale, w_out, w_mlp1, w_mlp2, final_norm_w, loss_target, m_c_ctx, m_mod_w, m_mod_b, m_norm1_w, m_norm2_w, m_w_in, m_conv_w, m_conv_b, m_dt_bias, m_a_log, m_ssd_d, m_ssd_norm_w, m_q_a_norm_w, m_w_q_b, m_kv_a_norm_w, m_w_kv_b, m_pool_w, m_pool_scale, m_w_out, m_w_mlp1, m_w_mlp2, m_final_norm_w, v_c_ctx, v_mod_w, v_mod_b, v_norm1_w, v_norm2_w, v_w_in, v_conv_w, v_conv_b, v_dt_bias, v_a_log, v_ssd_d, v_ssd_norm_w, v_q_a_norm_w, v_w_q_b, v_kv_a_norm_w, v_w_kv_b, v_pool_w, v_pool_scale, v_w_out, v_w_mlp1, v_w_mlp2, v_final_norm_w):
    nb, N = x.shape[0], x.shape[1]
    cst = _consts(nb, N)
    xi, yi, ci = _place()
    me = 4 * xi + 2 * yi + ci
    kchip = 2 * xi + yi
    mcols = mod_w.shape[2]
    cshard = conv_w.shape[2]

    blk = jnp.zeros((16, D), F32).at[0:nb].set(c).at[8:16, 0:cshard].set(conv_w.reshape(8, cshard))
    g1 = allgather_small(blk, "gather_cond")
    cond = jnp.concatenate([g1[:, 0:nb].reshape(NDEV * nb, D), c_ctx[None],
                            jnp.zeros((MODR - NDEV * nb - 1, D), F32)], axis=0)
    conv_full = [jnp.concatenate([g1[2 * k, 8 + 4 * l:12 + 4 * l, 0:cshard] for k in range(4)], axis=1)
                 for l in range(DEPTH)]

    mb = [lax.dynamic_slice_in_dim(mod_b[l], kchip * mcols, mcols)[None] for l in range(DEPTH)]
    ms = jnp.concatenate([mod_fwd(cond, mod_w[l], mb[l]) for l in range(DEPTH)], axis=0)
    g2 = allgather_chips(ms, "gather_mod")
    bms = []
    for l in range(DEPTH):
        m_all = jnp.concatenate([g2[k, MODR * l:MODR * (l + 1)] for k in range(4)], axis=1)
        mine = jnp.concatenate([lax.dynamic_slice_in_dim(m_all, nb * me, nb), m_all[NDEV * nb:NDEV * nb + 1]], axis=0)
        bms.append(_block_mod(mine.reshape(nb + 1, 6, D), cst))

    assert DEPTH == 2
    big = (w_in, w_q_b, w_kv_b, w_out, w_mlp1, w_mlp2)
    names = ("w_in", "w_q_b", "w_kv_b", "w_out", "w_mlp1", "w_mlp2")
    concat_axis = dict(w_in=1, w_q_b=1, w_kv_b=1, w_out=0, w_mlp1=1, w_mlp2=0)
    cidx = jnp.reshape(ci, (1,)).astype(jnp.int32)
    kidx = jnp.reshape(kchip, (1,)).astype(jnp.int32)
    shards = [{n: a[l].astype(MXU) for n, a in zip(names, big)} for l in range(DEPTH)]

    def core_sums(gs, got=None):
        ns = list(gs)
        got = run_job(core_swap_job([gs[n] for n in ns]), "swap_core_halves") if got is None else got
        return {n: add_half(gs[n], r, cidx, "add_half_" + n) for n, r in zip(ns, got)}

    class Hooks:
        gathered = [dict(w_in=run_job(gather_job([shards[0]["w_in"]]), "gather_w_in")[0]), {}]
        core_sum = [{}, {}]
        received = [{}, {}]

        def whole(self, l, n):
            return jnp.concatenate([jnp.where(kchip == k, shards[l][n], self.gathered[l][n][k]) for k in range(4)],
                                   axis=concat_axis[n])

        def weights_in(self, l):
            return _prep_in(self.whole(l, "w_in"), conv_full[l], conv_b[l], dt_bias[l], a_log[l], ssd_d[l], ssd_norm_w[l],
                            q_a_norm_w[l], kv_a_norm_w[l], pool_w[l], pool_scale[l], norm1_w[l], norm2_w[l])

        def weights_rest(self, l, scan_out):
            if l == 0:
                self.gathered[0].update(zip(names[1:], scan_out))
            return _prep_rest(*[self.whole(l, n) for n in names[1:]])

        def job(self, where, l, early=None):
            if l == 1 and where == "bwd_scan":
                self.early1 = early
                return core_swap_job([early[n] for n in names[1:]])
            if l != 0:
                return None
            if where == "fwd_scan":
                return gather_job([shards[0][n] for n in names[1:]])
            if where == "fwd_mlp":
                return gather_job([shards[1][n] for n in names])
            if where == "bwd_mlp":
                return chip_swap_job([self.core_sum[1][n][1] for n in names])
            self.core_sum[0].update(core_sums(early))
            return chip_swap_job([self.core_sum[0][n][1] for n in names[1:]])

        def done(self, where, l, out):
            if l == 1 and where == "bwd_scan":
                self.core_sum[1].update(core_sums(self.early1, out))
            if l != 0:
                return
            if where == "fwd_mlp":
                self.gathered[1].update(zip(names, out))
            elif where == "bwd_mlp":
                self.received[1].update(zip(names, out))
            elif where == "bwd_scan":
                self.received[0].update(zip(names[1:], out))

        def layer_grads(self, l, g):
            if l == 1:
                self.core_sum[1].update(core_sums(dict(w_in=g.w_in)))
            else:
                self.core_sum[0].update(core_sums(dict(w_in=g.w_in)))
                self.received[0]["w_in"] = run_job(chip_swap_job([self.core_sum[0]["w_in"][1]]), "swap_w_in")[0]

    hooks = Hooks()
    loss_part, grad_x, grads, dfw = _local_step(x, ctx, loss_target, bms, None, final_norm_w, cst, hooks)
    loss = lax.psum(loss_part, ("x", "y", "c"))
    g_own = [sum_parts(hooks.core_sum[l][n][0], hooks.received[l][n], kidx, "sum_parts_" + n)
             for n in names for l in range(DEPTH)]
    g_oth = swap_reduced_halves(g_own)

    small = {n: jnp.stack([getattr(grads[l], n) for l in range(DEPTH)]) for n, _ in SMALL if n not in ("final_norm_w", "mod_b")}
    small["final_norm_w"] = dfw
    small["mod_b"] = jnp.stack([jnp.sum(grads[l].dm_rows, axis=0) for l in range(DEPTH)])
    dm = jnp.pad(jnp.concatenate([grads[l].dm_rows for l in range(DEPTH)], axis=0), ((0, 8 - 3 * DEPTH), (0, 0)))
    g3 = allgather_small(jnp.concatenate([_pack_small(small), dm.reshape(DM_ROWS, D)], axis=0), "gather_small")
    tot = sum_leading(g3, "sum_small")
    gsmall = _unpack_small(tot[0:SMALL_ROWS])
    ctx_sum = tot[SMALL_ROWS:].reshape(8, 6 * D)
    dm_dev = g3[:, SMALL_ROWS:].reshape(NDEV, 8, 6 * D)
    g_mod_w, dpart = [], jnp.zeros((8, D), F32)
    for l in range(DEPTH):
        dm_all = jnp.concatenate([dm_dev[:, 3 * l:3 * l + nb].reshape(NDEV * nb, 6 * D), ctx_sum[3 * l + nb:3 * l + nb + 1],
                                  jnp.zeros((MODR - NDEV * nb - 1, 6 * D), F32)], axis=0)
        g_mod_w.append(mod_wgrad(cond, lax.dynamic_slice_in_dim(dm_all, kchip * mcols, mcols, axis=1)))
        dctx = jnp.pad(lax.dynamic_slice_in_dim(ctx_sum[3 * l + nb:3 * l + nb + 1], kchip * mcols, mcols, axis=1), ((0, 7), (0, 0)))
        dpart = dpart + mod_dgrad(dctx, mod_w[l])
    g_c_ctx = cctx_grad(allgather_chips(dpart, "gather_cctx"), c_ctx[None])[0]

    res = {}
    moments = ((m_w_in, v_w_in), (m_w_q_b, v_w_q_b), (m_w_kv_b, v_w_kv_b), (m_w_out, v_w_out), (m_w_mlp1, v_w_mlp1),
               (m_w_mlp2, v_w_mlp2))
    for i, (n, w, (m, v)) in enumerate(zip(names, big, moments)):
        res[n] = tuple(adamw_halves(w, m, v, g_own[DEPTH * i:DEPTH * (i + 1)], g_oth[DEPTH * i:DEPTH * (i + 1)], cidx,
                                    "adamw_" + n))
    g_mw = jnp.stack(g_mod_w)
    r_mw = adamw(mod_w.reshape(-1, mcols), g_mw.reshape(-1, mcols), m_mod_w.reshape(-1, mcols),
                 v_mod_w.reshape(-1, mcols), name="adamw_mod_w")
    res["mod_w"] = (g_mw,) + tuple(a.reshape(mod_w.shape) for a in r_mw)

    given = dict(norm1_w=(norm1_w, m_norm1_w, v_norm1_w), norm2_w=(norm2_w, m_norm2_w, v_norm2_w),
                 conv_b=(conv_b, m_conv_b, v_conv_b), dt_bias=(dt_bias, m_dt_bias, v_dt_bias),
                 a_log=(a_log, m_a_log, v_a_log), ssd_d=(ssd_d, m_ssd_d, v_ssd_d),
                 ssd_norm_w=(ssd_norm_w, m_ssd_norm_w, v_ssd_norm_w), q_a_norm_w=(q_a_norm_w, m_q_a_norm_w, v_q_a_norm_w),
                 kv_a_norm_w=(kv_a_norm_w, m_kv_a_norm_w, v_kv_a_norm_w), pool_w=(pool_w, m_pool_w, v_pool_w),
                 pool_scale=(pool_scale, m_pool_scale, v_pool_scale),
                 final_norm_w=(final_norm_w, m_final_norm_w, v_final_norm_w), mod_b=(mod_b, m_mod_b, v_mod_b))
    zero_cw = jnp.zeros((2, 4, XBC), F32)
    packs = [_pack_small({n: (given[n][i] if n in given else zero_cw) for n, _ in SMALL}) for i in range(3)]
    r_small = [_unpack_small(a) for a in adamw(packs[0], tot[0:SMALL_ROWS], packs[1], packs[2], name="adamw_small")]
    for n in given:
        res[n] = (gsmall[n], r_small[0][n], r_small[1][n], r_small[2][n])

    g_cw = lax.dynamic_slice_in_dim(gsmall["conv_w"], kchip * cshard, cshard, axis=2)
    padcw = lambda a: jnp.pad(a.reshape(8, cshard), ((0, 0), (0, 256 - cshard)))
    r_cw = adamw(padcw(conv_w), padcw(g_cw), padcw(m_conv_w), padcw(v_conv_w), name="adamw_conv_w")
    res["conv_w"] = (g_cw,) + tuple(a[:, 0:cshard].reshape(conv_w.shape) for a in r_cw)
    r_cc = adamw(c_ctx.reshape(8, 128), g_c_ctx.reshape(8, 128), m_c_ctx.reshape(8, 128), v_c_ctx.reshape(8, 128),
                 name="adamw_c_ctx")
    res["c_ctx"] = (g_c_ctx,) + tuple(a.reshape(D) for a in r_cc)

    order = ("c_ctx", "mod_w", "mod_b", "norm1_w", "norm2_w", "w_in", "conv_w", "conv_b", "dt_bias", "a_log", "ssd_d",
             "ssd_norm_w", "q_a_norm_w", "w_q_b", "kv_a_norm_w", "w_kv_b", "pool_w", "pool_scale", "w_out", "w_mlp1",
             "w_mlp2", "final_norm_w")
    return (loss, grad_x) + tuple(res[n][i] for i in range(4) for n in order)
```
